```python
import math
import jax, jax.numpy as jnp
from jax import lax
import numpy as np

D_MODEL = 1024
BATCH = 8
SEQ = 8192
DEPTH = 1

D_MIX = D_MODEL
RET_WIDTH = D_MIX // 2
RET_HEADS = 4
RET_HEAD_DIM = RET_WIDTH // RET_HEADS
RET_CHUNK = 128
ROPE_BASE = 10000.0
SSM_WIDTH = D_MIX - RET_WIDTH
SSM_GROUP = 16
SSM_GROUPS = SSM_WIDTH // SSM_GROUP
SSM_STATE = 64
DT_MIN = 1e-3
DT_MAX = 1e-1
D_FF = 4 * D_MODEL
NORM_EPS = 1e-6
IN_COLS = 4 * RET_WIDTH + SSM_WIDTH

kernel_name = "hymba_retnet_s5_sandwich_block"


def rmsnorm(x, g):
    xf = x.astype(jnp.float32)
    y = xf * lax.rsqrt(jnp.mean(xf * xf, axis=-1, keepdims=True) + NORM_EPS) * g.astype(jnp.float32)
    return y.astype(x.dtype)


def rope(x):
    L, d = x.shape[1], x.shape[-1]
    half = d // 2
    inv_freq = ROPE_BASE ** (-jnp.arange(half, dtype=jnp.float32) / half)
    ang = jnp.arange(L, dtype=jnp.float32)[:, None] * inv_freq[None, :]
    cos = jnp.cos(ang)[None, :, None, :]
    sin = jnp.sin(ang)[None, :, None, :]
    x1, x2 = x[..., :half], x[..., half:]
    return jnp.concatenate([x1 * cos - x2 * sin, x1 * sin + x2 * cos], axis=-1)


def retention_chunkwise(q, k, v):
    B, L, H, d = q.shape
    C = RET_CHUNK
    nc = L // C
    log_gamma = jnp.log(1.0 - jnp.exp(jnp.linspace(math.log(1.0 / 32), math.log(1.0 / 512), H))).astype(jnp.float32)
    q = q.reshape(B, nc, C, H, d)
    k = k.reshape(B, nc, C, H, d)
    v = v.reshape(B, nc, C, H, d)
    idx = jnp.arange(C, dtype=jnp.float32)
    diff = idx[:, None] - idx[None, :]
    decay = jnp.where(diff[None] >= 0, jnp.exp(jnp.maximum(diff, 0.0)[None] * log_gamma[:, None, None]), 0.0)
    s = jnp.einsum('bnihk,bnjhk->bnhij', q, k) * decay[None, None]
    inner = jnp.einsum('bnhij,bnjhd->bnihd', s, v)
    zeta = jnp.exp((C - 1 - idx)[None, :] * log_gamma[:, None])
    S = jnp.einsum('bnjhk,bnjhd,hj->bnhkd', k, v, zeta)
    g_chunk = jnp.exp(C * log_gamma)[None, :, None, None]

    def step(R, S_i):
        return g_chunk * R + S_i, R

    R0 = jnp.zeros((B, H, d, d), jnp.float32)
    _, R_prev = lax.scan(step, R0, jnp.moveaxis(S, 1, 0))
    R_prev = jnp.moveaxis(R_prev, 0, 1)
    xi = jnp.exp((idx + 1.0)[None, :] * log_gamma[:, None])
    cross = jnp.einsum('bnihk,bnhkd,hi->bnihd', q, R_prev, xi)
    return (inner + cross).reshape(B, L, H, d)


def head_groupnorm(y, g):
    mu = jnp.mean(y, axis=-1, keepdims=True)
    var = jnp.mean(jnp.square(y - mu), axis=-1, keepdims=True)
    yn = (y - mu) * lax.rsqrt(var + NORM_EPS)
    return yn * g.astype(jnp.float32).reshape(RET_HEADS, RET_HEAD_DIM)


def s5_scan(u, lam_re, lam_im, log_dt, b_re, b_im, c_re, c_im, d_skip):
    B, L, _ = u.shape
    uf = u.astype(jnp.float32).reshape(B, L, SSM_GROUPS, SSM_GROUP)
    lam = lax.complex(jnp.minimum(lam_re.astype(jnp.float32), -1e-4), lam_im.astype(jnp.float32))
    dt = jnp.exp(log_dt.astype(jnp.float32))[:, None]
    lam_bar = jnp.exp(lam * dt)
    b_c = lax.complex(b_re.astype(jnp.float32), b_im.astype(jnp.float32))
    b_bar = ((lam_bar - 1.0) / lam)[:, :, None] * b_c
    bu = jnp.einsum('blgc,gpc->blgp', uf.astype(jnp.complex64), b_bar)
    a = jnp.broadcast_to(lam_bar, bu.shape)

    def combine(e1, e2):
        a1, x1 = e1
        a2, x2 = e2
        return a2 * a1, a2 * x1 + x2

    _, states = lax.associative_scan(combine, (a, bu), axis=1)
    c_c = lax.complex(c_re.astype(jnp.float32), c_im.astype(jnp.float32))
    y = jnp.real(jnp.einsum('blgp,gcp->blgc', states, c_c))
    y = y + d_skip.astype(jnp.float32).reshape(SSM_GROUPS, SSM_GROUP) * uf
    return y.reshape(B, L, SSM_WIDTH)


def _fwd_setup_inputs(seed: int = 0) -> dict:
    key = jax.random.key(seed)
    ks = jax.random.split(key, 20)
    f32 = jnp.float32
    nrm = lambda k, shape, scale: (jax.random.normal(k, shape, f32) * scale)
    gain = lambda k, shape: 1.0 + 0.02 * jax.random.normal(k, shape, f32)
    x = jax.random.normal(ks[0], (BATCH, SEQ, D_MODEL), f32)
    lam_im_base = math.pi * jnp.arange(SSM_STATE, dtype=f32)
    return {
        "x": x,
        "norm_mix_pre": gain(ks[1], (DEPTH, D_MODEL)),
        "norm_mix_post": gain(ks[2], (DEPTH, D_MODEL)),
        "w_in": nrm(ks[3], (DEPTH, D_MODEL, IN_COLS), D_MODEL ** -0.5),
        "ret_gn_gain": gain(ks[4], (DEPTH, RET_WIDTH)),
        "ssm_lambda_re": -0.5 + 0.01 * jax.random.normal(ks[5], (DEPTH, SSM_GROUPS, SSM_STATE), f32),
        "ssm_lambda_im": lam_im_base + 0.01 * jax.random.normal(ks[6], (DEPTH, SSM_GROUPS, SSM_STATE), f32),
        "ssm_log_dt": jax.random.uniform(ks[7], (DEPTH, SSM_GROUPS), f32, math.log(DT_MIN), math.log(DT_MAX)),
        "ssm_b_re": nrm(ks[8], (DEPTH, SSM_GROUPS, SSM_STATE, SSM_GROUP), (2 * SSM_GROUP) ** -0.5),
        "ssm_b_im": nrm(ks[9], (DEPTH, SSM_GROUPS, SSM_STATE, SSM_GROUP), (2 * SSM_GROUP) ** -0.5),
        "ssm_c_re": nrm(ks[10], (DEPTH, SSM_GROUPS, SSM_GROUP, SSM_STATE), (2 * SSM_STATE) ** -0.5),
        "ssm_c_im": nrm(ks[11], (DEPTH, SSM_GROUPS, SSM_GROUP, SSM_STATE), (2 * SSM_STATE) ** -0.5),
        "ssm_d": nrm(ks[12], (DEPTH, SSM_WIDTH), 1.0),
        "w_glu": nrm(ks[13], (DEPTH, SSM_WIDTH, 2 * SSM_WIDTH), SSM_WIDTH ** -0.5),
        "w_out": nrm(ks[14], (DEPTH, D_MIX, D_MODEL), D_MIX ** -0.5),
        "norm_mlp_pre": gain(ks[15], (DEPTH, D_MODEL)),
        "norm_mlp_post": gain(ks[16], (DEPTH, D_MODEL)),
        "w_ff1": nrm(ks[17], (DEPTH, D_MODEL, D_FF), D_MODEL ** -0.5),
        "w_ff2": nrm(ks[18], (DEPTH, D_FF, D_MODEL), D_FF ** -0.5),
    }


def _fwd_reference(x, norm_mix_pre, norm_mix_post, w_in, ret_gn_gain, ssm_lambda_re, ssm_lambda_im,
              ssm_log_dt, ssm_b_re, ssm_b_im, ssm_c_re, ssm_c_im, ssm_d, w_glu, w_out,
              norm_mlp_pre, norm_mlp_post, w_ff1, w_ff2):
    B, L, _ = x.shape
    for i in range(DEPTH):
        h = rmsnorm(x, norm_mix_pre[i])
        proj = h @ w_in[i]
        q, k, v, gate, u = jnp.split(proj, [RET_WIDTH, 2 * RET_WIDTH, 3 * RET_WIDTH, 4 * RET_WIDTH], axis=-1)
        heads = lambda t: t.astype(jnp.float32).reshape(B, L, RET_HEADS, RET_HEAD_DIM)
        qh = rope(heads(q))
        kh = rope(heads(k)) * (RET_HEAD_DIM ** -0.5)
        vh = heads(v)
        y_ret = head_groupnorm(retention_chunkwise(qh, kh, vh), ret_gn_gain[i]).reshape(B, L, RET_WIDTH)
        y_ret = (jax.nn.silu(gate.astype(jnp.float32)) * y_ret).astype(x.dtype)

        y_ssm = jax.nn.gelu(s5_scan(u, ssm_lambda_re[i], ssm_lambda_im[i], ssm_log_dt[i], ssm_b_re[i],
                                    ssm_b_im[i], ssm_c_re[i], ssm_c_im[i], ssm_d[i])).astype(x.dtype)
        glu_a, glu_b = jnp.split(y_ssm @ w_glu[i], 2, axis=-1)
        y_ssm = glu_a * jax.nn.sigmoid(glu_b)

        mix = jnp.concatenate([y_ret, y_ssm], axis=-1) @ w_out[i]
        x = x + rmsnorm(mix, norm_mix_post[i])

        h = rmsnorm(x, norm_mlp_pre[i])
        m = jnp.square(jax.nn.relu(h @ w_ff1[i])) @ w_ff2[i]
        x = x + rmsnorm(m, norm_mlp_post[i])
    return x


import jax as _jax
import jax.numpy as _jnp

TWIN_FORMAT = 'train_step'
FWD_PARAMS = ['x', 'norm_mix_pre', 'norm_mix_post', 'w_in', 'ret_gn_gain', 'ssm_lambda_re', 'ssm_lambda_im', 'ssm_log_dt', 'ssm_b_re', 'ssm_b_im', 'ssm_c_re', 'ssm_c_im', 'ssm_d', 'w_glu', 'w_out', 'norm_mlp_pre', 'norm_mlp_post', 'w_ff1', 'w_ff2']
TWIN_WEIGHTS = ['norm_mix_pre', 'norm_mix_post', 'w_in', 'ret_gn_gain', 'ssm_lambda_re', 'ssm_lambda_im', 'ssm_log_dt', 'ssm_b_re', 'ssm_b_im', 'ssm_c_re', 'ssm_c_im', 'ssm_d', 'w_glu', 'w_out', 'norm_mlp_pre', 'norm_mlp_post', 'w_ff1', 'w_ff2']
TWIN_DIFF_INPUT = 'x'
TWIN_INPUTS = ['x', 'norm_mix_pre', 'norm_mix_post', 'w_in', 'ret_gn_gain', 'ssm_lambda_re', 'ssm_lambda_im', 'ssm_log_dt', 'ssm_b_re', 'ssm_b_im', 'ssm_c_re', 'ssm_c_im', 'ssm_d', 'w_glu', 'w_out', 'norm_mlp_pre', 'norm_mlp_post', 'w_ff1', 'w_ff2', 'loss_target', 'm_norm_mix_pre', 'm_norm_mix_post', 'm_w_in', 'm_ret_gn_gain', 'm_ssm_lambda_re', 'm_ssm_lambda_im', 'm_ssm_log_dt', 'm_ssm_b_re', 'm_ssm_b_im', 'm_ssm_c_re', 'm_ssm_c_im', 'm_ssm_d', 'm_w_glu', 'm_w_out', 'm_norm_mlp_pre', 'm_norm_mlp_post', 'm_w_ff1', 'm_w_ff2', 'v_norm_mix_pre', 'v_norm_mix_post', 'v_w_in', 'v_ret_gn_gain', 'v_ssm_lambda_re', 'v_ssm_lambda_im', 'v_ssm_log_dt', 'v_ssm_b_re', 'v_ssm_b_im', 'v_ssm_c_re', 'v_ssm_c_im', 'v_ssm_d', 'v_w_glu', 'v_w_out', 'v_norm_mlp_pre', 'v_norm_mlp_post', 'v_w_ff1', 'v_w_ff2']
TWIN_OUTPUTS = ['loss', 'grad_x', 'grad_norm_mix_pre', 'grad_norm_mix_post', 'grad_w_in', 'grad_ret_gn_gain', 'grad_ssm_lambda_re', 'grad_ssm_lambda_im', 'grad_ssm_log_dt', 'grad_ssm_b_re', 'grad_ssm_b_im', 'grad_ssm_c_re', 'grad_ssm_c_im', 'grad_ssm_d', 'grad_w_glu', 'grad_w_out', 'grad_norm_mlp_pre', 'grad_norm_mlp_post', 'grad_w_ff1', 'grad_w_ff2', 'delta_norm_mix_pre', 'delta_norm_mix_post', 'delta_w_in', 'delta_ret_gn_gain', 'delta_ssm_lambda_re', 'delta_ssm_lambda_im', 'delta_ssm_log_dt', 'delta_ssm_b_re', 'delta_ssm_b_im', 'delta_ssm_c_re', 'delta_ssm_c_im', 'delta_ssm_d', 'delta_w_glu', 'delta_w_out', 'delta_norm_mlp_pre', 'delta_norm_mlp_post', 'delta_w_ff1', 'delta_w_ff2', 'new_m_norm_mix_pre', 'new_m_norm_mix_post', 'new_m_w_in', 'new_m_ret_gn_gain', 'new_m_ssm_lambda_re', 'new_m_ssm_lambda_im', 'new_m_ssm_log_dt', 'new_m_ssm_b_re', 'new_m_ssm_b_im', 'new_m_ssm_c_re', 'new_m_ssm_c_im', 'new_m_ssm_d', 'new_m_w_glu', 'new_m_w_out', 'new_m_norm_mlp_pre', 'new_m_norm_mlp_post', 'new_m_w_ff1', 'new_m_w_ff2', 'new_v_norm_mix_pre', 'new_v_norm_mix_post', 'new_v_w_in', 'new_v_ret_gn_gain', 'new_v_ssm_lambda_re', 'new_v_ssm_lambda_im', 'new_v_ssm_log_dt', 'new_v_ssm_b_re', 'new_v_ssm_b_im', 'new_v_ssm_c_re', 'new_v_ssm_c_im', 'new_v_ssm_d', 'new_v_w_glu', 'new_v_w_out', 'new_v_norm_mlp_pre', 'new_v_norm_mlp_post', 'new_v_w_ff1', 'new_v_w_ff2']
TWIN_LEAF_KINDS = {'loss': 'loss', 'grad_x': 'grad_x', 'grad_norm_mix_pre': 'grad_w', 'grad_norm_mix_post': 'grad_w', 'grad_w_in': 'grad_w', 'grad_ret_gn_gain': 'grad_w', 'grad_ssm_lambda_re': 'grad_w', 'grad_ssm_lambda_im': 'grad_w', 'grad_ssm_log_dt': 'grad_w', 'grad_ssm_b_re': 'grad_w', 'grad_ssm_b_im': 'grad_w', 'grad_ssm_c_re': 'grad_w', 'grad_ssm_c_im': 'grad_w', 'grad_ssm_d': 'grad_w', 'grad_w_glu': 'grad_w', 'grad_w_out': 'grad_w', 'grad_norm_mlp_pre': 'grad_w', 'grad_norm_mlp_post': 'grad_w', 'grad_w_ff1': 'grad_w', 'grad_w_ff2': 'grad_w', 'delta_norm_mix_pre': 'delta_w', 'delta_norm_mix_post': 'delta_w', 'delta_w_in': 'delta_w', 'delta_ret_gn_gain': 'delta_w', 'delta_ssm_lambda_re': 'delta_w', 'delta_ssm_lambda_im': 'delta_w', 'delta_ssm_log_dt': 'delta_w', 'delta_ssm_b_re': 'delta_w', 'delta_ssm_b_im': 'delta_w', 'delta_ssm_c_re': 'delta_w', 'delta_ssm_c_im': 'delta_w', 'delta_ssm_d': 'delta_w', 'delta_w_glu': 'delta_w', 'delta_w_out': 'delta_w', 'delta_norm_mlp_pre': 'delta_w', 'delta_norm_mlp_post': 'delta_w', 'delta_w_ff1': 'delta_w', 'delta_w_ff2': 'delta_w', 'new_m_norm_mix_pre': 'new_m', 'new_m_norm_mix_post': 'new_m', 'new_m_w_in': 'new_m', 'new_m_ret_gn_gain': 'new_m', 'new_m_ssm_lambda_re': 'new_m', 'new_m_ssm_lambda_im': 'new_m', 'new_m_ssm_log_dt': 'new_m', 'new_m_ssm_b_re': 'new_m', 'new_m_ssm_b_im': 'new_m', 'new_m_ssm_c_re': 'new_m', 'new_m_ssm_c_im': 'new_m', 'new_m_ssm_d': 'new_m', 'new_m_w_glu': 'new_m', 'new_m_w_out': 'new_m', 'new_m_norm_mlp_pre': 'new_m', 'new_m_norm_mlp_post': 'new_m', 'new_m_w_ff1': 'new_m', 'new_m_w_ff2': 'new_m', 'new_v_norm_mix_pre': 'new_v', 'new_v_norm_mix_post': 'new_v', 'new_v_w_in': 'new_v', 'new_v_ret_gn_gain': 'new_v', 'new_v_ssm_lambda_re': 'new_v', 'new_v_ssm_lambda_im': 'new_v', 'new_v_ssm_log_dt': 'new_v', 'new_v_ssm_b_re': 'new_v', 'new_v_ssm_b_im': 'new_v', 'new_v_ssm_c_re': 'new_v', 'new_v_ssm_c_im': 'new_v', 'new_v_ssm_d': 'new_v', 'new_v_w_glu': 'new_v', 'new_v_w_out': 'new_v', 'new_v_norm_mlp_pre': 'new_v', 'new_v_norm_mlp_post': 'new_v', 'new_v_w_ff1': 'new_v', 'new_v_w_ff2': 'new_v'}


def _forward(args):
    return _fwd_reference(*[args[k] for k in FWD_PARAMS])


def _output_shape():
    def fwd():
        inp = _fwd_setup_inputs(0)
        return _fwd_reference(*[inp[k] for k in FWD_PARAMS])
    out = _jax.eval_shape(fwd)
    return out.shape, out.dtype

N_MICROBATCH = 1
ADAM_LR = 0.001
ADAM_B1 = 0.9
ADAM_B2 = 0.999
ADAM_EPS = 1e-08
ADAM_WD = 0.01
ADAM_STEP = 10
PER_EXAMPLE_BATCH_AXIS = {'x': 0, 'loss_target': 0}
SHARED_INPUTS = []
_WEIGHT_DTYPES = {'norm_mix_pre': _jnp.float32, 'norm_mix_post': _jnp.float32, 'w_in': _jnp.float32, 'ret_gn_gain': _jnp.float32, 'ssm_lambda_re': _jnp.float32, 'ssm_lambda_im': _jnp.float32, 'ssm_log_dt': _jnp.float32, 'ssm_b_re': _jnp.float32, 'ssm_b_im': _jnp.float32, 'ssm_c_re': _jnp.float32, 'ssm_c_im': _jnp.float32, 'ssm_d': _jnp.float32, 'w_glu': _jnp.float32, 'w_out': _jnp.float32, 'norm_mlp_pre': _jnp.float32, 'norm_mlp_post': _jnp.float32, 'w_ff1': _jnp.float32, 'w_ff2': _jnp.float32}
MOMENT_SCALE = {'norm_mix_pre': 1.418727e+00, 'norm_mix_post': 6.462435e+01, 'w_in': 7.544300e-01, 'ret_gn_gain': 1.802169e+00, 'ssm_lambda_re': 2.945242e-02, 'ssm_lambda_im': 3.583090e-02, 'ssm_log_dt': 2.497435e+01, 'ssm_b_re': 1.758791e-02, 'ssm_b_im': 1.898992e-02, 'ssm_c_re': 3.781282e-02, 'ssm_c_im': 3.607607e-02, 'ssm_d': 1.008720e+01, 'w_glu': 6.383799e+00, 'w_out': 6.165068e+00, 'norm_mlp_pre': 2.130698e+00, 'norm_mlp_post': 6.645285e+01, 'w_ff1': 1.103276e+00, 'w_ff2': 6.167134e+00}


def _to_microbatches(a, axis):
    t = _jnp.moveaxis(a, axis, 0)
    t = t.reshape((N_MICROBATCH, t.shape[0] // N_MICROBATCH) + t.shape[1:])
    return _jnp.moveaxis(t, 1, axis + 1)


def setup_inputs(seed: int = 0) -> dict:
    inp = _fwd_setup_inputs(seed)
    key = _jax.random.fold_in(_jax.random.key(seed), 7919)
    shape, _ = _output_shape()
    out = dict(inp)
    out["loss_target"] = _jax.random.normal(_jax.random.fold_in(key, 0), shape, _jnp.float32)
    for i, name in enumerate(TWIN_WEIGHTS):
        w = inp[name].astype(_jnp.float32)
        if MOMENT_SCALE is None:
            s = _jnp.sqrt(_jnp.mean(_jnp.square(w)) + 1e-30)
        else:
            s = MOMENT_SCALE[name]
        km, kv = _jax.random.split(_jax.random.fold_in(key, i + 1))
        out[name] = w
        out["m_" + name] = s * _jax.random.normal(km, w.shape, _jnp.float32)
        out["v_" + name] = (s * s) * _jax.random.uniform(kv, w.shape, _jnp.float32, 0.5, 1.5)
    if N_MICROBATCH > 1:
        for name, axis in PER_EXAMPLE_BATCH_AXIS.items():
            out[name] = _to_microbatches(out[name], axis)
    return {'x': out['x'], 'norm_mix_pre': out['norm_mix_pre'], 'norm_mix_post': out['norm_mix_post'], 'w_in': out['w_in'], 'ret_gn_gain': out['ret_gn_gain'], 'ssm_lambda_re': out['ssm_lambda_re'], 'ssm_lambda_im': out['ssm_lambda_im'], 'ssm_log_dt': out['ssm_log_dt'], 'ssm_b_re': out['ssm_b_re'], 'ssm_b_im': out['ssm_b_im'], 'ssm_c_re': out['ssm_c_re'], 'ssm_c_im': out['ssm_c_im'], 'ssm_d': out['ssm_d'], 'w_glu': out['w_glu'], 'w_out': out['w_out'], 'norm_mlp_pre': out['norm_mlp_pre'], 'norm_mlp_post': out['norm_mlp_post'], 'w_ff1': out['w_ff1'], 'w_ff2': out['w_ff2'], 'loss_target': out['loss_target'], 'm_norm_mix_pre': out['m_norm_mix_pre'], 'm_norm_mix_post': out['m_norm_mix_post'], 'm_w_in': out['m_w_in'], 'm_ret_gn_gain': out['m_ret_gn_gain'], 'm_ssm_lambda_re': out['m_ssm_lambda_re'], 'm_ssm_lambda_im': out['m_ssm_lambda_im'], 'm_ssm_log_dt': out['m_ssm_log_dt'], 'm_ssm_b_re': out['m_ssm_b_re'], 'm_ssm_b_im': out['m_ssm_b_im'], 'm_ssm_c_re': out['m_ssm_c_re'], 'm_ssm_c_im': out['m_ssm_c_im'], 'm_ssm_d': out['m_ssm_d'], 'm_w_glu': out['m_w_glu'], 'm_w_out': out['m_w_out'], 'm_norm_mlp_pre': out['m_norm_mlp_pre'], 'm_norm_mlp_post': out['m_norm_mlp_post'], 'm_w_ff1': out['m_w_ff1'], 'm_w_ff2': out['m_w_ff2'], 'v_norm_mix_pre': out['v_norm_mix_pre'], 'v_norm_mix_post': out['v_norm_mix_post'], 'v_w_in': out['v_w_in'], 'v_ret_gn_gain': out['v_ret_gn_gain'], 'v_ssm_lambda_re': out['v_ssm_lambda_re'], 'v_ssm_lambda_im': out['v_ssm_lambda_im'], 'v_ssm_log_dt': out['v_ssm_log_dt'], 'v_ssm_b_re': out['v_ssm_b_re'], 'v_ssm_b_im': out['v_ssm_b_im'], 'v_ssm_c_re': out['v_ssm_c_re'], 'v_ssm_c_im': out['v_ssm_c_im'], 'v_ssm_d': out['v_ssm_d'], 'v_w_glu': out['v_w_glu'], 'v_w_out': out['v_w_out'], 'v_norm_mlp_pre': out['v_norm_mlp_pre'], 'v_norm_mlp_post': out['v_norm_mlp_post'], 'v_w_ff1': out['v_w_ff1'], 'v_w_ff2': out['v_w_ff2']}


def _loss(weights, diff, rest, loss_target):
    with _jax.named_scope("forward"):
        args = {**rest, TWIN_DIFF_INPUT: diff, **{k: w.astype(_WEIGHT_DTYPES[k]) for k, w in weights.items()}}
        y = _forward(args)
    with _jax.named_scope("loss_head"):
        err = _jnp.square(y.astype(_jnp.float32) - loss_target)
        return 0.5 * _jnp.sum(_jnp.mean(err, axis=-1)) if err.ndim else 0.5 * err


def _adamw(w, g, m, v):
    m = ADAM_B1 * m + (1.0 - ADAM_B1) * g
    v = ADAM_B2 * v + (1.0 - ADAM_B2) * _jnp.square(g)
    m_hat = m / (1.0 - ADAM_B1 ** ADAM_STEP)
    v_hat = v / (1.0 - ADAM_B2 ** ADAM_STEP)
    delta = -ADAM_LR * (m_hat / (_jnp.sqrt(v_hat) + ADAM_EPS) + ADAM_WD * w)
    return delta, m, v


def reference(x, norm_mix_pre, norm_mix_post, w_in, ret_gn_gain, ssm_lambda_re, ssm_lambda_im, ssm_log_dt, ssm_b_re, ssm_b_im, ssm_c_re, ssm_c_im, ssm_d, w_glu, w_out, norm_mlp_pre, norm_mlp_post, w_ff1, w_ff2, loss_target, m_norm_mix_pre, m_norm_mix_post, m_w_in, m_ret_gn_gain, m_ssm_lambda_re, m_ssm_lambda_im, m_ssm_log_dt, m_ssm_b_re, m_ssm_b_im, m_ssm_c_re, m_ssm_c_im, m_ssm_d, m_w_glu, m_w_out, m_norm_mlp_pre, m_norm_mlp_post, m_w_ff1, m_w_ff2, v_norm_mix_pre, v_norm_mix_post, v_w_in, v_ret_gn_gain, v_ssm_lambda_re, v_ssm_lambda_im, v_ssm_log_dt, v_ssm_b_re, v_ssm_b_im, v_ssm_c_re, v_ssm_c_im, v_ssm_d, v_w_glu, v_w_out, v_norm_mlp_pre, v_norm_mlp_post, v_w_ff1, v_w_ff2):
    given = dict(x=x, norm_mix_pre=norm_mix_pre, norm_mix_post=norm_mix_post, w_in=w_in, ret_gn_gain=ret_gn_gain, ssm_lambda_re=ssm_lambda_re, ssm_lambda_im=ssm_lambda_im, ssm_log_dt=ssm_log_dt, ssm_b_re=ssm_b_re, ssm_b_im=ssm_b_im, ssm_c_re=ssm_c_re, ssm_c_im=ssm_c_im, ssm_d=ssm_d, w_glu=w_glu, w_out=w_out, norm_mlp_pre=norm_mlp_pre, norm_mlp_post=norm_mlp_post, w_ff1=w_ff1, w_ff2=w_ff2, loss_target=loss_target, m_norm_mix_pre=m_norm_mix_pre, m_norm_mix_post=m_norm_mix_post, m_w_in=m_w_in, m_ret_gn_gain=m_ret_gn_gain, m_ssm_lambda_re=m_ssm_lambda_re, m_ssm_lambda_im=m_ssm_lambda_im, m_ssm_log_dt=m_ssm_log_dt, m_ssm_b_re=m_ssm_b_re, m_ssm_b_im=m_ssm_b_im, m_ssm_c_re=m_ssm_c_re, m_ssm_c_im=m_ssm_c_im, m_ssm_d=m_ssm_d, m_w_glu=m_w_glu, m_w_out=m_w_out, m_norm_mlp_pre=m_norm_mlp_pre, m_norm_mlp_post=m_norm_mlp_post, m_w_ff1=m_w_ff1, m_w_ff2=m_w_ff2, v_norm_mix_pre=v_norm_mix_pre, v_norm_mix_post=v_norm_mix_post, v_w_in=v_w_in, v_ret_gn_gain=v_ret_gn_gain, v_ssm_lambda_re=v_ssm_lambda_re, v_ssm_lambda_im=v_ssm_lambda_im, v_ssm_log_dt=v_ssm_log_dt, v_ssm_b_re=v_ssm_b_re, v_ssm_b_im=v_ssm_b_im, v_ssm_c_re=v_ssm_c_re, v_ssm_c_im=v_ssm_c_im, v_ssm_d=v_ssm_d, v_w_glu=v_w_glu, v_w_out=v_w_out, v_norm_mlp_pre=v_norm_mlp_pre, v_norm_mlp_post=v_norm_mlp_post, v_w_ff1=v_w_ff1, v_w_ff2=v_w_ff2)
    weights = {n: given[n] for n in TWIN_WEIGHTS}
    shared = {n: given[n] for n in SHARED_INPUTS}
    per_example = {n: given[n] for n in ['x']}
    grad_fn = _jax.value_and_grad(_loss, argnums=(0, 1))

    def one_microbatch(ex, loss_target):
        ex = dict(ex)
        diff = ex.pop(TWIN_DIFF_INPUT)
        return grad_fn(weights, diff, {**shared, **ex}, loss_target)

    if N_MICROBATCH == 1:
        loss, (grad_w, grad_x) = one_microbatch(per_example, given["loss_target"])
    else:
        def body(carry, xs):
            loss_sum, grad_sum = carry
            l_k, (gw_k, gx_k) = one_microbatch(xs[0], xs[1])
            with _jax.named_scope("update"):
                return (loss_sum + l_k, _jax.tree.map(_jnp.add, grad_sum, gw_k)), gx_k

        init = (_jnp.zeros((), _jnp.float32), _jax.tree.map(_jnp.zeros_like, weights))
        (loss, grad_w), grad_x = _jax.lax.scan(body, init, (per_example, given["loss_target"]))
    with _jax.named_scope("update"):
        delta_w, new_m, new_v = {}, {}, {}
        for n in TWIN_WEIGHTS:
            delta_w[n], new_m[n], new_v[n] = _adamw(weights[n], grad_w[n], given["m_" + n], given["v_" + n])
    return (loss, grad_x, *[grad_w[n] for n in TWIN_WEIGHTS], *[delta_w[n] for n in TWIN_WEIGHTS],
            *[new_m[n] for n in TWIN_WEIGHTS], *[new_v[n] for n in TWIN_WEIGHTS])
```

```python
import functools
import math

import jax
import jax.numpy as jnp
from jax import lax
from jax.experimental import pallas as pl
from jax.experimental.pallas import tpu as pltpu

_BF = jnp.bfloat16
_F32 = jnp.float32

D_MODEL = 1024
RET_W = 512
N_HEAD = 4
HEAD_D = 128
CHUNK = 128
SSM_W = 512
SSM_GC = 16
N_GROUP = 32
N_STATE = 64
GROUPS_PER_KB = 8
N_KB = 4
KB_STATES = GROUPS_PER_KB * N_STATE
D_FF = 4096
IN_COLS = 2560
NORM_EPS = 1e-6
ROPE_BASE = 10000.0
N_DEV = 8

ADAM_LR = 0.001
ADAM_B1 = 0.9
ADAM_B2 = 0.999
ADAM_EPS = 1e-08
ADAM_WD = 0.01
ADAM_STEP = 10

SUBLANES = 8
LANES = 128
VMEM_LIMIT = 52 * 1024 * 1024
SCAN_STRIP = 256

MESH = pl.DeviceIdType.MESH


def _params(*sem):
    return pltpu.CompilerParams(dimension_semantics=sem, vmem_limit_bytes=VMEM_LIMIT)


def _dot(a, b):
    return jnp.dot(a, b, preferred_element_type=_F32)


def _dot_nt(a, b):
    return lax.dot_general(a, b, (((1,), (1,)), ((), ())), preferred_element_type=_F32)


def _dot_tn(a, b):
    return lax.dot_general(a, b, (((0,), (0,)), ((), ())), preferred_element_type=_F32)


def _rms_r(z):
    return lax.rsqrt(jnp.mean(z * z, axis=-1, keepdims=True) + NORM_EPS)


def _rms_bwd(z, g, dn):
    r = _rms_r(z)
    t = dn * g
    dz = r * t - z * (r * r * r * jnp.mean(t * z, axis=-1, keepdims=True))
    return dz, dn * z * r


def _rope(t, cs, sn):
    return t * cs + pltpu.roll(t, HEAD_D // 2, 1) * sn


def _rope_t(t, cs, sn):
    return t * cs - pltpu.roll(t, HEAD_D // 2, 1) * sn


def _sigmoid(z):
    return 1.0 / (1.0 + jnp.exp(-z))


_GELU_C = math.sqrt(2.0 / math.pi)


def _gelu(z):
    return 0.5 * z * (1.0 + jnp.tanh(_GELU_C * (z + 0.044715 * z * z * z)))


def _gelu_grad(z):
    th = jnp.tanh(_GELU_C * (z + 0.044715 * z * z * z))
    return 0.5 * (1.0 + th) + 0.5 * z * (1.0 - th * th) * _GELU_C * (1.0 + 3 * 0.044715 * z * z)


def _row_spec(tm, n):
    return pl.BlockSpec((tm, n), lambda i: (i, 0))


def _full_spec(shape):
    nd = len(shape)
    return pl.BlockSpec(shape, lambda i: (0,) * nd)


def _inproj_fwd(x, g1, w_in, cosf, sinf, tm):
    L = x.shape[0]

    def body(x_ref, g_ref, w_ref, cos_ref, sin_ref, h_ref, q_ref, k_ref, v_ref, gate_ref, u_ref):
        xv = x_ref[...]
        h = (xv * _rms_r(xv) * g_ref[...]).astype(_BF)
        h_ref[...] = h
        proj = _dot(h, w_ref[...])
        cs, sn = cos_ref[...], sin_ref[...]
        for hh in range(N_HEAD):
            lo = hh * HEAD_D
            q_ref[:, lo:lo + HEAD_D] = _rope(proj[:, lo:lo + HEAD_D], cs, sn).astype(_BF)
            kh = _rope(proj[:, RET_W + lo:RET_W + lo + HEAD_D], cs, sn) * (HEAD_D ** -0.5)
            k_ref[:, lo:lo + HEAD_D] = kh.astype(_BF)
        v_ref[...] = proj[:, 2 * RET_W:3 * RET_W].astype(_BF)
        gate_ref[...] = proj[:, 3 * RET_W:4 * RET_W]
        u_ref[...] = proj[:, 4 * RET_W:]

    return pl.pallas_call(
        body, name="inproj_fwd", grid=(L // tm,),
        in_specs=[_row_spec(tm, D_MODEL), _full_spec((1, D_MODEL)), _full_spec((D_MODEL, IN_COLS)),
                  _row_spec(tm, HEAD_D), _row_spec(tm, HEAD_D)],
        out_specs=[_row_spec(tm, D_MODEL)] + [_row_spec(tm, RET_W)] * 5,
        out_shape=[jax.ShapeDtypeStruct((L, D_MODEL), _BF)] + [jax.ShapeDtypeStruct((L, RET_W), _BF)] * 3
        + [jax.ShapeDtypeStruct((L, RET_W), _F32)] * 2,
        compiler_params=_params("parallel"),
    )(x, g1, w_in, cosf, sinf)


def _ret_consts():
    lg = jnp.log(1.0 - jnp.exp(jnp.linspace(math.log(1.0 / 32), math.log(1.0 / 512), N_HEAD))).astype(_F32)
    idx = jnp.arange(CHUNK, dtype=_F32)
    diff = idx[:, None] - idx[None, :]
    decay = jnp.where(diff[None] >= 0, jnp.exp(jnp.maximum(diff, 0.0)[None] * lg[:, None, None]), 0.0)
    zeta = jnp.exp((CHUNK - 1 - idx)[None, :] * lg[:, None])
    xi = jnp.exp((idx + 1.0)[None, :] * lg[:, None])
    gc = jnp.exp(CHUNK * lg)
    wide = lambda t: jnp.broadcast_to(t[:, :, None], (N_HEAD, CHUNK, HEAD_D)).astype(_F32)
    gcw = jnp.broadcast_to(gc[:, None, None], (N_HEAD, SUBLANES, HEAD_D)).astype(_F32)
    return decay.astype(_F32), wide(xi), wide(zeta), gcw


def _head_specs():
    c3 = pl.BlockSpec((1, CHUNK, CHUNK), lambda h, n: (h, 0, 0))
    g3 = pl.BlockSpec((1, SUBLANES, HEAD_D), lambda h, n: (h, 0, 0))
    return [c3, c3, c3, g3]


def _retention_fwd(q, k, v, gate, ggn, consts):
    L = q.shape[0]
    nc = L // CHUNK
    blk = pl.BlockSpec((CHUNK, HEAD_D), lambda h, n: (n, h))

    def body(q_ref, k_ref, v_ref, gate_ref, ggn_ref, dm_ref, xi_ref, zeta_ref, gc_ref,
             o_ref, y_ref, rp_ref, r_scr):
        @pl.when(pl.program_id(1) == 0)
        def _():
            r_scr[...] = jnp.zeros_like(r_scr)

        qv, kv, vv = q_ref[...], k_ref[...], v_ref[...]
        r_prev = r_scr[...]
        s = _dot_nt(qv, kv) * dm_ref[0]
        o = _dot(s.astype(_BF), vv) + _dot(qv, r_prev.astype(_BF)) * xi_ref[0]
        o_ref[...] = o
        rp_ref[0, 0] = r_prev
        vz = (vv.astype(_F32) * zeta_ref[0]).astype(_BF)
        r_scr[...] = gc_ref[0, 0:1, :] * r_prev + _dot_tn(kv, vz)
        dlt = o - jnp.mean(o, axis=-1, keepdims=True)
        on = dlt * lax.rsqrt(jnp.mean(dlt * dlt, axis=-1, keepdims=True) + NORM_EPS)
        gt = gate_ref[...]
        y_ref[...] = (gt * _sigmoid(gt) * (on * ggn_ref[...])).astype(_BF)

    return pl.pallas_call(
        body, name="retention_fwd", grid=(N_HEAD, nc),
        in_specs=[blk, blk, blk, blk, pl.BlockSpec((1, HEAD_D), lambda h, n: (0, h))] + _head_specs(),
        out_specs=[blk, blk, pl.BlockSpec((1, 1, HEAD_D, HEAD_D), lambda h, n: (h, n, 0, 0))],
        out_shape=[jax.ShapeDtypeStruct((L, RET_W), _F32), jax.ShapeDtypeStruct((L, RET_W), _BF),
                   jax.ShapeDtypeStruct((N_HEAD, nc, HEAD_D, HEAD_D), _F32)],
        scratch_shapes=[pltpu.VMEM((HEAD_D, HEAD_D), _F32)],
        compiler_params=_params("parallel", "arbitrary"),
    )(q, k, v, gate, ggn, *consts)


def _scan_rows(x_ref, tab_ref, carry_ref, n_rows, reverse, xprev_ref=None):
    n_grp = n_rows // SUBLANES
    for strip in range(KB_STATES // SCAN_STRIP):
        re = pl.ds(strip * SCAN_STRIP, SCAN_STRIP)
        im = pl.ds(KB_STATES + strip * SCAN_STRIP, SCAN_STRIP)
        tabs = [tab_ref[0, j, :, re] for j in range(8)]
        first_row = lax.broadcasted_iota(jnp.int32, (SUBLANES, SCAN_STRIP), 0) == 0

        def step(i, carry, re=re, im=im, tabs=tabs, first_row=first_row):
            cr, ci = carry
            grp = (n_grp - 1 - i) if reverse else i
            rows = pl.ds(pl.multiple_of(grp * SUBLANES, SUBLANES), SUBLANES)
            xr, xi = x_ref[rows, re], x_ref[rows, im]
            for j, dist in enumerate((1, 2, 4)):
                pr, pi = tabs[2 * j], tabs[2 * j + 1]
                shift = (SUBLANES - dist) if reverse else dist
                sr, si = pltpu.roll(xr, shift, 0), pltpu.roll(xi, shift, 0)
                xr, xi = xr + pr * sr - pi * si, xi + pr * si + pi * sr
            pr, pi = tabs[6], tabs[7]
            xr, xi = xr + pr * cr - pi * ci, xi + pr * ci + pi * cr
            x_ref[rows, re] = xr
            x_ref[rows, im] = xi
            if xprev_ref is not None:
                xprev_ref[rows, re] = jnp.where(first_row, cr, pltpu.roll(xr, 1, 0))
                xprev_ref[rows, im] = jnp.where(first_row, ci, pltpu.roll(xi, 1, 0))
            edge = 0 if reverse else SUBLANES - 1
            ncr = jnp.broadcast_to(xr[edge:edge + 1, :], (SUBLANES, SCAN_STRIP))
            nci = jnp.broadcast_to(xi[edge:edge + 1, :], (SUBLANES, SCAN_STRIP))
            return ncr, nci

        cr, ci = lax.fori_loop(0, n_grp, step, (carry_ref[:, re], carry_ref[:, im]))
        carry_ref[:, re] = cr
        carry_ref[:, im] = ci


def _s5_fwd(u, bmat, cmat, tab_f, d_skip, tb):
    L = u.shape[0]
    nt = L // tb
    ucol = pl.BlockSpec((tb, LANES), lambda kb, t: (t, kb))

    def body(u_ref, b_ref, c_ref, tab_ref, d_ref, s_ref, cin_ref, x_scr, carry_scr):
        @pl.when(pl.program_id(1) == 0)
        def _():
            carry_scr[...] = jnp.zeros_like(carry_scr)

        cin_ref[0, 0] = carry_scr[...]
        uv = u_ref[...]
        x_scr[...] = _dot(uv.astype(_BF), b_ref[0])
        _scan_rows(x_scr, tab_ref, carry_scr, tb, reverse=False)
        s_ref[...] = _dot(x_scr[...].astype(_BF), c_ref[0]) + d_ref[...] * uv

    return pl.pallas_call(
        body, name="s5_fwd", grid=(N_KB, nt),
        in_specs=[ucol, pl.BlockSpec((1, LANES, 2 * KB_STATES), lambda kb, t: (kb, 0, 0)),
                  pl.BlockSpec((1, 2 * KB_STATES, LANES), lambda kb, t: (kb, 0, 0)),
                  pl.BlockSpec((1, 8, SUBLANES, KB_STATES), lambda kb, t: (kb, 0, 0, 0)),
                  pl.BlockSpec((1, LANES), lambda kb, t: (0, kb))],
        out_specs=[ucol, pl.BlockSpec((1, 1, SUBLANES, 2 * KB_STATES), lambda kb, t: (kb, t, 0, 0))],
        out_shape=[jax.ShapeDtypeStruct((L, SSM_W), _F32),
                   jax.ShapeDtypeStruct((N_KB, nt, SUBLANES, 2 * KB_STATES), _F32)],
        scratch_shapes=[pltpu.VMEM((tb, 2 * KB_STATES), _F32), pltpu.VMEM((SUBLANES, 2 * KB_STATES), _F32)],
        compiler_params=_params("parallel", "arbitrary"),
    )(u, bmat, cmat, tab_f, d_skip)


def _mixout_fwd(s, y_ret, x, w_glu, w_out, g2, tm):
    L = s.shape[0]

    def body(s_ref, yr_ref, x_ref, wg_ref, wo_ref, g_ref, ys_ref, glu_ref, yssm_ref, mix_ref, x2_ref):
        ys = _gelu(s_ref[...]).astype(_BF)
        ys_ref[...] = ys
        glu = _dot(ys, wg_ref[...])
        glu_ref[...] = glu
        y_ssm = (glu[:, :SSM_W] * _sigmoid(glu[:, SSM_W:])).astype(_BF)
        yssm_ref[...] = y_ssm
        mix = _dot(yr_ref[...], wo_ref[:RET_W, :]) + _dot(y_ssm, wo_ref[RET_W:, :])
        mix_ref[...] = mix
        x2_ref[...] = x_ref[...] + mix * _rms_r(mix) * g_ref[...]

    return pl.pallas_call(
        body, name="mixout_fwd", grid=(L // tm,),
        in_specs=[_row_spec(tm, SSM_W), _row_spec(tm, RET_W), _row_spec(tm, D_MODEL),
                  _full_spec((SSM_W, 2 * SSM_W)), _full_spec((D_MODEL, D_MODEL)), _full_spec((1, D_MODEL))],
        out_specs=[_row_spec(tm, SSM_W), _row_spec(tm, 2 * SSM_W), _row_spec(tm, SSM_W),
                   _row_spec(tm, D_MODEL), _row_spec(tm, D_MODEL)],
        out_shape=[jax.ShapeDtypeStruct((L, SSM_W), _BF), jax.ShapeDtypeStruct((L, 2 * SSM_W), _F32),
                   jax.ShapeDtypeStruct((L, SSM_W), _BF), jax.ShapeDtypeStruct((L, D_MODEL), _F32),
                   jax.ShapeDtypeStruct((L, D_MODEL), _F32)],
        compiler_params=_params("parallel"),
    )(s, y_ret, x, w_glu, w_out, g2)


def _ff1_fwd(x2, g3, w1, tm):
    L = x2.shape[0]

    def body(x_ref, g_ref, w_ref, h_ref, f_ref):
        xv = x_ref[...]
        h = (xv * _rms_r(xv) * g_ref[...]).astype(_BF)
        h_ref[...] = h
        f_ref[...] = _dot(h, w_ref[...])

    return pl.pallas_call(
        body, name="ff1_fwd", grid=(L // tm,),
        in_specs=[_row_spec(tm, D_MODEL), _full_spec((1, D_MODEL)), _full_spec((D_MODEL, D_FF))],
        out_specs=[_row_spec(tm, D_MODEL), _row_spec(tm, D_FF)],
        out_shape=[jax.ShapeDtypeStruct((L, D_MODEL), _BF), jax.ShapeDtypeStruct((L, D_FF), _F32)],
        compiler_params=_params("parallel"),
    )(x2, g3, w1)


def _ff2_loss(f1, x2, tgt, g4, w2, tm):
    L = f1.shape[0]

    def body(f_ref, x_ref, t_ref, g_ref, w_ref, dy_ref, dm_ref, dg_ref, ls_ref):
        @pl.when(pl.program_id(0) == 0)
        def _():
            dg_ref[...] = jnp.zeros_like(dg_ref)
            ls_ref[...] = jnp.zeros_like(ls_ref)

        rl = jnp.maximum(f_ref[...], 0.0)
        m = _dot((rl * rl).astype(_BF), w_ref[...])
        g = g_ref[...]
        y = x_ref[...] + m * _rms_r(m) * g
        err = y - t_ref[...]
        ls_ref[...] += jnp.sum(err * err, axis=0, keepdims=True)
        dy = err * (1.0 / D_MODEL)
        dy_ref[...] = dy
        dm, dgr = _rms_bwd(m, g, dy)
        dm_ref[...] = dm.astype(_BF)
        dg_ref[...] += jnp.sum(dgr, axis=0, keepdims=True)

    return pl.pallas_call(
        body, name="ff2_loss", grid=(L // tm,),
        in_specs=[_row_spec(tm, D_FF), _row_spec(tm, D_MODEL), _row_spec(tm, D_MODEL),
                  _full_spec((1, D_MODEL)), _full_spec((D_FF, D_MODEL))],
        out_specs=[_row_spec(tm, D_MODEL), _row_spec(tm, D_MODEL), _full_spec((1, D_MODEL)), _full_spec((1, D_MODEL))],
        out_shape=[jax.ShapeDtypeStruct((L, D_MODEL), _F32), jax.ShapeDtypeStruct((L, D_MODEL), _BF),
                   jax.ShapeDtypeStruct((1, D_MODEL), _F32), jax.ShapeDtypeStruct((1, D_MODEL), _F32)],
        compiler_params=_params("arbitrary"),
    )(f1, x2, tgt, g4, w2)


def _ff2_bwd(dm, f1, w2, tm, tn):
    L = dm.shape[0]

    def body(dm_ref, f_ref, w_ref, df_ref, dw_ref):
        @pl.when(pl.program_id(1) == 0)
        def _():
            dw_ref[...] = jnp.zeros_like(dw_ref)

        dmv = dm_ref[...]
        rl = jnp.maximum(f_ref[...], 0.0)
        df_ref[...] = (_dot_nt(dmv, w_ref[...]) * (2.0 * rl)).astype(_BF)
        dw_ref[...] += _dot_tn((rl * rl).astype(_BF), dmv)

    return pl.pallas_call(
        body, name="ff2_bwd", grid=(D_FF // tn, L // tm),
        in_specs=[pl.BlockSpec((tm, D_MODEL), lambda j, i: (i, 0)), pl.BlockSpec((tm, tn), lambda j, i: (i, j)),
                  pl.BlockSpec((tn, D_MODEL), lambda j, i: (j, 0))],
        out_specs=[pl.BlockSpec((tm, tn), lambda j, i: (i, j)), pl.BlockSpec((tn, D_MODEL), lambda j, i: (j, 0))],
        out_shape=[jax.ShapeDtypeStruct((L, D_FF), _BF), jax.ShapeDtypeStruct((D_FF, D_MODEL), _F32)],
        compiler_params=_params("parallel", "arbitrary"),
    )(dm, f1, w2)


def _ff1_bwd(df1, w1, x2, mix, dy, g3, g2, tm):
    L = df1.shape[0]

    def body(df_ref, w_ref, x2_ref, mix_ref, dy_ref, g3_ref, g2_ref, dx2_ref, dmix_ref, dg3_ref, dg2_ref):
        @pl.when(pl.program_id(0) == 0)
        def _():
            dg3_ref[...] = jnp.zeros_like(dg3_ref)
            dg2_ref[...] = jnp.zeros_like(dg2_ref)

        dh = _dot_nt(df_ref[...], w_ref[...])
        dz, dgr = _rms_bwd(x2_ref[...], g3_ref[...], dh)
        dg3_ref[...] += jnp.sum(dgr, axis=0, keepdims=True)
        dx2 = dy_ref[...] + dz
        dx2_ref[...] = dx2
        dmx, dgr2 = _rms_bwd(mix_ref[...], g2_ref[...], dx2)
        dg2_ref[...] += jnp.sum(dgr2, axis=0, keepdims=True)
        dmix_ref[...] = dmx.astype(_BF)

    vec = _full_spec((1, D_MODEL))
    return pl.pallas_call(
        body, name="ff1_bwd", grid=(L // tm,),
        in_specs=[_row_spec(tm, D_FF), _full_spec((D_MODEL, D_FF)), _row_spec(tm, D_MODEL), _row_spec(tm, D_MODEL),
                  _row_spec(tm, D_MODEL), vec, vec],
        out_specs=[_row_spec(tm, D_MODEL), _row_spec(tm, D_MODEL), vec, vec],
        out_shape=[jax.ShapeDtypeStruct((L, D_MODEL), _F32), jax.ShapeDtypeStruct((L, D_MODEL), _BF),
                   jax.ShapeDtypeStruct((1, D_MODEL), _F32), jax.ShapeDtypeStruct((1, D_MODEL), _F32)],
        compiler_params=_params("arbitrary"),
    )(df1, w1, x2, mix, dy, g3, g2)


def _matmul_tn(a, b, tm, tn, name):
    L, K = a.shape
    N = b.shape[1]

    def body(a_ref, b_ref, o_ref):
        @pl.when(pl.program_id(1) == 0)
        def _():
            o_ref[...] = jnp.zeros_like(o_ref)

        o_ref[...] += _dot_tn(a_ref[...].astype(_BF), b_ref[...].astype(_BF))

    return pl.pallas_call(
        body, name=name, grid=(N // tn, L // tm),
        in_specs=[pl.BlockSpec((tm, K), lambda j, i: (i, 0)), pl.BlockSpec((tm, tn), lambda j, i: (i, j))],
        out_specs=pl.BlockSpec((K, tn), lambda j, i: (0, j)),
        out_shape=jax.ShapeDtypeStruct((K, N), _F32),
        compiler_params=_params("parallel", "arbitrary"),
    )(a, b)


def _mixout_bwd(dmix, w_out, w_glu, glu, s, o, gate, ggn, tm):
    L = dmix.shape[0]

    def body(dmix_ref, wo_ref, wg_ref, glu_ref, s_ref, o_ref, gate_ref, ggn_ref,
             dglu_ref, ds_ref, dgate_ref, do_ref, dggn_ref):
        @pl.when(pl.program_id(0) == 0)
        def _():
            dggn_ref[...] = jnp.zeros_like(dggn_ref)

        dcat = _dot_nt(dmix_ref[...], wo_ref[...])
        dy_ret, dy_ssm = dcat[:, :RET_W], dcat[:, RET_W:]
        glu = glu_ref[...]
        ga, sg = glu[:, :SSM_W], _sigmoid(glu[:, SSM_W:])
        dga = (dy_ssm * sg).astype(_BF)
        dgb = (dy_ssm * ga * sg * (1.0 - sg)).astype(_BF)
        dglu_ref[:, :SSM_W] = dga
        dglu_ref[:, SSM_W:] = dgb
        dys = _dot_nt(dga, wg_ref[:, :SSM_W]) + _dot_nt(dgb, wg_ref[:, SSM_W:])
        ds_ref[...] = dys * _gelu_grad(s_ref[...])
        gt = gate_ref[...]
        sgt = _sigmoid(gt)
        ggn = ggn_ref[...]
        for hh in range(N_HEAD):
            cols = slice(hh * HEAD_D, (hh + 1) * HEAD_D)
            ov = o_ref[:, cols]
            dlt = ov - jnp.mean(ov, axis=-1, keepdims=True)
            rstd = lax.rsqrt(jnp.mean(dlt * dlt, axis=-1, keepdims=True) + NORM_EPS)
            on = dlt * rstd
            dyr = dy_ret[:, cols] * (gt[:, cols] * sgt[:, cols])
            dgate_ref[:, cols] = dy_ret[:, cols] * (on * ggn[:, cols]) * (sgt[:, cols] * (1.0 + gt[:, cols] * (1.0 - sgt[:, cols])))
            dggn_ref[:, cols] += jnp.sum(dyr * on, axis=0, keepdims=True)
            don = dyr * ggn[:, cols]
            do = rstd * (don - jnp.mean(don, axis=-1, keepdims=True) - on * jnp.mean(don * on, axis=-1, keepdims=True))
            do_ref[:, cols] = do.astype(_BF)

    return pl.pallas_call(
        body, name="mixout_bwd", grid=(L // tm,),
        in_specs=[_row_spec(tm, D_MODEL), _full_spec((D_MODEL, D_MODEL)), _full_spec((SSM_W, 2 * SSM_W)),
                  _row_spec(tm, 2 * SSM_W), _row_spec(tm, SSM_W), _row_spec(tm, RET_W), _row_spec(tm, RET_W),
                  _full_spec((1, RET_W))],
        out_specs=[_row_spec(tm, 2 * SSM_W), _row_spec(tm, SSM_W), _row_spec(tm, RET_W), _row_spec(tm, RET_W),
                   _full_spec((1, RET_W))],
        out_shape=[jax.ShapeDtypeStruct((L, 2 * SSM_W), _BF), jax.ShapeDtypeStruct((L, SSM_W), _F32),
                   jax.ShapeDtypeStruct((L, RET_W), _F32), jax.ShapeDtypeStruct((L, RET_W), _BF),
                   jax.ShapeDtypeStruct((1, RET_W), _F32)],
        compiler_params=_params("arbitrary"),
    )(dmix, w_out, w_glu, glu, s, o, gate, ggn)


def _s5_bwd(u, ds, cin, bmat, cmat, tab_f, tab_r, d_skip, tb):
    L = u.shape[0]
    nt = L // tb
    rcol = pl.BlockSpec((tb, LANES), lambda kb, t: (nt - 1 - t, kb))
    tab_spec = pl.BlockSpec((1, 8, SUBLANES, KB_STATES), lambda kb, t: (kb, 0, 0, 0))
    bspec = pl.BlockSpec((1, LANES, 2 * KB_STATES), lambda kb, t: (kb, 0, 0))
    cspec = pl.BlockSpec((1, 2 * KB_STATES, LANES), lambda kb, t: (kb, 0, 0))
    dcol = pl.BlockSpec((1, LANES), lambda kb, t: (0, kb))
    aspec = pl.BlockSpec((1, SUBLANES, 2 * KB_STATES), lambda kb, t: (kb, 0, 0))

    def body(u_ref, ds_ref, cin_ref, b_ref, c_ref, tf_ref, tr_ref, d_ref,
             du_ref, db_ref, dc_ref, da_ref, dd_ref, x_scr, xp_scr, g_scr, fc_scr, lc_scr):
        @pl.when(pl.program_id(1) == 0)
        def _():
            lc_scr[...] = jnp.zeros_like(lc_scr)
            db_ref[...] = jnp.zeros_like(db_ref)
            dc_ref[...] = jnp.zeros_like(dc_ref)
            da_ref[...] = jnp.zeros_like(da_ref)
            dd_ref[...] = jnp.zeros_like(dd_ref)

        uv, dsv = u_ref[...], ds_ref[...]
        ub, dsb = uv.astype(_BF), dsv.astype(_BF)
        fc_scr[...] = cin_ref[0, 0]
        x_scr[...] = _dot(ub, b_ref[0])
        _scan_rows(x_scr, tf_ref, fc_scr, tb, reverse=False, xprev_ref=xp_scr)
        g_scr[...] = _dot_nt(dsb, c_ref[0])
        _scan_rows(g_scr, tr_ref, lc_scr, tb, reverse=True)
        lam = g_scr[...]
        lamb = lam.astype(_BF)
        du_ref[...] = _dot_nt(lamb, b_ref[0]) + d_ref[...] * dsv
        db_ref[0] += _dot_tn(ub, lamb)
        dc_ref[0] += _dot_tn(x_scr[...].astype(_BF), dsb)
        dd_ref[...] += jnp.sum(dsv * uv, axis=0, keepdims=True)
        lr, li = lam[:, :KB_STATES], lam[:, KB_STATES:]
        xp = xp_scr[...]
        xr, xi = xp[:, :KB_STATES], xp[:, KB_STATES:]
        fold = lambda t: jnp.sum(t.reshape(tb // SUBLANES, SUBLANES, KB_STATES), axis=0)
        da_ref[0, :, :KB_STATES] += fold(lr * xr + li * xi)
        da_ref[0, :, KB_STATES:] += fold(li * xr - lr * xi)

    return pl.pallas_call(
        body, name="s5_bwd", grid=(N_KB, nt),
        in_specs=[rcol, rcol, pl.BlockSpec((1, 1, SUBLANES, 2 * KB_STATES), lambda kb, t: (kb, nt - 1 - t, 0, 0)),
                  bspec, cspec, tab_spec, tab_spec, dcol],
        out_specs=[rcol, bspec, cspec, aspec, dcol],
        out_shape=[jax.ShapeDtypeStruct((L, SSM_W), _F32),
                   jax.ShapeDtypeStruct((N_KB, LANES, 2 * KB_STATES), _F32),
                   jax.ShapeDtypeStruct((N_KB, 2 * KB_STATES, LANES), _F32),
                   jax.ShapeDtypeStruct((N_KB, SUBLANES, 2 * KB_STATES), _F32),
                   jax.ShapeDtypeStruct((1, SSM_W), _F32)],
        scratch_shapes=[pltpu.VMEM((tb, 2 * KB_STATES), _F32)] * 3 + [pltpu.VMEM((SUBLANES, 2 * KB_STATES), _F32)] * 2,
        compiler_params=_params("parallel", "arbitrary"),
    )(u, ds, cin, bmat, cmat, tab_f, tab_r, d_skip)


def _retention_bwd(q, k, v, do, r_prev, consts, cosf, sinf):
    L = q.shape[0]
    nc = L // CHUNK
    blk = pl.BlockSpec((CHUNK, HEAD_D), lambda h, n: (nc - 1 - n, h))
    rope_blk = pl.BlockSpec((CHUNK, HEAD_D), lambda h, n: (nc - 1 - n, 0))

    def body(q_ref, k_ref, v_ref, do_ref, rp_ref, dm_ref, xi_ref, zeta_ref, gc_ref, cos_ref, sin_ref,
             dq_ref, dk_ref, dv_ref, g_scr):
        @pl.when(pl.program_id(1) == 0)
        def _():
            g_scr[...] = jnp.zeros_like(g_scr)

        qv, kv, vv, dov = q_ref[...], k_ref[...], v_ref[...], do_ref[...]
        rb = rp_ref[0, 0].astype(_BF)
        gst = g_scr[...]
        gb = gst.astype(_BF)
        dm, zeta = dm_ref[0], zeta_ref[0]
        sb = (_dot_nt(qv, kv) * dm).astype(_BF)
        dab = (_dot_nt(dov, vv) * dm).astype(_BF)
        dox = (dov.astype(_F32) * xi_ref[0]).astype(_BF)
        vz = (vv.astype(_F32) * zeta).astype(_BF)
        dq = _dot(dab, kv) + _dot_nt(dox, rb)
        dk = _dot_tn(dab, qv) + _dot_nt(vz, gb)
        dv = _dot_tn(sb, dov) + _dot(kv, gb) * zeta
        g_scr[...] = gc_ref[0, 0:1, :] * gst + _dot_tn(qv, dox)
        cs, sn = cos_ref[...], sin_ref[...]
        dq_ref[...] = _rope_t(dq, cs, sn).astype(_BF)
        dk_ref[...] = (_rope_t(dk, cs, sn) * (HEAD_D ** -0.5)).astype(_BF)
        dv_ref[...] = dv.astype(_BF)

    return pl.pallas_call(
        body, name="retention_bwd", grid=(N_HEAD, nc),
        in_specs=[blk, blk, blk, blk, pl.BlockSpec((1, 1, HEAD_D, HEAD_D), lambda h, n: (h, nc - 1 - n, 0, 0))]
        + _head_specs() + [rope_blk, rope_blk],
        out_specs=[blk, blk, blk],
        out_shape=[jax.ShapeDtypeStruct((L, RET_W), _BF)] * 3,
        scratch_shapes=[pltpu.VMEM((HEAD_D, HEAD_D), _F32)],
        compiler_params=_params("parallel", "arbitrary"),
    )(q, k, v, do, r_prev, *consts, cosf, sinf)


def _inproj_bwd(pieces, w_in, x, dx2, g1, tm):
    L = x.shape[0]

    def body(p0, p1, p2, p3, p4, w_ref, x_ref, dx2_ref, g_ref, dx_ref, dg_ref):
        @pl.when(pl.program_id(0) == 0)
        def _():
            dg_ref[...] = jnp.zeros_like(dg_ref)

        dh = None
        for j, p in enumerate((p0, p1, p2, p3, p4)):
            part = _dot_nt(p[...].astype(_BF), w_ref[:, j * RET_W:(j + 1) * RET_W])
            dh = part if dh is None else dh + part
        dz, dgr = _rms_bwd(x_ref[...], g_ref[...], dh)
        dx_ref[...] = dx2_ref[...] + dz
        dg_ref[...] += jnp.sum(dgr, axis=0, keepdims=True)

    return pl.pallas_call(
        body, name="inproj_bwd", grid=(L // tm,),
        in_specs=[_row_spec(tm, RET_W)] * 5 + [_full_spec((D_MODEL, IN_COLS)), _row_spec(tm, D_MODEL),
                                                 _row_spec(tm, D_MODEL), _full_spec((1, D_MODEL))],
        out_specs=[_row_spec(tm, D_MODEL), _full_spec((1, D_MODEL))],
        out_shape=[jax.ShapeDtypeStruct((L, D_MODEL), _F32), jax.ShapeDtypeStruct((1, D_MODEL), _F32)],
        compiler_params=_params("arbitrary"),
    )(*pieces, w_in, x, dx2, g1)


def _adamw(w, g, m, v, tr, name):
    R, Cc = w.shape

    def body(w_ref, g_ref, m_ref, v_ref, d_ref, nm_ref, nv_ref):
        gv = g_ref[...]
        nm = ADAM_B1 * m_ref[...] + (1.0 - ADAM_B1) * gv
        nv = ADAM_B2 * v_ref[...] + (1.0 - ADAM_B2) * (gv * gv)
        m_hat = nm / (1.0 - ADAM_B1 ** ADAM_STEP)
        v_hat = nv / (1.0 - ADAM_B2 ** ADAM_STEP)
        d_ref[...] = -ADAM_LR * (m_hat / (jnp.sqrt(v_hat) + ADAM_EPS) + ADAM_WD * w_ref[...])
        nm_ref[...] = nm
        nv_ref[...] = nv

    spec = _row_spec(tr, Cc)
    return pl.pallas_call(
        body, name=name, grid=(R // tr,),
        in_specs=[spec] * 4, out_specs=[spec] * 3,
        out_shape=[jax.ShapeDtypeStruct((R, Cc), _F32)] * 3,
        compiler_params=_params("parallel"),
    )(w, g, m, v)


def _my_place():
    return lax.axis_index("x"), lax.axis_index("y"), lax.axis_index("c")


def _all_gather(block):
    R, Cc = block.shape

    def body(x_ref, out_ref, send_sems, recv_sems, local_sem):
        x, y, c = _my_place()
        me, sibling = (x, y, c), (x, y, 1 - c)
        chips = [(1 - x, y), (x, 1 - y), (1 - x, 1 - y)]

        def slot(px, py, pc):
            return out_ref.at[4 * px + 2 * py + pc]

        def copy(k, blk, to, src=None):
            return pltpu.make_async_remote_copy(
                src_ref=slot(*blk) if src is None else src, dst_ref=slot(*blk),
                send_sem=send_sems.at[k], recv_sem=recv_sems.at[k], device_id=to, device_id_type=MESH)

        mine = pltpu.make_async_copy(x_ref, slot(*me), local_sem)
        mine.start()
        first = [copy(0, me, sibling, src=x_ref)]
        first += [copy(1 + j, me, (*chip, c), src=x_ref) for j, chip in enumerate(chips)]
        for cp in first:
            cp.start()
        passed = [copy(4 + j, (*chip, c), sibling) for j, chip in enumerate(chips)]
        for j, chip in enumerate(chips):
            copy(1 + j, (*chip, c), me).wait_recv()
            passed[j].start()
        copy(0, sibling, me).wait_recv()
        for j, chip in enumerate(chips):
            copy(4 + j, (*chip, 1 - c), me).wait_recv()
        for cp in first + passed:
            cp.wait_send()
        mine.wait()

    return pl.pallas_call(
        body, name="weights_all_gather",
        in_specs=[pl.BlockSpec(memory_space=pl.ANY)], out_specs=pl.BlockSpec(memory_space=pl.ANY),
        out_shape=jax.ShapeDtypeStruct((N_DEV, R, Cc), block.dtype),
        scratch_shapes=[pltpu.SemaphoreType.DMA((7,)), pltpu.SemaphoreType.DMA((7,)), pltpu.SemaphoreType.DMA],
    )(block)


def _exchange(big, small):
    _, R, Cc = big.shape
    r = small.shape[0]

    def body(big_ref, small_ref, bout_ref, sout_ref, send_sems, recv_sems, local_sems):
        x, y, c = _my_place()
        me = 4 * x + 2 * y + c
        own_b = pltpu.make_async_copy(big_ref.at[me], bout_ref.at[me], local_sems.at[0])
        own_s = pltpu.make_async_copy(small_ref, sout_ref.at[me], local_sems.at[1])
        own_b.start()
        own_s.start()
        copies = []
        for kk in range(1, N_DEV):
            px, py, pc = x ^ (kk >> 2), y ^ ((kk >> 1) & 1), c ^ (kk & 1)
            peer = 4 * px + 2 * py + pc
            copies.append(pltpu.make_async_remote_copy(
                src_ref=big_ref.at[peer], dst_ref=bout_ref.at[me],
                send_sem=send_sems.at[kk - 1], recv_sem=recv_sems.at[kk - 1],
                device_id=(px, py, pc), device_id_type=MESH))
            copies.append(pltpu.make_async_remote_copy(
                src_ref=small_ref, dst_ref=sout_ref.at[me],
                send_sem=send_sems.at[7 + kk - 1], recv_sem=recv_sems.at[7 + kk - 1],
                device_id=(px, py, pc), device_id_type=MESH))
        for cp in copies:
            cp.start()
        for cp in copies:
            cp.wait_recv()
        for cp in copies:
            cp.wait_send()
        own_b.wait()
        own_s.wait()

    any_spec = pl.BlockSpec(memory_space=pl.ANY)
    return pl.pallas_call(
        body, name="grad_exchange",
        in_specs=[any_spec, any_spec], out_specs=[any_spec, any_spec],
        out_shape=[jax.ShapeDtypeStruct((N_DEV, R, Cc), big.dtype), jax.ShapeDtypeStruct((N_DEV, r, LANES), small.dtype)],
        scratch_shapes=[pltpu.SemaphoreType.DMA((14,)), pltpu.SemaphoreType.DMA((14,)), pltpu.SemaphoreType.DMA((2,))],
    )(big, small)


def _sum_slots(parts, tr, name):
    _, R, Cc = parts.shape

    def body(p_ref, o_ref):
        acc = p_ref[0].astype(_F32)
        for s in range(1, N_DEV):
            acc = acc + p_ref[s].astype(_F32)
        o_ref[...] = acc

    return pl.pallas_call(
        body, name=name, grid=(R // tr,),
        in_specs=[pl.BlockSpec((N_DEV, tr, Cc), lambda i: (0, i, 0))], out_specs=_row_spec(tr, Cc),
        out_shape=jax.ShapeDtypeStruct((R, Cc), _F32),
        compiler_params=_params("parallel"),
    )(parts)


def _discretize(lam_re, lam_im, log_dt, b_re, b_im):
    lr = jnp.minimum(lam_re, -1e-4)
    li = lam_im
    dt = jnp.exp(log_dt)[:, None]
    er = jnp.exp(lr * dt)
    ar, ai = er * jnp.cos(li * dt), er * jnp.sin(li * dt)
    den = lr * lr + li * li
    cr = ((ar - 1.0) * lr + ai * li) / den
    ci = (ai * lr - (ar - 1.0) * li) / den
    bbr = cr[:, :, None] * b_re - ci[:, :, None] * b_im
    bbi = cr[:, :, None] * b_im + ci[:, :, None] * b_re
    return ar, ai, bbr, bbi


def _cmul(ar, ai, br, bi):
    return ar * br - ai * bi, ar * bi + ai * br


def _scan_tables(ar, ai, reverse):
    if reverse:
        ai = -ai
    pows = [(ar, ai)]
    for _ in range(7):
        pows.append(_cmul(*pows[-1], ar, ai))
    flat = lambda t: t.reshape(N_KB, 1, KB_STATES)
    row = jnp.arange(SUBLANES)[None, :, None]
    tabs = []
    for dist in (1, 2, 4):
        keep = (row < SUBLANES - dist) if reverse else (row >= dist)
        for part in pows[dist - 1]:
            tabs.append(jnp.where(keep, flat(part), 0.0))
    for comp in (0, 1):
        stack = jnp.stack([pows[j][comp].reshape(N_KB, KB_STATES) for j in range(SUBLANES)], axis=1)
        tabs.append(stack[:, ::-1, :] if reverse else stack)
    return jnp.stack(tabs, axis=1).astype(_F32)


def _block_diag_in(br, bi):
    eye = jnp.eye(GROUPS_PER_KB, dtype=_F32)
    one = lambda t: jnp.einsum("kgpc,gh->kgchp", t.reshape(N_KB, GROUPS_PER_KB, N_STATE, SSM_GC), eye).reshape(
        N_KB, LANES, KB_STATES)
    return jnp.concatenate([one(br), one(bi)], axis=-1)


def _block_diag_in_t(dmat):
    d6 = dmat.reshape(N_KB, GROUPS_PER_KB, SSM_GC, 2, GROUPS_PER_KB, N_STATE)
    eye = jnp.eye(GROUPS_PER_KB, dtype=_F32)
    both = jnp.einsum("kgcrhp,gh->rkgpc", d6, eye).reshape(2, N_GROUP, N_STATE, SSM_GC)
    return both[0], both[1]


def _block_diag_out(c_re, c_im):
    eye = jnp.eye(GROUPS_PER_KB, dtype=_F32)
    one = lambda t: jnp.einsum("kgcp,gh->khpgc", t.reshape(N_KB, GROUPS_PER_KB, SSM_GC, N_STATE), eye).reshape(
        N_KB, KB_STATES, LANES)
    return jnp.concatenate([one(c_re), -one(c_im)], axis=1)


def _block_diag_out_t(dmat):
    d6 = dmat.reshape(N_KB, 2, GROUPS_PER_KB, N_STATE, GROUPS_PER_KB, SSM_GC)
    eye = jnp.eye(GROUPS_PER_KB, dtype=_F32)
    both = jnp.einsum("krhpgc,gh->rkgcp", d6, eye).reshape(2, N_GROUP, SSM_GC, N_STATE)
    return both[0], -both[1]


SMALL_NAMES = ("norm_mix_pre", "norm_mix_post", "ret_gn_gain", "ssm_lambda_re", "ssm_lambda_im", "ssm_log_dt",
               "ssm_b_re", "ssm_b_im", "ssm_c_re", "ssm_c_im", "ssm_d", "norm_mlp_pre", "norm_mlp_post")


def _local_grads(x, tgt, small, w_in, w_glu, w_out, w_ff1, w_ff2, tm, tb):
    L = x.shape[0]
    g1, g2, ggn = small["norm_mix_pre"], small["norm_mix_post"], small["ret_gn_gain"]
    g3, g4, d_skip = small["norm_mlp_pre"], small["norm_mlp_post"], small["ssm_d"]

    half = HEAD_D // 2
    inv_freq = ROPE_BASE ** (-jnp.arange(half, dtype=_F32) / half)
    ang = jnp.arange(L, dtype=_F32)[:, None] * inv_freq[None, :]
    cosf = jnp.concatenate([jnp.cos(ang), jnp.cos(ang)], axis=-1)
    sinf = jnp.concatenate([-jnp.sin(ang), jnp.sin(ang)], axis=-1)
    consts = _ret_consts()

    disc_in = (small["ssm_lambda_re"][0], small["ssm_lambda_im"][0], small["ssm_log_dt"][0],
               small["ssm_b_re"][0], small["ssm_b_im"][0])
    (ar, ai, bbr, bbi), disc_vjp = jax.vjp(_discretize, *disc_in)
    bmat = _block_diag_in(bbr, bbi).astype(_BF)
    cmat = _block_diag_out(small["ssm_c_re"][0], small["ssm_c_im"][0]).astype(_BF)
    tab_f, tab_r = _scan_tables(ar, ai, False), _scan_tables(ar, ai, True)

    h1, q, k, v, gate, u = _inproj_fwd(x, g1, w_in, cosf, sinf, tm)
    o, y_ret, r_prev = _retention_fwd(q, k, v, gate, ggn, consts)
    s, cin = _s5_fwd(u, bmat, cmat, tab_f, d_skip, tb)
    ys, glu, y_ssm, mix, x2 = _mixout_fwd(s, y_ret, x, w_glu, w_out, g2, tm)
    h3, f1 = _ff1_fwd(x2, g3, w_ff1, tm)
    dy, dm, dg4, sq = _ff2_loss(f1, x2, tgt, g4, w_ff2, tm)

    df1, dw_ff2 = _ff2_bwd(dm, f1, w_ff2, tm, 1024)
    dx2, dmix, dg3, dg2 = _ff1_bwd(df1, w_ff1, x2, mix, dy, g3, g2, tm)
    dw_ff1 = _matmul_tn(h3, df1, tm, 1024, "dw_ff1")
    dglu, ds, dgate, do, dggn = _mixout_bwd(dmix, w_out, w_glu, glu, s, o, gate, ggn, tm)
    dw_out = jnp.concatenate([_matmul_tn(y_ret, dmix, tm, 1024, "dw_out_ret"),
                              _matmul_tn(y_ssm, dmix, tm, 1024, "dw_out_ssm")], axis=0)
    dw_glu = _matmul_tn(ys, dglu, tm, 1024, "dw_glu")
    du, dbmat, dcmat, da8, dd = _s5_bwd(u, ds, cin, bmat, cmat, tab_f, tab_r, d_skip, tb)
    dq, dk, dv = _retention_bwd(q, k, v, do, r_prev, consts, cosf, sinf)
    pieces = (dq, dk, dv, dgate, du)
    gx, dg1 = _inproj_bwd(pieces, w_in, x, dx2, g1, tm)
    dw_in = jnp.concatenate([_matmul_tn(h1, p, tm, RET_W, "dw_in_%d" % j) for j, p in enumerate(pieces)], axis=1)

    da = jnp.sum(da8, axis=1)
    dar = da[:, :KB_STATES].reshape(N_GROUP, N_STATE)
    dai = da[:, KB_STATES:].reshape(N_GROUP, N_STATE)
    dbr, dbi = _block_diag_in_t(dbmat)
    dlre, dlim, dldt, dbre, dbim = disc_vjp((dar, dai, dbr, dbi))
    dcre, dcim = _block_diag_out_t(dcmat)

    gsmall = {
        "norm_mix_pre": dg1, "norm_mix_post": dg2, "ret_gn_gain": dggn,
        "ssm_lambda_re": dlre[None], "ssm_lambda_im": dlim[None], "ssm_log_dt": dldt[None],
        "ssm_b_re": dbre[None], "ssm_b_im": dbim[None], "ssm_c_re": dcre[None], "ssm_c_im": dcim[None],
        "ssm_d": dd, "norm_mlp_pre": dg3, "norm_mlp_post": dg4,
    }
    return sq, gx, gsmall, (dw_in, dw_glu, dw_out, dw_ff1, dw_ff2)


PACK_COLS = 1024
BIG_SHAPES = {"w_in": (D_MODEL, IN_COLS // N_DEV), "w_glu": (SSM_W, 2 * SSM_W // N_DEV), "w_out": (D_MODEL // N_DEV, D_MODEL),
              "w_ff1": (D_MODEL, D_FF // N_DEV), "w_ff2": (D_FF // N_DEV, D_MODEL)}
BIG_NAMES = ("w_in", "w_glu", "w_out", "w_ff1", "w_ff2")
COL_SHARDED = ("w_in", "w_glu", "w_ff1")


def _pack_rows(name):
    r, c = BIG_SHAPES[name]
    return r * c // PACK_COLS


def _unpack_gathered(g, name, row0):
    r, c = BIG_SHAPES[name]
    blocks = g[:, row0:row0 + _pack_rows(name), :].reshape(N_DEV, r, c)
    if name in COL_SHARDED:
        return jnp.transpose(blocks, (1, 0, 2)).reshape(r, N_DEV * c)
    return blocks.reshape(N_DEV * r, c)


def _pack_grad(dw, name):
    r, c = BIG_SHAPES[name]
    if name in COL_SHARDED:
        blocks = jnp.transpose(dw.reshape(r, N_DEV, c), (1, 0, 2))
    else:
        blocks = dw.reshape(N_DEV, r, c)
    return blocks.reshape(N_DEV, _pack_rows(name), PACK_COLS)


def _small_layout(shapes):
    off, rows = {}, 0
    for n in SMALL_NAMES:
        size = math.prod(shapes[n])
        off[n] = rows
        rows += -(-size // LANES)
    loss_row = rows
    rows += 1
    total = -(-rows // SUBLANES) * SUBLANES
    return off, loss_row, total


def _pack_small(vals, shapes, loss_row_val=None):
    off, loss_row, total = _small_layout(shapes)
    parts = []
    for n in SMALL_NAMES:
        flat = vals[n].reshape(-1).astype(_F32)
        pad = -flat.shape[0] % LANES
        if pad:
            flat = jnp.concatenate([flat, jnp.zeros((pad,), _F32)])
        parts.append(flat.reshape(-1, LANES))
    parts.append(jnp.zeros((1, LANES), _F32) if loss_row_val is None else loss_row_val)
    used = loss_row + 1
    if total > used:
        parts.append(jnp.zeros((total - used, LANES), _F32))
    return jnp.concatenate(parts, axis=0)


def _unpack_small(buf, shapes):
    off, _, _ = _small_layout(shapes)
    out = {}
    for n in SMALL_NAMES:
        size = math.prod(shapes[n])
        rows = -(-size // LANES)
        out[n] = buf[off[n]:off[n] + rows].reshape(-1)[:size].reshape(shapes[n])
    return out


WEIGHT_NAMES = ('norm_mix_pre', 'norm_mix_post', 'w_in', 'ret_gn_gain', 'ssm_lambda_re', 'ssm_lambda_im', 'ssm_log_dt',
                'ssm_b_re', 'ssm_b_im', 'ssm_c_re', 'ssm_c_im', 'ssm_d', 'w_glu', 'w_out', 'norm_mlp_pre',
                'norm_mlp_post', 'w_ff1', 'w_ff2')


def kernel(x, norm_mix_pre, norm_mix_post, w_in, ret_gn_gain, ssm_lambda_re, ssm_lambda_im, ssm_log_dt, ssm_b_re, ssm_b_im, ssm_c_re, ssm_c_im, ssm_d, w_glu, w_out, norm_mlp_pre, norm_mlp_post, w_ff1, w_ff2, loss_target, m_norm_mix_pre, m_norm_mix_post, m_w_in, m_ret_gn_gain, m_ssm_lambda_re, m_ssm_lambda_im, m_ssm_log_dt, m_ssm_b_re, m_ssm_b_im, m_ssm_c_re, m_ssm_c_im, m_ssm_d, m_w_glu, m_w_out, m_norm_mlp_pre, m_norm_mlp_post, m_w_ff1, m_w_ff2, v_norm_mix_pre, v_norm_mix_post, v_w_in, v_ret_gn_gain, v_ssm_lambda_re, v_ssm_lambda_im, v_ssm_log_dt, v_ssm_b_re, v_ssm_b_im, v_ssm_c_re, v_ssm_c_im, v_ssm_d, v_w_glu, v_w_out, v_norm_mlp_pre, v_norm_mlp_post, v_w_ff1, v_w_ff2):
    args = dict(locals())
    w = {n: args[n] for n in WEIGHT_NAMES}
    m = {n: args["m_" + n] for n in WEIGHT_NAMES}
    v = {n: args["v_" + n] for n in WEIGHT_NAMES}
    L = x.shape[1]
    tm = min(256, L)
    tb = min(256, L)

    packed = jnp.concatenate([w[n][0].astype(_BF).reshape(_pack_rows(n), PACK_COLS) for n in BIG_NAMES], axis=0)
    gathered = _all_gather(packed)
    full, row0 = {}, 0
    for n in BIG_NAMES:
        full[n] = _unpack_gathered(gathered, n, row0)
        row0 += _pack_rows(n)

    small_w = {n: w[n] for n in SMALL_NAMES}
    sq, gx, gsmall, gbig = _local_grads(x[0], loss_target[0], small_w, full["w_in"], full["w_glu"], full["w_out"],
                                        full["w_ff1"], full["w_ff2"], tm, tb)

    shapes = {n: w[n].shape for n in SMALL_NAMES}
    big_buf = jnp.concatenate([_pack_grad(dw, n) for dw, n in zip(gbig, BIG_NAMES)], axis=1).astype(_BF)
    loss_row = jnp.concatenate([0.5 / D_MODEL * jnp.sum(sq).reshape(1, 1), jnp.zeros((1, LANES - 1), _F32)], axis=1)
    small_buf = _pack_small(gsmall, shapes, loss_row)
    big_parts, small_parts = _exchange(big_buf, small_buf)
    big_sum = _sum_slots(big_parts, 256, "big_grad_sum")
    small_sum = _sum_slots(small_parts, small_buf.shape[0], "small_grad_sum")
    _, loss_at, _ = _small_layout(shapes)
    loss = small_sum[loss_at, 0]

    grads, delta, new_m, new_v = {}, {}, {}, {}
    row0 = 0
    for n in BIG_NAMES:
        r, c = BIG_SHAPES[n]
        g = big_sum[row0:row0 + _pack_rows(n)].reshape(r, c)
        row0 += _pack_rows(n)
        d_, m_, v_ = _adamw(w[n][0], g, m[n][0], v[n][0], min(256, r), "adamw_" + n)
        grads[n], delta[n], new_m[n], new_v[n] = g[None], d_[None], m_[None], v_[None]
    sw, sm, sv = _pack_small(w, shapes), _pack_small(m, shapes), _pack_small(v, shapes)
    d_, m_, v_ = _adamw(sw, small_sum, sm, sv, sw.shape[0], "adamw_small")
    for dst, buf in ((grads, small_sum), (delta, d_), (new_m, m_), (new_v, v_)):
        dst.update(_unpack_small(buf, shapes))

    return (loss, gx[None], *[grads[n] for n in WEIGHT_NAMES], *[delta[n] for n in WEIGHT_NAMES],
            *[new_m[n] for n in WEIGHT_NAMES], *[new_v[n] for n in WEIGHT_NAMES])
```

```python
import math

import jax
import jax.numpy as jnp
from jax import lax
from jax.experimental import pallas as pl
from jax.experimental.pallas import tpu as pltpu

_BF = jnp.bfloat16
_F32 = jnp.float32

D_MODEL = 1024
RET_W = 512
N_HEAD = 4
HEAD_D = 128
CHUNK = 128
SSM_W = 512
SSM_GC = 16
N_GROUP = 32
N_STATE = 64
GROUPS_PER_KB = 8
N_KB = 4
KB_STATES = GROUPS_PER_KB * N_STATE
D_FF = 4096
IN_COLS = 2560
NORM_EPS = 1e-6
ROPE_BASE = 10000.0
N_DEV = 8

ADAM_LR = 0.001
ADAM_B1 = 0.9
ADAM_B2 = 0.999
ADAM_EPS = 1e-08
ADAM_WD = 0.01
ADAM_STEP = 10

SUBLANES = 8
LANES = 128
VMEM_LIMIT = 52 * 1024 * 1024
SCAN_STRIP = 512

MESH = pl.DeviceIdType.MESH


def _params(*sem):
    return pltpu.CompilerParams(dimension_semantics=sem, vmem_limit_bytes=VMEM_LIMIT)


def _dot(a, b):
    return jnp.dot(a, b, preferred_element_type=_F32)


def _dot_nt(a, b):
    return lax.dot_general(a, b, (((1,), (1,)), ((), ())), preferred_element_type=_F32)


def _dot_tn(a, b):
    return lax.dot_general(a, b, (((0,), (0,)), ((), ())), preferred_element_type=_F32)


def _rms_r(z):
    return lax.rsqrt(jnp.mean(z * z, axis=-1, keepdims=True) + NORM_EPS)


def _rms_bwd(z, g, dn):
    r = _rms_r(z)
    t = dn * g
    dz = r * t - z * (r * r * r * jnp.mean(t * z, axis=-1, keepdims=True))
    return dz, dn * z * r


def _rope(t, cs, sn):
    return t * cs + pltpu.roll(t, HEAD_D // 2, 1) * sn


def _rope_t(t, cs, sn):
    return t * cs - pltpu.roll(t, HEAD_D // 2, 1) * sn


def _sigmoid(z):
    return 1.0 / (1.0 + jnp.exp(-z))


_GELU_C = math.sqrt(2.0 / math.pi)


def _gelu(z):
    return 0.5 * z * (1.0 + jnp.tanh(_GELU_C * (z + 0.044715 * z * z * z)))


def _gelu_grad(z):
    th = jnp.tanh(_GELU_C * (z + 0.044715 * z * z * z))
    return 0.5 * (1.0 + th) + 0.5 * z * (1.0 - th * th) * _GELU_C * (1.0 + 3 * 0.044715 * z * z)


def _row_spec(tm, n):
    return pl.BlockSpec((tm, n), lambda i: (i, 0))


def _full_spec(shape):
    nd = len(shape)
    return pl.BlockSpec(shape, lambda *_: (0,) * nd)


def _inproj_fwd(x, g1, w_in, cosf, sinf, tm):
    L = x.shape[0]

    def body(x_ref, g_ref, w_ref, cos_ref, sin_ref, h_ref, q_ref, k_ref, v_ref, gate_ref, u_ref):
        xv = x_ref[...]
        h = (xv * _rms_r(xv) * g_ref[...]).astype(_BF)
        h_ref[...] = h
        proj = _dot(h, w_ref[...])
        cs, sn = cos_ref[...], sin_ref[...]
        for hh in range(N_HEAD):
            lo = hh * HEAD_D
            q_ref[:, lo:lo + HEAD_D] = _rope(proj[:, lo:lo + HEAD_D], cs, sn).astype(_BF)
            kh = _rope(proj[:, RET_W + lo:RET_W + lo + HEAD_D], cs, sn) * (HEAD_D ** -0.5)
            k_ref[:, lo:lo + HEAD_D] = kh.astype(_BF)
        v_ref[...] = proj[:, 2 * RET_W:3 * RET_W].astype(_BF)
        gate_ref[...] = proj[:, 3 * RET_W:4 * RET_W]
        u_ref[...] = proj[:, 4 * RET_W:]

    return pl.pallas_call(
        body, name="inproj_fwd", grid=(L // tm,),
        in_specs=[_row_spec(tm, D_MODEL), _full_spec((1, D_MODEL)), _full_spec((D_MODEL, IN_COLS)),
                  _row_spec(tm, HEAD_D), _row_spec(tm, HEAD_D)],
        out_specs=[_row_spec(tm, D_MODEL)] + [_row_spec(tm, RET_W)] * 5,
        out_shape=[jax.ShapeDtypeStruct((L, D_MODEL), _BF)] + [jax.ShapeDtypeStruct((L, RET_W), _BF)] * 3
        + [jax.ShapeDtypeStruct((L, RET_W), _F32)] * 2,
        compiler_params=_params("parallel"),
    )(x, g1, w_in, cosf, sinf)


def _ret_consts():
    lg = jnp.log(1.0 - jnp.exp(jnp.linspace(math.log(1.0 / 32), math.log(1.0 / 512), N_HEAD))).astype(_F32)
    idx = jnp.arange(CHUNK, dtype=_F32)
    diff = idx[:, None] - idx[None, :]
    decay = jnp.where(diff[None] >= 0, jnp.exp(jnp.maximum(diff, 0.0)[None] * lg[:, None, None]), 0.0)
    zeta = jnp.exp((CHUNK - 1 - idx)[None, :] * lg[:, None])
    xi = jnp.exp((idx + 1.0)[None, :] * lg[:, None])
    gc = jnp.exp(CHUNK * lg)
    wide = lambda t: jnp.broadcast_to(t[:, :, None], (N_HEAD, CHUNK, HEAD_D)).astype(_F32)
    gcw = jnp.broadcast_to(gc[:, None, None], (N_HEAD, SUBLANES, HEAD_D)).astype(_F32)
    return decay.astype(_F32), wide(xi), wide(zeta), gcw


def _head_specs():
    c3 = _full_spec((N_HEAD, CHUNK, CHUNK))
    return [c3, c3, c3, _full_spec((N_HEAD, SUBLANES, HEAD_D))]


def _retention_fwd(q, k, v, gate, ggn, consts):
    L = q.shape[0]
    nc = L // CHUNK
    blk = pl.BlockSpec((CHUNK, RET_W), lambda n: (n, 0))

    def body(q_ref, k_ref, v_ref, gate_ref, ggn_ref, dm_ref, xi_ref, zeta_ref, gc_ref,
             o_ref, y_ref, rp_ref, r_scr):
        @pl.when(pl.program_id(0) == 0)
        def _():
            r_scr[...] = jnp.zeros_like(r_scr)

        for hh in range(N_HEAD):
            cols = slice(hh * HEAD_D, (hh + 1) * HEAD_D)
            qv, kv, vv = q_ref[:, cols], k_ref[:, cols], v_ref[:, cols]
            r_prev = r_scr[hh]
            s = _dot_nt(qv, kv) * dm_ref[hh]
            o = _dot(s.astype(_BF), vv) + _dot(qv, r_prev.astype(_BF)) * xi_ref[hh]
            o_ref[:, cols] = o
            rp_ref[hh, 0] = r_prev
            vz = (vv.astype(_F32) * zeta_ref[hh]).astype(_BF)
            r_scr[hh] = gc_ref[hh, 0:1, :] * r_prev + _dot_tn(kv, vz)
            dlt = o - jnp.mean(o, axis=-1, keepdims=True)
            on = dlt * lax.rsqrt(jnp.mean(dlt * dlt, axis=-1, keepdims=True) + NORM_EPS)
            gt = gate_ref[:, cols]
            y_ref[:, cols] = (gt * _sigmoid(gt) * (on * ggn_ref[:, cols])).astype(_BF)

    return pl.pallas_call(
        body, name="retention_fwd", grid=(nc,),
        in_specs=[blk, blk, blk, blk, _full_spec((1, RET_W))] + _head_specs(),
        out_specs=[blk, blk, pl.BlockSpec((N_HEAD, 1, HEAD_D, HEAD_D), lambda n: (0, n, 0, 0))],
        out_shape=[jax.ShapeDtypeStruct((L, RET_W), _F32), jax.ShapeDtypeStruct((L, RET_W), _BF),
                   jax.ShapeDtypeStruct((N_HEAD, nc, HEAD_D, HEAD_D), _F32)],
        scratch_shapes=[pltpu.VMEM((N_HEAD, HEAD_D, HEAD_D), _F32)],
        compiler_params=_params("arbitrary"),
    )(q, k, v, gate, ggn, *consts)


def _rows_to_segments(dst_ref, src_ref, seg):
    for r in range(seg):
        dst_ref[pl.ds(r * SUBLANES, SUBLANES), :] = src_ref[pl.ds(r, SUBLANES, stride=seg), :]


def _segments_to_rows(dst_ref, val, seg):
    for r in range(seg):
        dst_ref[pl.ds(r, SUBLANES, stride=seg), :] = val[r * SUBLANES:(r + 1) * SUBLANES, :]


def _scan_segments(x_ref, tab_ref, pw_ref, carry_ref, seg, reverse, xprev_ref=None, da_ref=None):
    W = SCAN_STRIP
    row_id = lax.broadcasted_iota(jnp.int32, (SUBLANES, W), 0)
    edge_in = (row_id == SUBLANES - 1) if reverse else (row_id == 0)
    edge_out = 0 if reverse else SUBLANES - 1
    for strip in range(KB_STATES // W):
        re = pl.ds(strip * W, W)
        im = pl.ds(KB_STATES + strip * W, W)
        ar, ai = tab_ref[0, 0, :, re], tab_ref[0, 1, :, re]

        def local(i, st, re=re, im=im, ar=ar, ai=ai):
            sr, si = st
            r = (seg - 1 - i) if reverse else i
            nr = ar * sr - ai * si + x_ref[r, :, re]
            ni = ar * si + ai * sr + x_ref[r, :, im]
            x_ref[r, :, re] = nr
            x_ref[r, :, im] = ni
            return nr, ni

        zero = jnp.zeros((SUBLANES, W), _F32)
        er, ei = lax.fori_loop(0, seg, local, (zero, zero))

        shift = (SUBLANES - 1) if reverse else 1
        fr = jnp.where(edge_in, carry_ref[:, re], pltpu.roll(er, shift, 0))
        fi = jnp.where(edge_in, carry_ref[:, im], pltpu.roll(ei, shift, 0))
        for j, dist in enumerate((1, 2, 4)):
            pr, pi = tab_ref[0, 2 + 2 * j, :, re], tab_ref[0, 3 + 2 * j, :, re]
            sh = (SUBLANES - dist) if reverse else dist
            sr, si = pltpu.roll(fr, sh, 0), pltpu.roll(fi, sh, 0)
            fr, fi = fr + pr * sr - pi * si, fi + pr * si + pi * sr
        br, bi = tab_ref[0, 8, :, re], tab_ref[0, 9, :, re]
        outr = br * fr - bi * fi + er
        outi = br * fi + bi * fr + ei
        carry_ref[:, re] = jnp.broadcast_to(outr[edge_out:edge_out + 1, :], (SUBLANES, W))
        carry_ref[:, im] = jnp.broadcast_to(outi[edge_out:edge_out + 1, :], (SUBLANES, W))

        keep_prev = xprev_ref is not None and not reverse
        add_da = da_ref is not None

        def fix(r, st, re=re, im=im, fr=fr, fi=fi):
            pwr, pwi = pw_ref[0, r, :, re], pw_ref[0, r, :, im]
            xr = x_ref[r, :, re] + (pwr * fr - pwi * fi)
            xi = x_ref[r, :, im] + (pwr * fi + pwi * fr)
            x_ref[r, :, re] = xr
            x_ref[r, :, im] = xi
            if keep_prev:
                xprev_ref[r, :, re] = st[0]
                xprev_ref[r, :, im] = st[1]
                return xr, xi
            if add_da:
                xpr, xpi = xprev_ref[r, :, re], xprev_ref[r, :, im]
                return st[0] + (xr * xpr + xi * xpi), st[1] + (xi * xpr - xr * xpi)
            return st

        st = lax.fori_loop(0, seg, fix, (fr, fi) if keep_prev else (zero, zero))
        if add_da:
            da_ref[0, :, re] += st[0]
            da_ref[0, :, im] += st[1]


def _s5_specs(seg):
    return dict(
        b=pl.BlockSpec((1, LANES, 2 * KB_STATES), lambda kb, t: (kb, 0, 0)),
        c=pl.BlockSpec((1, 2 * KB_STATES, LANES), lambda kb, t: (kb, 0, 0)),
        tab=pl.BlockSpec((1, 10, SUBLANES, KB_STATES), lambda kb, t: (kb, 0, 0, 0)),
        pw=pl.BlockSpec((1, seg, 1, 2 * KB_STATES), lambda kb, t: (kb, 0, 0, 0)),
        d=pl.BlockSpec((1, LANES), lambda kb, t: (0, kb)),
    )


def _s5_fwd(u, bmat, cmat, tab_f, pw_f, d_skip, tb):
    L = u.shape[0]
    nt = L // tb
    seg = tb // SUBLANES
    ucol = pl.BlockSpec((tb, LANES), lambda kb, t: (t, kb))
    sp = _s5_specs(seg)

    def body(u_ref, b_ref, c_ref, tab_ref, pw_ref, d_ref, s_ref, cin_ref, up_scr, x_scr, carry_scr):
        @pl.when(pl.program_id(1) == 0)
        def _():
            carry_scr[...] = jnp.zeros_like(carry_scr)

        cin_ref[0, 0] = carry_scr[...]
        _rows_to_segments(up_scr, u_ref, seg)
        up = up_scr[...]
        x_scr[...] = _dot(up.astype(_BF), b_ref[0]).reshape(seg, SUBLANES, 2 * KB_STATES)
        _scan_segments(x_scr, tab_ref, pw_ref, carry_scr, seg, reverse=False)
        y = _dot(x_scr[...].reshape(tb, 2 * KB_STATES).astype(_BF), c_ref[0]) + d_ref[...] * up
        _segments_to_rows(s_ref, y, seg)

    return pl.pallas_call(
        body, name="s5_fwd", grid=(N_KB, nt),
        in_specs=[ucol, sp["b"], sp["c"], sp["tab"], sp["pw"], sp["d"]],
        out_specs=[ucol, pl.BlockSpec((1, 1, SUBLANES, 2 * KB_STATES), lambda kb, t: (kb, t, 0, 0))],
        out_shape=[jax.ShapeDtypeStruct((L, SSM_W), _F32),
                   jax.ShapeDtypeStruct((N_KB, nt, SUBLANES, 2 * KB_STATES), _F32)],
        scratch_shapes=[pltpu.VMEM((tb, LANES), _F32), pltpu.VMEM((seg, SUBLANES, 2 * KB_STATES), _F32),
                        pltpu.VMEM((SUBLANES, 2 * KB_STATES), _F32)],
        compiler_params=_params("parallel", "arbitrary"),
    )(u, bmat, cmat, tab_f, pw_f, d_skip)


def _mixout_fwd(s, y_ret, x, w_glu, w_out, g2, tm):
    L = s.shape[0]

    def body(s_ref, yr_ref, x_ref, wg_ref, wo_ref, g_ref, ys_ref, glu_ref, cat_ref, mix_ref, x2_ref):
        ys = _gelu(s_ref[...]).astype(_BF)
        ys_ref[...] = ys
        glu = _dot(ys, wg_ref[...])
        glu_ref[...] = glu
        cat_ref[:, :RET_W] = yr_ref[...]
        cat_ref[:, RET_W:] = (glu[:, :SSM_W] * _sigmoid(glu[:, SSM_W:])).astype(_BF)
        mix = _dot(cat_ref[...], wo_ref[...])
        mix_ref[...] = mix
        x2_ref[...] = x_ref[...] + mix * _rms_r(mix) * g_ref[...]

    return pl.pallas_call(
        body, name="mixout_fwd", grid=(L // tm,),
        in_specs=[_row_spec(tm, SSM_W), _row_spec(tm, RET_W), _row_spec(tm, D_MODEL),
                  _full_spec((SSM_W, 2 * SSM_W)), _full_spec((D_MODEL, D_MODEL)), _full_spec((1, D_MODEL))],
        out_specs=[_row_spec(tm, SSM_W), _row_spec(tm, 2 * SSM_W), _row_spec(tm, D_MODEL),
                   _row_spec(tm, D_MODEL), _row_spec(tm, D_MODEL)],
        out_shape=[jax.ShapeDtypeStruct((L, SSM_W), _BF), jax.ShapeDtypeStruct((L, 2 * SSM_W), _F32),
                   jax.ShapeDtypeStruct((L, D_MODEL), _BF), jax.ShapeDtypeStruct((L, D_MODEL), _F32),
                   jax.ShapeDtypeStruct((L, D_MODEL), _F32)],
        compiler_params=_params("parallel"),
    )(s, y_ret, x, w_glu, w_out, g2)


FF1_COLS = D_FF // N_DEV


def _ff1_fwd(x2, g3, w1, tm):
    L = x2.shape[0]

    def body(x_ref, g_ref, w_ref, h_ref, f_ref):
        xv = x_ref[...]
        h = (xv * _rms_r(xv) * g_ref[...]).astype(_BF)
        h_ref[...] = h
        for j in range(N_DEV):
            f_ref[:, j * FF1_COLS:(j + 1) * FF1_COLS] = _dot(h, w_ref[j])

    return pl.pallas_call(
        body, name="ff1_fwd", grid=(L // tm,),
        in_specs=[_row_spec(tm, D_MODEL), _full_spec((1, D_MODEL)), _full_spec((N_DEV, D_MODEL, FF1_COLS))],
        out_specs=[_row_spec(tm, D_MODEL), _row_spec(tm, D_FF)],
        out_shape=[jax.ShapeDtypeStruct((L, D_MODEL), _BF), jax.ShapeDtypeStruct((L, D_FF), _F32)],
        compiler_params=_params("parallel"),
    )(x2, g3, w1)


def _ff2_loss(f1, x2, tgt, g4, w2, tm):
    L = f1.shape[0]

    def body(f_ref, x_ref, t_ref, g_ref, w_ref, dy_ref, dm_ref, dg_ref, ls_ref):
        @pl.when(pl.program_id(0) == 0)
        def _():
            dg_ref[...] = jnp.zeros_like(dg_ref)
            ls_ref[...] = jnp.zeros_like(ls_ref)

        rl = jnp.maximum(f_ref[...], 0.0)
        m = _dot((rl * rl).astype(_BF), w_ref[...])
        g = g_ref[...]
        y = x_ref[...] + m * _rms_r(m) * g
        err = y - t_ref[...]
        ls_ref[...] += jnp.sum(err * err, axis=0, keepdims=True)
        dy = err * (1.0 / D_MODEL)
        dy_ref[...] = dy
        dm, dgr = _rms_bwd(m, g, dy)
        dm_ref[...] = dm.astype(_BF)
        dg_ref[...] += jnp.sum(dgr, axis=0, keepdims=True)

    return pl.pallas_call(
        body, name="ff2_loss", grid=(L // tm,),
        in_specs=[_row_spec(tm, D_FF), _row_spec(tm, D_MODEL), _row_spec(tm, D_MODEL),
                  _full_spec((1, D_MODEL)), _full_spec((D_FF, D_MODEL))],
        out_specs=[_row_spec(tm, D_MODEL), _row_spec(tm, D_MODEL), _full_spec((1, D_MODEL)), _full_spec((1, D_MODEL))],
        out_shape=[jax.ShapeDtypeStruct((L, D_MODEL), _F32), jax.ShapeDtypeStruct((L, D_MODEL), _BF),
                   jax.ShapeDtypeStruct((1, D_MODEL), _F32), jax.ShapeDtypeStruct((1, D_MODEL), _F32)],
        compiler_params=_params("arbitrary"),
    )(f1, x2, tgt, g4, w2)


def _ff2_bwd(dm, f1, w2, tm, tn):
    L = dm.shape[0]
    last = L // tm - 1

    def body(dm_ref, f_ref, w_ref, df_ref, dw_ref, acc):
        @pl.when(pl.program_id(1) == 0)
        def _():
            acc[...] = jnp.zeros_like(acc)

        dmv = dm_ref[...]
        rl = jnp.maximum(f_ref[...], 0.0)
        df_ref[...] = (_dot_nt(dmv, w_ref[...]) * (2.0 * rl)).astype(_BF)
        acc[...] += _dot_tn((rl * rl).astype(_BF), dmv)

        @pl.when(pl.program_id(1) == last)
        def _():
            dw_ref[...] = acc[...].astype(_BF)

    return pl.pallas_call(
        body, name="ff2_bwd", grid=(D_FF // tn, L // tm),
        in_specs=[pl.BlockSpec((tm, D_MODEL), lambda j, i: (i, 0)), pl.BlockSpec((tm, tn), lambda j, i: (i, j)),
                  pl.BlockSpec((tn, D_MODEL), lambda j, i: (j, 0))],
        out_specs=[pl.BlockSpec((tm, tn), lambda j, i: (i, j)), pl.BlockSpec((tn, D_MODEL), lambda j, i: (j, 0))],
        out_shape=[jax.ShapeDtypeStruct((L, D_FF), _BF), jax.ShapeDtypeStruct((D_FF, D_MODEL), _BF)],
        scratch_shapes=[pltpu.VMEM((tn, D_MODEL), _F32)],
        compiler_params=_params("parallel", "arbitrary"),
    )(dm, f1, w2)


def _ff1_bwd(df1, w1, x2, mix, dy, g3, g2, tm):
    L = df1.shape[0]

    def body(df_ref, w_ref, x2_ref, mix_ref, dy_ref, g3_ref, g2_ref, dx2_ref, dmix_ref, dg3_ref, dg2_ref):
        @pl.when(pl.program_id(0) == 0)
        def _():
            dg3_ref[...] = jnp.zeros_like(dg3_ref)
            dg2_ref[...] = jnp.zeros_like(dg2_ref)

        dh = _dot_nt(df_ref[:, 0:FF1_COLS], w_ref[0])
        for j in range(1, N_DEV):
            dh = dh + _dot_nt(df_ref[:, j * FF1_COLS:(j + 1) * FF1_COLS], w_ref[j])
        dz, dgr = _rms_bwd(x2_ref[...], g3_ref[...], dh)
        dg3_ref[...] += jnp.sum(dgr, axis=0, keepdims=True)
        dx2 = dy_ref[...] + dz
        dx2_ref[...] = dx2
        dmx, dgr2 = _rms_bwd(mix_ref[...], g2_ref[...], dx2)
        dg2_ref[...] += jnp.sum(dgr2, axis=0, keepdims=True)
        dmix_ref[...] = dmx.astype(_BF)

    vec = _full_spec((1, D_MODEL))
    return pl.pallas_call(
        body, name="ff1_bwd", grid=(L // tm,),
        in_specs=[_row_spec(tm, D_FF), _full_spec((N_DEV, D_MODEL, FF1_COLS)), _row_spec(tm, D_MODEL),
                  _row_spec(tm, D_MODEL), _row_spec(tm, D_MODEL), vec, vec],
        out_specs=[_row_spec(tm, D_MODEL), _row_spec(tm, D_MODEL), vec, vec],
        out_shape=[jax.ShapeDtypeStruct((L, D_MODEL), _F32), jax.ShapeDtypeStruct((L, D_MODEL), _BF),
                   jax.ShapeDtypeStruct((1, D_MODEL), _F32), jax.ShapeDtypeStruct((1, D_MODEL), _F32)],
        compiler_params=_params("arbitrary"),
    )(df1, w1, x2, mix, dy, g3, g2)


def _matmul_tn(a, b, tm, tn, name, slots=False):
    L, K = a.shape
    N = b.shape[1]
    last = L // tm - 1

    def body(a_ref, b_ref, o_ref, acc):
        @pl.when(pl.program_id(1) == 0)
        def _():
            acc[...] = jnp.zeros_like(acc)

        acc[...] += _dot_tn(a_ref[...].astype(_BF), b_ref[...].astype(_BF))

        @pl.when(pl.program_id(1) == last)
        def _():
            if slots:
                o_ref[0] = acc[...].astype(_BF)
            else:
                o_ref[...] = acc[...].astype(_BF)

    if slots:
        out_spec = pl.BlockSpec((1, K, tn), lambda j, i: (j, 0, 0))
        out_shape = jax.ShapeDtypeStruct((N // tn, K, tn), _BF)
    else:
        out_spec = pl.BlockSpec((K, tn), lambda j, i: (0, j))
        out_shape = jax.ShapeDtypeStruct((K, N), _BF)
    return pl.pallas_call(
        body, name=name, grid=(N // tn, L // tm),
        in_specs=[pl.BlockSpec((tm, K), lambda j, i: (i, 0)), pl.BlockSpec((tm, tn), lambda j, i: (i, j))],
        out_specs=out_spec, out_shape=out_shape,
        scratch_shapes=[pltpu.VMEM((K, tn), _F32)],
        compiler_params=_params("parallel", "arbitrary"),
    )(a, b)


def _mixout_bwd(dmix, w_out, w_glu, glu, s, o, gate, ggn, tm):
    L = dmix.shape[0]

    def body(dmix_ref, wo_ref, wg_ref, glu_ref, s_ref, o_ref, gate_ref, ggn_ref,
             dglu_ref, ds_ref, dgate_ref, do_ref, dggn_ref):
        @pl.when(pl.program_id(0) == 0)
        def _():
            dggn_ref[...] = jnp.zeros_like(dggn_ref)

        dcat = _dot_nt(dmix_ref[...], wo_ref[...])
        dy_ret, dy_ssm = dcat[:, :RET_W], dcat[:, RET_W:]
        glu = glu_ref[...]
        ga, sg = glu[:, :SSM_W], _sigmoid(glu[:, SSM_W:])
        dga = (dy_ssm * sg).astype(_BF)
        dgb = (dy_ssm * ga * sg * (1.0 - sg)).astype(_BF)
        dglu_ref[:, :SSM_W] = dga
        dglu_ref[:, SSM_W:] = dgb
        dys = _dot_nt(dga, wg_ref[:, :SSM_W]) + _dot_nt(dgb, wg_ref[:, SSM_W:])
        ds_ref[...] = dys * _gelu_grad(s_ref[...])
        gt = gate_ref[...]
        sgt = _sigmoid(gt)
        ggn = ggn_ref[...]
        for hh in range(N_HEAD):
            cols = slice(hh * HEAD_D, (hh + 1) * HEAD_D)
            ov = o_ref[:, cols]
            dlt = ov - jnp.mean(ov, axis=-1, keepdims=True)
            rstd = lax.rsqrt(jnp.mean(dlt * dlt, axis=-1, keepdims=True) + NORM_EPS)
            on = dlt * rstd
            dyr = dy_ret[:, cols] * (gt[:, cols] * sgt[:, cols])
            dgate_ref[:, cols] = dy_ret[:, cols] * (on * ggn[:, cols]) * (sgt[:, cols] * (1.0 + gt[:, cols] * (1.0 - sgt[:, cols])))
            dggn_ref[:, cols] += jnp.sum(dyr * on, axis=0, keepdims=True)
            don = dyr * ggn[:, cols]
            do = rstd * (don - jnp.mean(don, axis=-1, keepdims=True) - on * jnp.mean(don * on, axis=-1, keepdims=True))
            do_ref[:, cols] = do.astype(_BF)

    return pl.pallas_call(
        body, name="mixout_bwd", grid=(L // tm,),
        in_specs=[_row_spec(tm, D_MODEL), _full_spec((D_MODEL, D_MODEL)), _full_spec((SSM_W, 2 * SSM_W)),
                  _row_spec(tm, 2 * SSM_W), _row_spec(tm, SSM_W), _row_spec(tm, RET_W), _row_spec(tm, RET_W),
                  _full_spec((1, RET_W))],
        out_specs=[_row_spec(tm, 2 * SSM_W), _row_spec(tm, SSM_W), _row_spec(tm, RET_W), _row_spec(tm, RET_W),
                   _full_spec((1, RET_W))],
        out_shape=[jax.ShapeDtypeStruct((L, 2 * SSM_W), _BF), jax.ShapeDtypeStruct((L, SSM_W), _F32),
                   jax.ShapeDtypeStruct((L, RET_W), _F32), jax.ShapeDtypeStruct((L, RET_W), _BF),
                   jax.ShapeDtypeStruct((1, RET_W), _F32)],
        compiler_params=_params("arbitrary"),
    )(dmix, w_out, w_glu, glu, s, o, gate, ggn)


def _s5_bwd(u, ds, cin, bmat, cmat, tab_f, pw_f, tab_r, pw_r, d_skip, tb):
    L = u.shape[0]
    nt = L // tb
    seg = tb // SUBLANES
    rcol = pl.BlockSpec((tb, LANES), lambda kb, t: (nt - 1 - t, kb))
    sp = _s5_specs(seg)
    aspec = pl.BlockSpec((1, SUBLANES, 2 * KB_STATES), lambda kb, t: (kb, 0, 0))

    def body(u_ref, ds_ref, cin_ref, b_ref, c_ref, tf_ref, pf_ref, tr_ref, pr_ref, d_ref,
             du_ref, db_ref, dc_ref, da_ref, dd_ref, up_scr, dp_scr, x_scr, xp_scr, g_scr, fc_scr, lc_scr):
        @pl.when(pl.program_id(1) == 0)
        def _():
            lc_scr[...] = jnp.zeros_like(lc_scr)
            db_ref[...] = jnp.zeros_like(db_ref)
            dc_ref[...] = jnp.zeros_like(dc_ref)
            da_ref[...] = jnp.zeros_like(da_ref)
            dd_ref[...] = jnp.zeros_like(dd_ref)

        _rows_to_segments(up_scr, u_ref, seg)
        _rows_to_segments(dp_scr, ds_ref, seg)
        uv, dsv = up_scr[...], dp_scr[...]
        ub, dsb = uv.astype(_BF), dsv.astype(_BF)
        fc_scr[...] = cin_ref[0, 0]
        x_scr[...] = _dot(ub, b_ref[0]).reshape(seg, SUBLANES, 2 * KB_STATES)
        _scan_segments(x_scr, tf_ref, pf_ref, fc_scr, seg, reverse=False, xprev_ref=xp_scr)
        g_scr[...] = _dot_nt(dsb, c_ref[0]).reshape(seg, SUBLANES, 2 * KB_STATES)
        _scan_segments(g_scr, tr_ref, pr_ref, lc_scr, seg, reverse=True, xprev_ref=xp_scr, da_ref=da_ref)
        lamb = g_scr[...].reshape(tb, 2 * KB_STATES).astype(_BF)
        _segments_to_rows(du_ref, _dot_nt(lamb, b_ref[0]) + d_ref[...] * dsv, seg)
        db_ref[0] += _dot_tn(ub, lamb)
        dc_ref[0] += _dot_tn(x_scr[...].reshape(tb, 2 * KB_STATES).astype(_BF), dsb)
        dd_ref[...] += jnp.sum(dsv * uv, axis=0, keepdims=True)

    state = pltpu.VMEM((seg, SUBLANES, 2 * KB_STATES), _F32)
    return pl.pallas_call(
        body, name="s5_bwd", grid=(N_KB, nt),
        in_specs=[rcol, rcol, pl.BlockSpec((1, 1, SUBLANES, 2 * KB_STATES), lambda kb, t: (kb, nt - 1 - t, 0, 0)),
                  sp["b"], sp["c"], sp["tab"], sp["pw"], sp["tab"], sp["pw"], sp["d"]],
        out_specs=[rcol, sp["b"], sp["c"], aspec, sp["d"]],
        out_shape=[jax.ShapeDtypeStruct((L, SSM_W), _F32),
                   jax.ShapeDtypeStruct((N_KB, LANES, 2 * KB_STATES), _F32),
                   jax.ShapeDtypeStruct((N_KB, 2 * KB_STATES, LANES), _F32),
                   jax.ShapeDtypeStruct((N_KB, SUBLANES, 2 * KB_STATES), _F32),
                   jax.ShapeDtypeStruct((1, SSM_W), _F32)],
        scratch_shapes=[pltpu.VMEM((tb, LANES), _F32)] * 2 + [state] * 3 + [pltpu.VMEM((SUBLANES, 2 * KB_STATES), _F32)] * 2,
        compiler_params=_params("parallel", "arbitrary"),
    )(u, ds, cin, bmat, cmat, tab_f, pw_f, tab_r, pw_r, d_skip)


def _retention_bwd(q, k, v, do, r_prev, consts, cosf, sinf):
    L = q.shape[0]
    nc = L // CHUNK
    blk = pl.BlockSpec((CHUNK, RET_W), lambda n: (nc - 1 - n, 0))
    rope_blk = pl.BlockSpec((CHUNK, HEAD_D), lambda n: (nc - 1 - n, 0))

    def body(q_ref, k_ref, v_ref, do_ref, rp_ref, dm_ref, xi_ref, zeta_ref, gc_ref, cos_ref, sin_ref,
             dq_ref, dk_ref, dv_ref, g_scr):
        @pl.when(pl.program_id(0) == 0)
        def _():
            g_scr[...] = jnp.zeros_like(g_scr)

        cs, sn = cos_ref[...], sin_ref[...]
        for hh in range(N_HEAD):
            cols = slice(hh * HEAD_D, (hh + 1) * HEAD_D)
            qv, kv, vv, dov = q_ref[:, cols], k_ref[:, cols], v_ref[:, cols], do_ref[:, cols]
            rb = rp_ref[hh, 0].astype(_BF)
            gst = g_scr[hh]
            gb = gst.astype(_BF)
            dm, zeta = dm_ref[hh], zeta_ref[hh]
            sb = (_dot_nt(qv, kv) * dm).astype(_BF)
            dab = (_dot_nt(dov, vv) * dm).astype(_BF)
            dox = (dov.astype(_F32) * xi_ref[hh]).astype(_BF)
            vz = (vv.astype(_F32) * zeta).astype(_BF)
            dq = _dot(dab, kv) + _dot_nt(dox, rb)
            dk = _dot_tn(dab, qv) + _dot_nt(vz, gb)
            dv = _dot_tn(sb, dov) + _dot(kv, gb) * zeta
            g_scr[hh] = gc_ref[hh, 0:1, :] * gst + _dot_tn(qv, dox)
            dq_ref[:, cols] = _rope_t(dq, cs, sn).astype(_BF)
            dk_ref[:, cols] = (_rope_t(dk, cs, sn) * (HEAD_D ** -0.5)).astype(_BF)
            dv_ref[:, cols] = dv.astype(_BF)

    return pl.pallas_call(
        body, name="retention_bwd", grid=(nc,),
        in_specs=[blk, blk, blk, blk, pl.BlockSpec((N_HEAD, 1, HEAD_D, HEAD_D), lambda n: (0, nc - 1 - n, 0, 0))]
        + _head_specs() + [rope_blk, rope_blk],
        out_specs=[blk, blk, blk],
        out_shape=[jax.ShapeDtypeStruct((L, RET_W), _BF)] * 3,
        scratch_shapes=[pltpu.VMEM((N_HEAD, HEAD_D, HEAD_D), _F32)],
        compiler_params=_params("arbitrary"),
    )(q, k, v, do, r_prev, *consts, cosf, sinf)


def _inproj_bwd(pieces, w_in, x, dx2, g1, tm):
    L = x.shape[0]

    def body(p0, p1, p2, p3, p4, w_ref, x_ref, dx2_ref, g_ref, dx_ref, dg_ref):
        @pl.when(pl.program_id(0) == 0)
        def _():
            dg_ref[...] = jnp.zeros_like(dg_ref)

        dh = None
        for j, p in enumerate((p0, p1, p2, p3, p4)):
            part = _dot_nt(p[...].astype(_BF), w_ref[:, j * RET_W:(j + 1) * RET_W])
            dh = part if dh is None else dh + part
        dz, dgr = _rms_bwd(x_ref[...], g_ref[...], dh)
        dx_ref[...] = dx2_ref[...] + dz
        dg_ref[...] += jnp.sum(dgr, axis=0, keepdims=True)

    return pl.pallas_call(
        body, name="inproj_bwd", grid=(L // tm,),
        in_specs=[_row_spec(tm, RET_W)] * 5 + [_full_spec((D_MODEL, IN_COLS)), _row_spec(tm, D_MODEL),
                                                 _row_spec(tm, D_MODEL), _full_spec((1, D_MODEL))],
        out_specs=[_row_spec(tm, D_MODEL), _full_spec((1, D_MODEL))],
        out_shape=[jax.ShapeDtypeStruct((L, D_MODEL), _F32), jax.ShapeDtypeStruct((1, D_MODEL), _F32)],
        compiler_params=_params("arbitrary"),
    )(*pieces, w_in, x, dx2, g1)


def _sum_adamw(parts, w, m, v, tr, name):
    _, R, Cc = parts.shape

    def body(p_ref, w_ref, m_ref, v_ref, g_ref, d_ref, nm_ref, nv_ref):
        gv = p_ref[0].astype(_F32)
        for s in range(1, N_DEV):
            gv = gv + p_ref[s].astype(_F32)
        g_ref[...] = gv
        nm = ADAM_B1 * m_ref[...] + (1.0 - ADAM_B1) * gv
        nv = ADAM_B2 * v_ref[...] + (1.0 - ADAM_B2) * (gv * gv)
        m_hat = nm / (1.0 - ADAM_B1 ** ADAM_STEP)
        v_hat = nv / (1.0 - ADAM_B2 ** ADAM_STEP)
        d_ref[...] = -ADAM_LR * (m_hat / (jnp.sqrt(v_hat) + ADAM_EPS) + ADAM_WD * w_ref[...])
        nm_ref[...] = nm
        nv_ref[...] = nv

    spec = _row_spec(tr, Cc)
    return pl.pallas_call(
        body, name=name, grid=(R // tr,),
        in_specs=[pl.BlockSpec((N_DEV, tr, Cc), lambda i: (0, i, 0))] + [spec] * 3, out_specs=[spec] * 4,
        out_shape=[jax.ShapeDtypeStruct((R, Cc), _F32)] * 4,
        compiler_params=_params("parallel"),
    )(parts, w, m, v)


def _my_place():
    return lax.axis_index("x"), lax.axis_index("y"), lax.axis_index("c")


def _all_gather(blocks):
    n = len(blocks)

    def body(*refs):
        x_refs, out_refs = refs[:n], refs[n:2 * n]
        send_sems, recv_sems, local_sems = refs[2 * n:]
        x, y, c = _my_place()
        me, sibling = (x, y, c), (x, y, 1 - c)
        chips = [(1 - x, y), (x, 1 - y), (1 - x, 1 - y)]

        def slot(a, px, py, pc):
            return out_refs[a].at[4 * px + 2 * py + pc]

        def copy(a, k, blk, to, own=False):
            return pltpu.make_async_remote_copy(
                src_ref=x_refs[a] if own else slot(a, *blk), dst_ref=slot(a, *blk),
                send_sem=send_sems.at[a, k], recv_sem=recv_sems.at[a, k], device_id=to, device_id_type=MESH)

        mine = [pltpu.make_async_copy(x_refs[a], slot(a, *me), local_sems.at[a]) for a in range(n)]
        for cp in mine:
            cp.start()
        first = []
        for a in range(n):
            first.append(copy(a, 0, me, sibling, own=True))
            first += [copy(a, 1 + j, me, (*chip, c), own=True) for j, chip in enumerate(chips)]
        for cp in first:
            cp.start()
        passed = []
        for j, chip in enumerate(chips):
            for a in range(n):
                copy(a, 1 + j, (*chip, c), me).wait_recv()
                fwd = copy(a, 4 + j, (*chip, c), sibling)
                fwd.start()
                passed.append(fwd)
        for a in range(n):
            copy(a, 0, sibling, me).wait_recv()
            for j, chip in enumerate(chips):
                copy(a, 4 + j, (*chip, 1 - c), me).wait_recv()
        for cp in first + passed:
            cp.wait_send()
        for cp in mine:
            cp.wait()

    any_spec = pl.BlockSpec(memory_space=pl.ANY)
    return pl.pallas_call(
        body, name="weights_all_gather",
        in_specs=[any_spec] * n, out_specs=[any_spec] * n,
        out_shape=[jax.ShapeDtypeStruct((N_DEV,) + b.shape, b.dtype) for b in blocks],
        scratch_shapes=[pltpu.SemaphoreType.DMA((n, 7)), pltpu.SemaphoreType.DMA((n, 7)), pltpu.SemaphoreType.DMA((n,))],
    )(*blocks)


def _exchange(bigs, small):
    n = len(bigs)
    r = small.shape[0]

    def body(*refs):
        in_refs, out_refs = refs[:n + 1], refs[n + 1:2 * n + 2]
        send_sems, recv_sems, local_sems = refs[2 * n + 2:]
        x, y, c = _my_place()
        me = 4 * x + 2 * y + c
        own = [pltpu.make_async_copy(in_refs[a].at[me], out_refs[a].at[me], local_sems.at[a]) for a in range(n)]
        own.append(pltpu.make_async_copy(in_refs[n], out_refs[n].at[me], local_sems.at[n]))
        for cp in own:
            cp.start()
        copies = []
        for kk in range(1, N_DEV):
            px, py, pc = x ^ (kk >> 2), y ^ ((kk >> 1) & 1), c ^ (kk & 1)
            peer = 4 * px + 2 * py + pc
            for a in range(n + 1):
                src = in_refs[a].at[peer] if a < n else in_refs[a]
                copies.append(pltpu.make_async_remote_copy(
                    src_ref=src, dst_ref=out_refs[a].at[me],
                    send_sem=send_sems.at[a, kk - 1], recv_sem=recv_sems.at[a, kk - 1],
                    device_id=(px, py, pc), device_id_type=MESH))
        for cp in copies:
            cp.start()
        for cp in copies:
            cp.wait_recv()
        for cp in copies:
            cp.wait_send()
        for cp in own:
            cp.wait()

    any_spec = pl.BlockSpec(memory_space=pl.ANY)
    outs = pl.pallas_call(
        body, name="grad_exchange",
        in_specs=[any_spec] * (n + 1), out_specs=[any_spec] * (n + 1),
        out_shape=[jax.ShapeDtypeStruct(b.shape, b.dtype) for b in bigs]
        + [jax.ShapeDtypeStruct((N_DEV, r, LANES), small.dtype)],
        scratch_shapes=[pltpu.SemaphoreType.DMA((n + 1, 7)), pltpu.SemaphoreType.DMA((n + 1, 7)),
                        pltpu.SemaphoreType.DMA((n + 1,))],
    )(*bigs, small)
    return outs[:n], outs[n]


def _discretize(lam_re, lam_im, log_dt, b_re, b_im):
    lr = jnp.minimum(lam_re, -1e-4)
    li = lam_im
    dt = jnp.exp(log_dt)[:, None]
    er = jnp.exp(lr * dt)
    ar, ai = er * jnp.cos(li * dt), er * jnp.sin(li * dt)
    den = lr * lr + li * li
    cr = ((ar - 1.0) * lr + ai * li) / den
    ci = (ai * lr - (ar - 1.0) * li) / den
    bbr = cr[:, :, None] * b_re - ci[:, :, None] * b_im
    bbi = cr[:, :, None] * b_im + ci[:, :, None] * b_re
    return ar, ai, bbr, bbi


def _cmul(ar, ai, br, bi):
    return ar * br - ai * bi, ar * bi + ai * br


def _cpowers(ar, ai, n):
    pr, pi = ar[None], ai[None]
    while pr.shape[0] < n:
        nr, ni = _cmul(pr, pi, pr[-1][None], pi[-1][None])
        pr, pi = jnp.concatenate([pr, nr]), jnp.concatenate([pi, ni])
    return pr[:n], pi[:n]


def _scan_tables(ar, ai, seg, reverse):
    if reverse:
        ai = -ai
    ar, ai = ar.reshape(N_KB, KB_STATES), ai.reshape(N_KB, KB_STATES)
    pr, pi = _cpowers(ar, ai, seg)
    a1 = (pr[-1], pi[-1])
    a2 = _cmul(*a1, *a1)
    a4 = _cmul(*a2, *a2)
    row = jnp.arange(SUBLANES)[None, :, None]
    wide = lambda t: jnp.broadcast_to(t[:, None, :], (N_KB, SUBLANES, KB_STATES))
    tabs = [wide(ar), wide(ai)]
    for dist, (qr, qi) in ((1, a1), (2, a2), (4, a4)):
        keep = (row < SUBLANES - dist) if reverse else (row >= dist)
        tabs += [jnp.where(keep, wide(qr), 0.0), jnp.where(keep, wide(qi), 0.0)]
    tabs += [wide(a1[0]), wide(a1[1])]
    if reverse:
        pr, pi = pr[::-1], pi[::-1]
    pw = jnp.transpose(jnp.concatenate([pr, pi], axis=-1), (1, 0, 2))[:, :, None, :]
    return jnp.stack(tabs, axis=1).astype(_F32), pw.astype(_F32)


def _block_diag_in(br, bi):
    eye = jnp.eye(GROUPS_PER_KB, dtype=_F32)
    one = lambda t: jnp.einsum("kgpc,gh->kgchp", t.reshape(N_KB, GROUPS_PER_KB, N_STATE, SSM_GC), eye).reshape(
        N_KB, LANES, KB_STATES)
    return jnp.concatenate([one(br), one(bi)], axis=-1)


def _block_diag_in_t(dmat):
    d6 = dmat.reshape(N_KB, GROUPS_PER_KB, SSM_GC, 2, GROUPS_PER_KB, N_STATE)
    eye = jnp.eye(GROUPS_PER_KB, dtype=_F32)
    both = jnp.einsum("kgcrhp,gh->rkgpc", d6, eye).reshape(2, N_GROUP, N_STATE, SSM_GC)
    return both[0], both[1]


def _block_diag_out(c_re, c_im):
    eye = jnp.eye(GROUPS_PER_KB, dtype=_F32)
    one = lambda t: jnp.einsum("kgcp,gh->khpgc", t.reshape(N_KB, GROUPS_PER_KB, SSM_GC, N_STATE), eye).reshape(
        N_KB, KB_STATES, LANES)
    return jnp.concatenate([one(c_re), -one(c_im)], axis=1)


def _block_diag_out_t(dmat):
    d6 = dmat.reshape(N_KB, 2, GROUPS_PER_KB, N_STATE, GROUPS_PER_KB, SSM_GC)
    eye = jnp.eye(GROUPS_PER_KB, dtype=_F32)
    both = jnp.einsum("krhpgc,gh->rkgcp", d6, eye).reshape(2, N_GROUP, SSM_GC, N_STATE)
    return both[0], -both[1]


SMALL_NAMES = ("norm_mix_pre", "norm_mix_post", "ret_gn_gain", "ssm_lambda_re", "ssm_lambda_im", "ssm_log_dt",
               "ssm_b_re", "ssm_b_im", "ssm_c_re", "ssm_c_im", "ssm_d", "norm_mlp_pre", "norm_mlp_post")


def _local_grads(x, tgt, small, w_in, w_glu, w_out, w_ff1, w_ff2, tm, tk, tb):
    L = x.shape[0]
    g1, g2, ggn = small["norm_mix_pre"], small["norm_mix_post"], small["ret_gn_gain"]
    g3, g4, d_skip = small["norm_mlp_pre"], small["norm_mlp_post"], small["ssm_d"]

    half = HEAD_D // 2
    inv_freq = ROPE_BASE ** (-jnp.arange(half, dtype=_F32) / half)
    ang = jnp.arange(L, dtype=_F32)[:, None] * inv_freq[None, :]
    cosf = jnp.concatenate([jnp.cos(ang), jnp.cos(ang)], axis=-1)
    sinf = jnp.concatenate([-jnp.sin(ang), jnp.sin(ang)], axis=-1)
    consts = _ret_consts()

    disc_in = (small["ssm_lambda_re"][0], small["ssm_lambda_im"][0], small["ssm_log_dt"][0],
               small["ssm_b_re"][0], small["ssm_b_im"][0])
    (ar, ai, bbr, bbi), disc_vjp = jax.vjp(_discretize, *disc_in)
    bmat = _block_diag_in(bbr, bbi).astype(_BF)
    cmat = _block_diag_out(small["ssm_c_re"][0], small["ssm_c_im"][0]).astype(_BF)
    seg = tb // SUBLANES
    tab_f, pw_f = _scan_tables(ar, ai, seg, False)
    tab_r, pw_r = _scan_tables(ar, ai, seg, True)

    h1, q, k, v, gate, u = _inproj_fwd(x, g1, w_in, cosf, sinf, tm)
    o, y_ret, r_prev = _retention_fwd(q, k, v, gate, ggn, consts)
    s, cin = _s5_fwd(u, bmat, cmat, tab_f, pw_f, d_skip, tb)
    ys, glu, cat, mix, x2 = _mixout_fwd(s, y_ret, x, w_glu, w_out, g2, tm)
    h3, f1 = _ff1_fwd(x2, g3, w_ff1, tm)
    dy, dm, dg4, sq = _ff2_loss(f1, x2, tgt, g4, w_ff2, tm)

    df1, dw_ff2 = _ff2_bwd(dm, f1, w_ff2, min(1024, L), 1024)
    dx2, dmix, dg3, dg2 = _ff1_bwd(df1, w_ff1, x2, mix, dy, g3, g2, tm)
    dw_ff1 = _matmul_tn(h3, df1, tk, FF1_COLS, "dw_ff1", slots=True)
    dglu, ds, dgate, do, dggn = _mixout_bwd(dmix, w_out, w_glu, glu, s, o, gate, ggn, tm)
    dw_out = _matmul_tn(cat, dmix, tk, 1024, "dw_out")
    dw_glu = _matmul_tn(ys, dglu, tk, 1024, "dw_glu")
    du, dbmat, dcmat, da8, dd = _s5_bwd(u, ds, cin, bmat, cmat, tab_f, pw_f, tab_r, pw_r, d_skip, tb)
    dq, dk, dv = _retention_bwd(q, k, v, do, r_prev, consts, cosf, sinf)
    pieces = (dq, dk, dv, dgate, du)
    gx, dg1 = _inproj_bwd(pieces, w_in, x, dx2, g1, tm)
    dw_in = jnp.concatenate([_matmul_tn(h1, p, tk, RET_W, "dw_in_%d" % j) for j, p in enumerate(pieces)], axis=1)

    da = jnp.sum(da8, axis=1)
    dar = da[:, :KB_STATES].reshape(N_GROUP, N_STATE)
    dai = da[:, KB_STATES:].reshape(N_GROUP, N_STATE)
    dbr, dbi = _block_diag_in_t(dbmat)
    dlre, dlim, dldt, dbre, dbim = disc_vjp((dar, dai, dbr, dbi))
    dcre, dcim = _block_diag_out_t(dcmat)

    gsmall = {
        "norm_mix_pre": dg1, "norm_mix_post": dg2, "ret_gn_gain": dggn,
        "ssm_lambda_re": dlre[None], "ssm_lambda_im": dlim[None], "ssm_log_dt": dldt[None],
        "ssm_b_re": dbre[None], "ssm_b_im": dbim[None], "ssm_c_re": dcre[None], "ssm_c_im": dcim[None],
        "ssm_d": dd, "norm_mlp_pre": dg3, "norm_mlp_post": dg4,
    }
    return sq, gx, gsmall, (dw_in, dw_glu, dw_out, dw_ff1, dw_ff2)


BIG_SHAPES = {"w_in": (D_MODEL, IN_COLS // N_DEV), "w_glu": (SSM_W, 2 * SSM_W // N_DEV), "w_out": (D_MODEL // N_DEV, D_MODEL),
              "w_ff1": (D_MODEL, FF1_COLS), "w_ff2": (D_FF // N_DEV, D_MODEL)}
BIG_NAMES = ("w_in", "w_glu", "w_out", "w_ff1", "w_ff2")


def _cols_from_slots(g):
    return jnp.transpose(g, (1, 0, 2)).reshape(g.shape[1], N_DEV * g.shape[2])


def _cols_to_slots(dw):
    r, cols = dw.shape
    return jnp.transpose(dw.reshape(r, N_DEV, cols // N_DEV), (1, 0, 2))


PIECE_ROWS = 8


def _small_layout(shapes):
    off, rows = {}, 0
    for n in SMALL_NAMES:
        off[n] = rows
        rows += -(-math.prod(shapes[n]) // (PIECE_ROWS * LANES)) * PIECE_ROWS
    return off, rows, rows + PIECE_ROWS


def _pack_small(vals, shapes, last=None):
    parts = []
    for n in SMALL_NAMES:
        flat = vals[n].reshape(-1).astype(_F32)
        pad = -flat.shape[0] % (PIECE_ROWS * LANES)
        if pad:
            flat = jnp.concatenate([flat, jnp.zeros((pad,), _F32)])
        parts.append(flat.reshape(-1, LANES))
    parts.append(jnp.zeros((PIECE_ROWS, LANES), _F32) if last is None else last)
    return jnp.concatenate(parts, axis=0)


def _unpack_small(buf, shapes):
    off, _, _ = _small_layout(shapes)
    out = {}
    for n in SMALL_NAMES:
        size = math.prod(shapes[n])
        rows = -(-size // LANES)
        out[n] = buf[off[n]:off[n] + rows].reshape(-1)[:size].reshape(shapes[n])
    return out


WEIGHT_NAMES = ('norm_mix_pre', 'norm_mix_post', 'w_in', 'ret_gn_gain', 'ssm_lambda_re', 'ssm_lambda_im', 'ssm_log_dt',
                'ssm_b_re', 'ssm_b_im', 'ssm_c_re', 'ssm_c_im', 'ssm_d', 'w_glu', 'w_out', 'norm_mlp_pre',
                'norm_mlp_post', 'w_ff1', 'w_ff2')


def kernel(x, norm_mix_pre, norm_mix_post, w_in, ret_gn_gain, ssm_lambda_re, ssm_lambda_im, ssm_log_dt, ssm_b_re, ssm_b_im, ssm_c_re, ssm_c_im, ssm_d, w_glu, w_out, norm_mlp_pre, norm_mlp_post, w_ff1, w_ff2, loss_target, m_norm_mix_pre, m_norm_mix_post, m_w_in, m_ret_gn_gain, m_ssm_lambda_re, m_ssm_lambda_im, m_ssm_log_dt, m_ssm_b_re, m_ssm_b_im, m_ssm_c_re, m_ssm_c_im, m_ssm_d, m_w_glu, m_w_out, m_norm_mlp_pre, m_norm_mlp_post, m_w_ff1, m_w_ff2, v_norm_mix_pre, v_norm_mix_post, v_w_in, v_ret_gn_gain, v_ssm_lambda_re, v_ssm_lambda_im, v_ssm_log_dt, v_ssm_b_re, v_ssm_b_im, v_ssm_c_re, v_ssm_c_im, v_ssm_d, v_w_glu, v_w_out, v_norm_mlp_pre, v_norm_mlp_post, v_w_ff1, v_w_ff2):
    args = dict(locals())
    w = {n: args[n] for n in WEIGHT_NAMES}
    m = {n: args["m_" + n] for n in WEIGHT_NAMES}
    v = {n: args["v_" + n] for n in WEIGHT_NAMES}
    L = x.shape[1]
    tm = min(256, L)
    tk = min(2048, L)
    tb = min(512, L)

    gathered = dict(zip(BIG_NAMES, _all_gather([w[n][0].astype(_BF) for n in BIG_NAMES])))
    full = {
        "w_in": _cols_from_slots(gathered["w_in"]),
        "w_glu": _cols_from_slots(gathered["w_glu"]),
        "w_out": gathered["w_out"].reshape(D_MODEL, D_MODEL),
        "w_ff1": gathered["w_ff1"],
        "w_ff2": gathered["w_ff2"].reshape(D_FF, D_MODEL),
    }

    small_w = {n: w[n] for n in SMALL_NAMES}
    sq, gx, gsmall, gbig = _local_grads(x[0], loss_target[0], small_w, full["w_in"], full["w_glu"], full["w_out"],
                                        full["w_ff1"], full["w_ff2"], tm, tk, tb)
    dw_in, dw_glu, dw_out, dw_ff1, dw_ff2 = gbig

    shapes = {n: w[n].shape for n in SMALL_NAMES}
    slots = [_cols_to_slots(dw_in), _cols_to_slots(dw_glu), dw_out.reshape((N_DEV,) + BIG_SHAPES["w_out"]),
             dw_ff1, dw_ff2.reshape((N_DEV,) + BIG_SHAPES["w_ff2"])]
    loss_rows = jnp.broadcast_to(0.5 / D_MODEL * jnp.sum(sq), (PIECE_ROWS, LANES)).astype(_F32)
    big_parts, small_parts = _exchange(slots, _pack_small(gsmall, shapes, loss_rows))

    grads, delta, new_m, new_v = {}, {}, {}, {}
    for n, parts in zip(BIG_NAMES, big_parts):
        r = BIG_SHAPES[n][0]
        res = _sum_adamw(parts, w[n][0], m[n][0], v[n][0], min(256, r), "adamw_" + n)
        grads[n], delta[n], new_m[n], new_v[n] = (t[None] for t in res)
    sw, sm, sv = _pack_small(w, shapes), _pack_small(m, shapes), _pack_small(v, shapes)
    res = _sum_adamw(small_parts, sw, sm, sv, sw.shape[0], "adamw_small")
    for dst, buf in zip((grads, delta, new_m, new_v), res):
        dst.update(_unpack_small(buf, shapes))
    _, loss_at, _ = _small_layout(shapes)
    loss = res[0][loss_at, 0]

    return (loss, gx[None], *[grads[n] for n in WEIGHT_NAMES], *[delta[n] for n in WEIGHT_NAMES],
            *[new_m[n] for n in WEIGHT_NAMES], *[new_v[n] for n in WEIGHT_NAMES])
```

```python
import math

import jax
import jax.numpy as jnp
from jax import lax
from jax.experimental import pallas as pl
from jax.experimental.pallas import tpu as pltpu

_BF = jnp.bfloat16
_F32 = jnp.float32

D_MODEL = 1024
RET_W = 512
N_HEAD = 4
HEAD_D = 128
CHUNK = 128
SSM_W = 512
SSM_GC = 16
N_GROUP = 32
N_STATE = 64
GROUPS_PER_KB = 8
N_KB = 4
KB_STATES = GROUPS_PER_KB * N_STATE
D_FF = 4096
IN_COLS = 2560
NORM_EPS = 1e-6
ROPE_BASE = 10000.0
N_DEV = 8

ADAM_LR = 0.001
ADAM_B1 = 0.9
ADAM_B2 = 0.999
ADAM_EPS = 1e-08
ADAM_WD = 0.01
ADAM_STEP = 10

SUBLANES = 8
LANES = 128
VMEM_LIMIT = 52 * 1024 * 1024
SCAN_STRIP = 512

MESH = pl.DeviceIdType.MESH


def _params(*sem):
    return pltpu.CompilerParams(dimension_semantics=sem, vmem_limit_bytes=VMEM_LIMIT)


def _dot(a, b):
    return jnp.dot(a, b, preferred_element_type=_F32)


def _dot_nt(a, b):
    return lax.dot_general(a, b, (((1,), (1,)), ((), ())), preferred_element_type=_F32)


def _dot_tn(a, b):
    return lax.dot_general(a, b, (((0,), (0,)), ((), ())), preferred_element_type=_F32)


def _rms_r(z):
    return lax.rsqrt(jnp.mean(z * z, axis=-1, keepdims=True) + NORM_EPS)


def _rms_bwd(z, g, dn):
    r = _rms_r(z)
    t = dn * g
    dz = r * t - z * (r * r * r * jnp.mean(t * z, axis=-1, keepdims=True))
    return dz, dn * z * r


def _rope(t, cs, sn):
    return t * cs + pltpu.roll(t, HEAD_D // 2, 1) * sn


def _rope_t(t, cs, sn):
    return t * cs - pltpu.roll(t, HEAD_D // 2, 1) * sn


def _sigmoid(z):
    return 1.0 / (1.0 + jnp.exp(-z))


_GELU_C = math.sqrt(2.0 / math.pi)


def _gelu(z):
    return 0.5 * z * (1.0 + jnp.tanh(_GELU_C * (z + 0.044715 * z * z * z)))


def _gelu_grad(z):
    th = jnp.tanh(_GELU_C * (z + 0.044715 * z * z * z))
    return 0.5 * (1.0 + th) + 0.5 * z * (1.0 - th * th) * _GELU_C * (1.0 + 3 * 0.044715 * z * z)


def _row_spec(tm, n):
    return pl.BlockSpec((tm, n), lambda i: (i, 0))


def _full_spec(shape):
    nd = len(shape)
    return pl.BlockSpec(shape, lambda *_: (0,) * nd)


def _inproj_fwd(x, g1, w_in, cosf, sinf, tm):
    L = x.shape[0]

    def body(x_ref, g_ref, w_ref, cos_ref, sin_ref, h_ref, q_ref, k_ref, v_ref, gate_ref, u_ref):
        xv = x_ref[...]
        h = (xv * _rms_r(xv) * g_ref[...]).astype(_BF)
        h_ref[...] = h
        proj = _dot(h, w_ref[...])
        cs, sn = cos_ref[...], sin_ref[...]
        for hh in range(N_HEAD):
            lo = hh * HEAD_D
            q_ref[:, lo:lo + HEAD_D] = _rope(proj[:, lo:lo + HEAD_D], cs, sn).astype(_BF)
            kh = _rope(proj[:, RET_W + lo:RET_W + lo + HEAD_D], cs, sn) * (HEAD_D ** -0.5)
            k_ref[:, lo:lo + HEAD_D] = kh.astype(_BF)
        v_ref[...] = proj[:, 2 * RET_W:3 * RET_W].astype(_BF)
        gate_ref[...] = proj[:, 3 * RET_W:4 * RET_W]
        u_ref[...] = proj[:, 4 * RET_W:]

    return pl.pallas_call(
        body, name="inproj_fwd", grid=(L // tm,),
        in_specs=[_row_spec(tm, D_MODEL), _full_spec((1, D_MODEL)), _full_spec((D_MODEL, IN_COLS)),
                  _row_spec(tm, HEAD_D), _row_spec(tm, HEAD_D)],
        out_specs=[_row_spec(tm, D_MODEL)] + [_row_spec(tm, RET_W)] * 5,
        out_shape=[jax.ShapeDtypeStruct((L, D_MODEL), _BF)] + [jax.ShapeDtypeStruct((L, RET_W), _BF)] * 3
        + [jax.ShapeDtypeStruct((L, RET_W), _F32)] * 2,
        compiler_params=_params("parallel"),
    )(x, g1, w_in, cosf, sinf)


def _ret_consts():
    lg = jnp.log(1.0 - jnp.exp(jnp.linspace(math.log(1.0 / 32), math.log(1.0 / 512), N_HEAD))).astype(_F32)
    idx = jnp.arange(CHUNK, dtype=_F32)
    diff = idx[:, None] - idx[None, :]
    decay = jnp.where(diff[None] >= 0, jnp.exp(jnp.maximum(diff, 0.0)[None] * lg[:, None, None]), 0.0)
    zeta = jnp.exp((CHUNK - 1 - idx)[None, :] * lg[:, None])
    xi = jnp.exp((idx + 1.0)[None, :] * lg[:, None])
    gc = jnp.exp(CHUNK * lg)
    wide = lambda t: jnp.broadcast_to(t[:, :, None], (N_HEAD, CHUNK, HEAD_D)).astype(_F32)
    gcw = jnp.broadcast_to(gc[:, None, None], (N_HEAD, SUBLANES, HEAD_D)).astype(_F32)
    return decay.astype(_F32), wide(xi), wide(zeta), gcw


def _head_specs():
    c3 = _full_spec((N_HEAD, CHUNK, CHUNK))
    return [c3, c3, c3, _full_spec((N_HEAD, SUBLANES, HEAD_D))]


def _retention_fwd(q, k, v, gate, ggn, consts):
    L = q.shape[0]
    nc = L // CHUNK
    blk = pl.BlockSpec((CHUNK, RET_W), lambda n: (n, 0))

    def body(q_ref, k_ref, v_ref, gate_ref, ggn_ref, dm_ref, xi_ref, zeta_ref, gc_ref,
             o_ref, y_ref, rp_ref, r_scr):
        @pl.when(pl.program_id(0) == 0)
        def _():
            r_scr[...] = jnp.zeros_like(r_scr)

        for hh in range(N_HEAD):
            cols = slice(hh * HEAD_D, (hh + 1) * HEAD_D)
            qv, kv, vv = q_ref[:, cols], k_ref[:, cols], v_ref[:, cols]
            r_prev = r_scr[hh]
            s = _dot_nt(qv, kv) * dm_ref[hh]
            o = _dot(s.astype(_BF), vv) + _dot(qv, r_prev.astype(_BF)) * xi_ref[hh]
            o_ref[:, cols] = o
            rp_ref[hh, 0] = r_prev
            vz = (vv.astype(_F32) * zeta_ref[hh]).astype(_BF)
            r_scr[hh] = gc_ref[hh, 0:1, :] * r_prev + _dot_tn(kv, vz)
            dlt = o - jnp.mean(o, axis=-1, keepdims=True)
            on = dlt * lax.rsqrt(jnp.mean(dlt * dlt, axis=-1, keepdims=True) + NORM_EPS)
            gt = gate_ref[:, cols]
            y_ref[:, cols] = (gt * _sigmoid(gt) * (on * ggn_ref[:, cols])).astype(_BF)

    return pl.pallas_call(
        body, name="retention_fwd", grid=(nc,),
        in_specs=[blk, blk, blk, blk, _full_spec((1, RET_W))] + _head_specs(),
        out_specs=[blk, blk, pl.BlockSpec((N_HEAD, 1, HEAD_D, HEAD_D), lambda n: (0, n, 0, 0))],
        out_shape=[jax.ShapeDtypeStruct((L, RET_W), _F32), jax.ShapeDtypeStruct((L, RET_W), _BF),
                   jax.ShapeDtypeStruct((N_HEAD, nc, HEAD_D, HEAD_D), _F32)],
        scratch_shapes=[pltpu.VMEM((N_HEAD, HEAD_D, HEAD_D), _F32)],
        compiler_params=_params("arbitrary"),
    )(q, k, v, gate, ggn, *consts)


def _rows_to_segments(dst_ref, src_ref, seg):
    for r in range(seg):
        dst_ref[pl.ds(r * SUBLANES, SUBLANES), :] = src_ref[pl.ds(r, SUBLANES, stride=seg), :]


def _segments_to_rows(dst_ref, val, seg):
    for r in range(seg):
        dst_ref[pl.ds(r, SUBLANES, stride=seg), :] = val[r * SUBLANES:(r + 1) * SUBLANES, :]


def _scan_segments(x_ref, tab_ref, pw_ref, carry_ref, seg, reverse, xprev_ref=None, da_ref=None):
    W = SCAN_STRIP
    row_id = lax.broadcasted_iota(jnp.int32, (SUBLANES, W), 0)
    edge_in = (row_id == SUBLANES - 1) if reverse else (row_id == 0)
    edge_out = 0 if reverse else SUBLANES - 1
    for strip in range(KB_STATES // W):
        re = pl.ds(strip * W, W)
        im = pl.ds(KB_STATES + strip * W, W)
        ar, ai = tab_ref[0, 0, :, re], tab_ref[0, 1, :, re]

        def local(i, st, re=re, im=im, ar=ar, ai=ai):
            sr, si = st
            r = (seg - 1 - i) if reverse else i
            nr = ar * sr - ai * si + x_ref[r, :, re]
            ni = ar * si + ai * sr + x_ref[r, :, im]
            x_ref[r, :, re] = nr
            x_ref[r, :, im] = ni
            return nr, ni

        zero = jnp.zeros((SUBLANES, W), _F32)
        er, ei = lax.fori_loop(0, seg, local, (zero, zero))

        shift = (SUBLANES - 1) if reverse else 1
        fr = jnp.where(edge_in, carry_ref[:, re], pltpu.roll(er, shift, 0))
        fi = jnp.where(edge_in, carry_ref[:, im], pltpu.roll(ei, shift, 0))
        for j, dist in enumerate((1, 2, 4)):
            pr, pi = tab_ref[0, 2 + 2 * j, :, re], tab_ref[0, 3 + 2 * j, :, re]
            sh = (SUBLANES - dist) if reverse else dist
            sr, si = pltpu.roll(fr, sh, 0), pltpu.roll(fi, sh, 0)
            fr, fi = fr + pr * sr - pi * si, fi + pr * si + pi * sr
        br, bi = tab_ref[0, 8, :, re], tab_ref[0, 9, :, re]
        outr = br * fr - bi * fi + er
        outi = br * fi + bi * fr + ei
        carry_ref[:, re] = jnp.broadcast_to(outr[edge_out:edge_out + 1, :], (SUBLANES, W))
        carry_ref[:, im] = jnp.broadcast_to(outi[edge_out:edge_out + 1, :], (SUBLANES, W))

        keep_prev = xprev_ref is not None and not reverse
        add_da = da_ref is not None

        def fix(r, st, re=re, im=im, fr=fr, fi=fi):
            pwr, pwi = pw_ref[0, r, :, re], pw_ref[0, r, :, im]
            xr = x_ref[r, :, re] + (pwr * fr - pwi * fi)
            xi = x_ref[r, :, im] + (pwr * fi + pwi * fr)
            x_ref[r, :, re] = xr
            x_ref[r, :, im] = xi
            if keep_prev:
                xprev_ref[r, :, re] = st[0]
                xprev_ref[r, :, im] = st[1]
                return xr, xi
            if add_da:
                xpr, xpi = xprev_ref[r, :, re], xprev_ref[r, :, im]
                return st[0] + (xr * xpr + xi * xpi), st[1] + (xi * xpr - xr * xpi)
            return st

        st = lax.fori_loop(0, seg, fix, (fr, fi) if keep_prev else (zero, zero))
        if add_da:
            da_ref[0, :, re] += st[0]
            da_ref[0, :, im] += st[1]


def _s5_specs(seg):
    return dict(
        b=pl.BlockSpec((1, LANES, 2 * KB_STATES), lambda kb, t: (kb, 0, 0)),
        c=pl.BlockSpec((1, 2 * KB_STATES, LANES), lambda kb, t: (kb, 0, 0)),
        tab=pl.BlockSpec((1, 10, SUBLANES, KB_STATES), lambda kb, t: (kb, 0, 0, 0)),
        pw=pl.BlockSpec((1, seg, 1, 2 * KB_STATES), lambda kb, t: (kb, 0, 0, 0)),
        d=pl.BlockSpec((1, LANES), lambda kb, t: (0, kb)),
    )


def _s5_fwd(u, bmat, cmat, tab_f, pw_f, d_skip, tb):
    L = u.shape[0]
    nt = L // tb
    seg = tb // SUBLANES
    ucol = pl.BlockSpec((tb, LANES), lambda kb, t: (t, kb))
    sp = _s5_specs(seg)

    def body(u_ref, b_ref, c_ref, tab_ref, pw_ref, d_ref, s_ref, cin_ref, up_scr, x_scr, carry_scr):
        @pl.when(pl.program_id(1) == 0)
        def _():
            carry_scr[...] = jnp.zeros_like(carry_scr)

        cin_ref[0, 0] = carry_scr[...]
        _rows_to_segments(up_scr, u_ref, seg)
        up = up_scr[...]
        x_scr[...] = _dot(up.astype(_BF), b_ref[0]).reshape(seg, SUBLANES, 2 * KB_STATES)
        _scan_segments(x_scr, tab_ref, pw_ref, carry_scr, seg, reverse=False)
        y = _dot(x_scr[...].reshape(tb, 2 * KB_STATES).astype(_BF), c_ref[0]) + d_ref[...] * up
        _segments_to_rows(s_ref, y, seg)

    return pl.pallas_call(
        body, name="s5_fwd", grid=(N_KB, nt),
        in_specs=[ucol, sp["b"], sp["c"], sp["tab"], sp["pw"], sp["d"]],
        out_specs=[ucol, pl.BlockSpec((1, 1, SUBLANES, 2 * KB_STATES), lambda kb, t: (kb, t, 0, 0))],
        out_shape=[jax.ShapeDtypeStruct((L, SSM_W), _F32),
                   jax.ShapeDtypeStruct((N_KB, nt, SUBLANES, 2 * KB_STATES), _F32)],
        scratch_shapes=[pltpu.VMEM((tb, LANES), _F32), pltpu.VMEM((seg, SUBLANES, 2 * KB_STATES), _F32),
                        pltpu.VMEM((SUBLANES, 2 * KB_STATES), _F32)],
        compiler_params=_params("parallel", "arbitrary"),
    )(u, bmat, cmat, tab_f, pw_f, d_skip)


def _mixout_fwd(s, y_ret, x, w_glu, w_out, g2, tm):
    L = s.shape[0]

    def body(s_ref, yr_ref, x_ref, wg_ref, wo_ref, g_ref, ys_ref, glu_ref, cat_ref, mix_ref, x2_ref):
        ys = _gelu(s_ref[...]).astype(_BF)
        ys_ref[...] = ys
        glu = _dot(ys, wg_ref[...])
        glu_ref[...] = glu
        cat_ref[:, :RET_W] = yr_ref[...]
        cat_ref[:, RET_W:] = (glu[:, :SSM_W] * _sigmoid(glu[:, SSM_W:])).astype(_BF)
        mix = _dot(cat_ref[...], wo_ref[...])
        mix_ref[...] = mix
        x2_ref[...] = x_ref[...] + mix * _rms_r(mix) * g_ref[...]

    return pl.pallas_call(
        body, name="mixout_fwd", grid=(L // tm,),
        in_specs=[_row_spec(tm, SSM_W), _row_spec(tm, RET_W), _row_spec(tm, D_MODEL),
                  _full_spec((SSM_W, 2 * SSM_W)), _full_spec((D_MODEL, D_MODEL)), _full_spec((1, D_MODEL))],
        out_specs=[_row_spec(tm, SSM_W), _row_spec(tm, 2 * SSM_W), _row_spec(tm, D_MODEL),
                   _row_spec(tm, D_MODEL), _row_spec(tm, D_MODEL)],
        out_shape=[jax.ShapeDtypeStruct((L, SSM_W), _BF), jax.ShapeDtypeStruct((L, 2 * SSM_W), _F32),
                   jax.ShapeDtypeStruct((L, D_MODEL), _BF), jax.ShapeDtypeStruct((L, D_MODEL), _F32),
                   jax.ShapeDtypeStruct((L, D_MODEL), _F32)],
        compiler_params=_params("parallel"),
    )(s, y_ret, x, w_glu, w_out, g2)


FF1_COLS = D_FF // N_DEV


def _ff1_fwd(x2, g3, w1, tm):
    L = x2.shape[0]

    def body(x_ref, g_ref, w_ref, h_ref, f_ref):
        xv = x_ref[...]
        h = (xv * _rms_r(xv) * g_ref[...]).astype(_BF)
        h_ref[...] = h
        for j in range(N_DEV):
            f_ref[:, j * FF1_COLS:(j + 1) * FF1_COLS] = _dot(h, w_ref[j])

    return pl.pallas_call(
        body, name="ff1_fwd", grid=(L // tm,),
        in_specs=[_row_spec(tm, D_MODEL), _full_spec((1, D_MODEL)), _full_spec((N_DEV, D_MODEL, FF1_COLS))],
        out_specs=[_row_spec(tm, D_MODEL), _row_spec(tm, D_FF)],
        out_shape=[jax.ShapeDtypeStruct((L, D_MODEL), _BF), jax.ShapeDtypeStruct((L, D_FF), _F32)],
        compiler_params=_params("parallel"),
    )(x2, g3, w1)


def _ff2_loss(f1, x2, tgt, g4, w2, tm):
    L = f1.shape[0]

    def body(f_ref, x_ref, t_ref, g_ref, w_ref, dy_ref, dm_ref, dg_ref, ls_ref):
        @pl.when(pl.program_id(0) == 0)
        def _():
            dg_ref[...] = jnp.zeros_like(dg_ref)
            ls_ref[...] = jnp.zeros_like(ls_ref)

        rl = jnp.maximum(f_ref[...], 0.0)
        m = _dot((rl * rl).astype(_BF), w_ref[...])
        g = g_ref[...]
        y = x_ref[...] + m * _rms_r(m) * g
        err = y - t_ref[...]
        ls_ref[...] += jnp.sum(err * err, axis=0, keepdims=True)
        dy = err * (1.0 / D_MODEL)
        dy_ref[...] = dy
        dm, dgr = _rms_bwd(m, g, dy)
        dm_ref[...] = dm.astype(_BF)
        dg_ref[...] += jnp.sum(dgr, axis=0, keepdims=True)

    return pl.pallas_call(
        body, name="ff2_loss", grid=(L // tm,),
        in_specs=[_row_spec(tm, D_FF), _row_spec(tm, D_MODEL), _row_spec(tm, D_MODEL),
                  _full_spec((1, D_MODEL)), _full_spec((D_FF, D_MODEL))],
        out_specs=[_row_spec(tm, D_MODEL), _row_spec(tm, D_MODEL), _full_spec((1, D_MODEL)), _full_spec((1, D_MODEL))],
        out_shape=[jax.ShapeDtypeStruct((L, D_MODEL), _F32), jax.ShapeDtypeStruct((L, D_MODEL), _BF),
                   jax.ShapeDtypeStruct((1, D_MODEL), _F32), jax.ShapeDtypeStruct((1, D_MODEL), _F32)],
        compiler_params=_params("arbitrary"),
    )(f1, x2, tgt, g4, w2)


def _ff2_bwd(dm, f1, w2, tm, tn):
    L = dm.shape[0]
    last = L // tm - 1

    def body(dm_ref, f_ref, w_ref, df_ref, dw_ref, acc):
        @pl.when(pl.program_id(1) == 0)
        def _():
            acc[...] = jnp.zeros_like(acc)

        dmv = dm_ref[...]
        rl = jnp.maximum(f_ref[...], 0.0)
        df_ref[...] = (_dot_nt(dmv, w_ref[...]) * (2.0 * rl)).astype(_BF)
        acc[...] += _dot_tn((rl * rl).astype(_BF), dmv)

        @pl.when(pl.program_id(1) == last)
        def _():
            dw_ref[...] = acc[...].astype(_BF)

    return pl.pallas_call(
        body, name="ff2_bwd", grid=(D_FF // tn, L // tm),
        in_specs=[pl.BlockSpec((tm, D_MODEL), lambda j, i: (i, 0)), pl.BlockSpec((tm, tn), lambda j, i: (i, j)),
                  pl.BlockSpec((tn, D_MODEL), lambda j, i: (j, 0))],
        out_specs=[pl.BlockSpec((tm, tn), lambda j, i: (i, j)), pl.BlockSpec((tn, D_MODEL), lambda j, i: (j, 0))],
        out_shape=[jax.ShapeDtypeStruct((L, D_FF), _BF), jax.ShapeDtypeStruct((D_FF, D_MODEL), _BF)],
        scratch_shapes=[pltpu.VMEM((tn, D_MODEL), _F32)],
        compiler_params=_params("parallel", "arbitrary"),
    )(dm, f1, w2)


def _ff1_bwd(df1, w1, x2, mix, dy, g3, g2, tm):
    L = df1.shape[0]

    def body(df_ref, w_ref, x2_ref, mix_ref, dy_ref, g3_ref, g2_ref, dx2_ref, dmix_ref, dg3_ref, dg2_ref):
        @pl.when(pl.program_id(0) == 0)
        def _():
            dg3_ref[...] = jnp.zeros_like(dg3_ref)
            dg2_ref[...] = jnp.zeros_like(dg2_ref)

        dh = _dot_nt(df_ref[:, 0:FF1_COLS], w_ref[0])
        for j in range(1, N_DEV):
            dh = dh + _dot_nt(df_ref[:, j * FF1_COLS:(j + 1) * FF1_COLS], w_ref[j])
        dz, dgr = _rms_bwd(x2_ref[...], g3_ref[...], dh)
        dg3_ref[...] += jnp.sum(dgr, axis=0, keepdims=True)
        dx2 = dy_ref[...] + dz
        dx2_ref[...] = dx2
        dmx, dgr2 = _rms_bwd(mix_ref[...], g2_ref[...], dx2)
        dg2_ref[...] += jnp.sum(dgr2, axis=0, keepdims=True)
        dmix_ref[...] = dmx.astype(_BF)

    vec = _full_spec((1, D_MODEL))
    return pl.pallas_call(
        body, name="ff1_bwd", grid=(L // tm,),
        in_specs=[_row_spec(tm, D_FF), _full_spec((N_DEV, D_MODEL, FF1_COLS)), _row_spec(tm, D_MODEL),
                  _row_spec(tm, D_MODEL), _row_spec(tm, D_MODEL), vec, vec],
        out_specs=[_row_spec(tm, D_MODEL), _row_spec(tm, D_MODEL), vec, vec],
        out_shape=[jax.ShapeDtypeStruct((L, D_MODEL), _F32), jax.ShapeDtypeStruct((L, D_MODEL), _BF),
                   jax.ShapeDtypeStruct((1, D_MODEL), _F32), jax.ShapeDtypeStruct((1, D_MODEL), _F32)],
        compiler_params=_params("arbitrary"),
    )(df1, w1, x2, mix, dy, g3, g2)


def _matmul_tn(a, b, tm, tn, name, slots=False):
    L, K = a.shape
    N = b.shape[1]
    last = L // tm - 1

    def body(a_ref, b_ref, o_ref, acc):
        @pl.when(pl.program_id(1) == 0)
        def _():
            acc[...] = jnp.zeros_like(acc)

        acc[...] += _dot_tn(a_ref[...].astype(_BF), b_ref[...].astype(_BF))

        @pl.when(pl.program_id(1) == last)
        def _():
            if slots:
                o_ref[0] = acc[...].astype(_BF)
            else:
                o_ref[...] = acc[...].astype(_BF)

    if slots:
        out_spec = pl.BlockSpec((1, K, tn), lambda j, i: (j, 0, 0))
        out_shape = jax.ShapeDtypeStruct((N // tn, K, tn), _BF)
    else:
        out_spec = pl.BlockSpec((K, tn), lambda j, i: (0, j))
        out_shape = jax.ShapeDtypeStruct((K, N), _BF)
    return pl.pallas_call(
        body, name=name, grid=(N // tn, L // tm),
        in_specs=[pl.BlockSpec((tm, K), lambda j, i: (i, 0)), pl.BlockSpec((tm, tn), lambda j, i: (i, j))],
        out_specs=out_spec, out_shape=out_shape,
        scratch_shapes=[pltpu.VMEM((K, tn), _F32)],
        compiler_params=_params("parallel", "arbitrary"),
    )(a, b)


def _mixout_bwd(dmix, w_out, w_glu, glu, s, o, gate, ggn, tm):
    L = dmix.shape[0]

    def body(dmix_ref, wo_ref, wg_ref, glu_ref, s_ref, o_ref, gate_ref, ggn_ref,
             dglu_ref, ds_ref, dgate_ref, do_ref, dggn_ref):
        @pl.when(pl.program_id(0) == 0)
        def _():
            dggn_ref[...] = jnp.zeros_like(dggn_ref)

        dcat = _dot_nt(dmix_ref[...], wo_ref[...])
        dy_ret, dy_ssm = dcat[:, :RET_W], dcat[:, RET_W:]
        glu = glu_ref[...]
        ga, sg = glu[:, :SSM_W], _sigmoid(glu[:, SSM_W:])
        dga = (dy_ssm * sg).astype(_BF)
        dgb = (dy_ssm * ga * sg * (1.0 - sg)).astype(_BF)
        dglu_ref[:, :SSM_W] = dga
        dglu_ref[:, SSM_W:] = dgb
        dys = _dot_nt(dga, wg_ref[:, :SSM_W]) + _dot_nt(dgb, wg_ref[:, SSM_W:])
        ds_ref[...] = dys * _gelu_grad(s_ref[...])
        gt = gate_ref[...]
        sgt = _sigmoid(gt)
        ggn = ggn_ref[...]
        for hh in range(N_HEAD):
            cols = slice(hh * HEAD_D, (hh + 1) * HEAD_D)
            ov = o_ref[:, cols]
            dlt = ov - jnp.mean(ov, axis=-1, keepdims=True)
            rstd = lax.rsqrt(jnp.mean(dlt * dlt, axis=-1, keepdims=True) + NORM_EPS)
            on = dlt * rstd
            dyr = dy_ret[:, cols] * (gt[:, cols] * sgt[:, cols])
            dgate_ref[:, cols] = dy_ret[:, cols] * (on * ggn[:, cols]) * (sgt[:, cols] * (1.0 + gt[:, cols] * (1.0 - sgt[:, cols])))
            dggn_ref[:, cols] += jnp.sum(dyr * on, axis=0, keepdims=True)
            don = dyr * ggn[:, cols]
            do = rstd * (don - jnp.mean(don, axis=-1, keepdims=True) - on * jnp.mean(don * on, axis=-1, keepdims=True))
            do_ref[:, cols] = do.astype(_BF)

    return pl.pallas_call(
        body, name="mixout_bwd", grid=(L // tm,),
        in_specs=[_row_spec(tm, D_MODEL), _full_spec((D_MODEL, D_MODEL)), _full_spec((SSM_W, 2 * SSM_W)),
                  _row_spec(tm, 2 * SSM_W), _row_spec(tm, SSM_W), _row_spec(tm, RET_W), _row_spec(tm, RET_W),
                  _full_spec((1, RET_W))],
        out_specs=[_row_spec(tm, 2 * SSM_W), _row_spec(tm, SSM_W), _row_spec(tm, RET_W), _row_spec(tm, RET_W),
                   _full_spec((1, RET_W))],
        out_shape=[jax.ShapeDtypeStruct((L, 2 * SSM_W), _BF), jax.ShapeDtypeStruct((L, SSM_W), _F32),
                   jax.ShapeDtypeStruct((L, RET_W), _F32), jax.ShapeDtypeStruct((L, RET_W), _BF),
                   jax.ShapeDtypeStruct((1, RET_W), _F32)],
        compiler_params=_params("arbitrary"),
    )(dmix, w_out, w_glu, glu, s, o, gate, ggn)


def _s5_bwd(u, ds, cin, bmat, cmat, tab_f, pw_f, tab_r, pw_r, d_skip, tb):
    L = u.shape[0]
    nt = L // tb
    seg = tb // SUBLANES
    rcol = pl.BlockSpec((tb, LANES), lambda kb, t: (nt - 1 - t, kb))
    sp = _s5_specs(seg)
    aspec = pl.BlockSpec((1, SUBLANES, 2 * KB_STATES), lambda kb, t: (kb, 0, 0))

    def body(u_ref, ds_ref, cin_ref, b_ref, c_ref, tf_ref, pf_ref, tr_ref, pr_ref, d_ref,
             du_ref, db_ref, dc_ref, da_ref, dd_ref, up_scr, dp_scr, x_scr, xp_scr, g_scr, fc_scr, lc_scr):
        @pl.when(pl.program_id(1) == 0)
        def _():
            lc_scr[...] = jnp.zeros_like(lc_scr)
            db_ref[...] = jnp.zeros_like(db_ref)
            dc_ref[...] = jnp.zeros_like(dc_ref)
            da_ref[...] = jnp.zeros_like(da_ref)
            dd_ref[...] = jnp.zeros_like(dd_ref)

        _rows_to_segments(up_scr, u_ref, seg)
        _rows_to_segments(dp_scr, ds_ref, seg)
        uv, dsv = up_scr[...], dp_scr[...]
        ub, dsb = uv.astype(_BF), dsv.astype(_BF)
        fc_scr[...] = cin_ref[0, 0]
        x_scr[...] = _dot(ub, b_ref[0]).reshape(seg, SUBLANES, 2 * KB_STATES)
        _scan_segments(x_scr, tf_ref, pf_ref, fc_scr, seg, reverse=False, xprev_ref=xp_scr)
        g_scr[...] = _dot_nt(dsb, c_ref[0]).reshape(seg, SUBLANES, 2 * KB_STATES)
        _scan_segments(g_scr, tr_ref, pr_ref, lc_scr, seg, reverse=True, xprev_ref=xp_scr, da_ref=da_ref)
        lamb = g_scr[...].reshape(tb, 2 * KB_STATES).astype(_BF)
        _segments_to_rows(du_ref, _dot_nt(lamb, b_ref[0]) + d_ref[...] * dsv, seg)
        db_ref[0] += _dot_tn(ub, lamb)
        dc_ref[0] += _dot_tn(x_scr[...].reshape(tb, 2 * KB_STATES).astype(_BF), dsb)
        dd_ref[...] += jnp.sum(dsv * uv, axis=0, keepdims=True)

    state = pltpu.VMEM((seg, SUBLANES, 2 * KB_STATES), _F32)
    return pl.pallas_call(
        body, name="s5_bwd", grid=(N_KB, nt),
        in_specs=[rcol, rcol, pl.BlockSpec((1, 1, SUBLANES, 2 * KB_STATES), lambda kb, t: (kb, nt - 1 - t, 0, 0)),
                  sp["b"], sp["c"], sp["tab"], sp["pw"], sp["tab"], sp["pw"], sp["d"]],
        out_specs=[rcol, sp["b"], sp["c"], aspec, sp["d"]],
        out_shape=[jax.ShapeDtypeStruct((L, SSM_W), _F32),
                   jax.ShapeDtypeStruct((N_KB, LANES, 2 * KB_STATES), _F32),
                   jax.ShapeDtypeStruct((N_KB, 2 * KB_STATES, LANES), _F32),
                   jax.ShapeDtypeStruct((N_KB, SUBLANES, 2 * KB_STATES), _F32),
                   jax.ShapeDtypeStruct((1, SSM_W), _F32)],
        scratch_shapes=[pltpu.VMEM((tb, LANES), _F32)] * 2 + [state] * 3 + [pltpu.VMEM((SUBLANES, 2 * KB_STATES), _F32)] * 2,
        compiler_params=_params("parallel", "arbitrary"),
    )(u, ds, cin, bmat, cmat, tab_f, pw_f, tab_r, pw_r, d_skip)


def _retention_bwd(q, k, v, do, r_prev, consts, cosf, sinf):
    L = q.shape[0]
    nc = L // CHUNK
    blk = pl.BlockSpec((CHUNK, RET_W), lambda n: (nc - 1 - n, 0))
    rope_blk = pl.BlockSpec((CHUNK, HEAD_D), lambda n: (nc - 1 - n, 0))

    def body(q_ref, k_ref, v_ref, do_ref, rp_ref, dm_ref, xi_ref, zeta_ref, gc_ref, cos_ref, sin_ref,
             dq_ref, dk_ref, dv_ref, g_scr):
        @pl.when(pl.program_id(0) == 0)
        def _():
            g_scr[...] = jnp.zeros_like(g_scr)

        cs, sn = cos_ref[...], sin_ref[...]
        for hh in range(N_HEAD):
            cols = slice(hh * HEAD_D, (hh + 1) * HEAD_D)
            qv, kv, vv, dov = q_ref[:, cols], k_ref[:, cols], v_ref[:, cols], do_ref[:, cols]
            rb = rp_ref[hh, 0].astype(_BF)
            gst = g_scr[hh]
            gb = gst.astype(_BF)
            dm, zeta = dm_ref[hh], zeta_ref[hh]
            sb = (_dot_nt(qv, kv) * dm).astype(_BF)
            dab = (_dot_nt(dov, vv) * dm).astype(_BF)
            dox = (dov.astype(_F32) * xi_ref[hh]).astype(_BF)
            vz = (vv.astype(_F32) * zeta).astype(_BF)
            dq = _dot(dab, kv) + _dot_nt(dox, rb)
            dk = _dot_tn(dab, qv) + _dot_nt(vz, gb)
            dv = _dot_tn(sb, dov) + _dot(kv, gb) * zeta
            g_scr[hh] = gc_ref[hh, 0:1, :] * gst + _dot_tn(qv, dox)
            dq_ref[:, cols] = _rope_t(dq, cs, sn).astype(_BF)
            dk_ref[:, cols] = (_rope_t(dk, cs, sn) * (HEAD_D ** -0.5)).astype(_BF)
            dv_ref[:, cols] = dv.astype(_BF)

    return pl.pallas_call(
        body, name="retention_bwd", grid=(nc,),
        in_specs=[blk, blk, blk, blk, pl.BlockSpec((N_HEAD, 1, HEAD_D, HEAD_D), lambda n: (0, nc - 1 - n, 0, 0))]
        + _head_specs() + [rope_blk, rope_blk],
        out_specs=[blk, blk, blk],
        out_shape=[jax.ShapeDtypeStruct((L, RET_W), _BF)] * 3,
        scratch_shapes=[pltpu.VMEM((N_HEAD, HEAD_D, HEAD_D), _F32)],
        compiler_params=_params("arbitrary"),
    )(q, k, v, do, r_prev, *consts, cosf, sinf)


def _inproj_bwd(pieces, w_in, x, dx2, g1, tm):
    L = x.shape[0]

    def body(p0, p1, p2, p3, p4, w_ref, x_ref, dx2_ref, g_ref, dx_ref, dg_ref):
        @pl.when(pl.program_id(0) == 0)
        def _():
            dg_ref[...] = jnp.zeros_like(dg_ref)

        dh = None
        for j, p in enumerate((p0, p1, p2, p3, p4)):
            part = _dot_nt(p[...].astype(_BF), w_ref[:, j * RET_W:(j + 1) * RET_W])
            dh = part if dh is None else dh + part
        dz, dgr = _rms_bwd(x_ref[...], g_ref[...], dh)
        dx_ref[...] = dx2_ref[...] + dz
        dg_ref[...] += jnp.sum(dgr, axis=0, keepdims=True)

    return pl.pallas_call(
        body, name="inproj_bwd", grid=(L // tm,),
        in_specs=[_row_spec(tm, RET_W)] * 5 + [_full_spec((D_MODEL, IN_COLS)), _row_spec(tm, D_MODEL),
                                                 _row_spec(tm, D_MODEL), _full_spec((1, D_MODEL))],
        out_specs=[_row_spec(tm, D_MODEL), _full_spec((1, D_MODEL))],
        out_shape=[jax.ShapeDtypeStruct((L, D_MODEL), _F32), jax.ShapeDtypeStruct((1, D_MODEL), _F32)],
        compiler_params=_params("arbitrary"),
    )(*pieces, w_in, x, dx2, g1)


def _sum_adamw(parts, w, m, v, tr, name):
    _, R, Cc = parts.shape

    def body(p_ref, w_ref, m_ref, v_ref, g_ref, d_ref, nm_ref, nv_ref):
        gv = p_ref[0].astype(_F32)
        for s in range(1, N_DEV):
            gv = gv + p_ref[s].astype(_F32)
        g_ref[...] = gv
        nm = ADAM_B1 * m_ref[...] + (1.0 - ADAM_B1) * gv
        nv = ADAM_B2 * v_ref[...] + (1.0 - ADAM_B2) * (gv * gv)
        m_hat = nm / (1.0 - ADAM_B1 ** ADAM_STEP)
        v_hat = nv / (1.0 - ADAM_B2 ** ADAM_STEP)
        d_ref[...] = -ADAM_LR * (m_hat / (jnp.sqrt(v_hat) + ADAM_EPS) + ADAM_WD * w_ref[...])
        nm_ref[...] = nm
        nv_ref[...] = nv

    spec = _row_spec(tr, Cc)
    return pl.pallas_call(
        body, name=name, grid=(R // tr,),
        in_specs=[pl.BlockSpec((N_DEV, tr, Cc), lambda i: (0, i, 0))] + [spec] * 3, out_specs=[spec] * 4,
        out_shape=[jax.ShapeDtypeStruct((R, Cc), _F32)] * 4,
        compiler_params=_params("parallel"),
    )(parts, w, m, v)


def _my_place():
    return lax.axis_index("x"), lax.axis_index("y"), lax.axis_index("c")


def _all_gather(blocks):
    n = len(blocks)

    def body(*refs):
        x_refs, out_refs = refs[:n], refs[n:2 * n]
        send_sems, recv_sems, local_sems = refs[2 * n:]
        x, y, c = _my_place()
        me, sibling = (x, y, c), (x, y, 1 - c)
        chips = [(1 - x, y), (x, 1 - y), (1 - x, 1 - y)]

        def slot(a, px, py, pc):
            return out_refs[a].at[4 * px + 2 * py + pc]

        def copy(a, k, blk, to, own=False):
            return pltpu.make_async_remote_copy(
                src_ref=x_refs[a] if own else slot(a, *blk), dst_ref=slot(a, *blk),
                send_sem=send_sems.at[a, k], recv_sem=recv_sems.at[a, k], device_id=to, device_id_type=MESH)

        mine = [pltpu.make_async_copy(x_refs[a], slot(a, *me), local_sems.at[a]) for a in range(n)]
        for cp in mine:
            cp.start()
        first = []
        for a in range(n):
            first.append(copy(a, 0, me, sibling, own=True))
            first += [copy(a, 1 + j, me, (*chip, c), own=True) for j, chip in enumerate(chips)]
        for cp in first:
            cp.start()
        passed = []
        for j, chip in enumerate(chips):
            for a in range(n):
                copy(a, 1 + j, (*chip, c), me).wait_recv()
                fwd = copy(a, 4 + j, (*chip, c), sibling)
                fwd.start()
                passed.append(fwd)
        for a in range(n):
            copy(a, 0, sibling, me).wait_recv()
            for j, chip in enumerate(chips):
                copy(a, 4 + j, (*chip, 1 - c), me).wait_recv()
        for cp in first + passed:
            cp.wait_send()
        for cp in mine:
            cp.wait()

    any_spec = pl.BlockSpec(memory_space=pl.ANY)
    return pl.pallas_call(
        body, name="weights_all_gather",
        in_specs=[any_spec] * n, out_specs=[any_spec] * n,
        out_shape=[jax.ShapeDtypeStruct((N_DEV,) + b.shape, b.dtype) for b in blocks],
        scratch_shapes=[pltpu.SemaphoreType.DMA((n, 7)), pltpu.SemaphoreType.DMA((n, 7)), pltpu.SemaphoreType.DMA((n,))],
    )(*blocks)


def _exchange(bigs, small):
    n = len(bigs)
    r = small.shape[0]

    def body(*refs):
        in_refs, out_refs = refs[:n + 1], refs[n + 1:2 * n + 2]
        send_sems, recv_sems, local_sems = refs[2 * n + 2:]
        x, y, c = _my_place()
        me = 4 * x + 2 * y + c
        own = [pltpu.make_async_copy(in_refs[a].at[me], out_refs[a].at[me], local_sems.at[a]) for a in range(n)]
        own.append(pltpu.make_async_copy(in_refs[n], out_refs[n].at[me], local_sems.at[n]))
        for cp in own:
            cp.start()
        copies = []
        for kk in range(1, N_DEV):
            px, py, pc = x ^ (kk >> 2), y ^ ((kk >> 1) & 1), c ^ (kk & 1)
            peer = 4 * px + 2 * py + pc
            for a in range(n + 1):
                src = in_refs[a].at[peer] if a < n else in_refs[a]
                copies.append(pltpu.make_async_remote_copy(
                    src_ref=src, dst_ref=out_refs[a].at[me],
                    send_sem=send_sems.at[a, kk - 1], recv_sem=recv_sems.at[a, kk - 1],
                    device_id=(px, py, pc), device_id_type=MESH))
        for cp in copies:
            cp.start()
        for cp in copies:
            cp.wait_recv()
        for cp in copies:
            cp.wait_send()
        for cp in own:
            cp.wait()

    any_spec = pl.BlockSpec(memory_space=pl.ANY)
    outs = pl.pallas_call(
        body, name="grad_exchange",
        in_specs=[any_spec] * (n + 1), out_specs=[any_spec] * (n + 1),
        out_shape=[jax.ShapeDtypeStruct(b.shape, b.dtype) for b in bigs]
        + [jax.ShapeDtypeStruct((N_DEV, r, LANES), small.dtype)],
        scratch_shapes=[pltpu.SemaphoreType.DMA((n + 1, 7)), pltpu.SemaphoreType.DMA((n + 1, 7)),
                        pltpu.SemaphoreType.DMA((n + 1,))],
    )(*bigs, small)
    return outs[:n], outs[n]


HBM_SPEC = pl.BlockSpec(memory_space=pltpu.HBM)
SEM_SPEC = pl.BlockSpec(memory_space=pltpu.SEMAPHORE)
DATAFLOW = pltpu.SideEffectType.DATAFLOW_SIDE_EFFECTING


def _my_index():
    x, y, c = _my_place()
    return 4 * x + 2 * y + c


def _landing(own_block):
    zone = lax.empty((N_DEV,) + own_block.shape, own_block.dtype)
    return lax.dynamic_update_index_in_dim(zone, own_block, _my_index(), 0)


def _split_copies(src_refs, land_refs, send_sems, recv_sems, gather):
    x, y, c = _my_place()
    me = 4 * x + 2 * y + c
    copies = []
    for kk in range(1, N_DEV):
        px, py, pc = x ^ (kk >> 2), y ^ ((kk >> 1) & 1), c ^ (kk & 1)
        peer = 4 * px + 2 * py + pc
        for a, (src, land) in enumerate(zip(src_refs, land_refs)):
            copies.append(pltpu.make_async_remote_copy(
                src_ref=src if gather else src.at[peer], dst_ref=land.at[me],
                send_sem=send_sems.at[a * 7 + kk - 1], recv_sem=recv_sems.at[a * 7 + kk - 1],
                device_id=(px, py, pc), device_id_type=MESH))
    return copies


def _split_start(srcs, lands, gather, name):
    n = len(srcs)

    def body(*refs):
        src_refs, land_refs = refs[:n], refs[n:2 * n]
        send_sems, recv_sems = refs[2 * n], refs[2 * n + 1]
        token = refs[-1]
        for cp in _split_copies(src_refs, land_refs, send_sems, recv_sems, gather):
            cp.start()
        token[...] = jnp.zeros_like(token)

    outs = pl.pallas_call(
        body, name=name,
        out_shape=(pltpu.SemaphoreType.DMA((7 * n,)), pltpu.SemaphoreType.DMA((7 * n,)),
                   *[pltpu.HBM(t.shape, t.dtype) for t in srcs], *[pltpu.HBM(t.shape, t.dtype) for t in lands],
                   jax.ShapeDtypeStruct((SUBLANES, LANES), _F32)),
        in_specs=[HBM_SPEC] * (2 * n),
        out_specs=(SEM_SPEC, SEM_SPEC, *[HBM_SPEC] * (2 * n), pl.BlockSpec(memory_space=pltpu.VMEM)),
        input_output_aliases={i: 2 + i for i in range(2 * n)},
        compiler_params=pltpu.CompilerParams(has_side_effects=DATAFLOW),
    )(*[pltpu.with_memory_space_constraint(t, pltpu.HBM) for t in list(srcs) + list(lands)])
    return outs[0], outs[1], outs[2:2 + n], outs[2 + n:2 + 2 * n], outs[-1]


def _split_wait(send_sems, recv_sems, srcs, lands, after, gather, name):
    n = len(srcs)

    def body(*refs):
        src_refs, land_refs = refs[:n], refs[n:2 * n]
        send_s, recv_s = refs[2 * n], refs[2 * n + 1]
        for cp in _split_copies(src_refs, land_refs, send_s, recv_s, gather):
            cp.wait_send()
            cp.wait_recv()

    outs = pl.pallas_call(
        body, name=name,
        out_shape=tuple(pltpu.HBM(t.shape, t.dtype) for t in list(srcs) + list(lands)),
        in_specs=[HBM_SPEC] * (2 * n) + [SEM_SPEC, SEM_SPEC, pl.BlockSpec(memory_space=pl.ANY)],
        out_specs=tuple([HBM_SPEC] * (2 * n)),
        input_output_aliases={i: i for i in range(2 * n)},
        compiler_params=pltpu.CompilerParams(has_side_effects=DATAFLOW),
    )(*srcs, *lands, send_sems, recv_sems, after)
    return outs[n:]


def _discretize(lam_re, lam_im, log_dt, b_re, b_im):
    lr = jnp.minimum(lam_re, -1e-4)
    li = lam_im
    dt = jnp.exp(log_dt)[:, None]
    er = jnp.exp(lr * dt)
    ar, ai = er * jnp.cos(li * dt), er * jnp.sin(li * dt)
    den = lr * lr + li * li
    cr = ((ar - 1.0) * lr + ai * li) / den
    ci = (ai * lr - (ar - 1.0) * li) / den
    bbr = cr[:, :, None] * b_re - ci[:, :, None] * b_im
    bbi = cr[:, :, None] * b_im + ci[:, :, None] * b_re
    return ar, ai, bbr, bbi


def _cmul(ar, ai, br, bi):
    return ar * br - ai * bi, ar * bi + ai * br


def _cpowers(ar, ai, n):
    pr, pi = ar[None], ai[None]
    while pr.shape[0] < n:
        nr, ni = _cmul(pr, pi, pr[-1][None], pi[-1][None])
        pr, pi = jnp.concatenate([pr, nr]), jnp.concatenate([pi, ni])
    return pr[:n], pi[:n]


def _scan_tables(ar, ai, seg, reverse):
    if reverse:
        ai = -ai
    ar, ai = ar.reshape(N_KB, KB_STATES), ai.reshape(N_KB, KB_STATES)
    pr, pi = _cpowers(ar, ai, seg)
    a1 = (pr[-1], pi[-1])
    a2 = _cmul(*a1, *a1)
    a4 = _cmul(*a2, *a2)
    row = jnp.arange(SUBLANES)[None, :, None]
    wide = lambda t: jnp.broadcast_to(t[:, None, :], (N_KB, SUBLANES, KB_STATES))
    tabs = [wide(ar), wide(ai)]
    for dist, (qr, qi) in ((1, a1), (2, a2), (4, a4)):
        keep = (row < SUBLANES - dist) if reverse else (row >= dist)
        tabs += [jnp.where(keep, wide(qr), 0.0), jnp.where(keep, wide(qi), 0.0)]
    tabs += [wide(a1[0]), wide(a1[1])]
    if reverse:
        pr, pi = pr[::-1], pi[::-1]
    pw = jnp.transpose(jnp.concatenate([pr, pi], axis=-1), (1, 0, 2))[:, :, None, :]
    return jnp.stack(tabs, axis=1).astype(_F32), pw.astype(_F32)


def _block_diag_in(br, bi):
    eye = jnp.eye(GROUPS_PER_KB, dtype=_F32)
    one = lambda t: jnp.einsum("kgpc,gh->kgchp", t.reshape(N_KB, GROUPS_PER_KB, N_STATE, SSM_GC), eye).reshape(
        N_KB, LANES, KB_STATES)
    return jnp.concatenate([one(br), one(bi)], axis=-1)


def _block_diag_in_t(dmat):
    d6 = dmat.reshape(N_KB, GROUPS_PER_KB, SSM_GC, 2, GROUPS_PER_KB, N_STATE)
    eye = jnp.eye(GROUPS_PER_KB, dtype=_F32)
    both = jnp.einsum("kgcrhp,gh->rkgpc", d6, eye).reshape(2, N_GROUP, N_STATE, SSM_GC)
    return both[0], both[1]


def _block_diag_out(c_re, c_im):
    eye = jnp.eye(GROUPS_PER_KB, dtype=_F32)
    one = lambda t: jnp.einsum("kgcp,gh->khpgc", t.reshape(N_KB, GROUPS_PER_KB, SSM_GC, N_STATE), eye).reshape(
        N_KB, KB_STATES, LANES)
    return jnp.concatenate([one(c_re), -one(c_im)], axis=1)


def _block_diag_out_t(dmat):
    d6 = dmat.reshape(N_KB, 2, GROUPS_PER_KB, N_STATE, GROUPS_PER_KB, SSM_GC)
    eye = jnp.eye(GROUPS_PER_KB, dtype=_F32)
    both = jnp.einsum("krhpgc,gh->rkgcp", d6, eye).reshape(2, N_GROUP, SSM_GC, N_STATE)
    return both[0], -both[1]


SMALL_NAMES = ("norm_mix_pre", "norm_mix_post", "ret_gn_gain", "ssm_lambda_re", "ssm_lambda_im", "ssm_log_dt",
               "ssm_b_re", "ssm_b_im", "ssm_c_re", "ssm_c_im", "ssm_d", "norm_mlp_pre", "norm_mlp_post")


def _local_grads(x, tgt, small, w_in, late_weights, emit, tm, tk, tb):
    L = x.shape[0]
    g1, g2, ggn = small["norm_mix_pre"], small["norm_mix_post"], small["ret_gn_gain"]
    g3, g4, d_skip = small["norm_mlp_pre"], small["norm_mlp_post"], small["ssm_d"]

    half = HEAD_D // 2
    inv_freq = ROPE_BASE ** (-jnp.arange(half, dtype=_F32) / half)
    ang = jnp.arange(L, dtype=_F32)[:, None] * inv_freq[None, :]
    cosf = jnp.concatenate([jnp.cos(ang), jnp.cos(ang)], axis=-1)
    sinf = jnp.concatenate([-jnp.sin(ang), jnp.sin(ang)], axis=-1)
    consts = _ret_consts()

    disc_in = (small["ssm_lambda_re"][0], small["ssm_lambda_im"][0], small["ssm_log_dt"][0],
               small["ssm_b_re"][0], small["ssm_b_im"][0])
    (ar, ai, bbr, bbi), disc_vjp = jax.vjp(_discretize, *disc_in)
    bmat = _block_diag_in(bbr, bbi).astype(_BF)
    cmat = _block_diag_out(small["ssm_c_re"][0], small["ssm_c_im"][0]).astype(_BF)
    seg = tb // SUBLANES
    tab_f, pw_f = _scan_tables(ar, ai, seg, False)
    tab_r, pw_r = _scan_tables(ar, ai, seg, True)

    h1, q, k, v, gate, u = _inproj_fwd(x, g1, w_in, cosf, sinf, tm)
    o, y_ret, r_prev = _retention_fwd(q, k, v, gate, ggn, consts)
    s, cin = _s5_fwd(u, bmat, cmat, tab_f, pw_f, d_skip, tb)
    w_glu, w_out, w_ff1, w_ff2 = late_weights(s)
    ys, glu, cat, mix, x2 = _mixout_fwd(s, y_ret, x, w_glu, w_out, g2, tm)
    h3, f1 = _ff1_fwd(x2, g3, w_ff1, tm)
    dy, dm, dg4, sq = _ff2_loss(f1, x2, tgt, g4, w_ff2, tm)

    df1, dw_ff2 = _ff2_bwd(dm, f1, w_ff2, min(1024, L), 1024)
    dx2, dmix, dg3, dg2 = _ff1_bwd(df1, w_ff1, x2, mix, dy, g3, g2, tm)
    dw_ff1 = _matmul_tn(h3, df1, tk, FF1_COLS, "dw_ff1", slots=True)
    zero = emit({"w_ff1": dw_ff1, "w_ff2": dw_ff2})
    dglu, ds, dgate, do, dggn = _mixout_bwd(dmix, w_out, w_glu, glu, s, o, gate, ggn if zero is None else ggn + zero, tm)
    dw_out = _matmul_tn(cat, dmix, tk, 1024, "dw_out")
    dw_glu = _matmul_tn(ys, dglu, tk, 1024, "dw_glu")
    zero = emit({"w_glu": dw_glu, "w_out": dw_out})
    du, dbmat, dcmat, da8, dd = _s5_bwd(u, ds, cin, bmat, cmat, tab_f, pw_f, tab_r, pw_r,
                                        d_skip if zero is None else d_skip + zero, tb)
    dq, dk, dv = _retention_bwd(q, k, v, do, r_prev, consts, cosf, sinf)
    pieces = (dq, dk, dv, dgate, du)
    gx, dg1 = _inproj_bwd(pieces, w_in, x, dx2, g1, tm)
    dw_in = jnp.concatenate([_matmul_tn(h1, p, tk, RET_W, "dw_in_%d" % j) for j, p in enumerate(pieces)], axis=1)

    da = jnp.sum(da8, axis=1)
    dar = da[:, :KB_STATES].reshape(N_GROUP, N_STATE)
    dai = da[:, KB_STATES:].reshape(N_GROUP, N_STATE)
    dbr, dbi = _block_diag_in_t(dbmat)
    dlre, dlim, dldt, dbre, dbim = disc_vjp((dar, dai, dbr, dbi))
    dcre, dcim = _block_diag_out_t(dcmat)

    gsmall = {
        "norm_mix_pre": dg1, "norm_mix_post": dg2, "ret_gn_gain": dggn,
        "ssm_lambda_re": dlre[None], "ssm_lambda_im": dlim[None], "ssm_log_dt": dldt[None],
        "ssm_b_re": dbre[None], "ssm_b_im": dbim[None], "ssm_c_re": dcre[None], "ssm_c_im": dcim[None],
        "ssm_d": dd, "norm_mlp_pre": dg3, "norm_mlp_post": dg4,
    }
    return sq, gx, gsmall, dw_in


BIG_SHAPES = {"w_in": (D_MODEL, IN_COLS // N_DEV), "w_glu": (SSM_W, 2 * SSM_W // N_DEV), "w_out": (D_MODEL // N_DEV, D_MODEL),
              "w_ff1": (D_MODEL, FF1_COLS), "w_ff2": (D_FF // N_DEV, D_MODEL)}
BIG_NAMES = ("w_in", "w_glu", "w_out", "w_ff1", "w_ff2")


def _cols_from_slots(g):
    return jnp.transpose(g, (1, 0, 2)).reshape(g.shape[1], N_DEV * g.shape[2])


def _cols_to_slots(dw):
    r, cols = dw.shape
    return jnp.transpose(dw.reshape(r, N_DEV, cols // N_DEV), (1, 0, 2))


LATE_NAMES = ("w_glu", "w_out", "w_ff1", "w_ff2")


def _grad_slots(name, dw):
    if name in ("w_in", "w_glu"):
        return _cols_to_slots(dw)
    if name == "w_ff1":
        return dw
    return dw.reshape((N_DEV,) + BIG_SHAPES[name])


PIECE_ROWS = 8


def _small_layout(shapes):
    off, rows = {}, 0
    for n in SMALL_NAMES:
        off[n] = rows
        rows += -(-math.prod(shapes[n]) // (PIECE_ROWS * LANES)) * PIECE_ROWS
    return off, rows, rows + PIECE_ROWS


def _pack_small(vals, shapes, last=None):
    parts = []
    for n in SMALL_NAMES:
        flat = vals[n].reshape(-1).astype(_F32)
        pad = -flat.shape[0] % (PIECE_ROWS * LANES)
        if pad:
            flat = jnp.concatenate([flat, jnp.zeros((pad,), _F32)])
        parts.append(flat.reshape(-1, LANES))
    parts.append(jnp.zeros((PIECE_ROWS, LANES), _F32) if last is None else last)
    return jnp.concatenate(parts, axis=0)


def _unpack_small(buf, shapes):
    off, _, _ = _small_layout(shapes)
    out = {}
    for n in SMALL_NAMES:
        size = math.prod(shapes[n])
        rows = -(-size // LANES)
        out[n] = buf[off[n]:off[n] + rows].reshape(-1)[:size].reshape(shapes[n])
    return out


WEIGHT_NAMES = ('norm_mix_pre', 'norm_mix_post', 'w_in', 'ret_gn_gain', 'ssm_lambda_re', 'ssm_lambda_im', 'ssm_log_dt',
                'ssm_b_re', 'ssm_b_im', 'ssm_c_re', 'ssm_c_im', 'ssm_d', 'w_glu', 'w_out', 'norm_mlp_pre',
                'norm_mlp_post', 'w_ff1', 'w_ff2')


def kernel(x, norm_mix_pre, norm_mix_post, w_in, ret_gn_gain, ssm_lambda_re, ssm_lambda_im, ssm_log_dt, ssm_b_re, ssm_b_im, ssm_c_re, ssm_c_im, ssm_d, w_glu, w_out, norm_mlp_pre, norm_mlp_post, w_ff1, w_ff2, loss_target, m_norm_mix_pre, m_norm_mix_post, m_w_in, m_ret_gn_gain, m_ssm_lambda_re, m_ssm_lambda_im, m_ssm_log_dt, m_ssm_b_re, m_ssm_b_im, m_ssm_c_re, m_ssm_c_im, m_ssm_d, m_w_glu, m_w_out, m_norm_mlp_pre, m_norm_mlp_post, m_w_ff1, m_w_ff2, v_norm_mix_pre, v_norm_mix_post, v_w_in, v_ret_gn_gain, v_ssm_lambda_re, v_ssm_lambda_im, v_ssm_log_dt, v_ssm_b_re, v_ssm_b_im, v_ssm_c_re, v_ssm_c_im, v_ssm_d, v_w_glu, v_w_out, v_norm_mlp_pre, v_norm_mlp_post, v_w_ff1, v_w_ff2):
    args = dict(locals())
    w = {n: args[n] for n in WEIGHT_NAMES}
    m = {n: args["m_" + n] for n in WEIGHT_NAMES}
    v = {n: args["v_" + n] for n in WEIGHT_NAMES}
    L = x.shape[1]
    tm = min(256, L)
    tk = min(2048, L)
    tb = min(512, L)

    w_in_full = _cols_from_slots(_all_gather([w["w_in"][0].astype(_BF)])[0])
    late_blocks = [w[n][0].astype(_BF) for n in LATE_NAMES]
    late = _split_start(late_blocks, [_landing(b) for b in late_blocks], True, "late_weights_start")

    def late_weights(after):
        g = dict(zip(LATE_NAMES, _split_wait(*late[:4], after, True, "late_weights_wait")))
        return (_cols_from_slots(g["w_glu"]), g["w_out"].reshape(D_MODEL, D_MODEL), g["w_ff1"],
                g["w_ff2"].reshape(D_FF, D_MODEL))

    in_flight = []

    def emit(dws):
        names = sorted(dws)
        srcs = [_grad_slots(n, dws[n]) for n in names]
        lands = [_landing(lax.dynamic_index_in_dim(t, _my_index(), 0, keepdims=False)) for t in srcs]
        started = _split_start(srcs, lands, False, "grads_start_" + "_".join(names))
        in_flight.append((names, started))
        return started[4][0, 0]

    small_w = {n: w[n] for n in SMALL_NAMES}
    small_w["norm_mix_pre"] = small_w["norm_mix_pre"] + late[4][0, 0]
    sq, gx, gsmall, dw_in = _local_grads(x[0], loss_target[0], small_w, w_in_full, late_weights, emit, tm, tk, tb)

    shapes = {n: w[n].shape for n in SMALL_NAMES}
    loss_rows = jnp.broadcast_to(0.5 / D_MODEL * jnp.sum(sq), (PIECE_ROWS, LANES)).astype(_F32)
    (in_parts,), small_parts = _exchange([_grad_slots("w_in", dw_in)], _pack_small(gsmall, shapes, loss_rows))
    big_parts = {"w_in": in_parts}
    for names, started in in_flight:
        landed = _split_wait(*started[:4], small_parts, False, "grads_wait_" + "_".join(names))
        big_parts.update(zip(names, landed))

    grads, delta, new_m, new_v = {}, {}, {}, {}
    for n in BIG_NAMES:
        parts = big_parts[n]
        r = BIG_SHAPES[n][0]
        res = _sum_adamw(parts, w[n][0], m[n][0], v[n][0], min(256, r), "adamw_" + n)
        grads[n], delta[n], new_m[n], new_v[n] = (t[None] for t in res)
    sw, sm, sv = _pack_small(w, shapes), _pack_small(m, shapes), _pack_small(v, shapes)
    res = _sum_adamw(small_parts, sw, sm, sv, sw.shape[0], "adamw_small")
    for dst, buf in zip((grads, delta, new_m, new_v), res):
        dst.update(_unpack_small(buf, shapes))
    _, loss_at, _ = _small_layout(shapes)
    loss = res[0][loss_at, 0]

    return (loss, gx[None], *[grads[n] for n in WEIGHT_NAMES], *[delta[n] for n in WEIGHT_NAMES],
            *[new_m[n] for n in WEIGHT_NAMES], *[new_v[n] for n in WEIGHT_NAMES])
```

```python
import math

import jax
import jax.numpy as jnp
from jax import lax
from jax.experimental import pallas as pl
from jax.experimental.pallas import tpu as pltpu

_BF = jnp.bfloat16
_F32 = jnp.float32

D_MODEL = 1024
RET_W = 512
N_HEAD = 4
HEAD_D = 128
CHUNK = 128
SSM_W = 512
SSM_GC = 16
N_GROUP = 32
N_STATE = 64
GROUPS_PER_KB = 8
N_KB = 4
KB_STATES = GROUPS_PER_KB * N_STATE
D_FF = 4096
IN_COLS = 2560
NORM_EPS = 1e-6
ROPE_BASE = 10000.0
N_DEV = 8

ADAM_LR = 0.001
ADAM_B1 = 0.9
ADAM_B2 = 0.999
ADAM_EPS = 1e-08
ADAM_WD = 0.01
ADAM_STEP = 10

SUBLANES = 8
LANES = 128
VMEM_LIMIT = 52 * 1024 * 1024
SCAN_STRIP = 512

MESH = pl.DeviceIdType.MESH


def _params(*sem):
    return pltpu.CompilerParams(dimension_semantics=sem, vmem_limit_bytes=VMEM_LIMIT)


def _dot(a, b):
    return jnp.dot(a, b, preferred_element_type=_F32)


def _dot_nt(a, b):
    return lax.dot_general(a, b, (((1,), (1,)), ((), ())), preferred_element_type=_F32)


def _dot_tn(a, b):
    return lax.dot_general(a, b, (((0,), (0,)), ((), ())), preferred_element_type=_F32)


def _rms_r(z):
    return lax.rsqrt(jnp.mean(z * z, axis=-1, keepdims=True) + NORM_EPS)


def _rms_bwd(z, g, dn):
    r = _rms_r(z)
    t = dn * g
    dz = r * t - z * (r * r * r * jnp.mean(t * z, axis=-1, keepdims=True))
    return dz, dn * z * r


def _rope(t, cs, sn):
    return t * cs + pltpu.roll(t, HEAD_D // 2, 1) * sn


def _rope_t(t, cs, sn):
    return t * cs - pltpu.roll(t, HEAD_D // 2, 1) * sn


def _sigmoid(z):
    return 1.0 / (1.0 + jnp.exp(-z))


_GELU_C = math.sqrt(2.0 / math.pi)


def _gelu(z):
    return 0.5 * z * (1.0 + jnp.tanh(_GELU_C * (z + 0.044715 * z * z * z)))


def _gelu_grad(z):
    th = jnp.tanh(_GELU_C * (z + 0.044715 * z * z * z))
    return 0.5 * (1.0 + th) + 0.5 * z * (1.0 - th * th) * _GELU_C * (1.0 + 3 * 0.044715 * z * z)


def _row_spec(tm, n):
    return pl.BlockSpec((tm, n), lambda i: (i, 0))


def _full_spec(shape):
    nd = len(shape)
    return pl.BlockSpec(shape, lambda *_: (0,) * nd)


def _inproj_fwd(x, g1, w_in, cosf, sinf, tm):
    L = x.shape[0]

    def body(x_ref, g_ref, w_ref, cos_ref, sin_ref, h_ref, q_ref, k_ref, v_ref, gate_ref, u_ref):
        xv = x_ref[...]
        h = (xv * _rms_r(xv) * g_ref[...]).astype(_BF)
        h_ref[...] = h
        proj = _dot(h, w_ref[...])
        cs, sn = cos_ref[...], sin_ref[...]
        for hh in range(N_HEAD):
            lo = hh * HEAD_D
            q_ref[:, lo:lo + HEAD_D] = _rope(proj[:, lo:lo + HEAD_D], cs, sn).astype(_BF)
            kh = _rope(proj[:, RET_W + lo:RET_W + lo + HEAD_D], cs, sn) * (HEAD_D ** -0.5)
            k_ref[:, lo:lo + HEAD_D] = kh.astype(_BF)
        v_ref[...] = proj[:, 2 * RET_W:3 * RET_W].astype(_BF)
        gate_ref[...] = proj[:, 3 * RET_W:4 * RET_W]
        u_ref[...] = proj[:, 4 * RET_W:]

    return pl.pallas_call(
        body, name="inproj_fwd", grid=(L // tm,),
        in_specs=[_row_spec(tm, D_MODEL), _full_spec((1, D_MODEL)), _full_spec((D_MODEL, IN_COLS)),
                  _row_spec(tm, HEAD_D), _row_spec(tm, HEAD_D)],
        out_specs=[_row_spec(tm, D_MODEL)] + [_row_spec(tm, RET_W)] * 5,
        out_shape=[jax.ShapeDtypeStruct((L, D_MODEL), _BF)] + [jax.ShapeDtypeStruct((L, RET_W), _BF)] * 3
        + [jax.ShapeDtypeStruct((L, RET_W), _F32)] * 2,
        compiler_params=_params("parallel"),
    )(x, g1, w_in, cosf, sinf)


def _ret_consts():
    lg = jnp.log(1.0 - jnp.exp(jnp.linspace(math.log(1.0 / 32), math.log(1.0 / 512), N_HEAD))).astype(_F32)
    idx = jnp.arange(CHUNK, dtype=_F32)
    diff = idx[:, None] - idx[None, :]
    decay = jnp.where(diff[None] >= 0, jnp.exp(jnp.maximum(diff, 0.0)[None] * lg[:, None, None]), 0.0)
    zeta = jnp.exp((CHUNK - 1 - idx)[None, :] * lg[:, None])
    xi = jnp.exp((idx + 1.0)[None, :] * lg[:, None])
    gc = jnp.exp(CHUNK * lg)
    wide = lambda t: jnp.broadcast_to(t[:, :, None], (N_HEAD, CHUNK, HEAD_D)).astype(_F32)
    gcw = jnp.broadcast_to(gc[:, None, None], (N_HEAD, SUBLANES, HEAD_D)).astype(_F32)
    return decay.astype(_F32), wide(xi), wide(zeta), gcw


def _head_specs():
    c3 = _full_spec((N_HEAD, CHUNK, CHUNK))
    return [c3, c3, c3, _full_spec((N_HEAD, SUBLANES, HEAD_D))]


def _retention_fwd(q, k, v, gate, ggn, consts):
    L = q.shape[0]
    nc = L // CHUNK
    blk = pl.BlockSpec((CHUNK, RET_W), lambda n: (n, 0))

    def body(q_ref, k_ref, v_ref, gate_ref, ggn_ref, dm_ref, xi_ref, zeta_ref, gc_ref,
             o_ref, y_ref, rp_ref, r_scr):
        @pl.when(pl.program_id(0) == 0)
        def _():
            r_scr[...] = jnp.zeros_like(r_scr)

        for hh in range(N_HEAD):
            cols = slice(hh * HEAD_D, (hh + 1) * HEAD_D)
            qv, kv, vv = q_ref[:, cols], k_ref[:, cols], v_ref[:, cols]
            r_prev = r_scr[hh]
            s = _dot_nt(qv, kv) * dm_ref[hh]
            o = _dot(s.astype(_BF), vv) + _dot(qv, r_prev.astype(_BF)) * xi_ref[hh]
            o_ref[:, cols] = o
            rp_ref[hh, 0] = r_prev
            vz = (vv.astype(_F32) * zeta_ref[hh]).astype(_BF)
            r_scr[hh] = gc_ref[hh, 0:1, :] * r_prev + _dot_tn(kv, vz)
            dlt = o - jnp.mean(o, axis=-1, keepdims=True)
            on = dlt * lax.rsqrt(jnp.mean(dlt * dlt, axis=-1, keepdims=True) + NORM_EPS)
            gt = gate_ref[:, cols]
            y_ref[:, cols] = (gt * _sigmoid(gt) * (on * ggn_ref[:, cols])).astype(_BF)

    return pl.pallas_call(
        body, name="retention_fwd", grid=(nc,),
        in_specs=[blk, blk, blk, blk, _full_spec((1, RET_W))] + _head_specs(),
        out_specs=[blk, blk, pl.BlockSpec((N_HEAD, 1, HEAD_D, HEAD_D), lambda n: (0, n, 0, 0))],
        out_shape=[jax.ShapeDtypeStruct((L, RET_W), _F32), jax.ShapeDtypeStruct((L, RET_W), _BF),
                   jax.ShapeDtypeStruct((N_HEAD, nc, HEAD_D, HEAD_D), _F32)],
        scratch_shapes=[pltpu.VMEM((N_HEAD, HEAD_D, HEAD_D), _F32)],
        compiler_params=_params("arbitrary"),
    )(q, k, v, gate, ggn, *consts)


def _rows_to_segments(dst_ref, src_ref, seg):
    for r in range(seg):
        dst_ref[pl.ds(r * SUBLANES, SUBLANES), :] = src_ref[pl.ds(r, SUBLANES, stride=seg), :]


def _segments_to_rows(dst_ref, val, seg):
    for r in range(seg):
        dst_ref[pl.ds(r, SUBLANES, stride=seg), :] = val[r * SUBLANES:(r + 1) * SUBLANES, :]


def _scan_segments(x_ref, tab_ref, pw_ref, carry_ref, seg, reverse, xprev_ref=None, da_ref=None):
    W = SCAN_STRIP
    row_id = lax.broadcasted_iota(jnp.int32, (SUBLANES, W), 0)
    edge_in = (row_id == SUBLANES - 1) if reverse else (row_id == 0)
    edge_out = 0 if reverse else SUBLANES - 1
    for strip in range(KB_STATES // W):
        re = pl.ds(strip * W, W)
        im = pl.ds(KB_STATES + strip * W, W)
        ar, ai = tab_ref[0, 0, :, re], tab_ref[0, 1, :, re]

        def local(i, st, re=re, im=im, ar=ar, ai=ai):
            sr, si = st
            r = (seg - 1 - i) if reverse else i
            nr = ar * sr - ai * si + x_ref[r, :, re]
            ni = ar * si + ai * sr + x_ref[r, :, im]
            x_ref[r, :, re] = nr
            x_ref[r, :, im] = ni
            return nr, ni

        zero = jnp.zeros((SUBLANES, W), _F32)
        er, ei = lax.fori_loop(0, seg, local, (zero, zero))

        shift = (SUBLANES - 1) if reverse else 1
        fr = jnp.where(edge_in, carry_ref[:, re], pltpu.roll(er, shift, 0))
        fi = jnp.where(edge_in, carry_ref[:, im], pltpu.roll(ei, shift, 0))
        for j, dist in enumerate((1, 2, 4)):
            pr, pi = tab_ref[0, 2 + 2 * j, :, re], tab_ref[0, 3 + 2 * j, :, re]
            sh = (SUBLANES - dist) if reverse else dist
            sr, si = pltpu.roll(fr, sh, 0), pltpu.roll(fi, sh, 0)
            fr, fi = fr + pr * sr - pi * si, fi + pr * si + pi * sr
        br, bi = tab_ref[0, 8, :, re], tab_ref[0, 9, :, re]
        outr = br * fr - bi * fi + er
        outi = br * fi + bi * fr + ei
        carry_ref[:, re] = jnp.broadcast_to(outr[edge_out:edge_out + 1, :], (SUBLANES, W))
        carry_ref[:, im] = jnp.broadcast_to(outi[edge_out:edge_out + 1, :], (SUBLANES, W))

        keep_prev = xprev_ref is not None and not reverse
        add_da = da_ref is not None

        def fix(r, st, re=re, im=im, fr=fr, fi=fi):
            pwr, pwi = pw_ref[0, r, :, re], pw_ref[0, r, :, im]
            xr = x_ref[r, :, re] + (pwr * fr - pwi * fi)
            xi = x_ref[r, :, im] + (pwr * fi + pwi * fr)
            x_ref[r, :, re] = xr
            x_ref[r, :, im] = xi
            if keep_prev:
                xprev_ref[r, :, re] = st[0]
                xprev_ref[r, :, im] = st[1]
                return xr, xi
            if add_da:
                xpr, xpi = xprev_ref[r, :, re], xprev_ref[r, :, im]
                return st[0] + (xr * xpr + xi * xpi), st[1] + (xi * xpr - xr * xpi)
            return st

        st = lax.fori_loop(0, seg, fix, (fr, fi) if keep_prev else (zero, zero))
        if add_da:
            da_ref[0, :, re] += st[0]
            da_ref[0, :, im] += st[1]


def _s5_specs(seg):
    return dict(
        b=pl.BlockSpec((1, LANES, 2 * KB_STATES), lambda kb, t: (kb, 0, 0)),
        c=pl.BlockSpec((1, 2 * KB_STATES, LANES), lambda kb, t: (kb, 0, 0)),
        tab=pl.BlockSpec((1, 10, SUBLANES, KB_STATES), lambda kb, t: (kb, 0, 0, 0)),
        pw=pl.BlockSpec((1, seg, 1, 2 * KB_STATES), lambda kb, t: (kb, 0, 0, 0)),
        d=pl.BlockSpec((1, LANES), lambda kb, t: (0, kb)),
    )


def _s5_fwd(u, bmat, cmat, tab_f, pw_f, d_skip, tb):
    L = u.shape[0]
    nt = L // tb
    seg = tb // SUBLANES
    ucol = pl.BlockSpec((tb, LANES), lambda kb, t: (t, kb))
    sp = _s5_specs(seg)

    def body(u_ref, b_ref, c_ref, tab_ref, pw_ref, d_ref, s_ref, cin_ref, up_scr, x_scr, carry_scr):
        @pl.when(pl.program_id(1) == 0)
        def _():
            carry_scr[...] = jnp.zeros_like(carry_scr)

        cin_ref[0, 0] = carry_scr[...]
        _rows_to_segments(up_scr, u_ref, seg)
        up = up_scr[...]
        x_scr[...] = _dot(up.astype(_BF), b_ref[0]).reshape(seg, SUBLANES, 2 * KB_STATES)
        _scan_segments(x_scr, tab_ref, pw_ref, carry_scr, seg, reverse=False)
        y = _dot(x_scr[...].reshape(tb, 2 * KB_STATES).astype(_BF), c_ref[0]) + d_ref[...] * up
        _segments_to_rows(s_ref, y, seg)

    return pl.pallas_call(
        body, name="s5_fwd", grid=(N_KB, nt),
        in_specs=[ucol, sp["b"], sp["c"], sp["tab"], sp["pw"], sp["d"]],
        out_specs=[ucol, pl.BlockSpec((1, 1, SUBLANES, 2 * KB_STATES), lambda kb, t: (kb, t, 0, 0))],
        out_shape=[jax.ShapeDtypeStruct((L, SSM_W), _F32),
                   jax.ShapeDtypeStruct((N_KB, nt, SUBLANES, 2 * KB_STATES), _F32)],
        scratch_shapes=[pltpu.VMEM((tb, LANES), _F32), pltpu.VMEM((seg, SUBLANES, 2 * KB_STATES), _F32),
                        pltpu.VMEM((SUBLANES, 2 * KB_STATES), _F32)],
        compiler_params=_params("parallel", "arbitrary"),
    )(u, bmat, cmat, tab_f, pw_f, d_skip)


def _mixout_fwd(s, y_ret, x, w_glu, w_out, g2, tm):
    L = s.shape[0]

    def body(s_ref, yr_ref, x_ref, wg_ref, wo_ref, g_ref, ys_ref, glu_ref, cat_ref, mix_ref, x2_ref):
        ys = _gelu(s_ref[...]).astype(_BF)
        ys_ref[...] = ys
        glu = _dot(ys, wg_ref[...])
        glu_ref[...] = glu
        cat_ref[:, :RET_W] = yr_ref[...]
        cat_ref[:, RET_W:] = (glu[:, :SSM_W] * _sigmoid(glu[:, SSM_W:])).astype(_BF)
        mix = _dot(cat_ref[...], wo_ref[...])
        mix_ref[...] = mix
        x2_ref[...] = x_ref[...] + mix * _rms_r(mix) * g_ref[...]

    return pl.pallas_call(
        body, name="mixout_fwd", grid=(L // tm,),
        in_specs=[_row_spec(tm, SSM_W), _row_spec(tm, RET_W), _row_spec(tm, D_MODEL),
                  _full_spec((SSM_W, 2 * SSM_W)), _full_spec((D_MODEL, D_MODEL)), _full_spec((1, D_MODEL))],
        out_specs=[_row_spec(tm, SSM_W), _row_spec(tm, 2 * SSM_W), _row_spec(tm, D_MODEL),
                   _row_spec(tm, D_MODEL), _row_spec(tm, D_MODEL)],
        out_shape=[jax.ShapeDtypeStruct((L, SSM_W), _BF), jax.ShapeDtypeStruct((L, 2 * SSM_W), _F32),
                   jax.ShapeDtypeStruct((L, D_MODEL), _BF), jax.ShapeDtypeStruct((L, D_MODEL), _F32),
                   jax.ShapeDtypeStruct((L, D_MODEL), _F32)],
        compiler_params=_params("parallel"),
    )(s, y_ret, x, w_glu, w_out, g2)


FF1_COLS = D_FF // N_DEV


def _ff1_fwd(x2, g3, w1, tm):
    L = x2.shape[0]

    def body(x_ref, g_ref, w_ref, h_ref, f_ref):
        xv = x_ref[...]
        h = (xv * _rms_r(xv) * g_ref[...]).astype(_BF)
        h_ref[...] = h
        for j in range(N_DEV):
            f_ref[:, j * FF1_COLS:(j + 1) * FF1_COLS] = _dot(h, w_ref[j])

    return pl.pallas_call(
        body, name="ff1_fwd", grid=(L // tm,),
        in_specs=[_row_spec(tm, D_MODEL), _full_spec((1, D_MODEL)), _full_spec((N_DEV, D_MODEL, FF1_COLS))],
        out_specs=[_row_spec(tm, D_MODEL), _row_spec(tm, D_FF)],
        out_shape=[jax.ShapeDtypeStruct((L, D_MODEL), _BF), jax.ShapeDtypeStruct((L, D_FF), _F32)],
        compiler_params=_params("parallel"),
    )(x2, g3, w1)


def _ff2_loss(f1, x2, tgt, g4, w2, tm):
    L = f1.shape[0]

    def body(f_ref, x_ref, t_ref, g_ref, w_ref, dy_ref, dm_ref, dg_ref, ls_ref):
        @pl.when(pl.program_id(0) == 0)
        def _():
            dg_ref[...] = jnp.zeros_like(dg_ref)
            ls_ref[...] = jnp.zeros_like(ls_ref)

        rl = jnp.maximum(f_ref[...], 0.0)
        m = _dot((rl * rl).astype(_BF), w_ref[...])
        g = g_ref[...]
        y = x_ref[...] + m * _rms_r(m) * g
        err = y - t_ref[...]
        ls_ref[...] += jnp.sum(err * err, axis=0, keepdims=True)
        dy = err * (1.0 / D_MODEL)
        dy_ref[...] = dy
        dm, dgr = _rms_bwd(m, g, dy)
        dm_ref[...] = dm.astype(_BF)
        dg_ref[...] += jnp.sum(dgr, axis=0, keepdims=True)

    return pl.pallas_call(
        body, name="ff2_loss", grid=(L // tm,),
        in_specs=[_row_spec(tm, D_FF), _row_spec(tm, D_MODEL), _row_spec(tm, D_MODEL),
                  _full_spec((1, D_MODEL)), _full_spec((D_FF, D_MODEL))],
        out_specs=[_row_spec(tm, D_MODEL), _row_spec(tm, D_MODEL), _full_spec((1, D_MODEL)), _full_spec((1, D_MODEL))],
        out_shape=[jax.ShapeDtypeStruct((L, D_MODEL), _F32), jax.ShapeDtypeStruct((L, D_MODEL), _BF),
                   jax.ShapeDtypeStruct((1, D_MODEL), _F32), jax.ShapeDtypeStruct((1, D_MODEL), _F32)],
        compiler_params=_params("arbitrary"),
    )(f1, x2, tgt, g4, w2)


def _ff2_bwd(dm, f1, w2, tm, tn):
    L = dm.shape[0]
    last = L // tm - 1

    def body(dm_ref, f_ref, w_ref, df_ref, dw_ref, acc):
        @pl.when(pl.program_id(1) == 0)
        def _():
            acc[...] = jnp.zeros_like(acc)

        dmv = dm_ref[...]
        rl = jnp.maximum(f_ref[...], 0.0)
        df_ref[...] = (_dot_nt(dmv, w_ref[...]) * (2.0 * rl)).astype(_BF)
        acc[...] += _dot_tn((rl * rl).astype(_BF), dmv)

        @pl.when(pl.program_id(1) == last)
        def _():
            dw_ref[...] = acc[...].astype(_BF)

    return pl.pallas_call(
        body, name="ff2_bwd", grid=(D_FF // tn, L // tm),
        in_specs=[pl.BlockSpec((tm, D_MODEL), lambda j, i: (i, 0)), pl.BlockSpec((tm, tn), lambda j, i: (i, j)),
                  pl.BlockSpec((tn, D_MODEL), lambda j, i: (j, 0))],
        out_specs=[pl.BlockSpec((tm, tn), lambda j, i: (i, j)), pl.BlockSpec((tn, D_MODEL), lambda j, i: (j, 0))],
        out_shape=[jax.ShapeDtypeStruct((L, D_FF), _BF), jax.ShapeDtypeStruct((D_FF, D_MODEL), _BF)],
        scratch_shapes=[pltpu.VMEM((tn, D_MODEL), _F32)],
        compiler_params=_params("parallel", "arbitrary"),
    )(dm, f1, w2)


def _ff1_bwd(df1, w1, x2, mix, dy, g3, g2, tm):
    L = df1.shape[0]

    def body(df_ref, w_ref, x2_ref, mix_ref, dy_ref, g3_ref, g2_ref, dx2_ref, dmix_ref, dg3_ref, dg2_ref):
        @pl.when(pl.program_id(0) == 0)
        def _():
            dg3_ref[...] = jnp.zeros_like(dg3_ref)
            dg2_ref[...] = jnp.zeros_like(dg2_ref)

        dh = _dot_nt(df_ref[:, 0:FF1_COLS], w_ref[0])
        for j in range(1, N_DEV):
            dh = dh + _dot_nt(df_ref[:, j * FF1_COLS:(j + 1) * FF1_COLS], w_ref[j])
        dz, dgr = _rms_bwd(x2_ref[...], g3_ref[...], dh)
        dg3_ref[...] += jnp.sum(dgr, axis=0, keepdims=True)
        dx2 = dy_ref[...] + dz
        dx2_ref[...] = dx2
        dmx, dgr2 = _rms_bwd(mix_ref[...], g2_ref[...], dx2)
        dg2_ref[...] += jnp.sum(dgr2, axis=0, keepdims=True)
        dmix_ref[...] = dmx.astype(_BF)

    vec = _full_spec((1, D_MODEL))
    return pl.pallas_call(
        body, name="ff1_bwd", grid=(L // tm,),
        in_specs=[_row_spec(tm, D_FF), _full_spec((N_DEV, D_MODEL, FF1_COLS)), _row_spec(tm, D_MODEL),
                  _row_spec(tm, D_MODEL), _row_spec(tm, D_MODEL), vec, vec],
        out_specs=[_row_spec(tm, D_MODEL), _row_spec(tm, D_MODEL), vec, vec],
        out_shape=[jax.ShapeDtypeStruct((L, D_MODEL), _F32), jax.ShapeDtypeStruct((L, D_MODEL), _BF),
                   jax.ShapeDtypeStruct((1, D_MODEL), _F32), jax.ShapeDtypeStruct((1, D_MODEL), _F32)],
        compiler_params=_params("arbitrary"),
    )(df1, w1, x2, mix, dy, g3, g2)


def _matmul_tn(a, b, tm, tn, name, slots=False):
    L, K = a.shape
    N = b.shape[1]
    last = L // tm - 1

    def body(a_ref, b_ref, o_ref, acc):
        @pl.when(pl.program_id(1) == 0)
        def _():
            acc[...] = jnp.zeros_like(acc)

        acc[...] += _dot_tn(a_ref[...].astype(_BF), b_ref[...].astype(_BF))

        @pl.when(pl.program_id(1) == last)
        def _():
            if slots:
                o_ref[0] = acc[...].astype(_BF)
            else:
                o_ref[...] = acc[...].astype(_BF)

    if slots:
        out_spec = pl.BlockSpec((1, K, tn), lambda j, i: (j, 0, 0))
        out_shape = jax.ShapeDtypeStruct((N // tn, K, tn), _BF)
    else:
        out_spec = pl.BlockSpec((K, tn), lambda j, i: (0, j))
        out_shape = jax.ShapeDtypeStruct((K, N), _BF)
    return pl.pallas_call(
        body, name=name, grid=(N // tn, L // tm),
        in_specs=[pl.BlockSpec((tm, K), lambda j, i: (i, 0)), pl.BlockSpec((tm, tn), lambda j, i: (i, j))],
        out_specs=out_spec, out_shape=out_shape,
        scratch_shapes=[pltpu.VMEM((K, tn), _F32)],
        compiler_params=_params("parallel", "arbitrary"),
    )(a, b)


def _mixout_bwd(dmix, w_out, w_glu, glu, s, o, gate, ggn, tm):
    L = dmix.shape[0]

    def body(dmix_ref, wo_ref, wg_ref, glu_ref, s_ref, o_ref, gate_ref, ggn_ref,
             dglu_ref, ds_ref, dgate_ref, do_ref, dggn_ref):
        @pl.when(pl.program_id(0) == 0)
        def _():
            dggn_ref[...] = jnp.zeros_like(dggn_ref)

        dcat = _dot_nt(dmix_ref[...], wo_ref[...])
        dy_ret, dy_ssm = dcat[:, :RET_W], dcat[:, RET_W:]
        glu = glu_ref[...]
        ga, sg = glu[:, :SSM_W], _sigmoid(glu[:, SSM_W:])
        dga = (dy_ssm * sg).astype(_BF)
        dgb = (dy_ssm * ga * sg * (1.0 - sg)).astype(_BF)
        dglu_ref[:, :SSM_W] = dga
        dglu_ref[:, SSM_W:] = dgb
        dys = _dot_nt(dga, wg_ref[:, :SSM_W]) + _dot_nt(dgb, wg_ref[:, SSM_W:])
        ds_ref[...] = dys * _gelu_grad(s_ref[...])
        gt = gate_ref[...]
        sgt = _sigmoid(gt)
        ggn = ggn_ref[...]
        for hh in range(N_HEAD):
            cols = slice(hh * HEAD_D, (hh + 1) * HEAD_D)
            ov = o_ref[:, cols]
            dlt = ov - jnp.mean(ov, axis=-1, keepdims=True)
            rstd = lax.rsqrt(jnp.mean(dlt * dlt, axis=-1, keepdims=True) + NORM_EPS)
            on = dlt * rstd
            dyr = dy_ret[:, cols] * (gt[:, cols] * sgt[:, cols])
            dgate_ref[:, cols] = dy_ret[:, cols] * (on * ggn[:, cols]) * (sgt[:, cols] * (1.0 + gt[:, cols] * (1.0 - sgt[:, cols])))
            dggn_ref[:, cols] += jnp.sum(dyr * on, axis=0, keepdims=True)
            don = dyr * ggn[:, cols]
            do = rstd * (don - jnp.mean(don, axis=-1, keepdims=True) - on * jnp.mean(don * on, axis=-1, keepdims=True))
            do_ref[:, cols] = do.astype(_BF)

    return pl.pallas_call(
        body, name="mixout_bwd", grid=(L // tm,),
        in_specs=[_row_spec(tm, D_MODEL), _full_spec((D_MODEL, D_MODEL)), _full_spec((SSM_W, 2 * SSM_W)),
                  _row_spec(tm, 2 * SSM_W), _row_spec(tm, SSM_W), _row_spec(tm, RET_W), _row_spec(tm, RET_W),
                  _full_spec((1, RET_W))],
        out_specs=[_row_spec(tm, 2 * SSM_W), _row_spec(tm, SSM_W), _row_spec(tm, RET_W), _row_spec(tm, RET_W),
                   _full_spec((1, RET_W))],
        out_shape=[jax.ShapeDtypeStruct((L, 2 * SSM_W), _BF), jax.ShapeDtypeStruct((L, SSM_W), _F32),
                   jax.ShapeDtypeStruct((L, RET_W), _F32), jax.ShapeDtypeStruct((L, RET_W), _BF),
                   jax.ShapeDtypeStruct((1, RET_W), _F32)],
        compiler_params=_params("arbitrary"),
    )(dmix, w_out, w_glu, glu, s, o, gate, ggn)


def _s5_bwd(u, ds, cin, bmat, cmat, tab_f, pw_f, tab_r, pw_r, d_skip, tb):
    L = u.shape[0]
    nt = L // tb
    seg = tb // SUBLANES
    rcol = pl.BlockSpec((tb, LANES), lambda kb, t: (nt - 1 - t, kb))
    sp = _s5_specs(seg)
    aspec = pl.BlockSpec((1, SUBLANES, 2 * KB_STATES), lambda kb, t: (kb, 0, 0))

    def body(u_ref, ds_ref, cin_ref, b_ref, c_ref, tf_ref, pf_ref, tr_ref, pr_ref, d_ref,
             du_ref, db_ref, dc_ref, da_ref, dd_ref, up_scr, dp_scr, x_scr, xp_scr, g_scr, fc_scr, lc_scr):
        @pl.when(pl.program_id(1) == 0)
        def _():
            lc_scr[...] = jnp.zeros_like(lc_scr)
            db_ref[...] = jnp.zeros_like(db_ref)
            dc_ref[...] = jnp.zeros_like(dc_ref)
            da_ref[...] = jnp.zeros_like(da_ref)
            dd_ref[...] = jnp.zeros_like(dd_ref)

        _rows_to_segments(up_scr, u_ref, seg)
        _rows_to_segments(dp_scr, ds_ref, seg)
        uv, dsv = up_scr[...], dp_scr[...]
        ub, dsb = uv.astype(_BF), dsv.astype(_BF)
        fc_scr[...] = cin_ref[0, 0]
        x_scr[...] = _dot(ub, b_ref[0]).reshape(seg, SUBLANES, 2 * KB_STATES)
        _scan_segments(x_scr, tf_ref, pf_ref, fc_scr, seg, reverse=False, xprev_ref=xp_scr)
        g_scr[...] = _dot_nt(dsb, c_ref[0]).reshape(seg, SUBLANES, 2 * KB_STATES)
        _scan_segments(g_scr, tr_ref, pr_ref, lc_scr, seg, reverse=True, xprev_ref=xp_scr, da_ref=da_ref)
        lamb = g_scr[...].reshape(tb, 2 * KB_STATES).astype(_BF)
        _segments_to_rows(du_ref, _dot_nt(lamb, b_ref[0]) + d_ref[...] * dsv, seg)
        db_ref[0] += _dot_tn(ub, lamb)
        dc_ref[0] += _dot_tn(x_scr[...].reshape(tb, 2 * KB_STATES).astype(_BF), dsb)
        dd_ref[...] += jnp.sum(dsv * uv, axis=0, keepdims=True)

    state = pltpu.VMEM((seg, SUBLANES, 2 * KB_STATES), _F32)
    return pl.pallas_call(
        body, name="s5_bwd", grid=(N_KB, nt),
        in_specs=[rcol, rcol, pl.BlockSpec((1, 1, SUBLANES, 2 * KB_STATES), lambda kb, t: (kb, nt - 1 - t, 0, 0)),
                  sp["b"], sp["c"], sp["tab"], sp["pw"], sp["tab"], sp["pw"], sp["d"]],
        out_specs=[rcol, sp["b"], sp["c"], aspec, sp["d"]],
        out_shape=[jax.ShapeDtypeStruct((L, SSM_W), _F32),
                   jax.ShapeDtypeStruct((N_KB, LANES, 2 * KB_STATES), _F32),
                   jax.ShapeDtypeStruct((N_KB, 2 * KB_STATES, LANES), _F32),
                   jax.ShapeDtypeStruct((N_KB, SUBLANES, 2 * KB_STATES), _F32),
                   jax.ShapeDtypeStruct((1, SSM_W), _F32)],
        scratch_shapes=[pltpu.VMEM((tb, LANES), _F32)] * 2 + [state] * 3 + [pltpu.VMEM((SUBLANES, 2 * KB_STATES), _F32)] * 2,
        compiler_params=_params("parallel", "arbitrary"),
    )(u, ds, cin, bmat, cmat, tab_f, pw_f, tab_r, pw_r, d_skip)


def _retention_bwd(q, k, v, do, r_prev, consts, cosf, sinf):
    L = q.shape[0]
    nc = L // CHUNK
    blk = pl.BlockSpec((CHUNK, RET_W), lambda n: (nc - 1 - n, 0))
    rope_blk = pl.BlockSpec((CHUNK, HEAD_D), lambda n: (nc - 1 - n, 0))

    def body(q_ref, k_ref, v_ref, do_ref, rp_ref, dm_ref, xi_ref, zeta_ref, gc_ref, cos_ref, sin_ref,
             dq_ref, dk_ref, dv_ref, g_scr):
        @pl.when(pl.program_id(0) == 0)
        def _():
            g_scr[...] = jnp.zeros_like(g_scr)

        cs, sn = cos_ref[...], sin_ref[...]
        for hh in range(N_HEAD):
            cols = slice(hh * HEAD_D, (hh + 1) * HEAD_D)
            qv, kv, vv, dov = q_ref[:, cols], k_ref[:, cols], v_ref[:, cols], do_ref[:, cols]
            rb = rp_ref[hh, 0].astype(_BF)
            gst = g_scr[hh]
            gb = gst.astype(_BF)
            dm, zeta = dm_ref[hh], zeta_ref[hh]
            sb = (_dot_nt(qv, kv) * dm).astype(_BF)
            dab = (_dot_nt(dov, vv) * dm).astype(_BF)
            dox = (dov.astype(_F32) * xi_ref[hh]).astype(_BF)
            vz = (vv.astype(_F32) * zeta).astype(_BF)
            dq = _dot(dab, kv) + _dot_nt(dox, rb)
            dk = _dot_tn(dab, qv) + _dot_nt(vz, gb)
            dv = _dot_tn(sb, dov) + _dot(kv, gb) * zeta
            g_scr[hh] = gc_ref[hh, 0:1, :] * gst + _dot_tn(qv, dox)
            dq_ref[:, cols] = _rope_t(dq, cs, sn).astype(_BF)
            dk_ref[:, cols] = (_rope_t(dk, cs, sn) * (HEAD_D ** -0.5)).astype(_BF)
            dv_ref[:, cols] = dv.astype(_BF)

    return pl.pallas_call(
        body, name="retention_bwd", grid=(nc,),
        in_specs=[blk, blk, blk, blk, pl.BlockSpec((N_HEAD, 1, HEAD_D, HEAD_D), lambda n: (0, nc - 1 - n, 0, 0))]
        + _head_specs() + [rope_blk, rope_blk],
        out_specs=[blk, blk, blk],
        out_shape=[jax.ShapeDtypeStruct((L, RET_W), _BF)] * 3,
        scratch_shapes=[pltpu.VMEM((N_HEAD, HEAD_D, HEAD_D), _F32)],
        compiler_params=_params("arbitrary"),
    )(q, k, v, do, r_prev, *consts, cosf, sinf)


def _inproj_bwd(pieces, w_in, x, dx2, g1, tm):
    L = x.shape[0]

    def body(p0, p1, p2, p3, p4, w_ref, x_ref, dx2_ref, g_ref, dx_ref, dg_ref):
        @pl.when(pl.program_id(0) == 0)
        def _():
            dg_ref[...] = jnp.zeros_like(dg_ref)

        dh = None
        for j, p in enumerate((p0, p1, p2, p3, p4)):
            part = _dot_nt(p[...].astype(_BF), w_ref[:, j * RET_W:(j + 1) * RET_W])
            dh = part if dh is None else dh + part
        dz, dgr = _rms_bwd(x_ref[...], g_ref[...], dh)
        dx_ref[...] = dx2_ref[...] + dz
        dg_ref[...] += jnp.sum(dgr, axis=0, keepdims=True)

    return pl.pallas_call(
        body, name="inproj_bwd", grid=(L // tm,),
        in_specs=[_row_spec(tm, RET_W)] * 5 + [_full_spec((D_MODEL, IN_COLS)), _row_spec(tm, D_MODEL),
                                                 _row_spec(tm, D_MODEL), _full_spec((1, D_MODEL))],
        out_specs=[_row_spec(tm, D_MODEL), _full_spec((1, D_MODEL))],
        out_shape=[jax.ShapeDtypeStruct((L, D_MODEL), _F32), jax.ShapeDtypeStruct((1, D_MODEL), _F32)],
        compiler_params=_params("arbitrary"),
    )(*pieces, w_in, x, dx2, g1)


def _sum_adamw(parts, w, m, v, tr, name):
    _, R, Cc = parts.shape

    def body(p_ref, w_ref, m_ref, v_ref, g_ref, d_ref, nm_ref, nv_ref):
        gv = p_ref[0].astype(_F32)
        for s in range(1, N_DEV):
            gv = gv + p_ref[s].astype(_F32)
        g_ref[...] = gv
        nm = ADAM_B1 * m_ref[...] + (1.0 - ADAM_B1) * gv
        nv = ADAM_B2 * v_ref[...] + (1.0 - ADAM_B2) * (gv * gv)
        m_hat = nm / (1.0 - ADAM_B1 ** ADAM_STEP)
        v_hat = nv / (1.0 - ADAM_B2 ** ADAM_STEP)
        d_ref[...] = -ADAM_LR * (m_hat / (jnp.sqrt(v_hat) + ADAM_EPS) + ADAM_WD * w_ref[...])
        nm_ref[...] = nm
        nv_ref[...] = nv

    spec = _row_spec(tr, Cc)
    return pl.pallas_call(
        body, name=name, grid=(R // tr,),
        in_specs=[pl.BlockSpec((N_DEV, tr, Cc), lambda i: (0, i, 0))] + [spec] * 3, out_specs=[spec] * 4,
        out_shape=[jax.ShapeDtypeStruct((R, Cc), _F32)] * 4,
        compiler_params=_params("parallel"),
    )(parts, w, m, v)


def _my_place():
    return lax.axis_index("x"), lax.axis_index("y"), lax.axis_index("c")


def _all_gather(blocks):
    n = len(blocks)

    def body(*refs):
        x_refs, out_refs, done_ref = refs[:n], refs[n:2 * n], refs[2 * n]
        send_sems, recv_sems, local_sems = refs[2 * n + 1:]
        done_ref[...] = jnp.zeros_like(done_ref)
        x, y, c = _my_place()
        me, sibling = (x, y, c), (x, y, 1 - c)
        chips = [(1 - x, y), (x, 1 - y), (1 - x, 1 - y)]

        def slot(a, px, py, pc):
            return out_refs[a].at[4 * px + 2 * py + pc]

        def copy(a, k, blk, to, own=False):
            return pltpu.make_async_remote_copy(
                src_ref=x_refs[a] if own else slot(a, *blk), dst_ref=slot(a, *blk),
                send_sem=send_sems.at[a, k], recv_sem=recv_sems.at[a, k], device_id=to, device_id_type=MESH)

        mine = [pltpu.make_async_copy(x_refs[a], slot(a, *me), local_sems.at[a]) for a in range(n)]
        for cp in mine:
            cp.start()
        first = []
        for a in range(n):
            first.append(copy(a, 0, me, sibling, own=True))
            first += [copy(a, 1 + j, me, (*chip, c), own=True) for j, chip in enumerate(chips)]
        for cp in first:
            cp.start()
        passed = []
        for j, chip in enumerate(chips):
            for a in range(n):
                copy(a, 1 + j, (*chip, c), me).wait_recv()
                fwd = copy(a, 4 + j, (*chip, c), sibling)
                fwd.start()
                passed.append(fwd)
        for a in range(n):
            copy(a, 0, sibling, me).wait_recv()
            for j, chip in enumerate(chips):
                copy(a, 4 + j, (*chip, 1 - c), me).wait_recv()
        for cp in first + passed:
            cp.wait_send()
        for cp in mine:
            cp.wait()

    any_spec = pl.BlockSpec(memory_space=pl.ANY)
    outs = pl.pallas_call(
        body, name="weights_all_gather",
        in_specs=[any_spec] * n, out_specs=[any_spec] * n + [pl.BlockSpec(memory_space=pltpu.VMEM)],
        out_shape=[jax.ShapeDtypeStruct((N_DEV,) + b.shape, b.dtype) for b in blocks]
        + [jax.ShapeDtypeStruct((SUBLANES, LANES), _F32)],
        scratch_shapes=[pltpu.SemaphoreType.DMA((n, 7)), pltpu.SemaphoreType.DMA((n, 7)), pltpu.SemaphoreType.DMA((n,))],
    )(*blocks)
    return outs[:n], outs[n]


def _exchange(bigs, small):
    n = len(bigs)
    r = small.shape[0]

    def body(*refs):
        in_refs, out_refs = refs[:n + 1], refs[n + 1:2 * n + 2]
        send_sems, recv_sems, local_sems = refs[2 * n + 2:]
        x, y, c = _my_place()
        me = 4 * x + 2 * y + c
        own = [pltpu.make_async_copy(in_refs[a].at[me], out_refs[a].at[me], local_sems.at[a]) for a in range(n)]
        own.append(pltpu.make_async_copy(in_refs[n], out_refs[n].at[me], local_sems.at[n]))
        for cp in own:
            cp.start()
        copies = []
        for kk in range(1, N_DEV):
            px, py, pc = x ^ (kk >> 2), y ^ ((kk >> 1) & 1), c ^ (kk & 1)
            peer = 4 * px + 2 * py + pc
            for a in range(n + 1):
                src = in_refs[a].at[peer] if a < n else in_refs[a]
                copies.append(pltpu.make_async_remote_copy(
                    src_ref=src, dst_ref=out_refs[a].at[me],
                    send_sem=send_sems.at[a, kk - 1], recv_sem=recv_sems.at[a, kk - 1],
                    device_id=(px, py, pc), device_id_type=MESH))
        for cp in copies:
            cp.start()
        for cp in copies:
            cp.wait_recv()
        for cp in copies:
            cp.wait_send()
        for cp in own:
            cp.wait()

    any_spec = pl.BlockSpec(memory_space=pl.ANY)
    outs = pl.pallas_call(
        body, name="grad_exchange",
        in_specs=[any_spec] * (n + 1), out_specs=[any_spec] * (n + 1),
        out_shape=[jax.ShapeDtypeStruct(b.shape, b.dtype) for b in bigs]
        + [jax.ShapeDtypeStruct((N_DEV, r, LANES), small.dtype)],
        scratch_shapes=[pltpu.SemaphoreType.DMA((n + 1, 7)), pltpu.SemaphoreType.DMA((n + 1, 7)),
                        pltpu.SemaphoreType.DMA((n + 1,))],
    )(*bigs, small)
    return outs[:n], outs[n]


HBM_SPEC = pl.BlockSpec(memory_space=pltpu.HBM)
SEM_SPEC = pl.BlockSpec(memory_space=pltpu.SEMAPHORE)
DATAFLOW = pltpu.SideEffectType.DATAFLOW_SIDE_EFFECTING


def _my_index():
    x, y, c = _my_place()
    return 4 * x + 2 * y + c


def _landing(own_block):
    zone = lax.empty((N_DEV,) + own_block.shape, own_block.dtype)
    return lax.dynamic_update_index_in_dim(zone, own_block, _my_index(), 0)


def _split_copies(src_refs, land_refs, send_sems, recv_sems, gather):
    x, y, c = _my_place()
    me = 4 * x + 2 * y + c
    copies = []
    for kk in range(1, N_DEV):
        px, py, pc = x ^ (kk >> 2), y ^ ((kk >> 1) & 1), c ^ (kk & 1)
        peer = 4 * px + 2 * py + pc
        for a, (src, land) in enumerate(zip(src_refs, land_refs)):
            copies.append(pltpu.make_async_remote_copy(
                src_ref=src if gather else src.at[peer], dst_ref=land.at[me],
                send_sem=send_sems.at[a * 7 + kk - 1], recv_sem=recv_sems.at[a * 7 + kk - 1],
                device_id=(px, py, pc), device_id_type=MESH))
    return copies


def _split_start(srcs, lands, gather, name):
    n = len(srcs)

    def body(*refs):
        src_refs, land_refs = refs[:n], refs[n:2 * n]
        send_sems, recv_sems = refs[2 * n], refs[2 * n + 1]
        token = refs[-1]
        for cp in _split_copies(src_refs, land_refs, send_sems, recv_sems, gather):
            cp.start()
        token[...] = jnp.zeros_like(token)

    outs = pl.pallas_call(
        body, name=name,
        out_shape=(pltpu.SemaphoreType.DMA((7 * n,)), pltpu.SemaphoreType.DMA((7 * n,)),
                   *[pltpu.HBM(t.shape, t.dtype) for t in srcs], *[pltpu.HBM(t.shape, t.dtype) for t in lands],
                   jax.ShapeDtypeStruct((SUBLANES, LANES), _F32)),
        in_specs=[HBM_SPEC] * (2 * n),
        out_specs=(SEM_SPEC, SEM_SPEC, *[HBM_SPEC] * (2 * n), pl.BlockSpec(memory_space=pltpu.VMEM)),
        input_output_aliases={i: 2 + i for i in range(2 * n)},
        compiler_params=pltpu.CompilerParams(has_side_effects=DATAFLOW),
    )(*[pltpu.with_memory_space_constraint(t, pltpu.HBM) for t in list(srcs) + list(lands)])
    return outs[0], outs[1], outs[2:2 + n], outs[2 + n:2 + 2 * n], outs[-1]


def _split_wait(send_sems, recv_sems, srcs, lands, after, gather, name):
    n = len(srcs)

    def body(*refs):
        src_refs, land_refs = refs[:n], refs[n:2 * n]
        send_s, recv_s = refs[2 * n], refs[2 * n + 1]
        for cp in _split_copies(src_refs, land_refs, send_s, recv_s, gather):
            cp.wait_send()
            cp.wait_recv()

    outs = pl.pallas_call(
        body, name=name,
        out_shape=tuple(pltpu.HBM(t.shape, t.dtype) for t in list(srcs) + list(lands)),
        in_specs=[HBM_SPEC] * (2 * n) + [SEM_SPEC, SEM_SPEC, pl.BlockSpec(memory_space=pl.ANY)],
        out_specs=tuple([HBM_SPEC] * (2 * n)),
        input_output_aliases={i: i for i in range(2 * n)},
        compiler_params=pltpu.CompilerParams(has_side_effects=DATAFLOW),
    )(*srcs, *lands, send_sems, recv_sems, after)
    return outs[n:]


def _discretize(lam_re, lam_im, log_dt, b_re, b_im):
    lr = jnp.minimum(lam_re, -1e-4)
    li = lam_im
    dt = jnp.exp(log_dt)[:, None]
    er = jnp.exp(lr * dt)
    ar, ai = er * jnp.cos(li * dt), er * jnp.sin(li * dt)
    den = lr * lr + li * li
    cr = ((ar - 1.0) * lr + ai * li) / den
    ci = (ai * lr - (ar - 1.0) * li) / den
    bbr = cr[:, :, None] * b_re - ci[:, :, None] * b_im
    bbi = cr[:, :, None] * b_im + ci[:, :, None] * b_re
    return ar, ai, bbr, bbi


def _cmul(ar, ai, br, bi):
    return ar * br - ai * bi, ar * bi + ai * br


def _cpowers(ar, ai, n):
    pr, pi = ar[None], ai[None]
    while pr.shape[0] < n:
        nr, ni = _cmul(pr, pi, pr[-1][None], pi[-1][None])
        pr, pi = jnp.concatenate([pr, nr]), jnp.concatenate([pi, ni])
    return pr[:n], pi[:n]


def _scan_tables(ar, ai, seg, reverse):
    if reverse:
        ai = -ai
    ar, ai = ar.reshape(N_KB, KB_STATES), ai.reshape(N_KB, KB_STATES)
    pr, pi = _cpowers(ar, ai, seg)
    a1 = (pr[-1], pi[-1])
    a2 = _cmul(*a1, *a1)
    a4 = _cmul(*a2, *a2)
    row = jnp.arange(SUBLANES)[None, :, None]
    wide = lambda t: jnp.broadcast_to(t[:, None, :], (N_KB, SUBLANES, KB_STATES))
    tabs = [wide(ar), wide(ai)]
    for dist, (qr, qi) in ((1, a1), (2, a2), (4, a4)):
        keep = (row < SUBLANES - dist) if reverse else (row >= dist)
        tabs += [jnp.where(keep, wide(qr), 0.0), jnp.where(keep, wide(qi), 0.0)]
    tabs += [wide(a1[0]), wide(a1[1])]
    if reverse:
        pr, pi = pr[::-1], pi[::-1]
    pw = jnp.transpose(jnp.concatenate([pr, pi], axis=-1), (1, 0, 2))[:, :, None, :]
    return jnp.stack(tabs, axis=1).astype(_F32), pw.astype(_F32)


def _block_diag_in(br, bi):
    eye = jnp.eye(GROUPS_PER_KB, dtype=_F32)
    one = lambda t: jnp.einsum("kgpc,gh->kgchp", t.reshape(N_KB, GROUPS_PER_KB, N_STATE, SSM_GC), eye).reshape(
        N_KB, LANES, KB_STATES)
    return jnp.concatenate([one(br), one(bi)], axis=-1)


def _block_diag_in_t(dmat):
    d6 = dmat.reshape(N_KB, GROUPS_PER_KB, SSM_GC, 2, GROUPS_PER_KB, N_STATE)
    eye = jnp.eye(GROUPS_PER_KB, dtype=_F32)
    both = jnp.einsum("kgcrhp,gh->rkgpc", d6, eye).reshape(2, N_GROUP, N_STATE, SSM_GC)
    return both[0], both[1]


def _block_diag_out(c_re, c_im):
    eye = jnp.eye(GROUPS_PER_KB, dtype=_F32)
    one = lambda t: jnp.einsum("kgcp,gh->khpgc", t.reshape(N_KB, GROUPS_PER_KB, SSM_GC, N_STATE), eye).reshape(
        N_KB, KB_STATES, LANES)
    return jnp.concatenate([one(c_re), -one(c_im)], axis=1)


def _block_diag_out_t(dmat):
    d6 = dmat.reshape(N_KB, 2, GROUPS_PER_KB, N_STATE, GROUPS_PER_KB, SSM_GC)
    eye = jnp.eye(GROUPS_PER_KB, dtype=_F32)
    both = jnp.einsum("krhpgc,gh->rkgcp", d6, eye).reshape(2, N_GROUP, SSM_GC, N_STATE)
    return both[0], -both[1]


SMALL_NAMES = ("norm_mix_pre", "norm_mix_post", "ret_gn_gain", "ssm_lambda_re", "ssm_lambda_im", "ssm_log_dt",
               "ssm_b_re", "ssm_b_im", "ssm_c_re", "ssm_c_im", "ssm_d", "norm_mlp_pre", "norm_mlp_post")


def _local_grads(x, tgt, small, w_in, late_weights, emit, tm, tk, tb):
    L = x.shape[0]
    g1, g2, ggn = small["norm_mix_pre"], small["norm_mix_post"], small["ret_gn_gain"]
    g3, g4, d_skip = small["norm_mlp_pre"], small["norm_mlp_post"], small["ssm_d"]

    half = HEAD_D // 2
    inv_freq = ROPE_BASE ** (-jnp.arange(half, dtype=_F32) / half)
    ang = jnp.arange(L, dtype=_F32)[:, None] * inv_freq[None, :]
    cosf = jnp.concatenate([jnp.cos(ang), jnp.cos(ang)], axis=-1)
    sinf = jnp.concatenate([-jnp.sin(ang), jnp.sin(ang)], axis=-1)
    consts = _ret_consts()

    disc_in = (small["ssm_lambda_re"][0], small["ssm_lambda_im"][0], small["ssm_log_dt"][0],
               small["ssm_b_re"][0], small["ssm_b_im"][0])
    (ar, ai, bbr, bbi), disc_vjp = jax.vjp(_discretize, *disc_in)
    bmat = _block_diag_in(bbr, bbi).astype(_BF)
    cmat = _block_diag_out(small["ssm_c_re"][0], small["ssm_c_im"][0]).astype(_BF)
    seg = tb // SUBLANES
    tab_f, pw_f = _scan_tables(ar, ai, seg, False)
    tab_r, pw_r = _scan_tables(ar, ai, seg, True)

    h1, q, k, v, gate, u = _inproj_fwd(x, g1, w_in, cosf, sinf, tm)
    o, y_ret, r_prev = _retention_fwd(q, k, v, gate, ggn, consts)
    s, cin = _s5_fwd(u, bmat, cmat, tab_f, pw_f, d_skip, tb)
    w_glu, w_out, w_ff1, w_ff2 = late_weights(s)
    ys, glu, cat, mix, x2 = _mixout_fwd(s, y_ret, x, w_glu, w_out, g2, tm)
    h3, f1 = _ff1_fwd(x2, g3, w_ff1, tm)
    dy, dm, dg4, sq = _ff2_loss(f1, x2, tgt, g4, w_ff2, tm)

    df1, dw_ff2 = _ff2_bwd(dm, f1, w_ff2, min(1024, L), 1024)
    dx2, dmix, dg3, dg2 = _ff1_bwd(df1, w_ff1, x2, mix, dy, g3, g2, tm)
    dw_ff1 = _matmul_tn(h3, df1, tk, FF1_COLS, "dw_ff1", slots=True)
    zero = emit({"w_ff1": dw_ff1, "w_ff2": dw_ff2})
    dglu, ds, dgate, do, dggn = _mixout_bwd(dmix, w_out, w_glu, glu, s, o, gate, ggn if zero is None else ggn + zero, tm)
    dw_out = _matmul_tn(cat, dmix, tk, 1024, "dw_out")
    dw_glu = _matmul_tn(ys, dglu, tk, 1024, "dw_glu")
    zero = emit({"w_glu": dw_glu, "w_out": dw_out})
    du, dbmat, dcmat, da8, dd = _s5_bwd(u, ds, cin, bmat, cmat, tab_f, pw_f, tab_r, pw_r,
                                        d_skip if zero is None else d_skip + zero, tb)
    dq, dk, dv = _retention_bwd(q, k, v, do, r_prev, consts, cosf, sinf)
    pieces = (dq, dk, dv, dgate, du)
    dw_in = jnp.concatenate([_matmul_tn(h1, p, tk, RET_W, "dw_in_%d" % j) for j, p in enumerate(pieces)], axis=1)
    zero = emit({"w_in": dw_in})
    gx, dg1 = _inproj_bwd(pieces, w_in, x, dx2, g1 if zero is None else g1 + zero, tm)

    da = jnp.sum(da8, axis=1)
    dar = da[:, :KB_STATES].reshape(N_GROUP, N_STATE)
    dai = da[:, KB_STATES:].reshape(N_GROUP, N_STATE)
    dbr, dbi = _block_diag_in_t(dbmat)
    dlre, dlim, dldt, dbre, dbim = disc_vjp((dar, dai, dbr, dbi))
    dcre, dcim = _block_diag_out_t(dcmat)

    gsmall = {
        "norm_mix_pre": dg1, "norm_mix_post": dg2, "ret_gn_gain": dggn,
        "ssm_lambda_re": dlre[None], "ssm_lambda_im": dlim[None], "ssm_log_dt": dldt[None],
        "ssm_b_re": dbre[None], "ssm_b_im": dbim[None], "ssm_c_re": dcre[None], "ssm_c_im": dcim[None],
        "ssm_d": dd, "norm_mlp_pre": dg3, "norm_mlp_post": dg4,
    }
    return sq, gx, gsmall


BIG_SHAPES = {"w_in": (D_MODEL, IN_COLS // N_DEV), "w_glu": (SSM_W, 2 * SSM_W // N_DEV), "w_out": (D_MODEL // N_DEV, D_MODEL),
              "w_ff1": (D_MODEL, FF1_COLS), "w_ff2": (D_FF // N_DEV, D_MODEL)}
BIG_NAMES = ("w_in", "w_glu", "w_out", "w_ff1", "w_ff2")


def _cols_from_slots(g):
    return jnp.transpose(g, (1, 0, 2)).reshape(g.shape[1], N_DEV * g.shape[2])


def _cols_to_slots(dw):
    r, cols = dw.shape
    return jnp.transpose(dw.reshape(r, N_DEV, cols // N_DEV), (1, 0, 2))


LATE_NAMES = ("w_glu", "w_out", "w_ff1", "w_ff2")


def _grad_slots(name, dw):
    if name in ("w_in", "w_glu"):
        return _cols_to_slots(dw)
    if name == "w_ff1":
        return dw
    return dw.reshape((N_DEV,) + BIG_SHAPES[name])


PIECE_ROWS = 8


def _small_layout(shapes):
    off, rows = {}, 0
    for n in SMALL_NAMES:
        off[n] = rows
        rows += -(-math.prod(shapes[n]) // (PIECE_ROWS * LANES)) * PIECE_ROWS
    return off, rows, rows + PIECE_ROWS


def _pack_small(vals, shapes, last=None):
    parts = []
    for n in SMALL_NAMES:
        flat = vals[n].reshape(-1).astype(_F32)
        pad = -flat.shape[0] % (PIECE_ROWS * LANES)
        if pad:
            flat = jnp.concatenate([flat, jnp.zeros((pad,), _F32)])
        parts.append(flat.reshape(-1, LANES))
    parts.append(jnp.zeros((PIECE_ROWS, LANES), _F32) if last is None else last)
    return jnp.concatenate(parts, axis=0)


def _unpack_small(buf, shapes):
    off, _, _ = _small_layout(shapes)
    out = {}
    for n in SMALL_NAMES:
        size = math.prod(shapes[n])
        rows = -(-size // LANES)
        out[n] = buf[off[n]:off[n] + rows].reshape(-1)[:size].reshape(shapes[n])
    return out


WEIGHT_NAMES = ('norm_mix_pre', 'norm_mix_post', 'w_in', 'ret_gn_gain', 'ssm_lambda_re', 'ssm_lambda_im', 'ssm_log_dt',
                'ssm_b_re', 'ssm_b_im', 'ssm_c_re', 'ssm_c_im', 'ssm_d', 'w_glu', 'w_out', 'norm_mlp_pre',
                'norm_mlp_post', 'w_ff1', 'w_ff2')


def kernel(x, norm_mix_pre, norm_mix_post, w_in, ret_gn_gain, ssm_lambda_re, ssm_lambda_im, ssm_log_dt, ssm_b_re, ssm_b_im, ssm_c_re, ssm_c_im, ssm_d, w_glu, w_out, norm_mlp_pre, norm_mlp_post, w_ff1, w_ff2, loss_target, m_norm_mix_pre, m_norm_mix_post, m_w_in, m_ret_gn_gain, m_ssm_lambda_re, m_ssm_lambda_im, m_ssm_log_dt, m_ssm_b_re, m_ssm_b_im, m_ssm_c_re, m_ssm_c_im, m_ssm_d, m_w_glu, m_w_out, m_norm_mlp_pre, m_norm_mlp_post, m_w_ff1, m_w_ff2, v_norm_mix_pre, v_norm_mix_post, v_w_in, v_ret_gn_gain, v_ssm_lambda_re, v_ssm_lambda_im, v_ssm_log_dt, v_ssm_b_re, v_ssm_b_im, v_ssm_c_re, v_ssm_c_im, v_ssm_d, v_w_glu, v_w_out, v_norm_mlp_pre, v_norm_mlp_post, v_w_ff1, v_w_ff2):
    args = dict(locals())
    w = {n: args[n] for n in WEIGHT_NAMES}
    m = {n: args["m_" + n] for n in WEIGHT_NAMES}
    v = {n: args["v_" + n] for n in WEIGHT_NAMES}
    L = x.shape[1]
    tm = min(256, L)
    tk = min(2048, L)
    tb = min(512, L)

    (w_in_slots,), gathered_zero = _all_gather([w["w_in"][0].astype(_BF)])
    w_in_full = _cols_from_slots(w_in_slots)
    late_blocks = [w[n][0].astype(_BF) for n in LATE_NAMES]
    late_blocks[0] = late_blocks[0] + gathered_zero[0, 0].astype(_BF)
    late = _split_start(late_blocks, [_landing(b) for b in late_blocks], True, "late_weights_start")

    def late_weights(after):
        g = dict(zip(LATE_NAMES, _split_wait(*late[:4], after, True, "late_weights_wait")))
        return (_cols_from_slots(g["w_glu"]), g["w_out"].reshape(D_MODEL, D_MODEL), g["w_ff1"],
                g["w_ff2"].reshape(D_FF, D_MODEL))

    in_flight = []

    def emit(dws):
        names = sorted(dws)
        srcs = [_grad_slots(n, dws[n]) for n in names]
        lands = [_landing(lax.dynamic_index_in_dim(t, _my_index(), 0, keepdims=False)) for t in srcs]
        started = _split_start(srcs, lands, False, "grads_start_" + "_".join(names))
        in_flight.append((names, started))
        return started[4][0, 0]

    small_w = {n: w[n] for n in SMALL_NAMES}
    small_w["norm_mix_pre"] = small_w["norm_mix_pre"] + late[4][0, 0]
    sq, gx, gsmall = _local_grads(x[0], loss_target[0], small_w, w_in_full, late_weights, emit, tm, tk, tb)

    grads, delta, new_m, new_v = {}, {}, {}, {}

    def finish(names, started, after):
        landed = _split_wait(*started[:4], after, False, "grads_wait_" + "_".join(names))
        for n, parts in zip(names, landed):
            res = _sum_adamw(parts, w[n][0], m[n][0], v[n][0], min(256, BIG_SHAPES[n][0]), "adamw_" + n)
            grads[n], delta[n], new_m[n], new_v[n] = (t[None] for t in res)
        return res[1]

    after = gx
    for names, started in in_flight[:-1]:
        after = finish(names, started, after)

    shapes = {n: w[n].shape for n in SMALL_NAMES}
    loss_rows = jnp.broadcast_to(0.5 / D_MODEL * jnp.sum(sq), (PIECE_ROWS, LANES)).astype(_F32)
    _, small_parts = _exchange([], _pack_small(gsmall, shapes, loss_rows))
    finish(*in_flight[-1], small_parts)
    sw, sm, sv = _pack_small(w, shapes), _pack_small(m, shapes), _pack_small(v, shapes)
    res = _sum_adamw(small_parts, sw, sm, sv, sw.shape[0], "adamw_small")
    for dst, buf in zip((grads, delta, new_m, new_v), res):
        dst.update(_unpack_small(buf, shapes))
    _, loss_at, _ = _small_layout(shapes)
    loss = res[0][loss_at, 0]

    return (loss, gx[None], *[grads[n] for n in WEIGHT_NAMES], *[delta[n] for n in WEIGHT_NAMES],
            *[new_m[n] for n in WEIGHT_NAMES], *[new_v[n] for n in WEIGHT_NAMES])
```

```python
import math

import jax
import jax.numpy as jnp
from jax import lax
from jax.experimental import pallas as pl
from jax.experimental.pallas import tpu as pltpu

_BF = jnp.bfloat16
_F32 = jnp.float32

D_MODEL = 1024
RET_W = 512
N_HEAD = 4
HEAD_D = 128
CHUNK = 128
SSM_W = 512
SSM_GC = 16
N_GROUP = 32
N_STATE = 64
GROUPS_PER_KB = 8
N_KB = 4
KB_STATES = GROUPS_PER_KB * N_STATE
D_FF = 4096
IN_COLS = 2560
NORM_EPS = 1e-6
ROPE_BASE = 10000.0
N_DEV = 8

ADAM_LR = 0.001
ADAM_B1 = 0.9
ADAM_B2 = 0.999
ADAM_EPS = 1e-08
ADAM_WD = 0.01
ADAM_STEP = 10

SUBLANES = 8
LANES = 128
VMEM_LIMIT = 52 * 1024 * 1024
KB_PER_STEP = 2
SCAN_UNROLL = 2

MESH = pl.DeviceIdType.MESH


def _params(*sem):
    return pltpu.CompilerParams(dimension_semantics=sem, vmem_limit_bytes=VMEM_LIMIT)


def _dot(a, b):
    return jnp.dot(a, b, preferred_element_type=_F32)


def _dot_nt(a, b):
    return lax.dot_general(a, b, (((1,), (1,)), ((), ())), preferred_element_type=_F32)


def _dot_tn(a, b):
    return lax.dot_general(a, b, (((0,), (0,)), ((), ())), preferred_element_type=_F32)


def _rms_r(z):
    return lax.rsqrt(jnp.mean(z * z, axis=-1, keepdims=True) + NORM_EPS)


def _rms_bwd(z, g, dn):
    r = _rms_r(z)
    t = dn * g
    dz = r * t - z * (r * r * r * jnp.mean(t * z, axis=-1, keepdims=True))
    return dz, dn * z * r


def _rope(t, cs, sn):
    return t * cs + pltpu.roll(t, HEAD_D // 2, 1) * sn


def _rope_t(t, cs, sn):
    return t * cs - pltpu.roll(t, HEAD_D // 2, 1) * sn


def _sigmoid(z):
    return 1.0 / (1.0 + jnp.exp(-z))


_GELU_C = math.sqrt(2.0 / math.pi)


def _gelu(z):
    return 0.5 * z * (1.0 + jnp.tanh(_GELU_C * (z + 0.044715 * z * z * z)))


def _gelu_grad(z):
    th = jnp.tanh(_GELU_C * (z + 0.044715 * z * z * z))
    return 0.5 * (1.0 + th) + 0.5 * z * (1.0 - th * th) * _GELU_C * (1.0 + 3 * 0.044715 * z * z)


def _row_spec(tm, n):
    return pl.BlockSpec((tm, n), lambda i: (i, 0))


def _full_spec(shape):
    nd = len(shape)
    return pl.BlockSpec(shape, lambda *_: (0,) * nd)


def _inproj_fwd(x, g1, w_in, cosf, sinf, tm):
    L = x.shape[0]

    def body(x_ref, g_ref, w_ref, cos_ref, sin_ref, h_ref, q_ref, k_ref, v_ref, gate_ref, u_ref):
        xv = x_ref[...]
        h = (xv * _rms_r(xv) * g_ref[...]).astype(_BF)
        h_ref[...] = h
        proj = _dot(h, w_ref[...])
        cs, sn = cos_ref[...], sin_ref[...]
        for hh in range(N_HEAD):
            lo = hh * HEAD_D
            q_ref[:, lo:lo + HEAD_D] = _rope(proj[:, lo:lo + HEAD_D], cs, sn).astype(_BF)
            kh = _rope(proj[:, RET_W + lo:RET_W + lo + HEAD_D], cs, sn) * (HEAD_D ** -0.5)
            k_ref[:, lo:lo + HEAD_D] = kh.astype(_BF)
        v_ref[...] = proj[:, 2 * RET_W:3 * RET_W].astype(_BF)
        gate_ref[...] = proj[:, 3 * RET_W:4 * RET_W]
        u_ref[...] = proj[:, 4 * RET_W:]

    return pl.pallas_call(
        body, name="inproj_fwd", grid=(L // tm,),
        in_specs=[_row_spec(tm, D_MODEL), _full_spec((1, D_MODEL)), _full_spec((D_MODEL, IN_COLS)),
                  _row_spec(tm, HEAD_D), _row_spec(tm, HEAD_D)],
        out_specs=[_row_spec(tm, D_MODEL)] + [_row_spec(tm, RET_W)] * 5,
        out_shape=[jax.ShapeDtypeStruct((L, D_MODEL), _BF)] + [jax.ShapeDtypeStruct((L, RET_W), _BF)] * 3
        + [jax.ShapeDtypeStruct((L, RET_W), _F32)] * 2,
        compiler_params=_params("parallel"),
    )(x, g1, w_in, cosf, sinf)


def _ret_consts():
    lg = jnp.log(1.0 - jnp.exp(jnp.linspace(math.log(1.0 / 32), math.log(1.0 / 512), N_HEAD))).astype(_F32)
    idx = jnp.arange(CHUNK, dtype=_F32)
    diff = idx[:, None] - idx[None, :]
    decay = jnp.where(diff[None] >= 0, jnp.exp(jnp.maximum(diff, 0.0)[None] * lg[:, None, None]), 0.0)
    zeta = jnp.exp((CHUNK - 1 - idx)[None, :] * lg[:, None])
    xi = jnp.exp((idx + 1.0)[None, :] * lg[:, None])
    gc = jnp.exp(CHUNK * lg)
    wide = lambda t: jnp.broadcast_to(t[:, :, None], (N_HEAD, CHUNK, HEAD_D)).astype(_F32)
    gcw = jnp.broadcast_to(gc[:, None, None], (N_HEAD, SUBLANES, HEAD_D)).astype(_F32)
    return decay.astype(_F32), wide(xi), wide(zeta), gcw


def _head_specs():
    c3 = _full_spec((N_HEAD, CHUNK, CHUNK))
    return [c3, c3, c3, _full_spec((N_HEAD, SUBLANES, HEAD_D))]


def _retention_fwd(q, k, v, gate, ggn, consts):
    L = q.shape[0]
    nc = L // CHUNK
    blk = pl.BlockSpec((CHUNK, RET_W), lambda n: (n, 0))

    def body(q_ref, k_ref, v_ref, gate_ref, ggn_ref, dm_ref, xi_ref, zeta_ref, gc_ref,
             o_ref, y_ref, rp_ref, r_scr):
        @pl.when(pl.program_id(0) == 0)
        def _():
            r_scr[...] = jnp.zeros_like(r_scr)

        for hh in range(N_HEAD):
            cols = slice(hh * HEAD_D, (hh + 1) * HEAD_D)
            qv, kv, vv = q_ref[:, cols], k_ref[:, cols], v_ref[:, cols]
            r_prev = r_scr[hh]
            s = _dot_nt(qv, kv) * dm_ref[hh]
            o = _dot(s.astype(_BF), vv) + _dot(qv, r_prev.astype(_BF)) * xi_ref[hh]
            o_ref[:, cols] = o
            rp_ref[hh, 0] = r_prev
            vz = (vv.astype(_F32) * zeta_ref[hh]).astype(_BF)
            r_scr[hh] = gc_ref[hh, 0:1, :] * r_prev + _dot_tn(kv, vz)
            dlt = o - jnp.mean(o, axis=-1, keepdims=True)
            on = dlt * lax.rsqrt(jnp.mean(dlt * dlt, axis=-1, keepdims=True) + NORM_EPS)
            gt = gate_ref[:, cols]
            y_ref[:, cols] = (gt * _sigmoid(gt) * (on * ggn_ref[:, cols])).astype(_BF)

    return pl.pallas_call(
        body, name="retention_fwd", grid=(nc,),
        in_specs=[blk, blk, blk, blk, _full_spec((1, RET_W))] + _head_specs(),
        out_specs=[blk, blk, pl.BlockSpec((N_HEAD, 1, HEAD_D, HEAD_D), lambda n: (0, n, 0, 0))],
        out_shape=[jax.ShapeDtypeStruct((L, RET_W), _F32), jax.ShapeDtypeStruct((L, RET_W), _BF),
                   jax.ShapeDtypeStruct((N_HEAD, nc, HEAD_D, HEAD_D), _F32)],
        scratch_shapes=[pltpu.VMEM((N_HEAD, HEAD_D, HEAD_D), _F32)],
        compiler_params=_params("arbitrary"),
    )(q, k, v, gate, ggn, *consts)


def _rows_to_segments(dst_scr, src_ref, seg):
    for g in range(dst_scr.shape[0]):
        for j in range(SUBLANES):
            dst_scr[g, pl.ds(j, seg, stride=SUBLANES), :] = src_ref[pl.ds(j * seg, seg), g * LANES:(g + 1) * LANES]


def _segments_to_rows(dst_ref, src_scr, seg):
    for g in range(src_scr.shape[0]):
        for j in range(SUBLANES):
            dst_ref[pl.ds(j * seg, seg), g * LANES:(g + 1) * LANES] = src_scr[g, pl.ds(j, seg, stride=SUBLANES), :]


def _scan_segments(x_ref, tab_ref, pw_ref, carry_ref, seg, reverse, xprev_ref=None, da_ref=None):
    G = x_ref.shape[0]
    W = KB_STATES
    re, im = pl.ds(0, W), pl.ds(W, W)
    row_id = lax.broadcasted_iota(jnp.int32, (SUBLANES, W), 0)
    edge_in = (row_id == SUBLANES - 1) if reverse else (row_id == 0)
    edge_out = 0 if reverse else SUBLANES - 1
    a_tab = [(tab_ref[g, 0], tab_ref[g, 1]) for g in range(G)]

    def local(i, st):
        r = (seg - 1 - i) if reverse else i
        out = []
        for g in range(G):
            (ar, ai), (sr, si) = a_tab[g], st[g]
            nr = ar * sr - ai * si + x_ref[g, r, :, re]
            ni = ar * si + ai * sr + x_ref[g, r, :, im]
            x_ref[g, r, :, re] = nr
            x_ref[g, r, :, im] = ni
            out.append((nr, ni))
        return tuple(out)

    zero = jnp.zeros((SUBLANES, W), _F32)
    ends = lax.fori_loop(0, seg, local, tuple((zero, zero) for _ in range(G)), unroll=SCAN_UNROLL)

    entry = []
    shift = (SUBLANES - 1) if reverse else 1
    for g in range(G):
        er, ei = ends[g]
        fr = jnp.where(edge_in, carry_ref[g, :, re], pltpu.roll(er, shift, 0))
        fi = jnp.where(edge_in, carry_ref[g, :, im], pltpu.roll(ei, shift, 0))
        for j, dist in enumerate((1, 2, 4)):
            pr, pi = tab_ref[g, 2 + 2 * j], tab_ref[g, 3 + 2 * j]
            sh = (SUBLANES - dist) if reverse else dist
            sr, si = pltpu.roll(fr, sh, 0), pltpu.roll(fi, sh, 0)
            fr, fi = fr + pr * sr - pi * si, fi + pr * si + pi * sr
        br, bi = tab_ref[g, 8], tab_ref[g, 9]
        outr = br * fr - bi * fi + er
        outi = br * fi + bi * fr + ei
        carry_ref[g, :, re] = jnp.broadcast_to(outr[edge_out:edge_out + 1, :], (SUBLANES, W))
        carry_ref[g, :, im] = jnp.broadcast_to(outi[edge_out:edge_out + 1, :], (SUBLANES, W))
        entry.append((fr, fi))

    keep_prev = xprev_ref is not None and not reverse
    add_da = da_ref is not None

    def fix(r, st):
        out = []
        for g in range(G):
            fr, fi = entry[g]
            pwr, pwi = pw_ref[g, r, :, re], pw_ref[g, r, :, im]
            xr = x_ref[g, r, :, re] + (pwr * fr - pwi * fi)
            xi = x_ref[g, r, :, im] + (pwr * fi + pwi * fr)
            x_ref[g, r, :, re] = xr
            x_ref[g, r, :, im] = xi
            if keep_prev:
                xprev_ref[g, r, :, re] = st[g][0]
                xprev_ref[g, r, :, im] = st[g][1]
                out.append((xr, xi))
            elif add_da:
                xpr, xpi = xprev_ref[g, r, :, re], xprev_ref[g, r, :, im]
                out.append((st[g][0] + (xr * xpr + xi * xpi), st[g][1] + (xi * xpr - xr * xpi)))
            else:
                out.append(st[g])
        return tuple(out)

    if keep_prev:
        init = tuple(entry)
    elif add_da:
        init = tuple((zero, zero) for _ in range(G))
    else:
        init = tuple((zero[0:1, 0:LANES], zero[0:1, 0:LANES]) for _ in range(G))
    st = lax.fori_loop(0, seg, fix, init, unroll=SCAN_UNROLL)
    if add_da:
        for g in range(G):
            da_ref[g, :, re] += st[g][0]
            da_ref[g, :, im] += st[g][1]


def _s5_specs(seg):
    G = KB_PER_STEP
    return dict(
        b=pl.BlockSpec((G, LANES, 2 * KB_STATES), lambda kb, t: (kb, 0, 0)),
        c=pl.BlockSpec((G, 2 * KB_STATES, LANES), lambda kb, t: (kb, 0, 0)),
        tab=pl.BlockSpec((G, 10, SUBLANES, KB_STATES), lambda kb, t: (kb, 0, 0, 0)),
        pw=pl.BlockSpec((G, seg, 1, 2 * KB_STATES), lambda kb, t: (kb, 0, 0, 0)),
        d=pl.BlockSpec((1, G * LANES), lambda kb, t: (0, kb)),
    )


def _s5_fwd(u, bmat, cmat, tab_f, pw_f, d_skip, tb):
    L = u.shape[0]
    nt = L // tb
    seg = tb // SUBLANES
    G = KB_PER_STEP
    ucol = pl.BlockSpec((tb, G * LANES), lambda kb, t: (t, kb))
    sp = _s5_specs(seg)

    def body(u_ref, b_ref, c_ref, tab_ref, pw_ref, d_ref, s_ref, cin_ref, up_scr, y_scr, x_scr, carry_scr):
        @pl.when(pl.program_id(1) == 0)
        def _():
            carry_scr[...] = jnp.zeros_like(carry_scr)

        cin_ref[:, 0] = carry_scr[...]
        _rows_to_segments(up_scr, u_ref, seg)
        for g in range(G):
            x_scr[g] = _dot(up_scr[g].astype(_BF), b_ref[g]).reshape(seg, SUBLANES, 2 * KB_STATES)
        _scan_segments(x_scr, tab_ref, pw_ref, carry_scr, seg, reverse=False)
        for g in range(G):
            y = _dot(x_scr[g].reshape(tb, 2 * KB_STATES).astype(_BF), c_ref[g])
            y_scr[g] = y + d_ref[:, g * LANES:(g + 1) * LANES] * up_scr[g]
        _segments_to_rows(s_ref, y_scr, seg)

    return pl.pallas_call(
        body, name="s5_fwd", grid=(N_KB // G, nt),
        in_specs=[ucol, sp["b"], sp["c"], sp["tab"], sp["pw"], sp["d"]],
        out_specs=[ucol, pl.BlockSpec((G, 1, SUBLANES, 2 * KB_STATES), lambda kb, t: (kb, t, 0, 0))],
        out_shape=[jax.ShapeDtypeStruct((L, SSM_W), _F32),
                   jax.ShapeDtypeStruct((N_KB, nt, SUBLANES, 2 * KB_STATES), _F32)],
        scratch_shapes=[pltpu.VMEM((G, tb, LANES), _F32)] * 2
        + [pltpu.VMEM((G, seg, SUBLANES, 2 * KB_STATES), _F32), pltpu.VMEM((G, SUBLANES, 2 * KB_STATES), _F32)],
        compiler_params=_params("parallel", "arbitrary"),
    )(u, bmat, cmat, tab_f, pw_f, d_skip)


def _mixout_fwd(s, y_ret, x, w_glu, w_out, g2, tm):
    L = s.shape[0]

    def body(s_ref, yr_ref, x_ref, wg_ref, wo_ref, g_ref, ys_ref, glu_ref, cat_ref, mix_ref, x2_ref):
        ys = _gelu(s_ref[...]).astype(_BF)
        ys_ref[...] = ys
        glu = _dot(ys, wg_ref[...])
        glu_ref[...] = glu
        cat_ref[:, :RET_W] = yr_ref[...]
        cat_ref[:, RET_W:] = (glu[:, :SSM_W] * _sigmoid(glu[:, SSM_W:])).astype(_BF)
        mix = _dot(cat_ref[...], wo_ref[...])
        mix_ref[...] = mix
        x2_ref[...] = x_ref[...] + mix * _rms_r(mix) * g_ref[...]

    return pl.pallas_call(
        body, name="mixout_fwd", grid=(L // tm,),
        in_specs=[_row_spec(tm, SSM_W), _row_spec(tm, RET_W), _row_spec(tm, D_MODEL),
                  _full_spec((SSM_W, 2 * SSM_W)), _full_spec((D_MODEL, D_MODEL)), _full_spec((1, D_MODEL))],
        out_specs=[_row_spec(tm, SSM_W), _row_spec(tm, 2 * SSM_W), _row_spec(tm, D_MODEL),
                   _row_spec(tm, D_MODEL), _row_spec(tm, D_MODEL)],
        out_shape=[jax.ShapeDtypeStruct((L, SSM_W), _BF), jax.ShapeDtypeStruct((L, 2 * SSM_W), _F32),
                   jax.ShapeDtypeStruct((L, D_MODEL), _BF), jax.ShapeDtypeStruct((L, D_MODEL), _F32),
                   jax.ShapeDtypeStruct((L, D_MODEL), _F32)],
        compiler_params=_params("parallel"),
    )(s, y_ret, x, w_glu, w_out, g2)


FF1_COLS = D_FF // N_DEV


def _ff1_fwd(x2, g3, w1, tm):
    L = x2.shape[0]

    def body(x_ref, g_ref, w_ref, h_ref, f_ref):
        xv = x_ref[...]
        h = (xv * _rms_r(xv) * g_ref[...]).astype(_BF)
        h_ref[...] = h
        for j in range(N_DEV):
            f_ref[:, j * FF1_COLS:(j + 1) * FF1_COLS] = _dot(h, w_ref[j])

    return pl.pallas_call(
        body, name="ff1_fwd", grid=(L // tm,),
        in_specs=[_row_spec(tm, D_MODEL), _full_spec((1, D_MODEL)), _full_spec((N_DEV, D_MODEL, FF1_COLS))],
        out_specs=[_row_spec(tm, D_MODEL), _row_spec(tm, D_FF)],
        out_shape=[jax.ShapeDtypeStruct((L, D_MODEL), _BF), jax.ShapeDtypeStruct((L, D_FF), _F32)],
        compiler_params=_params("parallel"),
    )(x2, g3, w1)


def _ff2_loss(f1, x2, tgt, g4, w2, tm):
    L = f1.shape[0]

    def body(f_ref, x_ref, t_ref, g_ref, w_ref, dy_ref, dm_ref, dg_ref, ls_ref):
        @pl.when(pl.program_id(0) == 0)
        def _():
            dg_ref[...] = jnp.zeros_like(dg_ref)
            ls_ref[...] = jnp.zeros_like(ls_ref)

        rl = jnp.maximum(f_ref[...], 0.0)
        m = _dot((rl * rl).astype(_BF), w_ref[...])
        g = g_ref[...]
        y = x_ref[...] + m * _rms_r(m) * g
        err = y - t_ref[...]
        ls_ref[...] += jnp.sum(err * err, axis=0, keepdims=True)
        dy = err * (1.0 / D_MODEL)
        dy_ref[...] = dy
        dm, dgr = _rms_bwd(m, g, dy)
        dm_ref[...] = dm.astype(_BF)
        dg_ref[...] += jnp.sum(dgr, axis=0, keepdims=True)

    return pl.pallas_call(
        body, name="ff2_loss", grid=(L // tm,),
        in_specs=[_row_spec(tm, D_FF), _row_spec(tm, D_MODEL), _row_spec(tm, D_MODEL),
                  _full_spec((1, D_MODEL)), _full_spec((D_FF, D_MODEL))],
        out_specs=[_row_spec(tm, D_MODEL), _row_spec(tm, D_MODEL), _full_spec((1, D_MODEL)), _full_spec((1, D_MODEL))],
        out_shape=[jax.ShapeDtypeStruct((L, D_MODEL), _F32), jax.ShapeDtypeStruct((L, D_MODEL), _BF),
                   jax.ShapeDtypeStruct((1, D_MODEL), _F32), jax.ShapeDtypeStruct((1, D_MODEL), _F32)],
        compiler_params=_params("arbitrary"),
    )(f1, x2, tgt, g4, w2)


def _ff2_bwd(dm, f1, w2, tm, tn):
    L = dm.shape[0]
    last = L // tm - 1

    def body(dm_ref, f_ref, w_ref, df_ref, dw_ref, acc):
        @pl.when(pl.program_id(1) == 0)
        def _():
            acc[...] = jnp.zeros_like(acc)

        dmv = dm_ref[...]
        rl = jnp.maximum(f_ref[...], 0.0)
        df_ref[...] = (_dot_nt(dmv, w_ref[...]) * (2.0 * rl)).astype(_BF)
        acc[...] += _dot_tn((rl * rl).astype(_BF), dmv)

        @pl.when(pl.program_id(1) == last)
        def _():
            dw_ref[...] = acc[...].astype(_BF)

    return pl.pallas_call(
        body, name="ff2_bwd", grid=(D_FF // tn, L // tm),
        in_specs=[pl.BlockSpec((tm, D_MODEL), lambda j, i: (i, 0)), pl.BlockSpec((tm, tn), lambda j, i: (i, j)),
                  pl.BlockSpec((tn, D_MODEL), lambda j, i: (j, 0))],
        out_specs=[pl.BlockSpec((tm, tn), lambda j, i: (i, j)), pl.BlockSpec((tn, D_MODEL), lambda j, i: (j, 0))],
        out_shape=[jax.ShapeDtypeStruct((L, D_FF), _BF), jax.ShapeDtypeStruct((D_FF, D_MODEL), _BF)],
        scratch_shapes=[pltpu.VMEM((tn, D_MODEL), _F32)],
        compiler_params=_params("parallel", "arbitrary"),
    )(dm, f1, w2)


def _ff1_bwd(df1, w1, x2, mix, dy, g3, g2, tm):
    L = df1.shape[0]

    def body(df_ref, w_ref, x2_ref, mix_ref, dy_ref, g3_ref, g2_ref, dx2_ref, dmix_ref, dg3_ref, dg2_ref):
        @pl.when(pl.program_id(0) == 0)
        def _():
            dg3_ref[...] = jnp.zeros_like(dg3_ref)
            dg2_ref[...] = jnp.zeros_like(dg2_ref)

        dh = _dot_nt(df_ref[:, 0:FF1_COLS], w_ref[0])
        for j in range(1, N_DEV):
            dh = dh + _dot_nt(df_ref[:, j * FF1_COLS:(j + 1) * FF1_COLS], w_ref[j])
        dz, dgr = _rms_bwd(x2_ref[...], g3_ref[...], dh)
        dg3_ref[...] += jnp.sum(dgr, axis=0, keepdims=True)
        dx2 = dy_ref[...] + dz
        dx2_ref[...] = dx2
        dmx, dgr2 = _rms_bwd(mix_ref[...], g2_ref[...], dx2)
        dg2_ref[...] += jnp.sum(dgr2, axis=0, keepdims=True)
        dmix_ref[...] = dmx.astype(_BF)

    vec = _full_spec((1, D_MODEL))
    return pl.pallas_call(
        body, name="ff1_bwd", grid=(L // tm,),
        in_specs=[_row_spec(tm, D_FF), _full_spec((N_DEV, D_MODEL, FF1_COLS)), _row_spec(tm, D_MODEL),
                  _row_spec(tm, D_MODEL), _row_spec(tm, D_MODEL), vec, vec],
        out_specs=[_row_spec(tm, D_MODEL), _row_spec(tm, D_MODEL), vec, vec],
        out_shape=[jax.ShapeDtypeStruct((L, D_MODEL), _F32), jax.ShapeDtypeStruct((L, D_MODEL), _BF),
                   jax.ShapeDtypeStruct((1, D_MODEL), _F32), jax.ShapeDtypeStruct((1, D_MODEL), _F32)],
        compiler_params=_params("arbitrary"),
    )(df1, w1, x2, mix, dy, g3, g2)


def _matmul_tn(a, b, tm, tn, name, slots=False):
    L, K = a.shape
    N = b.shape[1]
    last = L // tm - 1

    def body(a_ref, b_ref, o_ref, acc):
        @pl.when(pl.program_id(1) == 0)
        def _():
            acc[...] = jnp.zeros_like(acc)

        acc[...] += _dot_tn(a_ref[...].astype(_BF), b_ref[...].astype(_BF))

        @pl.when(pl.program_id(1) == last)
        def _():
            if slots:
                o_ref[0] = acc[...].astype(_BF)
            else:
                o_ref[...] = acc[...].astype(_BF)

    if slots:
        out_spec = pl.BlockSpec((1, K, tn), lambda j, i: (j, 0, 0))
        out_shape = jax.ShapeDtypeStruct((N // tn, K, tn), _BF)
    else:
        out_spec = pl.BlockSpec((K, tn), lambda j, i: (0, j))
        out_shape = jax.ShapeDtypeStruct((K, N), _BF)
    return pl.pallas_call(
        body, name=name, grid=(N // tn, L // tm),
        in_specs=[pl.BlockSpec((tm, K), lambda j, i: (i, 0)), pl.BlockSpec((tm, tn), lambda j, i: (i, j))],
        out_specs=out_spec, out_shape=out_shape,
        scratch_shapes=[pltpu.VMEM((K, tn), _F32)],
        compiler_params=_params("parallel", "arbitrary"),
    )(a, b)


def _mixout_bwd(dmix, w_out, w_glu, glu, s, o, gate, ggn, tm):
    L = dmix.shape[0]

    def body(dmix_ref, wo_ref, wg_ref, glu_ref, s_ref, o_ref, gate_ref, ggn_ref,
             dglu_ref, ds_ref, dgate_ref, do_ref, dggn_ref):
        @pl.when(pl.program_id(0) == 0)
        def _():
            dggn_ref[...] = jnp.zeros_like(dggn_ref)

        dcat = _dot_nt(dmix_ref[...], wo_ref[...])
        dy_ret, dy_ssm = dcat[:, :RET_W], dcat[:, RET_W:]
        glu = glu_ref[...]
        ga, sg = glu[:, :SSM_W], _sigmoid(glu[:, SSM_W:])
        dga = (dy_ssm * sg).astype(_BF)
        dgb = (dy_ssm * ga * sg * (1.0 - sg)).astype(_BF)
        dglu_ref[:, :SSM_W] = dga
        dglu_ref[:, SSM_W:] = dgb
        dys = _dot_nt(dga, wg_ref[:, :SSM_W]) + _dot_nt(dgb, wg_ref[:, SSM_W:])
        ds_ref[...] = dys * _gelu_grad(s_ref[...])
        gt = gate_ref[...]
        sgt = _sigmoid(gt)
        ggn = ggn_ref[...]
        for hh in range(N_HEAD):
            cols = slice(hh * HEAD_D, (hh + 1) * HEAD_D)
            ov = o_ref[:, cols]
            dlt = ov - jnp.mean(ov, axis=-1, keepdims=True)
            rstd = lax.rsqrt(jnp.mean(dlt * dlt, axis=-1, keepdims=True) + NORM_EPS)
            on = dlt * rstd
            dyr = dy_ret[:, cols] * (gt[:, cols] * sgt[:, cols])
            dgate_ref[:, cols] = dy_ret[:, cols] * (on * ggn[:, cols]) * (sgt[:, cols] * (1.0 + gt[:, cols] * (1.0 - sgt[:, cols])))
            dggn_ref[:, cols] += jnp.sum(dyr * on, axis=0, keepdims=True)
            don = dyr * ggn[:, cols]
            do = rstd * (don - jnp.mean(don, axis=-1, keepdims=True) - on * jnp.mean(don * on, axis=-1, keepdims=True))
            do_ref[:, cols] = do.astype(_BF)

    return pl.pallas_call(
        body, name="mixout_bwd", grid=(L // tm,),
        in_specs=[_row_spec(tm, D_MODEL), _full_spec((D_MODEL, D_MODEL)), _full_spec((SSM_W, 2 * SSM_W)),
                  _row_spec(tm, 2 * SSM_W), _row_spec(tm, SSM_W), _row_spec(tm, RET_W), _row_spec(tm, RET_W),
                  _full_spec((1, RET_W))],
        out_specs=[_row_spec(tm, 2 * SSM_W), _row_spec(tm, SSM_W), _row_spec(tm, RET_W), _row_spec(tm, RET_W),
                   _full_spec((1, RET_W))],
        out_shape=[jax.ShapeDtypeStruct((L, 2 * SSM_W), _BF), jax.ShapeDtypeStruct((L, SSM_W), _F32),
                   jax.ShapeDtypeStruct((L, RET_W), _F32), jax.ShapeDtypeStruct((L, RET_W), _BF),
                   jax.ShapeDtypeStruct((1, RET_W), _F32)],
        compiler_params=_params("arbitrary"),
    )(dmix, w_out, w_glu, glu, s, o, gate, ggn)


def _s5_bwd(u, ds, cin, bmat, cmat, tab_f, pw_f, tab_r, pw_r, d_skip, tb):
    L = u.shape[0]
    nt = L // tb
    seg = tb // SUBLANES
    G = KB_PER_STEP
    rcol = pl.BlockSpec((tb, G * LANES), lambda kb, t: (nt - 1 - t, kb))
    sp = _s5_specs(seg)
    aspec = pl.BlockSpec((G, SUBLANES, 2 * KB_STATES), lambda kb, t: (kb, 0, 0))

    def body(u_ref, ds_ref, cin_ref, b_ref, c_ref, tf_ref, pf_ref, tr_ref, pr_ref, d_ref,
             du_ref, db_ref, dc_ref, da_ref, dd_ref, up_scr, dp_scr, x_scr, xp_scr, g_scr, fc_scr, lc_scr):
        @pl.when(pl.program_id(1) == 0)
        def _():
            lc_scr[...] = jnp.zeros_like(lc_scr)
            db_ref[...] = jnp.zeros_like(db_ref)
            dc_ref[...] = jnp.zeros_like(dc_ref)
            da_ref[...] = jnp.zeros_like(da_ref)
            dd_ref[...] = jnp.zeros_like(dd_ref)

        _rows_to_segments(up_scr, u_ref, seg)
        _rows_to_segments(dp_scr, ds_ref, seg)
        fc_scr[...] = cin_ref[:, 0]
        for g in range(G):
            x_scr[g] = _dot(up_scr[g].astype(_BF), b_ref[g]).reshape(seg, SUBLANES, 2 * KB_STATES)
            g_scr[g] = _dot_nt(dp_scr[g].astype(_BF), c_ref[g]).reshape(seg, SUBLANES, 2 * KB_STATES)
        _scan_segments(x_scr, tf_ref, pf_ref, fc_scr, seg, reverse=False, xprev_ref=xp_scr)
        _scan_segments(g_scr, tr_ref, pr_ref, lc_scr, seg, reverse=True, xprev_ref=xp_scr, da_ref=da_ref)
        for g in range(G):
            cols = slice(g * LANES, (g + 1) * LANES)
            uv, dsv = up_scr[g], dp_scr[g]
            ub, dsb = uv.astype(_BF), dsv.astype(_BF)
            lamb = g_scr[g].reshape(tb, 2 * KB_STATES).astype(_BF)
            db_ref[g] += _dot_tn(ub, lamb)
            dc_ref[g] += _dot_tn(x_scr[g].reshape(tb, 2 * KB_STATES).astype(_BF), dsb)
            dd_ref[:, cols] += jnp.sum(dsv * uv, axis=0, keepdims=True)
            up_scr[g] = _dot_nt(lamb, b_ref[g]) + d_ref[:, cols] * dsv
        _segments_to_rows(du_ref, up_scr, seg)

    state = pltpu.VMEM((G, seg, SUBLANES, 2 * KB_STATES), _F32)
    carry = pltpu.VMEM((G, SUBLANES, 2 * KB_STATES), _F32)
    return pl.pallas_call(
        body, name="s5_bwd", grid=(N_KB // G, nt),
        in_specs=[rcol, rcol, pl.BlockSpec((G, 1, SUBLANES, 2 * KB_STATES), lambda kb, t: (kb, nt - 1 - t, 0, 0)),
                  sp["b"], sp["c"], sp["tab"], sp["pw"], sp["tab"], sp["pw"], sp["d"]],
        out_specs=[rcol, sp["b"], sp["c"], aspec, sp["d"]],
        out_shape=[jax.ShapeDtypeStruct((L, SSM_W), _F32),
                   jax.ShapeDtypeStruct((N_KB, LANES, 2 * KB_STATES), _F32),
                   jax.ShapeDtypeStruct((N_KB, 2 * KB_STATES, LANES), _F32),
                   jax.ShapeDtypeStruct((N_KB, SUBLANES, 2 * KB_STATES), _F32),
                   jax.ShapeDtypeStruct((1, SSM_W), _F32)],
        scratch_shapes=[pltpu.VMEM((G, tb, LANES), _F32)] * 2 + [state] * 3 + [carry] * 2,
        compiler_params=_params("parallel", "arbitrary"),
    )(u, ds, cin, bmat, cmat, tab_f, pw_f, tab_r, pw_r, d_skip)


def _retention_bwd(q, k, v, do, r_prev, consts, cosf, sinf):
    L = q.shape[0]
    nc = L // CHUNK
    blk = pl.BlockSpec((CHUNK, RET_W), lambda n: (nc - 1 - n, 0))
    rope_blk = pl.BlockSpec((CHUNK, HEAD_D), lambda n: (nc - 1 - n, 0))

    def body(q_ref, k_ref, v_ref, do_ref, rp_ref, dm_ref, xi_ref, zeta_ref, gc_ref, cos_ref, sin_ref,
             dq_ref, dk_ref, dv_ref, g_scr):
        @pl.when(pl.program_id(0) == 0)
        def _():
            g_scr[...] = jnp.zeros_like(g_scr)

        cs, sn = cos_ref[...], sin_ref[...]
        for hh in range(N_HEAD):
            cols = slice(hh * HEAD_D, (hh + 1) * HEAD_D)
            qv, kv, vv, dov = q_ref[:, cols], k_ref[:, cols], v_ref[:, cols], do_ref[:, cols]
            rb = rp_ref[hh, 0].astype(_BF)
            gst = g_scr[hh]
            gb = gst.astype(_BF)
            dm, zeta = dm_ref[hh], zeta_ref[hh]
            sb = (_dot_nt(qv, kv) * dm).astype(_BF)
            dab = (_dot_nt(dov, vv) * dm).astype(_BF)
            dox = (dov.astype(_F32) * xi_ref[hh]).astype(_BF)
            vz = (vv.astype(_F32) * zeta).astype(_BF)
            dq = _dot(dab, kv) + _dot_nt(dox, rb)
            dk = _dot_tn(dab, qv) + _dot_nt(vz, gb)
            dv = _dot_tn(sb, dov) + _dot(kv, gb) * zeta
            g_scr[hh] = gc_ref[hh, 0:1, :] * gst + _dot_tn(qv, dox)
            dq_ref[:, cols] = _rope_t(dq, cs, sn).astype(_BF)
            dk_ref[:, cols] = (_rope_t(dk, cs, sn) * (HEAD_D ** -0.5)).astype(_BF)
            dv_ref[:, cols] = dv.astype(_BF)

    return pl.pallas_call(
        body, name="retention_bwd", grid=(nc,),
        in_specs=[blk, blk, blk, blk, pl.BlockSpec((N_HEAD, 1, HEAD_D, HEAD_D), lambda n: (0, nc - 1 - n, 0, 0))]
        + _head_specs() + [rope_blk, rope_blk],
        out_specs=[blk, blk, blk],
        out_shape=[jax.ShapeDtypeStruct((L, RET_W), _BF)] * 3,
        scratch_shapes=[pltpu.VMEM((N_HEAD, HEAD_D, HEAD_D), _F32)],
        compiler_params=_params("arbitrary"),
    )(q, k, v, do, r_prev, *consts, cosf, sinf)


def _inproj_bwd(pieces, w_in, x, dx2, g1, tm):
    L = x.shape[0]

    def body(p0, p1, p2, p3, p4, w_ref, x_ref, dx2_ref, g_ref, dx_ref, dg_ref):
        @pl.when(pl.program_id(0) == 0)
        def _():
            dg_ref[...] = jnp.zeros_like(dg_ref)

        dh = None
        for j, p in enumerate((p0, p1, p2, p3, p4)):
            part = _dot_nt(p[...].astype(_BF), w_ref[:, j * RET_W:(j + 1) * RET_W])
            dh = part if dh is None else dh + part
        dz, dgr = _rms_bwd(x_ref[...], g_ref[...], dh)
        dx_ref[...] = dx2_ref[...] + dz
        dg_ref[...] += jnp.sum(dgr, axis=0, keepdims=True)

    return pl.pallas_call(
        body, name="inproj_bwd", grid=(L // tm,),
        in_specs=[_row_spec(tm, RET_W)] * 5 + [_full_spec((D_MODEL, IN_COLS)), _row_spec(tm, D_MODEL),
                                                 _row_spec(tm, D_MODEL), _full_spec((1, D_MODEL))],
        out_specs=[_row_spec(tm, D_MODEL), _full_spec((1, D_MODEL))],
        out_shape=[jax.ShapeDtypeStruct((L, D_MODEL), _F32), jax.ShapeDtypeStruct((1, D_MODEL), _F32)],
        compiler_params=_params("arbitrary"),
    )(*pieces, w_in, x, dx2, g1)


def _sum_adamw(parts, w, m, v, tr, name):
    _, R, Cc = parts.shape

    def body(p_ref, w_ref, m_ref, v_ref, g_ref, d_ref, nm_ref, nv_ref):
        gv = p_ref[0].astype(_F32)
        for s in range(1, N_DEV):
            gv = gv + p_ref[s].astype(_F32)
        g_ref[...] = gv
        nm = ADAM_B1 * m_ref[...] + (1.0 - ADAM_B1) * gv
        nv = ADAM_B2 * v_ref[...] + (1.0 - ADAM_B2) * (gv * gv)
        m_hat = nm / (1.0 - ADAM_B1 ** ADAM_STEP)
        v_hat = nv / (1.0 - ADAM_B2 ** ADAM_STEP)
        d_ref[...] = -ADAM_LR * (m_hat / (jnp.sqrt(v_hat) + ADAM_EPS) + ADAM_WD * w_ref[...])
        nm_ref[...] = nm
        nv_ref[...] = nv

    spec = _row_spec(tr, Cc)
    return pl.pallas_call(
        body, name=name, grid=(R // tr,),
        in_specs=[pl.BlockSpec((N_DEV, tr, Cc), lambda i: (0, i, 0))] + [spec] * 3, out_specs=[spec] * 4,
        out_shape=[jax.ShapeDtypeStruct((R, Cc), _F32)] * 4,
        compiler_params=_params("parallel"),
    )(parts, w, m, v)


def _my_place():
    return lax.axis_index("x"), lax.axis_index("y"), lax.axis_index("c")


def _all_gather(blocks):
    n = len(blocks)

    def body(*refs):
        x_refs, out_refs, done_ref = refs[:n], refs[n:2 * n], refs[2 * n]
        send_sems, recv_sems, local_sems = refs[2 * n + 1:]
        done_ref[...] = jnp.zeros_like(done_ref)
        x, y, c = _my_place()
        me, sibling = (x, y, c), (x, y, 1 - c)
        chips = [(1 - x, y), (x, 1 - y), (1 - x, 1 - y)]

        def slot(a, px, py, pc):
            return out_refs[a].at[4 * px + 2 * py + pc]

        def copy(a, k, blk, to, own=False):
            return pltpu.make_async_remote_copy(
                src_ref=x_refs[a] if own else slot(a, *blk), dst_ref=slot(a, *blk),
                send_sem=send_sems.at[a, k], recv_sem=recv_sems.at[a, k], device_id=to, device_id_type=MESH)

        mine = [pltpu.make_async_copy(x_refs[a], slot(a, *me), local_sems.at[a]) for a in range(n)]
        for cp in mine:
            cp.start()
        first = []
        for a in range(n):
            first.append(copy(a, 0, me, sibling, own=True))
            first += [copy(a, 1 + j, me, (*chip, c), own=True) for j, chip in enumerate(chips)]
        for cp in first:
            cp.start()
        passed = []
        for j, chip in enumerate(chips):
            for a in range(n):
                copy(a, 1 + j, (*chip, c), me).wait_recv()
                fwd = copy(a, 4 + j, (*chip, c), sibling)
                fwd.start()
                passed.append(fwd)
        for a in range(n):
            copy(a, 0, sibling, me).wait_recv()
            for j, chip in enumerate(chips):
                copy(a, 4 + j, (*chip, 1 - c), me).wait_recv()
        for cp in first + passed:
            cp.wait_send()
        for cp in mine:
            cp.wait()

    any_spec = pl.BlockSpec(memory_space=pl.ANY)
    outs = pl.pallas_call(
        body, name="weights_all_gather",
        in_specs=[any_spec] * n, out_specs=[any_spec] * n + [pl.BlockSpec(memory_space=pltpu.VMEM)],
        out_shape=[jax.ShapeDtypeStruct((N_DEV,) + b.shape, b.dtype) for b in blocks]
        + [jax.ShapeDtypeStruct((SUBLANES, LANES), _F32)],
        scratch_shapes=[pltpu.SemaphoreType.DMA((n, 7)), pltpu.SemaphoreType.DMA((n, 7)), pltpu.SemaphoreType.DMA((n,))],
    )(*blocks)
    return outs[:n], outs[n]


def _exchange(bigs, small):
    n = len(bigs)
    r = small.shape[0]

    def body(*refs):
        in_refs, out_refs = refs[:n + 1], refs[n + 1:2 * n + 2]
        send_sems, recv_sems, local_sems = refs[2 * n + 2:]
        x, y, c = _my_place()
        me = 4 * x + 2 * y + c
        own = [pltpu.make_async_copy(in_refs[a].at[me], out_refs[a].at[me], local_sems.at[a]) for a in range(n)]
        own.append(pltpu.make_async_copy(in_refs[n], out_refs[n].at[me], local_sems.at[n]))
        for cp in own:
            cp.start()
        copies = []
        for kk in range(1, N_DEV):
            px, py, pc = x ^ (kk >> 2), y ^ ((kk >> 1) & 1), c ^ (kk & 1)
            peer = 4 * px + 2 * py + pc
            for a in range(n + 1):
                src = in_refs[a].at[peer] if a < n else in_refs[a]
                copies.append(pltpu.make_async_remote_copy(
                    src_ref=src, dst_ref=out_refs[a].at[me],
                    send_sem=send_sems.at[a, kk - 1], recv_sem=recv_sems.at[a, kk - 1],
                    device_id=(px, py, pc), device_id_type=MESH))
        for cp in copies:
            cp.start()
        for cp in copies:
            cp.wait_recv()
        for cp in copies:
            cp.wait_send()
        for cp in own:
            cp.wait()

    any_spec = pl.BlockSpec(memory_space=pl.ANY)
    outs = pl.pallas_call(
        body, name="grad_exchange",
        in_specs=[any_spec] * (n + 1), out_specs=[any_spec] * (n + 1),
        out_shape=[jax.ShapeDtypeStruct(b.shape, b.dtype) for b in bigs]
        + [jax.ShapeDtypeStruct((N_DEV, r, LANES), small.dtype)],
        scratch_shapes=[pltpu.SemaphoreType.DMA((n + 1, 7)), pltpu.SemaphoreType.DMA((n + 1, 7)),
                        pltpu.SemaphoreType.DMA((n + 1,))],
    )(*bigs, small)
    return outs[:n], outs[n]


HBM_SPEC = pl.BlockSpec(memory_space=pltpu.HBM)
SEM_SPEC = pl.BlockSpec(memory_space=pltpu.SEMAPHORE)
DATAFLOW = pltpu.SideEffectType.DATAFLOW_SIDE_EFFECTING


def _my_index():
    x, y, c = _my_place()
    return 4 * x + 2 * y + c


def _landing(own_block):
    zone = lax.empty((N_DEV,) + own_block.shape, own_block.dtype)
    return lax.dynamic_update_index_in_dim(zone, own_block, _my_index(), 0)


def _split_copies(src_refs, land_refs, send_sems, recv_sems, gather):
    x, y, c = _my_place()
    me = 4 * x + 2 * y + c
    copies = []
    for kk in range(1, N_DEV):
        px, py, pc = x ^ (kk >> 2), y ^ ((kk >> 1) & 1), c ^ (kk & 1)
        peer = 4 * px + 2 * py + pc
        for a, (src, land) in enumerate(zip(src_refs, land_refs)):
            copies.append(pltpu.make_async_remote_copy(
                src_ref=src if gather else src.at[peer], dst_ref=land.at[me],
                send_sem=send_sems.at[a * 7 + kk - 1], recv_sem=recv_sems.at[a * 7 + kk - 1],
                device_id=(px, py, pc), device_id_type=MESH))
    return copies


def _split_start(srcs, lands, gather, name):
    n = len(srcs)

    def body(*refs):
        src_refs, land_refs = refs[:n], refs[n:2 * n]
        send_sems, recv_sems = refs[2 * n], refs[2 * n + 1]
        token = refs[-1]
        for cp in _split_copies(src_refs, land_refs, send_sems, recv_sems, gather):
            cp.start()
        token[...] = jnp.zeros_like(token)

    outs = pl.pallas_call(
        body, name=name,
        out_shape=(pltpu.SemaphoreType.DMA((7 * n,)), pltpu.SemaphoreType.DMA((7 * n,)),
                   *[pltpu.HBM(t.shape, t.dtype) for t in srcs], *[pltpu.HBM(t.shape, t.dtype) for t in lands],
                   jax.ShapeDtypeStruct((SUBLANES, LANES), _F32)),
        in_specs=[HBM_SPEC] * (2 * n),
        out_specs=(SEM_SPEC, SEM_SPEC, *[HBM_SPEC] * (2 * n), pl.BlockSpec(memory_space=pltpu.VMEM)),
        input_output_aliases={i: 2 + i for i in range(2 * n)},
        compiler_params=pltpu.CompilerParams(has_side_effects=DATAFLOW),
    )(*[pltpu.with_memory_space_constraint(t, pltpu.HBM) for t in list(srcs) + list(lands)])
    return outs[0], outs[1], outs[2:2 + n], outs[2 + n:2 + 2 * n], outs[-1]


def _split_wait(send_sems, recv_sems, srcs, lands, after, gather, name):
    n = len(srcs)

    def body(*refs):
        src_refs, land_refs = refs[:n], refs[n:2 * n]
        send_s, recv_s = refs[2 * n], refs[2 * n + 1]
        for cp in _split_copies(src_refs, land_refs, send_s, recv_s, gather):
            cp.wait_send()
            cp.wait_recv()

    outs = pl.pallas_call(
        body, name=name,
        out_shape=tuple(pltpu.HBM(t.shape, t.dtype) for t in list(srcs) + list(lands)),
        in_specs=[HBM_SPEC] * (2 * n) + [SEM_SPEC, SEM_SPEC, pl.BlockSpec(memory_space=pl.ANY)],
        out_specs=tuple([HBM_SPEC] * (2 * n)),
        input_output_aliases={i: i for i in range(2 * n)},
        compiler_params=pltpu.CompilerParams(has_side_effects=DATAFLOW),
    )(*srcs, *lands, send_sems, recv_sems, after)
    return outs[n:]


def _discretize(lam_re, lam_im, log_dt, b_re, b_im):
    lr = jnp.minimum(lam_re, -1e-4)
    li = lam_im
    dt = jnp.exp(log_dt)[:, None]
    er = jnp.exp(lr * dt)
    ar, ai = er * jnp.cos(li * dt), er * jnp.sin(li * dt)
    den = lr * lr + li * li
    cr = ((ar - 1.0) * lr + ai * li) / den
    ci = (ai * lr - (ar - 1.0) * li) / den
    bbr = cr[:, :, None] * b_re - ci[:, :, None] * b_im
    bbi = cr[:, :, None] * b_im + ci[:, :, None] * b_re
    return ar, ai, bbr, bbi


def _cmul(ar, ai, br, bi):
    return ar * br - ai * bi, ar * bi + ai * br


def _cpowers(ar, ai, n):
    pr, pi = ar[None], ai[None]
    while pr.shape[0] < n:
        nr, ni = _cmul(pr, pi, pr[-1][None], pi[-1][None])
        pr, pi = jnp.concatenate([pr, nr]), jnp.concatenate([pi, ni])
    return pr[:n], pi[:n]


def _scan_tables(ar, ai, seg, reverse):
    if reverse:
        ai = -ai
    ar, ai = ar.reshape(N_KB, KB_STATES), ai.reshape(N_KB, KB_STATES)
    pr, pi = _cpowers(ar, ai, seg)
    a1 = (pr[-1], pi[-1])
    a2 = _cmul(*a1, *a1)
    a4 = _cmul(*a2, *a2)
    row = jnp.arange(SUBLANES)[None, :, None]
    wide = lambda t: jnp.broadcast_to(t[:, None, :], (N_KB, SUBLANES, KB_STATES))
    tabs = [wide(ar), wide(ai)]
    for dist, (qr, qi) in ((1, a1), (2, a2), (4, a4)):
        keep = (row < SUBLANES - dist) if reverse else (row >= dist)
        tabs += [jnp.where(keep, wide(qr), 0.0), jnp.where(keep, wide(qi), 0.0)]
    tabs += [wide(a1[0]), wide(a1[1])]
    if reverse:
        pr, pi = pr[::-1], pi[::-1]
    pw = jnp.transpose(jnp.concatenate([pr, pi], axis=-1), (1, 0, 2))[:, :, None, :]
    return jnp.stack(tabs, axis=1).astype(_F32), pw.astype(_F32)


def _block_diag_in(br, bi):
    eye = jnp.eye(GROUPS_PER_KB, dtype=_F32)
    one = lambda t: jnp.einsum("kgpc,gh->kgchp", t.reshape(N_KB, GROUPS_PER_KB, N_STATE, SSM_GC), eye).reshape(
        N_KB, LANES, KB_STATES)
    return jnp.concatenate([one(br), one(bi)], axis=-1)


def _block_diag_in_t(dmat):
    d6 = dmat.reshape(N_KB, GROUPS_PER_KB, SSM_GC, 2, GROUPS_PER_KB, N_STATE)
    eye = jnp.eye(GROUPS_PER_KB, dtype=_F32)
    both = jnp.einsum("kgcrhp,gh->rkgpc", d6, eye).reshape(2, N_GROUP, N_STATE, SSM_GC)
    return both[0], both[1]


def _block_diag_out(c_re, c_im):
    eye = jnp.eye(GROUPS_PER_KB, dtype=_F32)
    one = lambda t: jnp.einsum("kgcp,gh->khpgc", t.reshape(N_KB, GROUPS_PER_KB, SSM_GC, N_STATE), eye).reshape(
        N_KB, KB_STATES, LANES)
    return jnp.concatenate([one(c_re), -one(c_im)], axis=1)


def _block_diag_out_t(dmat):
    d6 = dmat.reshape(N_KB, 2, GROUPS_PER_KB, N_STATE, GROUPS_PER_KB, SSM_GC)
    eye = jnp.eye(GROUPS_PER_KB, dtype=_F32)
    both = jnp.einsum("krhpgc,gh->rkgcp", d6, eye).reshape(2, N_GROUP, SSM_GC, N_STATE)
    return both[0], -both[1]


SMALL_NAMES = ("norm_mix_pre", "norm_mix_post", "ret_gn_gain", "ssm_lambda_re", "ssm_lambda_im", "ssm_log_dt",
               "ssm_b_re", "ssm_b_im", "ssm_c_re", "ssm_c_im", "ssm_d", "norm_mlp_pre", "norm_mlp_post")


def _local_grads(x, tgt, small, w_in, late_weights, emit, tm, tk, tb):
    L = x.shape[0]
    g1, g2, ggn = small["norm_mix_pre"], small["norm_mix_post"], small["ret_gn_gain"]
    g3, g4, d_skip = small["norm_mlp_pre"], small["norm_mlp_post"], small["ssm_d"]

    half = HEAD_D // 2
    inv_freq = ROPE_BASE ** (-jnp.arange(half, dtype=_F32) / half)
    ang = jnp.arange(L, dtype=_F32)[:, None] * inv_freq[None, :]
    cosf = jnp.concatenate([jnp.cos(ang), jnp.cos(ang)], axis=-1)
    sinf = jnp.concatenate([-jnp.sin(ang), jnp.sin(ang)], axis=-1)
    consts = _ret_consts()

    disc_in = (small["ssm_lambda_re"][0], small["ssm_lambda_im"][0], small["ssm_log_dt"][0],
               small["ssm_b_re"][0], small["ssm_b_im"][0])
    (ar, ai, bbr, bbi), disc_vjp = jax.vjp(_discretize, *disc_in)
    bmat = _block_diag_in(bbr, bbi).astype(_BF)
    cmat = _block_diag_out(small["ssm_c_re"][0], small["ssm_c_im"][0]).astype(_BF)
    seg = tb // SUBLANES
    tab_f, pw_f = _scan_tables(ar, ai, seg, False)
    tab_r, pw_r = _scan_tables(ar, ai, seg, True)

    h1, q, k, v, gate, u = _inproj_fwd(x, g1, w_in, cosf, sinf, tm)
    o, y_ret, r_prev = _retention_fwd(q, k, v, gate, ggn, consts)
    s, cin = _s5_fwd(u, bmat, cmat, tab_f, pw_f, d_skip, tb)
    w_glu, w_out, w_ff1, w_ff2 = late_weights(s)
    ys, glu, cat, mix, x2 = _mixout_fwd(s, y_ret, x, w_glu, w_out, g2, tm)
    h3, f1 = _ff1_fwd(x2, g3, w_ff1, tm)
    dy, dm, dg4, sq = _ff2_loss(f1, x2, tgt, g4, w_ff2, tm)

    df1, dw_ff2 = _ff2_bwd(dm, f1, w_ff2, min(1024, L), 1024)
    dx2, dmix, dg3, dg2 = _ff1_bwd(df1, w_ff1, x2, mix, dy, g3, g2, tm)
    dw_ff1 = _matmul_tn(h3, df1, tk, FF1_COLS, "dw_ff1", slots=True)
    zero = emit({"w_ff1": dw_ff1, "w_ff2": dw_ff2})
    dglu, ds, dgate, do, dggn = _mixout_bwd(dmix, w_out, w_glu, glu, s, o, gate, ggn if zero is None else ggn + zero, tm)
    dw_out = _matmul_tn(cat, dmix, tk, 1024, "dw_out")
    dw_glu = _matmul_tn(ys, dglu, tk, 1024, "dw_glu")
    zero = emit({"w_glu": dw_glu, "w_out": dw_out})
    du, dbmat, dcmat, da8, dd = _s5_bwd(u, ds, cin, bmat, cmat, tab_f, pw_f, tab_r, pw_r,
                                        d_skip if zero is None else d_skip + zero, tb)
    dq, dk, dv = _retention_bwd(q, k, v, do, r_prev, consts, cosf, sinf)
    pieces = (dq, dk, dv, dgate, du)
    dw_in = jnp.concatenate([_matmul_tn(h1, p, tk, RET_W, "dw_in_%d" % j) for j, p in enumerate(pieces)], axis=1)
    zero = emit({"w_in": dw_in})
    gx, dg1 = _inproj_bwd(pieces, w_in, x, dx2, g1 if zero is None else g1 + zero, tm)

    da = jnp.sum(da8, axis=1)
    dar = da[:, :KB_STATES].reshape(N_GROUP, N_STATE)
    dai = da[:, KB_STATES:].reshape(N_GROUP, N_STATE)
    dbr, dbi = _block_diag_in_t(dbmat)
    dlre, dlim, dldt, dbre, dbim = disc_vjp((dar, dai, dbr, dbi))
    dcre, dcim = _block_diag_out_t(dcmat)

    gsmall = {
        "norm_mix_pre": dg1, "norm_mix_post": dg2, "ret_gn_gain": dggn,
        "ssm_lambda_re": dlre[None], "ssm_lambda_im": dlim[None], "ssm_log_dt": dldt[None],
        "ssm_b_re": dbre[None], "ssm_b_im": dbim[None], "ssm_c_re": dcre[None], "ssm_c_im": dcim[None],
        "ssm_d": dd, "norm_mlp_pre": dg3, "norm_mlp_post": dg4,
    }
    return sq, gx, gsmall


BIG_SHAPES = {"w_in": (D_MODEL, IN_COLS // N_DEV), "w_glu": (SSM_W, 2 * SSM_W // N_DEV), "w_out": (D_MODEL // N_DEV, D_MODEL),
              "w_ff1": (D_MODEL, FF1_COLS), "w_ff2": (D_FF // N_DEV, D_MODEL)}
BIG_NAMES = ("w_in", "w_glu", "w_out", "w_ff1", "w_ff2")


def _cols_from_slots(g):
    return jnp.transpose(g, (1, 0, 2)).reshape(g.shape[1], N_DEV * g.shape[2])


def _cols_to_slots(dw):
    r, cols = dw.shape
    return jnp.transpose(dw.reshape(r, N_DEV, cols // N_DEV), (1, 0, 2))


LATE_NAMES = ("w_glu", "w_out", "w_ff1", "w_ff2")


def _grad_slots(name, dw):
    if name in ("w_in", "w_glu"):
        return _cols_to_slots(dw)
    if name == "w_ff1":
        return dw
    return dw.reshape((N_DEV,) + BIG_SHAPES[name])


PIECE_ROWS = 8


def _small_layout(shapes):
    off, rows = {}, 0
    for n in SMALL_NAMES:
        off[n] = rows
        rows += -(-math.prod(shapes[n]) // (PIECE_ROWS * LANES)) * PIECE_ROWS
    return off, rows, rows + PIECE_ROWS


def _pack_small(vals, shapes, last=None):
    parts = []
    for n in SMALL_NAMES:
        flat = vals[n].reshape(-1).astype(_F32)
        pad = -flat.shape[0] % (PIECE_ROWS * LANES)
        if pad:
            flat = jnp.concatenate([flat, jnp.zeros((pad,), _F32)])
        parts.append(flat.reshape(-1, LANES))
    parts.append(jnp.zeros((PIECE_ROWS, LANES), _F32) if last is None else last)
    return jnp.concatenate(parts, axis=0)


def _unpack_small(buf, shapes):
    off, _, _ = _small_layout(shapes)
    out = {}
    for n in SMALL_NAMES:
        size = math.prod(shapes[n])
        rows = -(-size // LANES)
        out[n] = buf[off[n]:off[n] + rows].reshape(-1)[:size].reshape(shapes[n])
    return out


WEIGHT_NAMES = ('norm_mix_pre', 'norm_mix_post', 'w_in', 'ret_gn_gain', 'ssm_lambda_re', 'ssm_lambda_im', 'ssm_log_dt',
                'ssm_b_re', 'ssm_b_im', 'ssm_c_re', 'ssm_c_im', 'ssm_d', 'w_glu', 'w_out', 'norm_mlp_pre',
                'norm_mlp_post', 'w_ff1', 'w_ff2')


def kernel(x, norm_mix_pre, norm_mix_post, w_in, ret_gn_gain, ssm_lambda_re, ssm_lambda_im, ssm_log_dt, ssm_b_re, ssm_b_im, ssm_c_re, ssm_c_im, ssm_d, w_glu, w_out, norm_mlp_pre, norm_mlp_post, w_ff1, w_ff2, loss_target, m_norm_mix_pre, m_norm_mix_post, m_w_in, m_ret_gn_gain, m_ssm_lambda_re, m_ssm_lambda_im, m_ssm_log_dt, m_ssm_b_re, m_ssm_b_im, m_ssm_c_re, m_ssm_c_im, m_ssm_d, m_w_glu, m_w_out, m_norm_mlp_pre, m_norm_mlp_post, m_w_ff1, m_w_ff2, v_norm_mix_pre, v_norm_mix_post, v_w_in, v_ret_gn_gain, v_ssm_lambda_re, v_ssm_lambda_im, v_ssm_log_dt, v_ssm_b_re, v_ssm_b_im, v_ssm_c_re, v_ssm_c_im, v_ssm_d, v_w_glu, v_w_out, v_norm_mlp_pre, v_norm_mlp_post, v_w_ff1, v_w_ff2):
    args = dict(locals())
    w = {n: args[n] for n in WEIGHT_NAMES}
    m = {n: args["m_" + n] for n in WEIGHT_NAMES}
    v = {n: args["v_" + n] for n in WEIGHT_NAMES}
    L = x.shape[1]
    tm = min(256, L)
    tk = min(2048, L)
    tb = min(512, L)

    (w_in_slots,), gathered_zero = _all_gather([w["w_in"][0].astype(_BF)])
    w_in_full = _cols_from_slots(w_in_slots)
    late_blocks = [w[n][0].astype(_BF) for n in LATE_NAMES]
    late_blocks[0] = late_blocks[0] + gathered_zero[0, 0].astype(_BF)
    late = _split_start(late_blocks, [_landing(b) for b in late_blocks], True, "late_weights_start")

    def late_weights(after):
        g = dict(zip(LATE_NAMES, _split_wait(*late[:4], after, True, "late_weights_wait")))
        return (_cols_from_slots(g["w_glu"]), g["w_out"].reshape(D_MODEL, D_MODEL), g["w_ff1"],
                g["w_ff2"].reshape(D_FF, D_MODEL))

    in_flight = []

    def emit(dws):
        names = sorted(dws)
        srcs = [_grad_slots(n, dws[n]) for n in names]
        lands = [_landing(lax.dynamic_index_in_dim(t, _my_index(), 0, keepdims=False)) for t in srcs]
        started = _split_start(srcs, lands, False, "grads_start_" + "_".join(names))
        in_flight.append((names, started))
        return started[4][0, 0]

    small_w = {n: w[n] for n in SMALL_NAMES}
    small_w["norm_mix_pre"] = small_w["norm_mix_pre"] + late[4][0, 0]
    sq, gx, gsmall = _local_grads(x[0], loss_target[0], small_w, w_in_full, late_weights, emit, tm, tk, tb)

    grads, delta, new_m, new_v = {}, {}, {}, {}

    def finish(names, started, after):
        landed = _split_wait(*started[:4], after, False, "grads_wait_" + "_".join(names))
        for n, parts in zip(names, landed):
            res = _sum_adamw(parts, w[n][0], m[n][0], v[n][0], min(256, BIG_SHAPES[n][0]), "adamw_" + n)
            grads[n], delta[n], new_m[n], new_v[n] = (t[None] for t in res)
        return res[1]

    after = gx
    for names, started in in_flight[:-1]:
        after = finish(names, started, after)

    shapes = {n: w[n].shape for n in SMALL_NAMES}
    loss_rows = jnp.broadcast_to(0.5 / D_MODEL * jnp.sum(sq), (PIECE_ROWS, LANES)).astype(_F32)
    _, small_parts = _exchange([], _pack_small(gsmall, shapes, loss_rows))
    finish(*in_flight[-1], small_parts)
    sw, sm, sv = _pack_small(w, shapes), _pack_small(m, shapes), _pack_small(v, shapes)
    res = _sum_adamw(small_parts, sw, sm, sv, sw.shape[0], "adamw_small")
    for dst, buf in zip((grads, delta, new_m, new_v), res):
        dst.update(_unpack_small(buf, shapes))
    _, loss_at, _ = _small_layout(shapes)
    loss = res[0][loss_at, 0]

    return (loss, gx[None], *[grads[n] for n in WEIGHT_NAMES], *[delta[n] for n in WEIGHT_NAMES],
            *[new_m[n] for n in WEIGHT_NAMES], *[new_v[n] for n in WEIGHT_NAMES])
```

```python
import math

import jax
import jax.numpy as jnp
from jax import lax
from jax.experimental import pallas as pl
from jax.experimental.pallas import tpu as pltpu

_BF = jnp.bfloat16
_F32 = jnp.float32

D_MODEL = 1024
RET_W = 512
N_HEAD = 4
HEAD_D = 128
CHUNK = 128
SSM_W = 512
SSM_GC = 16
N_GROUP = 32
N_STATE = 64
GROUPS_PER_KB = 8
N_KB = 4
KB_STATES = GROUPS_PER_KB * N_STATE
D_FF = 4096
IN_COLS = 2560
NORM_EPS = 1e-6
ROPE_BASE = 10000.0
N_DEV = 8

ADAM_LR = 0.001
ADAM_B1 = 0.9
ADAM_B2 = 0.999
ADAM_EPS = 1e-08
ADAM_WD = 0.01
ADAM_STEP = 10

SUBLANES = 8
LANES = 128
VMEM_LIMIT = 52 * 1024 * 1024
KB_PER_STEP = 2
SCAN_UNROLL = 2

MESH = pl.DeviceIdType.MESH


def _params(*sem):
    return pltpu.CompilerParams(dimension_semantics=sem, vmem_limit_bytes=VMEM_LIMIT)


def _dot(a, b):
    return jnp.dot(a, b, preferred_element_type=_F32)


def _dot_nt(a, b):
    return lax.dot_general(a, b, (((1,), (1,)), ((), ())), preferred_element_type=_F32)


def _dot_tn(a, b):
    return lax.dot_general(a, b, (((0,), (0,)), ((), ())), preferred_element_type=_F32)


def _rms_r(z):
    return lax.rsqrt(jnp.mean(z * z, axis=-1, keepdims=True) + NORM_EPS)


def _rms_bwd(z, g, dn):
    r = _rms_r(z)
    t = dn * g
    dz = r * t - z * (r * r * r * jnp.mean(t * z, axis=-1, keepdims=True))
    return dz, dn * z * r


def _rope(t, cs, sn):
    return t * cs + pltpu.roll(t, HEAD_D // 2, 1) * sn


def _rope_t(t, cs, sn):
    return t * cs - pltpu.roll(t, HEAD_D // 2, 1) * sn


def _sigmoid(z):
    return 1.0 / (1.0 + jnp.exp(-z))


_GELU_C = math.sqrt(2.0 / math.pi)


def _gelu(z):
    return 0.5 * z * (1.0 + jnp.tanh(_GELU_C * (z + 0.044715 * z * z * z)))


def _gelu_grad(z):
    th = jnp.tanh(_GELU_C * (z + 0.044715 * z * z * z))
    return 0.5 * (1.0 + th) + 0.5 * z * (1.0 - th * th) * _GELU_C * (1.0 + 3 * 0.044715 * z * z)


def _row_spec(tm, n):
    return pl.BlockSpec((tm, n), lambda i: (i, 0))


def _full_spec(shape):
    nd = len(shape)
    return pl.BlockSpec(shape, lambda *_: (0,) * nd)


def _inproj_fwd(x, g1, w_in, cosf, sinf, tm):
    L = x.shape[0]

    def body(x_ref, g_ref, w_ref, cos_ref, sin_ref, h_ref, q_ref, k_ref, v_ref, gate_ref, u_ref):
        xv = x_ref[...]
        h = (xv * _rms_r(xv) * g_ref[...]).astype(_BF)
        h_ref[...] = h
        proj = _dot(h, w_ref[...])
        cs, sn = cos_ref[...], sin_ref[...]
        for hh in range(N_HEAD):
            lo = hh * HEAD_D
            q_ref[:, lo:lo + HEAD_D] = _rope(proj[:, lo:lo + HEAD_D], cs, sn).astype(_BF)
            kh = _rope(proj[:, RET_W + lo:RET_W + lo + HEAD_D], cs, sn) * (HEAD_D ** -0.5)
            k_ref[:, lo:lo + HEAD_D] = kh.astype(_BF)
        v_ref[...] = proj[:, 2 * RET_W:3 * RET_W].astype(_BF)
        gate_ref[...] = proj[:, 3 * RET_W:4 * RET_W]
        u_ref[...] = proj[:, 4 * RET_W:]

    return pl.pallas_call(
        body, name="inproj_fwd", grid=(L // tm,),
        in_specs=[_row_spec(tm, D_MODEL), _full_spec((1, D_MODEL)), _full_spec((D_MODEL, IN_COLS)),
                  _row_spec(tm, HEAD_D), _row_spec(tm, HEAD_D)],
        out_specs=[_row_spec(tm, D_MODEL)] + [_row_spec(tm, RET_W)] * 5,
        out_shape=[jax.ShapeDtypeStruct((L, D_MODEL), _BF)] + [jax.ShapeDtypeStruct((L, RET_W), _BF)] * 3
        + [jax.ShapeDtypeStruct((L, RET_W), _F32)] * 2,
        compiler_params=_params("parallel"),
    )(x, g1, w_in, cosf, sinf)


def _ret_consts():
    lg = jnp.log(1.0 - jnp.exp(jnp.linspace(math.log(1.0 / 32), math.log(1.0 / 512), N_HEAD))).astype(_F32)
    idx = jnp.arange(CHUNK, dtype=_F32)
    diff = idx[:, None] - idx[None, :]
    decay = jnp.where(diff[None] >= 0, jnp.exp(jnp.maximum(diff, 0.0)[None] * lg[:, None, None]), 0.0)
    zeta = jnp.exp((CHUNK - 1 - idx)[None, :] * lg[:, None])
    xi = jnp.exp((idx + 1.0)[None, :] * lg[:, None])
    gc = jnp.exp(CHUNK * lg)
    wide = lambda t: jnp.broadcast_to(t[:, :, None], (N_HEAD, CHUNK, HEAD_D)).astype(_F32)
    gcw = jnp.broadcast_to(gc[:, None, None], (N_HEAD, SUBLANES, HEAD_D)).astype(_F32)
    return decay.astype(_F32), wide(xi), wide(zeta), gcw


def _head_specs():
    c3 = _full_spec((N_HEAD, CHUNK, CHUNK))
    return [c3, c3, c3, _full_spec((N_HEAD, SUBLANES, HEAD_D))]


def _retention_fwd(q, k, v, gate, ggn, consts):
    L = q.shape[0]
    nc = L // CHUNK
    blk = pl.BlockSpec((CHUNK, RET_W), lambda n: (n, 0))

    def body(q_ref, k_ref, v_ref, gate_ref, ggn_ref, dm_ref, xi_ref, zeta_ref, gc_ref,
             o_ref, y_ref, rp_ref, r_scr):
        @pl.when(pl.program_id(0) == 0)
        def _():
            r_scr[...] = jnp.zeros_like(r_scr)

        for hh in range(N_HEAD):
            cols = slice(hh * HEAD_D, (hh + 1) * HEAD_D)
            qv, kv, vv = q_ref[:, cols], k_ref[:, cols], v_ref[:, cols]
            r_prev = r_scr[hh]
            s = _dot_nt(qv, kv) * dm_ref[hh]
            o = _dot(s.astype(_BF), vv) + _dot(qv, r_prev.astype(_BF)) * xi_ref[hh]
            o_ref[:, cols] = o
            rp_ref[hh, 0] = r_prev
            vz = (vv.astype(_F32) * zeta_ref[hh]).astype(_BF)
            r_scr[hh] = gc_ref[hh, 0:1, :] * r_prev + _dot_tn(kv, vz)
            dlt = o - jnp.mean(o, axis=-1, keepdims=True)
            on = dlt * lax.rsqrt(jnp.mean(dlt * dlt, axis=-1, keepdims=True) + NORM_EPS)
            gt = gate_ref[:, cols]
            y_ref[:, cols] = (gt * _sigmoid(gt) * (on * ggn_ref[:, cols])).astype(_BF)

    return pl.pallas_call(
        body, name="retention_fwd", grid=(nc,),
        in_specs=[blk, blk, blk, blk, _full_spec((1, RET_W))] + _head_specs(),
        out_specs=[blk, blk, pl.BlockSpec((N_HEAD, 1, HEAD_D, HEAD_D), lambda n: (0, n, 0, 0))],
        out_shape=[jax.ShapeDtypeStruct((L, RET_W), _F32), jax.ShapeDtypeStruct((L, RET_W), _BF),
                   jax.ShapeDtypeStruct((N_HEAD, nc, HEAD_D, HEAD_D), _F32)],
        scratch_shapes=[pltpu.VMEM((N_HEAD, HEAD_D, HEAD_D), _F32)],
        compiler_params=_params("arbitrary"),
    )(q, k, v, gate, ggn, *consts)


def _rows_to_segments(dst_scr, src_ref, seg):
    for g in range(dst_scr.shape[0]):
        for j in range(SUBLANES):
            dst_scr[g, pl.ds(j, seg, stride=SUBLANES), :] = src_ref[pl.ds(j * seg, seg), g * LANES:(g + 1) * LANES]


def _segments_to_rows(dst_ref, src_scr, seg):
    for g in range(src_scr.shape[0]):
        for j in range(SUBLANES):
            dst_ref[pl.ds(j * seg, seg), g * LANES:(g + 1) * LANES] = src_scr[g, pl.ds(j, seg, stride=SUBLANES), :]


def _scan_segments(x_ref, tab_ref, pw_ref, carry_ref, seg, reverse, xprev_ref=None, da_ref=None):
    G = x_ref.shape[0]
    W = KB_STATES
    re, im = pl.ds(0, W), pl.ds(W, W)
    row_id = lax.broadcasted_iota(jnp.int32, (SUBLANES, W), 0)
    edge_in = (row_id == SUBLANES - 1) if reverse else (row_id == 0)
    edge_out = 0 if reverse else SUBLANES - 1
    a_tab = [(tab_ref[g, 0], tab_ref[g, 1]) for g in range(G)]

    def local(i, st):
        r = (seg - 1 - i) if reverse else i
        out = []
        for g in range(G):
            (ar, ai), (sr, si) = a_tab[g], st[g]
            nr = ar * sr - ai * si + x_ref[g, r, :, re]
            ni = ar * si + ai * sr + x_ref[g, r, :, im]
            x_ref[g, r, :, re] = nr
            x_ref[g, r, :, im] = ni
            out.append((nr, ni))
        return tuple(out)

    zero = jnp.zeros((SUBLANES, W), _F32)
    ends = lax.fori_loop(0, seg, local, tuple((zero, zero) for _ in range(G)), unroll=SCAN_UNROLL)

    entry = []
    shift = (SUBLANES - 1) if reverse else 1
    for g in range(G):
        er, ei = ends[g]
        fr = jnp.where(edge_in, carry_ref[g, :, re], pltpu.roll(er, shift, 0))
        fi = jnp.where(edge_in, carry_ref[g, :, im], pltpu.roll(ei, shift, 0))
        for j, dist in enumerate((1, 2, 4)):
            pr, pi = tab_ref[g, 2 + 2 * j], tab_ref[g, 3 + 2 * j]
            sh = (SUBLANES - dist) if reverse else dist
            sr, si = pltpu.roll(fr, sh, 0), pltpu.roll(fi, sh, 0)
            fr, fi = fr + pr * sr - pi * si, fi + pr * si + pi * sr
        br, bi = tab_ref[g, 8], tab_ref[g, 9]
        outr = br * fr - bi * fi + er
        outi = br * fi + bi * fr + ei
        carry_ref[g, :, re] = jnp.broadcast_to(outr[edge_out:edge_out + 1, :], (SUBLANES, W))
        carry_ref[g, :, im] = jnp.broadcast_to(outi[edge_out:edge_out + 1, :], (SUBLANES, W))
        entry.append((fr, fi))

    keep_prev = xprev_ref is not None and not reverse
    add_da = da_ref is not None

    def fix(r, st):
        out = []
        for g in range(G):
            fr, fi = entry[g]
            pwr, pwi = pw_ref[g, r, :, re], pw_ref[g, r, :, im]
            xr = x_ref[g, r, :, re] + (pwr * fr - pwi * fi)
            xi = x_ref[g, r, :, im] + (pwr * fi + pwi * fr)
            x_ref[g, r, :, re] = xr
            x_ref[g, r, :, im] = xi
            if keep_prev:
                xprev_ref[g, r, :, re] = st[g][0]
                xprev_ref[g, r, :, im] = st[g][1]
                out.append((xr, xi))
            elif add_da:
                xpr, xpi = xprev_ref[g, r, :, re], xprev_ref[g, r, :, im]
                out.append((st[g][0] + (xr * xpr + xi * xpi), st[g][1] + (xi * xpr - xr * xpi)))
            else:
                out.append(st[g])
        return tuple(out)

    if keep_prev:
        init = tuple(entry)
    elif add_da:
        init = tuple((zero, zero) for _ in range(G))
    else:
        init = tuple((zero[0:1, 0:LANES], zero[0:1, 0:LANES]) for _ in range(G))
    st = lax.fori_loop(0, seg, fix, init, unroll=SCAN_UNROLL)
    if add_da:
        for g in range(G):
            da_ref[g, :, re] += st[g][0]
            da_ref[g, :, im] += st[g][1]


def _s5_specs(seg):
    G = KB_PER_STEP
    return dict(
        b=pl.BlockSpec((G, LANES, 2 * KB_STATES), lambda kb, t: (kb, 0, 0)),
        c=pl.BlockSpec((G, 2 * KB_STATES, LANES), lambda kb, t: (kb, 0, 0)),
        tab=pl.BlockSpec((G, 10, SUBLANES, KB_STATES), lambda kb, t: (kb, 0, 0, 0)),
        pw=pl.BlockSpec((G, seg, 1, 2 * KB_STATES), lambda kb, t: (kb, 0, 0, 0)),
        d=pl.BlockSpec((1, G * LANES), lambda kb, t: (0, kb)),
    )


def _s5_fwd(u, bmat, cmat, tab_f, pw_f, d_skip, tb):
    L = u.shape[0]
    nt = L // tb
    seg = tb // SUBLANES
    G = KB_PER_STEP
    ucol = pl.BlockSpec((tb, G * LANES), lambda kb, t: (t, kb))
    sp = _s5_specs(seg)

    def body(u_ref, b_ref, c_ref, tab_ref, pw_ref, d_ref, s_ref, cin_ref, up_scr, y_scr, x_scr, carry_scr):
        @pl.when(pl.program_id(1) == 0)
        def _():
            carry_scr[...] = jnp.zeros_like(carry_scr)

        cin_ref[:, 0] = carry_scr[...]
        _rows_to_segments(up_scr, u_ref, seg)
        for g in range(G):
            x_scr[g] = _dot(up_scr[g].astype(_BF), b_ref[g]).reshape(seg, SUBLANES, 2 * KB_STATES)
        _scan_segments(x_scr, tab_ref, pw_ref, carry_scr, seg, reverse=False)
        for g in range(G):
            y = _dot(x_scr[g].reshape(tb, 2 * KB_STATES).astype(_BF), c_ref[g])
            y_scr[g] = y + d_ref[:, g * LANES:(g + 1) * LANES] * up_scr[g]
        _segments_to_rows(s_ref, y_scr, seg)

    return pl.pallas_call(
        body, name="s5_fwd", grid=(N_KB // G, nt),
        in_specs=[ucol, sp["b"], sp["c"], sp["tab"], sp["pw"], sp["d"]],
        out_specs=[ucol, pl.BlockSpec((G, 1, SUBLANES, 2 * KB_STATES), lambda kb, t: (kb, t, 0, 0))],
        out_shape=[jax.ShapeDtypeStruct((L, SSM_W), _F32),
                   jax.ShapeDtypeStruct((N_KB, nt, SUBLANES, 2 * KB_STATES), _F32)],
        scratch_shapes=[pltpu.VMEM((G, tb, LANES), _F32)] * 2
        + [pltpu.VMEM((G, seg, SUBLANES, 2 * KB_STATES), _F32), pltpu.VMEM((G, SUBLANES, 2 * KB_STATES), _F32)],
        compiler_params=_params("parallel", "arbitrary"),
    )(u, bmat, cmat, tab_f, pw_f, d_skip)


def _mixout_fwd(s, y_ret, x, w_glu, w_out, g2, tm):
    L = s.shape[0]

    def body(s_ref, yr_ref, x_ref, wg_ref, wo_ref, g_ref, ys_ref, glu_ref, cat_ref, mix_ref, x2_ref):
        ys = _gelu(s_ref[...]).astype(_BF)
        ys_ref[...] = ys
        glu = _dot(ys, wg_ref[...])
        glu_ref[...] = glu
        cat_ref[:, :RET_W] = yr_ref[...]
        cat_ref[:, RET_W:] = (glu[:, :SSM_W] * _sigmoid(glu[:, SSM_W:])).astype(_BF)
        mix = _dot(cat_ref[...], wo_ref[...])
        mix_ref[...] = mix
        x2_ref[...] = x_ref[...] + mix * _rms_r(mix) * g_ref[...]

    return pl.pallas_call(
        body, name="mixout_fwd", grid=(L // tm,),
        in_specs=[_row_spec(tm, SSM_W), _row_spec(tm, RET_W), _row_spec(tm, D_MODEL),
                  _full_spec((SSM_W, 2 * SSM_W)), _full_spec((D_MODEL, D_MODEL)), _full_spec((1, D_MODEL))],
        out_specs=[_row_spec(tm, SSM_W), _row_spec(tm, 2 * SSM_W), _row_spec(tm, D_MODEL),
                   _row_spec(tm, D_MODEL), _row_spec(tm, D_MODEL)],
        out_shape=[jax.ShapeDtypeStruct((L, SSM_W), _BF), jax.ShapeDtypeStruct((L, 2 * SSM_W), _F32),
                   jax.ShapeDtypeStruct((L, D_MODEL), _BF), jax.ShapeDtypeStruct((L, D_MODEL), _F32),
                   jax.ShapeDtypeStruct((L, D_MODEL), _F32)],
        compiler_params=_params("parallel"),
    )(s, y_ret, x, w_glu, w_out, g2)


FF1_COLS = D_FF // N_DEV


def _ff1_fwd(x2, g3, w1, tm):
    L = x2.shape[0]

    def body(x_ref, g_ref, w_ref, h_ref, f_ref):
        xv = x_ref[...]
        h = (xv * _rms_r(xv) * g_ref[...]).astype(_BF)
        h_ref[...] = h
        for j in range(N_DEV):
            f_ref[:, j * FF1_COLS:(j + 1) * FF1_COLS] = _dot(h, w_ref[j])

    return pl.pallas_call(
        body, name="ff1_fwd", grid=(L // tm,),
        in_specs=[_row_spec(tm, D_MODEL), _full_spec((1, D_MODEL)), _full_spec((N_DEV, D_MODEL, FF1_COLS))],
        out_specs=[_row_spec(tm, D_MODEL), _row_spec(tm, D_FF)],
        out_shape=[jax.ShapeDtypeStruct((L, D_MODEL), _BF), jax.ShapeDtypeStruct((L, D_FF), _F32)],
        compiler_params=_params("parallel"),
    )(x2, g3, w1)


def _ff2_loss(f1, x2, tgt, g4, w2, tm):
    L = f1.shape[0]

    def body(f_ref, x_ref, t_ref, g_ref, w_ref, dy_ref, dm_ref, dg_ref, ls_ref):
        @pl.when(pl.program_id(0) == 0)
        def _():
            dg_ref[...] = jnp.zeros_like(dg_ref)
            ls_ref[...] = jnp.zeros_like(ls_ref)

        rl = jnp.maximum(f_ref[...], 0.0)
        m = _dot((rl * rl).astype(_BF), w_ref[...])
        g = g_ref[...]
        y = x_ref[...] + m * _rms_r(m) * g
        err = y - t_ref[...]
        ls_ref[...] += jnp.sum(err * err, axis=0, keepdims=True)
        dy = err * (1.0 / D_MODEL)
        dy_ref[...] = dy
        dm, dgr = _rms_bwd(m, g, dy)
        dm_ref[...] = dm.astype(_BF)
        dg_ref[...] += jnp.sum(dgr, axis=0, keepdims=True)

    return pl.pallas_call(
        body, name="ff2_loss", grid=(L // tm,),
        in_specs=[_row_spec(tm, D_FF), _row_spec(tm, D_MODEL), _row_spec(tm, D_MODEL),
                  _full_spec((1, D_MODEL)), _full_spec((D_FF, D_MODEL))],
        out_specs=[_row_spec(tm, D_MODEL), _row_spec(tm, D_MODEL), _full_spec((1, D_MODEL)), _full_spec((1, D_MODEL))],
        out_shape=[jax.ShapeDtypeStruct((L, D_MODEL), _F32), jax.ShapeDtypeStruct((L, D_MODEL), _BF),
                   jax.ShapeDtypeStruct((1, D_MODEL), _F32), jax.ShapeDtypeStruct((1, D_MODEL), _F32)],
        compiler_params=_params("arbitrary"),
    )(f1, x2, tgt, g4, w2)


def _ff2_bwd(dm, f1, w2, tm, tn):
    L = dm.shape[0]
    last = L // tm - 1

    def body(dm_ref, f_ref, w_ref, df_ref, dw_ref, acc):
        @pl.when(pl.program_id(1) == 0)
        def _():
            acc[...] = jnp.zeros_like(acc)

        dmv = dm_ref[...]
        rl = jnp.maximum(f_ref[...], 0.0)
        df_ref[...] = (_dot_nt(dmv, w_ref[...]) * (2.0 * rl)).astype(_BF)
        acc[...] += _dot_tn((rl * rl).astype(_BF), dmv)

        @pl.when(pl.program_id(1) == last)
        def _():
            dw_ref[...] = acc[...].astype(_BF)

    return pl.pallas_call(
        body, name="ff2_bwd", grid=(D_FF // tn, L // tm),
        in_specs=[pl.BlockSpec((tm, D_MODEL), lambda j, i: (i, 0)), pl.BlockSpec((tm, tn), lambda j, i: (i, j)),
                  pl.BlockSpec((tn, D_MODEL), lambda j, i: (j, 0))],
        out_specs=[pl.BlockSpec((tm, tn), lambda j, i: (i, j)), pl.BlockSpec((tn, D_MODEL), lambda j, i: (j, 0))],
        out_shape=[jax.ShapeDtypeStruct((L, D_FF), _BF), jax.ShapeDtypeStruct((D_FF, D_MODEL), _BF)],
        scratch_shapes=[pltpu.VMEM((tn, D_MODEL), _F32)],
        compiler_params=_params("parallel", "arbitrary"),
    )(dm, f1, w2)


def _ff1_bwd(df1, w1, x2, mix, dy, g3, g2, tm):
    L = df1.shape[0]

    def body(df_ref, w_ref, x2_ref, mix_ref, dy_ref, g3_ref, g2_ref, dx2_ref, dmix_ref, dg3_ref, dg2_ref):
        @pl.when(pl.program_id(0) == 0)
        def _():
            dg3_ref[...] = jnp.zeros_like(dg3_ref)
            dg2_ref[...] = jnp.zeros_like(dg2_ref)

        dh = _dot_nt(df_ref[:, 0:FF1_COLS], w_ref[0])
        for j in range(1, N_DEV):
            dh = dh + _dot_nt(df_ref[:, j * FF1_COLS:(j + 1) * FF1_COLS], w_ref[j])
        dz, dgr = _rms_bwd(x2_ref[...], g3_ref[...], dh)
        dg3_ref[...] += jnp.sum(dgr, axis=0, keepdims=True)
        dx2 = dy_ref[...] + dz
        dx2_ref[...] = dx2
        dmx, dgr2 = _rms_bwd(mix_ref[...], g2_ref[...], dx2)
        dg2_ref[...] += jnp.sum(dgr2, axis=0, keepdims=True)
        dmix_ref[...] = dmx.astype(_BF)

    vec = _full_spec((1, D_MODEL))
    return pl.pallas_call(
        body, name="ff1_bwd", grid=(L // tm,),
        in_specs=[_row_spec(tm, D_FF), _full_spec((N_DEV, D_MODEL, FF1_COLS)), _row_spec(tm, D_MODEL),
                  _row_spec(tm, D_MODEL), _row_spec(tm, D_MODEL), vec, vec],
        out_specs=[_row_spec(tm, D_MODEL), _row_spec(tm, D_MODEL), vec, vec],
        out_shape=[jax.ShapeDtypeStruct((L, D_MODEL), _F32), jax.ShapeDtypeStruct((L, D_MODEL), _BF),
                   jax.ShapeDtypeStruct((1, D_MODEL), _F32), jax.ShapeDtypeStruct((1, D_MODEL), _F32)],
        compiler_params=_params("arbitrary"),
    )(df1, w1, x2, mix, dy, g3, g2)


def _matmul_tn(a, b, tm, tn, name, slots=False):
    L, K = a.shape
    N = b.shape[1]
    last = L // tm - 1

    def body(a_ref, b_ref, o_ref, acc):
        @pl.when(pl.program_id(1) == 0)
        def _():
            acc[...] = jnp.zeros_like(acc)

        acc[...] += _dot_tn(a_ref[...].astype(_BF), b_ref[...].astype(_BF))

        @pl.when(pl.program_id(1) == last)
        def _():
            if slots:
                o_ref[0] = acc[...].astype(_BF)
            else:
                o_ref[...] = acc[...].astype(_BF)

    if slots:
        out_spec = pl.BlockSpec((1, K, tn), lambda j, i: (j, 0, 0))
        out_shape = jax.ShapeDtypeStruct((N // tn, K, tn), _BF)
    else:
        out_spec = pl.BlockSpec((K, tn), lambda j, i: (0, j))
        out_shape = jax.ShapeDtypeStruct((K, N), _BF)
    return pl.pallas_call(
        body, name=name, grid=(N // tn, L // tm),
        in_specs=[pl.BlockSpec((tm, K), lambda j, i: (i, 0)), pl.BlockSpec((tm, tn), lambda j, i: (i, j))],
        out_specs=out_spec, out_shape=out_shape,
        scratch_shapes=[pltpu.VMEM((K, tn), _F32)],
        compiler_params=_params("parallel", "arbitrary"),
    )(a, b)


def _mixout_bwd(dmix, w_out, w_glu, glu, s, o, gate, ggn, tm):
    L = dmix.shape[0]

    def body(dmix_ref, wo_ref, wg_ref, glu_ref, s_ref, o_ref, gate_ref, ggn_ref,
             dglu_ref, ds_ref, dgate_ref, do_ref, dggn_ref):
        @pl.when(pl.program_id(0) == 0)
        def _():
            dggn_ref[...] = jnp.zeros_like(dggn_ref)

        dcat = _dot_nt(dmix_ref[...], wo_ref[...])
        dy_ret, dy_ssm = dcat[:, :RET_W], dcat[:, RET_W:]
        glu = glu_ref[...]
        ga, sg = glu[:, :SSM_W], _sigmoid(glu[:, SSM_W:])
        dga = (dy_ssm * sg).astype(_BF)
        dgb = (dy_ssm * ga * sg * (1.0 - sg)).astype(_BF)
        dglu_ref[:, :SSM_W] = dga
        dglu_ref[:, SSM_W:] = dgb
        dys = _dot_nt(dga, wg_ref[:, :SSM_W]) + _dot_nt(dgb, wg_ref[:, SSM_W:])
        ds_ref[...] = dys * _gelu_grad(s_ref[...])
        gt = gate_ref[...]
        sgt = _sigmoid(gt)
        ggn = ggn_ref[...]
        for hh in range(N_HEAD):
            cols = slice(hh * HEAD_D, (hh + 1) * HEAD_D)
            ov = o_ref[:, cols]
            dlt = ov - jnp.mean(ov, axis=-1, keepdims=True)
            rstd = lax.rsqrt(jnp.mean(dlt * dlt, axis=-1, keepdims=True) + NORM_EPS)
            on = dlt * rstd
            dyr = dy_ret[:, cols] * (gt[:, cols] * sgt[:, cols])
            dgate_ref[:, cols] = dy_ret[:, cols] * (on * ggn[:, cols]) * (sgt[:, cols] * (1.0 + gt[:, cols] * (1.0 - sgt[:, cols])))
            dggn_ref[:, cols] += jnp.sum(dyr * on, axis=0, keepdims=True)
            don = dyr * ggn[:, cols]
            do = rstd * (don - jnp.mean(don, axis=-1, keepdims=True) - on * jnp.mean(don * on, axis=-1, keepdims=True))
            do_ref[:, cols] = do.astype(_BF)

    return pl.pallas_call(
        body, name="mixout_bwd", grid=(L // tm,),
        in_specs=[_row_spec(tm, D_MODEL), _full_spec((D_MODEL, D_MODEL)), _full_spec((SSM_W, 2 * SSM_W)),
                  _row_spec(tm, 2 * SSM_W), _row_spec(tm, SSM_W), _row_spec(tm, RET_W), _row_spec(tm, RET_W),
                  _full_spec((1, RET_W))],
        out_specs=[_row_spec(tm, 2 * SSM_W), _row_spec(tm, SSM_W), _row_spec(tm, RET_W), _row_spec(tm, RET_W),
                   _full_spec((1, RET_W))],
        out_shape=[jax.ShapeDtypeStruct((L, 2 * SSM_W), _BF), jax.ShapeDtypeStruct((L, SSM_W), _F32),
                   jax.ShapeDtypeStruct((L, RET_W), _F32), jax.ShapeDtypeStruct((L, RET_W), _BF),
                   jax.ShapeDtypeStruct((1, RET_W), _F32)],
        compiler_params=_params("arbitrary"),
    )(dmix, w_out, w_glu, glu, s, o, gate, ggn)


def _s5_bwd(u, ds, cin, bmat, cmat, tab_f, pw_f, tab_r, pw_r, d_skip, tb):
    L = u.shape[0]
    nt = L // tb
    seg = tb // SUBLANES
    G = KB_PER_STEP
    rcol = pl.BlockSpec((tb, G * LANES), lambda kb, t: (nt - 1 - t, kb))
    sp = _s5_specs(seg)
    aspec = pl.BlockSpec((G, SUBLANES, 2 * KB_STATES), lambda kb, t: (kb, 0, 0))

    def body(u_ref, ds_ref, cin_ref, b_ref, c_ref, tf_ref, pf_ref, tr_ref, pr_ref, d_ref,
             du_ref, db_ref, dc_ref, da_ref, dd_ref, up_scr, dp_scr, x_scr, xp_scr, g_scr, fc_scr, lc_scr):
        @pl.when(pl.program_id(1) == 0)
        def _():
            lc_scr[...] = jnp.zeros_like(lc_scr)
            db_ref[...] = jnp.zeros_like(db_ref)
            dc_ref[...] = jnp.zeros_like(dc_ref)
            da_ref[...] = jnp.zeros_like(da_ref)
            dd_ref[...] = jnp.zeros_like(dd_ref)

        _rows_to_segments(up_scr, u_ref, seg)
        _rows_to_segments(dp_scr, ds_ref, seg)
        fc_scr[...] = cin_ref[:, 0]
        for g in range(G):
            x_scr[g] = _dot(up_scr[g].astype(_BF), b_ref[g]).reshape(seg, SUBLANES, 2 * KB_STATES)
            g_scr[g] = _dot_nt(dp_scr[g].astype(_BF), c_ref[g]).reshape(seg, SUBLANES, 2 * KB_STATES)
        _scan_segments(x_scr, tf_ref, pf_ref, fc_scr, seg, reverse=False, xprev_ref=xp_scr)
        _scan_segments(g_scr, tr_ref, pr_ref, lc_scr, seg, reverse=True, xprev_ref=xp_scr, da_ref=da_ref)
        for g in range(G):
            cols = slice(g * LANES, (g + 1) * LANES)
            uv, dsv = up_scr[g], dp_scr[g]
            ub, dsb = uv.astype(_BF), dsv.astype(_BF)
            lamb = g_scr[g].reshape(tb, 2 * KB_STATES).astype(_BF)
            db_ref[g] += _dot_tn(ub, lamb)
            dc_ref[g] += _dot_tn(x_scr[g].reshape(tb, 2 * KB_STATES).astype(_BF), dsb)
            dd_ref[:, cols] += jnp.sum(dsv * uv, axis=0, keepdims=True)
            up_scr[g] = _dot_nt(lamb, b_ref[g]) + d_ref[:, cols] * dsv
        _segments_to_rows(du_ref, up_scr, seg)

    state = pltpu.VMEM((G, seg, SUBLANES, 2 * KB_STATES), _F32)
    carry = pltpu.VMEM((G, SUBLANES, 2 * KB_STATES), _F32)
    return pl.pallas_call(
        body, name="s5_bwd", grid=(N_KB // G, nt),
        in_specs=[rcol, rcol, pl.BlockSpec((G, 1, SUBLANES, 2 * KB_STATES), lambda kb, t: (kb, nt - 1 - t, 0, 0)),
                  sp["b"], sp["c"], sp["tab"], sp["pw"], sp["tab"], sp["pw"], sp["d"]],
        out_specs=[rcol, sp["b"], sp["c"], aspec, sp["d"]],
        out_shape=[jax.ShapeDtypeStruct((L, SSM_W), _F32),
                   jax.ShapeDtypeStruct((N_KB, LANES, 2 * KB_STATES), _F32),
                   jax.ShapeDtypeStruct((N_KB, 2 * KB_STATES, LANES), _F32),
                   jax.ShapeDtypeStruct((N_KB, SUBLANES, 2 * KB_STATES), _F32),
                   jax.ShapeDtypeStruct((1, SSM_W), _F32)],
        scratch_shapes=[pltpu.VMEM((G, tb, LANES), _F32)] * 2 + [state] * 3 + [carry] * 2,
        compiler_params=_params("parallel", "arbitrary"),
    )(u, ds, cin, bmat, cmat, tab_f, pw_f, tab_r, pw_r, d_skip)


def _retention_bwd(q, k, v, do, r_prev, consts, cosf, sinf):
    L = q.shape[0]
    nc = L // CHUNK
    blk = pl.BlockSpec((CHUNK, RET_W), lambda n: (nc - 1 - n, 0))
    rope_blk = pl.BlockSpec((CHUNK, HEAD_D), lambda n: (nc - 1 - n, 0))

    def body(q_ref, k_ref, v_ref, do_ref, rp_ref, dm_ref, xi_ref, zeta_ref, gc_ref, cos_ref, sin_ref,
             dq_ref, dk_ref, dv_ref, g_scr):
        @pl.when(pl.program_id(0) == 0)
        def _():
            g_scr[...] = jnp.zeros_like(g_scr)

        cs, sn = cos_ref[...], sin_ref[...]
        for hh in range(N_HEAD):
            cols = slice(hh * HEAD_D, (hh + 1) * HEAD_D)
            qv, kv, vv, dov = q_ref[:, cols], k_ref[:, cols], v_ref[:, cols], do_ref[:, cols]
            rb = rp_ref[hh, 0].astype(_BF)
            gst = g_scr[hh]
            gb = gst.astype(_BF)
            dm, zeta = dm_ref[hh], zeta_ref[hh]
            sb = (_dot_nt(qv, kv) * dm).astype(_BF)
            dab = (_dot_nt(dov, vv) * dm).astype(_BF)
            dox = (dov.astype(_F32) * xi_ref[hh]).astype(_BF)
            vz = (vv.astype(_F32) * zeta).astype(_BF)
            dq = _dot(dab, kv) + _dot_nt(dox, rb)
            dk = _dot_tn(dab, qv) + _dot_nt(vz, gb)
            dv = _dot_tn(sb, dov) + _dot(kv, gb) * zeta
            g_scr[hh] = gc_ref[hh, 0:1, :] * gst + _dot_tn(qv, dox)
            dq_ref[:, cols] = _rope_t(dq, cs, sn).astype(_BF)
            dk_ref[:, cols] = (_rope_t(dk, cs, sn) * (HEAD_D ** -0.5)).astype(_BF)
            dv_ref[:, cols] = dv.astype(_BF)

    return pl.pallas_call(
        body, name="retention_bwd", grid=(nc,),
        in_specs=[blk, blk, blk, blk, pl.BlockSpec((N_HEAD, 1, HEAD_D, HEAD_D), lambda n: (0, nc - 1 - n, 0, 0))]
        + _head_specs() + [rope_blk, rope_blk],
        out_specs=[blk, blk, blk],
        out_shape=[jax.ShapeDtypeStruct((L, RET_W), _BF)] * 3,
        scratch_shapes=[pltpu.VMEM((N_HEAD, HEAD_D, HEAD_D), _F32)],
        compiler_params=_params("arbitrary"),
    )(q, k, v, do, r_prev, *consts, cosf, sinf)


def _inproj_bwd(pieces, w_in, x, dx2, g1, tm):
    L = x.shape[0]

    def body(p0, p1, p2, p3, p4, w_ref, x_ref, dx2_ref, g_ref, dx_ref, dg_ref):
        @pl.when(pl.program_id(0) == 0)
        def _():
            dg_ref[...] = jnp.zeros_like(dg_ref)

        dh = None
        for j, p in enumerate((p0, p1, p2, p3, p4)):
            part = _dot_nt(p[...].astype(_BF), w_ref[:, j * RET_W:(j + 1) * RET_W])
            dh = part if dh is None else dh + part
        dz, dgr = _rms_bwd(x_ref[...], g_ref[...], dh)
        dx_ref[...] = dx2_ref[...] + dz
        dg_ref[...] += jnp.sum(dgr, axis=0, keepdims=True)

    return pl.pallas_call(
        body, name="inproj_bwd", grid=(L // tm,),
        in_specs=[_row_spec(tm, RET_W)] * 5 + [_full_spec((D_MODEL, IN_COLS)), _row_spec(tm, D_MODEL),
                                                 _row_spec(tm, D_MODEL), _full_spec((1, D_MODEL))],
        out_specs=[_row_spec(tm, D_MODEL), _full_spec((1, D_MODEL))],
        out_shape=[jax.ShapeDtypeStruct((L, D_MODEL), _F32), jax.ShapeDtypeStruct((1, D_MODEL), _F32)],
        compiler_params=_params("arbitrary"),
    )(*pieces, w_in, x, dx2, g1)


def _sum_adamw(parts, w, m, v, tr, name):
    _, R, Cc = parts.shape

    def body(p_ref, w_ref, m_ref, v_ref, g_ref, d_ref, nm_ref, nv_ref):
        gv = p_ref[0].astype(_F32)
        for s in range(1, N_DEV):
            gv = gv + p_ref[s].astype(_F32)
        g_ref[...] = gv
        nm = ADAM_B1 * m_ref[...] + (1.0 - ADAM_B1) * gv
        nv = ADAM_B2 * v_ref[...] + (1.0 - ADAM_B2) * (gv * gv)
        m_hat = nm / (1.0 - ADAM_B1 ** ADAM_STEP)
        v_hat = nv / (1.0 - ADAM_B2 ** ADAM_STEP)
        d_ref[...] = -ADAM_LR * (m_hat / (jnp.sqrt(v_hat) + ADAM_EPS) + ADAM_WD * w_ref[...])
        nm_ref[...] = nm
        nv_ref[...] = nv

    spec = _row_spec(tr, Cc)
    return pl.pallas_call(
        body, name=name, grid=(R // tr,),
        in_specs=[pl.BlockSpec((N_DEV, tr, Cc), lambda i: (0, i, 0))] + [spec] * 3, out_specs=[spec] * 4,
        out_shape=[jax.ShapeDtypeStruct((R, Cc), _F32)] * 4,
        compiler_params=_params("parallel"),
    )(parts, w, m, v)


def _my_place():
    return lax.axis_index("x"), lax.axis_index("y"), lax.axis_index("c")


def _all_gather(blocks):
    n = len(blocks)

    def body(*refs):
        x_refs, out_refs, done_ref = refs[:n], refs[n:2 * n], refs[2 * n]
        send_sems, recv_sems, local_sems = refs[2 * n + 1:]
        done_ref[...] = jnp.zeros_like(done_ref)
        x, y, c = _my_place()
        me, sibling = (x, y, c), (x, y, 1 - c)
        chips = [(1 - x, y), (x, 1 - y), (1 - x, 1 - y)]

        def slot(a, px, py, pc):
            return out_refs[a].at[4 * px + 2 * py + pc]

        def copy(a, k, blk, to, own=False):
            return pltpu.make_async_remote_copy(
                src_ref=x_refs[a] if own else slot(a, *blk), dst_ref=slot(a, *blk),
                send_sem=send_sems.at[a, k], recv_sem=recv_sems.at[a, k], device_id=to, device_id_type=MESH)

        mine = [pltpu.make_async_copy(x_refs[a], slot(a, *me), local_sems.at[a]) for a in range(n)]
        for cp in mine:
            cp.start()
        first = []
        for a in range(n):
            first.append(copy(a, 0, me, sibling, own=True))
            first += [copy(a, 1 + j, me, (*chip, c), own=True) for j, chip in enumerate(chips)]
        for cp in first:
            cp.start()
        passed = []
        for j, chip in enumerate(chips):
            for a in range(n):
                copy(a, 1 + j, (*chip, c), me).wait_recv()
                fwd = copy(a, 4 + j, (*chip, c), sibling)
                fwd.start()
                passed.append(fwd)
        for a in range(n):
            copy(a, 0, sibling, me).wait_recv()
            for j, chip in enumerate(chips):
                copy(a, 4 + j, (*chip, 1 - c), me).wait_recv()
        for cp in first + passed:
            cp.wait_send()
        for cp in mine:
            cp.wait()

    any_spec = pl.BlockSpec(memory_space=pl.ANY)
    outs = pl.pallas_call(
        body, name="weights_all_gather",
        in_specs=[any_spec] * n, out_specs=[any_spec] * n + [pl.BlockSpec(memory_space=pltpu.VMEM)],
        out_shape=[jax.ShapeDtypeStruct((N_DEV,) + b.shape, b.dtype) for b in blocks]
        + [jax.ShapeDtypeStruct((SUBLANES, LANES), _F32)],
        scratch_shapes=[pltpu.SemaphoreType.DMA((n, 7)), pltpu.SemaphoreType.DMA((n, 7)), pltpu.SemaphoreType.DMA((n,))],
    )(*blocks)
    return outs[:n], outs[n]


def _exchange(bigs, small):
    n = len(bigs)
    r = small.shape[0]

    def body(*refs):
        in_refs, out_refs = refs[:n + 1], refs[n + 1:2 * n + 2]
        send_sems, recv_sems, local_sems = refs[2 * n + 2:]
        x, y, c = _my_place()
        me = 4 * x + 2 * y + c
        own = [pltpu.make_async_copy(in_refs[a].at[me], out_refs[a].at[me], local_sems.at[a]) for a in range(n)]
        own.append(pltpu.make_async_copy(in_refs[n], out_refs[n].at[me], local_sems.at[n]))
        for cp in own:
            cp.start()
        copies = []
        for kk in range(1, N_DEV):
            px, py, pc = x ^ (kk >> 2), y ^ ((kk >> 1) & 1), c ^ (kk & 1)
            peer = 4 * px + 2 * py + pc
            for a in range(n + 1):
                src = in_refs[a].at[peer] if a < n else in_refs[a]
                copies.append(pltpu.make_async_remote_copy(
                    src_ref=src, dst_ref=out_refs[a].at[me],
                    send_sem=send_sems.at[a, kk - 1], recv_sem=recv_sems.at[a, kk - 1],
                    device_id=(px, py, pc), device_id_type=MESH))
        for cp in copies:
            cp.start()
        for cp in copies:
            cp.wait_recv()
        for cp in copies:
            cp.wait_send()
        for cp in own:
            cp.wait()

    any_spec = pl.BlockSpec(memory_space=pl.ANY)
    outs = pl.pallas_call(
        body, name="grad_exchange",
        in_specs=[any_spec] * (n + 1), out_specs=[any_spec] * (n + 1),
        out_shape=[jax.ShapeDtypeStruct(b.shape, b.dtype) for b in bigs]
        + [jax.ShapeDtypeStruct((N_DEV, r, LANES), small.dtype)],
        scratch_shapes=[pltpu.SemaphoreType.DMA((n + 1, 7)), pltpu.SemaphoreType.DMA((n + 1, 7)),
                        pltpu.SemaphoreType.DMA((n + 1,))],
    )(*bigs, small)
    return outs[:n], outs[n]


HBM_SPEC = pl.BlockSpec(memory_space=pltpu.HBM)
SEM_SPEC = pl.BlockSpec(memory_space=pltpu.SEMAPHORE)
DATAFLOW = pltpu.SideEffectType.DATAFLOW_SIDE_EFFECTING


def _my_index():
    x, y, c = _my_place()
    return 4 * x + 2 * y + c


def _landing(own_block):
    zone = lax.empty((N_DEV,) + own_block.shape, own_block.dtype)
    return lax.dynamic_update_index_in_dim(zone, own_block, _my_index(), 0)


def _split_copies(src_refs, land_refs, send_sems, recv_sems, gather):
    x, y, c = _my_place()
    me = 4 * x + 2 * y + c
    copies = []
    for kk in range(1, N_DEV):
        px, py, pc = x ^ (kk >> 2), y ^ ((kk >> 1) & 1), c ^ (kk & 1)
        peer = 4 * px + 2 * py + pc
        for a, (src, land) in enumerate(zip(src_refs, land_refs)):
            copies.append(pltpu.make_async_remote_copy(
                src_ref=src if gather else src.at[peer], dst_ref=land.at[me],
                send_sem=send_sems.at[a * 7 + kk - 1], recv_sem=recv_sems.at[a * 7 + kk - 1],
                device_id=(px, py, pc), device_id_type=MESH))
    return copies


def _split_start(srcs, lands, gather, name):
    n = len(srcs)

    def body(*refs):
        src_refs, land_refs = refs[:n], refs[n:2 * n]
        send_sems, recv_sems = refs[2 * n], refs[2 * n + 1]
        token = refs[-1]
        for cp in _split_copies(src_refs, land_refs, send_sems, recv_sems, gather):
            cp.start()
        token[...] = jnp.zeros_like(token)

    outs = pl.pallas_call(
        body, name=name,
        out_shape=(pltpu.SemaphoreType.DMA((7 * n,)), pltpu.SemaphoreType.DMA((7 * n,)),
                   *[pltpu.HBM(t.shape, t.dtype) for t in srcs], *[pltpu.HBM(t.shape, t.dtype) for t in lands],
                   jax.ShapeDtypeStruct((SUBLANES, LANES), _F32)),
        in_specs=[HBM_SPEC] * (2 * n),
        out_specs=(SEM_SPEC, SEM_SPEC, *[HBM_SPEC] * (2 * n), pl.BlockSpec(memory_space=pltpu.VMEM)),
        input_output_aliases={i: 2 + i for i in range(2 * n)},
        compiler_params=pltpu.CompilerParams(has_side_effects=DATAFLOW),
    )(*[pltpu.with_memory_space_constraint(t, pltpu.HBM) for t in list(srcs) + list(lands)])
    return outs[0], outs[1], outs[2:2 + n], outs[2 + n:2 + 2 * n], outs[-1]


def _split_wait(send_sems, recv_sems, srcs, lands, after, gather, name):
    n = len(srcs)

    def body(*refs):
        src_refs, land_refs = refs[:n], refs[n:2 * n]
        send_s, recv_s = refs[2 * n], refs[2 * n + 1]
        for cp in _split_copies(src_refs, land_refs, send_s, recv_s, gather):
            cp.wait_send()
            cp.wait_recv()

    outs = pl.pallas_call(
        body, name=name,
        out_shape=tuple(pltpu.HBM(t.shape, t.dtype) for t in list(srcs) + list(lands)),
        in_specs=[HBM_SPEC] * (2 * n) + [SEM_SPEC, SEM_SPEC, pl.BlockSpec(memory_space=pl.ANY)],
        out_specs=tuple([HBM_SPEC] * (2 * n)),
        input_output_aliases={i: i for i in range(2 * n)},
        compiler_params=pltpu.CompilerParams(has_side_effects=DATAFLOW),
    )(*srcs, *lands, send_sems, recv_sems, after)
    return outs[n:]


def _discretize(lam_re, lam_im, log_dt, b_re, b_im):
    lr = jnp.minimum(lam_re, -1e-4)
    li = lam_im
    dt = jnp.exp(log_dt)[:, None]
    er = jnp.exp(lr * dt)
    ar, ai = er * jnp.cos(li * dt), er * jnp.sin(li * dt)
    den = lr * lr + li * li
    cr = ((ar - 1.0) * lr + ai * li) / den
    ci = (ai * lr - (ar - 1.0) * li) / den
    bbr = cr[:, :, None] * b_re - ci[:, :, None] * b_im
    bbi = cr[:, :, None] * b_im + ci[:, :, None] * b_re
    return ar, ai, bbr, bbi


def _cmul(ar, ai, br, bi):
    return ar * br - ai * bi, ar * bi + ai * br


def _cpowers(ar, ai, n):
    pr, pi = ar[None], ai[None]
    while pr.shape[0] < n:
        nr, ni = _cmul(pr, pi, pr[-1][None], pi[-1][None])
        pr, pi = jnp.concatenate([pr, nr]), jnp.concatenate([pi, ni])
    return pr[:n], pi[:n]


def _scan_tables(ar, ai, seg, reverse):
    if reverse:
        ai = -ai
    ar, ai = ar.reshape(N_KB, KB_STATES), ai.reshape(N_KB, KB_STATES)
    pr, pi = _cpowers(ar, ai, seg)
    a1 = (pr[-1], pi[-1])
    a2 = _cmul(*a1, *a1)
    a4 = _cmul(*a2, *a2)
    row = jnp.arange(SUBLANES)[None, :, None]
    wide = lambda t: jnp.broadcast_to(t[:, None, :], (N_KB, SUBLANES, KB_STATES))
    tabs = [wide(ar), wide(ai)]
    for dist, (qr, qi) in ((1, a1), (2, a2), (4, a4)):
        keep = (row < SUBLANES - dist) if reverse else (row >= dist)
        tabs += [jnp.where(keep, wide(qr), 0.0), jnp.where(keep, wide(qi), 0.0)]
    tabs += [wide(a1[0]), wide(a1[1])]
    if reverse:
        pr, pi = pr[::-1], pi[::-1]
    pw = jnp.transpose(jnp.concatenate([pr, pi], axis=-1), (1, 0, 2))[:, :, None, :]
    return jnp.stack(tabs, axis=1).astype(_F32), pw.astype(_F32)


def _block_diag_in(br, bi):
    eye = jnp.eye(GROUPS_PER_KB, dtype=_F32)
    one = lambda t: jnp.einsum("kgpc,gh->kgchp", t.reshape(N_KB, GROUPS_PER_KB, N_STATE, SSM_GC), eye).reshape(
        N_KB, LANES, KB_STATES)
    return jnp.concatenate([one(br), one(bi)], axis=-1)


def _block_diag_in_t(dmat):
    d6 = dmat.reshape(N_KB, GROUPS_PER_KB, SSM_GC, 2, GROUPS_PER_KB, N_STATE)
    eye = jnp.eye(GROUPS_PER_KB, dtype=_F32)
    both = jnp.einsum("kgcrhp,gh->rkgpc", d6, eye).reshape(2, N_GROUP, N_STATE, SSM_GC)
    return both[0], both[1]


def _block_diag_out(c_re, c_im):
    eye = jnp.eye(GROUPS_PER_KB, dtype=_F32)
    one = lambda t: jnp.einsum("kgcp,gh->khpgc", t.reshape(N_KB, GROUPS_PER_KB, SSM_GC, N_STATE), eye).reshape(
        N_KB, KB_STATES, LANES)
    return jnp.concatenate([one(c_re), -one(c_im)], axis=1)


def _block_diag_out_t(dmat):
    d6 = dmat.reshape(N_KB, 2, GROUPS_PER_KB, N_STATE, GROUPS_PER_KB, SSM_GC)
    eye = jnp.eye(GROUPS_PER_KB, dtype=_F32)
    both = jnp.einsum("krhpgc,gh->rkgcp", d6, eye).reshape(2, N_GROUP, SSM_GC, N_STATE)
    return both[0], -both[1]


SMALL_NAMES = ("norm_mix_pre", "norm_mix_post", "ret_gn_gain", "ssm_lambda_re", "ssm_lambda_im", "ssm_log_dt",
               "ssm_b_re", "ssm_b_im", "ssm_c_re", "ssm_c_im", "ssm_d", "norm_mlp_pre", "norm_mlp_post")


def _local_grads(x, tgt, small, first_weight, late_weights, emit, tm, tk, tb, zero=0.0):
    L = x.shape[0]
    g1, g2, ggn = small["norm_mix_pre"], small["norm_mix_post"], small["ret_gn_gain"]
    g3, g4, d_skip = small["norm_mlp_pre"], small["norm_mlp_post"], small["ssm_d"]

    half = HEAD_D // 2
    inv_freq = ROPE_BASE ** (-jnp.arange(half, dtype=_F32) / half)
    ang = jnp.arange(L, dtype=_F32)[:, None] * inv_freq[None, :] + zero
    cosf = jnp.concatenate([jnp.cos(ang), jnp.cos(ang)], axis=-1)
    sinf = jnp.concatenate([-jnp.sin(ang), jnp.sin(ang)], axis=-1)
    consts = _ret_consts()

    disc_in = (small["ssm_lambda_re"][0], small["ssm_lambda_im"][0], small["ssm_log_dt"][0],
               small["ssm_b_re"][0], small["ssm_b_im"][0])
    (ar, ai, bbr, bbi), disc_vjp = jax.vjp(_discretize, *disc_in)
    bmat = _block_diag_in(bbr, bbi).astype(_BF)
    cmat = _block_diag_out(small["ssm_c_re"][0], small["ssm_c_im"][0]).astype(_BF)
    seg = tb // SUBLANES
    tab_f, pw_f = _scan_tables(ar, ai, seg, False)
    tab_r, pw_r = _scan_tables(ar, ai, seg, True)

    w_in = first_weight(sinf)
    h1, q, k, v, gate, u = _inproj_fwd(x, g1, w_in, cosf, sinf, tm)
    o, y_ret, r_prev = _retention_fwd(q, k, v, gate, ggn, consts)
    s, cin = _s5_fwd(u, bmat, cmat, tab_f, pw_f, d_skip, tb)
    w_glu, w_out, w_ff1, w_ff2 = late_weights(s)
    ys, glu, cat, mix, x2 = _mixout_fwd(s, y_ret, x, w_glu, w_out, g2, tm)
    h3, f1 = _ff1_fwd(x2, g3, w_ff1, tm)
    dy, dm, dg4, sq = _ff2_loss(f1, x2, tgt, g4, w_ff2, tm)

    df1, dw_ff2 = _ff2_bwd(dm, f1, w_ff2, min(1024, L), 1024)
    dx2, dmix, dg3, dg2 = _ff1_bwd(df1, w_ff1, x2, mix, dy, g3, g2, tm)
    dw_ff1 = _matmul_tn(h3, df1, tk, FF1_COLS, "dw_ff1", slots=True)
    zero = emit({"w_ff1": dw_ff1, "w_ff2": dw_ff2})
    dglu, ds, dgate, do, dggn = _mixout_bwd(dmix, w_out, w_glu, glu, s, o, gate, ggn if zero is None else ggn + zero, tm)
    dw_out = _matmul_tn(cat, dmix, tk, 1024, "dw_out")
    dw_glu = _matmul_tn(ys, dglu, tk, 1024, "dw_glu")
    zero = emit({"w_glu": dw_glu, "w_out": dw_out})
    du, dbmat, dcmat, da8, dd = _s5_bwd(u, ds, cin, bmat, cmat, tab_f, pw_f, tab_r, pw_r,
                                        d_skip if zero is None else d_skip + zero, tb)
    dq, dk, dv = _retention_bwd(q, k, v, do, r_prev, consts, cosf, sinf)
    pieces = (dq, dk, dv, dgate, du)
    dw_in = jnp.concatenate([_matmul_tn(h1, p, tk, RET_W, "dw_in_%d" % j) for j, p in enumerate(pieces)], axis=1)
    zero = emit({"w_in": dw_in})
    gx, dg1 = _inproj_bwd(pieces, w_in, x, dx2, g1 if zero is None else g1 + zero, tm)

    da = jnp.sum(da8, axis=1)
    dar = da[:, :KB_STATES].reshape(N_GROUP, N_STATE)
    dai = da[:, KB_STATES:].reshape(N_GROUP, N_STATE)
    dbr, dbi = _block_diag_in_t(dbmat)
    dlre, dlim, dldt, dbre, dbim = disc_vjp((dar, dai, dbr, dbi))
    dcre, dcim = _block_diag_out_t(dcmat)

    gsmall = {
        "norm_mix_pre": dg1, "norm_mix_post": dg2, "ret_gn_gain": dggn,
        "ssm_lambda_re": dlre[None], "ssm_lambda_im": dlim[None], "ssm_log_dt": dldt[None],
        "ssm_b_re": dbre[None], "ssm_b_im": dbim[None], "ssm_c_re": dcre[None], "ssm_c_im": dcim[None],
        "ssm_d": dd, "norm_mlp_pre": dg3, "norm_mlp_post": dg4,
    }
    return sq, gx, gsmall


BIG_SHAPES = {"w_in": (D_MODEL, IN_COLS // N_DEV), "w_glu": (SSM_W, 2 * SSM_W // N_DEV), "w_out": (D_MODEL // N_DEV, D_MODEL),
              "w_ff1": (D_MODEL, FF1_COLS), "w_ff2": (D_FF // N_DEV, D_MODEL)}
BIG_NAMES = ("w_in", "w_glu", "w_out", "w_ff1", "w_ff2")


def _cols_from_slots(g):
    return jnp.transpose(g, (1, 0, 2)).reshape(g.shape[1], N_DEV * g.shape[2])


def _cols_to_slots(dw):
    r, cols = dw.shape
    return jnp.transpose(dw.reshape(r, N_DEV, cols // N_DEV), (1, 0, 2))


LATE_NAMES = ("w_glu", "w_out", "w_ff1", "w_ff2")


def _grad_slots(name, dw):
    if name in ("w_in", "w_glu"):
        return _cols_to_slots(dw)
    if name == "w_ff1":
        return dw
    return dw.reshape((N_DEV,) + BIG_SHAPES[name])


PIECE_ROWS = 8


def _small_layout(shapes):
    off, rows = {}, 0
    for n in SMALL_NAMES:
        off[n] = rows
        rows += -(-math.prod(shapes[n]) // (PIECE_ROWS * LANES)) * PIECE_ROWS
    return off, rows, rows + PIECE_ROWS


def _pack_small(vals, shapes, last=None):
    parts = []
    for n in SMALL_NAMES:
        flat = vals[n].reshape(-1).astype(_F32)
        pad = -flat.shape[0] % (PIECE_ROWS * LANES)
        if pad:
            flat = jnp.concatenate([flat, jnp.zeros((pad,), _F32)])
        parts.append(flat.reshape(-1, LANES))
    parts.append(jnp.zeros((PIECE_ROWS, LANES), _F32) if last is None else last)
    return jnp.concatenate(parts, axis=0)


def _unpack_small(buf, shapes):
    off, _, _ = _small_layout(shapes)
    out = {}
    for n in SMALL_NAMES:
        size = math.prod(shapes[n])
        rows = -(-size // LANES)
        out[n] = buf[off[n]:off[n] + rows].reshape(-1)[:size].reshape(shapes[n])
    return out


WEIGHT_NAMES = ('norm_mix_pre', 'norm_mix_post', 'w_in', 'ret_gn_gain', 'ssm_lambda_re', 'ssm_lambda_im', 'ssm_log_dt',
                'ssm_b_re', 'ssm_b_im', 'ssm_c_re', 'ssm_c_im', 'ssm_d', 'w_glu', 'w_out', 'norm_mlp_pre',
                'norm_mlp_post', 'w_ff1', 'w_ff2')


def kernel(x, norm_mix_pre, norm_mix_post, w_in, ret_gn_gain, ssm_lambda_re, ssm_lambda_im, ssm_log_dt, ssm_b_re, ssm_b_im, ssm_c_re, ssm_c_im, ssm_d, w_glu, w_out, norm_mlp_pre, norm_mlp_post, w_ff1, w_ff2, loss_target, m_norm_mix_pre, m_norm_mix_post, m_w_in, m_ret_gn_gain, m_ssm_lambda_re, m_ssm_lambda_im, m_ssm_log_dt, m_ssm_b_re, m_ssm_b_im, m_ssm_c_re, m_ssm_c_im, m_ssm_d, m_w_glu, m_w_out, m_norm_mlp_pre, m_norm_mlp_post, m_w_ff1, m_w_ff2, v_norm_mix_pre, v_norm_mix_post, v_w_in, v_ret_gn_gain, v_ssm_lambda_re, v_ssm_lambda_im, v_ssm_log_dt, v_ssm_b_re, v_ssm_b_im, v_ssm_c_re, v_ssm_c_im, v_ssm_d, v_w_glu, v_w_out, v_norm_mlp_pre, v_norm_mlp_post, v_w_ff1, v_w_ff2):
    args = dict(locals())
    w = {n: args[n] for n in WEIGHT_NAMES}
    m = {n: args["m_" + n] for n in WEIGHT_NAMES}
    v = {n: args["v_" + n] for n in WEIGHT_NAMES}
    L = x.shape[1]
    tm = min(256, L)
    tk = min(2048, L)
    tb = min(512, L)

    def gather_start(names, zero, name):
        blocks = [w[n][0].astype(_BF) for n in names]
        blocks[0] = blocks[0] + zero
        return _split_start(blocks, [_landing(b) for b in blocks], True, name)

    first = gather_start(("w_in",), jnp.zeros((), _BF), "w_in_start")
    late = gather_start(LATE_NAMES, first[4][0, 0].astype(_BF), "late_weights_start")

    def first_weight(after):
        return _cols_from_slots(_split_wait(*first[:4], after, True, "w_in_wait")[0])

    def late_weights(after):
        g = dict(zip(LATE_NAMES, _split_wait(*late[:4], after, True, "late_weights_wait")))
        return (_cols_from_slots(g["w_glu"]), g["w_out"].reshape(D_MODEL, D_MODEL), g["w_ff1"],
                g["w_ff2"].reshape(D_FF, D_MODEL))

    in_flight = []

    def emit(dws):
        names = sorted(dws)
        srcs = [_grad_slots(n, dws[n]) for n in names]
        lands = [_landing(lax.dynamic_index_in_dim(t, _my_index(), 0, keepdims=False)) for t in srcs]
        started = _split_start(srcs, lands, False, "grads_start_" + "_".join(names))
        in_flight.append((names, started))
        return started[4][0, 0]

    small_w = {n: w[n] for n in SMALL_NAMES}
    sq, gx, gsmall = _local_grads(x[0], loss_target[0], small_w, first_weight, late_weights, emit, tm, tk, tb,
                                  zero=late[4][0, 0])

    shapes = {n: w[n].shape for n in SMALL_NAMES}
    loss_rows = jnp.broadcast_to(0.5 / D_MODEL * jnp.sum(sq), (PIECE_ROWS, LANES)).astype(_F32)
    small_buf = _pack_small(gsmall, shapes, loss_rows)
    small_started = _split_start([small_buf], [_landing(small_buf)], True, "small_grads_start")
    grads, delta, new_m, new_v = {}, {}, {}, {}
    after = small_started[4]
    for names, started in in_flight:
        landed = _split_wait(*started[:4], after, False, "grads_wait_" + "_".join(names))
        for n, parts in zip(names, landed):
            res = _sum_adamw(parts, w[n][0], m[n][0], v[n][0], min(256, BIG_SHAPES[n][0]), "adamw_" + n)
            grads[n], delta[n], new_m[n], new_v[n] = (t[None] for t in res)
        after = res[1]
    small_parts = _split_wait(*small_started[:4], after, True, "small_grads_wait")[0]
    sw, sm, sv = _pack_small(w, shapes), _pack_small(m, shapes), _pack_small(v, shapes)
    res = _sum_adamw(small_parts, sw, sm, sv, sw.shape[0], "adamw_small")
    for dst, buf in zip((grads, delta, new_m, new_v), res):
        dst.update(_unpack_small(buf, shapes))
    _, loss_at, _ = _small_layout(shapes)
    loss = res[0][loss_at, 0]

    return (loss, gx[None], *[grads[n] for n in WEIGHT_NAMES], *[delta[n] for n in WEIGHT_NAMES],
            *[new_m[n] for n in WEIGHT_NAMES], *[new_v[n] for n in WEIGHT_NAMES])
```

```python
import math

import jax
import jax.numpy as jnp
from jax import lax
from jax.experimental import pallas as pl
from jax.experimental.pallas import tpu as pltpu

_BF = jnp.bfloat16
_F32 = jnp.float32

D_MODEL = 1024
RET_W = 512
N_HEAD = 4
HEAD_D = 128
CHUNK = 128
SSM_W = 512
SSM_GC = 16
N_GROUP = 32
N_STATE = 64
GROUPS_PER_KB = 8
N_KB = 4
KB_STATES = GROUPS_PER_KB * N_STATE
D_FF = 4096
IN_COLS = 2560
NORM_EPS = 1e-6
ROPE_BASE = 10000.0
N_DEV = 8

ADAM_LR = 0.001
ADAM_B1 = 0.9
ADAM_B2 = 0.999
ADAM_EPS = 1e-08
ADAM_WD = 0.01
ADAM_STEP = 10

SUBLANES = 8
LANES = 128
VMEM_LIMIT = 52 * 1024 * 1024
KB_PER_STEP = 2
SCAN_UNROLL = 2

MESH = pl.DeviceIdType.MESH


def _params(*sem):
    return pltpu.CompilerParams(dimension_semantics=sem, vmem_limit_bytes=VMEM_LIMIT)


def _dot(a, b):
    return jnp.dot(a, b, preferred_element_type=_F32)


def _dot_nt(a, b):
    return lax.dot_general(a, b, (((1,), (1,)), ((), ())), preferred_element_type=_F32)


def _dot_tn(a, b):
    return lax.dot_general(a, b, (((0,), (0,)), ((), ())), preferred_element_type=_F32)


def _rms_r(z):
    return lax.rsqrt(jnp.mean(z * z, axis=-1, keepdims=True) + NORM_EPS)


def _rms_bwd(z, g, dn):
    r = _rms_r(z)
    t = dn * g
    dz = r * t - z * (r * r * r * jnp.mean(t * z, axis=-1, keepdims=True))
    return dz, dn * z * r


def _rope(t, cs, sn):
    return t * cs + pltpu.roll(t, HEAD_D // 2, 1) * sn


def _rope_t(t, cs, sn):
    return t * cs - pltpu.roll(t, HEAD_D // 2, 1) * sn


def _sigmoid(z):
    return 1.0 / (1.0 + jnp.exp(-z))


_GELU_C = math.sqrt(2.0 / math.pi)


def _gelu(z):
    return 0.5 * z * (1.0 + jnp.tanh(_GELU_C * (z + 0.044715 * z * z * z)))


def _gelu_grad(z):
    th = jnp.tanh(_GELU_C * (z + 0.044715 * z * z * z))
    return 0.5 * (1.0 + th) + 0.5 * z * (1.0 - th * th) * _GELU_C * (1.0 + 3 * 0.044715 * z * z)


ROW_CHUNK = 256


def _row_chunks(tm):
    return [pl.ds(i, min(ROW_CHUNK, tm)) for i in range(0, tm, ROW_CHUNK)]


def _row_spec(tm, n):
    return pl.BlockSpec((tm, n), lambda i: (i, 0))


def _full_spec(shape):
    nd = len(shape)
    return pl.BlockSpec(shape, lambda *_: (0,) * nd)


def _weight_spec(shape):
    nd = len(shape)
    return pl.BlockSpec(shape, lambda *_: (0,) * nd, pipeline_mode=pl.Buffered(1))


def _inproj_fwd(x, g1, w_in, cosf, sinf, tm):
    L = x.shape[0]

    def body(x_ref, g_ref, w_ref, cos_ref, sin_ref, h_ref, q_ref, k_ref, v_ref, gate_ref, u_ref):
        xv = x_ref[...]
        h = (xv * _rms_r(xv) * g_ref[...]).astype(_BF)
        h_ref[...] = h
        proj = _dot(h, w_ref[...])
        cs, sn = cos_ref[...], sin_ref[...]
        for hh in range(N_HEAD):
            lo = hh * HEAD_D
            q_ref[:, lo:lo + HEAD_D] = _rope(proj[:, lo:lo + HEAD_D], cs, sn).astype(_BF)
            kh = _rope(proj[:, RET_W + lo:RET_W + lo + HEAD_D], cs, sn) * (HEAD_D ** -0.5)
            k_ref[:, lo:lo + HEAD_D] = kh.astype(_BF)
        v_ref[...] = proj[:, 2 * RET_W:3 * RET_W].astype(_BF)
        gate_ref[...] = proj[:, 3 * RET_W:4 * RET_W]
        u_ref[...] = proj[:, 4 * RET_W:]

    return pl.pallas_call(
        body, name="inproj_fwd", grid=(L // tm,),
        in_specs=[_row_spec(tm, D_MODEL), _full_spec((1, D_MODEL)), _weight_spec((D_MODEL, IN_COLS)),
                  _row_spec(tm, HEAD_D), _row_spec(tm, HEAD_D)],
        out_specs=[_row_spec(tm, D_MODEL)] + [_row_spec(tm, RET_W)] * 5,
        out_shape=[jax.ShapeDtypeStruct((L, D_MODEL), _BF)] + [jax.ShapeDtypeStruct((L, RET_W), _BF)] * 3
        + [jax.ShapeDtypeStruct((L, RET_W), _F32)] * 2,
        compiler_params=_params("parallel"),
    )(x, g1, w_in, cosf, sinf)


def _ret_consts():
    lg = jnp.log(1.0 - jnp.exp(jnp.linspace(math.log(1.0 / 32), math.log(1.0 / 512), N_HEAD))).astype(_F32)
    idx = jnp.arange(CHUNK, dtype=_F32)
    diff = idx[:, None] - idx[None, :]
    decay = jnp.where(diff[None] >= 0, jnp.exp(jnp.maximum(diff, 0.0)[None] * lg[:, None, None]), 0.0)
    zeta = jnp.exp((CHUNK - 1 - idx)[None, :] * lg[:, None])
    xi = jnp.exp((idx + 1.0)[None, :] * lg[:, None])
    gc = jnp.exp(CHUNK * lg)
    wide = lambda t: jnp.broadcast_to(t[:, :, None], (N_HEAD, CHUNK, HEAD_D)).astype(_F32)
    gcw = jnp.broadcast_to(gc[:, None, None], (N_HEAD, SUBLANES, HEAD_D)).astype(_F32)
    return decay.astype(_F32), wide(xi), wide(zeta), gcw


def _head_specs():
    c3 = _full_spec((N_HEAD, CHUNK, CHUNK))
    return [c3, c3, c3, _full_spec((N_HEAD, SUBLANES, HEAD_D))]


def _retention_fwd(q, k, v, gate, ggn, consts):
    L = q.shape[0]
    nc = L // CHUNK
    blk = pl.BlockSpec((CHUNK, RET_W), lambda n: (n, 0))

    def body(q_ref, k_ref, v_ref, gate_ref, ggn_ref, dm_ref, xi_ref, zeta_ref, gc_ref,
             o_ref, y_ref, rp_ref, r_scr):
        @pl.when(pl.program_id(0) == 0)
        def _():
            r_scr[...] = jnp.zeros_like(r_scr)

        for hh in range(N_HEAD):
            cols = slice(hh * HEAD_D, (hh + 1) * HEAD_D)
            qv, kv, vv = q_ref[:, cols], k_ref[:, cols], v_ref[:, cols]
            r_prev = r_scr[hh]
            s = _dot_nt(qv, kv) * dm_ref[hh]
            o = _dot(s.astype(_BF), vv) + _dot(qv, r_prev.astype(_BF)) * xi_ref[hh]
            o_ref[:, cols] = o
            rp_ref[hh, 0] = r_prev
            vz = (vv.astype(_F32) * zeta_ref[hh]).astype(_BF)
            r_scr[hh] = gc_ref[hh, 0:1, :] * r_prev + _dot_tn(kv, vz)
            dlt = o - jnp.mean(o, axis=-1, keepdims=True)
            on = dlt * lax.rsqrt(jnp.mean(dlt * dlt, axis=-1, keepdims=True) + NORM_EPS)
            gt = gate_ref[:, cols]
            y_ref[:, cols] = (gt * _sigmoid(gt) * (on * ggn_ref[:, cols])).astype(_BF)

    return pl.pallas_call(
        body, name="retention_fwd", grid=(nc,),
        in_specs=[blk, blk, blk, blk, _full_spec((1, RET_W))] + _head_specs(),
        out_specs=[blk, blk, pl.BlockSpec((N_HEAD, 1, HEAD_D, HEAD_D), lambda n: (0, n, 0, 0))],
        out_shape=[jax.ShapeDtypeStruct((L, RET_W), _F32), jax.ShapeDtypeStruct((L, RET_W), _BF),
                   jax.ShapeDtypeStruct((N_HEAD, nc, HEAD_D, HEAD_D), _F32)],
        scratch_shapes=[pltpu.VMEM((N_HEAD, HEAD_D, HEAD_D), _F32)],
        compiler_params=_params("arbitrary"),
    )(q, k, v, gate, ggn, *consts)


def _rows_to_segments(dst_scr, src_ref, seg):
    for g in range(dst_scr.shape[0]):
        for j in range(SUBLANES):
            dst_scr[g, pl.ds(j, seg, stride=SUBLANES), :] = src_ref[pl.ds(j * seg, seg), g * LANES:(g + 1) * LANES]


def _segments_to_rows(dst_ref, src_scr, seg):
    for g in range(src_scr.shape[0]):
        for j in range(SUBLANES):
            dst_ref[pl.ds(j * seg, seg), g * LANES:(g + 1) * LANES] = src_scr[g, pl.ds(j, seg, stride=SUBLANES), :]


def _scan_segments(x_ref, tab_ref, pw_ref, carry_ref, seg, reverse, xprev_ref=None, da_ref=None):
    G = x_ref.shape[0]
    W = KB_STATES
    re, im = pl.ds(0, W), pl.ds(W, W)
    row_id = lax.broadcasted_iota(jnp.int32, (SUBLANES, W), 0)
    edge_in = (row_id == SUBLANES - 1) if reverse else (row_id == 0)
    edge_out = 0 if reverse else SUBLANES - 1
    a_tab = [(tab_ref[g, 0], tab_ref[g, 1]) for g in range(G)]

    def local(i, st):
        r = (seg - 1 - i) if reverse else i
        out = []
        for g in range(G):
            (ar, ai), (sr, si) = a_tab[g], st[g]
            nr = ar * sr - ai * si + x_ref[g, r, :, re]
            ni = ar * si + ai * sr + x_ref[g, r, :, im]
            x_ref[g, r, :, re] = nr
            x_ref[g, r, :, im] = ni
            out.append((nr, ni))
        return tuple(out)

    zero = jnp.zeros((SUBLANES, W), _F32)
    ends = lax.fori_loop(0, seg, local, tuple((zero, zero) for _ in range(G)), unroll=SCAN_UNROLL)

    entry = []
    shift = (SUBLANES - 1) if reverse else 1
    for g in range(G):
        er, ei = ends[g]
        fr = jnp.where(edge_in, carry_ref[g, :, re], pltpu.roll(er, shift, 0))
        fi = jnp.where(edge_in, carry_ref[g, :, im], pltpu.roll(ei, shift, 0))
        for j, dist in enumerate((1, 2, 4)):
            pr, pi = tab_ref[g, 2 + 2 * j], tab_ref[g, 3 + 2 * j]
            sh = (SUBLANES - dist) if reverse else dist
            sr, si = pltpu.roll(fr, sh, 0), pltpu.roll(fi, sh, 0)
            fr, fi = fr + pr * sr - pi * si, fi + pr * si + pi * sr
        br, bi = tab_ref[g, 8], tab_ref[g, 9]
        outr = br * fr - bi * fi + er
        outi = br * fi + bi * fr + ei
        carry_ref[g, :, re] = jnp.broadcast_to(outr[edge_out:edge_out + 1, :], (SUBLANES, W))
        carry_ref[g, :, im] = jnp.broadcast_to(outi[edge_out:edge_out + 1, :], (SUBLANES, W))
        entry.append((fr, fi))

    keep_prev = xprev_ref is not None and not reverse
    add_da = da_ref is not None

    def fix(r, st):
        out = []
        for g in range(G):
            fr, fi = entry[g]
            pwr, pwi = pw_ref[g, r, :, re], pw_ref[g, r, :, im]
            xr = x_ref[g, r, :, re] + (pwr * fr - pwi * fi)
            xi = x_ref[g, r, :, im] + (pwr * fi + pwi * fr)
            x_ref[g, r, :, re] = xr
            x_ref[g, r, :, im] = xi
            if keep_prev:
                xprev_ref[g, r, :, re] = st[g][0]
                xprev_ref[g, r, :, im] = st[g][1]
                out.append((xr, xi))
            elif add_da:
                xpr, xpi = xprev_ref[g, r, :, re], xprev_ref[g, r, :, im]
                out.append((st[g][0] + (xr * xpr + xi * xpi), st[g][1] + (xi * xpr - xr * xpi)))
            else:
                out.append(st[g])
        return tuple(out)

    if keep_prev:
        init = tuple(entry)
    elif add_da:
        init = tuple((zero, zero) for _ in range(G))
    else:
        init = tuple((zero[0:1, 0:LANES], zero[0:1, 0:LANES]) for _ in range(G))
    st = lax.fori_loop(0, seg, fix, init, unroll=SCAN_UNROLL)
    if add_da:
        for g in range(G):
            da_ref[g, :, re] += st[g][0]
            da_ref[g, :, im] += st[g][1]


def _s5_specs(seg):
    G = KB_PER_STEP
    return dict(
        b=pl.BlockSpec((G, LANES, 2 * KB_STATES), lambda kb, t: (kb, 0, 0)),
        c=pl.BlockSpec((G, 2 * KB_STATES, LANES), lambda kb, t: (kb, 0, 0)),
        tab=pl.BlockSpec((G, 10, SUBLANES, KB_STATES), lambda kb, t: (kb, 0, 0, 0)),
        pw=pl.BlockSpec((G, seg, 1, 2 * KB_STATES), lambda kb, t: (kb, 0, 0, 0)),
        d=pl.BlockSpec((1, G * LANES), lambda kb, t: (0, kb)),
    )


def _s5_fwd(u, bmat, cmat, tab_f, pw_f, d_skip, tb):
    L = u.shape[0]
    nt = L // tb
    seg = tb // SUBLANES
    G = KB_PER_STEP
    ucol = pl.BlockSpec((tb, G * LANES), lambda kb, t: (t, kb))
    sp = _s5_specs(seg)

    def body(u_ref, b_ref, c_ref, tab_ref, pw_ref, d_ref, s_ref, cin_ref, up_scr, y_scr, x_scr, carry_scr):
        @pl.when(pl.program_id(1) == 0)
        def _():
            carry_scr[...] = jnp.zeros_like(carry_scr)

        cin_ref[:, 0] = carry_scr[...]
        _rows_to_segments(up_scr, u_ref, seg)
        for g in range(G):
            x_scr[g] = _dot(up_scr[g].astype(_BF), b_ref[g]).reshape(seg, SUBLANES, 2 * KB_STATES)
        _scan_segments(x_scr, tab_ref, pw_ref, carry_scr, seg, reverse=False)
        for g in range(G):
            y = _dot(x_scr[g].reshape(tb, 2 * KB_STATES).astype(_BF), c_ref[g])
            y_scr[g] = y + d_ref[:, g * LANES:(g + 1) * LANES] * up_scr[g]
        _segments_to_rows(s_ref, y_scr, seg)

    return pl.pallas_call(
        body, name="s5_fwd", grid=(N_KB // G, nt),
        in_specs=[ucol, sp["b"], sp["c"], sp["tab"], sp["pw"], sp["d"]],
        out_specs=[ucol, pl.BlockSpec((G, 1, SUBLANES, 2 * KB_STATES), lambda kb, t: (kb, t, 0, 0))],
        out_shape=[jax.ShapeDtypeStruct((L, SSM_W), _F32),
                   jax.ShapeDtypeStruct((N_KB, nt, SUBLANES, 2 * KB_STATES), _F32)],
        scratch_shapes=[pltpu.VMEM((G, tb, LANES), _F32)] * 2
        + [pltpu.VMEM((G, seg, SUBLANES, 2 * KB_STATES), _F32), pltpu.VMEM((G, SUBLANES, 2 * KB_STATES), _F32)],
        compiler_params=_params("parallel", "arbitrary"),
    )(u, bmat, cmat, tab_f, pw_f, d_skip)


def _mixout_fwd(s, y_ret, x, w_glu, w_out, g2, tm):
    L = s.shape[0]

    def body(s_ref, yr_ref, x_ref, wg_ref, wo_ref, g_ref, ys_ref, glu_ref, cat_ref, mix_ref, x2_ref):
        ys = _gelu(s_ref[...]).astype(_BF)
        ys_ref[...] = ys
        glu = _dot(ys, wg_ref[...])
        glu_ref[...] = glu
        cat_ref[:, :RET_W] = yr_ref[...]
        cat_ref[:, RET_W:] = (glu[:, :SSM_W] * _sigmoid(glu[:, SSM_W:])).astype(_BF)
        mix = _dot(cat_ref[...], wo_ref[...])
        mix_ref[...] = mix
        x2_ref[...] = x_ref[...] + mix * _rms_r(mix) * g_ref[...]

    return pl.pallas_call(
        body, name="mixout_fwd", grid=(L // tm,),
        in_specs=[_row_spec(tm, SSM_W), _row_spec(tm, RET_W), _row_spec(tm, D_MODEL),
                  _weight_spec((SSM_W, 2 * SSM_W)), _weight_spec((D_MODEL, D_MODEL)), _full_spec((1, D_MODEL))],
        out_specs=[_row_spec(tm, SSM_W), _row_spec(tm, 2 * SSM_W), _row_spec(tm, D_MODEL),
                   _row_spec(tm, D_MODEL), _row_spec(tm, D_MODEL)],
        out_shape=[jax.ShapeDtypeStruct((L, SSM_W), _BF), jax.ShapeDtypeStruct((L, 2 * SSM_W), _F32),
                   jax.ShapeDtypeStruct((L, D_MODEL), _BF), jax.ShapeDtypeStruct((L, D_MODEL), _F32),
                   jax.ShapeDtypeStruct((L, D_MODEL), _F32)],
        compiler_params=_params("parallel"),
    )(s, y_ret, x, w_glu, w_out, g2)


FF1_COLS = D_FF // N_DEV


def _ff1_fwd(x2, g3, w1, tm):
    L = x2.shape[0]

    def body(x_ref, g_ref, w_ref, h_ref, f_ref):
        xv = x_ref[...]
        h = (xv * _rms_r(xv) * g_ref[...]).astype(_BF)
        h_ref[...] = h
        for j in range(N_DEV):
            f_ref[:, j * FF1_COLS:(j + 1) * FF1_COLS] = _dot(h, w_ref[j])

    return pl.pallas_call(
        body, name="ff1_fwd", grid=(L // tm,),
        in_specs=[_row_spec(tm, D_MODEL), _full_spec((1, D_MODEL)), _weight_spec((N_DEV, D_MODEL, FF1_COLS))],
        out_specs=[_row_spec(tm, D_MODEL), _row_spec(tm, D_FF)],
        out_shape=[jax.ShapeDtypeStruct((L, D_MODEL), _BF), jax.ShapeDtypeStruct((L, D_FF), _F32)],
        compiler_params=_params("parallel"),
    )(x2, g3, w1)


def _ff2_loss(f1, x2, tgt, g4, w2, tm):
    L = f1.shape[0]

    def body(f_ref, x_ref, t_ref, g_ref, w_ref, dy_ref, dm_ref, dg_ref, ls_ref):
        @pl.when(pl.program_id(0) == 0)
        def _():
            dg_ref[...] = jnp.zeros_like(dg_ref)
            ls_ref[...] = jnp.zeros_like(ls_ref)

        g = g_ref[...]
        for rows in _row_chunks(tm):
            rl = jnp.maximum(f_ref[rows, :], 0.0)
            m = _dot((rl * rl).astype(_BF), w_ref[...])
            y = x_ref[rows, :] + m * _rms_r(m) * g
            err = y - t_ref[rows, :]
            ls_ref[...] += jnp.sum(err * err, axis=0, keepdims=True)
            dy = err * (1.0 / D_MODEL)
            dy_ref[rows, :] = dy
            dm, dgr = _rms_bwd(m, g, dy)
            dm_ref[rows, :] = dm.astype(_BF)
            dg_ref[...] += jnp.sum(dgr, axis=0, keepdims=True)

    return pl.pallas_call(
        body, name="ff2_loss", grid=(L // tm,),
        in_specs=[_row_spec(tm, D_FF), _row_spec(tm, D_MODEL), _row_spec(tm, D_MODEL),
                  _full_spec((1, D_MODEL)), _weight_spec((D_FF, D_MODEL))],
        out_specs=[_row_spec(tm, D_MODEL), _row_spec(tm, D_MODEL), _full_spec((1, D_MODEL)), _full_spec((1, D_MODEL))],
        out_shape=[jax.ShapeDtypeStruct((L, D_MODEL), _F32), jax.ShapeDtypeStruct((L, D_MODEL), _BF),
                   jax.ShapeDtypeStruct((1, D_MODEL), _F32), jax.ShapeDtypeStruct((1, D_MODEL), _F32)],
        compiler_params=_params("arbitrary"),
    )(f1, x2, tgt, g4, w2)


def _ff2_bwd(dm, f1, w2, tm, tn):
    L = dm.shape[0]
    last = L // tm - 1

    def body(dm_ref, f_ref, w_ref, df_ref, dw_ref, acc):
        @pl.when(pl.program_id(1) == 0)
        def _():
            acc[...] = jnp.zeros_like(acc)

        dmv = dm_ref[...]
        rl = jnp.maximum(f_ref[...], 0.0)
        df_ref[...] = (_dot_nt(dmv, w_ref[...]) * (2.0 * rl)).astype(_BF)
        acc[...] += _dot_tn((rl * rl).astype(_BF), dmv)

        @pl.when(pl.program_id(1) == last)
        def _():
            dw_ref[...] = acc[...].astype(_BF)

    return pl.pallas_call(
        body, name="ff2_bwd", grid=(D_FF // tn, L // tm),
        in_specs=[pl.BlockSpec((tm, D_MODEL), lambda j, i: (i, 0)), pl.BlockSpec((tm, tn), lambda j, i: (i, j)),
                  pl.BlockSpec((tn, D_MODEL), lambda j, i: (j, 0))],
        out_specs=[pl.BlockSpec((tm, tn), lambda j, i: (i, j)), pl.BlockSpec((tn, D_MODEL), lambda j, i: (j, 0))],
        out_shape=[jax.ShapeDtypeStruct((L, D_FF), _BF), jax.ShapeDtypeStruct((D_FF, D_MODEL), _BF)],
        scratch_shapes=[pltpu.VMEM((tn, D_MODEL), _F32)],
        compiler_params=_params("parallel", "arbitrary"),
    )(dm, f1, w2)


def _ff1_bwd(df1, w1, x2, mix, dy, g3, g2, tm):
    L = df1.shape[0]

    def body(df_ref, w_ref, x2_ref, mix_ref, dy_ref, g3_ref, g2_ref, dx2_ref, dmix_ref, dg3_ref, dg2_ref):
        @pl.when(pl.program_id(0) == 0)
        def _():
            dg3_ref[...] = jnp.zeros_like(dg3_ref)
            dg2_ref[...] = jnp.zeros_like(dg2_ref)

        for rows in _row_chunks(tm):
            dh = _dot_nt(df_ref[rows, 0:FF1_COLS], w_ref[0])
            for j in range(1, N_DEV):
                dh = dh + _dot_nt(df_ref[rows, j * FF1_COLS:(j + 1) * FF1_COLS], w_ref[j])
            dz, dgr = _rms_bwd(x2_ref[rows, :], g3_ref[...], dh)
            dg3_ref[...] += jnp.sum(dgr, axis=0, keepdims=True)
            dx2 = dy_ref[rows, :] + dz
            dx2_ref[rows, :] = dx2
            dmx, dgr2 = _rms_bwd(mix_ref[rows, :], g2_ref[...], dx2)
            dg2_ref[...] += jnp.sum(dgr2, axis=0, keepdims=True)
            dmix_ref[rows, :] = dmx.astype(_BF)

    vec = _full_spec((1, D_MODEL))
    return pl.pallas_call(
        body, name="ff1_bwd", grid=(L // tm,),
        in_specs=[_row_spec(tm, D_FF), _weight_spec((N_DEV, D_MODEL, FF1_COLS)), _row_spec(tm, D_MODEL),
                  _row_spec(tm, D_MODEL), _row_spec(tm, D_MODEL), vec, vec],
        out_specs=[_row_spec(tm, D_MODEL), _row_spec(tm, D_MODEL), vec, vec],
        out_shape=[jax.ShapeDtypeStruct((L, D_MODEL), _F32), jax.ShapeDtypeStruct((L, D_MODEL), _BF),
                   jax.ShapeDtypeStruct((1, D_MODEL), _F32), jax.ShapeDtypeStruct((1, D_MODEL), _F32)],
        compiler_params=_params("arbitrary"),
    )(df1, w1, x2, mix, dy, g3, g2)


def _matmul_tn(a, b, tm, tn, name, slots=False):
    L, K = a.shape
    N = b.shape[1]
    last = L // tm - 1

    def body(a_ref, b_ref, o_ref, acc):
        @pl.when(pl.program_id(1) == 0)
        def _():
            acc[...] = jnp.zeros_like(acc)

        acc[...] += _dot_tn(a_ref[...].astype(_BF), b_ref[...].astype(_BF))

        @pl.when(pl.program_id(1) == last)
        def _():
            if slots:
                o_ref[0] = acc[...].astype(_BF)
            else:
                o_ref[...] = acc[...].astype(_BF)

    if slots:
        out_spec = pl.BlockSpec((1, K, tn), lambda j, i: (j, 0, 0))
        out_shape = jax.ShapeDtypeStruct((N // tn, K, tn), _BF)
    else:
        out_spec = pl.BlockSpec((K, tn), lambda j, i: (0, j))
        out_shape = jax.ShapeDtypeStruct((K, N), _BF)
    return pl.pallas_call(
        body, name=name, grid=(N // tn, L // tm),
        in_specs=[pl.BlockSpec((tm, K), lambda j, i: (i, 0)), pl.BlockSpec((tm, tn), lambda j, i: (i, j))],
        out_specs=out_spec, out_shape=out_shape,
        scratch_shapes=[pltpu.VMEM((K, tn), _F32)],
        compiler_params=_params("parallel", "arbitrary"),
    )(a, b)


def _mixout_bwd(dmix, w_out, w_glu, glu, s, o, gate, ggn, tm):
    L = dmix.shape[0]

    def body(dmix_ref, wo_ref, wg_ref, glu_ref, s_ref, o_ref, gate_ref, ggn_ref,
             dglu_ref, ds_ref, dgate_ref, do_ref, dggn_ref):
        @pl.when(pl.program_id(0) == 0)
        def _():
            dggn_ref[...] = jnp.zeros_like(dggn_ref)

        dcat = _dot_nt(dmix_ref[...], wo_ref[...])
        dy_ret, dy_ssm = dcat[:, :RET_W], dcat[:, RET_W:]
        glu = glu_ref[...]
        ga, sg = glu[:, :SSM_W], _sigmoid(glu[:, SSM_W:])
        dga = (dy_ssm * sg).astype(_BF)
        dgb = (dy_ssm * ga * sg * (1.0 - sg)).astype(_BF)
        dglu_ref[:, :SSM_W] = dga
        dglu_ref[:, SSM_W:] = dgb
        dys = _dot_nt(dga, wg_ref[:, :SSM_W]) + _dot_nt(dgb, wg_ref[:, SSM_W:])
        ds_ref[...] = dys * _gelu_grad(s_ref[...])
        gt = gate_ref[...]
        sgt = _sigmoid(gt)
        ggn = ggn_ref[...]
        for hh in range(N_HEAD):
            cols = slice(hh * HEAD_D, (hh + 1) * HEAD_D)
            ov = o_ref[:, cols]
            dlt = ov - jnp.mean(ov, axis=-1, keepdims=True)
            rstd = lax.rsqrt(jnp.mean(dlt * dlt, axis=-1, keepdims=True) + NORM_EPS)
            on = dlt * rstd
            dyr = dy_ret[:, cols] * (gt[:, cols] * sgt[:, cols])
            dgate_ref[:, cols] = dy_ret[:, cols] * (on * ggn[:, cols]) * (sgt[:, cols] * (1.0 + gt[:, cols] * (1.0 - sgt[:, cols])))
            dggn_ref[:, cols] += jnp.sum(dyr * on, axis=0, keepdims=True)
            don = dyr * ggn[:, cols]
            do = rstd * (don - jnp.mean(don, axis=-1, keepdims=True) - on * jnp.mean(don * on, axis=-1, keepdims=True))
            do_ref[:, cols] = do.astype(_BF)

    return pl.pallas_call(
        body, name="mixout_bwd", grid=(L // tm,),
        in_specs=[_row_spec(tm, D_MODEL), _weight_spec((D_MODEL, D_MODEL)), _weight_spec((SSM_W, 2 * SSM_W)),
                  _row_spec(tm, 2 * SSM_W), _row_spec(tm, SSM_W), _row_spec(tm, RET_W), _row_spec(tm, RET_W),
                  _full_spec((1, RET_W))],
        out_specs=[_row_spec(tm, 2 * SSM_W), _row_spec(tm, SSM_W), _row_spec(tm, RET_W), _row_spec(tm, RET_W),
                   _full_spec((1, RET_W))],
        out_shape=[jax.ShapeDtypeStruct((L, 2 * SSM_W), _BF), jax.ShapeDtypeStruct((L, SSM_W), _F32),
                   jax.ShapeDtypeStruct((L, RET_W), _F32), jax.ShapeDtypeStruct((L, RET_W), _BF),
                   jax.ShapeDtypeStruct((1, RET_W), _F32)],
        compiler_params=_params("arbitrary"),
    )(dmix, w_out, w_glu, glu, s, o, gate, ggn)


def _s5_bwd(u, ds, cin, bmat, cmat, tab_f, pw_f, tab_r, pw_r, d_skip, tb):
    L = u.shape[0]
    nt = L // tb
    seg = tb // SUBLANES
    G = KB_PER_STEP
    rcol = pl.BlockSpec((tb, G * LANES), lambda kb, t: (nt - 1 - t, kb))
    sp = _s5_specs(seg)
    aspec = pl.BlockSpec((G, SUBLANES, 2 * KB_STATES), lambda kb, t: (kb, 0, 0))

    def body(u_ref, ds_ref, cin_ref, b_ref, c_ref, tf_ref, pf_ref, tr_ref, pr_ref, d_ref,
             du_ref, db_ref, dc_ref, da_ref, dd_ref, up_scr, dp_scr, x_scr, xp_scr, g_scr, fc_scr, lc_scr):
        @pl.when(pl.program_id(1) == 0)
        def _():
            lc_scr[...] = jnp.zeros_like(lc_scr)
            db_ref[...] = jnp.zeros_like(db_ref)
            dc_ref[...] = jnp.zeros_like(dc_ref)
            da_ref[...] = jnp.zeros_like(da_ref)
            dd_ref[...] = jnp.zeros_like(dd_ref)

        _rows_to_segments(up_scr, u_ref, seg)
        _rows_to_segments(dp_scr, ds_ref, seg)
        fc_scr[...] = cin_ref[:, 0]
        for g in range(G):
            x_scr[g] = _dot(up_scr[g].astype(_BF), b_ref[g]).reshape(seg, SUBLANES, 2 * KB_STATES)
            g_scr[g] = _dot_nt(dp_scr[g].astype(_BF), c_ref[g]).reshape(seg, SUBLANES, 2 * KB_STATES)
        _scan_segments(x_scr, tf_ref, pf_ref, fc_scr, seg, reverse=False, xprev_ref=xp_scr)
        _scan_segments(g_scr, tr_ref, pr_ref, lc_scr, seg, reverse=True, xprev_ref=xp_scr, da_ref=da_ref)
        for g in range(G):
            cols = slice(g * LANES, (g + 1) * LANES)
            uv, dsv = up_scr[g], dp_scr[g]
            ub, dsb = uv.astype(_BF), dsv.astype(_BF)
            lamb = g_scr[g].reshape(tb, 2 * KB_STATES).astype(_BF)
            db_ref[g] += _dot_tn(ub, lamb)
            dc_ref[g] += _dot_tn(x_scr[g].reshape(tb, 2 * KB_STATES).astype(_BF), dsb)
            dd_ref[:, cols] += jnp.sum(dsv * uv, axis=0, keepdims=True)
            up_scr[g] = _dot_nt(lamb, b_ref[g]) + d_ref[:, cols] * dsv
        _segments_to_rows(du_ref, up_scr, seg)

    state = pltpu.VMEM((G, seg, SUBLANES, 2 * KB_STATES), _F32)
    carry = pltpu.VMEM((G, SUBLANES, 2 * KB_STATES), _F32)
    return pl.pallas_call(
        body, name="s5_bwd", grid=(N_KB // G, nt),
        in_specs=[rcol, rcol, pl.BlockSpec((G, 1, SUBLANES, 2 * KB_STATES), lambda kb, t: (kb, nt - 1 - t, 0, 0)),
                  sp["b"], sp["c"], sp["tab"], sp["pw"], sp["tab"], sp["pw"], sp["d"]],
        out_specs=[rcol, sp["b"], sp["c"], aspec, sp["d"]],
        out_shape=[jax.ShapeDtypeStruct((L, SSM_W), _F32),
                   jax.ShapeDtypeStruct((N_KB, LANES, 2 * KB_STATES), _F32),
                   jax.ShapeDtypeStruct((N_KB, 2 * KB_STATES, LANES), _F32),
                   jax.ShapeDtypeStruct((N_KB, SUBLANES, 2 * KB_STATES), _F32),
                   jax.ShapeDtypeStruct((1, SSM_W), _F32)],
        scratch_shapes=[pltpu.VMEM((G, tb, LANES), _F32)] * 2 + [state] * 3 + [carry] * 2,
        compiler_params=_params("parallel", "arbitrary"),
    )(u, ds, cin, bmat, cmat, tab_f, pw_f, tab_r, pw_r, d_skip)


def _retention_bwd(q, k, v, do, r_prev, consts, cosf, sinf):
    L = q.shape[0]
    nc = L // CHUNK
    blk = pl.BlockSpec((CHUNK, RET_W), lambda n: (nc - 1 - n, 0))
    rope_blk = pl.BlockSpec((CHUNK, HEAD_D), lambda n: (nc - 1 - n, 0))

    def body(q_ref, k_ref, v_ref, do_ref, rp_ref, dm_ref, xi_ref, zeta_ref, gc_ref, cos_ref, sin_ref,
             dq_ref, dk_ref, dv_ref, g_scr):
        @pl.when(pl.program_id(0) == 0)
        def _():
            g_scr[...] = jnp.zeros_like(g_scr)

        cs, sn = cos_ref[...], sin_ref[...]
        for hh in range(N_HEAD):
            cols = slice(hh * HEAD_D, (hh + 1) * HEAD_D)
            qv, kv, vv, dov = q_ref[:, cols], k_ref[:, cols], v_ref[:, cols], do_ref[:, cols]
            rb = rp_ref[hh, 0].astype(_BF)
            gst = g_scr[hh]
            gb = gst.astype(_BF)
            dm, zeta = dm_ref[hh], zeta_ref[hh]
            sb = (_dot_nt(qv, kv) * dm).astype(_BF)
            dab = (_dot_nt(dov, vv) * dm).astype(_BF)
            dox = (dov.astype(_F32) * xi_ref[hh]).astype(_BF)
            vz = (vv.astype(_F32) * zeta).astype(_BF)
            dq = _dot(dab, kv) + _dot_nt(dox, rb)
            dk = _dot_tn(dab, qv) + _dot_nt(vz, gb)
            dv = _dot_tn(sb, dov) + _dot(kv, gb) * zeta
            g_scr[hh] = gc_ref[hh, 0:1, :] * gst + _dot_tn(qv, dox)
            dq_ref[:, cols] = _rope_t(dq, cs, sn).astype(_BF)
            dk_ref[:, cols] = (_rope_t(dk, cs, sn) * (HEAD_D ** -0.5)).astype(_BF)
            dv_ref[:, cols] = dv.astype(_BF)

    return pl.pallas_call(
        body, name="retention_bwd", grid=(nc,),
        in_specs=[blk, blk, blk, blk, pl.BlockSpec((N_HEAD, 1, HEAD_D, HEAD_D), lambda n: (0, nc - 1 - n, 0, 0))]
        + _head_specs() + [rope_blk, rope_blk],
        out_specs=[blk, blk, blk],
        out_shape=[jax.ShapeDtypeStruct((L, RET_W), _BF)] * 3,
        scratch_shapes=[pltpu.VMEM((N_HEAD, HEAD_D, HEAD_D), _F32)],
        compiler_params=_params("arbitrary"),
    )(q, k, v, do, r_prev, *consts, cosf, sinf)


def _inproj_bwd(pieces, w_in, x, dx2, g1, tm):
    L = x.shape[0]

    def body(p0, p1, p2, p3, p4, w_ref, x_ref, dx2_ref, g_ref, dx_ref, dg_ref):
        @pl.when(pl.program_id(0) == 0)
        def _():
            dg_ref[...] = jnp.zeros_like(dg_ref)

        dh = None
        for j, p in enumerate((p0, p1, p2, p3, p4)):
            part = _dot_nt(p[...].astype(_BF), w_ref[:, j * RET_W:(j + 1) * RET_W])
            dh = part if dh is None else dh + part
        dz, dgr = _rms_bwd(x_ref[...], g_ref[...], dh)
        dx_ref[...] = dx2_ref[...] + dz
        dg_ref[...] += jnp.sum(dgr, axis=0, keepdims=True)

    return pl.pallas_call(
        body, name="inproj_bwd", grid=(L // tm,),
        in_specs=[_row_spec(tm, RET_W)] * 5 + [_weight_spec((D_MODEL, IN_COLS)), _row_spec(tm, D_MODEL),
                                                 _row_spec(tm, D_MODEL), _full_spec((1, D_MODEL))],
        out_specs=[_row_spec(tm, D_MODEL), _full_spec((1, D_MODEL))],
        out_shape=[jax.ShapeDtypeStruct((L, D_MODEL), _F32), jax.ShapeDtypeStruct((1, D_MODEL), _F32)],
        compiler_params=_params("arbitrary"),
    )(*pieces, w_in, x, dx2, g1)


def _sum_adamw(parts, w, m, v, tr, name):
    _, R, Cc = parts.shape

    def body(p_ref, w_ref, m_ref, v_ref, g_ref, d_ref, nm_ref, nv_ref):
        gv = p_ref[0].astype(_F32)
        for s in range(1, N_DEV):
            gv = gv + p_ref[s].astype(_F32)
        g_ref[...] = gv
        nm = ADAM_B1 * m_ref[...] + (1.0 - ADAM_B1) * gv
        nv = ADAM_B2 * v_ref[...] + (1.0 - ADAM_B2) * (gv * gv)
        m_hat = nm / (1.0 - ADAM_B1 ** ADAM_STEP)
        v_hat = nv / (1.0 - ADAM_B2 ** ADAM_STEP)
        d_ref[...] = -ADAM_LR * (m_hat / (jnp.sqrt(v_hat) + ADAM_EPS) + ADAM_WD * w_ref[...])
        nm_ref[...] = nm
        nv_ref[...] = nv

    spec = _row_spec(tr, Cc)
    return pl.pallas_call(
        body, name=name, grid=(R // tr,),
        in_specs=[pl.BlockSpec((N_DEV, tr, Cc), lambda i: (0, i, 0))] + [spec] * 3, out_specs=[spec] * 4,
        out_shape=[jax.ShapeDtypeStruct((R, Cc), _F32)] * 4,
        compiler_params=_params("parallel"),
    )(parts, w, m, v)


def _my_place():
    return lax.axis_index("x"), lax.axis_index("y"), lax.axis_index("c")


def _all_gather(blocks):
    n = len(blocks)

    def body(*refs):
        x_refs, out_refs, done_ref = refs[:n], refs[n:2 * n], refs[2 * n]
        send_sems, recv_sems, local_sems = refs[2 * n + 1:]
        done_ref[...] = jnp.zeros_like(done_ref)
        x, y, c = _my_place()
        me, sibling = (x, y, c), (x, y, 1 - c)
        chips = [(1 - x, y), (x, 1 - y), (1 - x, 1 - y)]

        def slot(a, px, py, pc):
            return out_refs[a].at[4 * px + 2 * py + pc]

        def copy(a, k, blk, to, own=False):
            return pltpu.make_async_remote_copy(
                src_ref=x_refs[a] if own else slot(a, *blk), dst_ref=slot(a, *blk),
                send_sem=send_sems.at[a, k], recv_sem=recv_sems.at[a, k], device_id=to, device_id_type=MESH)

        mine = [pltpu.make_async_copy(x_refs[a], slot(a, *me), local_sems.at[a]) for a in range(n)]
        for cp in mine:
            cp.start()
        first = []
        for a in range(n):
            first.append(copy(a, 0, me, sibling, own=True))
            first += [copy(a, 1 + j, me, (*chip, c), own=True) for j, chip in enumerate(chips)]
        for cp in first:
            cp.start()
        passed = []
        for j, chip in enumerate(chips):
            for a in range(n):
                copy(a, 1 + j, (*chip, c), me).wait_recv()
                fwd = copy(a, 4 + j, (*chip, c), sibling)
                fwd.start()
                passed.append(fwd)
        for a in range(n):
            copy(a, 0, sibling, me).wait_recv()
            for j, chip in enumerate(chips):
                copy(a, 4 + j, (*chip, 1 - c), me).wait_recv()
        for cp in first + passed:
            cp.wait_send()
        for cp in mine:
            cp.wait()

    any_spec = pl.BlockSpec(memory_space=pl.ANY)
    outs = pl.pallas_call(
        body, name="weights_all_gather",
        in_specs=[any_spec] * n, out_specs=[any_spec] * n + [pl.BlockSpec(memory_space=pltpu.VMEM)],
        out_shape=[jax.ShapeDtypeStruct((N_DEV,) + b.shape, b.dtype) for b in blocks]
        + [jax.ShapeDtypeStruct((SUBLANES, LANES), _F32)],
        scratch_shapes=[pltpu.SemaphoreType.DMA((n, 7)), pltpu.SemaphoreType.DMA((n, 7)), pltpu.SemaphoreType.DMA((n,))],
    )(*blocks)
    return outs[:n], outs[n]


def _exchange(bigs, small):
    n = len(bigs)
    r = small.shape[0]

    def body(*refs):
        in_refs, out_refs = refs[:n + 1], refs[n + 1:2 * n + 2]
        send_sems, recv_sems, local_sems = refs[2 * n + 2:]
        x, y, c = _my_place()
        me = 4 * x + 2 * y + c
        own = [pltpu.make_async_copy(in_refs[a].at[me], out_refs[a].at[me], local_sems.at[a]) for a in range(n)]
        own.append(pltpu.make_async_copy(in_refs[n], out_refs[n].at[me], local_sems.at[n]))
        for cp in own:
            cp.start()
        copies = []
        for kk in range(1, N_DEV):
            px, py, pc = x ^ (kk >> 2), y ^ ((kk >> 1) & 1), c ^ (kk & 1)
            peer = 4 * px + 2 * py + pc
            for a in range(n + 1):
                src = in_refs[a].at[peer] if a < n else in_refs[a]
                copies.append(pltpu.make_async_remote_copy(
                    src_ref=src, dst_ref=out_refs[a].at[me],
                    send_sem=send_sems.at[a, kk - 1], recv_sem=recv_sems.at[a, kk - 1],
                    device_id=(px, py, pc), device_id_type=MESH))
        for cp in copies:
            cp.start()
        for cp in copies:
            cp.wait_recv()
        for cp in copies:
            cp.wait_send()
        for cp in own:
            cp.wait()

    any_spec = pl.BlockSpec(memory_space=pl.ANY)
    outs = pl.pallas_call(
        body, name="grad_exchange",
        in_specs=[any_spec] * (n + 1), out_specs=[any_spec] * (n + 1),
        out_shape=[jax.ShapeDtypeStruct(b.shape, b.dtype) for b in bigs]
        + [jax.ShapeDtypeStruct((N_DEV, r, LANES), small.dtype)],
        scratch_shapes=[pltpu.SemaphoreType.DMA((n + 1, 7)), pltpu.SemaphoreType.DMA((n + 1, 7)),
                        pltpu.SemaphoreType.DMA((n + 1,))],
    )(*bigs, small)
    return outs[:n], outs[n]


HBM_SPEC = pl.BlockSpec(memory_space=pltpu.HBM)
SEM_SPEC = pl.BlockSpec(memory_space=pltpu.SEMAPHORE)
DATAFLOW = pltpu.SideEffectType.DATAFLOW_SIDE_EFFECTING


def _my_index():
    x, y, c = _my_place()
    return 4 * x + 2 * y + c


def _landing(own_block):
    zone = lax.empty((N_DEV,) + own_block.shape, own_block.dtype)
    return lax.dynamic_update_index_in_dim(zone, own_block, _my_index(), 0)


def _split_copies(src_refs, land_refs, send_sems, recv_sems, gather):
    x, y, c = _my_place()
    me = 4 * x + 2 * y + c
    copies = []
    for kk in range(1, N_DEV):
        px, py, pc = x ^ (kk >> 2), y ^ ((kk >> 1) & 1), c ^ (kk & 1)
        peer = 4 * px + 2 * py + pc
        for a, (src, land) in enumerate(zip(src_refs, land_refs)):
            copies.append(pltpu.make_async_remote_copy(
                src_ref=src if gather else src.at[peer], dst_ref=land.at[me],
                send_sem=send_sems.at[a * 7 + kk - 1], recv_sem=recv_sems.at[a * 7 + kk - 1],
                device_id=(px, py, pc), device_id_type=MESH))
    return copies


def _split_start(srcs, lands, gather, name):
    n = len(srcs)

    def body(*refs):
        src_refs, land_refs = refs[:n], refs[n:2 * n]
        send_sems, recv_sems = refs[2 * n], refs[2 * n + 1]
        token = refs[-1]
        for cp in _split_copies(src_refs, land_refs, send_sems, recv_sems, gather):
            cp.start()
        token[...] = jnp.zeros_like(token)

    outs = pl.pallas_call(
        body, name=name,
        out_shape=(pltpu.SemaphoreType.DMA((7 * n,)), pltpu.SemaphoreType.DMA((7 * n,)),
                   *[pltpu.HBM(t.shape, t.dtype) for t in srcs], *[pltpu.HBM(t.shape, t.dtype) for t in lands],
                   jax.ShapeDtypeStruct((SUBLANES, LANES), _F32)),
        in_specs=[HBM_SPEC] * (2 * n),
        out_specs=(SEM_SPEC, SEM_SPEC, *[HBM_SPEC] * (2 * n), pl.BlockSpec(memory_space=pltpu.VMEM)),
        input_output_aliases={i: 2 + i for i in range(2 * n)},
        compiler_params=pltpu.CompilerParams(has_side_effects=DATAFLOW),
    )(*[pltpu.with_memory_space_constraint(t, pltpu.HBM) for t in list(srcs) + list(lands)])
    return outs[0], outs[1], outs[2:2 + n], outs[2 + n:2 + 2 * n], outs[-1]


def _split_wait(send_sems, recv_sems, srcs, lands, after, gather, name):
    n = len(srcs)

    def body(*refs):
        src_refs, land_refs = refs[:n], refs[n:2 * n]
        send_s, recv_s = refs[2 * n], refs[2 * n + 1]
        for cp in _split_copies(src_refs, land_refs, send_s, recv_s, gather):
            cp.wait_send()
            cp.wait_recv()

    outs = pl.pallas_call(
        body, name=name,
        out_shape=tuple(pltpu.HBM(t.shape, t.dtype) for t in list(srcs) + list(lands)),
        in_specs=[HBM_SPEC] * (2 * n) + [SEM_SPEC, SEM_SPEC, pl.BlockSpec(memory_space=pl.ANY)],
        out_specs=tuple([HBM_SPEC] * (2 * n)),
        input_output_aliases={i: i for i in range(2 * n)},
        compiler_params=pltpu.CompilerParams(has_side_effects=DATAFLOW),
    )(*srcs, *lands, send_sems, recv_sems, after)
    return outs[n:]


def _discretize(lam_re, lam_im, log_dt, b_re, b_im):
    lr = jnp.minimum(lam_re, -1e-4)
    li = lam_im
    dt = jnp.exp(log_dt)[:, None]
    er = jnp.exp(lr * dt)
    ar, ai = er * jnp.cos(li * dt), er * jnp.sin(li * dt)
    den = lr * lr + li * li
    cr = ((ar - 1.0) * lr + ai * li) / den
    ci = (ai * lr - (ar - 1.0) * li) / den
    bbr = cr[:, :, None] * b_re - ci[:, :, None] * b_im
    bbi = cr[:, :, None] * b_im + ci[:, :, None] * b_re
    return ar, ai, bbr, bbi


def _cmul(ar, ai, br, bi):
    return ar * br - ai * bi, ar * bi + ai * br


def _cpowers(ar, ai, n):
    pr, pi = ar[None], ai[None]
    while pr.shape[0] < n:
        nr, ni = _cmul(pr, pi, pr[-1][None], pi[-1][None])
        pr, pi = jnp.concatenate([pr, nr]), jnp.concatenate([pi, ni])
    return pr[:n], pi[:n]


def _scan_tables(ar, ai, seg, reverse):
    if reverse:
        ai = -ai
    ar, ai = ar.reshape(N_KB, KB_STATES), ai.reshape(N_KB, KB_STATES)
    pr, pi = _cpowers(ar, ai, seg)
    a1 = (pr[-1], pi[-1])
    a2 = _cmul(*a1, *a1)
    a4 = _cmul(*a2, *a2)
    row = jnp.arange(SUBLANES)[None, :, None]
    wide = lambda t: jnp.broadcast_to(t[:, None, :], (N_KB, SUBLANES, KB_STATES))
    tabs = [wide(ar), wide(ai)]
    for dist, (qr, qi) in ((1, a1), (2, a2), (4, a4)):
        keep = (row < SUBLANES - dist) if reverse else (row >= dist)
        tabs += [jnp.where(keep, wide(qr), 0.0), jnp.where(keep, wide(qi), 0.0)]
    tabs += [wide(a1[0]), wide(a1[1])]
    if reverse:
        pr, pi = pr[::-1], pi[::-1]
    pw = jnp.transpose(jnp.concatenate([pr, pi], axis=-1), (1, 0, 2))[:, :, None, :]
    return jnp.stack(tabs, axis=1).astype(_F32), pw.astype(_F32)


def _block_diag_in(br, bi):
    eye = jnp.eye(GROUPS_PER_KB, dtype=_F32)
    one = lambda t: jnp.einsum("kgpc,gh->kgchp", t.reshape(N_KB, GROUPS_PER_KB, N_STATE, SSM_GC), eye).reshape(
        N_KB, LANES, KB_STATES)
    return jnp.concatenate([one(br), one(bi)], axis=-1)


def _block_diag_in_t(dmat):
    d6 = dmat.reshape(N_KB, GROUPS_PER_KB, SSM_GC, 2, GROUPS_PER_KB, N_STATE)
    eye = jnp.eye(GROUPS_PER_KB, dtype=_F32)
    both = jnp.einsum("kgcrhp,gh->rkgpc", d6, eye).reshape(2, N_GROUP, N_STATE, SSM_GC)
    return both[0], both[1]


def _block_diag_out(c_re, c_im):
    eye = jnp.eye(GROUPS_PER_KB, dtype=_F32)
    one = lambda t: jnp.einsum("kgcp,gh->khpgc", t.reshape(N_KB, GROUPS_PER_KB, SSM_GC, N_STATE), eye).reshape(
        N_KB, KB_STATES, LANES)
    return jnp.concatenate([one(c_re), -one(c_im)], axis=1)


def _block_diag_out_t(dmat):
    d6 = dmat.reshape(N_KB, 2, GROUPS_PER_KB, N_STATE, GROUPS_PER_KB, SSM_GC)
    eye = jnp.eye(GROUPS_PER_KB, dtype=_F32)
    both = jnp.einsum("krhpgc,gh->rkgcp", d6, eye).reshape(2, N_GROUP, SSM_GC, N_STATE)
    return both[0], -both[1]


SMALL_NAMES = ("norm_mix_pre", "norm_mix_post", "ret_gn_gain", "ssm_lambda_re", "ssm_lambda_im", "ssm_log_dt",
               "ssm_b_re", "ssm_b_im", "ssm_c_re", "ssm_c_im", "ssm_d", "norm_mlp_pre", "norm_mlp_post")


def _local_grads(x, tgt, small, weights, emit, tm, tk, tb, zero=0.0):
    L = x.shape[0]
    g1, g2, ggn = small["norm_mix_pre"], small["norm_mix_post"], small["ret_gn_gain"]
    g3, g4, d_skip = small["norm_mlp_pre"], small["norm_mlp_post"], small["ssm_d"]

    half = HEAD_D // 2
    inv_freq = ROPE_BASE ** (-jnp.arange(half, dtype=_F32) / half)
    ang = jnp.arange(L, dtype=_F32)[:, None] * inv_freq[None, :] + zero
    cosf = jnp.concatenate([jnp.cos(ang), jnp.cos(ang)], axis=-1)
    sinf = jnp.concatenate([-jnp.sin(ang), jnp.sin(ang)], axis=-1)
    consts = _ret_consts()

    disc_in = (small["ssm_lambda_re"][0], small["ssm_lambda_im"][0], small["ssm_log_dt"][0],
               small["ssm_b_re"][0], small["ssm_b_im"][0])
    (ar, ai, bbr, bbi), disc_vjp = jax.vjp(_discretize, *disc_in)
    bmat = _block_diag_in(bbr, bbi).astype(_BF)
    cmat = _block_diag_out(small["ssm_c_re"][0], small["ssm_c_im"][0]).astype(_BF)
    seg = tb // SUBLANES
    tab_f, pw_f = _scan_tables(ar, ai, seg, False)
    tab_r, pw_r = _scan_tables(ar, ai, seg, True)

    (w_in,) = weights("in", sinf)
    h1, q, k, v, gate, u = _inproj_fwd(x, g1, w_in, cosf, sinf, tm)
    o, y_ret, r_prev = _retention_fwd(q, k, v, gate, ggn, consts)
    s, cin = _s5_fwd(u, bmat, cmat, tab_f, pw_f, d_skip, tb)
    w_glu, w_out = weights("mix", s)
    ys, glu, cat, mix, x2 = _mixout_fwd(s, y_ret, x, w_glu, w_out, g2, tm)
    w_ff1, w_ff2 = weights("mlp", x2)
    h3, f1 = _ff1_fwd(x2, g3, w_ff1, tm)
    dy, dm, dg4, sq = _ff2_loss(f1, x2, tgt, g4, w_ff2, min(2 * tm, L))

    df1, dw_ff2 = _ff2_bwd(dm, f1, w_ff2, min(1024, L), 1024)
    dx2, dmix, dg3, dg2 = _ff1_bwd(df1, w_ff1, x2, mix, dy, g3, g2, min(2 * tm, L))
    dw_ff1 = _matmul_tn(h3, df1, tk, FF1_COLS, "dw_ff1", slots=True)
    zero = emit({"w_ff1": dw_ff1, "w_ff2": dw_ff2})
    dglu, ds, dgate, do, dggn = _mixout_bwd(dmix, w_out, w_glu, glu, s, o, gate, ggn if zero is None else ggn + zero, tm)
    dw_out = _matmul_tn(cat, dmix, tk, 1024, "dw_out")
    dw_glu = _matmul_tn(ys, dglu, tk, 1024, "dw_glu")
    zero = emit({"w_glu": dw_glu, "w_out": dw_out})
    du, dbmat, dcmat, da8, dd = _s5_bwd(u, ds, cin, bmat, cmat, tab_f, pw_f, tab_r, pw_r,
                                        d_skip if zero is None else d_skip + zero, tb)
    dq, dk, dv = _retention_bwd(q, k, v, do, r_prev, consts, cosf, sinf)
    pieces = (dq, dk, dv, dgate, du)
    dw_in = jnp.concatenate([_matmul_tn(h1, p, tk, RET_W, "dw_in_%d" % j) for j, p in enumerate(pieces)], axis=1)
    zero = emit({"w_in": dw_in})
    gx, dg1 = _inproj_bwd(pieces, w_in, x, dx2, g1 if zero is None else g1 + zero, tm)

    da = jnp.sum(da8, axis=1)
    dar = da[:, :KB_STATES].reshape(N_GROUP, N_STATE)
    dai = da[:, KB_STATES:].reshape(N_GROUP, N_STATE)
    dbr, dbi = _block_diag_in_t(dbmat)
    dlre, dlim, dldt, dbre, dbim = disc_vjp((dar, dai, dbr, dbi))
    dcre, dcim = _block_diag_out_t(dcmat)

    gsmall = {
        "norm_mix_pre": dg1, "norm_mix_post": dg2, "ret_gn_gain": dggn,
        "ssm_lambda_re": dlre[None], "ssm_lambda_im": dlim[None], "ssm_log_dt": dldt[None],
        "ssm_b_re": dbre[None], "ssm_b_im": dbim[None], "ssm_c_re": dcre[None], "ssm_c_im": dcim[None],
        "ssm_d": dd, "norm_mlp_pre": dg3, "norm_mlp_post": dg4,
    }
    return sq, gx, gsmall


BIG_SHAPES = {"w_in": (D_MODEL, IN_COLS // N_DEV), "w_glu": (SSM_W, 2 * SSM_W // N_DEV), "w_out": (D_MODEL // N_DEV, D_MODEL),
              "w_ff1": (D_MODEL, FF1_COLS), "w_ff2": (D_FF // N_DEV, D_MODEL)}
BIG_NAMES = ("w_in", "w_glu", "w_out", "w_ff1", "w_ff2")


def _cols_from_slots(g):
    return jnp.transpose(g, (1, 0, 2)).reshape(g.shape[1], N_DEV * g.shape[2])


def _cols_to_slots(dw):
    r, cols = dw.shape
    return jnp.transpose(dw.reshape(r, N_DEV, cols // N_DEV), (1, 0, 2))


WEIGHT_GROUPS = {"in": ("w_in",), "mix": ("w_glu", "w_out"), "mlp": ("w_ff1", "w_ff2")}


def _weight_from_slots(name, g):
    if name in ("w_in", "w_glu"):
        return _cols_from_slots(g)
    if name == "w_ff1":
        return g
    return g.reshape(N_DEV * g.shape[1], g.shape[2])


def _grad_slots(name, dw):
    if name in ("w_in", "w_glu"):
        return _cols_to_slots(dw)
    if name == "w_ff1":
        return dw
    return dw.reshape((N_DEV,) + BIG_SHAPES[name])


PIECE_ROWS = 8


def _small_layout(shapes):
    off, rows = {}, 0
    for n in SMALL_NAMES:
        off[n] = rows
        rows += -(-math.prod(shapes[n]) // (PIECE_ROWS * LANES)) * PIECE_ROWS
    return off, rows, rows + PIECE_ROWS


def _pack_small(vals, shapes, last=None):
    parts = []
    for n in SMALL_NAMES:
        flat = vals[n].reshape(-1).astype(_F32)
        pad = -flat.shape[0] % (PIECE_ROWS * LANES)
        if pad:
            flat = jnp.concatenate([flat, jnp.zeros((pad,), _F32)])
        parts.append(flat.reshape(-1, LANES))
    parts.append(jnp.zeros((PIECE_ROWS, LANES), _F32) if last is None else last)
    return jnp.concatenate(parts, axis=0)


def _unpack_small(buf, shapes):
    off, _, _ = _small_layout(shapes)
    out = {}
    for n in SMALL_NAMES:
        size = math.prod(shapes[n])
        rows = -(-size // LANES)
        out[n] = buf[off[n]:off[n] + rows].reshape(-1)[:size].reshape(shapes[n])
    return out


WEIGHT_NAMES = ('norm_mix_pre', 'norm_mix_post', 'w_in', 'ret_gn_gain', 'ssm_lambda_re', 'ssm_lambda_im', 'ssm_log_dt',
                'ssm_b_re', 'ssm_b_im', 'ssm_c_re', 'ssm_c_im', 'ssm_d', 'w_glu', 'w_out', 'norm_mlp_pre',
                'norm_mlp_post', 'w_ff1', 'w_ff2')


def kernel(x, norm_mix_pre, norm_mix_post, w_in, ret_gn_gain, ssm_lambda_re, ssm_lambda_im, ssm_log_dt, ssm_b_re, ssm_b_im, ssm_c_re, ssm_c_im, ssm_d, w_glu, w_out, norm_mlp_pre, norm_mlp_post, w_ff1, w_ff2, loss_target, m_norm_mix_pre, m_norm_mix_post, m_w_in, m_ret_gn_gain, m_ssm_lambda_re, m_ssm_lambda_im, m_ssm_log_dt, m_ssm_b_re, m_ssm_b_im, m_ssm_c_re, m_ssm_c_im, m_ssm_d, m_w_glu, m_w_out, m_norm_mlp_pre, m_norm_mlp_post, m_w_ff1, m_w_ff2, v_norm_mix_pre, v_norm_mix_post, v_w_in, v_ret_gn_gain, v_ssm_lambda_re, v_ssm_lambda_im, v_ssm_log_dt, v_ssm_b_re, v_ssm_b_im, v_ssm_c_re, v_ssm_c_im, v_ssm_d, v_w_glu, v_w_out, v_norm_mlp_pre, v_norm_mlp_post, v_w_ff1, v_w_ff2):
    args = dict(locals())
    w = {n: args[n] for n in WEIGHT_NAMES}
    m = {n: args["m_" + n] for n in WEIGHT_NAMES}
    v = {n: args["v_" + n] for n in WEIGHT_NAMES}
    L = x.shape[1]
    tm = min(256, L)
    tk = min(2048, L)
    tb = min(512, L)

    gathers, zero = {}, jnp.zeros((), _F32)
    for group, names in WEIGHT_GROUPS.items():
        blocks = [w[n][0].astype(_BF) for n in names]
        blocks[0] = blocks[0] + zero.astype(_BF)
        gathers[group] = _split_start(blocks, [_landing(b) for b in blocks], True, "weights_start_" + group)
        zero = gathers[group][4][0, 0]

    def weights(group, after):
        landed = _split_wait(*gathers[group][:4], after, True, "weights_wait_" + group)
        return [_weight_from_slots(n, g) for n, g in zip(WEIGHT_GROUPS[group], landed)]

    in_flight = []

    def emit(dws):
        names = sorted(dws)
        srcs = [_grad_slots(n, dws[n]) for n in names]
        lands = [_landing(lax.dynamic_index_in_dim(t, _my_index(), 0, keepdims=False)) for t in srcs]
        started = _split_start(srcs, lands, False, "grads_start_" + "_".join(names))
        in_flight.append((names, started))
        return started[4][0, 0]

    small_w = {n: w[n] for n in SMALL_NAMES}
    sq, gx, gsmall = _local_grads(x[0], loss_target[0], small_w, weights, emit, tm, tk, tb, zero=zero)

    shapes = {n: w[n].shape for n in SMALL_NAMES}
    loss_rows = jnp.broadcast_to(0.5 / D_MODEL * jnp.sum(sq), (PIECE_ROWS, LANES)).astype(_F32)
    small_buf = _pack_small(gsmall, shapes, loss_rows)
    small_started = _split_start([small_buf], [_landing(small_buf)], True, "small_grads_start")
    grads, delta, new_m, new_v = {}, {}, {}, {}
    after = small_started[4]
    for names, started in in_flight:
        landed = _split_wait(*started[:4], after, False, "grads_wait_" + "_".join(names))
        for n, parts in zip(names, landed):
            res = _sum_adamw(parts, w[n][0], m[n][0], v[n][0], min(256, BIG_SHAPES[n][0]), "adamw_" + n)
            grads[n], delta[n], new_m[n], new_v[n] = (t[None] for t in res)
        after = res[1]
    small_parts = _split_wait(*small_started[:4], after, True, "small_grads_wait")[0]
    sw, sm, sv = _pack_small(w, shapes), _pack_small(m, shapes), _pack_small(v, shapes)
    res = _sum_adamw(small_parts, sw, sm, sv, sw.shape[0], "adamw_small")
    for dst, buf in zip((grads, delta, new_m, new_v), res):
        dst.update(_unpack_small(buf, shapes))
    _, loss_at, _ = _small_layout(shapes)
    loss = res[0][loss_at, 0]

    return (loss, gx[None], *[grads[n] for n in WEIGHT_NAMES], *[delta[n] for n in WEIGHT_NAMES],
            *[new_m[n] for n in WEIGHT_NAMES], *[new_v[n] for n in WEIGHT_NAMES])
```

```python
import math

import jax
import jax.numpy as jnp
from jax import lax
from jax.experimental import pallas as pl
from jax.experimental.pallas import tpu as pltpu

_BF = jnp.bfloat16
_F32 = jnp.float32

D_MODEL = 1024
RET_W = 512
N_HEAD = 4
HEAD_D = 128
CHUNK = 128
SSM_W = 512
SSM_GC = 16
N_GROUP = 32
N_STATE = 64
GROUPS_PER_KB = 8
N_KB = 4
KB_STATES = GROUPS_PER_KB * N_STATE
D_FF = 4096
IN_COLS = 2560
NORM_EPS = 1e-6
ROPE_BASE = 10000.0
N_DEV = 8

ADAM_LR = 0.001
ADAM_B1 = 0.9
ADAM_B2 = 0.999
ADAM_EPS = 1e-08
ADAM_WD = 0.01
ADAM_STEP = 10

SUBLANES = 8
LANES = 128
VMEM_LIMIT = 52 * 1024 * 1024
KB_PER_STEP = 2
SCAN_UNROLL = 2

MESH = pl.DeviceIdType.MESH


def _params(*sem):
    return pltpu.CompilerParams(dimension_semantics=sem, vmem_limit_bytes=VMEM_LIMIT)


def _dot(a, b):
    return jnp.dot(a, b, preferred_element_type=_F32)


def _dot_nt(a, b):
    return lax.dot_general(a, b, (((1,), (1,)), ((), ())), preferred_element_type=_F32)


def _dot_tn(a, b):
    return lax.dot_general(a, b, (((0,), (0,)), ((), ())), preferred_element_type=_F32)


def _rms_r(z):
    return lax.rsqrt(jnp.mean(z * z, axis=-1, keepdims=True) + NORM_EPS)


def _rms_bwd(z, g, dn):
    r = _rms_r(z)
    t = dn * g
    dz = r * t - z * (r * r * r * jnp.mean(t * z, axis=-1, keepdims=True))
    return dz, dn * z * r


def _rope(t, cs, sn):
    return t * cs + pltpu.roll(t, HEAD_D // 2, 1) * sn


def _rope_t(t, cs, sn):
    return t * cs - pltpu.roll(t, HEAD_D // 2, 1) * sn


def _sigmoid(z):
    return 1.0 / (1.0 + jnp.exp(-z))


_GELU_C = math.sqrt(2.0 / math.pi)


def _gelu(z):
    return 0.5 * z * (1.0 + jnp.tanh(_GELU_C * (z + 0.044715 * z * z * z)))


def _gelu_grad(z):
    th = jnp.tanh(_GELU_C * (z + 0.044715 * z * z * z))
    return 0.5 * (1.0 + th) + 0.5 * z * (1.0 - th * th) * _GELU_C * (1.0 + 3 * 0.044715 * z * z)


ROW_CHUNK = 256


def _row_chunks(tm):
    return [pl.ds(i, min(ROW_CHUNK, tm)) for i in range(0, tm, ROW_CHUNK)]


def _row_spec(tm, n):
    return pl.BlockSpec((tm, n), lambda i: (i, 0))


def _full_spec(shape):
    nd = len(shape)
    return pl.BlockSpec(shape, lambda *_: (0,) * nd)


def _weight_spec(shape):
    nd = len(shape)
    return pl.BlockSpec(shape, lambda *_: (0,) * nd, pipeline_mode=pl.Buffered(1))


def _inproj_fwd(x, g1, w_in, cosf, sinf, tm):
    L = x.shape[0]

    def body(x_ref, g_ref, w_ref, cos_ref, sin_ref, h_ref, q_ref, k_ref, v_ref, gate_ref, u_ref):
        xv = x_ref[...]
        h = (xv * _rms_r(xv) * g_ref[...]).astype(_BF)
        h_ref[...] = h
        proj = _dot(h, w_ref[...])
        cs, sn = cos_ref[...], sin_ref[...]
        for hh in range(N_HEAD):
            lo = hh * HEAD_D
            q_ref[:, lo:lo + HEAD_D] = _rope(proj[:, lo:lo + HEAD_D], cs, sn).astype(_BF)
            kh = _rope(proj[:, RET_W + lo:RET_W + lo + HEAD_D], cs, sn) * (HEAD_D ** -0.5)
            k_ref[:, lo:lo + HEAD_D] = kh.astype(_BF)
        v_ref[...] = proj[:, 2 * RET_W:3 * RET_W].astype(_BF)
        gate_ref[...] = proj[:, 3 * RET_W:4 * RET_W]
        u_ref[...] = proj[:, 4 * RET_W:]

    return pl.pallas_call(
        body, name="inproj_fwd", grid=(L // tm,),
        in_specs=[_row_spec(tm, D_MODEL), _full_spec((1, D_MODEL)), _weight_spec((D_MODEL, IN_COLS)),
                  _row_spec(tm, HEAD_D), _row_spec(tm, HEAD_D)],
        out_specs=[_row_spec(tm, D_MODEL)] + [_row_spec(tm, RET_W)] * 5,
        out_shape=[jax.ShapeDtypeStruct((L, D_MODEL), _BF)] + [jax.ShapeDtypeStruct((L, RET_W), _BF)] * 3
        + [jax.ShapeDtypeStruct((L, RET_W), _F32)] * 2,
        compiler_params=_params("parallel"),
    )(x, g1, w_in, cosf, sinf)


def _ret_consts():
    lg = jnp.log(1.0 - jnp.exp(jnp.linspace(math.log(1.0 / 32), math.log(1.0 / 512), N_HEAD))).astype(_F32)
    idx = jnp.arange(CHUNK, dtype=_F32)
    diff = idx[:, None] - idx[None, :]
    decay = jnp.where(diff[None] >= 0, jnp.exp(jnp.maximum(diff, 0.0)[None] * lg[:, None, None]), 0.0)
    zeta = jnp.exp((CHUNK - 1 - idx)[None, :] * lg[:, None])
    xi = jnp.exp((idx + 1.0)[None, :] * lg[:, None])
    gc = jnp.exp(CHUNK * lg)
    wide = lambda t: jnp.broadcast_to(t[:, :, None], (N_HEAD, CHUNK, HEAD_D)).astype(_F32)
    gcw = jnp.broadcast_to(gc[:, None, None], (N_HEAD, SUBLANES, HEAD_D)).astype(_F32)
    return decay.astype(_F32), wide(xi), wide(zeta), gcw


def _head_specs():
    c3 = _full_spec((N_HEAD, CHUNK, CHUNK))
    return [c3, c3, c3, _full_spec((N_HEAD, SUBLANES, HEAD_D))]


def _retention_fwd(q, k, v, gate, ggn, consts):
    L = q.shape[0]
    nc = L // CHUNK
    blk = pl.BlockSpec((CHUNK, RET_W), lambda n: (n, 0))

    def body(q_ref, k_ref, v_ref, gate_ref, ggn_ref, dm_ref, xi_ref, zeta_ref, gc_ref,
             o_ref, y_ref, rp_ref, r_scr):
        @pl.when(pl.program_id(0) == 0)
        def _():
            r_scr[...] = jnp.zeros_like(r_scr)

        for hh in range(N_HEAD):
            cols = slice(hh * HEAD_D, (hh + 1) * HEAD_D)
            qv, kv, vv = q_ref[:, cols], k_ref[:, cols], v_ref[:, cols]
            r_prev = r_scr[hh]
            s = _dot_nt(qv, kv) * dm_ref[hh]
            o = _dot(s.astype(_BF), vv) + _dot(qv, r_prev.astype(_BF)) * xi_ref[hh]
            o_ref[:, cols] = o
            rp_ref[hh, 0] = r_prev
            vz = (vv.astype(_F32) * zeta_ref[hh]).astype(_BF)
            r_scr[hh] = gc_ref[hh, 0:1, :] * r_prev + _dot_tn(kv, vz)
            dlt = o - jnp.mean(o, axis=-1, keepdims=True)
            on = dlt * lax.rsqrt(jnp.mean(dlt * dlt, axis=-1, keepdims=True) + NORM_EPS)
            gt = gate_ref[:, cols]
            y_ref[:, cols] = (gt * _sigmoid(gt) * (on * ggn_ref[:, cols])).astype(_BF)

    return pl.pallas_call(
        body, name="retention_fwd", grid=(nc,),
        in_specs=[blk, blk, blk, blk, _full_spec((1, RET_W))] + _head_specs(),
        out_specs=[blk, blk, pl.BlockSpec((N_HEAD, 1, HEAD_D, HEAD_D), lambda n: (0, n, 0, 0))],
        out_shape=[jax.ShapeDtypeStruct((L, RET_W), _F32), jax.ShapeDtypeStruct((L, RET_W), _BF),
                   jax.ShapeDtypeStruct((N_HEAD, nc, HEAD_D, HEAD_D), _F32)],
        scratch_shapes=[pltpu.VMEM((N_HEAD, HEAD_D, HEAD_D), _F32)],
        compiler_params=_params("arbitrary"),
    )(q, k, v, gate, ggn, *consts)


def _rows_to_segments(dst_scr, src_ref, seg):
    for g in range(dst_scr.shape[0]):
        for j in range(SUBLANES):
            dst_scr[g, pl.ds(j, seg, stride=SUBLANES), :] = src_ref[pl.ds(j * seg, seg), g * LANES:(g + 1) * LANES]


def _segments_to_rows(dst_ref, src_scr, seg):
    for g in range(src_scr.shape[0]):
        for j in range(SUBLANES):
            dst_ref[pl.ds(j * seg, seg), g * LANES:(g + 1) * LANES] = src_scr[g, pl.ds(j, seg, stride=SUBLANES), :]


def _scan_segments(x_ref, tab_ref, pw_ref, carry_ref, seg, reverse, entry_ref=None, fwd_ref=None, fwd_entry_ref=None,
                   da_ref=None):
    G = x_ref.shape[0]
    W = KB_STATES
    re, im = pl.ds(0, W), pl.ds(W, W)
    row_id = lax.broadcasted_iota(jnp.int32, (SUBLANES, W), 0)
    edge_in = (row_id == SUBLANES - 1) if reverse else (row_id == 0)
    edge_out = 0 if reverse else SUBLANES - 1
    a_tab = [(tab_ref[g, 0], tab_ref[g, 1]) for g in range(G)]

    def local(i, st):
        r = (seg - 1 - i) if reverse else i
        out = []
        for g in range(G):
            (ar, ai), (sr, si) = a_tab[g], st[g]
            nr = ar * sr - ai * si + x_ref[g, r, :, re]
            ni = ar * si + ai * sr + x_ref[g, r, :, im]
            x_ref[g, r, :, re] = nr
            x_ref[g, r, :, im] = ni
            out.append((nr, ni))
        return tuple(out)

    zero = jnp.zeros((SUBLANES, W), _F32)
    ends = lax.fori_loop(0, seg, local, tuple((zero, zero) for _ in range(G)), unroll=SCAN_UNROLL)

    entry = []
    shift = (SUBLANES - 1) if reverse else 1
    for g in range(G):
        er, ei = ends[g]
        fr = jnp.where(edge_in, carry_ref[g, :, re], pltpu.roll(er, shift, 0))
        fi = jnp.where(edge_in, carry_ref[g, :, im], pltpu.roll(ei, shift, 0))
        for j, dist in enumerate((1, 2, 4)):
            pr, pi = tab_ref[g, 2 + 2 * j], tab_ref[g, 3 + 2 * j]
            sh = (SUBLANES - dist) if reverse else dist
            sr, si = pltpu.roll(fr, sh, 0), pltpu.roll(fi, sh, 0)
            fr, fi = fr + pr * sr - pi * si, fi + pr * si + pi * sr
        br, bi = tab_ref[g, 8], tab_ref[g, 9]
        outr = br * fr - bi * fi + er
        outi = br * fi + bi * fr + ei
        carry_ref[g, :, re] = jnp.broadcast_to(outr[edge_out:edge_out + 1, :], (SUBLANES, W))
        carry_ref[g, :, im] = jnp.broadcast_to(outi[edge_out:edge_out + 1, :], (SUBLANES, W))
        entry.append((fr, fi))
        if entry_ref is not None:
            entry_ref[g, :, re] = fr
            entry_ref[g, :, im] = fi

    add_da = da_ref is not None

    def fix(r, st, first=False):
        out = []
        for g in range(G):
            fr, fi = entry[g]
            pwr, pwi = pw_ref[g, r, :, re], pw_ref[g, r, :, im]
            xr = x_ref[g, r, :, re] + (pwr * fr - pwi * fi)
            xi = x_ref[g, r, :, im] + (pwr * fi + pwi * fr)
            x_ref[g, r, :, re] = xr
            x_ref[g, r, :, im] = xi
            if add_da:
                prev = fwd_entry_ref.at[g] if first else fwd_ref.at[g, r - 1]
                xpr, xpi = prev[:, re], prev[:, im]
                out.append((st[g][0] + (xr * xpr + xi * xpi), st[g][1] + (xi * xpr - xr * xpi)))
            else:
                out.append(st[g])
        return tuple(out)

    if add_da:
        st = fix(0, tuple((zero, zero) for _ in range(G)), first=True)
        st = lax.fori_loop(1, seg, fix, st, unroll=SCAN_UNROLL)
        for g in range(G):
            da_ref[g, :, re] += st[g][0]
            da_ref[g, :, im] += st[g][1]
    else:
        lax.fori_loop(0, seg, fix, tuple((zero[0:1, 0:LANES],) for _ in range(G)), unroll=SCAN_UNROLL)


def _s5_specs(seg, time=lambda t: t):
    G = KB_PER_STEP
    return dict(
        x=pl.BlockSpec((G, seg, SUBLANES, 2 * KB_STATES), lambda kb, t: (kb, time(t), 0, 0)),
        ent=pl.BlockSpec((G, 1, SUBLANES, 2 * KB_STATES), lambda kb, t: (kb, time(t), 0, 0)),
        b=pl.BlockSpec((G, LANES, 2 * KB_STATES), lambda kb, t: (kb, 0, 0)),
        c=pl.BlockSpec((G, 2 * KB_STATES, LANES), lambda kb, t: (kb, 0, 0)),
        tab=pl.BlockSpec((G, 10, SUBLANES, KB_STATES), lambda kb, t: (kb, 0, 0, 0)),
        pw=pl.BlockSpec((G, seg, 1, 2 * KB_STATES), lambda kb, t: (kb, 0, 0, 0)),
        d=pl.BlockSpec((1, G * LANES), lambda kb, t: (0, kb)),
    )


def _s5_fwd(u, bmat, cmat, tab_f, pw_f, d_skip, tb):
    L = u.shape[0]
    nt = L // tb
    seg = tb // SUBLANES
    G = KB_PER_STEP
    ucol = pl.BlockSpec((tb, G * LANES), lambda kb, t: (t, kb))
    sp = _s5_specs(seg)

    def body(u_ref, b_ref, c_ref, tab_ref, pw_ref, d_ref, s_ref, x_ref, ent_ref, up_scr, y_scr, carry_scr):
        @pl.when(pl.program_id(1) == 0)
        def _():
            carry_scr[...] = jnp.zeros_like(carry_scr)

        _rows_to_segments(up_scr, u_ref, seg)
        for g in range(G):
            x_ref[g] = _dot(up_scr[g].astype(_BF), b_ref[g]).reshape(seg, SUBLANES, 2 * KB_STATES)
        _scan_segments(x_ref, tab_ref, pw_ref, carry_scr, seg, reverse=False, entry_ref=ent_ref.at[:, 0])
        for g in range(G):
            y = _dot(x_ref[g].reshape(tb, 2 * KB_STATES).astype(_BF), c_ref[g])
            y_scr[g] = y + d_ref[:, g * LANES:(g + 1) * LANES] * up_scr[g]
        _segments_to_rows(s_ref, y_scr, seg)

    return pl.pallas_call(
        body, name="s5_fwd", grid=(N_KB // G, nt),
        in_specs=[ucol, sp["b"], sp["c"], sp["tab"], sp["pw"], sp["d"]],
        out_specs=[ucol, sp["x"], sp["ent"]],
        out_shape=[jax.ShapeDtypeStruct((L, SSM_W), _F32),
                   jax.ShapeDtypeStruct((N_KB, L // SUBLANES, SUBLANES, 2 * KB_STATES), _F32),
                   jax.ShapeDtypeStruct((N_KB, nt, SUBLANES, 2 * KB_STATES), _F32)],
        scratch_shapes=[pltpu.VMEM((G, tb, LANES), _F32)] * 2 + [pltpu.VMEM((G, SUBLANES, 2 * KB_STATES), _F32)],
        compiler_params=_params("parallel", "arbitrary"),
    )(u, bmat, cmat, tab_f, pw_f, d_skip)


def _mixout_fwd(s, y_ret, x, w_glu, w_out, g2, tm):
    L = s.shape[0]

    def body(s_ref, yr_ref, x_ref, wg_ref, wo_ref, g_ref, ys_ref, glu_ref, cat_ref, mix_ref, x2_ref):
        ys = _gelu(s_ref[...]).astype(_BF)
        ys_ref[...] = ys
        glu = _dot(ys, wg_ref[...])
        glu_ref[...] = glu
        cat_ref[:, :RET_W] = yr_ref[...]
        cat_ref[:, RET_W:] = (glu[:, :SSM_W] * _sigmoid(glu[:, SSM_W:])).astype(_BF)
        mix = _dot(cat_ref[...], wo_ref[...])
        mix_ref[...] = mix
        x2_ref[...] = x_ref[...] + mix * _rms_r(mix) * g_ref[...]

    return pl.pallas_call(
        body, name="mixout_fwd", grid=(L // tm,),
        in_specs=[_row_spec(tm, SSM_W), _row_spec(tm, RET_W), _row_spec(tm, D_MODEL),
                  _weight_spec((SSM_W, 2 * SSM_W)), _weight_spec((D_MODEL, D_MODEL)), _full_spec((1, D_MODEL))],
        out_specs=[_row_spec(tm, SSM_W), _row_spec(tm, 2 * SSM_W), _row_spec(tm, D_MODEL),
                   _row_spec(tm, D_MODEL), _row_spec(tm, D_MODEL)],
        out_shape=[jax.ShapeDtypeStruct((L, SSM_W), _BF), jax.ShapeDtypeStruct((L, 2 * SSM_W), _F32),
                   jax.ShapeDtypeStruct((L, D_MODEL), _BF), jax.ShapeDtypeStruct((L, D_MODEL), _F32),
                   jax.ShapeDtypeStruct((L, D_MODEL), _F32)],
        compiler_params=_params("parallel"),
    )(s, y_ret, x, w_glu, w_out, g2)


FF1_COLS = D_FF // N_DEV


def _ff1_fwd(x2, g3, w1, tm):
    L = x2.shape[0]

    def body(x_ref, g_ref, w_ref, h_ref, f_ref):
        xv = x_ref[...]
        h = (xv * _rms_r(xv) * g_ref[...]).astype(_BF)
        h_ref[...] = h
        for j in range(N_DEV):
            f_ref[:, j * FF1_COLS:(j + 1) * FF1_COLS] = _dot(h, w_ref[j])

    return pl.pallas_call(
        body, name="ff1_fwd", grid=(L // tm,),
        in_specs=[_row_spec(tm, D_MODEL), _full_spec((1, D_MODEL)), _weight_spec((N_DEV, D_MODEL, FF1_COLS))],
        out_specs=[_row_spec(tm, D_MODEL), _row_spec(tm, D_FF)],
        out_shape=[jax.ShapeDtypeStruct((L, D_MODEL), _BF), jax.ShapeDtypeStruct((L, D_FF), _F32)],
        compiler_params=_params("parallel"),
    )(x2, g3, w1)


def _ff2_loss(f1, x2, tgt, g4, w2, tm):
    L = f1.shape[0]

    def body(f_ref, x_ref, t_ref, g_ref, w_ref, dy_ref, dm_ref, dg_ref, ls_ref):
        @pl.when(pl.program_id(0) == 0)
        def _():
            dg_ref[...] = jnp.zeros_like(dg_ref)
            ls_ref[...] = jnp.zeros_like(ls_ref)

        g = g_ref[...]
        for rows in _row_chunks(tm):
            rl = jnp.maximum(f_ref[rows, :], 0.0)
            m = _dot((rl * rl).astype(_BF), w_ref[...])
            y = x_ref[rows, :] + m * _rms_r(m) * g
            err = y - t_ref[rows, :]
            ls_ref[...] += jnp.sum(err * err, axis=0, keepdims=True)
            dy = err * (1.0 / D_MODEL)
            dy_ref[rows, :] = dy
            dm, dgr = _rms_bwd(m, g, dy)
            dm_ref[rows, :] = dm.astype(_BF)
            dg_ref[...] += jnp.sum(dgr, axis=0, keepdims=True)

    return pl.pallas_call(
        body, name="ff2_loss", grid=(L // tm,),
        in_specs=[_row_spec(tm, D_FF), _row_spec(tm, D_MODEL), _row_spec(tm, D_MODEL),
                  _full_spec((1, D_MODEL)), _weight_spec((D_FF, D_MODEL))],
        out_specs=[_row_spec(tm, D_MODEL), _row_spec(tm, D_MODEL), _full_spec((1, D_MODEL)), _full_spec((1, D_MODEL))],
        out_shape=[jax.ShapeDtypeStruct((L, D_MODEL), _F32), jax.ShapeDtypeStruct((L, D_MODEL), _BF),
                   jax.ShapeDtypeStruct((1, D_MODEL), _F32), jax.ShapeDtypeStruct((1, D_MODEL), _F32)],
        compiler_params=_params("arbitrary"),
    )(f1, x2, tgt, g4, w2)


def _ff2_bwd(dm, f1, w2, tm, tn):
    L = dm.shape[0]
    last = L // tm - 1

    def body(dm_ref, f_ref, w_ref, df_ref, dw_ref, acc):
        @pl.when(pl.program_id(1) == 0)
        def _():
            acc[...] = jnp.zeros_like(acc)

        dmv = dm_ref[...]
        rl = jnp.maximum(f_ref[...], 0.0)
        df_ref[...] = (_dot_nt(dmv, w_ref[...]) * (2.0 * rl)).astype(_BF)
        acc[...] += _dot_tn((rl * rl).astype(_BF), dmv)

        @pl.when(pl.program_id(1) == last)
        def _():
            dw_ref[...] = acc[...].astype(_BF)

    return pl.pallas_call(
        body, name="ff2_bwd", grid=(D_FF // tn, L // tm),
        in_specs=[pl.BlockSpec((tm, D_MODEL), lambda j, i: (i, 0)), pl.BlockSpec((tm, tn), lambda j, i: (i, j)),
                  pl.BlockSpec((tn, D_MODEL), lambda j, i: (j, 0))],
        out_specs=[pl.BlockSpec((tm, tn), lambda j, i: (i, j)), pl.BlockSpec((tn, D_MODEL), lambda j, i: (j, 0))],
        out_shape=[jax.ShapeDtypeStruct((L, D_FF), _BF), jax.ShapeDtypeStruct((D_FF, D_MODEL), _BF)],
        scratch_shapes=[pltpu.VMEM((tn, D_MODEL), _F32)],
        compiler_params=_params("parallel", "arbitrary"),
    )(dm, f1, w2)


def _ff1_bwd(df1, w1, x2, mix, dy, g3, g2, tm):
    L = df1.shape[0]

    def body(df_ref, w_ref, x2_ref, mix_ref, dy_ref, g3_ref, g2_ref, dx2_ref, dmix_ref, dg3_ref, dg2_ref):
        @pl.when(pl.program_id(0) == 0)
        def _():
            dg3_ref[...] = jnp.zeros_like(dg3_ref)
            dg2_ref[...] = jnp.zeros_like(dg2_ref)

        for rows in _row_chunks(tm):
            dh = _dot_nt(df_ref[rows, 0:FF1_COLS], w_ref[0])
            for j in range(1, N_DEV):
                dh = dh + _dot_nt(df_ref[rows, j * FF1_COLS:(j + 1) * FF1_COLS], w_ref[j])
            dz, dgr = _rms_bwd(x2_ref[rows, :], g3_ref[...], dh)
            dg3_ref[...] += jnp.sum(dgr, axis=0, keepdims=True)
            dx2 = dy_ref[rows, :] + dz
            dx2_ref[rows, :] = dx2
            dmx, dgr2 = _rms_bwd(mix_ref[rows, :], g2_ref[...], dx2)
            dg2_ref[...] += jnp.sum(dgr2, axis=0, keepdims=True)
            dmix_ref[rows, :] = dmx.astype(_BF)

    vec = _full_spec((1, D_MODEL))
    return pl.pallas_call(
        body, name="ff1_bwd", grid=(L // tm,),
        in_specs=[_row_spec(tm, D_FF), _weight_spec((N_DEV, D_MODEL, FF1_COLS)), _row_spec(tm, D_MODEL),
                  _row_spec(tm, D_MODEL), _row_spec(tm, D_MODEL), vec, vec],
        out_specs=[_row_spec(tm, D_MODEL), _row_spec(tm, D_MODEL), vec, vec],
        out_shape=[jax.ShapeDtypeStruct((L, D_MODEL), _F32), jax.ShapeDtypeStruct((L, D_MODEL), _BF),
                   jax.ShapeDtypeStruct((1, D_MODEL), _F32), jax.ShapeDtypeStruct((1, D_MODEL), _F32)],
        compiler_params=_params("arbitrary"),
    )(df1, w1, x2, mix, dy, g3, g2)


def _matmul_tn(a, b, tm, tn, name, slots=False):
    L, K = a.shape
    N = b.shape[1]
    last = L // tm - 1

    def body(a_ref, b_ref, o_ref, acc):
        @pl.when(pl.program_id(1) == 0)
        def _():
            acc[...] = jnp.zeros_like(acc)

        acc[...] += _dot_tn(a_ref[...].astype(_BF), b_ref[...].astype(_BF))

        @pl.when(pl.program_id(1) == last)
        def _():
            if slots:
                o_ref[0] = acc[...].astype(_BF)
            else:
                o_ref[...] = acc[...].astype(_BF)

    if slots:
        out_spec = pl.BlockSpec((1, K, tn), lambda j, i: (j, 0, 0))
        out_shape = jax.ShapeDtypeStruct((N // tn, K, tn), _BF)
    else:
        out_spec = pl.BlockSpec((K, tn), lambda j, i: (0, j))
        out_shape = jax.ShapeDtypeStruct((K, N), _BF)
    return pl.pallas_call(
        body, name=name, grid=(N // tn, L // tm),
        in_specs=[pl.BlockSpec((tm, K), lambda j, i: (i, 0)), pl.BlockSpec((tm, tn), lambda j, i: (i, j))],
        out_specs=out_spec, out_shape=out_shape,
        scratch_shapes=[pltpu.VMEM((K, tn), _F32)],
        compiler_params=_params("parallel", "arbitrary"),
    )(a, b)


def _mixout_bwd(dmix, w_out, w_glu, glu, s, o, gate, ggn, tm):
    L = dmix.shape[0]

    def body(dmix_ref, wo_ref, wg_ref, glu_ref, s_ref, o_ref, gate_ref, ggn_ref,
             dglu_ref, ds_ref, dgate_ref, do_ref, dggn_ref):
        @pl.when(pl.program_id(0) == 0)
        def _():
            dggn_ref[...] = jnp.zeros_like(dggn_ref)

        dcat = _dot_nt(dmix_ref[...], wo_ref[...])
        dy_ret, dy_ssm = dcat[:, :RET_W], dcat[:, RET_W:]
        glu = glu_ref[...]
        ga, sg = glu[:, :SSM_W], _sigmoid(glu[:, SSM_W:])
        dga = (dy_ssm * sg).astype(_BF)
        dgb = (dy_ssm * ga * sg * (1.0 - sg)).astype(_BF)
        dglu_ref[:, :SSM_W] = dga
        dglu_ref[:, SSM_W:] = dgb
        dys = _dot_nt(dga, wg_ref[:, :SSM_W]) + _dot_nt(dgb, wg_ref[:, SSM_W:])
        ds_ref[...] = dys * _gelu_grad(s_ref[...])
        gt = gate_ref[...]
        sgt = _sigmoid(gt)
        ggn = ggn_ref[...]
        for hh in range(N_HEAD):
            cols = slice(hh * HEAD_D, (hh + 1) * HEAD_D)
            ov = o_ref[:, cols]
            dlt = ov - jnp.mean(ov, axis=-1, keepdims=True)
            rstd = lax.rsqrt(jnp.mean(dlt * dlt, axis=-1, keepdims=True) + NORM_EPS)
            on = dlt * rstd
            dyr = dy_ret[:, cols] * (gt[:, cols] * sgt[:, cols])
            dgate_ref[:, cols] = dy_ret[:, cols] * (on * ggn[:, cols]) * (sgt[:, cols] * (1.0 + gt[:, cols] * (1.0 - sgt[:, cols])))
            dggn_ref[:, cols] += jnp.sum(dyr * on, axis=0, keepdims=True)
            don = dyr * ggn[:, cols]
            do = rstd * (don - jnp.mean(don, axis=-1, keepdims=True) - on * jnp.mean(don * on, axis=-1, keepdims=True))
            do_ref[:, cols] = do.astype(_BF)

    return pl.pallas_call(
        body, name="mixout_bwd", grid=(L // tm,),
        in_specs=[_row_spec(tm, D_MODEL), _weight_spec((D_MODEL, D_MODEL)), _weight_spec((SSM_W, 2 * SSM_W)),
                  _row_spec(tm, 2 * SSM_W), _row_spec(tm, SSM_W), _row_spec(tm, RET_W), _row_spec(tm, RET_W),
                  _full_spec((1, RET_W))],
        out_specs=[_row_spec(tm, 2 * SSM_W), _row_spec(tm, SSM_W), _row_spec(tm, RET_W), _row_spec(tm, RET_W),
                   _full_spec((1, RET_W))],
        out_shape=[jax.ShapeDtypeStruct((L, 2 * SSM_W), _BF), jax.ShapeDtypeStruct((L, SSM_W), _F32),
                   jax.ShapeDtypeStruct((L, RET_W), _F32), jax.ShapeDtypeStruct((L, RET_W), _BF),
                   jax.ShapeDtypeStruct((1, RET_W), _F32)],
        compiler_params=_params("arbitrary"),
    )(dmix, w_out, w_glu, glu, s, o, gate, ggn)


def _s5_bwd(u, ds, xs, ent, bmat, cmat, tab_r, pw_r, d_skip, tb):
    L = u.shape[0]
    nt = L // tb
    seg = tb // SUBLANES
    G = KB_PER_STEP
    rcol = pl.BlockSpec((tb, G * LANES), lambda kb, t: (nt - 1 - t, kb))
    sp = _s5_specs(seg, time=lambda t: nt - 1 - t)
    aspec = pl.BlockSpec((G, SUBLANES, 2 * KB_STATES), lambda kb, t: (kb, 0, 0))

    def body(u_ref, ds_ref, x_ref, ent_ref, b_ref, c_ref, tr_ref, pr_ref, d_ref,
             du_ref, db_ref, dc_ref, da_ref, dd_ref, up_scr, dp_scr, g_scr, lc_scr):
        @pl.when(pl.program_id(1) == 0)
        def _():
            lc_scr[...] = jnp.zeros_like(lc_scr)
            db_ref[...] = jnp.zeros_like(db_ref)
            dc_ref[...] = jnp.zeros_like(dc_ref)
            da_ref[...] = jnp.zeros_like(da_ref)
            dd_ref[...] = jnp.zeros_like(dd_ref)

        _rows_to_segments(up_scr, u_ref, seg)
        _rows_to_segments(dp_scr, ds_ref, seg)
        for g in range(G):
            g_scr[g] = _dot_nt(dp_scr[g].astype(_BF), c_ref[g]).reshape(seg, SUBLANES, 2 * KB_STATES)
        _scan_segments(g_scr, tr_ref, pr_ref, lc_scr, seg, reverse=True, fwd_ref=x_ref, fwd_entry_ref=ent_ref.at[:, 0],
                       da_ref=da_ref)
        for g in range(G):
            cols = slice(g * LANES, (g + 1) * LANES)
            uv, dsv = up_scr[g], dp_scr[g]
            ub, dsb = uv.astype(_BF), dsv.astype(_BF)
            lamb = g_scr[g].reshape(tb, 2 * KB_STATES).astype(_BF)
            db_ref[g] += _dot_tn(ub, lamb)
            dc_ref[g] += _dot_tn(x_ref[g].reshape(tb, 2 * KB_STATES).astype(_BF), dsb)
            dd_ref[:, cols] += jnp.sum(dsv * uv, axis=0, keepdims=True)
            up_scr[g] = _dot_nt(lamb, b_ref[g]) + d_ref[:, cols] * dsv
        _segments_to_rows(du_ref, up_scr, seg)

    return pl.pallas_call(
        body, name="s5_bwd", grid=(N_KB // G, nt),
        in_specs=[rcol, rcol, sp["x"], sp["ent"], sp["b"], sp["c"], sp["tab"], sp["pw"], sp["d"]],
        out_specs=[rcol, sp["b"], sp["c"], aspec, sp["d"]],
        out_shape=[jax.ShapeDtypeStruct((L, SSM_W), _F32),
                   jax.ShapeDtypeStruct((N_KB, LANES, 2 * KB_STATES), _F32),
                   jax.ShapeDtypeStruct((N_KB, 2 * KB_STATES, LANES), _F32),
                   jax.ShapeDtypeStruct((N_KB, SUBLANES, 2 * KB_STATES), _F32),
                   jax.ShapeDtypeStruct((1, SSM_W), _F32)],
        scratch_shapes=[pltpu.VMEM((G, tb, LANES), _F32)] * 2
        + [pltpu.VMEM((G, seg, SUBLANES, 2 * KB_STATES), _F32), pltpu.VMEM((G, SUBLANES, 2 * KB_STATES), _F32)],
        compiler_params=_params("parallel", "arbitrary"),
    )(u, ds, xs, ent, bmat, cmat, tab_r, pw_r, d_skip)


def _retention_bwd(q, k, v, do, r_prev, consts, cosf, sinf):
    L = q.shape[0]
    nc = L // CHUNK
    blk = pl.BlockSpec((CHUNK, RET_W), lambda n: (nc - 1 - n, 0))
    rope_blk = pl.BlockSpec((CHUNK, HEAD_D), lambda n: (nc - 1 - n, 0))

    def body(q_ref, k_ref, v_ref, do_ref, rp_ref, dm_ref, xi_ref, zeta_ref, gc_ref, cos_ref, sin_ref,
             dq_ref, dk_ref, dv_ref, g_scr):
        @pl.when(pl.program_id(0) == 0)
        def _():
            g_scr[...] = jnp.zeros_like(g_scr)

        cs, sn = cos_ref[...], sin_ref[...]
        for hh in range(N_HEAD):
            cols = slice(hh * HEAD_D, (hh + 1) * HEAD_D)
            qv, kv, vv, dov = q_ref[:, cols], k_ref[:, cols], v_ref[:, cols], do_ref[:, cols]
            rb = rp_ref[hh, 0].astype(_BF)
            gst = g_scr[hh]
            gb = gst.astype(_BF)
            dm, zeta = dm_ref[hh], zeta_ref[hh]
            sb = (_dot_nt(qv, kv) * dm).astype(_BF)
            dab = (_dot_nt(dov, vv) * dm).astype(_BF)
            dox = (dov.astype(_F32) * xi_ref[hh]).astype(_BF)
            vz = (vv.astype(_F32) * zeta).astype(_BF)
            dq = _dot(dab, kv) + _dot_nt(dox, rb)
            dk = _dot_tn(dab, qv) + _dot_nt(vz, gb)
            dv = _dot_tn(sb, dov) + _dot(kv, gb) * zeta
            g_scr[hh] = gc_ref[hh, 0:1, :] * gst + _dot_tn(qv, dox)
            dq_ref[:, cols] = _rope_t(dq, cs, sn).astype(_BF)
            dk_ref[:, cols] = (_rope_t(dk, cs, sn) * (HEAD_D ** -0.5)).astype(_BF)
            dv_ref[:, cols] = dv.astype(_BF)

    return pl.pallas_call(
        body, name="retention_bwd", grid=(nc,),
        in_specs=[blk, blk, blk, blk, pl.BlockSpec((N_HEAD, 1, HEAD_D, HEAD_D), lambda n: (0, nc - 1 - n, 0, 0))]
        + _head_specs() + [rope_blk, rope_blk],
        out_specs=[blk, blk, blk],
        out_shape=[jax.ShapeDtypeStruct((L, RET_W), _BF)] * 3,
        scratch_shapes=[pltpu.VMEM((N_HEAD, HEAD_D, HEAD_D), _F32)],
        compiler_params=_params("arbitrary"),
    )(q, k, v, do, r_prev, *consts, cosf, sinf)


def _inproj_bwd(pieces, w_in, x, dx2, g1, tm):
    L = x.shape[0]

    def body(p0, p1, p2, p3, p4, w_ref, x_ref, dx2_ref, g_ref, dx_ref, dg_ref):
        @pl.when(pl.program_id(0) == 0)
        def _():
            dg_ref[...] = jnp.zeros_like(dg_ref)

        dh = None
        for j, p in enumerate((p0, p1, p2, p3, p4)):
            part = _dot_nt(p[...].astype(_BF), w_ref[:, j * RET_W:(j + 1) * RET_W])
            dh = part if dh is None else dh + part
        dz, dgr = _rms_bwd(x_ref[...], g_ref[...], dh)
        dx_ref[...] = dx2_ref[...] + dz
        dg_ref[...] += jnp.sum(dgr, axis=0, keepdims=True)

    return pl.pallas_call(
        body, name="inproj_bwd", grid=(L // tm,),
        in_specs=[_row_spec(tm, RET_W)] * 5 + [_weight_spec((D_MODEL, IN_COLS)), _row_spec(tm, D_MODEL),
                                                 _row_spec(tm, D_MODEL), _full_spec((1, D_MODEL))],
        out_specs=[_row_spec(tm, D_MODEL), _full_spec((1, D_MODEL))],
        out_shape=[jax.ShapeDtypeStruct((L, D_MODEL), _F32), jax.ShapeDtypeStruct((1, D_MODEL), _F32)],
        compiler_params=_params("arbitrary"),
    )(*pieces, w_in, x, dx2, g1)


def _sum_adamw(parts, w, m, v, tr, name):
    _, R, Cc = parts.shape

    def body(p_ref, w_ref, m_ref, v_ref, g_ref, d_ref, nm_ref, nv_ref):
        gv = p_ref[0].astype(_F32)
        for s in range(1, N_DEV):
            gv = gv + p_ref[s].astype(_F32)
        g_ref[...] = gv
        nm = ADAM_B1 * m_ref[...] + (1.0 - ADAM_B1) * gv
        nv = ADAM_B2 * v_ref[...] + (1.0 - ADAM_B2) * (gv * gv)
        m_hat = nm / (1.0 - ADAM_B1 ** ADAM_STEP)
        v_hat = nv / (1.0 - ADAM_B2 ** ADAM_STEP)
        d_ref[...] = -ADAM_LR * (m_hat / (jnp.sqrt(v_hat) + ADAM_EPS) + ADAM_WD * w_ref[...])
        nm_ref[...] = nm
        nv_ref[...] = nv

    spec = _row_spec(tr, Cc)
    return pl.pallas_call(
        body, name=name, grid=(R // tr,),
        in_specs=[pl.BlockSpec((N_DEV, tr, Cc), lambda i: (0, i, 0))] + [spec] * 3, out_specs=[spec] * 4,
        out_shape=[jax.ShapeDtypeStruct((R, Cc), _F32)] * 4,
        compiler_params=_params("parallel"),
    )(parts, w, m, v)


def _my_place():
    return lax.axis_index("x"), lax.axis_index("y"), lax.axis_index("c")


def _all_gather(blocks):
    n = len(blocks)

    def body(*refs):
        x_refs, out_refs, done_ref = refs[:n], refs[n:2 * n], refs[2 * n]
        send_sems, recv_sems, local_sems = refs[2 * n + 1:]
        done_ref[...] = jnp.zeros_like(done_ref)
        x, y, c = _my_place()
        me, sibling = (x, y, c), (x, y, 1 - c)
        chips = [(1 - x, y), (x, 1 - y), (1 - x, 1 - y)]

        def slot(a, px, py, pc):
            return out_refs[a].at[4 * px + 2 * py + pc]

        def copy(a, k, blk, to, own=False):
            return pltpu.make_async_remote_copy(
                src_ref=x_refs[a] if own else slot(a, *blk), dst_ref=slot(a, *blk),
                send_sem=send_sems.at[a, k], recv_sem=recv_sems.at[a, k], device_id=to, device_id_type=MESH)

        mine = [pltpu.make_async_copy(x_refs[a], slot(a, *me), local_sems.at[a]) for a in range(n)]
        for cp in mine:
            cp.start()
        first = []
        for a in range(n):
            first.append(copy(a, 0, me, sibling, own=True))
            first += [copy(a, 1 + j, me, (*chip, c), own=True) for j, chip in enumerate(chips)]
        for cp in first:
            cp.start()
        passed = []
        for j, chip in enumerate(chips):
            for a in range(n):
                copy(a, 1 + j, (*chip, c), me).wait_recv()
                fwd = copy(a, 4 + j, (*chip, c), sibling)
                fwd.start()
                passed.append(fwd)
        for a in range(n):
            copy(a, 0, sibling, me).wait_recv()
            for j, chip in enumerate(chips):
                copy(a, 4 + j, (*chip, 1 - c), me).wait_recv()
        for cp in first + passed:
            cp.wait_send()
        for cp in mine:
            cp.wait()

    any_spec = pl.BlockSpec(memory_space=pl.ANY)
    outs = pl.pallas_call(
        body, name="weights_all_gather",
        in_specs=[any_spec] * n, out_specs=[any_spec] * n + [pl.BlockSpec(memory_space=pltpu.VMEM)],
        out_shape=[jax.ShapeDtypeStruct((N_DEV,) + b.shape, b.dtype) for b in blocks]
        + [jax.ShapeDtypeStruct((SUBLANES, LANES), _F32)],
        scratch_shapes=[pltpu.SemaphoreType.DMA((n, 7)), pltpu.SemaphoreType.DMA((n, 7)), pltpu.SemaphoreType.DMA((n,))],
    )(*blocks)
    return outs[:n], outs[n]


def _exchange(bigs, small):
    n = len(bigs)
    r = small.shape[0]

    def body(*refs):
        in_refs, out_refs = refs[:n + 1], refs[n + 1:2 * n + 2]
        send_sems, recv_sems, local_sems = refs[2 * n + 2:]
        x, y, c = _my_place()
        me = 4 * x + 2 * y + c
        own = [pltpu.make_async_copy(in_refs[a].at[me], out_refs[a].at[me], local_sems.at[a]) for a in range(n)]
        own.append(pltpu.make_async_copy(in_refs[n], out_refs[n].at[me], local_sems.at[n]))
        for cp in own:
            cp.start()
        copies = []
        for kk in range(1, N_DEV):
            px, py, pc = x ^ (kk >> 2), y ^ ((kk >> 1) & 1), c ^ (kk & 1)
            peer = 4 * px + 2 * py + pc
            for a in range(n + 1):
                src = in_refs[a].at[peer] if a < n else in_refs[a]
                copies.append(pltpu.make_async_remote_copy(
                    src_ref=src, dst_ref=out_refs[a].at[me],
                    send_sem=send_sems.at[a, kk - 1], recv_sem=recv_sems.at[a, kk - 1],
                    device_id=(px, py, pc), device_id_type=MESH))
        for cp in copies:
            cp.start()
        for cp in copies:
            cp.wait_recv()
        for cp in copies:
            cp.wait_send()
        for cp in own:
            cp.wait()

    any_spec = pl.BlockSpec(memory_space=pl.ANY)
    outs = pl.pallas_call(
        body, name="grad_exchange",
        in_specs=[any_spec] * (n + 1), out_specs=[any_spec] * (n + 1),
        out_shape=[jax.ShapeDtypeStruct(b.shape, b.dtype) for b in bigs]
        + [jax.ShapeDtypeStruct((N_DEV, r, LANES), small.dtype)],
        scratch_shapes=[pltpu.SemaphoreType.DMA((n + 1, 7)), pltpu.SemaphoreType.DMA((n + 1, 7)),
                        pltpu.SemaphoreType.DMA((n + 1,))],
    )(*bigs, small)
    return outs[:n], outs[n]


HBM_SPEC = pl.BlockSpec(memory_space=pltpu.HBM)
SEM_SPEC = pl.BlockSpec(memory_space=pltpu.SEMAPHORE)
DATAFLOW = pltpu.SideEffectType.DATAFLOW_SIDE_EFFECTING


def _my_index():
    x, y, c = _my_place()
    return 4 * x + 2 * y + c


def _landing(own_block):
    zone = lax.empty((N_DEV,) + own_block.shape, own_block.dtype)
    return lax.dynamic_update_index_in_dim(zone, own_block, _my_index(), 0)


def _split_copies(src_refs, land_refs, send_sems, recv_sems, gather):
    x, y, c = _my_place()
    me = 4 * x + 2 * y + c
    copies = []
    for kk in range(1, N_DEV):
        px, py, pc = x ^ (kk >> 2), y ^ ((kk >> 1) & 1), c ^ (kk & 1)
        peer = 4 * px + 2 * py + pc
        for a, (src, land) in enumerate(zip(src_refs, land_refs)):
            copies.append(pltpu.make_async_remote_copy(
                src_ref=src if gather else src.at[peer], dst_ref=land.at[me],
                send_sem=send_sems.at[a * 7 + kk - 1], recv_sem=recv_sems.at[a * 7 + kk - 1],
                device_id=(px, py, pc), device_id_type=MESH))
    return copies


def _split_start(srcs, lands, gather, name):
    n = len(srcs)

    def body(*refs):
        src_refs, land_refs = refs[:n], refs[n:2 * n]
        send_sems, recv_sems = refs[2 * n], refs[2 * n + 1]
        token = refs[-1]
        for cp in _split_copies(src_refs, land_refs, send_sems, recv_sems, gather):
            cp.start()
        token[...] = jnp.zeros_like(token)

    outs = pl.pallas_call(
        body, name=name,
        out_shape=(pltpu.SemaphoreType.DMA((7 * n,)), pltpu.SemaphoreType.DMA((7 * n,)),
                   *[pltpu.HBM(t.shape, t.dtype) for t in srcs], *[pltpu.HBM(t.shape, t.dtype) for t in lands],
                   jax.ShapeDtypeStruct((SUBLANES, LANES), _F32)),
        in_specs=[HBM_SPEC] * (2 * n),
        out_specs=(SEM_SPEC, SEM_SPEC, *[HBM_SPEC] * (2 * n), pl.BlockSpec(memory_space=pltpu.VMEM)),
        input_output_aliases={i: 2 + i for i in range(2 * n)},
        compiler_params=pltpu.CompilerParams(has_side_effects=DATAFLOW),
    )(*[pltpu.with_memory_space_constraint(t, pltpu.HBM) for t in list(srcs) + list(lands)])
    return outs[0], outs[1], outs[2:2 + n], outs[2 + n:2 + 2 * n], outs[-1]


def _split_wait(send_sems, recv_sems, srcs, lands, after, gather, name):
    n = len(srcs)

    def body(*refs):
        src_refs, land_refs = refs[:n], refs[n:2 * n]
        send_s, recv_s = refs[2 * n], refs[2 * n + 1]
        for cp in _split_copies(src_refs, land_refs, send_s, recv_s, gather):
            cp.wait_send()
            cp.wait_recv()

    outs = pl.pallas_call(
        body, name=name,
        out_shape=tuple(pltpu.HBM(t.shape, t.dtype) for t in list(srcs) + list(lands)),
        in_specs=[HBM_SPEC] * (2 * n) + [SEM_SPEC, SEM_SPEC, pl.BlockSpec(memory_space=pl.ANY)],
        out_specs=tuple([HBM_SPEC] * (2 * n)),
        input_output_aliases={i: i for i in range(2 * n)},
        compiler_params=pltpu.CompilerParams(has_side_effects=DATAFLOW),
    )(*srcs, *lands, send_sems, recv_sems, after)
    return outs[n:]


def _discretize(lam_re, lam_im, log_dt, b_re, b_im):
    lr = jnp.minimum(lam_re, -1e-4)
    li = lam_im
    dt = jnp.exp(log_dt)[:, None]
    er = jnp.exp(lr * dt)
    ar, ai = er * jnp.cos(li * dt), er * jnp.sin(li * dt)
    den = lr * lr + li * li
    cr = ((ar - 1.0) * lr + ai * li) / den
    ci = (ai * lr - (ar - 1.0) * li) / den
    bbr = cr[:, :, None] * b_re - ci[:, :, None] * b_im
    bbi = cr[:, :, None] * b_im + ci[:, :, None] * b_re
    return ar, ai, bbr, bbi


def _cmul(ar, ai, br, bi):
    return ar * br - ai * bi, ar * bi + ai * br


def _cpowers(ar, ai, n):
    pr, pi = ar[None], ai[None]
    while pr.shape[0] < n:
        nr, ni = _cmul(pr, pi, pr[-1][None], pi[-1][None])
        pr, pi = jnp.concatenate([pr, nr]), jnp.concatenate([pi, ni])
    return pr[:n], pi[:n]


def _scan_tables(ar, ai, seg, reverse):
    if reverse:
        ai = -ai
    ar, ai = ar.reshape(N_KB, KB_STATES), ai.reshape(N_KB, KB_STATES)
    pr, pi = _cpowers(ar, ai, seg)
    a1 = (pr[-1], pi[-1])
    a2 = _cmul(*a1, *a1)
    a4 = _cmul(*a2, *a2)
    row = jnp.arange(SUBLANES)[None, :, None]
    wide = lambda t: jnp.broadcast_to(t[:, None, :], (N_KB, SUBLANES, KB_STATES))
    tabs = [wide(ar), wide(ai)]
    for dist, (qr, qi) in ((1, a1), (2, a2), (4, a4)):
        keep = (row < SUBLANES - dist) if reverse else (row >= dist)
        tabs += [jnp.where(keep, wide(qr), 0.0), jnp.where(keep, wide(qi), 0.0)]
    tabs += [wide(a1[0]), wide(a1[1])]
    if reverse:
        pr, pi = pr[::-1], pi[::-1]
    pw = jnp.transpose(jnp.concatenate([pr, pi], axis=-1), (1, 0, 2))[:, :, None, :]
    return jnp.stack(tabs, axis=1).astype(_F32), pw.astype(_F32)


def _block_diag_in(br, bi):
    eye = jnp.eye(GROUPS_PER_KB, dtype=_F32)
    one = lambda t: jnp.einsum("kgpc,gh->kgchp", t.reshape(N_KB, GROUPS_PER_KB, N_STATE, SSM_GC), eye).reshape(
        N_KB, LANES, KB_STATES)
    return jnp.concatenate([one(br), one(bi)], axis=-1)


def _block_diag_in_t(dmat):
    d6 = dmat.reshape(N_KB, GROUPS_PER_KB, SSM_GC, 2, GROUPS_PER_KB, N_STATE)
    eye = jnp.eye(GROUPS_PER_KB, dtype=_F32)
    both = jnp.einsum("kgcrhp,gh->rkgpc", d6, eye).reshape(2, N_GROUP, N_STATE, SSM_GC)
    return both[0], both[1]


def _block_diag_out(c_re, c_im):
    eye = jnp.eye(GROUPS_PER_KB, dtype=_F32)
    one = lambda t: jnp.einsum("kgcp,gh->khpgc", t.reshape(N_KB, GROUPS_PER_KB, SSM_GC, N_STATE), eye).reshape(
        N_KB, KB_STATES, LANES)
    return jnp.concatenate([one(c_re), -one(c_im)], axis=1)


def _block_diag_out_t(dmat):
    d6 = dmat.reshape(N_KB, 2, GROUPS_PER_KB, N_STATE, GROUPS_PER_KB, SSM_GC)
    eye = jnp.eye(GROUPS_PER_KB, dtype=_F32)
    both = jnp.einsum("krhpgc,gh->rkgcp", d6, eye).reshape(2, N_GROUP, SSM_GC, N_STATE)
    return both[0], -both[1]


SMALL_NAMES = ("norm_mix_pre", "norm_mix_post", "ret_gn_gain", "ssm_lambda_re", "ssm_lambda_im", "ssm_log_dt",
               "ssm_b_re", "ssm_b_im", "ssm_c_re", "ssm_c_im", "ssm_d", "norm_mlp_pre", "norm_mlp_post")


def _local_grads(x, tgt, small, weights, emit, tm, tk, tb, zero=0.0):
    L = x.shape[0]
    g1, g2, ggn = small["norm_mix_pre"], small["norm_mix_post"], small["ret_gn_gain"]
    g3, g4, d_skip = small["norm_mlp_pre"], small["norm_mlp_post"], small["ssm_d"]

    half = HEAD_D // 2
    inv_freq = ROPE_BASE ** (-jnp.arange(half, dtype=_F32) / half)
    ang = jnp.arange(L, dtype=_F32)[:, None] * inv_freq[None, :] + zero
    cosf = jnp.concatenate([jnp.cos(ang), jnp.cos(ang)], axis=-1)
    sinf = jnp.concatenate([-jnp.sin(ang), jnp.sin(ang)], axis=-1)
    consts = _ret_consts()

    disc_in = (small["ssm_lambda_re"][0], small["ssm_lambda_im"][0], small["ssm_log_dt"][0],
               small["ssm_b_re"][0], small["ssm_b_im"][0])
    (ar, ai, bbr, bbi), disc_vjp = jax.vjp(_discretize, *disc_in)
    bmat = _block_diag_in(bbr, bbi).astype(_BF)
    cmat = _block_diag_out(small["ssm_c_re"][0], small["ssm_c_im"][0]).astype(_BF)
    seg = tb // SUBLANES
    tab_f, pw_f = _scan_tables(ar, ai, seg, False)
    tab_r, pw_r = _scan_tables(ar, ai, seg, True)

    (w_in,) = weights("in", sinf)
    h1, q, k, v, gate, u = _inproj_fwd(x, g1, w_in, cosf, sinf, tm)
    o, y_ret, r_prev = _retention_fwd(q, k, v, gate, ggn, consts)
    s, xs, ent = _s5_fwd(u, bmat, cmat, tab_f, pw_f, d_skip, tb)
    w_glu, w_out = weights("mix", s)
    ys, glu, cat, mix, x2 = _mixout_fwd(s, y_ret, x, w_glu, w_out, g2, tm)
    w_ff1, w_ff2 = weights("mlp", x2)
    h3, f1 = _ff1_fwd(x2, g3, w_ff1, tm)
    dy, dm, dg4, sq = _ff2_loss(f1, x2, tgt, g4, w_ff2, min(2 * tm, L))

    df1, dw_ff2 = _ff2_bwd(dm, f1, w_ff2, min(1024, L), 1024)
    dx2, dmix, dg3, dg2 = _ff1_bwd(df1, w_ff1, x2, mix, dy, g3, g2, min(2 * tm, L))
    dw_ff1 = _matmul_tn(h3, df1, tk, FF1_COLS, "dw_ff1", slots=True)
    zero = emit({"w_ff1": dw_ff1, "w_ff2": dw_ff2})
    dglu, ds, dgate, do, dggn = _mixout_bwd(dmix, w_out, w_glu, glu, s, o, gate, ggn if zero is None else ggn + zero, tm)
    dw_out = _matmul_tn(cat, dmix, tk, 1024, "dw_out")
    dw_glu = _matmul_tn(ys, dglu, tk, 1024, "dw_glu")
    zero = emit({"w_glu": dw_glu, "w_out": dw_out})
    du, dbmat, dcmat, da8, dd = _s5_bwd(u, ds, xs, ent, bmat, cmat, tab_r, pw_r,
                                        d_skip if zero is None else d_skip + zero, tb)
    dq, dk, dv = _retention_bwd(q, k, v, do, r_prev, consts, cosf, sinf)
    pieces = (dq, dk, dv, dgate, du)
    dw_in = jnp.concatenate([_matmul_tn(h1, p, tk, RET_W, "dw_in_%d" % j) for j, p in enumerate(pieces)], axis=1)
    zero = emit({"w_in": dw_in})
    gx, dg1 = _inproj_bwd(pieces, w_in, x, dx2, g1 if zero is None else g1 + zero, tm)

    da = jnp.sum(da8, axis=1)
    dar = da[:, :KB_STATES].reshape(N_GROUP, N_STATE)
    dai = da[:, KB_STATES:].reshape(N_GROUP, N_STATE)
    dbr, dbi = _block_diag_in_t(dbmat)
    dlre, dlim, dldt, dbre, dbim = disc_vjp((dar, dai, dbr, dbi))
    dcre, dcim = _block_diag_out_t(dcmat)

    gsmall = {
        "norm_mix_pre": dg1, "norm_mix_post": dg2, "ret_gn_gain": dggn,
        "ssm_lambda_re": dlre[None], "ssm_lambda_im": dlim[None], "ssm_log_dt": dldt[None],
        "ssm_b_re": dbre[None], "ssm_b_im": dbim[None], "ssm_c_re": dcre[None], "ssm_c_im": dcim[None],
        "ssm_d": dd, "norm_mlp_pre": dg3, "norm_mlp_post": dg4,
    }
    return sq, gx, gsmall


BIG_SHAPES = {"w_in": (D_MODEL, IN_COLS // N_DEV), "w_glu": (SSM_W, 2 * SSM_W // N_DEV), "w_out": (D_MODEL // N_DEV, D_MODEL),
              "w_ff1": (D_MODEL, FF1_COLS), "w_ff2": (D_FF // N_DEV, D_MODEL)}
BIG_NAMES = ("w_in", "w_glu", "w_out", "w_ff1", "w_ff2")


def _cols_from_slots(g):
    return jnp.transpose(g, (1, 0, 2)).reshape(g.shape[1], N_DEV * g.shape[2])


def _cols_to_slots(dw):
    r, cols = dw.shape
    return jnp.transpose(dw.reshape(r, N_DEV, cols // N_DEV), (1, 0, 2))


WEIGHT_GROUPS = {"in": ("w_in",), "mix": ("w_glu", "w_out"), "mlp": ("w_ff1", "w_ff2")}


def _weight_from_slots(name, g):
    if name in ("w_in", "w_glu"):
        return _cols_from_slots(g)
    if name == "w_ff1":
        return g
    return g.reshape(N_DEV * g.shape[1], g.shape[2])


def _grad_slots(name, dw):
    if name in ("w_in", "w_glu"):
        return _cols_to_slots(dw)
    if name == "w_ff1":
        return dw
    return dw.reshape((N_DEV,) + BIG_SHAPES[name])


PIECE_ROWS = 8


def _small_layout(shapes):
    off, rows = {}, 0
    for n in SMALL_NAMES:
        off[n] = rows
        rows += -(-math.prod(shapes[n]) // (PIECE_ROWS * LANES)) * PIECE_ROWS
    return off, rows, rows + PIECE_ROWS


def _pack_small(vals, shapes, last=None):
    parts = []
    for n in SMALL_NAMES:
        flat = vals[n].reshape(-1).astype(_F32)
        pad = -flat.shape[0] % (PIECE_ROWS * LANES)
        if pad:
            flat = jnp.concatenate([flat, jnp.zeros((pad,), _F32)])
        parts.append(flat.reshape(-1, LANES))
    parts.append(jnp.zeros((PIECE_ROWS, LANES), _F32) if last is None else last)
    return jnp.concatenate(parts, axis=0)


def _unpack_small(buf, shapes):
    off, _, _ = _small_layout(shapes)
    out = {}
    for n in SMALL_NAMES:
        size = math.prod(shapes[n])
        rows = -(-size // LANES)
        out[n] = buf[off[n]:off[n] + rows].reshape(-1)[:size].reshape(shapes[n])
    return out


WEIGHT_NAMES = ('norm_mix_pre', 'norm_mix_post', 'w_in', 'ret_gn_gain', 'ssm_lambda_re', 'ssm_lambda_im', 'ssm_log_dt',
                'ssm_b_re', 'ssm_b_im', 'ssm_c_re', 'ssm_c_im', 'ssm_d', 'w_glu', 'w_out', 'norm_mlp_pre',
                'norm_mlp_post', 'w_ff1', 'w_ff2')


def kernel(x, norm_mix_pre, norm_mix_post, w_in, ret_gn_gain, ssm_lambda_re, ssm_lambda_im, ssm_log_dt, ssm_b_re, ssm_b_im, ssm_c_re, ssm_c_im, ssm_d, w_glu, w_out, norm_mlp_pre, norm_mlp_post, w_ff1, w_ff2, loss_target, m_norm_mix_pre, m_norm_mix_post, m_w_in, m_ret_gn_gain, m_ssm_lambda_re, m_ssm_lambda_im, m_ssm_log_dt, m_ssm_b_re, m_ssm_b_im, m_ssm_c_re, m_ssm_c_im, m_ssm_d, m_w_glu, m_w_out, m_norm_mlp_pre, m_norm_mlp_post, m_w_ff1, m_w_ff2, v_norm_mix_pre, v_norm_mix_post, v_w_in, v_ret_gn_gain, v_ssm_lambda_re, v_ssm_lambda_im, v_ssm_log_dt, v_ssm_b_re, v_ssm_b_im, v_ssm_c_re, v_ssm_c_im, v_ssm_d, v_w_glu, v_w_out, v_norm_mlp_pre, v_norm_mlp_post, v_w_ff1, v_w_ff2):
    args = dict(locals())
    w = {n: args[n] for n in WEIGHT_NAMES}
    m = {n: args["m_" + n] for n in WEIGHT_NAMES}
    v = {n: args["v_" + n] for n in WEIGHT_NAMES}
    L = x.shape[1]
    tm = min(256, L)
    tk = min(2048, L)
    tb = min(512, L)

    gathers, zero = {}, jnp.zeros((), _F32)
    for group, names in WEIGHT_GROUPS.items():
        blocks = [w[n][0].astype(_BF) for n in names]
        blocks[0] = blocks[0] + zero.astype(_BF)
        gathers[group] = _split_start(blocks, [_landing(b) for b in blocks], True, "weights_start_" + group)
        zero = gathers[group][4][0, 0]

    def weights(group, after):
        landed = _split_wait(*gathers[group][:4], after, True, "weights_wait_" + group)
        return [_weight_from_slots(n, g) for n, g in zip(WEIGHT_GROUPS[group], landed)]

    in_flight = []

    def emit(dws):
        names = sorted(dws)
        srcs = [_grad_slots(n, dws[n]) for n in names]
        lands = [_landing(lax.dynamic_index_in_dim(t, _my_index(), 0, keepdims=False)) for t in srcs]
        started = _split_start(srcs, lands, False, "grads_start_" + "_".join(names))
        in_flight.append((names, started))
        return started[4][0, 0]

    small_w = {n: w[n] for n in SMALL_NAMES}
    sq, gx, gsmall = _local_grads(x[0], loss_target[0], small_w, weights, emit, tm, tk, tb, zero=zero)

    shapes = {n: w[n].shape for n in SMALL_NAMES}
    loss_rows = jnp.broadcast_to(0.5 / D_MODEL * jnp.sum(sq), (PIECE_ROWS, LANES)).astype(_F32)
    small_buf = _pack_small(gsmall, shapes, loss_rows)
    small_started = _split_start([small_buf], [_landing(small_buf)], True, "small_grads_start")
    grads, delta, new_m, new_v = {}, {}, {}, {}
    after = small_started[4]
    for names, started in in_flight:
        landed = _split_wait(*started[:4], after, False, "grads_wait_" + "_".join(names))
        for n, parts in zip(names, landed):
            res = _sum_adamw(parts, w[n][0], m[n][0], v[n][0], min(256, BIG_SHAPES[n][0]), "adamw_" + n)
            grads[n], delta[n], new_m[n], new_v[n] = (t[None] for t in res)
        after = res[1]
    small_parts = _split_wait(*small_started[:4], after, True, "small_grads_wait")[0]
    sw, sm, sv = _pack_small(w, shapes), _pack_small(m, shapes), _pack_small(v, shapes)
    res = _sum_adamw(small_parts, sw, sm, sv, sw.shape[0], "adamw_small")
    for dst, buf in zip((grads, delta, new_m, new_v), res):
        dst.update(_unpack_small(buf, shapes))
    _, loss_at, _ = _small_layout(shapes)
    loss = res[0][loss_at, 0]

    return (loss, gx[None], *[grads[n] for n in WEIGHT_NAMES], *[delta[n] for n in WEIGHT_NAMES],
            *[new_m[n] for n in WEIGHT_NAMES], *[new_v[n] for n in WEIGHT_NAMES])
```

```python
import math

import jax
import jax.numpy as jnp
from jax import lax
from jax.experimental import pallas as pl
from jax.experimental.pallas import tpu as pltpu

_BF = jnp.bfloat16
_F32 = jnp.float32

D_MODEL = 1024
RET_W = 512
N_HEAD = 4
HEAD_D = 128
CHUNK = 128
SSM_W = 512
SSM_GC = 16
N_GROUP = 32
N_STATE = 64
GROUPS_PER_KB = 8
N_KB = 4
KB_STATES = GROUPS_PER_KB * N_STATE
D_FF = 4096
IN_COLS = 2560
NORM_EPS = 1e-6
ROPE_BASE = 10000.0
N_DEV = 8

ADAM_LR = 0.001
ADAM_B1 = 0.9
ADAM_B2 = 0.999
ADAM_EPS = 1e-08
ADAM_WD = 0.01
ADAM_STEP = 10

SUBLANES = 8
LANES = 128
VMEM_LIMIT = 52 * 1024 * 1024
RET_STEP_CHUNKS = 4
KB_PER_STEP = 2
SCAN_UNROLL = 2

MESH = pl.DeviceIdType.MESH


def _params(*sem):
    return pltpu.CompilerParams(dimension_semantics=sem, vmem_limit_bytes=VMEM_LIMIT)


def _dot(a, b):
    return jnp.dot(a, b, preferred_element_type=_F32)


def _dot_nt(a, b):
    return lax.dot_general(a, b, (((1,), (1,)), ((), ())), preferred_element_type=_F32)


def _dot_tn(a, b):
    return lax.dot_general(a, b, (((0,), (0,)), ((), ())), preferred_element_type=_F32)


def _rms_r(z):
    return lax.rsqrt(jnp.mean(z * z, axis=-1, keepdims=True) + NORM_EPS)


def _rms_bwd(z, g, dn):
    r = _rms_r(z)
    t = dn * g
    dz = r * t - z * (r * r * r * jnp.mean(t * z, axis=-1, keepdims=True))
    return dz, dn * z * r


def _rope(t, cs, sn):
    return t * cs + pltpu.roll(t, HEAD_D // 2, 1) * sn


def _rope_t(t, cs, sn):
    return t * cs - pltpu.roll(t, HEAD_D // 2, 1) * sn


def _sigmoid(z):
    return 1.0 / (1.0 + jnp.exp(-z))


_GELU_C = math.sqrt(2.0 / math.pi)


def _gelu(z):
    return 0.5 * z * (1.0 + jnp.tanh(_GELU_C * (z + 0.044715 * z * z * z)))


def _gelu_grad(z):
    th = jnp.tanh(_GELU_C * (z + 0.044715 * z * z * z))
    return 0.5 * (1.0 + th) + 0.5 * z * (1.0 - th * th) * _GELU_C * (1.0 + 3 * 0.044715 * z * z)


ROW_CHUNK = 256


def _row_chunks(tm):
    return [pl.ds(i, min(ROW_CHUNK, tm)) for i in range(0, tm, ROW_CHUNK)]


def _row_spec(tm, n):
    return pl.BlockSpec((tm, n), lambda i: (i, 0))


def _full_spec(shape):
    nd = len(shape)
    return pl.BlockSpec(shape, lambda *_: (0,) * nd)


def _weight_spec(shape):
    nd = len(shape)
    return pl.BlockSpec(shape, lambda *_: (0,) * nd, pipeline_mode=pl.Buffered(1))


def _inproj_fwd(x, g1, w_in, cosf, sinf, tm):
    L = x.shape[0]

    def body(x_ref, g_ref, w_ref, cos_ref, sin_ref, h_ref, q_ref, k_ref, v_ref, gate_ref, u_ref):
        xv = x_ref[...]
        h = (xv * _rms_r(xv) * g_ref[...]).astype(_BF)
        h_ref[...] = h
        proj = _dot(h, w_ref[...])
        cs, sn = cos_ref[...], sin_ref[...]
        for hh in range(N_HEAD):
            lo = hh * HEAD_D
            q_ref[:, lo:lo + HEAD_D] = _rope(proj[:, lo:lo + HEAD_D], cs, sn).astype(_BF)
            kh = _rope(proj[:, RET_W + lo:RET_W + lo + HEAD_D], cs, sn) * (HEAD_D ** -0.5)
            k_ref[:, lo:lo + HEAD_D] = kh.astype(_BF)
        v_ref[...] = proj[:, 2 * RET_W:3 * RET_W].astype(_BF)
        gate_ref[...] = proj[:, 3 * RET_W:4 * RET_W]
        u_ref[...] = proj[:, 4 * RET_W:]

    return pl.pallas_call(
        body, name="inproj_fwd", grid=(L // tm,),
        in_specs=[_row_spec(tm, D_MODEL), _full_spec((1, D_MODEL)), _weight_spec((D_MODEL, IN_COLS)),
                  _row_spec(tm, HEAD_D), _row_spec(tm, HEAD_D)],
        out_specs=[_row_spec(tm, D_MODEL)] + [_row_spec(tm, RET_W)] * 5,
        out_shape=[jax.ShapeDtypeStruct((L, D_MODEL), _BF)] + [jax.ShapeDtypeStruct((L, RET_W), _BF)] * 3
        + [jax.ShapeDtypeStruct((L, RET_W), _F32)] * 2,
        compiler_params=_params("parallel"),
    )(x, g1, w_in, cosf, sinf)


def _ret_consts():
    lg = jnp.log(1.0 - jnp.exp(jnp.linspace(math.log(1.0 / 32), math.log(1.0 / 512), N_HEAD))).astype(_F32)
    idx = jnp.arange(CHUNK, dtype=_F32)
    diff = idx[:, None] - idx[None, :]
    decay = jnp.where(diff[None] >= 0, jnp.exp(jnp.maximum(diff, 0.0)[None] * lg[:, None, None]), 0.0)
    zeta = jnp.exp((CHUNK - 1 - idx)[None, :] * lg[:, None])
    xi = jnp.exp((idx + 1.0)[None, :] * lg[:, None])
    gc = jnp.exp(CHUNK * lg)
    wide = lambda t: jnp.broadcast_to(t[:, :, None], (N_HEAD, CHUNK, HEAD_D)).astype(_F32)
    gcw = jnp.broadcast_to(gc[:, None, None], (N_HEAD, SUBLANES, HEAD_D)).astype(_F32)
    return decay.astype(_F32), wide(xi), wide(zeta), gcw


def _head_specs():
    c3 = _full_spec((N_HEAD, CHUNK, CHUNK))
    return [c3, c3, c3, _full_spec((N_HEAD, SUBLANES, HEAD_D))]


def _retention_fwd(q, k, v, gate, ggn, consts):
    L = q.shape[0]
    nc = L // CHUNK
    cps = math.gcd(RET_STEP_CHUNKS, nc)
    blk = pl.BlockSpec((cps * CHUNK, RET_W), lambda n: (n, 0))

    def body(q_ref, k_ref, v_ref, gate_ref, ggn_ref, dm_ref, xi_ref, zeta_ref, gc_ref,
             o_ref, y_ref, rp_ref, r_scr):
        @pl.when(pl.program_id(0) == 0)
        def _():
            r_scr[...] = jnp.zeros_like(r_scr)

        for hh in range(N_HEAD):
            cols = slice(hh * HEAD_D, (hh + 1) * HEAD_D)
            state = r_scr[hh]
            for c in range(cps):
                rows = slice(c * CHUNK, (c + 1) * CHUNK)
                qv, kv, vv = q_ref[rows, cols], k_ref[rows, cols], v_ref[rows, cols]
                s = _dot_nt(qv, kv) * dm_ref[hh]
                o = _dot(s.astype(_BF), vv) + _dot(qv, state.astype(_BF)) * xi_ref[hh]
                o_ref[rows, cols] = o
                rp_ref[hh, c] = state
                vz = (vv.astype(_F32) * zeta_ref[hh]).astype(_BF)
                state = gc_ref[hh, 0:1, :] * state + _dot_tn(kv, vz)
                dlt = o - jnp.mean(o, axis=-1, keepdims=True)
                on = dlt * lax.rsqrt(jnp.mean(dlt * dlt, axis=-1, keepdims=True) + NORM_EPS)
                gt = gate_ref[rows, cols]
                y_ref[rows, cols] = (gt * _sigmoid(gt) * (on * ggn_ref[:, cols])).astype(_BF)
            r_scr[hh] = state

    return pl.pallas_call(
        body, name="retention_fwd", grid=(nc // cps,),
        in_specs=[blk, blk, blk, blk, _full_spec((1, RET_W))] + _head_specs(),
        out_specs=[blk, blk, pl.BlockSpec((N_HEAD, cps, HEAD_D, HEAD_D), lambda n: (0, n, 0, 0))],
        out_shape=[jax.ShapeDtypeStruct((L, RET_W), _F32), jax.ShapeDtypeStruct((L, RET_W), _BF),
                   jax.ShapeDtypeStruct((N_HEAD, nc, HEAD_D, HEAD_D), _F32)],
        scratch_shapes=[pltpu.VMEM((N_HEAD, HEAD_D, HEAD_D), _F32)],
        compiler_params=_params("arbitrary"),
    )(q, k, v, gate, ggn, *consts)


def _rows_to_segments(dst_scr, src_ref, seg):
    for g in range(dst_scr.shape[0]):
        for j in range(SUBLANES):
            dst_scr[g, pl.ds(j, seg, stride=SUBLANES), :] = src_ref[pl.ds(j * seg, seg), g * LANES:(g + 1) * LANES]


def _segments_to_rows(dst_ref, src_scr, seg):
    for g in range(src_scr.shape[0]):
        for j in range(SUBLANES):
            dst_ref[pl.ds(j * seg, seg), g * LANES:(g + 1) * LANES] = src_scr[g, pl.ds(j, seg, stride=SUBLANES), :]


def _scan_segments(x_ref, tab_ref, pw_ref, carry_ref, seg, reverse, entry_ref=None, fwd_ref=None, fwd_entry_ref=None,
                   da_ref=None):
    G = x_ref.shape[0]
    W = KB_STATES
    re, im = pl.ds(0, W), pl.ds(W, W)
    row_id = lax.broadcasted_iota(jnp.int32, (SUBLANES, W), 0)
    edge_in = (row_id == SUBLANES - 1) if reverse else (row_id == 0)
    edge_out = 0 if reverse else SUBLANES - 1
    a_tab = [(tab_ref[g, 0], tab_ref[g, 1]) for g in range(G)]

    def local(i, st):
        r = (seg - 1 - i) if reverse else i
        out = []
        for g in range(G):
            (ar, ai), (sr, si) = a_tab[g], st[g]
            nr = ar * sr - ai * si + x_ref[g, r, :, re]
            ni = ar * si + ai * sr + x_ref[g, r, :, im]
            x_ref[g, r, :, re] = nr
            x_ref[g, r, :, im] = ni
            out.append((nr, ni))
        return tuple(out)

    zero = jnp.zeros((SUBLANES, W), _F32)
    ends = lax.fori_loop(0, seg, local, tuple((zero, zero) for _ in range(G)), unroll=SCAN_UNROLL)

    entry = []
    shift = (SUBLANES - 1) if reverse else 1
    for g in range(G):
        er, ei = ends[g]
        fr = jnp.where(edge_in, carry_ref[g, :, re], pltpu.roll(er, shift, 0))
        fi = jnp.where(edge_in, carry_ref[g, :, im], pltpu.roll(ei, shift, 0))
        for j, dist in enumerate((1, 2, 4)):
            pr, pi = tab_ref[g, 2 + 2 * j], tab_ref[g, 3 + 2 * j]
            sh = (SUBLANES - dist) if reverse else dist
            sr, si = pltpu.roll(fr, sh, 0), pltpu.roll(fi, sh, 0)
            fr, fi = fr + pr * sr - pi * si, fi + pr * si + pi * sr
        br, bi = tab_ref[g, 8], tab_ref[g, 9]
        outr = br * fr - bi * fi + er
        outi = br * fi + bi * fr + ei
        carry_ref[g, :, re] = jnp.broadcast_to(outr[edge_out:edge_out + 1, :], (SUBLANES, W))
        carry_ref[g, :, im] = jnp.broadcast_to(outi[edge_out:edge_out + 1, :], (SUBLANES, W))
        entry.append((fr, fi))
        if entry_ref is not None:
            entry_ref[g, :, re] = fr
            entry_ref[g, :, im] = fi

    add_da = da_ref is not None

    def fix(r, st, first=False):
        out = []
        for g in range(G):
            fr, fi = entry[g]
            pwr, pwi = pw_ref[g, r, :, re], pw_ref[g, r, :, im]
            xr = x_ref[g, r, :, re] + (pwr * fr - pwi * fi)
            xi = x_ref[g, r, :, im] + (pwr * fi + pwi * fr)
            x_ref[g, r, :, re] = xr
            x_ref[g, r, :, im] = xi
            if add_da:
                prev = fwd_entry_ref.at[g] if first else fwd_ref.at[g, r - 1]
                xpr, xpi = prev[:, re], prev[:, im]
                out.append((st[g][0] + (xr * xpr + xi * xpi), st[g][1] + (xi * xpr - xr * xpi)))
            else:
                out.append(st[g])
        return tuple(out)

    if add_da:
        st = fix(0, tuple((zero, zero) for _ in range(G)), first=True)
        st = lax.fori_loop(1, seg, fix, st, unroll=SCAN_UNROLL)
        for g in range(G):
            da_ref[g, :, re] += st[g][0]
            da_ref[g, :, im] += st[g][1]
    else:
        lax.fori_loop(0, seg, fix, tuple((zero[0:1, 0:LANES],) for _ in range(G)), unroll=SCAN_UNROLL)


def _s5_specs(seg, time=lambda t: t):
    G = KB_PER_STEP
    return dict(
        x=pl.BlockSpec((G, seg, SUBLANES, 2 * KB_STATES), lambda kb, t: (kb, time(t), 0, 0)),
        ent=pl.BlockSpec((G, 1, SUBLANES, 2 * KB_STATES), lambda kb, t: (kb, time(t), 0, 0)),
        b=pl.BlockSpec((G, LANES, 2 * KB_STATES), lambda kb, t: (kb, 0, 0)),
        c=pl.BlockSpec((G, 2 * KB_STATES, LANES), lambda kb, t: (kb, 0, 0)),
        tab=pl.BlockSpec((G, 10, SUBLANES, KB_STATES), lambda kb, t: (kb, 0, 0, 0)),
        pw=pl.BlockSpec((G, seg, 1, 2 * KB_STATES), lambda kb, t: (kb, 0, 0, 0)),
        d=pl.BlockSpec((1, G * LANES), lambda kb, t: (0, kb)),
    )


def _s5_fwd(u, bmat, cmat, tab_f, pw_f, d_skip, tb):
    L = u.shape[0]
    nt = L // tb
    seg = tb // SUBLANES
    G = KB_PER_STEP
    ucol = pl.BlockSpec((tb, G * LANES), lambda kb, t: (t, kb))
    sp = _s5_specs(seg)

    def body(u_ref, b_ref, c_ref, tab_ref, pw_ref, d_ref, s_ref, x_ref, ent_ref, up_scr, y_scr, carry_scr):
        @pl.when(pl.program_id(1) == 0)
        def _():
            carry_scr[...] = jnp.zeros_like(carry_scr)

        _rows_to_segments(up_scr, u_ref, seg)
        for g in range(G):
            x_ref[g] = _dot(up_scr[g].astype(_BF), b_ref[g]).reshape(seg, SUBLANES, 2 * KB_STATES)
        _scan_segments(x_ref, tab_ref, pw_ref, carry_scr, seg, reverse=False, entry_ref=ent_ref.at[:, 0])
        for g in range(G):
            y = _dot(x_ref[g].reshape(tb, 2 * KB_STATES).astype(_BF), c_ref[g])
            y_scr[g] = y + d_ref[:, g * LANES:(g + 1) * LANES] * up_scr[g]
        _segments_to_rows(s_ref, y_scr, seg)

    return pl.pallas_call(
        body, name="s5_fwd", grid=(N_KB // G, nt),
        in_specs=[ucol, sp["b"], sp["c"], sp["tab"], sp["pw"], sp["d"]],
        out_specs=[ucol, sp["x"], sp["ent"]],
        out_shape=[jax.ShapeDtypeStruct((L, SSM_W), _F32),
                   jax.ShapeDtypeStruct((N_KB, L // SUBLANES, SUBLANES, 2 * KB_STATES), _F32),
                   jax.ShapeDtypeStruct((N_KB, nt, SUBLANES, 2 * KB_STATES), _F32)],
        scratch_shapes=[pltpu.VMEM((G, tb, LANES), _F32)] * 2 + [pltpu.VMEM((G, SUBLANES, 2 * KB_STATES), _F32)],
        compiler_params=_params("parallel", "arbitrary"),
    )(u, bmat, cmat, tab_f, pw_f, d_skip)


def _mixout_fwd(s, y_ret, x, w_glu, w_out, g2, tm):
    L = s.shape[0]

    def body(s_ref, yr_ref, x_ref, wg_ref, wo_ref, g_ref, ys_ref, glu_ref, cat_ref, mix_ref, x2_ref):
        ys = _gelu(s_ref[...]).astype(_BF)
        ys_ref[...] = ys
        glu = _dot(ys, wg_ref[...])
        glu_ref[...] = glu
        cat_ref[:, :RET_W] = yr_ref[...]
        cat_ref[:, RET_W:] = (glu[:, :SSM_W] * _sigmoid(glu[:, SSM_W:])).astype(_BF)
        mix = _dot(cat_ref[...], wo_ref[...])
        mix_ref[...] = mix
        x2_ref[...] = x_ref[...] + mix * _rms_r(mix) * g_ref[...]

    return pl.pallas_call(
        body, name="mixout_fwd", grid=(L // tm,),
        in_specs=[_row_spec(tm, SSM_W), _row_spec(tm, RET_W), _row_spec(tm, D_MODEL),
                  _weight_spec((SSM_W, 2 * SSM_W)), _weight_spec((D_MODEL, D_MODEL)), _full_spec((1, D_MODEL))],
        out_specs=[_row_spec(tm, SSM_W), _row_spec(tm, 2 * SSM_W), _row_spec(tm, D_MODEL),
                   _row_spec(tm, D_MODEL), _row_spec(tm, D_MODEL)],
        out_shape=[jax.ShapeDtypeStruct((L, SSM_W), _BF), jax.ShapeDtypeStruct((L, 2 * SSM_W), _F32),
                   jax.ShapeDtypeStruct((L, D_MODEL), _BF), jax.ShapeDtypeStruct((L, D_MODEL), _F32),
                   jax.ShapeDtypeStruct((L, D_MODEL), _F32)],
        compiler_params=_params("parallel"),
    )(s, y_ret, x, w_glu, w_out, g2)


FF1_COLS = D_FF // N_DEV


def _ff1_fwd(x2, g3, w1, tm):
    L = x2.shape[0]

    def body(x_ref, g_ref, w_ref, h_ref, f_ref):
        xv = x_ref[...]
        h = (xv * _rms_r(xv) * g_ref[...]).astype(_BF)
        h_ref[...] = h
        for j in range(N_DEV):
            f_ref[:, j * FF1_COLS:(j + 1) * FF1_COLS] = _dot(h, w_ref[j])

    return pl.pallas_call(
        body, name="ff1_fwd", grid=(L // tm,),
        in_specs=[_row_spec(tm, D_MODEL), _full_spec((1, D_MODEL)), _weight_spec((N_DEV, D_MODEL, FF1_COLS))],
        out_specs=[_row_spec(tm, D_MODEL), _row_spec(tm, D_FF)],
        out_shape=[jax.ShapeDtypeStruct((L, D_MODEL), _BF), jax.ShapeDtypeStruct((L, D_FF), _F32)],
        compiler_params=_params("parallel"),
    )(x2, g3, w1)


def _ff2_loss(f1, x2, tgt, g4, w2, tm):
    L = f1.shape[0]

    def body(f_ref, x_ref, t_ref, g_ref, w_ref, dy_ref, dm_ref, dg_ref, ls_ref):
        @pl.when(pl.program_id(0) == 0)
        def _():
            dg_ref[...] = jnp.zeros_like(dg_ref)
            ls_ref[...] = jnp.zeros_like(ls_ref)

        g = g_ref[...]
        for rows in _row_chunks(tm):
            rl = jnp.maximum(f_ref[rows, :], 0.0)
            m = _dot((rl * rl).astype(_BF), w_ref[...])
            y = x_ref[rows, :] + m * _rms_r(m) * g
            err = y - t_ref[rows, :]
            ls_ref[...] += jnp.sum(err * err, axis=0, keepdims=True)
            dy = err * (1.0 / D_MODEL)
            dy_ref[rows, :] = dy
            dm, dgr = _rms_bwd(m, g, dy)
            dm_ref[rows, :] = dm.astype(_BF)
            dg_ref[...] += jnp.sum(dgr, axis=0, keepdims=True)

    return pl.pallas_call(
        body, name="ff2_loss", grid=(L // tm,),
        in_specs=[_row_spec(tm, D_FF), _row_spec(tm, D_MODEL), _row_spec(tm, D_MODEL),
                  _full_spec((1, D_MODEL)), _weight_spec((D_FF, D_MODEL))],
        out_specs=[_row_spec(tm, D_MODEL), _row_spec(tm, D_MODEL), _full_spec((1, D_MODEL)), _full_spec((1, D_MODEL))],
        out_shape=[jax.ShapeDtypeStruct((L, D_MODEL), _F32), jax.ShapeDtypeStruct((L, D_MODEL), _BF),
                   jax.ShapeDtypeStruct((1, D_MODEL), _F32), jax.ShapeDtypeStruct((1, D_MODEL), _F32)],
        compiler_params=_params("arbitrary"),
    )(f1, x2, tgt, g4, w2)


def _ff2_bwd(dm, f1, w2, tm, tn):
    L = dm.shape[0]
    last = L // tm - 1

    def body(dm_ref, f_ref, w_ref, df_ref, dw_ref, acc):
        @pl.when(pl.program_id(1) == 0)
        def _():
            acc[...] = jnp.zeros_like(acc)

        dmv = dm_ref[...]
        rl = jnp.maximum(f_ref[...], 0.0)
        df_ref[...] = (_dot_nt(dmv, w_ref[...]) * (2.0 * rl)).astype(_BF)
        acc[...] += _dot_tn((rl * rl).astype(_BF), dmv)

        @pl.when(pl.program_id(1) == last)
        def _():
            dw_ref[...] = acc[...].astype(_BF)

    return pl.pallas_call(
        body, name="ff2_bwd", grid=(D_FF // tn, L // tm),
        in_specs=[pl.BlockSpec((tm, D_MODEL), lambda j, i: (i, 0)), pl.BlockSpec((tm, tn), lambda j, i: (i, j)),
                  pl.BlockSpec((tn, D_MODEL), lambda j, i: (j, 0))],
        out_specs=[pl.BlockSpec((tm, tn), lambda j, i: (i, j)), pl.BlockSpec((tn, D_MODEL), lambda j, i: (j, 0))],
        out_shape=[jax.ShapeDtypeStruct((L, D_FF), _BF), jax.ShapeDtypeStruct((D_FF, D_MODEL), _BF)],
        scratch_shapes=[pltpu.VMEM((tn, D_MODEL), _F32)],
        compiler_params=_params("parallel", "arbitrary"),
    )(dm, f1, w2)


def _ff1_bwd(df1, w1, x2, mix, dy, g3, g2, tm):
    L = df1.shape[0]

    def body(df_ref, w_ref, x2_ref, mix_ref, dy_ref, g3_ref, g2_ref, dx2_ref, dmix_ref, dg3_ref, dg2_ref):
        @pl.when(pl.program_id(0) == 0)
        def _():
            dg3_ref[...] = jnp.zeros_like(dg3_ref)
            dg2_ref[...] = jnp.zeros_like(dg2_ref)

        for rows in _row_chunks(tm):
            dh = _dot_nt(df_ref[rows, 0:FF1_COLS], w_ref[0])
            for j in range(1, N_DEV):
                dh = dh + _dot_nt(df_ref[rows, j * FF1_COLS:(j + 1) * FF1_COLS], w_ref[j])
            dz, dgr = _rms_bwd(x2_ref[rows, :], g3_ref[...], dh)
            dg3_ref[...] += jnp.sum(dgr, axis=0, keepdims=True)
            dx2 = dy_ref[rows, :] + dz
            dx2_ref[rows, :] = dx2
            dmx, dgr2 = _rms_bwd(mix_ref[rows, :], g2_ref[...], dx2)
            dg2_ref[...] += jnp.sum(dgr2, axis=0, keepdims=True)
            dmix_ref[rows, :] = dmx.astype(_BF)

    vec = _full_spec((1, D_MODEL))
    return pl.pallas_call(
        body, name="ff1_bwd", grid=(L // tm,),
        in_specs=[_row_spec(tm, D_FF), _weight_spec((N_DEV, D_MODEL, FF1_COLS)), _row_spec(tm, D_MODEL),
                  _row_spec(tm, D_MODEL), _row_spec(tm, D_MODEL), vec, vec],
        out_specs=[_row_spec(tm, D_MODEL), _row_spec(tm, D_MODEL), vec, vec],
        out_shape=[jax.ShapeDtypeStruct((L, D_MODEL), _F32), jax.ShapeDtypeStruct((L, D_MODEL), _BF),
                   jax.ShapeDtypeStruct((1, D_MODEL), _F32), jax.ShapeDtypeStruct((1, D_MODEL), _F32)],
        compiler_params=_params("arbitrary"),
    )(df1, w1, x2, mix, dy, g3, g2)


def _matmul_tn(a, b, tm, tn, name, slots=False):
    L, K = a.shape
    N = b.shape[1]
    last = L // tm - 1

    def body(a_ref, b_ref, o_ref, acc):
        @pl.when(pl.program_id(1) == 0)
        def _():
            acc[...] = jnp.zeros_like(acc)

        acc[...] += _dot_tn(a_ref[...].astype(_BF), b_ref[...].astype(_BF))

        @pl.when(pl.program_id(1) == last)
        def _():
            if slots:
                o_ref[0] = acc[...].astype(_BF)
            else:
                o_ref[...] = acc[...].astype(_BF)

    if slots:
        out_spec = pl.BlockSpec((1, K, tn), lambda j, i: (j, 0, 0))
        out_shape = jax.ShapeDtypeStruct((N // tn, K, tn), _BF)
    else:
        out_spec = pl.BlockSpec((K, tn), lambda j, i: (0, j))
        out_shape = jax.ShapeDtypeStruct((K, N), _BF)
    return pl.pallas_call(
        body, name=name, grid=(N // tn, L // tm),
        in_specs=[pl.BlockSpec((tm, K), lambda j, i: (i, 0)), pl.BlockSpec((tm, tn), lambda j, i: (i, j))],
        out_specs=out_spec, out_shape=out_shape,
        scratch_shapes=[pltpu.VMEM((K, tn), _F32)],
        compiler_params=_params("parallel", "arbitrary"),
    )(a, b)


def _mixout_bwd(dmix, w_out, w_glu, glu, s, o, gate, ggn, tm):
    L = dmix.shape[0]

    def body(dmix_ref, wo_ref, wg_ref, glu_ref, s_ref, o_ref, gate_ref, ggn_ref,
             dglu_ref, ds_ref, dgate_ref, do_ref, dggn_ref):
        @pl.when(pl.program_id(0) == 0)
        def _():
            dggn_ref[...] = jnp.zeros_like(dggn_ref)

        dcat = _dot_nt(dmix_ref[...], wo_ref[...])
        dy_ret, dy_ssm = dcat[:, :RET_W], dcat[:, RET_W:]
        glu = glu_ref[...]
        ga, sg = glu[:, :SSM_W], _sigmoid(glu[:, SSM_W:])
        dga = (dy_ssm * sg).astype(_BF)
        dgb = (dy_ssm * ga * sg * (1.0 - sg)).astype(_BF)
        dglu_ref[:, :SSM_W] = dga
        dglu_ref[:, SSM_W:] = dgb
        dys = _dot_nt(dga, wg_ref[:, :SSM_W]) + _dot_nt(dgb, wg_ref[:, SSM_W:])
        ds_ref[...] = dys * _gelu_grad(s_ref[...])
        gt = gate_ref[...]
        sgt = _sigmoid(gt)
        ggn = ggn_ref[...]
        for hh in range(N_HEAD):
            cols = slice(hh * HEAD_D, (hh + 1) * HEAD_D)
            ov = o_ref[:, cols]
            dlt = ov - jnp.mean(ov, axis=-1, keepdims=True)
            rstd = lax.rsqrt(jnp.mean(dlt * dlt, axis=-1, keepdims=True) + NORM_EPS)
            on = dlt * rstd
            dyr = dy_ret[:, cols] * (gt[:, cols] * sgt[:, cols])
            dgate_ref[:, cols] = dy_ret[:, cols] * (on * ggn[:, cols]) * (sgt[:, cols] * (1.0 + gt[:, cols] * (1.0 - sgt[:, cols])))
            dggn_ref[:, cols] += jnp.sum(dyr * on, axis=0, keepdims=True)
            don = dyr * ggn[:, cols]
            do = rstd * (don - jnp.mean(don, axis=-1, keepdims=True) - on * jnp.mean(don * on, axis=-1, keepdims=True))
            do_ref[:, cols] = do.astype(_BF)

    return pl.pallas_call(
        body, name="mixout_bwd", grid=(L // tm,),
        in_specs=[_row_spec(tm, D_MODEL), _weight_spec((D_MODEL, D_MODEL)), _weight_spec((SSM_W, 2 * SSM_W)),
                  _row_spec(tm, 2 * SSM_W), _row_spec(tm, SSM_W), _row_spec(tm, RET_W), _row_spec(tm, RET_W),
                  _full_spec((1, RET_W))],
        out_specs=[_row_spec(tm, 2 * SSM_W), _row_spec(tm, SSM_W), _row_spec(tm, RET_W), _row_spec(tm, RET_W),
                   _full_spec((1, RET_W))],
        out_shape=[jax.ShapeDtypeStruct((L, 2 * SSM_W), _BF), jax.ShapeDtypeStruct((L, SSM_W), _F32),
                   jax.ShapeDtypeStruct((L, RET_W), _F32), jax.ShapeDtypeStruct((L, RET_W), _BF),
                   jax.ShapeDtypeStruct((1, RET_W), _F32)],
        compiler_params=_params("arbitrary"),
    )(dmix, w_out, w_glu, glu, s, o, gate, ggn)


def _s5_bwd(u, ds, xs, ent, bmat, cmat, tab_r, pw_r, d_skip, tb):
    L = u.shape[0]
    nt = L // tb
    seg = tb // SUBLANES
    G = KB_PER_STEP
    rcol = pl.BlockSpec((tb, G * LANES), lambda kb, t: (nt - 1 - t, kb))
    sp = _s5_specs(seg, time=lambda t: nt - 1 - t)
    aspec = pl.BlockSpec((G, SUBLANES, 2 * KB_STATES), lambda kb, t: (kb, 0, 0))

    def body(u_ref, ds_ref, x_ref, ent_ref, b_ref, c_ref, tr_ref, pr_ref, d_ref,
             du_ref, db_ref, dc_ref, da_ref, dd_ref, up_scr, dp_scr, g_scr, lc_scr):
        @pl.when(pl.program_id(1) == 0)
        def _():
            lc_scr[...] = jnp.zeros_like(lc_scr)
            db_ref[...] = jnp.zeros_like(db_ref)
            dc_ref[...] = jnp.zeros_like(dc_ref)
            da_ref[...] = jnp.zeros_like(da_ref)
            dd_ref[...] = jnp.zeros_like(dd_ref)

        _rows_to_segments(up_scr, u_ref, seg)
        _rows_to_segments(dp_scr, ds_ref, seg)
        for g in range(G):
            g_scr[g] = _dot_nt(dp_scr[g].astype(_BF), c_ref[g]).reshape(seg, SUBLANES, 2 * KB_STATES)
        _scan_segments(g_scr, tr_ref, pr_ref, lc_scr, seg, reverse=True, fwd_ref=x_ref, fwd_entry_ref=ent_ref.at[:, 0],
                       da_ref=da_ref)
        for g in range(G):
            cols = slice(g * LANES, (g + 1) * LANES)
            uv, dsv = up_scr[g], dp_scr[g]
            ub, dsb = uv.astype(_BF), dsv.astype(_BF)
            lamb = g_scr[g].reshape(tb, 2 * KB_STATES).astype(_BF)
            db_ref[g] += _dot_tn(ub, lamb)
            dc_ref[g] += _dot_tn(x_ref[g].reshape(tb, 2 * KB_STATES).astype(_BF), dsb)
            dd_ref[:, cols] += jnp.sum(dsv * uv, axis=0, keepdims=True)
            up_scr[g] = _dot_nt(lamb, b_ref[g]) + d_ref[:, cols] * dsv
        _segments_to_rows(du_ref, up_scr, seg)

    return pl.pallas_call(
        body, name="s5_bwd", grid=(N_KB // G, nt),
        in_specs=[rcol, rcol, sp["x"], sp["ent"], sp["b"], sp["c"], sp["tab"], sp["pw"], sp["d"]],
        out_specs=[rcol, sp["b"], sp["c"], aspec, sp["d"]],
        out_shape=[jax.ShapeDtypeStruct((L, SSM_W), _F32),
                   jax.ShapeDtypeStruct((N_KB, LANES, 2 * KB_STATES), _F32),
                   jax.ShapeDtypeStruct((N_KB, 2 * KB_STATES, LANES), _F32),
                   jax.ShapeDtypeStruct((N_KB, SUBLANES, 2 * KB_STATES), _F32),
                   jax.ShapeDtypeStruct((1, SSM_W), _F32)],
        scratch_shapes=[pltpu.VMEM((G, tb, LANES), _F32)] * 2
        + [pltpu.VMEM((G, seg, SUBLANES, 2 * KB_STATES), _F32), pltpu.VMEM((G, SUBLANES, 2 * KB_STATES), _F32)],
        compiler_params=_params("parallel", "arbitrary"),
    )(u, ds, xs, ent, bmat, cmat, tab_r, pw_r, d_skip)


def _retention_bwd(q, k, v, do, r_prev, consts, cosf, sinf):
    L = q.shape[0]
    nc = L // CHUNK
    cps = math.gcd(RET_STEP_CHUNKS, nc)
    nb = nc // cps
    blk = pl.BlockSpec((cps * CHUNK, RET_W), lambda n: (nb - 1 - n, 0))
    rope_blk = pl.BlockSpec((cps * CHUNK, HEAD_D), lambda n: (nb - 1 - n, 0))

    def body(q_ref, k_ref, v_ref, do_ref, rp_ref, dm_ref, xi_ref, zeta_ref, gc_ref, cos_ref, sin_ref,
             dq_ref, dk_ref, dv_ref, g_scr):
        @pl.when(pl.program_id(0) == 0)
        def _():
            g_scr[...] = jnp.zeros_like(g_scr)

        for hh in range(N_HEAD):
            cols = slice(hh * HEAD_D, (hh + 1) * HEAD_D)
            dm, zeta = dm_ref[hh], zeta_ref[hh]
            gst = g_scr[hh]
            for c in reversed(range(cps)):
                rows = slice(c * CHUNK, (c + 1) * CHUNK)
                qv, kv, vv, dov = q_ref[rows, cols], k_ref[rows, cols], v_ref[rows, cols], do_ref[rows, cols]
                rb = rp_ref[hh, c].astype(_BF)
                gb = gst.astype(_BF)
                sb = (_dot_nt(qv, kv) * dm).astype(_BF)
                dab = (_dot_nt(dov, vv) * dm).astype(_BF)
                dox = (dov.astype(_F32) * xi_ref[hh]).astype(_BF)
                vz = (vv.astype(_F32) * zeta).astype(_BF)
                dq = _dot(dab, kv) + _dot_nt(dox, rb)
                dk = _dot_tn(dab, qv) + _dot_nt(vz, gb)
                dv = _dot_tn(sb, dov) + _dot(kv, gb) * zeta
                gst = gc_ref[hh, 0:1, :] * gst + _dot_tn(qv, dox)
                cs, sn = cos_ref[rows, :], sin_ref[rows, :]
                dq_ref[rows, cols] = _rope_t(dq, cs, sn).astype(_BF)
                dk_ref[rows, cols] = (_rope_t(dk, cs, sn) * (HEAD_D ** -0.5)).astype(_BF)
                dv_ref[rows, cols] = dv.astype(_BF)
            g_scr[hh] = gst

    return pl.pallas_call(
        body, name="retention_bwd", grid=(nb,),
        in_specs=[blk, blk, blk, blk, pl.BlockSpec((N_HEAD, cps, HEAD_D, HEAD_D), lambda n: (0, nb - 1 - n, 0, 0))]
        + _head_specs() + [rope_blk, rope_blk],
        out_specs=[blk, blk, blk],
        out_shape=[jax.ShapeDtypeStruct((L, RET_W), _BF)] * 3,
        scratch_shapes=[pltpu.VMEM((N_HEAD, HEAD_D, HEAD_D), _F32)],
        compiler_params=_params("arbitrary"),
    )(q, k, v, do, r_prev, *consts, cosf, sinf)


def _inproj_bwd(pieces, w_in, x, dx2, g1, tm):
    L = x.shape[0]

    def body(p0, p1, p2, p3, p4, w_ref, x_ref, dx2_ref, g_ref, dx_ref, dg_ref):
        @pl.when(pl.program_id(0) == 0)
        def _():
            dg_ref[...] = jnp.zeros_like(dg_ref)

        dh = None
        for j, p in enumerate((p0, p1, p2, p3, p4)):
            part = _dot_nt(p[...].astype(_BF), w_ref[:, j * RET_W:(j + 1) * RET_W])
            dh = part if dh is None else dh + part
        dz, dgr = _rms_bwd(x_ref[...], g_ref[...], dh)
        dx_ref[...] = dx2_ref[...] + dz
        dg_ref[...] += jnp.sum(dgr, axis=0, keepdims=True)

    return pl.pallas_call(
        body, name="inproj_bwd", grid=(L // tm,),
        in_specs=[_row_spec(tm, RET_W)] * 5 + [_weight_spec((D_MODEL, IN_COLS)), _row_spec(tm, D_MODEL),
                                                 _row_spec(tm, D_MODEL), _full_spec((1, D_MODEL))],
        out_specs=[_row_spec(tm, D_MODEL), _full_spec((1, D_MODEL))],
        out_shape=[jax.ShapeDtypeStruct((L, D_MODEL), _F32), jax.ShapeDtypeStruct((1, D_MODEL), _F32)],
        compiler_params=_params("arbitrary"),
    )(*pieces, w_in, x, dx2, g1)


def _sum_adamw(parts, w, m, v, tr, name):
    _, R, Cc = parts.shape

    def body(p_ref, w_ref, m_ref, v_ref, g_ref, d_ref, nm_ref, nv_ref):
        gv = p_ref[0].astype(_F32)
        for s in range(1, N_DEV):
            gv = gv + p_ref[s].astype(_F32)
        g_ref[...] = gv
        nm = ADAM_B1 * m_ref[...] + (1.0 - ADAM_B1) * gv
        nv = ADAM_B2 * v_ref[...] + (1.0 - ADAM_B2) * (gv * gv)
        m_hat = nm / (1.0 - ADAM_B1 ** ADAM_STEP)
        v_hat = nv / (1.0 - ADAM_B2 ** ADAM_STEP)
        d_ref[...] = -ADAM_LR * (m_hat / (jnp.sqrt(v_hat) + ADAM_EPS) + ADAM_WD * w_ref[...])
        nm_ref[...] = nm
        nv_ref[...] = nv

    spec = _row_spec(tr, Cc)
    return pl.pallas_call(
        body, name=name, grid=(R // tr,),
        in_specs=[pl.BlockSpec((N_DEV, tr, Cc), lambda i: (0, i, 0))] + [spec] * 3, out_specs=[spec] * 4,
        out_shape=[jax.ShapeDtypeStruct((R, Cc), _F32)] * 4,
        compiler_params=_params("parallel"),
    )(parts, w, m, v)


def _my_place():
    return lax.axis_index("x"), lax.axis_index("y"), lax.axis_index("c")


def _all_gather(blocks):
    n = len(blocks)

    def body(*refs):
        x_refs, out_refs, done_ref = refs[:n], refs[n:2 * n], refs[2 * n]
        send_sems, recv_sems, local_sems = refs[2 * n + 1:]
        done_ref[...] = jnp.zeros_like(done_ref)
        x, y, c = _my_place()
        me, sibling = (x, y, c), (x, y, 1 - c)
        chips = [(1 - x, y), (x, 1 - y), (1 - x, 1 - y)]

        def slot(a, px, py, pc):
            return out_refs[a].at[4 * px + 2 * py + pc]

        def copy(a, k, blk, to, own=False):
            return pltpu.make_async_remote_copy(
                src_ref=x_refs[a] if own else slot(a, *blk), dst_ref=slot(a, *blk),
                send_sem=send_sems.at[a, k], recv_sem=recv_sems.at[a, k], device_id=to, device_id_type=MESH)

        mine = [pltpu.make_async_copy(x_refs[a], slot(a, *me), local_sems.at[a]) for a in range(n)]
        for cp in mine:
            cp.start()
        first = []
        for a in range(n):
            first.append(copy(a, 0, me, sibling, own=True))
            first += [copy(a, 1 + j, me, (*chip, c), own=True) for j, chip in enumerate(chips)]
        for cp in first:
            cp.start()
        passed = []
        for j, chip in enumerate(chips):
            for a in range(n):
                copy(a, 1 + j, (*chip, c), me).wait_recv()
                fwd = copy(a, 4 + j, (*chip, c), sibling)
                fwd.start()
                passed.append(fwd)
        for a in range(n):
            copy(a, 0, sibling, me).wait_recv()
            for j, chip in enumerate(chips):
                copy(a, 4 + j, (*chip, 1 - c), me).wait_recv()
        for cp in first + passed:
            cp.wait_send()
        for cp in mine:
            cp.wait()

    any_spec = pl.BlockSpec(memory_space=pl.ANY)
    outs = pl.pallas_call(
        body, name="weights_all_gather",
        in_specs=[any_spec] * n, out_specs=[any_spec] * n + [pl.BlockSpec(memory_space=pltpu.VMEM)],
        out_shape=[jax.ShapeDtypeStruct((N_DEV,) + b.shape, b.dtype) for b in blocks]
        + [jax.ShapeDtypeStruct((SUBLANES, LANES), _F32)],
        scratch_shapes=[pltpu.SemaphoreType.DMA((n, 7)), pltpu.SemaphoreType.DMA((n, 7)), pltpu.SemaphoreType.DMA((n,))],
    )(*blocks)
    return outs[:n], outs[n]


def _exchange(bigs, small):
    n = len(bigs)
    r = small.shape[0]

    def body(*refs):
        in_refs, out_refs = refs[:n + 1], refs[n + 1:2 * n + 2]
        send_sems, recv_sems, local_sems = refs[2 * n + 2:]
        x, y, c = _my_place()
        me = 4 * x + 2 * y + c
        own = [pltpu.make_async_copy(in_refs[a].at[me], out_refs[a].at[me], local_sems.at[a]) for a in range(n)]
        own.append(pltpu.make_async_copy(in_refs[n], out_refs[n].at[me], local_sems.at[n]))
        for cp in own:
            cp.start()
        copies = []
        for kk in range(1, N_DEV):
            px, py, pc = x ^ (kk >> 2), y ^ ((kk >> 1) & 1), c ^ (kk & 1)
            peer = 4 * px + 2 * py + pc
            for a in range(n + 1):
                src = in_refs[a].at[peer] if a < n else in_refs[a]
                copies.append(pltpu.make_async_remote_copy(
                    src_ref=src, dst_ref=out_refs[a].at[me],
                    send_sem=send_sems.at[a, kk - 1], recv_sem=recv_sems.at[a, kk - 1],
                    device_id=(px, py, pc), device_id_type=MESH))
        for cp in copies:
            cp.start()
        for cp in copies:
            cp.wait_recv()
        for cp in copies:
            cp.wait_send()
        for cp in own:
            cp.wait()

    any_spec = pl.BlockSpec(memory_space=pl.ANY)
    outs = pl.pallas_call(
        body, name="grad_exchange",
        in_specs=[any_spec] * (n + 1), out_specs=[any_spec] * (n + 1),
        out_shape=[jax.ShapeDtypeStruct(b.shape, b.dtype) for b in bigs]
        + [jax.ShapeDtypeStruct((N_DEV, r, LANES), small.dtype)],
        scratch_shapes=[pltpu.SemaphoreType.DMA((n + 1, 7)), pltpu.SemaphoreType.DMA((n + 1, 7)),
                        pltpu.SemaphoreType.DMA((n + 1,))],
    )(*bigs, small)
    return outs[:n], outs[n]


HBM_SPEC = pl.BlockSpec(memory_space=pltpu.HBM)
SEM_SPEC = pl.BlockSpec(memory_space=pltpu.SEMAPHORE)
DATAFLOW = pltpu.SideEffectType.DATAFLOW_SIDE_EFFECTING


def _my_index():
    x, y, c = _my_place()
    return 4 * x + 2 * y + c


def _landing(own_block):
    zone = lax.empty((N_DEV,) + own_block.shape, own_block.dtype)
    return lax.dynamic_update_index_in_dim(zone, own_block, _my_index(), 0)


def _split_copies(src_refs, land_refs, send_sems, recv_sems, gather):
    x, y, c = _my_place()
    me = 4 * x + 2 * y + c
    copies = []
    for kk in range(1, N_DEV):
        px, py, pc = x ^ (kk >> 2), y ^ ((kk >> 1) & 1), c ^ (kk & 1)
        peer = 4 * px + 2 * py + pc
        for a, (src, land) in enumerate(zip(src_refs, land_refs)):
            copies.append(pltpu.make_async_remote_copy(
                src_ref=src if gather else src.at[peer], dst_ref=land.at[me],
                send_sem=send_sems.at[a * 7 + kk - 1], recv_sem=recv_sems.at[a * 7 + kk - 1],
                device_id=(px, py, pc), device_id_type=MESH))
    return copies


def _split_start(srcs, lands, gather, name):
    n = len(srcs)

    def body(*refs):
        src_refs, land_refs = refs[:n], refs[n:2 * n]
        send_sems, recv_sems = refs[2 * n], refs[2 * n + 1]
        token = refs[-1]
        for cp in _split_copies(src_refs, land_refs, send_sems, recv_sems, gather):
            cp.start()
        token[...] = jnp.zeros_like(token)

    outs = pl.pallas_call(
        body, name=name,
        out_shape=(pltpu.SemaphoreType.DMA((7 * n,)), pltpu.SemaphoreType.DMA((7 * n,)),
                   *[pltpu.HBM(t.shape, t.dtype) for t in srcs], *[pltpu.HBM(t.shape, t.dtype) for t in lands],
                   jax.ShapeDtypeStruct((SUBLANES, LANES), _F32)),
        in_specs=[HBM_SPEC] * (2 * n),
        out_specs=(SEM_SPEC, SEM_SPEC, *[HBM_SPEC] * (2 * n), pl.BlockSpec(memory_space=pltpu.VMEM)),
        input_output_aliases={i: 2 + i for i in range(2 * n)},
        compiler_params=pltpu.CompilerParams(has_side_effects=DATAFLOW),
    )(*[pltpu.with_memory_space_constraint(t, pltpu.HBM) for t in list(srcs) + list(lands)])
    return outs[0], outs[1], outs[2:2 + n], outs[2 + n:2 + 2 * n], outs[-1]


def _split_wait(send_sems, recv_sems, srcs, lands, after, gather, name):
    n = len(srcs)

    def body(*refs):
        src_refs, land_refs = refs[:n], refs[n:2 * n]
        send_s, recv_s = refs[2 * n], refs[2 * n + 1]
        for cp in _split_copies(src_refs, land_refs, send_s, recv_s, gather):
            cp.wait_send()
            cp.wait_recv()

    outs = pl.pallas_call(
        body, name=name,
        out_shape=tuple(pltpu.HBM(t.shape, t.dtype) for t in list(srcs) + list(lands)),
        in_specs=[HBM_SPEC] * (2 * n) + [SEM_SPEC, SEM_SPEC, pl.BlockSpec(memory_space=pl.ANY)],
        out_specs=tuple([HBM_SPEC] * (2 * n)),
        input_output_aliases={i: i for i in range(2 * n)},
        compiler_params=pltpu.CompilerParams(has_side_effects=DATAFLOW),
    )(*srcs, *lands, send_sems, recv_sems, after)
    return outs[n:]


def _discretize(lam_re, lam_im, log_dt, b_re, b_im):
    lr = jnp.minimum(lam_re, -1e-4)
    li = lam_im
    dt = jnp.exp(log_dt)[:, None]
    er = jnp.exp(lr * dt)
    ar, ai = er * jnp.cos(li * dt), er * jnp.sin(li * dt)
    den = lr * lr + li * li
    cr = ((ar - 1.0) * lr + ai * li) / den
    ci = (ai * lr - (ar - 1.0) * li) / den
    bbr = cr[:, :, None] * b_re - ci[:, :, None] * b_im
    bbi = cr[:, :, None] * b_im + ci[:, :, None] * b_re
    return ar, ai, bbr, bbi


def _cmul(ar, ai, br, bi):
    return ar * br - ai * bi, ar * bi + ai * br


def _cpowers(ar, ai, n):
    pr, pi = ar[None], ai[None]
    while pr.shape[0] < n:
        nr, ni = _cmul(pr, pi, pr[-1][None], pi[-1][None])
        pr, pi = jnp.concatenate([pr, nr]), jnp.concatenate([pi, ni])
    return pr[:n], pi[:n]


def _scan_tables(ar, ai, seg, reverse):
    if reverse:
        ai = -ai
    ar, ai = ar.reshape(N_KB, KB_STATES), ai.reshape(N_KB, KB_STATES)
    pr, pi = _cpowers(ar, ai, seg)
    a1 = (pr[-1], pi[-1])
    a2 = _cmul(*a1, *a1)
    a4 = _cmul(*a2, *a2)
    row = jnp.arange(SUBLANES)[None, :, None]
    wide = lambda t: jnp.broadcast_to(t[:, None, :], (N_KB, SUBLANES, KB_STATES))
    tabs = [wide(ar), wide(ai)]
    for dist, (qr, qi) in ((1, a1), (2, a2), (4, a4)):
        keep = (row < SUBLANES - dist) if reverse else (row >= dist)
        tabs += [jnp.where(keep, wide(qr), 0.0), jnp.where(keep, wide(qi), 0.0)]
    tabs += [wide(a1[0]), wide(a1[1])]
    if reverse:
        pr, pi = pr[::-1], pi[::-1]
    pw = jnp.transpose(jnp.concatenate([pr, pi], axis=-1), (1, 0, 2))[:, :, None, :]
    return jnp.stack(tabs, axis=1).astype(_F32), pw.astype(_F32)


def _block_diag_in(br, bi):
    eye = jnp.eye(GROUPS_PER_KB, dtype=_F32)
    one = lambda t: jnp.einsum("kgpc,gh->kgchp", t.reshape(N_KB, GROUPS_PER_KB, N_STATE, SSM_GC), eye).reshape(
        N_KB, LANES, KB_STATES)
    return jnp.concatenate([one(br), one(bi)], axis=-1)


def _block_diag_in_t(dmat):
    d6 = dmat.reshape(N_KB, GROUPS_PER_KB, SSM_GC, 2, GROUPS_PER_KB, N_STATE)
    eye = jnp.eye(GROUPS_PER_KB, dtype=_F32)
    both = jnp.einsum("kgcrhp,gh->rkgpc", d6, eye).reshape(2, N_GROUP, N_STATE, SSM_GC)
    return both[0], both[1]


def _block_diag_out(c_re, c_im):
    eye = jnp.eye(GROUPS_PER_KB, dtype=_F32)
    one = lambda t: jnp.einsum("kgcp,gh->khpgc", t.reshape(N_KB, GROUPS_PER_KB, SSM_GC, N_STATE), eye).reshape(
        N_KB, KB_STATES, LANES)
    return jnp.concatenate([one(c_re), -one(c_im)], axis=1)


def _block_diag_out_t(dmat):
    d6 = dmat.reshape(N_KB, 2, GROUPS_PER_KB, N_STATE, GROUPS_PER_KB, SSM_GC)
    eye = jnp.eye(GROUPS_PER_KB, dtype=_F32)
    both = jnp.einsum("krhpgc,gh->rkgcp", d6, eye).reshape(2, N_GROUP, SSM_GC, N_STATE)
    return both[0], -both[1]


SMALL_NAMES = ("norm_mix_pre", "norm_mix_post", "ret_gn_gain", "ssm_lambda_re", "ssm_lambda_im", "ssm_log_dt",
               "ssm_b_re", "ssm_b_im", "ssm_c_re", "ssm_c_im", "ssm_d", "norm_mlp_pre", "norm_mlp_post")


def _local_grads(x, tgt, small, weights, emit, tm, tk, tb, zero=0.0):
    L = x.shape[0]
    g1, g2, ggn = small["norm_mix_pre"], small["norm_mix_post"], small["ret_gn_gain"]
    g3, g4, d_skip = small["norm_mlp_pre"], small["norm_mlp_post"], small["ssm_d"]

    half = HEAD_D // 2
    inv_freq = ROPE_BASE ** (-jnp.arange(half, dtype=_F32) / half)
    ang = jnp.arange(L, dtype=_F32)[:, None] * inv_freq[None, :] + zero
    cosf = jnp.concatenate([jnp.cos(ang), jnp.cos(ang)], axis=-1)
    sinf = jnp.concatenate([-jnp.sin(ang), jnp.sin(ang)], axis=-1)
    consts = _ret_consts()

    disc_in = (small["ssm_lambda_re"][0], small["ssm_lambda_im"][0], small["ssm_log_dt"][0],
               small["ssm_b_re"][0], small["ssm_b_im"][0])
    (ar, ai, bbr, bbi), disc_vjp = jax.vjp(_discretize, *disc_in)
    bmat = _block_diag_in(bbr, bbi).astype(_BF)
    cmat = _block_diag_out(small["ssm_c_re"][0], small["ssm_c_im"][0]).astype(_BF)
    seg = tb // SUBLANES
    tab_f, pw_f = _scan_tables(ar, ai, seg, False)
    tab_r, pw_r = _scan_tables(ar, ai, seg, True)

    (w_in,) = weights("in", sinf)
    h1, q, k, v, gate, u = _inproj_fwd(x, g1, w_in, cosf, sinf, tm)
    o, y_ret, r_prev = _retention_fwd(q, k, v, gate, ggn, consts)
    s, xs, ent = _s5_fwd(u, bmat, cmat, tab_f, pw_f, d_skip, tb)
    w_glu, w_out = weights("mix", s)
    ys, glu, cat, mix, x2 = _mixout_fwd(s, y_ret, x, w_glu, w_out, g2, tm)
    w_ff1, w_ff2 = weights("mlp", x2)
    h3, f1 = _ff1_fwd(x2, g3, w_ff1, tm)
    dy, dm, dg4, sq = _ff2_loss(f1, x2, tgt, g4, w_ff2, min(2 * tm, L))

    df1, dw_ff2 = _ff2_bwd(dm, f1, w_ff2, min(1024, L), 1024)
    dx2, dmix, dg3, dg2 = _ff1_bwd(df1, w_ff1, x2, mix, dy, g3, g2, min(2 * tm, L))
    dw_ff1 = _matmul_tn(h3, df1, tk, FF1_COLS, "dw_ff1", slots=True)
    zero = emit({"w_ff1": dw_ff1, "w_ff2": dw_ff2})
    dglu, ds, dgate, do, dggn = _mixout_bwd(dmix, w_out, w_glu, glu, s, o, gate, ggn if zero is None else ggn + zero, tm)
    dw_out = _matmul_tn(cat, dmix, tk, 1024, "dw_out")
    dw_glu = _matmul_tn(ys, dglu, tk, 1024, "dw_glu")
    zero = emit({"w_glu": dw_glu, "w_out": dw_out})
    du, dbmat, dcmat, da8, dd = _s5_bwd(u, ds, xs, ent, bmat, cmat, tab_r, pw_r,
                                        d_skip if zero is None else d_skip + zero, tb)
    dq, dk, dv = _retention_bwd(q, k, v, do, r_prev, consts, cosf, sinf)
    pieces = (dq, dk, dv, dgate, du)
    dw_in = jnp.concatenate([_matmul_tn(h1, p, tk, RET_W, "dw_in_%d" % j) for j, p in enumerate(pieces)], axis=1)
    zero = emit({"w_in": dw_in})
    gx, dg1 = _inproj_bwd(pieces, w_in, x, dx2, g1 if zero is None else g1 + zero, tm)

    da = jnp.sum(da8, axis=1)
    dar = da[:, :KB_STATES].reshape(N_GROUP, N_STATE)
    dai = da[:, KB_STATES:].reshape(N_GROUP, N_STATE)
    dbr, dbi = _block_diag_in_t(dbmat)
    dlre, dlim, dldt, dbre, dbim = disc_vjp((dar, dai, dbr, dbi))
    dcre, dcim = _block_diag_out_t(dcmat)

    gsmall = {
        "norm_mix_pre": dg1, "norm_mix_post": dg2, "ret_gn_gain": dggn,
        "ssm_lambda_re": dlre[None], "ssm_lambda_im": dlim[None], "ssm_log_dt": dldt[None],
        "ssm_b_re": dbre[None], "ssm_b_im": dbim[None], "ssm_c_re": dcre[None], "ssm_c_im": dcim[None],
        "ssm_d": dd, "norm_mlp_pre": dg3, "norm_mlp_post": dg4,
    }
    return sq, gx, gsmall


BIG_SHAPES = {"w_in": (D_MODEL, IN_COLS // N_DEV), "w_glu": (SSM_W, 2 * SSM_W // N_DEV), "w_out": (D_MODEL // N_DEV, D_MODEL),
              "w_ff1": (D_MODEL, FF1_COLS), "w_ff2": (D_FF // N_DEV, D_MODEL)}
BIG_NAMES = ("w_in", "w_glu", "w_out", "w_ff1", "w_ff2")


def _cols_from_slots(g):
    return jnp.transpose(g, (1, 0, 2)).reshape(g.shape[1], N_DEV * g.shape[2])


def _cols_to_slots(dw):
    r, cols = dw.shape
    return jnp.transpose(dw.reshape(r, N_DEV, cols // N_DEV), (1, 0, 2))


WEIGHT_GROUPS = {"in": ("w_in",), "mix": ("w_glu", "w_out"), "mlp": ("w_ff1", "w_ff2")}


def _weight_from_slots(name, g):
    if name in ("w_in", "w_glu"):
        return _cols_from_slots(g)
    if name == "w_ff1":
        return g
    return g.reshape(N_DEV * g.shape[1], g.shape[2])


def _grad_slots(name, dw):
    if name in ("w_in", "w_glu"):
        return _cols_to_slots(dw)
    if name == "w_ff1":
        return dw
    return dw.reshape((N_DEV,) + BIG_SHAPES[name])


PIECE_ROWS = 8


def _small_layout(shapes):
    off, rows = {}, 0
    for n in SMALL_NAMES:
        off[n] = rows
        rows += -(-math.prod(shapes[n]) // (PIECE_ROWS * LANES)) * PIECE_ROWS
    return off, rows, rows + PIECE_ROWS


def _pack_small(vals, shapes, last=None):
    parts = []
    for n in SMALL_NAMES:
        flat = vals[n].reshape(-1).astype(_F32)
        pad = -flat.shape[0] % (PIECE_ROWS * LANES)
        if pad:
            flat = jnp.concatenate([flat, jnp.zeros((pad,), _F32)])
        parts.append(flat.reshape(-1, LANES))
    parts.append(jnp.zeros((PIECE_ROWS, LANES), _F32) if last is None else last)
    return jnp.concatenate(parts, axis=0)


def _unpack_small(buf, shapes):
    off, _, _ = _small_layout(shapes)
    out = {}
    for n in SMALL_NAMES:
        size = math.prod(shapes[n])
        rows = -(-size // LANES)
        out[n] = buf[off[n]:off[n] + rows].reshape(-1)[:size].reshape(shapes[n])
    return out


WEIGHT_NAMES = ('norm_mix_pre', 'norm_mix_post', 'w_in', 'ret_gn_gain', 'ssm_lambda_re', 'ssm_lambda_im', 'ssm_log_dt',
                'ssm_b_re', 'ssm_b_im', 'ssm_c_re', 'ssm_c_im', 'ssm_d', 'w_glu', 'w_out', 'norm_mlp_pre',
                'norm_mlp_post', 'w_ff1', 'w_ff2')


def kernel(x, norm_mix_pre, norm_mix_post, w_in, ret_gn_gain, ssm_lambda_re, ssm_lambda_im, ssm_log_dt, ssm_b_re, ssm_b_im, ssm_c_re, ssm_c_im, ssm_d, w_glu, w_out, norm_mlp_pre, norm_mlp_post, w_ff1, w_ff2, loss_target, m_norm_mix_pre, m_norm_mix_post, m_w_in, m_ret_gn_gain, m_ssm_lambda_re, m_ssm_lambda_im, m_ssm_log_dt, m_ssm_b_re, m_ssm_b_im, m_ssm_c_re, m_ssm_c_im, m_ssm_d, m_w_glu, m_w_out, m_norm_mlp_pre, m_norm_mlp_post, m_w_ff1, m_w_ff2, v_norm_mix_pre, v_norm_mix_post, v_w_in, v_ret_gn_gain, v_ssm_lambda_re, v_ssm_lambda_im, v_ssm_log_dt, v_ssm_b_re, v_ssm_b_im, v_ssm_c_re, v_ssm_c_im, v_ssm_d, v_w_glu, v_w_out, v_norm_mlp_pre, v_norm_mlp_post, v_w_ff1, v_w_ff2):
    args = dict(locals())
    w = {n: args[n] for n in WEIGHT_NAMES}
    m = {n: args["m_" + n] for n in WEIGHT_NAMES}
    v = {n: args["v_" + n] for n in WEIGHT_NAMES}
    L = x.shape[1]
    tm = min(256, L)
    tk = min(2048, L)
    tb = min(512, L)

    gathers, zero = {}, jnp.zeros((), _F32)
    for group, names in WEIGHT_GROUPS.items():
        blocks = [w[n][0].astype(_BF) for n in names]
        blocks[0] = blocks[0] + zero.astype(_BF)
        gathers[group] = _split_start(blocks, [_landing(b) for b in blocks], True, "weights_start_" + group)
        zero = gathers[group][4][0, 0]

    def weights(group, after):
        landed = _split_wait(*gathers[group][:4], after, True, "weights_wait_" + group)
        return [_weight_from_slots(n, g) for n, g in zip(WEIGHT_GROUPS[group], landed)]

    in_flight = []

    def emit(dws):
        names = sorted(dws)
        srcs = [_grad_slots(n, dws[n]) for n in names]
        lands = [_landing(lax.dynamic_index_in_dim(t, _my_index(), 0, keepdims=False)) for t in srcs]
        started = _split_start(srcs, lands, False, "grads_start_" + "_".join(names))
        in_flight.append((names, started))
        return started[4][0, 0]

    small_w = {n: w[n] for n in SMALL_NAMES}
    sq, gx, gsmall = _local_grads(x[0], loss_target[0], small_w, weights, emit, tm, tk, tb, zero=zero)

    shapes = {n: w[n].shape for n in SMALL_NAMES}
    loss_rows = jnp.broadcast_to(0.5 / D_MODEL * jnp.sum(sq), (PIECE_ROWS, LANES)).astype(_F32)
    small_buf = _pack_small(gsmall, shapes, loss_rows)
    small_started = _split_start([small_buf], [_landing(small_buf)], True, "small_grads_start")
    grads, delta, new_m, new_v = {}, {}, {}, {}
    after = small_started[4]
    for names, started in in_flight:
        landed = _split_wait(*started[:4], after, False, "grads_wait_" + "_".join(names))
        for n, parts in zip(names, landed):
            res = _sum_adamw(parts, w[n][0], m[n][0], v[n][0], min(256, BIG_SHAPES[n][0]), "adamw_" + n)
            grads[n], delta[n], new_m[n], new_v[n] = (t[None] for t in res)
        after = res[1]
    small_parts = _split_wait(*small_started[:4], after, True, "small_grads_wait")[0]
    sw, sm, sv = _pack_small(w, shapes), _pack_small(m, shapes), _pack_small(v, shapes)
    res = _sum_adamw(small_parts, sw, sm, sv, sw.shape[0], "adamw_small")
    for dst, buf in zip((grads, delta, new_m, new_v), res):
        dst.update(_unpack_small(buf, shapes))
    _, loss_at, _ = _small_layout(shapes)
    loss = res[0][loss_at, 0]

    return (loss, gx[None], *[grads[n] for n in WEIGHT_NAMES], *[delta[n] for n in WEIGHT_NAMES],
            *[new_m[n] for n in WEIGHT_NAMES], *[new_v[n] for n in WEIGHT_NAMES])
```

```python
import math

import jax
import jax.numpy as jnp
from jax import lax
from jax.experimental import pallas as pl
from jax.experimental.pallas import tpu as pltpu

_BF = jnp.bfloat16
_F32 = jnp.float32

D_MODEL = 1024
RET_W = 512
N_HEAD = 4
HEAD_D = 128
CHUNK = 128
SSM_W = 512
SSM_GC = 16
N_GROUP = 32
N_STATE = 64
GROUPS_PER_KB = 8
N_KB = 4
KB_STATES = GROUPS_PER_KB * N_STATE
D_FF = 4096
IN_COLS = 2560
NORM_EPS = 1e-6
ROPE_BASE = 10000.0
N_DEV = 8

ADAM_LR = 0.001
ADAM_B1 = 0.9
ADAM_B2 = 0.999
ADAM_EPS = 1e-08
ADAM_WD = 0.01
ADAM_STEP = 10

SUBLANES = 8
LANES = 128
VMEM_LIMIT = 52 * 1024 * 1024
RET_STEP_CHUNKS = 4
KB_PER_STEP = 2
SCAN_UNROLL = 2

MESH = pl.DeviceIdType.MESH


def _params(*sem):
    return pltpu.CompilerParams(dimension_semantics=sem, vmem_limit_bytes=VMEM_LIMIT)


def _dot(a, b):
    return jnp.dot(a, b, preferred_element_type=_F32)


def _dot_nt(a, b):
    return lax.dot_general(a, b, (((1,), (1,)), ((), ())), preferred_element_type=_F32)


def _dot_tn(a, b):
    return lax.dot_general(a, b, (((0,), (0,)), ((), ())), preferred_element_type=_F32)


def _rms_r(z):
    return lax.rsqrt(jnp.mean(z * z, axis=-1, keepdims=True) + NORM_EPS)


def _rms_bwd(z, g, dn):
    r = _rms_r(z)
    t = dn * g
    dz = r * t - z * (r * r * r * jnp.mean(t * z, axis=-1, keepdims=True))
    return dz, dn * z * r


def _rope(t, cs, sn):
    return t * cs + pltpu.roll(t, HEAD_D // 2, 1) * sn


def _rope_t(t, cs, sn):
    return t * cs - pltpu.roll(t, HEAD_D // 2, 1) * sn


def _sigmoid(z):
    return 1.0 / (1.0 + jnp.exp(-z))


_GELU_C = math.sqrt(2.0 / math.pi)


def _gelu(z):
    return 0.5 * z * (1.0 + jnp.tanh(_GELU_C * (z + 0.044715 * z * z * z)))


def _gelu_grad(z):
    th = jnp.tanh(_GELU_C * (z + 0.044715 * z * z * z))
    return 0.5 * (1.0 + th) + 0.5 * z * (1.0 - th * th) * _GELU_C * (1.0 + 3 * 0.044715 * z * z)


ROW_CHUNK = 256


def _row_chunks(tm):
    return [pl.ds(i, min(ROW_CHUNK, tm)) for i in range(0, tm, ROW_CHUNK)]


def _row_spec(tm, n):
    return pl.BlockSpec((tm, n), lambda i: (i, 0))


def _full_spec(shape):
    nd = len(shape)
    return pl.BlockSpec(shape, lambda *_: (0,) * nd)


def _weight_spec(shape):
    nd = len(shape)
    return pl.BlockSpec(shape, lambda *_: (0,) * nd, pipeline_mode=pl.Buffered(1))


def _rope_tables(L):
    half = HEAD_D // 2
    inv_freq = ROPE_BASE ** (-jnp.arange(half, dtype=_F32) / half)
    twice = lambda t: jnp.concatenate([t, t], axis=-1)
    off = jnp.arange(CHUNK, dtype=_F32)[:, None] * inv_freq[None, :]
    start = (CHUNK * jnp.arange(L // CHUNK, dtype=_F32))[:, None] * inv_freq[None, :]
    return (twice(jnp.cos(off)), twice(jnp.sin(off)),
            twice(jnp.cos(start))[:, None, :], twice(jnp.sin(start))[:, None, :])


def _inproj_fwd(x, g1, w_in_t, rope, tm):
    L = x.shape[0]
    n_chunks = tm // CHUNK

    def body(x_ref, g_ref, w_ref, co_ref, so_ref, cs_ref, ss_ref, h_ref, q_ref, k_ref, v_ref, gate_ref, u_ref,
             cos_ref, sin_ref):
        xv = x_ref[...]
        h = (xv * _rms_r(xv) * g_ref[...]).astype(_BF)
        h_ref[...] = h
        proj = _dot_nt(h, w_ref[...])
        lane = lax.broadcasted_iota(jnp.int32, (CHUNK, HEAD_D), 1)
        sign = jnp.where(lane < HEAD_D // 2, -1.0, 1.0)
        co, so = co_ref[...], so_ref[...]
        for c in range(n_chunks):
            chunk = pl.program_id(0) * n_chunks + c
            cst, sst = cs_ref[chunk], ss_ref[chunk]
            rows = slice(c * CHUNK, (c + 1) * CHUNK)
            cs = co * cst - so * sst
            sn = (so * cst + co * sst) * sign
            cos_ref[rows, :] = cs
            sin_ref[rows, :] = sn
            for hh in range(N_HEAD):
                lo = hh * HEAD_D
                q_ref[rows, lo:lo + HEAD_D] = _rope(proj[rows, lo:lo + HEAD_D], cs, sn).astype(_BF)
                kh = _rope(proj[rows, RET_W + lo:RET_W + lo + HEAD_D], cs, sn) * (HEAD_D ** -0.5)
                k_ref[rows, lo:lo + HEAD_D] = kh.astype(_BF)
        v_ref[...] = proj[:, 2 * RET_W:3 * RET_W].astype(_BF)
        gate_ref[...] = proj[:, 3 * RET_W:4 * RET_W]
        u_ref[...] = proj[:, 4 * RET_W:]

    nc = L // CHUNK
    return pl.pallas_call(
        body, name="inproj_fwd", grid=(L // tm,),
        in_specs=[_row_spec(tm, D_MODEL), _full_spec((1, D_MODEL)), _weight_spec((IN_COLS, D_MODEL)),
                  _full_spec((CHUNK, HEAD_D)), _full_spec((CHUNK, HEAD_D)),
                  _full_spec((nc, 1, HEAD_D)), _full_spec((nc, 1, HEAD_D))],
        out_specs=[_row_spec(tm, D_MODEL)] + [_row_spec(tm, RET_W)] * 5 + [_row_spec(tm, HEAD_D)] * 2,
        out_shape=[jax.ShapeDtypeStruct((L, D_MODEL), _BF)] + [jax.ShapeDtypeStruct((L, RET_W), _BF)] * 3
        + [jax.ShapeDtypeStruct((L, RET_W), _F32)] * 2 + [jax.ShapeDtypeStruct((L, HEAD_D), _F32)] * 2,
        compiler_params=_params("parallel"),
    )(x, g1, w_in_t, *rope)


def _ret_consts():
    lg = jnp.log(1.0 - jnp.exp(jnp.linspace(math.log(1.0 / 32), math.log(1.0 / 512), N_HEAD))).astype(_F32)
    idx = jnp.arange(CHUNK, dtype=_F32)
    diff = idx[:, None] - idx[None, :]
    decay = jnp.where(diff[None] >= 0, jnp.exp(jnp.maximum(diff, 0.0)[None] * lg[:, None, None]), 0.0)
    zeta = jnp.exp((CHUNK - 1 - idx)[None, :] * lg[:, None])
    xi = jnp.exp((idx + 1.0)[None, :] * lg[:, None])
    gc = jnp.exp(CHUNK * lg)
    wide = lambda t: jnp.broadcast_to(t[:, :, None], (N_HEAD, CHUNK, HEAD_D)).astype(_F32)
    gcw = jnp.broadcast_to(gc[:, None, None], (N_HEAD, SUBLANES, HEAD_D)).astype(_F32)
    return decay.astype(_F32), wide(xi), wide(zeta), gcw


def _head_specs():
    c3 = _full_spec((N_HEAD, CHUNK, CHUNK))
    return [c3, c3, c3, _full_spec((N_HEAD, SUBLANES, HEAD_D))]


def _retention_fwd(q, k, v, gate, ggn, consts):
    L = q.shape[0]
    nc = L // CHUNK
    cps = math.gcd(RET_STEP_CHUNKS, nc)
    blk = pl.BlockSpec((cps * CHUNK, RET_W), lambda n: (n, 0))

    def body(q_ref, k_ref, v_ref, gate_ref, ggn_ref, dm_ref, xi_ref, zeta_ref, gc_ref,
             o_ref, y_ref, rp_ref, r_scr):
        @pl.when(pl.program_id(0) == 0)
        def _():
            r_scr[...] = jnp.zeros_like(r_scr)

        for hh in range(N_HEAD):
            cols = slice(hh * HEAD_D, (hh + 1) * HEAD_D)
            state = r_scr[hh]
            for c in range(cps):
                rows = slice(c * CHUNK, (c + 1) * CHUNK)
                qv, kv, vv = q_ref[rows, cols], k_ref[rows, cols], v_ref[rows, cols]
                s = _dot_nt(qv, kv) * dm_ref[hh]
                o = _dot(s.astype(_BF), vv) + _dot(qv, state.astype(_BF)) * xi_ref[hh]
                o_ref[rows, cols] = o
                rp_ref[hh, c] = state
                vz = (vv.astype(_F32) * zeta_ref[hh]).astype(_BF)
                state = gc_ref[hh, 0:1, :] * state + _dot_tn(kv, vz)
                dlt = o - jnp.mean(o, axis=-1, keepdims=True)
                on = dlt * lax.rsqrt(jnp.mean(dlt * dlt, axis=-1, keepdims=True) + NORM_EPS)
                gt = gate_ref[rows, cols]
                y_ref[rows, cols] = (gt * _sigmoid(gt) * (on * ggn_ref[:, cols])).astype(_BF)
            r_scr[hh] = state

    return pl.pallas_call(
        body, name="retention_fwd", grid=(nc // cps,),
        in_specs=[blk, blk, blk, blk, _full_spec((1, RET_W))] + _head_specs(),
        out_specs=[blk, blk, pl.BlockSpec((N_HEAD, cps, HEAD_D, HEAD_D), lambda n: (0, n, 0, 0))],
        out_shape=[jax.ShapeDtypeStruct((L, RET_W), _F32), jax.ShapeDtypeStruct((L, RET_W), _BF),
                   jax.ShapeDtypeStruct((N_HEAD, nc, HEAD_D, HEAD_D), _F32)],
        scratch_shapes=[pltpu.VMEM((N_HEAD, HEAD_D, HEAD_D), _F32)],
        compiler_params=_params("arbitrary"),
    )(q, k, v, gate, ggn, *consts)


def _rows_to_segments(dst_scr, src_ref, seg):
    for g in range(dst_scr.shape[0]):
        for j in range(SUBLANES):
            dst_scr[g, pl.ds(j, seg, stride=SUBLANES), :] = src_ref[pl.ds(j * seg, seg), g * LANES:(g + 1) * LANES]


def _segments_to_rows(dst_ref, src_scr, seg):
    for g in range(src_scr.shape[0]):
        for j in range(SUBLANES):
            dst_ref[pl.ds(j * seg, seg), g * LANES:(g + 1) * LANES] = src_scr[g, pl.ds(j, seg, stride=SUBLANES), :]


def _scan_segments(x_ref, tab_ref, pw_ref, carry_ref, seg, reverse, entry_ref=None, fwd_ref=None, fwd_entry_ref=None,
                   da_ref=None):
    G = x_ref.shape[0]
    W = KB_STATES
    re, im = pl.ds(0, W), pl.ds(W, W)
    row_id = lax.broadcasted_iota(jnp.int32, (SUBLANES, W), 0)
    edge_in = (row_id == SUBLANES - 1) if reverse else (row_id == 0)
    edge_out = 0 if reverse else SUBLANES - 1
    a_tab = [(tab_ref[g, 0], tab_ref[g, 1]) for g in range(G)]

    def local(i, st):
        r = (seg - 1 - i) if reverse else i
        out = []
        for g in range(G):
            (ar, ai), (sr, si) = a_tab[g], st[g]
            nr = ar * sr - ai * si + x_ref[g, r, :, re]
            ni = ar * si + ai * sr + x_ref[g, r, :, im]
            x_ref[g, r, :, re] = nr
            x_ref[g, r, :, im] = ni
            out.append((nr, ni))
        return tuple(out)

    zero = jnp.zeros((SUBLANES, W), _F32)
    ends = lax.fori_loop(0, seg, local, tuple((zero, zero) for _ in range(G)), unroll=SCAN_UNROLL)

    entry = []
    shift = (SUBLANES - 1) if reverse else 1
    for g in range(G):
        er, ei = ends[g]
        fr = jnp.where(edge_in, carry_ref[g, :, re], pltpu.roll(er, shift, 0))
        fi = jnp.where(edge_in, carry_ref[g, :, im], pltpu.roll(ei, shift, 0))
        for j, dist in enumerate((1, 2, 4)):
            pr, pi = tab_ref[g, 2 + 2 * j], tab_ref[g, 3 + 2 * j]
            sh = (SUBLANES - dist) if reverse else dist
            sr, si = pltpu.roll(fr, sh, 0), pltpu.roll(fi, sh, 0)
            fr, fi = fr + pr * sr - pi * si, fi + pr * si + pi * sr
        br, bi = tab_ref[g, 8], tab_ref[g, 9]
        outr = br * fr - bi * fi + er
        outi = br * fi + bi * fr + ei
        carry_ref[g, :, re] = jnp.broadcast_to(outr[edge_out:edge_out + 1, :], (SUBLANES, W))
        carry_ref[g, :, im] = jnp.broadcast_to(outi[edge_out:edge_out + 1, :], (SUBLANES, W))
        entry.append((fr, fi))
        if entry_ref is not None:
            entry_ref[g, :, re] = fr
            entry_ref[g, :, im] = fi

    add_da = da_ref is not None

    def fix(r, st, first=False):
        out = []
        for g in range(G):
            fr, fi = entry[g]
            pwr, pwi = pw_ref[g, r, :, re], pw_ref[g, r, :, im]
            xr = x_ref[g, r, :, re] + (pwr * fr - pwi * fi)
            xi = x_ref[g, r, :, im] + (pwr * fi + pwi * fr)
            x_ref[g, r, :, re] = xr
            x_ref[g, r, :, im] = xi
            if add_da:
                prev = fwd_entry_ref.at[g] if first else fwd_ref.at[g, r - 1]
                xpr, xpi = prev[:, re], prev[:, im]
                out.append((st[g][0] + (xr * xpr + xi * xpi), st[g][1] + (xi * xpr - xr * xpi)))
            else:
                out.append(st[g])
        return tuple(out)

    if add_da:
        st = fix(0, tuple((zero, zero) for _ in range(G)), first=True)
        st = lax.fori_loop(1, seg, fix, st, unroll=SCAN_UNROLL)
        for g in range(G):
            da_ref[g, :, re] += st[g][0]
            da_ref[g, :, im] += st[g][1]
    else:
        lax.fori_loop(0, seg, fix, tuple((zero[0:1, 0:LANES],) for _ in range(G)), unroll=SCAN_UNROLL)


def _s5_specs(seg, time=lambda t: t):
    G = KB_PER_STEP
    return dict(
        x=pl.BlockSpec((G, seg, SUBLANES, 2 * KB_STATES), lambda kb, t: (kb, time(t), 0, 0)),
        ent=pl.BlockSpec((G, 1, SUBLANES, 2 * KB_STATES), lambda kb, t: (kb, time(t), 0, 0)),
        b=pl.BlockSpec((G, LANES, 2 * KB_STATES), lambda kb, t: (kb, 0, 0)),
        c=pl.BlockSpec((G, 2 * KB_STATES, LANES), lambda kb, t: (kb, 0, 0)),
        tab=pl.BlockSpec((G, 10, SUBLANES, KB_STATES), lambda kb, t: (kb, 0, 0, 0)),
        pw=pl.BlockSpec((G, seg, 1, 2 * KB_STATES), lambda kb, t: (kb, 0, 0, 0)),
        d=pl.BlockSpec((1, G * LANES), lambda kb, t: (0, kb)),
    )


def _s5_fwd(u, bmat, cmat, tab_f, pw_f, d_skip, tb):
    L = u.shape[0]
    nt = L // tb
    seg = tb // SUBLANES
    G = KB_PER_STEP
    ucol = pl.BlockSpec((tb, G * LANES), lambda kb, t: (t, kb))
    sp = _s5_specs(seg)

    def body(u_ref, b_ref, c_ref, tab_ref, pw_ref, d_ref, s_ref, x_ref, ent_ref, up_scr, y_scr, carry_scr):
        @pl.when(pl.program_id(1) == 0)
        def _():
            carry_scr[...] = jnp.zeros_like(carry_scr)

        _rows_to_segments(up_scr, u_ref, seg)
        for g in range(G):
            x_ref[g] = _dot(up_scr[g].astype(_BF), b_ref[g]).reshape(seg, SUBLANES, 2 * KB_STATES)
        _scan_segments(x_ref, tab_ref, pw_ref, carry_scr, seg, reverse=False, entry_ref=ent_ref.at[:, 0])
        for g in range(G):
            y = _dot(x_ref[g].reshape(tb, 2 * KB_STATES).astype(_BF), c_ref[g])
            y_scr[g] = y + d_ref[:, g * LANES:(g + 1) * LANES] * up_scr[g]
        _segments_to_rows(s_ref, y_scr, seg)

    return pl.pallas_call(
        body, name="s5_fwd", grid=(N_KB // G, nt),
        in_specs=[ucol, sp["b"], sp["c"], sp["tab"], sp["pw"], sp["d"]],
        out_specs=[ucol, sp["x"], sp["ent"]],
        out_shape=[jax.ShapeDtypeStruct((L, SSM_W), _F32),
                   jax.ShapeDtypeStruct((N_KB, L // SUBLANES, SUBLANES, 2 * KB_STATES), _F32),
                   jax.ShapeDtypeStruct((N_KB, nt, SUBLANES, 2 * KB_STATES), _F32)],
        scratch_shapes=[pltpu.VMEM((G, tb, LANES), _F32)] * 2 + [pltpu.VMEM((G, SUBLANES, 2 * KB_STATES), _F32)],
        compiler_params=_params("parallel", "arbitrary"),
    )(u, bmat, cmat, tab_f, pw_f, d_skip)


def _mixout_fwd(s, y_ret, x, w_glu, w_out, g2, tm):
    L = s.shape[0]

    def body(s_ref, yr_ref, x_ref, wg_ref, wo_ref, g_ref, ys_ref, glu_ref, cat_ref, mix_ref, x2_ref):
        ys = _gelu(s_ref[...]).astype(_BF)
        ys_ref[...] = ys
        glu = _dot(ys, wg_ref[...])
        glu_ref[...] = glu
        cat_ref[:, :RET_W] = yr_ref[...]
        cat_ref[:, RET_W:] = (glu[:, :SSM_W] * _sigmoid(glu[:, SSM_W:])).astype(_BF)
        mix = _dot(cat_ref[...], wo_ref[...])
        mix_ref[...] = mix
        x2_ref[...] = x_ref[...] + mix * _rms_r(mix) * g_ref[...]

    return pl.pallas_call(
        body, name="mixout_fwd", grid=(L // tm,),
        in_specs=[_row_spec(tm, SSM_W), _row_spec(tm, RET_W), _row_spec(tm, D_MODEL),
                  _weight_spec((SSM_W, 2 * SSM_W)), _weight_spec((D_MODEL, D_MODEL)), _full_spec((1, D_MODEL))],
        out_specs=[_row_spec(tm, SSM_W), _row_spec(tm, 2 * SSM_W), _row_spec(tm, D_MODEL),
                   _row_spec(tm, D_MODEL), _row_spec(tm, D_MODEL)],
        out_shape=[jax.ShapeDtypeStruct((L, SSM_W), _BF), jax.ShapeDtypeStruct((L, 2 * SSM_W), _F32),
                   jax.ShapeDtypeStruct((L, D_MODEL), _BF), jax.ShapeDtypeStruct((L, D_MODEL), _F32),
                   jax.ShapeDtypeStruct((L, D_MODEL), _F32)],
        compiler_params=_params("parallel"),
    )(s, y_ret, x, w_glu, w_out, g2)


FF1_COLS = D_FF // N_DEV


def _ff1_fwd(x2, g3, w1, tm):
    L = x2.shape[0]

    def body(x_ref, g_ref, w_ref, h_ref, f_ref):
        xv = x_ref[...]
        h = (xv * _rms_r(xv) * g_ref[...]).astype(_BF)
        h_ref[...] = h
        for j in range(N_DEV):
            f_ref[:, j * FF1_COLS:(j + 1) * FF1_COLS] = _dot(h, w_ref[j])

    return pl.pallas_call(
        body, name="ff1_fwd", grid=(L // tm,),
        in_specs=[_row_spec(tm, D_MODEL), _full_spec((1, D_MODEL)), _weight_spec((N_DEV, D_MODEL, FF1_COLS))],
        out_specs=[_row_spec(tm, D_MODEL), _row_spec(tm, D_FF)],
        out_shape=[jax.ShapeDtypeStruct((L, D_MODEL), _BF), jax.ShapeDtypeStruct((L, D_FF), _F32)],
        compiler_params=_params("parallel"),
    )(x2, g3, w1)


def _ff2_loss(f1, x2, tgt, g4, w2, tm):
    L = f1.shape[0]

    def body(f_ref, x_ref, t_ref, g_ref, w_ref, dy_ref, dm_ref, dg_ref, ls_ref):
        @pl.when(pl.program_id(0) == 0)
        def _():
            dg_ref[...] = jnp.zeros_like(dg_ref)
            ls_ref[...] = jnp.zeros_like(ls_ref)

        g = g_ref[...]
        for rows in _row_chunks(tm):
            rl = jnp.maximum(f_ref[rows, :], 0.0)
            m = _dot((rl * rl).astype(_BF), w_ref[...])
            y = x_ref[rows, :] + m * _rms_r(m) * g
            err = y - t_ref[rows, :]
            ls_ref[...] += jnp.sum(err * err, axis=0, keepdims=True)
            dy = err * (1.0 / D_MODEL)
            dy_ref[rows, :] = dy
            dm, dgr = _rms_bwd(m, g, dy)
            dm_ref[rows, :] = dm.astype(_BF)
            dg_ref[...] += jnp.sum(dgr, axis=0, keepdims=True)

    return pl.pallas_call(
        body, name="ff2_loss", grid=(L // tm,),
        in_specs=[_row_spec(tm, D_FF), _row_spec(tm, D_MODEL), _row_spec(tm, D_MODEL),
                  _full_spec((1, D_MODEL)), _weight_spec((D_FF, D_MODEL))],
        out_specs=[_row_spec(tm, D_MODEL), _row_spec(tm, D_MODEL), _full_spec((1, D_MODEL)), _full_spec((1, D_MODEL))],
        out_shape=[jax.ShapeDtypeStruct((L, D_MODEL), _F32), jax.ShapeDtypeStruct((L, D_MODEL), _BF),
                   jax.ShapeDtypeStruct((1, D_MODEL), _F32), jax.ShapeDtypeStruct((1, D_MODEL), _F32)],
        compiler_params=_params("arbitrary"),
    )(f1, x2, tgt, g4, w2)


def _ff2_bwd(dm, f1, w2, tm, tn):
    L = dm.shape[0]
    last = L // tm - 1

    def body(dm_ref, f_ref, w_ref, df_ref, dw_ref, acc):
        @pl.when(pl.program_id(1) == 0)
        def _():
            acc[...] = jnp.zeros_like(acc)

        dmv = dm_ref[...]
        rl = jnp.maximum(f_ref[...], 0.0)
        df_ref[...] = (_dot_nt(dmv, w_ref[...]) * (2.0 * rl)).astype(_BF)
        acc[...] += _dot_tn((rl * rl).astype(_BF), dmv)

        @pl.when(pl.program_id(1) == last)
        def _():
            dw_ref[...] = acc[...].astype(_BF)

    return pl.pallas_call(
        body, name="ff2_bwd", grid=(D_FF // tn, L // tm),
        in_specs=[pl.BlockSpec((tm, D_MODEL), lambda j, i: (i, 0)), pl.BlockSpec((tm, tn), lambda j, i: (i, j)),
                  pl.BlockSpec((tn, D_MODEL), lambda j, i: (j, 0))],
        out_specs=[pl.BlockSpec((tm, tn), lambda j, i: (i, j)), pl.BlockSpec((tn, D_MODEL), lambda j, i: (j, 0))],
        out_shape=[jax.ShapeDtypeStruct((L, D_FF), _BF), jax.ShapeDtypeStruct((D_FF, D_MODEL), _BF)],
        scratch_shapes=[pltpu.VMEM((tn, D_MODEL), _F32)],
        compiler_params=_params("parallel", "arbitrary"),
    )(dm, f1, w2)


def _ff1_bwd(df1, w1, x2, mix, dy, g3, g2, tm):
    L = df1.shape[0]

    def body(df_ref, w_ref, x2_ref, mix_ref, dy_ref, g3_ref, g2_ref, dx2_ref, dmix_ref, dg3_ref, dg2_ref):
        @pl.when(pl.program_id(0) == 0)
        def _():
            dg3_ref[...] = jnp.zeros_like(dg3_ref)
            dg2_ref[...] = jnp.zeros_like(dg2_ref)

        for rows in _row_chunks(tm):
            dh = _dot_nt(df_ref[rows, 0:FF1_COLS], w_ref[0])
            for j in range(1, N_DEV):
                dh = dh + _dot_nt(df_ref[rows, j * FF1_COLS:(j + 1) * FF1_COLS], w_ref[j])
            dz, dgr = _rms_bwd(x2_ref[rows, :], g3_ref[...], dh)
            dg3_ref[...] += jnp.sum(dgr, axis=0, keepdims=True)
            dx2 = dy_ref[rows, :] + dz
            dx2_ref[rows, :] = dx2
            dmx, dgr2 = _rms_bwd(mix_ref[rows, :], g2_ref[...], dx2)
            dg2_ref[...] += jnp.sum(dgr2, axis=0, keepdims=True)
            dmix_ref[rows, :] = dmx.astype(_BF)

    vec = _full_spec((1, D_MODEL))
    return pl.pallas_call(
        body, name="ff1_bwd", grid=(L // tm,),
        in_specs=[_row_spec(tm, D_FF), _weight_spec((N_DEV, D_MODEL, FF1_COLS)), _row_spec(tm, D_MODEL),
                  _row_spec(tm, D_MODEL), _row_spec(tm, D_MODEL), vec, vec],
        out_specs=[_row_spec(tm, D_MODEL), _row_spec(tm, D_MODEL), vec, vec],
        out_shape=[jax.ShapeDtypeStruct((L, D_MODEL), _F32), jax.ShapeDtypeStruct((L, D_MODEL), _BF),
                   jax.ShapeDtypeStruct((1, D_MODEL), _F32), jax.ShapeDtypeStruct((1, D_MODEL), _F32)],
        compiler_params=_params("arbitrary"),
    )(df1, w1, x2, mix, dy, g3, g2)


def _matmul_tn(a, b, tm, tn, name, slots=False):
    L, K = a.shape
    N = b.shape[1]
    last = L // tm - 1

    def body(a_ref, b_ref, o_ref, acc):
        @pl.when(pl.program_id(1) == 0)
        def _():
            acc[...] = jnp.zeros_like(acc)

        acc[...] += _dot_tn(a_ref[...].astype(_BF), b_ref[...].astype(_BF))

        @pl.when(pl.program_id(1) == last)
        def _():
            if slots:
                o_ref[0] = acc[...].astype(_BF)
            else:
                o_ref[...] = acc[...].astype(_BF)

    if slots:
        out_spec = pl.BlockSpec((1, K, tn), lambda j, i: (j, 0, 0))
        out_shape = jax.ShapeDtypeStruct((N // tn, K, tn), _BF)
    else:
        out_spec = pl.BlockSpec((K, tn), lambda j, i: (0, j))
        out_shape = jax.ShapeDtypeStruct((K, N), _BF)
    return pl.pallas_call(
        body, name=name, grid=(N // tn, L // tm),
        in_specs=[pl.BlockSpec((tm, K), lambda j, i: (i, 0)), pl.BlockSpec((tm, tn), lambda j, i: (i, j))],
        out_specs=out_spec, out_shape=out_shape,
        scratch_shapes=[pltpu.VMEM((K, tn), _F32)],
        compiler_params=_params("parallel", "arbitrary"),
    )(a, b)


def _mixout_bwd(dmix, w_out, w_glu, glu, s, o, gate, ggn, tm):
    L = dmix.shape[0]

    def body(dmix_ref, wo_ref, wg_ref, glu_ref, s_ref, o_ref, gate_ref, ggn_ref,
             dglu_ref, ds_ref, dgate_ref, do_ref, dggn_ref):
        @pl.when(pl.program_id(0) == 0)
        def _():
            dggn_ref[...] = jnp.zeros_like(dggn_ref)

        dcat = _dot_nt(dmix_ref[...], wo_ref[...])
        dy_ret, dy_ssm = dcat[:, :RET_W], dcat[:, RET_W:]
        glu = glu_ref[...]
        ga, sg = glu[:, :SSM_W], _sigmoid(glu[:, SSM_W:])
        dga = (dy_ssm * sg).astype(_BF)
        dgb = (dy_ssm * ga * sg * (1.0 - sg)).astype(_BF)
        dglu_ref[:, :SSM_W] = dga
        dglu_ref[:, SSM_W:] = dgb
        dys = _dot_nt(dga, wg_ref[:, :SSM_W]) + _dot_nt(dgb, wg_ref[:, SSM_W:])
        ds_ref[...] = dys * _gelu_grad(s_ref[...])
        gt = gate_ref[...]
        sgt = _sigmoid(gt)
        ggn = ggn_ref[...]
        for hh in range(N_HEAD):
            cols = slice(hh * HEAD_D, (hh + 1) * HEAD_D)
            ov = o_ref[:, cols]
            dlt = ov - jnp.mean(ov, axis=-1, keepdims=True)
            rstd = lax.rsqrt(jnp.mean(dlt * dlt, axis=-1, keepdims=True) + NORM_EPS)
            on = dlt * rstd
            dyr = dy_ret[:, cols] * (gt[:, cols] * sgt[:, cols])
            dgate_ref[:, cols] = dy_ret[:, cols] * (on * ggn[:, cols]) * (sgt[:, cols] * (1.0 + gt[:, cols] * (1.0 - sgt[:, cols])))
            dggn_ref[:, cols] += jnp.sum(dyr * on, axis=0, keepdims=True)
            don = dyr * ggn[:, cols]
            do = rstd * (don - jnp.mean(don, axis=-1, keepdims=True) - on * jnp.mean(don * on, axis=-1, keepdims=True))
            do_ref[:, cols] = do.astype(_BF)

    return pl.pallas_call(
        body, name="mixout_bwd", grid=(L // tm,),
        in_specs=[_row_spec(tm, D_MODEL), _weight_spec((D_MODEL, D_MODEL)), _weight_spec((SSM_W, 2 * SSM_W)),
                  _row_spec(tm, 2 * SSM_W), _row_spec(tm, SSM_W), _row_spec(tm, RET_W), _row_spec(tm, RET_W),
                  _full_spec((1, RET_W))],
        out_specs=[_row_spec(tm, 2 * SSM_W), _row_spec(tm, SSM_W), _row_spec(tm, RET_W), _row_spec(tm, RET_W),
                   _full_spec((1, RET_W))],
        out_shape=[jax.ShapeDtypeStruct((L, 2 * SSM_W), _BF), jax.ShapeDtypeStruct((L, SSM_W), _F32),
                   jax.ShapeDtypeStruct((L, RET_W), _F32), jax.ShapeDtypeStruct((L, RET_W), _BF),
                   jax.ShapeDtypeStruct((1, RET_W), _F32)],
        compiler_params=_params("arbitrary"),
    )(dmix, w_out, w_glu, glu, s, o, gate, ggn)


def _s5_bwd(u, ds, xs, ent, bmat, cmat, tab_r, pw_r, d_skip, tb):
    L = u.shape[0]
    nt = L // tb
    seg = tb // SUBLANES
    G = KB_PER_STEP
    rcol = pl.BlockSpec((tb, G * LANES), lambda kb, t: (nt - 1 - t, kb))
    sp = _s5_specs(seg, time=lambda t: nt - 1 - t)
    aspec = pl.BlockSpec((G, SUBLANES, 2 * KB_STATES), lambda kb, t: (kb, 0, 0))

    def body(u_ref, ds_ref, x_ref, ent_ref, b_ref, c_ref, tr_ref, pr_ref, d_ref,
             du_ref, db_ref, dc_ref, da_ref, dd_ref, up_scr, dp_scr, g_scr, lc_scr):
        @pl.when(pl.program_id(1) == 0)
        def _():
            lc_scr[...] = jnp.zeros_like(lc_scr)
            db_ref[...] = jnp.zeros_like(db_ref)
            dc_ref[...] = jnp.zeros_like(dc_ref)
            da_ref[...] = jnp.zeros_like(da_ref)
            dd_ref[...] = jnp.zeros_like(dd_ref)

        _rows_to_segments(up_scr, u_ref, seg)
        _rows_to_segments(dp_scr, ds_ref, seg)
        for g in range(G):
            g_scr[g] = _dot_nt(dp_scr[g].astype(_BF), c_ref[g]).reshape(seg, SUBLANES, 2 * KB_STATES)
        _scan_segments(g_scr, tr_ref, pr_ref, lc_scr, seg, reverse=True, fwd_ref=x_ref, fwd_entry_ref=ent_ref.at[:, 0],
                       da_ref=da_ref)
        for g in range(G):
            cols = slice(g * LANES, (g + 1) * LANES)
            uv, dsv = up_scr[g], dp_scr[g]
            ub, dsb = uv.astype(_BF), dsv.astype(_BF)
            lamb = g_scr[g].reshape(tb, 2 * KB_STATES).astype(_BF)
            db_ref[g] += _dot_tn(ub, lamb)
            dc_ref[g] += _dot_tn(x_ref[g].reshape(tb, 2 * KB_STATES).astype(_BF), dsb)
            dd_ref[:, cols] += jnp.sum(dsv * uv, axis=0, keepdims=True)
            up_scr[g] = _dot_nt(lamb, b_ref[g]) + d_ref[:, cols] * dsv
        _segments_to_rows(du_ref, up_scr, seg)

    return pl.pallas_call(
        body, name="s5_bwd", grid=(N_KB // G, nt),
        in_specs=[rcol, rcol, sp["x"], sp["ent"], sp["b"], sp["c"], sp["tab"], sp["pw"], sp["d"]],
        out_specs=[rcol, sp["b"], sp["c"], aspec, sp["d"]],
        out_shape=[jax.ShapeDtypeStruct((L, SSM_W), _F32),
                   jax.ShapeDtypeStruct((N_KB, LANES, 2 * KB_STATES), _F32),
                   jax.ShapeDtypeStruct((N_KB, 2 * KB_STATES, LANES), _F32),
                   jax.ShapeDtypeStruct((N_KB, SUBLANES, 2 * KB_STATES), _F32),
                   jax.ShapeDtypeStruct((1, SSM_W), _F32)],
        scratch_shapes=[pltpu.VMEM((G, tb, LANES), _F32)] * 2
        + [pltpu.VMEM((G, seg, SUBLANES, 2 * KB_STATES), _F32), pltpu.VMEM((G, SUBLANES, 2 * KB_STATES), _F32)],
        compiler_params=_params("parallel", "arbitrary"),
    )(u, ds, xs, ent, bmat, cmat, tab_r, pw_r, d_skip)


def _retention_bwd(q, k, v, do, r_prev, consts, cosf, sinf):
    L = q.shape[0]
    nc = L // CHUNK
    cps = math.gcd(RET_STEP_CHUNKS, nc)
    nb = nc // cps
    blk = pl.BlockSpec((cps * CHUNK, RET_W), lambda n: (nb - 1 - n, 0))
    rope_blk = pl.BlockSpec((cps * CHUNK, HEAD_D), lambda n: (nb - 1 - n, 0))

    def body(q_ref, k_ref, v_ref, do_ref, rp_ref, dm_ref, xi_ref, zeta_ref, gc_ref, cos_ref, sin_ref,
             dq_ref, dk_ref, dv_ref, g_scr):
        @pl.when(pl.program_id(0) == 0)
        def _():
            g_scr[...] = jnp.zeros_like(g_scr)

        for hh in range(N_HEAD):
            cols = slice(hh * HEAD_D, (hh + 1) * HEAD_D)
            dm, zeta = dm_ref[hh], zeta_ref[hh]
            gst = g_scr[hh]
            for c in reversed(range(cps)):
                rows = slice(c * CHUNK, (c + 1) * CHUNK)
                qv, kv, vv, dov = q_ref[rows, cols], k_ref[rows, cols], v_ref[rows, cols], do_ref[rows, cols]
                rb = rp_ref[hh, c].astype(_BF)
                gb = gst.astype(_BF)
                sb = (_dot_nt(qv, kv) * dm).astype(_BF)
                dab = (_dot_nt(dov, vv) * dm).astype(_BF)
                dox = (dov.astype(_F32) * xi_ref[hh]).astype(_BF)
                vz = (vv.astype(_F32) * zeta).astype(_BF)
                dq = _dot(dab, kv) + _dot_nt(dox, rb)
                dk = _dot_tn(dab, qv) + _dot_nt(vz, gb)
                dv = _dot_tn(sb, dov) + _dot(kv, gb) * zeta
                gst = gc_ref[hh, 0:1, :] * gst + _dot_tn(qv, dox)
                cs, sn = cos_ref[rows, :], sin_ref[rows, :]
                dq_ref[rows, cols] = _rope_t(dq, cs, sn).astype(_BF)
                dk_ref[rows, cols] = (_rope_t(dk, cs, sn) * (HEAD_D ** -0.5)).astype(_BF)
                dv_ref[rows, cols] = dv.astype(_BF)
            g_scr[hh] = gst

    return pl.pallas_call(
        body, name="retention_bwd", grid=(nb,),
        in_specs=[blk, blk, blk, blk, pl.BlockSpec((N_HEAD, cps, HEAD_D, HEAD_D), lambda n: (0, nb - 1 - n, 0, 0))]
        + _head_specs() + [rope_blk, rope_blk],
        out_specs=[blk, blk, blk],
        out_shape=[jax.ShapeDtypeStruct((L, RET_W), _BF)] * 3,
        scratch_shapes=[pltpu.VMEM((N_HEAD, HEAD_D, HEAD_D), _F32)],
        compiler_params=_params("arbitrary"),
    )(q, k, v, do, r_prev, *consts, cosf, sinf)


def _inproj_bwd(pieces, w_in_t, x, dx2, g1, tm):
    L = x.shape[0]

    def body(p0, p1, p2, p3, p4, w_ref, x_ref, dx2_ref, g_ref, dx_ref, dg_ref):
        @pl.when(pl.program_id(0) == 0)
        def _():
            dg_ref[...] = jnp.zeros_like(dg_ref)

        dh = None
        for j, p in enumerate((p0, p1, p2, p3, p4)):
            part = _dot(p[...].astype(_BF), w_ref[j * RET_W:(j + 1) * RET_W, :])
            dh = part if dh is None else dh + part
        dz, dgr = _rms_bwd(x_ref[...], g_ref[...], dh)
        dx_ref[...] = dx2_ref[...] + dz
        dg_ref[...] += jnp.sum(dgr, axis=0, keepdims=True)

    return pl.pallas_call(
        body, name="inproj_bwd", grid=(L // tm,),
        in_specs=[_row_spec(tm, RET_W)] * 5 + [_weight_spec((IN_COLS, D_MODEL)), _row_spec(tm, D_MODEL),
                                                 _row_spec(tm, D_MODEL), _full_spec((1, D_MODEL))],
        out_specs=[_row_spec(tm, D_MODEL), _full_spec((1, D_MODEL))],
        out_shape=[jax.ShapeDtypeStruct((L, D_MODEL), _F32), jax.ShapeDtypeStruct((1, D_MODEL), _F32)],
        compiler_params=_params("arbitrary"),
    )(*pieces, w_in_t, x, dx2, g1)


def _sum_adamw(parts, w, m, v, tr, name):
    _, R, Cc = parts.shape

    def body(p_ref, w_ref, m_ref, v_ref, g_ref, d_ref, nm_ref, nv_ref):
        gv = p_ref[0].astype(_F32)
        for s in range(1, N_DEV):
            gv = gv + p_ref[s].astype(_F32)
        g_ref[...] = gv
        nm = ADAM_B1 * m_ref[...] + (1.0 - ADAM_B1) * gv
        nv = ADAM_B2 * v_ref[...] + (1.0 - ADAM_B2) * (gv * gv)
        m_hat = nm / (1.0 - ADAM_B1 ** ADAM_STEP)
        v_hat = nv / (1.0 - ADAM_B2 ** ADAM_STEP)
        d_ref[...] = -ADAM_LR * (m_hat / (jnp.sqrt(v_hat) + ADAM_EPS) + ADAM_WD * w_ref[...])
        nm_ref[...] = nm
        nv_ref[...] = nv

    spec = _row_spec(tr, Cc)
    return pl.pallas_call(
        body, name=name, grid=(R // tr,),
        in_specs=[pl.BlockSpec((N_DEV, tr, Cc), lambda i: (0, i, 0))] + [spec] * 3, out_specs=[spec] * 4,
        out_shape=[jax.ShapeDtypeStruct((R, Cc), _F32)] * 4,
        compiler_params=_params("parallel"),
    )(parts, w, m, v)


def _my_place():
    return lax.axis_index("x"), lax.axis_index("y"), lax.axis_index("c")


def _all_gather(blocks):
    n = len(blocks)

    def body(*refs):
        x_refs, out_refs, done_ref = refs[:n], refs[n:2 * n], refs[2 * n]
        send_sems, recv_sems, local_sems = refs[2 * n + 1:]
        done_ref[...] = jnp.zeros_like(done_ref)
        x, y, c = _my_place()
        me, sibling = (x, y, c), (x, y, 1 - c)
        chips = [(1 - x, y), (x, 1 - y), (1 - x, 1 - y)]

        def slot(a, px, py, pc):
            return out_refs[a].at[4 * px + 2 * py + pc]

        def copy(a, k, blk, to, own=False):
            return pltpu.make_async_remote_copy(
                src_ref=x_refs[a] if own else slot(a, *blk), dst_ref=slot(a, *blk),
                send_sem=send_sems.at[a, k], recv_sem=recv_sems.at[a, k], device_id=to, device_id_type=MESH)

        mine = [pltpu.make_async_copy(x_refs[a], slot(a, *me), local_sems.at[a]) for a in range(n)]
        for cp in mine:
            cp.start()
        first = []
        for a in range(n):
            first.append(copy(a, 0, me, sibling, own=True))
            first += [copy(a, 1 + j, me, (*chip, c), own=True) for j, chip in enumerate(chips)]
        for cp in first:
            cp.start()
        passed = []
        for j, chip in enumerate(chips):
            for a in range(n):
                copy(a, 1 + j, (*chip, c), me).wait_recv()
                fwd = copy(a, 4 + j, (*chip, c), sibling)
                fwd.start()
                passed.append(fwd)
        for a in range(n):
            copy(a, 0, sibling, me).wait_recv()
            for j, chip in enumerate(chips):
                copy(a, 4 + j, (*chip, 1 - c), me).wait_recv()
        for cp in first + passed:
            cp.wait_send()
        for cp in mine:
            cp.wait()

    any_spec = pl.BlockSpec(memory_space=pl.ANY)
    outs = pl.pallas_call(
        body, name="weights_all_gather",
        in_specs=[any_spec] * n, out_specs=[any_spec] * n + [pl.BlockSpec(memory_space=pltpu.VMEM)],
        out_shape=[jax.ShapeDtypeStruct((N_DEV,) + b.shape, b.dtype) for b in blocks]
        + [jax.ShapeDtypeStruct((SUBLANES, LANES), _F32)],
        scratch_shapes=[pltpu.SemaphoreType.DMA((n, 7)), pltpu.SemaphoreType.DMA((n, 7)), pltpu.SemaphoreType.DMA((n,))],
    )(*blocks)
    return outs[:n], outs[n]


def _exchange(bigs, small):
    n = len(bigs)
    r = small.shape[0]

    def body(*refs):
        in_refs, out_refs = refs[:n + 1], refs[n + 1:2 * n + 2]
        send_sems, recv_sems, local_sems = refs[2 * n + 2:]
        x, y, c = _my_place()
        me = 4 * x + 2 * y + c
        own = [pltpu.make_async_copy(in_refs[a].at[me], out_refs[a].at[me], local_sems.at[a]) for a in range(n)]
        own.append(pltpu.make_async_copy(in_refs[n], out_refs[n].at[me], local_sems.at[n]))
        for cp in own:
            cp.start()
        copies = []
        for kk in range(1, N_DEV):
            px, py, pc = x ^ (kk >> 2), y ^ ((kk >> 1) & 1), c ^ (kk & 1)
            peer = 4 * px + 2 * py + pc
            for a in range(n + 1):
                src = in_refs[a].at[peer] if a < n else in_refs[a]
                copies.append(pltpu.make_async_remote_copy(
                    src_ref=src, dst_ref=out_refs[a].at[me],
                    send_sem=send_sems.at[a, kk - 1], recv_sem=recv_sems.at[a, kk - 1],
                    device_id=(px, py, pc), device_id_type=MESH))
        for cp in copies:
            cp.start()
        for cp in copies:
            cp.wait_recv()
        for cp in copies:
            cp.wait_send()
        for cp in own:
            cp.wait()

    any_spec = pl.BlockSpec(memory_space=pl.ANY)
    outs = pl.pallas_call(
        body, name="grad_exchange",
        in_specs=[any_spec] * (n + 1), out_specs=[any_spec] * (n + 1),
        out_shape=[jax.ShapeDtypeStruct(b.shape, b.dtype) for b in bigs]
        + [jax.ShapeDtypeStruct((N_DEV, r, LANES), small.dtype)],
        scratch_shapes=[pltpu.SemaphoreType.DMA((n + 1, 7)), pltpu.SemaphoreType.DMA((n + 1, 7)),
                        pltpu.SemaphoreType.DMA((n + 1,))],
    )(*bigs, small)
    return outs[:n], outs[n]


HBM_SPEC = pl.BlockSpec(memory_space=pltpu.HBM)
SEM_SPEC = pl.BlockSpec(memory_space=pltpu.SEMAPHORE)
DATAFLOW = pltpu.SideEffectType.DATAFLOW_SIDE_EFFECTING


def _my_index():
    x, y, c = _my_place()
    return 4 * x + 2 * y + c


def _landing(own_block):
    zone = lax.empty((N_DEV,) + own_block.shape, own_block.dtype)
    return lax.dynamic_update_index_in_dim(zone, own_block, _my_index(), 0)


def _split_copies(src_refs, land_refs, send_sems, recv_sems, gather):
    x, y, c = _my_place()
    me = 4 * x + 2 * y + c
    copies = []
    for kk in range(1, N_DEV):
        px, py, pc = x ^ (kk >> 2), y ^ ((kk >> 1) & 1), c ^ (kk & 1)
        peer = 4 * px + 2 * py + pc
        for a, (src, land) in enumerate(zip(src_refs, land_refs)):
            copies.append(pltpu.make_async_remote_copy(
                src_ref=src if gather else src.at[peer], dst_ref=land.at[me],
                send_sem=send_sems.at[a * 7 + kk - 1], recv_sem=recv_sems.at[a * 7 + kk - 1],
                device_id=(px, py, pc), device_id_type=MESH))
    return copies


def _split_start(srcs, lands, gather, name):
    n = len(srcs)

    def body(*refs):
        src_refs, land_refs = refs[:n], refs[n:2 * n]
        send_sems, recv_sems = refs[2 * n], refs[2 * n + 1]
        token = refs[-1]
        for cp in _split_copies(src_refs, land_refs, send_sems, recv_sems, gather):
            cp.start()
        token[...] = jnp.zeros_like(token)

    outs = pl.pallas_call(
        body, name=name,
        out_shape=(pltpu.SemaphoreType.DMA((7 * n,)), pltpu.SemaphoreType.DMA((7 * n,)),
                   *[pltpu.HBM(t.shape, t.dtype) for t in srcs], *[pltpu.HBM(t.shape, t.dtype) for t in lands],
                   jax.ShapeDtypeStruct((SUBLANES, LANES), _F32)),
        in_specs=[HBM_SPEC] * (2 * n),
        out_specs=(SEM_SPEC, SEM_SPEC, *[HBM_SPEC] * (2 * n), pl.BlockSpec(memory_space=pltpu.VMEM)),
        input_output_aliases={i: 2 + i for i in range(2 * n)},
        compiler_params=pltpu.CompilerParams(has_side_effects=DATAFLOW),
    )(*[pltpu.with_memory_space_constraint(t, pltpu.HBM) for t in list(srcs) + list(lands)])
    return outs[0], outs[1], outs[2:2 + n], outs[2 + n:2 + 2 * n], outs[-1]


def _split_wait(send_sems, recv_sems, srcs, lands, after, gather, name):
    n = len(srcs)

    def body(*refs):
        src_refs, land_refs = refs[:n], refs[n:2 * n]
        send_s, recv_s = refs[2 * n], refs[2 * n + 1]
        for cp in _split_copies(src_refs, land_refs, send_s, recv_s, gather):
            cp.wait_send()
            cp.wait_recv()

    outs = pl.pallas_call(
        body, name=name,
        out_shape=tuple(pltpu.HBM(t.shape, t.dtype) for t in list(srcs) + list(lands)),
        in_specs=[HBM_SPEC] * (2 * n) + [SEM_SPEC, SEM_SPEC, pl.BlockSpec(memory_space=pl.ANY)],
        out_specs=tuple([HBM_SPEC] * (2 * n)),
        input_output_aliases={i: i for i in range(2 * n)},
        compiler_params=pltpu.CompilerParams(has_side_effects=DATAFLOW),
    )(*srcs, *lands, send_sems, recv_sems, after)
    return outs[n:]


def _discretize(lam_re, lam_im, log_dt, b_re, b_im):
    lr = jnp.minimum(lam_re, -1e-4)
    li = lam_im
    dt = jnp.exp(log_dt)[:, None]
    er = jnp.exp(lr * dt)
    ar, ai = er * jnp.cos(li * dt), er * jnp.sin(li * dt)
    den = lr * lr + li * li
    cr = ((ar - 1.0) * lr + ai * li) / den
    ci = (ai * lr - (ar - 1.0) * li) / den
    bbr = cr[:, :, None] * b_re - ci[:, :, None] * b_im
    bbi = cr[:, :, None] * b_im + ci[:, :, None] * b_re
    return ar, ai, bbr, bbi


def _cmul(ar, ai, br, bi):
    return ar * br - ai * bi, ar * bi + ai * br


def _cpowers(ar, ai, n):
    pr, pi = ar[None], ai[None]
    while pr.shape[0] < n:
        nr, ni = _cmul(pr, pi, pr[-1][None], pi[-1][None])
        pr, pi = jnp.concatenate([pr, nr]), jnp.concatenate([pi, ni])
    return pr[:n], pi[:n]


def _scan_tables(ar, ai, seg, reverse):
    if reverse:
        ai = -ai
    ar, ai = ar.reshape(N_KB, KB_STATES), ai.reshape(N_KB, KB_STATES)
    pr, pi = _cpowers(ar, ai, seg)
    a1 = (pr[-1], pi[-1])
    a2 = _cmul(*a1, *a1)
    a4 = _cmul(*a2, *a2)
    row = jnp.arange(SUBLANES)[None, :, None]
    wide = lambda t: jnp.broadcast_to(t[:, None, :], (N_KB, SUBLANES, KB_STATES))
    tabs = [wide(ar), wide(ai)]
    for dist, (qr, qi) in ((1, a1), (2, a2), (4, a4)):
        keep = (row < SUBLANES - dist) if reverse else (row >= dist)
        tabs += [jnp.where(keep, wide(qr), 0.0), jnp.where(keep, wide(qi), 0.0)]
    tabs += [wide(a1[0]), wide(a1[1])]
    if reverse:
        pr, pi = pr[::-1], pi[::-1]
    pw = jnp.transpose(jnp.concatenate([pr, pi], axis=-1), (1, 0, 2))[:, :, None, :]
    return jnp.stack(tabs, axis=1).astype(_F32), pw.astype(_F32)


def _block_diag_in(br, bi):
    eye = jnp.eye(GROUPS_PER_KB, dtype=_F32)
    one = lambda t: jnp.einsum("kgpc,gh->kgchp", t.reshape(N_KB, GROUPS_PER_KB, N_STATE, SSM_GC), eye).reshape(
        N_KB, LANES, KB_STATES)
    return jnp.concatenate([one(br), one(bi)], axis=-1)


def _block_diag_in_t(dmat):
    d6 = dmat.reshape(N_KB, GROUPS_PER_KB, SSM_GC, 2, GROUPS_PER_KB, N_STATE)
    eye = jnp.eye(GROUPS_PER_KB, dtype=_F32)
    both = jnp.einsum("kgcrhp,gh->rkgpc", d6, eye).reshape(2, N_GROUP, N_STATE, SSM_GC)
    return both[0], both[1]


def _block_diag_out(c_re, c_im):
    eye = jnp.eye(GROUPS_PER_KB, dtype=_F32)
    one = lambda t: jnp.einsum("kgcp,gh->khpgc", t.reshape(N_KB, GROUPS_PER_KB, SSM_GC, N_STATE), eye).reshape(
        N_KB, KB_STATES, LANES)
    return jnp.concatenate([one(c_re), -one(c_im)], axis=1)


def _block_diag_out_t(dmat):
    d6 = dmat.reshape(N_KB, 2, GROUPS_PER_KB, N_STATE, GROUPS_PER_KB, SSM_GC)
    eye = jnp.eye(GROUPS_PER_KB, dtype=_F32)
    both = jnp.einsum("krhpgc,gh->rkgcp", d6, eye).reshape(2, N_GROUP, SSM_GC, N_STATE)
    return both[0], -both[1]


SMALL_NAMES = ("norm_mix_pre", "norm_mix_post", "ret_gn_gain", "ssm_lambda_re", "ssm_lambda_im", "ssm_log_dt",
               "ssm_b_re", "ssm_b_im", "ssm_c_re", "ssm_c_im", "ssm_d", "norm_mlp_pre", "norm_mlp_post")


def _local_grads(x, tgt, small, weights, emit, emit_small, tm, tk, tb, zero=0.0):
    L = x.shape[0]
    g1, g2, ggn = small["norm_mix_pre"], small["norm_mix_post"], small["ret_gn_gain"]
    g3, g4, d_skip = small["norm_mlp_pre"], small["norm_mlp_post"], small["ssm_d"]

    rope = _rope_tables(L)
    consts = _ret_consts()

    disc_in = (small["ssm_lambda_re"][0], small["ssm_lambda_im"][0], small["ssm_log_dt"][0] + zero,
               small["ssm_b_re"][0], small["ssm_b_im"][0])
    (ar, ai, bbr, bbi), disc_vjp = jax.vjp(_discretize, *disc_in)
    bmat = _block_diag_in(bbr, bbi).astype(_BF)
    cmat = _block_diag_out(small["ssm_c_re"][0], small["ssm_c_im"][0]).astype(_BF)
    seg = tb // SUBLANES
    tab_f, pw_f = _scan_tables(ar, ai, seg, False)
    tab_r, pw_r = _scan_tables(ar, ai, seg, True)

    (w_in_t,) = weights("in", pw_r)
    h1, q, k, v, gate, u, cosf, sinf = _inproj_fwd(x, g1, w_in_t, rope, tm)
    o, y_ret, r_prev = _retention_fwd(q, k, v, gate, ggn, consts)
    s, xs, ent = _s5_fwd(u, bmat, cmat, tab_f, pw_f, d_skip, tb)
    w_glu, w_out = weights("mix", s)
    ys, glu, cat, mix, x2 = _mixout_fwd(s, y_ret, x, w_glu, w_out, g2, tm)
    w_ff1, w_ff2 = weights("mlp", x2)
    h3, f1 = _ff1_fwd(x2, g3, w_ff1, tm)
    dy, dm, dg4, sq = _ff2_loss(f1, x2, tgt, g4, w_ff2, min(2 * tm, L))

    df1, dw_ff2 = _ff2_bwd(dm, f1, w_ff2, min(1024, L), 1024)
    dx2, dmix, dg3, dg2 = _ff1_bwd(df1, w_ff1, x2, mix, dy, g3, g2, min(2 * tm, L))
    dw_ff1 = _matmul_tn(h3, df1, tk, FF1_COLS, "dw_ff1", slots=True)
    zero = emit({"w_ff1": dw_ff1, "w_ff2": dw_ff2})
    dglu, ds, dgate, do, dggn = _mixout_bwd(dmix, w_out, w_glu, glu, s, o, gate, ggn if zero is None else ggn + zero, tm)
    dw_out = _matmul_tn(cat, dmix, tk, 1024, "dw_out")
    dw_glu = _matmul_tn(ys, dglu, tk, 1024, "dw_glu")
    zero = emit({"w_glu": dw_glu, "w_out": dw_out})
    du, dbmat, dcmat, da8, dd = _s5_bwd(u, ds, xs, ent, bmat, cmat, tab_r, pw_r,
                                        d_skip if zero is None else d_skip + zero, tb)
    dq, dk, dv = _retention_bwd(q, k, v, do, r_prev, consts, cosf, sinf)
    pieces = (dq, dk, dv, dgate, du)
    dw_in = jnp.concatenate([_matmul_tn(h1, p, tk, RET_W, "dw_in_%d" % j) for j, p in enumerate(pieces)], axis=1)
    zero = emit({"w_in": dw_in})

    da = jnp.sum(da8, axis=1)
    dar = da[:, :KB_STATES].reshape(N_GROUP, N_STATE)
    dai = da[:, KB_STATES:].reshape(N_GROUP, N_STATE)
    dbr, dbi = _block_diag_in_t(dbmat)
    dlre, dlim, dldt, dbre, dbim = disc_vjp((dar, dai, dbr, dbi))
    dcre, dcim = _block_diag_out_t(dcmat)

    zero2 = emit_small({
        "norm_mix_post": dg2, "ret_gn_gain": dggn,
        "ssm_lambda_re": dlre[None], "ssm_lambda_im": dlim[None], "ssm_log_dt": dldt[None],
        "ssm_b_re": dbre[None], "ssm_b_im": dbim[None], "ssm_c_re": dcre[None], "ssm_c_im": dcim[None],
        "ssm_d": dd, "norm_mlp_pre": dg3, "norm_mlp_post": dg4,
    }, sq)
    for z in (zero, zero2):
        g1 = g1 if z is None else g1 + z
    gx, dg1 = _inproj_bwd(pieces, w_in_t, x, dx2, g1, tm)
    return gx, dg1


BIG_SHAPES = {"w_in": (D_MODEL, IN_COLS // N_DEV), "w_glu": (SSM_W, 2 * SSM_W // N_DEV), "w_out": (D_MODEL // N_DEV, D_MODEL),
              "w_ff1": (D_MODEL, FF1_COLS), "w_ff2": (D_FF // N_DEV, D_MODEL)}
BIG_NAMES = ("w_in", "w_glu", "w_out", "w_ff1", "w_ff2")


def _cols_from_slots(g):
    return jnp.transpose(g, (1, 0, 2)).reshape(g.shape[1], N_DEV * g.shape[2])


def _cols_to_slots(dw):
    r, cols = dw.shape
    return jnp.transpose(dw.reshape(r, N_DEV, cols // N_DEV), (1, 0, 2))


WEIGHT_GROUPS = {"in": ("w_in",), "mix": ("w_glu", "w_out"), "mlp": ("w_ff1", "w_ff2")}


def _weight_from_slots(name, g):
    if name == "w_glu":
        return _cols_from_slots(g)
    if name == "w_ff1":
        return g
    return g.reshape(N_DEV * g.shape[1], g.shape[2])


def _grad_slots(name, dw):
    if name in ("w_in", "w_glu"):
        return _cols_to_slots(dw)
    if name == "w_ff1":
        return dw
    return dw.reshape((N_DEV,) + BIG_SHAPES[name])


PIECE_ROWS = 8


def _small_layout(shapes):
    off, rows = {}, 0
    for n in SMALL_NAMES:
        off[n] = rows
        rows += -(-math.prod(shapes[n]) // (PIECE_ROWS * LANES)) * PIECE_ROWS
    return off, rows, rows + PIECE_ROWS


def _pack_small(vals, shapes, last=None):
    parts = []
    for n in SMALL_NAMES:
        flat = vals[n].reshape(-1).astype(_F32)
        pad = -flat.shape[0] % (PIECE_ROWS * LANES)
        if pad:
            flat = jnp.concatenate([flat, jnp.zeros((pad,), _F32)])
        parts.append(flat.reshape(-1, LANES))
    parts.append(jnp.zeros((PIECE_ROWS, LANES), _F32) if last is None else last)
    return jnp.concatenate(parts, axis=0)


def _unpack_small(buf, shapes):
    off, _, _ = _small_layout(shapes)
    out = {}
    for n in SMALL_NAMES:
        size = math.prod(shapes[n])
        rows = -(-size // LANES)
        out[n] = buf[off[n]:off[n] + rows].reshape(-1)[:size].reshape(shapes[n])
    return out


WEIGHT_NAMES = ('norm_mix_pre', 'norm_mix_post', 'w_in', 'ret_gn_gain', 'ssm_lambda_re', 'ssm_lambda_im', 'ssm_log_dt',
                'ssm_b_re', 'ssm_b_im', 'ssm_c_re', 'ssm_c_im', 'ssm_d', 'w_glu', 'w_out', 'norm_mlp_pre',
                'norm_mlp_post', 'w_ff1', 'w_ff2')


def kernel(x, norm_mix_pre, norm_mix_post, w_in, ret_gn_gain, ssm_lambda_re, ssm_lambda_im, ssm_log_dt, ssm_b_re, ssm_b_im, ssm_c_re, ssm_c_im, ssm_d, w_glu, w_out, norm_mlp_pre, norm_mlp_post, w_ff1, w_ff2, loss_target, m_norm_mix_pre, m_norm_mix_post, m_w_in, m_ret_gn_gain, m_ssm_lambda_re, m_ssm_lambda_im, m_ssm_log_dt, m_ssm_b_re, m_ssm_b_im, m_ssm_c_re, m_ssm_c_im, m_ssm_d, m_w_glu, m_w_out, m_norm_mlp_pre, m_norm_mlp_post, m_w_ff1, m_w_ff2, v_norm_mix_pre, v_norm_mix_post, v_w_in, v_ret_gn_gain, v_ssm_lambda_re, v_ssm_lambda_im, v_ssm_log_dt, v_ssm_b_re, v_ssm_b_im, v_ssm_c_re, v_ssm_c_im, v_ssm_d, v_w_glu, v_w_out, v_norm_mlp_pre, v_norm_mlp_post, v_w_ff1, v_w_ff2):
    args = dict(locals())
    w = {n: args[n] for n in WEIGHT_NAMES}
    m = {n: args["m_" + n] for n in WEIGHT_NAMES}
    v = {n: args["v_" + n] for n in WEIGHT_NAMES}
    L = x.shape[1]
    tm = min(256, L)
    tk = min(2048, L)
    tb = min(512, L)

    gathers, zero = {}, jnp.zeros((), _F32)
    for group, names in WEIGHT_GROUPS.items():
        blocks = [(w[n][0].T if n == "w_in" else w[n][0]).astype(_BF) for n in names]
        blocks[0] = blocks[0] + zero.astype(_BF)
        gathers[group] = _split_start(blocks, [_landing(b) for b in blocks], True, "weights_start_" + group)
        zero = gathers[group][4][0, 0]

    def weights(group, after):
        landed = _split_wait(*gathers[group][:4], after, True, "weights_wait_" + group)
        return [_weight_from_slots(n, g) for n, g in zip(WEIGHT_GROUPS[group], landed)]

    in_flight = []

    def emit(dws):
        names = sorted(dws)
        srcs = [_grad_slots(n, dws[n]) for n in names]
        lands = [_landing(lax.dynamic_index_in_dim(t, _my_index(), 0, keepdims=False)) for t in srcs]
        started = _split_start(srcs, lands, False, "grads_start_" + "_".join(names))
        in_flight.append((names, started))
        return started[4][0, 0]

    shapes = {n: w[n].shape for n in SMALL_NAMES}
    first_piece = {SMALL_NAMES[0]: jnp.zeros(shapes[SMALL_NAMES[0]], _F32)}
    small_flight = []

    def emit_small(gs, sq):
        loss_rows = jnp.broadcast_to(0.5 / D_MODEL * jnp.sum(sq), (PIECE_ROWS, LANES)).astype(_F32)
        buf = _pack_small({**first_piece, **gs}, shapes, loss_rows)
        small_flight.append(_split_start([buf], [_landing(buf)], True, "small_grads_start"))
        return small_flight[0][4][0, 0]

    small_w = {n: w[n] for n in SMALL_NAMES}
    gx, dg1 = _local_grads(x[0], loss_target[0], small_w, weights, emit, emit_small, tm, tk, tb, zero=zero)
    last_buf = dg1.reshape(PIECE_ROWS, LANES)
    last_started = _split_start([last_buf], [_landing(last_buf)], True, "last_grad_start")

    grads, delta, new_m, new_v = {}, {}, {}, {}
    after = last_started[4]
    for names, started in in_flight:
        landed = _split_wait(*started[:4], after, False, "grads_wait_" + "_".join(names))
        for n, parts in zip(names, landed):
            res = _sum_adamw(parts, w[n][0], m[n][0], v[n][0], min(256, BIG_SHAPES[n][0]), "adamw_" + n)
            grads[n], delta[n], new_m[n], new_v[n] = (t[None] for t in res)
        after = res[1]
    small_parts = _split_wait(*small_flight[0][:4], after, True, "small_grads_wait")[0]
    last_parts = _split_wait(*last_started[:4], small_parts, True, "last_grad_wait")[0]
    small_parts = lax.dynamic_update_slice(small_parts, last_parts, (0, 0, 0))
    sw, sm, sv = _pack_small(w, shapes), _pack_small(m, shapes), _pack_small(v, shapes)
    res = _sum_adamw(small_parts, sw, sm, sv, sw.shape[0], "adamw_small")
    for dst, buf in zip((grads, delta, new_m, new_v), res):
        dst.update(_unpack_small(buf, shapes))
    _, loss_at, _ = _small_layout(shapes)
    loss = res[0][loss_at, 0]

    return (loss, gx[None], *[grads[n] for n in WEIGHT_NAMES], *[delta[n] for n in WEIGHT_NAMES],
            *[new_m[n] for n in WEIGHT_NAMES], *[new_v[n] for n in WEIGHT_NAMES])
```

```python
import math

import jax
import jax.numpy as jnp
from jax import lax
from jax.experimental import pallas as pl
from jax.experimental.pallas import tpu as pltpu

_BF = jnp.bfloat16
_F32 = jnp.float32

D_MODEL = 1024
RET_W = 512
N_HEAD = 4
HEAD_D = 128
CHUNK = 256
ROPE_CHUNK = 128
SSM_W = 512
SSM_GC = 16
N_GROUP = 32
N_STATE = 64
GROUPS_PER_KB = 8
N_KB = 4
KB_STATES = GROUPS_PER_KB * N_STATE
D_FF = 4096
IN_COLS = 2560
NORM_EPS = 1e-6
ROPE_BASE = 10000.0
N_DEV = 8

ADAM_LR = 0.001
ADAM_B1 = 0.9
ADAM_B2 = 0.999
ADAM_EPS = 1e-08
ADAM_WD = 0.01
ADAM_STEP = 10

SUBLANES = 8
LANES = 128
VMEM_LIMIT = 52 * 1024 * 1024
RET_STEP_CHUNKS = 2
KB_PER_STEP = 2
SCAN_UNROLL = 2

MESH = pl.DeviceIdType.MESH


def _params(*sem):
    return pltpu.CompilerParams(dimension_semantics=sem, vmem_limit_bytes=VMEM_LIMIT)


def _dot(a, b):
    return jnp.dot(a, b, preferred_element_type=_F32)


def _dot_nt(a, b):
    return lax.dot_general(a, b, (((1,), (1,)), ((), ())), preferred_element_type=_F32)


def _dot_tn(a, b):
    return lax.dot_general(a, b, (((0,), (0,)), ((), ())), preferred_element_type=_F32)


def _rms_r(z):
    return lax.rsqrt(jnp.mean(z * z, axis=-1, keepdims=True) + NORM_EPS)


def _rms_bwd(z, g, dn):
    r = _rms_r(z)
    t = dn * g
    dz = r * t - z * (r * r * r * jnp.mean(t * z, axis=-1, keepdims=True))
    return dz, dn * z * r


def _rope(t, cs, sn):
    return t * cs + pltpu.roll(t, HEAD_D // 2, 1) * sn


def _rope_t(t, cs, sn):
    return t * cs - pltpu.roll(t, HEAD_D // 2, 1) * sn


def _sigmoid(z):
    return 1.0 / (1.0 + jnp.exp(-z))


_GELU_C = math.sqrt(2.0 / math.pi)


def _gelu(z):
    return 0.5 * z * (1.0 + jnp.tanh(_GELU_C * (z + 0.044715 * z * z * z)))


def _gelu_grad(z):
    th = jnp.tanh(_GELU_C * (z + 0.044715 * z * z * z))
    return 0.5 * (1.0 + th) + 0.5 * z * (1.0 - th * th) * _GELU_C * (1.0 + 3 * 0.044715 * z * z)


ROW_CHUNK = 256


def _row_chunks(tm):
    return [pl.ds(i, min(ROW_CHUNK, tm)) for i in range(0, tm, ROW_CHUNK)]


def _row_spec(tm, n):
    return pl.BlockSpec((tm, n), lambda i: (i, 0))


def _full_spec(shape):
    nd = len(shape)
    return pl.BlockSpec(shape, lambda *_: (0,) * nd)


def _weight_spec(shape):
    nd = len(shape)
    return pl.BlockSpec(shape, lambda *_: (0,) * nd, pipeline_mode=pl.Buffered(1))


def _rope_tables(L):
    half = HEAD_D // 2
    inv_freq = ROPE_BASE ** (-jnp.arange(half, dtype=_F32) / half)
    twice = lambda t: jnp.concatenate([t, t], axis=-1)
    off = jnp.arange(ROPE_CHUNK, dtype=_F32)[:, None] * inv_freq[None, :]
    start = (ROPE_CHUNK * jnp.arange(L // ROPE_CHUNK, dtype=_F32))[:, None] * inv_freq[None, :]
    return (twice(jnp.cos(off)), twice(jnp.sin(off)),
            twice(jnp.cos(start))[:, None, :], twice(jnp.sin(start))[:, None, :])


def _inproj_fwd(x, g1, w_in_t, rope, tm):
    L = x.shape[0]
    n_chunks = tm // ROPE_CHUNK

    def body(x_ref, g_ref, w_ref, co_ref, so_ref, cs_ref, ss_ref, h_ref, q_ref, k_ref, v_ref, gate_ref, u_ref,
             cos_ref, sin_ref):
        xv = x_ref[...]
        h = (xv * _rms_r(xv) * g_ref[...]).astype(_BF)
        h_ref[...] = h
        proj = _dot_nt(h, w_ref[...])
        lane = lax.broadcasted_iota(jnp.int32, (ROPE_CHUNK, HEAD_D), 1)
        sign = jnp.where(lane < HEAD_D // 2, -1.0, 1.0)
        co, so = co_ref[...], so_ref[...]
        for c in range(n_chunks):
            chunk = pl.program_id(0) * n_chunks + c
            cst, sst = cs_ref[chunk], ss_ref[chunk]
            rows = slice(c * ROPE_CHUNK, (c + 1) * ROPE_CHUNK)
            cs = co * cst - so * sst
            sn = (so * cst + co * sst) * sign
            cos_ref[rows, :] = cs
            sin_ref[rows, :] = sn
            for hh in range(N_HEAD):
                lo = hh * HEAD_D
                q_ref[rows, lo:lo + HEAD_D] = _rope(proj[rows, lo:lo + HEAD_D], cs, sn).astype(_BF)
                kh = _rope(proj[rows, RET_W + lo:RET_W + lo + HEAD_D], cs, sn) * (HEAD_D ** -0.5)
                k_ref[rows, lo:lo + HEAD_D] = kh.astype(_BF)
        v_ref[...] = proj[:, 2 * RET_W:3 * RET_W].astype(_BF)
        gate_ref[...] = proj[:, 3 * RET_W:4 * RET_W]
        u_ref[...] = proj[:, 4 * RET_W:]

    nc = L // ROPE_CHUNK
    return pl.pallas_call(
        body, name="inproj_fwd", grid=(L // tm,),
        in_specs=[_row_spec(tm, D_MODEL), _full_spec((1, D_MODEL)), _weight_spec((IN_COLS, D_MODEL)),
                  _full_spec((ROPE_CHUNK, HEAD_D)), _full_spec((ROPE_CHUNK, HEAD_D)),
                  _full_spec((nc, 1, HEAD_D)), _full_spec((nc, 1, HEAD_D))],
        out_specs=[_row_spec(tm, D_MODEL)] + [_row_spec(tm, RET_W)] * 5 + [_row_spec(tm, HEAD_D)] * 2,
        out_shape=[jax.ShapeDtypeStruct((L, D_MODEL), _BF)] + [jax.ShapeDtypeStruct((L, RET_W), _BF)] * 3
        + [jax.ShapeDtypeStruct((L, RET_W), _F32)] * 2 + [jax.ShapeDtypeStruct((L, HEAD_D), _F32)] * 2,
        compiler_params=_params("parallel"),
    )(x, g1, w_in_t, *rope)


def _ret_consts():
    lg = jnp.log(1.0 - jnp.exp(jnp.linspace(math.log(1.0 / 32), math.log(1.0 / 512), N_HEAD))).astype(_F32)
    idx = jnp.arange(CHUNK, dtype=_F32)
    diff = idx[:, None] - idx[None, :]
    decay = jnp.where(diff[None] >= 0, jnp.exp(jnp.maximum(diff, 0.0)[None] * lg[:, None, None]), 0.0)
    zeta = jnp.exp((CHUNK - 1 - idx)[None, :] * lg[:, None])
    xi = jnp.exp((idx + 1.0)[None, :] * lg[:, None])
    gc = jnp.exp(CHUNK * lg)
    wide = lambda t: jnp.broadcast_to(t[:, :, None], (N_HEAD, CHUNK, HEAD_D)).astype(_F32)
    gcw = jnp.broadcast_to(gc[:, None, None], (N_HEAD, SUBLANES, HEAD_D)).astype(_F32)
    return decay.astype(_F32), wide(xi), wide(zeta), gcw


def _head_specs():
    wide = _full_spec((N_HEAD, CHUNK, HEAD_D))
    return [_full_spec((N_HEAD, CHUNK, CHUNK)), wide, wide, _full_spec((N_HEAD, SUBLANES, HEAD_D))]


def _retention_fwd(q, k, v, gate, ggn, consts):
    L = q.shape[0]
    nc = L // CHUNK
    cps = math.gcd(RET_STEP_CHUNKS, nc)
    blk = pl.BlockSpec((cps * CHUNK, RET_W), lambda n: (n, 0))

    def body(q_ref, k_ref, v_ref, gate_ref, ggn_ref, dm_ref, xi_ref, zeta_ref, gc_ref,
             o_ref, y_ref, rp_ref, r_scr):
        @pl.when(pl.program_id(0) == 0)
        def _():
            r_scr[...] = jnp.zeros_like(r_scr)

        for hh in range(N_HEAD):
            cols = slice(hh * HEAD_D, (hh + 1) * HEAD_D)
            state = r_scr[hh]
            for c in range(cps):
                rows = slice(c * CHUNK, (c + 1) * CHUNK)
                qv, kv, vv = q_ref[rows, cols], k_ref[rows, cols], v_ref[rows, cols]
                s = _dot_nt(qv, kv) * dm_ref[hh]
                o = _dot(s.astype(_BF), vv) + _dot(qv, state.astype(_BF)) * xi_ref[hh]
                o_ref[rows, cols] = o
                rp_ref[hh, c] = state
                vz = (vv.astype(_F32) * zeta_ref[hh]).astype(_BF)
                state = gc_ref[hh, 0:1, :] * state + _dot_tn(kv, vz)
                dlt = o - jnp.mean(o, axis=-1, keepdims=True)
                on = dlt * lax.rsqrt(jnp.mean(dlt * dlt, axis=-1, keepdims=True) + NORM_EPS)
                gt = gate_ref[rows, cols]
                y_ref[rows, cols] = (gt * _sigmoid(gt) * (on * ggn_ref[:, cols])).astype(_BF)
            r_scr[hh] = state

    return pl.pallas_call(
        body, name="retention_fwd", grid=(nc // cps,),
        in_specs=[blk, blk, blk, blk, _full_spec((1, RET_W))] + _head_specs(),
        out_specs=[blk, blk, pl.BlockSpec((N_HEAD, cps, HEAD_D, HEAD_D), lambda n: (0, n, 0, 0))],
        out_shape=[jax.ShapeDtypeStruct((L, RET_W), _F32), jax.ShapeDtypeStruct((L, RET_W), _BF),
                   jax.ShapeDtypeStruct((N_HEAD, nc, HEAD_D, HEAD_D), _F32)],
        scratch_shapes=[pltpu.VMEM((N_HEAD, HEAD_D, HEAD_D), _F32)],
        compiler_params=_params("arbitrary"),
    )(q, k, v, gate, ggn, *consts)


def _rows_to_segments(dst_scr, src_ref, seg):
    for g in range(dst_scr.shape[0]):
        for j in range(SUBLANES):
            dst_scr[g, pl.ds(j, seg, stride=SUBLANES), :] = src_ref[pl.ds(j * seg, seg), g * LANES:(g + 1) * LANES]


def _segments_to_rows(dst_ref, src_scr, seg):
    for g in range(src_scr.shape[0]):
        for j in range(SUBLANES):
            dst_ref[pl.ds(j * seg, seg), g * LANES:(g + 1) * LANES] = src_scr[g, pl.ds(j, seg, stride=SUBLANES), :]


def _scan_segments(x_ref, tab_ref, pw_ref, carry_ref, seg, reverse, entry_ref=None, fwd_ref=None, fwd_entry_ref=None,
                   da_ref=None):
    G = x_ref.shape[0]
    W = KB_STATES
    re, im = pl.ds(0, W), pl.ds(W, W)
    row_id = lax.broadcasted_iota(jnp.int32, (SUBLANES, W), 0)
    edge_in = (row_id == SUBLANES - 1) if reverse else (row_id == 0)
    edge_out = 0 if reverse else SUBLANES - 1
    a_tab = [(tab_ref[g, 0], tab_ref[g, 1]) for g in range(G)]

    def local(i, st):
        r = (seg - 1 - i) if reverse else i
        out = []
        for g in range(G):
            (ar, ai), (sr, si) = a_tab[g], st[g]
            nr = ar * sr - ai * si + x_ref[g, r, :, re]
            ni = ar * si + ai * sr + x_ref[g, r, :, im]
            x_ref[g, r, :, re] = nr
            x_ref[g, r, :, im] = ni
            out.append((nr, ni))
        return tuple(out)

    zero = jnp.zeros((SUBLANES, W), _F32)
    ends = lax.fori_loop(0, seg, local, tuple((zero, zero) for _ in range(G)), unroll=SCAN_UNROLL)

    entry = []
    shift = (SUBLANES - 1) if reverse else 1
    for g in range(G):
        er, ei = ends[g]
        fr = jnp.where(edge_in, carry_ref[g, :, re], pltpu.roll(er, shift, 0))
        fi = jnp.where(edge_in, carry_ref[g, :, im], pltpu.roll(ei, shift, 0))
        for j, dist in enumerate((1, 2, 4)):
            pr, pi = tab_ref[g, 2 + 2 * j], tab_ref[g, 3 + 2 * j]
            sh = (SUBLANES - dist) if reverse else dist
            sr, si = pltpu.roll(fr, sh, 0), pltpu.roll(fi, sh, 0)
            fr, fi = fr + pr * sr - pi * si, fi + pr * si + pi * sr
        br, bi = tab_ref[g, 8], tab_ref[g, 9]
        outr = br * fr - bi * fi + er
        outi = br * fi + bi * fr + ei
        carry_ref[g, :, re] = jnp.broadcast_to(outr[edge_out:edge_out + 1, :], (SUBLANES, W))
        carry_ref[g, :, im] = jnp.broadcast_to(outi[edge_out:edge_out + 1, :], (SUBLANES, W))
        entry.append((fr, fi))
        if entry_ref is not None:
            entry_ref[g, :, re] = fr
            entry_ref[g, :, im] = fi

    add_da = da_ref is not None

    def fix(r, st, first=False):
        out = []
        for g in range(G):
            fr, fi = entry[g]
            pwr, pwi = pw_ref[g, r, :, re], pw_ref[g, r, :, im]
            xr = x_ref[g, r, :, re] + (pwr * fr - pwi * fi)
            xi = x_ref[g, r, :, im] + (pwr * fi + pwi * fr)
            x_ref[g, r, :, re] = xr
            x_ref[g, r, :, im] = xi
            if add_da:
                prev = fwd_entry_ref.at[g] if first else fwd_ref.at[g, r - 1]
                xpr, xpi = prev[:, re], prev[:, im]
                out.append((st[g][0] + (xr * xpr + xi * xpi), st[g][1] + (xi * xpr - xr * xpi)))
            else:
                out.append(st[g])
        return tuple(out)

    if add_da:
        st = fix(0, tuple((zero, zero) for _ in range(G)), first=True)
        st = lax.fori_loop(1, seg, fix, st, unroll=SCAN_UNROLL)
        for g in range(G):
            da_ref[g, :, re] += st[g][0]
            da_ref[g, :, im] += st[g][1]
    else:
        lax.fori_loop(0, seg, fix, tuple((zero[0:1, 0:LANES],) for _ in range(G)), unroll=SCAN_UNROLL)


def _s5_specs(seg, time=lambda t: t):
    G = KB_PER_STEP
    return dict(
        x=pl.BlockSpec((G, seg, SUBLANES, 2 * KB_STATES), lambda kb, t: (kb, time(t), 0, 0)),
        ent=pl.BlockSpec((G, 1, SUBLANES, 2 * KB_STATES), lambda kb, t: (kb, time(t), 0, 0)),
        b=pl.BlockSpec((G, LANES, 2 * KB_STATES), lambda kb, t: (kb, 0, 0)),
        c=pl.BlockSpec((G, 2 * KB_STATES, LANES), lambda kb, t: (kb, 0, 0)),
        tab=pl.BlockSpec((G, 10, SUBLANES, KB_STATES), lambda kb, t: (kb, 0, 0, 0)),
        pw=pl.BlockSpec((G, seg, 1, 2 * KB_STATES), lambda kb, t: (kb, 0, 0, 0)),
        d=pl.BlockSpec((1, G * LANES), lambda kb, t: (0, kb)),
    )


def _s5_fwd(u, bmat, cmat, tab_f, pw_f, d_skip, tb):
    L = u.shape[0]
    nt = L // tb
    seg = tb // SUBLANES
    G = KB_PER_STEP
    ucol = pl.BlockSpec((tb, G * LANES), lambda kb, t: (t, kb))
    sp = _s5_specs(seg)

    def body(u_ref, b_ref, c_ref, tab_ref, pw_ref, d_ref, s_ref, x_ref, ent_ref, up_scr, y_scr, carry_scr):
        @pl.when(pl.program_id(1) == 0)
        def _():
            carry_scr[...] = jnp.zeros_like(carry_scr)

        _rows_to_segments(up_scr, u_ref, seg)
        for g in range(G):
            x_ref[g] = _dot(up_scr[g].astype(_BF), b_ref[g]).reshape(seg, SUBLANES, 2 * KB_STATES)
        _scan_segments(x_ref, tab_ref, pw_ref, carry_scr, seg, reverse=False, entry_ref=ent_ref.at[:, 0])
        for g in range(G):
            y = _dot(x_ref[g].reshape(tb, 2 * KB_STATES).astype(_BF), c_ref[g])
            y_scr[g] = y + d_ref[:, g * LANES:(g + 1) * LANES] * up_scr[g]
        _segments_to_rows(s_ref, y_scr, seg)

    return pl.pallas_call(
        body, name="s5_fwd", grid=(N_KB // G, nt),
        in_specs=[ucol, sp["b"], sp["c"], sp["tab"], sp["pw"], sp["d"]],
        out_specs=[ucol, sp["x"], sp["ent"]],
        out_shape=[jax.ShapeDtypeStruct((L, SSM_W), _F32),
                   jax.ShapeDtypeStruct((N_KB, L // SUBLANES, SUBLANES, 2 * KB_STATES), _F32),
                   jax.ShapeDtypeStruct((N_KB, nt, SUBLANES, 2 * KB_STATES), _F32)],
        scratch_shapes=[pltpu.VMEM((G, tb, LANES), _F32)] * 2 + [pltpu.VMEM((G, SUBLANES, 2 * KB_STATES), _F32)],
        compiler_params=_params("parallel", "arbitrary"),
    )(u, bmat, cmat, tab_f, pw_f, d_skip)


def _mixout_fwd(s, y_ret, x, w_glu, w_out, g2, tm):
    L = s.shape[0]

    def body(s_ref, yr_ref, x_ref, wg_ref, wo_ref, g_ref, ys_ref, glu_ref, cat_ref, mix_ref, x2_ref):
        ys = _gelu(s_ref[...]).astype(_BF)
        ys_ref[...] = ys
        glu = _dot(ys, wg_ref[...])
        glu_ref[...] = glu
        cat_ref[:, :RET_W] = yr_ref[...]
        cat_ref[:, RET_W:] = (glu[:, :SSM_W] * _sigmoid(glu[:, SSM_W:])).astype(_BF)
        mix = _dot(cat_ref[...], wo_ref[...])
        mix_ref[...] = mix
        x2_ref[...] = x_ref[...] + mix * _rms_r(mix) * g_ref[...]

    return pl.pallas_call(
        body, name="mixout_fwd", grid=(L // tm,),
        in_specs=[_row_spec(tm, SSM_W), _row_spec(tm, RET_W), _row_spec(tm, D_MODEL),
                  _weight_spec((SSM_W, 2 * SSM_W)), _weight_spec((D_MODEL, D_MODEL)), _full_spec((1, D_MODEL))],
        out_specs=[_row_spec(tm, SSM_W), _row_spec(tm, 2 * SSM_W), _row_spec(tm, D_MODEL),
                   _row_spec(tm, D_MODEL), _row_spec(tm, D_MODEL)],
        out_shape=[jax.ShapeDtypeStruct((L, SSM_W), _BF), jax.ShapeDtypeStruct((L, 2 * SSM_W), _F32),
                   jax.ShapeDtypeStruct((L, D_MODEL), _BF), jax.ShapeDtypeStruct((L, D_MODEL), _F32),
                   jax.ShapeDtypeStruct((L, D_MODEL), _F32)],
        compiler_params=_params("parallel"),
    )(s, y_ret, x, w_glu, w_out, g2)


FF1_COLS = D_FF // N_DEV


def _ff1_fwd(x2, g3, w1, tm):
    L = x2.shape[0]

    def body(x_ref, g_ref, w_ref, h_ref, f_ref):
        xv = x_ref[...]
        h = (xv * _rms_r(xv) * g_ref[...]).astype(_BF)
        h_ref[...] = h
        for j in range(N_DEV):
            f_ref[:, j * FF1_COLS:(j + 1) * FF1_COLS] = _dot(h, w_ref[j])

    return pl.pallas_call(
        body, name="ff1_fwd", grid=(L // tm,),
        in_specs=[_row_spec(tm, D_MODEL), _full_spec((1, D_MODEL)), _weight_spec((N_DEV, D_MODEL, FF1_COLS))],
        out_specs=[_row_spec(tm, D_MODEL), _row_spec(tm, D_FF)],
        out_shape=[jax.ShapeDtypeStruct((L, D_MODEL), _BF), jax.ShapeDtypeStruct((L, D_FF), _F32)],
        compiler_params=_params("parallel"),
    )(x2, g3, w1)


def _ff2_loss(f1, x2, tgt, g4, w2, tm):
    L = f1.shape[0]

    def body(f_ref, x_ref, t_ref, g_ref, w_ref, dy_ref, dm_ref, dg_ref, ls_ref):
        @pl.when(pl.program_id(0) == 0)
        def _():
            dg_ref[...] = jnp.zeros_like(dg_ref)
            ls_ref[...] = jnp.zeros_like(ls_ref)

        g = g_ref[...]
        for rows in _row_chunks(tm):
            rl = jnp.maximum(f_ref[rows, :], 0.0)
            m = _dot((rl * rl).astype(_BF), w_ref[...])
            y = x_ref[rows, :] + m * _rms_r(m) * g
            err = y - t_ref[rows, :]
            ls_ref[...] += jnp.sum(err * err, axis=0, keepdims=True)
            dy = err * (1.0 / D_MODEL)
            dy_ref[rows, :] = dy
            dm, dgr = _rms_bwd(m, g, dy)
            dm_ref[rows, :] = dm.astype(_BF)
            dg_ref[...] += jnp.sum(dgr, axis=0, keepdims=True)

    return pl.pallas_call(
        body, name="ff2_loss", grid=(L // tm,),
        in_specs=[_row_spec(tm, D_FF), _row_spec(tm, D_MODEL), _row_spec(tm, D_MODEL),
                  _full_spec((1, D_MODEL)), _weight_spec((D_FF, D_MODEL))],
        out_specs=[_row_spec(tm, D_MODEL), _row_spec(tm, D_MODEL), _full_spec((1, D_MODEL)), _full_spec((1, D_MODEL))],
        out_shape=[jax.ShapeDtypeStruct((L, D_MODEL), _F32), jax.ShapeDtypeStruct((L, D_MODEL), _BF),
                   jax.ShapeDtypeStruct((1, D_MODEL), _F32), jax.ShapeDtypeStruct((1, D_MODEL), _F32)],
        compiler_params=_params("arbitrary"),
    )(f1, x2, tgt, g4, w2)


def _ff2_bwd(dm, f1, w2, tm, tn):
    L = dm.shape[0]
    last = L // tm - 1

    def body(dm_ref, f_ref, w_ref, df_ref, dw_ref, acc):
        @pl.when(pl.program_id(1) == 0)
        def _():
            acc[...] = jnp.zeros_like(acc)

        dmv = dm_ref[...]
        rl = jnp.maximum(f_ref[...], 0.0)
        df_ref[...] = (_dot_nt(dmv, w_ref[...]) * (2.0 * rl)).astype(_BF)
        acc[...] += _dot_tn((rl * rl).astype(_BF), dmv)

        @pl.when(pl.program_id(1) == last)
        def _():
            dw_ref[...] = acc[...].astype(_BF)

    return pl.pallas_call(
        body, name="ff2_bwd", grid=(D_FF // tn, L // tm),
        in_specs=[pl.BlockSpec((tm, D_MODEL), lambda j, i: (i, 0)), pl.BlockSpec((tm, tn), lambda j, i: (i, j)),
                  pl.BlockSpec((tn, D_MODEL), lambda j, i: (j, 0))],
        out_specs=[pl.BlockSpec((tm, tn), lambda j, i: (i, j)), pl.BlockSpec((tn, D_MODEL), lambda j, i: (j, 0))],
        out_shape=[jax.ShapeDtypeStruct((L, D_FF), _BF), jax.ShapeDtypeStruct((D_FF, D_MODEL), _BF)],
        scratch_shapes=[pltpu.VMEM((tn, D_MODEL), _F32)],
        compiler_params=_params("parallel", "arbitrary"),
    )(dm, f1, w2)


def _ff1_bwd(df1, w1, x2, mix, dy, g3, g2, tm):
    L = df1.shape[0]

    def body(df_ref, w_ref, x2_ref, mix_ref, dy_ref, g3_ref, g2_ref, dx2_ref, dmix_ref, dg3_ref, dg2_ref):
        @pl.when(pl.program_id(0) == 0)
        def _():
            dg3_ref[...] = jnp.zeros_like(dg3_ref)
            dg2_ref[...] = jnp.zeros_like(dg2_ref)

        for rows in _row_chunks(tm):
            dh = _dot_nt(df_ref[rows, 0:FF1_COLS], w_ref[0])
            for j in range(1, N_DEV):
                dh = dh + _dot_nt(df_ref[rows, j * FF1_COLS:(j + 1) * FF1_COLS], w_ref[j])
            dz, dgr = _rms_bwd(x2_ref[rows, :], g3_ref[...], dh)
            dg3_ref[...] += jnp.sum(dgr, axis=0, keepdims=True)
            dx2 = dy_ref[rows, :] + dz
            dx2_ref[rows, :] = dx2
            dmx, dgr2 = _rms_bwd(mix_ref[rows, :], g2_ref[...], dx2)
            dg2_ref[...] += jnp.sum(dgr2, axis=0, keepdims=True)
            dmix_ref[rows, :] = dmx.astype(_BF)

    vec = _full_spec((1, D_MODEL))
    return pl.pallas_call(
        body, name="ff1_bwd", grid=(L // tm,),
        in_specs=[_row_spec(tm, D_FF), _weight_spec((N_DEV, D_MODEL, FF1_COLS)), _row_spec(tm, D_MODEL),
                  _row_spec(tm, D_MODEL), _row_spec(tm, D_MODEL), vec, vec],
        out_specs=[_row_spec(tm, D_MODEL), _row_spec(tm, D_MODEL), vec, vec],
        out_shape=[jax.ShapeDtypeStruct((L, D_MODEL), _F32), jax.ShapeDtypeStruct((L, D_MODEL), _BF),
                   jax.ShapeDtypeStruct((1, D_MODEL), _F32), jax.ShapeDtypeStruct((1, D_MODEL), _F32)],
        compiler_params=_params("arbitrary"),
    )(df1, w1, x2, mix, dy, g3, g2)


def _matmul_tn(a, b, tm, tn, name, slots=False):
    L, K = a.shape
    N = b.shape[1]
    last = L // tm - 1

    def body(a_ref, b_ref, o_ref, acc):
        @pl.when(pl.program_id(1) == 0)
        def _():
            acc[...] = jnp.zeros_like(acc)

        acc[...] += _dot_tn(a_ref[...].astype(_BF), b_ref[...].astype(_BF))

        @pl.when(pl.program_id(1) == last)
        def _():
            if slots:
                o_ref[0] = acc[...].astype(_BF)
            else:
                o_ref[...] = acc[...].astype(_BF)

    if slots:
        out_spec = pl.BlockSpec((1, K, tn), lambda j, i: (j, 0, 0))
        out_shape = jax.ShapeDtypeStruct((N // tn, K, tn), _BF)
    else:
        out_spec = pl.BlockSpec((K, tn), lambda j, i: (0, j))
        out_shape = jax.ShapeDtypeStruct((K, N), _BF)
    return pl.pallas_call(
        body, name=name, grid=(N // tn, L // tm),
        in_specs=[pl.BlockSpec((tm, K), lambda j, i: (i, 0)), pl.BlockSpec((tm, tn), lambda j, i: (i, j))],
        out_specs=out_spec, out_shape=out_shape,
        scratch_shapes=[pltpu.VMEM((K, tn), _F32)],
        compiler_params=_params("parallel", "arbitrary"),
    )(a, b)


def _mixout_bwd(dmix, w_out, w_glu, glu, s, o, gate, ggn, tm):
    L = dmix.shape[0]

    def body(dmix_ref, wo_ref, wg_ref, glu_ref, s_ref, o_ref, gate_ref, ggn_ref,
             dglu_ref, ds_ref, dgate_ref, do_ref, dggn_ref):
        @pl.when(pl.program_id(0) == 0)
        def _():
            dggn_ref[...] = jnp.zeros_like(dggn_ref)

        dcat = _dot_nt(dmix_ref[...], wo_ref[...])
        dy_ret, dy_ssm = dcat[:, :RET_W], dcat[:, RET_W:]
        glu = glu_ref[...]
        ga, sg = glu[:, :SSM_W], _sigmoid(glu[:, SSM_W:])
        dga = (dy_ssm * sg).astype(_BF)
        dgb = (dy_ssm * ga * sg * (1.0 - sg)).astype(_BF)
        dglu_ref[:, :SSM_W] = dga
        dglu_ref[:, SSM_W:] = dgb
        dys = _dot_nt(dga, wg_ref[:, :SSM_W]) + _dot_nt(dgb, wg_ref[:, SSM_W:])
        ds_ref[...] = dys * _gelu_grad(s_ref[...])
        gt = gate_ref[...]
        sgt = _sigmoid(gt)
        ggn = ggn_ref[...]
        for hh in range(N_HEAD):
            cols = slice(hh * HEAD_D, (hh + 1) * HEAD_D)
            ov = o_ref[:, cols]
            dlt = ov - jnp.mean(ov, axis=-1, keepdims=True)
            rstd = lax.rsqrt(jnp.mean(dlt * dlt, axis=-1, keepdims=True) + NORM_EPS)
            on = dlt * rstd
            dyr = dy_ret[:, cols] * (gt[:, cols] * sgt[:, cols])
            dgate_ref[:, cols] = dy_ret[:, cols] * (on * ggn[:, cols]) * (sgt[:, cols] * (1.0 + gt[:, cols] * (1.0 - sgt[:, cols])))
            dggn_ref[:, cols] += jnp.sum(dyr * on, axis=0, keepdims=True)
            don = dyr * ggn[:, cols]
            do = rstd * (don - jnp.mean(don, axis=-1, keepdims=True) - on * jnp.mean(don * on, axis=-1, keepdims=True))
            do_ref[:, cols] = do.astype(_BF)

    return pl.pallas_call(
        body, name="mixout_bwd", grid=(L // tm,),
        in_specs=[_row_spec(tm, D_MODEL), _weight_spec((D_MODEL, D_MODEL)), _weight_spec((SSM_W, 2 * SSM_W)),
                  _row_spec(tm, 2 * SSM_W), _row_spec(tm, SSM_W), _row_spec(tm, RET_W), _row_spec(tm, RET_W),
                  _full_spec((1, RET_W))],
        out_specs=[_row_spec(tm, 2 * SSM_W), _row_spec(tm, SSM_W), _row_spec(tm, RET_W), _row_spec(tm, RET_W),
                   _full_spec((1, RET_W))],
        out_shape=[jax.ShapeDtypeStruct((L, 2 * SSM_W), _BF), jax.ShapeDtypeStruct((L, SSM_W), _F32),
                   jax.ShapeDtypeStruct((L, RET_W), _F32), jax.ShapeDtypeStruct((L, RET_W), _BF),
                   jax.ShapeDtypeStruct((1, RET_W), _F32)],
        compiler_params=_params("arbitrary"),
    )(dmix, w_out, w_glu, glu, s, o, gate, ggn)


def _s5_bwd(u, ds, xs, ent, bmat, cmat, tab_r, pw_r, d_skip, tb):
    L = u.shape[0]
    nt = L // tb
    seg = tb // SUBLANES
    G = KB_PER_STEP
    rcol = pl.BlockSpec((tb, G * LANES), lambda kb, t: (nt - 1 - t, kb))
    sp = _s5_specs(seg, time=lambda t: nt - 1 - t)
    aspec = pl.BlockSpec((G, SUBLANES, 2 * KB_STATES), lambda kb, t: (kb, 0, 0))

    def body(u_ref, ds_ref, x_ref, ent_ref, b_ref, c_ref, tr_ref, pr_ref, d_ref,
             du_ref, db_ref, dc_ref, da_ref, dd_ref, up_scr, dp_scr, g_scr, lc_scr):
        @pl.when(pl.program_id(1) == 0)
        def _():
            lc_scr[...] = jnp.zeros_like(lc_scr)
            db_ref[...] = jnp.zeros_like(db_ref)
            dc_ref[...] = jnp.zeros_like(dc_ref)
            da_ref[...] = jnp.zeros_like(da_ref)
            dd_ref[...] = jnp.zeros_like(dd_ref)

        _rows_to_segments(up_scr, u_ref, seg)
        _rows_to_segments(dp_scr, ds_ref, seg)
        for g in range(G):
            g_scr[g] = _dot_nt(dp_scr[g].astype(_BF), c_ref[g]).reshape(seg, SUBLANES, 2 * KB_STATES)
        _scan_segments(g_scr, tr_ref, pr_ref, lc_scr, seg, reverse=True, fwd_ref=x_ref, fwd_entry_ref=ent_ref.at[:, 0],
                       da_ref=da_ref)
        for g in range(G):
            cols = slice(g * LANES, (g + 1) * LANES)
            uv, dsv = up_scr[g], dp_scr[g]
            ub, dsb = uv.astype(_BF), dsv.astype(_BF)
            lamb = g_scr[g].reshape(tb, 2 * KB_STATES).astype(_BF)
            db_ref[g] += _dot_tn(ub, lamb)
            dc_ref[g] += _dot_tn(x_ref[g].reshape(tb, 2 * KB_STATES).astype(_BF), dsb)
            dd_ref[:, cols] += jnp.sum(dsv * uv, axis=0, keepdims=True)
            up_scr[g] = _dot_nt(lamb, b_ref[g]) + d_ref[:, cols] * dsv
        _segments_to_rows(du_ref, up_scr, seg)

    return pl.pallas_call(
        body, name="s5_bwd", grid=(N_KB // G, nt),
        in_specs=[rcol, rcol, sp["x"], sp["ent"], sp["b"], sp["c"], sp["tab"], sp["pw"], sp["d"]],
        out_specs=[rcol, sp["b"], sp["c"], aspec, sp["d"]],
        out_shape=[jax.ShapeDtypeStruct((L, SSM_W), _F32),
                   jax.ShapeDtypeStruct((N_KB, LANES, 2 * KB_STATES), _F32),
                   jax.ShapeDtypeStruct((N_KB, 2 * KB_STATES, LANES), _F32),
                   jax.ShapeDtypeStruct((N_KB, SUBLANES, 2 * KB_STATES), _F32),
                   jax.ShapeDtypeStruct((1, SSM_W), _F32)],
        scratch_shapes=[pltpu.VMEM((G, tb, LANES), _F32)] * 2
        + [pltpu.VMEM((G, seg, SUBLANES, 2 * KB_STATES), _F32), pltpu.VMEM((G, SUBLANES, 2 * KB_STATES), _F32)],
        compiler_params=_params("parallel", "arbitrary"),
    )(u, ds, xs, ent, bmat, cmat, tab_r, pw_r, d_skip)


def _retention_bwd(q, k, v, do, r_prev, consts, cosf, sinf):
    L = q.shape[0]
    nc = L // CHUNK
    cps = math.gcd(RET_STEP_CHUNKS, nc)
    nb = nc // cps
    blk = pl.BlockSpec((cps * CHUNK, RET_W), lambda n: (nb - 1 - n, 0))
    rope_blk = pl.BlockSpec((cps * CHUNK, HEAD_D), lambda n: (nb - 1 - n, 0))

    def body(q_ref, k_ref, v_ref, do_ref, rp_ref, dm_ref, xi_ref, zeta_ref, gc_ref, cos_ref, sin_ref,
             dq_ref, dk_ref, dv_ref, g_scr):
        @pl.when(pl.program_id(0) == 0)
        def _():
            g_scr[...] = jnp.zeros_like(g_scr)

        for hh in range(N_HEAD):
            cols = slice(hh * HEAD_D, (hh + 1) * HEAD_D)
            dm, zeta = dm_ref[hh], zeta_ref[hh]
            gst = g_scr[hh]
            for c in reversed(range(cps)):
                rows = slice(c * CHUNK, (c + 1) * CHUNK)
                qv, kv, vv, dov = q_ref[rows, cols], k_ref[rows, cols], v_ref[rows, cols], do_ref[rows, cols]
                rb = rp_ref[hh, c].astype(_BF)
                gb = gst.astype(_BF)
                sb = (_dot_nt(qv, kv) * dm).astype(_BF)
                dab = (_dot_nt(dov, vv) * dm).astype(_BF)
                dox = (dov.astype(_F32) * xi_ref[hh]).astype(_BF)
                vz = (vv.astype(_F32) * zeta).astype(_BF)
                dq = _dot(dab, kv) + _dot_nt(dox, rb)
                dk = _dot_tn(dab, qv) + _dot_nt(vz, gb)
                dv = _dot_tn(sb, dov) + _dot(kv, gb) * zeta
                gst = gc_ref[hh, 0:1, :] * gst + _dot_tn(qv, dox)
                cs, sn = cos_ref[rows, :], sin_ref[rows, :]
                dq_ref[rows, cols] = _rope_t(dq, cs, sn).astype(_BF)
                dk_ref[rows, cols] = (_rope_t(dk, cs, sn) * (HEAD_D ** -0.5)).astype(_BF)
                dv_ref[rows, cols] = dv.astype(_BF)
            g_scr[hh] = gst

    return pl.pallas_call(
        body, name="retention_bwd", grid=(nb,),
        in_specs=[blk, blk, blk, blk, pl.BlockSpec((N_HEAD, cps, HEAD_D, HEAD_D), lambda n: (0, nb - 1 - n, 0, 0))]
        + _head_specs() + [rope_blk, rope_blk],
        out_specs=[blk, blk, blk],
        out_shape=[jax.ShapeDtypeStruct((L, RET_W), _BF)] * 3,
        scratch_shapes=[pltpu.VMEM((N_HEAD, HEAD_D, HEAD_D), _F32)],
        compiler_params=_params("arbitrary"),
    )(q, k, v, do, r_prev, *consts, cosf, sinf)


def _inproj_bwd(pieces, w_in_t, x, dx2, g1, tm):
    L = x.shape[0]

    def body(p0, p1, p2, p3, p4, w_ref, x_ref, dx2_ref, g_ref, dx_ref, dg_ref):
        @pl.when(pl.program_id(0) == 0)
        def _():
            dg_ref[...] = jnp.zeros_like(dg_ref)

        dh = None
        for j, p in enumerate((p0, p1, p2, p3, p4)):
            part = _dot(p[...].astype(_BF), w_ref[j * RET_W:(j + 1) * RET_W, :])
            dh = part if dh is None else dh + part
        dz, dgr = _rms_bwd(x_ref[...], g_ref[...], dh)
        dx_ref[...] = dx2_ref[...] + dz
        dg_ref[...] += jnp.sum(dgr, axis=0, keepdims=True)

    return pl.pallas_call(
        body, name="inproj_bwd", grid=(L // tm,),
        in_specs=[_row_spec(tm, RET_W)] * 5 + [_weight_spec((IN_COLS, D_MODEL)), _row_spec(tm, D_MODEL),
                                                 _row_spec(tm, D_MODEL), _full_spec((1, D_MODEL))],
        out_specs=[_row_spec(tm, D_MODEL), _full_spec((1, D_MODEL))],
        out_shape=[jax.ShapeDtypeStruct((L, D_MODEL), _F32), jax.ShapeDtypeStruct((1, D_MODEL), _F32)],
        compiler_params=_params("arbitrary"),
    )(*pieces, w_in_t, x, dx2, g1)


def _sum_adamw(parts, w, m, v, tr, name):
    _, R, Cc = parts.shape

    def body(p_ref, w_ref, m_ref, v_ref, g_ref, d_ref, nm_ref, nv_ref):
        gv = p_ref[0].astype(_F32)
        for s in range(1, N_DEV):
            gv = gv + p_ref[s].astype(_F32)
        g_ref[...] = gv
        nm = ADAM_B1 * m_ref[...] + (1.0 - ADAM_B1) * gv
        nv = ADAM_B2 * v_ref[...] + (1.0 - ADAM_B2) * (gv * gv)
        m_hat = nm / (1.0 - ADAM_B1 ** ADAM_STEP)
        v_hat = nv / (1.0 - ADAM_B2 ** ADAM_STEP)
        d_ref[...] = -ADAM_LR * (m_hat / (jnp.sqrt(v_hat) + ADAM_EPS) + ADAM_WD * w_ref[...])
        nm_ref[...] = nm
        nv_ref[...] = nv

    spec = _row_spec(tr, Cc)
    return pl.pallas_call(
        body, name=name, grid=(R // tr,),
        in_specs=[pl.BlockSpec((N_DEV, tr, Cc), lambda i: (0, i, 0))] + [spec] * 3, out_specs=[spec] * 4,
        out_shape=[jax.ShapeDtypeStruct((R, Cc), _F32)] * 4,
        compiler_params=_params("parallel"),
    )(parts, w, m, v)


def _my_place():
    return lax.axis_index("x"), lax.axis_index("y"), lax.axis_index("c")


def _all_gather(blocks):
    n = len(blocks)

    def body(*refs):
        x_refs, out_refs, done_ref = refs[:n], refs[n:2 * n], refs[2 * n]
        send_sems, recv_sems, local_sems = refs[2 * n + 1:]
        done_ref[...] = jnp.zeros_like(done_ref)
        x, y, c = _my_place()
        me, sibling = (x, y, c), (x, y, 1 - c)
        chips = [(1 - x, y), (x, 1 - y), (1 - x, 1 - y)]

        def slot(a, px, py, pc):
            return out_refs[a].at[4 * px + 2 * py + pc]

        def copy(a, k, blk, to, own=False):
            return pltpu.make_async_remote_copy(
                src_ref=x_refs[a] if own else slot(a, *blk), dst_ref=slot(a, *blk),
                send_sem=send_sems.at[a, k], recv_sem=recv_sems.at[a, k], device_id=to, device_id_type=MESH)

        mine = [pltpu.make_async_copy(x_refs[a], slot(a, *me), local_sems.at[a]) for a in range(n)]
        for cp in mine:
            cp.start()
        first = []
        for a in range(n):
            first.append(copy(a, 0, me, sibling, own=True))
            first += [copy(a, 1 + j, me, (*chip, c), own=True) for j, chip in enumerate(chips)]
        for cp in first:
            cp.start()
        passed = []
        for j, chip in enumerate(chips):
            for a in range(n):
                copy(a, 1 + j, (*chip, c), me).wait_recv()
                fwd = copy(a, 4 + j, (*chip, c), sibling)
                fwd.start()
                passed.append(fwd)
        for a in range(n):
            copy(a, 0, sibling, me).wait_recv()
            for j, chip in enumerate(chips):
                copy(a, 4 + j, (*chip, 1 - c), me).wait_recv()
        for cp in first + passed:
            cp.wait_send()
        for cp in mine:
            cp.wait()

    any_spec = pl.BlockSpec(memory_space=pl.ANY)
    outs = pl.pallas_call(
        body, name="weights_all_gather",
        in_specs=[any_spec] * n, out_specs=[any_spec] * n + [pl.BlockSpec(memory_space=pltpu.VMEM)],
        out_shape=[jax.ShapeDtypeStruct((N_DEV,) + b.shape, b.dtype) for b in blocks]
        + [jax.ShapeDtypeStruct((SUBLANES, LANES), _F32)],
        scratch_shapes=[pltpu.SemaphoreType.DMA((n, 7)), pltpu.SemaphoreType.DMA((n, 7)), pltpu.SemaphoreType.DMA((n,))],
    )(*blocks)
    return outs[:n], outs[n]


def _exchange(bigs, small):
    n = len(bigs)
    r = small.shape[0]

    def body(*refs):
        in_refs, out_refs = refs[:n + 1], refs[n + 1:2 * n + 2]
        send_sems, recv_sems, local_sems = refs[2 * n + 2:]
        x, y, c = _my_place()
        me = 4 * x + 2 * y + c
        own = [pltpu.make_async_copy(in_refs[a].at[me], out_refs[a].at[me], local_sems.at[a]) for a in range(n)]
        own.append(pltpu.make_async_copy(in_refs[n], out_refs[n].at[me], local_sems.at[n]))
        for cp in own:
            cp.start()
        copies = []
        for kk in range(1, N_DEV):
            px, py, pc = x ^ (kk >> 2), y ^ ((kk >> 1) & 1), c ^ (kk & 1)
            peer = 4 * px + 2 * py + pc
            for a in range(n + 1):
                src = in_refs[a].at[peer] if a < n else in_refs[a]
                copies.append(pltpu.make_async_remote_copy(
                    src_ref=src, dst_ref=out_refs[a].at[me],
                    send_sem=send_sems.at[a, kk - 1], recv_sem=recv_sems.at[a, kk - 1],
                    device_id=(px, py, pc), device_id_type=MESH))
        for cp in copies:
            cp.start()
        for cp in copies:
            cp.wait_recv()
        for cp in copies:
            cp.wait_send()
        for cp in own:
            cp.wait()

    any_spec = pl.BlockSpec(memory_space=pl.ANY)
    outs = pl.pallas_call(
        body, name="grad_exchange",
        in_specs=[any_spec] * (n + 1), out_specs=[any_spec] * (n + 1),
        out_shape=[jax.ShapeDtypeStruct(b.shape, b.dtype) for b in bigs]
        + [jax.ShapeDtypeStruct((N_DEV, r, LANES), small.dtype)],
        scratch_shapes=[pltpu.SemaphoreType.DMA((n + 1, 7)), pltpu.SemaphoreType.DMA((n + 1, 7)),
                        pltpu.SemaphoreType.DMA((n + 1,))],
    )(*bigs, small)
    return outs[:n], outs[n]


HBM_SPEC = pl.BlockSpec(memory_space=pltpu.HBM)
SEM_SPEC = pl.BlockSpec(memory_space=pltpu.SEMAPHORE)
DATAFLOW = pltpu.SideEffectType.DATAFLOW_SIDE_EFFECTING


def _my_index():
    x, y, c = _my_place()
    return 4 * x + 2 * y + c


def _landing(own_block):
    zone = lax.empty((N_DEV,) + own_block.shape, own_block.dtype)
    return lax.dynamic_update_index_in_dim(zone, own_block, _my_index(), 0)


def _split_copies(src_refs, land_refs, send_sems, recv_sems, gather):
    x, y, c = _my_place()
    me = 4 * x + 2 * y + c
    copies = []
    for kk in range(1, N_DEV):
        px, py, pc = x ^ (kk >> 2), y ^ ((kk >> 1) & 1), c ^ (kk & 1)
        peer = 4 * px + 2 * py + pc
        for a, (src, land) in enumerate(zip(src_refs, land_refs)):
            copies.append(pltpu.make_async_remote_copy(
                src_ref=src if gather else src.at[peer], dst_ref=land.at[me],
                send_sem=send_sems.at[a * 7 + kk - 1], recv_sem=recv_sems.at[a * 7 + kk - 1],
                device_id=(px, py, pc), device_id_type=MESH))
    return copies


def _split_start(srcs, lands, gather, name):
    n = len(srcs)

    def body(*refs):
        src_refs, land_refs = refs[:n], refs[n:2 * n]
        send_sems, recv_sems = refs[2 * n], refs[2 * n + 1]
        token = refs[-1]
        for cp in _split_copies(src_refs, land_refs, send_sems, recv_sems, gather):
            cp.start()
        token[...] = jnp.zeros_like(token)

    outs = pl.pallas_call(
        body, name=name,
        out_shape=(pltpu.SemaphoreType.DMA((7 * n,)), pltpu.SemaphoreType.DMA((7 * n,)),
                   *[pltpu.HBM(t.shape, t.dtype) for t in srcs], *[pltpu.HBM(t.shape, t.dtype) for t in lands],
                   jax.ShapeDtypeStruct((SUBLANES, LANES), _F32)),
        in_specs=[HBM_SPEC] * (2 * n),
        out_specs=(SEM_SPEC, SEM_SPEC, *[HBM_SPEC] * (2 * n), pl.BlockSpec(memory_space=pltpu.VMEM)),
        input_output_aliases={i: 2 + i for i in range(2 * n)},
        compiler_params=pltpu.CompilerParams(has_side_effects=DATAFLOW),
    )(*[pltpu.with_memory_space_constraint(t, pltpu.HBM) for t in list(srcs) + list(lands)])
    return outs[0], outs[1], outs[2:2 + n], outs[2 + n:2 + 2 * n], outs[-1]


def _split_wait(send_sems, recv_sems, srcs, lands, after, gather, name):
    n = len(srcs)

    def body(*refs):
        src_refs, land_refs = refs[:n], refs[n:2 * n]
        send_s, recv_s = refs[2 * n], refs[2 * n + 1]
        for cp in _split_copies(src_refs, land_refs, send_s, recv_s, gather):
            cp.wait_send()
            cp.wait_recv()

    outs = pl.pallas_call(
        body, name=name,
        out_shape=tuple(pltpu.HBM(t.shape, t.dtype) for t in list(srcs) + list(lands)),
        in_specs=[HBM_SPEC] * (2 * n) + [SEM_SPEC, SEM_SPEC, pl.BlockSpec(memory_space=pl.ANY)],
        out_specs=tuple([HBM_SPEC] * (2 * n)),
        input_output_aliases={i: i for i in range(2 * n)},
        compiler_params=pltpu.CompilerParams(has_side_effects=DATAFLOW),
    )(*srcs, *lands, send_sems, recv_sems, after)
    return outs[n:]


def _discretize(lam_re, lam_im, log_dt, b_re, b_im):
    lr = jnp.minimum(lam_re, -1e-4)
    li = lam_im
    dt = jnp.exp(log_dt)[:, None]
    er = jnp.exp(lr * dt)
    ar, ai = er * jnp.cos(li * dt), er * jnp.sin(li * dt)
    den = lr * lr + li * li
    cr = ((ar - 1.0) * lr + ai * li) / den
    ci = (ai * lr - (ar - 1.0) * li) / den
    bbr = cr[:, :, None] * b_re - ci[:, :, None] * b_im
    bbi = cr[:, :, None] * b_im + ci[:, :, None] * b_re
    return ar, ai, bbr, bbi


def _cmul(ar, ai, br, bi):
    return ar * br - ai * bi, ar * bi + ai * br


def _cpowers(ar, ai, n):
    pr, pi = ar[None], ai[None]
    while pr.shape[0] < n:
        nr, ni = _cmul(pr, pi, pr[-1][None], pi[-1][None])
        pr, pi = jnp.concatenate([pr, nr]), jnp.concatenate([pi, ni])
    return pr[:n], pi[:n]


def _scan_tables(ar, ai, seg, reverse):
    if reverse:
        ai = -ai
    ar, ai = ar.reshape(N_KB, KB_STATES), ai.reshape(N_KB, KB_STATES)
    pr, pi = _cpowers(ar, ai, seg)
    a1 = (pr[-1], pi[-1])
    a2 = _cmul(*a1, *a1)
    a4 = _cmul(*a2, *a2)
    row = jnp.arange(SUBLANES)[None, :, None]
    wide = lambda t: jnp.broadcast_to(t[:, None, :], (N_KB, SUBLANES, KB_STATES))
    tabs = [wide(ar), wide(ai)]
    for dist, (qr, qi) in ((1, a1), (2, a2), (4, a4)):
        keep = (row < SUBLANES - dist) if reverse else (row >= dist)
        tabs += [jnp.where(keep, wide(qr), 0.0), jnp.where(keep, wide(qi), 0.0)]
    tabs += [wide(a1[0]), wide(a1[1])]
    if reverse:
        pr, pi = pr[::-1], pi[::-1]
    pw = jnp.transpose(jnp.concatenate([pr, pi], axis=-1), (1, 0, 2))[:, :, None, :]
    return jnp.stack(tabs, axis=1).astype(_F32), pw.astype(_F32)


def _block_diag_in(br, bi):
    eye = jnp.eye(GROUPS_PER_KB, dtype=_F32)
    one = lambda t: jnp.einsum("kgpc,gh->kgchp", t.reshape(N_KB, GROUPS_PER_KB, N_STATE, SSM_GC), eye).reshape(
        N_KB, LANES, KB_STATES)
    return jnp.concatenate([one(br), one(bi)], axis=-1)


def _block_diag_in_t(dmat):
    d6 = dmat.reshape(N_KB, GROUPS_PER_KB, SSM_GC, 2, GROUPS_PER_KB, N_STATE)
    eye = jnp.eye(GROUPS_PER_KB, dtype=_F32)
    both = jnp.einsum("kgcrhp,gh->rkgpc", d6, eye).reshape(2, N_GROUP, N_STATE, SSM_GC)
    return both[0], both[1]


def _block_diag_out(c_re, c_im):
    eye = jnp.eye(GROUPS_PER_KB, dtype=_F32)
    one = lambda t: jnp.einsum("kgcp,gh->khpgc", t.reshape(N_KB, GROUPS_PER_KB, SSM_GC, N_STATE), eye).reshape(
        N_KB, KB_STATES, LANES)
    return jnp.concatenate([one(c_re), -one(c_im)], axis=1)


def _block_diag_out_t(dmat):
    d6 = dmat.reshape(N_KB, 2, GROUPS_PER_KB, N_STATE, GROUPS_PER_KB, SSM_GC)
    eye = jnp.eye(GROUPS_PER_KB, dtype=_F32)
    both = jnp.einsum("krhpgc,gh->rkgcp", d6, eye).reshape(2, N_GROUP, SSM_GC, N_STATE)
    return both[0], -both[1]


SMALL_NAMES = ("norm_mix_pre", "norm_mix_post", "ret_gn_gain", "ssm_lambda_re", "ssm_lambda_im", "ssm_log_dt",
               "ssm_b_re", "ssm_b_im", "ssm_c_re", "ssm_c_im", "ssm_d", "norm_mlp_pre", "norm_mlp_post")


def _local_grads(x, tgt, small, weights, emit, emit_small, tm, tk, tb, zero=0.0):
    L = x.shape[0]
    g1, g2, ggn = small["norm_mix_pre"], small["norm_mix_post"], small["ret_gn_gain"]
    g3, g4, d_skip = small["norm_mlp_pre"], small["norm_mlp_post"], small["ssm_d"]

    rope = _rope_tables(L)
    consts = _ret_consts()

    disc_in = (small["ssm_lambda_re"][0], small["ssm_lambda_im"][0], small["ssm_log_dt"][0] + zero,
               small["ssm_b_re"][0], small["ssm_b_im"][0])
    (ar, ai, bbr, bbi), disc_vjp = jax.vjp(_discretize, *disc_in)
    bmat = _block_diag_in(bbr, bbi).astype(_BF)
    cmat = _block_diag_out(small["ssm_c_re"][0], small["ssm_c_im"][0]).astype(_BF)
    seg = tb // SUBLANES
    tab_f, pw_f = _scan_tables(ar, ai, seg, False)
    tab_r, pw_r = _scan_tables(ar, ai, seg, True)

    (w_in_t,) = weights("in", pw_r)
    h1, q, k, v, gate, u, cosf, sinf = _inproj_fwd(x, g1, w_in_t, rope, tm)
    o, y_ret, r_prev = _retention_fwd(q, k, v, gate, ggn, consts)
    s, xs, ent = _s5_fwd(u, bmat, cmat, tab_f, pw_f, d_skip, tb)
    w_glu, w_out = weights("mix", s)
    ys, glu, cat, mix, x2 = _mixout_fwd(s, y_ret, x, w_glu, w_out, g2, tm)
    w_ff1, w_ff2 = weights("mlp", x2)
    h3, f1 = _ff1_fwd(x2, g3, w_ff1, tm)
    dy, dm, dg4, sq = _ff2_loss(f1, x2, tgt, g4, w_ff2, min(2 * tm, L))

    df1, dw_ff2 = _ff2_bwd(dm, f1, w_ff2, min(1024, L), 1024)
    dx2, dmix, dg3, dg2 = _ff1_bwd(df1, w_ff1, x2, mix, dy, g3, g2, min(2 * tm, L))
    dw_ff1 = _matmul_tn(h3, df1, tk, FF1_COLS, "dw_ff1", slots=True)
    zero = emit({"w_ff1": dw_ff1, "w_ff2": dw_ff2})
    dglu, ds, dgate, do, dggn = _mixout_bwd(dmix, w_out, w_glu, glu, s, o, gate, ggn if zero is None else ggn + zero, tm)
    dw_out = _matmul_tn(cat, dmix, tk, 1024, "dw_out")
    dw_glu = _matmul_tn(ys, dglu, tk, 1024, "dw_glu")
    zero = emit({"w_glu": dw_glu, "w_out": dw_out})
    du, dbmat, dcmat, da8, dd = _s5_bwd(u, ds, xs, ent, bmat, cmat, tab_r, pw_r,
                                        d_skip if zero is None else d_skip + zero, tb)
    dq, dk, dv = _retention_bwd(q, k, v, do, r_prev, consts, cosf, sinf)
    pieces = (dq, dk, dv, dgate, du)
    dw_in_t = jnp.concatenate([_matmul_tn(p, h1, tk, D_MODEL, "dw_in_%d" % j) for j, p in enumerate(pieces)], axis=0)
    zero = emit({"w_in": dw_in_t})

    da = jnp.sum(da8, axis=1)
    dar = da[:, :KB_STATES].reshape(N_GROUP, N_STATE)
    dai = da[:, KB_STATES:].reshape(N_GROUP, N_STATE)
    dbr, dbi = _block_diag_in_t(dbmat)
    dlre, dlim, dldt, dbre, dbim = disc_vjp((dar, dai, dbr, dbi))
    dcre, dcim = _block_diag_out_t(dcmat)

    zero2 = emit_small({
        "norm_mix_post": dg2, "ret_gn_gain": dggn,
        "ssm_lambda_re": dlre[None], "ssm_lambda_im": dlim[None], "ssm_log_dt": dldt[None],
        "ssm_b_re": dbre[None], "ssm_b_im": dbim[None], "ssm_c_re": dcre[None], "ssm_c_im": dcim[None],
        "ssm_d": dd, "norm_mlp_pre": dg3, "norm_mlp_post": dg4,
    }, sq)
    for z in (zero, zero2):
        g1 = g1 if z is None else g1 + z
    gx, dg1 = _inproj_bwd(pieces, w_in_t, x, dx2, g1, tm)
    return gx, dg1


BIG_SHAPES = {"w_in": (D_MODEL, IN_COLS // N_DEV), "w_glu": (SSM_W, 2 * SSM_W // N_DEV), "w_out": (D_MODEL // N_DEV, D_MODEL),
              "w_ff1": (D_MODEL, FF1_COLS), "w_ff2": (D_FF // N_DEV, D_MODEL)}
BIG_NAMES = ("w_in", "w_glu", "w_out", "w_ff1", "w_ff2")


def _cols_from_slots(g):
    return jnp.transpose(g, (1, 0, 2)).reshape(g.shape[1], N_DEV * g.shape[2])


def _cols_to_slots(dw):
    r, cols = dw.shape
    return jnp.transpose(dw.reshape(r, N_DEV, cols // N_DEV), (1, 0, 2))


WEIGHT_GROUPS = {"in": ("w_in",), "mix": ("w_glu", "w_out"), "mlp": ("w_ff1", "w_ff2")}


def _weight_from_slots(name, g):
    if name == "w_glu":
        return _cols_from_slots(g)
    if name == "w_ff1":
        return g
    return g.reshape(N_DEV * g.shape[1], g.shape[2])


def _grad_slots(name, dw):
    if name == "w_glu":
        return _cols_to_slots(dw)
    if name == "w_ff1":
        return dw
    if name == "w_in":
        return dw.reshape(N_DEV, BIG_SHAPES[name][1], BIG_SHAPES[name][0])
    return dw.reshape((N_DEV,) + BIG_SHAPES[name])


PIECE_ROWS = 8


def _small_layout(shapes):
    off, rows = {}, 0
    for n in SMALL_NAMES:
        off[n] = rows
        rows += -(-math.prod(shapes[n]) // (PIECE_ROWS * LANES)) * PIECE_ROWS
    return off, rows, rows + PIECE_ROWS


def _pack_small(vals, shapes, last=None):
    parts = []
    for n in SMALL_NAMES:
        flat = vals[n].reshape(-1).astype(_F32)
        pad = -flat.shape[0] % (PIECE_ROWS * LANES)
        if pad:
            flat = jnp.concatenate([flat, jnp.zeros((pad,), _F32)])
        parts.append(flat.reshape(-1, LANES))
    parts.append(jnp.zeros((PIECE_ROWS, LANES), _F32) if last is None else last)
    return jnp.concatenate(parts, axis=0)


def _unpack_small(buf, shapes):
    off, _, _ = _small_layout(shapes)
    out = {}
    for n in SMALL_NAMES:
        size = math.prod(shapes[n])
        rows = -(-size // LANES)
        out[n] = buf[off[n]:off[n] + rows].reshape(-1)[:size].reshape(shapes[n])
    return out


WEIGHT_NAMES = ('norm_mix_pre', 'norm_mix_post', 'w_in', 'ret_gn_gain', 'ssm_lambda_re', 'ssm_lambda_im', 'ssm_log_dt',
                'ssm_b_re', 'ssm_b_im', 'ssm_c_re', 'ssm_c_im', 'ssm_d', 'w_glu', 'w_out', 'norm_mlp_pre',
                'norm_mlp_post', 'w_ff1', 'w_ff2')


def kernel(x, norm_mix_pre, norm_mix_post, w_in, ret_gn_gain, ssm_lambda_re, ssm_lambda_im, ssm_log_dt, ssm_b_re, ssm_b_im, ssm_c_re, ssm_c_im, ssm_d, w_glu, w_out, norm_mlp_pre, norm_mlp_post, w_ff1, w_ff2, loss_target, m_norm_mix_pre, m_norm_mix_post, m_w_in, m_ret_gn_gain, m_ssm_lambda_re, m_ssm_lambda_im, m_ssm_log_dt, m_ssm_b_re, m_ssm_b_im, m_ssm_c_re, m_ssm_c_im, m_ssm_d, m_w_glu, m_w_out, m_norm_mlp_pre, m_norm_mlp_post, m_w_ff1, m_w_ff2, v_norm_mix_pre, v_norm_mix_post, v_w_in, v_ret_gn_gain, v_ssm_lambda_re, v_ssm_lambda_im, v_ssm_log_dt, v_ssm_b_re, v_ssm_b_im, v_ssm_c_re, v_ssm_c_im, v_ssm_d, v_w_glu, v_w_out, v_norm_mlp_pre, v_norm_mlp_post, v_w_ff1, v_w_ff2):
    args = dict(locals())
    w = {n: args[n] for n in WEIGHT_NAMES}
    m = {n: args["m_" + n] for n in WEIGHT_NAMES}
    v = {n: args["v_" + n] for n in WEIGHT_NAMES}
    L = x.shape[1]
    tm = min(256, L)
    tk = min(2048, L)
    tb = min(512, L)

    gathers, zero = {}, jnp.zeros((), _F32)
    for group, names in WEIGHT_GROUPS.items():
        blocks = [(w[n][0].T if n == "w_in" else w[n][0]).astype(_BF) for n in names]
        blocks[0] = blocks[0] + zero.astype(_BF)
        gathers[group] = _split_start(blocks, [_landing(b) for b in blocks], True, "weights_start_" + group)
        zero = gathers[group][4][0, 0]

    def weights(group, after):
        landed = _split_wait(*gathers[group][:4], after, True, "weights_wait_" + group)
        return [_weight_from_slots(n, g) for n, g in zip(WEIGHT_GROUPS[group], landed)]

    in_flight = []

    def emit(dws):
        names = sorted(dws)
        srcs = [_grad_slots(n, dws[n]) for n in names]
        lands = [_landing(lax.dynamic_index_in_dim(t, _my_index(), 0, keepdims=False)) for t in srcs]
        started = _split_start(srcs, lands, False, "grads_start_" + "_".join(names))
        in_flight.append((names, started))
        return started[4][0, 0]

    shapes = {n: w[n].shape for n in SMALL_NAMES}
    first_piece = {SMALL_NAMES[0]: jnp.zeros(shapes[SMALL_NAMES[0]], _F32)}
    small_flight = []

    def emit_small(gs, sq):
        loss_rows = jnp.broadcast_to(0.5 / D_MODEL * jnp.sum(sq), (PIECE_ROWS, LANES)).astype(_F32)
        buf = _pack_small({**first_piece, **gs}, shapes, loss_rows)
        small_flight.append(_split_start([buf], [_landing(buf)], True, "small_grads_start"))
        return small_flight[0][4][0, 0]

    small_w = {n: w[n] for n in SMALL_NAMES}
    gx, dg1 = _local_grads(x[0], loss_target[0], small_w, weights, emit, emit_small, tm, tk, tb, zero=zero)
    last_buf = dg1.reshape(PIECE_ROWS, LANES)
    last_started = _split_start([last_buf], [_landing(last_buf)], True, "last_grad_start")

    grads, delta, new_m, new_v = {}, {}, {}, {}
    after = last_started[4]
    for names, started in in_flight:
        landed = _split_wait(*started[:4], after, False, "grads_wait_" + "_".join(names))
        for n, parts in zip(names, landed):
            flip = (lambda t: t.T) if n == "w_in" else (lambda t: t)
            res = _sum_adamw(parts, flip(w[n][0]), flip(m[n][0]), flip(v[n][0]), math.gcd(256, parts.shape[1]), "adamw_" + n)
            grads[n], delta[n], new_m[n], new_v[n] = (flip(t)[None] for t in res)
        after = res[1]
    small_parts = _split_wait(*small_flight[0][:4], after, True, "small_grads_wait")[0]
    last_parts = _split_wait(*last_started[:4], small_parts, True, "last_grad_wait")[0]
    small_parts = lax.dynamic_update_slice(small_parts, last_parts, (0, 0, 0))
    sw, sm, sv = _pack_small(w, shapes), _pack_small(m, shapes), _pack_small(v, shapes)
    res = _sum_adamw(small_parts, sw, sm, sv, sw.shape[0], "adamw_small")
    for dst, buf in zip((grads, delta, new_m, new_v), res):
        dst.update(_unpack_small(buf, shapes))
    _, loss_at, _ = _small_layout(shapes)
    loss = res[0][loss_at, 0]

    return (loss, gx[None], *[grads[n] for n in WEIGHT_NAMES], *[delta[n] for n in WEIGHT_NAMES],
            *[new_m[n] for n in WEIGHT_NAMES], *[new_v[n] for n in WEIGHT_NAMES])
```

```python
import math

import jax
import jax.numpy as jnp
from jax import lax
from jax.experimental import pallas as pl
from jax.experimental.pallas import tpu as pltpu

_BF = jnp.bfloat16
_F32 = jnp.float32

D_MODEL = 1024
RET_W = 512
N_HEAD = 4
HEAD_D = 128
CHUNK = 256
ROPE_CHUNK = 128
SSM_W = 512
SSM_GC = 16
N_GROUP = 32
N_STATE = 64
GROUPS_PER_KB = 8
N_KB = 4
KB_STATES = GROUPS_PER_KB * N_STATE
D_FF = 4096
IN_COLS = 2560
NORM_EPS = 1e-6
ROPE_BASE = 10000.0
N_DEV = 8

ADAM_LR = 0.001
ADAM_B1 = 0.9
ADAM_B2 = 0.999
ADAM_EPS = 1e-08
ADAM_WD = 0.01
ADAM_STEP = 10

SUBLANES = 8
LANES = 128
VMEM_LIMIT = 52 * 1024 * 1024
RET_STEP_CHUNKS = 2
KB_PER_STEP = 2
SCAN_UNROLL = 2

MESH = pl.DeviceIdType.MESH


def _params(*sem):
    return pltpu.CompilerParams(dimension_semantics=sem, vmem_limit_bytes=VMEM_LIMIT)


def _dot(a, b):
    return jnp.dot(a, b, preferred_element_type=_F32)


def _dot_nt(a, b):
    return lax.dot_general(a, b, (((1,), (1,)), ((), ())), preferred_element_type=_F32)


def _dot_tn(a, b):
    return lax.dot_general(a, b, (((0,), (0,)), ((), ())), preferred_element_type=_F32)


def _rms_r(z):
    return lax.rsqrt(jnp.mean(z * z, axis=-1, keepdims=True) + NORM_EPS)


def _rms_bwd(z, g, dn):
    r = _rms_r(z)
    t = dn * g
    dz = r * t - z * (r * r * r * jnp.mean(t * z, axis=-1, keepdims=True))
    return dz, dn * z * r


def _rope(t, cs, sn):
    return t * cs + pltpu.roll(t, HEAD_D // 2, 1) * sn


def _rope_t(t, cs, sn):
    return t * cs - pltpu.roll(t, HEAD_D // 2, 1) * sn


def _sigmoid(z):
    return 1.0 / (1.0 + jnp.exp(-z))


_GELU_C = math.sqrt(2.0 / math.pi)


def _gelu(z):
    return 0.5 * z * (1.0 + jnp.tanh(_GELU_C * (z + 0.044715 * z * z * z)))


def _gelu_grad(z):
    th = jnp.tanh(_GELU_C * (z + 0.044715 * z * z * z))
    return 0.5 * (1.0 + th) + 0.5 * z * (1.0 - th * th) * _GELU_C * (1.0 + 3 * 0.044715 * z * z)


ROW_CHUNK = 256


def _row_chunks(tm):
    return [pl.ds(i, min(ROW_CHUNK, tm)) for i in range(0, tm, ROW_CHUNK)]


def _row_spec(tm, n):
    return pl.BlockSpec((tm, n), lambda i: (i, 0))


def _full_spec(shape):
    nd = len(shape)
    return pl.BlockSpec(shape, lambda *_: (0,) * nd)


def _weight_spec(shape):
    nd = len(shape)
    return pl.BlockSpec(shape, lambda *_: (0,) * nd, pipeline_mode=pl.Buffered(1))


def _rope_tables(L):
    half = HEAD_D // 2
    inv_freq = ROPE_BASE ** (-jnp.arange(half, dtype=_F32) / half)
    twice = lambda t: jnp.concatenate([t, t], axis=-1)
    off = jnp.arange(ROPE_CHUNK, dtype=_F32)[:, None] * inv_freq[None, :]
    start = (ROPE_CHUNK * jnp.arange(L // ROPE_CHUNK, dtype=_F32))[:, None] * inv_freq[None, :]
    return (twice(jnp.cos(off)), twice(jnp.sin(off)),
            twice(jnp.cos(start))[:, None, :], twice(jnp.sin(start))[:, None, :])


def _inproj_fwd(x, g1, w_in_t, rope, tm):
    L = x.shape[0]
    n_chunks = tm // ROPE_CHUNK

    def body(x_ref, g_ref, w_ref, co_ref, so_ref, cs_ref, ss_ref, h_ref, q_ref, k_ref, v_ref, gate_ref, u_ref,
             cos_ref, sin_ref):
        xv = x_ref[...]
        h = (xv * _rms_r(xv) * g_ref[...]).astype(_BF)
        h_ref[...] = h
        proj = _dot_nt(h, w_ref[...])
        lane = lax.broadcasted_iota(jnp.int32, (ROPE_CHUNK, HEAD_D), 1)
        sign = jnp.where(lane < HEAD_D // 2, -1.0, 1.0)
        co, so = co_ref[...], so_ref[...]
        for c in range(n_chunks):
            chunk = pl.program_id(0) * n_chunks + c
            cst, sst = cs_ref[chunk], ss_ref[chunk]
            rows = slice(c * ROPE_CHUNK, (c + 1) * ROPE_CHUNK)
            cs = co * cst - so * sst
            sn = (so * cst + co * sst) * sign
            cos_ref[rows, :] = cs
            sin_ref[rows, :] = sn
            for hh in range(N_HEAD):
                lo = hh * HEAD_D
                q_ref[rows, lo:lo + HEAD_D] = _rope(proj[rows, lo:lo + HEAD_D], cs, sn).astype(_BF)
                kh = _rope(proj[rows, RET_W + lo:RET_W + lo + HEAD_D], cs, sn) * (HEAD_D ** -0.5)
                k_ref[rows, lo:lo + HEAD_D] = kh.astype(_BF)
        v_ref[...] = proj[:, 2 * RET_W:3 * RET_W].astype(_BF)
        gate_ref[...] = proj[:, 3 * RET_W:4 * RET_W]
        u_ref[...] = proj[:, 4 * RET_W:]

    nc = L // ROPE_CHUNK
    return pl.pallas_call(
        body, name="inproj_fwd", grid=(L // tm,),
        in_specs=[_row_spec(tm, D_MODEL), _full_spec((1, D_MODEL)), _weight_spec((IN_COLS, D_MODEL)),
                  _full_spec((ROPE_CHUNK, HEAD_D)), _full_spec((ROPE_CHUNK, HEAD_D)),
                  _full_spec((nc, 1, HEAD_D)), _full_spec((nc, 1, HEAD_D))],
        out_specs=[_row_spec(tm, D_MODEL)] + [_row_spec(tm, RET_W)] * 5 + [_row_spec(tm, HEAD_D)] * 2,
        out_shape=[jax.ShapeDtypeStruct((L, D_MODEL), _BF)] + [jax.ShapeDtypeStruct((L, RET_W), _BF)] * 3
        + [jax.ShapeDtypeStruct((L, RET_W), _F32)] * 2 + [jax.ShapeDtypeStruct((L, HEAD_D), _F32)] * 2,
        compiler_params=_params("parallel"),
    )(x, g1, w_in_t, *rope)


def _ret_consts():
    lg = jnp.log(1.0 - jnp.exp(jnp.linspace(math.log(1.0 / 32), math.log(1.0 / 512), N_HEAD))).astype(_F32)
    idx = jnp.arange(CHUNK, dtype=_F32)
    diff = idx[:, None] - idx[None, :]
    decay = jnp.where(diff[None] >= 0, jnp.exp(jnp.maximum(diff, 0.0)[None] * lg[:, None, None]), 0.0)
    zeta = jnp.exp((CHUNK - 1 - idx)[None, :] * lg[:, None])
    xi = jnp.exp((idx + 1.0)[None, :] * lg[:, None])
    gc = jnp.exp(CHUNK * lg)
    wide = lambda t: jnp.broadcast_to(t[:, :, None], (N_HEAD, CHUNK, HEAD_D)).astype(_F32)
    gcw = jnp.broadcast_to(gc[:, None, None], (N_HEAD, SUBLANES, HEAD_D)).astype(_F32)
    return decay.astype(_F32), wide(xi), wide(zeta), gcw


def _head_specs():
    wide = _full_spec((N_HEAD, CHUNK, HEAD_D))
    return [_full_spec((N_HEAD, CHUNK, CHUNK)), wide, wide, _full_spec((N_HEAD, SUBLANES, HEAD_D))]


def _retention_fwd(q, k, v, gate, ggn, consts):
    L = q.shape[0]
    nc = L // CHUNK
    cps = math.gcd(RET_STEP_CHUNKS, nc)
    blk = pl.BlockSpec((cps * CHUNK, RET_W), lambda n: (n, 0))

    def body(q_ref, k_ref, v_ref, gate_ref, ggn_ref, dm_ref, xi_ref, zeta_ref, gc_ref,
             o_ref, y_ref, rp_ref, r_scr):
        @pl.when(pl.program_id(0) == 0)
        def _():
            r_scr[...] = jnp.zeros_like(r_scr)

        for hh in range(N_HEAD):
            cols = slice(hh * HEAD_D, (hh + 1) * HEAD_D)
            state = r_scr[hh]
            for c in range(cps):
                rows = slice(c * CHUNK, (c + 1) * CHUNK)
                qv, kv, vv = q_ref[rows, cols], k_ref[rows, cols], v_ref[rows, cols]
                s = _dot_nt(qv, kv) * dm_ref[hh]
                o = _dot(s.astype(_BF), vv) + _dot(qv, state.astype(_BF)) * xi_ref[hh]
                o_ref[rows, cols] = o
                rp_ref[hh, c] = state
                vz = (vv.astype(_F32) * zeta_ref[hh]).astype(_BF)
                state = gc_ref[hh, 0:1, :] * state + _dot_tn(kv, vz)
                dlt = o - jnp.mean(o, axis=-1, keepdims=True)
                on = dlt * lax.rsqrt(jnp.mean(dlt * dlt, axis=-1, keepdims=True) + NORM_EPS)
                gt = gate_ref[rows, cols]
                y_ref[rows, cols] = (gt * _sigmoid(gt) * (on * ggn_ref[:, cols])).astype(_BF)
            r_scr[hh] = state

    return pl.pallas_call(
        body, name="retention_fwd", grid=(nc // cps,),
        in_specs=[blk, blk, blk, blk, _full_spec((1, RET_W))] + _head_specs(),
        out_specs=[blk, blk, pl.BlockSpec((N_HEAD, cps, HEAD_D, HEAD_D), lambda n: (0, n, 0, 0))],
        out_shape=[jax.ShapeDtypeStruct((L, RET_W), _F32), jax.ShapeDtypeStruct((L, RET_W), _BF),
                   jax.ShapeDtypeStruct((N_HEAD, nc, HEAD_D, HEAD_D), _F32)],
        scratch_shapes=[pltpu.VMEM((N_HEAD, HEAD_D, HEAD_D), _F32)],
        compiler_params=_params("arbitrary"),
    )(q, k, v, gate, ggn, *consts)


def _rows_to_segments(dst_scr, src_ref, seg):
    for g in range(dst_scr.shape[0]):
        for j in range(SUBLANES):
            dst_scr[g, pl.ds(j, seg, stride=SUBLANES), :] = src_ref[pl.ds(j * seg, seg), g * LANES:(g + 1) * LANES]


def _segments_to_rows(dst_ref, src_scr, seg):
    for g in range(src_scr.shape[0]):
        for j in range(SUBLANES):
            dst_ref[pl.ds(j * seg, seg), g * LANES:(g + 1) * LANES] = src_scr[g, pl.ds(j, seg, stride=SUBLANES), :]


def _scan_segments(x_ref, tab_ref, pw_ref, carry_ref, seg, reverse, entry_ref=None, fwd_ref=None, fwd_entry_ref=None,
                   da_ref=None):
    G = x_ref.shape[0]
    W = KB_STATES
    re, im = pl.ds(0, W), pl.ds(W, W)
    row_id = lax.broadcasted_iota(jnp.int32, (SUBLANES, W), 0)
    edge_in = (row_id == SUBLANES - 1) if reverse else (row_id == 0)
    edge_out = 0 if reverse else SUBLANES - 1
    a_tab = [(tab_ref[g, 0], tab_ref[g, 1]) for g in range(G)]

    def local(i, st):
        r = (seg - 1 - i) if reverse else i
        out = []
        for g in range(G):
            (ar, ai), (sr, si) = a_tab[g], st[g]
            nr = ar * sr - ai * si + x_ref[g, r, :, re]
            ni = ar * si + ai * sr + x_ref[g, r, :, im]
            x_ref[g, r, :, re] = nr
            x_ref[g, r, :, im] = ni
            out.append((nr, ni))
        return tuple(out)

    zero = jnp.zeros((SUBLANES, W), _F32)
    ends = lax.fori_loop(0, seg, local, tuple((zero, zero) for _ in range(G)), unroll=SCAN_UNROLL)

    entry = []
    shift = (SUBLANES - 1) if reverse else 1
    for g in range(G):
        er, ei = ends[g]
        fr = jnp.where(edge_in, carry_ref[g, :, re], pltpu.roll(er, shift, 0))
        fi = jnp.where(edge_in, carry_ref[g, :, im], pltpu.roll(ei, shift, 0))
        for j, dist in enumerate((1, 2, 4)):
            pr, pi = tab_ref[g, 2 + 2 * j], tab_ref[g, 3 + 2 * j]
            sh = (SUBLANES - dist) if reverse else dist
            sr, si = pltpu.roll(fr, sh, 0), pltpu.roll(fi, sh, 0)
            fr, fi = fr + pr * sr - pi * si, fi + pr * si + pi * sr
        br, bi = tab_ref[g, 8], tab_ref[g, 9]
        outr = br * fr - bi * fi + er
        outi = br * fi + bi * fr + ei
        carry_ref[g, :, re] = jnp.broadcast_to(outr[edge_out:edge_out + 1, :], (SUBLANES, W))
        carry_ref[g, :, im] = jnp.broadcast_to(outi[edge_out:edge_out + 1, :], (SUBLANES, W))
        entry.append((fr, fi))
        if entry_ref is not None:
            entry_ref[g, :, re] = fr
            entry_ref[g, :, im] = fi

    add_da = da_ref is not None

    def fix(r, st, first=False):
        out = []
        for g in range(G):
            fr, fi = entry[g]
            pwr, pwi = pw_ref[g, r, :, re], pw_ref[g, r, :, im]
            xr = x_ref[g, r, :, re] + (pwr * fr - pwi * fi)
            xi = x_ref[g, r, :, im] + (pwr * fi + pwi * fr)
            x_ref[g, r, :, re] = xr
            x_ref[g, r, :, im] = xi
            if add_da:
                prev = fwd_entry_ref.at[g] if first else fwd_ref.at[g, r - 1]
                xpr, xpi = prev[:, re], prev[:, im]
                out.append((st[g][0] + (xr * xpr + xi * xpi), st[g][1] + (xi * xpr - xr * xpi)))
            else:
                out.append(st[g])
        return tuple(out)

    if add_da:
        st = fix(0, tuple((zero, zero) for _ in range(G)), first=True)
        st = lax.fori_loop(1, seg, fix, st, unroll=SCAN_UNROLL)
        for g in range(G):
            da_ref[g, :, re] += st[g][0]
            da_ref[g, :, im] += st[g][1]
    else:
        lax.fori_loop(0, seg, fix, tuple((zero[0:1, 0:LANES],) for _ in range(G)), unroll=SCAN_UNROLL)


def _s5_specs(seg, time=lambda t: t):
    G = KB_PER_STEP
    return dict(
        x=pl.BlockSpec((G, seg, SUBLANES, 2 * KB_STATES), lambda kb, t: (kb, time(t), 0, 0)),
        ent=pl.BlockSpec((G, 1, SUBLANES, 2 * KB_STATES), lambda kb, t: (kb, time(t), 0, 0)),
        b=pl.BlockSpec((G, LANES, 2 * KB_STATES), lambda kb, t: (kb, 0, 0)),
        c=pl.BlockSpec((G, 2 * KB_STATES, LANES), lambda kb, t: (kb, 0, 0)),
        tab=pl.BlockSpec((G, 10, SUBLANES, KB_STATES), lambda kb, t: (kb, 0, 0, 0)),
        pw=pl.BlockSpec((G, seg, 1, 2 * KB_STATES), lambda kb, t: (kb, 0, 0, 0)),
        d=pl.BlockSpec((1, G * LANES), lambda kb, t: (0, kb)),
    )


def _s5_fwd(u, bmat, cmat, tab_f, pw_f, d_skip, tb):
    L = u.shape[0]
    nt = L // tb
    seg = tb // SUBLANES
    G = KB_PER_STEP
    ucol = pl.BlockSpec((tb, G * LANES), lambda kb, t: (t, kb))
    sp = _s5_specs(seg)

    def body(u_ref, b_ref, c_ref, tab_ref, pw_ref, d_ref, s_ref, x_ref, ent_ref, up_scr, y_scr, carry_scr):
        @pl.when(pl.program_id(1) == 0)
        def _():
            carry_scr[...] = jnp.zeros_like(carry_scr)

        _rows_to_segments(up_scr, u_ref, seg)
        for g in range(G):
            x_ref[g] = _dot(up_scr[g].astype(_BF), b_ref[g]).reshape(seg, SUBLANES, 2 * KB_STATES)
        _scan_segments(x_ref, tab_ref, pw_ref, carry_scr, seg, reverse=False, entry_ref=ent_ref.at[:, 0])
        for g in range(G):
            y = _dot(x_ref[g].reshape(tb, 2 * KB_STATES).astype(_BF), c_ref[g])
            y_scr[g] = y + d_ref[:, g * LANES:(g + 1) * LANES] * up_scr[g]
        _segments_to_rows(s_ref, y_scr, seg)

    return pl.pallas_call(
        body, name="s5_fwd", grid=(N_KB // G, nt),
        in_specs=[ucol, sp["b"], sp["c"], sp["tab"], sp["pw"], sp["d"]],
        out_specs=[ucol, sp["x"], sp["ent"]],
        out_shape=[jax.ShapeDtypeStruct((L, SSM_W), _F32),
                   jax.ShapeDtypeStruct((N_KB, L // SUBLANES, SUBLANES, 2 * KB_STATES), _F32),
                   jax.ShapeDtypeStruct((N_KB, nt, SUBLANES, 2 * KB_STATES), _F32)],
        scratch_shapes=[pltpu.VMEM((G, tb, LANES), _F32)] * 2 + [pltpu.VMEM((G, SUBLANES, 2 * KB_STATES), _F32)],
        compiler_params=_params("parallel", "arbitrary"),
    )(u, bmat, cmat, tab_f, pw_f, d_skip)


def _mixout_fwd(s, y_ret, x, w_glu, w_out, g2, tm):
    L = s.shape[0]

    def body(s_ref, yr_ref, x_ref, wg_ref, wo_ref, g_ref, ys_ref, glu_ref, cat_ref, mix_ref, x2_ref):
        for rows in _row_chunks(tm):
            ys = _gelu(s_ref[rows, :]).astype(_BF)
            ys_ref[rows, :] = ys
            glu = _dot(ys, wg_ref[...])
            glu_ref[rows, :] = glu
            cat_ref[rows, :RET_W] = yr_ref[rows, :]
            cat_ref[rows, RET_W:] = (glu[:, :SSM_W] * _sigmoid(glu[:, SSM_W:])).astype(_BF)
            mix = _dot(cat_ref[rows, :], wo_ref[...])
            mix_ref[rows, :] = mix
            x2_ref[rows, :] = x_ref[rows, :] + mix * _rms_r(mix) * g_ref[...]

    return pl.pallas_call(
        body, name="mixout_fwd", grid=(L // tm,),
        in_specs=[_row_spec(tm, SSM_W), _row_spec(tm, RET_W), _row_spec(tm, D_MODEL),
                  _weight_spec((SSM_W, 2 * SSM_W)), _weight_spec((D_MODEL, D_MODEL)), _full_spec((1, D_MODEL))],
        out_specs=[_row_spec(tm, SSM_W), _row_spec(tm, 2 * SSM_W), _row_spec(tm, D_MODEL),
                   _row_spec(tm, D_MODEL), _row_spec(tm, D_MODEL)],
        out_shape=[jax.ShapeDtypeStruct((L, SSM_W), _BF), jax.ShapeDtypeStruct((L, 2 * SSM_W), _F32),
                   jax.ShapeDtypeStruct((L, D_MODEL), _BF), jax.ShapeDtypeStruct((L, D_MODEL), _F32),
                   jax.ShapeDtypeStruct((L, D_MODEL), _F32)],
        compiler_params=_params("parallel"),
    )(s, y_ret, x, w_glu, w_out, g2)


FF1_COLS = D_FF // N_DEV


def _ff1_fwd(x2, g3, w1, tm):
    L = x2.shape[0]

    def body(x_ref, g_ref, w_ref, h_ref, f_ref):
        xv = x_ref[...]
        h = (xv * _rms_r(xv) * g_ref[...]).astype(_BF)
        h_ref[...] = h
        for j in range(N_DEV):
            f_ref[:, j * FF1_COLS:(j + 1) * FF1_COLS] = _dot(h, w_ref[j])

    return pl.pallas_call(
        body, name="ff1_fwd", grid=(L // tm,),
        in_specs=[_row_spec(tm, D_MODEL), _full_spec((1, D_MODEL)), _weight_spec((N_DEV, D_MODEL, FF1_COLS))],
        out_specs=[_row_spec(tm, D_MODEL), _row_spec(tm, D_FF)],
        out_shape=[jax.ShapeDtypeStruct((L, D_MODEL), _BF), jax.ShapeDtypeStruct((L, D_FF), _F32)],
        compiler_params=_params("parallel"),
    )(x2, g3, w1)


def _ff2_loss(f1, x2, tgt, g4, w2, tm):
    L = f1.shape[0]

    def body(f_ref, x_ref, t_ref, g_ref, w_ref, dy_ref, dm_ref, dg_ref, ls_ref):
        @pl.when(pl.program_id(0) == 0)
        def _():
            dg_ref[...] = jnp.zeros_like(dg_ref)
            ls_ref[...] = jnp.zeros_like(ls_ref)

        g = g_ref[...]
        for rows in _row_chunks(tm):
            rl = jnp.maximum(f_ref[rows, :], 0.0)
            m = _dot((rl * rl).astype(_BF), w_ref[...])
            y = x_ref[rows, :] + m * _rms_r(m) * g
            err = y - t_ref[rows, :]
            ls_ref[...] += jnp.sum(err * err, axis=0, keepdims=True)
            dy = err * (1.0 / D_MODEL)
            dy_ref[rows, :] = dy
            dm, dgr = _rms_bwd(m, g, dy)
            dm_ref[rows, :] = dm.astype(_BF)
            dg_ref[...] += jnp.sum(dgr, axis=0, keepdims=True)

    return pl.pallas_call(
        body, name="ff2_loss", grid=(L // tm,),
        in_specs=[_row_spec(tm, D_FF), _row_spec(tm, D_MODEL), _row_spec(tm, D_MODEL),
                  _full_spec((1, D_MODEL)), _weight_spec((D_FF, D_MODEL))],
        out_specs=[_row_spec(tm, D_MODEL), _row_spec(tm, D_MODEL), _full_spec((1, D_MODEL)), _full_spec((1, D_MODEL))],
        out_shape=[jax.ShapeDtypeStruct((L, D_MODEL), _F32), jax.ShapeDtypeStruct((L, D_MODEL), _BF),
                   jax.ShapeDtypeStruct((1, D_MODEL), _F32), jax.ShapeDtypeStruct((1, D_MODEL), _F32)],
        compiler_params=_params("arbitrary"),
    )(f1, x2, tgt, g4, w2)


def _ff2_bwd(dm, f1, w2, tm, tn):
    L = dm.shape[0]
    last = L // tm - 1

    def body(dm_ref, f_ref, w_ref, df_ref, dw_ref, acc):
        @pl.when(pl.program_id(1) == 0)
        def _():
            acc[...] = jnp.zeros_like(acc)

        dmv = dm_ref[...]
        rl = jnp.maximum(f_ref[...], 0.0)
        df_ref[...] = (_dot_nt(dmv, w_ref[...]) * (2.0 * rl)).astype(_BF)
        acc[...] += _dot_tn((rl * rl).astype(_BF), dmv)

        @pl.when(pl.program_id(1) == last)
        def _():
            dw_ref[...] = acc[...].astype(_BF)

    return pl.pallas_call(
        body, name="ff2_bwd", grid=(D_FF // tn, L // tm),
        in_specs=[pl.BlockSpec((tm, D_MODEL), lambda j, i: (i, 0)), pl.BlockSpec((tm, tn), lambda j, i: (i, j)),
                  pl.BlockSpec((tn, D_MODEL), lambda j, i: (j, 0))],
        out_specs=[pl.BlockSpec((tm, tn), lambda j, i: (i, j)), pl.BlockSpec((tn, D_MODEL), lambda j, i: (j, 0))],
        out_shape=[jax.ShapeDtypeStruct((L, D_FF), _BF), jax.ShapeDtypeStruct((D_FF, D_MODEL), _BF)],
        scratch_shapes=[pltpu.VMEM((tn, D_MODEL), _F32)],
        compiler_params=_params("parallel", "arbitrary"),
    )(dm, f1, w2)


def _ff1_bwd(df1, w1, x2, mix, dy, g3, g2, tm):
    L = df1.shape[0]

    def body(df_ref, w_ref, x2_ref, mix_ref, dy_ref, g3_ref, g2_ref, dx2_ref, dmix_ref, dg3_ref, dg2_ref):
        @pl.when(pl.program_id(0) == 0)
        def _():
            dg3_ref[...] = jnp.zeros_like(dg3_ref)
            dg2_ref[...] = jnp.zeros_like(dg2_ref)

        for rows in _row_chunks(tm):
            dh = _dot_nt(df_ref[rows, 0:FF1_COLS], w_ref[0])
            for j in range(1, N_DEV):
                dh = dh + _dot_nt(df_ref[rows, j * FF1_COLS:(j + 1) * FF1_COLS], w_ref[j])
            dz, dgr = _rms_bwd(x2_ref[rows, :], g3_ref[...], dh)
            dg3_ref[...] += jnp.sum(dgr, axis=0, keepdims=True)
            dx2 = dy_ref[rows, :] + dz
            dx2_ref[rows, :] = dx2
            dmx, dgr2 = _rms_bwd(mix_ref[rows, :], g2_ref[...], dx2)
            dg2_ref[...] += jnp.sum(dgr2, axis=0, keepdims=True)
            dmix_ref[rows, :] = dmx.astype(_BF)

    vec = _full_spec((1, D_MODEL))
    return pl.pallas_call(
        body, name="ff1_bwd", grid=(L // tm,),
        in_specs=[_row_spec(tm, D_FF), _weight_spec((N_DEV, D_MODEL, FF1_COLS)), _row_spec(tm, D_MODEL),
                  _row_spec(tm, D_MODEL), _row_spec(tm, D_MODEL), vec, vec],
        out_specs=[_row_spec(tm, D_MODEL), _row_spec(tm, D_MODEL), vec, vec],
        out_shape=[jax.ShapeDtypeStruct((L, D_MODEL), _F32), jax.ShapeDtypeStruct((L, D_MODEL), _BF),
                   jax.ShapeDtypeStruct((1, D_MODEL), _F32), jax.ShapeDtypeStruct((1, D_MODEL), _F32)],
        compiler_params=_params("arbitrary"),
    )(df1, w1, x2, mix, dy, g3, g2)


def _matmul_tn(a, b, tm, tn, name, slots=False):
    L, K = a.shape
    N = b.shape[1]
    last = L // tm - 1

    def body(a_ref, b_ref, o_ref, acc):
        @pl.when(pl.program_id(1) == 0)
        def _():
            acc[...] = jnp.zeros_like(acc)

        acc[...] += _dot_tn(a_ref[...].astype(_BF), b_ref[...].astype(_BF))

        @pl.when(pl.program_id(1) == last)
        def _():
            if slots:
                o_ref[0] = acc[...].astype(_BF)
            else:
                o_ref[...] = acc[...].astype(_BF)

    if slots:
        out_spec = pl.BlockSpec((1, K, tn), lambda j, i: (j, 0, 0))
        out_shape = jax.ShapeDtypeStruct((N // tn, K, tn), _BF)
    else:
        out_spec = pl.BlockSpec((K, tn), lambda j, i: (0, j))
        out_shape = jax.ShapeDtypeStruct((K, N), _BF)
    return pl.pallas_call(
        body, name=name, grid=(N // tn, L // tm),
        in_specs=[pl.BlockSpec((tm, K), lambda j, i: (i, 0)), pl.BlockSpec((tm, tn), lambda j, i: (i, j))],
        out_specs=out_spec, out_shape=out_shape,
        scratch_shapes=[pltpu.VMEM((K, tn), _F32)],
        compiler_params=_params("parallel", "arbitrary"),
    )(a, b)


def _mixout_bwd(dmix, w_out, w_glu, glu, s, o, gate, ggn, tm):
    L = dmix.shape[0]

    def body(dmix_ref, wo_ref, wg_ref, glu_ref, s_ref, o_ref, gate_ref, ggn_ref,
             dglu_ref, ds_ref, dgate_ref, do_ref, dggn_ref):
        @pl.when(pl.program_id(0) == 0)
        def _():
            dggn_ref[...] = jnp.zeros_like(dggn_ref)

        dcat = _dot_nt(dmix_ref[...], wo_ref[...])
        dy_ret, dy_ssm = dcat[:, :RET_W], dcat[:, RET_W:]
        glu = glu_ref[...]
        ga, sg = glu[:, :SSM_W], _sigmoid(glu[:, SSM_W:])
        dga = (dy_ssm * sg).astype(_BF)
        dgb = (dy_ssm * ga * sg * (1.0 - sg)).astype(_BF)
        dglu_ref[:, :SSM_W] = dga
        dglu_ref[:, SSM_W:] = dgb
        dys = _dot_nt(dga, wg_ref[:, :SSM_W]) + _dot_nt(dgb, wg_ref[:, SSM_W:])
        ds_ref[...] = dys * _gelu_grad(s_ref[...])
        gt = gate_ref[...]
        sgt = _sigmoid(gt)
        ggn = ggn_ref[...]
        for hh in range(N_HEAD):
            cols = slice(hh * HEAD_D, (hh + 1) * HEAD_D)
            ov = o_ref[:, cols]
            dlt = ov - jnp.mean(ov, axis=-1, keepdims=True)
            rstd = lax.rsqrt(jnp.mean(dlt * dlt, axis=-1, keepdims=True) + NORM_EPS)
            on = dlt * rstd
            dyr = dy_ret[:, cols] * (gt[:, cols] * sgt[:, cols])
            dgate_ref[:, cols] = dy_ret[:, cols] * (on * ggn[:, cols]) * (sgt[:, cols] * (1.0 + gt[:, cols] * (1.0 - sgt[:, cols])))
            dggn_ref[:, cols] += jnp.sum(dyr * on, axis=0, keepdims=True)
            don = dyr * ggn[:, cols]
            do = rstd * (don - jnp.mean(don, axis=-1, keepdims=True) - on * jnp.mean(don * on, axis=-1, keepdims=True))
            do_ref[:, cols] = do.astype(_BF)

    return pl.pallas_call(
        body, name="mixout_bwd", grid=(L // tm,),
        in_specs=[_row_spec(tm, D_MODEL), _weight_spec((D_MODEL, D_MODEL)), _weight_spec((SSM_W, 2 * SSM_W)),
                  _row_spec(tm, 2 * SSM_W), _row_spec(tm, SSM_W), _row_spec(tm, RET_W), _row_spec(tm, RET_W),
                  _full_spec((1, RET_W))],
        out_specs=[_row_spec(tm, 2 * SSM_W), _row_spec(tm, SSM_W), _row_spec(tm, RET_W), _row_spec(tm, RET_W),
                   _full_spec((1, RET_W))],
        out_shape=[jax.ShapeDtypeStruct((L, 2 * SSM_W), _BF), jax.ShapeDtypeStruct((L, SSM_W), _F32),
                   jax.ShapeDtypeStruct((L, RET_W), _F32), jax.ShapeDtypeStruct((L, RET_W), _BF),
                   jax.ShapeDtypeStruct((1, RET_W), _F32)],
        compiler_params=_params("arbitrary"),
    )(dmix, w_out, w_glu, glu, s, o, gate, ggn)


def _s5_bwd(u, ds, xs, ent, bmat, cmat, tab_r, pw_r, d_skip, tb):
    L = u.shape[0]
    nt = L // tb
    seg = tb // SUBLANES
    G = KB_PER_STEP
    rcol = pl.BlockSpec((tb, G * LANES), lambda kb, t: (nt - 1 - t, kb))
    sp = _s5_specs(seg, time=lambda t: nt - 1 - t)
    aspec = pl.BlockSpec((G, SUBLANES, 2 * KB_STATES), lambda kb, t: (kb, 0, 0))

    def body(u_ref, ds_ref, x_ref, ent_ref, b_ref, c_ref, tr_ref, pr_ref, d_ref,
             du_ref, db_ref, dc_ref, da_ref, dd_ref, up_scr, dp_scr, g_scr, lc_scr):
        @pl.when(pl.program_id(1) == 0)
        def _():
            lc_scr[...] = jnp.zeros_like(lc_scr)
            db_ref[...] = jnp.zeros_like(db_ref)
            dc_ref[...] = jnp.zeros_like(dc_ref)
            da_ref[...] = jnp.zeros_like(da_ref)
            dd_ref[...] = jnp.zeros_like(dd_ref)

        _rows_to_segments(up_scr, u_ref, seg)
        _rows_to_segments(dp_scr, ds_ref, seg)
        for g in range(G):
            g_scr[g] = _dot_nt(dp_scr[g].astype(_BF), c_ref[g]).reshape(seg, SUBLANES, 2 * KB_STATES)
        _scan_segments(g_scr, tr_ref, pr_ref, lc_scr, seg, reverse=True, fwd_ref=x_ref, fwd_entry_ref=ent_ref.at[:, 0],
                       da_ref=da_ref)
        for g in range(G):
            cols = slice(g * LANES, (g + 1) * LANES)
            uv, dsv = up_scr[g], dp_scr[g]
            ub, dsb = uv.astype(_BF), dsv.astype(_BF)
            lamb = g_scr[g].reshape(tb, 2 * KB_STATES).astype(_BF)
            db_ref[g] += _dot_tn(ub, lamb)
            dc_ref[g] += _dot_tn(dsb, x_ref[g].reshape(tb, 2 * KB_STATES).astype(_BF))
            dd_ref[:, cols] += jnp.sum(dsv * uv, axis=0, keepdims=True)
            up_scr[g] = _dot_nt(lamb, b_ref[g]) + d_ref[:, cols] * dsv
        _segments_to_rows(du_ref, up_scr, seg)

    return pl.pallas_call(
        body, name="s5_bwd", grid=(N_KB // G, nt),
        in_specs=[rcol, rcol, sp["x"], sp["ent"], sp["b"], sp["c"], sp["tab"], sp["pw"], sp["d"]],
        out_specs=[rcol, sp["b"], sp["b"], aspec, sp["d"]],
        out_shape=[jax.ShapeDtypeStruct((L, SSM_W), _F32),
                   jax.ShapeDtypeStruct((N_KB, LANES, 2 * KB_STATES), _F32),
                   jax.ShapeDtypeStruct((N_KB, LANES, 2 * KB_STATES), _F32),
                   jax.ShapeDtypeStruct((N_KB, SUBLANES, 2 * KB_STATES), _F32),
                   jax.ShapeDtypeStruct((1, SSM_W), _F32)],
        scratch_shapes=[pltpu.VMEM((G, tb, LANES), _F32)] * 2
        + [pltpu.VMEM((G, seg, SUBLANES, 2 * KB_STATES), _F32), pltpu.VMEM((G, SUBLANES, 2 * KB_STATES), _F32)],
        compiler_params=_params("parallel", "arbitrary"),
    )(u, ds, xs, ent, bmat, cmat, tab_r, pw_r, d_skip)


def _retention_bwd(q, k, v, do, r_prev, consts, cosf, sinf):
    L = q.shape[0]
    nc = L // CHUNK
    cps = math.gcd(RET_STEP_CHUNKS, nc)
    nb = nc // cps
    blk = pl.BlockSpec((cps * CHUNK, RET_W), lambda n: (nb - 1 - n, 0))
    rope_blk = pl.BlockSpec((cps * CHUNK, HEAD_D), lambda n: (nb - 1 - n, 0))

    def body(q_ref, k_ref, v_ref, do_ref, rp_ref, dm_ref, xi_ref, zeta_ref, gc_ref, cos_ref, sin_ref,
             dq_ref, dk_ref, dv_ref, g_scr):
        @pl.when(pl.program_id(0) == 0)
        def _():
            g_scr[...] = jnp.zeros_like(g_scr)

        for hh in range(N_HEAD):
            cols = slice(hh * HEAD_D, (hh + 1) * HEAD_D)
            dm, zeta = dm_ref[hh], zeta_ref[hh]
            gst = g_scr[hh]
            for c in reversed(range(cps)):
                rows = slice(c * CHUNK, (c + 1) * CHUNK)
                qv, kv, vv, dov = q_ref[rows, cols], k_ref[rows, cols], v_ref[rows, cols], do_ref[rows, cols]
                rb = rp_ref[hh, c].astype(_BF)
                gb = gst.astype(_BF)
                sb = (_dot_nt(qv, kv) * dm).astype(_BF)
                dab = (_dot_nt(dov, vv) * dm).astype(_BF)
                dox = (dov.astype(_F32) * xi_ref[hh]).astype(_BF)
                vz = (vv.astype(_F32) * zeta).astype(_BF)
                dq = _dot(dab, kv) + _dot_nt(dox, rb)
                dk = _dot_tn(dab, qv) + _dot_nt(vz, gb)
                dv = _dot_tn(sb, dov) + _dot(kv, gb) * zeta
                gst = gc_ref[hh, 0:1, :] * gst + _dot_tn(qv, dox)
                cs, sn = cos_ref[rows, :], sin_ref[rows, :]
                dq_ref[rows, cols] = _rope_t(dq, cs, sn).astype(_BF)
                dk_ref[rows, cols] = (_rope_t(dk, cs, sn) * (HEAD_D ** -0.5)).astype(_BF)
                dv_ref[rows, cols] = dv.astype(_BF)
            g_scr[hh] = gst

    return pl.pallas_call(
        body, name="retention_bwd", grid=(nb,),
        in_specs=[blk, blk, blk, blk, pl.BlockSpec((N_HEAD, cps, HEAD_D, HEAD_D), lambda n: (0, nb - 1 - n, 0, 0))]
        + _head_specs() + [rope_blk, rope_blk],
        out_specs=[blk, blk, blk],
        out_shape=[jax.ShapeDtypeStruct((L, RET_W), _BF)] * 3,
        scratch_shapes=[pltpu.VMEM((N_HEAD, HEAD_D, HEAD_D), _F32)],
        compiler_params=_params("arbitrary"),
    )(q, k, v, do, r_prev, *consts, cosf, sinf)


def _inproj_bwd(pieces, w_in_t, x, dx2, g1, tm):
    L = x.shape[0]

    def body(p0, p1, p2, p3, p4, w_ref, x_ref, dx2_ref, g_ref, dx_ref, dg_ref):
        @pl.when(pl.program_id(0) == 0)
        def _():
            dg_ref[...] = jnp.zeros_like(dg_ref)

        for rows in _row_chunks(tm):
            dh = None
            for j, p in enumerate((p0, p1, p2, p3, p4)):
                part = _dot(p[rows, :].astype(_BF), w_ref[j * RET_W:(j + 1) * RET_W, :])
                dh = part if dh is None else dh + part
            dz, dgr = _rms_bwd(x_ref[rows, :], g_ref[...], dh)
            dx_ref[rows, :] = dx2_ref[rows, :] + dz
            dg_ref[...] += jnp.sum(dgr, axis=0, keepdims=True)

    return pl.pallas_call(
        body, name="inproj_bwd", grid=(L // tm,),
        in_specs=[_row_spec(tm, RET_W)] * 5 + [_weight_spec((IN_COLS, D_MODEL)), _row_spec(tm, D_MODEL),
                                                 _row_spec(tm, D_MODEL), _full_spec((1, D_MODEL))],
        out_specs=[_row_spec(tm, D_MODEL), _full_spec((1, D_MODEL))],
        out_shape=[jax.ShapeDtypeStruct((L, D_MODEL), _F32), jax.ShapeDtypeStruct((1, D_MODEL), _F32)],
        compiler_params=_params("arbitrary"),
    )(*pieces, w_in_t, x, dx2, g1)


def _sum_adamw(parts, w, m, v, tr, name):
    _, R, Cc = parts.shape

    def body(p_ref, w_ref, m_ref, v_ref, g_ref, d_ref, nm_ref, nv_ref):
        gv = p_ref[0].astype(_F32)
        for s in range(1, N_DEV):
            gv = gv + p_ref[s].astype(_F32)
        g_ref[...] = gv
        nm = ADAM_B1 * m_ref[...] + (1.0 - ADAM_B1) * gv
        nv = ADAM_B2 * v_ref[...] + (1.0 - ADAM_B2) * (gv * gv)
        m_hat = nm / (1.0 - ADAM_B1 ** ADAM_STEP)
        v_hat = nv / (1.0 - ADAM_B2 ** ADAM_STEP)
        d_ref[...] = -ADAM_LR * (m_hat / (jnp.sqrt(v_hat) + ADAM_EPS) + ADAM_WD * w_ref[...])
        nm_ref[...] = nm
        nv_ref[...] = nv

    spec = _row_spec(tr, Cc)
    return pl.pallas_call(
        body, name=name, grid=(R // tr,),
        in_specs=[pl.BlockSpec((N_DEV, tr, Cc), lambda i: (0, i, 0))] + [spec] * 3, out_specs=[spec] * 4,
        out_shape=[jax.ShapeDtypeStruct((R, Cc), _F32)] * 4,
        compiler_params=_params("parallel"),
    )(parts, w, m, v)


def _my_place():
    return lax.axis_index("x"), lax.axis_index("y"), lax.axis_index("c")


def _all_gather(blocks):
    n = len(blocks)

    def body(*refs):
        x_refs, out_refs, done_ref = refs[:n], refs[n:2 * n], refs[2 * n]
        send_sems, recv_sems, local_sems = refs[2 * n + 1:]
        done_ref[...] = jnp.zeros_like(done_ref)
        x, y, c = _my_place()
        me, sibling = (x, y, c), (x, y, 1 - c)
        chips = [(1 - x, y), (x, 1 - y), (1 - x, 1 - y)]

        def slot(a, px, py, pc):
            return out_refs[a].at[4 * px + 2 * py + pc]

        def copy(a, k, blk, to, own=False):
            return pltpu.make_async_remote_copy(
                src_ref=x_refs[a] if own else slot(a, *blk), dst_ref=slot(a, *blk),
                send_sem=send_sems.at[a, k], recv_sem=recv_sems.at[a, k], device_id=to, device_id_type=MESH)

        mine = [pltpu.make_async_copy(x_refs[a], slot(a, *me), local_sems.at[a]) for a in range(n)]
        for cp in mine:
            cp.start()
        first = []
        for a in range(n):
            first.append(copy(a, 0, me, sibling, own=True))
            first += [copy(a, 1 + j, me, (*chip, c), own=True) for j, chip in enumerate(chips)]
        for cp in first:
            cp.start()
        passed = []
        for j, chip in enumerate(chips):
            for a in range(n):
                copy(a, 1 + j, (*chip, c), me).wait_recv()
                fwd = copy(a, 4 + j, (*chip, c), sibling)
                fwd.start()
                passed.append(fwd)
        for a in range(n):
            copy(a, 0, sibling, me).wait_recv()
            for j, chip in enumerate(chips):
                copy(a, 4 + j, (*chip, 1 - c), me).wait_recv()
        for cp in first + passed:
            cp.wait_send()
        for cp in mine:
            cp.wait()

    any_spec = pl.BlockSpec(memory_space=pl.ANY)
    outs = pl.pallas_call(
        body, name="weights_all_gather",
        in_specs=[any_spec] * n, out_specs=[any_spec] * n + [pl.BlockSpec(memory_space=pltpu.VMEM)],
        out_shape=[jax.ShapeDtypeStruct((N_DEV,) + b.shape, b.dtype) for b in blocks]
        + [jax.ShapeDtypeStruct((SUBLANES, LANES), _F32)],
        scratch_shapes=[pltpu.SemaphoreType.DMA((n, 7)), pltpu.SemaphoreType.DMA((n, 7)), pltpu.SemaphoreType.DMA((n,))],
    )(*blocks)
    return outs[:n], outs[n]


def _exchange(bigs, small):
    n = len(bigs)
    r = small.shape[0]

    def body(*refs):
        in_refs, out_refs = refs[:n + 1], refs[n + 1:2 * n + 2]
        send_sems, recv_sems, local_sems = refs[2 * n + 2:]
        x, y, c = _my_place()
        me = 4 * x + 2 * y + c
        own = [pltpu.make_async_copy(in_refs[a].at[me], out_refs[a].at[me], local_sems.at[a]) for a in range(n)]
        own.append(pltpu.make_async_copy(in_refs[n], out_refs[n].at[me], local_sems.at[n]))
        for cp in own:
            cp.start()
        copies = []
        for kk in range(1, N_DEV):
            px, py, pc = x ^ (kk >> 2), y ^ ((kk >> 1) & 1), c ^ (kk & 1)
            peer = 4 * px + 2 * py + pc
            for a in range(n + 1):
                src = in_refs[a].at[peer] if a < n else in_refs[a]
                copies.append(pltpu.make_async_remote_copy(
                    src_ref=src, dst_ref=out_refs[a].at[me],
                    send_sem=send_sems.at[a, kk - 1], recv_sem=recv_sems.at[a, kk - 1],
                    device_id=(px, py, pc), device_id_type=MESH))
        for cp in copies:
            cp.start()
        for cp in copies:
            cp.wait_recv()
        for cp in copies:
            cp.wait_send()
        for cp in own:
            cp.wait()

    any_spec = pl.BlockSpec(memory_space=pl.ANY)
    outs = pl.pallas_call(
        body, name="grad_exchange",
        in_specs=[any_spec] * (n + 1), out_specs=[any_spec] * (n + 1),
        out_shape=[jax.ShapeDtypeStruct(b.shape, b.dtype) for b in bigs]
        + [jax.ShapeDtypeStruct((N_DEV, r, LANES), small.dtype)],
        scratch_shapes=[pltpu.SemaphoreType.DMA((n + 1, 7)), pltpu.SemaphoreType.DMA((n + 1, 7)),
                        pltpu.SemaphoreType.DMA((n + 1,))],
    )(*bigs, small)
    return outs[:n], outs[n]


HBM_SPEC = pl.BlockSpec(memory_space=pltpu.HBM)
SEM_SPEC = pl.BlockSpec(memory_space=pltpu.SEMAPHORE)
DATAFLOW = pltpu.SideEffectType.DATAFLOW_SIDE_EFFECTING


def _my_index():
    x, y, c = _my_place()
    return 4 * x + 2 * y + c


def _landing(own_block):
    zone = lax.empty((N_DEV,) + own_block.shape, own_block.dtype)
    return lax.dynamic_update_index_in_dim(zone, own_block, _my_index(), 0)


def _split_copies(src_refs, land_refs, send_sems, recv_sems, gather):
    x, y, c = _my_place()
    me = 4 * x + 2 * y + c
    copies = []
    for kk in range(1, N_DEV):
        px, py, pc = x ^ (kk >> 2), y ^ ((kk >> 1) & 1), c ^ (kk & 1)
        peer = 4 * px + 2 * py + pc
        for a, (src, land) in enumerate(zip(src_refs, land_refs)):
            copies.append(pltpu.make_async_remote_copy(
                src_ref=src if gather else src.at[peer], dst_ref=land.at[me],
                send_sem=send_sems.at[a * 7 + kk - 1], recv_sem=recv_sems.at[a * 7 + kk - 1],
                device_id=(px, py, pc), device_id_type=MESH))
    return copies


def _split_start(srcs, lands, gather, name):
    n = len(srcs)

    def body(*refs):
        src_refs, land_refs = refs[:n], refs[n:2 * n]
        send_sems, recv_sems = refs[2 * n], refs[2 * n + 1]
        token = refs[-1]
        for cp in _split_copies(src_refs, land_refs, send_sems, recv_sems, gather):
            cp.start()
        token[...] = jnp.zeros_like(token)

    outs = pl.pallas_call(
        body, name=name,
        out_shape=(pltpu.SemaphoreType.DMA((7 * n,)), pltpu.SemaphoreType.DMA((7 * n,)),
                   *[pltpu.HBM(t.shape, t.dtype) for t in srcs], *[pltpu.HBM(t.shape, t.dtype) for t in lands],
                   jax.ShapeDtypeStruct((SUBLANES, LANES), _F32)),
        in_specs=[HBM_SPEC] * (2 * n),
        out_specs=(SEM_SPEC, SEM_SPEC, *[HBM_SPEC] * (2 * n), pl.BlockSpec(memory_space=pltpu.VMEM)),
        input_output_aliases={i: 2 + i for i in range(2 * n)},
        compiler_params=pltpu.CompilerParams(has_side_effects=DATAFLOW),
    )(*[pltpu.with_memory_space_constraint(t, pltpu.HBM) for t in list(srcs) + list(lands)])
    return outs[0], outs[1], outs[2:2 + n], outs[2 + n:2 + 2 * n], outs[-1]


def _split_wait(send_sems, recv_sems, srcs, lands, after, gather, name):
    n = len(srcs)

    def body(*refs):
        src_refs, land_refs = refs[:n], refs[n:2 * n]
        send_s, recv_s = refs[2 * n], refs[2 * n + 1]
        for cp in _split_copies(src_refs, land_refs, send_s, recv_s, gather):
            cp.wait_send()
            cp.wait_recv()

    outs = pl.pallas_call(
        body, name=name,
        out_shape=tuple(pltpu.HBM(t.shape, t.dtype) for t in list(srcs) + list(lands)),
        in_specs=[HBM_SPEC] * (2 * n) + [SEM_SPEC, SEM_SPEC, pl.BlockSpec(memory_space=pl.ANY)],
        out_specs=tuple([HBM_SPEC] * (2 * n)),
        input_output_aliases={i: i for i in range(2 * n)},
        compiler_params=pltpu.CompilerParams(has_side_effects=DATAFLOW),
    )(*srcs, *lands, send_sems, recv_sems, after)
    return outs[n:]


def _discretize(lam_re, lam_im, log_dt, b_re, b_im):
    lr = jnp.minimum(lam_re, -1e-4)
    li = lam_im
    dt = jnp.exp(log_dt)[:, None]
    er = jnp.exp(lr * dt)
    ar, ai = er * jnp.cos(li * dt), er * jnp.sin(li * dt)
    den = lr * lr + li * li
    cr = ((ar - 1.0) * lr + ai * li) / den
    ci = (ai * lr - (ar - 1.0) * li) / den
    bbr = cr[:, :, None] * b_re - ci[:, :, None] * b_im
    bbi = cr[:, :, None] * b_im + ci[:, :, None] * b_re
    return ar, ai, bbr, bbi


def _cmul(ar, ai, br, bi):
    return ar * br - ai * bi, ar * bi + ai * br


def _cpowers(ar, ai, n):
    pr, pi = ar[None], ai[None]
    while pr.shape[0] < n:
        nr, ni = _cmul(pr, pi, pr[-1][None], pi[-1][None])
        pr, pi = jnp.concatenate([pr, nr]), jnp.concatenate([pi, ni])
    return pr[:n], pi[:n]


def _scan_tables(ar, ai, seg, reverse):
    if reverse:
        ai = -ai
    ar, ai = ar.reshape(N_KB, KB_STATES), ai.reshape(N_KB, KB_STATES)
    pr, pi = _cpowers(ar, ai, seg)
    a1 = (pr[-1], pi[-1])
    a2 = _cmul(*a1, *a1)
    a4 = _cmul(*a2, *a2)
    row = jnp.arange(SUBLANES)[None, :, None]
    wide = lambda t: jnp.broadcast_to(t[:, None, :], (N_KB, SUBLANES, KB_STATES))
    tabs = [wide(ar), wide(ai)]
    for dist, (qr, qi) in ((1, a1), (2, a2), (4, a4)):
        keep = (row < SUBLANES - dist) if reverse else (row >= dist)
        tabs += [jnp.where(keep, wide(qr), 0.0), jnp.where(keep, wide(qi), 0.0)]
    tabs += [wide(a1[0]), wide(a1[1])]
    if reverse:
        pr, pi = pr[::-1], pi[::-1]
    pw = jnp.transpose(jnp.concatenate([pr, pi], axis=-1), (1, 0, 2))[:, :, None, :]
    return jnp.stack(tabs, axis=1).astype(_F32), pw.astype(_F32)


def _block_diag_in(br, bi):
    eye = jnp.eye(GROUPS_PER_KB, dtype=_F32)
    one = lambda t: jnp.einsum("kgpc,gh->kgchp", t.reshape(N_KB, GROUPS_PER_KB, N_STATE, SSM_GC), eye).reshape(
        N_KB, LANES, KB_STATES)
    return jnp.concatenate([one(br), one(bi)], axis=-1)


def _block_diag_in_t(dmat):
    d6 = dmat.reshape(N_KB, GROUPS_PER_KB, SSM_GC, 2, GROUPS_PER_KB, N_STATE)
    eye = jnp.eye(GROUPS_PER_KB, dtype=_F32)
    both = jnp.einsum("kgcrhp,gh->rkgpc", d6, eye).reshape(2, N_GROUP, N_STATE, SSM_GC)
    return both[0], both[1]


def _block_diag_out(c_re, c_im):
    eye = jnp.eye(GROUPS_PER_KB, dtype=_F32)
    one = lambda t: jnp.einsum("kgcp,gh->khpgc", t.reshape(N_KB, GROUPS_PER_KB, SSM_GC, N_STATE), eye).reshape(
        N_KB, KB_STATES, LANES)
    return jnp.concatenate([one(c_re), -one(c_im)], axis=1)


def _block_diag_out_t(dmat_t):
    d6 = dmat_t.reshape(N_KB, GROUPS_PER_KB, SSM_GC, 2, GROUPS_PER_KB, N_STATE)
    eye = jnp.eye(GROUPS_PER_KB, dtype=_F32)
    both = jnp.einsum("kgcrhp,gh->rkgcp", d6, eye).reshape(2, N_GROUP, SSM_GC, N_STATE)
    return both[0], -both[1]


SMALL_NAMES = ("norm_mix_pre", "norm_mix_post", "ret_gn_gain", "ssm_lambda_re", "ssm_lambda_im", "ssm_log_dt",
               "ssm_b_re", "ssm_b_im", "ssm_c_re", "ssm_c_im", "ssm_d", "norm_mlp_pre", "norm_mlp_post")


def _local_grads(x, tgt, small, weights, emit, emit_small, tm, tk, tb, zero=0.0):
    L = x.shape[0]
    g1, g2, ggn = small["norm_mix_pre"], small["norm_mix_post"], small["ret_gn_gain"]
    g3, g4, d_skip = small["norm_mlp_pre"], small["norm_mlp_post"], small["ssm_d"]

    rope = _rope_tables(L)
    consts = _ret_consts()

    disc_in = (small["ssm_lambda_re"][0], small["ssm_lambda_im"][0], small["ssm_log_dt"][0] + zero,
               small["ssm_b_re"][0], small["ssm_b_im"][0])
    (ar, ai, bbr, bbi), disc_vjp = jax.vjp(_discretize, *disc_in)
    bmat = _block_diag_in(bbr, bbi).astype(_BF)
    cmat = _block_diag_out(small["ssm_c_re"][0], small["ssm_c_im"][0]).astype(_BF)
    seg = tb // SUBLANES
    tab_f, pw_f = _scan_tables(ar, ai, seg, False)
    tab_r, pw_r = _scan_tables(ar, ai, seg, True)

    (w_in_t,) = weights("in", pw_r)
    h1, q, k, v, gate, u, cosf, sinf = _inproj_fwd(x, g1, w_in_t, rope, tm)
    o, y_ret, r_prev = _retention_fwd(q, k, v, gate, ggn, consts)
    s, xs, ent = _s5_fwd(u, bmat, cmat, tab_f, pw_f, d_skip, tb)
    w_glu, w_out = weights("mix", s)
    ys, glu, cat, mix, x2 = _mixout_fwd(s, y_ret, x, w_glu, w_out, g2, min(2 * tm, L))
    w_ff1, w_ff2 = weights("mlp", x2)
    h3, f1 = _ff1_fwd(x2, g3, w_ff1, tm)
    dy, dm, dg4, sq = _ff2_loss(f1, x2, tgt, g4, w_ff2, min(2 * tm, L))

    df1, dw_ff2 = _ff2_bwd(dm, f1, w_ff2, min(1024, L), 1024)
    dx2, dmix, dg3, dg2 = _ff1_bwd(df1, w_ff1, x2, mix, dy, g3, g2, min(2 * tm, L))
    dw_ff1 = _matmul_tn(h3, df1, tk, FF1_COLS, "dw_ff1", slots=True)
    zero = emit({"w_ff1": dw_ff1, "w_ff2": dw_ff2})
    dglu, ds, dgate, do, dggn = _mixout_bwd(dmix, w_out, w_glu, glu, s, o, gate, ggn if zero is None else ggn + zero, tm)
    dw_out = _matmul_tn(cat, dmix, tk, 1024, "dw_out")
    dw_glu = _matmul_tn(ys, dglu, tk, 1024, "dw_glu")
    zero = emit({"w_glu": dw_glu, "w_out": dw_out})
    du, dbmat, dcmat, da8, dd = _s5_bwd(u, ds, xs, ent, bmat, cmat, tab_r, pw_r,
                                        d_skip if zero is None else d_skip + zero, tb)
    dq, dk, dv = _retention_bwd(q, k, v, do, r_prev, consts, cosf, sinf)
    pieces = (dq, dk, dv, dgate, du)
    dw_in_t = jnp.concatenate([_matmul_tn(p, h1, tk, D_MODEL, "dw_in_%d" % j) for j, p in enumerate(pieces)], axis=0)
    zero = emit({"w_in": dw_in_t})

    da = jnp.sum(da8, axis=1)
    dar = da[:, :KB_STATES].reshape(N_GROUP, N_STATE)
    dai = da[:, KB_STATES:].reshape(N_GROUP, N_STATE)
    dbr, dbi = _block_diag_in_t(dbmat)
    dlre, dlim, dldt, dbre, dbim = disc_vjp((dar, dai, dbr, dbi))
    dcre, dcim = _block_diag_out_t(dcmat)

    zero2 = emit_small({
        "norm_mix_post": dg2, "ret_gn_gain": dggn,
        "ssm_lambda_re": dlre[None], "ssm_lambda_im": dlim[None], "ssm_log_dt": dldt[None],
        "ssm_b_re": dbre[None], "ssm_b_im": dbim[None], "ssm_c_re": dcre[None], "ssm_c_im": dcim[None],
        "ssm_d": dd, "norm_mlp_pre": dg3, "norm_mlp_post": dg4,
    }, sq)
    for z in (zero, zero2):
        g1 = g1 if z is None else g1 + z
    gx, dg1 = _inproj_bwd(pieces, w_in_t, x, dx2, g1, min(2 * tm, L))
    return gx, dg1


BIG_SHAPES = {"w_in": (D_MODEL, IN_COLS // N_DEV), "w_glu": (SSM_W, 2 * SSM_W // N_DEV), "w_out": (D_MODEL // N_DEV, D_MODEL),
              "w_ff1": (D_MODEL, FF1_COLS), "w_ff2": (D_FF // N_DEV, D_MODEL)}
BIG_NAMES = ("w_in", "w_glu", "w_out", "w_ff1", "w_ff2")


def _cols_from_slots(g):
    return jnp.transpose(g, (1, 0, 2)).reshape(g.shape[1], N_DEV * g.shape[2])


def _cols_to_slots(dw):
    r, cols = dw.shape
    return jnp.transpose(dw.reshape(r, N_DEV, cols // N_DEV), (1, 0, 2))


WEIGHT_GROUPS = {"in": ("w_in",), "mix": ("w_glu", "w_out"), "mlp": ("w_ff1", "w_ff2")}


def _weight_from_slots(name, g):
    if name == "w_glu":
        return _cols_from_slots(g)
    if name == "w_ff1":
        return g
    return g.reshape(N_DEV * g.shape[1], g.shape[2])


def _grad_slots(name, dw):
    if name == "w_glu":
        return _cols_to_slots(dw)
    if name == "w_ff1":
        return dw
    if name == "w_in":
        return dw.reshape(N_DEV, BIG_SHAPES[name][1], BIG_SHAPES[name][0])
    return dw.reshape((N_DEV,) + BIG_SHAPES[name])


PIECE_ROWS = 8


def _small_layout(shapes):
    off, rows = {}, 0
    for n in SMALL_NAMES:
        off[n] = rows
        rows += -(-math.prod(shapes[n]) // (PIECE_ROWS * LANES)) * PIECE_ROWS
    return off, rows, rows + PIECE_ROWS


def _pack_small(vals, shapes, last=None):
    parts = []
    for n in SMALL_NAMES:
        flat = vals[n].reshape(-1).astype(_F32)
        pad = -flat.shape[0] % (PIECE_ROWS * LANES)
        if pad:
            flat = jnp.concatenate([flat, jnp.zeros((pad,), _F32)])
        parts.append(flat.reshape(-1, LANES))
    parts.append(jnp.zeros((PIECE_ROWS, LANES), _F32) if last is None else last)
    return jnp.concatenate(parts, axis=0)


def _unpack_small(buf, shapes):
    off, _, _ = _small_layout(shapes)
    out = {}
    for n in SMALL_NAMES:
        size = math.prod(shapes[n])
        rows = -(-size // LANES)
        out[n] = buf[off[n]:off[n] + rows].reshape(-1)[:size].reshape(shapes[n])
    return out


WEIGHT_NAMES = ('norm_mix_pre', 'norm_mix_post', 'w_in', 'ret_gn_gain', 'ssm_lambda_re', 'ssm_lambda_im', 'ssm_log_dt',
                'ssm_b_re', 'ssm_b_im', 'ssm_c_re', 'ssm_c_im', 'ssm_d', 'w_glu', 'w_out', 'norm_mlp_pre',
                'norm_mlp_post', 'w_ff1', 'w_ff2')


def kernel(x, norm_mix_pre, norm_mix_post, w_in, ret_gn_gain, ssm_lambda_re, ssm_lambda_im, ssm_log_dt, ssm_b_re, ssm_b_im, ssm_c_re, ssm_c_im, ssm_d, w_glu, w_out, norm_mlp_pre, norm_mlp_post, w_ff1, w_ff2, loss_target, m_norm_mix_pre, m_norm_mix_post, m_w_in, m_ret_gn_gain, m_ssm_lambda_re, m_ssm_lambda_im, m_ssm_log_dt, m_ssm_b_re, m_ssm_b_im, m_ssm_c_re, m_ssm_c_im, m_ssm_d, m_w_glu, m_w_out, m_norm_mlp_pre, m_norm_mlp_post, m_w_ff1, m_w_ff2, v_norm_mix_pre, v_norm_mix_post, v_w_in, v_ret_gn_gain, v_ssm_lambda_re, v_ssm_lambda_im, v_ssm_log_dt, v_ssm_b_re, v_ssm_b_im, v_ssm_c_re, v_ssm_c_im, v_ssm_d, v_w_glu, v_w_out, v_norm_mlp_pre, v_norm_mlp_post, v_w_ff1, v_w_ff2):
    args = dict(locals())
    w = {n: args[n] for n in WEIGHT_NAMES}
    m = {n: args["m_" + n] for n in WEIGHT_NAMES}
    v = {n: args["v_" + n] for n in WEIGHT_NAMES}
    L = x.shape[1]
    tm = min(256, L)
    tk = min(2048, L)
    tb = min(512, L)

    gathers, zero = {}, jnp.zeros((), _F32)
    for group, names in WEIGHT_GROUPS.items():
        blocks = [(w[n][0].T if n == "w_in" else w[n][0]).astype(_BF) for n in names]
        blocks[0] = blocks[0] + zero.astype(_BF)
        gathers[group] = _split_start(blocks, [_landing(b) for b in blocks], True, "weights_start_" + group)
        zero = gathers[group][4][0, 0]

    def weights(group, after):
        landed = _split_wait(*gathers[group][:4], after, True, "weights_wait_" + group)
        return [_weight_from_slots(n, g) for n, g in zip(WEIGHT_GROUPS[group], landed)]

    in_flight = []

    def emit(dws):
        names = sorted(dws)
        srcs = [_grad_slots(n, dws[n]) for n in names]
        lands = [_landing(lax.dynamic_index_in_dim(t, _my_index(), 0, keepdims=False)) for t in srcs]
        started = _split_start(srcs, lands, False, "grads_start_" + "_".join(names))
        in_flight.append((names, started))
        return started[4][0, 0]

    shapes = {n: w[n].shape for n in SMALL_NAMES}
    first_piece = {SMALL_NAMES[0]: jnp.zeros(shapes[SMALL_NAMES[0]], _F32)}
    small_flight = []

    def emit_small(gs, sq):
        loss_rows = jnp.broadcast_to(0.5 / D_MODEL * jnp.sum(sq), (PIECE_ROWS, LANES)).astype(_F32)
        buf = _pack_small({**first_piece, **gs}, shapes, loss_rows)
        small_flight.append(_split_start([buf], [_landing(buf)], True, "small_grads_start"))
        return small_flight[0][4][0, 0]

    small_w = {n: w[n] for n in SMALL_NAMES}
    gx, dg1 = _local_grads(x[0], loss_target[0], small_w, weights, emit, emit_small, tm, tk, tb, zero=zero)
    last_buf = dg1.reshape(PIECE_ROWS, LANES)
    last_started = _split_start([last_buf], [_landing(last_buf)], True, "last_grad_start")

    grads, delta, new_m, new_v = {}, {}, {}, {}
    after = last_started[4]
    for names, started in in_flight:
        landed = _split_wait(*started[:4], after, False, "grads_wait_" + "_".join(names))
        for n, parts in zip(names, landed):
            flip = (lambda t: t.T) if n == "w_in" else (lambda t: t)
            res = _sum_adamw(parts, flip(w[n][0]), flip(m[n][0]), flip(v[n][0]), math.gcd(256, parts.shape[1]), "adamw_" + n)
            grads[n], delta[n], new_m[n], new_v[n] = (flip(t)[None] for t in res)
        after = res[1]
    small_parts = _split_wait(*small_flight[0][:4], after, True, "small_grads_wait")[0]
    last_parts = _split_wait(*last_started[:4], small_parts, True, "last_grad_wait")[0]
    small_parts = lax.dynamic_update_slice(small_parts, last_parts, (0, 0, 0))
    sw, sm, sv = _pack_small(w, shapes), _pack_small(m, shapes), _pack_small(v, shapes)
    res = _sum_adamw(small_parts, sw, sm, sv, sw.shape[0], "adamw_small")
    for dst, buf in zip((grads, delta, new_m, new_v), res):
        dst.update(_unpack_small(buf, shapes))
    _, loss_at, _ = _small_layout(shapes)
    loss = res[0][loss_at, 0]

    return (loss, gx[None], *[grads[n] for n in WEIGHT_NAMES], *[delta[n] for n in WEIGHT_NAMES],
            *[new_m[n] for n in WEIGHT_NAMES], *[new_v[n] for n in WEIGHT_NAMES])
```

```python
import math

import jax
import jax.numpy as jnp
from jax import lax
from jax.experimental import pallas as pl
from jax.experimental.pallas import tpu as pltpu

_BF = jnp.bfloat16
_F32 = jnp.float32

D_MODEL = 1024
RET_W = 512
N_HEAD = 4
HEAD_D = 128
CHUNK = 256
ROPE_CHUNK = 128
SSM_W = 512
SSM_GC = 16
N_GROUP = 32
N_STATE = 64
GROUPS_PER_KB = 8
N_KB = 4
KB_STATES = GROUPS_PER_KB * N_STATE
D_FF = 4096
IN_COLS = 2560
NORM_EPS = 1e-6
ROPE_BASE = 10000.0
N_DEV = 8

ADAM_LR = 0.001
ADAM_B1 = 0.9
ADAM_B2 = 0.999
ADAM_EPS = 1e-08
ADAM_WD = 0.01
ADAM_STEP = 10

SUBLANES = 8
LANES = 128
VMEM_LIMIT = 52 * 1024 * 1024
RET_STEP_CHUNKS = 2
KB_PER_STEP = 2
SCAN_UNROLL = True

MESH = pl.DeviceIdType.MESH


def _params(*sem):
    return pltpu.CompilerParams(dimension_semantics=sem, vmem_limit_bytes=VMEM_LIMIT)


def _dot(a, b):
    return jnp.dot(a, b, preferred_element_type=_F32)


def _dot_nt(a, b):
    return lax.dot_general(a, b, (((1,), (1,)), ((), ())), preferred_element_type=_F32)


def _dot_tn(a, b):
    return lax.dot_general(a, b, (((0,), (0,)), ((), ())), preferred_element_type=_F32)


def _rms_r(z):
    return lax.rsqrt(jnp.mean(z * z, axis=-1, keepdims=True) + NORM_EPS)


def _rms_bwd(z, g, dn):
    r = _rms_r(z)
    t = dn * g
    dz = r * t - z * (r * r * r * jnp.mean(t * z, axis=-1, keepdims=True))
    return dz, dn * z * r


def _rope(t, cs, sn):
    return t * cs + pltpu.roll(t, HEAD_D // 2, 1) * sn


def _rope_t(t, cs, sn):
    return t * cs - pltpu.roll(t, HEAD_D // 2, 1) * sn


def _sigmoid(z):
    return 1.0 / (1.0 + jnp.exp(-z))


_GELU_C = math.sqrt(2.0 / math.pi)


def _gelu(z):
    return 0.5 * z * (1.0 + jnp.tanh(_GELU_C * (z + 0.044715 * z * z * z)))


def _gelu_grad(z):
    th = jnp.tanh(_GELU_C * (z + 0.044715 * z * z * z))
    return 0.5 * (1.0 + th) + 0.5 * z * (1.0 - th * th) * _GELU_C * (1.0 + 3 * 0.044715 * z * z)


ROW_CHUNK = 256


def _row_chunks(tm):
    return [pl.ds(i, min(ROW_CHUNK, tm)) for i in range(0, tm, ROW_CHUNK)]


def _row_spec(tm, n):
    return pl.BlockSpec((tm, n), lambda i: (i, 0))


def _full_spec(shape):
    nd = len(shape)
    return pl.BlockSpec(shape, lambda *_: (0,) * nd)


def _weight_spec(shape):
    nd = len(shape)
    return pl.BlockSpec(shape, lambda *_: (0,) * nd, pipeline_mode=pl.Buffered(1))


def _rope_tables(L):
    half = HEAD_D // 2
    inv_freq = ROPE_BASE ** (-jnp.arange(half, dtype=_F32) / half)
    twice = lambda t: jnp.concatenate([t, t], axis=-1)
    off = jnp.arange(ROPE_CHUNK, dtype=_F32)[:, None] * inv_freq[None, :]
    start = (ROPE_CHUNK * jnp.arange(L // ROPE_CHUNK, dtype=_F32))[:, None] * inv_freq[None, :]
    return (twice(jnp.cos(off)), twice(jnp.sin(off)),
            twice(jnp.cos(start))[:, None, :], twice(jnp.sin(start))[:, None, :])


def _inproj_fwd(x, g1, w_in_t, rope, tm):
    L = x.shape[0]
    n_chunks = tm // ROPE_CHUNK

    def body(x_ref, g_ref, w_ref, co_ref, so_ref, cs_ref, ss_ref, h_ref, q_ref, k_ref, v_ref, gate_ref, u_ref,
             cos_ref, sin_ref):
        xv = x_ref[...]
        h = (xv * _rms_r(xv) * g_ref[...]).astype(_BF)
        h_ref[...] = h
        proj = _dot_nt(h, w_ref[...])
        lane = lax.broadcasted_iota(jnp.int32, (ROPE_CHUNK, HEAD_D), 1)
        sign = jnp.where(lane < HEAD_D // 2, -1.0, 1.0)
        co, so = co_ref[...], so_ref[...]
        for c in range(n_chunks):
            chunk = pl.program_id(0) * n_chunks + c
            cst, sst = cs_ref[chunk], ss_ref[chunk]
            rows = slice(c * ROPE_CHUNK, (c + 1) * ROPE_CHUNK)
            cs = co * cst - so * sst
            sn = (so * cst + co * sst) * sign
            cos_ref[rows, :] = cs
            sin_ref[rows, :] = sn
            for hh in range(N_HEAD):
                lo = hh * HEAD_D
                q_ref[rows, lo:lo + HEAD_D] = _rope(proj[rows, lo:lo + HEAD_D], cs, sn).astype(_BF)
                kh = _rope(proj[rows, RET_W + lo:RET_W + lo + HEAD_D], cs, sn) * (HEAD_D ** -0.5)
                k_ref[rows, lo:lo + HEAD_D] = kh.astype(_BF)
        v_ref[...] = proj[:, 2 * RET_W:3 * RET_W].astype(_BF)
        gate_ref[...] = proj[:, 3 * RET_W:4 * RET_W]
        u_ref[...] = proj[:, 4 * RET_W:]

    nc = L // ROPE_CHUNK
    return pl.pallas_call(
        body, name="inproj_fwd", grid=(L // tm,),
        in_specs=[_row_spec(tm, D_MODEL), _full_spec((1, D_MODEL)), _weight_spec((IN_COLS, D_MODEL)),
                  _full_spec((ROPE_CHUNK, HEAD_D)), _full_spec((ROPE_CHUNK, HEAD_D)),
                  _full_spec((nc, 1, HEAD_D)), _full_spec((nc, 1, HEAD_D))],
        out_specs=[_row_spec(tm, D_MODEL)] + [_row_spec(tm, RET_W)] * 5 + [_row_spec(tm, HEAD_D)] * 2,
        out_shape=[jax.ShapeDtypeStruct((L, D_MODEL), _BF)] + [jax.ShapeDtypeStruct((L, RET_W), _BF)] * 3
        + [jax.ShapeDtypeStruct((L, RET_W), _F32)] * 2 + [jax.ShapeDtypeStruct((L, HEAD_D), _F32)] * 2,
        compiler_params=_params("parallel"),
    )(x, g1, w_in_t, *rope)


def _ret_consts():
    lg = jnp.log(1.0 - jnp.exp(jnp.linspace(math.log(1.0 / 32), math.log(1.0 / 512), N_HEAD))).astype(_F32)
    idx = jnp.arange(CHUNK, dtype=_F32)
    diff = idx[:, None] - idx[None, :]
    decay = jnp.where(diff[None] >= 0, jnp.exp(jnp.maximum(diff, 0.0)[None] * lg[:, None, None]), 0.0)
    zeta = jnp.exp((CHUNK - 1 - idx)[None, :] * lg[:, None])
    xi = jnp.exp((idx + 1.0)[None, :] * lg[:, None])
    gc = jnp.exp(CHUNK * lg)
    wide = lambda t: jnp.broadcast_to(t[:, :, None], (N_HEAD, CHUNK, HEAD_D)).astype(_F32)
    gcw = jnp.broadcast_to(gc[:, None, None], (N_HEAD, SUBLANES, HEAD_D)).astype(_F32)
    return decay.astype(_F32), wide(xi), wide(zeta), gcw


def _head_specs():
    wide = _full_spec((N_HEAD, CHUNK, HEAD_D))
    return [_full_spec((N_HEAD, CHUNK, CHUNK)), wide, wide, _full_spec((N_HEAD, SUBLANES, HEAD_D))]


def _retention_fwd(q, k, v, gate, ggn, consts):
    L = q.shape[0]
    nc = L // CHUNK
    cps = math.gcd(RET_STEP_CHUNKS, nc)
    blk = pl.BlockSpec((cps * CHUNK, RET_W), lambda n: (n, 0))

    def body(q_ref, k_ref, v_ref, gate_ref, ggn_ref, dm_ref, xi_ref, zeta_ref, gc_ref,
             o_ref, y_ref, rp_ref, r_scr):
        @pl.when(pl.program_id(0) == 0)
        def _():
            r_scr[...] = jnp.zeros_like(r_scr)

        for hh in range(N_HEAD):
            cols = slice(hh * HEAD_D, (hh + 1) * HEAD_D)
            state = r_scr[hh]
            for c in range(cps):
                rows = slice(c * CHUNK, (c + 1) * CHUNK)
                qv, kv, vv = q_ref[rows, cols], k_ref[rows, cols], v_ref[rows, cols]
                s = _dot_nt(qv, kv) * dm_ref[hh]
                o = _dot(s.astype(_BF), vv) + _dot(qv, state.astype(_BF)) * xi_ref[hh]
                o_ref[rows, cols] = o
                rp_ref[hh, c] = state
                vz = (vv.astype(_F32) * zeta_ref[hh]).astype(_BF)
                state = gc_ref[hh, 0:1, :] * state + _dot_tn(kv, vz)
                dlt = o - jnp.mean(o, axis=-1, keepdims=True)
                on = dlt * lax.rsqrt(jnp.mean(dlt * dlt, axis=-1, keepdims=True) + NORM_EPS)
                gt = gate_ref[rows, cols]
                y_ref[rows, cols] = (gt * _sigmoid(gt) * (on * ggn_ref[:, cols])).astype(_BF)
            r_scr[hh] = state

    return pl.pallas_call(
        body, name="retention_fwd", grid=(nc // cps,),
        in_specs=[blk, blk, blk, blk, _full_spec((1, RET_W))] + _head_specs(),
        out_specs=[blk, blk, pl.BlockSpec((N_HEAD, cps, HEAD_D, HEAD_D), lambda n: (0, n, 0, 0))],
        out_shape=[jax.ShapeDtypeStruct((L, RET_W), _F32), jax.ShapeDtypeStruct((L, RET_W), _BF),
                   jax.ShapeDtypeStruct((N_HEAD, nc, HEAD_D, HEAD_D), _F32)],
        scratch_shapes=[pltpu.VMEM((N_HEAD, HEAD_D, HEAD_D), _F32)],
        compiler_params=_params("arbitrary"),
    )(q, k, v, gate, ggn, *consts)


def _rows_to_segments(dst_scr, src_ref, seg):
    for g in range(dst_scr.shape[0]):
        for j in range(SUBLANES):
            dst_scr[g, pl.ds(j, seg, stride=SUBLANES), :] = src_ref[pl.ds(j * seg, seg), g * LANES:(g + 1) * LANES]


def _segments_to_rows(dst_ref, src_scr, seg):
    for g in range(src_scr.shape[0]):
        for j in range(SUBLANES):
            dst_ref[pl.ds(j * seg, seg), g * LANES:(g + 1) * LANES] = src_scr[g, pl.ds(j, seg, stride=SUBLANES), :]


def _scan_segments(x_ref, tab_ref, pw_ref, carry_ref, seg, reverse, entry_ref=None, fwd_ref=None, fwd_entry_ref=None,
                   da_ref=None):
    G = x_ref.shape[0]
    W = KB_STATES
    re, im = pl.ds(0, W), pl.ds(W, W)
    row_id = lax.broadcasted_iota(jnp.int32, (SUBLANES, W), 0)
    edge_in = (row_id == SUBLANES - 1) if reverse else (row_id == 0)
    edge_out = 0 if reverse else SUBLANES - 1
    a_tab = [(tab_ref[g, 0], tab_ref[g, 1]) for g in range(G)]

    def local(i, st):
        r = (seg - 1 - i) if reverse else i
        out = []
        for g in range(G):
            (ar, ai), (sr, si) = a_tab[g], st[g]
            nr = ar * sr - ai * si + x_ref[g, r, :, re]
            ni = ar * si + ai * sr + x_ref[g, r, :, im]
            x_ref[g, r, :, re] = nr
            x_ref[g, r, :, im] = ni
            out.append((nr, ni))
        return tuple(out)

    zero = jnp.zeros((SUBLANES, W), _F32)
    ends = lax.fori_loop(0, seg, local, tuple((zero, zero) for _ in range(G)), unroll=SCAN_UNROLL)

    entry = []
    shift = (SUBLANES - 1) if reverse else 1
    for g in range(G):
        er, ei = ends[g]
        fr = jnp.where(edge_in, carry_ref[g, :, re], pltpu.roll(er, shift, 0))
        fi = jnp.where(edge_in, carry_ref[g, :, im], pltpu.roll(ei, shift, 0))
        for j, dist in enumerate((1, 2, 4)):
            pr, pi = tab_ref[g, 2 + 2 * j], tab_ref[g, 3 + 2 * j]
            sh = (SUBLANES - dist) if reverse else dist
            sr, si = pltpu.roll(fr, sh, 0), pltpu.roll(fi, sh, 0)
            fr, fi = fr + pr * sr - pi * si, fi + pr * si + pi * sr
        br, bi = tab_ref[g, 8], tab_ref[g, 9]
        outr = br * fr - bi * fi + er
        outi = br * fi + bi * fr + ei
        carry_ref[g, :, re] = jnp.broadcast_to(outr[edge_out:edge_out + 1, :], (SUBLANES, W))
        carry_ref[g, :, im] = jnp.broadcast_to(outi[edge_out:edge_out + 1, :], (SUBLANES, W))
        entry.append((fr, fi))
        if entry_ref is not None:
            entry_ref[g, :, re] = fr
            entry_ref[g, :, im] = fi

    add_da = da_ref is not None

    def fix(r, st, first=False):
        out = []
        for g in range(G):
            fr, fi = entry[g]
            pwr, pwi = pw_ref[g, r, :, re], pw_ref[g, r, :, im]
            xr = x_ref[g, r, :, re] + (pwr * fr - pwi * fi)
            xi = x_ref[g, r, :, im] + (pwr * fi + pwi * fr)
            x_ref[g, r, :, re] = xr
            x_ref[g, r, :, im] = xi
            if add_da:
                prev = fwd_entry_ref.at[g] if first else fwd_ref.at[g, r - 1]
                xpr, xpi = prev[:, re], prev[:, im]
                out.append((st[g][0] + (xr * xpr + xi * xpi), st[g][1] + (xi * xpr - xr * xpi)))
            else:
                out.append(st[g])
        return tuple(out)

    if add_da:
        st = fix(0, tuple((zero, zero) for _ in range(G)), first=True)
        st = lax.fori_loop(1, seg, fix, st, unroll=SCAN_UNROLL)
        for g in range(G):
            da_ref[g, :, re] += st[g][0]
            da_ref[g, :, im] += st[g][1]
    else:
        lax.fori_loop(0, seg, fix, tuple((zero[0:1, 0:LANES],) for _ in range(G)), unroll=SCAN_UNROLL)


def _s5_specs(seg, time=lambda t: t):
    G = KB_PER_STEP
    return dict(
        x=pl.BlockSpec((G, seg, SUBLANES, 2 * KB_STATES), lambda kb, t: (kb, time(t), 0, 0)),
        ent=pl.BlockSpec((G, 1, SUBLANES, 2 * KB_STATES), lambda kb, t: (kb, time(t), 0, 0)),
        b=pl.BlockSpec((G, LANES, 2 * KB_STATES), lambda kb, t: (kb, 0, 0)),
        c=pl.BlockSpec((G, 2 * KB_STATES, LANES), lambda kb, t: (kb, 0, 0)),
        tab=pl.BlockSpec((G, 10, SUBLANES, KB_STATES), lambda kb, t: (kb, 0, 0, 0)),
        pw=pl.BlockSpec((G, seg, 1, 2 * KB_STATES), lambda kb, t: (kb, 0, 0, 0)),
        d=pl.BlockSpec((1, G * LANES), lambda kb, t: (0, kb)),
    )


def _s5_fwd(u, bmat, cmat, tab_f, pw_f, d_skip, tb):
    L = u.shape[0]
    nt = L // tb
    seg = tb // SUBLANES
    G = KB_PER_STEP
    ucol = pl.BlockSpec((tb, G * LANES), lambda kb, t: (t, kb))
    sp = _s5_specs(seg)

    def body(u_ref, b_ref, c_ref, tab_ref, pw_ref, d_ref, s_ref, x_ref, ent_ref, up_scr, y_scr, carry_scr):
        @pl.when(pl.program_id(1) == 0)
        def _():
            carry_scr[...] = jnp.zeros_like(carry_scr)

        _rows_to_segments(up_scr, u_ref, seg)
        for g in range(G):
            x_ref[g] = _dot(up_scr[g].astype(_BF), b_ref[g]).reshape(seg, SUBLANES, 2 * KB_STATES)
        _scan_segments(x_ref, tab_ref, pw_ref, carry_scr, seg, reverse=False, entry_ref=ent_ref.at[:, 0])
        for g in range(G):
            y = _dot(x_ref[g].reshape(tb, 2 * KB_STATES).astype(_BF), c_ref[g])
            y_scr[g] = y + d_ref[:, g * LANES:(g + 1) * LANES] * up_scr[g]
        _segments_to_rows(s_ref, y_scr, seg)

    return pl.pallas_call(
        body, name="s5_fwd", grid=(N_KB // G, nt),
        in_specs=[ucol, sp["b"], sp["c"], sp["tab"], sp["pw"], sp["d"]],
        out_specs=[ucol, sp["x"], sp["ent"]],
        out_shape=[jax.ShapeDtypeStruct((L, SSM_W), _F32),
                   jax.ShapeDtypeStruct((N_KB, L // SUBLANES, SUBLANES, 2 * KB_STATES), _F32),
                   jax.ShapeDtypeStruct((N_KB, nt, SUBLANES, 2 * KB_STATES), _F32)],
        scratch_shapes=[pltpu.VMEM((G, tb, LANES), _F32)] * 2 + [pltpu.VMEM((G, SUBLANES, 2 * KB_STATES), _F32)],
        compiler_params=_params("parallel", "arbitrary"),
    )(u, bmat, cmat, tab_f, pw_f, d_skip)


def _mixout_fwd(s, y_ret, x, w_glu, w_out, g2, tm):
    L = s.shape[0]

    def body(s_ref, yr_ref, x_ref, wg_ref, wo_ref, g_ref, ys_ref, glu_ref, cat_ref, mix_ref, x2_ref):
        for rows in _row_chunks(tm):
            ys = _gelu(s_ref[rows, :]).astype(_BF)
            ys_ref[rows, :] = ys
            glu = _dot(ys, wg_ref[...])
            glu_ref[rows, :] = glu
            cat_ref[rows, :RET_W] = yr_ref[rows, :]
            cat_ref[rows, RET_W:] = (glu[:, :SSM_W] * _sigmoid(glu[:, SSM_W:])).astype(_BF)
            mix = _dot(cat_ref[rows, :], wo_ref[...])
            mix_ref[rows, :] = mix
            x2_ref[rows, :] = x_ref[rows, :] + mix * _rms_r(mix) * g_ref[...]

    return pl.pallas_call(
        body, name="mixout_fwd", grid=(L // tm,),
        in_specs=[_row_spec(tm, SSM_W), _row_spec(tm, RET_W), _row_spec(tm, D_MODEL),
                  _weight_spec((SSM_W, 2 * SSM_W)), _weight_spec((D_MODEL, D_MODEL)), _full_spec((1, D_MODEL))],
        out_specs=[_row_spec(tm, SSM_W), _row_spec(tm, 2 * SSM_W), _row_spec(tm, D_MODEL),
                   _row_spec(tm, D_MODEL), _row_spec(tm, D_MODEL)],
        out_shape=[jax.ShapeDtypeStruct((L, SSM_W), _BF), jax.ShapeDtypeStruct((L, 2 * SSM_W), _F32),
                   jax.ShapeDtypeStruct((L, D_MODEL), _BF), jax.ShapeDtypeStruct((L, D_MODEL), _F32),
                   jax.ShapeDtypeStruct((L, D_MODEL), _F32)],
        compiler_params=_params("parallel"),
    )(s, y_ret, x, w_glu, w_out, g2)


FF1_COLS = D_FF // N_DEV


def _ff1_fwd(x2, g3, w1, tm):
    L = x2.shape[0]

    def body(x_ref, g_ref, w_ref, h_ref, f_ref):
        xv = x_ref[...]
        h = (xv * _rms_r(xv) * g_ref[...]).astype(_BF)
        h_ref[...] = h
        for j in range(N_DEV):
            f_ref[:, j * FF1_COLS:(j + 1) * FF1_COLS] = _dot(h, w_ref[j])

    return pl.pallas_call(
        body, name="ff1_fwd", grid=(L // tm,),
        in_specs=[_row_spec(tm, D_MODEL), _full_spec((1, D_MODEL)), _weight_spec((N_DEV, D_MODEL, FF1_COLS))],
        out_specs=[_row_spec(tm, D_MODEL), _row_spec(tm, D_FF)],
        out_shape=[jax.ShapeDtypeStruct((L, D_MODEL), _BF), jax.ShapeDtypeStruct((L, D_FF), _F32)],
        compiler_params=_params("parallel"),
    )(x2, g3, w1)


def _ff2_loss(f1, x2, tgt, g4, w2, tm):
    L = f1.shape[0]

    def body(f_ref, x_ref, t_ref, g_ref, w_ref, dy_ref, dm_ref, dg_ref, ls_ref):
        @pl.when(pl.program_id(0) == 0)
        def _():
            dg_ref[...] = jnp.zeros_like(dg_ref)
            ls_ref[...] = jnp.zeros_like(ls_ref)

        g = g_ref[...]
        for rows in _row_chunks(tm):
            rl = jnp.maximum(f_ref[rows, :], 0.0)
            m = _dot((rl * rl).astype(_BF), w_ref[...])
            y = x_ref[rows, :] + m * _rms_r(m) * g
            err = y - t_ref[rows, :]
            ls_ref[...] += jnp.sum(err * err, axis=0, keepdims=True)
            dy = err * (1.0 / D_MODEL)
            dy_ref[rows, :] = dy
            dm, dgr = _rms_bwd(m, g, dy)
            dm_ref[rows, :] = dm.astype(_BF)
            dg_ref[...] += jnp.sum(dgr, axis=0, keepdims=True)

    return pl.pallas_call(
        body, name="ff2_loss", grid=(L // tm,),
        in_specs=[_row_spec(tm, D_FF), _row_spec(tm, D_MODEL), _row_spec(tm, D_MODEL),
                  _full_spec((1, D_MODEL)), _weight_spec((D_FF, D_MODEL))],
        out_specs=[_row_spec(tm, D_MODEL), _row_spec(tm, D_MODEL), _full_spec((1, D_MODEL)), _full_spec((1, D_MODEL))],
        out_shape=[jax.ShapeDtypeStruct((L, D_MODEL), _F32), jax.ShapeDtypeStruct((L, D_MODEL), _BF),
                   jax.ShapeDtypeStruct((1, D_MODEL), _F32), jax.ShapeDtypeStruct((1, D_MODEL), _F32)],
        compiler_params=_params("arbitrary"),
    )(f1, x2, tgt, g4, w2)


def _ff2_bwd(dm, f1, w2, tm, tn):
    L = dm.shape[0]
    last = L // tm - 1

    def body(dm_ref, f_ref, w_ref, df_ref, dw_ref, acc):
        @pl.when(pl.program_id(1) == 0)
        def _():
            acc[...] = jnp.zeros_like(acc)

        dmv = dm_ref[...]
        rl = jnp.maximum(f_ref[...], 0.0)
        df_ref[...] = (_dot_nt(dmv, w_ref[...]) * (2.0 * rl)).astype(_BF)
        acc[...] += _dot_tn((rl * rl).astype(_BF), dmv)

        @pl.when(pl.program_id(1) == last)
        def _():
            dw_ref[...] = acc[...].astype(_BF)

    return pl.pallas_call(
        body, name="ff2_bwd", grid=(D_FF // tn, L // tm),
        in_specs=[pl.BlockSpec((tm, D_MODEL), lambda j, i: (i, 0)), pl.BlockSpec((tm, tn), lambda j, i: (i, j)),
                  pl.BlockSpec((tn, D_MODEL), lambda j, i: (j, 0))],
        out_specs=[pl.BlockSpec((tm, tn), lambda j, i: (i, j)), pl.BlockSpec((tn, D_MODEL), lambda j, i: (j, 0))],
        out_shape=[jax.ShapeDtypeStruct((L, D_FF), _BF), jax.ShapeDtypeStruct((D_FF, D_MODEL), _BF)],
        scratch_shapes=[pltpu.VMEM((tn, D_MODEL), _F32)],
        compiler_params=_params("parallel", "arbitrary"),
    )(dm, f1, w2)


def _ff1_bwd(df1, w1, x2, mix, dy, g3, g2, tm):
    L = df1.shape[0]

    def body(df_ref, w_ref, x2_ref, mix_ref, dy_ref, g3_ref, g2_ref, dx2_ref, dmix_ref, dg3_ref, dg2_ref):
        @pl.when(pl.program_id(0) == 0)
        def _():
            dg3_ref[...] = jnp.zeros_like(dg3_ref)
            dg2_ref[...] = jnp.zeros_like(dg2_ref)

        for rows in _row_chunks(tm):
            dh = _dot_nt(df_ref[rows, 0:FF1_COLS], w_ref[0])
            for j in range(1, N_DEV):
                dh = dh + _dot_nt(df_ref[rows, j * FF1_COLS:(j + 1) * FF1_COLS], w_ref[j])
            dz, dgr = _rms_bwd(x2_ref[rows, :], g3_ref[...], dh)
            dg3_ref[...] += jnp.sum(dgr, axis=0, keepdims=True)
            dx2 = dy_ref[rows, :] + dz
            dx2_ref[rows, :] = dx2
            dmx, dgr2 = _rms_bwd(mix_ref[rows, :], g2_ref[...], dx2)
            dg2_ref[...] += jnp.sum(dgr2, axis=0, keepdims=True)
            dmix_ref[rows, :] = dmx.astype(_BF)

    vec = _full_spec((1, D_MODEL))
    return pl.pallas_call(
        body, name="ff1_bwd", grid=(L // tm,),
        in_specs=[_row_spec(tm, D_FF), _weight_spec((N_DEV, D_MODEL, FF1_COLS)), _row_spec(tm, D_MODEL),
                  _row_spec(tm, D_MODEL), _row_spec(tm, D_MODEL), vec, vec],
        out_specs=[_row_spec(tm, D_MODEL), _row_spec(tm, D_MODEL), vec, vec],
        out_shape=[jax.ShapeDtypeStruct((L, D_MODEL), _F32), jax.ShapeDtypeStruct((L, D_MODEL), _BF),
                   jax.ShapeDtypeStruct((1, D_MODEL), _F32), jax.ShapeDtypeStruct((1, D_MODEL), _F32)],
        compiler_params=_params("arbitrary"),
    )(df1, w1, x2, mix, dy, g3, g2)


def _matmul_tn(a, b, tm, tn, name, slots=False):
    L, K = a.shape
    N = b.shape[1]
    last = L // tm - 1

    def body(a_ref, b_ref, o_ref, acc):
        @pl.when(pl.program_id(1) == 0)
        def _():
            acc[...] = jnp.zeros_like(acc)

        acc[...] += _dot_tn(a_ref[...].astype(_BF), b_ref[...].astype(_BF))

        @pl.when(pl.program_id(1) == last)
        def _():
            if slots:
                o_ref[0] = acc[...].astype(_BF)
            else:
                o_ref[...] = acc[...].astype(_BF)

    if slots:
        out_spec = pl.BlockSpec((1, K, tn), lambda j, i: (j, 0, 0))
        out_shape = jax.ShapeDtypeStruct((N // tn, K, tn), _BF)
    else:
        out_spec = pl.BlockSpec((K, tn), lambda j, i: (0, j))
        out_shape = jax.ShapeDtypeStruct((K, N), _BF)
    return pl.pallas_call(
        body, name=name, grid=(N // tn, L // tm),
        in_specs=[pl.BlockSpec((tm, K), lambda j, i: (i, 0)), pl.BlockSpec((tm, tn), lambda j, i: (i, j))],
        out_specs=out_spec, out_shape=out_shape,
        scratch_shapes=[pltpu.VMEM((K, tn), _F32)],
        compiler_params=_params("parallel", "arbitrary"),
    )(a, b)


def _mixout_bwd(dmix, w_out, w_glu, glu, s, o, gate, ggn, tm):
    L = dmix.shape[0]

    def body(dmix_ref, wo_ref, wg_ref, glu_ref, s_ref, o_ref, gate_ref, ggn_ref,
             dglu_ref, ds_ref, dgate_ref, do_ref, dggn_ref):
        @pl.when(pl.program_id(0) == 0)
        def _():
            dggn_ref[...] = jnp.zeros_like(dggn_ref)

        dcat = _dot_nt(dmix_ref[...], wo_ref[...])
        dy_ret, dy_ssm = dcat[:, :RET_W], dcat[:, RET_W:]
        glu = glu_ref[...]
        ga, sg = glu[:, :SSM_W], _sigmoid(glu[:, SSM_W:])
        dga = (dy_ssm * sg).astype(_BF)
        dgb = (dy_ssm * ga * sg * (1.0 - sg)).astype(_BF)
        dglu_ref[:, :SSM_W] = dga
        dglu_ref[:, SSM_W:] = dgb
        dys = _dot_nt(dga, wg_ref[:, :SSM_W]) + _dot_nt(dgb, wg_ref[:, SSM_W:])
        ds_ref[...] = dys * _gelu_grad(s_ref[...])
        gt = gate_ref[...]
        sgt = _sigmoid(gt)
        ggn = ggn_ref[...]
        for hh in range(N_HEAD):
            cols = slice(hh * HEAD_D, (hh + 1) * HEAD_D)
            ov = o_ref[:, cols]
            dlt = ov - jnp.mean(ov, axis=-1, keepdims=True)
            rstd = lax.rsqrt(jnp.mean(dlt * dlt, axis=-1, keepdims=True) + NORM_EPS)
            on = dlt * rstd
            dyr = dy_ret[:, cols] * (gt[:, cols] * sgt[:, cols])
            dgate_ref[:, cols] = dy_ret[:, cols] * (on * ggn[:, cols]) * (sgt[:, cols] * (1.0 + gt[:, cols] * (1.0 - sgt[:, cols])))
            dggn_ref[:, cols] += jnp.sum(dyr * on, axis=0, keepdims=True)
            don = dyr * ggn[:, cols]
            do = rstd * (don - jnp.mean(don, axis=-1, keepdims=True) - on * jnp.mean(don * on, axis=-1, keepdims=True))
            do_ref[:, cols] = do.astype(_BF)

    return pl.pallas_call(
        body, name="mixout_bwd", grid=(L // tm,),
        in_specs=[_row_spec(tm, D_MODEL), _weight_spec((D_MODEL, D_MODEL)), _weight_spec((SSM_W, 2 * SSM_W)),
                  _row_spec(tm, 2 * SSM_W), _row_spec(tm, SSM_W), _row_spec(tm, RET_W), _row_spec(tm, RET_W),
                  _full_spec((1, RET_W))],
        out_specs=[_row_spec(tm, 2 * SSM_W), _row_spec(tm, SSM_W), _row_spec(tm, RET_W), _row_spec(tm, RET_W),
                   _full_spec((1, RET_W))],
        out_shape=[jax.ShapeDtypeStruct((L, 2 * SSM_W), _BF), jax.ShapeDtypeStruct((L, SSM_W), _F32),
                   jax.ShapeDtypeStruct((L, RET_W), _F32), jax.ShapeDtypeStruct((L, RET_W), _BF),
                   jax.ShapeDtypeStruct((1, RET_W), _F32)],
        compiler_params=_params("arbitrary"),
    )(dmix, w_out, w_glu, glu, s, o, gate, ggn)


def _s5_bwd(u, ds, xs, ent, bmat, cmat, tab_r, pw_r, d_skip, tb):
    L = u.shape[0]
    nt = L // tb
    seg = tb // SUBLANES
    G = KB_PER_STEP
    rcol = pl.BlockSpec((tb, G * LANES), lambda kb, t: (nt - 1 - t, kb))
    sp = _s5_specs(seg, time=lambda t: nt - 1 - t)
    aspec = pl.BlockSpec((G, SUBLANES, 2 * KB_STATES), lambda kb, t: (kb, 0, 0))

    def body(u_ref, ds_ref, x_ref, ent_ref, b_ref, c_ref, tr_ref, pr_ref, d_ref,
             du_ref, db_ref, dc_ref, da_ref, dd_ref, up_scr, dp_scr, g_scr, lc_scr):
        @pl.when(pl.program_id(1) == 0)
        def _():
            lc_scr[...] = jnp.zeros_like(lc_scr)
            db_ref[...] = jnp.zeros_like(db_ref)
            dc_ref[...] = jnp.zeros_like(dc_ref)
            da_ref[...] = jnp.zeros_like(da_ref)
            dd_ref[...] = jnp.zeros_like(dd_ref)

        _rows_to_segments(up_scr, u_ref, seg)
        _rows_to_segments(dp_scr, ds_ref, seg)
        for g in range(G):
            g_scr[g] = _dot_nt(dp_scr[g].astype(_BF), c_ref[g]).reshape(seg, SUBLANES, 2 * KB_STATES)
        _scan_segments(g_scr, tr_ref, pr_ref, lc_scr, seg, reverse=True, fwd_ref=x_ref, fwd_entry_ref=ent_ref.at[:, 0],
                       da_ref=da_ref)
        for g in range(G):
            cols = slice(g * LANES, (g + 1) * LANES)
            uv, dsv = up_scr[g], dp_scr[g]
            ub, dsb = uv.astype(_BF), dsv.astype(_BF)
            lamb = g_scr[g].reshape(tb, 2 * KB_STATES).astype(_BF)
            db_ref[g] += _dot_tn(ub, lamb)
            dc_ref[g] += _dot_tn(dsb, x_ref[g].reshape(tb, 2 * KB_STATES).astype(_BF))
            dd_ref[:, cols] += jnp.sum(dsv * uv, axis=0, keepdims=True)
            up_scr[g] = _dot_nt(lamb, b_ref[g]) + d_ref[:, cols] * dsv
        _segments_to_rows(du_ref, up_scr, seg)

    return pl.pallas_call(
        body, name="s5_bwd", grid=(N_KB // G, nt),
        in_specs=[rcol, rcol, sp["x"], sp["ent"], sp["b"], sp["c"], sp["tab"], sp["pw"], sp["d"]],
        out_specs=[rcol, sp["b"], sp["b"], aspec, sp["d"]],
        out_shape=[jax.ShapeDtypeStruct((L, SSM_W), _F32),
                   jax.ShapeDtypeStruct((N_KB, LANES, 2 * KB_STATES), _F32),
                   jax.ShapeDtypeStruct((N_KB, LANES, 2 * KB_STATES), _F32),
                   jax.ShapeDtypeStruct((N_KB, SUBLANES, 2 * KB_STATES), _F32),
                   jax.ShapeDtypeStruct((1, SSM_W), _F32)],
        scratch_shapes=[pltpu.VMEM((G, tb, LANES), _F32)] * 2
        + [pltpu.VMEM((G, seg, SUBLANES, 2 * KB_STATES), _F32), pltpu.VMEM((G, SUBLANES, 2 * KB_STATES), _F32)],
        compiler_params=_params("parallel", "arbitrary"),
    )(u, ds, xs, ent, bmat, cmat, tab_r, pw_r, d_skip)


def _retention_bwd(q, k, v, do, r_prev, consts, cosf, sinf):
    L = q.shape[0]
    nc = L // CHUNK
    cps = math.gcd(RET_STEP_CHUNKS, nc)
    nb = nc // cps
    blk = pl.BlockSpec((cps * CHUNK, RET_W), lambda n: (nb - 1 - n, 0))
    rope_blk = pl.BlockSpec((cps * CHUNK, HEAD_D), lambda n: (nb - 1 - n, 0))

    def body(q_ref, k_ref, v_ref, do_ref, rp_ref, dm_ref, xi_ref, zeta_ref, gc_ref, cos_ref, sin_ref,
             dq_ref, dk_ref, dv_ref, g_scr):
        @pl.when(pl.program_id(0) == 0)
        def _():
            g_scr[...] = jnp.zeros_like(g_scr)

        for hh in range(N_HEAD):
            cols = slice(hh * HEAD_D, (hh + 1) * HEAD_D)
            dm, zeta = dm_ref[hh], zeta_ref[hh]
            gst = g_scr[hh]
            for c in reversed(range(cps)):
                rows = slice(c * CHUNK, (c + 1) * CHUNK)
                qv, kv, vv, dov = q_ref[rows, cols], k_ref[rows, cols], v_ref[rows, cols], do_ref[rows, cols]
                rb = rp_ref[hh, c].astype(_BF)
                gb = gst.astype(_BF)
                sb = (_dot_nt(qv, kv) * dm).astype(_BF)
                dab = (_dot_nt(dov, vv) * dm).astype(_BF)
                dox = (dov.astype(_F32) * xi_ref[hh]).astype(_BF)
                vz = (vv.astype(_F32) * zeta).astype(_BF)
                dq = _dot(dab, kv) + _dot_nt(dox, rb)
                dk = _dot_tn(dab, qv) + _dot_nt(vz, gb)
                dv = _dot_tn(sb, dov) + _dot(kv, gb) * zeta
                gst = gc_ref[hh, 0:1, :] * gst + _dot_tn(qv, dox)
                cs, sn = cos_ref[rows, :], sin_ref[rows, :]
                dq_ref[rows, cols] = _rope_t(dq, cs, sn).astype(_BF)
                dk_ref[rows, cols] = (_rope_t(dk, cs, sn) * (HEAD_D ** -0.5)).astype(_BF)
                dv_ref[rows, cols] = dv.astype(_BF)
            g_scr[hh] = gst

    return pl.pallas_call(
        body, name="retention_bwd", grid=(nb,),
        in_specs=[blk, blk, blk, blk, pl.BlockSpec((N_HEAD, cps, HEAD_D, HEAD_D), lambda n: (0, nb - 1 - n, 0, 0))]
        + _head_specs() + [rope_blk, rope_blk],
        out_specs=[blk, blk, blk],
        out_shape=[jax.ShapeDtypeStruct((L, RET_W), _BF)] * 3,
        scratch_shapes=[pltpu.VMEM((N_HEAD, HEAD_D, HEAD_D), _F32)],
        compiler_params=_params("arbitrary"),
    )(q, k, v, do, r_prev, *consts, cosf, sinf)


def _inproj_bwd(pieces, w_in_t, x, dx2, g1, tm):
    L = x.shape[0]

    def body(p0, p1, p2, p3, p4, w_ref, x_ref, dx2_ref, g_ref, dx_ref, dg_ref):
        @pl.when(pl.program_id(0) == 0)
        def _():
            dg_ref[...] = jnp.zeros_like(dg_ref)

        for rows in _row_chunks(tm):
            dh = None
            for j, p in enumerate((p0, p1, p2, p3, p4)):
                part = _dot(p[rows, :].astype(_BF), w_ref[j * RET_W:(j + 1) * RET_W, :])
                dh = part if dh is None else dh + part
            dz, dgr = _rms_bwd(x_ref[rows, :], g_ref[...], dh)
            dx_ref[rows, :] = dx2_ref[rows, :] + dz
            dg_ref[...] += jnp.sum(dgr, axis=0, keepdims=True)

    return pl.pallas_call(
        body, name="inproj_bwd", grid=(L // tm,),
        in_specs=[_row_spec(tm, RET_W)] * 5 + [_weight_spec((IN_COLS, D_MODEL)), _row_spec(tm, D_MODEL),
                                                 _row_spec(tm, D_MODEL), _full_spec((1, D_MODEL))],
        out_specs=[_row_spec(tm, D_MODEL), _full_spec((1, D_MODEL))],
        out_shape=[jax.ShapeDtypeStruct((L, D_MODEL), _F32), jax.ShapeDtypeStruct((1, D_MODEL), _F32)],
        compiler_params=_params("arbitrary"),
    )(*pieces, w_in_t, x, dx2, g1)


def _sum_adamw(parts, w, m, v, tr, name):
    _, R, Cc = parts.shape

    def body(p_ref, w_ref, m_ref, v_ref, g_ref, d_ref, nm_ref, nv_ref):
        gv = p_ref[0].astype(_F32)
        for s in range(1, N_DEV):
            gv = gv + p_ref[s].astype(_F32)
        g_ref[...] = gv
        nm = ADAM_B1 * m_ref[...] + (1.0 - ADAM_B1) * gv
        nv = ADAM_B2 * v_ref[...] + (1.0 - ADAM_B2) * (gv * gv)
        m_hat = nm / (1.0 - ADAM_B1 ** ADAM_STEP)
        v_hat = nv / (1.0 - ADAM_B2 ** ADAM_STEP)
        d_ref[...] = -ADAM_LR * (m_hat / (jnp.sqrt(v_hat) + ADAM_EPS) + ADAM_WD * w_ref[...])
        nm_ref[...] = nm
        nv_ref[...] = nv

    spec = _row_spec(tr, Cc)
    return pl.pallas_call(
        body, name=name, grid=(R // tr,),
        in_specs=[pl.BlockSpec((N_DEV, tr, Cc), lambda i: (0, i, 0))] + [spec] * 3, out_specs=[spec] * 4,
        out_shape=[jax.ShapeDtypeStruct((R, Cc), _F32)] * 4,
        compiler_params=_params("parallel"),
    )(parts, w, m, v)


def _my_place():
    return lax.axis_index("x"), lax.axis_index("y"), lax.axis_index("c")


def _all_gather(blocks):
    n = len(blocks)

    def body(*refs):
        x_refs, out_refs, done_ref = refs[:n], refs[n:2 * n], refs[2 * n]
        send_sems, recv_sems, local_sems = refs[2 * n + 1:]
        done_ref[...] = jnp.zeros_like(done_ref)
        x, y, c = _my_place()
        me, sibling = (x, y, c), (x, y, 1 - c)
        chips = [(1 - x, y), (x, 1 - y), (1 - x, 1 - y)]

        def slot(a, px, py, pc):
            return out_refs[a].at[4 * px + 2 * py + pc]

        def copy(a, k, blk, to, own=False):
            return pltpu.make_async_remote_copy(
                src_ref=x_refs[a] if own else slot(a, *blk), dst_ref=slot(a, *blk),
                send_sem=send_sems.at[a, k], recv_sem=recv_sems.at[a, k], device_id=to, device_id_type=MESH)

        mine = [pltpu.make_async_copy(x_refs[a], slot(a, *me), local_sems.at[a]) for a in range(n)]
        for cp in mine:
            cp.start()
        first = []
        for a in range(n):
            first.append(copy(a, 0, me, sibling, own=True))
            first += [copy(a, 1 + j, me, (*chip, c), own=True) for j, chip in enumerate(chips)]
        for cp in first:
            cp.start()
        passed = []
        for j, chip in enumerate(chips):
            for a in range(n):
                copy(a, 1 + j, (*chip, c), me).wait_recv()
                fwd = copy(a, 4 + j, (*chip, c), sibling)
                fwd.start()
                passed.append(fwd)
        for a in range(n):
            copy(a, 0, sibling, me).wait_recv()
            for j, chip in enumerate(chips):
                copy(a, 4 + j, (*chip, 1 - c), me).wait_recv()
        for cp in first + passed:
            cp.wait_send()
        for cp in mine:
            cp.wait()

    any_spec = pl.BlockSpec(memory_space=pl.ANY)
    outs = pl.pallas_call(
        body, name="weights_all_gather",
        in_specs=[any_spec] * n, out_specs=[any_spec] * n + [pl.BlockSpec(memory_space=pltpu.VMEM)],
        out_shape=[jax.ShapeDtypeStruct((N_DEV,) + b.shape, b.dtype) for b in blocks]
        + [jax.ShapeDtypeStruct((SUBLANES, LANES), _F32)],
        scratch_shapes=[pltpu.SemaphoreType.DMA((n, 7)), pltpu.SemaphoreType.DMA((n, 7)), pltpu.SemaphoreType.DMA((n,))],
    )(*blocks)
    return outs[:n], outs[n]


def _exchange(bigs, small):
    n = len(bigs)
    r = small.shape[0]

    def body(*refs):
        in_refs, out_refs = refs[:n + 1], refs[n + 1:2 * n + 2]
        send_sems, recv_sems, local_sems = refs[2 * n + 2:]
        x, y, c = _my_place()
        me = 4 * x + 2 * y + c
        own = [pltpu.make_async_copy(in_refs[a].at[me], out_refs[a].at[me], local_sems.at[a]) for a in range(n)]
        own.append(pltpu.make_async_copy(in_refs[n], out_refs[n].at[me], local_sems.at[n]))
        for cp in own:
            cp.start()
        copies = []
        for kk in range(1, N_DEV):
            px, py, pc = x ^ (kk >> 2), y ^ ((kk >> 1) & 1), c ^ (kk & 1)
            peer = 4 * px + 2 * py + pc
            for a in range(n + 1):
                src = in_refs[a].at[peer] if a < n else in_refs[a]
                copies.append(pltpu.make_async_remote_copy(
                    src_ref=src, dst_ref=out_refs[a].at[me],
                    send_sem=send_sems.at[a, kk - 1], recv_sem=recv_sems.at[a, kk - 1],
                    device_id=(px, py, pc), device_id_type=MESH))
        for cp in copies:
            cp.start()
        for cp in copies:
            cp.wait_recv()
        for cp in copies:
            cp.wait_send()
        for cp in own:
            cp.wait()

    any_spec = pl.BlockSpec(memory_space=pl.ANY)
    outs = pl.pallas_call(
        body, name="grad_exchange",
        in_specs=[any_spec] * (n + 1), out_specs=[any_spec] * (n + 1),
        out_shape=[jax.ShapeDtypeStruct(b.shape, b.dtype) for b in bigs]
        + [jax.ShapeDtypeStruct((N_DEV, r, LANES), small.dtype)],
        scratch_shapes=[pltpu.SemaphoreType.DMA((n + 1, 7)), pltpu.SemaphoreType.DMA((n + 1, 7)),
                        pltpu.SemaphoreType.DMA((n + 1,))],
    )(*bigs, small)
    return outs[:n], outs[n]


HBM_SPEC = pl.BlockSpec(memory_space=pltpu.HBM)
SEM_SPEC = pl.BlockSpec(memory_space=pltpu.SEMAPHORE)
DATAFLOW = pltpu.SideEffectType.DATAFLOW_SIDE_EFFECTING


def _my_index():
    x, y, c = _my_place()
    return 4 * x + 2 * y + c


def _landing(own_block):
    zone = lax.empty((N_DEV,) + own_block.shape, own_block.dtype)
    return lax.dynamic_update_index_in_dim(zone, own_block, _my_index(), 0)


def _split_copies(src_refs, land_refs, send_sems, recv_sems, gather):
    x, y, c = _my_place()
    me = 4 * x + 2 * y + c
    copies = []
    for kk in range(1, N_DEV):
        px, py, pc = x ^ (kk >> 2), y ^ ((kk >> 1) & 1), c ^ (kk & 1)
        peer = 4 * px + 2 * py + pc
        for a, (src, land) in enumerate(zip(src_refs, land_refs)):
            copies.append(pltpu.make_async_remote_copy(
                src_ref=src if gather else src.at[peer], dst_ref=land.at[me],
                send_sem=send_sems.at[a * 7 + kk - 1], recv_sem=recv_sems.at[a * 7 + kk - 1],
                device_id=(px, py, pc), device_id_type=MESH))
    return copies


def _split_start(srcs, lands, gather, name):
    n = len(srcs)

    def body(*refs):
        src_refs, land_refs = refs[:n], refs[n:2 * n]
        send_sems, recv_sems = refs[2 * n], refs[2 * n + 1]
        token = refs[-1]
        for cp in _split_copies(src_refs, land_refs, send_sems, recv_sems, gather):
            cp.start()
        token[...] = jnp.zeros_like(token)

    outs = pl.pallas_call(
        body, name=name,
        out_shape=(pltpu.SemaphoreType.DMA((7 * n,)), pltpu.SemaphoreType.DMA((7 * n,)),
                   *[pltpu.HBM(t.shape, t.dtype) for t in srcs], *[pltpu.HBM(t.shape, t.dtype) for t in lands],
                   jax.ShapeDtypeStruct((SUBLANES, LANES), _F32)),
        in_specs=[HBM_SPEC] * (2 * n),
        out_specs=(SEM_SPEC, SEM_SPEC, *[HBM_SPEC] * (2 * n), pl.BlockSpec(memory_space=pltpu.VMEM)),
        input_output_aliases={i: 2 + i for i in range(2 * n)},
        compiler_params=pltpu.CompilerParams(has_side_effects=DATAFLOW),
    )(*[pltpu.with_memory_space_constraint(t, pltpu.HBM) for t in list(srcs) + list(lands)])
    return outs[0], outs[1], outs[2:2 + n], outs[2 + n:2 + 2 * n], outs[-1]


def _split_wait(send_sems, recv_sems, srcs, lands, after, gather, name):
    n = len(srcs)

    def body(*refs):
        src_refs, land_refs = refs[:n], refs[n:2 * n]
        send_s, recv_s = refs[2 * n], refs[2 * n + 1]
        for cp in _split_copies(src_refs, land_refs, send_s, recv_s, gather):
            cp.wait_send()
            cp.wait_recv()

    outs = pl.pallas_call(
        body, name=name,
        out_shape=tuple(pltpu.HBM(t.shape, t.dtype) for t in list(srcs) + list(lands)),
        in_specs=[HBM_SPEC] * (2 * n) + [SEM_SPEC, SEM_SPEC, pl.BlockSpec(memory_space=pl.ANY)],
        out_specs=tuple([HBM_SPEC] * (2 * n)),
        input_output_aliases={i: i for i in range(2 * n)},
        compiler_params=pltpu.CompilerParams(has_side_effects=DATAFLOW),
    )(*srcs, *lands, send_sems, recv_sems, after)
    return outs[n:]


def _discretize(lam_re, lam_im, log_dt, b_re, b_im):
    lr = jnp.minimum(lam_re, -1e-4)
    li = lam_im
    dt = jnp.exp(log_dt)[:, None]
    er = jnp.exp(lr * dt)
    ar, ai = er * jnp.cos(li * dt), er * jnp.sin(li * dt)
    den = lr * lr + li * li
    cr = ((ar - 1.0) * lr + ai * li) / den
    ci = (ai * lr - (ar - 1.0) * li) / den
    bbr = cr[:, :, None] * b_re - ci[:, :, None] * b_im
    bbi = cr[:, :, None] * b_im + ci[:, :, None] * b_re
    return ar, ai, bbr, bbi


def _cmul(ar, ai, br, bi):
    return ar * br - ai * bi, ar * bi + ai * br


def _cpowers(ar, ai, n):
    pr, pi = ar[None], ai[None]
    while pr.shape[0] < n:
        nr, ni = _cmul(pr, pi, pr[-1][None], pi[-1][None])
        pr, pi = jnp.concatenate([pr, nr]), jnp.concatenate([pi, ni])
    return pr[:n], pi[:n]


def _scan_tables(ar, ai, seg, reverse):
    if reverse:
        ai = -ai
    ar, ai = ar.reshape(N_KB, KB_STATES), ai.reshape(N_KB, KB_STATES)
    pr, pi = _cpowers(ar, ai, seg)
    a1 = (pr[-1], pi[-1])
    a2 = _cmul(*a1, *a1)
    a4 = _cmul(*a2, *a2)
    row = jnp.arange(SUBLANES)[None, :, None]
    wide = lambda t: jnp.broadcast_to(t[:, None, :], (N_KB, SUBLANES, KB_STATES))
    tabs = [wide(ar), wide(ai)]
    for dist, (qr, qi) in ((1, a1), (2, a2), (4, a4)):
        keep = (row < SUBLANES - dist) if reverse else (row >= dist)
        tabs += [jnp.where(keep, wide(qr), 0.0), jnp.where(keep, wide(qi), 0.0)]
    tabs += [wide(a1[0]), wide(a1[1])]
    if reverse:
        pr, pi = pr[::-1], pi[::-1]
    pw = jnp.transpose(jnp.concatenate([pr, pi], axis=-1), (1, 0, 2))[:, :, None, :]
    return jnp.stack(tabs, axis=1).astype(_F32), pw.astype(_F32)


def _block_diag_in(br, bi):
    eye = jnp.eye(GROUPS_PER_KB, dtype=_F32)
    one = lambda t: jnp.einsum("kgpc,gh->kgchp", t.reshape(N_KB, GROUPS_PER_KB, N_STATE, SSM_GC), eye).reshape(
        N_KB, LANES, KB_STATES)
    return jnp.concatenate([one(br), one(bi)], axis=-1)


def _block_diag_in_t(dmat):
    d6 = dmat.reshape(N_KB, GROUPS_PER_KB, SSM_GC, 2, GROUPS_PER_KB, N_STATE)
    eye = jnp.eye(GROUPS_PER_KB, dtype=_F32)
    both = jnp.einsum("kgcrhp,gh->rkgpc", d6, eye).reshape(2, N_GROUP, N_STATE, SSM_GC)
    return both[0], both[1]


def _block_diag_out(c_re, c_im):
    eye = jnp.eye(GROUPS_PER_KB, dtype=_F32)
    one = lambda t: jnp.einsum("kgcp,gh->khpgc", t.reshape(N_KB, GROUPS_PER_KB, SSM_GC, N_STATE), eye).reshape(
        N_KB, KB_STATES, LANES)
    return jnp.concatenate([one(c_re), -one(c_im)], axis=1)


def _block_diag_out_t(dmat_t):
    d6 = dmat_t.reshape(N_KB, GROUPS_PER_KB, SSM_GC, 2, GROUPS_PER_KB, N_STATE)
    eye = jnp.eye(GROUPS_PER_KB, dtype=_F32)
    both = jnp.einsum("kgcrhp,gh->rkgcp", d6, eye).reshape(2, N_GROUP, SSM_GC, N_STATE)
    return both[0], -both[1]


SMALL_NAMES = ("norm_mix_pre", "norm_mix_post", "ret_gn_gain", "ssm_lambda_re", "ssm_lambda_im", "ssm_log_dt",
               "ssm_b_re", "ssm_b_im", "ssm_c_re", "ssm_c_im", "ssm_d", "norm_mlp_pre", "norm_mlp_post")


def _local_grads(x, tgt, small, weights, emit, emit_small, tm, tk, tb, zero=0.0):
    L = x.shape[0]
    g1, g2, ggn = small["norm_mix_pre"], small["norm_mix_post"], small["ret_gn_gain"]
    g3, g4, d_skip = small["norm_mlp_pre"], small["norm_mlp_post"], small["ssm_d"]

    rope = _rope_tables(L)
    consts = _ret_consts()

    disc_in = (small["ssm_lambda_re"][0], small["ssm_lambda_im"][0], small["ssm_log_dt"][0] + zero,
               small["ssm_b_re"][0], small["ssm_b_im"][0])
    (ar, ai, bbr, bbi), disc_vjp = jax.vjp(_discretize, *disc_in)
    bmat = _block_diag_in(bbr, bbi).astype(_BF)
    cmat = _block_diag_out(small["ssm_c_re"][0], small["ssm_c_im"][0]).astype(_BF)
    seg = tb // SUBLANES
    tab_f, pw_f = _scan_tables(ar, ai, seg, False)
    tab_r, pw_r = _scan_tables(ar, ai, seg, True)

    (w_in_t,) = weights("in", pw_r)
    h1, q, k, v, gate, u, cosf, sinf = _inproj_fwd(x, g1, w_in_t, rope, tm)
    o, y_ret, r_prev = _retention_fwd(q, k, v, gate, ggn, consts)
    s, xs, ent = _s5_fwd(u, bmat, cmat, tab_f, pw_f, d_skip, tb)
    w_glu, w_out = weights("mix", s)
    ys, glu, cat, mix, x2 = _mixout_fwd(s, y_ret, x, w_glu, w_out, g2, min(2 * tm, L))
    w_ff1, w_ff2 = weights("mlp", x2)
    h3, f1 = _ff1_fwd(x2, g3, w_ff1, tm)
    dy, dm, dg4, sq = _ff2_loss(f1, x2, tgt, g4, w_ff2, min(2 * tm, L))

    df1, dw_ff2 = _ff2_bwd(dm, f1, w_ff2, min(1024, L), 1024)
    dx2, dmix, dg3, dg2 = _ff1_bwd(df1, w_ff1, x2, mix, dy, g3, g2, min(2 * tm, L))
    dw_ff1 = _matmul_tn(h3, df1, tk, FF1_COLS, "dw_ff1", slots=True)
    zero = emit({"w_ff1": dw_ff1, "w_ff2": dw_ff2})
    dglu, ds, dgate, do, dggn = _mixout_bwd(dmix, w_out, w_glu, glu, s, o, gate, ggn if zero is None else ggn + zero, tm)
    dw_out = _matmul_tn(cat, dmix, tk, 1024, "dw_out")
    dw_glu = _matmul_tn(ys, dglu, tk, 1024, "dw_glu")
    zero = emit({"w_glu": dw_glu, "w_out": dw_out})
    du, dbmat, dcmat, da8, dd = _s5_bwd(u, ds, xs, ent, bmat, cmat, tab_r, pw_r,
                                        d_skip if zero is None else d_skip + zero, tb)
    dq, dk, dv = _retention_bwd(q, k, v, do, r_prev, consts, cosf, sinf)
    pieces = (dq, dk, dv, dgate, du)
    dw_in_t = jnp.concatenate([_matmul_tn(p, h1, tk, D_MODEL, "dw_in_%d" % j) for j, p in enumerate(pieces)], axis=0)
    zero = emit({"w_in": dw_in_t})

    da = jnp.sum(da8, axis=1)
    dar = da[:, :KB_STATES].reshape(N_GROUP, N_STATE)
    dai = da[:, KB_STATES:].reshape(N_GROUP, N_STATE)
    dbr, dbi = _block_diag_in_t(dbmat)
    dlre, dlim, dldt, dbre, dbim = disc_vjp((dar, dai, dbr, dbi))
    dcre, dcim = _block_diag_out_t(dcmat)

    zero2 = emit_small({
        "norm_mix_post": dg2, "ret_gn_gain": dggn,
        "ssm_lambda_re": dlre[None], "ssm_lambda_im": dlim[None], "ssm_log_dt": dldt[None],
        "ssm_b_re": dbre[None], "ssm_b_im": dbim[None], "ssm_c_re": dcre[None], "ssm_c_im": dcim[None],
        "ssm_d": dd, "norm_mlp_pre": dg3, "norm_mlp_post": dg4,
    }, sq)
    for z in (zero, zero2):
        g1 = g1 if z is None else g1 + z
    gx, dg1 = _inproj_bwd(pieces, w_in_t, x, dx2, g1, min(2 * tm, L))
    return gx, dg1


BIG_SHAPES = {"w_in": (D_MODEL, IN_COLS // N_DEV), "w_glu": (SSM_W, 2 * SSM_W // N_DEV), "w_out": (D_MODEL // N_DEV, D_MODEL),
              "w_ff1": (D_MODEL, FF1_COLS), "w_ff2": (D_FF // N_DEV, D_MODEL)}
BIG_NAMES = ("w_in", "w_glu", "w_out", "w_ff1", "w_ff2")


def _cols_from_slots(g):
    return jnp.transpose(g, (1, 0, 2)).reshape(g.shape[1], N_DEV * g.shape[2])


def _cols_to_slots(dw):
    r, cols = dw.shape
    return jnp.transpose(dw.reshape(r, N_DEV, cols // N_DEV), (1, 0, 2))


WEIGHT_GROUPS = {"in": ("w_in",), "mix": ("w_glu", "w_out"), "mlp": ("w_ff1", "w_ff2")}


def _weight_from_slots(name, g):
    if name == "w_glu":
        return _cols_from_slots(g)
    if name == "w_ff1":
        return g
    return g.reshape(N_DEV * g.shape[1], g.shape[2])


def _grad_slots(name, dw):
    if name == "w_glu":
        return _cols_to_slots(dw)
    if name == "w_ff1":
        return dw
    if name == "w_in":
        return dw.reshape(N_DEV, BIG_SHAPES[name][1], BIG_SHAPES[name][0])
    return dw.reshape((N_DEV,) + BIG_SHAPES[name])


PIECE_ROWS = 8


def _small_layout(shapes):
    off, rows = {}, 0
    for n in SMALL_NAMES:
        off[n] = rows
        rows += -(-math.prod(shapes[n]) // (PIECE_ROWS * LANES)) * PIECE_ROWS
    return off, rows, rows + PIECE_ROWS


def _pack_small(vals, shapes, last=None):
    parts = []
    for n in SMALL_NAMES:
        flat = vals[n].reshape(-1).astype(_F32)
        pad = -flat.shape[0] % (PIECE_ROWS * LANES)
        if pad:
            flat = jnp.concatenate([flat, jnp.zeros((pad,), _F32)])
        parts.append(flat.reshape(-1, LANES))
    parts.append(jnp.zeros((PIECE_ROWS, LANES), _F32) if last is None else last)
    return jnp.concatenate(parts, axis=0)


def _unpack_small(buf, shapes):
    off, _, _ = _small_layout(shapes)
    out = {}
    for n in SMALL_NAMES:
        size = math.prod(shapes[n])
        rows = -(-size // LANES)
        out[n] = buf[off[n]:off[n] + rows].reshape(-1)[:size].reshape(shapes[n])
    return out


WEIGHT_NAMES = ('norm_mix_pre', 'norm_mix_post', 'w_in', 'ret_gn_gain', 'ssm_lambda_re', 'ssm_lambda_im', 'ssm_log_dt',
                'ssm_b_re', 'ssm_b_im', 'ssm_c_re', 'ssm_c_im', 'ssm_d', 'w_glu', 'w_out', 'norm_mlp_pre',
                'norm_mlp_post', 'w_ff1', 'w_ff2')


def kernel(x, norm_mix_pre, norm_mix_post, w_in, ret_gn_gain, ssm_lambda_re, ssm_lambda_im, ssm_log_dt, ssm_b_re, ssm_b_im, ssm_c_re, ssm_c_im, ssm_d, w_glu, w_out, norm_mlp_pre, norm_mlp_post, w_ff1, w_ff2, loss_target, m_norm_mix_pre, m_norm_mix_post, m_w_in, m_ret_gn_gain, m_ssm_lambda_re, m_ssm_lambda_im, m_ssm_log_dt, m_ssm_b_re, m_ssm_b_im, m_ssm_c_re, m_ssm_c_im, m_ssm_d, m_w_glu, m_w_out, m_norm_mlp_pre, m_norm_mlp_post, m_w_ff1, m_w_ff2, v_norm_mix_pre, v_norm_mix_post, v_w_in, v_ret_gn_gain, v_ssm_lambda_re, v_ssm_lambda_im, v_ssm_log_dt, v_ssm_b_re, v_ssm_b_im, v_ssm_c_re, v_ssm_c_im, v_ssm_d, v_w_glu, v_w_out, v_norm_mlp_pre, v_norm_mlp_post, v_w_ff1, v_w_ff2):
    args = dict(locals())
    w = {n: args[n] for n in WEIGHT_NAMES}
    m = {n: args["m_" + n] for n in WEIGHT_NAMES}
    v = {n: args["v_" + n] for n in WEIGHT_NAMES}
    L = x.shape[1]
    tm = min(256, L)
    tk = min(2048, L)
    tb = min(512, L)

    gathers, zero = {}, jnp.zeros((), _F32)
    for group, names in WEIGHT_GROUPS.items():
        blocks = [(w[n][0].T if n == "w_in" else w[n][0]).astype(_BF) for n in names]
        blocks[0] = blocks[0] + zero.astype(_BF)
        gathers[group] = _split_start(blocks, [_landing(b) for b in blocks], True, "weights_start_" + group)
        zero = gathers[group][4][0, 0]

    def weights(group, after):
        landed = _split_wait(*gathers[group][:4], after, True, "weights_wait_" + group)
        return [_weight_from_slots(n, g) for n, g in zip(WEIGHT_GROUPS[group], landed)]

    in_flight = []

    def emit(dws):
        names = sorted(dws)
        srcs = [_grad_slots(n, dws[n]) for n in names]
        lands = [_landing(lax.dynamic_index_in_dim(t, _my_index(), 0, keepdims=False)) for t in srcs]
        started = _split_start(srcs, lands, False, "grads_start_" + "_".join(names))
        in_flight.append((names, started))
        return started[4][0, 0]

    shapes = {n: w[n].shape for n in SMALL_NAMES}
    first_piece = {SMALL_NAMES[0]: jnp.zeros(shapes[SMALL_NAMES[0]], _F32)}
    small_flight = []

    def emit_small(gs, sq):
        loss_rows = jnp.broadcast_to(0.5 / D_MODEL * jnp.sum(sq), (PIECE_ROWS, LANES)).astype(_F32)
        buf = _pack_small({**first_piece, **gs}, shapes, loss_rows)
        small_flight.append(_split_start([buf], [_landing(buf)], True, "small_grads_start"))
        return small_flight[0][4][0, 0]

    small_w = {n: w[n] for n in SMALL_NAMES}
    gx, dg1 = _local_grads(x[0], loss_target[0], small_w, weights, emit, emit_small, tm, tk, tb, zero=zero)
    last_buf = dg1.reshape(PIECE_ROWS, LANES)
    last_started = _split_start([last_buf], [_landing(last_buf)], True, "last_grad_start")

    grads, delta, new_m, new_v = {}, {}, {}, {}
    after = last_started[4]
    for names, started in in_flight:
        landed = _split_wait(*started[:4], after, False, "grads_wait_" + "_".join(names))
        for n, parts in zip(names, landed):
            flip = (lambda t: t.T) if n == "w_in" else (lambda t: t)
            res = _sum_adamw(parts, flip(w[n][0]), flip(m[n][0]), flip(v[n][0]), math.gcd(256, parts.shape[1]), "adamw_" + n)
            grads[n], delta[n], new_m[n], new_v[n] = (flip(t)[None] for t in res)
        after = res[1]
    small_parts = _split_wait(*small_flight[0][:4], after, True, "small_grads_wait")[0]
    last_parts = _split_wait(*last_started[:4], small_parts, True, "last_grad_wait")[0]
    small_parts = lax.dynamic_update_slice(small_parts, last_parts, (0, 0, 0))
    sw, sm, sv = _pack_small(w, shapes), _pack_small(m, shapes), _pack_small(v, shapes)
    res = _sum_adamw(small_parts, sw, sm, sv, sw.shape[0], "adamw_small")
    for dst, buf in zip((grads, delta, new_m, new_v), res):
        dst.update(_unpack_small(buf, shapes))
    _, loss_at, _ = _small_layout(shapes)
    loss = res[0][loss_at, 0]

    return (loss, gx[None], *[grads[n] for n in WEIGHT_NAMES], *[delta[n] for n in WEIGHT_NAMES],
            *[new_m[n] for n in WEIGHT_NAMES], *[new_v[n] for n in WEIGHT_NAMES])
```

```python
import math

import jax
import jax.numpy as jnp
from jax import lax
from jax.experimental import pallas as pl
from jax.experimental.pallas import tpu as pltpu

_BF = jnp.bfloat16
_F32 = jnp.float32

D_MODEL = 1024
RET_W = 512
N_HEAD = 4
HEAD_D = 128
CHUNK = 256
ROPE_CHUNK = 128
SSM_W = 512
SSM_GC = 16
N_GROUP = 32
N_STATE = 64
GROUPS_PER_KB = 8
N_KB = 4
KB_STATES = GROUPS_PER_KB * N_STATE
D_FF = 4096
IN_COLS = 2560
NORM_EPS = 1e-6
ROPE_BASE = 10000.0
N_DEV = 8

ADAM_LR = 0.001
ADAM_B1 = 0.9
ADAM_B2 = 0.999
ADAM_EPS = 1e-08
ADAM_WD = 0.01
ADAM_STEP = 10

SUBLANES = 8
LANES = 128
VMEM_LIMIT = 52 * 1024 * 1024
RET_STEP_CHUNKS = 2
KB_PER_STEP = 2
SCAN_UNROLL = True
FIX_UNROLL = 8

MESH = pl.DeviceIdType.MESH


def _params(*sem):
    return pltpu.CompilerParams(dimension_semantics=sem, vmem_limit_bytes=VMEM_LIMIT)


def _dot(a, b):
    return jnp.dot(a, b, preferred_element_type=_F32)


def _dot_nt(a, b):
    return lax.dot_general(a, b, (((1,), (1,)), ((), ())), preferred_element_type=_F32)


def _dot_tn(a, b):
    return lax.dot_general(a, b, (((0,), (0,)), ((), ())), preferred_element_type=_F32)


def _rms_r(z):
    return lax.rsqrt(jnp.mean(z * z, axis=-1, keepdims=True) + NORM_EPS)


def _rms_bwd(z, g, dn):
    r = _rms_r(z)
    t = dn * g
    dz = r * t - z * (r * r * r * jnp.mean(t * z, axis=-1, keepdims=True))
    return dz, dn * z * r


def _rope(t, cs, sn):
    return t * cs + pltpu.roll(t, HEAD_D // 2, 1) * sn


def _rope_t(t, cs, sn):
    return t * cs - pltpu.roll(t, HEAD_D // 2, 1) * sn


def _sigmoid(z):
    return 1.0 / (1.0 + jnp.exp(-z))


_GELU_C = math.sqrt(2.0 / math.pi)


def _gelu(z):
    return 0.5 * z * (1.0 + jnp.tanh(_GELU_C * (z + 0.044715 * z * z * z)))


def _gelu_grad(z):
    th = jnp.tanh(_GELU_C * (z + 0.044715 * z * z * z))
    return 0.5 * (1.0 + th) + 0.5 * z * (1.0 - th * th) * _GELU_C * (1.0 + 3 * 0.044715 * z * z)


ROW_CHUNK = 256


def _row_chunks(tm):
    return [pl.ds(i, min(ROW_CHUNK, tm)) for i in range(0, tm, ROW_CHUNK)]


def _row_spec(tm, n):
    return pl.BlockSpec((tm, n), lambda i: (i, 0))


def _full_spec(shape):
    nd = len(shape)
    return pl.BlockSpec(shape, lambda *_: (0,) * nd)


def _weight_spec(shape):
    nd = len(shape)
    return pl.BlockSpec(shape, lambda *_: (0,) * nd, pipeline_mode=pl.Buffered(1))


def _rope_tables(L):
    half = HEAD_D // 2
    inv_freq = ROPE_BASE ** (-jnp.arange(half, dtype=_F32) / half)
    twice = lambda t: jnp.concatenate([t, t], axis=-1)
    off = jnp.arange(ROPE_CHUNK, dtype=_F32)[:, None] * inv_freq[None, :]
    start = (ROPE_CHUNK * jnp.arange(L // ROPE_CHUNK, dtype=_F32))[:, None] * inv_freq[None, :]
    return (twice(jnp.cos(off)), twice(jnp.sin(off)),
            twice(jnp.cos(start))[:, None, :], twice(jnp.sin(start))[:, None, :])


def _inproj_fwd(x, g1, w_in_t, rope, tm):
    L = x.shape[0]
    n_chunks = tm // ROPE_CHUNK

    def body(x_ref, g_ref, w_ref, co_ref, so_ref, cs_ref, ss_ref, h_ref, q_ref, k_ref, v_ref, gate_ref, u_ref,
             cos_ref, sin_ref):
        xv = x_ref[...]
        h = (xv * _rms_r(xv) * g_ref[...]).astype(_BF)
        h_ref[...] = h
        proj = _dot_nt(h, w_ref[...])
        lane = lax.broadcasted_iota(jnp.int32, (ROPE_CHUNK, HEAD_D), 1)
        sign = jnp.where(lane < HEAD_D // 2, -1.0, 1.0)
        co, so = co_ref[...], so_ref[...]
        for c in range(n_chunks):
            chunk = pl.program_id(0) * n_chunks + c
            cst, sst = cs_ref[chunk], ss_ref[chunk]
            rows = slice(c * ROPE_CHUNK, (c + 1) * ROPE_CHUNK)
            cs = co * cst - so * sst
            sn = (so * cst + co * sst) * sign
            cos_ref[rows, :] = cs
            sin_ref[rows, :] = sn
            for hh in range(N_HEAD):
                lo = hh * HEAD_D
                q_ref[rows, lo:lo + HEAD_D] = _rope(proj[rows, lo:lo + HEAD_D], cs, sn).astype(_BF)
                kh = _rope(proj[rows, RET_W + lo:RET_W + lo + HEAD_D], cs, sn) * (HEAD_D ** -0.5)
                k_ref[rows, lo:lo + HEAD_D] = kh.astype(_BF)
        v_ref[...] = proj[:, 2 * RET_W:3 * RET_W].astype(_BF)
        gate_ref[...] = proj[:, 3 * RET_W:4 * RET_W]
        u_ref[...] = proj[:, 4 * RET_W:]

    nc = L // ROPE_CHUNK
    return pl.pallas_call(
        body, name="inproj_fwd", grid=(L // tm,),
        in_specs=[_row_spec(tm, D_MODEL), _full_spec((1, D_MODEL)), _weight_spec((IN_COLS, D_MODEL)),
                  _full_spec((ROPE_CHUNK, HEAD_D)), _full_spec((ROPE_CHUNK, HEAD_D)),
                  _full_spec((nc, 1, HEAD_D)), _full_spec((nc, 1, HEAD_D))],
        out_specs=[_row_spec(tm, D_MODEL)] + [_row_spec(tm, RET_W)] * 5 + [_row_spec(tm, HEAD_D)] * 2,
        out_shape=[jax.ShapeDtypeStruct((L, D_MODEL), _BF)] + [jax.ShapeDtypeStruct((L, RET_W), _BF)] * 3
        + [jax.ShapeDtypeStruct((L, RET_W), _F32)] * 2 + [jax.ShapeDtypeStruct((L, HEAD_D), _F32)] * 2,
        compiler_params=_params("parallel"),
    )(x, g1, w_in_t, *rope)


def _ret_consts():
    lg = jnp.log(1.0 - jnp.exp(jnp.linspace(math.log(1.0 / 32), math.log(1.0 / 512), N_HEAD))).astype(_F32)
    idx = jnp.arange(CHUNK, dtype=_F32)
    diff = idx[:, None] - idx[None, :]
    decay = jnp.where(diff[None] >= 0, jnp.exp(jnp.maximum(diff, 0.0)[None] * lg[:, None, None]), 0.0)
    zeta = jnp.exp((CHUNK - 1 - idx)[None, :] * lg[:, None])
    xi = jnp.exp((idx + 1.0)[None, :] * lg[:, None])
    gc = jnp.exp(CHUNK * lg)
    wide = lambda t: jnp.broadcast_to(t[:, :, None], (N_HEAD, CHUNK, HEAD_D)).astype(_F32)
    gcw = jnp.broadcast_to(gc[:, None, None], (N_HEAD, SUBLANES, HEAD_D)).astype(_F32)
    return decay.astype(_F32), wide(xi), wide(zeta), gcw


def _head_specs():
    wide = _full_spec((N_HEAD, CHUNK, HEAD_D))
    return [_full_spec((N_HEAD, CHUNK, CHUNK)), wide, wide, _full_spec((N_HEAD, SUBLANES, HEAD_D))]


def _retention_fwd(q, k, v, gate, ggn, consts):
    L = q.shape[0]
    nc = L // CHUNK
    cps = math.gcd(RET_STEP_CHUNKS, nc)
    blk = pl.BlockSpec((cps * CHUNK, RET_W), lambda n: (n, 0))

    def body(q_ref, k_ref, v_ref, gate_ref, ggn_ref, dm_ref, xi_ref, zeta_ref, gc_ref,
             o_ref, y_ref, rp_ref, r_scr):
        @pl.when(pl.program_id(0) == 0)
        def _():
            r_scr[...] = jnp.zeros_like(r_scr)

        for hh in range(N_HEAD):
            cols = slice(hh * HEAD_D, (hh + 1) * HEAD_D)
            state = r_scr[hh]
            for c in range(cps):
                rows = slice(c * CHUNK, (c + 1) * CHUNK)
                qv, kv, vv = q_ref[rows, cols], k_ref[rows, cols], v_ref[rows, cols]
                s = _dot_nt(qv, kv) * dm_ref[hh]
                o = _dot(s.astype(_BF), vv) + _dot(qv, state.astype(_BF)) * xi_ref[hh]
                o_ref[rows, cols] = o
                rp_ref[hh, c] = state
                vz = (vv.astype(_F32) * zeta_ref[hh]).astype(_BF)
                state = gc_ref[hh, 0:1, :] * state + _dot_tn(kv, vz)
                dlt = o - jnp.mean(o, axis=-1, keepdims=True)
                on = dlt * lax.rsqrt(jnp.mean(dlt * dlt, axis=-1, keepdims=True) + NORM_EPS)
                gt = gate_ref[rows, cols]
                y_ref[rows, cols] = (gt * _sigmoid(gt) * (on * ggn_ref[:, cols])).astype(_BF)
            r_scr[hh] = state

    return pl.pallas_call(
        body, name="retention_fwd", grid=(nc // cps,),
        in_specs=[blk, blk, blk, blk, _full_spec((1, RET_W))] + _head_specs(),
        out_specs=[blk, blk, pl.BlockSpec((N_HEAD, cps, HEAD_D, HEAD_D), lambda n: (0, n, 0, 0))],
        out_shape=[jax.ShapeDtypeStruct((L, RET_W), _F32), jax.ShapeDtypeStruct((L, RET_W), _BF),
                   jax.ShapeDtypeStruct((N_HEAD, nc, HEAD_D, HEAD_D), _F32)],
        scratch_shapes=[pltpu.VMEM((N_HEAD, HEAD_D, HEAD_D), _F32)],
        compiler_params=_params("arbitrary"),
    )(q, k, v, gate, ggn, *consts)


def _rows_to_segments(dst_scr, src_ref, seg):
    for g in range(dst_scr.shape[0]):
        for j in range(SUBLANES):
            dst_scr[g, pl.ds(j, seg, stride=SUBLANES), :] = src_ref[pl.ds(j * seg, seg), g * LANES:(g + 1) * LANES]


def _segments_to_rows(dst_ref, src_scr, seg):
    for g in range(src_scr.shape[0]):
        for j in range(SUBLANES):
            dst_ref[pl.ds(j * seg, seg), g * LANES:(g + 1) * LANES] = src_scr[g, pl.ds(j, seg, stride=SUBLANES), :]


def _scan_segments(x_ref, tab_ref, pw_ref, carry_ref, seg, reverse, entry_ref=None, fwd_ref=None, fwd_entry_ref=None,
                   da_ref=None):
    G = x_ref.shape[0]
    W = KB_STATES
    re, im = pl.ds(0, W), pl.ds(W, W)
    row_id = lax.broadcasted_iota(jnp.int32, (SUBLANES, W), 0)
    edge_in = (row_id == SUBLANES - 1) if reverse else (row_id == 0)
    edge_out = 0 if reverse else SUBLANES - 1
    a_tab = [(tab_ref[g, 0], tab_ref[g, 1]) for g in range(G)]

    def local(i, st):
        r = (seg - 1 - i) if reverse else i
        out = []
        for g in range(G):
            (ar, ai), (sr, si) = a_tab[g], st[g]
            nr = ar * sr - ai * si + x_ref[g, r, :, re]
            ni = ar * si + ai * sr + x_ref[g, r, :, im]
            x_ref[g, r, :, re] = nr
            x_ref[g, r, :, im] = ni
            out.append((nr, ni))
        return tuple(out)

    zero = jnp.zeros((SUBLANES, W), _F32)
    ends = lax.fori_loop(0, seg, local, tuple((zero, zero) for _ in range(G)), unroll=SCAN_UNROLL)

    entry = []
    shift = (SUBLANES - 1) if reverse else 1
    for g in range(G):
        er, ei = ends[g]
        fr = jnp.where(edge_in, carry_ref[g, :, re], pltpu.roll(er, shift, 0))
        fi = jnp.where(edge_in, carry_ref[g, :, im], pltpu.roll(ei, shift, 0))
        for j, dist in enumerate((1, 2, 4)):
            pr, pi = tab_ref[g, 2 + 2 * j], tab_ref[g, 3 + 2 * j]
            sh = (SUBLANES - dist) if reverse else dist
            sr, si = pltpu.roll(fr, sh, 0), pltpu.roll(fi, sh, 0)
            fr, fi = fr + pr * sr - pi * si, fi + pr * si + pi * sr
        br, bi = tab_ref[g, 8], tab_ref[g, 9]
        outr = br * fr - bi * fi + er
        outi = br * fi + bi * fr + ei
        carry_ref[g, :, re] = jnp.broadcast_to(outr[edge_out:edge_out + 1, :], (SUBLANES, W))
        carry_ref[g, :, im] = jnp.broadcast_to(outi[edge_out:edge_out + 1, :], (SUBLANES, W))
        entry.append((fr, fi))
        if entry_ref is not None:
            entry_ref[g, :, re] = fr
            entry_ref[g, :, im] = fi

    add_da = da_ref is not None

    def fix(r, st, first=False):
        out = []
        for g in range(G):
            fr, fi = entry[g]
            pwr, pwi = pw_ref[g, r, :, re], pw_ref[g, r, :, im]
            xr = x_ref[g, r, :, re] + (pwr * fr - pwi * fi)
            xi = x_ref[g, r, :, im] + (pwr * fi + pwi * fr)
            x_ref[g, r, :, re] = xr
            x_ref[g, r, :, im] = xi
            if add_da:
                prev = fwd_entry_ref.at[g] if first else fwd_ref.at[g, r - 1]
                xpr, xpi = prev[:, re], prev[:, im]
                out.append((st[g][0] + (xr * xpr + xi * xpi), st[g][1] + (xi * xpr - xr * xpi)))
            else:
                out.append(st[g])
        return tuple(out)

    if add_da:
        st = fix(0, tuple((zero, zero) for _ in range(G)), first=True)
        st = lax.fori_loop(1, seg, fix, st, unroll=SCAN_UNROLL)
        for g in range(G):
            da_ref[g, :, re] += st[g][0]
            da_ref[g, :, im] += st[g][1]
    else:
        lax.fori_loop(0, seg, fix, tuple((zero[0:1, 0:LANES],) for _ in range(G)), unroll=FIX_UNROLL)


def _s5_specs(seg, time=lambda t: t):
    G = KB_PER_STEP
    return dict(
        x=pl.BlockSpec((G, seg, SUBLANES, 2 * KB_STATES), lambda kb, t: (kb, time(t), 0, 0)),
        ent=pl.BlockSpec((G, 1, SUBLANES, 2 * KB_STATES), lambda kb, t: (kb, time(t), 0, 0)),
        b=pl.BlockSpec((G, LANES, 2 * KB_STATES), lambda kb, t: (kb, 0, 0)),
        c=pl.BlockSpec((G, 2 * KB_STATES, LANES), lambda kb, t: (kb, 0, 0)),
        tab=pl.BlockSpec((G, 10, SUBLANES, KB_STATES), lambda kb, t: (kb, 0, 0, 0)),
        pw=pl.BlockSpec((G, seg, 1, 2 * KB_STATES), lambda kb, t: (kb, 0, 0, 0)),
        d=pl.BlockSpec((1, G * LANES), lambda kb, t: (0, kb)),
    )


def _s5_fwd(u, bmat, cmat, tab_f, pw_f, d_skip, tb):
    L = u.shape[0]
    nt = L // tb
    seg = tb // SUBLANES
    G = KB_PER_STEP
    ucol = pl.BlockSpec((tb, G * LANES), lambda kb, t: (t, kb))
    sp = _s5_specs(seg)

    def body(u_ref, b_ref, c_ref, tab_ref, pw_ref, d_ref, s_ref, x_ref, ent_ref, up_scr, y_scr, carry_scr):
        @pl.when(pl.program_id(1) == 0)
        def _():
            carry_scr[...] = jnp.zeros_like(carry_scr)

        _rows_to_segments(up_scr, u_ref, seg)
        for g in range(G):
            x_ref[g] = _dot(up_scr[g].astype(_BF), b_ref[g]).reshape(seg, SUBLANES, 2 * KB_STATES)
        _scan_segments(x_ref, tab_ref, pw_ref, carry_scr, seg, reverse=False, entry_ref=ent_ref.at[:, 0])
        for g in range(G):
            y = _dot(x_ref[g].reshape(tb, 2 * KB_STATES).astype(_BF), c_ref[g])
            y_scr[g] = y + d_ref[:, g * LANES:(g + 1) * LANES] * up_scr[g]
        _segments_to_rows(s_ref, y_scr, seg)

    return pl.pallas_call(
        body, name="s5_fwd", grid=(N_KB // G, nt),
        in_specs=[ucol, sp["b"], sp["c"], sp["tab"], sp["pw"], sp["d"]],
        out_specs=[ucol, sp["x"], sp["ent"]],
        out_shape=[jax.ShapeDtypeStruct((L, SSM_W), _F32),
                   jax.ShapeDtypeStruct((N_KB, L // SUBLANES, SUBLANES, 2 * KB_STATES), _F32),
                   jax.ShapeDtypeStruct((N_KB, nt, SUBLANES, 2 * KB_STATES), _F32)],
        scratch_shapes=[pltpu.VMEM((G, tb, LANES), _F32)] * 2 + [pltpu.VMEM((G, SUBLANES, 2 * KB_STATES), _F32)],
        compiler_params=_params("parallel", "arbitrary"),
    )(u, bmat, cmat, tab_f, pw_f, d_skip)


def _mixout_fwd(s, y_ret, x, w_glu, w_out, g2, tm):
    L = s.shape[0]

    def body(s_ref, yr_ref, x_ref, wg_ref, wo_ref, g_ref, ys_ref, glu_ref, cat_ref, mix_ref, x2_ref):
        for rows in _row_chunks(tm):
            ys = _gelu(s_ref[rows, :]).astype(_BF)
            ys_ref[rows, :] = ys
            glu = _dot(ys, wg_ref[...])
            glu_ref[rows, :] = glu
            cat_ref[rows, :RET_W] = yr_ref[rows, :]
            cat_ref[rows, RET_W:] = (glu[:, :SSM_W] * _sigmoid(glu[:, SSM_W:])).astype(_BF)
            mix = _dot(cat_ref[rows, :], wo_ref[...])
            mix_ref[rows, :] = mix
            x2_ref[rows, :] = x_ref[rows, :] + mix * _rms_r(mix) * g_ref[...]

    return pl.pallas_call(
        body, name="mixout_fwd", grid=(L // tm,),
        in_specs=[_row_spec(tm, SSM_W), _row_spec(tm, RET_W), _row_spec(tm, D_MODEL),
                  _weight_spec((SSM_W, 2 * SSM_W)), _weight_spec((D_MODEL, D_MODEL)), _full_spec((1, D_MODEL))],
        out_specs=[_row_spec(tm, SSM_W), _row_spec(tm, 2 * SSM_W), _row_spec(tm, D_MODEL),
                   _row_spec(tm, D_MODEL), _row_spec(tm, D_MODEL)],
        out_shape=[jax.ShapeDtypeStruct((L, SSM_W), _BF), jax.ShapeDtypeStruct((L, 2 * SSM_W), _F32),
                   jax.ShapeDtypeStruct((L, D_MODEL), _BF), jax.ShapeDtypeStruct((L, D_MODEL), _F32),
                   jax.ShapeDtypeStruct((L, D_MODEL), _F32)],
        compiler_params=_params("parallel"),
    )(s, y_ret, x, w_glu, w_out, g2)


FF1_COLS = D_FF // N_DEV


def _ff1_fwd(x2, g3, w1, tm):
    L = x2.shape[0]

    def body(x_ref, g_ref, w_ref, h_ref, f_ref):
        xv = x_ref[...]
        h = (xv * _rms_r(xv) * g_ref[...]).astype(_BF)
        h_ref[...] = h
        for j in range(N_DEV):
            f_ref[:, j * FF1_COLS:(j + 1) * FF1_COLS] = _dot(h, w_ref[j])

    return pl.pallas_call(
        body, name="ff1_fwd", grid=(L // tm,),
        in_specs=[_row_spec(tm, D_MODEL), _full_spec((1, D_MODEL)), _weight_spec((N_DEV, D_MODEL, FF1_COLS))],
        out_specs=[_row_spec(tm, D_MODEL), _row_spec(tm, D_FF)],
        out_shape=[jax.ShapeDtypeStruct((L, D_MODEL), _BF), jax.ShapeDtypeStruct((L, D_FF), _F32)],
        compiler_params=_params("parallel"),
    )(x2, g3, w1)


def _ff2_loss(f1, x2, tgt, g4, w2, tm):
    L = f1.shape[0]

    def body(f_ref, x_ref, t_ref, g_ref, w_ref, dy_ref, dm_ref, dg_ref, ls_ref):
        @pl.when(pl.program_id(0) == 0)
        def _():
            dg_ref[...] = jnp.zeros_like(dg_ref)
            ls_ref[...] = jnp.zeros_like(ls_ref)

        g = g_ref[...]
        for rows in _row_chunks(tm):
            rl = jnp.maximum(f_ref[rows, :], 0.0)
            m = _dot((rl * rl).astype(_BF), w_ref[...])
            y = x_ref[rows, :] + m * _rms_r(m) * g
            err = y - t_ref[rows, :]
            ls_ref[...] += jnp.sum(err * err, axis=0, keepdims=True)
            dy = err * (1.0 / D_MODEL)
            dy_ref[rows, :] = dy
            dm, dgr = _rms_bwd(m, g, dy)
            dm_ref[rows, :] = dm.astype(_BF)
            dg_ref[...] += jnp.sum(dgr, axis=0, keepdims=True)

    return pl.pallas_call(
        body, name="ff2_loss", grid=(L // tm,),
        in_specs=[_row_spec(tm, D_FF), _row_spec(tm, D_MODEL), _row_spec(tm, D_MODEL),
                  _full_spec((1, D_MODEL)), _weight_spec((D_FF, D_MODEL))],
        out_specs=[_row_spec(tm, D_MODEL), _row_spec(tm, D_MODEL), _full_spec((1, D_MODEL)), _full_spec((1, D_MODEL))],
        out_shape=[jax.ShapeDtypeStruct((L, D_MODEL), _F32), jax.ShapeDtypeStruct((L, D_MODEL), _BF),
                   jax.ShapeDtypeStruct((1, D_MODEL), _F32), jax.ShapeDtypeStruct((1, D_MODEL), _F32)],
        compiler_params=_params("arbitrary"),
    )(f1, x2, tgt, g4, w2)


def _ff2_bwd(dm, f1, w2, tm, tn):
    L = dm.shape[0]
    last = L // tm - 1

    def body(dm_ref, f_ref, w_ref, df_ref, dw_ref, acc):
        @pl.when(pl.program_id(1) == 0)
        def _():
            acc[...] = jnp.zeros_like(acc)

        dmv = dm_ref[...]
        rl = jnp.maximum(f_ref[...], 0.0)
        df_ref[...] = (_dot_nt(dmv, w_ref[...]) * (2.0 * rl)).astype(_BF)
        acc[...] += _dot_tn((rl * rl).astype(_BF), dmv)

        @pl.when(pl.program_id(1) == last)
        def _():
            dw_ref[...] = acc[...].astype(_BF)

    return pl.pallas_call(
        body, name="ff2_bwd", grid=(D_FF // tn, L // tm),
        in_specs=[pl.BlockSpec((tm, D_MODEL), lambda j, i: (i, 0)), pl.BlockSpec((tm, tn), lambda j, i: (i, j)),
                  pl.BlockSpec((tn, D_MODEL), lambda j, i: (j, 0))],
        out_specs=[pl.BlockSpec((tm, tn), lambda j, i: (i, j)), pl.BlockSpec((tn, D_MODEL), lambda j, i: (j, 0))],
        out_shape=[jax.ShapeDtypeStruct((L, D_FF), _BF), jax.ShapeDtypeStruct((D_FF, D_MODEL), _BF)],
        scratch_shapes=[pltpu.VMEM((tn, D_MODEL), _F32)],
        compiler_params=_params("parallel", "arbitrary"),
    )(dm, f1, w2)


def _ff1_bwd(df1, w1, x2, mix, dy, g3, g2, tm):
    L = df1.shape[0]

    def body(df_ref, w_ref, x2_ref, mix_ref, dy_ref, g3_ref, g2_ref, dx2_ref, dmix_ref, dg3_ref, dg2_ref):
        @pl.when(pl.program_id(0) == 0)
        def _():
            dg3_ref[...] = jnp.zeros_like(dg3_ref)
            dg2_ref[...] = jnp.zeros_like(dg2_ref)

        for rows in _row_chunks(tm):
            dh = _dot_nt(df_ref[rows, 0:FF1_COLS], w_ref[0])
            for j in range(1, N_DEV):
                dh = dh + _dot_nt(df_ref[rows, j * FF1_COLS:(j + 1) * FF1_COLS], w_ref[j])
            dz, dgr = _rms_bwd(x2_ref[rows, :], g3_ref[...], dh)
            dg3_ref[...] += jnp.sum(dgr, axis=0, keepdims=True)
            dx2 = dy_ref[rows, :] + dz
            dx2_ref[rows, :] = dx2
            dmx, dgr2 = _rms_bwd(mix_ref[rows, :], g2_ref[...], dx2)
            dg2_ref[...] += jnp.sum(dgr2, axis=0, keepdims=True)
            dmix_ref[rows, :] = dmx.astype(_BF)

    vec = _full_spec((1, D_MODEL))
    return pl.pallas_call(
        body, name="ff1_bwd", grid=(L // tm,),
        in_specs=[_row_spec(tm, D_FF), _weight_spec((N_DEV, D_MODEL, FF1_COLS)), _row_spec(tm, D_MODEL),
                  _row_spec(tm, D_MODEL), _row_spec(tm, D_MODEL), vec, vec],
        out_specs=[_row_spec(tm, D_MODEL), _row_spec(tm, D_MODEL), vec, vec],
        out_shape=[jax.ShapeDtypeStruct((L, D_MODEL), _F32), jax.ShapeDtypeStruct((L, D_MODEL), _BF),
                   jax.ShapeDtypeStruct((1, D_MODEL), _F32), jax.ShapeDtypeStruct((1, D_MODEL), _F32)],
        compiler_params=_params("arbitrary"),
    )(df1, w1, x2, mix, dy, g3, g2)


def _matmul_tn(a, b, tm, tn, name, slots=False):
    L, K = a.shape
    N = b.shape[1]
    last = L // tm - 1

    def body(a_ref, b_ref, o_ref, acc):
        @pl.when(pl.program_id(1) == 0)
        def _():
            acc[...] = jnp.zeros_like(acc)

        acc[...] += _dot_tn(a_ref[...].astype(_BF), b_ref[...].astype(_BF))

        @pl.when(pl.program_id(1) == last)
        def _():
            if slots:
                o_ref[0] = acc[...].astype(_BF)
            else:
                o_ref[...] = acc[...].astype(_BF)

    if slots:
        out_spec = pl.BlockSpec((1, K, tn), lambda j, i: (j, 0, 0))
        out_shape = jax.ShapeDtypeStruct((N // tn, K, tn), _BF)
    else:
        out_spec = pl.BlockSpec((K, tn), lambda j, i: (0, j))
        out_shape = jax.ShapeDtypeStruct((K, N), _BF)
    return pl.pallas_call(
        body, name=name, grid=(N // tn, L // tm),
        in_specs=[pl.BlockSpec((tm, K), lambda j, i: (i, 0)), pl.BlockSpec((tm, tn), lambda j, i: (i, j))],
        out_specs=out_spec, out_shape=out_shape,
        scratch_shapes=[pltpu.VMEM((K, tn), _F32)],
        compiler_params=_params("parallel", "arbitrary"),
    )(a, b)


def _mixout_bwd(dmix, w_out, w_glu, glu, s, o, gate, ggn, tm):
    L = dmix.shape[0]

    def body(dmix_ref, wo_ref, wg_ref, glu_ref, s_ref, o_ref, gate_ref, ggn_ref,
             dglu_ref, ds_ref, dgate_ref, do_ref, dggn_ref):
        @pl.when(pl.program_id(0) == 0)
        def _():
            dggn_ref[...] = jnp.zeros_like(dggn_ref)

        dcat = _dot_nt(dmix_ref[...], wo_ref[...])
        dy_ret, dy_ssm = dcat[:, :RET_W], dcat[:, RET_W:]
        glu = glu_ref[...]
        ga, sg = glu[:, :SSM_W], _sigmoid(glu[:, SSM_W:])
        dga = (dy_ssm * sg).astype(_BF)
        dgb = (dy_ssm * ga * sg * (1.0 - sg)).astype(_BF)
        dglu_ref[:, :SSM_W] = dga
        dglu_ref[:, SSM_W:] = dgb
        dys = _dot_nt(dga, wg_ref[:, :SSM_W]) + _dot_nt(dgb, wg_ref[:, SSM_W:])
        ds_ref[...] = dys * _gelu_grad(s_ref[...])
        gt = gate_ref[...]
        sgt = _sigmoid(gt)
        ggn = ggn_ref[...]
        for hh in range(N_HEAD):
            cols = slice(hh * HEAD_D, (hh + 1) * HEAD_D)
            ov = o_ref[:, cols]
            dlt = ov - jnp.mean(ov, axis=-1, keepdims=True)
            rstd = lax.rsqrt(jnp.mean(dlt * dlt, axis=-1, keepdims=True) + NORM_EPS)
            on = dlt * rstd
            dyr = dy_ret[:, cols] * (gt[:, cols] * sgt[:, cols])
            dgate_ref[:, cols] = dy_ret[:, cols] * (on * ggn[:, cols]) * (sgt[:, cols] * (1.0 + gt[:, cols] * (1.0 - sgt[:, cols])))
            dggn_ref[:, cols] += jnp.sum(dyr * on, axis=0, keepdims=True)
            don = dyr * ggn[:, cols]
            do = rstd * (don - jnp.mean(don, axis=-1, keepdims=True) - on * jnp.mean(don * on, axis=-1, keepdims=True))
            do_ref[:, cols] = do.astype(_BF)

    return pl.pallas_call(
        body, name="mixout_bwd", grid=(L // tm,),
        in_specs=[_row_spec(tm, D_MODEL), _weight_spec((D_MODEL, D_MODEL)), _weight_spec((SSM_W, 2 * SSM_W)),
                  _row_spec(tm, 2 * SSM_W), _row_spec(tm, SSM_W), _row_spec(tm, RET_W), _row_spec(tm, RET_W),
                  _full_spec((1, RET_W))],
        out_specs=[_row_spec(tm, 2 * SSM_W), _row_spec(tm, SSM_W), _row_spec(tm, RET_W), _row_spec(tm, RET_W),
                   _full_spec((1, RET_W))],
        out_shape=[jax.ShapeDtypeStruct((L, 2 * SSM_W), _BF), jax.ShapeDtypeStruct((L, SSM_W), _F32),
                   jax.ShapeDtypeStruct((L, RET_W), _F32), jax.ShapeDtypeStruct((L, RET_W), _BF),
                   jax.ShapeDtypeStruct((1, RET_W), _F32)],
        compiler_params=_params("arbitrary"),
    )(dmix, w_out, w_glu, glu, s, o, gate, ggn)


def _s5_bwd(u, ds, xs, ent, bmat, cmat, tab_r, pw_r, d_skip, tb):
    L = u.shape[0]
    nt = L // tb
    seg = tb // SUBLANES
    G = KB_PER_STEP
    rcol = pl.BlockSpec((tb, G * LANES), lambda kb, t: (nt - 1 - t, kb))
    sp = _s5_specs(seg, time=lambda t: nt - 1 - t)
    aspec = pl.BlockSpec((G, SUBLANES, 2 * KB_STATES), lambda kb, t: (kb, 0, 0))

    def body(u_ref, ds_ref, x_ref, ent_ref, b_ref, c_ref, tr_ref, pr_ref, d_ref,
             du_ref, db_ref, dc_ref, da_ref, dd_ref, up_scr, dp_scr, g_scr, lc_scr):
        @pl.when(pl.program_id(1) == 0)
        def _():
            lc_scr[...] = jnp.zeros_like(lc_scr)
            db_ref[...] = jnp.zeros_like(db_ref)
            dc_ref[...] = jnp.zeros_like(dc_ref)
            da_ref[...] = jnp.zeros_like(da_ref)
            dd_ref[...] = jnp.zeros_like(dd_ref)

        _rows_to_segments(up_scr, u_ref, seg)
        _rows_to_segments(dp_scr, ds_ref, seg)
        for g in range(G):
            g_scr[g] = _dot_nt(dp_scr[g].astype(_BF), c_ref[g]).reshape(seg, SUBLANES, 2 * KB_STATES)
        _scan_segments(g_scr, tr_ref, pr_ref, lc_scr, seg, reverse=True, fwd_ref=x_ref, fwd_entry_ref=ent_ref.at[:, 0],
                       da_ref=da_ref)
        for g in range(G):
            cols = slice(g * LANES, (g + 1) * LANES)
            uv, dsv = up_scr[g], dp_scr[g]
            ub, dsb = uv.astype(_BF), dsv.astype(_BF)
            lamb = g_scr[g].reshape(tb, 2 * KB_STATES).astype(_BF)
            db_ref[g] += _dot_tn(ub, lamb)
            dc_ref[g] += _dot_tn(dsb, x_ref[g].reshape(tb, 2 * KB_STATES).astype(_BF))
            dd_ref[:, cols] += jnp.sum(dsv * uv, axis=0, keepdims=True)
            up_scr[g] = _dot_nt(lamb, b_ref[g]) + d_ref[:, cols] * dsv
        _segments_to_rows(du_ref, up_scr, seg)

    return pl.pallas_call(
        body, name="s5_bwd", grid=(N_KB // G, nt),
        in_specs=[rcol, rcol, sp["x"], sp["ent"], sp["b"], sp["c"], sp["tab"], sp["pw"], sp["d"]],
        out_specs=[rcol, sp["b"], sp["b"], aspec, sp["d"]],
        out_shape=[jax.ShapeDtypeStruct((L, SSM_W), _F32),
                   jax.ShapeDtypeStruct((N_KB, LANES, 2 * KB_STATES), _F32),
                   jax.ShapeDtypeStruct((N_KB, LANES, 2 * KB_STATES), _F32),
                   jax.ShapeDtypeStruct((N_KB, SUBLANES, 2 * KB_STATES), _F32),
                   jax.ShapeDtypeStruct((1, SSM_W), _F32)],
        scratch_shapes=[pltpu.VMEM((G, tb, LANES), _F32)] * 2
        + [pltpu.VMEM((G, seg, SUBLANES, 2 * KB_STATES), _F32), pltpu.VMEM((G, SUBLANES, 2 * KB_STATES), _F32)],
        compiler_params=_params("parallel", "arbitrary"),
    )(u, ds, xs, ent, bmat, cmat, tab_r, pw_r, d_skip)


def _retention_bwd(q, k, v, do, r_prev, consts, cosf, sinf):
    L = q.shape[0]
    nc = L // CHUNK
    cps = math.gcd(RET_STEP_CHUNKS, nc)
    nb = nc // cps
    blk = pl.BlockSpec((cps * CHUNK, RET_W), lambda n: (nb - 1 - n, 0))
    rope_blk = pl.BlockSpec((cps * CHUNK, HEAD_D), lambda n: (nb - 1 - n, 0))

    def body(q_ref, k_ref, v_ref, do_ref, rp_ref, dm_ref, xi_ref, zeta_ref, gc_ref, cos_ref, sin_ref,
             dq_ref, dk_ref, dv_ref, g_scr):
        @pl.when(pl.program_id(0) == 0)
        def _():
            g_scr[...] = jnp.zeros_like(g_scr)

        for hh in range(N_HEAD):
            cols = slice(hh * HEAD_D, (hh + 1) * HEAD_D)
            dm, zeta = dm_ref[hh], zeta_ref[hh]
            gst = g_scr[hh]
            for c in reversed(range(cps)):
                rows = slice(c * CHUNK, (c + 1) * CHUNK)
                qv, kv, vv, dov = q_ref[rows, cols], k_ref[rows, cols], v_ref[rows, cols], do_ref[rows, cols]
                rb = rp_ref[hh, c].astype(_BF)
                gb = gst.astype(_BF)
                sb = (_dot_nt(qv, kv) * dm).astype(_BF)
                dab = (_dot_nt(dov, vv) * dm).astype(_BF)
                dox = (dov.astype(_F32) * xi_ref[hh]).astype(_BF)
                vz = (vv.astype(_F32) * zeta).astype(_BF)
                dq = _dot(dab, kv) + _dot_nt(dox, rb)
                dk = _dot_tn(dab, qv) + _dot_nt(vz, gb)
                dv = _dot_tn(sb, dov) + _dot(kv, gb) * zeta
                gst = gc_ref[hh, 0:1, :] * gst + _dot_tn(qv, dox)
                cs, sn = cos_ref[rows, :], sin_ref[rows, :]
                dq_ref[rows, cols] = _rope_t(dq, cs, sn).astype(_BF)
                dk_ref[rows, cols] = (_rope_t(dk, cs, sn) * (HEAD_D ** -0.5)).astype(_BF)
                dv_ref[rows, cols] = dv.astype(_BF)
            g_scr[hh] = gst

    return pl.pallas_call(
        body, name="retention_bwd", grid=(nb,),
        in_specs=[blk, blk, blk, blk, pl.BlockSpec((N_HEAD, cps, HEAD_D, HEAD_D), lambda n: (0, nb - 1 - n, 0, 0))]
        + _head_specs() + [rope_blk, rope_blk],
        out_specs=[blk, blk, blk],
        out_shape=[jax.ShapeDtypeStruct((L, RET_W), _BF)] * 3,
        scratch_shapes=[pltpu.VMEM((N_HEAD, HEAD_D, HEAD_D), _F32)],
        compiler_params=_params("arbitrary"),
    )(q, k, v, do, r_prev, *consts, cosf, sinf)


def _inproj_bwd(pieces, w_in_t, x, dx2, g1, tm):
    L = x.shape[0]

    def body(p0, p1, p2, p3, p4, w_ref, x_ref, dx2_ref, g_ref, dx_ref, dg_ref):
        @pl.when(pl.program_id(0) == 0)
        def _():
            dg_ref[...] = jnp.zeros_like(dg_ref)

        for rows in _row_chunks(tm):
            dh = None
            for j, p in enumerate((p0, p1, p2, p3, p4)):
                part = _dot(p[rows, :].astype(_BF), w_ref[j * RET_W:(j + 1) * RET_W, :])
                dh = part if dh is None else dh + part
            dz, dgr = _rms_bwd(x_ref[rows, :], g_ref[...], dh)
            dx_ref[rows, :] = dx2_ref[rows, :] + dz
            dg_ref[...] += jnp.sum(dgr, axis=0, keepdims=True)

    return pl.pallas_call(
        body, name="inproj_bwd", grid=(L // tm,),
        in_specs=[_row_spec(tm, RET_W)] * 5 + [_weight_spec((IN_COLS, D_MODEL)), _row_spec(tm, D_MODEL),
                                                 _row_spec(tm, D_MODEL), _full_spec((1, D_MODEL))],
        out_specs=[_row_spec(tm, D_MODEL), _full_spec((1, D_MODEL))],
        out_shape=[jax.ShapeDtypeStruct((L, D_MODEL), _F32), jax.ShapeDtypeStruct((1, D_MODEL), _F32)],
        compiler_params=_params("arbitrary"),
    )(*pieces, w_in_t, x, dx2, g1)


def _sum_adamw(parts, w, m, v, tr, name):
    _, R, Cc = parts.shape

    def body(p_ref, w_ref, m_ref, v_ref, g_ref, d_ref, nm_ref, nv_ref):
        gv = p_ref[0].astype(_F32)
        for s in range(1, N_DEV):
            gv = gv + p_ref[s].astype(_F32)
        g_ref[...] = gv
        nm = ADAM_B1 * m_ref[...] + (1.0 - ADAM_B1) * gv
        nv = ADAM_B2 * v_ref[...] + (1.0 - ADAM_B2) * (gv * gv)
        m_hat = nm / (1.0 - ADAM_B1 ** ADAM_STEP)
        v_hat = nv / (1.0 - ADAM_B2 ** ADAM_STEP)
        d_ref[...] = -ADAM_LR * (m_hat / (jnp.sqrt(v_hat) + ADAM_EPS) + ADAM_WD * w_ref[...])
        nm_ref[...] = nm
        nv_ref[...] = nv

    spec = _row_spec(tr, Cc)
    return pl.pallas_call(
        body, name=name, grid=(R // tr,),
        in_specs=[pl.BlockSpec((N_DEV, tr, Cc), lambda i: (0, i, 0))] + [spec] * 3, out_specs=[spec] * 4,
        out_shape=[jax.ShapeDtypeStruct((R, Cc), _F32)] * 4,
        compiler_params=_params("parallel"),
    )(parts, w, m, v)


def _my_place():
    return lax.axis_index("x"), lax.axis_index("y"), lax.axis_index("c")


def _all_gather(blocks):
    n = len(blocks)

    def body(*refs):
        x_refs, out_refs, done_ref = refs[:n], refs[n:2 * n], refs[2 * n]
        send_sems, recv_sems, local_sems = refs[2 * n + 1:]
        done_ref[...] = jnp.zeros_like(done_ref)
        x, y, c = _my_place()
        me, sibling = (x, y, c), (x, y, 1 - c)
        chips = [(1 - x, y), (x, 1 - y), (1 - x, 1 - y)]

        def slot(a, px, py, pc):
            return out_refs[a].at[4 * px + 2 * py + pc]

        def copy(a, k, blk, to, own=False):
            return pltpu.make_async_remote_copy(
                src_ref=x_refs[a] if own else slot(a, *blk), dst_ref=slot(a, *blk),
                send_sem=send_sems.at[a, k], recv_sem=recv_sems.at[a, k], device_id=to, device_id_type=MESH)

        mine = [pltpu.make_async_copy(x_refs[a], slot(a, *me), local_sems.at[a]) for a in range(n)]
        for cp in mine:
            cp.start()
        first = []
        for a in range(n):
            first.append(copy(a, 0, me, sibling, own=True))
            first += [copy(a, 1 + j, me, (*chip, c), own=True) for j, chip in enumerate(chips)]
        for cp in first:
            cp.start()
        passed = []
        for j, chip in enumerate(chips):
            for a in range(n):
                copy(a, 1 + j, (*chip, c), me).wait_recv()
                fwd = copy(a, 4 + j, (*chip, c), sibling)
                fwd.start()
                passed.append(fwd)
        for a in range(n):
            copy(a, 0, sibling, me).wait_recv()
            for j, chip in enumerate(chips):
                copy(a, 4 + j, (*chip, 1 - c), me).wait_recv()
        for cp in first + passed:
            cp.wait_send()
        for cp in mine:
            cp.wait()

    any_spec = pl.BlockSpec(memory_space=pl.ANY)
    outs = pl.pallas_call(
        body, name="weights_all_gather",
        in_specs=[any_spec] * n, out_specs=[any_spec] * n + [pl.BlockSpec(memory_space=pltpu.VMEM)],
        out_shape=[jax.ShapeDtypeStruct((N_DEV,) + b.shape, b.dtype) for b in blocks]
        + [jax.ShapeDtypeStruct((SUBLANES, LANES), _F32)],
        scratch_shapes=[pltpu.SemaphoreType.DMA((n, 7)), pltpu.SemaphoreType.DMA((n, 7)), pltpu.SemaphoreType.DMA((n,))],
    )(*blocks)
    return outs[:n], outs[n]


def _exchange(bigs, small):
    n = len(bigs)
    r = small.shape[0]

    def body(*refs):
        in_refs, out_refs = refs[:n + 1], refs[n + 1:2 * n + 2]
        send_sems, recv_sems, local_sems = refs[2 * n + 2:]
        x, y, c = _my_place()
        me = 4 * x + 2 * y + c
        own = [pltpu.make_async_copy(in_refs[a].at[me], out_refs[a].at[me], local_sems.at[a]) for a in range(n)]
        own.append(pltpu.make_async_copy(in_refs[n], out_refs[n].at[me], local_sems.at[n]))
        for cp in own:
            cp.start()
        copies = []
        for kk in range(1, N_DEV):
            px, py, pc = x ^ (kk >> 2), y ^ ((kk >> 1) & 1), c ^ (kk & 1)
            peer = 4 * px + 2 * py + pc
            for a in range(n + 1):
                src = in_refs[a].at[peer] if a < n else in_refs[a]
                copies.append(pltpu.make_async_remote_copy(
                    src_ref=src, dst_ref=out_refs[a].at[me],
                    send_sem=send_sems.at[a, kk - 1], recv_sem=recv_sems.at[a, kk - 1],
                    device_id=(px, py, pc), device_id_type=MESH))
        for cp in copies:
            cp.start()
        for cp in copies:
            cp.wait_recv()
        for cp in copies:
            cp.wait_send()
        for cp in own:
            cp.wait()

    any_spec = pl.BlockSpec(memory_space=pl.ANY)
    outs = pl.pallas_call(
        body, name="grad_exchange",
        in_specs=[any_spec] * (n + 1), out_specs=[any_spec] * (n + 1),
        out_shape=[jax.ShapeDtypeStruct(b.shape, b.dtype) for b in bigs]
        + [jax.ShapeDtypeStruct((N_DEV, r, LANES), small.dtype)],
        scratch_shapes=[pltpu.SemaphoreType.DMA((n + 1, 7)), pltpu.SemaphoreType.DMA((n + 1, 7)),
                        pltpu.SemaphoreType.DMA((n + 1,))],
    )(*bigs, small)
    return outs[:n], outs[n]


HBM_SPEC = pl.BlockSpec(memory_space=pltpu.HBM)
SEM_SPEC = pl.BlockSpec(memory_space=pltpu.SEMAPHORE)
DATAFLOW = pltpu.SideEffectType.DATAFLOW_SIDE_EFFECTING


def _my_index():
    x, y, c = _my_place()
    return 4 * x + 2 * y + c


def _landing(own_block):
    zone = lax.empty((N_DEV,) + own_block.shape, own_block.dtype)
    return lax.dynamic_update_index_in_dim(zone, own_block, _my_index(), 0)


def _split_copies(src_refs, land_refs, send_sems, recv_sems, gather):
    x, y, c = _my_place()
    me = 4 * x + 2 * y + c
    copies = []
    for kk in range(1, N_DEV):
        px, py, pc = x ^ (kk >> 2), y ^ ((kk >> 1) & 1), c ^ (kk & 1)
        peer = 4 * px + 2 * py + pc
        for a, (src, land) in enumerate(zip(src_refs, land_refs)):
            copies.append(pltpu.make_async_remote_copy(
                src_ref=src if gather else src.at[peer], dst_ref=land.at[me],
                send_sem=send_sems.at[a * 7 + kk - 1], recv_sem=recv_sems.at[a * 7 + kk - 1],
                device_id=(px, py, pc), device_id_type=MESH))
    return copies


def _split_start(srcs, lands, gather, name):
    n = len(srcs)

    def body(*refs):
        src_refs, land_refs = refs[:n], refs[n:2 * n]
        send_sems, recv_sems = refs[2 * n], refs[2 * n + 1]
        token = refs[-1]
        for cp in _split_copies(src_refs, land_refs, send_sems, recv_sems, gather):
            cp.start()
        token[...] = jnp.zeros_like(token)

    outs = pl.pallas_call(
        body, name=name,
        out_shape=(pltpu.SemaphoreType.DMA((7 * n,)), pltpu.SemaphoreType.DMA((7 * n,)),
                   *[pltpu.HBM(t.shape, t.dtype) for t in srcs], *[pltpu.HBM(t.shape, t.dtype) for t in lands],
                   jax.ShapeDtypeStruct((SUBLANES, LANES), _F32)),
        in_specs=[HBM_SPEC] * (2 * n),
        out_specs=(SEM_SPEC, SEM_SPEC, *[HBM_SPEC] * (2 * n), pl.BlockSpec(memory_space=pltpu.VMEM)),
        input_output_aliases={i: 2 + i for i in range(2 * n)},
        compiler_params=pltpu.CompilerParams(has_side_effects=DATAFLOW),
    )(*[pltpu.with_memory_space_constraint(t, pltpu.HBM) for t in list(srcs) + list(lands)])
    return outs[0], outs[1], outs[2:2 + n], outs[2 + n:2 + 2 * n], outs[-1]


def _split_wait(send_sems, recv_sems, srcs, lands, after, gather, name):
    n = len(srcs)

    def body(*refs):
        src_refs, land_refs = refs[:n], refs[n:2 * n]
        send_s, recv_s = refs[2 * n], refs[2 * n + 1]
        for cp in _split_copies(src_refs, land_refs, send_s, recv_s, gather):
            cp.wait_send()
            cp.wait_recv()

    outs = pl.pallas_call(
        body, name=name,
        out_shape=tuple(pltpu.HBM(t.shape, t.dtype) for t in list(srcs) + list(lands)),
        in_specs=[HBM_SPEC] * (2 * n) + [SEM_SPEC, SEM_SPEC, pl.BlockSpec(memory_space=pl.ANY)],
        out_specs=tuple([HBM_SPEC] * (2 * n)),
        input_output_aliases={i: i for i in range(2 * n)},
        compiler_params=pltpu.CompilerParams(has_side_effects=DATAFLOW),
    )(*srcs, *lands, send_sems, recv_sems, after)
    return outs[n:]


def _discretize(lam_re, lam_im, log_dt, b_re, b_im):
    lr = jnp.minimum(lam_re, -1e-4)
    li = lam_im
    dt = jnp.exp(log_dt)[:, None]
    er = jnp.exp(lr * dt)
    ar, ai = er * jnp.cos(li * dt), er * jnp.sin(li * dt)
    den = lr * lr + li * li
    cr = ((ar - 1.0) * lr + ai * li) / den
    ci = (ai * lr - (ar - 1.0) * li) / den
    bbr = cr[:, :, None] * b_re - ci[:, :, None] * b_im
    bbi = cr[:, :, None] * b_im + ci[:, :, None] * b_re
    return ar, ai, bbr, bbi


def _cmul(ar, ai, br, bi):
    return ar * br - ai * bi, ar * bi + ai * br


def _cpowers(ar, ai, n):
    pr, pi = ar[None], ai[None]
    while pr.shape[0] < n:
        nr, ni = _cmul(pr, pi, pr[-1][None], pi[-1][None])
        pr, pi = jnp.concatenate([pr, nr]), jnp.concatenate([pi, ni])
    return pr[:n], pi[:n]


def _scan_tables(ar, ai, seg, reverse):
    if reverse:
        ai = -ai
    ar, ai = ar.reshape(N_KB, KB_STATES), ai.reshape(N_KB, KB_STATES)
    pr, pi = _cpowers(ar, ai, seg)
    a1 = (pr[-1], pi[-1])
    a2 = _cmul(*a1, *a1)
    a4 = _cmul(*a2, *a2)
    row = jnp.arange(SUBLANES)[None, :, None]
    wide = lambda t: jnp.broadcast_to(t[:, None, :], (N_KB, SUBLANES, KB_STATES))
    tabs = [wide(ar), wide(ai)]
    for dist, (qr, qi) in ((1, a1), (2, a2), (4, a4)):
        keep = (row < SUBLANES - dist) if reverse else (row >= dist)
        tabs += [jnp.where(keep, wide(qr), 0.0), jnp.where(keep, wide(qi), 0.0)]
    tabs += [wide(a1[0]), wide(a1[1])]
    if reverse:
        pr, pi = pr[::-1], pi[::-1]
    pw = jnp.transpose(jnp.concatenate([pr, pi], axis=-1), (1, 0, 2))[:, :, None, :]
    return jnp.stack(tabs, axis=1).astype(_F32), pw.astype(_F32)


def _block_diag_in(br, bi):
    eye = jnp.eye(GROUPS_PER_KB, dtype=_F32)
    one = lambda t: jnp.einsum("kgpc,gh->kgchp", t.reshape(N_KB, GROUPS_PER_KB, N_STATE, SSM_GC), eye).reshape(
        N_KB, LANES, KB_STATES)
    return jnp.concatenate([one(br), one(bi)], axis=-1)


def _block_diag_in_t(dmat):
    d6 = dmat.reshape(N_KB, GROUPS_PER_KB, SSM_GC, 2, GROUPS_PER_KB, N_STATE)
    eye = jnp.eye(GROUPS_PER_KB, dtype=_F32)
    both = jnp.einsum("kgcrhp,gh->rkgpc", d6, eye).reshape(2, N_GROUP, N_STATE, SSM_GC)
    return both[0], both[1]


def _block_diag_out(c_re, c_im):
    eye = jnp.eye(GROUPS_PER_KB, dtype=_F32)
    one = lambda t: jnp.einsum("kgcp,gh->khpgc", t.reshape(N_KB, GROUPS_PER_KB, SSM_GC, N_STATE), eye).reshape(
        N_KB, KB_STATES, LANES)
    return jnp.concatenate([one(c_re), -one(c_im)], axis=1)


def _block_diag_out_t(dmat_t):
    d6 = dmat_t.reshape(N_KB, GROUPS_PER_KB, SSM_GC, 2, GROUPS_PER_KB, N_STATE)
    eye = jnp.eye(GROUPS_PER_KB, dtype=_F32)
    both = jnp.einsum("kgcrhp,gh->rkgcp", d6, eye).reshape(2, N_GROUP, SSM_GC, N_STATE)
    return both[0], -both[1]


SMALL_NAMES = ("norm_mix_pre", "norm_mix_post", "ret_gn_gain", "ssm_lambda_re", "ssm_lambda_im", "ssm_log_dt",
               "ssm_b_re", "ssm_b_im", "ssm_c_re", "ssm_c_im", "ssm_d", "norm_mlp_pre", "norm_mlp_post")


def _local_grads(x, tgt, small, weights, emit, emit_small, tm, tk, tb, zero=0.0):
    L = x.shape[0]
    g1, g2, ggn = small["norm_mix_pre"], small["norm_mix_post"], small["ret_gn_gain"]
    g3, g4, d_skip = small["norm_mlp_pre"], small["norm_mlp_post"], small["ssm_d"]

    rope = _rope_tables(L)
    consts = _ret_consts()

    disc_in = (small["ssm_lambda_re"][0], small["ssm_lambda_im"][0], small["ssm_log_dt"][0] + zero,
               small["ssm_b_re"][0], small["ssm_b_im"][0])
    (ar, ai, bbr, bbi), disc_vjp = jax.vjp(_discretize, *disc_in)
    bmat = _block_diag_in(bbr, bbi).astype(_BF)
    cmat = _block_diag_out(small["ssm_c_re"][0], small["ssm_c_im"][0]).astype(_BF)
    seg = tb // SUBLANES
    tab_f, pw_f = _scan_tables(ar, ai, seg, False)
    tab_r, pw_r = _scan_tables(ar, ai, seg, True)

    (w_in_t,) = weights("in", pw_r)
    h1, q, k, v, gate, u, cosf, sinf = _inproj_fwd(x, g1, w_in_t, rope, tm)
    o, y_ret, r_prev = _retention_fwd(q, k, v, gate, ggn, consts)
    s, xs, ent = _s5_fwd(u, bmat, cmat, tab_f, pw_f, d_skip, tb)
    w_glu, w_out = weights("mix", s)
    ys, glu, cat, mix, x2 = _mixout_fwd(s, y_ret, x, w_glu, w_out, g2, min(2 * tm, L))
    w_ff1, w_ff2 = weights("mlp", x2)
    h3, f1 = _ff1_fwd(x2, g3, w_ff1, tm)
    dy, dm, dg4, sq = _ff2_loss(f1, x2, tgt, g4, w_ff2, min(2 * tm, L))

    df1, dw_ff2 = _ff2_bwd(dm, f1, w_ff2, min(1024, L), 1024)
    dx2, dmix, dg3, dg2 = _ff1_bwd(df1, w_ff1, x2, mix, dy, g3, g2, min(2 * tm, L))
    dw_ff1 = _matmul_tn(h3, df1, tk, FF1_COLS, "dw_ff1", slots=True)
    zero = emit({"w_ff1": dw_ff1, "w_ff2": dw_ff2})
    dglu, ds, dgate, do, dggn = _mixout_bwd(dmix, w_out, w_glu, glu, s, o, gate, ggn if zero is None else ggn + zero, tm)
    dw_out = _matmul_tn(cat, dmix, tk, 1024, "dw_out")
    dw_glu = _matmul_tn(ys, dglu, tk, 1024, "dw_glu")
    zero = emit({"w_glu": dw_glu, "w_out": dw_out})
    du, dbmat, dcmat, da8, dd = _s5_bwd(u, ds, xs, ent, bmat, cmat, tab_r, pw_r,
                                        d_skip if zero is None else d_skip + zero, tb)
    dq, dk, dv = _retention_bwd(q, k, v, do, r_prev, consts, cosf, sinf)
    pieces = (dq, dk, dv, dgate, du)
    dw_in_t = jnp.concatenate([_matmul_tn(p, h1, tk, D_MODEL, "dw_in_%d" % j) for j, p in enumerate(pieces)], axis=0)
    zero = emit({"w_in": dw_in_t})

    da = jnp.sum(da8, axis=1)
    dar = da[:, :KB_STATES].reshape(N_GROUP, N_STATE)
    dai = da[:, KB_STATES:].reshape(N_GROUP, N_STATE)
    dbr, dbi = _block_diag_in_t(dbmat)
    dlre, dlim, dldt, dbre, dbim = disc_vjp((dar, dai, dbr, dbi))
    dcre, dcim = _block_diag_out_t(dcmat)

    zero2 = emit_small({
        "norm_mix_post": dg2, "ret_gn_gain": dggn,
        "ssm_lambda_re": dlre[None], "ssm_lambda_im": dlim[None], "ssm_log_dt": dldt[None],
        "ssm_b_re": dbre[None], "ssm_b_im": dbim[None], "ssm_c_re": dcre[None], "ssm_c_im": dcim[None],
        "ssm_d": dd, "norm_mlp_pre": dg3, "norm_mlp_post": dg4,
    }, sq)
    for z in (zero, zero2):
        g1 = g1 if z is None else g1 + z
    gx, dg1 = _inproj_bwd(pieces, w_in_t, x, dx2, g1, min(2 * tm, L))
    return gx, dg1


BIG_SHAPES = {"w_in": (D_MODEL, IN_COLS // N_DEV), "w_glu": (SSM_W, 2 * SSM_W // N_DEV), "w_out": (D_MODEL // N_DEV, D_MODEL),
              "w_ff1": (D_MODEL, FF1_COLS), "w_ff2": (D_FF // N_DEV, D_MODEL)}
BIG_NAMES = ("w_in", "w_glu", "w_out", "w_ff1", "w_ff2")


def _cols_from_slots(g):
    return jnp.transpose(g, (1, 0, 2)).reshape(g.shape[1], N_DEV * g.shape[2])


def _cols_to_slots(dw):
    r, cols = dw.shape
    return jnp.transpose(dw.reshape(r, N_DEV, cols // N_DEV), (1, 0, 2))


WEIGHT_GROUPS = {"in": ("w_in",), "mix": ("w_glu", "w_out"), "mlp": ("w_ff1", "w_ff2")}


def _weight_from_slots(name, g):
    if name == "w_glu":
        return _cols_from_slots(g)
    if name == "w_ff1":
        return g
    return g.reshape(N_DEV * g.shape[1], g.shape[2])


def _grad_slots(name, dw):
    if name == "w_glu":
        return _cols_to_slots(dw)
    if name == "w_ff1":
        return dw
    if name == "w_in":
        return dw.reshape(N_DEV, BIG_SHAPES[name][1], BIG_SHAPES[name][0])
    return dw.reshape((N_DEV,) + BIG_SHAPES[name])


PIECE_ROWS = 8


def _small_layout(shapes):
    off, rows = {}, 0
    for n in SMALL_NAMES:
        off[n] = rows
        rows += -(-math.prod(shapes[n]) // (PIECE_ROWS * LANES)) * PIECE_ROWS
    return off, rows, rows + PIECE_ROWS


def _pack_small(vals, shapes, last=None):
    parts = []
    for n in SMALL_NAMES:
        flat = vals[n].reshape(-1).astype(_F32)
        pad = -flat.shape[0] % (PIECE_ROWS * LANES)
        if pad:
            flat = jnp.concatenate([flat, jnp.zeros((pad,), _F32)])
        parts.append(flat.reshape(-1, LANES))
    parts.append(jnp.zeros((PIECE_ROWS, LANES), _F32) if last is None else last)
    return jnp.concatenate(parts, axis=0)


def _unpack_small(buf, shapes):
    off, _, _ = _small_layout(shapes)
    out = {}
    for n in SMALL_NAMES:
        size = math.prod(shapes[n])
        rows = -(-size // LANES)
        out[n] = buf[off[n]:off[n] + rows].reshape(-1)[:size].reshape(shapes[n])
    return out


WEIGHT_NAMES = ('norm_mix_pre', 'norm_mix_post', 'w_in', 'ret_gn_gain', 'ssm_lambda_re', 'ssm_lambda_im', 'ssm_log_dt',
                'ssm_b_re', 'ssm_b_im', 'ssm_c_re', 'ssm_c_im', 'ssm_d', 'w_glu', 'w_out', 'norm_mlp_pre',
                'norm_mlp_post', 'w_ff1', 'w_ff2')


def kernel(x, norm_mix_pre, norm_mix_post, w_in, ret_gn_gain, ssm_lambda_re, ssm_lambda_im, ssm_log_dt, ssm_b_re, ssm_b_im, ssm_c_re, ssm_c_im, ssm_d, w_glu, w_out, norm_mlp_pre, norm_mlp_post, w_ff1, w_ff2, loss_target, m_norm_mix_pre, m_norm_mix_post, m_w_in, m_ret_gn_gain, m_ssm_lambda_re, m_ssm_lambda_im, m_ssm_log_dt, m_ssm_b_re, m_ssm_b_im, m_ssm_c_re, m_ssm_c_im, m_ssm_d, m_w_glu, m_w_out, m_norm_mlp_pre, m_norm_mlp_post, m_w_ff1, m_w_ff2, v_norm_mix_pre, v_norm_mix_post, v_w_in, v_ret_gn_gain, v_ssm_lambda_re, v_ssm_lambda_im, v_ssm_log_dt, v_ssm_b_re, v_ssm_b_im, v_ssm_c_re, v_ssm_c_im, v_ssm_d, v_w_glu, v_w_out, v_norm_mlp_pre, v_norm_mlp_post, v_w_ff1, v_w_ff2):
    args = dict(locals())
    w = {n: args[n] for n in WEIGHT_NAMES}
    m = {n: args["m_" + n] for n in WEIGHT_NAMES}
    v = {n: args["v_" + n] for n in WEIGHT_NAMES}
    L = x.shape[1]
    tm = min(256, L)
    tk = min(2048, L)
    tb = min(512, L)

    gathers, zero = {}, jnp.zeros((), _F32)
    for group, names in WEIGHT_GROUPS.items():
        blocks = [(w[n][0].T if n == "w_in" else w[n][0]).astype(_BF) for n in names]
        blocks[0] = blocks[0] + zero.astype(_BF)
        gathers[group] = _split_start(blocks, [_landing(b) for b in blocks], True, "weights_start_" + group)
        zero = gathers[group][4][0, 0]

    def weights(group, after):
        landed = _split_wait(*gathers[group][:4], after, True, "weights_wait_" + group)
        return [_weight_from_slots(n, g) for n, g in zip(WEIGHT_GROUPS[group], landed)]

    in_flight = []

    def emit(dws):
        names = sorted(dws)
        srcs = [_grad_slots(n, dws[n]) for n in names]
        lands = [_landing(lax.dynamic_index_in_dim(t, _my_index(), 0, keepdims=False)) for t in srcs]
        started = _split_start(srcs, lands, False, "grads_start_" + "_".join(names))
        in_flight.append((names, started))
        return started[4][0, 0]

    shapes = {n: w[n].shape for n in SMALL_NAMES}
    first_piece = {SMALL_NAMES[0]: jnp.zeros(shapes[SMALL_NAMES[0]], _F32)}
    small_flight = []

    def emit_small(gs, sq):
        loss_rows = jnp.broadcast_to(0.5 / D_MODEL * jnp.sum(sq), (PIECE_ROWS, LANES)).astype(_F32)
        buf = _pack_small({**first_piece, **gs}, shapes, loss_rows)
        small_flight.append(_split_start([buf], [_landing(buf)], True, "small_grads_start"))
        return small_flight[0][4][0, 0]

    small_w = {n: w[n] for n in SMALL_NAMES}
    gx, dg1 = _local_grads(x[0], loss_target[0], small_w, weights, emit, emit_small, tm, tk, tb, zero=zero)
    last_buf = dg1.reshape(PIECE_ROWS, LANES)
    last_started = _split_start([last_buf], [_landing(last_buf)], True, "last_grad_start")

    grads, delta, new_m, new_v = {}, {}, {}, {}
    after = last_started[4]
    for names, started in in_flight:
        landed = _split_wait(*started[:4], after, False, "grads_wait_" + "_".join(names))
        for n, parts in zip(names, landed):
            flip = (lambda t: t.T) if n == "w_in" else (lambda t: t)
            res = _sum_adamw(parts, flip(w[n][0]), flip(m[n][0]), flip(v[n][0]), math.gcd(256, parts.shape[1]), "adamw_" + n)
            grads[n], delta[n], new_m[n], new_v[n] = (flip(t)[None] for t in res)
        after = res[1]
    small_parts = _split_wait(*small_flight[0][:4], after, True, "small_grads_wait")[0]
    last_parts = _split_wait(*last_started[:4], small_parts, True, "last_grad_wait")[0]
    small_parts = lax.dynamic_update_slice(small_parts, last_parts, (0, 0, 0))
    sw, sm, sv = _pack_small(w, shapes), _pack_small(m, shapes), _pack_small(v, shapes)
    res = _sum_adamw(small_parts, sw, sm, sv, sw.shape[0], "adamw_small")
    for dst, buf in zip((grads, delta, new_m, new_v), res):
        dst.update(_unpack_small(buf, shapes))
    _, loss_at, _ = _small_layout(shapes)
    loss = res[0][loss_at, 0]

    return (loss, gx[None], *[grads[n] for n in WEIGHT_NAMES], *[delta[n] for n in WEIGHT_NAMES],
            *[new_m[n] for n in WEIGHT_NAMES], *[new_v[n] for n in WEIGHT_NAMES])
```

```python
import math

import jax
import jax.numpy as jnp
from jax import lax
from jax.experimental import pallas as pl
from jax.experimental.pallas import tpu as pltpu

_BF = jnp.bfloat16
_F32 = jnp.float32

D_MODEL = 1024
RET_W = 512
N_HEAD = 4
HEAD_D = 128
CHUNK = 256
ROPE_CHUNK = 128
SSM_W = 512
SSM_GC = 16
N_GROUP = 32
N_STATE = 64
GROUPS_PER_KB = 8
N_KB = 4
KB_STATES = GROUPS_PER_KB * N_STATE
D_FF = 4096
IN_COLS = 2560
NORM_EPS = 1e-6
ROPE_BASE = 10000.0
N_DEV = 8

ADAM_LR = 0.001
ADAM_B1 = 0.9
ADAM_B2 = 0.999
ADAM_EPS = 1e-08
ADAM_WD = 0.01
ADAM_STEP = 10

SUBLANES = 8
LANES = 128
VMEM_LIMIT = 52 * 1024 * 1024
RET_STEP_CHUNKS = 2
KB_PER_STEP = 2
SCAN_UNROLL = True
FIX_UNROLL = 8

MESH = pl.DeviceIdType.MESH


def _params(*sem):
    return pltpu.CompilerParams(dimension_semantics=sem, vmem_limit_bytes=VMEM_LIMIT)


def _dot(a, b):
    return jnp.dot(a, b, preferred_element_type=_F32)


def _dot_nt(a, b):
    return lax.dot_general(a, b, (((1,), (1,)), ((), ())), preferred_element_type=_F32)


def _dot_tn(a, b):
    return lax.dot_general(a, b, (((0,), (0,)), ((), ())), preferred_element_type=_F32)


def _rms_r(z):
    return lax.rsqrt(jnp.mean(z * z, axis=-1, keepdims=True) + NORM_EPS)


def _rms_bwd(z, g, dn):
    r = _rms_r(z)
    t = dn * g
    dz = r * t - z * (r * r * r * jnp.mean(t * z, axis=-1, keepdims=True))
    return dz, dn * z * r


def _rope(t, cs, sn):
    return t * cs + pltpu.roll(t, HEAD_D // 2, 1) * sn


def _rope_t(t, cs, sn):
    return t * cs - pltpu.roll(t, HEAD_D // 2, 1) * sn


def _sigmoid(z):
    return 1.0 / (1.0 + jnp.exp(-z))


_GELU_C = math.sqrt(2.0 / math.pi)


def _gelu(z):
    return 0.5 * z * (1.0 + jnp.tanh(_GELU_C * (z + 0.044715 * z * z * z)))


def _gelu_grad(z):
    th = jnp.tanh(_GELU_C * (z + 0.044715 * z * z * z))
    return 0.5 * (1.0 + th) + 0.5 * z * (1.0 - th * th) * _GELU_C * (1.0 + 3 * 0.044715 * z * z)


ROW_CHUNK = 256


def _row_chunks(tm):
    return [pl.ds(i, min(ROW_CHUNK, tm)) for i in range(0, tm, ROW_CHUNK)]


def _row_spec(tm, n):
    return pl.BlockSpec((tm, n), lambda i: (i, 0))


def _full_spec(shape):
    nd = len(shape)
    return pl.BlockSpec(shape, lambda *_: (0,) * nd)


def _weight_spec(shape):
    nd = len(shape)
    return pl.BlockSpec(shape, lambda *_: (0,) * nd, pipeline_mode=pl.Buffered(1))


def _rope_tables(L):
    half = HEAD_D // 2
    inv_freq = ROPE_BASE ** (-jnp.arange(half, dtype=_F32) / half)
    twice = lambda t: jnp.concatenate([t, t], axis=-1)
    off = jnp.arange(ROPE_CHUNK, dtype=_F32)[:, None] * inv_freq[None, :]
    start = (ROPE_CHUNK * jnp.arange(L // ROPE_CHUNK, dtype=_F32))[:, None] * inv_freq[None, :]
    return (twice(jnp.cos(off)), twice(jnp.sin(off)),
            twice(jnp.cos(start))[:, None, :], twice(jnp.sin(start))[:, None, :])


def _prenorm(x, g, tm):
    L = x.shape[0]

    def body(x_ref, g_ref, h_ref):
        xv = x_ref[...]
        h_ref[...] = (xv * _rms_r(xv) * g_ref[...]).astype(_BF)

    return pl.pallas_call(
        body, name="prenorm", grid=(L // tm,),
        in_specs=[_row_spec(tm, D_MODEL), _full_spec((1, D_MODEL))], out_specs=_row_spec(tm, D_MODEL),
        out_shape=jax.ShapeDtypeStruct((L, D_MODEL), _BF),
        compiler_params=_params("parallel"),
    )(x, g)


def _inproj_fwd(h, w_in_t, rope, tm):
    L = h.shape[0]
    n_chunks = tm // ROPE_CHUNK

    def body(h_ref, w_ref, co_ref, so_ref, cs_ref, ss_ref, q_ref, k_ref, v_ref, gate_ref, u_ref, cos_ref, sin_ref):
        proj = _dot_nt(h_ref[...], w_ref[...])
        lane = lax.broadcasted_iota(jnp.int32, (ROPE_CHUNK, HEAD_D), 1)
        sign = jnp.where(lane < HEAD_D // 2, -1.0, 1.0)
        co, so = co_ref[...], so_ref[...]
        for c in range(n_chunks):
            chunk = pl.program_id(0) * n_chunks + c
            cst, sst = cs_ref[chunk], ss_ref[chunk]
            rows = slice(c * ROPE_CHUNK, (c + 1) * ROPE_CHUNK)
            cs = co * cst - so * sst
            sn = (so * cst + co * sst) * sign
            cos_ref[rows, :] = cs
            sin_ref[rows, :] = sn
            for hh in range(N_HEAD):
                lo = hh * HEAD_D
                q_ref[rows, lo:lo + HEAD_D] = _rope(proj[rows, lo:lo + HEAD_D], cs, sn).astype(_BF)
                kh = _rope(proj[rows, RET_W + lo:RET_W + lo + HEAD_D], cs, sn) * (HEAD_D ** -0.5)
                k_ref[rows, lo:lo + HEAD_D] = kh.astype(_BF)
        v_ref[...] = proj[:, 2 * RET_W:3 * RET_W].astype(_BF)
        gate_ref[...] = proj[:, 3 * RET_W:4 * RET_W]
        u_ref[...] = proj[:, 4 * RET_W:]

    nc = L // ROPE_CHUNK
    return pl.pallas_call(
        body, name="inproj_fwd", grid=(L // tm,),
        in_specs=[_row_spec(tm, D_MODEL), _weight_spec((IN_COLS, D_MODEL)),
                  _full_spec((ROPE_CHUNK, HEAD_D)), _full_spec((ROPE_CHUNK, HEAD_D)),
                  _full_spec((nc, 1, HEAD_D)), _full_spec((nc, 1, HEAD_D))],
        out_specs=[_row_spec(tm, RET_W)] * 5 + [_row_spec(tm, HEAD_D)] * 2,
        out_shape=[jax.ShapeDtypeStruct((L, RET_W), _BF)] * 3 + [jax.ShapeDtypeStruct((L, RET_W), _F32)] * 2
        + [jax.ShapeDtypeStruct((L, HEAD_D), _F32)] * 2,
        compiler_params=_params("parallel"),
    )(h, w_in_t, *rope)


def _ret_consts():
    lg = jnp.log(1.0 - jnp.exp(jnp.linspace(math.log(1.0 / 32), math.log(1.0 / 512), N_HEAD))).astype(_F32)
    idx = jnp.arange(CHUNK, dtype=_F32)
    diff = idx[:, None] - idx[None, :]
    decay = jnp.where(diff[None] >= 0, jnp.exp(jnp.maximum(diff, 0.0)[None] * lg[:, None, None]), 0.0)
    zeta = jnp.exp((CHUNK - 1 - idx)[None, :] * lg[:, None])
    xi = jnp.exp((idx + 1.0)[None, :] * lg[:, None])
    gc = jnp.exp(CHUNK * lg)
    wide = lambda t: jnp.broadcast_to(t[:, :, None], (N_HEAD, CHUNK, HEAD_D)).astype(_F32)
    gcw = jnp.broadcast_to(gc[:, None, None], (N_HEAD, SUBLANES, HEAD_D)).astype(_F32)
    return decay.astype(_F32), wide(xi), wide(zeta), gcw


def _head_specs():
    wide = _full_spec((N_HEAD, CHUNK, HEAD_D))
    return [_full_spec((N_HEAD, CHUNK, CHUNK)), wide, wide, _full_spec((N_HEAD, SUBLANES, HEAD_D))]


def _retention_fwd(q, k, v, gate, ggn, consts):
    L = q.shape[0]
    nc = L // CHUNK
    cps = math.gcd(RET_STEP_CHUNKS, nc)
    blk = pl.BlockSpec((cps * CHUNK, RET_W), lambda n: (n, 0))

    def body(q_ref, k_ref, v_ref, gate_ref, ggn_ref, dm_ref, xi_ref, zeta_ref, gc_ref,
             o_ref, y_ref, rp_ref, r_scr):
        @pl.when(pl.program_id(0) == 0)
        def _():
            r_scr[...] = jnp.zeros_like(r_scr)

        for hh in range(N_HEAD):
            cols = slice(hh * HEAD_D, (hh + 1) * HEAD_D)
            state = r_scr[hh]
            for c in range(cps):
                rows = slice(c * CHUNK, (c + 1) * CHUNK)
                qv, kv, vv = q_ref[rows, cols], k_ref[rows, cols], v_ref[rows, cols]
                s = _dot_nt(qv, kv) * dm_ref[hh]
                o = _dot(s.astype(_BF), vv) + _dot(qv, state.astype(_BF)) * xi_ref[hh]
                o_ref[rows, cols] = o
                rp_ref[hh, c] = state
                vz = (vv.astype(_F32) * zeta_ref[hh]).astype(_BF)
                state = gc_ref[hh, 0:1, :] * state + _dot_tn(kv, vz)
                dlt = o - jnp.mean(o, axis=-1, keepdims=True)
                on = dlt * lax.rsqrt(jnp.mean(dlt * dlt, axis=-1, keepdims=True) + NORM_EPS)
                gt = gate_ref[rows, cols]
                y_ref[rows, cols] = (gt * _sigmoid(gt) * (on * ggn_ref[:, cols])).astype(_BF)
            r_scr[hh] = state

    return pl.pallas_call(
        body, name="retention_fwd", grid=(nc // cps,),
        in_specs=[blk, blk, blk, blk, _full_spec((1, RET_W))] + _head_specs(),
        out_specs=[blk, blk, pl.BlockSpec((N_HEAD, cps, HEAD_D, HEAD_D), lambda n: (0, n, 0, 0))],
        out_shape=[jax.ShapeDtypeStruct((L, RET_W), _F32), jax.ShapeDtypeStruct((L, RET_W), _BF),
                   jax.ShapeDtypeStruct((N_HEAD, nc, HEAD_D, HEAD_D), _F32)],
        scratch_shapes=[pltpu.VMEM((N_HEAD, HEAD_D, HEAD_D), _F32)],
        compiler_params=_params("arbitrary"),
    )(q, k, v, gate, ggn, *consts)


def _rows_to_segments(dst_scr, src_ref, seg):
    for g in range(dst_scr.shape[0]):
        for j in range(SUBLANES):
            dst_scr[g, pl.ds(j, seg, stride=SUBLANES), :] = src_ref[pl.ds(j * seg, seg), g * LANES:(g + 1) * LANES]


def _segments_to_rows(dst_ref, src_scr, seg):
    for g in range(src_scr.shape[0]):
        for j in range(SUBLANES):
            dst_ref[pl.ds(j * seg, seg), g * LANES:(g + 1) * LANES] = src_scr[g, pl.ds(j, seg, stride=SUBLANES), :]


def _scan_segments(x_ref, tab_ref, pw_ref, carry_ref, seg, reverse, entry_ref=None, fwd_ref=None, fwd_entry_ref=None,
                   da_ref=None):
    G = x_ref.shape[0]
    W = KB_STATES
    re, im = pl.ds(0, W), pl.ds(W, W)
    row_id = lax.broadcasted_iota(jnp.int32, (SUBLANES, W), 0)
    edge_in = (row_id == SUBLANES - 1) if reverse else (row_id == 0)
    edge_out = 0 if reverse else SUBLANES - 1
    a_tab = [(tab_ref[g, 0], tab_ref[g, 1]) for g in range(G)]

    def local(i, st):
        r = (seg - 1 - i) if reverse else i
        out = []
        for g in range(G):
            (ar, ai), (sr, si) = a_tab[g], st[g]
            nr = ar * sr - ai * si + x_ref[g, r, :, re]
            ni = ar * si + ai * sr + x_ref[g, r, :, im]
            x_ref[g, r, :, re] = nr
            x_ref[g, r, :, im] = ni
            out.append((nr, ni))
        return tuple(out)

    zero = jnp.zeros((SUBLANES, W), _F32)
    ends = lax.fori_loop(0, seg, local, tuple((zero, zero) for _ in range(G)), unroll=SCAN_UNROLL)

    entry = []
    shift = (SUBLANES - 1) if reverse else 1
    for g in range(G):
        er, ei = ends[g]
        fr = jnp.where(edge_in, carry_ref[g, :, re], pltpu.roll(er, shift, 0))
        fi = jnp.where(edge_in, carry_ref[g, :, im], pltpu.roll(ei, shift, 0))
        for j, dist in enumerate((1, 2, 4)):
            pr, pi = tab_ref[g, 2 + 2 * j], tab_ref[g, 3 + 2 * j]
            sh = (SUBLANES - dist) if reverse else dist
            sr, si = pltpu.roll(fr, sh, 0), pltpu.roll(fi, sh, 0)
            fr, fi = fr + pr * sr - pi * si, fi + pr * si + pi * sr
        br, bi = tab_ref[g, 8], tab_ref[g, 9]
        outr = br * fr - bi * fi + er
        outi = br * fi + bi * fr + ei
        carry_ref[g, :, re] = jnp.broadcast_to(outr[edge_out:edge_out + 1, :], (SUBLANES, W))
        carry_ref[g, :, im] = jnp.broadcast_to(outi[edge_out:edge_out + 1, :], (SUBLANES, W))
        entry.append((fr, fi))
        if entry_ref is not None:
            entry_ref[g, :, re] = fr
            entry_ref[g, :, im] = fi

    add_da = da_ref is not None

    def fix(r, st, first=False):
        out = []
        for g in range(G):
            fr, fi = entry[g]
            pwr, pwi = pw_ref[g, r, :, re], pw_ref[g, r, :, im]
            xr = x_ref[g, r, :, re] + (pwr * fr - pwi * fi)
            xi = x_ref[g, r, :, im] + (pwr * fi + pwi * fr)
            x_ref[g, r, :, re] = xr
            x_ref[g, r, :, im] = xi
            if add_da:
                prev = fwd_entry_ref.at[g] if first else fwd_ref.at[g, r - 1]
                xpr, xpi = prev[:, re], prev[:, im]
                out.append((st[g][0] + (xr * xpr + xi * xpi), st[g][1] + (xi * xpr - xr * xpi)))
            else:
                out.append(st[g])
        return tuple(out)

    if add_da:
        st = fix(0, tuple((zero, zero) for _ in range(G)), first=True)
        st = lax.fori_loop(1, seg, fix, st, unroll=SCAN_UNROLL)
        for g in range(G):
            da_ref[g, :, re] += st[g][0]
            da_ref[g, :, im] += st[g][1]
    else:
        lax.fori_loop(0, seg, fix, tuple((zero[0:1, 0:LANES],) for _ in range(G)), unroll=FIX_UNROLL)


def _s5_specs(seg, time=lambda t: t):
    G = KB_PER_STEP
    return dict(
        x=pl.BlockSpec((G, seg, SUBLANES, 2 * KB_STATES), lambda kb, t: (kb, time(t), 0, 0)),
        ent=pl.BlockSpec((G, 1, SUBLANES, 2 * KB_STATES), lambda kb, t: (kb, time(t), 0, 0)),
        b=pl.BlockSpec((G, LANES, 2 * KB_STATES), lambda kb, t: (kb, 0, 0)),
        c=pl.BlockSpec((G, 2 * KB_STATES, LANES), lambda kb, t: (kb, 0, 0)),
        tab=pl.BlockSpec((G, 10, SUBLANES, KB_STATES), lambda kb, t: (kb, 0, 0, 0)),
        pw=pl.BlockSpec((G, seg, 1, 2 * KB_STATES), lambda kb, t: (kb, 0, 0, 0)),
        d=pl.BlockSpec((1, G * LANES), lambda kb, t: (0, kb)),
    )


def _s5_fwd(u, bmat, cmat, tab_f, pw_f, d_skip, tb):
    L = u.shape[0]
    nt = L // tb
    seg = tb // SUBLANES
    G = KB_PER_STEP
    ucol = pl.BlockSpec((tb, G * LANES), lambda kb, t: (t, kb))
    sp = _s5_specs(seg)

    def body(u_ref, b_ref, c_ref, tab_ref, pw_ref, d_ref, s_ref, x_ref, ent_ref, up_scr, y_scr, carry_scr):
        @pl.when(pl.program_id(1) == 0)
        def _():
            carry_scr[...] = jnp.zeros_like(carry_scr)

        _rows_to_segments(up_scr, u_ref, seg)
        for g in range(G):
            x_ref[g] = _dot(up_scr[g].astype(_BF), b_ref[g]).reshape(seg, SUBLANES, 2 * KB_STATES)
        _scan_segments(x_ref, tab_ref, pw_ref, carry_scr, seg, reverse=False, entry_ref=ent_ref.at[:, 0])
        for g in range(G):
            y = _dot(x_ref[g].reshape(tb, 2 * KB_STATES).astype(_BF), c_ref[g])
            y_scr[g] = y + d_ref[:, g * LANES:(g + 1) * LANES] * up_scr[g]
        _segments_to_rows(s_ref, y_scr, seg)

    return pl.pallas_call(
        body, name="s5_fwd", grid=(N_KB // G, nt),
        in_specs=[ucol, sp["b"], sp["c"], sp["tab"], sp["pw"], sp["d"]],
        out_specs=[ucol, sp["x"], sp["ent"]],
        out_shape=[jax.ShapeDtypeStruct((L, SSM_W), _F32),
                   jax.ShapeDtypeStruct((N_KB, L // SUBLANES, SUBLANES, 2 * KB_STATES), _F32),
                   jax.ShapeDtypeStruct((N_KB, nt, SUBLANES, 2 * KB_STATES), _F32)],
        scratch_shapes=[pltpu.VMEM((G, tb, LANES), _F32)] * 2 + [pltpu.VMEM((G, SUBLANES, 2 * KB_STATES), _F32)],
        compiler_params=_params("parallel", "arbitrary"),
    )(u, bmat, cmat, tab_f, pw_f, d_skip)


def _mixout_fwd(s, y_ret, x, w_glu, w_out, g2, tm):
    L = s.shape[0]

    def body(s_ref, yr_ref, x_ref, wg_ref, wo_ref, g_ref, ys_ref, glu_ref, cat_ref, mix_ref, x2_ref):
        for rows in _row_chunks(tm):
            ys = _gelu(s_ref[rows, :]).astype(_BF)
            ys_ref[rows, :] = ys
            glu = _dot(ys, wg_ref[...])
            glu_ref[rows, :] = glu
            cat_ref[rows, :RET_W] = yr_ref[rows, :]
            cat_ref[rows, RET_W:] = (glu[:, :SSM_W] * _sigmoid(glu[:, SSM_W:])).astype(_BF)
            mix = _dot(cat_ref[rows, :], wo_ref[...])
            mix_ref[rows, :] = mix
            x2_ref[rows, :] = x_ref[rows, :] + mix * _rms_r(mix) * g_ref[...]

    return pl.pallas_call(
        body, name="mixout_fwd", grid=(L // tm,),
        in_specs=[_row_spec(tm, SSM_W), _row_spec(tm, RET_W), _row_spec(tm, D_MODEL),
                  _weight_spec((SSM_W, 2 * SSM_W)), _weight_spec((D_MODEL, D_MODEL)), _full_spec((1, D_MODEL))],
        out_specs=[_row_spec(tm, SSM_W), _row_spec(tm, 2 * SSM_W), _row_spec(tm, D_MODEL),
                   _row_spec(tm, D_MODEL), _row_spec(tm, D_MODEL)],
        out_shape=[jax.ShapeDtypeStruct((L, SSM_W), _BF), jax.ShapeDtypeStruct((L, 2 * SSM_W), _F32),
                   jax.ShapeDtypeStruct((L, D_MODEL), _BF), jax.ShapeDtypeStruct((L, D_MODEL), _F32),
                   jax.ShapeDtypeStruct((L, D_MODEL), _F32)],
        compiler_params=_params("parallel"),
    )(s, y_ret, x, w_glu, w_out, g2)


FF1_COLS = D_FF // N_DEV


def _ff1_fwd(x2, g3, w1, tm):
    L = x2.shape[0]

    def body(x_ref, g_ref, w_ref, h_ref, f_ref):
        xv = x_ref[...]
        h = (xv * _rms_r(xv) * g_ref[...]).astype(_BF)
        h_ref[...] = h
        for j in range(N_DEV):
            f_ref[:, j * FF1_COLS:(j + 1) * FF1_COLS] = _dot(h, w_ref[j])

    return pl.pallas_call(
        body, name="ff1_fwd", grid=(L // tm,),
        in_specs=[_row_spec(tm, D_MODEL), _full_spec((1, D_MODEL)), _weight_spec((N_DEV, D_MODEL, FF1_COLS))],
        out_specs=[_row_spec(tm, D_MODEL), _row_spec(tm, D_FF)],
        out_shape=[jax.ShapeDtypeStruct((L, D_MODEL), _BF), jax.ShapeDtypeStruct((L, D_FF), _F32)],
        compiler_params=_params("parallel"),
    )(x2, g3, w1)


def _ff2_loss(f1, x2, tgt, g4, w2, tm):
    L = f1.shape[0]

    def body(f_ref, x_ref, t_ref, g_ref, w_ref, dy_ref, dm_ref, dg_ref, ls_ref):
        @pl.when(pl.program_id(0) == 0)
        def _():
            dg_ref[...] = jnp.zeros_like(dg_ref)
            ls_ref[...] = jnp.zeros_like(ls_ref)

        g = g_ref[...]
        for rows in _row_chunks(tm):
            rl = jnp.maximum(f_ref[rows, :], 0.0)
            m = _dot((rl * rl).astype(_BF), w_ref[...])
            y = x_ref[rows, :] + m * _rms_r(m) * g
            err = y - t_ref[rows, :]
            ls_ref[...] += jnp.sum(err * err, axis=0, keepdims=True)
            dy = err * (1.0 / D_MODEL)
            dy_ref[rows, :] = dy
            dm, dgr = _rms_bwd(m, g, dy)
            dm_ref[rows, :] = dm.astype(_BF)
            dg_ref[...] += jnp.sum(dgr, axis=0, keepdims=True)

    return pl.pallas_call(
        body, name="ff2_loss", grid=(L // tm,),
        in_specs=[_row_spec(tm, D_FF), _row_spec(tm, D_MODEL), _row_spec(tm, D_MODEL),
                  _full_spec((1, D_MODEL)), _weight_spec((D_FF, D_MODEL))],
        out_specs=[_row_spec(tm, D_MODEL), _row_spec(tm, D_MODEL), _full_spec((1, D_MODEL)), _full_spec((1, D_MODEL))],
        out_shape=[jax.ShapeDtypeStruct((L, D_MODEL), _F32), jax.ShapeDtypeStruct((L, D_MODEL), _BF),
                   jax.ShapeDtypeStruct((1, D_MODEL), _F32), jax.ShapeDtypeStruct((1, D_MODEL), _F32)],
        compiler_params=_params("arbitrary"),
    )(f1, x2, tgt, g4, w2)


def _ff2_bwd(dm, f1, w2, tm, tn):
    L = dm.shape[0]
    last = L // tm - 1

    def body(dm_ref, f_ref, w_ref, df_ref, dw_ref, acc):
        @pl.when(pl.program_id(1) == 0)
        def _():
            acc[...] = jnp.zeros_like(acc)

        dmv = dm_ref[...]
        rl = jnp.maximum(f_ref[...], 0.0)
        df_ref[...] = (_dot_nt(dmv, w_ref[...]) * (2.0 * rl)).astype(_BF)
        acc[...] += _dot_tn((rl * rl).astype(_BF), dmv)

        @pl.when(pl.program_id(1) == last)
        def _():
            dw_ref[...] = acc[...].astype(_BF)

    return pl.pallas_call(
        body, name="ff2_bwd", grid=(D_FF // tn, L // tm),
        in_specs=[pl.BlockSpec((tm, D_MODEL), lambda j, i: (i, 0)), pl.BlockSpec((tm, tn), lambda j, i: (i, j)),
                  pl.BlockSpec((tn, D_MODEL), lambda j, i: (j, 0))],
        out_specs=[pl.BlockSpec((tm, tn), lambda j, i: (i, j)), pl.BlockSpec((tn, D_MODEL), lambda j, i: (j, 0))],
        out_shape=[jax.ShapeDtypeStruct((L, D_FF), _BF), jax.ShapeDtypeStruct((D_FF, D_MODEL), _BF)],
        scratch_shapes=[pltpu.VMEM((tn, D_MODEL), _F32)],
        compiler_params=_params("parallel", "arbitrary"),
    )(dm, f1, w2)


def _ff1_bwd(df1, w1, x2, mix, dy, g3, g2, tm):
    L = df1.shape[0]

    def body(df_ref, w_ref, x2_ref, mix_ref, dy_ref, g3_ref, g2_ref, dx2_ref, dmix_ref, dg3_ref, dg2_ref):
        @pl.when(pl.program_id(0) == 0)
        def _():
            dg3_ref[...] = jnp.zeros_like(dg3_ref)
            dg2_ref[...] = jnp.zeros_like(dg2_ref)

        for rows in _row_chunks(tm):
            dh = _dot_nt(df_ref[rows, 0:FF1_COLS], w_ref[0])
            for j in range(1, N_DEV):
                dh = dh + _dot_nt(df_ref[rows, j * FF1_COLS:(j + 1) * FF1_COLS], w_ref[j])
            dz, dgr = _rms_bwd(x2_ref[rows, :], g3_ref[...], dh)
            dg3_ref[...] += jnp.sum(dgr, axis=0, keepdims=True)
            dx2 = dy_ref[rows, :] + dz
            dx2_ref[rows, :] = dx2
            dmx, dgr2 = _rms_bwd(mix_ref[rows, :], g2_ref[...], dx2)
            dg2_ref[...] += jnp.sum(dgr2, axis=0, keepdims=True)
            dmix_ref[rows, :] = dmx.astype(_BF)

    vec = _full_spec((1, D_MODEL))
    return pl.pallas_call(
        body, name="ff1_bwd", grid=(L // tm,),
        in_specs=[_row_spec(tm, D_FF), _weight_spec((N_DEV, D_MODEL, FF1_COLS)), _row_spec(tm, D_MODEL),
                  _row_spec(tm, D_MODEL), _row_spec(tm, D_MODEL), vec, vec],
        out_specs=[_row_spec(tm, D_MODEL), _row_spec(tm, D_MODEL), vec, vec],
        out_shape=[jax.ShapeDtypeStruct((L, D_MODEL), _F32), jax.ShapeDtypeStruct((L, D_MODEL), _BF),
                   jax.ShapeDtypeStruct((1, D_MODEL), _F32), jax.ShapeDtypeStruct((1, D_MODEL), _F32)],
        compiler_params=_params("arbitrary"),
    )(df1, w1, x2, mix, dy, g3, g2)


def _matmul_tn(a, b, tm, tn, name, slots=False):
    L, K = a.shape
    N = b.shape[1]
    last = L // tm - 1

    def body(a_ref, b_ref, o_ref, acc):
        @pl.when(pl.program_id(1) == 0)
        def _():
            acc[...] = jnp.zeros_like(acc)

        acc[...] += _dot_tn(a_ref[...].astype(_BF), b_ref[...].astype(_BF))

        @pl.when(pl.program_id(1) == last)
        def _():
            if slots:
                o_ref[0] = acc[...].astype(_BF)
            else:
                o_ref[...] = acc[...].astype(_BF)

    if slots:
        out_spec = pl.BlockSpec((1, K, tn), lambda j, i: (j, 0, 0))
        out_shape = jax.ShapeDtypeStruct((N // tn, K, tn), _BF)
    else:
        out_spec = pl.BlockSpec((K, tn), lambda j, i: (0, j))
        out_shape = jax.ShapeDtypeStruct((K, N), _BF)
    return pl.pallas_call(
        body, name=name, grid=(N // tn, L // tm),
        in_specs=[pl.BlockSpec((tm, K), lambda j, i: (i, 0)), pl.BlockSpec((tm, tn), lambda j, i: (i, j))],
        out_specs=out_spec, out_shape=out_shape,
        scratch_shapes=[pltpu.VMEM((K, tn), _F32)],
        compiler_params=_params("parallel", "arbitrary"),
    )(a, b)


def _dw_in_t(pieces, h, tk):
    L = h.shape[0]
    last = L // tk - 1

    def body(p0, p1, p2, p3, p4, h_ref, o_ref, acc):
        @pl.when(pl.program_id(0) == 0)
        def _():
            acc[...] = jnp.zeros_like(acc)

        hv = h_ref[...]
        for j, p in enumerate((p0, p1, p2, p3, p4)):
            acc[j * RET_W:(j + 1) * RET_W, :] += _dot_tn(p[...].astype(_BF), hv)

        @pl.when(pl.program_id(0) == last)
        def _():
            o_ref[...] = acc[...].astype(_BF)

    return pl.pallas_call(
        body, name="dw_in", grid=(L // tk,),
        in_specs=[_row_spec(tk, RET_W)] * 5 + [_row_spec(tk, D_MODEL)],
        out_specs=_full_spec((IN_COLS, D_MODEL)), out_shape=jax.ShapeDtypeStruct((IN_COLS, D_MODEL), _BF),
        scratch_shapes=[pltpu.VMEM((IN_COLS, D_MODEL), _F32)],
        compiler_params=_params("arbitrary"),
    )(*pieces, h)


def _mixout_bwd(dmix, w_out, w_glu, glu, s, o, gate, ggn, tm):
    L = dmix.shape[0]

    def body(dmix_ref, wo_ref, wg_ref, glu_ref, s_ref, o_ref, gate_ref, ggn_ref,
             dglu_ref, ds_ref, dgate_ref, do_ref, dggn_ref):
        @pl.when(pl.program_id(0) == 0)
        def _():
            dggn_ref[...] = jnp.zeros_like(dggn_ref)

        dcat = _dot_nt(dmix_ref[...], wo_ref[...])
        dy_ret, dy_ssm = dcat[:, :RET_W], dcat[:, RET_W:]
        glu = glu_ref[...]
        ga, sg = glu[:, :SSM_W], _sigmoid(glu[:, SSM_W:])
        dga = (dy_ssm * sg).astype(_BF)
        dgb = (dy_ssm * ga * sg * (1.0 - sg)).astype(_BF)
        dglu_ref[:, :SSM_W] = dga
        dglu_ref[:, SSM_W:] = dgb
        dys = _dot_nt(dga, wg_ref[:, :SSM_W]) + _dot_nt(dgb, wg_ref[:, SSM_W:])
        ds_ref[...] = dys * _gelu_grad(s_ref[...])
        gt = gate_ref[...]
        sgt = _sigmoid(gt)
        ggn = ggn_ref[...]
        for hh in range(N_HEAD):
            cols = slice(hh * HEAD_D, (hh + 1) * HEAD_D)
            ov = o_ref[:, cols]
            dlt = ov - jnp.mean(ov, axis=-1, keepdims=True)
            rstd = lax.rsqrt(jnp.mean(dlt * dlt, axis=-1, keepdims=True) + NORM_EPS)
            on = dlt * rstd
            dyr = dy_ret[:, cols] * (gt[:, cols] * sgt[:, cols])
            dgate_ref[:, cols] = dy_ret[:, cols] * (on * ggn[:, cols]) * (sgt[:, cols] * (1.0 + gt[:, cols] * (1.0 - sgt[:, cols])))
            dggn_ref[:, cols] += jnp.sum(dyr * on, axis=0, keepdims=True)
            don = dyr * ggn[:, cols]
            do = rstd * (don - jnp.mean(don, axis=-1, keepdims=True) - on * jnp.mean(don * on, axis=-1, keepdims=True))
            do_ref[:, cols] = do.astype(_BF)

    return pl.pallas_call(
        body, name="mixout_bwd", grid=(L // tm,),
        in_specs=[_row_spec(tm, D_MODEL), _weight_spec((D_MODEL, D_MODEL)), _weight_spec((SSM_W, 2 * SSM_W)),
                  _row_spec(tm, 2 * SSM_W), _row_spec(tm, SSM_W), _row_spec(tm, RET_W), _row_spec(tm, RET_W),
                  _full_spec((1, RET_W))],
        out_specs=[_row_spec(tm, 2 * SSM_W), _row_spec(tm, SSM_W), _row_spec(tm, RET_W), _row_spec(tm, RET_W),
                   _full_spec((1, RET_W))],
        out_shape=[jax.ShapeDtypeStruct((L, 2 * SSM_W), _BF), jax.ShapeDtypeStruct((L, SSM_W), _F32),
                   jax.ShapeDtypeStruct((L, RET_W), _F32), jax.ShapeDtypeStruct((L, RET_W), _BF),
                   jax.ShapeDtypeStruct((1, RET_W), _F32)],
        compiler_params=_params("arbitrary"),
    )(dmix, w_out, w_glu, glu, s, o, gate, ggn)


def _s5_bwd(u, ds, xs, ent, bmat, cmat, tab_r, pw_r, d_skip, tb):
    L = u.shape[0]
    nt = L // tb
    seg = tb // SUBLANES
    G = KB_PER_STEP
    rcol = pl.BlockSpec((tb, G * LANES), lambda kb, t: (nt - 1 - t, kb))
    sp = _s5_specs(seg, time=lambda t: nt - 1 - t)
    aspec = pl.BlockSpec((G, SUBLANES, 2 * KB_STATES), lambda kb, t: (kb, 0, 0))

    def body(u_ref, ds_ref, x_ref, ent_ref, b_ref, c_ref, tr_ref, pr_ref, d_ref,
             du_ref, db_ref, dc_ref, da_ref, dd_ref, up_scr, dp_scr, g_scr, lc_scr):
        @pl.when(pl.program_id(1) == 0)
        def _():
            lc_scr[...] = jnp.zeros_like(lc_scr)
            db_ref[...] = jnp.zeros_like(db_ref)
            dc_ref[...] = jnp.zeros_like(dc_ref)
            da_ref[...] = jnp.zeros_like(da_ref)
            dd_ref[...] = jnp.zeros_like(dd_ref)

        _rows_to_segments(up_scr, u_ref, seg)
        _rows_to_segments(dp_scr, ds_ref, seg)
        for g in range(G):
            g_scr[g] = _dot_nt(dp_scr[g].astype(_BF), c_ref[g]).reshape(seg, SUBLANES, 2 * KB_STATES)
        _scan_segments(g_scr, tr_ref, pr_ref, lc_scr, seg, reverse=True, fwd_ref=x_ref, fwd_entry_ref=ent_ref.at[:, 0],
                       da_ref=da_ref)
        for g in range(G):
            cols = slice(g * LANES, (g + 1) * LANES)
            uv, dsv = up_scr[g], dp_scr[g]
            ub, dsb = uv.astype(_BF), dsv.astype(_BF)
            lamb = g_scr[g].reshape(tb, 2 * KB_STATES).astype(_BF)
            db_ref[g] += _dot_tn(ub, lamb)
            dc_ref[g] += _dot_tn(dsb, x_ref[g].reshape(tb, 2 * KB_STATES).astype(_BF))
            dd_ref[:, cols] += jnp.sum(dsv * uv, axis=0, keepdims=True)
            up_scr[g] = _dot_nt(lamb, b_ref[g]) + d_ref[:, cols] * dsv
        _segments_to_rows(du_ref, up_scr, seg)

    return pl.pallas_call(
        body, name="s5_bwd", grid=(N_KB // G, nt),
        in_specs=[rcol, rcol, sp["x"], sp["ent"], sp["b"], sp["c"], sp["tab"], sp["pw"], sp["d"]],
        out_specs=[rcol, sp["b"], sp["b"], aspec, sp["d"]],
        out_shape=[jax.ShapeDtypeStruct((L, SSM_W), _F32),
                   jax.ShapeDtypeStruct((N_KB, LANES, 2 * KB_STATES), _F32),
                   jax.ShapeDtypeStruct((N_KB, LANES, 2 * KB_STATES), _F32),
                   jax.ShapeDtypeStruct((N_KB, SUBLANES, 2 * KB_STATES), _F32),
                   jax.ShapeDtypeStruct((1, SSM_W), _F32)],
        scratch_shapes=[pltpu.VMEM((G, tb, LANES), _F32)] * 2
        + [pltpu.VMEM((G, seg, SUBLANES, 2 * KB_STATES), _F32), pltpu.VMEM((G, SUBLANES, 2 * KB_STATES), _F32)],
        compiler_params=_params("parallel", "arbitrary"),
    )(u, ds, xs, ent, bmat, cmat, tab_r, pw_r, d_skip)


def _retention_bwd(q, k, v, do, r_prev, consts, cosf, sinf):
    L = q.shape[0]
    nc = L // CHUNK
    cps = math.gcd(RET_STEP_CHUNKS, nc)
    nb = nc // cps
    blk = pl.BlockSpec((cps * CHUNK, RET_W), lambda n: (nb - 1 - n, 0))
    rope_blk = pl.BlockSpec((cps * CHUNK, HEAD_D), lambda n: (nb - 1 - n, 0))

    def body(q_ref, k_ref, v_ref, do_ref, rp_ref, dm_ref, xi_ref, zeta_ref, gc_ref, cos_ref, sin_ref,
             dq_ref, dk_ref, dv_ref, g_scr):
        @pl.when(pl.program_id(0) == 0)
        def _():
            g_scr[...] = jnp.zeros_like(g_scr)

        for hh in range(N_HEAD):
            cols = slice(hh * HEAD_D, (hh + 1) * HEAD_D)
            dm, zeta = dm_ref[hh], zeta_ref[hh]
            gst = g_scr[hh]
            for c in reversed(range(cps)):
                rows = slice(c * CHUNK, (c + 1) * CHUNK)
                qv, kv, vv, dov = q_ref[rows, cols], k_ref[rows, cols], v_ref[rows, cols], do_ref[rows, cols]
                rb = rp_ref[hh, c].astype(_BF)
                gb = gst.astype(_BF)
                sb = (_dot_nt(qv, kv) * dm).astype(_BF)
                dab = (_dot_nt(dov, vv) * dm).astype(_BF)
                dox = (dov.astype(_F32) * xi_ref[hh]).astype(_BF)
                vz = (vv.astype(_F32) * zeta).astype(_BF)
                dq = _dot(dab, kv) + _dot_nt(dox, rb)
                dk = _dot_tn(dab, qv) + _dot_nt(vz, gb)
                dv = _dot_tn(sb, dov) + _dot(kv, gb) * zeta
                gst = gc_ref[hh, 0:1, :] * gst + _dot_tn(qv, dox)
                cs, sn = cos_ref[rows, :], sin_ref[rows, :]
                dq_ref[rows, cols] = _rope_t(dq, cs, sn).astype(_BF)
                dk_ref[rows, cols] = (_rope_t(dk, cs, sn) * (HEAD_D ** -0.5)).astype(_BF)
                dv_ref[rows, cols] = dv.astype(_BF)
            g_scr[hh] = gst

    return pl.pallas_call(
        body, name="retention_bwd", grid=(nb,),
        in_specs=[blk, blk, blk, blk, pl.BlockSpec((N_HEAD, cps, HEAD_D, HEAD_D), lambda n: (0, nb - 1 - n, 0, 0))]
        + _head_specs() + [rope_blk, rope_blk],
        out_specs=[blk, blk, blk],
        out_shape=[jax.ShapeDtypeStruct((L, RET_W), _BF)] * 3,
        scratch_shapes=[pltpu.VMEM((N_HEAD, HEAD_D, HEAD_D), _F32)],
        compiler_params=_params("arbitrary"),
    )(q, k, v, do, r_prev, *consts, cosf, sinf)


def _inproj_bwd(pieces, w_in_t, x, dx2, g1, tm):
    L = x.shape[0]

    def body(p0, p1, p2, p3, p4, w_ref, x_ref, dx2_ref, g_ref, dx_ref, dg_ref):
        @pl.when(pl.program_id(0) == 0)
        def _():
            dg_ref[...] = jnp.zeros_like(dg_ref)

        for rows in _row_chunks(tm):
            dh = None
            for j, p in enumerate((p0, p1, p2, p3, p4)):
                part = _dot(p[rows, :].astype(_BF), w_ref[j * RET_W:(j + 1) * RET_W, :])
                dh = part if dh is None else dh + part
            dz, dgr = _rms_bwd(x_ref[rows, :], g_ref[...], dh)
            dx_ref[rows, :] = dx2_ref[rows, :] + dz
            dg_ref[...] += jnp.sum(dgr, axis=0, keepdims=True)

    return pl.pallas_call(
        body, name="inproj_bwd", grid=(L // tm,),
        in_specs=[_row_spec(tm, RET_W)] * 5 + [_weight_spec((IN_COLS, D_MODEL)), _row_spec(tm, D_MODEL),
                                                 _row_spec(tm, D_MODEL), _full_spec((1, D_MODEL))],
        out_specs=[_row_spec(tm, D_MODEL), _full_spec((1, D_MODEL))],
        out_shape=[jax.ShapeDtypeStruct((L, D_MODEL), _F32), jax.ShapeDtypeStruct((1, D_MODEL), _F32)],
        compiler_params=_params("arbitrary"),
    )(*pieces, w_in_t, x, dx2, g1)


def _sum_adamw(parts, w, m, v, tr, name):
    _, R, Cc = parts.shape

    def body(p_ref, w_ref, m_ref, v_ref, g_ref, d_ref, nm_ref, nv_ref):
        gv = p_ref[0].astype(_F32)
        for s in range(1, N_DEV):
            gv = gv + p_ref[s].astype(_F32)
        g_ref[...] = gv
        nm = ADAM_B1 * m_ref[...] + (1.0 - ADAM_B1) * gv
        nv = ADAM_B2 * v_ref[...] + (1.0 - ADAM_B2) * (gv * gv)
        m_hat = nm / (1.0 - ADAM_B1 ** ADAM_STEP)
        v_hat = nv / (1.0 - ADAM_B2 ** ADAM_STEP)
        d_ref[...] = -ADAM_LR * (m_hat / (jnp.sqrt(v_hat) + ADAM_EPS) + ADAM_WD * w_ref[...])
        nm_ref[...] = nm
        nv_ref[...] = nv

    spec = _row_spec(tr, Cc)
    return pl.pallas_call(
        body, name=name, grid=(R // tr,),
        in_specs=[pl.BlockSpec((N_DEV, tr, Cc), lambda i: (0, i, 0))] + [spec] * 3, out_specs=[spec] * 4,
        out_shape=[jax.ShapeDtypeStruct((R, Cc), _F32)] * 4,
        compiler_params=_params("parallel"),
    )(parts, w, m, v)


def _my_place():
    return lax.axis_index("x"), lax.axis_index("y"), lax.axis_index("c")


def _all_gather(blocks):
    n = len(blocks)

    def body(*refs):
        x_refs, out_refs, done_ref = refs[:n], refs[n:2 * n], refs[2 * n]
        send_sems, recv_sems, local_sems = refs[2 * n + 1:]
        done_ref[...] = jnp.zeros_like(done_ref)
        x, y, c = _my_place()
        me, sibling = (x, y, c), (x, y, 1 - c)
        chips = [(1 - x, y), (x, 1 - y), (1 - x, 1 - y)]

        def slot(a, px, py, pc):
            return out_refs[a].at[4 * px + 2 * py + pc]

        def copy(a, k, blk, to, own=False):
            return pltpu.make_async_remote_copy(
                src_ref=x_refs[a] if own else slot(a, *blk), dst_ref=slot(a, *blk),
                send_sem=send_sems.at[a, k], recv_sem=recv_sems.at[a, k], device_id=to, device_id_type=MESH)

        mine = [pltpu.make_async_copy(x_refs[a], slot(a, *me), local_sems.at[a]) for a in range(n)]
        for cp in mine:
            cp.start()
        first = []
        for a in range(n):
            first.append(copy(a, 0, me, sibling, own=True))
            first += [copy(a, 1 + j, me, (*chip, c), own=True) for j, chip in enumerate(chips)]
        for cp in first:
            cp.start()
        passed = []
        for j, chip in enumerate(chips):
            for a in range(n):
                copy(a, 1 + j, (*chip, c), me).wait_recv()
                fwd = copy(a, 4 + j, (*chip, c), sibling)
                fwd.start()
                passed.append(fwd)
        for a in range(n):
            copy(a, 0, sibling, me).wait_recv()
            for j, chip in enumerate(chips):
                copy(a, 4 + j, (*chip, 1 - c), me).wait_recv()
        for cp in first + passed:
            cp.wait_send()
        for cp in mine:
            cp.wait()

    any_spec = pl.BlockSpec(memory_space=pl.ANY)
    outs = pl.pallas_call(
        body, name="weights_all_gather",
        in_specs=[any_spec] * n, out_specs=[any_spec] * n + [pl.BlockSpec(memory_space=pltpu.VMEM)],
        out_shape=[jax.ShapeDtypeStruct((N_DEV,) + b.shape, b.dtype) for b in blocks]
        + [jax.ShapeDtypeStruct((SUBLANES, LANES), _F32)],
        scratch_shapes=[pltpu.SemaphoreType.DMA((n, 7)), pltpu.SemaphoreType.DMA((n, 7)), pltpu.SemaphoreType.DMA((n,))],
    )(*blocks)
    return outs[:n], outs[n]


def _exchange(bigs, small):
    n = len(bigs)
    r = small.shape[0]

    def body(*refs):
        in_refs, out_refs = refs[:n + 1], refs[n + 1:2 * n + 2]
        send_sems, recv_sems, local_sems = refs[2 * n + 2:]
        x, y, c = _my_place()
        me = 4 * x + 2 * y + c
        own = [pltpu.make_async_copy(in_refs[a].at[me], out_refs[a].at[me], local_sems.at[a]) for a in range(n)]
        own.append(pltpu.make_async_copy(in_refs[n], out_refs[n].at[me], local_sems.at[n]))
        for cp in own:
            cp.start()
        copies = []
        for kk in range(1, N_DEV):
            px, py, pc = x ^ (kk >> 2), y ^ ((kk >> 1) & 1), c ^ (kk & 1)
            peer = 4 * px + 2 * py + pc
            for a in range(n + 1):
                src = in_refs[a].at[peer] if a < n else in_refs[a]
                copies.append(pltpu.make_async_remote_copy(
                    src_ref=src, dst_ref=out_refs[a].at[me],
                    send_sem=send_sems.at[a, kk - 1], recv_sem=recv_sems.at[a, kk - 1],
                    device_id=(px, py, pc), device_id_type=MESH))
        for cp in copies:
            cp.start()
        for cp in copies:
            cp.wait_recv()
        for cp in copies:
            cp.wait_send()
        for cp in own:
            cp.wait()

    any_spec = pl.BlockSpec(memory_space=pl.ANY)
    outs = pl.pallas_call(
        body, name="grad_exchange",
        in_specs=[any_spec] * (n + 1), out_specs=[any_spec] * (n + 1),
        out_shape=[jax.ShapeDtypeStruct(b.shape, b.dtype) for b in bigs]
        + [jax.ShapeDtypeStruct((N_DEV, r, LANES), small.dtype)],
        scratch_shapes=[pltpu.SemaphoreType.DMA((n + 1, 7)), pltpu.SemaphoreType.DMA((n + 1, 7)),
                        pltpu.SemaphoreType.DMA((n + 1,))],
    )(*bigs, small)
    return outs[:n], outs[n]


HBM_SPEC = pl.BlockSpec(memory_space=pltpu.HBM)
SEM_SPEC = pl.BlockSpec(memory_space=pltpu.SEMAPHORE)
DATAFLOW = pltpu.SideEffectType.DATAFLOW_SIDE_EFFECTING


def _my_index():
    x, y, c = _my_place()
    return 4 * x + 2 * y + c


def _landing(own_block):
    zone = lax.empty((N_DEV,) + own_block.shape, own_block.dtype)
    return lax.dynamic_update_index_in_dim(zone, own_block, _my_index(), 0)


def _split_copies(src_refs, land_refs, send_sems, recv_sems, gather):
    x, y, c = _my_place()
    me = 4 * x + 2 * y + c
    copies = []
    for kk in range(1, N_DEV):
        px, py, pc = x ^ (kk >> 2), y ^ ((kk >> 1) & 1), c ^ (kk & 1)
        peer = 4 * px + 2 * py + pc
        for a, (src, land) in enumerate(zip(src_refs, land_refs)):
            copies.append(pltpu.make_async_remote_copy(
                src_ref=src if gather else src.at[peer], dst_ref=land.at[me],
                send_sem=send_sems.at[a * 7 + kk - 1], recv_sem=recv_sems.at[a * 7 + kk - 1],
                device_id=(px, py, pc), device_id_type=MESH))
    return copies


def _split_start(srcs, lands, gather, name):
    n = len(srcs)

    def body(*refs):
        src_refs, land_refs = refs[:n], refs[n:2 * n]
        send_sems, recv_sems = refs[2 * n], refs[2 * n + 1]
        token = refs[-1]
        for cp in _split_copies(src_refs, land_refs, send_sems, recv_sems, gather):
            cp.start()
        token[...] = jnp.zeros_like(token)

    outs = pl.pallas_call(
        body, name=name,
        out_shape=(pltpu.SemaphoreType.DMA((7 * n,)), pltpu.SemaphoreType.DMA((7 * n,)),
                   *[pltpu.HBM(t.shape, t.dtype) for t in srcs], *[pltpu.HBM(t.shape, t.dtype) for t in lands],
                   jax.ShapeDtypeStruct((SUBLANES, LANES), _F32)),
        in_specs=[HBM_SPEC] * (2 * n),
        out_specs=(SEM_SPEC, SEM_SPEC, *[HBM_SPEC] * (2 * n), pl.BlockSpec(memory_space=pltpu.VMEM)),
        input_output_aliases={i: 2 + i for i in range(2 * n)},
        compiler_params=pltpu.CompilerParams(has_side_effects=DATAFLOW),
    )(*[pltpu.with_memory_space_constraint(t, pltpu.HBM) for t in list(srcs) + list(lands)])
    return outs[0], outs[1], outs[2:2 + n], outs[2 + n:2 + 2 * n], outs[-1]


def _split_wait(send_sems, recv_sems, srcs, lands, after, gather, name):
    n = len(srcs)

    def body(*refs):
        src_refs, land_refs = refs[:n], refs[n:2 * n]
        send_s, recv_s = refs[2 * n], refs[2 * n + 1]
        for cp in _split_copies(src_refs, land_refs, send_s, recv_s, gather):
            cp.wait_send()
            cp.wait_recv()

    outs = pl.pallas_call(
        body, name=name,
        out_shape=tuple(pltpu.HBM(t.shape, t.dtype) for t in list(srcs) + list(lands)),
        in_specs=[HBM_SPEC] * (2 * n) + [SEM_SPEC, SEM_SPEC, pl.BlockSpec(memory_space=pl.ANY)],
        out_specs=tuple([HBM_SPEC] * (2 * n)),
        input_output_aliases={i: i for i in range(2 * n)},
        compiler_params=pltpu.CompilerParams(has_side_effects=DATAFLOW),
    )(*srcs, *lands, send_sems, recv_sems, after)
    return outs[n:]


def _discretize(lam_re, lam_im, log_dt, b_re, b_im):
    lr = jnp.minimum(lam_re, -1e-4)
    li = lam_im
    dt = jnp.exp(log_dt)[:, None]
    er = jnp.exp(lr * dt)
    ar, ai = er * jnp.cos(li * dt), er * jnp.sin(li * dt)
    den = lr * lr + li * li
    cr = ((ar - 1.0) * lr + ai * li) / den
    ci = (ai * lr - (ar - 1.0) * li) / den
    bbr = cr[:, :, None] * b_re - ci[:, :, None] * b_im
    bbi = cr[:, :, None] * b_im + ci[:, :, None] * b_re
    return ar, ai, bbr, bbi


def _cmul(ar, ai, br, bi):
    return ar * br - ai * bi, ar * bi + ai * br


def _cpowers(ar, ai, n):
    pr, pi = ar[None], ai[None]
    while pr.shape[0] < n:
        nr, ni = _cmul(pr, pi, pr[-1][None], pi[-1][None])
        pr, pi = jnp.concatenate([pr, nr]), jnp.concatenate([pi, ni])
    return pr[:n], pi[:n]


def _scan_tables(ar, ai, seg, reverse):
    if reverse:
        ai = -ai
    ar, ai = ar.reshape(N_KB, KB_STATES), ai.reshape(N_KB, KB_STATES)
    pr, pi = _cpowers(ar, ai, seg)
    a1 = (pr[-1], pi[-1])
    a2 = _cmul(*a1, *a1)
    a4 = _cmul(*a2, *a2)
    row = jnp.arange(SUBLANES)[None, :, None]
    wide = lambda t: jnp.broadcast_to(t[:, None, :], (N_KB, SUBLANES, KB_STATES))
    tabs = [wide(ar), wide(ai)]
    for dist, (qr, qi) in ((1, a1), (2, a2), (4, a4)):
        keep = (row < SUBLANES - dist) if reverse else (row >= dist)
        tabs += [jnp.where(keep, wide(qr), 0.0), jnp.where(keep, wide(qi), 0.0)]
    tabs += [wide(a1[0]), wide(a1[1])]
    if reverse:
        pr, pi = pr[::-1], pi[::-1]
    pw = jnp.transpose(jnp.concatenate([pr, pi], axis=-1), (1, 0, 2))[:, :, None, :]
    return jnp.stack(tabs, axis=1).astype(_F32), pw.astype(_F32)


def _block_diag_in(br, bi):
    eye = jnp.eye(GROUPS_PER_KB, dtype=_F32)
    one = lambda t: jnp.einsum("kgpc,gh->kgchp", t.reshape(N_KB, GROUPS_PER_KB, N_STATE, SSM_GC), eye).reshape(
        N_KB, LANES, KB_STATES)
    return jnp.concatenate([one(br), one(bi)], axis=-1)


def _block_diag_in_t(dmat):
    d6 = dmat.reshape(N_KB, GROUPS_PER_KB, SSM_GC, 2, GROUPS_PER_KB, N_STATE)
    eye = jnp.eye(GROUPS_PER_KB, dtype=_F32)
    both = jnp.einsum("kgcrhp,gh->rkgpc", d6, eye).reshape(2, N_GROUP, N_STATE, SSM_GC)
    return both[0], both[1]


def _block_diag_out(c_re, c_im):
    eye = jnp.eye(GROUPS_PER_KB, dtype=_F32)
    one = lambda t: jnp.einsum("kgcp,gh->khpgc", t.reshape(N_KB, GROUPS_PER_KB, SSM_GC, N_STATE), eye).reshape(
        N_KB, KB_STATES, LANES)
    return jnp.concatenate([one(c_re), -one(c_im)], axis=1)


def _block_diag_out_t(dmat_t):
    d6 = dmat_t.reshape(N_KB, GROUPS_PER_KB, SSM_GC, 2, GROUPS_PER_KB, N_STATE)
    eye = jnp.eye(GROUPS_PER_KB, dtype=_F32)
    both = jnp.einsum("kgcrhp,gh->rkgcp", d6, eye).reshape(2, N_GROUP, SSM_GC, N_STATE)
    return both[0], -both[1]


SMALL_NAMES = ("norm_mix_pre", "norm_mix_post", "ret_gn_gain", "ssm_lambda_re", "ssm_lambda_im", "ssm_log_dt",
               "ssm_b_re", "ssm_b_im", "ssm_c_re", "ssm_c_im", "ssm_d", "norm_mlp_pre", "norm_mlp_post")


def _local_grads(x, tgt, small, weights, emit, emit_small, tm, tk, tb, zero=0.0):
    L = x.shape[0]
    g1, g2, ggn = small["norm_mix_pre"], small["norm_mix_post"], small["ret_gn_gain"]
    g3, g4, d_skip = small["norm_mlp_pre"], small["norm_mlp_post"], small["ssm_d"]

    rope = _rope_tables(L)
    consts = _ret_consts()

    disc_in = (small["ssm_lambda_re"][0], small["ssm_lambda_im"][0], small["ssm_log_dt"][0] + zero,
               small["ssm_b_re"][0], small["ssm_b_im"][0])
    (ar, ai, bbr, bbi), disc_vjp = jax.vjp(_discretize, *disc_in)
    bmat = _block_diag_in(bbr, bbi).astype(_BF)
    cmat = _block_diag_out(small["ssm_c_re"][0], small["ssm_c_im"][0]).astype(_BF)
    seg = tb // SUBLANES
    tab_f, pw_f = _scan_tables(ar, ai, seg, False)
    tab_r, pw_r = _scan_tables(ar, ai, seg, True)

    h1 = _prenorm(x, g1 + 0.0 * pw_r[0, 0, 0, 0], min(2 * tm, L))
    (w_in_t,) = weights("in", h1)
    q, k, v, gate, u, cosf, sinf = _inproj_fwd(h1, w_in_t, rope, tm)
    o, y_ret, r_prev = _retention_fwd(q, k, v, gate, ggn, consts)
    s, xs, ent = _s5_fwd(u, bmat, cmat, tab_f, pw_f, d_skip, tb)
    w_glu, w_out = weights("mix", s)
    ys, glu, cat, mix, x2 = _mixout_fwd(s, y_ret, x, w_glu, w_out, g2, min(2 * tm, L))
    w_ff1, w_ff2 = weights("mlp", x2)
    h3, f1 = _ff1_fwd(x2, g3, w_ff1, tm)
    dy, dm, dg4, sq = _ff2_loss(f1, x2, tgt, g4, w_ff2, min(2 * tm, L))

    df1, dw_ff2 = _ff2_bwd(dm, f1, w_ff2, min(1024, L), 1024)
    dx2, dmix, dg3, dg2 = _ff1_bwd(df1, w_ff1, x2, mix, dy, g3, g2, min(2 * tm, L))
    dw_ff1 = _matmul_tn(h3, df1, tk, FF1_COLS, "dw_ff1", slots=True)
    zero = emit({"w_ff1": dw_ff1, "w_ff2": dw_ff2})
    dglu, ds, dgate, do, dggn = _mixout_bwd(dmix, w_out, w_glu, glu, s, o, gate, ggn if zero is None else ggn + zero, tm)
    dw_out = _matmul_tn(cat, dmix, tk, 1024, "dw_out")
    dw_glu = _matmul_tn(ys, dglu, tk, 1024, "dw_glu")
    zero = emit({"w_glu": dw_glu, "w_out": dw_out})
    du, dbmat, dcmat, da8, dd = _s5_bwd(u, ds, xs, ent, bmat, cmat, tab_r, pw_r,
                                        d_skip if zero is None else d_skip + zero, tb)
    dq, dk, dv = _retention_bwd(q, k, v, do, r_prev, consts, cosf, sinf)
    pieces = (dq, dk, dv, dgate, du)
    dw_in_t = _dw_in_t(pieces, h1, min(1024, L))
    zero = emit({"w_in": dw_in_t})

    da = jnp.sum(da8, axis=1)
    dar = da[:, :KB_STATES].reshape(N_GROUP, N_STATE)
    dai = da[:, KB_STATES:].reshape(N_GROUP, N_STATE)
    dbr, dbi = _block_diag_in_t(dbmat)
    dlre, dlim, dldt, dbre, dbim = disc_vjp((dar, dai, dbr, dbi))
    dcre, dcim = _block_diag_out_t(dcmat)

    zero2 = emit_small({
        "norm_mix_post": dg2, "ret_gn_gain": dggn,
        "ssm_lambda_re": dlre[None], "ssm_lambda_im": dlim[None], "ssm_log_dt": dldt[None],
        "ssm_b_re": dbre[None], "ssm_b_im": dbim[None], "ssm_c_re": dcre[None], "ssm_c_im": dcim[None],
        "ssm_d": dd, "norm_mlp_pre": dg3, "norm_mlp_post": dg4,
    }, sq)
    for z in (zero, zero2):
        g1 = g1 if z is None else g1 + z
    gx, dg1 = _inproj_bwd(pieces, w_in_t, x, dx2, g1, min(2 * tm, L))
    return gx, dg1


BIG_SHAPES = {"w_in": (D_MODEL, IN_COLS // N_DEV), "w_glu": (SSM_W, 2 * SSM_W // N_DEV), "w_out": (D_MODEL // N_DEV, D_MODEL),
              "w_ff1": (D_MODEL, FF1_COLS), "w_ff2": (D_FF // N_DEV, D_MODEL)}
BIG_NAMES = ("w_in", "w_glu", "w_out", "w_ff1", "w_ff2")


def _cols_from_slots(g):
    return jnp.transpose(g, (1, 0, 2)).reshape(g.shape[1], N_DEV * g.shape[2])


def _cols_to_slots(dw):
    r, cols = dw.shape
    return jnp.transpose(dw.reshape(r, N_DEV, cols // N_DEV), (1, 0, 2))


WEIGHT_GROUPS = {"in": ("w_in",), "mix": ("w_glu", "w_out"), "mlp": ("w_ff1", "w_ff2")}


def _weight_from_slots(name, g):
    if name == "w_glu":
        return _cols_from_slots(g)
    if name == "w_ff1":
        return g
    return g.reshape(N_DEV * g.shape[1], g.shape[2])


def _grad_slots(name, dw):
    if name == "w_glu":
        return _cols_to_slots(dw)
    if name == "w_ff1":
        return dw
    if name == "w_in":
        return dw.reshape(N_DEV, BIG_SHAPES[name][1], BIG_SHAPES[name][0])
    return dw.reshape((N_DEV,) + BIG_SHAPES[name])


PIECE_ROWS = 8


def _small_layout(shapes):
    off, rows = {}, 0
    for n in SMALL_NAMES:
        off[n] = rows
        rows += -(-math.prod(shapes[n]) // (PIECE_ROWS * LANES)) * PIECE_ROWS
    return off, rows, rows + PIECE_ROWS


def _pack_small(vals, shapes, last=None):
    parts = []
    for n in SMALL_NAMES:
        flat = vals[n].reshape(-1).astype(_F32)
        pad = -flat.shape[0] % (PIECE_ROWS * LANES)
        if pad:
            flat = jnp.concatenate([flat, jnp.zeros((pad,), _F32)])
        parts.append(flat.reshape(-1, LANES))
    parts.append(jnp.zeros((PIECE_ROWS, LANES), _F32) if last is None else last)
    return jnp.concatenate(parts, axis=0)


def _unpack_small(buf, shapes):
    off, _, _ = _small_layout(shapes)
    out = {}
    for n in SMALL_NAMES:
        size = math.prod(shapes[n])
        rows = -(-size // LANES)
        out[n] = buf[off[n]:off[n] + rows].reshape(-1)[:size].reshape(shapes[n])
    return out


WEIGHT_NAMES = ('norm_mix_pre', 'norm_mix_post', 'w_in', 'ret_gn_gain', 'ssm_lambda_re', 'ssm_lambda_im', 'ssm_log_dt',
                'ssm_b_re', 'ssm_b_im', 'ssm_c_re', 'ssm_c_im', 'ssm_d', 'w_glu', 'w_out', 'norm_mlp_pre',
                'norm_mlp_post', 'w_ff1', 'w_ff2')


def kernel(x, norm_mix_pre, norm_mix_post, w_in, ret_gn_gain, ssm_lambda_re, ssm_lambda_im, ssm_log_dt, ssm_b_re, ssm_b_im, ssm_c_re, ssm_c_im, ssm_d, w_glu, w_out, norm_mlp_pre, norm_mlp_post, w_ff1, w_ff2, loss_target, m_norm_mix_pre, m_norm_mix_post, m_w_in, m_ret_gn_gain, m_ssm_lambda_re, m_ssm_lambda_im, m_ssm_log_dt, m_ssm_b_re, m_ssm_b_im, m_ssm_c_re, m_ssm_c_im, m_ssm_d, m_w_glu, m_w_out, m_norm_mlp_pre, m_norm_mlp_post, m_w_ff1, m_w_ff2, v_norm_mix_pre, v_norm_mix_post, v_w_in, v_ret_gn_gain, v_ssm_lambda_re, v_ssm_lambda_im, v_ssm_log_dt, v_ssm_b_re, v_ssm_b_im, v_ssm_c_re, v_ssm_c_im, v_ssm_d, v_w_glu, v_w_out, v_norm_mlp_pre, v_norm_mlp_post, v_w_ff1, v_w_ff2):
    args = dict(locals())
    w = {n: args[n] for n in WEIGHT_NAMES}
    m = {n: args["m_" + n] for n in WEIGHT_NAMES}
    v = {n: args["v_" + n] for n in WEIGHT_NAMES}
    L = x.shape[1]
    tm = min(256, L)
    tk = min(2048, L)
    tb = min(512, L)

    gathers, zero = {}, jnp.zeros((), _F32)
    for group, names in WEIGHT_GROUPS.items():
        blocks = [(w[n][0].T if n == "w_in" else w[n][0]).astype(_BF) for n in names]
        blocks[0] = blocks[0] + zero.astype(_BF)
        gathers[group] = _split_start(blocks, [_landing(b) for b in blocks], True, "weights_start_" + group)
        zero = gathers[group][4][0, 0]

    def weights(group, after):
        landed = _split_wait(*gathers[group][:4], after, True, "weights_wait_" + group)
        return [_weight_from_slots(n, g) for n, g in zip(WEIGHT_GROUPS[group], landed)]

    in_flight = []

    def emit(dws):
        names = sorted(dws)
        srcs = [_grad_slots(n, dws[n]) for n in names]
        lands = [_landing(lax.dynamic_index_in_dim(t, _my_index(), 0, keepdims=False)) for t in srcs]
        started = _split_start(srcs, lands, False, "grads_start_" + "_".join(names))
        in_flight.append((names, started))
        return started[4][0, 0]

    shapes = {n: w[n].shape for n in SMALL_NAMES}
    first_piece = {SMALL_NAMES[0]: jnp.zeros(shapes[SMALL_NAMES[0]], _F32)}
    small_flight = []

    def emit_small(gs, sq):
        loss_rows = jnp.broadcast_to(0.5 / D_MODEL * jnp.sum(sq), (PIECE_ROWS, LANES)).astype(_F32)
        buf = _pack_small({**first_piece, **gs}, shapes, loss_rows)
        small_flight.append(_split_start([buf], [_landing(buf)], True, "small_grads_start"))
        return small_flight[0][4][0, 0]

    small_w = {n: w[n] for n in SMALL_NAMES}
    gx, dg1 = _local_grads(x[0], loss_target[0], small_w, weights, emit, emit_small, tm, tk, tb, zero=zero)
    last_buf = dg1.reshape(PIECE_ROWS, LANES)
    last_started = _split_start([last_buf], [_landing(last_buf)], True, "last_grad_start")

    grads, delta, new_m, new_v = {}, {}, {}, {}
    after = last_started[4]
    for names, started in in_flight:
        landed = _split_wait(*started[:4], after, False, "grads_wait_" + "_".join(names))
        for n, parts in zip(names, landed):
            flip = (lambda t: t.T) if n == "w_in" else (lambda t: t)
            res = _sum_adamw(parts, flip(w[n][0]), flip(m[n][0]), flip(v[n][0]), math.gcd(256, parts.shape[1]), "adamw_" + n)
            grads[n], delta[n], new_m[n], new_v[n] = (flip(t)[None] for t in res)
        after = res[1]
    small_parts = _split_wait(*small_flight[0][:4], after, True, "small_grads_wait")[0]
    last_parts = _split_wait(*last_started[:4], small_parts, True, "last_grad_wait")[0]
    small_parts = lax.dynamic_update_slice(small_parts, last_parts, (0, 0, 0))
    sw, sm, sv = _pack_small(w, shapes), _pack_small(m, shapes), _pack_small(v, shapes)
    res = _sum_adamw(small_parts, sw, sm, sv, sw.shape[0], "adamw_small")
    for dst, buf in zip((grads, delta, new_m, new_v), res):
        dst.update(_unpack_small(buf, shapes))
    _, loss_at, _ = _small_layout(shapes)
    loss = res[0][loss_at, 0]

    return (loss, gx[None], *[grads[n] for n in WEIGHT_NAMES], *[delta[n] for n in WEIGHT_NAMES],
            *[new_m[n] for n in WEIGHT_NAMES], *[new_v[n] for n in WEIGHT_NAMES])
```

```python
import math

import jax
import jax.numpy as jnp
from jax import lax
from jax.experimental import pallas as pl
from jax.experimental.pallas import tpu as pltpu

_BF = jnp.bfloat16
_F32 = jnp.float32

D_MODEL = 1024
RET_W = 512
N_HEAD = 4
HEAD_D = 128
CHUNK = 256
ROPE_CHUNK = 128
SSM_W = 512
SSM_GC = 16
N_GROUP = 32
N_STATE = 64
GROUPS_PER_KB = 8
N_KB = 4
KB_STATES = GROUPS_PER_KB * N_STATE
D_FF = 4096
IN_COLS = 2560
NORM_EPS = 1e-6
ROPE_BASE = 10000.0
N_DEV = 8

ADAM_LR = 0.001
ADAM_B1 = 0.9
ADAM_B2 = 0.999
ADAM_EPS = 1e-08
ADAM_WD = 0.01
ADAM_STEP = 10

SUBLANES = 8
LANES = 128
VMEM_LIMIT = 52 * 1024 * 1024
RET_STEP_CHUNKS = 2
KB_PER_STEP = 2
SCAN_UNROLL = True
FIX_UNROLL = 8

MESH = pl.DeviceIdType.MESH


def _params(*sem):
    return pltpu.CompilerParams(dimension_semantics=sem, vmem_limit_bytes=VMEM_LIMIT)


def _dot(a, b):
    return jnp.dot(a, b, preferred_element_type=_F32)


def _dot_nt(a, b):
    return lax.dot_general(a, b, (((1,), (1,)), ((), ())), preferred_element_type=_F32)


def _dot_tn(a, b):
    return lax.dot_general(a, b, (((0,), (0,)), ((), ())), preferred_element_type=_F32)


def _rms_r(z):
    return lax.rsqrt(jnp.mean(z * z, axis=-1, keepdims=True) + NORM_EPS)


def _rms_bwd(z, g, dn):
    r = _rms_r(z)
    t = dn * g
    dz = r * t - z * (r * r * r * jnp.mean(t * z, axis=-1, keepdims=True))
    return dz, dn * z * r


def _rope(t, cs, sn):
    return t * cs + pltpu.roll(t, HEAD_D // 2, 1) * sn


def _rope_t(t, cs, sn):
    return t * cs - pltpu.roll(t, HEAD_D // 2, 1) * sn


def _sigmoid(z):
    return 1.0 / (1.0 + jnp.exp(-z))


_GELU_C = math.sqrt(2.0 / math.pi)


def _gelu(z):
    return 0.5 * z * (1.0 + jnp.tanh(_GELU_C * (z + 0.044715 * z * z * z)))


def _gelu_grad(z):
    th = jnp.tanh(_GELU_C * (z + 0.044715 * z * z * z))
    return 0.5 * (1.0 + th) + 0.5 * z * (1.0 - th * th) * _GELU_C * (1.0 + 3 * 0.044715 * z * z)


ROW_CHUNK = 256


def _row_chunks(tm):
    return [pl.ds(i, min(ROW_CHUNK, tm)) for i in range(0, tm, ROW_CHUNK)]


def _ordered(body, in_specs, operands, after):
    k = len(after)
    if not k:
        return body, list(in_specs), tuple(operands)
    return ((lambda *refs: body(*refs[k:])), [pl.BlockSpec(memory_space=pl.ANY)] * k + list(in_specs),
            tuple(after) + tuple(operands))


def _row_spec(tm, n):
    return pl.BlockSpec((tm, n), lambda i: (i, 0))


def _full_spec(shape):
    nd = len(shape)
    return pl.BlockSpec(shape, lambda *_: (0,) * nd)


def _weight_spec(shape):
    nd = len(shape)
    return pl.BlockSpec(shape, lambda *_: (0,) * nd, pipeline_mode=pl.Buffered(1))


def _rope_tables(L):
    half = HEAD_D // 2
    inv_freq = ROPE_BASE ** (-jnp.arange(half, dtype=_F32) / half)
    twice = lambda t: jnp.concatenate([t, t], axis=-1)
    off = jnp.arange(ROPE_CHUNK, dtype=_F32)[:, None] * inv_freq[None, :]
    start = (ROPE_CHUNK * jnp.arange(L // ROPE_CHUNK, dtype=_F32))[:, None] * inv_freq[None, :]
    return (twice(jnp.cos(off)), twice(jnp.sin(off)),
            twice(jnp.cos(start))[:, None, :], twice(jnp.sin(start))[:, None, :])


def _prenorm(x, g, tm, after=()):
    L = x.shape[0]

    def body(x_ref, g_ref, h_ref):
        xv = x_ref[...]
        h_ref[...] = (xv * _rms_r(xv) * g_ref[...]).astype(_BF)

    body, in_specs, operands = _ordered(body, [_row_spec(tm, D_MODEL), _full_spec((1, D_MODEL))], (x, g), after)
    return pl.pallas_call(
        body, name="prenorm", grid=(L // tm,),
        in_specs=in_specs, out_specs=_row_spec(tm, D_MODEL),
        out_shape=jax.ShapeDtypeStruct((L, D_MODEL), _BF),
        compiler_params=_params("parallel"),
    )(*operands)


def _inproj_fwd(h, w_in_t, rope, tm):
    L = h.shape[0]
    n_chunks = tm // ROPE_CHUNK

    def body(h_ref, w_ref, co_ref, so_ref, cs_ref, ss_ref, q_ref, k_ref, v_ref, gate_ref, u_ref, cos_ref, sin_ref):
        proj = _dot_nt(h_ref[...], w_ref[...])
        lane = lax.broadcasted_iota(jnp.int32, (ROPE_CHUNK, HEAD_D), 1)
        sign = jnp.where(lane < HEAD_D // 2, -1.0, 1.0)
        co, so = co_ref[...], so_ref[...]
        for c in range(n_chunks):
            chunk = pl.program_id(0) * n_chunks + c
            cst, sst = cs_ref[chunk], ss_ref[chunk]
            rows = slice(c * ROPE_CHUNK, (c + 1) * ROPE_CHUNK)
            cs = co * cst - so * sst
            sn = (so * cst + co * sst) * sign
            cos_ref[rows, :] = cs
            sin_ref[rows, :] = sn
            for hh in range(N_HEAD):
                lo = hh * HEAD_D
                q_ref[rows, lo:lo + HEAD_D] = _rope(proj[rows, lo:lo + HEAD_D], cs, sn).astype(_BF)
                kh = _rope(proj[rows, RET_W + lo:RET_W + lo + HEAD_D], cs, sn) * (HEAD_D ** -0.5)
                k_ref[rows, lo:lo + HEAD_D] = kh.astype(_BF)
        v_ref[...] = proj[:, 2 * RET_W:3 * RET_W].astype(_BF)
        gate_ref[...] = proj[:, 3 * RET_W:4 * RET_W]
        u_ref[...] = proj[:, 4 * RET_W:]

    nc = L // ROPE_CHUNK
    return pl.pallas_call(
        body, name="inproj_fwd", grid=(L // tm,),
        in_specs=[_row_spec(tm, D_MODEL), _weight_spec((IN_COLS, D_MODEL)),
                  _full_spec((ROPE_CHUNK, HEAD_D)), _full_spec((ROPE_CHUNK, HEAD_D)),
                  _full_spec((nc, 1, HEAD_D)), _full_spec((nc, 1, HEAD_D))],
        out_specs=[_row_spec(tm, RET_W)] * 5 + [_row_spec(tm, HEAD_D)] * 2,
        out_shape=[jax.ShapeDtypeStruct((L, RET_W), _BF)] * 3 + [jax.ShapeDtypeStruct((L, RET_W), _F32)] * 2
        + [jax.ShapeDtypeStruct((L, HEAD_D), _F32)] * 2,
        compiler_params=_params("parallel"),
    )(h, w_in_t, *rope)


def _ret_consts():
    lg = jnp.log(1.0 - jnp.exp(jnp.linspace(math.log(1.0 / 32), math.log(1.0 / 512), N_HEAD))).astype(_F32)
    idx = jnp.arange(CHUNK, dtype=_F32)
    diff = idx[:, None] - idx[None, :]
    decay = jnp.where(diff[None] >= 0, jnp.exp(jnp.maximum(diff, 0.0)[None] * lg[:, None, None]), 0.0)
    zeta = jnp.exp((CHUNK - 1 - idx)[None, :] * lg[:, None])
    xi = jnp.exp((idx + 1.0)[None, :] * lg[:, None])
    gc = jnp.exp(CHUNK * lg)
    wide = lambda t: jnp.broadcast_to(t[:, :, None], (N_HEAD, CHUNK, HEAD_D)).astype(_F32)
    gcw = jnp.broadcast_to(gc[:, None, None], (N_HEAD, SUBLANES, HEAD_D)).astype(_F32)
    return decay.astype(_F32), wide(xi), wide(zeta), gcw


def _head_specs():
    wide = _full_spec((N_HEAD, CHUNK, HEAD_D))
    return [_full_spec((N_HEAD, CHUNK, CHUNK)), wide, wide, _full_spec((N_HEAD, SUBLANES, HEAD_D))]


def _retention_fwd(q, k, v, gate, ggn, consts):
    L = q.shape[0]
    nc = L // CHUNK
    cps = math.gcd(RET_STEP_CHUNKS, nc)
    blk = pl.BlockSpec((cps * CHUNK, RET_W), lambda n: (n, 0))

    def body(q_ref, k_ref, v_ref, gate_ref, ggn_ref, dm_ref, xi_ref, zeta_ref, gc_ref,
             o_ref, y_ref, rp_ref, r_scr):
        @pl.when(pl.program_id(0) == 0)
        def _():
            r_scr[...] = jnp.zeros_like(r_scr)

        for hh in range(N_HEAD):
            cols = slice(hh * HEAD_D, (hh + 1) * HEAD_D)
            state = r_scr[hh]
            for c in range(cps):
                rows = slice(c * CHUNK, (c + 1) * CHUNK)
                qv, kv, vv = q_ref[rows, cols], k_ref[rows, cols], v_ref[rows, cols]
                s = _dot_nt(qv, kv) * dm_ref[hh]
                o = _dot(s.astype(_BF), vv) + _dot(qv, state.astype(_BF)) * xi_ref[hh]
                o_ref[rows, cols] = o
                rp_ref[hh, c] = state
                vz = (vv.astype(_F32) * zeta_ref[hh]).astype(_BF)
                state = gc_ref[hh, 0:1, :] * state + _dot_tn(kv, vz)
                dlt = o - jnp.mean(o, axis=-1, keepdims=True)
                on = dlt * lax.rsqrt(jnp.mean(dlt * dlt, axis=-1, keepdims=True) + NORM_EPS)
                gt = gate_ref[rows, cols]
                y_ref[rows, cols] = (gt * _sigmoid(gt) * (on * ggn_ref[:, cols])).astype(_BF)
            r_scr[hh] = state

    return pl.pallas_call(
        body, name="retention_fwd", grid=(nc // cps,),
        in_specs=[blk, blk, blk, blk, _full_spec((1, RET_W))] + _head_specs(),
        out_specs=[blk, blk, pl.BlockSpec((N_HEAD, cps, HEAD_D, HEAD_D), lambda n: (0, n, 0, 0))],
        out_shape=[jax.ShapeDtypeStruct((L, RET_W), _F32), jax.ShapeDtypeStruct((L, RET_W), _BF),
                   jax.ShapeDtypeStruct((N_HEAD, nc, HEAD_D, HEAD_D), _F32)],
        scratch_shapes=[pltpu.VMEM((N_HEAD, HEAD_D, HEAD_D), _F32)],
        compiler_params=_params("arbitrary"),
    )(q, k, v, gate, ggn, *consts)


def _rows_to_segments(dst_scr, src_ref, seg):
    for g in range(dst_scr.shape[0]):
        for j in range(SUBLANES):
            dst_scr[g, pl.ds(j, seg, stride=SUBLANES), :] = src_ref[pl.ds(j * seg, seg), g * LANES:(g + 1) * LANES]


def _segments_to_rows(dst_ref, src_scr, seg):
    for g in range(src_scr.shape[0]):
        for j in range(SUBLANES):
            dst_ref[pl.ds(j * seg, seg), g * LANES:(g + 1) * LANES] = src_scr[g, pl.ds(j, seg, stride=SUBLANES), :]


def _scan_segments(x_ref, tab_ref, pw_ref, carry_ref, seg, reverse, entry_ref=None, fwd_ref=None, fwd_entry_ref=None,
                   da_ref=None):
    G = x_ref.shape[0]
    W = KB_STATES
    re, im = pl.ds(0, W), pl.ds(W, W)
    row_id = lax.broadcasted_iota(jnp.int32, (SUBLANES, W), 0)
    edge_in = (row_id == SUBLANES - 1) if reverse else (row_id == 0)
    edge_out = 0 if reverse else SUBLANES - 1
    a_tab = [(tab_ref[g, 0], tab_ref[g, 1]) for g in range(G)]

    def local(i, st):
        r = (seg - 1 - i) if reverse else i
        out = []
        for g in range(G):
            (ar, ai), (sr, si) = a_tab[g], st[g]
            nr = ar * sr - ai * si + x_ref[g, r, :, re]
            ni = ar * si + ai * sr + x_ref[g, r, :, im]
            x_ref[g, r, :, re] = nr
            x_ref[g, r, :, im] = ni
            out.append((nr, ni))
        return tuple(out)

    zero = jnp.zeros((SUBLANES, W), _F32)
    ends = lax.fori_loop(0, seg, local, tuple((zero, zero) for _ in range(G)), unroll=SCAN_UNROLL)

    entry = []
    shift = (SUBLANES - 1) if reverse else 1
    for g in range(G):
        er, ei = ends[g]
        fr = jnp.where(edge_in, carry_ref[g, :, re], pltpu.roll(er, shift, 0))
        fi = jnp.where(edge_in, carry_ref[g, :, im], pltpu.roll(ei, shift, 0))
        for j, dist in enumerate((1, 2, 4)):
            pr, pi = tab_ref[g, 2 + 2 * j], tab_ref[g, 3 + 2 * j]
            sh = (SUBLANES - dist) if reverse else dist
            sr, si = pltpu.roll(fr, sh, 0), pltpu.roll(fi, sh, 0)
            fr, fi = fr + pr * sr - pi * si, fi + pr * si + pi * sr
        br, bi = tab_ref[g, 8], tab_ref[g, 9]
        outr = br * fr - bi * fi + er
        outi = br * fi + bi * fr + ei
        carry_ref[g, :, re] = jnp.broadcast_to(outr[edge_out:edge_out + 1, :], (SUBLANES, W))
        carry_ref[g, :, im] = jnp.broadcast_to(outi[edge_out:edge_out + 1, :], (SUBLANES, W))
        entry.append((fr, fi))
        if entry_ref is not None:
            entry_ref[g, :, re] = fr
            entry_ref[g, :, im] = fi

    add_da = da_ref is not None

    def fix(r, st, first=False):
        out = []
        for g in range(G):
            fr, fi = entry[g]
            pwr, pwi = pw_ref[g, r, :, re], pw_ref[g, r, :, im]
            xr = x_ref[g, r, :, re] + (pwr * fr - pwi * fi)
            xi = x_ref[g, r, :, im] + (pwr * fi + pwi * fr)
            x_ref[g, r, :, re] = xr
            x_ref[g, r, :, im] = xi
            if add_da:
                prev = fwd_entry_ref.at[g] if first else fwd_ref.at[g, r - 1]
                xpr, xpi = prev[:, re], prev[:, im]
                out.append((st[g][0] + (xr * xpr + xi * xpi), st[g][1] + (xi * xpr - xr * xpi)))
            else:
                out.append(st[g])
        return tuple(out)

    if add_da:
        st = fix(0, tuple((zero, zero) for _ in range(G)), first=True)
        st = lax.fori_loop(1, seg, fix, st, unroll=SCAN_UNROLL)
        for g in range(G):
            da_ref[g, :, re] += st[g][0]
            da_ref[g, :, im] += st[g][1]
    else:
        lax.fori_loop(0, seg, fix, tuple((zero[0:1, 0:LANES],) for _ in range(G)), unroll=FIX_UNROLL)


def _s5_specs(seg, time=lambda t: t):
    G = KB_PER_STEP
    return dict(
        x=pl.BlockSpec((G, seg, SUBLANES, 2 * KB_STATES), lambda kb, t: (kb, time(t), 0, 0)),
        ent=pl.BlockSpec((G, 1, SUBLANES, 2 * KB_STATES), lambda kb, t: (kb, time(t), 0, 0)),
        b=pl.BlockSpec((G, LANES, 2 * KB_STATES), lambda kb, t: (kb, 0, 0)),
        c=pl.BlockSpec((G, 2 * KB_STATES, LANES), lambda kb, t: (kb, 0, 0)),
        tab=pl.BlockSpec((G, 10, SUBLANES, KB_STATES), lambda kb, t: (kb, 0, 0, 0)),
        pw=pl.BlockSpec((G, seg, 1, 2 * KB_STATES), lambda kb, t: (kb, 0, 0, 0)),
        d=pl.BlockSpec((1, G * LANES), lambda kb, t: (0, kb)),
    )


def _s5_fwd(u, bmat, cmat, tab_f, pw_f, d_skip, tb):
    L = u.shape[0]
    nt = L // tb
    seg = tb // SUBLANES
    G = KB_PER_STEP
    ucol = pl.BlockSpec((tb, G * LANES), lambda kb, t: (t, kb))
    sp = _s5_specs(seg)

    def body(u_ref, b_ref, c_ref, tab_ref, pw_ref, d_ref, s_ref, x_ref, ent_ref, up_scr, y_scr, carry_scr):
        @pl.when(pl.program_id(1) == 0)
        def _():
            carry_scr[...] = jnp.zeros_like(carry_scr)

        _rows_to_segments(up_scr, u_ref, seg)
        for g in range(G):
            x_ref[g] = _dot(up_scr[g].astype(_BF), b_ref[g]).reshape(seg, SUBLANES, 2 * KB_STATES)
        _scan_segments(x_ref, tab_ref, pw_ref, carry_scr, seg, reverse=False, entry_ref=ent_ref.at[:, 0])
        for g in range(G):
            y = _dot(x_ref[g].reshape(tb, 2 * KB_STATES).astype(_BF), c_ref[g])
            y_scr[g] = y + d_ref[:, g * LANES:(g + 1) * LANES] * up_scr[g]
        _segments_to_rows(s_ref, y_scr, seg)

    return pl.pallas_call(
        body, name="s5_fwd", grid=(N_KB // G, nt),
        in_specs=[ucol, sp["b"], sp["c"], sp["tab"], sp["pw"], sp["d"]],
        out_specs=[ucol, sp["x"], sp["ent"]],
        out_shape=[jax.ShapeDtypeStruct((L, SSM_W), _F32),
                   jax.ShapeDtypeStruct((N_KB, L // SUBLANES, SUBLANES, 2 * KB_STATES), _F32),
                   jax.ShapeDtypeStruct((N_KB, nt, SUBLANES, 2 * KB_STATES), _F32)],
        scratch_shapes=[pltpu.VMEM((G, tb, LANES), _F32)] * 2 + [pltpu.VMEM((G, SUBLANES, 2 * KB_STATES), _F32)],
        compiler_params=_params("parallel", "arbitrary"),
    )(u, bmat, cmat, tab_f, pw_f, d_skip)


def _mixout_fwd(s, y_ret, x, w_glu, w_out, g2, tm):
    L = s.shape[0]

    def body(s_ref, yr_ref, x_ref, wg_ref, wo_ref, g_ref, ys_ref, glu_ref, cat_ref, mix_ref, x2_ref):
        for rows in _row_chunks(tm):
            ys = _gelu(s_ref[rows, :]).astype(_BF)
            ys_ref[rows, :] = ys
            glu = _dot(ys, wg_ref[...])
            glu_ref[rows, :] = glu
            cat_ref[rows, :RET_W] = yr_ref[rows, :]
            cat_ref[rows, RET_W:] = (glu[:, :SSM_W] * _sigmoid(glu[:, SSM_W:])).astype(_BF)
            mix = _dot(cat_ref[rows, :], wo_ref[...])
            mix_ref[rows, :] = mix
            x2_ref[rows, :] = x_ref[rows, :] + mix * _rms_r(mix) * g_ref[...]

    return pl.pallas_call(
        body, name="mixout_fwd", grid=(L // tm,),
        in_specs=[_row_spec(tm, SSM_W), _row_spec(tm, RET_W), _row_spec(tm, D_MODEL),
                  _weight_spec((SSM_W, 2 * SSM_W)), _weight_spec((D_MODEL, D_MODEL)), _full_spec((1, D_MODEL))],
        out_specs=[_row_spec(tm, SSM_W), _row_spec(tm, 2 * SSM_W), _row_spec(tm, D_MODEL),
                   _row_spec(tm, D_MODEL), _row_spec(tm, D_MODEL)],
        out_shape=[jax.ShapeDtypeStruct((L, SSM_W), _BF), jax.ShapeDtypeStruct((L, 2 * SSM_W), _F32),
                   jax.ShapeDtypeStruct((L, D_MODEL), _BF), jax.ShapeDtypeStruct((L, D_MODEL), _F32),
                   jax.ShapeDtypeStruct((L, D_MODEL), _F32)],
        compiler_params=_params("parallel"),
    )(s, y_ret, x, w_glu, w_out, g2)


FF1_COLS = D_FF // N_DEV


def _ff1_fwd(x2, g3, w1, tm):
    L = x2.shape[0]

    def body(x_ref, g_ref, w_ref, h_ref, f_ref):
        xv = x_ref[...]
        h = (xv * _rms_r(xv) * g_ref[...]).astype(_BF)
        h_ref[...] = h
        for j in range(N_DEV):
            f_ref[:, j * FF1_COLS:(j + 1) * FF1_COLS] = _dot(h, w_ref[j])

    return pl.pallas_call(
        body, name="ff1_fwd", grid=(L // tm,),
        in_specs=[_row_spec(tm, D_MODEL), _full_spec((1, D_MODEL)), _weight_spec((N_DEV, D_MODEL, FF1_COLS))],
        out_specs=[_row_spec(tm, D_MODEL), _row_spec(tm, D_FF)],
        out_shape=[jax.ShapeDtypeStruct((L, D_MODEL), _BF), jax.ShapeDtypeStruct((L, D_FF), _F32)],
        compiler_params=_params("parallel"),
    )(x2, g3, w1)


def _ff2_loss(f1, x2, tgt, g4, w2, tm):
    L = f1.shape[0]

    def body(f_ref, x_ref, t_ref, g_ref, w_ref, dy_ref, dm_ref, dg_ref, ls_ref):
        @pl.when(pl.program_id(0) == 0)
        def _():
            dg_ref[...] = jnp.zeros_like(dg_ref)
            ls_ref[...] = jnp.zeros_like(ls_ref)

        g = g_ref[...]
        for rows in _row_chunks(tm):
            rl = jnp.maximum(f_ref[rows, :], 0.0)
            m = _dot((rl * rl).astype(_BF), w_ref[...])
            y = x_ref[rows, :] + m * _rms_r(m) * g
            err = y - t_ref[rows, :]
            ls_ref[...] += jnp.sum(err * err, axis=0, keepdims=True)
            dy = err * (1.0 / D_MODEL)
            dy_ref[rows, :] = dy
            dm, dgr = _rms_bwd(m, g, dy)
            dm_ref[rows, :] = dm.astype(_BF)
            dg_ref[...] += jnp.sum(dgr, axis=0, keepdims=True)

    return pl.pallas_call(
        body, name="ff2_loss", grid=(L // tm,),
        in_specs=[_row_spec(tm, D_FF), _row_spec(tm, D_MODEL), _row_spec(tm, D_MODEL),
                  _full_spec((1, D_MODEL)), _weight_spec((D_FF, D_MODEL))],
        out_specs=[_row_spec(tm, D_MODEL), _row_spec(tm, D_MODEL), _full_spec((1, D_MODEL)), _full_spec((1, D_MODEL))],
        out_shape=[jax.ShapeDtypeStruct((L, D_MODEL), _F32), jax.ShapeDtypeStruct((L, D_MODEL), _BF),
                   jax.ShapeDtypeStruct((1, D_MODEL), _F32), jax.ShapeDtypeStruct((1, D_MODEL), _F32)],
        compiler_params=_params("arbitrary"),
    )(f1, x2, tgt, g4, w2)


def _ff2_bwd(dm, f1, w2, tm, tn):
    L = dm.shape[0]
    last = L // tm - 1

    def body(dm_ref, f_ref, w_ref, df_ref, dw_ref, acc):
        @pl.when(pl.program_id(1) == 0)
        def _():
            acc[...] = jnp.zeros_like(acc)

        dmv = dm_ref[...]
        rl = jnp.maximum(f_ref[...], 0.0)
        df_ref[...] = (_dot_nt(dmv, w_ref[...]) * (2.0 * rl)).astype(_BF)
        acc[...] += _dot_tn((rl * rl).astype(_BF), dmv)

        @pl.when(pl.program_id(1) == last)
        def _():
            dw_ref[...] = acc[...].astype(_BF)

    return pl.pallas_call(
        body, name="ff2_bwd", grid=(D_FF // tn, L // tm),
        in_specs=[pl.BlockSpec((tm, D_MODEL), lambda j, i: (i, 0)), pl.BlockSpec((tm, tn), lambda j, i: (i, j)),
                  pl.BlockSpec((tn, D_MODEL), lambda j, i: (j, 0))],
        out_specs=[pl.BlockSpec((tm, tn), lambda j, i: (i, j)), pl.BlockSpec((tn, D_MODEL), lambda j, i: (j, 0))],
        out_shape=[jax.ShapeDtypeStruct((L, D_FF), _BF), jax.ShapeDtypeStruct((D_FF, D_MODEL), _BF)],
        scratch_shapes=[pltpu.VMEM((tn, D_MODEL), _F32)],
        compiler_params=_params("parallel", "arbitrary"),
    )(dm, f1, w2)


def _ff1_bwd(df1, w1, x2, mix, dy, g3, g2, tm):
    L = df1.shape[0]

    def body(df_ref, w_ref, x2_ref, mix_ref, dy_ref, g3_ref, g2_ref, dx2_ref, dmix_ref, dg3_ref, dg2_ref):
        @pl.when(pl.program_id(0) == 0)
        def _():
            dg3_ref[...] = jnp.zeros_like(dg3_ref)
            dg2_ref[...] = jnp.zeros_like(dg2_ref)

        for rows in _row_chunks(tm):
            dh = _dot_nt(df_ref[rows, 0:FF1_COLS], w_ref[0])
            for j in range(1, N_DEV):
                dh = dh + _dot_nt(df_ref[rows, j * FF1_COLS:(j + 1) * FF1_COLS], w_ref[j])
            dz, dgr = _rms_bwd(x2_ref[rows, :], g3_ref[...], dh)
            dg3_ref[...] += jnp.sum(dgr, axis=0, keepdims=True)
            dx2 = dy_ref[rows, :] + dz
            dx2_ref[rows, :] = dx2
            dmx, dgr2 = _rms_bwd(mix_ref[rows, :], g2_ref[...], dx2)
            dg2_ref[...] += jnp.sum(dgr2, axis=0, keepdims=True)
            dmix_ref[rows, :] = dmx.astype(_BF)

    vec = _full_spec((1, D_MODEL))
    return pl.pallas_call(
        body, name="ff1_bwd", grid=(L // tm,),
        in_specs=[_row_spec(tm, D_FF), _weight_spec((N_DEV, D_MODEL, FF1_COLS)), _row_spec(tm, D_MODEL),
                  _row_spec(tm, D_MODEL), _row_spec(tm, D_MODEL), vec, vec],
        out_specs=[_row_spec(tm, D_MODEL), _row_spec(tm, D_MODEL), vec, vec],
        out_shape=[jax.ShapeDtypeStruct((L, D_MODEL), _F32), jax.ShapeDtypeStruct((L, D_MODEL), _BF),
                   jax.ShapeDtypeStruct((1, D_MODEL), _F32), jax.ShapeDtypeStruct((1, D_MODEL), _F32)],
        compiler_params=_params("arbitrary"),
    )(df1, w1, x2, mix, dy, g3, g2)


def _matmul_tn(a, b, tm, tn, name, slots=False):
    L, K = a.shape
    N = b.shape[1]
    last = L // tm - 1

    def body(a_ref, b_ref, o_ref, acc):
        @pl.when(pl.program_id(1) == 0)
        def _():
            acc[...] = jnp.zeros_like(acc)

        acc[...] += _dot_tn(a_ref[...].astype(_BF), b_ref[...].astype(_BF))

        @pl.when(pl.program_id(1) == last)
        def _():
            if slots:
                o_ref[0] = acc[...].astype(_BF)
            else:
                o_ref[...] = acc[...].astype(_BF)

    if slots:
        out_spec = pl.BlockSpec((1, K, tn), lambda j, i: (j, 0, 0))
        out_shape = jax.ShapeDtypeStruct((N // tn, K, tn), _BF)
    else:
        out_spec = pl.BlockSpec((K, tn), lambda j, i: (0, j))
        out_shape = jax.ShapeDtypeStruct((K, N), _BF)
    return pl.pallas_call(
        body, name=name, grid=(N // tn, L // tm),
        in_specs=[pl.BlockSpec((tm, K), lambda j, i: (i, 0)), pl.BlockSpec((tm, tn), lambda j, i: (i, j))],
        out_specs=out_spec, out_shape=out_shape,
        scratch_shapes=[pltpu.VMEM((K, tn), _F32)],
        compiler_params=_params("parallel", "arbitrary"),
    )(a, b)


def _dw_in_t(pieces, h, tk):
    L = h.shape[0]
    last = L // tk - 1

    def body(p0, p1, p2, p3, p4, h_ref, o_ref, acc):
        @pl.when(pl.program_id(0) == 0)
        def _():
            acc[...] = jnp.zeros_like(acc)

        hv = h_ref[...]
        for j, p in enumerate((p0, p1, p2, p3, p4)):
            acc[j * RET_W:(j + 1) * RET_W, :] += _dot_tn(p[...].astype(_BF), hv)

        @pl.when(pl.program_id(0) == last)
        def _():
            o_ref[...] = acc[...].astype(_BF)

    return pl.pallas_call(
        body, name="dw_in", grid=(L // tk,),
        in_specs=[_row_spec(tk, RET_W)] * 5 + [_row_spec(tk, D_MODEL)],
        out_specs=_full_spec((IN_COLS, D_MODEL)), out_shape=jax.ShapeDtypeStruct((IN_COLS, D_MODEL), _BF),
        scratch_shapes=[pltpu.VMEM((IN_COLS, D_MODEL), _F32)],
        compiler_params=_params("arbitrary"),
    )(*pieces, h)


def _mixout_bwd(dmix, w_out, w_glu, glu, s, o, gate, ggn, tm, after=()):
    L = dmix.shape[0]

    def body(dmix_ref, wo_ref, wg_ref, glu_ref, s_ref, o_ref, gate_ref, ggn_ref,
             dglu_ref, ds_ref, dgate_ref, do_ref, dggn_ref):
        @pl.when(pl.program_id(0) == 0)
        def _():
            dggn_ref[...] = jnp.zeros_like(dggn_ref)

        dcat = _dot_nt(dmix_ref[...], wo_ref[...])
        dy_ret, dy_ssm = dcat[:, :RET_W], dcat[:, RET_W:]
        glu = glu_ref[...]
        ga, sg = glu[:, :SSM_W], _sigmoid(glu[:, SSM_W:])
        dga = (dy_ssm * sg).astype(_BF)
        dgb = (dy_ssm * ga * sg * (1.0 - sg)).astype(_BF)
        dglu_ref[:, :SSM_W] = dga
        dglu_ref[:, SSM_W:] = dgb
        dys = _dot_nt(dga, wg_ref[:, :SSM_W]) + _dot_nt(dgb, wg_ref[:, SSM_W:])
        ds_ref[...] = dys * _gelu_grad(s_ref[...])
        gt = gate_ref[...]
        sgt = _sigmoid(gt)
        ggn = ggn_ref[...]
        for hh in range(N_HEAD):
            cols = slice(hh * HEAD_D, (hh + 1) * HEAD_D)
            ov = o_ref[:, cols]
            dlt = ov - jnp.mean(ov, axis=-1, keepdims=True)
            rstd = lax.rsqrt(jnp.mean(dlt * dlt, axis=-1, keepdims=True) + NORM_EPS)
            on = dlt * rstd
            dyr = dy_ret[:, cols] * (gt[:, cols] * sgt[:, cols])
            dgate_ref[:, cols] = dy_ret[:, cols] * (on * ggn[:, cols]) * (sgt[:, cols] * (1.0 + gt[:, cols] * (1.0 - sgt[:, cols])))
            dggn_ref[:, cols] += jnp.sum(dyr * on, axis=0, keepdims=True)
            don = dyr * ggn[:, cols]
            do = rstd * (don - jnp.mean(don, axis=-1, keepdims=True) - on * jnp.mean(don * on, axis=-1, keepdims=True))
            do_ref[:, cols] = do.astype(_BF)

    body, in_specs, operands = _ordered(
        body, [_row_spec(tm, D_MODEL), _weight_spec((D_MODEL, D_MODEL)), _weight_spec((SSM_W, 2 * SSM_W)),
               _row_spec(tm, 2 * SSM_W), _row_spec(tm, SSM_W), _row_spec(tm, RET_W), _row_spec(tm, RET_W),
               _full_spec((1, RET_W))], (dmix, w_out, w_glu, glu, s, o, gate, ggn), after)
    return pl.pallas_call(
        body, name="mixout_bwd", grid=(L // tm,),
        in_specs=in_specs,
        out_specs=[_row_spec(tm, 2 * SSM_W), _row_spec(tm, SSM_W), _row_spec(tm, RET_W), _row_spec(tm, RET_W),
                   _full_spec((1, RET_W))],
        out_shape=[jax.ShapeDtypeStruct((L, 2 * SSM_W), _BF), jax.ShapeDtypeStruct((L, SSM_W), _F32),
                   jax.ShapeDtypeStruct((L, RET_W), _F32), jax.ShapeDtypeStruct((L, RET_W), _BF),
                   jax.ShapeDtypeStruct((1, RET_W), _F32)],
        compiler_params=_params("arbitrary"),
    )(*operands)


def _s5_bwd(u, ds, xs, ent, bmat, cmat, tab_r, pw_r, d_skip, tb, after=()):
    L = u.shape[0]
    nt = L // tb
    seg = tb // SUBLANES
    G = KB_PER_STEP
    rcol = pl.BlockSpec((tb, G * LANES), lambda kb, t: (nt - 1 - t, kb))
    sp = _s5_specs(seg, time=lambda t: nt - 1 - t)
    aspec = pl.BlockSpec((G, SUBLANES, 2 * KB_STATES), lambda kb, t: (kb, 0, 0))

    def body(u_ref, ds_ref, x_ref, ent_ref, b_ref, c_ref, tr_ref, pr_ref, d_ref,
             du_ref, db_ref, dc_ref, da_ref, dd_ref, up_scr, dp_scr, g_scr, lc_scr):
        @pl.when(pl.program_id(1) == 0)
        def _():
            lc_scr[...] = jnp.zeros_like(lc_scr)
            db_ref[...] = jnp.zeros_like(db_ref)
            dc_ref[...] = jnp.zeros_like(dc_ref)
            da_ref[...] = jnp.zeros_like(da_ref)
            dd_ref[...] = jnp.zeros_like(dd_ref)

        _rows_to_segments(up_scr, u_ref, seg)
        _rows_to_segments(dp_scr, ds_ref, seg)
        for g in range(G):
            g_scr[g] = _dot_nt(dp_scr[g].astype(_BF), c_ref[g]).reshape(seg, SUBLANES, 2 * KB_STATES)
        _scan_segments(g_scr, tr_ref, pr_ref, lc_scr, seg, reverse=True, fwd_ref=x_ref, fwd_entry_ref=ent_ref.at[:, 0],
                       da_ref=da_ref)
        for g in range(G):
            cols = slice(g * LANES, (g + 1) * LANES)
            uv, dsv = up_scr[g], dp_scr[g]
            ub, dsb = uv.astype(_BF), dsv.astype(_BF)
            lamb = g_scr[g].reshape(tb, 2 * KB_STATES).astype(_BF)
            db_ref[g] += _dot_tn(ub, lamb)
            dc_ref[g] += _dot_tn(dsb, x_ref[g].reshape(tb, 2 * KB_STATES).astype(_BF))
            dd_ref[:, cols] += jnp.sum(dsv * uv, axis=0, keepdims=True)
            up_scr[g] = _dot_nt(lamb, b_ref[g]) + d_ref[:, cols] * dsv
        _segments_to_rows(du_ref, up_scr, seg)

    body, in_specs, operands = _ordered(
        body, [rcol, rcol, sp["x"], sp["ent"], sp["b"], sp["c"], sp["tab"], sp["pw"], sp["d"]],
        (u, ds, xs, ent, bmat, cmat, tab_r, pw_r, d_skip), after)
    return pl.pallas_call(
        body, name="s5_bwd", grid=(N_KB // G, nt),
        in_specs=in_specs,
        out_specs=[rcol, sp["b"], sp["b"], aspec, sp["d"]],
        out_shape=[jax.ShapeDtypeStruct((L, SSM_W), _F32),
                   jax.ShapeDtypeStruct((N_KB, LANES, 2 * KB_STATES), _F32),
                   jax.ShapeDtypeStruct((N_KB, LANES, 2 * KB_STATES), _F32),
                   jax.ShapeDtypeStruct((N_KB, SUBLANES, 2 * KB_STATES), _F32),
                   jax.ShapeDtypeStruct((1, SSM_W), _F32)],
        scratch_shapes=[pltpu.VMEM((G, tb, LANES), _F32)] * 2
        + [pltpu.VMEM((G, seg, SUBLANES, 2 * KB_STATES), _F32), pltpu.VMEM((G, SUBLANES, 2 * KB_STATES), _F32)],
        compiler_params=_params("parallel", "arbitrary"),
    )(*operands)


def _retention_bwd(q, k, v, do, r_prev, consts, cosf, sinf):
    L = q.shape[0]
    nc = L // CHUNK
    cps = math.gcd(RET_STEP_CHUNKS, nc)
    nb = nc // cps
    blk = pl.BlockSpec((cps * CHUNK, RET_W), lambda n: (nb - 1 - n, 0))
    rope_blk = pl.BlockSpec((cps * CHUNK, HEAD_D), lambda n: (nb - 1 - n, 0))

    def body(q_ref, k_ref, v_ref, do_ref, rp_ref, dm_ref, xi_ref, zeta_ref, gc_ref, cos_ref, sin_ref,
             dq_ref, dk_ref, dv_ref, g_scr):
        @pl.when(pl.program_id(0) == 0)
        def _():
            g_scr[...] = jnp.zeros_like(g_scr)

        for hh in range(N_HEAD):
            cols = slice(hh * HEAD_D, (hh + 1) * HEAD_D)
            dm, zeta = dm_ref[hh], zeta_ref[hh]
            gst = g_scr[hh]
            for c in reversed(range(cps)):
                rows = slice(c * CHUNK, (c + 1) * CHUNK)
                qv, kv, vv, dov = q_ref[rows, cols], k_ref[rows, cols], v_ref[rows, cols], do_ref[rows, cols]
                rb = rp_ref[hh, c].astype(_BF)
                gb = gst.astype(_BF)
                sb = (_dot_nt(qv, kv) * dm).astype(_BF)
                dab = (_dot_nt(dov, vv) * dm).astype(_BF)
                dox = (dov.astype(_F32) * xi_ref[hh]).astype(_BF)
                vz = (vv.astype(_F32) * zeta).astype(_BF)
                dq = _dot(dab, kv) + _dot_nt(dox, rb)
                dk = _dot_tn(dab, qv) + _dot_nt(vz, gb)
                dv = _dot_tn(sb, dov) + _dot(kv, gb) * zeta
                gst = gc_ref[hh, 0:1, :] * gst + _dot_tn(qv, dox)
                cs, sn = cos_ref[rows, :], sin_ref[rows, :]
                dq_ref[rows, cols] = _rope_t(dq, cs, sn).astype(_BF)
                dk_ref[rows, cols] = (_rope_t(dk, cs, sn) * (HEAD_D ** -0.5)).astype(_BF)
                dv_ref[rows, cols] = dv.astype(_BF)
            g_scr[hh] = gst

    return pl.pallas_call(
        body, name="retention_bwd", grid=(nb,),
        in_specs=[blk, blk, blk, blk, pl.BlockSpec((N_HEAD, cps, HEAD_D, HEAD_D), lambda n: (0, nb - 1 - n, 0, 0))]
        + _head_specs() + [rope_blk, rope_blk],
        out_specs=[blk, blk, blk],
        out_shape=[jax.ShapeDtypeStruct((L, RET_W), _BF)] * 3,
        scratch_shapes=[pltpu.VMEM((N_HEAD, HEAD_D, HEAD_D), _F32)],
        compiler_params=_params("arbitrary"),
    )(q, k, v, do, r_prev, *consts, cosf, sinf)


def _inproj_bwd(pieces, w_in_t, x, dx2, g1, tm, after=()):
    L = x.shape[0]

    def body(p0, p1, p2, p3, p4, w_ref, x_ref, dx2_ref, g_ref, dx_ref, dg_ref):
        @pl.when(pl.program_id(0) == 0)
        def _():
            dg_ref[...] = jnp.zeros_like(dg_ref)

        for rows in _row_chunks(tm):
            dh = None
            for j, p in enumerate((p0, p1, p2, p3, p4)):
                part = _dot(p[rows, :].astype(_BF), w_ref[j * RET_W:(j + 1) * RET_W, :])
                dh = part if dh is None else dh + part
            dz, dgr = _rms_bwd(x_ref[rows, :], g_ref[...], dh)
            dx_ref[rows, :] = dx2_ref[rows, :] + dz
            dg_ref[...] += jnp.sum(dgr, axis=0, keepdims=True)

    body, in_specs, operands = _ordered(
        body, [_row_spec(tm, RET_W)] * 5 + [_weight_spec((IN_COLS, D_MODEL)), _row_spec(tm, D_MODEL),
                                             _row_spec(tm, D_MODEL), _full_spec((1, D_MODEL))],
        (*pieces, w_in_t, x, dx2, g1), after)
    return pl.pallas_call(
        body, name="inproj_bwd", grid=(L // tm,),
        in_specs=in_specs,
        out_specs=[_row_spec(tm, D_MODEL), _full_spec((1, D_MODEL))],
        out_shape=[jax.ShapeDtypeStruct((L, D_MODEL), _F32), jax.ShapeDtypeStruct((1, D_MODEL), _F32)],
        compiler_params=_params("arbitrary"),
    )(*operands)


def _sum_adamw(parts, w, m, v, tr, name):
    _, R, Cc = parts.shape

    def body(p_ref, w_ref, m_ref, v_ref, g_ref, d_ref, nm_ref, nv_ref):
        gv = p_ref[0].astype(_F32)
        for s in range(1, N_DEV):
            gv = gv + p_ref[s].astype(_F32)
        g_ref[...] = gv
        nm = ADAM_B1 * m_ref[...] + (1.0 - ADAM_B1) * gv
        nv = ADAM_B2 * v_ref[...] + (1.0 - ADAM_B2) * (gv * gv)
        m_hat = nm / (1.0 - ADAM_B1 ** ADAM_STEP)
        v_hat = nv / (1.0 - ADAM_B2 ** ADAM_STEP)
        d_ref[...] = -ADAM_LR * (m_hat / (jnp.sqrt(v_hat) + ADAM_EPS) + ADAM_WD * w_ref[...])
        nm_ref[...] = nm
        nv_ref[...] = nv

    spec = _row_spec(tr, Cc)
    return pl.pallas_call(
        body, name=name, grid=(R // tr,),
        in_specs=[pl.BlockSpec((N_DEV, tr, Cc), lambda i: (0, i, 0))] + [spec] * 3, out_specs=[spec] * 4,
        out_shape=[jax.ShapeDtypeStruct((R, Cc), _F32)] * 4,
        compiler_params=_params("parallel"),
    )(parts, w, m, v)


def _my_place():
    return lax.axis_index("x"), lax.axis_index("y"), lax.axis_index("c")


def _all_gather(blocks):
    n = len(blocks)

    def body(*refs):
        x_refs, out_refs, done_ref = refs[:n], refs[n:2 * n], refs[2 * n]
        send_sems, recv_sems, local_sems = refs[2 * n + 1:]
        done_ref[...] = jnp.zeros_like(done_ref)
        x, y, c = _my_place()
        me, sibling = (x, y, c), (x, y, 1 - c)
        chips = [(1 - x, y), (x, 1 - y), (1 - x, 1 - y)]

        def slot(a, px, py, pc):
            return out_refs[a].at[4 * px + 2 * py + pc]

        def copy(a, k, blk, to, own=False):
            return pltpu.make_async_remote_copy(
                src_ref=x_refs[a] if own else slot(a, *blk), dst_ref=slot(a, *blk),
                send_sem=send_sems.at[a, k], recv_sem=recv_sems.at[a, k], device_id=to, device_id_type=MESH)

        mine = [pltpu.make_async_copy(x_refs[a], slot(a, *me), local_sems.at[a]) for a in range(n)]
        for cp in mine:
            cp.start()
        first = []
        for a in range(n):
            first.append(copy(a, 0, me, sibling, own=True))
            first += [copy(a, 1 + j, me, (*chip, c), own=True) for j, chip in enumerate(chips)]
        for cp in first:
            cp.start()
        passed = []
        for j, chip in enumerate(chips):
            for a in range(n):
                copy(a, 1 + j, (*chip, c), me).wait_recv()
                fwd = copy(a, 4 + j, (*chip, c), sibling)
                fwd.start()
                passed.append(fwd)
        for a in range(n):
            copy(a, 0, sibling, me).wait_recv()
            for j, chip in enumerate(chips):
                copy(a, 4 + j, (*chip, 1 - c), me).wait_recv()
        for cp in first + passed:
            cp.wait_send()
        for cp in mine:
            cp.wait()

    any_spec = pl.BlockSpec(memory_space=pl.ANY)
    outs = pl.pallas_call(
        body, name="weights_all_gather",
        in_specs=[any_spec] * n, out_specs=[any_spec] * n + [pl.BlockSpec(memory_space=pltpu.VMEM)],
        out_shape=[jax.ShapeDtypeStruct((N_DEV,) + b.shape, b.dtype) for b in blocks]
        + [jax.ShapeDtypeStruct((SUBLANES, LANES), _F32)],
        scratch_shapes=[pltpu.SemaphoreType.DMA((n, 7)), pltpu.SemaphoreType.DMA((n, 7)), pltpu.SemaphoreType.DMA((n,))],
    )(*blocks)
    return outs[:n], outs[n]


def _exchange(bigs, small):
    n = len(bigs)
    r = small.shape[0]

    def body(*refs):
        in_refs, out_refs = refs[:n + 1], refs[n + 1:2 * n + 2]
        send_sems, recv_sems, local_sems = refs[2 * n + 2:]
        x, y, c = _my_place()
        me = 4 * x + 2 * y + c
        own = [pltpu.make_async_copy(in_refs[a].at[me], out_refs[a].at[me], local_sems.at[a]) for a in range(n)]
        own.append(pltpu.make_async_copy(in_refs[n], out_refs[n].at[me], local_sems.at[n]))
        for cp in own:
            cp.start()
        copies = []
        for kk in range(1, N_DEV):
            px, py, pc = x ^ (kk >> 2), y ^ ((kk >> 1) & 1), c ^ (kk & 1)
            peer = 4 * px + 2 * py + pc
            for a in range(n + 1):
                src = in_refs[a].at[peer] if a < n else in_refs[a]
                copies.append(pltpu.make_async_remote_copy(
                    src_ref=src, dst_ref=out_refs[a].at[me],
                    send_sem=send_sems.at[a, kk - 1], recv_sem=recv_sems.at[a, kk - 1],
                    device_id=(px, py, pc), device_id_type=MESH))
        for cp in copies:
            cp.start()
        for cp in copies:
            cp.wait_recv()
        for cp in copies:
            cp.wait_send()
        for cp in own:
            cp.wait()

    any_spec = pl.BlockSpec(memory_space=pl.ANY)
    outs = pl.pallas_call(
        body, name="grad_exchange",
        in_specs=[any_spec] * (n + 1), out_specs=[any_spec] * (n + 1),
        out_shape=[jax.ShapeDtypeStruct(b.shape, b.dtype) for b in bigs]
        + [jax.ShapeDtypeStruct((N_DEV, r, LANES), small.dtype)],
        scratch_shapes=[pltpu.SemaphoreType.DMA((n + 1, 7)), pltpu.SemaphoreType.DMA((n + 1, 7)),
                        pltpu.SemaphoreType.DMA((n + 1,))],
    )(*bigs, small)
    return outs[:n], outs[n]


HBM_SPEC = pl.BlockSpec(memory_space=pltpu.HBM)
SEM_SPEC = pl.BlockSpec(memory_space=pltpu.SEMAPHORE)
DATAFLOW = pltpu.SideEffectType.DATAFLOW_SIDE_EFFECTING


def _my_index():
    x, y, c = _my_place()
    return 4 * x + 2 * y + c


def _landing(own_block):
    zone = lax.empty((N_DEV,) + own_block.shape, own_block.dtype)
    return lax.dynamic_update_index_in_dim(zone, own_block, _my_index(), 0)


def _split_copies(src_refs, land_refs, send_sems, recv_sems, gather, first=0):
    x, y, c = _my_place()
    me = 4 * x + 2 * y + c
    copies = []
    for kk in range(1, N_DEV):
        px, py, pc = x ^ (kk >> 2), y ^ ((kk >> 1) & 1), c ^ (kk & 1)
        peer = 4 * px + 2 * py + pc
        for a, (src, land) in enumerate(zip(src_refs, land_refs)):
            copies.append(pltpu.make_async_remote_copy(
                src_ref=src if gather else src.at[peer], dst_ref=land.at[me],
                send_sem=send_sems.at[(first + a) * 7 + kk - 1], recv_sem=recv_sems.at[(first + a) * 7 + kk - 1],
                device_id=(px, py, pc), device_id_type=MESH))
    return copies


def _split_start(srcs, lands, gather, name):
    n = len(srcs)

    def body(*refs):
        src_refs, land_refs = refs[:n], refs[n:2 * n]
        send_sems, recv_sems = refs[2 * n], refs[2 * n + 1]
        token = refs[-1]
        for cp in _split_copies(src_refs, land_refs, send_sems, recv_sems, gather):
            cp.start()
        token[...] = jnp.zeros_like(token)

    outs = pl.pallas_call(
        body, name=name,
        out_shape=(pltpu.SemaphoreType.DMA((7 * n,)), pltpu.SemaphoreType.DMA((7 * n,)),
                   *[pltpu.HBM(t.shape, t.dtype) for t in srcs], *[pltpu.HBM(t.shape, t.dtype) for t in lands],
                   jax.ShapeDtypeStruct((SUBLANES, LANES), _F32)),
        in_specs=[HBM_SPEC] * (2 * n),
        out_specs=(SEM_SPEC, SEM_SPEC, *[HBM_SPEC] * (2 * n), pl.BlockSpec(memory_space=pltpu.VMEM)),
        input_output_aliases={i: 2 + i for i in range(2 * n)},
        compiler_params=pltpu.CompilerParams(has_side_effects=DATAFLOW),
    )(*[pltpu.with_memory_space_constraint(t, pltpu.HBM) for t in list(srcs) + list(lands)])
    return outs[0], outs[1], outs[2:2 + n], outs[2 + n:2 + 2 * n], outs[-1]


def _split_wait(send_sems, recv_sems, srcs, lands, after, gather, name, first=0):
    n = len(srcs)

    def body(*refs):
        src_refs, land_refs = refs[:n], refs[n:2 * n]
        send_s, recv_s = refs[2 * n], refs[2 * n + 1]
        for cp in _split_copies(src_refs, land_refs, send_s, recv_s, gather, first):
            cp.wait_send()
            cp.wait_recv()

    outs = pl.pallas_call(
        body, name=name,
        out_shape=tuple(pltpu.HBM(t.shape, t.dtype) for t in list(srcs) + list(lands)),
        in_specs=[HBM_SPEC] * (2 * n) + [SEM_SPEC, SEM_SPEC, pl.BlockSpec(memory_space=pl.ANY)],
        out_specs=tuple([HBM_SPEC] * (2 * n)),
        input_output_aliases={i: i for i in range(2 * n)},
        compiler_params=pltpu.CompilerParams(has_side_effects=DATAFLOW),
    )(*srcs, *lands, send_sems, recv_sems, after)
    return outs[n:]


def _discretize(lam_re, lam_im, log_dt, b_re, b_im):
    lr = jnp.minimum(lam_re, -1e-4)
    li = lam_im
    dt = jnp.exp(log_dt)[:, None]
    er = jnp.exp(lr * dt)
    ar, ai = er * jnp.cos(li * dt), er * jnp.sin(li * dt)
    den = lr * lr + li * li
    cr = ((ar - 1.0) * lr + ai * li) / den
    ci = (ai * lr - (ar - 1.0) * li) / den
    bbr = cr[:, :, None] * b_re - ci[:, :, None] * b_im
    bbi = cr[:, :, None] * b_im + ci[:, :, None] * b_re
    return ar, ai, bbr, bbi


def _cmul(ar, ai, br, bi):
    return ar * br - ai * bi, ar * bi + ai * br


def _cpowers(ar, ai, n):
    pr, pi = ar[None], ai[None]
    while pr.shape[0] < n:
        nr, ni = _cmul(pr, pi, pr[-1][None], pi[-1][None])
        pr, pi = jnp.concatenate([pr, nr]), jnp.concatenate([pi, ni])
    return pr[:n], pi[:n]


def _scan_tables(ar, ai, seg, reverse):
    if reverse:
        ai = -ai
    ar, ai = ar.reshape(N_KB, KB_STATES), ai.reshape(N_KB, KB_STATES)
    pr, pi = _cpowers(ar, ai, seg)
    a1 = (pr[-1], pi[-1])
    a2 = _cmul(*a1, *a1)
    a4 = _cmul(*a2, *a2)
    row = jnp.arange(SUBLANES)[None, :, None]
    wide = lambda t: jnp.broadcast_to(t[:, None, :], (N_KB, SUBLANES, KB_STATES))
    tabs = [wide(ar), wide(ai)]
    for dist, (qr, qi) in ((1, a1), (2, a2), (4, a4)):
        keep = (row < SUBLANES - dist) if reverse else (row >= dist)
        tabs += [jnp.where(keep, wide(qr), 0.0), jnp.where(keep, wide(qi), 0.0)]
    tabs += [wide(a1[0]), wide(a1[1])]
    if reverse:
        pr, pi = pr[::-1], pi[::-1]
    pw = jnp.transpose(jnp.concatenate([pr, pi], axis=-1), (1, 0, 2))[:, :, None, :]
    return jnp.stack(tabs, axis=1).astype(_F32), pw.astype(_F32)


def _block_diag_in(br, bi):
    eye = jnp.eye(GROUPS_PER_KB, dtype=_F32)
    one = lambda t: jnp.einsum("kgpc,gh->kgchp", t.reshape(N_KB, GROUPS_PER_KB, N_STATE, SSM_GC), eye).reshape(
        N_KB, LANES, KB_STATES)
    return jnp.concatenate([one(br), one(bi)], axis=-1)


def _block_diag_in_t(dmat):
    d6 = dmat.reshape(N_KB, GROUPS_PER_KB, SSM_GC, 2, GROUPS_PER_KB, N_STATE)
    eye = jnp.eye(GROUPS_PER_KB, dtype=_F32)
    both = jnp.einsum("kgcrhp,gh->rkgpc", d6, eye).reshape(2, N_GROUP, N_STATE, SSM_GC)
    return both[0], both[1]


def _block_diag_out(c_re, c_im):
    eye = jnp.eye(GROUPS_PER_KB, dtype=_F32)
    one = lambda t: jnp.einsum("kgcp,gh->khpgc", t.reshape(N_KB, GROUPS_PER_KB, SSM_GC, N_STATE), eye).reshape(
        N_KB, KB_STATES, LANES)
    return jnp.concatenate([one(c_re), -one(c_im)], axis=1)


def _block_diag_out_t(dmat_t):
    d6 = dmat_t.reshape(N_KB, GROUPS_PER_KB, SSM_GC, 2, GROUPS_PER_KB, N_STATE)
    eye = jnp.eye(GROUPS_PER_KB, dtype=_F32)
    both = jnp.einsum("kgcrhp,gh->rkgcp", d6, eye).reshape(2, N_GROUP, SSM_GC, N_STATE)
    return both[0], -both[1]


SMALL_NAMES = ("norm_mix_pre", "norm_mix_post", "ret_gn_gain", "ssm_lambda_re", "ssm_lambda_im", "ssm_log_dt",
               "ssm_b_re", "ssm_b_im", "ssm_c_re", "ssm_c_im", "ssm_d", "norm_mlp_pre", "norm_mlp_post")


def _local_grads(x, tgt, small, weights, emit, emit_small, tm, tk, tb, zero=0.0):
    L = x.shape[0]
    g1, g2, ggn = small["norm_mix_pre"], small["norm_mix_post"], small["ret_gn_gain"]
    g3, g4, d_skip = small["norm_mlp_pre"], small["norm_mlp_post"], small["ssm_d"]

    rope = _rope_tables(L)
    consts = _ret_consts()

    disc_in = (small["ssm_lambda_re"][0], small["ssm_lambda_im"][0], small["ssm_log_dt"][0] + zero,
               small["ssm_b_re"][0], small["ssm_b_im"][0])
    (ar, ai, bbr, bbi), disc_vjp = jax.vjp(_discretize, *disc_in)
    bmat = _block_diag_in(bbr, bbi).astype(_BF)
    cmat = _block_diag_out(small["ssm_c_re"][0], small["ssm_c_im"][0]).astype(_BF)
    seg = tb // SUBLANES
    tab_f, pw_f = _scan_tables(ar, ai, seg, False)
    tab_r, pw_r = _scan_tables(ar, ai, seg, True)

    h1 = _prenorm(x, g1, min(2 * tm, L), after=(pw_r,))
    (w_in_t,) = weights("in", h1)
    q, k, v, gate, u, cosf, sinf = _inproj_fwd(h1, w_in_t, rope, tm)
    o, y_ret, r_prev = _retention_fwd(q, k, v, gate, ggn, consts)
    s, xs, ent = _s5_fwd(u, bmat, cmat, tab_f, pw_f, d_skip, tb)
    w_glu, w_out = weights("mix", s)
    ys, glu, cat, mix, x2 = _mixout_fwd(s, y_ret, x, w_glu, w_out, g2, min(2 * tm, L))
    w_ff1, w_ff2 = weights("mlp", x2)
    h3, f1 = _ff1_fwd(x2, g3, w_ff1, tm)
    dy, dm, dg4, sq = _ff2_loss(f1, x2, tgt, g4, w_ff2, min(2 * tm, L))

    df1, dw_ff2 = _ff2_bwd(dm, f1, w_ff2, min(1024, L), 1024)
    dx2, dmix, dg3, dg2 = _ff1_bwd(df1, w_ff1, x2, mix, dy, g3, g2, min(2 * tm, L))
    dw_ff1 = _matmul_tn(h3, df1, tk, FF1_COLS, "dw_ff1", slots=True)
    token = emit({"w_ff1": dw_ff1, "w_ff2": dw_ff2})
    dglu, ds, dgate, do, dggn = _mixout_bwd(dmix, w_out, w_glu, glu, s, o, gate, ggn, tm, after=token)
    dw_out = _matmul_tn(cat, dmix, tk, 1024, "dw_out")
    dw_glu = _matmul_tn(ys, dglu, tk, 1024, "dw_glu")
    token = emit({"w_glu": dw_glu, "w_out": dw_out})
    du, dbmat, dcmat, da8, dd = _s5_bwd(u, ds, xs, ent, bmat, cmat, tab_r, pw_r, d_skip, tb, after=token)
    dq, dk, dv = _retention_bwd(q, k, v, do, r_prev, consts, cosf, sinf)
    pieces = (dq, dk, dv, dgate, du)
    dw_in_t = _dw_in_t(pieces, h1, min(1024, L))
    token = emit({"w_in": dw_in_t})

    da = jnp.sum(da8, axis=1)
    dar = da[:, :KB_STATES].reshape(N_GROUP, N_STATE)
    dai = da[:, KB_STATES:].reshape(N_GROUP, N_STATE)
    dbr, dbi = _block_diag_in_t(dbmat)
    dlre, dlim, dldt, dbre, dbim = disc_vjp((dar, dai, dbr, dbi))
    dcre, dcim = _block_diag_out_t(dcmat)

    token2 = emit_small({
        "norm_mix_post": dg2, "ret_gn_gain": dggn,
        "ssm_lambda_re": dlre[None], "ssm_lambda_im": dlim[None], "ssm_log_dt": dldt[None],
        "ssm_b_re": dbre[None], "ssm_b_im": dbim[None], "ssm_c_re": dcre[None], "ssm_c_im": dcim[None],
        "ssm_d": dd, "norm_mlp_pre": dg3, "norm_mlp_post": dg4,
    }, sq)
    gx, dg1 = _inproj_bwd(pieces, w_in_t, x, dx2, g1, min(2 * tm, L), after=token + token2)
    return gx, dg1


BIG_SHAPES = {"w_in": (D_MODEL, IN_COLS // N_DEV), "w_glu": (SSM_W, 2 * SSM_W // N_DEV), "w_out": (D_MODEL // N_DEV, D_MODEL),
              "w_ff1": (D_MODEL, FF1_COLS), "w_ff2": (D_FF // N_DEV, D_MODEL)}
BIG_NAMES = ("w_in", "w_glu", "w_out", "w_ff1", "w_ff2")


def _cols_from_slots(g):
    return jnp.transpose(g, (1, 0, 2)).reshape(g.shape[1], N_DEV * g.shape[2])


def _cols_to_slots(dw):
    r, cols = dw.shape
    return jnp.transpose(dw.reshape(r, N_DEV, cols // N_DEV), (1, 0, 2))


WEIGHT_GROUPS = {"in": ("w_in",), "mix": ("w_glu", "w_out"), "mlp": ("w_ff1", "w_ff2")}


def _weight_from_slots(name, g):
    if name == "w_glu":
        return _cols_from_slots(g)
    if name == "w_ff1":
        return g
    return g.reshape(N_DEV * g.shape[1], g.shape[2])


def _grad_slots(name, dw):
    if name == "w_glu":
        return _cols_to_slots(dw)
    if name == "w_ff1":
        return dw
    if name == "w_in":
        return dw.reshape(N_DEV, BIG_SHAPES[name][1], BIG_SHAPES[name][0])
    return dw.reshape((N_DEV,) + BIG_SHAPES[name])


PIECE_ROWS = 8


def _small_layout(shapes):
    off, rows = {}, 0
    for n in SMALL_NAMES:
        off[n] = rows
        rows += -(-math.prod(shapes[n]) // (PIECE_ROWS * LANES)) * PIECE_ROWS
    return off, rows, rows + PIECE_ROWS


def _pack_small(vals, shapes, last=None):
    parts = []
    for n in SMALL_NAMES:
        flat = vals[n].reshape(-1).astype(_F32)
        pad = -flat.shape[0] % (PIECE_ROWS * LANES)
        if pad:
            flat = jnp.concatenate([flat, jnp.zeros((pad,), _F32)])
        parts.append(flat.reshape(-1, LANES))
    parts.append(jnp.zeros((PIECE_ROWS, LANES), _F32) if last is None else last)
    return jnp.concatenate(parts, axis=0)


def _unpack_small(buf, shapes):
    off, _, _ = _small_layout(shapes)
    out = {}
    for n in SMALL_NAMES:
        size = math.prod(shapes[n])
        rows = -(-size // LANES)
        out[n] = buf[off[n]:off[n] + rows].reshape(-1)[:size].reshape(shapes[n])
    return out


WEIGHT_NAMES = ('norm_mix_pre', 'norm_mix_post', 'w_in', 'ret_gn_gain', 'ssm_lambda_re', 'ssm_lambda_im', 'ssm_log_dt',
                'ssm_b_re', 'ssm_b_im', 'ssm_c_re', 'ssm_c_im', 'ssm_d', 'w_glu', 'w_out', 'norm_mlp_pre',
                'norm_mlp_post', 'w_ff1', 'w_ff2')


def kernel(x, norm_mix_pre, norm_mix_post, w_in, ret_gn_gain, ssm_lambda_re, ssm_lambda_im, ssm_log_dt, ssm_b_re, ssm_b_im, ssm_c_re, ssm_c_im, ssm_d, w_glu, w_out, norm_mlp_pre, norm_mlp_post, w_ff1, w_ff2, loss_target, m_norm_mix_pre, m_norm_mix_post, m_w_in, m_ret_gn_gain, m_ssm_lambda_re, m_ssm_lambda_im, m_ssm_log_dt, m_ssm_b_re, m_ssm_b_im, m_ssm_c_re, m_ssm_c_im, m_ssm_d, m_w_glu, m_w_out, m_norm_mlp_pre, m_norm_mlp_post, m_w_ff1, m_w_ff2, v_norm_mix_pre, v_norm_mix_post, v_w_in, v_ret_gn_gain, v_ssm_lambda_re, v_ssm_lambda_im, v_ssm_log_dt, v_ssm_b_re, v_ssm_b_im, v_ssm_c_re, v_ssm_c_im, v_ssm_d, v_w_glu, v_w_out, v_norm_mlp_pre, v_norm_mlp_post, v_w_ff1, v_w_ff2):
    args = dict(locals())
    w = {n: args[n] for n in WEIGHT_NAMES}
    m = {n: args["m_" + n] for n in WEIGHT_NAMES}
    v = {n: args["v_" + n] for n in WEIGHT_NAMES}
    L = x.shape[1]
    tm = min(256, L)
    tk = min(2048, L)
    tb = min(512, L)

    order = [n for names in WEIGHT_GROUPS.values() for n in names]
    blocks = [(w[n][0].T if n == "w_in" else w[n][0]).astype(_BF) for n in order]
    gathered = _split_start(blocks, [_landing(b) for b in blocks], True, "weights_start")
    zero = gathered[4][0, 0]

    def weights(group, after):
        names = WEIGHT_GROUPS[group]
        first = order.index(names[0])
        part = slice(first, first + len(names))
        landed = _split_wait(gathered[0], gathered[1], gathered[2][part], gathered[3][part], after, True,
                             "weights_wait_" + group, first=first)
        return [_weight_from_slots(n, g) for n, g in zip(names, landed)]

    in_flight = []

    def emit(dws):
        names = sorted(dws)
        srcs = [_grad_slots(n, dws[n]) for n in names]
        lands = [_landing(lax.dynamic_index_in_dim(t, _my_index(), 0, keepdims=False)) for t in srcs]
        started = _split_start(srcs, lands, False, "grads_start_" + "_".join(names))
        in_flight.append((names, started))
        return (started[4],)

    shapes = {n: w[n].shape for n in SMALL_NAMES}
    first_piece = {SMALL_NAMES[0]: jnp.zeros(shapes[SMALL_NAMES[0]], _F32)}
    small_flight = []

    def emit_small(gs, sq):
        loss_rows = jnp.broadcast_to(0.5 / D_MODEL * jnp.sum(sq), (PIECE_ROWS, LANES)).astype(_F32)
        buf = _pack_small({**first_piece, **gs}, shapes, loss_rows)
        small_flight.append(_split_start([buf], [_landing(buf)], True, "small_grads_start"))
        return (small_flight[0][4],)

    small_w = {n: w[n] for n in SMALL_NAMES}
    gx, dg1 = _local_grads(x[0], loss_target[0], small_w, weights, emit, emit_small, tm, tk, tb, zero=zero)
    last_buf = dg1.reshape(PIECE_ROWS, LANES)
    last_started = _split_start([last_buf], [_landing(last_buf)], True, "last_grad_start")

    grads, delta, new_m, new_v = {}, {}, {}, {}
    after = last_started[4]
    for names, started in in_flight:
        landed = _split_wait(*started[:4], after, False, "grads_wait_" + "_".join(names))
        for n, parts in zip(names, landed):
            flip = (lambda t: t.T) if n == "w_in" else (lambda t: t)
            res = _sum_adamw(parts, flip(w[n][0]), flip(m[n][0]), flip(v[n][0]), math.gcd(256, parts.shape[1]), "adamw_" + n)
            grads[n], delta[n], new_m[n], new_v[n] = (flip(t)[None] for t in res)
        after = res[1]
    small_parts = _split_wait(*small_flight[0][:4], after, True, "small_grads_wait")[0]
    last_parts = _split_wait(*last_started[:4], small_parts, True, "last_grad_wait")[0]
    small_parts = lax.dynamic_update_slice(small_parts, last_parts, (0, 0, 0))
    sw, sm, sv = _pack_small(w, shapes), _pack_small(m, shapes), _pack_small(v, shapes)
    res = _sum_adamw(small_parts, sw, sm, sv, sw.shape[0], "adamw_small")
    for dst, buf in zip((grads, delta, new_m, new_v), res):
        dst.update(_unpack_small(buf, shapes))
    _, loss_at, _ = _small_layout(shapes)
    loss = res[0][loss_at, 0]

    return (loss, gx[None], *[grads[n] for n in WEIGHT_NAMES], *[delta[n] for n in WEIGHT_NAMES],
            *[new_m[n] for n in WEIGHT_NAMES], *[new_v[n] for n in WEIGHT_NAMES])
```

```python
import math

import jax
import jax.numpy as jnp
from jax import lax
from jax.experimental import pallas as pl
from jax.experimental.pallas import tpu as pltpu

_BF = jnp.bfloat16
_F32 = jnp.float32

D_MODEL = 1024
RET_W = 512
N_HEAD = 4
HEAD_D = 128
CHUNK = 256
ROPE_CHUNK = 128
SSM_W = 512
SSM_GC = 16
N_GROUP = 32
N_STATE = 64
GROUPS_PER_KB = 8
N_KB = 4
KB_STATES = GROUPS_PER_KB * N_STATE
D_FF = 4096
IN_COLS = 2560
NORM_EPS = 1e-6
ROPE_BASE = 10000.0
N_DEV = 8

ADAM_LR = 0.001
ADAM_B1 = 0.9
ADAM_B2 = 0.999
ADAM_EPS = 1e-08
ADAM_WD = 0.01
ADAM_STEP = 10

SUBLANES = 8
LANES = 128
VMEM_LIMIT = 52 * 1024 * 1024
RET_STEP_CHUNKS = 2
KB_PER_STEP = 2
SCAN_UNROLL = True
FIX_UNROLL = 8

MESH = pl.DeviceIdType.MESH


def _params(*sem):
    return pltpu.CompilerParams(dimension_semantics=sem, vmem_limit_bytes=VMEM_LIMIT)


def _dot(a, b):
    return jnp.dot(a, b, preferred_element_type=_F32)


def _dot_nt(a, b):
    return lax.dot_general(a, b, (((1,), (1,)), ((), ())), preferred_element_type=_F32)


def _dot_tn(a, b):
    return lax.dot_general(a, b, (((0,), (0,)), ((), ())), preferred_element_type=_F32)


def _rms_r(z):
    return lax.rsqrt(jnp.mean(z * z, axis=-1, keepdims=True) + NORM_EPS)


def _rms_bwd(z, g, dn):
    r = _rms_r(z)
    t = dn * g
    dz = r * t - z * (r * r * r * jnp.mean(t * z, axis=-1, keepdims=True))
    return dz, dn * z * r


def _rope(t, cs, sn):
    return t * cs + pltpu.roll(t, HEAD_D // 2, 1) * sn


def _rope_t(t, cs, sn):
    return t * cs - pltpu.roll(t, HEAD_D // 2, 1) * sn


def _sigmoid(z):
    return 1.0 / (1.0 + jnp.exp(-z))


_GELU_C = math.sqrt(2.0 / math.pi)


def _gelu(z):
    return 0.5 * z * (1.0 + jnp.tanh(_GELU_C * (z + 0.044715 * z * z * z)))


def _gelu_grad(z):
    th = jnp.tanh(_GELU_C * (z + 0.044715 * z * z * z))
    return 0.5 * (1.0 + th) + 0.5 * z * (1.0 - th * th) * _GELU_C * (1.0 + 3 * 0.044715 * z * z)


ROW_CHUNK = 256


def _row_chunks(tm):
    return [pl.ds(i, min(ROW_CHUNK, tm)) for i in range(0, tm, ROW_CHUNK)]


def _ordered(body, in_specs, operands, after):
    k = len(after)
    if not k:
        return body, list(in_specs), tuple(operands)
    return ((lambda *refs: body(*refs[k:])), [pl.BlockSpec(memory_space=pl.ANY)] * k + list(in_specs),
            tuple(after) + tuple(operands))


def _row_spec(tm, n):
    return pl.BlockSpec((tm, n), lambda i: (i, 0))


def _full_spec(shape):
    nd = len(shape)
    return pl.BlockSpec(shape, lambda *_: (0,) * nd)


def _weight_spec(shape):
    nd = len(shape)
    return pl.BlockSpec(shape, lambda *_: (0,) * nd, pipeline_mode=pl.Buffered(1))


def _rope_tables(L):
    half = HEAD_D // 2
    inv_freq = ROPE_BASE ** (-jnp.arange(half, dtype=_F32) / half)
    twice = lambda t: jnp.concatenate([t, t], axis=-1)
    off = jnp.arange(ROPE_CHUNK, dtype=_F32)[:, None] * inv_freq[None, :]
    start = (ROPE_CHUNK * jnp.arange(L // ROPE_CHUNK, dtype=_F32))[:, None] * inv_freq[None, :]
    return (twice(jnp.cos(off)), twice(jnp.sin(off)),
            twice(jnp.cos(start))[:, None, :], twice(jnp.sin(start))[:, None, :])


def _prenorm(x, g, tm, after=()):
    L = x.shape[0]

    def body(x_ref, g_ref, h_ref):
        xv = x_ref[...]
        h_ref[...] = (xv * _rms_r(xv) * g_ref[...]).astype(_BF)

    body, in_specs, operands = _ordered(body, [_row_spec(tm, D_MODEL), _full_spec((1, D_MODEL))], (x, g), after)
    return pl.pallas_call(
        body, name="prenorm", grid=(L // tm,),
        in_specs=in_specs, out_specs=_row_spec(tm, D_MODEL),
        out_shape=jax.ShapeDtypeStruct((L, D_MODEL), _BF),
        compiler_params=_params("parallel"),
    )(*operands)


def _inproj_fwd(h, w_in_t, rope, tm):
    L = h.shape[0]
    n_chunks = tm // ROPE_CHUNK

    def body(h_ref, w_ref, co_ref, so_ref, cs_ref, ss_ref, q_ref, k_ref, v_ref, gate_ref, u_ref, cos_ref, sin_ref):
        proj = _dot_nt(h_ref[...], w_ref[...])
        lane = lax.broadcasted_iota(jnp.int32, (ROPE_CHUNK, HEAD_D), 1)
        sign = jnp.where(lane < HEAD_D // 2, -1.0, 1.0)
        co, so = co_ref[...], so_ref[...]
        for c in range(n_chunks):
            chunk = pl.program_id(0) * n_chunks + c
            cst, sst = cs_ref[chunk], ss_ref[chunk]
            rows = slice(c * ROPE_CHUNK, (c + 1) * ROPE_CHUNK)
            cs = co * cst - so * sst
            sn = (so * cst + co * sst) * sign
            cos_ref[rows, :] = cs
            sin_ref[rows, :] = sn
            for hh in range(N_HEAD):
                lo = hh * HEAD_D
                q_ref[rows, lo:lo + HEAD_D] = _rope(proj[rows, lo:lo + HEAD_D], cs, sn).astype(_BF)
                kh = _rope(proj[rows, RET_W + lo:RET_W + lo + HEAD_D], cs, sn) * (HEAD_D ** -0.5)
                k_ref[rows, lo:lo + HEAD_D] = kh.astype(_BF)
        v_ref[...] = proj[:, 2 * RET_W:3 * RET_W].astype(_BF)
        gate_ref[...] = proj[:, 3 * RET_W:4 * RET_W]
        u_ref[...] = proj[:, 4 * RET_W:]

    nc = L // ROPE_CHUNK
    return pl.pallas_call(
        body, name="inproj_fwd", grid=(L // tm,),
        in_specs=[_row_spec(tm, D_MODEL), _weight_spec((IN_COLS, D_MODEL)),
                  _full_spec((ROPE_CHUNK, HEAD_D)), _full_spec((ROPE_CHUNK, HEAD_D)),
                  _full_spec((nc, 1, HEAD_D)), _full_spec((nc, 1, HEAD_D))],
        out_specs=[_row_spec(tm, RET_W)] * 5 + [_row_spec(tm, HEAD_D)] * 2,
        out_shape=[jax.ShapeDtypeStruct((L, RET_W), _BF)] * 3 + [jax.ShapeDtypeStruct((L, RET_W), _F32)] * 2
        + [jax.ShapeDtypeStruct((L, HEAD_D), _F32)] * 2,
        compiler_params=_params("parallel"),
    )(h, w_in_t, *rope)


def _ret_consts():
    lg = jnp.log(1.0 - jnp.exp(jnp.linspace(math.log(1.0 / 32), math.log(1.0 / 512), N_HEAD))).astype(_F32)
    idx = jnp.arange(CHUNK, dtype=_F32)
    diff = idx[:, None] - idx[None, :]
    decay = jnp.where(diff[None] >= 0, jnp.exp(jnp.maximum(diff, 0.0)[None] * lg[:, None, None]), 0.0)
    zeta = jnp.exp((CHUNK - 1 - idx)[None, :] * lg[:, None])
    xi = jnp.exp((idx + 1.0)[None, :] * lg[:, None])
    gc = jnp.exp(CHUNK * lg)
    wide = lambda t: jnp.broadcast_to(t[:, :, None], (N_HEAD, CHUNK, HEAD_D)).astype(_F32)
    gcw = jnp.broadcast_to(gc[:, None, None], (N_HEAD, SUBLANES, HEAD_D)).astype(_F32)
    return decay.astype(_F32), wide(xi), wide(zeta), gcw


def _head_specs():
    wide = _full_spec((N_HEAD, CHUNK, HEAD_D))
    return [_full_spec((N_HEAD, CHUNK, CHUNK)), wide, wide, _full_spec((N_HEAD, SUBLANES, HEAD_D))]


def _retention_fwd(q, k, v, gate, ggn, consts):
    L = q.shape[0]
    nc = L // CHUNK
    cps = math.gcd(RET_STEP_CHUNKS, nc)
    blk = pl.BlockSpec((cps * CHUNK, RET_W), lambda n: (n, 0))

    def body(q_ref, k_ref, v_ref, gate_ref, ggn_ref, dm_ref, xi_ref, zeta_ref, gc_ref,
             o_ref, y_ref, rp_ref, r_scr):
        @pl.when(pl.program_id(0) == 0)
        def _():
            r_scr[...] = jnp.zeros_like(r_scr)

        for hh in range(N_HEAD):
            cols = slice(hh * HEAD_D, (hh + 1) * HEAD_D)
            state = r_scr[hh]
            for c in range(cps):
                rows = slice(c * CHUNK, (c + 1) * CHUNK)
                qv, kv, vv = q_ref[rows, cols], k_ref[rows, cols], v_ref[rows, cols]
                s = _dot_nt(qv, kv) * dm_ref[hh]
                o = _dot(s.astype(_BF), vv) + _dot(qv, state.astype(_BF)) * xi_ref[hh]
                o_ref[rows, cols] = o
                rp_ref[hh, c] = state
                vz = (vv.astype(_F32) * zeta_ref[hh]).astype(_BF)
                state = gc_ref[hh, 0:1, :] * state + _dot_tn(kv, vz)
                dlt = o - jnp.mean(o, axis=-1, keepdims=True)
                on = dlt * lax.rsqrt(jnp.mean(dlt * dlt, axis=-1, keepdims=True) + NORM_EPS)
                gt = gate_ref[rows, cols]
                y_ref[rows, cols] = (gt * _sigmoid(gt) * (on * ggn_ref[:, cols])).astype(_BF)
            r_scr[hh] = state

    return pl.pallas_call(
        body, name="retention_fwd", grid=(nc // cps,),
        in_specs=[blk, blk, blk, blk, _full_spec((1, RET_W))] + _head_specs(),
        out_specs=[blk, blk, pl.BlockSpec((N_HEAD, cps, HEAD_D, HEAD_D), lambda n: (0, n, 0, 0))],
        out_shape=[jax.ShapeDtypeStruct((L, RET_W), _F32), jax.ShapeDtypeStruct((L, RET_W), _BF),
                   jax.ShapeDtypeStruct((N_HEAD, nc, HEAD_D, HEAD_D), _F32)],
        scratch_shapes=[pltpu.VMEM((N_HEAD, HEAD_D, HEAD_D), _F32)],
        compiler_params=_params("arbitrary"),
    )(q, k, v, gate, ggn, *consts)


def _rows_to_segments(dst_scr, src_ref, seg):
    for g in range(dst_scr.shape[0]):
        for j in range(SUBLANES):
            dst_scr[g, pl.ds(j, seg, stride=SUBLANES), :] = src_ref[pl.ds(j * seg, seg), g * LANES:(g + 1) * LANES]


def _segments_to_rows(dst_ref, src_scr, seg):
    for g in range(src_scr.shape[0]):
        for j in range(SUBLANES):
            dst_ref[pl.ds(j * seg, seg), g * LANES:(g + 1) * LANES] = src_scr[g, pl.ds(j, seg, stride=SUBLANES), :]


def _scan_segments(x_ref, tab_ref, pw_ref, carry_ref, seg, reverse, entry_ref=None, fwd_ref=None, fwd_entry_ref=None,
                   da_ref=None):
    G = x_ref.shape[0]
    W = KB_STATES
    re, im = pl.ds(0, W), pl.ds(W, W)
    row_id = lax.broadcasted_iota(jnp.int32, (SUBLANES, W), 0)
    edge_in = (row_id == SUBLANES - 1) if reverse else (row_id == 0)
    edge_out = 0 if reverse else SUBLANES - 1
    a_tab = [(tab_ref[g, 0], tab_ref[g, 1]) for g in range(G)]

    def local(i, st):
        r = (seg - 1 - i) if reverse else i
        out = []
        for g in range(G):
            (ar, ai), (sr, si) = a_tab[g], st[g]
            nr = ar * sr - ai * si + x_ref[g, r, :, re]
            ni = ar * si + ai * sr + x_ref[g, r, :, im]
            x_ref[g, r, :, re] = nr
            x_ref[g, r, :, im] = ni
            out.append((nr, ni))
        return tuple(out)

    zero = jnp.zeros((SUBLANES, W), _F32)
    ends = lax.fori_loop(0, seg, local, tuple((zero, zero) for _ in range(G)), unroll=SCAN_UNROLL)

    entry = []
    shift = (SUBLANES - 1) if reverse else 1
    for g in range(G):
        er, ei = ends[g]
        fr = jnp.where(edge_in, carry_ref[g, :, re], pltpu.roll(er, shift, 0))
        fi = jnp.where(edge_in, carry_ref[g, :, im], pltpu.roll(ei, shift, 0))
        for j, dist in enumerate((1, 2, 4)):
            pr, pi = tab_ref[g, 2 + 2 * j], tab_ref[g, 3 + 2 * j]
            sh = (SUBLANES - dist) if reverse else dist
            sr, si = pltpu.roll(fr, sh, 0), pltpu.roll(fi, sh, 0)
            fr, fi = fr + pr * sr - pi * si, fi + pr * si + pi * sr
        br, bi = tab_ref[g, 8], tab_ref[g, 9]
        outr = br * fr - bi * fi + er
        outi = br * fi + bi * fr + ei
        carry_ref[g, :, re] = jnp.broadcast_to(outr[edge_out:edge_out + 1, :], (SUBLANES, W))
        carry_ref[g, :, im] = jnp.broadcast_to(outi[edge_out:edge_out + 1, :], (SUBLANES, W))
        entry.append((fr, fi))
        if entry_ref is not None:
            entry_ref[g, :, re] = fr
            entry_ref[g, :, im] = fi

    add_da = da_ref is not None

    def fix(r, st, first=False):
        out = []
        for g in range(G):
            fr, fi = entry[g]
            pwr, pwi = pw_ref[g, r, :, re], pw_ref[g, r, :, im]
            xr = x_ref[g, r, :, re] + (pwr * fr - pwi * fi)
            xi = x_ref[g, r, :, im] + (pwr * fi + pwi * fr)
            x_ref[g, r, :, re] = xr
            x_ref[g, r, :, im] = xi
            if add_da:
                prev = fwd_entry_ref.at[g] if first else fwd_ref.at[g, r - 1]
                xpr, xpi = prev[:, re], prev[:, im]
                out.append((st[g][0] + (xr * xpr + xi * xpi), st[g][1] + (xi * xpr - xr * xpi)))
            else:
                out.append(st[g])
        return tuple(out)

    if add_da:
        st = fix(0, tuple((zero, zero) for _ in range(G)), first=True)
        st = lax.fori_loop(1, seg, fix, st, unroll=SCAN_UNROLL)
        for g in range(G):
            da_ref[g, :, re] += st[g][0]
            da_ref[g, :, im] += st[g][1]
    else:
        lax.fori_loop(0, seg, fix, tuple((zero[0:1, 0:LANES],) for _ in range(G)), unroll=FIX_UNROLL)


def _s5_specs(seg, time=lambda t: t):
    G = KB_PER_STEP
    return dict(
        x=pl.BlockSpec((G, seg, SUBLANES, 2 * KB_STATES), lambda kb, t: (kb, time(t), 0, 0)),
        ent=pl.BlockSpec((G, 1, SUBLANES, 2 * KB_STATES), lambda kb, t: (kb, time(t), 0, 0)),
        b=pl.BlockSpec((G, LANES, 2 * KB_STATES), lambda kb, t: (kb, 0, 0)),
        c=pl.BlockSpec((G, 2 * KB_STATES, LANES), lambda kb, t: (kb, 0, 0)),
        tab=pl.BlockSpec((G, 10, SUBLANES, KB_STATES), lambda kb, t: (kb, 0, 0, 0)),
        pw=pl.BlockSpec((G, seg, 1, 2 * KB_STATES), lambda kb, t: (kb, 0, 0, 0)),
        d=pl.BlockSpec((1, G * LANES), lambda kb, t: (0, kb)),
    )


def _s5_fwd(u, bmat, cmat, tab_f, pw_f, d_skip, tb):
    L = u.shape[0]
    nt = L // tb
    seg = tb // SUBLANES
    G = KB_PER_STEP
    ucol = pl.BlockSpec((tb, G * LANES), lambda kb, t: (t, kb))
    sp = _s5_specs(seg)

    def body(u_ref, b_ref, c_ref, tab_ref, pw_ref, d_ref, s_ref, x_ref, ent_ref, up_scr, y_scr, carry_scr):
        @pl.when(pl.program_id(1) == 0)
        def _():
            carry_scr[...] = jnp.zeros_like(carry_scr)

        _rows_to_segments(up_scr, u_ref, seg)
        for g in range(G):
            x_ref[g] = _dot(up_scr[g].astype(_BF), b_ref[g]).reshape(seg, SUBLANES, 2 * KB_STATES)
        _scan_segments(x_ref, tab_ref, pw_ref, carry_scr, seg, reverse=False, entry_ref=ent_ref.at[:, 0])
        for g in range(G):
            y = _dot(x_ref[g].reshape(tb, 2 * KB_STATES).astype(_BF), c_ref[g])
            y_scr[g] = y + d_ref[:, g * LANES:(g + 1) * LANES] * up_scr[g]
        _segments_to_rows(s_ref, y_scr, seg)

    return pl.pallas_call(
        body, name="s5_fwd", grid=(N_KB // G, nt),
        in_specs=[ucol, sp["b"], sp["c"], sp["tab"], sp["pw"], sp["d"]],
        out_specs=[ucol, sp["x"], sp["ent"]],
        out_shape=[jax.ShapeDtypeStruct((L, SSM_W), _F32),
                   jax.ShapeDtypeStruct((N_KB, L // SUBLANES, SUBLANES, 2 * KB_STATES), _F32),
                   jax.ShapeDtypeStruct((N_KB, nt, SUBLANES, 2 * KB_STATES), _F32)],
        scratch_shapes=[pltpu.VMEM((G, tb, LANES), _F32)] * 2 + [pltpu.VMEM((G, SUBLANES, 2 * KB_STATES), _F32)],
        compiler_params=_params("parallel", "arbitrary"),
    )(u, bmat, cmat, tab_f, pw_f, d_skip)


def _mixout_fwd(s, y_ret, x, w_glu, w_out, g2, tm):
    L = s.shape[0]

    def body(s_ref, yr_ref, x_ref, wg_ref, wo_ref, g_ref, ys_ref, glu_ref, cat_ref, mix_ref, x2_ref):
        for rows in _row_chunks(tm):
            ys = _gelu(s_ref[rows, :]).astype(_BF)
            ys_ref[rows, :] = ys
            glu = _dot(ys, wg_ref[...])
            glu_ref[rows, :] = glu
            cat_ref[rows, :RET_W] = yr_ref[rows, :]
            cat_ref[rows, RET_W:] = (glu[:, :SSM_W] * _sigmoid(glu[:, SSM_W:])).astype(_BF)
            mix = _dot(cat_ref[rows, :], wo_ref[...])
            mix_ref[rows, :] = mix
            x2_ref[rows, :] = x_ref[rows, :] + mix * _rms_r(mix) * g_ref[...]

    return pl.pallas_call(
        body, name="mixout_fwd", grid=(L // tm,),
        in_specs=[_row_spec(tm, SSM_W), _row_spec(tm, RET_W), _row_spec(tm, D_MODEL),
                  _weight_spec((SSM_W, 2 * SSM_W)), _weight_spec((D_MODEL, D_MODEL)), _full_spec((1, D_MODEL))],
        out_specs=[_row_spec(tm, SSM_W), _row_spec(tm, 2 * SSM_W), _row_spec(tm, D_MODEL),
                   _row_spec(tm, D_MODEL), _row_spec(tm, D_MODEL)],
        out_shape=[jax.ShapeDtypeStruct((L, SSM_W), _BF), jax.ShapeDtypeStruct((L, 2 * SSM_W), _F32),
                   jax.ShapeDtypeStruct((L, D_MODEL), _BF), jax.ShapeDtypeStruct((L, D_MODEL), _F32),
                   jax.ShapeDtypeStruct((L, D_MODEL), _F32)],
        compiler_params=_params("parallel"),
    )(s, y_ret, x, w_glu, w_out, g2)


FF1_COLS = D_FF // N_DEV


def _ff1_fwd(x2, g3, w1, tm):
    L = x2.shape[0]

    def body(x_ref, g_ref, w_ref, h_ref, f_ref):
        xv = x_ref[...]
        h = (xv * _rms_r(xv) * g_ref[...]).astype(_BF)
        h_ref[...] = h
        for j in range(N_DEV):
            f_ref[:, j * FF1_COLS:(j + 1) * FF1_COLS] = _dot(h, w_ref[j])

    return pl.pallas_call(
        body, name="ff1_fwd", grid=(L // tm,),
        in_specs=[_row_spec(tm, D_MODEL), _full_spec((1, D_MODEL)), _weight_spec((N_DEV, D_MODEL, FF1_COLS))],
        out_specs=[_row_spec(tm, D_MODEL), _row_spec(tm, D_FF)],
        out_shape=[jax.ShapeDtypeStruct((L, D_MODEL), _BF), jax.ShapeDtypeStruct((L, D_FF), _F32)],
        compiler_params=_params("parallel"),
    )(x2, g3, w1)


def _ff2_loss(f1, x2, tgt, g4, w2, tm):
    L = f1.shape[0]

    def body(f_ref, x_ref, t_ref, g_ref, w_ref, dy_ref, dm_ref, dg_ref, ls_ref):
        @pl.when(pl.program_id(0) == 0)
        def _():
            dg_ref[...] = jnp.zeros_like(dg_ref)
            ls_ref[...] = jnp.zeros_like(ls_ref)

        g = g_ref[...]
        for rows in _row_chunks(tm):
            rl = jnp.maximum(f_ref[rows, :], 0.0)
            m = _dot((rl * rl).astype(_BF), w_ref[...])
            y = x_ref[rows, :] + m * _rms_r(m) * g
            err = y - t_ref[rows, :]
            ls_ref[...] += jnp.sum(err * err, axis=0, keepdims=True)
            dy = err * (1.0 / D_MODEL)
            dy_ref[rows, :] = dy
            dm, dgr = _rms_bwd(m, g, dy)
            dm_ref[rows, :] = dm.astype(_BF)
            dg_ref[...] += jnp.sum(dgr, axis=0, keepdims=True)

    return pl.pallas_call(
        body, name="ff2_loss", grid=(L // tm,),
        in_specs=[_row_spec(tm, D_FF), _row_spec(tm, D_MODEL), _row_spec(tm, D_MODEL),
                  _full_spec((1, D_MODEL)), _weight_spec((D_FF, D_MODEL))],
        out_specs=[_row_spec(tm, D_MODEL), _row_spec(tm, D_MODEL), _full_spec((1, D_MODEL)), _full_spec((1, D_MODEL))],
        out_shape=[jax.ShapeDtypeStruct((L, D_MODEL), _F32), jax.ShapeDtypeStruct((L, D_MODEL), _BF),
                   jax.ShapeDtypeStruct((1, D_MODEL), _F32), jax.ShapeDtypeStruct((1, D_MODEL), _F32)],
        compiler_params=_params("arbitrary"),
    )(f1, x2, tgt, g4, w2)


def _ff2_bwd(dm, f1, w2, tm, tn):
    L = dm.shape[0]
    last = L // tm - 1

    def body(dm_ref, f_ref, w_ref, df_ref, dw_ref, acc):
        @pl.when(pl.program_id(1) == 0)
        def _():
            acc[...] = jnp.zeros_like(acc)

        dmv = dm_ref[...]
        rl = jnp.maximum(f_ref[...], 0.0)
        df_ref[...] = (_dot_nt(dmv, w_ref[...]) * (2.0 * rl)).astype(_BF)
        acc[...] += _dot_tn((rl * rl).astype(_BF), dmv)

        @pl.when(pl.program_id(1) == last)
        def _():
            dw_ref[...] = acc[...].astype(_BF)

    return pl.pallas_call(
        body, name="ff2_bwd", grid=(D_FF // tn, L // tm),
        in_specs=[pl.BlockSpec((tm, D_MODEL), lambda j, i: (i, 0)), pl.BlockSpec((tm, tn), lambda j, i: (i, j)),
                  pl.BlockSpec((tn, D_MODEL), lambda j, i: (j, 0))],
        out_specs=[pl.BlockSpec((tm, tn), lambda j, i: (i, j)), pl.BlockSpec((tn, D_MODEL), lambda j, i: (j, 0))],
        out_shape=[jax.ShapeDtypeStruct((L, D_FF), _BF), jax.ShapeDtypeStruct((D_FF, D_MODEL), _BF)],
        scratch_shapes=[pltpu.VMEM((tn, D_MODEL), _F32)],
        compiler_params=_params("parallel", "arbitrary"),
    )(dm, f1, w2)


def _ff1_bwd(df1, w1, x2, mix, dy, g3, g2, tm):
    L = df1.shape[0]

    def body(df_ref, w_ref, x2_ref, mix_ref, dy_ref, g3_ref, g2_ref, dx2_ref, dmix_ref, dg3_ref, dg2_ref):
        @pl.when(pl.program_id(0) == 0)
        def _():
            dg3_ref[...] = jnp.zeros_like(dg3_ref)
            dg2_ref[...] = jnp.zeros_like(dg2_ref)

        for rows in _row_chunks(tm):
            dh = _dot_nt(df_ref[rows, 0:FF1_COLS], w_ref[0])
            for j in range(1, N_DEV):
                dh = dh + _dot_nt(df_ref[rows, j * FF1_COLS:(j + 1) * FF1_COLS], w_ref[j])
            dz, dgr = _rms_bwd(x2_ref[rows, :], g3_ref[...], dh)
            dg3_ref[...] += jnp.sum(dgr, axis=0, keepdims=True)
            dx2 = dy_ref[rows, :] + dz
            dx2_ref[rows, :] = dx2
            dmx, dgr2 = _rms_bwd(mix_ref[rows, :], g2_ref[...], dx2)
            dg2_ref[...] += jnp.sum(dgr2, axis=0, keepdims=True)
            dmix_ref[rows, :] = dmx.astype(_BF)

    vec = _full_spec((1, D_MODEL))
    return pl.pallas_call(
        body, name="ff1_bwd", grid=(L // tm,),
        in_specs=[_row_spec(tm, D_FF), _weight_spec((N_DEV, D_MODEL, FF1_COLS)), _row_spec(tm, D_MODEL),
                  _row_spec(tm, D_MODEL), _row_spec(tm, D_MODEL), vec, vec],
        out_specs=[_row_spec(tm, D_MODEL), _row_spec(tm, D_MODEL), vec, vec],
        out_shape=[jax.ShapeDtypeStruct((L, D_MODEL), _F32), jax.ShapeDtypeStruct((L, D_MODEL), _BF),
                   jax.ShapeDtypeStruct((1, D_MODEL), _F32), jax.ShapeDtypeStruct((1, D_MODEL), _F32)],
        compiler_params=_params("arbitrary"),
    )(df1, w1, x2, mix, dy, g3, g2)


def _matmul_tn(a, b, tm, tn, name, slots=False):
    L, K = a.shape
    N = b.shape[1]
    last = L // tm - 1

    def body(a_ref, b_ref, o_ref, acc):
        @pl.when(pl.program_id(1) == 0)
        def _():
            acc[...] = jnp.zeros_like(acc)

        acc[...] += _dot_tn(a_ref[...].astype(_BF), b_ref[...].astype(_BF))

        @pl.when(pl.program_id(1) == last)
        def _():
            if slots:
                o_ref[0] = acc[...].astype(_BF)
            else:
                o_ref[...] = acc[...].astype(_BF)

    if slots:
        out_spec = pl.BlockSpec((1, K, tn), lambda j, i: (j, 0, 0))
        out_shape = jax.ShapeDtypeStruct((N // tn, K, tn), _BF)
    else:
        out_spec = pl.BlockSpec((K, tn), lambda j, i: (0, j))
        out_shape = jax.ShapeDtypeStruct((K, N), _BF)
    return pl.pallas_call(
        body, name=name, grid=(N // tn, L // tm),
        in_specs=[pl.BlockSpec((tm, K), lambda j, i: (i, 0)), pl.BlockSpec((tm, tn), lambda j, i: (i, j))],
        out_specs=out_spec, out_shape=out_shape,
        scratch_shapes=[pltpu.VMEM((K, tn), _F32)],
        compiler_params=_params("parallel", "arbitrary"),
    )(a, b)


def _dw_in_t(pieces, h, tk):
    L = h.shape[0]
    last = L // tk - 1

    def body(p0, p1, p2, p3, p4, h_ref, o_ref, acc):
        @pl.when(pl.program_id(0) == 0)
        def _():
            acc[...] = jnp.zeros_like(acc)

        hv = h_ref[...]
        for j, p in enumerate((p0, p1, p2, p3, p4)):
            acc[j * RET_W:(j + 1) * RET_W, :] += _dot_tn(p[...].astype(_BF), hv)

        @pl.when(pl.program_id(0) == last)
        def _():
            o_ref[...] = acc[...].astype(_BF)

    return pl.pallas_call(
        body, name="dw_in", grid=(L // tk,),
        in_specs=[_row_spec(tk, RET_W)] * 5 + [_row_spec(tk, D_MODEL)],
        out_specs=_full_spec((IN_COLS, D_MODEL)), out_shape=jax.ShapeDtypeStruct((IN_COLS, D_MODEL), _BF),
        scratch_shapes=[pltpu.VMEM((IN_COLS, D_MODEL), _F32)],
        compiler_params=_params("arbitrary"),
    )(*pieces, h)


def _mixout_bwd(dmix, w_out, w_glu, glu, s, o, gate, ggn, tm, after=()):
    L = dmix.shape[0]

    def body(dmix_ref, wo_ref, wg_ref, glu_ref, s_ref, o_ref, gate_ref, ggn_ref,
             dglu_ref, ds_ref, dgate_ref, do_ref, dggn_ref):
        @pl.when(pl.program_id(0) == 0)
        def _():
            dggn_ref[...] = jnp.zeros_like(dggn_ref)

        dcat = _dot_nt(dmix_ref[...], wo_ref[...])
        dy_ret, dy_ssm = dcat[:, :RET_W], dcat[:, RET_W:]
        glu = glu_ref[...]
        ga, sg = glu[:, :SSM_W], _sigmoid(glu[:, SSM_W:])
        dga = (dy_ssm * sg).astype(_BF)
        dgb = (dy_ssm * ga * sg * (1.0 - sg)).astype(_BF)
        dglu_ref[:, :SSM_W] = dga
        dglu_ref[:, SSM_W:] = dgb
        dys = _dot_nt(dga, wg_ref[:, :SSM_W]) + _dot_nt(dgb, wg_ref[:, SSM_W:])
        ds_ref[...] = dys * _gelu_grad(s_ref[...])
        gt = gate_ref[...]
        sgt = _sigmoid(gt)
        ggn = ggn_ref[...]
        for hh in range(N_HEAD):
            cols = slice(hh * HEAD_D, (hh + 1) * HEAD_D)
            ov = o_ref[:, cols]
            dlt = ov - jnp.mean(ov, axis=-1, keepdims=True)
            rstd = lax.rsqrt(jnp.mean(dlt * dlt, axis=-1, keepdims=True) + NORM_EPS)
            on = dlt * rstd
            dyr = dy_ret[:, cols] * (gt[:, cols] * sgt[:, cols])
            dgate_ref[:, cols] = dy_ret[:, cols] * (on * ggn[:, cols]) * (sgt[:, cols] * (1.0 + gt[:, cols] * (1.0 - sgt[:, cols])))
            dggn_ref[:, cols] += jnp.sum(dyr * on, axis=0, keepdims=True)
            don = dyr * ggn[:, cols]
            do = rstd * (don - jnp.mean(don, axis=-1, keepdims=True) - on * jnp.mean(don * on, axis=-1, keepdims=True))
            do_ref[:, cols] = do.astype(_BF)

    body, in_specs, operands = _ordered(
        body, [_row_spec(tm, D_MODEL), _weight_spec((D_MODEL, D_MODEL)), _weight_spec((SSM_W, 2 * SSM_W)),
               _row_spec(tm, 2 * SSM_W), _row_spec(tm, SSM_W), _row_spec(tm, RET_W), _row_spec(tm, RET_W),
               _full_spec((1, RET_W))], (dmix, w_out, w_glu, glu, s, o, gate, ggn), after)
    return pl.pallas_call(
        body, name="mixout_bwd", grid=(L // tm,),
        in_specs=in_specs,
        out_specs=[_row_spec(tm, 2 * SSM_W), _row_spec(tm, SSM_W), _row_spec(tm, RET_W), _row_spec(tm, RET_W),
                   _full_spec((1, RET_W))],
        out_shape=[jax.ShapeDtypeStruct((L, 2 * SSM_W), _BF), jax.ShapeDtypeStruct((L, SSM_W), _F32),
                   jax.ShapeDtypeStruct((L, RET_W), _F32), jax.ShapeDtypeStruct((L, RET_W), _BF),
                   jax.ShapeDtypeStruct((1, RET_W), _F32)],
        compiler_params=_params("arbitrary"),
    )(*operands)


def _s5_bwd(u, ds, xs, ent, bmat, cmat, tab_r, pw_r, d_skip, tb, after=()):
    L = u.shape[0]
    nt = L // tb
    seg = tb // SUBLANES
    G = KB_PER_STEP
    rcol = pl.BlockSpec((tb, G * LANES), lambda kb, t: (nt - 1 - t, kb))
    sp = _s5_specs(seg, time=lambda t: nt - 1 - t)
    aspec = pl.BlockSpec((G, SUBLANES, 2 * KB_STATES), lambda kb, t: (kb, 0, 0))

    def body(u_ref, ds_ref, x_ref, ent_ref, b_ref, c_ref, tr_ref, pr_ref, d_ref,
             du_ref, db_ref, dc_ref, da_ref, dd_ref, up_scr, dp_scr, g_scr, lc_scr):
        @pl.when(pl.program_id(1) == 0)
        def _():
            lc_scr[...] = jnp.zeros_like(lc_scr)
            db_ref[...] = jnp.zeros_like(db_ref)
            dc_ref[...] = jnp.zeros_like(dc_ref)
            da_ref[...] = jnp.zeros_like(da_ref)
            dd_ref[...] = jnp.zeros_like(dd_ref)

        _rows_to_segments(up_scr, u_ref, seg)
        _rows_to_segments(dp_scr, ds_ref, seg)
        for g in range(G):
            g_scr[g] = _dot_nt(dp_scr[g].astype(_BF), c_ref[g]).reshape(seg, SUBLANES, 2 * KB_STATES)
        _scan_segments(g_scr, tr_ref, pr_ref, lc_scr, seg, reverse=True, fwd_ref=x_ref, fwd_entry_ref=ent_ref.at[:, 0],
                       da_ref=da_ref)
        for g in range(G):
            cols = slice(g * LANES, (g + 1) * LANES)
            uv, dsv = up_scr[g], dp_scr[g]
            ub, dsb = uv.astype(_BF), dsv.astype(_BF)
            lamb = g_scr[g].reshape(tb, 2 * KB_STATES).astype(_BF)
            db_ref[g] += _dot_tn(ub, lamb)
            dc_ref[g] += _dot_tn(dsb, x_ref[g].reshape(tb, 2 * KB_STATES).astype(_BF))
            dd_ref[:, cols] += jnp.sum(dsv * uv, axis=0, keepdims=True)
            up_scr[g] = _dot_nt(lamb, b_ref[g]) + d_ref[:, cols] * dsv
        _segments_to_rows(du_ref, up_scr, seg)

    body, in_specs, operands = _ordered(
        body, [rcol, rcol, sp["x"], sp["ent"], sp["b"], sp["c"], sp["tab"], sp["pw"], sp["d"]],
        (u, ds, xs, ent, bmat, cmat, tab_r, pw_r, d_skip), after)
    return pl.pallas_call(
        body, name="s5_bwd", grid=(N_KB // G, nt),
        in_specs=in_specs,
        out_specs=[rcol, sp["b"], sp["b"], aspec, sp["d"]],
        out_shape=[jax.ShapeDtypeStruct((L, SSM_W), _F32),
                   jax.ShapeDtypeStruct((N_KB, LANES, 2 * KB_STATES), _F32),
                   jax.ShapeDtypeStruct((N_KB, LANES, 2 * KB_STATES), _F32),
                   jax.ShapeDtypeStruct((N_KB, SUBLANES, 2 * KB_STATES), _F32),
                   jax.ShapeDtypeStruct((1, SSM_W), _F32)],
        scratch_shapes=[pltpu.VMEM((G, tb, LANES), _F32)] * 2
        + [pltpu.VMEM((G, seg, SUBLANES, 2 * KB_STATES), _F32), pltpu.VMEM((G, SUBLANES, 2 * KB_STATES), _F32)],
        compiler_params=_params("parallel", "arbitrary"),
    )(*operands)


def _retention_bwd(q, k, v, do, r_prev, consts, cosf, sinf):
    L = q.shape[0]
    nc = L // CHUNK
    cps = math.gcd(RET_STEP_CHUNKS, nc)
    nb = nc // cps
    blk = pl.BlockSpec((cps * CHUNK, RET_W), lambda n: (nb - 1 - n, 0))
    rope_blk = pl.BlockSpec((cps * CHUNK, HEAD_D), lambda n: (nb - 1 - n, 0))

    def body(q_ref, k_ref, v_ref, do_ref, rp_ref, dm_ref, xi_ref, zeta_ref, gc_ref, cos_ref, sin_ref,
             dq_ref, dk_ref, dv_ref, g_scr):
        @pl.when(pl.program_id(0) == 0)
        def _():
            g_scr[...] = jnp.zeros_like(g_scr)

        for hh in range(N_HEAD):
            cols = slice(hh * HEAD_D, (hh + 1) * HEAD_D)
            dm, zeta = dm_ref[hh], zeta_ref[hh]
            gst = g_scr[hh]
            for c in reversed(range(cps)):
                rows = slice(c * CHUNK, (c + 1) * CHUNK)
                qv, kv, vv, dov = q_ref[rows, cols], k_ref[rows, cols], v_ref[rows, cols], do_ref[rows, cols]
                rb = rp_ref[hh, c].astype(_BF)
                gb = gst.astype(_BF)
                sb = (_dot_nt(qv, kv) * dm).astype(_BF)
                dab = (_dot_nt(dov, vv) * dm).astype(_BF)
                dox = (dov.astype(_F32) * xi_ref[hh]).astype(_BF)
                vz = (vv.astype(_F32) * zeta).astype(_BF)
                dq = _dot(dab, kv) + _dot_nt(dox, rb)
                dk = _dot_tn(dab, qv) + _dot_nt(vz, gb)
                dv = _dot_tn(sb, dov) + _dot(kv, gb) * zeta
                gst = gc_ref[hh, 0:1, :] * gst + _dot_tn(qv, dox)
                cs, sn = cos_ref[rows, :], sin_ref[rows, :]
                dq_ref[rows, cols] = _rope_t(dq, cs, sn).astype(_BF)
                dk_ref[rows, cols] = (_rope_t(dk, cs, sn) * (HEAD_D ** -0.5)).astype(_BF)
                dv_ref[rows, cols] = dv.astype(_BF)
            g_scr[hh] = gst

    return pl.pallas_call(
        body, name="retention_bwd", grid=(nb,),
        in_specs=[blk, blk, blk, blk, pl.BlockSpec((N_HEAD, cps, HEAD_D, HEAD_D), lambda n: (0, nb - 1 - n, 0, 0))]
        + _head_specs() + [rope_blk, rope_blk],
        out_specs=[blk, blk, blk],
        out_shape=[jax.ShapeDtypeStruct((L, RET_W), _BF)] * 3,
        scratch_shapes=[pltpu.VMEM((N_HEAD, HEAD_D, HEAD_D), _F32)],
        compiler_params=_params("arbitrary"),
    )(q, k, v, do, r_prev, *consts, cosf, sinf)


def _inproj_bwd(pieces, w_in_t, x, dx2, g1, tm, after=()):
    L = x.shape[0]

    def body(p0, p1, p2, p3, p4, w_ref, x_ref, dx2_ref, g_ref, dx_ref, dg_ref):
        @pl.when(pl.program_id(0) == 0)
        def _():
            dg_ref[...] = jnp.zeros_like(dg_ref)

        for rows in _row_chunks(tm):
            dh = None
            for j, p in enumerate((p0, p1, p2, p3, p4)):
                part = _dot(p[rows, :].astype(_BF), w_ref[j * RET_W:(j + 1) * RET_W, :])
                dh = part if dh is None else dh + part
            dz, dgr = _rms_bwd(x_ref[rows, :], g_ref[...], dh)
            dx_ref[rows, :] = dx2_ref[rows, :] + dz
            dg_ref[...] += jnp.sum(dgr, axis=0, keepdims=True)

    body, in_specs, operands = _ordered(
        body, [_row_spec(tm, RET_W)] * 5 + [_weight_spec((IN_COLS, D_MODEL)), _row_spec(tm, D_MODEL),
                                             _row_spec(tm, D_MODEL), _full_spec((1, D_MODEL))],
        (*pieces, w_in_t, x, dx2, g1), after)
    return pl.pallas_call(
        body, name="inproj_bwd", grid=(L // tm,),
        in_specs=in_specs,
        out_specs=[_row_spec(tm, D_MODEL), _full_spec((1, D_MODEL))],
        out_shape=[jax.ShapeDtypeStruct((L, D_MODEL), _F32), jax.ShapeDtypeStruct((1, D_MODEL), _F32)],
        compiler_params=_params("arbitrary"),
    )(*operands)


def _sum_adamw(parts, w, m, v, tr, name):
    _, R, Cc = parts.shape

    def body(p_ref, w_ref, m_ref, v_ref, g_ref, d_ref, nm_ref, nv_ref):
        gv = p_ref[0].astype(_F32)
        for s in range(1, N_DEV):
            gv = gv + p_ref[s].astype(_F32)
        g_ref[...] = gv
        nm = ADAM_B1 * m_ref[...] + (1.0 - ADAM_B1) * gv
        nv = ADAM_B2 * v_ref[...] + (1.0 - ADAM_B2) * (gv * gv)
        m_hat = nm / (1.0 - ADAM_B1 ** ADAM_STEP)
        v_hat = nv / (1.0 - ADAM_B2 ** ADAM_STEP)
        d_ref[...] = -ADAM_LR * (m_hat / (jnp.sqrt(v_hat) + ADAM_EPS) + ADAM_WD * w_ref[...])
        nm_ref[...] = nm
        nv_ref[...] = nv

    spec = _row_spec(tr, Cc)
    return pl.pallas_call(
        body, name=name, grid=(R // tr,),
        in_specs=[pl.BlockSpec((N_DEV, tr, Cc), lambda i: (0, i, 0))] + [spec] * 3, out_specs=[spec] * 4,
        out_shape=[jax.ShapeDtypeStruct((R, Cc), _F32)] * 4,
        compiler_params=_params("parallel"),
    )(parts, w, m, v)


def _my_place():
    return lax.axis_index("x"), lax.axis_index("y"), lax.axis_index("c")


def _all_gather(blocks):
    n = len(blocks)

    def body(*refs):
        x_refs, out_refs, done_ref = refs[:n], refs[n:2 * n], refs[2 * n]
        send_sems, recv_sems, local_sems = refs[2 * n + 1:]
        done_ref[...] = jnp.zeros_like(done_ref)
        x, y, c = _my_place()
        me, sibling = (x, y, c), (x, y, 1 - c)
        chips = [(1 - x, y), (x, 1 - y), (1 - x, 1 - y)]

        def slot(a, px, py, pc):
            return out_refs[a].at[4 * px + 2 * py + pc]

        def copy(a, k, blk, to, own=False):
            return pltpu.make_async_remote_copy(
                src_ref=x_refs[a] if own else slot(a, *blk), dst_ref=slot(a, *blk),
                send_sem=send_sems.at[a, k], recv_sem=recv_sems.at[a, k], device_id=to, device_id_type=MESH)

        mine = [pltpu.make_async_copy(x_refs[a], slot(a, *me), local_sems.at[a]) for a in range(n)]
        for cp in mine:
            cp.start()
        first = []
        for a in range(n):
            first.append(copy(a, 0, me, sibling, own=True))
            first += [copy(a, 1 + j, me, (*chip, c), own=True) for j, chip in enumerate(chips)]
        for cp in first:
            cp.start()
        passed = []
        for j, chip in enumerate(chips):
            for a in range(n):
                copy(a, 1 + j, (*chip, c), me).wait_recv()
                fwd = copy(a, 4 + j, (*chip, c), sibling)
                fwd.start()
                passed.append(fwd)
        for a in range(n):
            copy(a, 0, sibling, me).wait_recv()
            for j, chip in enumerate(chips):
                copy(a, 4 + j, (*chip, 1 - c), me).wait_recv()
        for cp in first + passed:
            cp.wait_send()
        for cp in mine:
            cp.wait()

    any_spec = pl.BlockSpec(memory_space=pl.ANY)
    outs = pl.pallas_call(
        body, name="weights_all_gather",
        in_specs=[any_spec] * n, out_specs=[any_spec] * n + [pl.BlockSpec(memory_space=pltpu.VMEM)],
        out_shape=[jax.ShapeDtypeStruct((N_DEV,) + b.shape, b.dtype) for b in blocks]
        + [jax.ShapeDtypeStruct((SUBLANES, LANES), _F32)],
        scratch_shapes=[pltpu.SemaphoreType.DMA((n, 7)), pltpu.SemaphoreType.DMA((n, 7)), pltpu.SemaphoreType.DMA((n,))],
    )(*blocks)
    return outs[:n], outs[n]


def _exchange(bigs, small):
    n = len(bigs)
    r = small.shape[0]

    def body(*refs):
        in_refs, out_refs = refs[:n + 1], refs[n + 1:2 * n + 2]
        send_sems, recv_sems, local_sems = refs[2 * n + 2:]
        x, y, c = _my_place()
        me = 4 * x + 2 * y + c
        own = [pltpu.make_async_copy(in_refs[a].at[me], out_refs[a].at[me], local_sems.at[a]) for a in range(n)]
        own.append(pltpu.make_async_copy(in_refs[n], out_refs[n].at[me], local_sems.at[n]))
        for cp in own:
            cp.start()
        copies = []
        for kk in range(1, N_DEV):
            px, py, pc = x ^ (kk >> 2), y ^ ((kk >> 1) & 1), c ^ (kk & 1)
            peer = 4 * px + 2 * py + pc
            for a in range(n + 1):
                src = in_refs[a].at[peer] if a < n else in_refs[a]
                copies.append(pltpu.make_async_remote_copy(
                    src_ref=src, dst_ref=out_refs[a].at[me],
                    send_sem=send_sems.at[a, kk - 1], recv_sem=recv_sems.at[a, kk - 1],
                    device_id=(px, py, pc), device_id_type=MESH))
        for cp in copies:
            cp.start()
        for cp in copies:
            cp.wait_recv()
        for cp in copies:
            cp.wait_send()
        for cp in own:
            cp.wait()

    any_spec = pl.BlockSpec(memory_space=pl.ANY)
    outs = pl.pallas_call(
        body, name="grad_exchange",
        in_specs=[any_spec] * (n + 1), out_specs=[any_spec] * (n + 1),
        out_shape=[jax.ShapeDtypeStruct(b.shape, b.dtype) for b in bigs]
        + [jax.ShapeDtypeStruct((N_DEV, r, LANES), small.dtype)],
        scratch_shapes=[pltpu.SemaphoreType.DMA((n + 1, 7)), pltpu.SemaphoreType.DMA((n + 1, 7)),
                        pltpu.SemaphoreType.DMA((n + 1,))],
    )(*bigs, small)
    return outs[:n], outs[n]


HBM_SPEC = pl.BlockSpec(memory_space=pltpu.HBM)
SEM_SPEC = pl.BlockSpec(memory_space=pltpu.SEMAPHORE)
DATAFLOW = pltpu.SideEffectType.DATAFLOW_SIDE_EFFECTING


def _my_index():
    x, y, c = _my_place()
    return 4 * x + 2 * y + c


def _landing(own_block):
    zone = lax.empty((N_DEV,) + own_block.shape, own_block.dtype)
    return lax.dynamic_update_index_in_dim(zone, own_block, _my_index(), 0)


def _split_copies(src_refs, land_refs, send_sems, recv_sems, gather, first=0):
    x, y, c = _my_place()
    me = 4 * x + 2 * y + c
    copies = []
    for a, (src, land) in enumerate(zip(src_refs, land_refs)):
        for kk in range(1, N_DEV):
            px, py, pc = x ^ (kk >> 2), y ^ ((kk >> 1) & 1), c ^ (kk & 1)
            peer = 4 * px + 2 * py + pc
            copies.append(pltpu.make_async_remote_copy(
                src_ref=src if gather else src.at[peer], dst_ref=land.at[me],
                send_sem=send_sems.at[(first + a) * 7 + kk - 1], recv_sem=recv_sems.at[(first + a) * 7 + kk - 1],
                device_id=(px, py, pc), device_id_type=MESH))
    return copies


def _split_start(srcs, lands, gather, name):
    n = len(srcs)

    def body(*refs):
        src_refs, land_refs = refs[:n], refs[n:2 * n]
        send_sems, recv_sems = refs[2 * n], refs[2 * n + 1]
        token = refs[-1]
        for cp in _split_copies(src_refs, land_refs, send_sems, recv_sems, gather):
            cp.start()
        token[...] = jnp.zeros_like(token)

    outs = pl.pallas_call(
        body, name=name,
        out_shape=(pltpu.SemaphoreType.DMA((7 * n,)), pltpu.SemaphoreType.DMA((7 * n,)),
                   *[pltpu.HBM(t.shape, t.dtype) for t in srcs], *[pltpu.HBM(t.shape, t.dtype) for t in lands],
                   jax.ShapeDtypeStruct((SUBLANES, LANES), _F32)),
        in_specs=[HBM_SPEC] * (2 * n),
        out_specs=(SEM_SPEC, SEM_SPEC, *[HBM_SPEC] * (2 * n), pl.BlockSpec(memory_space=pltpu.VMEM)),
        input_output_aliases={i: 2 + i for i in range(2 * n)},
        compiler_params=pltpu.CompilerParams(has_side_effects=DATAFLOW),
    )(*[pltpu.with_memory_space_constraint(t, pltpu.HBM) for t in list(srcs) + list(lands)])
    return outs[0], outs[1], outs[2:2 + n], outs[2 + n:2 + 2 * n], outs[-1]


def _split_wait(send_sems, recv_sems, srcs, lands, after, gather, name, first=0):
    n = len(srcs)

    def body(*refs):
        src_refs, land_refs = refs[:n], refs[n:2 * n]
        send_s, recv_s = refs[2 * n], refs[2 * n + 1]
        for cp in _split_copies(src_refs, land_refs, send_s, recv_s, gather, first):
            cp.wait_send()
            cp.wait_recv()

    outs = pl.pallas_call(
        body, name=name,
        out_shape=tuple(pltpu.HBM(t.shape, t.dtype) for t in list(srcs) + list(lands)),
        in_specs=[HBM_SPEC] * (2 * n) + [SEM_SPEC, SEM_SPEC, pl.BlockSpec(memory_space=pl.ANY)],
        out_specs=tuple([HBM_SPEC] * (2 * n)),
        input_output_aliases={i: i for i in range(2 * n)},
        compiler_params=pltpu.CompilerParams(has_side_effects=DATAFLOW),
    )(*srcs, *lands, send_sems, recv_sems, after)
    return outs[n:]


def _discretize(lam_re, lam_im, log_dt, b_re, b_im):
    lr = jnp.minimum(lam_re, -1e-4)
    li = lam_im
    dt = jnp.exp(log_dt)[:, None]
    er = jnp.exp(lr * dt)
    ar, ai = er * jnp.cos(li * dt), er * jnp.sin(li * dt)
    den = lr * lr + li * li
    cr = ((ar - 1.0) * lr + ai * li) / den
    ci = (ai * lr - (ar - 1.0) * li) / den
    bbr = cr[:, :, None] * b_re - ci[:, :, None] * b_im
    bbi = cr[:, :, None] * b_im + ci[:, :, None] * b_re
    return ar, ai, bbr, bbi


def _cmul(ar, ai, br, bi):
    return ar * br - ai * bi, ar * bi + ai * br


def _cpowers(ar, ai, n):
    pr, pi = ar[None], ai[None]
    while pr.shape[0] < n:
        nr, ni = _cmul(pr, pi, pr[-1][None], pi[-1][None])
        pr, pi = jnp.concatenate([pr, nr]), jnp.concatenate([pi, ni])
    return pr[:n], pi[:n]


def _scan_tables(ar, ai, seg, reverse):
    if reverse:
        ai = -ai
    ar, ai = ar.reshape(N_KB, KB_STATES), ai.reshape(N_KB, KB_STATES)
    pr, pi = _cpowers(ar, ai, seg)
    a1 = (pr[-1], pi[-1])
    a2 = _cmul(*a1, *a1)
    a4 = _cmul(*a2, *a2)
    row = jnp.arange(SUBLANES)[None, :, None]
    wide = lambda t: jnp.broadcast_to(t[:, None, :], (N_KB, SUBLANES, KB_STATES))
    tabs = [wide(ar), wide(ai)]
    for dist, (qr, qi) in ((1, a1), (2, a2), (4, a4)):
        keep = (row < SUBLANES - dist) if reverse else (row >= dist)
        tabs += [jnp.where(keep, wide(qr), 0.0), jnp.where(keep, wide(qi), 0.0)]
    tabs += [wide(a1[0]), wide(a1[1])]
    if reverse:
        pr, pi = pr[::-1], pi[::-1]
    pw = jnp.transpose(jnp.concatenate([pr, pi], axis=-1), (1, 0, 2))[:, :, None, :]
    return jnp.stack(tabs, axis=1).astype(_F32), pw.astype(_F32)


def _block_diag_in(br, bi):
    eye = jnp.eye(GROUPS_PER_KB, dtype=_F32)
    one = lambda t: jnp.einsum("kgpc,gh->kgchp", t.reshape(N_KB, GROUPS_PER_KB, N_STATE, SSM_GC), eye).reshape(
        N_KB, LANES, KB_STATES)
    return jnp.concatenate([one(br), one(bi)], axis=-1)


def _block_diag_in_t(dmat):
    d6 = dmat.reshape(N_KB, GROUPS_PER_KB, SSM_GC, 2, GROUPS_PER_KB, N_STATE)
    eye = jnp.eye(GROUPS_PER_KB, dtype=_F32)
    both = jnp.einsum("kgcrhp,gh->rkgpc", d6, eye).reshape(2, N_GROUP, N_STATE, SSM_GC)
    return both[0], both[1]


def _block_diag_out(c_re, c_im):
    eye = jnp.eye(GROUPS_PER_KB, dtype=_F32)
    one = lambda t: jnp.einsum("kgcp,gh->khpgc", t.reshape(N_KB, GROUPS_PER_KB, SSM_GC, N_STATE), eye).reshape(
        N_KB, KB_STATES, LANES)
    return jnp.concatenate([one(c_re), -one(c_im)], axis=1)


def _block_diag_out_t(dmat_t):
    d6 = dmat_t.reshape(N_KB, GROUPS_PER_KB, SSM_GC, 2, GROUPS_PER_KB, N_STATE)
    eye = jnp.eye(GROUPS_PER_KB, dtype=_F32)
    both = jnp.einsum("kgcrhp,gh->rkgcp", d6, eye).reshape(2, N_GROUP, SSM_GC, N_STATE)
    return both[0], -both[1]


SMALL_NAMES = ("norm_mix_pre", "norm_mix_post", "ret_gn_gain", "ssm_lambda_re", "ssm_lambda_im", "ssm_log_dt",
               "ssm_b_re", "ssm_b_im", "ssm_c_re", "ssm_c_im", "ssm_d", "norm_mlp_pre", "norm_mlp_post")


def _local_grads(x, tgt, small, weights, emit, emit_small, tm, tk, tb, zero=0.0):
    L = x.shape[0]
    g1, g2, ggn = small["norm_mix_pre"], small["norm_mix_post"], small["ret_gn_gain"]
    g3, g4, d_skip = small["norm_mlp_pre"], small["norm_mlp_post"], small["ssm_d"]

    rope = _rope_tables(L)
    consts = _ret_consts()

    disc_in = (small["ssm_lambda_re"][0], small["ssm_lambda_im"][0], small["ssm_log_dt"][0] + zero,
               small["ssm_b_re"][0], small["ssm_b_im"][0])
    (ar, ai, bbr, bbi), disc_vjp = jax.vjp(_discretize, *disc_in)
    bmat = _block_diag_in(bbr, bbi).astype(_BF)
    cmat = _block_diag_out(small["ssm_c_re"][0], small["ssm_c_im"][0]).astype(_BF)
    seg = tb // SUBLANES
    tab_f, pw_f = _scan_tables(ar, ai, seg, False)
    tab_r, pw_r = _scan_tables(ar, ai, seg, True)

    h1 = _prenorm(x, g1, min(2 * tm, L), after=(pw_r,))
    (w_in_t,) = weights("in", h1)
    q, k, v, gate, u, cosf, sinf = _inproj_fwd(h1, w_in_t, rope, tm)
    o, y_ret, r_prev = _retention_fwd(q, k, v, gate, ggn, consts)
    s, xs, ent = _s5_fwd(u, bmat, cmat, tab_f, pw_f, d_skip, tb)
    w_glu, w_out = weights("mix", s)
    ys, glu, cat, mix, x2 = _mixout_fwd(s, y_ret, x, w_glu, w_out, g2, min(2 * tm, L))
    w_ff1, w_ff2 = weights("mlp", x2)
    h3, f1 = _ff1_fwd(x2, g3, w_ff1, tm)
    dy, dm, dg4, sq = _ff2_loss(f1, x2, tgt, g4, w_ff2, min(2 * tm, L))

    df1, dw_ff2 = _ff2_bwd(dm, f1, w_ff2, min(1024, L), 1024)
    dx2, dmix, dg3, dg2 = _ff1_bwd(df1, w_ff1, x2, mix, dy, g3, g2, min(2 * tm, L))
    dw_ff1 = _matmul_tn(h3, df1, tk, FF1_COLS, "dw_ff1", slots=True)
    token = emit({"w_ff1": dw_ff1, "w_ff2": dw_ff2})
    dglu, ds, dgate, do, dggn = _mixout_bwd(dmix, w_out, w_glu, glu, s, o, gate, ggn, tm, after=token)
    dw_out = _matmul_tn(cat, dmix, tk, 1024, "dw_out")
    dw_glu = _matmul_tn(ys, dglu, tk, 1024, "dw_glu")
    token = emit({"w_glu": dw_glu, "w_out": dw_out})
    du, dbmat, dcmat, da8, dd = _s5_bwd(u, ds, xs, ent, bmat, cmat, tab_r, pw_r, d_skip, tb, after=token)
    dq, dk, dv = _retention_bwd(q, k, v, do, r_prev, consts, cosf, sinf)
    pieces = (dq, dk, dv, dgate, du)
    dw_in_t = _dw_in_t(pieces, h1, min(1024, L))
    token = emit({"w_in": dw_in_t})

    da = jnp.sum(da8, axis=1)
    dar = da[:, :KB_STATES].reshape(N_GROUP, N_STATE)
    dai = da[:, KB_STATES:].reshape(N_GROUP, N_STATE)
    dbr, dbi = _block_diag_in_t(dbmat)
    dlre, dlim, dldt, dbre, dbim = disc_vjp((dar, dai, dbr, dbi))
    dcre, dcim = _block_diag_out_t(dcmat)

    token2 = emit_small({
        "norm_mix_post": dg2, "ret_gn_gain": dggn,
        "ssm_lambda_re": dlre[None], "ssm_lambda_im": dlim[None], "ssm_log_dt": dldt[None],
        "ssm_b_re": dbre[None], "ssm_b_im": dbim[None], "ssm_c_re": dcre[None], "ssm_c_im": dcim[None],
        "ssm_d": dd, "norm_mlp_pre": dg3, "norm_mlp_post": dg4,
    }, sq)
    gx, dg1 = _inproj_bwd(pieces, w_in_t, x, dx2, g1, min(2 * tm, L), after=token + token2)
    return gx, dg1


BIG_SHAPES = {"w_in": (D_MODEL, IN_COLS // N_DEV), "w_glu": (SSM_W, 2 * SSM_W // N_DEV), "w_out": (D_MODEL // N_DEV, D_MODEL),
              "w_ff1": (D_MODEL, FF1_COLS), "w_ff2": (D_FF // N_DEV, D_MODEL)}
BIG_NAMES = ("w_in", "w_glu", "w_out", "w_ff1", "w_ff2")


def _cols_from_slots(g):
    return jnp.transpose(g, (1, 0, 2)).reshape(g.shape[1], N_DEV * g.shape[2])


def _cols_to_slots(dw):
    r, cols = dw.shape
    return jnp.transpose(dw.reshape(r, N_DEV, cols // N_DEV), (1, 0, 2))


WEIGHT_GROUPS = {"in": ("w_in",), "mix": ("w_glu", "w_out"), "mlp": ("w_ff1", "w_ff2")}


def _weight_from_slots(name, g):
    if name == "w_glu":
        return _cols_from_slots(g)
    if name == "w_ff1":
        return g
    return g.reshape(N_DEV * g.shape[1], g.shape[2])


def _grad_slots(name, dw):
    if name == "w_glu":
        return _cols_to_slots(dw)
    if name == "w_ff1":
        return dw
    if name == "w_in":
        return dw.reshape(N_DEV, BIG_SHAPES[name][1], BIG_SHAPES[name][0])
    return dw.reshape((N_DEV,) + BIG_SHAPES[name])


PIECE_ROWS = 8


def _small_layout(shapes):
    off, rows = {}, 0
    for n in SMALL_NAMES:
        off[n] = rows
        rows += -(-math.prod(shapes[n]) // (PIECE_ROWS * LANES)) * PIECE_ROWS
    return off, rows, rows + PIECE_ROWS


def _pack_small(vals, shapes, last=None):
    parts = []
    for n in SMALL_NAMES:
        flat = vals[n].reshape(-1).astype(_F32)
        pad = -flat.shape[0] % (PIECE_ROWS * LANES)
        if pad:
            flat = jnp.concatenate([flat, jnp.zeros((pad,), _F32)])
        parts.append(flat.reshape(-1, LANES))
    parts.append(jnp.zeros((PIECE_ROWS, LANES), _F32) if last is None else last)
    return jnp.concatenate(parts, axis=0)


def _unpack_small(buf, shapes):
    off, _, _ = _small_layout(shapes)
    out = {}
    for n in SMALL_NAMES:
        size = math.prod(shapes[n])
        rows = -(-size // LANES)
        out[n] = buf[off[n]:off[n] + rows].reshape(-1)[:size].reshape(shapes[n])
    return out


WEIGHT_NAMES = ('norm_mix_pre', 'norm_mix_post', 'w_in', 'ret_gn_gain', 'ssm_lambda_re', 'ssm_lambda_im', 'ssm_log_dt',
                'ssm_b_re', 'ssm_b_im', 'ssm_c_re', 'ssm_c_im', 'ssm_d', 'w_glu', 'w_out', 'norm_mlp_pre',
                'norm_mlp_post', 'w_ff1', 'w_ff2')


def kernel(x, norm_mix_pre, norm_mix_post, w_in, ret_gn_gain, ssm_lambda_re, ssm_lambda_im, ssm_log_dt, ssm_b_re, ssm_b_im, ssm_c_re, ssm_c_im, ssm_d, w_glu, w_out, norm_mlp_pre, norm_mlp_post, w_ff1, w_ff2, loss_target, m_norm_mix_pre, m_norm_mix_post, m_w_in, m_ret_gn_gain, m_ssm_lambda_re, m_ssm_lambda_im, m_ssm_log_dt, m_ssm_b_re, m_ssm_b_im, m_ssm_c_re, m_ssm_c_im, m_ssm_d, m_w_glu, m_w_out, m_norm_mlp_pre, m_norm_mlp_post, m_w_ff1, m_w_ff2, v_norm_mix_pre, v_norm_mix_post, v_w_in, v_ret_gn_gain, v_ssm_lambda_re, v_ssm_lambda_im, v_ssm_log_dt, v_ssm_b_re, v_ssm_b_im, v_ssm_c_re, v_ssm_c_im, v_ssm_d, v_w_glu, v_w_out, v_norm_mlp_pre, v_norm_mlp_post, v_w_ff1, v_w_ff2):
    args = dict(locals())
    w = {n: args[n] for n in WEIGHT_NAMES}
    m = {n: args["m_" + n] for n in WEIGHT_NAMES}
    v = {n: args["v_" + n] for n in WEIGHT_NAMES}
    L = x.shape[1]
    tm = min(256, L)
    tk = min(2048, L)
    tb = min(512, L)

    order = [n for names in WEIGHT_GROUPS.values() for n in names]
    blocks = [(w[n][0].T if n == "w_in" else w[n][0]).astype(_BF) for n in order]
    gathered = _split_start(blocks, [_landing(b) for b in blocks], True, "weights_start")
    zero = gathered[4][0, 0]

    def weights(group, after):
        names = WEIGHT_GROUPS[group]
        first = order.index(names[0])
        part = slice(first, first + len(names))
        landed = _split_wait(gathered[0], gathered[1], gathered[2][part], gathered[3][part], after, True,
                             "weights_wait_" + group, first=first)
        return [_weight_from_slots(n, g) for n, g in zip(names, landed)]

    in_flight = []

    def emit(dws):
        names = sorted(dws)
        srcs = [_grad_slots(n, dws[n]) for n in names]
        lands = [_landing(lax.dynamic_index_in_dim(t, _my_index(), 0, keepdims=False)) for t in srcs]
        started = _split_start(srcs, lands, False, "grads_start_" + "_".join(names))
        in_flight.append((names, started))
        return (started[4],)

    shapes = {n: w[n].shape for n in SMALL_NAMES}
    first_piece = {SMALL_NAMES[0]: jnp.zeros(shapes[SMALL_NAMES[0]], _F32)}
    small_flight = []

    def emit_small(gs, sq):
        loss_rows = jnp.broadcast_to(0.5 / D_MODEL * jnp.sum(sq), (PIECE_ROWS, LANES)).astype(_F32)
        buf = _pack_small({**first_piece, **gs}, shapes, loss_rows)
        small_flight.append(_split_start([buf], [_landing(buf)], True, "small_grads_start"))
        return (small_flight[0][4],)

    small_w = {n: w[n] for n in SMALL_NAMES}
    gx, dg1 = _local_grads(x[0], loss_target[0], small_w, weights, emit, emit_small, tm, tk, tb, zero=zero)
    last_buf = dg1.reshape(PIECE_ROWS, LANES)
    last_started = _split_start([last_buf], [_landing(last_buf)], True, "last_grad_start")

    grads, delta, new_m, new_v = {}, {}, {}, {}
    after = last_started[4]
    for names, started in in_flight:
        landed = _split_wait(*started[:4], after, False, "grads_wait_" + "_".join(names))
        for n, parts in zip(names, landed):
            flip = (lambda t: t.T) if n == "w_in" else (lambda t: t)
            res = _sum_adamw(parts, flip(w[n][0]), flip(m[n][0]), flip(v[n][0]), math.gcd(256, parts.shape[1]), "adamw_" + n)
            grads[n], delta[n], new_m[n], new_v[n] = (flip(t)[None] for t in res)
        after = res[1]
    small_parts = _split_wait(*small_flight[0][:4], after, True, "small_grads_wait")[0]
    last_parts = _split_wait(*last_started[:4], small_parts, True, "last_grad_wait")[0]
    small_parts = lax.dynamic_update_slice(small_parts, last_parts, (0, 0, 0))
    sw, sm, sv = _pack_small(w, shapes), _pack_small(m, shapes), _pack_small(v, shapes)
    res = _sum_adamw(small_parts, sw, sm, sv, sw.shape[0], "adamw_small")
    for dst, buf in zip((grads, delta, new_m, new_v), res):
        dst.update(_unpack_small(buf, shapes))
    _, loss_at, _ = _small_layout(shapes)
    loss = res[0][loss_at, 0]

    return (loss, gx[None], *[grads[n] for n in WEIGHT_NAMES], *[delta[n] for n in WEIGHT_NAMES],
            *[new_m[n] for n in WEIGHT_NAMES], *[new_v[n] for n in WEIGHT_NAMES])
```

```python
import math

import jax
import jax.numpy as jnp
from jax import lax
from jax.experimental import pallas as pl
from jax.experimental.pallas import tpu as pltpu

_BF = jnp.bfloat16
_F32 = jnp.float32

D_MODEL = 1024
RET_W = 512
N_HEAD = 4
HEAD_D = 128
CHUNK = 256
ROPE_CHUNK = 128
SSM_W = 512
SSM_GC = 16
N_GROUP = 32
N_STATE = 64
GROUPS_PER_KB = 8
N_KB = 4
KB_STATES = GROUPS_PER_KB * N_STATE
D_FF = 4096
IN_COLS = 2560
NORM_EPS = 1e-6
ROPE_BASE = 10000.0
N_DEV = 8

ADAM_LR = 0.001
ADAM_B1 = 0.9
ADAM_B2 = 0.999
ADAM_EPS = 1e-08
ADAM_WD = 0.01
ADAM_STEP = 10

SUBLANES = 8
LANES = 128
VMEM_LIMIT = 52 * 1024 * 1024
RET_STEP_CHUNKS = 2
KB_PER_STEP = 2
SCAN_UNROLL = True
FIX_UNROLL = 8

MESH = pl.DeviceIdType.MESH


def _params(*sem):
    return pltpu.CompilerParams(dimension_semantics=sem, vmem_limit_bytes=VMEM_LIMIT)


def _dot(a, b):
    return jnp.dot(a, b, preferred_element_type=_F32)


def _dot_nt(a, b):
    return lax.dot_general(a, b, (((1,), (1,)), ((), ())), preferred_element_type=_F32)


def _dot_tn(a, b):
    return lax.dot_general(a, b, (((0,), (0,)), ((), ())), preferred_element_type=_F32)


def _rms_r(z):
    return lax.rsqrt(jnp.mean(z * z, axis=-1, keepdims=True) + NORM_EPS)


def _rms_bwd(z, g, dn):
    r = _rms_r(z)
    t = dn * g
    dz = r * t - z * (r * r * r * jnp.mean(t * z, axis=-1, keepdims=True))
    return dz, dn * z * r


def _rope(t, cs, sn):
    return t * cs + pltpu.roll(t, HEAD_D // 2, 1) * sn


def _rope_t(t, cs, sn):
    return t * cs - pltpu.roll(t, HEAD_D // 2, 1) * sn


def _sigmoid(z):
    return 1.0 / (1.0 + jnp.exp(-z))


_GELU_C = math.sqrt(2.0 / math.pi)


def _gelu(z):
    return 0.5 * z * (1.0 + jnp.tanh(_GELU_C * (z + 0.044715 * z * z * z)))


def _gelu_grad(z):
    th = jnp.tanh(_GELU_C * (z + 0.044715 * z * z * z))
    return 0.5 * (1.0 + th) + 0.5 * z * (1.0 - th * th) * _GELU_C * (1.0 + 3 * 0.044715 * z * z)


ROW_CHUNK = 256


def _row_chunks(tm):
    return [pl.ds(i, min(ROW_CHUNK, tm)) for i in range(0, tm, ROW_CHUNK)]


def _ordered(body, in_specs, operands, after):
    k = len(after)
    if not k:
        return body, list(in_specs), tuple(operands)
    return ((lambda *refs: body(*refs[k:])), [pl.BlockSpec(memory_space=pl.ANY)] * k + list(in_specs),
            tuple(after) + tuple(operands))


def _row_spec(tm, n):
    return pl.BlockSpec((tm, n), lambda i: (i, 0))


def _full_spec(shape):
    nd = len(shape)
    return pl.BlockSpec(shape, lambda *_: (0,) * nd)


def _weight_spec(shape):
    nd = len(shape)
    return pl.BlockSpec(shape, lambda *_: (0,) * nd, pipeline_mode=pl.Buffered(1))


def _rope_tables(L):
    half = HEAD_D // 2
    inv_freq = ROPE_BASE ** (-jnp.arange(half, dtype=_F32) / half)
    twice = lambda t: jnp.concatenate([t, t], axis=-1)
    off = jnp.arange(ROPE_CHUNK, dtype=_F32)[:, None] * inv_freq[None, :]
    start = (ROPE_CHUNK * jnp.arange(L // ROPE_CHUNK, dtype=_F32))[:, None] * inv_freq[None, :]
    return (twice(jnp.cos(off)), twice(jnp.sin(off)),
            twice(jnp.cos(start))[:, None, :], twice(jnp.sin(start))[:, None, :])


def _prenorm(x, g, tm, after=()):
    L = x.shape[0]

    def body(x_ref, g_ref, h_ref):
        xv = x_ref[...]
        h_ref[...] = (xv * _rms_r(xv) * g_ref[...]).astype(_BF)

    body, in_specs, operands = _ordered(body, [_row_spec(tm, D_MODEL), _full_spec((1, D_MODEL))], (x, g), after)
    return pl.pallas_call(
        body, name="prenorm", grid=(L // tm,),
        in_specs=in_specs, out_specs=_row_spec(tm, D_MODEL),
        out_shape=jax.ShapeDtypeStruct((L, D_MODEL), _BF),
        compiler_params=_params("parallel"),
    )(*operands)


def _inproj_fwd(h, w_in_t, rope, tm):
    L = h.shape[0]
    n_chunks = tm // ROPE_CHUNK

    def body(h_ref, w_ref, co_ref, so_ref, cs_ref, ss_ref, q_ref, k_ref, v_ref, gate_ref, u_ref, cos_ref, sin_ref):
        proj = _dot_nt(h_ref[...], w_ref[...])
        lane = lax.broadcasted_iota(jnp.int32, (ROPE_CHUNK, HEAD_D), 1)
        sign = jnp.where(lane < HEAD_D // 2, -1.0, 1.0)
        co, so = co_ref[...], so_ref[...]
        for c in range(n_chunks):
            chunk = pl.program_id(0) * n_chunks + c
            cst, sst = cs_ref[chunk], ss_ref[chunk]
            rows = slice(c * ROPE_CHUNK, (c + 1) * ROPE_CHUNK)
            cs = co * cst - so * sst
            sn = (so * cst + co * sst) * sign
            cos_ref[rows, :] = cs
            sin_ref[rows, :] = sn
            for hh in range(N_HEAD):
                lo = hh * HEAD_D
                q_ref[rows, lo:lo + HEAD_D] = _rope(proj[rows, lo:lo + HEAD_D], cs, sn).astype(_BF)
                kh = _rope(proj[rows, RET_W + lo:RET_W + lo + HEAD_D], cs, sn) * (HEAD_D ** -0.5)
                k_ref[rows, lo:lo + HEAD_D] = kh.astype(_BF)
        v_ref[...] = proj[:, 2 * RET_W:3 * RET_W].astype(_BF)
        gate_ref[...] = proj[:, 3 * RET_W:4 * RET_W]
        u_ref[...] = proj[:, 4 * RET_W:]

    nc = L // ROPE_CHUNK
    return pl.pallas_call(
        body, name="inproj_fwd", grid=(L // tm,),
        in_specs=[_row_spec(tm, D_MODEL), _weight_spec((IN_COLS, D_MODEL)),
                  _full_spec((ROPE_CHUNK, HEAD_D)), _full_spec((ROPE_CHUNK, HEAD_D)),
                  _full_spec((nc, 1, HEAD_D)), _full_spec((nc, 1, HEAD_D))],
        out_specs=[_row_spec(tm, RET_W)] * 5 + [_row_spec(tm, HEAD_D)] * 2,
        out_shape=[jax.ShapeDtypeStruct((L, RET_W), _BF)] * 3 + [jax.ShapeDtypeStruct((L, RET_W), _F32)] * 2
        + [jax.ShapeDtypeStruct((L, HEAD_D), _F32)] * 2,
        compiler_params=_params("parallel"),
    )(h, w_in_t, *rope)


def _ret_consts():
    lg = jnp.log(1.0 - jnp.exp(jnp.linspace(math.log(1.0 / 32), math.log(1.0 / 512), N_HEAD))).astype(_F32)
    idx = jnp.arange(CHUNK, dtype=_F32)
    diff = idx[:, None] - idx[None, :]
    decay = jnp.where(diff[None] >= 0, jnp.exp(jnp.maximum(diff, 0.0)[None] * lg[:, None, None]), 0.0)
    zeta = jnp.exp((CHUNK - 1 - idx)[None, :] * lg[:, None])
    xi = jnp.exp((idx + 1.0)[None, :] * lg[:, None])
    gc = jnp.exp(CHUNK * lg)
    wide = lambda t: jnp.broadcast_to(t[:, :, None], (N_HEAD, CHUNK, HEAD_D)).astype(_F32)
    gcw = jnp.broadcast_to(gc[:, None, None], (N_HEAD, SUBLANES, HEAD_D)).astype(_F32)
    return decay.astype(_F32), wide(xi), wide(zeta), gcw


def _head_specs():
    wide = _full_spec((N_HEAD, CHUNK, HEAD_D))
    return [_full_spec((N_HEAD, CHUNK, CHUNK)), wide, wide, _full_spec((N_HEAD, SUBLANES, HEAD_D))]


def _retention_fwd(q, k, v, gate, ggn, consts):
    L = q.shape[0]
    nc = L // CHUNK
    cps = math.gcd(RET_STEP_CHUNKS, nc)
    blk = pl.BlockSpec((cps * CHUNK, RET_W), lambda n: (n, 0))

    def body(q_ref, k_ref, v_ref, gate_ref, ggn_ref, dm_ref, xi_ref, zeta_ref, gc_ref,
             o_ref, y_ref, rp_ref, r_scr):
        @pl.when(pl.program_id(0) == 0)
        def _():
            r_scr[...] = jnp.zeros_like(r_scr)

        for hh in range(N_HEAD):
            cols = slice(hh * HEAD_D, (hh + 1) * HEAD_D)
            state = r_scr[hh]
            for c in range(cps):
                rows = slice(c * CHUNK, (c + 1) * CHUNK)
                qv, kv, vv = q_ref[rows, cols], k_ref[rows, cols], v_ref[rows, cols]
                s = _dot_nt(qv, kv) * dm_ref[hh]
                o = _dot(s.astype(_BF), vv) + _dot(qv, state.astype(_BF)) * xi_ref[hh]
                o_ref[rows, cols] = o
                rp_ref[hh, c] = state
                vz = (vv.astype(_F32) * zeta_ref[hh]).astype(_BF)
                state = gc_ref[hh, 0:1, :] * state + _dot_tn(kv, vz)
                dlt = o - jnp.mean(o, axis=-1, keepdims=True)
                on = dlt * lax.rsqrt(jnp.mean(dlt * dlt, axis=-1, keepdims=True) + NORM_EPS)
                gt = gate_ref[rows, cols]
                y_ref[rows, cols] = (gt * _sigmoid(gt) * (on * ggn_ref[:, cols])).astype(_BF)
            r_scr[hh] = state

    return pl.pallas_call(
        body, name="retention_fwd", grid=(nc // cps,),
        in_specs=[blk, blk, blk, blk, _full_spec((1, RET_W))] + _head_specs(),
        out_specs=[blk, blk, pl.BlockSpec((N_HEAD, cps, HEAD_D, HEAD_D), lambda n: (0, n, 0, 0))],
        out_shape=[jax.ShapeDtypeStruct((L, RET_W), _F32), jax.ShapeDtypeStruct((L, RET_W), _BF),
                   jax.ShapeDtypeStruct((N_HEAD, nc, HEAD_D, HEAD_D), _F32)],
        scratch_shapes=[pltpu.VMEM((N_HEAD, HEAD_D, HEAD_D), _F32)],
        compiler_params=_params("arbitrary"),
    )(q, k, v, gate, ggn, *consts)


def _rows_to_segments(dst_scr, src_ref, seg):
    for g in range(dst_scr.shape[0]):
        for j in range(SUBLANES):
            dst_scr[g, pl.ds(j, seg, stride=SUBLANES), :] = src_ref[pl.ds(j * seg, seg), g * LANES:(g + 1) * LANES]


def _segments_to_rows(dst_ref, src_scr, seg):
    for g in range(src_scr.shape[0]):
        for j in range(SUBLANES):
            dst_ref[pl.ds(j * seg, seg), g * LANES:(g + 1) * LANES] = src_scr[g, pl.ds(j, seg, stride=SUBLANES), :]


def _scan_segments(x_ref, tab_ref, pw_ref, carry_ref, seg, reverse, entry_ref=None, fwd_ref=None, fwd_entry_ref=None,
                   da_ref=None):
    G = x_ref.shape[0]
    W = KB_STATES
    re, im = pl.ds(0, W), pl.ds(W, W)
    row_id = lax.broadcasted_iota(jnp.int32, (SUBLANES, W), 0)
    edge_in = (row_id == SUBLANES - 1) if reverse else (row_id == 0)
    edge_out = 0 if reverse else SUBLANES - 1
    a_tab = [(tab_ref[g, 0], tab_ref[g, 1]) for g in range(G)]

    def local(i, st):
        r = (seg - 1 - i) if reverse else i
        out = []
        for g in range(G):
            (ar, ai), (sr, si) = a_tab[g], st[g]
            nr = ar * sr - ai * si + x_ref[g, r, :, re]
            ni = ar * si + ai * sr + x_ref[g, r, :, im]
            x_ref[g, r, :, re] = nr
            x_ref[g, r, :, im] = ni
            out.append((nr, ni))
        return tuple(out)

    zero = jnp.zeros((SUBLANES, W), _F32)
    ends = lax.fori_loop(0, seg, local, tuple((zero, zero) for _ in range(G)), unroll=SCAN_UNROLL)

    entry = []
    shift = (SUBLANES - 1) if reverse else 1
    for g in range(G):
        er, ei = ends[g]
        fr = jnp.where(edge_in, carry_ref[g, :, re], pltpu.roll(er, shift, 0))
        fi = jnp.where(edge_in, carry_ref[g, :, im], pltpu.roll(ei, shift, 0))
        for j, dist in enumerate((1, 2, 4)):
            pr, pi = tab_ref[g, 2 + 2 * j], tab_ref[g, 3 + 2 * j]
            sh = (SUBLANES - dist) if reverse else dist
            sr, si = pltpu.roll(fr, sh, 0), pltpu.roll(fi, sh, 0)
            fr, fi = fr + pr * sr - pi * si, fi + pr * si + pi * sr
        br, bi = tab_ref[g, 8], tab_ref[g, 9]
        outr = br * fr - bi * fi + er
        outi = br * fi + bi * fr + ei
        carry_ref[g, :, re] = jnp.broadcast_to(outr[edge_out:edge_out + 1, :], (SUBLANES, W))
        carry_ref[g, :, im] = jnp.broadcast_to(outi[edge_out:edge_out + 1, :], (SUBLANES, W))
        entry.append((fr, fi))
        if entry_ref is not None:
            entry_ref[g, :, re] = fr
            entry_ref[g, :, im] = fi

    add_da = da_ref is not None

    def fix(r, st, first=False):
        out = []
        for g in range(G):
            fr, fi = entry[g]
            pwr, pwi = pw_ref[g, r, :, re], pw_ref[g, r, :, im]
            xr = x_ref[g, r, :, re] + (pwr * fr - pwi * fi)
            xi = x_ref[g, r, :, im] + (pwr * fi + pwi * fr)
            x_ref[g, r, :, re] = xr
            x_ref[g, r, :, im] = xi
            if add_da:
                prev = fwd_entry_ref.at[g] if first else fwd_ref.at[g, r - 1]
                xpr, xpi = prev[:, re], prev[:, im]
                out.append((st[g][0] + (xr * xpr + xi * xpi), st[g][1] + (xi * xpr - xr * xpi)))
            else:
                out.append(st[g])
        return tuple(out)

    if add_da:
        st = fix(0, tuple((zero, zero) for _ in range(G)), first=True)
        st = lax.fori_loop(1, seg, fix, st, unroll=SCAN_UNROLL)
        for g in range(G):
            da_ref[g, :, re] += st[g][0]
            da_ref[g, :, im] += st[g][1]
    else:
        lax.fori_loop(0, seg, fix, tuple((zero[0:1, 0:LANES],) for _ in range(G)), unroll=FIX_UNROLL)


def _s5_specs(seg, time=lambda t: t):
    G = KB_PER_STEP
    return dict(
        x=pl.BlockSpec((G, seg, SUBLANES, 2 * KB_STATES), lambda kb, t: (kb, time(t), 0, 0)),
        ent=pl.BlockSpec((G, 1, SUBLANES, 2 * KB_STATES), lambda kb, t: (kb, time(t), 0, 0)),
        b=pl.BlockSpec((G, LANES, 2 * KB_STATES), lambda kb, t: (kb, 0, 0)),
        c=pl.BlockSpec((G, 2 * KB_STATES, LANES), lambda kb, t: (kb, 0, 0)),
        tab=pl.BlockSpec((G, 10, SUBLANES, KB_STATES), lambda kb, t: (kb, 0, 0, 0)),
        pw=pl.BlockSpec((G, seg, 1, 2 * KB_STATES), lambda kb, t: (kb, 0, 0, 0)),
        d=pl.BlockSpec((1, G * LANES), lambda kb, t: (0, kb)),
    )


def _s5_fwd(u, bmat, cmat, tab_f, pw_f, d_skip, tb):
    L = u.shape[0]
    nt = L // tb
    seg = tb // SUBLANES
    G = KB_PER_STEP
    ucol = pl.BlockSpec((tb, G * LANES), lambda kb, t: (t, kb))
    sp = _s5_specs(seg)

    def body(u_ref, b_ref, c_ref, tab_ref, pw_ref, d_ref, s_ref, x_ref, ent_ref, up_scr, y_scr, carry_scr):
        @pl.when(pl.program_id(1) == 0)
        def _():
            carry_scr[...] = jnp.zeros_like(carry_scr)

        _rows_to_segments(up_scr, u_ref, seg)
        for g in range(G):
            x_ref[g] = _dot(up_scr[g].astype(_BF), b_ref[g]).reshape(seg, SUBLANES, 2 * KB_STATES)
        _scan_segments(x_ref, tab_ref, pw_ref, carry_scr, seg, reverse=False, entry_ref=ent_ref.at[:, 0])
        for g in range(G):
            y = _dot(x_ref[g].reshape(tb, 2 * KB_STATES).astype(_BF), c_ref[g])
            y_scr[g] = y + d_ref[:, g * LANES:(g + 1) * LANES] * up_scr[g]
        _segments_to_rows(s_ref, y_scr, seg)

    return pl.pallas_call(
        body, name="s5_fwd", grid=(N_KB // G, nt),
        in_specs=[ucol, sp["b"], sp["c"], sp["tab"], sp["pw"], sp["d"]],
        out_specs=[ucol, sp["x"], sp["ent"]],
        out_shape=[jax.ShapeDtypeStruct((L, SSM_W), _F32),
                   jax.ShapeDtypeStruct((N_KB, L // SUBLANES, SUBLANES, 2 * KB_STATES), _F32),
                   jax.ShapeDtypeStruct((N_KB, nt, SUBLANES, 2 * KB_STATES), _F32)],
        scratch_shapes=[pltpu.VMEM((G, tb, LANES), _F32)] * 2 + [pltpu.VMEM((G, SUBLANES, 2 * KB_STATES), _F32)],
        compiler_params=_params("parallel", "arbitrary"),
    )(u, bmat, cmat, tab_f, pw_f, d_skip)


def _mixout_fwd(s, y_ret, x, w_glu, w_out, g2, tm):
    L = s.shape[0]

    def body(s_ref, yr_ref, x_ref, wg_ref, wo_ref, g_ref, ys_ref, glu_ref, cat_ref, mix_ref, x2_ref):
        for rows in _row_chunks(tm):
            ys = _gelu(s_ref[rows, :]).astype(_BF)
            ys_ref[rows, :] = ys
            glu = _dot(ys, wg_ref[...])
            glu_ref[rows, :] = glu
            cat_ref[rows, :RET_W] = yr_ref[rows, :]
            cat_ref[rows, RET_W:] = (glu[:, :SSM_W] * _sigmoid(glu[:, SSM_W:])).astype(_BF)
            mix = _dot(cat_ref[rows, :], wo_ref[...])
            mix_ref[rows, :] = mix
            x2_ref[rows, :] = x_ref[rows, :] + mix * _rms_r(mix) * g_ref[...]

    return pl.pallas_call(
        body, name="mixout_fwd", grid=(L // tm,),
        in_specs=[_row_spec(tm, SSM_W), _row_spec(tm, RET_W), _row_spec(tm, D_MODEL),
                  _weight_spec((SSM_W, 2 * SSM_W)), _weight_spec((D_MODEL, D_MODEL)), _full_spec((1, D_MODEL))],
        out_specs=[_row_spec(tm, SSM_W), _row_spec(tm, 2 * SSM_W), _row_spec(tm, D_MODEL),
                   _row_spec(tm, D_MODEL), _row_spec(tm, D_MODEL)],
        out_shape=[jax.ShapeDtypeStruct((L, SSM_W), _BF), jax.ShapeDtypeStruct((L, 2 * SSM_W), _F32),
                   jax.ShapeDtypeStruct((L, D_MODEL), _BF), jax.ShapeDtypeStruct((L, D_MODEL), _F32),
                   jax.ShapeDtypeStruct((L, D_MODEL), _F32)],
        compiler_params=_params("parallel"),
    )(s, y_ret, x, w_glu, w_out, g2)


FF1_COLS = D_FF // N_DEV


def _ff1_fwd(x2, g3, w1, tm):
    L = x2.shape[0]

    def body(x_ref, g_ref, w_ref, h_ref, f_ref):
        xv = x_ref[...]
        h = (xv * _rms_r(xv) * g_ref[...]).astype(_BF)
        h_ref[...] = h
        for j in range(N_DEV):
            f_ref[:, j * FF1_COLS:(j + 1) * FF1_COLS] = _dot(h, w_ref[j])

    return pl.pallas_call(
        body, name="ff1_fwd", grid=(L // tm,),
        in_specs=[_row_spec(tm, D_MODEL), _full_spec((1, D_MODEL)), _weight_spec((N_DEV, D_MODEL, FF1_COLS))],
        out_specs=[_row_spec(tm, D_MODEL), _row_spec(tm, D_FF)],
        out_shape=[jax.ShapeDtypeStruct((L, D_MODEL), _BF), jax.ShapeDtypeStruct((L, D_FF), _F32)],
        compiler_params=_params("parallel"),
    )(x2, g3, w1)


def _ff2_loss(f1, x2, tgt, g4, w2, tm):
    L = f1.shape[0]

    def body(f_ref, x_ref, t_ref, g_ref, w_ref, dy_ref, dm_ref, dg_ref, ls_ref):
        @pl.when(pl.program_id(0) == 0)
        def _():
            dg_ref[...] = jnp.zeros_like(dg_ref)
            ls_ref[...] = jnp.zeros_like(ls_ref)

        g = g_ref[...]
        for rows in _row_chunks(tm):
            rl = jnp.maximum(f_ref[rows, :], 0.0)
            m = _dot((rl * rl).astype(_BF), w_ref[...])
            y = x_ref[rows, :] + m * _rms_r(m) * g
            err = y - t_ref[rows, :]
            ls_ref[...] += jnp.sum(err * err, axis=0, keepdims=True)
            dy = err * (1.0 / D_MODEL)
            dy_ref[rows, :] = dy
            dm, dgr = _rms_bwd(m, g, dy)
            dm_ref[rows, :] = dm.astype(_BF)
            dg_ref[...] += jnp.sum(dgr, axis=0, keepdims=True)

    return pl.pallas_call(
        body, name="ff2_loss", grid=(L // tm,),
        in_specs=[_row_spec(tm, D_FF), _row_spec(tm, D_MODEL), _row_spec(tm, D_MODEL),
                  _full_spec((1, D_MODEL)), _weight_spec((D_FF, D_MODEL))],
        out_specs=[_row_spec(tm, D_MODEL), _row_spec(tm, D_MODEL), _full_spec((1, D_MODEL)), _full_spec((1, D_MODEL))],
        out_shape=[jax.ShapeDtypeStruct((L, D_MODEL), _F32), jax.ShapeDtypeStruct((L, D_MODEL), _BF),
                   jax.ShapeDtypeStruct((1, D_MODEL), _F32), jax.ShapeDtypeStruct((1, D_MODEL), _F32)],
        compiler_params=_params("arbitrary"),
    )(f1, x2, tgt, g4, w2)


def _ff2_bwd(dm, f1, w2, tm, tn):
    L = dm.shape[0]
    last = L // tm - 1

    def body(dm_ref, f_ref, w_ref, df_ref, dw_ref, acc):
        @pl.when(pl.program_id(1) == 0)
        def _():
            acc[...] = jnp.zeros_like(acc)

        dmv = dm_ref[...]
        rl = jnp.maximum(f_ref[...], 0.0)
        df_ref[...] = (_dot_nt(dmv, w_ref[...]) * (2.0 * rl)).astype(_BF)
        acc[...] += _dot_tn((rl * rl).astype(_BF), dmv)

        @pl.when(pl.program_id(1) == last)
        def _():
            dw_ref[...] = acc[...].astype(_BF)

    return pl.pallas_call(
        body, name="ff2_bwd", grid=(D_FF // tn, L // tm),
        in_specs=[pl.BlockSpec((tm, D_MODEL), lambda j, i: (i, 0)), pl.BlockSpec((tm, tn), lambda j, i: (i, j)),
                  pl.BlockSpec((tn, D_MODEL), lambda j, i: (j, 0))],
        out_specs=[pl.BlockSpec((tm, tn), lambda j, i: (i, j)), pl.BlockSpec((tn, D_MODEL), lambda j, i: (j, 0))],
        out_shape=[jax.ShapeDtypeStruct((L, D_FF), _BF), jax.ShapeDtypeStruct((D_FF, D_MODEL), _BF)],
        scratch_shapes=[pltpu.VMEM((tn, D_MODEL), _F32)],
        compiler_params=_params("parallel", "arbitrary"),
    )(dm, f1, w2)


def _ff1_bwd(df1, w1, x2, mix, dy, g3, g2, tm):
    L = df1.shape[0]

    def body(df_ref, w_ref, x2_ref, mix_ref, dy_ref, g3_ref, g2_ref, dx2_ref, dmix_ref, dg3_ref, dg2_ref):
        @pl.when(pl.program_id(0) == 0)
        def _():
            dg3_ref[...] = jnp.zeros_like(dg3_ref)
            dg2_ref[...] = jnp.zeros_like(dg2_ref)

        for rows in _row_chunks(tm):
            dh = _dot_nt(df_ref[rows, 0:FF1_COLS], w_ref[0])
            for j in range(1, N_DEV):
                dh = dh + _dot_nt(df_ref[rows, j * FF1_COLS:(j + 1) * FF1_COLS], w_ref[j])
            dz, dgr = _rms_bwd(x2_ref[rows, :], g3_ref[...], dh)
            dg3_ref[...] += jnp.sum(dgr, axis=0, keepdims=True)
            dx2 = dy_ref[rows, :] + dz
            dx2_ref[rows, :] = dx2
            dmx, dgr2 = _rms_bwd(mix_ref[rows, :], g2_ref[...], dx2)
            dg2_ref[...] += jnp.sum(dgr2, axis=0, keepdims=True)
            dmix_ref[rows, :] = dmx.astype(_BF)

    vec = _full_spec((1, D_MODEL))
    return pl.pallas_call(
        body, name="ff1_bwd", grid=(L // tm,),
        in_specs=[_row_spec(tm, D_FF), _weight_spec((N_DEV, D_MODEL, FF1_COLS)), _row_spec(tm, D_MODEL),
                  _row_spec(tm, D_MODEL), _row_spec(tm, D_MODEL), vec, vec],
        out_specs=[_row_spec(tm, D_MODEL), _row_spec(tm, D_MODEL), vec, vec],
        out_shape=[jax.ShapeDtypeStruct((L, D_MODEL), _F32), jax.ShapeDtypeStruct((L, D_MODEL), _BF),
                   jax.ShapeDtypeStruct((1, D_MODEL), _F32), jax.ShapeDtypeStruct((1, D_MODEL), _F32)],
        compiler_params=_params("arbitrary"),
    )(df1, w1, x2, mix, dy, g3, g2)


def _matmul_tn(a, b, tm, tn, name, slots=False):
    L, K = a.shape
    N = b.shape[1]
    last = L // tm - 1

    def body(a_ref, b_ref, o_ref, acc):
        @pl.when(pl.program_id(1) == 0)
        def _():
            acc[...] = jnp.zeros_like(acc)

        acc[...] += _dot_tn(a_ref[...].astype(_BF), b_ref[...].astype(_BF))

        @pl.when(pl.program_id(1) == last)
        def _():
            if slots:
                o_ref[0] = acc[...].astype(_BF)
            else:
                o_ref[...] = acc[...].astype(_BF)

    if slots:
        out_spec = pl.BlockSpec((1, K, tn), lambda j, i: (j, 0, 0))
        out_shape = jax.ShapeDtypeStruct((N // tn, K, tn), _BF)
    else:
        out_spec = pl.BlockSpec((K, tn), lambda j, i: (0, j))
        out_shape = jax.ShapeDtypeStruct((K, N), _BF)
    return pl.pallas_call(
        body, name=name, grid=(N // tn, L // tm),
        in_specs=[pl.BlockSpec((tm, K), lambda j, i: (i, 0)), pl.BlockSpec((tm, tn), lambda j, i: (i, j))],
        out_specs=out_spec, out_shape=out_shape,
        scratch_shapes=[pltpu.VMEM((K, tn), _F32)],
        compiler_params=_params("parallel", "arbitrary"),
    )(a, b)


def _dw_in_t(pieces, h, tk):
    L = h.shape[0]
    last = L // tk - 1

    def body(p0, p1, p2, p3, p4, h_ref, o_ref, acc):
        @pl.when(pl.program_id(0) == 0)
        def _():
            acc[...] = jnp.zeros_like(acc)

        hv = h_ref[...]
        for j, p in enumerate((p0, p1, p2, p3, p4)):
            acc[j * RET_W:(j + 1) * RET_W, :] += _dot_tn(p[...].astype(_BF), hv)

        @pl.when(pl.program_id(0) == last)
        def _():
            o_ref[...] = acc[...].astype(_BF)

    return pl.pallas_call(
        body, name="dw_in", grid=(L // tk,),
        in_specs=[_row_spec(tk, RET_W)] * 5 + [_row_spec(tk, D_MODEL)],
        out_specs=_full_spec((IN_COLS, D_MODEL)), out_shape=jax.ShapeDtypeStruct((IN_COLS, D_MODEL), _BF),
        scratch_shapes=[pltpu.VMEM((IN_COLS, D_MODEL), _F32)],
        compiler_params=_params("arbitrary"),
    )(*pieces, h)


def _mixout_bwd(dmix, w_out, w_glu, glu, s, o, gate, ggn, tm, after=()):
    L = dmix.shape[0]

    def body(dmix_ref, wo_ref, wg_ref, glu_ref, s_ref, o_ref, gate_ref, ggn_ref,
             dglu_ref, ds_ref, dgate_ref, do_ref, dggn_ref):
        @pl.when(pl.program_id(0) == 0)
        def _():
            dggn_ref[...] = jnp.zeros_like(dggn_ref)

        dcat = _dot_nt(dmix_ref[...], wo_ref[...])
        dy_ret, dy_ssm = dcat[:, :RET_W], dcat[:, RET_W:]
        glu = glu_ref[...]
        ga, sg = glu[:, :SSM_W], _sigmoid(glu[:, SSM_W:])
        dga = (dy_ssm * sg).astype(_BF)
        dgb = (dy_ssm * ga * sg * (1.0 - sg)).astype(_BF)
        dglu_ref[:, :SSM_W] = dga
        dglu_ref[:, SSM_W:] = dgb
        dys = _dot_nt(dga, wg_ref[:, :SSM_W]) + _dot_nt(dgb, wg_ref[:, SSM_W:])
        ds_ref[...] = dys * _gelu_grad(s_ref[...])
        gt = gate_ref[...]
        sgt = _sigmoid(gt)
        ggn = ggn_ref[...]
        for hh in range(N_HEAD):
            cols = slice(hh * HEAD_D, (hh + 1) * HEAD_D)
            ov = o_ref[:, cols]
            dlt = ov - jnp.mean(ov, axis=-1, keepdims=True)
            rstd = lax.rsqrt(jnp.mean(dlt * dlt, axis=-1, keepdims=True) + NORM_EPS)
            on = dlt * rstd
            dyr = dy_ret[:, cols] * (gt[:, cols] * sgt[:, cols])
            dgate_ref[:, cols] = dy_ret[:, cols] * (on * ggn[:, cols]) * (sgt[:, cols] * (1.0 + gt[:, cols] * (1.0 - sgt[:, cols])))
            dggn_ref[:, cols] += jnp.sum(dyr * on, axis=0, keepdims=True)
            don = dyr * ggn[:, cols]
            do = rstd * (don - jnp.mean(don, axis=-1, keepdims=True) - on * jnp.mean(don * on, axis=-1, keepdims=True))
            do_ref[:, cols] = do.astype(_BF)

    body, in_specs, operands = _ordered(
        body, [_row_spec(tm, D_MODEL), _weight_spec((D_MODEL, D_MODEL)), _weight_spec((SSM_W, 2 * SSM_W)),
               _row_spec(tm, 2 * SSM_W), _row_spec(tm, SSM_W), _row_spec(tm, RET_W), _row_spec(tm, RET_W),
               _full_spec((1, RET_W))], (dmix, w_out, w_glu, glu, s, o, gate, ggn), after)
    return pl.pallas_call(
        body, name="mixout_bwd", grid=(L // tm,),
        in_specs=in_specs,
        out_specs=[_row_spec(tm, 2 * SSM_W), _row_spec(tm, SSM_W), _row_spec(tm, RET_W), _row_spec(tm, RET_W),
                   _full_spec((1, RET_W))],
        out_shape=[jax.ShapeDtypeStruct((L, 2 * SSM_W), _BF), jax.ShapeDtypeStruct((L, SSM_W), _F32),
                   jax.ShapeDtypeStruct((L, RET_W), _F32), jax.ShapeDtypeStruct((L, RET_W), _BF),
                   jax.ShapeDtypeStruct((1, RET_W), _F32)],
        compiler_params=_params("arbitrary"),
    )(*operands)


def _s5_bwd(u, ds, xs, ent, bmat, cmat, tab_r, pw_r, d_skip, tb, after=()):
    L = u.shape[0]
    nt = L // tb
    seg = tb // SUBLANES
    G = KB_PER_STEP
    rcol = pl.BlockSpec((tb, G * LANES), lambda kb, t: (nt - 1 - t, kb))
    sp = _s5_specs(seg, time=lambda t: nt - 1 - t)
    aspec = pl.BlockSpec((G, SUBLANES, 2 * KB_STATES), lambda kb, t: (kb, 0, 0))

    def body(u_ref, ds_ref, x_ref, ent_ref, b_ref, c_ref, tr_ref, pr_ref, d_ref,
             du_ref, db_ref, dc_ref, da_ref, dd_ref, up_scr, dp_scr, g_scr, lc_scr):
        @pl.when(pl.program_id(1) == 0)
        def _():
            lc_scr[...] = jnp.zeros_like(lc_scr)
            db_ref[...] = jnp.zeros_like(db_ref)
            dc_ref[...] = jnp.zeros_like(dc_ref)
            da_ref[...] = jnp.zeros_like(da_ref)
            dd_ref[...] = jnp.zeros_like(dd_ref)

        _rows_to_segments(up_scr, u_ref, seg)
        _rows_to_segments(dp_scr, ds_ref, seg)
        for g in range(G):
            g_scr[g] = _dot_nt(dp_scr[g].astype(_BF), c_ref[g]).reshape(seg, SUBLANES, 2 * KB_STATES)
        _scan_segments(g_scr, tr_ref, pr_ref, lc_scr, seg, reverse=True, fwd_ref=x_ref, fwd_entry_ref=ent_ref.at[:, 0],
                       da_ref=da_ref)
        for g in range(G):
            cols = slice(g * LANES, (g + 1) * LANES)
            uv, dsv = up_scr[g], dp_scr[g]
            ub, dsb = uv.astype(_BF), dsv.astype(_BF)
            lamb = g_scr[g].reshape(tb, 2 * KB_STATES).astype(_BF)
            db_ref[g] += _dot_tn(ub, lamb)
            dc_ref[g] += _dot_tn(dsb, x_ref[g].reshape(tb, 2 * KB_STATES).astype(_BF))
            dd_ref[:, cols] += jnp.sum(dsv * uv, axis=0, keepdims=True)
            up_scr[g] = _dot_nt(lamb, b_ref[g]) + d_ref[:, cols] * dsv
        _segments_to_rows(du_ref, up_scr, seg)

    body, in_specs, operands = _ordered(
        body, [rcol, rcol, sp["x"], sp["ent"], sp["b"], sp["c"], sp["tab"], sp["pw"], sp["d"]],
        (u, ds, xs, ent, bmat, cmat, tab_r, pw_r, d_skip), after)
    return pl.pallas_call(
        body, name="s5_bwd", grid=(N_KB // G, nt),
        in_specs=in_specs,
        out_specs=[rcol, sp["b"], sp["b"], aspec, sp["d"]],
        out_shape=[jax.ShapeDtypeStruct((L, SSM_W), _F32),
                   jax.ShapeDtypeStruct((N_KB, LANES, 2 * KB_STATES), _F32),
                   jax.ShapeDtypeStruct((N_KB, LANES, 2 * KB_STATES), _F32),
                   jax.ShapeDtypeStruct((N_KB, SUBLANES, 2 * KB_STATES), _F32),
                   jax.ShapeDtypeStruct((1, SSM_W), _F32)],
        scratch_shapes=[pltpu.VMEM((G, tb, LANES), _F32)] * 2
        + [pltpu.VMEM((G, seg, SUBLANES, 2 * KB_STATES), _F32), pltpu.VMEM((G, SUBLANES, 2 * KB_STATES), _F32)],
        compiler_params=_params("parallel", "arbitrary"),
    )(*operands)


def _retention_bwd(q, k, v, do, r_prev, consts, cosf, sinf):
    L = q.shape[0]
    nc = L // CHUNK
    cps = math.gcd(RET_STEP_CHUNKS, nc)
    nb = nc // cps
    blk = pl.BlockSpec((cps * CHUNK, RET_W), lambda n: (nb - 1 - n, 0))
    rope_blk = pl.BlockSpec((cps * CHUNK, HEAD_D), lambda n: (nb - 1 - n, 0))

    def body(q_ref, k_ref, v_ref, do_ref, rp_ref, dm_ref, xi_ref, zeta_ref, gc_ref, cos_ref, sin_ref,
             dq_ref, dk_ref, dv_ref, g_scr):
        @pl.when(pl.program_id(0) == 0)
        def _():
            g_scr[...] = jnp.zeros_like(g_scr)

        for hh in range(N_HEAD):
            cols = slice(hh * HEAD_D, (hh + 1) * HEAD_D)
            dm, zeta = dm_ref[hh], zeta_ref[hh]
            gst = g_scr[hh]
            for c in reversed(range(cps)):
                rows = slice(c * CHUNK, (c + 1) * CHUNK)
                qv, kv, vv, dov = q_ref[rows, cols], k_ref[rows, cols], v_ref[rows, cols], do_ref[rows, cols]
                rb = rp_ref[hh, c].astype(_BF)
                gb = gst.astype(_BF)
                sb = (_dot_nt(qv, kv) * dm).astype(_BF)
                dab = (_dot_nt(dov, vv) * dm).astype(_BF)
                dox = (dov.astype(_F32) * xi_ref[hh]).astype(_BF)
                vz = (vv.astype(_F32) * zeta).astype(_BF)
                dq = _dot(dab, kv) + _dot_nt(dox, rb)
                dk = _dot_tn(dab, qv) + _dot_nt(vz, gb)
                dv = _dot_tn(sb, dov) + _dot(kv, gb) * zeta
                gst = gc_ref[hh, 0:1, :] * gst + _dot_tn(qv, dox)
                cs, sn = cos_ref[rows, :], sin_ref[rows, :]
                dq_ref[rows, cols] = _rope_t(dq, cs, sn).astype(_BF)
                dk_ref[rows, cols] = (_rope_t(dk, cs, sn) * (HEAD_D ** -0.5)).astype(_BF)
                dv_ref[rows, cols] = dv.astype(_BF)
            g_scr[hh] = gst

    return pl.pallas_call(
        body, name="retention_bwd", grid=(nb,),
        in_specs=[blk, blk, blk, blk, pl.BlockSpec((N_HEAD, cps, HEAD_D, HEAD_D), lambda n: (0, nb - 1 - n, 0, 0))]
        + _head_specs() + [rope_blk, rope_blk],
        out_specs=[blk, blk, blk],
        out_shape=[jax.ShapeDtypeStruct((L, RET_W), _BF)] * 3,
        scratch_shapes=[pltpu.VMEM((N_HEAD, HEAD_D, HEAD_D), _F32)],
        compiler_params=_params("arbitrary"),
    )(q, k, v, do, r_prev, *consts, cosf, sinf)


def _inproj_bwd(pieces, w_in_t, x, dx2, g1, tm, after=()):
    L = x.shape[0]

    def body(p0, p1, p2, p3, p4, w_ref, x_ref, dx2_ref, g_ref, dx_ref, dg_ref):
        @pl.when(pl.program_id(0) == 0)
        def _():
            dg_ref[...] = jnp.zeros_like(dg_ref)

        for rows in _row_chunks(tm):
            dh = None
            for j, p in enumerate((p0, p1, p2, p3, p4)):
                part = _dot(p[rows, :].astype(_BF), w_ref[j * RET_W:(j + 1) * RET_W, :])
                dh = part if dh is None else dh + part
            dz, dgr = _rms_bwd(x_ref[rows, :], g_ref[...], dh)
            dx_ref[rows, :] = dx2_ref[rows, :] + dz
            dg_ref[...] += jnp.sum(dgr, axis=0, keepdims=True)

    body, in_specs, operands = _ordered(
        body, [_row_spec(tm, RET_W)] * 5 + [_weight_spec((IN_COLS, D_MODEL)), _row_spec(tm, D_MODEL),
                                             _row_spec(tm, D_MODEL), _full_spec((1, D_MODEL))],
        (*pieces, w_in_t, x, dx2, g1), after)
    return pl.pallas_call(
        body, name="inproj_bwd", grid=(L // tm,),
        in_specs=in_specs,
        out_specs=[_row_spec(tm, D_MODEL), _full_spec((1, D_MODEL))],
        out_shape=[jax.ShapeDtypeStruct((L, D_MODEL), _F32), jax.ShapeDtypeStruct((1, D_MODEL), _F32)],
        compiler_params=_params("arbitrary"),
    )(*operands)


def _sum_adamw(parts, w, m, v, tr, name):
    _, R, Cc = parts.shape

    def body(p_ref, w_ref, m_ref, v_ref, g_ref, d_ref, nm_ref, nv_ref):
        gv = p_ref[0].astype(_F32)
        for s in range(1, N_DEV):
            gv = gv + p_ref[s].astype(_F32)
        g_ref[...] = gv
        nm = ADAM_B1 * m_ref[...] + (1.0 - ADAM_B1) * gv
        nv = ADAM_B2 * v_ref[...] + (1.0 - ADAM_B2) * (gv * gv)
        m_hat = nm / (1.0 - ADAM_B1 ** ADAM_STEP)
        v_hat = nv / (1.0 - ADAM_B2 ** ADAM_STEP)
        d_ref[...] = -ADAM_LR * (m_hat / (jnp.sqrt(v_hat) + ADAM_EPS) + ADAM_WD * w_ref[...])
        nm_ref[...] = nm
        nv_ref[...] = nv

    spec = _row_spec(tr, Cc)
    return pl.pallas_call(
        body, name=name, grid=(R // tr,),
        in_specs=[pl.BlockSpec((N_DEV, tr, Cc), lambda i: (0, i, 0))] + [spec] * 3, out_specs=[spec] * 4,
        out_shape=[jax.ShapeDtypeStruct((R, Cc), _F32)] * 4,
        compiler_params=_params("parallel"),
    )(parts, w, m, v)


def _my_place():
    return lax.axis_index("x"), lax.axis_index("y"), lax.axis_index("c")


def _all_gather(blocks):
    n = len(blocks)

    def body(*refs):
        x_refs, out_refs, done_ref = refs[:n], refs[n:2 * n], refs[2 * n]
        send_sems, recv_sems, local_sems = refs[2 * n + 1:]
        done_ref[...] = jnp.zeros_like(done_ref)
        x, y, c = _my_place()
        me, sibling = (x, y, c), (x, y, 1 - c)
        chips = [(1 - x, y), (x, 1 - y), (1 - x, 1 - y)]

        def slot(a, px, py, pc):
            return out_refs[a].at[4 * px + 2 * py + pc]

        def copy(a, k, blk, to, own=False):
            return pltpu.make_async_remote_copy(
                src_ref=x_refs[a] if own else slot(a, *blk), dst_ref=slot(a, *blk),
                send_sem=send_sems.at[a, k], recv_sem=recv_sems.at[a, k], device_id=to, device_id_type=MESH)

        mine = [pltpu.make_async_copy(x_refs[a], slot(a, *me), local_sems.at[a]) for a in range(n)]
        for cp in mine:
            cp.start()
        first = []
        for a in range(n):
            first.append(copy(a, 0, me, sibling, own=True))
            first += [copy(a, 1 + j, me, (*chip, c), own=True) for j, chip in enumerate(chips)]
        for cp in first:
            cp.start()
        passed = []
        for j, chip in enumerate(chips):
            for a in range(n):
                copy(a, 1 + j, (*chip, c), me).wait_recv()
                fwd = copy(a, 4 + j, (*chip, c), sibling)
                fwd.start()
                passed.append(fwd)
        for a in range(n):
            copy(a, 0, sibling, me).wait_recv()
            for j, chip in enumerate(chips):
                copy(a, 4 + j, (*chip, 1 - c), me).wait_recv()
        for cp in first + passed:
            cp.wait_send()
        for cp in mine:
            cp.wait()

    any_spec = pl.BlockSpec(memory_space=pl.ANY)
    outs = pl.pallas_call(
        body, name="weights_all_gather",
        in_specs=[any_spec] * n, out_specs=[any_spec] * n + [pl.BlockSpec(memory_space=pltpu.VMEM)],
        out_shape=[jax.ShapeDtypeStruct((N_DEV,) + b.shape, b.dtype) for b in blocks]
        + [jax.ShapeDtypeStruct((SUBLANES, LANES), _F32)],
        scratch_shapes=[pltpu.SemaphoreType.DMA((n, 7)), pltpu.SemaphoreType.DMA((n, 7)), pltpu.SemaphoreType.DMA((n,))],
    )(*blocks)
    return outs[:n], outs[n]


def _exchange(bigs, small):
    n = len(bigs)
    r = small.shape[0]

    def body(*refs):
        in_refs, out_refs = refs[:n + 1], refs[n + 1:2 * n + 2]
        send_sems, recv_sems, local_sems = refs[2 * n + 2:]
        x, y, c = _my_place()
        me = 4 * x + 2 * y + c
        own = [pltpu.make_async_copy(in_refs[a].at[me], out_refs[a].at[me], local_sems.at[a]) for a in range(n)]
        own.append(pltpu.make_async_copy(in_refs[n], out_refs[n].at[me], local_sems.at[n]))
        for cp in own:
            cp.start()
        copies = []
        for kk in range(1, N_DEV):
            px, py, pc = x ^ (kk >> 2), y ^ ((kk >> 1) & 1), c ^ (kk & 1)
            peer = 4 * px + 2 * py + pc
            for a in range(n + 1):
                src = in_refs[a].at[peer] if a < n else in_refs[a]
                copies.append(pltpu.make_async_remote_copy(
                    src_ref=src, dst_ref=out_refs[a].at[me],
                    send_sem=send_sems.at[a, kk - 1], recv_sem=recv_sems.at[a, kk - 1],
                    device_id=(px, py, pc), device_id_type=MESH))
        for cp in copies:
            cp.start()
        for cp in copies:
            cp.wait_recv()
        for cp in copies:
            cp.wait_send()
        for cp in own:
            cp.wait()

    any_spec = pl.BlockSpec(memory_space=pl.ANY)
    outs = pl.pallas_call(
        body, name="grad_exchange",
        in_specs=[any_spec] * (n + 1), out_specs=[any_spec] * (n + 1),
        out_shape=[jax.ShapeDtypeStruct(b.shape, b.dtype) for b in bigs]
        + [jax.ShapeDtypeStruct((N_DEV, r, LANES), small.dtype)],
        scratch_shapes=[pltpu.SemaphoreType.DMA((n + 1, 7)), pltpu.SemaphoreType.DMA((n + 1, 7)),
                        pltpu.SemaphoreType.DMA((n + 1,))],
    )(*bigs, small)
    return outs[:n], outs[n]


HBM_SPEC = pl.BlockSpec(memory_space=pltpu.HBM)
SEM_SPEC = pl.BlockSpec(memory_space=pltpu.SEMAPHORE)
DATAFLOW = pltpu.SideEffectType.DATAFLOW_SIDE_EFFECTING


def _my_index():
    x, y, c = _my_place()
    return 4 * x + 2 * y + c


def _landing(own_block):
    zone = lax.empty((N_DEV,) + own_block.shape, own_block.dtype)
    return lax.dynamic_update_index_in_dim(zone, own_block, _my_index(), 0)


def _split_copies(src_refs, land_refs, send_sems, recv_sems, gather, first=0):
    x, y, c = _my_place()
    me = 4 * x + 2 * y + c
    copies = []
    for a, (src, land) in enumerate(zip(src_refs, land_refs)):
        for kk in range(1, N_DEV):
            px, py, pc = x ^ (kk >> 2), y ^ ((kk >> 1) & 1), c ^ (kk & 1)
            peer = 4 * px + 2 * py + pc
            copies.append(pltpu.make_async_remote_copy(
                src_ref=src if gather else src.at[peer], dst_ref=land.at[me],
                send_sem=send_sems.at[(first + a) * 7 + kk - 1], recv_sem=recv_sems.at[(first + a) * 7 + kk - 1],
                device_id=(px, py, pc), device_id_type=MESH))
    return copies


def _split_start(srcs, lands, gather, name):
    n = len(srcs)

    def body(*refs):
        src_refs, land_refs = refs[:n], refs[n:2 * n]
        send_sems, recv_sems = refs[2 * n], refs[2 * n + 1]
        token = refs[-1]
        for cp in _split_copies(src_refs, land_refs, send_sems, recv_sems, gather):
            cp.start()
        token[...] = jnp.zeros_like(token)

    outs = pl.pallas_call(
        body, name=name,
        out_shape=(pltpu.SemaphoreType.DMA((7 * n,)), pltpu.SemaphoreType.DMA((7 * n,)),
                   *[pltpu.HBM(t.shape, t.dtype) for t in srcs], *[pltpu.HBM(t.shape, t.dtype) for t in lands],
                   jax.ShapeDtypeStruct((SUBLANES, LANES), _F32)),
        in_specs=[HBM_SPEC] * (2 * n),
        out_specs=(SEM_SPEC, SEM_SPEC, *[HBM_SPEC] * (2 * n), pl.BlockSpec(memory_space=pltpu.VMEM)),
        input_output_aliases={i: 2 + i for i in range(2 * n)},
        compiler_params=pltpu.CompilerParams(has_side_effects=DATAFLOW),
    )(*[pltpu.with_memory_space_constraint(t, pltpu.HBM) for t in list(srcs) + list(lands)])
    return outs[0], outs[1], outs[2:2 + n], outs[2 + n:2 + 2 * n], outs[-1]


def _split_wait(send_sems, recv_sems, srcs, lands, after, gather, name, first=0):
    n = len(srcs)

    def body(*refs):
        src_refs, land_refs = refs[:n], refs[n:2 * n]
        send_s, recv_s = refs[2 * n], refs[2 * n + 1]
        for cp in _split_copies(src_refs, land_refs, send_s, recv_s, gather, first):
            cp.wait_send()
            cp.wait_recv()

    outs = pl.pallas_call(
        body, name=name,
        out_shape=tuple(pltpu.HBM(t.shape, t.dtype) for t in list(srcs) + list(lands)),
        in_specs=[HBM_SPEC] * (2 * n) + [SEM_SPEC, SEM_SPEC, pl.BlockSpec(memory_space=pl.ANY)],
        out_specs=tuple([HBM_SPEC] * (2 * n)),
        input_output_aliases={i: i for i in range(2 * n)},
        compiler_params=pltpu.CompilerParams(has_side_effects=DATAFLOW),
    )(*srcs, *lands, send_sems, recv_sems, after)
    return outs[n:]


def _discretize(lam_re, lam_im, log_dt, b_re, b_im):
    lr = jnp.minimum(lam_re, -1e-4)
    li = lam_im
    dt = jnp.exp(log_dt)[:, None]
    er = jnp.exp(lr * dt)
    ar, ai = er * jnp.cos(li * dt), er * jnp.sin(li * dt)
    den = lr * lr + li * li
    cr = ((ar - 1.0) * lr + ai * li) / den
    ci = (ai * lr - (ar - 1.0) * li) / den
    bbr = cr[:, :, None] * b_re - ci[:, :, None] * b_im
    bbi = cr[:, :, None] * b_im + ci[:, :, None] * b_re
    return ar, ai, bbr, bbi


def _cmul(ar, ai, br, bi):
    return ar * br - ai * bi, ar * bi + ai * br


def _cpowers(ar, ai, n):
    pr, pi = ar[None], ai[None]
    while pr.shape[0] < n:
        nr, ni = _cmul(pr, pi, pr[-1][None], pi[-1][None])
        pr, pi = jnp.concatenate([pr, nr]), jnp.concatenate([pi, ni])
    return pr[:n], pi[:n]


def _scan_tables(ar, ai, seg, reverse):
    if reverse:
        ai = -ai
    ar, ai = ar.reshape(N_KB, KB_STATES), ai.reshape(N_KB, KB_STATES)
    pr, pi = _cpowers(ar, ai, seg)
    a1 = (pr[-1], pi[-1])
    a2 = _cmul(*a1, *a1)
    a4 = _cmul(*a2, *a2)
    row = jnp.arange(SUBLANES)[None, :, None]
    wide = lambda t: jnp.broadcast_to(t[:, None, :], (N_KB, SUBLANES, KB_STATES))
    tabs = [wide(ar), wide(ai)]
    for dist, (qr, qi) in ((1, a1), (2, a2), (4, a4)):
        keep = (row < SUBLANES - dist) if reverse else (row >= dist)
        tabs += [jnp.where(keep, wide(qr), 0.0), jnp.where(keep, wide(qi), 0.0)]
    tabs += [wide(a1[0]), wide(a1[1])]
    if reverse:
        pr, pi = pr[::-1], pi[::-1]
    pw = jnp.transpose(jnp.concatenate([pr, pi], axis=-1), (1, 0, 2))[:, :, None, :]
    return jnp.stack(tabs, axis=1).astype(_F32), pw.astype(_F32)


def _block_diag_in(br, bi):
    eye = jnp.eye(GROUPS_PER_KB, dtype=_F32)
    one = lambda t: jnp.einsum("kgpc,gh->kgchp", t.reshape(N_KB, GROUPS_PER_KB, N_STATE, SSM_GC), eye).reshape(
        N_KB, LANES, KB_STATES)
    return jnp.concatenate([one(br), one(bi)], axis=-1)


def _block_diag_in_t(dmat):
    d6 = dmat.reshape(N_KB, GROUPS_PER_KB, SSM_GC, 2, GROUPS_PER_KB, N_STATE)
    eye = jnp.eye(GROUPS_PER_KB, dtype=_F32)
    both = jnp.einsum("kgcrhp,gh->rkgpc", d6, eye).reshape(2, N_GROUP, N_STATE, SSM_GC)
    return both[0], both[1]


def _block_diag_out(c_re, c_im):
    eye = jnp.eye(GROUPS_PER_KB, dtype=_F32)
    one = lambda t: jnp.einsum("kgcp,gh->khpgc", t.reshape(N_KB, GROUPS_PER_KB, SSM_GC, N_STATE), eye).reshape(
        N_KB, KB_STATES, LANES)
    return jnp.concatenate([one(c_re), -one(c_im)], axis=1)


def _block_diag_out_t(dmat_t):
    d6 = dmat_t.reshape(N_KB, GROUPS_PER_KB, SSM_GC, 2, GROUPS_PER_KB, N_STATE)
    eye = jnp.eye(GROUPS_PER_KB, dtype=_F32)
    both = jnp.einsum("kgcrhp,gh->rkgcp", d6, eye).reshape(2, N_GROUP, SSM_GC, N_STATE)
    return both[0], -both[1]


SMALL_NAMES = ("norm_mix_pre", "norm_mix_post", "ret_gn_gain", "ssm_lambda_re", "ssm_lambda_im", "ssm_log_dt",
               "ssm_b_re", "ssm_b_im", "ssm_c_re", "ssm_c_im", "ssm_d", "norm_mlp_pre", "norm_mlp_post")


def _local_grads(x, tgt, small, weights, emit, emit_small, tm, tk, tb, zero=0.0):
    L = x.shape[0]
    g1, g2, ggn = small["norm_mix_pre"], small["norm_mix_post"], small["ret_gn_gain"]
    g3, g4, d_skip = small["norm_mlp_pre"], small["norm_mlp_post"], small["ssm_d"]

    rope = _rope_tables(L)
    consts = _ret_consts()

    disc_in = (small["ssm_lambda_re"][0], small["ssm_lambda_im"][0], small["ssm_log_dt"][0] + zero,
               small["ssm_b_re"][0], small["ssm_b_im"][0])
    (ar, ai, bbr, bbi), disc_vjp = jax.vjp(_discretize, *disc_in)
    bmat = _block_diag_in(bbr, bbi).astype(_BF)
    cmat = _block_diag_out(small["ssm_c_re"][0], small["ssm_c_im"][0]).astype(_BF)
    seg = tb // SUBLANES
    tab_f, pw_f = _scan_tables(ar, ai, seg, False)
    tab_r, pw_r = _scan_tables(ar, ai, seg, True)

    h1 = _prenorm(x, g1, min(2 * tm, L), after=(pw_r,))
    (w_in_t,) = weights("in", h1)
    q, k, v, gate, u, cosf, sinf = _inproj_fwd(h1, w_in_t, rope, tm)
    o, y_ret, r_prev = _retention_fwd(q, k, v, gate, ggn, consts)
    s, xs, ent = _s5_fwd(u, bmat, cmat, tab_f, pw_f, d_skip, tb)
    w_glu, w_out = weights("mix", s)
    ys, glu, cat, mix, x2 = _mixout_fwd(s, y_ret, x, w_glu, w_out, g2, min(2 * tm, L))
    w_ff1, w_ff2 = weights("mlp", x2)
    h3, f1 = _ff1_fwd(x2, g3, w_ff1, tm)
    dy, dm, dg4, sq = _ff2_loss(f1, x2, tgt, g4, w_ff2, min(2 * tm, L))

    df1, dw_ff2 = _ff2_bwd(dm, f1, w_ff2, min(1024, L), 1024)
    dx2, dmix, dg3, dg2 = _ff1_bwd(df1, w_ff1, x2, mix, dy, g3, g2, min(2 * tm, L))
    dw_ff1 = _matmul_tn(h3, df1, tk, FF1_COLS, "dw_ff1", slots=True)
    token = emit({"w_ff1": dw_ff1, "w_ff2": dw_ff2})
    dglu, ds, dgate, do, dggn = _mixout_bwd(dmix, w_out, w_glu, glu, s, o, gate, ggn, tm, after=token)
    dw_out = _matmul_tn(cat, dmix, tk, 1024, "dw_out")
    dw_glu = _matmul_tn(ys, dglu, tk, 1024, "dw_glu")
    token = emit({"w_glu": dw_glu, "w_out": dw_out})
    du, dbmat, dcmat, da8, dd = _s5_bwd(u, ds, xs, ent, bmat, cmat, tab_r, pw_r, d_skip, tb, after=token)
    dq, dk, dv = _retention_bwd(q, k, v, do, r_prev, consts, cosf, sinf)
    pieces = (dq, dk, dv, dgate, du)
    dw_in_t = _dw_in_t(pieces, h1, min(1024, L))
    token = emit({"w_in": dw_in_t})

    da = jnp.sum(da8, axis=1)
    dar = da[:, :KB_STATES].reshape(N_GROUP, N_STATE)
    dai = da[:, KB_STATES:].reshape(N_GROUP, N_STATE)
    dbr, dbi = _block_diag_in_t(dbmat)
    dlre, dlim, dldt, dbre, dbim = disc_vjp((dar, dai, dbr, dbi))
    dcre, dcim = _block_diag_out_t(dcmat)

    token2 = emit_small({
        "norm_mix_post": dg2, "ret_gn_gain": dggn,
        "ssm_lambda_re": dlre[None], "ssm_lambda_im": dlim[None], "ssm_log_dt": dldt[None],
        "ssm_b_re": dbre[None], "ssm_b_im": dbim[None], "ssm_c_re": dcre[None], "ssm_c_im": dcim[None],
        "ssm_d": dd, "norm_mlp_pre": dg3, "norm_mlp_post": dg4,
    }, sq)
    gx, dg1 = _inproj_bwd(pieces, w_in_t, x, dx2, g1, min(2 * tm, L), after=token + token2)
    return gx, dg1


BIG_SHAPES = {"w_in": (D_MODEL, IN_COLS // N_DEV), "w_glu": (SSM_W, 2 * SSM_W // N_DEV), "w_out": (D_MODEL // N_DEV, D_MODEL),
              "w_ff1": (D_MODEL, FF1_COLS), "w_ff2": (D_FF // N_DEV, D_MODEL)}
BIG_NAMES = ("w_in", "w_glu", "w_out", "w_ff1", "w_ff2")


def _cols_from_slots(g):
    return jnp.transpose(g, (1, 0, 2)).reshape(g.shape[1], N_DEV * g.shape[2])


def _cols_to_slots(dw):
    r, cols = dw.shape
    return jnp.transpose(dw.reshape(r, N_DEV, cols // N_DEV), (1, 0, 2))


WEIGHT_GROUPS = {"in": ("w_in",), "mix": ("w_glu", "w_out"), "mlp": ("w_ff1", "w_ff2")}


def _weight_from_slots(name, g):
    if name == "w_glu":
        return _cols_from_slots(g)
    if name == "w_ff1":
        return g
    return g.reshape(N_DEV * g.shape[1], g.shape[2])


def _grad_slots(name, dw):
    if name == "w_glu":
        return _cols_to_slots(dw)
    if name == "w_ff1":
        return dw
    if name == "w_in":
        return dw.reshape(N_DEV, BIG_SHAPES[name][1], BIG_SHAPES[name][0])
    return dw.reshape((N_DEV,) + BIG_SHAPES[name])


PIECE_ROWS = 8


VEC_NAMES = tuple(n for n in SMALL_NAMES if n[:6] not in ("ssm_b_", "ssm_c_"))
BC_NAMES = ("ssm_b_re", "ssm_b_im", "ssm_c_re", "ssm_c_im")
BC_ROWS = N_GROUP * SSM_GC


def _bc_view(name, t):
    t = t[0]
    if name.startswith("ssm_b_"):
        t = jnp.swapaxes(t, 1, 2)
    return t.reshape(BC_ROWS, N_STATE)


def _bc_unview(name, t):
    t = t.reshape(N_GROUP, SSM_GC, N_STATE)
    if name.startswith("ssm_b_"):
        t = jnp.swapaxes(t, 1, 2)
    return t[None]


def _pack_bc(vals):
    return jnp.concatenate([_bc_view(n, vals[n]).astype(_F32) for n in BC_NAMES], axis=0)


def _unpack_bc(buf):
    return {n: _bc_unview(n, buf[j * BC_ROWS:(j + 1) * BC_ROWS]) for j, n in enumerate(BC_NAMES)}


def _small_layout(shapes):
    off, rows = {}, 0
    for n in VEC_NAMES:
        off[n] = rows
        rows += -(-math.prod(shapes[n]) // (PIECE_ROWS * LANES)) * PIECE_ROWS
    return off, rows, rows + PIECE_ROWS


def _pack_small(vals, shapes, last=None):
    parts = []
    for n in VEC_NAMES:
        flat = vals[n].reshape(-1).astype(_F32)
        pad = -flat.shape[0] % (PIECE_ROWS * LANES)
        if pad:
            flat = jnp.concatenate([flat, jnp.zeros((pad,), _F32)])
        parts.append(flat.reshape(-1, LANES))
    parts.append(jnp.zeros((PIECE_ROWS, LANES), _F32) if last is None else last)
    return jnp.concatenate(parts, axis=0)


def _unpack_small(buf, shapes):
    off, _, _ = _small_layout(shapes)
    out = {}
    for n in VEC_NAMES:
        size = math.prod(shapes[n])
        rows = -(-size // LANES)
        out[n] = buf[off[n]:off[n] + rows].reshape(-1)[:size].reshape(shapes[n])
    return out


WEIGHT_NAMES = ('norm_mix_pre', 'norm_mix_post', 'w_in', 'ret_gn_gain', 'ssm_lambda_re', 'ssm_lambda_im', 'ssm_log_dt',
                'ssm_b_re', 'ssm_b_im', 'ssm_c_re', 'ssm_c_im', 'ssm_d', 'w_glu', 'w_out', 'norm_mlp_pre',
                'norm_mlp_post', 'w_ff1', 'w_ff2')


def kernel(x, norm_mix_pre, norm_mix_post, w_in, ret_gn_gain, ssm_lambda_re, ssm_lambda_im, ssm_log_dt, ssm_b_re, ssm_b_im, ssm_c_re, ssm_c_im, ssm_d, w_glu, w_out, norm_mlp_pre, norm_mlp_post, w_ff1, w_ff2, loss_target, m_norm_mix_pre, m_norm_mix_post, m_w_in, m_ret_gn_gain, m_ssm_lambda_re, m_ssm_lambda_im, m_ssm_log_dt, m_ssm_b_re, m_ssm_b_im, m_ssm_c_re, m_ssm_c_im, m_ssm_d, m_w_glu, m_w_out, m_norm_mlp_pre, m_norm_mlp_post, m_w_ff1, m_w_ff2, v_norm_mix_pre, v_norm_mix_post, v_w_in, v_ret_gn_gain, v_ssm_lambda_re, v_ssm_lambda_im, v_ssm_log_dt, v_ssm_b_re, v_ssm_b_im, v_ssm_c_re, v_ssm_c_im, v_ssm_d, v_w_glu, v_w_out, v_norm_mlp_pre, v_norm_mlp_post, v_w_ff1, v_w_ff2):
    args = dict(locals())
    w = {n: args[n] for n in WEIGHT_NAMES}
    m = {n: args["m_" + n] for n in WEIGHT_NAMES}
    v = {n: args["v_" + n] for n in WEIGHT_NAMES}
    L = x.shape[1]
    tm = min(256, L)
    tk = min(2048, L)
    tb = min(512, L)

    order = [n for names in WEIGHT_GROUPS.values() for n in names]
    blocks = [(w[n][0].T if n == "w_in" else w[n][0]).astype(_BF) for n in order]
    gathered = _split_start(blocks, [_landing(b) for b in blocks], True, "weights_start")
    zero = gathered[4][0, 0]

    def weights(group, after):
        names = WEIGHT_GROUPS[group]
        first = order.index(names[0])
        part = slice(first, first + len(names))
        landed = _split_wait(gathered[0], gathered[1], gathered[2][part], gathered[3][part], after, True,
                             "weights_wait_" + group, first=first)
        return [_weight_from_slots(n, g) for n, g in zip(names, landed)]

    in_flight = []

    def emit(dws):
        names = sorted(dws)
        srcs = [_grad_slots(n, dws[n]) for n in names]
        lands = [_landing(lax.dynamic_index_in_dim(t, _my_index(), 0, keepdims=False)) for t in srcs]
        started = _split_start(srcs, lands, False, "grads_start_" + "_".join(names))
        in_flight.append((names, started))
        return (started[4],)

    shapes = {n: w[n].shape for n in SMALL_NAMES}
    first_piece = {SMALL_NAMES[0]: jnp.zeros(shapes[SMALL_NAMES[0]], _F32)}
    small_flight = []

    def emit_small(gs, sq):
        loss_rows = jnp.broadcast_to(0.5 / D_MODEL * jnp.sum(sq), (PIECE_ROWS, LANES)).astype(_F32)
        bufs = [_pack_small({**first_piece, **gs}, shapes, loss_rows), _pack_bc(gs)]
        small_flight.append(_split_start(bufs, [_landing(b) for b in bufs], True, "small_grads_start"))
        return (small_flight[0][4],)

    small_w = {n: w[n] for n in SMALL_NAMES}
    gx, dg1 = _local_grads(x[0], loss_target[0], small_w, weights, emit, emit_small, tm, tk, tb, zero=zero)
    last_buf = dg1.reshape(PIECE_ROWS, LANES)
    last_started = _split_start([last_buf], [_landing(last_buf)], True, "last_grad_start")

    grads, delta, new_m, new_v = {}, {}, {}, {}
    after = last_started[4]
    for names, started in in_flight:
        landed = _split_wait(*started[:4], after, False, "grads_wait_" + "_".join(names))
        for n, parts in zip(names, landed):
            flip = (lambda t: t.T) if n == "w_in" else (lambda t: t)
            res = _sum_adamw(parts, flip(w[n][0]), flip(m[n][0]), flip(v[n][0]), math.gcd(256, parts.shape[1]), "adamw_" + n)
            grads[n], delta[n], new_m[n], new_v[n] = (flip(t)[None] for t in res)
        after = res[1]
    small_parts, bc_parts = _split_wait(*small_flight[0][:4], after, True, "small_grads_wait")
    last_parts = _split_wait(*last_started[:4], small_parts, True, "last_grad_wait")[0]
    small_parts = lax.dynamic_update_slice(small_parts, last_parts, (0, 0, 0))
    res_bc = _sum_adamw(bc_parts, _pack_bc(w), _pack_bc(m), _pack_bc(v), BC_ROWS, "adamw_bc")
    sw, sm, sv = _pack_small(w, shapes), _pack_small(m, shapes), _pack_small(v, shapes)
    res = _sum_adamw(small_parts, sw, sm, sv, sw.shape[0], "adamw_small")
    for dst, buf, buf_bc in zip((grads, delta, new_m, new_v), res, res_bc):
        dst.update(_unpack_small(buf, shapes))
        dst.update(_unpack_bc(buf_bc))
    _, loss_at, _ = _small_layout(shapes)
    loss = res[0][loss_at, 0]

    return (loss, gx[None], *[grads[n] for n in WEIGHT_NAMES], *[delta[n] for n in WEIGHT_NAMES],
            *[new_m[n] for n in WEIGHT_NAMES], *[new_v[n] for n in WEIGHT_NAMES])
```

```python
import math

import jax
import jax.numpy as jnp
from jax import lax
from jax.experimental import pallas as pl
from jax.experimental.pallas import tpu as pltpu

_BF = jnp.bfloat16
_F32 = jnp.float32

D_MODEL = 1024
RET_W = 512
N_HEAD = 4
HEAD_D = 128
CHUNK = 256
ROPE_CHUNK = 128
SSM_W = 512
SSM_GC = 16
N_GROUP = 32
N_STATE = 64
GROUPS_PER_KB = 8
N_KB = 4
KB_STATES = GROUPS_PER_KB * N_STATE
D_FF = 4096
IN_COLS = 2560
NORM_EPS = 1e-6
ROPE_BASE = 10000.0
N_DEV = 8

ADAM_LR = 0.001
ADAM_B1 = 0.9
ADAM_B2 = 0.999
ADAM_EPS = 1e-08
ADAM_WD = 0.01
ADAM_STEP = 10

SUBLANES = 8
LANES = 128
VMEM_LIMIT = 52 * 1024 * 1024
RET_STEP_CHUNKS = 2
KB_PER_STEP = 2
SCAN_UNROLL = True
FIX_UNROLL = 8

MESH = pl.DeviceIdType.MESH


def _params(*sem):
    return pltpu.CompilerParams(dimension_semantics=sem, vmem_limit_bytes=VMEM_LIMIT)


def _dot(a, b):
    return jnp.dot(a, b, preferred_element_type=_F32)


def _dot_nt(a, b):
    return lax.dot_general(a, b, (((1,), (1,)), ((), ())), preferred_element_type=_F32)


def _dot_tn(a, b):
    return lax.dot_general(a, b, (((0,), (0,)), ((), ())), preferred_element_type=_F32)


def _rms_r(z):
    return lax.rsqrt(jnp.mean(z * z, axis=-1, keepdims=True) + NORM_EPS)


def _rms_bwd(z, g, dn):
    r = _rms_r(z)
    t = dn * g
    dz = r * t - z * (r * r * r * jnp.mean(t * z, axis=-1, keepdims=True))
    return dz, dn * z * r


def _rope(t, cs, sn):
    return t * cs + pltpu.roll(t, HEAD_D // 2, 1) * sn


def _rope_t(t, cs, sn):
    return t * cs - pltpu.roll(t, HEAD_D // 2, 1) * sn


def _sigmoid(z):
    return 1.0 / (1.0 + jnp.exp(-z))


_GELU_C = math.sqrt(2.0 / math.pi)


def _gelu(z):
    return 0.5 * z * (1.0 + jnp.tanh(_GELU_C * (z + 0.044715 * z * z * z)))


def _gelu_grad(z):
    th = jnp.tanh(_GELU_C * (z + 0.044715 * z * z * z))
    return 0.5 * (1.0 + th) + 0.5 * z * (1.0 - th * th) * _GELU_C * (1.0 + 3 * 0.044715 * z * z)


ROW_CHUNK = 256


def _row_chunks(tm):
    return [pl.ds(i, min(ROW_CHUNK, tm)) for i in range(0, tm, ROW_CHUNK)]


def _ordered(body, in_specs, operands, after):
    k = len(after)
    if not k:
        return body, list(in_specs), tuple(operands)
    return ((lambda *refs: body(*refs[k:])), [pl.BlockSpec(memory_space=pl.ANY)] * k + list(in_specs),
            tuple(after) + tuple(operands))


def _row_spec(tm, n):
    return pl.BlockSpec((tm, n), lambda i: (i, 0))


def _full_spec(shape):
    nd = len(shape)
    return pl.BlockSpec(shape, lambda *_: (0,) * nd)


def _weight_spec(shape):
    nd = len(shape)
    return pl.BlockSpec(shape, lambda *_: (0,) * nd, pipeline_mode=pl.Buffered(1))


def _rope_tables(L):
    half = HEAD_D // 2
    inv_freq = ROPE_BASE ** (-jnp.arange(half, dtype=_F32) / half)
    twice = lambda t: jnp.concatenate([t, t], axis=-1)
    off = jnp.arange(ROPE_CHUNK, dtype=_F32)[:, None] * inv_freq[None, :]
    start = (ROPE_CHUNK * jnp.arange(L // ROPE_CHUNK, dtype=_F32))[:, None] * inv_freq[None, :]
    return (twice(jnp.cos(off)), twice(jnp.sin(off)),
            twice(jnp.cos(start))[:, None, :], twice(jnp.sin(start))[:, None, :])


def _prenorm(x, g, tm, after=()):
    L = x.shape[0]

    def body(x_ref, g_ref, h_ref):
        xv = x_ref[...]
        h_ref[...] = (xv * _rms_r(xv) * g_ref[...]).astype(_BF)

    body, in_specs, operands = _ordered(body, [_row_spec(tm, D_MODEL), _full_spec((1, D_MODEL))], (x, g), after)
    return pl.pallas_call(
        body, name="prenorm", grid=(L // tm,),
        in_specs=in_specs, out_specs=_row_spec(tm, D_MODEL),
        out_shape=jax.ShapeDtypeStruct((L, D_MODEL), _BF),
        compiler_params=_params("parallel"),
    )(*operands)


def _inproj_fwd(h, w_in_t, rope, tm):
    L = h.shape[0]
    n_chunks = tm // ROPE_CHUNK

    def body(h_ref, w_ref, co_ref, so_ref, cs_ref, ss_ref, q_ref, k_ref, v_ref, gate_ref, u_ref, cos_ref, sin_ref):
        proj = _dot_nt(h_ref[...], w_ref[...])
        lane = lax.broadcasted_iota(jnp.int32, (ROPE_CHUNK, HEAD_D), 1)
        sign = jnp.where(lane < HEAD_D // 2, -1.0, 1.0)
        co, so = co_ref[...], so_ref[...]
        for c in range(n_chunks):
            chunk = pl.program_id(0) * n_chunks + c
            cst, sst = cs_ref[chunk], ss_ref[chunk]
            rows = slice(c * ROPE_CHUNK, (c + 1) * ROPE_CHUNK)
            cs = co * cst - so * sst
            sn = (so * cst + co * sst) * sign
            cos_ref[rows, :] = cs
            sin_ref[rows, :] = sn
            for hh in range(N_HEAD):
                lo = hh * HEAD_D
                q_ref[rows, lo:lo + HEAD_D] = _rope(proj[rows, lo:lo + HEAD_D], cs, sn).astype(_BF)
                kh = _rope(proj[rows, RET_W + lo:RET_W + lo + HEAD_D], cs, sn) * (HEAD_D ** -0.5)
                k_ref[rows, lo:lo + HEAD_D] = kh.astype(_BF)
        v_ref[...] = proj[:, 2 * RET_W:3 * RET_W].astype(_BF)
        gate_ref[...] = proj[:, 3 * RET_W:4 * RET_W]
        u_ref[...] = proj[:, 4 * RET_W:]

    nc = L // ROPE_CHUNK
    return pl.pallas_call(
        body, name="inproj_fwd", grid=(L // tm,),
        in_specs=[_row_spec(tm, D_MODEL), _weight_spec((IN_COLS, D_MODEL)),
                  _full_spec((ROPE_CHUNK, HEAD_D)), _full_spec((ROPE_CHUNK, HEAD_D)),
                  _full_spec((nc, 1, HEAD_D)), _full_spec((nc, 1, HEAD_D))],
        out_specs=[_row_spec(tm, RET_W)] * 5 + [_row_spec(tm, HEAD_D)] * 2,
        out_shape=[jax.ShapeDtypeStruct((L, RET_W), _BF)] * 3 + [jax.ShapeDtypeStruct((L, RET_W), _F32)] * 2
        + [jax.ShapeDtypeStruct((L, HEAD_D), _F32)] * 2,
        compiler_params=_params("parallel"),
    )(h, w_in_t, *rope)


def _ret_consts():
    lg = jnp.log(1.0 - jnp.exp(jnp.linspace(math.log(1.0 / 32), math.log(1.0 / 512), N_HEAD))).astype(_F32)
    idx = jnp.arange(CHUNK, dtype=_F32)
    diff = idx[:, None] - idx[None, :]
    decay = jnp.where(diff[None] >= 0, jnp.exp(jnp.maximum(diff, 0.0)[None] * lg[:, None, None]), 0.0)
    zeta = jnp.exp((CHUNK - 1 - idx)[None, :] * lg[:, None])
    xi = jnp.exp((idx + 1.0)[None, :] * lg[:, None])
    gc = jnp.exp(CHUNK * lg)
    wide = lambda t: jnp.broadcast_to(t[:, :, None], (N_HEAD, CHUNK, HEAD_D)).astype(_F32)
    gcw = jnp.broadcast_to(gc[:, None, None], (N_HEAD, SUBLANES, HEAD_D)).astype(_F32)
    return decay.astype(_F32), wide(xi), wide(zeta), gcw


def _head_specs():
    wide = _full_spec((N_HEAD, CHUNK, HEAD_D))
    return [_full_spec((N_HEAD, CHUNK, CHUNK)), wide, wide, _full_spec((N_HEAD, SUBLANES, HEAD_D))]


def _retention_fwd(q, k, v, gate, ggn, consts):
    L = q.shape[0]
    nc = L // CHUNK
    cps = math.gcd(RET_STEP_CHUNKS, nc)
    blk = pl.BlockSpec((cps * CHUNK, RET_W), lambda n: (n, 0))

    def body(q_ref, k_ref, v_ref, gate_ref, ggn_ref, dm_ref, xi_ref, zeta_ref, gc_ref,
             o_ref, y_ref, rp_ref, r_scr):
        @pl.when(pl.program_id(0) == 0)
        def _():
            r_scr[...] = jnp.zeros_like(r_scr)

        for hh in range(N_HEAD):
            cols = slice(hh * HEAD_D, (hh + 1) * HEAD_D)
            state = r_scr[hh]
            for c in range(cps):
                rows = slice(c * CHUNK, (c + 1) * CHUNK)
                qv, kv, vv = q_ref[rows, cols], k_ref[rows, cols], v_ref[rows, cols]
                s = _dot_nt(qv, kv) * dm_ref[hh]
                o = _dot(s.astype(_BF), vv) + _dot(qv, state.astype(_BF)) * xi_ref[hh]
                o_ref[rows, cols] = o
                rp_ref[hh, c] = state
                vz = (vv.astype(_F32) * zeta_ref[hh]).astype(_BF)
                state = gc_ref[hh, 0:1, :] * state + _dot_tn(kv, vz)
                dlt = o - jnp.mean(o, axis=-1, keepdims=True)
                on = dlt * lax.rsqrt(jnp.mean(dlt * dlt, axis=-1, keepdims=True) + NORM_EPS)
                gt = gate_ref[rows, cols]
                y_ref[rows, cols] = (gt * _sigmoid(gt) * (on * ggn_ref[:, cols])).astype(_BF)
            r_scr[hh] = state

    return pl.pallas_call(
        body, name="retention_fwd", grid=(nc // cps,),
        in_specs=[blk, blk, blk, blk, _full_spec((1, RET_W))] + _head_specs(),
        out_specs=[blk, blk, pl.BlockSpec((N_HEAD, cps, HEAD_D, HEAD_D), lambda n: (0, n, 0, 0))],
        out_shape=[jax.ShapeDtypeStruct((L, RET_W), _F32), jax.ShapeDtypeStruct((L, RET_W), _BF),
                   jax.ShapeDtypeStruct((N_HEAD, nc, HEAD_D, HEAD_D), _F32)],
        scratch_shapes=[pltpu.VMEM((N_HEAD, HEAD_D, HEAD_D), _F32)],
        compiler_params=_params("arbitrary"),
    )(q, k, v, gate, ggn, *consts)


def _rows_to_segments(dst_scr, src_ref, seg):
    for g in range(dst_scr.shape[0]):
        for j in range(SUBLANES):
            dst_scr[g, pl.ds(j, seg, stride=SUBLANES), :] = src_ref[pl.ds(j * seg, seg), g * LANES:(g + 1) * LANES]


def _segments_to_rows(dst_ref, src_scr, seg):
    for g in range(src_scr.shape[0]):
        for j in range(SUBLANES):
            dst_ref[pl.ds(j * seg, seg), g * LANES:(g + 1) * LANES] = src_scr[g, pl.ds(j, seg, stride=SUBLANES), :]


def _scan_segments(x_ref, tab_ref, pw_ref, carry_ref, seg, reverse, entry_ref=None, fwd_ref=None, fwd_entry_ref=None,
                   da_ref=None):
    G = x_ref.shape[0]
    W = KB_STATES
    re, im = pl.ds(0, W), pl.ds(W, W)
    row_id = lax.broadcasted_iota(jnp.int32, (SUBLANES, W), 0)
    edge_in = (row_id == SUBLANES - 1) if reverse else (row_id == 0)
    edge_out = 0 if reverse else SUBLANES - 1
    a_tab = [(tab_ref[g, 0], tab_ref[g, 1]) for g in range(G)]

    def local(i, st):
        r = (seg - 1 - i) if reverse else i
        out = []
        for g in range(G):
            (ar, ai), (sr, si) = a_tab[g], st[g]
            nr = ar * sr - ai * si + x_ref[g, r, :, re]
            ni = ar * si + ai * sr + x_ref[g, r, :, im]
            x_ref[g, r, :, re] = nr
            x_ref[g, r, :, im] = ni
            out.append((nr, ni))
        return tuple(out)

    zero = jnp.zeros((SUBLANES, W), _F32)
    ends = lax.fori_loop(0, seg, local, tuple((zero, zero) for _ in range(G)), unroll=SCAN_UNROLL)

    entry = []
    shift = (SUBLANES - 1) if reverse else 1
    for g in range(G):
        er, ei = ends[g]
        fr = jnp.where(edge_in, carry_ref[g, :, re], pltpu.roll(er, shift, 0))
        fi = jnp.where(edge_in, carry_ref[g, :, im], pltpu.roll(ei, shift, 0))
        for j, dist in enumerate((1, 2, 4)):
            pr, pi = tab_ref[g, 2 + 2 * j], tab_ref[g, 3 + 2 * j]
            sh = (SUBLANES - dist) if reverse else dist
            sr, si = pltpu.roll(fr, sh, 0), pltpu.roll(fi, sh, 0)
            fr, fi = fr + pr * sr - pi * si, fi + pr * si + pi * sr
        br, bi = tab_ref[g, 8], tab_ref[g, 9]
        outr = br * fr - bi * fi + er
        outi = br * fi + bi * fr + ei
        carry_ref[g, :, re] = jnp.broadcast_to(outr[edge_out:edge_out + 1, :], (SUBLANES, W))
        carry_ref[g, :, im] = jnp.broadcast_to(outi[edge_out:edge_out + 1, :], (SUBLANES, W))
        entry.append((fr, fi))
        if entry_ref is not None:
            entry_ref[g, :, re] = fr
            entry_ref[g, :, im] = fi

    add_da = da_ref is not None

    def fix(r, st, first=False):
        out = []
        for g in range(G):
            fr, fi = entry[g]
            pwr, pwi = pw_ref[g, r, :, re], pw_ref[g, r, :, im]
            xr = x_ref[g, r, :, re] + (pwr * fr - pwi * fi)
            xi = x_ref[g, r, :, im] + (pwr * fi + pwi * fr)
            x_ref[g, r, :, re] = xr
            x_ref[g, r, :, im] = xi
            if add_da:
                prev = fwd_entry_ref.at[g] if first else fwd_ref.at[g, r - 1]
                xpr, xpi = prev[:, re], prev[:, im]
                out.append((st[g][0] + (xr * xpr + xi * xpi), st[g][1] + (xi * xpr - xr * xpi)))
            else:
                out.append(st[g])
        return tuple(out)

    if add_da:
        st = fix(0, tuple((zero, zero) for _ in range(G)), first=True)
        st = lax.fori_loop(1, seg, fix, st, unroll=SCAN_UNROLL)
        for g in range(G):
            da_ref[g, :, re] += st[g][0]
            da_ref[g, :, im] += st[g][1]
    else:
        lax.fori_loop(0, seg, fix, tuple((zero[0:1, 0:LANES],) for _ in range(G)), unroll=FIX_UNROLL)


def _s5_specs(seg, time=lambda t: t):
    G = KB_PER_STEP
    return dict(
        x=pl.BlockSpec((G, seg, SUBLANES, 2 * KB_STATES), lambda kb, t: (kb, time(t), 0, 0)),
        ent=pl.BlockSpec((G, 1, SUBLANES, 2 * KB_STATES), lambda kb, t: (kb, time(t), 0, 0)),
        b=pl.BlockSpec((G, LANES, 2 * KB_STATES), lambda kb, t: (kb, 0, 0)),
        c=pl.BlockSpec((G, 2 * KB_STATES, LANES), lambda kb, t: (kb, 0, 0)),
        tab=pl.BlockSpec((G, 10, SUBLANES, KB_STATES), lambda kb, t: (kb, 0, 0, 0)),
        pw=pl.BlockSpec((G, seg, 1, 2 * KB_STATES), lambda kb, t: (kb, 0, 0, 0)),
        d=pl.BlockSpec((1, G * LANES), lambda kb, t: (0, kb)),
    )


def _s5_fwd(u, bmat, cmat, tab_f, pw_f, d_skip, tb):
    L = u.shape[0]
    nt = L // tb
    seg = tb // SUBLANES
    G = KB_PER_STEP
    ucol = pl.BlockSpec((tb, G * LANES), lambda kb, t: (t, kb))
    sp = _s5_specs(seg)

    def body(u_ref, b_ref, c_ref, tab_ref, pw_ref, d_ref, s_ref, x_ref, ent_ref, up_scr, y_scr, carry_scr):
        @pl.when(pl.program_id(1) == 0)
        def _():
            carry_scr[...] = jnp.zeros_like(carry_scr)

        _rows_to_segments(up_scr, u_ref, seg)
        for g in range(G):
            x_ref[g] = _dot(up_scr[g].astype(_BF), b_ref[g]).reshape(seg, SUBLANES, 2 * KB_STATES)
        _scan_segments(x_ref, tab_ref, pw_ref, carry_scr, seg, reverse=False, entry_ref=ent_ref.at[:, 0])
        for g in range(G):
            y = _dot(x_ref[g].reshape(tb, 2 * KB_STATES).astype(_BF), c_ref[g])
            y_scr[g] = y + d_ref[:, g * LANES:(g + 1) * LANES] * up_scr[g]
        _segments_to_rows(s_ref, y_scr, seg)

    return pl.pallas_call(
        body, name="s5_fwd", grid=(N_KB // G, nt),
        in_specs=[ucol, sp["b"], sp["c"], sp["tab"], sp["pw"], sp["d"]],
        out_specs=[ucol, sp["x"], sp["ent"]],
        out_shape=[jax.ShapeDtypeStruct((L, SSM_W), _F32),
                   jax.ShapeDtypeStruct((N_KB, L // SUBLANES, SUBLANES, 2 * KB_STATES), _F32),
                   jax.ShapeDtypeStruct((N_KB, nt, SUBLANES, 2 * KB_STATES), _F32)],
        scratch_shapes=[pltpu.VMEM((G, tb, LANES), _F32)] * 2 + [pltpu.VMEM((G, SUBLANES, 2 * KB_STATES), _F32)],
        compiler_params=_params("parallel", "arbitrary"),
    )(u, bmat, cmat, tab_f, pw_f, d_skip)


def _mixout_fwd(s, y_ret, x, w_glu, w_out, g2, tm):
    L = s.shape[0]

    def body(s_ref, yr_ref, x_ref, wg_ref, wo_ref, g_ref, ys_ref, glu_ref, cat_ref, mix_ref, x2_ref):
        for rows in _row_chunks(tm):
            ys = _gelu(s_ref[rows, :]).astype(_BF)
            ys_ref[rows, :] = ys
            glu = _dot(ys, wg_ref[...])
            glu_ref[rows, :] = glu
            cat_ref[rows, :RET_W] = yr_ref[rows, :]
            cat_ref[rows, RET_W:] = (glu[:, :SSM_W] * _sigmoid(glu[:, SSM_W:])).astype(_BF)
            mix = _dot(cat_ref[rows, :], wo_ref[...])
            mix_ref[rows, :] = mix
            x2_ref[rows, :] = x_ref[rows, :] + mix * _rms_r(mix) * g_ref[...]

    return pl.pallas_call(
        body, name="mixout_fwd", grid=(L // tm,),
        in_specs=[_row_spec(tm, SSM_W), _row_spec(tm, RET_W), _row_spec(tm, D_MODEL),
                  _weight_spec((SSM_W, 2 * SSM_W)), _weight_spec((D_MODEL, D_MODEL)), _full_spec((1, D_MODEL))],
        out_specs=[_row_spec(tm, SSM_W), _row_spec(tm, 2 * SSM_W), _row_spec(tm, D_MODEL),
                   _row_spec(tm, D_MODEL), _row_spec(tm, D_MODEL)],
        out_shape=[jax.ShapeDtypeStruct((L, SSM_W), _BF), jax.ShapeDtypeStruct((L, 2 * SSM_W), _F32),
                   jax.ShapeDtypeStruct((L, D_MODEL), _BF), jax.ShapeDtypeStruct((L, D_MODEL), _F32),
                   jax.ShapeDtypeStruct((L, D_MODEL), _F32)],
        compiler_params=_params("parallel"),
    )(s, y_ret, x, w_glu, w_out, g2)


FF1_COLS = D_FF // N_DEV


def _ff1_fwd(x2, g3, w1, tm):
    L = x2.shape[0]

    def body(x_ref, g_ref, w_ref, h_ref, f_ref):
        xv = x_ref[...]
        h = (xv * _rms_r(xv) * g_ref[...]).astype(_BF)
        h_ref[...] = h
        for j in range(N_DEV):
            f_ref[:, j * FF1_COLS:(j + 1) * FF1_COLS] = _dot(h, w_ref[j])

    return pl.pallas_call(
        body, name="ff1_fwd", grid=(L // tm,),
        in_specs=[_row_spec(tm, D_MODEL), _full_spec((1, D_MODEL)), _weight_spec((N_DEV, D_MODEL, FF1_COLS))],
        out_specs=[_row_spec(tm, D_MODEL), _row_spec(tm, D_FF)],
        out_shape=[jax.ShapeDtypeStruct((L, D_MODEL), _BF), jax.ShapeDtypeStruct((L, D_FF), _F32)],
        compiler_params=_params("parallel"),
    )(x2, g3, w1)


def _ff2_loss(f1, x2, tgt, g4, w2, tm):
    L = f1.shape[0]

    def body(f_ref, x_ref, t_ref, g_ref, w_ref, dy_ref, dm_ref, dg_ref, ls_ref):
        @pl.when(pl.program_id(0) == 0)
        def _():
            dg_ref[...] = jnp.zeros_like(dg_ref)
            ls_ref[...] = jnp.zeros_like(ls_ref)

        g = g_ref[...]
        for rows in _row_chunks(tm):
            rl = jnp.maximum(f_ref[rows, :], 0.0)
            m = _dot((rl * rl).astype(_BF), w_ref[...])
            y = x_ref[rows, :] + m * _rms_r(m) * g
            err = y - t_ref[rows, :]
            ls_ref[...] += jnp.sum(err * err, axis=0, keepdims=True)
            dy = err * (1.0 / D_MODEL)
            dy_ref[rows, :] = dy
            dm, dgr = _rms_bwd(m, g, dy)
            dm_ref[rows, :] = dm.astype(_BF)
            dg_ref[...] += jnp.sum(dgr, axis=0, keepdims=True)

    return pl.pallas_call(
        body, name="ff2_loss", grid=(L // tm,),
        in_specs=[_row_spec(tm, D_FF), _row_spec(tm, D_MODEL), _row_spec(tm, D_MODEL),
                  _full_spec((1, D_MODEL)), _weight_spec((D_FF, D_MODEL))],
        out_specs=[_row_spec(tm, D_MODEL), _row_spec(tm, D_MODEL), _full_spec((1, D_MODEL)), _full_spec((1, D_MODEL))],
        out_shape=[jax.ShapeDtypeStruct((L, D_MODEL), _F32), jax.ShapeDtypeStruct((L, D_MODEL), _BF),
                   jax.ShapeDtypeStruct((1, D_MODEL), _F32), jax.ShapeDtypeStruct((1, D_MODEL), _F32)],
        compiler_params=_params("arbitrary"),
    )(f1, x2, tgt, g4, w2)


def _ff2_bwd(dm, f1, w2, tm, tn):
    L = dm.shape[0]
    last = L // tm - 1

    def body(dm_ref, f_ref, w_ref, df_ref, dw_ref, acc):
        @pl.when(pl.program_id(1) == 0)
        def _():
            acc[...] = jnp.zeros_like(acc)

        dmv = dm_ref[...]
        rl = jnp.maximum(f_ref[...], 0.0)
        df_ref[...] = (_dot_nt(dmv, w_ref[...]) * (2.0 * rl)).astype(_BF)
        acc[...] += _dot_tn((rl * rl).astype(_BF), dmv)

        @pl.when(pl.program_id(1) == last)
        def _():
            dw_ref[...] = acc[...].astype(_BF)

    return pl.pallas_call(
        body, name="ff2_bwd", grid=(D_FF // tn, L // tm),
        in_specs=[pl.BlockSpec((tm, D_MODEL), lambda j, i: (i, 0)), pl.BlockSpec((tm, tn), lambda j, i: (i, j)),
                  pl.BlockSpec((tn, D_MODEL), lambda j, i: (j, 0))],
        out_specs=[pl.BlockSpec((tm, tn), lambda j, i: (i, j)), pl.BlockSpec((tn, D_MODEL), lambda j, i: (j, 0))],
        out_shape=[jax.ShapeDtypeStruct((L, D_FF), _BF), jax.ShapeDtypeStruct((D_FF, D_MODEL), _BF)],
        scratch_shapes=[pltpu.VMEM((tn, D_MODEL), _F32)],
        compiler_params=_params("parallel", "arbitrary"),
    )(dm, f1, w2)


def _ff1_bwd(df1, w1, x2, mix, dy, g3, g2, tm):
    L = df1.shape[0]

    def body(df_ref, w_ref, x2_ref, mix_ref, dy_ref, g3_ref, g2_ref, dx2_ref, dmix_ref, dg3_ref, dg2_ref):
        @pl.when(pl.program_id(0) == 0)
        def _():
            dg3_ref[...] = jnp.zeros_like(dg3_ref)
            dg2_ref[...] = jnp.zeros_like(dg2_ref)

        for rows in _row_chunks(tm):
            dh = _dot_nt(df_ref[rows, 0:FF1_COLS], w_ref[0])
            for j in range(1, N_DEV):
                dh = dh + _dot_nt(df_ref[rows, j * FF1_COLS:(j + 1) * FF1_COLS], w_ref[j])
            dz, dgr = _rms_bwd(x2_ref[rows, :], g3_ref[...], dh)
            dg3_ref[...] += jnp.sum(dgr, axis=0, keepdims=True)
            dx2 = dy_ref[rows, :] + dz
            dx2_ref[rows, :] = dx2
            dmx, dgr2 = _rms_bwd(mix_ref[rows, :], g2_ref[...], dx2)
            dg2_ref[...] += jnp.sum(dgr2, axis=0, keepdims=True)
            dmix_ref[rows, :] = dmx.astype(_BF)

    vec = _full_spec((1, D_MODEL))
    return pl.pallas_call(
        body, name="ff1_bwd", grid=(L // tm,),
        in_specs=[_row_spec(tm, D_FF), _weight_spec((N_DEV, D_MODEL, FF1_COLS)), _row_spec(tm, D_MODEL),
                  _row_spec(tm, D_MODEL), _row_spec(tm, D_MODEL), vec, vec],
        out_specs=[_row_spec(tm, D_MODEL), _row_spec(tm, D_MODEL), vec, vec],
        out_shape=[jax.ShapeDtypeStruct((L, D_MODEL), _F32), jax.ShapeDtypeStruct((L, D_MODEL), _BF),
                   jax.ShapeDtypeStruct((1, D_MODEL), _F32), jax.ShapeDtypeStruct((1, D_MODEL), _F32)],
        compiler_params=_params("arbitrary"),
    )(df1, w1, x2, mix, dy, g3, g2)


def _matmul_tn(a, b, tm, tn, name, slots=False):
    L, K = a.shape
    N = b.shape[1]
    last = L // tm - 1

    def body(a_ref, b_ref, o_ref, acc):
        @pl.when(pl.program_id(1) == 0)
        def _():
            acc[...] = jnp.zeros_like(acc)

        acc[...] += _dot_tn(a_ref[...].astype(_BF), b_ref[...].astype(_BF))

        @pl.when(pl.program_id(1) == last)
        def _():
            if slots:
                o_ref[0] = acc[...].astype(_BF)
            else:
                o_ref[...] = acc[...].astype(_BF)

    if slots:
        out_spec = pl.BlockSpec((1, K, tn), lambda j, i: (j, 0, 0))
        out_shape = jax.ShapeDtypeStruct((N // tn, K, tn), _BF)
    else:
        out_spec = pl.BlockSpec((K, tn), lambda j, i: (0, j))
        out_shape = jax.ShapeDtypeStruct((K, N), _BF)
    return pl.pallas_call(
        body, name=name, grid=(N // tn, L // tm),
        in_specs=[pl.BlockSpec((tm, K), lambda j, i: (i, 0)), pl.BlockSpec((tm, tn), lambda j, i: (i, j))],
        out_specs=out_spec, out_shape=out_shape,
        scratch_shapes=[pltpu.VMEM((K, tn), _F32)],
        compiler_params=_params("parallel", "arbitrary"),
    )(a, b)


def _dw_in_t(pieces, h, tk):
    L = h.shape[0]
    last = L // tk - 1

    def body(p0, p1, p2, p3, p4, h_ref, o_ref, acc):
        @pl.when(pl.program_id(0) == 0)
        def _():
            acc[...] = jnp.zeros_like(acc)

        hv = h_ref[...]
        for j, p in enumerate((p0, p1, p2, p3, p4)):
            acc[j * RET_W:(j + 1) * RET_W, :] += _dot_tn(p[...].astype(_BF), hv)

        @pl.when(pl.program_id(0) == last)
        def _():
            o_ref[...] = acc[...].astype(_BF)

    return pl.pallas_call(
        body, name="dw_in", grid=(L // tk,),
        in_specs=[_row_spec(tk, RET_W)] * 5 + [_row_spec(tk, D_MODEL)],
        out_specs=_full_spec((IN_COLS, D_MODEL)), out_shape=jax.ShapeDtypeStruct((IN_COLS, D_MODEL), _BF),
        scratch_shapes=[pltpu.VMEM((IN_COLS, D_MODEL), _F32)],
        compiler_params=_params("arbitrary"),
    )(*pieces, h)


def _mixout_bwd(dmix, w_out, w_glu, glu, s, o, gate, ggn, tm, after=()):
    L = dmix.shape[0]

    def body(dmix_ref, wo_ref, wg_ref, glu_ref, s_ref, o_ref, gate_ref, ggn_ref,
             dglu_ref, ds_ref, dgate_ref, do_ref, dggn_ref):
        @pl.when(pl.program_id(0) == 0)
        def _():
            dggn_ref[...] = jnp.zeros_like(dggn_ref)

        dcat = _dot_nt(dmix_ref[...], wo_ref[...])
        dy_ret, dy_ssm = dcat[:, :RET_W], dcat[:, RET_W:]
        glu = glu_ref[...]
        ga, sg = glu[:, :SSM_W], _sigmoid(glu[:, SSM_W:])
        dga = (dy_ssm * sg).astype(_BF)
        dgb = (dy_ssm * ga * sg * (1.0 - sg)).astype(_BF)
        dglu_ref[:, :SSM_W] = dga
        dglu_ref[:, SSM_W:] = dgb
        dys = _dot_nt(dga, wg_ref[:, :SSM_W]) + _dot_nt(dgb, wg_ref[:, SSM_W:])
        ds_ref[...] = dys * _gelu_grad(s_ref[...])
        gt = gate_ref[...]
        sgt = _sigmoid(gt)
        ggn = ggn_ref[...]
        for hh in range(N_HEAD):
            cols = slice(hh * HEAD_D, (hh + 1) * HEAD_D)
            ov = o_ref[:, cols]
            dlt = ov - jnp.mean(ov, axis=-1, keepdims=True)
            rstd = lax.rsqrt(jnp.mean(dlt * dlt, axis=-1, keepdims=True) + NORM_EPS)
            on = dlt * rstd
            dyr = dy_ret[:, cols] * (gt[:, cols] * sgt[:, cols])
            dgate_ref[:, cols] = dy_ret[:, cols] * (on * ggn[:, cols]) * (sgt[:, cols] * (1.0 + gt[:, cols] * (1.0 - sgt[:, cols])))
            dggn_ref[:, cols] += jnp.sum(dyr * on, axis=0, keepdims=True)
            don = dyr * ggn[:, cols]
            do = rstd * (don - jnp.mean(don, axis=-1, keepdims=True) - on * jnp.mean(don * on, axis=-1, keepdims=True))
            do_ref[:, cols] = do.astype(_BF)

    body, in_specs, operands = _ordered(
        body, [_row_spec(tm, D_MODEL), _weight_spec((D_MODEL, D_MODEL)), _weight_spec((SSM_W, 2 * SSM_W)),
               _row_spec(tm, 2 * SSM_W), _row_spec(tm, SSM_W), _row_spec(tm, RET_W), _row_spec(tm, RET_W),
               _full_spec((1, RET_W))], (dmix, w_out, w_glu, glu, s, o, gate, ggn), after)
    return pl.pallas_call(
        body, name="mixout_bwd", grid=(L // tm,),
        in_specs=in_specs,
        out_specs=[_row_spec(tm, 2 * SSM_W), _row_spec(tm, SSM_W), _row_spec(tm, RET_W), _row_spec(tm, RET_W),
                   _full_spec((1, RET_W))],
        out_shape=[jax.ShapeDtypeStruct((L, 2 * SSM_W), _BF), jax.ShapeDtypeStruct((L, SSM_W), _F32),
                   jax.ShapeDtypeStruct((L, RET_W), _F32), jax.ShapeDtypeStruct((L, RET_W), _BF),
                   jax.ShapeDtypeStruct((1, RET_W), _F32)],
        compiler_params=_params("arbitrary"),
    )(*operands)


def _s5_bwd(u, ds, xs, ent, bmat, cmat, tab_r, pw_r, d_skip, tb, after=()):
    L = u.shape[0]
    nt = L // tb
    seg = tb // SUBLANES
    G = KB_PER_STEP
    rcol = pl.BlockSpec((tb, G * LANES), lambda kb, t: (nt - 1 - t, kb))
    sp = _s5_specs(seg, time=lambda t: nt - 1 - t)
    aspec = pl.BlockSpec((G, SUBLANES, 2 * KB_STATES), lambda kb, t: (kb, 0, 0))

    def body(u_ref, ds_ref, x_ref, ent_ref, b_ref, c_ref, tr_ref, pr_ref, d_ref,
             du_ref, db_ref, dc_ref, da_ref, dd_ref, up_scr, dp_scr, g_scr, lc_scr):
        @pl.when(pl.program_id(1) == 0)
        def _():
            lc_scr[...] = jnp.zeros_like(lc_scr)
            db_ref[...] = jnp.zeros_like(db_ref)
            dc_ref[...] = jnp.zeros_like(dc_ref)
            da_ref[...] = jnp.zeros_like(da_ref)
            dd_ref[...] = jnp.zeros_like(dd_ref)

        _rows_to_segments(up_scr, u_ref, seg)
        _rows_to_segments(dp_scr, ds_ref, seg)
        for g in range(G):
            g_scr[g] = _dot_nt(dp_scr[g].astype(_BF), c_ref[g]).reshape(seg, SUBLANES, 2 * KB_STATES)
        _scan_segments(g_scr, tr_ref, pr_ref, lc_scr, seg, reverse=True, fwd_ref=x_ref, fwd_entry_ref=ent_ref.at[:, 0],
                       da_ref=da_ref)
        for g in range(G):
            cols = slice(g * LANES, (g + 1) * LANES)
            uv, dsv = up_scr[g], dp_scr[g]
            ub, dsb = uv.astype(_BF), dsv.astype(_BF)
            lamb = g_scr[g].reshape(tb, 2 * KB_STATES).astype(_BF)
            db_ref[g] += _dot_tn(ub, lamb)
            dc_ref[g] += _dot_tn(dsb, x_ref[g].reshape(tb, 2 * KB_STATES).astype(_BF))
            dd_ref[:, cols] += jnp.sum(dsv * uv, axis=0, keepdims=True)
            up_scr[g] = _dot_nt(lamb, b_ref[g]) + d_ref[:, cols] * dsv
        _segments_to_rows(du_ref, up_scr, seg)

    body, in_specs, operands = _ordered(
        body, [rcol, rcol, sp["x"], sp["ent"], sp["b"], sp["c"], sp["tab"], sp["pw"], sp["d"]],
        (u, ds, xs, ent, bmat, cmat, tab_r, pw_r, d_skip), after)
    return pl.pallas_call(
        body, name="s5_bwd", grid=(N_KB // G, nt),
        in_specs=in_specs,
        out_specs=[rcol, sp["b"], sp["b"], aspec, sp["d"]],
        out_shape=[jax.ShapeDtypeStruct((L, SSM_W), _F32),
                   jax.ShapeDtypeStruct((N_KB, LANES, 2 * KB_STATES), _F32),
                   jax.ShapeDtypeStruct((N_KB, LANES, 2 * KB_STATES), _F32),
                   jax.ShapeDtypeStruct((N_KB, SUBLANES, 2 * KB_STATES), _F32),
                   jax.ShapeDtypeStruct((1, SSM_W), _F32)],
        scratch_shapes=[pltpu.VMEM((G, tb, LANES), _F32)] * 2
        + [pltpu.VMEM((G, seg, SUBLANES, 2 * KB_STATES), _F32), pltpu.VMEM((G, SUBLANES, 2 * KB_STATES), _F32)],
        compiler_params=_params("parallel", "arbitrary"),
    )(*operands)


def _retention_bwd(q, k, v, do, r_prev, consts, cosf, sinf, after=()):
    L = q.shape[0]
    nc = L // CHUNK
    cps = math.gcd(RET_STEP_CHUNKS, nc)
    nb = nc // cps
    blk = pl.BlockSpec((cps * CHUNK, RET_W), lambda n: (nb - 1 - n, 0))
    rope_blk = pl.BlockSpec((cps * CHUNK, HEAD_D), lambda n: (nb - 1 - n, 0))

    def body(q_ref, k_ref, v_ref, do_ref, rp_ref, dm_ref, xi_ref, zeta_ref, gc_ref, cos_ref, sin_ref,
             dq_ref, dk_ref, dv_ref, g_scr):
        @pl.when(pl.program_id(0) == 0)
        def _():
            g_scr[...] = jnp.zeros_like(g_scr)

        for hh in range(N_HEAD):
            cols = slice(hh * HEAD_D, (hh + 1) * HEAD_D)
            dm, zeta = dm_ref[hh], zeta_ref[hh]
            gst = g_scr[hh]
            for c in reversed(range(cps)):
                rows = slice(c * CHUNK, (c + 1) * CHUNK)
                qv, kv, vv, dov = q_ref[rows, cols], k_ref[rows, cols], v_ref[rows, cols], do_ref[rows, cols]
                rb = rp_ref[hh, c].astype(_BF)
                gb = gst.astype(_BF)
                sb = (_dot_nt(qv, kv) * dm).astype(_BF)
                dab = (_dot_nt(dov, vv) * dm).astype(_BF)
                dox = (dov.astype(_F32) * xi_ref[hh]).astype(_BF)
                vz = (vv.astype(_F32) * zeta).astype(_BF)
                dq = _dot(dab, kv) + _dot_nt(dox, rb)
                dk = _dot_tn(dab, qv) + _dot_nt(vz, gb)
                dv = _dot_tn(sb, dov) + _dot(kv, gb) * zeta
                gst = gc_ref[hh, 0:1, :] * gst + _dot_tn(qv, dox)
                cs, sn = cos_ref[rows, :], sin_ref[rows, :]
                dq_ref[rows, cols] = _rope_t(dq, cs, sn).astype(_BF)
                dk_ref[rows, cols] = (_rope_t(dk, cs, sn) * (HEAD_D ** -0.5)).astype(_BF)
                dv_ref[rows, cols] = dv.astype(_BF)
            g_scr[hh] = gst

    body, in_specs, operands = _ordered(
        body, [blk, blk, blk, blk, pl.BlockSpec((N_HEAD, cps, HEAD_D, HEAD_D), lambda n: (0, nb - 1 - n, 0, 0))]
        + _head_specs() + [rope_blk, rope_blk], (q, k, v, do, r_prev, *consts, cosf, sinf), after)
    return pl.pallas_call(
        body, name="retention_bwd", grid=(nb,),
        in_specs=in_specs,
        out_specs=[blk, blk, blk],
        out_shape=[jax.ShapeDtypeStruct((L, RET_W), _BF)] * 3,
        scratch_shapes=[pltpu.VMEM((N_HEAD, HEAD_D, HEAD_D), _F32)],
        compiler_params=_params("arbitrary"),
    )(*operands)


def _inproj_bwd(pieces, w_in_t, x, dx2, g1, tm, after=()):
    L = x.shape[0]

    def body(p0, p1, p2, p3, p4, w_ref, x_ref, dx2_ref, g_ref, dx_ref, dg_ref):
        @pl.when(pl.program_id(0) == 0)
        def _():
            dg_ref[...] = jnp.zeros_like(dg_ref)

        for rows in _row_chunks(tm):
            dh = None
            for j, p in enumerate((p0, p1, p2, p3, p4)):
                part = _dot(p[rows, :].astype(_BF), w_ref[j * RET_W:(j + 1) * RET_W, :])
                dh = part if dh is None else dh + part
            dz, dgr = _rms_bwd(x_ref[rows, :], g_ref[...], dh)
            dx_ref[rows, :] = dx2_ref[rows, :] + dz
            dg_ref[...] += jnp.sum(dgr, axis=0, keepdims=True)

    body, in_specs, operands = _ordered(
        body, [_row_spec(tm, RET_W)] * 5 + [_weight_spec((IN_COLS, D_MODEL)), _row_spec(tm, D_MODEL),
                                             _row_spec(tm, D_MODEL), _full_spec((1, D_MODEL))],
        (*pieces, w_in_t, x, dx2, g1), after)
    return pl.pallas_call(
        body, name="inproj_bwd", grid=(L // tm,),
        in_specs=in_specs,
        out_specs=[_row_spec(tm, D_MODEL), _full_spec((1, D_MODEL))],
        out_shape=[jax.ShapeDtypeStruct((L, D_MODEL), _F32), jax.ShapeDtypeStruct((1, D_MODEL), _F32)],
        compiler_params=_params("arbitrary"),
    )(*operands)


def _sum_adamw(parts, w, m, v, tr, name):
    _, R, Cc = parts.shape

    def body(p_ref, w_ref, m_ref, v_ref, g_ref, d_ref, nm_ref, nv_ref):
        gv = p_ref[0].astype(_F32)
        for s in range(1, N_DEV):
            gv = gv + p_ref[s].astype(_F32)
        g_ref[...] = gv
        nm = ADAM_B1 * m_ref[...] + (1.0 - ADAM_B1) * gv
        nv = ADAM_B2 * v_ref[...] + (1.0 - ADAM_B2) * (gv * gv)
        m_hat = nm / (1.0 - ADAM_B1 ** ADAM_STEP)
        v_hat = nv / (1.0 - ADAM_B2 ** ADAM_STEP)
        d_ref[...] = -ADAM_LR * (m_hat / (jnp.sqrt(v_hat) + ADAM_EPS) + ADAM_WD * w_ref[...])
        nm_ref[...] = nm
        nv_ref[...] = nv

    spec = _row_spec(tr, Cc)
    return pl.pallas_call(
        body, name=name, grid=(R // tr,),
        in_specs=[pl.BlockSpec((N_DEV, tr, Cc), lambda i: (0, i, 0))] + [spec] * 3, out_specs=[spec] * 4,
        out_shape=[jax.ShapeDtypeStruct((R, Cc), _F32)] * 4,
        compiler_params=_params("parallel"),
    )(parts, w, m, v)


def _my_place():
    return lax.axis_index("x"), lax.axis_index("y"), lax.axis_index("c")


def _all_gather(blocks):
    n = len(blocks)

    def body(*refs):
        x_refs, out_refs, done_ref = refs[:n], refs[n:2 * n], refs[2 * n]
        send_sems, recv_sems, local_sems = refs[2 * n + 1:]
        done_ref[...] = jnp.zeros_like(done_ref)
        x, y, c = _my_place()
        me, sibling = (x, y, c), (x, y, 1 - c)
        chips = [(1 - x, y), (x, 1 - y), (1 - x, 1 - y)]

        def slot(a, px, py, pc):
            return out_refs[a].at[4 * px + 2 * py + pc]

        def copy(a, k, blk, to, own=False):
            return pltpu.make_async_remote_copy(
                src_ref=x_refs[a] if own else slot(a, *blk), dst_ref=slot(a, *blk),
                send_sem=send_sems.at[a, k], recv_sem=recv_sems.at[a, k], device_id=to, device_id_type=MESH)

        mine = [pltpu.make_async_copy(x_refs[a], slot(a, *me), local_sems.at[a]) for a in range(n)]
        for cp in mine:
            cp.start()
        first = []
        for a in range(n):
            first.append(copy(a, 0, me, sibling, own=True))
            first += [copy(a, 1 + j, me, (*chip, c), own=True) for j, chip in enumerate(chips)]
        for cp in first:
            cp.start()
        passed = []
        for j, chip in enumerate(chips):
            for a in range(n):
                copy(a, 1 + j, (*chip, c), me).wait_recv()
                fwd = copy(a, 4 + j, (*chip, c), sibling)
                fwd.start()
                passed.append(fwd)
        for a in range(n):
            copy(a, 0, sibling, me).wait_recv()
            for j, chip in enumerate(chips):
                copy(a, 4 + j, (*chip, 1 - c), me).wait_recv()
        for cp in first + passed:
            cp.wait_send()
        for cp in mine:
            cp.wait()

    any_spec = pl.BlockSpec(memory_space=pl.ANY)
    outs = pl.pallas_call(
        body, name="weights_all_gather",
        in_specs=[any_spec] * n, out_specs=[any_spec] * n + [pl.BlockSpec(memory_space=pltpu.VMEM)],
        out_shape=[jax.ShapeDtypeStruct((N_DEV,) + b.shape, b.dtype) for b in blocks]
        + [jax.ShapeDtypeStruct((SUBLANES, LANES), _F32)],
        scratch_shapes=[pltpu.SemaphoreType.DMA((n, 7)), pltpu.SemaphoreType.DMA((n, 7)), pltpu.SemaphoreType.DMA((n,))],
    )(*blocks)
    return outs[:n], outs[n]


def _exchange(bigs, small):
    n = len(bigs)
    r = small.shape[0]

    def body(*refs):
        in_refs, out_refs = refs[:n + 1], refs[n + 1:2 * n + 2]
        send_sems, recv_sems, local_sems = refs[2 * n + 2:]
        x, y, c = _my_place()
        me = 4 * x + 2 * y + c
        own = [pltpu.make_async_copy(in_refs[a].at[me], out_refs[a].at[me], local_sems.at[a]) for a in range(n)]
        own.append(pltpu.make_async_copy(in_refs[n], out_refs[n].at[me], local_sems.at[n]))
        for cp in own:
            cp.start()
        copies = []
        for kk in range(1, N_DEV):
            px, py, pc = x ^ (kk >> 2), y ^ ((kk >> 1) & 1), c ^ (kk & 1)
            peer = 4 * px + 2 * py + pc
            for a in range(n + 1):
                src = in_refs[a].at[peer] if a < n else in_refs[a]
                copies.append(pltpu.make_async_remote_copy(
                    src_ref=src, dst_ref=out_refs[a].at[me],
                    send_sem=send_sems.at[a, kk - 1], recv_sem=recv_sems.at[a, kk - 1],
                    device_id=(px, py, pc), device_id_type=MESH))
        for cp in copies:
            cp.start()
        for cp in copies:
            cp.wait_recv()
        for cp in copies:
            cp.wait_send()
        for cp in own:
            cp.wait()

    any_spec = pl.BlockSpec(memory_space=pl.ANY)
    outs = pl.pallas_call(
        body, name="grad_exchange",
        in_specs=[any_spec] * (n + 1), out_specs=[any_spec] * (n + 1),
        out_shape=[jax.ShapeDtypeStruct(b.shape, b.dtype) for b in bigs]
        + [jax.ShapeDtypeStruct((N_DEV, r, LANES), small.dtype)],
        scratch_shapes=[pltpu.SemaphoreType.DMA((n + 1, 7)), pltpu.SemaphoreType.DMA((n + 1, 7)),
                        pltpu.SemaphoreType.DMA((n + 1,))],
    )(*bigs, small)
    return outs[:n], outs[n]


HBM_SPEC = pl.BlockSpec(memory_space=pltpu.HBM)
SEM_SPEC = pl.BlockSpec(memory_space=pltpu.SEMAPHORE)
DATAFLOW = pltpu.SideEffectType.DATAFLOW_SIDE_EFFECTING


def _my_index():
    x, y, c = _my_place()
    return 4 * x + 2 * y + c


def _landing(own_block):
    zone = lax.empty((N_DEV,) + own_block.shape, own_block.dtype)
    return lax.dynamic_update_index_in_dim(zone, own_block, _my_index(), 0)


def _split_copies(src_refs, land_refs, send_sems, recv_sems, gather, first=0):
    x, y, c = _my_place()
    me = 4 * x + 2 * y + c
    copies = []
    for a, (src, land) in enumerate(zip(src_refs, land_refs)):
        for kk in range(1, N_DEV):
            px, py, pc = x ^ (kk >> 2), y ^ ((kk >> 1) & 1), c ^ (kk & 1)
            peer = 4 * px + 2 * py + pc
            copies.append(pltpu.make_async_remote_copy(
                src_ref=src if gather else src.at[peer], dst_ref=land.at[me],
                send_sem=send_sems.at[(first + a) * 7 + kk - 1], recv_sem=recv_sems.at[(first + a) * 7 + kk - 1],
                device_id=(px, py, pc), device_id_type=MESH))
    return copies


def _split_start(srcs, lands, gather, name):
    n = len(srcs)

    def body(*refs):
        src_refs, land_refs = refs[:n], refs[n:2 * n]
        send_sems, recv_sems = refs[2 * n], refs[2 * n + 1]
        token = refs[-1]
        for cp in _split_copies(src_refs, land_refs, send_sems, recv_sems, gather):
            cp.start()
        token[...] = jnp.zeros_like(token)

    outs = pl.pallas_call(
        body, name=name,
        out_shape=(pltpu.SemaphoreType.DMA((7 * n,)), pltpu.SemaphoreType.DMA((7 * n,)),
                   *[pltpu.HBM(t.shape, t.dtype) for t in srcs], *[pltpu.HBM(t.shape, t.dtype) for t in lands],
                   jax.ShapeDtypeStruct((SUBLANES, LANES), _F32)),
        in_specs=[HBM_SPEC] * (2 * n),
        out_specs=(SEM_SPEC, SEM_SPEC, *[HBM_SPEC] * (2 * n), pl.BlockSpec(memory_space=pltpu.VMEM)),
        input_output_aliases={i: 2 + i for i in range(2 * n)},
        compiler_params=pltpu.CompilerParams(has_side_effects=DATAFLOW),
    )(*[pltpu.with_memory_space_constraint(t, pltpu.HBM) for t in list(srcs) + list(lands)])
    return outs[0], outs[1], outs[2:2 + n], outs[2 + n:2 + 2 * n], outs[-1]


def _split_wait(send_sems, recv_sems, srcs, lands, after, gather, name, first=0):
    n = len(srcs)

    def body(*refs):
        src_refs, land_refs = refs[:n], refs[n:2 * n]
        send_s, recv_s = refs[2 * n], refs[2 * n + 1]
        for cp in _split_copies(src_refs, land_refs, send_s, recv_s, gather, first):
            cp.wait_send()
            cp.wait_recv()

    outs = pl.pallas_call(
        body, name=name,
        out_shape=tuple(pltpu.HBM(t.shape, t.dtype) for t in list(srcs) + list(lands)),
        in_specs=[HBM_SPEC] * (2 * n) + [SEM_SPEC, SEM_SPEC, pl.BlockSpec(memory_space=pl.ANY)],
        out_specs=tuple([HBM_SPEC] * (2 * n)),
        input_output_aliases={i: i for i in range(2 * n)},
        compiler_params=pltpu.CompilerParams(has_side_effects=DATAFLOW),
    )(*srcs, *lands, send_sems, recv_sems, after)
    return outs[n:]


def _discretize(lam_re, lam_im, log_dt, b_re, b_im):
    lr = jnp.minimum(lam_re, -1e-4)
    li = lam_im
    dt = jnp.exp(log_dt)[:, None]
    er = jnp.exp(lr * dt)
    ar, ai = er * jnp.cos(li * dt), er * jnp.sin(li * dt)
    den = lr * lr + li * li
    cr = ((ar - 1.0) * lr + ai * li) / den
    ci = (ai * lr - (ar - 1.0) * li) / den
    bbr = cr[:, :, None] * b_re - ci[:, :, None] * b_im
    bbi = cr[:, :, None] * b_im + ci[:, :, None] * b_re
    return ar, ai, bbr, bbi


def _cmul(ar, ai, br, bi):
    return ar * br - ai * bi, ar * bi + ai * br


def _cpowers(ar, ai, n):
    pr, pi = ar[None], ai[None]
    while pr.shape[0] < n:
        nr, ni = _cmul(pr, pi, pr[-1][None], pi[-1][None])
        pr, pi = jnp.concatenate([pr, nr]), jnp.concatenate([pi, ni])
    return pr[:n], pi[:n]


def _scan_tables(ar, ai, seg, reverse):
    if reverse:
        ai = -ai
    ar, ai = ar.reshape(N_KB, KB_STATES), ai.reshape(N_KB, KB_STATES)
    pr, pi = _cpowers(ar, ai, seg)
    a1 = (pr[-1], pi[-1])
    a2 = _cmul(*a1, *a1)
    a4 = _cmul(*a2, *a2)
    row = jnp.arange(SUBLANES)[None, :, None]
    wide = lambda t: jnp.broadcast_to(t[:, None, :], (N_KB, SUBLANES, KB_STATES))
    tabs = [wide(ar), wide(ai)]
    for dist, (qr, qi) in ((1, a1), (2, a2), (4, a4)):
        keep = (row < SUBLANES - dist) if reverse else (row >= dist)
        tabs += [jnp.where(keep, wide(qr), 0.0), jnp.where(keep, wide(qi), 0.0)]
    tabs += [wide(a1[0]), wide(a1[1])]
    if reverse:
        pr, pi = pr[::-1], pi[::-1]
    pw = jnp.transpose(jnp.concatenate([pr, pi], axis=-1), (1, 0, 2))[:, :, None, :]
    return jnp.stack(tabs, axis=1).astype(_F32), pw.astype(_F32)


def _block_diag_in(br, bi):
    eye = jnp.eye(GROUPS_PER_KB, dtype=_F32)
    one = lambda t: jnp.einsum("kgpc,gh->kgchp", t.reshape(N_KB, GROUPS_PER_KB, N_STATE, SSM_GC), eye).reshape(
        N_KB, LANES, KB_STATES)
    return jnp.concatenate([one(br), one(bi)], axis=-1)


def _block_diag_in_t(dmat):
    d6 = dmat.reshape(N_KB, GROUPS_PER_KB, SSM_GC, 2, GROUPS_PER_KB, N_STATE)
    eye = jnp.eye(GROUPS_PER_KB, dtype=_F32)
    both = jnp.einsum("kgcrhp,gh->rkgpc", d6, eye).reshape(2, N_GROUP, N_STATE, SSM_GC)
    return both[0], both[1]


def _block_diag_out(c_re, c_im):
    eye = jnp.eye(GROUPS_PER_KB, dtype=_F32)
    one = lambda t: jnp.einsum("kgcp,gh->khpgc", t.reshape(N_KB, GROUPS_PER_KB, SSM_GC, N_STATE), eye).reshape(
        N_KB, KB_STATES, LANES)
    return jnp.concatenate([one(c_re), -one(c_im)], axis=1)


def _block_diag_out_t(dmat_t):
    d6 = dmat_t.reshape(N_KB, GROUPS_PER_KB, SSM_GC, 2, GROUPS_PER_KB, N_STATE)
    eye = jnp.eye(GROUPS_PER_KB, dtype=_F32)
    both = jnp.einsum("kgcrhp,gh->rkgcp", d6, eye).reshape(2, N_GROUP, SSM_GC, N_STATE)
    return both[0], -both[1]


SMALL_NAMES = ("norm_mix_pre", "norm_mix_post", "ret_gn_gain", "ssm_lambda_re", "ssm_lambda_im", "ssm_log_dt",
               "ssm_b_re", "ssm_b_im", "ssm_c_re", "ssm_c_im", "ssm_d", "norm_mlp_pre", "norm_mlp_post")


def _local_grads(x, tgt, small, weights, emit, emit_small, tm, tk, tb, zero=0.0):
    L = x.shape[0]
    g1, g2, ggn = small["norm_mix_pre"], small["norm_mix_post"], small["ret_gn_gain"]
    g3, g4, d_skip = small["norm_mlp_pre"], small["norm_mlp_post"], small["ssm_d"]

    rope = _rope_tables(L)
    consts = _ret_consts()

    disc_in = (small["ssm_lambda_re"][0], small["ssm_lambda_im"][0], small["ssm_log_dt"][0] + zero,
               small["ssm_b_re"][0], small["ssm_b_im"][0])
    (ar, ai, bbr, bbi), disc_vjp = jax.vjp(_discretize, *disc_in)
    bmat = _block_diag_in(bbr, bbi).astype(_BF)
    cmat = _block_diag_out(small["ssm_c_re"][0], small["ssm_c_im"][0]).astype(_BF)
    seg = tb // SUBLANES
    tab_f, pw_f = _scan_tables(ar, ai, seg, False)
    tab_r, pw_r = _scan_tables(ar, ai, seg, True)

    h1 = _prenorm(x, g1, min(2 * tm, L), after=(pw_r,))
    (w_in_t,) = weights("in", h1)
    q, k, v, gate, u, cosf, sinf = _inproj_fwd(h1, w_in_t, rope, tm)
    o, y_ret, r_prev = _retention_fwd(q, k, v, gate, ggn, consts)
    s, xs, ent = _s5_fwd(u, bmat, cmat, tab_f, pw_f, d_skip, tb)
    w_glu, w_out = weights("mix", s)
    ys, glu, cat, mix, x2 = _mixout_fwd(s, y_ret, x, w_glu, w_out, g2, min(2 * tm, L))
    w_ff1, w_ff2 = weights("mlp", x2)
    h3, f1 = _ff1_fwd(x2, g3, w_ff1, tm)
    dy, dm, dg4, sq = _ff2_loss(f1, x2, tgt, g4, w_ff2, min(2 * tm, L))

    df1, dw_ff2 = _ff2_bwd(dm, f1, w_ff2, min(1024, L), 1024)
    dx2, dmix, dg3, dg2 = _ff1_bwd(df1, w_ff1, x2, mix, dy, g3, g2, min(2 * tm, L))
    dw_ff1 = _matmul_tn(h3, df1, tk, FF1_COLS, "dw_ff1", slots=True)
    token = emit({"w_ff1": dw_ff1, "w_ff2": dw_ff2})
    dglu, ds, dgate, do, dggn = _mixout_bwd(dmix, w_out, w_glu, glu, s, o, gate, ggn, tm, after=token)
    dw_out = _matmul_tn(cat, dmix, tk, 1024, "dw_out")
    dw_glu = _matmul_tn(ys, dglu, tk, 1024, "dw_glu")
    token = emit({"w_glu": dw_glu, "w_out": dw_out})
    du, dbmat, dcmat, da8, dd = _s5_bwd(u, ds, xs, ent, bmat, cmat, tab_r, pw_r, d_skip, tb, after=token)

    da = jnp.sum(da8, axis=1)
    dar = da[:, :KB_STATES].reshape(N_GROUP, N_STATE)
    dai = da[:, KB_STATES:].reshape(N_GROUP, N_STATE)
    dbr, dbi = _block_diag_in_t(dbmat)
    dlre, dlim, dldt, dbre, dbim = disc_vjp((dar, dai, dbr, dbi))
    dcre, dcim = _block_diag_out_t(dcmat)
    token = emit_small({
        "norm_mix_post": dg2, "ret_gn_gain": dggn,
        "ssm_lambda_re": dlre[None], "ssm_lambda_im": dlim[None], "ssm_log_dt": dldt[None],
        "ssm_b_re": dbre[None], "ssm_b_im": dbim[None], "ssm_c_re": dcre[None], "ssm_c_im": dcim[None],
        "ssm_d": dd, "norm_mlp_pre": dg3, "norm_mlp_post": dg4,
    }, sq)

    dq, dk, dv = _retention_bwd(q, k, v, do, r_prev, consts, cosf, sinf, after=token)
    pieces = (dq, dk, dv, dgate, du)
    dw_in_t = _dw_in_t(pieces, h1, min(1024, L))
    token = emit({"w_in": dw_in_t})
    gx, dg1 = _inproj_bwd(pieces, w_in_t, x, dx2, g1, min(2 * tm, L), after=token)
    return gx, dg1


BIG_SHAPES = {"w_in": (D_MODEL, IN_COLS // N_DEV), "w_glu": (SSM_W, 2 * SSM_W // N_DEV), "w_out": (D_MODEL // N_DEV, D_MODEL),
              "w_ff1": (D_MODEL, FF1_COLS), "w_ff2": (D_FF // N_DEV, D_MODEL)}
BIG_NAMES = ("w_in", "w_glu", "w_out", "w_ff1", "w_ff2")


def _cols_from_slots(g):
    return jnp.transpose(g, (1, 0, 2)).reshape(g.shape[1], N_DEV * g.shape[2])


def _cols_to_slots(dw):
    r, cols = dw.shape
    return jnp.transpose(dw.reshape(r, N_DEV, cols // N_DEV), (1, 0, 2))


WEIGHT_GROUPS = {"in": ("w_in",), "mix": ("w_glu", "w_out"), "mlp": ("w_ff1", "w_ff2")}


def _weight_from_slots(name, g):
    if name == "w_glu":
        return _cols_from_slots(g)
    if name == "w_ff1":
        return g
    return g.reshape(N_DEV * g.shape[1], g.shape[2])


def _grad_slots(name, dw):
    if name == "w_glu":
        return _cols_to_slots(dw)
    if name == "w_ff1":
        return dw
    if name == "w_in":
        return dw.reshape(N_DEV, BIG_SHAPES[name][1], BIG_SHAPES[name][0])
    return dw.reshape((N_DEV,) + BIG_SHAPES[name])


PIECE_ROWS = 8


VEC_NAMES = tuple(n for n in SMALL_NAMES if n[:6] not in ("ssm_b_", "ssm_c_"))
BC_NAMES = ("ssm_b_re", "ssm_b_im", "ssm_c_re", "ssm_c_im")
BC_ROWS = N_GROUP * SSM_GC


def _bc_view(name, t):
    t = t[0]
    if name.startswith("ssm_b_"):
        t = jnp.swapaxes(t, 1, 2)
    return t.reshape(BC_ROWS, N_STATE)


def _bc_unview(name, t):
    t = t.reshape(N_GROUP, SSM_GC, N_STATE)
    if name.startswith("ssm_b_"):
        t = jnp.swapaxes(t, 1, 2)
    return t[None]


def _pack_bc(vals):
    return jnp.concatenate([_bc_view(n, vals[n]).astype(_F32) for n in BC_NAMES], axis=0)


def _unpack_bc(buf):
    return {n: _bc_unview(n, buf[j * BC_ROWS:(j + 1) * BC_ROWS]) for j, n in enumerate(BC_NAMES)}


def _small_layout(shapes):
    off, rows = {}, 0
    for n in VEC_NAMES:
        off[n] = rows
        rows += -(-math.prod(shapes[n]) // (PIECE_ROWS * LANES)) * PIECE_ROWS
    return off, rows, rows + PIECE_ROWS


def _pack_small(vals, shapes, last=None):
    parts = []
    for n in VEC_NAMES:
        flat = vals[n].reshape(-1).astype(_F32)
        pad = -flat.shape[0] % (PIECE_ROWS * LANES)
        if pad:
            flat = jnp.concatenate([flat, jnp.zeros((pad,), _F32)])
        parts.append(flat.reshape(-1, LANES))
    parts.append(jnp.zeros((PIECE_ROWS, LANES), _F32) if last is None else last)
    return jnp.concatenate(parts, axis=0)


def _unpack_small(buf, shapes):
    off, _, _ = _small_layout(shapes)
    out = {}
    for n in VEC_NAMES:
        size = math.prod(shapes[n])
        rows = -(-size // LANES)
        out[n] = buf[off[n]:off[n] + rows].reshape(-1)[:size].reshape(shapes[n])
    return out


WEIGHT_NAMES = ('norm_mix_pre', 'norm_mix_post', 'w_in', 'ret_gn_gain', 'ssm_lambda_re', 'ssm_lambda_im', 'ssm_log_dt',
                'ssm_b_re', 'ssm_b_im', 'ssm_c_re', 'ssm_c_im', 'ssm_d', 'w_glu', 'w_out', 'norm_mlp_pre',
                'norm_mlp_post', 'w_ff1', 'w_ff2')


def kernel(x, norm_mix_pre, norm_mix_post, w_in, ret_gn_gain, ssm_lambda_re, ssm_lambda_im, ssm_log_dt, ssm_b_re, ssm_b_im, ssm_c_re, ssm_c_im, ssm_d, w_glu, w_out, norm_mlp_pre, norm_mlp_post, w_ff1, w_ff2, loss_target, m_norm_mix_pre, m_norm_mix_post, m_w_in, m_ret_gn_gain, m_ssm_lambda_re, m_ssm_lambda_im, m_ssm_log_dt, m_ssm_b_re, m_ssm_b_im, m_ssm_c_re, m_ssm_c_im, m_ssm_d, m_w_glu, m_w_out, m_norm_mlp_pre, m_norm_mlp_post, m_w_ff1, m_w_ff2, v_norm_mix_pre, v_norm_mix_post, v_w_in, v_ret_gn_gain, v_ssm_lambda_re, v_ssm_lambda_im, v_ssm_log_dt, v_ssm_b_re, v_ssm_b_im, v_ssm_c_re, v_ssm_c_im, v_ssm_d, v_w_glu, v_w_out, v_norm_mlp_pre, v_norm_mlp_post, v_w_ff1, v_w_ff2):
    args = dict(locals())
    w = {n: args[n] for n in WEIGHT_NAMES}
    m = {n: args["m_" + n] for n in WEIGHT_NAMES}
    v = {n: args["v_" + n] for n in WEIGHT_NAMES}
    L = x.shape[1]
    tm = min(256, L)
    tk = min(2048, L)
    tb = min(512, L)

    order = [n for names in WEIGHT_GROUPS.values() for n in names]
    blocks = [(w[n][0].T if n == "w_in" else w[n][0]).astype(_BF) for n in order]
    gathered = _split_start(blocks, [_landing(b) for b in blocks], True, "weights_start")
    zero = gathered[4][0, 0]

    def weights(group, after):
        names = WEIGHT_GROUPS[group]
        first = order.index(names[0])
        part = slice(first, first + len(names))
        landed = _split_wait(gathered[0], gathered[1], gathered[2][part], gathered[3][part], after, True,
                             "weights_wait_" + group, first=first)
        return [_weight_from_slots(n, g) for n, g in zip(names, landed)]

    in_flight = []

    def emit(dws):
        names = sorted(dws)
        srcs = [_grad_slots(n, dws[n]) for n in names]
        lands = [_landing(lax.dynamic_index_in_dim(t, _my_index(), 0, keepdims=False)) for t in srcs]
        started = _split_start(srcs, lands, False, "grads_start_" + "_".join(names))
        in_flight.append((names, started))
        return (started[4],)

    shapes = {n: w[n].shape for n in SMALL_NAMES}
    first_piece = {SMALL_NAMES[0]: jnp.zeros(shapes[SMALL_NAMES[0]], _F32)}
    small_flight = []

    def emit_small(gs, sq):
        loss_rows = jnp.broadcast_to(0.5 / D_MODEL * jnp.sum(sq), (PIECE_ROWS, LANES)).astype(_F32)
        bufs = [_pack_small({**first_piece, **gs}, shapes, loss_rows), _pack_bc(gs)]
        small_flight.append(_split_start(bufs, [_landing(b) for b in bufs], True, "small_grads_start"))
        return (small_flight[0][4],)

    small_w = {n: w[n] for n in SMALL_NAMES}
    gx, dg1 = _local_grads(x[0], loss_target[0], small_w, weights, emit, emit_small, tm, tk, tb, zero=zero)
    last_buf = dg1.reshape(PIECE_ROWS, LANES)
    last_started = _split_start([last_buf], [_landing(last_buf)], True, "last_grad_start")

    grads, delta, new_m, new_v = {}, {}, {}, {}
    after = last_started[4]
    for names, started in in_flight:
        landed = _split_wait(*started[:4], after, False, "grads_wait_" + "_".join(names))
        for n, parts in zip(names, landed):
            flip = (lambda t: t.T) if n == "w_in" else (lambda t: t)
            res = _sum_adamw(parts, flip(w[n][0]), flip(m[n][0]), flip(v[n][0]), math.gcd(256, parts.shape[1]), "adamw_" + n)
            grads[n], delta[n], new_m[n], new_v[n] = (flip(t)[None] for t in res)
        after = res[1]
    small_parts, bc_parts = _split_wait(*small_flight[0][:4], after, True, "small_grads_wait")
    last_parts = _split_wait(*last_started[:4], small_parts, True, "last_grad_wait")[0]
    small_parts = lax.dynamic_update_slice(small_parts, last_parts, (0, 0, 0))
    res_bc = _sum_adamw(bc_parts, _pack_bc(w), _pack_bc(m), _pack_bc(v), BC_ROWS, "adamw_bc")
    sw, sm, sv = _pack_small(w, shapes), _pack_small(m, shapes), _pack_small(v, shapes)
    res = _sum_adamw(small_parts, sw, sm, sv, sw.shape[0], "adamw_small")
    for dst, buf, buf_bc in zip((grads, delta, new_m, new_v), res, res_bc):
        dst.update(_unpack_small(buf, shapes))
        dst.update(_unpack_bc(buf_bc))
    _, loss_at, _ = _small_layout(shapes)
    loss = res[0][loss_at, 0]

    return (loss, gx[None], *[grads[n] for n in WEIGHT_NAMES], *[delta[n] for n in WEIGHT_NAMES],
            *[new_m[n] for n in WEIGHT_NAMES], *[new_v[n] for n in WEIGHT_NAMES])
```

```python
import math

import jax
import jax.numpy as jnp
from jax import lax
from jax.experimental import pallas as pl
from jax.experimental.pallas import tpu as pltpu

_BF = jnp.bfloat16
_F32 = jnp.float32

D_MODEL = 1024
RET_W = 512
N_HEAD = 4
HEAD_D = 128
CHUNK = 256
ROPE_CHUNK = 128
SSM_W = 512
SSM_GC = 16
N_GROUP = 32
N_STATE = 64
GROUPS_PER_KB = 8
N_KB = 4
KB_STATES = GROUPS_PER_KB * N_STATE
D_FF = 4096
IN_COLS = 2560
NORM_EPS = 1e-6
ROPE_BASE = 10000.0
N_DEV = 8

ADAM_LR = 0.001
ADAM_B1 = 0.9
ADAM_B2 = 0.999
ADAM_EPS = 1e-08
ADAM_WD = 0.01
ADAM_STEP = 10

SUBLANES = 8
LANES = 128
VMEM_LIMIT = 52 * 1024 * 1024
RET_STEP_CHUNKS = 2
KB_PER_STEP = 2
SCAN_UNROLL = True
FIX_UNROLL = 8

MESH = pl.DeviceIdType.MESH


def _params(*sem):
    return pltpu.CompilerParams(dimension_semantics=sem, vmem_limit_bytes=VMEM_LIMIT)


def _dot(a, b):
    return jnp.dot(a, b, preferred_element_type=_F32)


def _dot_nt(a, b):
    return lax.dot_general(a, b, (((1,), (1,)), ((), ())), preferred_element_type=_F32)


def _dot_tn(a, b):
    return lax.dot_general(a, b, (((0,), (0,)), ((), ())), preferred_element_type=_F32)


def _rms_r(z):
    return lax.rsqrt(jnp.mean(z * z, axis=-1, keepdims=True) + NORM_EPS)


def _rms_bwd(z, g, dn):
    r = _rms_r(z)
    t = dn * g
    dz = r * t - z * (r * r * r * jnp.mean(t * z, axis=-1, keepdims=True))
    return dz, dn * z * r


def _rope(t, cs, sn):
    return t * cs + pltpu.roll(t, HEAD_D // 2, 1) * sn


def _rope_t(t, cs, sn):
    return t * cs - pltpu.roll(t, HEAD_D // 2, 1) * sn


def _sigmoid(z):
    return 1.0 / (1.0 + jnp.exp(-z))


_GELU_C = math.sqrt(2.0 / math.pi)


def _gelu(z):
    return 0.5 * z * (1.0 + jnp.tanh(_GELU_C * (z + 0.044715 * z * z * z)))


def _gelu_grad(z):
    th = jnp.tanh(_GELU_C * (z + 0.044715 * z * z * z))
    return 0.5 * (1.0 + th) + 0.5 * z * (1.0 - th * th) * _GELU_C * (1.0 + 3 * 0.044715 * z * z)


ROW_CHUNK = 256


def _row_chunks(tm):
    return [pl.ds(i, min(ROW_CHUNK, tm)) for i in range(0, tm, ROW_CHUNK)]


def _ordered(body, in_specs, operands, after):
    k = len(after)
    if not k:
        return body, list(in_specs), tuple(operands)
    return ((lambda *refs: body(*refs[k:])), [pl.BlockSpec(memory_space=pl.ANY)] * k + list(in_specs),
            tuple(after) + tuple(operands))


def _row_spec(tm, n):
    return pl.BlockSpec((tm, n), lambda i: (i, 0))


def _full_spec(shape):
    nd = len(shape)
    return pl.BlockSpec(shape, lambda *_: (0,) * nd)


def _weight_spec(shape):
    nd = len(shape)
    return pl.BlockSpec(shape, lambda *_: (0,) * nd, pipeline_mode=pl.Buffered(1))


def _rope_tables(L):
    half = HEAD_D // 2
    inv_freq = ROPE_BASE ** (-jnp.arange(half, dtype=_F32) / half)
    twice = lambda t: jnp.concatenate([t, t], axis=-1)
    off = jnp.arange(ROPE_CHUNK, dtype=_F32)[:, None] * inv_freq[None, :]
    start = (ROPE_CHUNK * jnp.arange(L // ROPE_CHUNK, dtype=_F32))[:, None] * inv_freq[None, :]
    return (twice(jnp.cos(off)), twice(jnp.sin(off)),
            twice(jnp.cos(start))[:, None, :], twice(jnp.sin(start))[:, None, :])


def _prenorm(x, g, tm, after=()):
    L = x.shape[0]

    def body(x_ref, g_ref, h_ref):
        xv = x_ref[...]
        h_ref[...] = (xv * _rms_r(xv) * g_ref[...]).astype(_BF)

    body, in_specs, operands = _ordered(body, [_row_spec(tm, D_MODEL), _full_spec((1, D_MODEL))], (x, g), after)
    return pl.pallas_call(
        body, name="prenorm", grid=(L // tm,),
        in_specs=in_specs, out_specs=_row_spec(tm, D_MODEL),
        out_shape=jax.ShapeDtypeStruct((L, D_MODEL), _BF),
        compiler_params=_params("parallel"),
    )(*operands)


def _inproj_fwd(h, w_in_t, rope, tm):
    L = h.shape[0]
    n_chunks = tm // ROPE_CHUNK

    def body(h_ref, w_ref, co_ref, so_ref, cs_ref, ss_ref, q_ref, k_ref, v_ref, gate_ref, u_ref, cos_ref, sin_ref):
        proj = _dot_nt(h_ref[...], w_ref[...])
        lane = lax.broadcasted_iota(jnp.int32, (ROPE_CHUNK, HEAD_D), 1)
        sign = jnp.where(lane < HEAD_D // 2, -1.0, 1.0)
        co, so = co_ref[...], so_ref[...]
        for c in range(n_chunks):
            chunk = pl.program_id(0) * n_chunks + c
            cst, sst = cs_ref[chunk], ss_ref[chunk]
            rows = slice(c * ROPE_CHUNK, (c + 1) * ROPE_CHUNK)
            cs = co * cst - so * sst
            sn = (so * cst + co * sst) * sign
            cos_ref[rows, :] = cs
            sin_ref[rows, :] = sn
            for hh in range(N_HEAD):
                lo = hh * HEAD_D
                q_ref[rows, lo:lo + HEAD_D] = _rope(proj[rows, lo:lo + HEAD_D], cs, sn).astype(_BF)
                kh = _rope(proj[rows, RET_W + lo:RET_W + lo + HEAD_D], cs, sn) * (HEAD_D ** -0.5)
                k_ref[rows, lo:lo + HEAD_D] = kh.astype(_BF)
        v_ref[...] = proj[:, 2 * RET_W:3 * RET_W].astype(_BF)
        gate_ref[...] = proj[:, 3 * RET_W:4 * RET_W]
        u_ref[...] = proj[:, 4 * RET_W:]

    nc = L // ROPE_CHUNK
    return pl.pallas_call(
        body, name="inproj_fwd", grid=(L // tm,),
        in_specs=[_row_spec(tm, D_MODEL), _weight_spec((IN_COLS, D_MODEL)),
                  _full_spec((ROPE_CHUNK, HEAD_D)), _full_spec((ROPE_CHUNK, HEAD_D)),
                  _full_spec((nc, 1, HEAD_D)), _full_spec((nc, 1, HEAD_D))],
        out_specs=[_row_spec(tm, RET_W)] * 5 + [_row_spec(tm, HEAD_D)] * 2,
        out_shape=[jax.ShapeDtypeStruct((L, RET_W), _BF)] * 3 + [jax.ShapeDtypeStruct((L, RET_W), _F32)] * 2
        + [jax.ShapeDtypeStruct((L, HEAD_D), _F32)] * 2,
        compiler_params=_params("parallel"),
    )(h, w_in_t, *rope)


def _ret_consts():
    lg = jnp.log(1.0 - jnp.exp(jnp.linspace(math.log(1.0 / 32), math.log(1.0 / 512), N_HEAD))).astype(_F32)
    idx = jnp.arange(CHUNK, dtype=_F32)
    diff = idx[:, None] - idx[None, :]
    decay = jnp.where(diff[None] >= 0, jnp.exp(jnp.maximum(diff, 0.0)[None] * lg[:, None, None]), 0.0)
    zeta = jnp.exp((CHUNK - 1 - idx)[None, :] * lg[:, None])
    xi = jnp.exp((idx + 1.0)[None, :] * lg[:, None])
    gc = jnp.exp(CHUNK * lg)
    wide = lambda t: jnp.broadcast_to(t[:, :, None], (N_HEAD, CHUNK, HEAD_D)).astype(_F32)
    gcw = jnp.broadcast_to(gc[:, None, None], (N_HEAD, SUBLANES, HEAD_D)).astype(_F32)
    return decay.astype(_F32), wide(xi), wide(zeta), gcw


def _head_specs():
    wide = _full_spec((N_HEAD, CHUNK, HEAD_D))
    return [_full_spec((N_HEAD, CHUNK, CHUNK)), wide, wide, _full_spec((N_HEAD, SUBLANES, HEAD_D))]


def _retention_fwd(q, k, v, gate, ggn, consts):
    L = q.shape[0]
    nc = L // CHUNK
    cps = math.gcd(RET_STEP_CHUNKS, nc)
    blk = pl.BlockSpec((cps * CHUNK, RET_W), lambda n: (n, 0))

    def body(q_ref, k_ref, v_ref, gate_ref, ggn_ref, dm_ref, xi_ref, zeta_ref, gc_ref,
             o_ref, y_ref, rp_ref, r_scr):
        @pl.when(pl.program_id(0) == 0)
        def _():
            r_scr[...] = jnp.zeros_like(r_scr)

        for hh in range(N_HEAD):
            cols = slice(hh * HEAD_D, (hh + 1) * HEAD_D)
            state = r_scr[hh]
            for c in range(cps):
                rows = slice(c * CHUNK, (c + 1) * CHUNK)
                qv, kv, vv = q_ref[rows, cols], k_ref[rows, cols], v_ref[rows, cols]
                s = _dot_nt(qv, kv) * dm_ref[hh]
                o = _dot(s.astype(_BF), vv) + _dot(qv, state.astype(_BF)) * xi_ref[hh]
                o_ref[rows, cols] = o
                rp_ref[hh, c] = state
                vz = (vv.astype(_F32) * zeta_ref[hh]).astype(_BF)
                state = gc_ref[hh, 0:1, :] * state + _dot_tn(kv, vz)
                dlt = o - jnp.mean(o, axis=-1, keepdims=True)
                on = dlt * lax.rsqrt(jnp.mean(dlt * dlt, axis=-1, keepdims=True) + NORM_EPS)
                gt = gate_ref[rows, cols]
                y_ref[rows, cols] = (gt * _sigmoid(gt) * (on * ggn_ref[:, cols])).astype(_BF)
            r_scr[hh] = state

    return pl.pallas_call(
        body, name="retention_fwd", grid=(nc // cps,),
        in_specs=[blk, blk, blk, blk, _full_spec((1, RET_W))] + _head_specs(),
        out_specs=[blk, blk, pl.BlockSpec((N_HEAD, cps, HEAD_D, HEAD_D), lambda n: (0, n, 0, 0))],
        out_shape=[jax.ShapeDtypeStruct((L, RET_W), _F32), jax.ShapeDtypeStruct((L, RET_W), _BF),
                   jax.ShapeDtypeStruct((N_HEAD, nc, HEAD_D, HEAD_D), _F32)],
        scratch_shapes=[pltpu.VMEM((N_HEAD, HEAD_D, HEAD_D), _F32)],
        compiler_params=_params("arbitrary"),
    )(q, k, v, gate, ggn, *consts)


def _rows_to_segments(dst_scr, src_ref, seg):
    for g in range(dst_scr.shape[0]):
        for j in range(SUBLANES):
            dst_scr[g, pl.ds(j, seg, stride=SUBLANES), :] = src_ref[pl.ds(j * seg, seg), g * LANES:(g + 1) * LANES]


def _segments_to_rows(dst_ref, src_scr, seg):
    for g in range(src_scr.shape[0]):
        for j in range(SUBLANES):
            dst_ref[pl.ds(j * seg, seg), g * LANES:(g + 1) * LANES] = src_scr[g, pl.ds(j, seg, stride=SUBLANES), :]


def _scan_segments(x_ref, tab_ref, pw_ref, carry_ref, seg, reverse, entry_ref=None, fwd_ref=None, fwd_entry_ref=None,
                   da_ref=None):
    G = x_ref.shape[0]
    W = KB_STATES
    re, im = pl.ds(0, W), pl.ds(W, W)
    row_id = lax.broadcasted_iota(jnp.int32, (SUBLANES, W), 0)
    edge_in = (row_id == SUBLANES - 1) if reverse else (row_id == 0)
    edge_out = 0 if reverse else SUBLANES - 1
    a_tab = [(tab_ref[g, 0], tab_ref[g, 1]) for g in range(G)]

    def local(i, st):
        r = (seg - 1 - i) if reverse else i
        out = []
        for g in range(G):
            (ar, ai), (sr, si) = a_tab[g], st[g]
            nr = ar * sr - ai * si + x_ref[g, r, :, re]
            ni = ar * si + ai * sr + x_ref[g, r, :, im]
            x_ref[g, r, :, re] = nr
            x_ref[g, r, :, im] = ni
            out.append((nr, ni))
        return tuple(out)

    zero = jnp.zeros((SUBLANES, W), _F32)
    ends = lax.fori_loop(0, seg, local, tuple((zero, zero) for _ in range(G)), unroll=SCAN_UNROLL)

    entry = []
    shift = (SUBLANES - 1) if reverse else 1
    for g in range(G):
        er, ei = ends[g]
        fr = jnp.where(edge_in, carry_ref[g, :, re], pltpu.roll(er, shift, 0))
        fi = jnp.where(edge_in, carry_ref[g, :, im], pltpu.roll(ei, shift, 0))
        for j, dist in enumerate((1, 2, 4)):
            pr, pi = tab_ref[g, 2 + 2 * j], tab_ref[g, 3 + 2 * j]
            sh = (SUBLANES - dist) if reverse else dist
            sr, si = pltpu.roll(fr, sh, 0), pltpu.roll(fi, sh, 0)
            fr, fi = fr + pr * sr - pi * si, fi + pr * si + pi * sr
        br, bi = tab_ref[g, 8], tab_ref[g, 9]
        outr = br * fr - bi * fi + er
        outi = br * fi + bi * fr + ei
        carry_ref[g, :, re] = jnp.broadcast_to(outr[edge_out:edge_out + 1, :], (SUBLANES, W))
        carry_ref[g, :, im] = jnp.broadcast_to(outi[edge_out:edge_out + 1, :], (SUBLANES, W))
        entry.append((fr, fi))
        if entry_ref is not None:
            entry_ref[g, :, re] = fr
            entry_ref[g, :, im] = fi

    add_da = da_ref is not None

    def fix(r, st, first=False):
        out = []
        for g in range(G):
            fr, fi = entry[g]
            pwr, pwi = pw_ref[g, r, :, re], pw_ref[g, r, :, im]
            xr = x_ref[g, r, :, re] + (pwr * fr - pwi * fi)
            xi = x_ref[g, r, :, im] + (pwr * fi + pwi * fr)
            x_ref[g, r, :, re] = xr
            x_ref[g, r, :, im] = xi
            if add_da:
                prev = fwd_entry_ref.at[g] if first else fwd_ref.at[g, r - 1]
                xpr, xpi = prev[:, re], prev[:, im]
                out.append((st[g][0] + (xr * xpr + xi * xpi), st[g][1] + (xi * xpr - xr * xpi)))
            else:
                out.append(st[g])
        return tuple(out)

    if add_da:
        st = fix(0, tuple((zero, zero) for _ in range(G)), first=True)
        st = lax.fori_loop(1, seg, fix, st, unroll=SCAN_UNROLL)
        for g in range(G):
            da_ref[g, :, re] += st[g][0]
            da_ref[g, :, im] += st[g][1]
    else:
        lax.fori_loop(0, seg, fix, tuple((zero[0:1, 0:LANES],) for _ in range(G)), unroll=FIX_UNROLL)


def _s5_specs(seg, time=lambda t: t):
    G = KB_PER_STEP
    return dict(
        x=pl.BlockSpec((G, seg, SUBLANES, 2 * KB_STATES), lambda kb, t: (kb, time(t), 0, 0)),
        ent=pl.BlockSpec((G, 1, SUBLANES, 2 * KB_STATES), lambda kb, t: (kb, time(t), 0, 0)),
        b=pl.BlockSpec((G, LANES, 2 * KB_STATES), lambda kb, t: (kb, 0, 0)),
        c=pl.BlockSpec((G, 2 * KB_STATES, LANES), lambda kb, t: (kb, 0, 0)),
        tab=pl.BlockSpec((G, 10, SUBLANES, KB_STATES), lambda kb, t: (kb, 0, 0, 0)),
        pw=pl.BlockSpec((G, seg, 1, 2 * KB_STATES), lambda kb, t: (kb, 0, 0, 0)),
        d=pl.BlockSpec((1, G * LANES), lambda kb, t: (0, kb)),
    )


def _s5_fwd(u, bmat, cmat, tab_f, pw_f, d_skip, tb):
    L = u.shape[0]
    nt = L // tb
    seg = tb // SUBLANES
    G = KB_PER_STEP
    ucol = pl.BlockSpec((tb, G * LANES), lambda kb, t: (t, kb))
    sp = _s5_specs(seg)

    def body(u_ref, b_ref, c_ref, tab_ref, pw_ref, d_ref, s_ref, x_ref, ent_ref, up_scr, y_scr, carry_scr):
        @pl.when(pl.program_id(1) == 0)
        def _():
            carry_scr[...] = jnp.zeros_like(carry_scr)

        _rows_to_segments(up_scr, u_ref, seg)
        for g in range(G):
            x_ref[g] = _dot(up_scr[g].astype(_BF), b_ref[g]).reshape(seg, SUBLANES, 2 * KB_STATES)
        _scan_segments(x_ref, tab_ref, pw_ref, carry_scr, seg, reverse=False, entry_ref=ent_ref.at[:, 0])
        for g in range(G):
            y = _dot(x_ref[g].reshape(tb, 2 * KB_STATES).astype(_BF), c_ref[g])
            y_scr[g] = y + d_ref[:, g * LANES:(g + 1) * LANES] * up_scr[g]
        _segments_to_rows(s_ref, y_scr, seg)

    return pl.pallas_call(
        body, name="s5_fwd", grid=(N_KB // G, nt),
        in_specs=[ucol, sp["b"], sp["c"], sp["tab"], sp["pw"], sp["d"]],
        out_specs=[ucol, sp["x"], sp["ent"]],
        out_shape=[jax.ShapeDtypeStruct((L, SSM_W), _F32),
                   jax.ShapeDtypeStruct((N_KB, L // SUBLANES, SUBLANES, 2 * KB_STATES), _F32),
                   jax.ShapeDtypeStruct((N_KB, nt, SUBLANES, 2 * KB_STATES), _F32)],
        scratch_shapes=[pltpu.VMEM((G, tb, LANES), _F32)] * 2 + [pltpu.VMEM((G, SUBLANES, 2 * KB_STATES), _F32)],
        compiler_params=_params("parallel", "arbitrary"),
    )(u, bmat, cmat, tab_f, pw_f, d_skip)


def _mixout_fwd(s, y_ret, x, w_glu, w_out, g2, tm):
    L = s.shape[0]

    def body(s_ref, yr_ref, x_ref, wg_ref, wo_ref, g_ref, ys_ref, glu_ref, cat_ref, mix_ref, x2_ref):
        for rows in _row_chunks(tm):
            ys = _gelu(s_ref[rows, :]).astype(_BF)
            ys_ref[rows, :] = ys
            glu = _dot(ys, wg_ref[...])
            glu_ref[rows, :] = glu
            cat_ref[rows, :RET_W] = yr_ref[rows, :]
            cat_ref[rows, RET_W:] = (glu[:, :SSM_W] * _sigmoid(glu[:, SSM_W:])).astype(_BF)
            mix = _dot(cat_ref[rows, :], wo_ref[...])
            mix_ref[rows, :] = mix
            x2_ref[rows, :] = x_ref[rows, :] + mix * _rms_r(mix) * g_ref[...]

    return pl.pallas_call(
        body, name="mixout_fwd", grid=(L // tm,),
        in_specs=[_row_spec(tm, SSM_W), _row_spec(tm, RET_W), _row_spec(tm, D_MODEL),
                  _weight_spec((SSM_W, 2 * SSM_W)), _weight_spec((D_MODEL, D_MODEL)), _full_spec((1, D_MODEL))],
        out_specs=[_row_spec(tm, SSM_W), _row_spec(tm, 2 * SSM_W), _row_spec(tm, D_MODEL),
                   _row_spec(tm, D_MODEL), _row_spec(tm, D_MODEL)],
        out_shape=[jax.ShapeDtypeStruct((L, SSM_W), _BF), jax.ShapeDtypeStruct((L, 2 * SSM_W), _F32),
                   jax.ShapeDtypeStruct((L, D_MODEL), _BF), jax.ShapeDtypeStruct((L, D_MODEL), _F32),
                   jax.ShapeDtypeStruct((L, D_MODEL), _F32)],
        compiler_params=_params("parallel"),
    )(s, y_ret, x, w_glu, w_out, g2)


FF1_COLS = D_FF // N_DEV


def _ff1_fwd(x2, g3, w1, tm):
    L = x2.shape[0]

    def body(x_ref, g_ref, w_ref, h_ref, f_ref):
        xv = x_ref[...]
        h = (xv * _rms_r(xv) * g_ref[...]).astype(_BF)
        h_ref[...] = h
        for j in range(N_DEV):
            f_ref[:, j * FF1_COLS:(j + 1) * FF1_COLS] = _dot(h, w_ref[j])

    return pl.pallas_call(
        body, name="ff1_fwd", grid=(L // tm,),
        in_specs=[_row_spec(tm, D_MODEL), _full_spec((1, D_MODEL)), _weight_spec((N_DEV, D_MODEL, FF1_COLS))],
        out_specs=[_row_spec(tm, D_MODEL), _row_spec(tm, D_FF)],
        out_shape=[jax.ShapeDtypeStruct((L, D_MODEL), _BF), jax.ShapeDtypeStruct((L, D_FF), _F32)],
        compiler_params=_params("parallel"),
    )(x2, g3, w1)


def _ff2_loss(f1, x2, tgt, g4, w2, tm):
    L = f1.shape[0]

    def body(f_ref, x_ref, t_ref, g_ref, w_ref, dy_ref, dm_ref, dg_ref, ls_ref):
        @pl.when(pl.program_id(0) == 0)
        def _():
            dg_ref[...] = jnp.zeros_like(dg_ref)
            ls_ref[...] = jnp.zeros_like(ls_ref)

        g = g_ref[...]
        for rows in _row_chunks(tm):
            rl = jnp.maximum(f_ref[rows, :], 0.0)
            m = _dot((rl * rl).astype(_BF), w_ref[...])
            y = x_ref[rows, :] + m * _rms_r(m) * g
            err = y - t_ref[rows, :]
            ls_ref[...] += jnp.sum(err * err, axis=0, keepdims=True)
            dy = err * (1.0 / D_MODEL)
            dy_ref[rows, :] = dy
            dm, dgr = _rms_bwd(m, g, dy)
            dm_ref[rows, :] = dm.astype(_BF)
            dg_ref[...] += jnp.sum(dgr, axis=0, keepdims=True)

    return pl.pallas_call(
        body, name="ff2_loss", grid=(L // tm,),
        in_specs=[_row_spec(tm, D_FF), _row_spec(tm, D_MODEL), _row_spec(tm, D_MODEL),
                  _full_spec((1, D_MODEL)), _weight_spec((D_FF, D_MODEL))],
        out_specs=[_row_spec(tm, D_MODEL), _row_spec(tm, D_MODEL), _full_spec((1, D_MODEL)), _full_spec((1, D_MODEL))],
        out_shape=[jax.ShapeDtypeStruct((L, D_MODEL), _F32), jax.ShapeDtypeStruct((L, D_MODEL), _BF),
                   jax.ShapeDtypeStruct((1, D_MODEL), _F32), jax.ShapeDtypeStruct((1, D_MODEL), _F32)],
        compiler_params=_params("arbitrary"),
    )(f1, x2, tgt, g4, w2)


def _ff2_bwd(dm, f1, w2, tm, tn):
    L = dm.shape[0]
    last = L // tm - 1

    def body(dm_ref, f_ref, w_ref, df_ref, dw_ref, acc):
        @pl.when(pl.program_id(1) == 0)
        def _():
            acc[...] = jnp.zeros_like(acc)

        dmv = dm_ref[...]
        rl = jnp.maximum(f_ref[...], 0.0)
        df_ref[...] = (_dot_nt(dmv, w_ref[...]) * (2.0 * rl)).astype(_BF)
        acc[...] += _dot_tn((rl * rl).astype(_BF), dmv)

        @pl.when(pl.program_id(1) == last)
        def _():
            dw_ref[...] = acc[...].astype(_BF)

    return pl.pallas_call(
        body, name="ff2_bwd", grid=(D_FF // tn, L // tm),
        in_specs=[pl.BlockSpec((tm, D_MODEL), lambda j, i: (i, 0)), pl.BlockSpec((tm, tn), lambda j, i: (i, j)),
                  pl.BlockSpec((tn, D_MODEL), lambda j, i: (j, 0))],
        out_specs=[pl.BlockSpec((tm, tn), lambda j, i: (i, j)), pl.BlockSpec((tn, D_MODEL), lambda j, i: (j, 0))],
        out_shape=[jax.ShapeDtypeStruct((L, D_FF), _BF), jax.ShapeDtypeStruct((D_FF, D_MODEL), _BF)],
        scratch_shapes=[pltpu.VMEM((tn, D_MODEL), _F32)],
        compiler_params=_params("parallel", "arbitrary"),
    )(dm, f1, w2)


def _ff1_bwd(df1, w1, x2, mix, dy, g3, g2, tm):
    L = df1.shape[0]

    def body(df_ref, w_ref, x2_ref, mix_ref, dy_ref, g3_ref, g2_ref, dx2_ref, dmix_ref, dg3_ref, dg2_ref):
        @pl.when(pl.program_id(0) == 0)
        def _():
            dg3_ref[...] = jnp.zeros_like(dg3_ref)
            dg2_ref[...] = jnp.zeros_like(dg2_ref)

        for rows in _row_chunks(tm):
            dh = _dot_nt(df_ref[rows, 0:FF1_COLS], w_ref[0])
            for j in range(1, N_DEV):
                dh = dh + _dot_nt(df_ref[rows, j * FF1_COLS:(j + 1) * FF1_COLS], w_ref[j])
            dz, dgr = _rms_bwd(x2_ref[rows, :], g3_ref[...], dh)
            dg3_ref[...] += jnp.sum(dgr, axis=0, keepdims=True)
            dx2 = dy_ref[rows, :] + dz
            dx2_ref[rows, :] = dx2
            dmx, dgr2 = _rms_bwd(mix_ref[rows, :], g2_ref[...], dx2)
            dg2_ref[...] += jnp.sum(dgr2, axis=0, keepdims=True)
            dmix_ref[rows, :] = dmx.astype(_BF)

    vec = _full_spec((1, D_MODEL))
    return pl.pallas_call(
        body, name="ff1_bwd", grid=(L // tm,),
        in_specs=[_row_spec(tm, D_FF), _weight_spec((N_DEV, D_MODEL, FF1_COLS)), _row_spec(tm, D_MODEL),
                  _row_spec(tm, D_MODEL), _row_spec(tm, D_MODEL), vec, vec],
        out_specs=[_row_spec(tm, D_MODEL), _row_spec(tm, D_MODEL), vec, vec],
        out_shape=[jax.ShapeDtypeStruct((L, D_MODEL), _F32), jax.ShapeDtypeStruct((L, D_MODEL), _BF),
                   jax.ShapeDtypeStruct((1, D_MODEL), _F32), jax.ShapeDtypeStruct((1, D_MODEL), _F32)],
        compiler_params=_params("arbitrary"),
    )(df1, w1, x2, mix, dy, g3, g2)


def _matmul_tn(a, b, tm, tn, name, slots=False):
    L, K = a.shape
    N = b.shape[1]
    last = L // tm - 1

    def body(a_ref, b_ref, o_ref, acc):
        @pl.when(pl.program_id(1) == 0)
        def _():
            acc[...] = jnp.zeros_like(acc)

        acc[...] += _dot_tn(a_ref[...].astype(_BF), b_ref[...].astype(_BF))

        @pl.when(pl.program_id(1) == last)
        def _():
            if slots:
                o_ref[0] = acc[...].astype(_BF)
            else:
                o_ref[...] = acc[...].astype(_BF)

    if slots:
        out_spec = pl.BlockSpec((1, K, tn), lambda j, i: (j, 0, 0))
        out_shape = jax.ShapeDtypeStruct((N // tn, K, tn), _BF)
    else:
        out_spec = pl.BlockSpec((K, tn), lambda j, i: (0, j))
        out_shape = jax.ShapeDtypeStruct((K, N), _BF)
    return pl.pallas_call(
        body, name=name, grid=(N // tn, L // tm),
        in_specs=[pl.BlockSpec((tm, K), lambda j, i: (i, 0)), pl.BlockSpec((tm, tn), lambda j, i: (i, j))],
        out_specs=out_spec, out_shape=out_shape,
        scratch_shapes=[pltpu.VMEM((K, tn), _F32)],
        compiler_params=_params("parallel", "arbitrary"),
    )(a, b)


def _dw_in_t(pieces, h, tk):
    L = h.shape[0]
    last = L // tk - 1

    def body(p0, p1, p2, p3, p4, h_ref, o_ref, acc):
        @pl.when(pl.program_id(0) == 0)
        def _():
            acc[...] = jnp.zeros_like(acc)

        hv = h_ref[...]
        for j, p in enumerate((p0, p1, p2, p3, p4)):
            acc[j * RET_W:(j + 1) * RET_W, :] += _dot_tn(p[...].astype(_BF), hv)

        @pl.when(pl.program_id(0) == last)
        def _():
            o_ref[...] = acc[...].astype(_BF)

    return pl.pallas_call(
        body, name="dw_in", grid=(L // tk,),
        in_specs=[_row_spec(tk, RET_W)] * 5 + [_row_spec(tk, D_MODEL)],
        out_specs=_full_spec((IN_COLS, D_MODEL)), out_shape=jax.ShapeDtypeStruct((IN_COLS, D_MODEL), _BF),
        scratch_shapes=[pltpu.VMEM((IN_COLS, D_MODEL), _F32)],
        compiler_params=_params("arbitrary"),
    )(*pieces, h)


def _mixout_bwd(dmix, w_out, w_glu, glu, s, o, gate, ggn, tm, after=()):
    L = dmix.shape[0]

    def body(dmix_ref, wo_ref, wg_ref, glu_ref, s_ref, o_ref, gate_ref, ggn_ref,
             dglu_ref, ds_ref, dgate_ref, do_ref, dggn_ref):
        @pl.when(pl.program_id(0) == 0)
        def _():
            dggn_ref[...] = jnp.zeros_like(dggn_ref)

        dcat = _dot_nt(dmix_ref[...], wo_ref[...])
        dy_ret, dy_ssm = dcat[:, :RET_W], dcat[:, RET_W:]
        glu = glu_ref[...]
        ga, sg = glu[:, :SSM_W], _sigmoid(glu[:, SSM_W:])
        dga = (dy_ssm * sg).astype(_BF)
        dgb = (dy_ssm * ga * sg * (1.0 - sg)).astype(_BF)
        dglu_ref[:, :SSM_W] = dga
        dglu_ref[:, SSM_W:] = dgb
        dys = _dot_nt(dga, wg_ref[:, :SSM_W]) + _dot_nt(dgb, wg_ref[:, SSM_W:])
        ds_ref[...] = dys * _gelu_grad(s_ref[...])
        gt = gate_ref[...]
        sgt = _sigmoid(gt)
        ggn = ggn_ref[...]
        for hh in range(N_HEAD):
            cols = slice(hh * HEAD_D, (hh + 1) * HEAD_D)
            ov = o_ref[:, cols]
            dlt = ov - jnp.mean(ov, axis=-1, keepdims=True)
            rstd = lax.rsqrt(jnp.mean(dlt * dlt, axis=-1, keepdims=True) + NORM_EPS)
            on = dlt * rstd
            dyr = dy_ret[:, cols] * (gt[:, cols] * sgt[:, cols])
            dgate_ref[:, cols] = dy_ret[:, cols] * (on * ggn[:, cols]) * (sgt[:, cols] * (1.0 + gt[:, cols] * (1.0 - sgt[:, cols])))
            dggn_ref[:, cols] += jnp.sum(dyr * on, axis=0, keepdims=True)
            don = dyr * ggn[:, cols]
            do = rstd * (don - jnp.mean(don, axis=-1, keepdims=True) - on * jnp.mean(don * on, axis=-1, keepdims=True))
            do_ref[:, cols] = do.astype(_BF)

    body, in_specs, operands = _ordered(
        body, [_row_spec(tm, D_MODEL), _weight_spec((D_MODEL, D_MODEL)), _weight_spec((SSM_W, 2 * SSM_W)),
               _row_spec(tm, 2 * SSM_W), _row_spec(tm, SSM_W), _row_spec(tm, RET_W), _row_spec(tm, RET_W),
               _full_spec((1, RET_W))], (dmix, w_out, w_glu, glu, s, o, gate, ggn), after)
    return pl.pallas_call(
        body, name="mixout_bwd", grid=(L // tm,),
        in_specs=in_specs,
        out_specs=[_row_spec(tm, 2 * SSM_W), _row_spec(tm, SSM_W), _row_spec(tm, RET_W), _row_spec(tm, RET_W),
                   _full_spec((1, RET_W))],
        out_shape=[jax.ShapeDtypeStruct((L, 2 * SSM_W), _BF), jax.ShapeDtypeStruct((L, SSM_W), _F32),
                   jax.ShapeDtypeStruct((L, RET_W), _F32), jax.ShapeDtypeStruct((L, RET_W), _BF),
                   jax.ShapeDtypeStruct((1, RET_W), _F32)],
        compiler_params=_params("arbitrary"),
    )(*operands)


def _s5_bwd(u, ds, xs, ent, bmat, cmat, tab_r, pw_r, d_skip, tb, after=()):
    L = u.shape[0]
    nt = L // tb
    seg = tb // SUBLANES
    G = KB_PER_STEP
    rcol = pl.BlockSpec((tb, G * LANES), lambda kb, t: (nt - 1 - t, kb))
    sp = _s5_specs(seg, time=lambda t: nt - 1 - t)
    aspec = pl.BlockSpec((G, SUBLANES, 2 * KB_STATES), lambda kb, t: (kb, 0, 0))

    def body(u_ref, ds_ref, x_ref, ent_ref, b_ref, c_ref, tr_ref, pr_ref, d_ref,
             du_ref, db_ref, dc_ref, da_ref, dd_ref, up_scr, dp_scr, g_scr, lc_scr):
        @pl.when(pl.program_id(1) == 0)
        def _():
            lc_scr[...] = jnp.zeros_like(lc_scr)
            db_ref[...] = jnp.zeros_like(db_ref)
            dc_ref[...] = jnp.zeros_like(dc_ref)
            da_ref[...] = jnp.zeros_like(da_ref)
            dd_ref[...] = jnp.zeros_like(dd_ref)

        _rows_to_segments(up_scr, u_ref, seg)
        _rows_to_segments(dp_scr, ds_ref, seg)
        for g in range(G):
            g_scr[g] = _dot_nt(dp_scr[g].astype(_BF), c_ref[g]).reshape(seg, SUBLANES, 2 * KB_STATES)
        _scan_segments(g_scr, tr_ref, pr_ref, lc_scr, seg, reverse=True, fwd_ref=x_ref, fwd_entry_ref=ent_ref.at[:, 0],
                       da_ref=da_ref)
        for g in range(G):
            cols = slice(g * LANES, (g + 1) * LANES)
            uv, dsv = up_scr[g], dp_scr[g]
            ub, dsb = uv.astype(_BF), dsv.astype(_BF)
            lamb = g_scr[g].reshape(tb, 2 * KB_STATES).astype(_BF)
            db_ref[g] += _dot_tn(ub, lamb)
            dc_ref[g] += _dot_tn(dsb, x_ref[g].reshape(tb, 2 * KB_STATES).astype(_BF))
            dd_ref[:, cols] += jnp.sum(dsv * uv, axis=0, keepdims=True)
            up_scr[g] = _dot_nt(lamb, b_ref[g]) + d_ref[:, cols] * dsv
        _segments_to_rows(du_ref, up_scr, seg)

    body, in_specs, operands = _ordered(
        body, [rcol, rcol, sp["x"], sp["ent"], sp["b"], sp["c"], sp["tab"], sp["pw"], sp["d"]],
        (u, ds, xs, ent, bmat, cmat, tab_r, pw_r, d_skip), after)
    return pl.pallas_call(
        body, name="s5_bwd", grid=(N_KB // G, nt),
        in_specs=in_specs,
        out_specs=[rcol, sp["b"], sp["b"], aspec, sp["d"]],
        out_shape=[jax.ShapeDtypeStruct((L, SSM_W), _F32),
                   jax.ShapeDtypeStruct((N_KB, LANES, 2 * KB_STATES), _F32),
                   jax.ShapeDtypeStruct((N_KB, LANES, 2 * KB_STATES), _F32),
                   jax.ShapeDtypeStruct((N_KB, SUBLANES, 2 * KB_STATES), _F32),
                   jax.ShapeDtypeStruct((1, SSM_W), _F32)],
        scratch_shapes=[pltpu.VMEM((G, tb, LANES), _F32)] * 2
        + [pltpu.VMEM((G, seg, SUBLANES, 2 * KB_STATES), _F32), pltpu.VMEM((G, SUBLANES, 2 * KB_STATES), _F32)],
        compiler_params=_params("parallel", "arbitrary"),
    )(*operands)


def _retention_bwd(q, k, v, do, r_prev, consts, cosf, sinf, after=()):
    L = q.shape[0]
    nc = L // CHUNK
    cps = math.gcd(RET_STEP_CHUNKS, nc)
    nb = nc // cps
    blk = pl.BlockSpec((cps * CHUNK, RET_W), lambda n: (nb - 1 - n, 0))
    rope_blk = pl.BlockSpec((cps * CHUNK, HEAD_D), lambda n: (nb - 1 - n, 0))

    def body(q_ref, k_ref, v_ref, do_ref, rp_ref, dm_ref, xi_ref, zeta_ref, gc_ref, cos_ref, sin_ref,
             dq_ref, dk_ref, dv_ref, g_scr):
        @pl.when(pl.program_id(0) == 0)
        def _():
            g_scr[...] = jnp.zeros_like(g_scr)

        for hh in range(N_HEAD):
            cols = slice(hh * HEAD_D, (hh + 1) * HEAD_D)
            dm, zeta = dm_ref[hh], zeta_ref[hh]
            gst = g_scr[hh]
            for c in reversed(range(cps)):
                rows = slice(c * CHUNK, (c + 1) * CHUNK)
                qv, kv, vv, dov = q_ref[rows, cols], k_ref[rows, cols], v_ref[rows, cols], do_ref[rows, cols]
                rb = rp_ref[hh, c].astype(_BF)
                gb = gst.astype(_BF)
                sb = (_dot_nt(qv, kv) * dm).astype(_BF)
                dab = (_dot_nt(dov, vv) * dm).astype(_BF)
                dox = (dov.astype(_F32) * xi_ref[hh]).astype(_BF)
                vz = (vv.astype(_F32) * zeta).astype(_BF)
                dq = _dot(dab, kv) + _dot_nt(dox, rb)
                dk = _dot_tn(dab, qv) + _dot_nt(vz, gb)
                dv = _dot_tn(sb, dov) + _dot(kv, gb) * zeta
                gst = gc_ref[hh, 0:1, :] * gst + _dot_tn(qv, dox)
                cs, sn = cos_ref[rows, :], sin_ref[rows, :]
                dq_ref[rows, cols] = _rope_t(dq, cs, sn).astype(_BF)
                dk_ref[rows, cols] = (_rope_t(dk, cs, sn) * (HEAD_D ** -0.5)).astype(_BF)
                dv_ref[rows, cols] = dv.astype(_BF)
            g_scr[hh] = gst

    body, in_specs, operands = _ordered(
        body, [blk, blk, blk, blk, pl.BlockSpec((N_HEAD, cps, HEAD_D, HEAD_D), lambda n: (0, nb - 1 - n, 0, 0))]
        + _head_specs() + [rope_blk, rope_blk], (q, k, v, do, r_prev, *consts, cosf, sinf), after)
    return pl.pallas_call(
        body, name="retention_bwd", grid=(nb,),
        in_specs=in_specs,
        out_specs=[blk, blk, blk],
        out_shape=[jax.ShapeDtypeStruct((L, RET_W), _BF)] * 3,
        scratch_shapes=[pltpu.VMEM((N_HEAD, HEAD_D, HEAD_D), _F32)],
        compiler_params=_params("arbitrary"),
    )(*operands)


def _inproj_bwd(pieces, w_in_t, x, dx2, g1, tm, after=()):
    L = x.shape[0]

    def body(p0, p1, p2, p3, p4, w_ref, x_ref, dx2_ref, g_ref, dx_ref, dg_ref):
        @pl.when(pl.program_id(0) == 0)
        def _():
            dg_ref[...] = jnp.zeros_like(dg_ref)

        for rows in _row_chunks(tm):
            dh = None
            for j, p in enumerate((p0, p1, p2, p3, p4)):
                part = _dot(p[rows, :].astype(_BF), w_ref[j * RET_W:(j + 1) * RET_W, :])
                dh = part if dh is None else dh + part
            dz, dgr = _rms_bwd(x_ref[rows, :], g_ref[...], dh)
            dx_ref[rows, :] = dx2_ref[rows, :] + dz
            dg_ref[...] += jnp.sum(dgr, axis=0, keepdims=True)

    body, in_specs, operands = _ordered(
        body, [_row_spec(tm, RET_W)] * 5 + [_weight_spec((IN_COLS, D_MODEL)), _row_spec(tm, D_MODEL),
                                             _row_spec(tm, D_MODEL), _full_spec((1, D_MODEL))],
        (*pieces, w_in_t, x, dx2, g1), after)
    return pl.pallas_call(
        body, name="inproj_bwd", grid=(L // tm,),
        in_specs=in_specs,
        out_specs=[_row_spec(tm, D_MODEL), _full_spec((1, D_MODEL))],
        out_shape=[jax.ShapeDtypeStruct((L, D_MODEL), _F32), jax.ShapeDtypeStruct((1, D_MODEL), _F32)],
        compiler_params=_params("arbitrary"),
    )(*operands)


def _sum_adamw(parts, w, m, v, tr, name):
    _, R, Cc = parts.shape

    def body(p_ref, w_ref, m_ref, v_ref, g_ref, d_ref, nm_ref, nv_ref):
        gv = p_ref[0].astype(_F32)
        for s in range(1, N_DEV):
            gv = gv + p_ref[s].astype(_F32)
        g_ref[...] = gv
        nm = ADAM_B1 * m_ref[...] + (1.0 - ADAM_B1) * gv
        nv = ADAM_B2 * v_ref[...] + (1.0 - ADAM_B2) * (gv * gv)
        m_hat = nm / (1.0 - ADAM_B1 ** ADAM_STEP)
        v_hat = nv / (1.0 - ADAM_B2 ** ADAM_STEP)
        d_ref[...] = -ADAM_LR * (m_hat / (jnp.sqrt(v_hat) + ADAM_EPS) + ADAM_WD * w_ref[...])
        nm_ref[...] = nm
        nv_ref[...] = nv

    spec = _row_spec(tr, Cc)
    return pl.pallas_call(
        body, name=name, grid=(R // tr,),
        in_specs=[pl.BlockSpec((N_DEV, tr, Cc), lambda i: (0, i, 0))] + [spec] * 3, out_specs=[spec] * 4,
        out_shape=[jax.ShapeDtypeStruct((R, Cc), _F32)] * 4,
        compiler_params=_params("parallel"),
    )(parts, w, m, v)


def _my_place():
    return lax.axis_index("x"), lax.axis_index("y"), lax.axis_index("c")


def _all_gather(blocks):
    n = len(blocks)

    def body(*refs):
        x_refs, out_refs, done_ref = refs[:n], refs[n:2 * n], refs[2 * n]
        send_sems, recv_sems, local_sems = refs[2 * n + 1:]
        done_ref[...] = jnp.zeros_like(done_ref)
        x, y, c = _my_place()
        me, sibling = (x, y, c), (x, y, 1 - c)
        chips = [(1 - x, y), (x, 1 - y), (1 - x, 1 - y)]

        def slot(a, px, py, pc):
            return out_refs[a].at[4 * px + 2 * py + pc]

        def copy(a, k, blk, to, own=False):
            return pltpu.make_async_remote_copy(
                src_ref=x_refs[a] if own else slot(a, *blk), dst_ref=slot(a, *blk),
                send_sem=send_sems.at[a, k], recv_sem=recv_sems.at[a, k], device_id=to, device_id_type=MESH)

        mine = [pltpu.make_async_copy(x_refs[a], slot(a, *me), local_sems.at[a]) for a in range(n)]
        for cp in mine:
            cp.start()
        first = []
        for a in range(n):
            first.append(copy(a, 0, me, sibling, own=True))
            first += [copy(a, 1 + j, me, (*chip, c), own=True) for j, chip in enumerate(chips)]
        for cp in first:
            cp.start()
        passed = []
        for j, chip in enumerate(chips):
            for a in range(n):
                copy(a, 1 + j, (*chip, c), me).wait_recv()
                fwd = copy(a, 4 + j, (*chip, c), sibling)
                fwd.start()
                passed.append(fwd)
        for a in range(n):
            copy(a, 0, sibling, me).wait_recv()
            for j, chip in enumerate(chips):
                copy(a, 4 + j, (*chip, 1 - c), me).wait_recv()
        for cp in first + passed:
            cp.wait_send()
        for cp in mine:
            cp.wait()

    any_spec = pl.BlockSpec(memory_space=pl.ANY)
    outs = pl.pallas_call(
        body, name="weights_all_gather",
        in_specs=[any_spec] * n, out_specs=[any_spec] * n + [pl.BlockSpec(memory_space=pltpu.VMEM)],
        out_shape=[jax.ShapeDtypeStruct((N_DEV,) + b.shape, b.dtype) for b in blocks]
        + [jax.ShapeDtypeStruct((SUBLANES, LANES), _F32)],
        scratch_shapes=[pltpu.SemaphoreType.DMA((n, 7)), pltpu.SemaphoreType.DMA((n, 7)), pltpu.SemaphoreType.DMA((n,))],
    )(*blocks)
    return outs[:n], outs[n]


def _exchange(bigs, small):
    n = len(bigs)
    r = small.shape[0]

    def body(*refs):
        in_refs, out_refs = refs[:n + 1], refs[n + 1:2 * n + 2]
        send_sems, recv_sems, local_sems = refs[2 * n + 2:]
        x, y, c = _my_place()
        me = 4 * x + 2 * y + c
        own = [pltpu.make_async_copy(in_refs[a].at[me], out_refs[a].at[me], local_sems.at[a]) for a in range(n)]
        own.append(pltpu.make_async_copy(in_refs[n], out_refs[n].at[me], local_sems.at[n]))
        for cp in own:
            cp.start()
        copies = []
        for kk in range(1, N_DEV):
            px, py, pc = x ^ (kk >> 2), y ^ ((kk >> 1) & 1), c ^ (kk & 1)
            peer = 4 * px + 2 * py + pc
            for a in range(n + 1):
                src = in_refs[a].at[peer] if a < n else in_refs[a]
                copies.append(pltpu.make_async_remote_copy(
                    src_ref=src, dst_ref=out_refs[a].at[me],
                    send_sem=send_sems.at[a, kk - 1], recv_sem=recv_sems.at[a, kk - 1],
                    device_id=(px, py, pc), device_id_type=MESH))
        for cp in copies:
            cp.start()
        for cp in copies:
            cp.wait_recv()
        for cp in copies:
            cp.wait_send()
        for cp in own:
            cp.wait()

    any_spec = pl.BlockSpec(memory_space=pl.ANY)
    outs = pl.pallas_call(
        body, name="grad_exchange",
        in_specs=[any_spec] * (n + 1), out_specs=[any_spec] * (n + 1),
        out_shape=[jax.ShapeDtypeStruct(b.shape, b.dtype) for b in bigs]
        + [jax.ShapeDtypeStruct((N_DEV, r, LANES), small.dtype)],
        scratch_shapes=[pltpu.SemaphoreType.DMA((n + 1, 7)), pltpu.SemaphoreType.DMA((n + 1, 7)),
                        pltpu.SemaphoreType.DMA((n + 1,))],
    )(*bigs, small)
    return outs[:n], outs[n]


HBM_SPEC = pl.BlockSpec(memory_space=pltpu.HBM)
SEM_SPEC = pl.BlockSpec(memory_space=pltpu.SEMAPHORE)
DATAFLOW = pltpu.SideEffectType.DATAFLOW_SIDE_EFFECTING


def _my_index():
    x, y, c = _my_place()
    return 4 * x + 2 * y + c


def _landing(own_block):
    zone = lax.empty((N_DEV,) + own_block.shape, own_block.dtype)
    return lax.dynamic_update_index_in_dim(zone, own_block, _my_index(), 0)


def _split_copies(src_refs, land_refs, send_sems, recv_sems, gather, first=0):
    x, y, c = _my_place()
    me = 4 * x + 2 * y + c
    copies = []
    for a, (src, land) in enumerate(zip(src_refs, land_refs)):
        for kk in range(1, N_DEV):
            px, py, pc = x ^ (kk >> 2), y ^ ((kk >> 1) & 1), c ^ (kk & 1)
            peer = 4 * px + 2 * py + pc
            copies.append(pltpu.make_async_remote_copy(
                src_ref=src if gather else src.at[peer], dst_ref=land.at[me],
                send_sem=send_sems.at[(first + a) * 7 + kk - 1], recv_sem=recv_sems.at[(first + a) * 7 + kk - 1],
                device_id=(px, py, pc), device_id_type=MESH))
    return copies


def _split_start(srcs, lands, gather, name):
    n = len(srcs)

    def body(*refs):
        src_refs, land_refs = refs[:n], refs[n:2 * n]
        send_sems, recv_sems = refs[2 * n], refs[2 * n + 1]
        token = refs[-1]
        for cp in _split_copies(src_refs, land_refs, send_sems, recv_sems, gather):
            cp.start()
        token[...] = jnp.zeros_like(token)

    outs = pl.pallas_call(
        body, name=name,
        out_shape=(pltpu.SemaphoreType.DMA((7 * n,)), pltpu.SemaphoreType.DMA((7 * n,)),
                   *[pltpu.HBM(t.shape, t.dtype) for t in srcs], *[pltpu.HBM(t.shape, t.dtype) for t in lands],
                   jax.ShapeDtypeStruct((SUBLANES, LANES), _F32)),
        in_specs=[HBM_SPEC] * (2 * n),
        out_specs=(SEM_SPEC, SEM_SPEC, *[HBM_SPEC] * (2 * n), pl.BlockSpec(memory_space=pltpu.VMEM)),
        input_output_aliases={i: 2 + i for i in range(2 * n)},
        compiler_params=pltpu.CompilerParams(has_side_effects=DATAFLOW),
    )(*[pltpu.with_memory_space_constraint(t, pltpu.HBM) for t in list(srcs) + list(lands)])
    return outs[0], outs[1], outs[2:2 + n], outs[2 + n:2 + 2 * n], outs[-1]


def _split_wait(send_sems, recv_sems, srcs, lands, after, gather, name, first=0):
    n = len(srcs)

    def body(*refs):
        src_refs, land_refs = refs[:n], refs[n:2 * n]
        send_s, recv_s = refs[2 * n], refs[2 * n + 1]
        for cp in _split_copies(src_refs, land_refs, send_s, recv_s, gather, first):
            cp.wait_send()
            cp.wait_recv()

    outs = pl.pallas_call(
        body, name=name,
        out_shape=tuple(pltpu.HBM(t.shape, t.dtype) for t in list(srcs) + list(lands)),
        in_specs=[HBM_SPEC] * (2 * n) + [SEM_SPEC, SEM_SPEC, pl.BlockSpec(memory_space=pl.ANY)],
        out_specs=tuple([HBM_SPEC] * (2 * n)),
        input_output_aliases={i: i for i in range(2 * n)},
        compiler_params=pltpu.CompilerParams(has_side_effects=DATAFLOW),
    )(*srcs, *lands, send_sems, recv_sems, after)
    return outs[n:]


def _discretize(lam_re, lam_im, log_dt, b_re, b_im):
    lr = jnp.minimum(lam_re, -1e-4)
    li = lam_im
    dt = jnp.exp(log_dt)[:, None]
    er = jnp.exp(lr * dt)
    ar, ai = er * jnp.cos(li * dt), er * jnp.sin(li * dt)
    den = lr * lr + li * li
    cr = ((ar - 1.0) * lr + ai * li) / den
    ci = (ai * lr - (ar - 1.0) * li) / den
    bbr = cr[:, :, None] * b_re - ci[:, :, None] * b_im
    bbi = cr[:, :, None] * b_im + ci[:, :, None] * b_re
    return ar, ai, bbr, bbi


def _cmul(ar, ai, br, bi):
    return ar * br - ai * bi, ar * bi + ai * br


def _cpowers(ar, ai, n):
    pr, pi = ar[None], ai[None]
    while pr.shape[0] < n:
        nr, ni = _cmul(pr, pi, pr[-1][None], pi[-1][None])
        pr, pi = jnp.concatenate([pr, nr]), jnp.concatenate([pi, ni])
    return pr[:n], pi[:n]


def _scan_tables(ar, ai, seg, reverse):
    if reverse:
        ai = -ai
    ar, ai = ar.reshape(N_KB, KB_STATES), ai.reshape(N_KB, KB_STATES)
    pr, pi = _cpowers(ar, ai, seg)
    a1 = (pr[-1], pi[-1])
    a2 = _cmul(*a1, *a1)
    a4 = _cmul(*a2, *a2)
    row = jnp.arange(SUBLANES)[None, :, None]
    wide = lambda t: jnp.broadcast_to(t[:, None, :], (N_KB, SUBLANES, KB_STATES))
    tabs = [wide(ar), wide(ai)]
    for dist, (qr, qi) in ((1, a1), (2, a2), (4, a4)):
        keep = (row < SUBLANES - dist) if reverse else (row >= dist)
        tabs += [jnp.where(keep, wide(qr), 0.0), jnp.where(keep, wide(qi), 0.0)]
    tabs += [wide(a1[0]), wide(a1[1])]
    if reverse:
        pr, pi = pr[::-1], pi[::-1]
    pw = jnp.transpose(jnp.concatenate([pr, pi], axis=-1), (1, 0, 2))[:, :, None, :]
    return jnp.stack(tabs, axis=1).astype(_F32), pw.astype(_F32)


def _block_diag_in(br, bi):
    eye = jnp.eye(GROUPS_PER_KB, dtype=_F32)
    one = lambda t: jnp.einsum("kgpc,gh->kgchp", t.reshape(N_KB, GROUPS_PER_KB, N_STATE, SSM_GC), eye).reshape(
        N_KB, LANES, KB_STATES)
    return jnp.concatenate([one(br), one(bi)], axis=-1)


def _block_diag_in_t(dmat):
    d6 = dmat.reshape(N_KB, GROUPS_PER_KB, SSM_GC, 2, GROUPS_PER_KB, N_STATE)
    eye = jnp.eye(GROUPS_PER_KB, dtype=_F32)
    both = jnp.einsum("kgcrhp,gh->rkgpc", d6, eye).reshape(2, N_GROUP, N_STATE, SSM_GC)
    return both[0], both[1]


def _block_diag_out(c_re, c_im):
    eye = jnp.eye(GROUPS_PER_KB, dtype=_F32)
    one = lambda t: jnp.einsum("kgcp,gh->khpgc", t.reshape(N_KB, GROUPS_PER_KB, SSM_GC, N_STATE), eye).reshape(
        N_KB, KB_STATES, LANES)
    return jnp.concatenate([one(c_re), -one(c_im)], axis=1)


def _block_diag_out_t(dmat_t):
    d6 = dmat_t.reshape(N_KB, GROUPS_PER_KB, SSM_GC, 2, GROUPS_PER_KB, N_STATE)
    eye = jnp.eye(GROUPS_PER_KB, dtype=_F32)
    both = jnp.einsum("kgcrhp,gh->rkgcp", d6, eye).reshape(2, N_GROUP, SSM_GC, N_STATE)
    return both[0], -both[1]


SMALL_NAMES = ("norm_mix_pre", "norm_mix_post", "ret_gn_gain", "ssm_lambda_re", "ssm_lambda_im", "ssm_log_dt",
               "ssm_b_re", "ssm_b_im", "ssm_c_re", "ssm_c_im", "ssm_d", "norm_mlp_pre", "norm_mlp_post")


def _local_grads(x, tgt, small, weights, emit, emit_small, tm, tk, tb, zero=0.0):
    L = x.shape[0]
    g1, g2, ggn = small["norm_mix_pre"], small["norm_mix_post"], small["ret_gn_gain"]
    g3, g4, d_skip = small["norm_mlp_pre"], small["norm_mlp_post"], small["ssm_d"]

    rope = _rope_tables(L)
    consts = _ret_consts()

    disc_in = (small["ssm_lambda_re"][0], small["ssm_lambda_im"][0], small["ssm_log_dt"][0] + zero,
               small["ssm_b_re"][0], small["ssm_b_im"][0])
    (ar, ai, bbr, bbi), disc_vjp = jax.vjp(_discretize, *disc_in)
    bmat = _block_diag_in(bbr, bbi).astype(_BF)
    cmat = _block_diag_out(small["ssm_c_re"][0], small["ssm_c_im"][0]).astype(_BF)
    seg = tb // SUBLANES
    tab_f, pw_f = _scan_tables(ar, ai, seg, False)
    tab_r, pw_r = _scan_tables(ar, ai, seg, True)

    h1 = _prenorm(x, g1, min(4 * tm, L), after=(pw_r,))
    (w_in_t,) = weights("in", h1)
    q, k, v, gate, u, cosf, sinf = _inproj_fwd(h1, w_in_t, rope, tm)
    o, y_ret, r_prev = _retention_fwd(q, k, v, gate, ggn, consts)
    s, xs, ent = _s5_fwd(u, bmat, cmat, tab_f, pw_f, d_skip, tb)
    w_glu, w_out = weights("mix", s)
    ys, glu, cat, mix, x2 = _mixout_fwd(s, y_ret, x, w_glu, w_out, g2, min(2 * tm, L))
    w_ff1, w_ff2 = weights("mlp", x2)
    h3, f1 = _ff1_fwd(x2, g3, w_ff1, tm)
    dy, dm, dg4, sq = _ff2_loss(f1, x2, tgt, g4, w_ff2, min(2 * tm, L))

    df1, dw_ff2 = _ff2_bwd(dm, f1, w_ff2, min(1024, L), 1024)
    dx2, dmix, dg3, dg2 = _ff1_bwd(df1, w_ff1, x2, mix, dy, g3, g2, min(2 * tm, L))
    dw_ff1 = _matmul_tn(h3, df1, tk, FF1_COLS, "dw_ff1", slots=True)
    token = emit({"w_ff1": dw_ff1, "w_ff2": dw_ff2})
    dglu, ds, dgate, do, dggn = _mixout_bwd(dmix, w_out, w_glu, glu, s, o, gate, ggn, tm, after=token)
    dw_out = _matmul_tn(cat, dmix, tk, 1024, "dw_out")
    dw_glu = _matmul_tn(ys, dglu, tk, 1024, "dw_glu")
    token = emit({"w_glu": dw_glu, "w_out": dw_out})
    du, dbmat, dcmat, da8, dd = _s5_bwd(u, ds, xs, ent, bmat, cmat, tab_r, pw_r, d_skip, tb, after=token)

    da = jnp.sum(da8, axis=1)
    dar = da[:, :KB_STATES].reshape(N_GROUP, N_STATE)
    dai = da[:, KB_STATES:].reshape(N_GROUP, N_STATE)
    dbr, dbi = _block_diag_in_t(dbmat)
    dlre, dlim, dldt, dbre, dbim = disc_vjp((dar, dai, dbr, dbi))
    dcre, dcim = _block_diag_out_t(dcmat)
    token = emit_small({
        "norm_mix_post": dg2, "ret_gn_gain": dggn,
        "ssm_lambda_re": dlre[None], "ssm_lambda_im": dlim[None], "ssm_log_dt": dldt[None],
        "ssm_b_re": dbre[None], "ssm_b_im": dbim[None], "ssm_c_re": dcre[None], "ssm_c_im": dcim[None],
        "ssm_d": dd, "norm_mlp_pre": dg3, "norm_mlp_post": dg4,
    }, sq)

    dq, dk, dv = _retention_bwd(q, k, v, do, r_prev, consts, cosf, sinf, after=token)
    pieces = (dq, dk, dv, dgate, du)
    dw_in_t = _dw_in_t(pieces, h1, min(1024, L))
    token = emit({"w_in": dw_in_t})
    gx, dg1 = _inproj_bwd(pieces, w_in_t, x, dx2, g1, min(2 * tm, L), after=token)
    return gx, dg1


BIG_SHAPES = {"w_in": (D_MODEL, IN_COLS // N_DEV), "w_glu": (SSM_W, 2 * SSM_W // N_DEV), "w_out": (D_MODEL // N_DEV, D_MODEL),
              "w_ff1": (D_MODEL, FF1_COLS), "w_ff2": (D_FF // N_DEV, D_MODEL)}
BIG_NAMES = ("w_in", "w_glu", "w_out", "w_ff1", "w_ff2")


def _cols_from_slots(g):
    return jnp.transpose(g, (1, 0, 2)).reshape(g.shape[1], N_DEV * g.shape[2])


def _cols_to_slots(dw):
    r, cols = dw.shape
    return jnp.transpose(dw.reshape(r, N_DEV, cols // N_DEV), (1, 0, 2))


WEIGHT_GROUPS = {"in": ("w_in",), "mix": ("w_glu", "w_out"), "mlp": ("w_ff1", "w_ff2")}


def _weight_from_slots(name, g):
    if name == "w_glu":
        return _cols_from_slots(g)
    if name == "w_ff1":
        return g
    return g.reshape(N_DEV * g.shape[1], g.shape[2])


def _grad_slots(name, dw):
    if name == "w_glu":
        return _cols_to_slots(dw)
    if name == "w_ff1":
        return dw
    if name == "w_in":
        return dw.reshape(N_DEV, BIG_SHAPES[name][1], BIG_SHAPES[name][0])
    return dw.reshape((N_DEV,) + BIG_SHAPES[name])


PIECE_ROWS = 8


VEC_NAMES = tuple(n for n in SMALL_NAMES if n[:6] not in ("ssm_b_", "ssm_c_"))
BC_NAMES = ("ssm_b_re", "ssm_b_im", "ssm_c_re", "ssm_c_im")
BC_ROWS = N_GROUP * SSM_GC


def _bc_view(name, t):
    t = t[0]
    if name.startswith("ssm_b_"):
        t = jnp.swapaxes(t, 1, 2)
    return t.reshape(BC_ROWS, N_STATE)


def _bc_unview(name, t):
    t = t.reshape(N_GROUP, SSM_GC, N_STATE)
    if name.startswith("ssm_b_"):
        t = jnp.swapaxes(t, 1, 2)
    return t[None]


def _pack_bc(vals):
    return jnp.concatenate([_bc_view(n, vals[n]).astype(_F32) for n in BC_NAMES], axis=0)


def _unpack_bc(buf):
    return {n: _bc_unview(n, buf[j * BC_ROWS:(j + 1) * BC_ROWS]) for j, n in enumerate(BC_NAMES)}


def _small_layout(shapes):
    off, rows = {}, 0
    for n in VEC_NAMES:
        off[n] = rows
        rows += -(-math.prod(shapes[n]) // (PIECE_ROWS * LANES)) * PIECE_ROWS
    return off, rows, rows + PIECE_ROWS


def _pack_small(vals, shapes, last=None):
    parts = []
    for n in VEC_NAMES:
        flat = vals[n].reshape(-1).astype(_F32)
        pad = -flat.shape[0] % (PIECE_ROWS * LANES)
        if pad:
            flat = jnp.concatenate([flat, jnp.zeros((pad,), _F32)])
        parts.append(flat.reshape(-1, LANES))
    parts.append(jnp.zeros((PIECE_ROWS, LANES), _F32) if last is None else last)
    return jnp.concatenate(parts, axis=0)


def _unpack_small(buf, shapes):
    off, _, _ = _small_layout(shapes)
    out = {}
    for n in VEC_NAMES:
        size = math.prod(shapes[n])
        rows = -(-size // LANES)
        out[n] = buf[off[n]:off[n] + rows].reshape(-1)[:size].reshape(shapes[n])
    return out


WEIGHT_NAMES = ('norm_mix_pre', 'norm_mix_post', 'w_in', 'ret_gn_gain', 'ssm_lambda_re', 'ssm_lambda_im', 'ssm_log_dt',
                'ssm_b_re', 'ssm_b_im', 'ssm_c_re', 'ssm_c_im', 'ssm_d', 'w_glu', 'w_out', 'norm_mlp_pre',
                'norm_mlp_post', 'w_ff1', 'w_ff2')


def kernel(x, norm_mix_pre, norm_mix_post, w_in, ret_gn_gain, ssm_lambda_re, ssm_lambda_im, ssm_log_dt, ssm_b_re, ssm_b_im, ssm_c_re, ssm_c_im, ssm_d, w_glu, w_out, norm_mlp_pre, norm_mlp_post, w_ff1, w_ff2, loss_target, m_norm_mix_pre, m_norm_mix_post, m_w_in, m_ret_gn_gain, m_ssm_lambda_re, m_ssm_lambda_im, m_ssm_log_dt, m_ssm_b_re, m_ssm_b_im, m_ssm_c_re, m_ssm_c_im, m_ssm_d, m_w_glu, m_w_out, m_norm_mlp_pre, m_norm_mlp_post, m_w_ff1, m_w_ff2, v_norm_mix_pre, v_norm_mix_post, v_w_in, v_ret_gn_gain, v_ssm_lambda_re, v_ssm_lambda_im, v_ssm_log_dt, v_ssm_b_re, v_ssm_b_im, v_ssm_c_re, v_ssm_c_im, v_ssm_d, v_w_glu, v_w_out, v_norm_mlp_pre, v_norm_mlp_post, v_w_ff1, v_w_ff2):
    args = dict(locals())
    w = {n: args[n] for n in WEIGHT_NAMES}
    m = {n: args["m_" + n] for n in WEIGHT_NAMES}
    v = {n: args["v_" + n] for n in WEIGHT_NAMES}
    L = x.shape[1]
    tm = min(256, L)
    tk = min(2048, L)
    tb = min(1024, L)

    order = [n for names in WEIGHT_GROUPS.values() for n in names]
    blocks = [(w[n][0].T if n == "w_in" else w[n][0]).astype(_BF) for n in order]
    gathered = _split_start(blocks, [_landing(b) for b in blocks], True, "weights_start")
    zero = gathered[4][0, 0]

    def weights(group, after):
        names = WEIGHT_GROUPS[group]
        first = order.index(names[0])
        part = slice(first, first + len(names))
        landed = _split_wait(gathered[0], gathered[1], gathered[2][part], gathered[3][part], after, True,
                             "weights_wait_" + group, first=first)
        return [_weight_from_slots(n, g) for n, g in zip(names, landed)]

    in_flight = []

    def emit(dws):
        names = sorted(dws)
        srcs = [_grad_slots(n, dws[n]) for n in names]
        lands = [_landing(lax.dynamic_index_in_dim(t, _my_index(), 0, keepdims=False)) for t in srcs]
        started = _split_start(srcs, lands, False, "grads_start_" + "_".join(names))
        in_flight.append((names, started))
        return (started[4],)

    shapes = {n: w[n].shape for n in SMALL_NAMES}
    first_piece = {SMALL_NAMES[0]: jnp.zeros(shapes[SMALL_NAMES[0]], _F32)}
    small_flight = []

    def emit_small(gs, sq):
        loss_rows = jnp.broadcast_to(0.5 / D_MODEL * jnp.sum(sq), (PIECE_ROWS, LANES)).astype(_F32)
        bufs = [_pack_small({**first_piece, **gs}, shapes, loss_rows), _pack_bc(gs)]
        small_flight.append(_split_start(bufs, [_landing(b) for b in bufs], True, "small_grads_start"))
        return (small_flight[0][4],)

    small_w = {n: w[n] for n in SMALL_NAMES}
    gx, dg1 = _local_grads(x[0], loss_target[0], small_w, weights, emit, emit_small, tm, tk, tb, zero=zero)
    last_buf = dg1.reshape(PIECE_ROWS, LANES)
    last_started = _split_start([last_buf], [_landing(last_buf)], True, "last_grad_start")

    grads, delta, new_m, new_v = {}, {}, {}, {}
    after = last_started[4]
    for names, started in in_flight:
        landed = _split_wait(*started[:4], after, False, "grads_wait_" + "_".join(names))
        for n, parts in zip(names, landed):
            flip = (lambda t: t.T) if n == "w_in" else (lambda t: t)
            res = _sum_adamw(parts, flip(w[n][0]), flip(m[n][0]), flip(v[n][0]), math.gcd(256, parts.shape[1]), "adamw_" + n)
            grads[n], delta[n], new_m[n], new_v[n] = (flip(t)[None] for t in res)
        after = res[1]
    small_parts, bc_parts = _split_wait(*small_flight[0][:4], after, True, "small_grads_wait")
    last_parts = _split_wait(*last_started[:4], small_parts, True, "last_grad_wait")[0]
    small_parts = lax.dynamic_update_slice(small_parts, last_parts, (0, 0, 0))
    res_bc = _sum_adamw(bc_parts, _pack_bc(w), _pack_bc(m), _pack_bc(v), BC_ROWS, "adamw_bc")
    sw, sm, sv = _pack_small(w, shapes), _pack_small(m, shapes), _pack_small(v, shapes)
    res = _sum_adamw(small_parts, sw, sm, sv, sw.shape[0], "adamw_small")
    for dst, buf, buf_bc in zip((grads, delta, new_m, new_v), res, res_bc):
        dst.update(_unpack_small(buf, shapes))
        dst.update(_unpack_bc(buf_bc))
    _, loss_at, _ = _small_layout(shapes)
    loss = res[0][loss_at, 0]

    return (loss, gx[None], *[grads[n] for n in WEIGHT_NAMES], *[delta[n] for n in WEIGHT_NAMES],
            *[new_m[n] for n in WEIGHT_NAMES], *[new_v[n] for n in WEIGHT_NAMES])
```

```python
import math

import jax
import jax.numpy as jnp
from jax import lax
from jax.experimental import pallas as pl
from jax.experimental.pallas import tpu as pltpu

_BF = jnp.bfloat16
_F32 = jnp.float32

D_MODEL = 1024
RET_W = 512
N_HEAD = 4
HEAD_D = 128
CHUNK = 256
ROPE_CHUNK = 128
SSM_W = 512
SSM_GC = 16
N_GROUP = 32
N_STATE = 64
GROUPS_PER_KB = 8
N_KB = 4
KB_STATES = GROUPS_PER_KB * N_STATE
D_FF = 4096
IN_COLS = 2560
NORM_EPS = 1e-6
ROPE_BASE = 10000.0
N_DEV = 8

ADAM_LR = 0.001
ADAM_B1 = 0.9
ADAM_B2 = 0.999
ADAM_EPS = 1e-08
ADAM_WD = 0.01
ADAM_STEP = 10

SUBLANES = 8
LANES = 128
VMEM_LIMIT = 52 * 1024 * 1024
RET_STEP_CHUNKS = 2
KB_PER_STEP = 2
SCAN_UNROLL = True
FIX_UNROLL = 8

MESH = pl.DeviceIdType.MESH


def _params(*sem):
    return pltpu.CompilerParams(dimension_semantics=sem, vmem_limit_bytes=VMEM_LIMIT)


def _dot(a, b):
    return jnp.dot(a, b, preferred_element_type=_F32)


def _dot_nt(a, b):
    return lax.dot_general(a, b, (((1,), (1,)), ((), ())), preferred_element_type=_F32)


def _dot_tn(a, b):
    return lax.dot_general(a, b, (((0,), (0,)), ((), ())), preferred_element_type=_F32)


def _rms_r(z):
    return lax.rsqrt(jnp.mean(z * z, axis=-1, keepdims=True) + NORM_EPS)


def _rms_bwd(z, g, dn):
    r = _rms_r(z)
    t = dn * g
    dz = r * t - z * (r * r * r * jnp.mean(t * z, axis=-1, keepdims=True))
    return dz, dn * z * r


def _rope(t, cs, sn):
    return t * cs + pltpu.roll(t, HEAD_D // 2, 1) * sn


def _rope_t(t, cs, sn):
    return t * cs - pltpu.roll(t, HEAD_D // 2, 1) * sn


def _sigmoid(z):
    return 1.0 / (1.0 + jnp.exp(-z))


_GELU_C = math.sqrt(2.0 / math.pi)


def _gelu(z):
    return 0.5 * z * (1.0 + jnp.tanh(_GELU_C * (z + 0.044715 * z * z * z)))


def _gelu_grad(z):
    th = jnp.tanh(_GELU_C * (z + 0.044715 * z * z * z))
    return 0.5 * (1.0 + th) + 0.5 * z * (1.0 - th * th) * _GELU_C * (1.0 + 3 * 0.044715 * z * z)


ROW_CHUNK = 256


def _row_chunks(tm):
    return [pl.ds(i, min(ROW_CHUNK, tm)) for i in range(0, tm, ROW_CHUNK)]


def _ordered(body, in_specs, operands, after):
    k = len(after)
    if not k:
        return body, list(in_specs), tuple(operands)
    return ((lambda *refs: body(*refs[k:])), [pl.BlockSpec(memory_space=pl.ANY)] * k + list(in_specs),
            tuple(after) + tuple(operands))


def _row_spec(tm, n):
    return pl.BlockSpec((tm, n), lambda i: (i, 0))


def _full_spec(shape):
    nd = len(shape)
    return pl.BlockSpec(shape, lambda *_: (0,) * nd)


def _weight_spec(shape):
    nd = len(shape)
    return pl.BlockSpec(shape, lambda *_: (0,) * nd, pipeline_mode=pl.Buffered(1))


def _rope_tables(L):
    half = HEAD_D // 2
    inv_freq = ROPE_BASE ** (-jnp.arange(half, dtype=_F32) / half)
    twice = lambda t: jnp.concatenate([t, t], axis=-1)
    off = jnp.arange(ROPE_CHUNK, dtype=_F32)[:, None] * inv_freq[None, :]
    start = (ROPE_CHUNK * jnp.arange(L // ROPE_CHUNK, dtype=_F32))[:, None] * inv_freq[None, :]
    return (twice(jnp.cos(off)), twice(jnp.sin(off)),
            twice(jnp.cos(start))[:, None, :], twice(jnp.sin(start))[:, None, :])


def _prenorm(x, g, tm, after=()):
    L = x.shape[0]

    def body(x_ref, g_ref, h_ref):
        xv = x_ref[...]
        h_ref[...] = (xv * _rms_r(xv) * g_ref[...]).astype(_BF)

    body, in_specs, operands = _ordered(body, [_row_spec(tm, D_MODEL), _full_spec((1, D_MODEL))], (x, g), after)
    return pl.pallas_call(
        body, name="prenorm", grid=(L // tm,),
        in_specs=in_specs, out_specs=_row_spec(tm, D_MODEL),
        out_shape=jax.ShapeDtypeStruct((L, D_MODEL), _BF),
        compiler_params=_params("parallel"),
    )(*operands)


def _inproj_fwd(h, w_in_t, rope, tm):
    L = h.shape[0]
    n_chunks = tm // ROPE_CHUNK

    def body(h_ref, w_ref, co_ref, so_ref, cs_ref, ss_ref, q_ref, k_ref, v_ref, gate_ref, u_ref, cos_ref, sin_ref):
        proj = _dot_nt(h_ref[...], w_ref[...])
        lane = lax.broadcasted_iota(jnp.int32, (ROPE_CHUNK, HEAD_D), 1)
        sign = jnp.where(lane < HEAD_D // 2, -1.0, 1.0)
        co, so = co_ref[...], so_ref[...]
        for c in range(n_chunks):
            chunk = pl.program_id(0) * n_chunks + c
            cst, sst = cs_ref[chunk], ss_ref[chunk]
            rows = slice(c * ROPE_CHUNK, (c + 1) * ROPE_CHUNK)
            cs = co * cst - so * sst
            sn = (so * cst + co * sst) * sign
            cos_ref[rows, :] = cs
            sin_ref[rows, :] = sn
            for hh in range(N_HEAD):
                lo = hh * HEAD_D
                q_ref[rows, lo:lo + HEAD_D] = _rope(proj[rows, lo:lo + HEAD_D], cs, sn).astype(_BF)
                kh = _rope(proj[rows, RET_W + lo:RET_W + lo + HEAD_D], cs, sn) * (HEAD_D ** -0.5)
                k_ref[rows, lo:lo + HEAD_D] = kh.astype(_BF)
        v_ref[...] = proj[:, 2 * RET_W:3 * RET_W].astype(_BF)
        gate_ref[...] = proj[:, 3 * RET_W:4 * RET_W]
        u_ref[...] = proj[:, 4 * RET_W:]

    nc = L // ROPE_CHUNK
    return pl.pallas_call(
        body, name="inproj_fwd", grid=(L // tm,),
        in_specs=[_row_spec(tm, D_MODEL), _weight_spec((IN_COLS, D_MODEL)),
                  _full_spec((ROPE_CHUNK, HEAD_D)), _full_spec((ROPE_CHUNK, HEAD_D)),
                  _full_spec((nc, 1, HEAD_D)), _full_spec((nc, 1, HEAD_D))],
        out_specs=[_row_spec(tm, RET_W)] * 5 + [_row_spec(tm, HEAD_D)] * 2,
        out_shape=[jax.ShapeDtypeStruct((L, RET_W), _BF)] * 3 + [jax.ShapeDtypeStruct((L, RET_W), _F32)] * 2
        + [jax.ShapeDtypeStruct((L, HEAD_D), _F32)] * 2,
        compiler_params=_params("parallel"),
    )(h, w_in_t, *rope)


def _ret_consts():
    lg = jnp.log(1.0 - jnp.exp(jnp.linspace(math.log(1.0 / 32), math.log(1.0 / 512), N_HEAD))).astype(_F32)
    idx = jnp.arange(CHUNK, dtype=_F32)
    diff = idx[:, None] - idx[None, :]
    decay = jnp.where(diff[None] >= 0, jnp.exp(jnp.maximum(diff, 0.0)[None] * lg[:, None, None]), 0.0)
    zeta = jnp.exp((CHUNK - 1 - idx)[None, :] * lg[:, None])
    xi = jnp.exp((idx + 1.0)[None, :] * lg[:, None])
    gc = jnp.exp(CHUNK * lg)
    wide = lambda t: jnp.broadcast_to(t[:, :, None], (N_HEAD, CHUNK, HEAD_D)).astype(_F32)
    gcw = jnp.broadcast_to(gc[:, None, None], (N_HEAD, SUBLANES, HEAD_D)).astype(_F32)
    return decay.astype(_F32), wide(xi), wide(zeta), gcw


def _head_specs():
    wide = _full_spec((N_HEAD, CHUNK, HEAD_D))
    return [_full_spec((N_HEAD, CHUNK, CHUNK)), wide, wide, _full_spec((N_HEAD, SUBLANES, HEAD_D))]


def _retention_fwd(q, k, v, gate, ggn, consts):
    L = q.shape[0]
    nc = L // CHUNK
    cps = math.gcd(RET_STEP_CHUNKS, nc)
    blk = pl.BlockSpec((cps * CHUNK, RET_W), lambda n: (n, 0))

    def body(q_ref, k_ref, v_ref, gate_ref, ggn_ref, dm_ref, xi_ref, zeta_ref, gc_ref,
             o_ref, y_ref, rp_ref, r_scr):
        @pl.when(pl.program_id(0) == 0)
        def _():
            r_scr[...] = jnp.zeros_like(r_scr)

        for hh in range(N_HEAD):
            cols = slice(hh * HEAD_D, (hh + 1) * HEAD_D)
            state = r_scr[hh]
            for c in range(cps):
                rows = slice(c * CHUNK, (c + 1) * CHUNK)
                qv, kv, vv = q_ref[rows, cols], k_ref[rows, cols], v_ref[rows, cols]
                s = _dot_nt(qv, kv) * dm_ref[hh]
                o = _dot(s.astype(_BF), vv) + _dot(qv, state.astype(_BF)) * xi_ref[hh]
                o_ref[rows, cols] = o
                rp_ref[hh, c] = state
                vz = (vv.astype(_F32) * zeta_ref[hh]).astype(_BF)
                state = gc_ref[hh, 0:1, :] * state + _dot_tn(kv, vz)
                dlt = o - jnp.mean(o, axis=-1, keepdims=True)
                on = dlt * lax.rsqrt(jnp.mean(dlt * dlt, axis=-1, keepdims=True) + NORM_EPS)
                gt = gate_ref[rows, cols]
                y_ref[rows, cols] = (gt * _sigmoid(gt) * (on * ggn_ref[:, cols])).astype(_BF)
            r_scr[hh] = state

    return pl.pallas_call(
        body, name="retention_fwd", grid=(nc // cps,),
        in_specs=[blk, blk, blk, blk, _full_spec((1, RET_W))] + _head_specs(),
        out_specs=[blk, blk, pl.BlockSpec((N_HEAD, cps, HEAD_D, HEAD_D), lambda n: (0, n, 0, 0))],
        out_shape=[jax.ShapeDtypeStruct((L, RET_W), _F32), jax.ShapeDtypeStruct((L, RET_W), _BF),
                   jax.ShapeDtypeStruct((N_HEAD, nc, HEAD_D, HEAD_D), _F32)],
        scratch_shapes=[pltpu.VMEM((N_HEAD, HEAD_D, HEAD_D), _F32)],
        compiler_params=_params("arbitrary"),
    )(q, k, v, gate, ggn, *consts)


def _rows_to_segments(dst_scr, src_ref, seg):
    for g in range(dst_scr.shape[0]):
        for j in range(SUBLANES):
            dst_scr[g, pl.ds(j, seg, stride=SUBLANES), :] = src_ref[pl.ds(j * seg, seg), g * LANES:(g + 1) * LANES]


def _segments_to_rows(dst_ref, src_scr, seg):
    for g in range(src_scr.shape[0]):
        for j in range(SUBLANES):
            dst_ref[pl.ds(j * seg, seg), g * LANES:(g + 1) * LANES] = src_scr[g, pl.ds(j, seg, stride=SUBLANES), :]


def _scan_segments(x_ref, tab_ref, pw_ref, carry_ref, seg, reverse, entry_ref=None, fwd_ref=None, fwd_entry_ref=None,
                   da_ref=None):
    G = x_ref.shape[0]
    W = KB_STATES
    re, im = pl.ds(0, W), pl.ds(W, W)
    row_id = lax.broadcasted_iota(jnp.int32, (SUBLANES, W), 0)
    edge_in = (row_id == SUBLANES - 1) if reverse else (row_id == 0)
    edge_out = 0 if reverse else SUBLANES - 1
    a_tab = [(tab_ref[g, 0], tab_ref[g, 1]) for g in range(G)]

    def local(i, st):
        r = (seg - 1 - i) if reverse else i
        out = []
        for g in range(G):
            (ar, ai), (sr, si) = a_tab[g], st[g]
            nr = ar * sr - ai * si + x_ref[g, r, :, re]
            ni = ar * si + ai * sr + x_ref[g, r, :, im]
            x_ref[g, r, :, re] = nr
            x_ref[g, r, :, im] = ni
            out.append((nr, ni))
        return tuple(out)

    zero = jnp.zeros((SUBLANES, W), _F32)
    ends = lax.fori_loop(0, seg, local, tuple((zero, zero) for _ in range(G)), unroll=SCAN_UNROLL)

    entry = []
    shift = (SUBLANES - 1) if reverse else 1
    for g in range(G):
        er, ei = ends[g]
        fr = jnp.where(edge_in, carry_ref[g, :, re], pltpu.roll(er, shift, 0))
        fi = jnp.where(edge_in, carry_ref[g, :, im], pltpu.roll(ei, shift, 0))
        for j, dist in enumerate((1, 2, 4)):
            pr, pi = tab_ref[g, 2 + 2 * j], tab_ref[g, 3 + 2 * j]
            sh = (SUBLANES - dist) if reverse else dist
            sr, si = pltpu.roll(fr, sh, 0), pltpu.roll(fi, sh, 0)
            fr, fi = fr + pr * sr - pi * si, fi + pr * si + pi * sr
        br, bi = tab_ref[g, 8], tab_ref[g, 9]
        outr = br * fr - bi * fi + er
        outi = br * fi + bi * fr + ei
        carry_ref[g, :, re] = jnp.broadcast_to(outr[edge_out:edge_out + 1, :], (SUBLANES, W))
        carry_ref[g, :, im] = jnp.broadcast_to(outi[edge_out:edge_out + 1, :], (SUBLANES, W))
        entry.append((fr, fi))
        if entry_ref is not None:
            entry_ref[g, :, re] = fr
            entry_ref[g, :, im] = fi

    add_da = da_ref is not None

    def fix(r, st, first=False):
        out = []
        for g in range(G):
            fr, fi = entry[g]
            pwr, pwi = pw_ref[g, r, :, re], pw_ref[g, r, :, im]
            xr = x_ref[g, r, :, re] + (pwr * fr - pwi * fi)
            xi = x_ref[g, r, :, im] + (pwr * fi + pwi * fr)
            x_ref[g, r, :, re] = xr
            x_ref[g, r, :, im] = xi
            if add_da:
                prev = fwd_entry_ref.at[g] if first else fwd_ref.at[g, r - 1]
                xpr, xpi = prev[:, re], prev[:, im]
                out.append((st[g][0] + (xr * xpr + xi * xpi), st[g][1] + (xi * xpr - xr * xpi)))
            else:
                out.append(st[g])
        return tuple(out)

    if add_da:
        st = fix(0, tuple((zero, zero) for _ in range(G)), first=True)
        st = lax.fori_loop(1, seg, fix, st, unroll=SCAN_UNROLL)
        for g in range(G):
            da_ref[g, :, re] += st[g][0]
            da_ref[g, :, im] += st[g][1]
    else:
        lax.fori_loop(0, seg, fix, tuple((zero[0:1, 0:LANES],) for _ in range(G)), unroll=FIX_UNROLL)


def _s5_specs(seg, time=lambda t: t):
    G = KB_PER_STEP
    return dict(
        x=pl.BlockSpec((G, seg, SUBLANES, 2 * KB_STATES), lambda kb, t: (kb, time(t), 0, 0)),
        ent=pl.BlockSpec((G, 1, SUBLANES, 2 * KB_STATES), lambda kb, t: (kb, time(t), 0, 0)),
        b=pl.BlockSpec((G, LANES, 2 * KB_STATES), lambda kb, t: (kb, 0, 0)),
        c=pl.BlockSpec((G, 2 * KB_STATES, LANES), lambda kb, t: (kb, 0, 0)),
        tab=pl.BlockSpec((G, 10, SUBLANES, KB_STATES), lambda kb, t: (kb, 0, 0, 0)),
        pw=pl.BlockSpec((G, seg, 1, 2 * KB_STATES), lambda kb, t: (kb, 0, 0, 0)),
        d=pl.BlockSpec((1, G * LANES), lambda kb, t: (0, kb)),
    )


def _s5_fwd(u, bmat, cmat, tab_f, pw_f, d_skip, tb):
    L = u.shape[0]
    nt = L // tb
    seg = tb // SUBLANES
    G = KB_PER_STEP
    ucol = pl.BlockSpec((tb, G * LANES), lambda kb, t: (t, kb))
    sp = _s5_specs(seg)

    def body(u_ref, b_ref, c_ref, tab_ref, pw_ref, d_ref, s_ref, x_ref, ent_ref, up_scr, y_scr, carry_scr):
        @pl.when(pl.program_id(1) == 0)
        def _():
            carry_scr[...] = jnp.zeros_like(carry_scr)

        _rows_to_segments(up_scr, u_ref, seg)
        for g in range(G):
            x_ref[g] = _dot(up_scr[g].astype(_BF), b_ref[g]).reshape(seg, SUBLANES, 2 * KB_STATES)
        _scan_segments(x_ref, tab_ref, pw_ref, carry_scr, seg, reverse=False, entry_ref=ent_ref.at[:, 0])
        for g in range(G):
            y = _dot(x_ref[g].reshape(tb, 2 * KB_STATES).astype(_BF), c_ref[g])
            y_scr[g] = y + d_ref[:, g * LANES:(g + 1) * LANES] * up_scr[g]
        _segments_to_rows(s_ref, y_scr, seg)

    return pl.pallas_call(
        body, name="s5_fwd", grid=(N_KB // G, nt),
        in_specs=[ucol, sp["b"], sp["c"], sp["tab"], sp["pw"], sp["d"]],
        out_specs=[ucol, sp["x"], sp["ent"]],
        out_shape=[jax.ShapeDtypeStruct((L, SSM_W), _F32),
                   jax.ShapeDtypeStruct((N_KB, L // SUBLANES, SUBLANES, 2 * KB_STATES), _F32),
                   jax.ShapeDtypeStruct((N_KB, nt, SUBLANES, 2 * KB_STATES), _F32)],
        scratch_shapes=[pltpu.VMEM((G, tb, LANES), _F32)] * 2 + [pltpu.VMEM((G, SUBLANES, 2 * KB_STATES), _F32)],
        compiler_params=_params("parallel", "arbitrary"),
    )(u, bmat, cmat, tab_f, pw_f, d_skip)


def _mixout_fwd(s, y_ret, x, w_glu, w_out, g2, tm):
    L = s.shape[0]

    def body(s_ref, yr_ref, x_ref, wg_ref, wo_ref, g_ref, ys_ref, glu_ref, cat_ref, mix_ref, x2_ref):
        for rows in _row_chunks(tm):
            ys = _gelu(s_ref[rows, :]).astype(_BF)
            ys_ref[rows, :] = ys
            glu = _dot(ys, wg_ref[...])
            glu_ref[rows, :] = glu
            cat_ref[rows, :RET_W] = yr_ref[rows, :]
            cat_ref[rows, RET_W:] = (glu[:, :SSM_W] * _sigmoid(glu[:, SSM_W:])).astype(_BF)
            mix = _dot(cat_ref[rows, :], wo_ref[...])
            mix_ref[rows, :] = mix
            x2_ref[rows, :] = x_ref[rows, :] + mix * _rms_r(mix) * g_ref[...]

    return pl.pallas_call(
        body, name="mixout_fwd", grid=(L // tm,),
        in_specs=[_row_spec(tm, SSM_W), _row_spec(tm, RET_W), _row_spec(tm, D_MODEL),
                  _weight_spec((SSM_W, 2 * SSM_W)), _weight_spec((D_MODEL, D_MODEL)), _full_spec((1, D_MODEL))],
        out_specs=[_row_spec(tm, SSM_W), _row_spec(tm, 2 * SSM_W), _row_spec(tm, D_MODEL),
                   _row_spec(tm, D_MODEL), _row_spec(tm, D_MODEL)],
        out_shape=[jax.ShapeDtypeStruct((L, SSM_W), _BF), jax.ShapeDtypeStruct((L, 2 * SSM_W), _F32),
                   jax.ShapeDtypeStruct((L, D_MODEL), _BF), jax.ShapeDtypeStruct((L, D_MODEL), _F32),
                   jax.ShapeDtypeStruct((L, D_MODEL), _F32)],
        compiler_params=_params("parallel"),
    )(s, y_ret, x, w_glu, w_out, g2)


FF1_COLS = D_FF // N_DEV


def _ff1_fwd(x2, g3, w1, tm):
    L = x2.shape[0]

    def body(x_ref, g_ref, w_ref, h_ref, f_ref):
        for rows in _row_chunks(tm):
            xv = x_ref[rows, :]
            h = (xv * _rms_r(xv) * g_ref[...]).astype(_BF)
            h_ref[rows, :] = h
            for j in range(N_DEV):
                f_ref[rows, j * FF1_COLS:(j + 1) * FF1_COLS] = _dot(h, w_ref[j])

    return pl.pallas_call(
        body, name="ff1_fwd", grid=(L // tm,),
        in_specs=[_row_spec(tm, D_MODEL), _full_spec((1, D_MODEL)), _weight_spec((N_DEV, D_MODEL, FF1_COLS))],
        out_specs=[_row_spec(tm, D_MODEL), _row_spec(tm, D_FF)],
        out_shape=[jax.ShapeDtypeStruct((L, D_MODEL), _BF), jax.ShapeDtypeStruct((L, D_FF), _F32)],
        compiler_params=_params("parallel"),
    )(x2, g3, w1)


def _ff2_loss(f1, x2, tgt, g4, w2, tm):
    L = f1.shape[0]

    def body(f_ref, x_ref, t_ref, g_ref, w_ref, dy_ref, dm_ref, dg_ref, ls_ref):
        @pl.when(pl.program_id(0) == 0)
        def _():
            dg_ref[...] = jnp.zeros_like(dg_ref)
            ls_ref[...] = jnp.zeros_like(ls_ref)

        g = g_ref[...]
        for rows in _row_chunks(tm):
            rl = jnp.maximum(f_ref[rows, :], 0.0)
            m = _dot((rl * rl).astype(_BF), w_ref[...])
            y = x_ref[rows, :] + m * _rms_r(m) * g
            err = y - t_ref[rows, :]
            ls_ref[...] += jnp.sum(err * err, axis=0, keepdims=True)
            dy = err * (1.0 / D_MODEL)
            dy_ref[rows, :] = dy
            dm, dgr = _rms_bwd(m, g, dy)
            dm_ref[rows, :] = dm.astype(_BF)
            dg_ref[...] += jnp.sum(dgr, axis=0, keepdims=True)

    return pl.pallas_call(
        body, name="ff2_loss", grid=(L // tm,),
        in_specs=[_row_spec(tm, D_FF), _row_spec(tm, D_MODEL), _row_spec(tm, D_MODEL),
                  _full_spec((1, D_MODEL)), _weight_spec((D_FF, D_MODEL))],
        out_specs=[_row_spec(tm, D_MODEL), _row_spec(tm, D_MODEL), _full_spec((1, D_MODEL)), _full_spec((1, D_MODEL))],
        out_shape=[jax.ShapeDtypeStruct((L, D_MODEL), _F32), jax.ShapeDtypeStruct((L, D_MODEL), _BF),
                   jax.ShapeDtypeStruct((1, D_MODEL), _F32), jax.ShapeDtypeStruct((1, D_MODEL), _F32)],
        compiler_params=_params("arbitrary"),
    )(f1, x2, tgt, g4, w2)


def _ff2_bwd(dm, f1, w2, tm, tn):
    L = dm.shape[0]
    last = L // tm - 1

    def body(dm_ref, f_ref, w_ref, df_ref, dw_ref, acc):
        @pl.when(pl.program_id(1) == 0)
        def _():
            acc[...] = jnp.zeros_like(acc)

        dmv = dm_ref[...]
        rl = jnp.maximum(f_ref[...], 0.0)
        df_ref[...] = (_dot_nt(dmv, w_ref[...]) * (2.0 * rl)).astype(_BF)
        acc[...] += _dot_tn((rl * rl).astype(_BF), dmv)

        @pl.when(pl.program_id(1) == last)
        def _():
            dw_ref[...] = acc[...].astype(_BF)

    return pl.pallas_call(
        body, name="ff2_bwd", grid=(D_FF // tn, L // tm),
        in_specs=[pl.BlockSpec((tm, D_MODEL), lambda j, i: (i, 0)), pl.BlockSpec((tm, tn), lambda j, i: (i, j)),
                  pl.BlockSpec((tn, D_MODEL), lambda j, i: (j, 0))],
        out_specs=[pl.BlockSpec((tm, tn), lambda j, i: (i, j)), pl.BlockSpec((tn, D_MODEL), lambda j, i: (j, 0))],
        out_shape=[jax.ShapeDtypeStruct((L, D_FF), _BF), jax.ShapeDtypeStruct((D_FF, D_MODEL), _BF)],
        scratch_shapes=[pltpu.VMEM((tn, D_MODEL), _F32)],
        compiler_params=_params("parallel", "arbitrary"),
    )(dm, f1, w2)


def _ff1_bwd(df1, w1, x2, mix, dy, g3, g2, tm):
    L = df1.shape[0]

    def body(df_ref, w_ref, x2_ref, mix_ref, dy_ref, g3_ref, g2_ref, dx2_ref, dmix_ref, dg3_ref, dg2_ref):
        @pl.when(pl.program_id(0) == 0)
        def _():
            dg3_ref[...] = jnp.zeros_like(dg3_ref)
            dg2_ref[...] = jnp.zeros_like(dg2_ref)

        for rows in _row_chunks(tm):
            dh = _dot_nt(df_ref[rows, 0:FF1_COLS], w_ref[0])
            for j in range(1, N_DEV):
                dh = dh + _dot_nt(df_ref[rows, j * FF1_COLS:(j + 1) * FF1_COLS], w_ref[j])
            dz, dgr = _rms_bwd(x2_ref[rows, :], g3_ref[...], dh)
            dg3_ref[...] += jnp.sum(dgr, axis=0, keepdims=True)
            dx2 = dy_ref[rows, :] + dz
            dx2_ref[rows, :] = dx2
            dmx, dgr2 = _rms_bwd(mix_ref[rows, :], g2_ref[...], dx2)
            dg2_ref[...] += jnp.sum(dgr2, axis=0, keepdims=True)
            dmix_ref[rows, :] = dmx.astype(_BF)

    vec = _full_spec((1, D_MODEL))
    return pl.pallas_call(
        body, name="ff1_bwd", grid=(L // tm,),
        in_specs=[_row_spec(tm, D_FF), _weight_spec((N_DEV, D_MODEL, FF1_COLS)), _row_spec(tm, D_MODEL),
                  _row_spec(tm, D_MODEL), _row_spec(tm, D_MODEL), vec, vec],
        out_specs=[_row_spec(tm, D_MODEL), _row_spec(tm, D_MODEL), vec, vec],
        out_shape=[jax.ShapeDtypeStruct((L, D_MODEL), _F32), jax.ShapeDtypeStruct((L, D_MODEL), _BF),
                   jax.ShapeDtypeStruct((1, D_MODEL), _F32), jax.ShapeDtypeStruct((1, D_MODEL), _F32)],
        compiler_params=_params("arbitrary"),
    )(df1, w1, x2, mix, dy, g3, g2)


def _matmul_tn(a, b, tm, tn, name, slots=False):
    L, K = a.shape
    N = b.shape[1]
    last = L // tm - 1

    def body(a_ref, b_ref, o_ref, acc):
        @pl.when(pl.program_id(1) == 0)
        def _():
            acc[...] = jnp.zeros_like(acc)

        acc[...] += _dot_tn(a_ref[...].astype(_BF), b_ref[...].astype(_BF))

        @pl.when(pl.program_id(1) == last)
        def _():
            if slots:
                o_ref[0] = acc[...].astype(_BF)
            else:
                o_ref[...] = acc[...].astype(_BF)

    if slots:
        out_spec = pl.BlockSpec((1, K, tn), lambda j, i: (j, 0, 0))
        out_shape = jax.ShapeDtypeStruct((N // tn, K, tn), _BF)
    else:
        out_spec = pl.BlockSpec((K, tn), lambda j, i: (0, j))
        out_shape = jax.ShapeDtypeStruct((K, N), _BF)
    return pl.pallas_call(
        body, name=name, grid=(N // tn, L // tm),
        in_specs=[pl.BlockSpec((tm, K), lambda j, i: (i, 0)), pl.BlockSpec((tm, tn), lambda j, i: (i, j))],
        out_specs=out_spec, out_shape=out_shape,
        scratch_shapes=[pltpu.VMEM((K, tn), _F32)],
        compiler_params=_params("parallel", "arbitrary"),
    )(a, b)


def _dw_in_t(pieces, h, tk):
    L = h.shape[0]
    last = L // tk - 1

    def body(p0, p1, p2, p3, p4, h_ref, o_ref, acc):
        @pl.when(pl.program_id(0) == 0)
        def _():
            acc[...] = jnp.zeros_like(acc)

        hv = h_ref[...]
        for j, p in enumerate((p0, p1, p2, p3, p4)):
            acc[j * RET_W:(j + 1) * RET_W, :] += _dot_tn(p[...].astype(_BF), hv)

        @pl.when(pl.program_id(0) == last)
        def _():
            o_ref[...] = acc[...].astype(_BF)

    return pl.pallas_call(
        body, name="dw_in", grid=(L // tk,),
        in_specs=[_row_spec(tk, RET_W)] * 5 + [_row_spec(tk, D_MODEL)],
        out_specs=_full_spec((IN_COLS, D_MODEL)), out_shape=jax.ShapeDtypeStruct((IN_COLS, D_MODEL), _BF),
        scratch_shapes=[pltpu.VMEM((IN_COLS, D_MODEL), _F32)],
        compiler_params=_params("arbitrary"),
    )(*pieces, h)


def _mixout_bwd(dmix, w_out, w_glu, glu, s, o, gate, ggn, tm, after=()):
    L = dmix.shape[0]

    def body(dmix_ref, wo_ref, wg_ref, glu_ref, s_ref, o_ref, gate_ref, ggn_ref,
             dglu_ref, ds_ref, dgate_ref, do_ref, dggn_ref):
        @pl.when(pl.program_id(0) == 0)
        def _():
            dggn_ref[...] = jnp.zeros_like(dggn_ref)

        ggn = ggn_ref[...]
        for rows in _row_chunks(tm):
            dcat = _dot_nt(dmix_ref[rows, :], wo_ref[...])
            dy_ret, dy_ssm = dcat[:, :RET_W], dcat[:, RET_W:]
            glu = glu_ref[rows, :]
            ga, sg = glu[:, :SSM_W], _sigmoid(glu[:, SSM_W:])
            dga = (dy_ssm * sg).astype(_BF)
            dgb = (dy_ssm * ga * sg * (1.0 - sg)).astype(_BF)
            dglu_ref[rows, :SSM_W] = dga
            dglu_ref[rows, SSM_W:] = dgb
            dys = _dot_nt(dga, wg_ref[:, :SSM_W]) + _dot_nt(dgb, wg_ref[:, SSM_W:])
            ds_ref[rows, :] = dys * _gelu_grad(s_ref[rows, :])
            gt = gate_ref[rows, :]
            sgt = _sigmoid(gt)
            for hh in range(N_HEAD):
                cols = slice(hh * HEAD_D, (hh + 1) * HEAD_D)
                ov = o_ref[rows, cols]
                dlt = ov - jnp.mean(ov, axis=-1, keepdims=True)
                rstd = lax.rsqrt(jnp.mean(dlt * dlt, axis=-1, keepdims=True) + NORM_EPS)
                on = dlt * rstd
                dyr = dy_ret[:, cols] * (gt[:, cols] * sgt[:, cols])
                dgate_ref[rows, cols] = dy_ret[:, cols] * (on * ggn[:, cols]) * (sgt[:, cols] * (1.0 + gt[:, cols] * (1.0 - sgt[:, cols])))
                dggn_ref[:, cols] += jnp.sum(dyr * on, axis=0, keepdims=True)
                don = dyr * ggn[:, cols]
                do = rstd * (don - jnp.mean(don, axis=-1, keepdims=True) - on * jnp.mean(don * on, axis=-1, keepdims=True))
                do_ref[rows, cols] = do.astype(_BF)

    body, in_specs, operands = _ordered(
        body, [_row_spec(tm, D_MODEL), _weight_spec((D_MODEL, D_MODEL)), _weight_spec((SSM_W, 2 * SSM_W)),
               _row_spec(tm, 2 * SSM_W), _row_spec(tm, SSM_W), _row_spec(tm, RET_W), _row_spec(tm, RET_W),
               _full_spec((1, RET_W))], (dmix, w_out, w_glu, glu, s, o, gate, ggn), after)
    return pl.pallas_call(
        body, name="mixout_bwd", grid=(L // tm,),
        in_specs=in_specs,
        out_specs=[_row_spec(tm, 2 * SSM_W), _row_spec(tm, SSM_W), _row_spec(tm, RET_W), _row_spec(tm, RET_W),
                   _full_spec((1, RET_W))],
        out_shape=[jax.ShapeDtypeStruct((L, 2 * SSM_W), _BF), jax.ShapeDtypeStruct((L, SSM_W), _F32),
                   jax.ShapeDtypeStruct((L, RET_W), _F32), jax.ShapeDtypeStruct((L, RET_W), _BF),
                   jax.ShapeDtypeStruct((1, RET_W), _F32)],
        compiler_params=_params("arbitrary"),
    )(*operands)


def _s5_bwd(u, ds, xs, ent, bmat, cmat, tab_r, pw_r, d_skip, tb, after=()):
    L = u.shape[0]
    nt = L // tb
    seg = tb // SUBLANES
    G = KB_PER_STEP
    rcol = pl.BlockSpec((tb, G * LANES), lambda kb, t: (nt - 1 - t, kb))
    sp = _s5_specs(seg, time=lambda t: nt - 1 - t)
    aspec = pl.BlockSpec((G, SUBLANES, 2 * KB_STATES), lambda kb, t: (kb, 0, 0))

    def body(u_ref, ds_ref, x_ref, ent_ref, b_ref, c_ref, tr_ref, pr_ref, d_ref,
             du_ref, db_ref, dc_ref, da_ref, dd_ref, up_scr, dp_scr, g_scr, lc_scr):
        @pl.when(pl.program_id(1) == 0)
        def _():
            lc_scr[...] = jnp.zeros_like(lc_scr)
            db_ref[...] = jnp.zeros_like(db_ref)
            dc_ref[...] = jnp.zeros_like(dc_ref)
            da_ref[...] = jnp.zeros_like(da_ref)
            dd_ref[...] = jnp.zeros_like(dd_ref)

        _rows_to_segments(up_scr, u_ref, seg)
        _rows_to_segments(dp_scr, ds_ref, seg)
        for g in range(G):
            g_scr[g] = _dot_nt(dp_scr[g].astype(_BF), c_ref[g]).reshape(seg, SUBLANES, 2 * KB_STATES)
        _scan_segments(g_scr, tr_ref, pr_ref, lc_scr, seg, reverse=True, fwd_ref=x_ref, fwd_entry_ref=ent_ref.at[:, 0],
                       da_ref=da_ref)
        for g in range(G):
            cols = slice(g * LANES, (g + 1) * LANES)
            uv, dsv = up_scr[g], dp_scr[g]
            ub, dsb = uv.astype(_BF), dsv.astype(_BF)
            lamb = g_scr[g].reshape(tb, 2 * KB_STATES).astype(_BF)
            db_ref[g] += _dot_tn(ub, lamb)
            dc_ref[g] += _dot_tn(dsb, x_ref[g].reshape(tb, 2 * KB_STATES).astype(_BF))
            dd_ref[:, cols] += jnp.sum(dsv * uv, axis=0, keepdims=True)
            up_scr[g] = _dot_nt(lamb, b_ref[g]) + d_ref[:, cols] * dsv
        _segments_to_rows(du_ref, up_scr, seg)

    body, in_specs, operands = _ordered(
        body, [rcol, rcol, sp["x"], sp["ent"], sp["b"], sp["c"], sp["tab"], sp["pw"], sp["d"]],
        (u, ds, xs, ent, bmat, cmat, tab_r, pw_r, d_skip), after)
    return pl.pallas_call(
        body, name="s5_bwd", grid=(N_KB // G, nt),
        in_specs=in_specs,
        out_specs=[rcol, sp["b"], sp["b"], aspec, sp["d"]],
        out_shape=[jax.ShapeDtypeStruct((L, SSM_W), _F32),
                   jax.ShapeDtypeStruct((N_KB, LANES, 2 * KB_STATES), _F32),
                   jax.ShapeDtypeStruct((N_KB, LANES, 2 * KB_STATES), _F32),
                   jax.ShapeDtypeStruct((N_KB, SUBLANES, 2 * KB_STATES), _F32),
                   jax.ShapeDtypeStruct((1, SSM_W), _F32)],
        scratch_shapes=[pltpu.VMEM((G, tb, LANES), _F32)] * 2
        + [pltpu.VMEM((G, seg, SUBLANES, 2 * KB_STATES), _F32), pltpu.VMEM((G, SUBLANES, 2 * KB_STATES), _F32)],
        compiler_params=_params("parallel", "arbitrary"),
    )(*operands)


def _retention_bwd(q, k, v, do, r_prev, consts, cosf, sinf, after=()):
    L = q.shape[0]
    nc = L // CHUNK
    cps = math.gcd(RET_STEP_CHUNKS, nc)
    nb = nc // cps
    blk = pl.BlockSpec((cps * CHUNK, RET_W), lambda n: (nb - 1 - n, 0))
    rope_blk = pl.BlockSpec((cps * CHUNK, HEAD_D), lambda n: (nb - 1 - n, 0))

    def body(q_ref, k_ref, v_ref, do_ref, rp_ref, dm_ref, xi_ref, zeta_ref, gc_ref, cos_ref, sin_ref,
             dq_ref, dk_ref, dv_ref, g_scr):
        @pl.when(pl.program_id(0) == 0)
        def _():
            g_scr[...] = jnp.zeros_like(g_scr)

        for hh in range(N_HEAD):
            cols = slice(hh * HEAD_D, (hh + 1) * HEAD_D)
            dm, zeta = dm_ref[hh], zeta_ref[hh]
            gst = g_scr[hh]
            for c in reversed(range(cps)):
                rows = slice(c * CHUNK, (c + 1) * CHUNK)
                qv, kv, vv, dov = q_ref[rows, cols], k_ref[rows, cols], v_ref[rows, cols], do_ref[rows, cols]
                rb = rp_ref[hh, c].astype(_BF)
                gb = gst.astype(_BF)
                sb = (_dot_nt(qv, kv) * dm).astype(_BF)
                dab = (_dot_nt(dov, vv) * dm).astype(_BF)
                dox = (dov.astype(_F32) * xi_ref[hh]).astype(_BF)
                vz = (vv.astype(_F32) * zeta).astype(_BF)
                dq = _dot(dab, kv) + _dot_nt(dox, rb)
                dk = _dot_tn(dab, qv) + _dot_nt(vz, gb)
                dv = _dot_tn(sb, dov) + _dot(kv, gb) * zeta
                gst = gc_ref[hh, 0:1, :] * gst + _dot_tn(qv, dox)
                cs, sn = cos_ref[rows, :], sin_ref[rows, :]
                dq_ref[rows, cols] = _rope_t(dq, cs, sn).astype(_BF)
                dk_ref[rows, cols] = (_rope_t(dk, cs, sn) * (HEAD_D ** -0.5)).astype(_BF)
                dv_ref[rows, cols] = dv.astype(_BF)
            g_scr[hh] = gst

    body, in_specs, operands = _ordered(
        body, [blk, blk, blk, blk, pl.BlockSpec((N_HEAD, cps, HEAD_D, HEAD_D), lambda n: (0, nb - 1 - n, 0, 0))]
        + _head_specs() + [rope_blk, rope_blk], (q, k, v, do, r_prev, *consts, cosf, sinf), after)
    return pl.pallas_call(
        body, name="retention_bwd", grid=(nb,),
        in_specs=in_specs,
        out_specs=[blk, blk, blk],
        out_shape=[jax.ShapeDtypeStruct((L, RET_W), _BF)] * 3,
        scratch_shapes=[pltpu.VMEM((N_HEAD, HEAD_D, HEAD_D), _F32)],
        compiler_params=_params("arbitrary"),
    )(*operands)


def _inproj_bwd(pieces, w_in_t, x, dx2, g1, tm, after=()):
    L = x.shape[0]

    def body(p0, p1, p2, p3, p4, w_ref, x_ref, dx2_ref, g_ref, dx_ref, dg_ref):
        @pl.when(pl.program_id(0) == 0)
        def _():
            dg_ref[...] = jnp.zeros_like(dg_ref)

        for rows in _row_chunks(tm):
            dh = None
            for j, p in enumerate((p0, p1, p2, p3, p4)):
                part = _dot(p[rows, :].astype(_BF), w_ref[j * RET_W:(j + 1) * RET_W, :])
                dh = part if dh is None else dh + part
            dz, dgr = _rms_bwd(x_ref[rows, :], g_ref[...], dh)
            dx_ref[rows, :] = dx2_ref[rows, :] + dz
            dg_ref[...] += jnp.sum(dgr, axis=0, keepdims=True)

    body, in_specs, operands = _ordered(
        body, [_row_spec(tm, RET_W)] * 5 + [_weight_spec((IN_COLS, D_MODEL)), _row_spec(tm, D_MODEL),
                                             _row_spec(tm, D_MODEL), _full_spec((1, D_MODEL))],
        (*pieces, w_in_t, x, dx2, g1), after)
    return pl.pallas_call(
        body, name="inproj_bwd", grid=(L // tm,),
        in_specs=in_specs,
        out_specs=[_row_spec(tm, D_MODEL), _full_spec((1, D_MODEL))],
        out_shape=[jax.ShapeDtypeStruct((L, D_MODEL), _F32), jax.ShapeDtypeStruct((1, D_MODEL), _F32)],
        compiler_params=_params("arbitrary"),
    )(*operands)


def _sum_adamw(parts, w, m, v, tr, name):
    _, R, Cc = parts.shape

    def body(p_ref, w_ref, m_ref, v_ref, g_ref, d_ref, nm_ref, nv_ref):
        gv = p_ref[0].astype(_F32)
        for s in range(1, N_DEV):
            gv = gv + p_ref[s].astype(_F32)
        g_ref[...] = gv
        nm = ADAM_B1 * m_ref[...] + (1.0 - ADAM_B1) * gv
        nv = ADAM_B2 * v_ref[...] + (1.0 - ADAM_B2) * (gv * gv)
        m_hat = nm / (1.0 - ADAM_B1 ** ADAM_STEP)
        v_hat = nv / (1.0 - ADAM_B2 ** ADAM_STEP)
        d_ref[...] = -ADAM_LR * (m_hat / (jnp.sqrt(v_hat) + ADAM_EPS) + ADAM_WD * w_ref[...])
        nm_ref[...] = nm
        nv_ref[...] = nv

    spec = _row_spec(tr, Cc)
    return pl.pallas_call(
        body, name=name, grid=(R // tr,),
        in_specs=[pl.BlockSpec((N_DEV, tr, Cc), lambda i: (0, i, 0))] + [spec] * 3, out_specs=[spec] * 4,
        out_shape=[jax.ShapeDtypeStruct((R, Cc), _F32)] * 4,
        compiler_params=_params("parallel"),
    )(parts, w, m, v)


def _my_place():
    return lax.axis_index("x"), lax.axis_index("y"), lax.axis_index("c")


def _all_gather(blocks):
    n = len(blocks)

    def body(*refs):
        x_refs, out_refs, done_ref = refs[:n], refs[n:2 * n], refs[2 * n]
        send_sems, recv_sems, local_sems = refs[2 * n + 1:]
        done_ref[...] = jnp.zeros_like(done_ref)
        x, y, c = _my_place()
        me, sibling = (x, y, c), (x, y, 1 - c)
        chips = [(1 - x, y), (x, 1 - y), (1 - x, 1 - y)]

        def slot(a, px, py, pc):
            return out_refs[a].at[4 * px + 2 * py + pc]

        def copy(a, k, blk, to, own=False):
            return pltpu.make_async_remote_copy(
                src_ref=x_refs[a] if own else slot(a, *blk), dst_ref=slot(a, *blk),
                send_sem=send_sems.at[a, k], recv_sem=recv_sems.at[a, k], device_id=to, device_id_type=MESH)

        mine = [pltpu.make_async_copy(x_refs[a], slot(a, *me), local_sems.at[a]) for a in range(n)]
        for cp in mine:
            cp.start()
        first = []
        for a in range(n):
            first.append(copy(a, 0, me, sibling, own=True))
            first += [copy(a, 1 + j, me, (*chip, c), own=True) for j, chip in enumerate(chips)]
        for cp in first:
            cp.start()
        passed = []
        for j, chip in enumerate(chips):
            for a in range(n):
                copy(a, 1 + j, (*chip, c), me).wait_recv()
                fwd = copy(a, 4 + j, (*chip, c), sibling)
                fwd.start()
                passed.append(fwd)
        for a in range(n):
            copy(a, 0, sibling, me).wait_recv()
            for j, chip in enumerate(chips):
                copy(a, 4 + j, (*chip, 1 - c), me).wait_recv()
        for cp in first + passed:
            cp.wait_send()
        for cp in mine:
            cp.wait()

    any_spec = pl.BlockSpec(memory_space=pl.ANY)
    outs = pl.pallas_call(
        body, name="weights_all_gather",
        in_specs=[any_spec] * n, out_specs=[any_spec] * n + [pl.BlockSpec(memory_space=pltpu.VMEM)],
        out_shape=[jax.ShapeDtypeStruct((N_DEV,) + b.shape, b.dtype) for b in blocks]
        + [jax.ShapeDtypeStruct((SUBLANES, LANES), _F32)],
        scratch_shapes=[pltpu.SemaphoreType.DMA((n, 7)), pltpu.SemaphoreType.DMA((n, 7)), pltpu.SemaphoreType.DMA((n,))],
    )(*blocks)
    return outs[:n], outs[n]


def _exchange(bigs, small):
    n = len(bigs)
    r = small.shape[0]

    def body(*refs):
        in_refs, out_refs = refs[:n + 1], refs[n + 1:2 * n + 2]
        send_sems, recv_sems, local_sems = refs[2 * n + 2:]
        x, y, c = _my_place()
        me = 4 * x + 2 * y + c
        own = [pltpu.make_async_copy(in_refs[a].at[me], out_refs[a].at[me], local_sems.at[a]) for a in range(n)]
        own.append(pltpu.make_async_copy(in_refs[n], out_refs[n].at[me], local_sems.at[n]))
        for cp in own:
            cp.start()
        copies = []
        for kk in range(1, N_DEV):
            px, py, pc = x ^ (kk >> 2), y ^ ((kk >> 1) & 1), c ^ (kk & 1)
            peer = 4 * px + 2 * py + pc
            for a in range(n + 1):
                src = in_refs[a].at[peer] if a < n else in_refs[a]
                copies.append(pltpu.make_async_remote_copy(
                    src_ref=src, dst_ref=out_refs[a].at[me],
                    send_sem=send_sems.at[a, kk - 1], recv_sem=recv_sems.at[a, kk - 1],
                    device_id=(px, py, pc), device_id_type=MESH))
        for cp in copies:
            cp.start()
        for cp in copies:
            cp.wait_recv()
        for cp in copies:
            cp.wait_send()
        for cp in own:
            cp.wait()

    any_spec = pl.BlockSpec(memory_space=pl.ANY)
    outs = pl.pallas_call(
        body, name="grad_exchange",
        in_specs=[any_spec] * (n + 1), out_specs=[any_spec] * (n + 1),
        out_shape=[jax.ShapeDtypeStruct(b.shape, b.dtype) for b in bigs]
        + [jax.ShapeDtypeStruct((N_DEV, r, LANES), small.dtype)],
        scratch_shapes=[pltpu.SemaphoreType.DMA((n + 1, 7)), pltpu.SemaphoreType.DMA((n + 1, 7)),
                        pltpu.SemaphoreType.DMA((n + 1,))],
    )(*bigs, small)
    return outs[:n], outs[n]


HBM_SPEC = pl.BlockSpec(memory_space=pltpu.HBM)
SEM_SPEC = pl.BlockSpec(memory_space=pltpu.SEMAPHORE)
DATAFLOW = pltpu.SideEffectType.DATAFLOW_SIDE_EFFECTING


def _my_index():
    x, y, c = _my_place()
    return 4 * x + 2 * y + c


def _landing(own_block):
    zone = lax.empty((N_DEV,) + own_block.shape, own_block.dtype)
    return lax.dynamic_update_index_in_dim(zone, own_block, _my_index(), 0)


def _split_copies(src_refs, land_refs, send_sems, recv_sems, gather, first=0):
    x, y, c = _my_place()
    me = 4 * x + 2 * y + c
    copies = []
    for a, (src, land) in enumerate(zip(src_refs, land_refs)):
        for kk in range(1, N_DEV):
            px, py, pc = x ^ (kk >> 2), y ^ ((kk >> 1) & 1), c ^ (kk & 1)
            peer = 4 * px + 2 * py + pc
            copies.append(pltpu.make_async_remote_copy(
                src_ref=src if gather else src.at[peer], dst_ref=land.at[me],
                send_sem=send_sems.at[(first + a) * 7 + kk - 1], recv_sem=recv_sems.at[(first + a) * 7 + kk - 1],
                device_id=(px, py, pc), device_id_type=MESH))
    return copies


def _split_start(srcs, lands, gather, name):
    n = len(srcs)

    def body(*refs):
        src_refs, land_refs = refs[:n], refs[n:2 * n]
        send_sems, recv_sems = refs[2 * n], refs[2 * n + 1]
        token = refs[-1]
        for cp in _split_copies(src_refs, land_refs, send_sems, recv_sems, gather):
            cp.start()
        token[...] = jnp.zeros_like(token)

    outs = pl.pallas_call(
        body, name=name,
        out_shape=(pltpu.SemaphoreType.DMA((7 * n,)), pltpu.SemaphoreType.DMA((7 * n,)),
                   *[pltpu.HBM(t.shape, t.dtype) for t in srcs], *[pltpu.HBM(t.shape, t.dtype) for t in lands],
                   jax.ShapeDtypeStruct((SUBLANES, LANES), _F32)),
        in_specs=[HBM_SPEC] * (2 * n),
        out_specs=(SEM_SPEC, SEM_SPEC, *[HBM_SPEC] * (2 * n), pl.BlockSpec(memory_space=pltpu.VMEM)),
        input_output_aliases={i: 2 + i for i in range(2 * n)},
        compiler_params=pltpu.CompilerParams(has_side_effects=DATAFLOW),
    )(*[pltpu.with_memory_space_constraint(t, pltpu.HBM) for t in list(srcs) + list(lands)])
    return outs[0], outs[1], outs[2:2 + n], outs[2 + n:2 + 2 * n], outs[-1]


def _split_wait(send_sems, recv_sems, srcs, lands, after, gather, name, first=0):
    n = len(srcs)

    def body(*refs):
        src_refs, land_refs = refs[:n], refs[n:2 * n]
        send_s, recv_s = refs[2 * n], refs[2 * n + 1]
        for cp in _split_copies(src_refs, land_refs, send_s, recv_s, gather, first):
            cp.wait_send()
            cp.wait_recv()

    outs = pl.pallas_call(
        body, name=name,
        out_shape=tuple(pltpu.HBM(t.shape, t.dtype) for t in list(srcs) + list(lands)),
        in_specs=[HBM_SPEC] * (2 * n) + [SEM_SPEC, SEM_SPEC, pl.BlockSpec(memory_space=pl.ANY)],
        out_specs=tuple([HBM_SPEC] * (2 * n)),
        input_output_aliases={i: i for i in range(2 * n)},
        compiler_params=pltpu.CompilerParams(has_side_effects=DATAFLOW),
    )(*srcs, *lands, send_sems, recv_sems, after)
    return outs[n:]


def _discretize(lam_re, lam_im, log_dt, b_re, b_im):
    lr = jnp.minimum(lam_re, -1e-4)
    li = lam_im
    dt = jnp.exp(log_dt)[:, None]
    er = jnp.exp(lr * dt)
    ar, ai = er * jnp.cos(li * dt), er * jnp.sin(li * dt)
    den = lr * lr + li * li
    cr = ((ar - 1.0) * lr + ai * li) / den
    ci = (ai * lr - (ar - 1.0) * li) / den
    bbr = cr[:, :, None] * b_re - ci[:, :, None] * b_im
    bbi = cr[:, :, None] * b_im + ci[:, :, None] * b_re
    return ar, ai, bbr, bbi


def _cmul(ar, ai, br, bi):
    return ar * br - ai * bi, ar * bi + ai * br


def _cpowers(ar, ai, n):
    pr, pi = ar[None], ai[None]
    while pr.shape[0] < n:
        nr, ni = _cmul(pr, pi, pr[-1][None], pi[-1][None])
        pr, pi = jnp.concatenate([pr, nr]), jnp.concatenate([pi, ni])
    return pr[:n], pi[:n]


def _scan_tables(ar, ai, seg, reverse):
    if reverse:
        ai = -ai
    ar, ai = ar.reshape(N_KB, KB_STATES), ai.reshape(N_KB, KB_STATES)
    pr, pi = _cpowers(ar, ai, seg)
    a1 = (pr[-1], pi[-1])
    a2 = _cmul(*a1, *a1)
    a4 = _cmul(*a2, *a2)
    row = jnp.arange(SUBLANES)[None, :, None]
    wide = lambda t: jnp.broadcast_to(t[:, None, :], (N_KB, SUBLANES, KB_STATES))
    tabs = [wide(ar), wide(ai)]
    for dist, (qr, qi) in ((1, a1), (2, a2), (4, a4)):
        keep = (row < SUBLANES - dist) if reverse else (row >= dist)
        tabs += [jnp.where(keep, wide(qr), 0.0), jnp.where(keep, wide(qi), 0.0)]
    tabs += [wide(a1[0]), wide(a1[1])]
    if reverse:
        pr, pi = pr[::-1], pi[::-1]
    pw = jnp.transpose(jnp.concatenate([pr, pi], axis=-1), (1, 0, 2))[:, :, None, :]
    return jnp.stack(tabs, axis=1).astype(_F32), pw.astype(_F32)


def _block_diag_in(br, bi):
    eye = jnp.eye(GROUPS_PER_KB, dtype=_F32)
    one = lambda t: jnp.einsum("kgpc,gh->kgchp", t.reshape(N_KB, GROUPS_PER_KB, N_STATE, SSM_GC), eye).reshape(
        N_KB, LANES, KB_STATES)
    return jnp.concatenate([one(br), one(bi)], axis=-1)


def _block_diag_in_t(dmat):
    d6 = dmat.reshape(N_KB, GROUPS_PER_KB, SSM_GC, 2, GROUPS_PER_KB, N_STATE)
    eye = jnp.eye(GROUPS_PER_KB, dtype=_F32)
    both = jnp.einsum("kgcrhp,gh->rkgpc", d6, eye).reshape(2, N_GROUP, N_STATE, SSM_GC)
    return both[0], both[1]


def _block_diag_out(c_re, c_im):
    eye = jnp.eye(GROUPS_PER_KB, dtype=_F32)
    one = lambda t: jnp.einsum("kgcp,gh->khpgc", t.reshape(N_KB, GROUPS_PER_KB, SSM_GC, N_STATE), eye).reshape(
        N_KB, KB_STATES, LANES)
    return jnp.concatenate([one(c_re), -one(c_im)], axis=1)


def _block_diag_out_t(dmat_t):
    d6 = dmat_t.reshape(N_KB, GROUPS_PER_KB, SSM_GC, 2, GROUPS_PER_KB, N_STATE)
    eye = jnp.eye(GROUPS_PER_KB, dtype=_F32)
    both = jnp.einsum("kgcrhp,gh->rkgcp", d6, eye).reshape(2, N_GROUP, SSM_GC, N_STATE)
    return both[0], -both[1]


SMALL_NAMES = ("norm_mix_pre", "norm_mix_post", "ret_gn_gain", "ssm_lambda_re", "ssm_lambda_im", "ssm_log_dt",
               "ssm_b_re", "ssm_b_im", "ssm_c_re", "ssm_c_im", "ssm_d", "norm_mlp_pre", "norm_mlp_post")


def _local_grads(x, tgt, small, weights, emit, emit_small, tm, tk, tb, zero=0.0):
    L = x.shape[0]
    g1, g2, ggn = small["norm_mix_pre"], small["norm_mix_post"], small["ret_gn_gain"]
    g3, g4, d_skip = small["norm_mlp_pre"], small["norm_mlp_post"], small["ssm_d"]

    rope = _rope_tables(L)
    consts = _ret_consts()

    disc_in = (small["ssm_lambda_re"][0], small["ssm_lambda_im"][0], small["ssm_log_dt"][0] + zero,
               small["ssm_b_re"][0], small["ssm_b_im"][0])
    (ar, ai, bbr, bbi), disc_vjp = jax.vjp(_discretize, *disc_in)
    bmat = _block_diag_in(bbr, bbi).astype(_BF)
    cmat = _block_diag_out(small["ssm_c_re"][0], small["ssm_c_im"][0]).astype(_BF)
    seg = tb // SUBLANES
    tab_f, pw_f = _scan_tables(ar, ai, seg, False)
    tab_r, pw_r = _scan_tables(ar, ai, seg, True)

    h1 = _prenorm(x, g1, min(4 * tm, L), after=(pw_r,))
    (w_in_t,) = weights("in", h1)
    q, k, v, gate, u, cosf, sinf = _inproj_fwd(h1, w_in_t, rope, min(2 * tm, L))
    o, y_ret, r_prev = _retention_fwd(q, k, v, gate, ggn, consts)
    s, xs, ent = _s5_fwd(u, bmat, cmat, tab_f, pw_f, d_skip, tb)
    w_glu, w_out = weights("mix", s)
    ys, glu, cat, mix, x2 = _mixout_fwd(s, y_ret, x, w_glu, w_out, g2, min(2 * tm, L))
    w_ff1, w_ff2 = weights("mlp", x2)
    h3, f1 = _ff1_fwd(x2, g3, w_ff1, min(2 * tm, L))
    dy, dm, dg4, sq = _ff2_loss(f1, x2, tgt, g4, w_ff2, min(2 * tm, L))

    df1, dw_ff2 = _ff2_bwd(dm, f1, w_ff2, min(1024, L), 1024)
    dx2, dmix, dg3, dg2 = _ff1_bwd(df1, w_ff1, x2, mix, dy, g3, g2, min(2 * tm, L))
    dw_ff1 = _matmul_tn(h3, df1, tk, FF1_COLS, "dw_ff1", slots=True)
    token = emit({"w_ff1": dw_ff1, "w_ff2": dw_ff2})
    dglu, ds, dgate, do, dggn = _mixout_bwd(dmix, w_out, w_glu, glu, s, o, gate, ggn, min(2 * tm, L), after=token)
    dw_out = _matmul_tn(cat, dmix, tk, 1024, "dw_out")
    dw_glu = _matmul_tn(ys, dglu, tk, 1024, "dw_glu")
    token = emit({"w_glu": dw_glu, "w_out": dw_out})
    du, dbmat, dcmat, da8, dd = _s5_bwd(u, ds, xs, ent, bmat, cmat, tab_r, pw_r, d_skip, tb, after=token)

    da = jnp.sum(da8, axis=1)
    dar = da[:, :KB_STATES].reshape(N_GROUP, N_STATE)
    dai = da[:, KB_STATES:].reshape(N_GROUP, N_STATE)
    dbr, dbi = _block_diag_in_t(dbmat)
    dlre, dlim, dldt, dbre, dbim = disc_vjp((dar, dai, dbr, dbi))
    dcre, dcim = _block_diag_out_t(dcmat)
    token = emit_small({
        "norm_mix_post": dg2, "ret_gn_gain": dggn,
        "ssm_lambda_re": dlre[None], "ssm_lambda_im": dlim[None], "ssm_log_dt": dldt[None],
        "ssm_b_re": dbre[None], "ssm_b_im": dbim[None], "ssm_c_re": dcre[None], "ssm_c_im": dcim[None],
        "ssm_d": dd, "norm_mlp_pre": dg3, "norm_mlp_post": dg4,
    }, sq)

    dq, dk, dv = _retention_bwd(q, k, v, do, r_prev, consts, cosf, sinf, after=token)
    pieces = (dq, dk, dv, dgate, du)
    dw_in_t = _dw_in_t(pieces, h1, min(1024, L))
    token = emit({"w_in": dw_in_t})
    gx, dg1 = _inproj_bwd(pieces, w_in_t, x, dx2, g1, min(2 * tm, L), after=token)
    return gx, dg1


BIG_SHAPES = {"w_in": (D_MODEL, IN_COLS // N_DEV), "w_glu": (SSM_W, 2 * SSM_W // N_DEV), "w_out": (D_MODEL // N_DEV, D_MODEL),
              "w_ff1": (D_MODEL, FF1_COLS), "w_ff2": (D_FF // N_DEV, D_MODEL)}
BIG_NAMES = ("w_in", "w_glu", "w_out", "w_ff1", "w_ff2")


def _cols_from_slots(g):
    return jnp.transpose(g, (1, 0, 2)).reshape(g.shape[1], N_DEV * g.shape[2])


def _cols_to_slots(dw):
    r, cols = dw.shape
    return jnp.transpose(dw.reshape(r, N_DEV, cols // N_DEV), (1, 0, 2))


WEIGHT_GROUPS = {"in": ("w_in",), "mix": ("w_glu", "w_out"), "mlp": ("w_ff1", "w_ff2")}


def _weight_from_slots(name, g):
    if name == "w_glu":
        return _cols_from_slots(g)
    if name == "w_ff1":
        return g
    return g.reshape(N_DEV * g.shape[1], g.shape[2])


def _grad_slots(name, dw):
    if name == "w_glu":
        return _cols_to_slots(dw)
    if name == "w_ff1":
        return dw
    if name == "w_in":
        return dw.reshape(N_DEV, BIG_SHAPES[name][1], BIG_SHAPES[name][0])
    return dw.reshape((N_DEV,) + BIG_SHAPES[name])


PIECE_ROWS = 8


VEC_NAMES = tuple(n for n in SMALL_NAMES if n[:6] not in ("ssm_b_", "ssm_c_"))
BC_NAMES = ("ssm_b_re", "ssm_b_im", "ssm_c_re", "ssm_c_im")
BC_ROWS = N_GROUP * SSM_GC


def _bc_view(name, t):
    t = t[0]
    if name.startswith("ssm_b_"):
        t = jnp.swapaxes(t, 1, 2)
    return t.reshape(BC_ROWS, N_STATE)


def _bc_unview(name, t):
    t = t.reshape(N_GROUP, SSM_GC, N_STATE)
    if name.startswith("ssm_b_"):
        t = jnp.swapaxes(t, 1, 2)
    return t[None]


def _pack_bc(vals):
    return jnp.concatenate([_bc_view(n, vals[n]).astype(_F32) for n in BC_NAMES], axis=0)


def _unpack_bc(buf):
    return {n: _bc_unview(n, buf[j * BC_ROWS:(j + 1) * BC_ROWS]) for j, n in enumerate(BC_NAMES)}


def _small_layout(shapes):
    off, rows = {}, 0
    for n in VEC_NAMES:
        off[n] = rows
        rows += -(-math.prod(shapes[n]) // (PIECE_ROWS * LANES)) * PIECE_ROWS
    return off, rows, rows + PIECE_ROWS


def _pack_small(vals, shapes, last=None):
    parts = []
    for n in VEC_NAMES:
        flat = vals[n].reshape(-1).astype(_F32)
        pad = -flat.shape[0] % (PIECE_ROWS * LANES)
        if pad:
            flat = jnp.concatenate([flat, jnp.zeros((pad,), _F32)])
        parts.append(flat.reshape(-1, LANES))
    parts.append(jnp.zeros((PIECE_ROWS, LANES), _F32) if last is None else last)
    return jnp.concatenate(parts, axis=0)


def _unpack_small(buf, shapes):
    off, _, _ = _small_layout(shapes)
    out = {}
    for n in VEC_NAMES:
        size = math.prod(shapes[n])
        rows = -(-size // LANES)
        out[n] = buf[off[n]:off[n] + rows].reshape(-1)[:size].reshape(shapes[n])
    return out


WEIGHT_NAMES = ('norm_mix_pre', 'norm_mix_post', 'w_in', 'ret_gn_gain', 'ssm_lambda_re', 'ssm_lambda_im', 'ssm_log_dt',
                'ssm_b_re', 'ssm_b_im', 'ssm_c_re', 'ssm_c_im', 'ssm_d', 'w_glu', 'w_out', 'norm_mlp_pre',
                'norm_mlp_post', 'w_ff1', 'w_ff2')


def kernel(x, norm_mix_pre, norm_mix_post, w_in, ret_gn_gain, ssm_lambda_re, ssm_lambda_im, ssm_log_dt, ssm_b_re, ssm_b_im, ssm_c_re, ssm_c_im, ssm_d, w_glu, w_out, norm_mlp_pre, norm_mlp_post, w_ff1, w_ff2, loss_target, m_norm_mix_pre, m_norm_mix_post, m_w_in, m_ret_gn_gain, m_ssm_lambda_re, m_ssm_lambda_im, m_ssm_log_dt, m_ssm_b_re, m_ssm_b_im, m_ssm_c_re, m_ssm_c_im, m_ssm_d, m_w_glu, m_w_out, m_norm_mlp_pre, m_norm_mlp_post, m_w_ff1, m_w_ff2, v_norm_mix_pre, v_norm_mix_post, v_w_in, v_ret_gn_gain, v_ssm_lambda_re, v_ssm_lambda_im, v_ssm_log_dt, v_ssm_b_re, v_ssm_b_im, v_ssm_c_re, v_ssm_c_im, v_ssm_d, v_w_glu, v_w_out, v_norm_mlp_pre, v_norm_mlp_post, v_w_ff1, v_w_ff2):
    args = dict(locals())
    w = {n: args[n] for n in WEIGHT_NAMES}
    m = {n: args["m_" + n] for n in WEIGHT_NAMES}
    v = {n: args["v_" + n] for n in WEIGHT_NAMES}
    L = x.shape[1]
    tm = min(256, L)
    tk = min(2048, L)
    tb = min(1024, L)

    order = [n for names in WEIGHT_GROUPS.values() for n in names]
    blocks = [(w[n][0].T if n == "w_in" else w[n][0]).astype(_BF) for n in order]
    gathered = _split_start(blocks, [_landing(b) for b in blocks], True, "weights_start")
    zero = gathered[4][0, 0]

    def weights(group, after):
        names = WEIGHT_GROUPS[group]
        first = order.index(names[0])
        part = slice(first, first + len(names))
        landed = _split_wait(gathered[0], gathered[1], gathered[2][part], gathered[3][part], after, True,
                             "weights_wait_" + group, first=first)
        return [_weight_from_slots(n, g) for n, g in zip(names, landed)]

    in_flight = []

    def emit(dws):
        names = sorted(dws)
        srcs = [_grad_slots(n, dws[n]) for n in names]
        lands = [_landing(lax.dynamic_index_in_dim(t, _my_index(), 0, keepdims=False)) for t in srcs]
        started = _split_start(srcs, lands, False, "grads_start_" + "_".join(names))
        in_flight.append((names, started))
        return (started[4],)

    shapes = {n: w[n].shape for n in SMALL_NAMES}
    first_piece = {SMALL_NAMES[0]: jnp.zeros(shapes[SMALL_NAMES[0]], _F32)}
    small_flight = []

    def emit_small(gs, sq):
        loss_rows = jnp.broadcast_to(0.5 / D_MODEL * jnp.sum(sq), (PIECE_ROWS, LANES)).astype(_F32)
        bufs = [_pack_small({**first_piece, **gs}, shapes, loss_rows), _pack_bc(gs)]
        small_flight.append(_split_start(bufs, [_landing(b) for b in bufs], True, "small_grads_start"))
        return (small_flight[0][4],)

    small_w = {n: w[n] for n in SMALL_NAMES}
    gx, dg1 = _local_grads(x[0], loss_target[0], small_w, weights, emit, emit_small, tm, tk, tb, zero=zero)
    last_buf = dg1.reshape(PIECE_ROWS, LANES)
    last_started = _split_start([last_buf], [_landing(last_buf)], True, "last_grad_start")

    grads, delta, new_m, new_v = {}, {}, {}, {}
    after = last_started[4]
    for names, started in in_flight:
        landed = _split_wait(*started[:4], after, False, "grads_wait_" + "_".join(names))
        for n, parts in zip(names, landed):
            flip = (lambda t: t.T) if n == "w_in" else (lambda t: t)
            res = _sum_adamw(parts, flip(w[n][0]), flip(m[n][0]), flip(v[n][0]), math.gcd(256, parts.shape[1]), "adamw_" + n)
            grads[n], delta[n], new_m[n], new_v[n] = (flip(t)[None] for t in res)
        after = res[1]
    small_parts, bc_parts = _split_wait(*small_flight[0][:4], after, True, "small_grads_wait")
    last_parts = _split_wait(*last_started[:4], small_parts, True, "last_grad_wait")[0]
    small_parts = lax.dynamic_update_slice(small_parts, last_parts, (0, 0, 0))
    res_bc = _sum_adamw(bc_parts, _pack_bc(w), _pack_bc(m), _pack_bc(v), BC_ROWS, "adamw_bc")
    sw, sm, sv = _pack_small(w, shapes), _pack_small(m, shapes), _pack_small(v, shapes)
    res = _sum_adamw(small_parts, sw, sm, sv, sw.shape[0], "adamw_small")
    for dst, buf, buf_bc in zip((grads, delta, new_m, new_v), res, res_bc):
        dst.update(_unpack_small(buf, shapes))
        dst.update(_unpack_bc(buf_bc))
    _, loss_at, _ = _small_layout(shapes)
    loss = res[0][loss_at, 0]

    return (loss, gx[None], *[grads[n] for n in WEIGHT_NAMES], *[delta[n] for n in WEIGHT_NAMES],
            *[new_m[n] for n in WEIGHT_NAMES], *[new_v[n] for n in WEIGHT_NAMES])
```

```python
import math

import jax
import jax.numpy as jnp
from jax import lax
from jax.experimental import pallas as pl
from jax.experimental.pallas import tpu as pltpu

_BF = jnp.bfloat16
_F32 = jnp.float32

D_MODEL = 1024
RET_W = 512
N_HEAD = 4
HEAD_D = 128
CHUNK = 256
ROPE_CHUNK = 128
SSM_W = 512
SSM_GC = 16
N_GROUP = 32
N_STATE = 64
GROUPS_PER_KB = 8
N_KB = 4
KB_STATES = GROUPS_PER_KB * N_STATE
D_FF = 4096
IN_COLS = 2560
NORM_EPS = 1e-6
ROPE_BASE = 10000.0
N_DEV = 8

ADAM_LR = 0.001
ADAM_B1 = 0.9
ADAM_B2 = 0.999
ADAM_EPS = 1e-08
ADAM_WD = 0.01
ADAM_STEP = 10

SUBLANES = 8
LANES = 128
VMEM_LIMIT = 52 * 1024 * 1024
RET_STEP_CHUNKS = 2
KB_PER_STEP = 2
SCAN_UNROLL = True
FIX_UNROLL = 8

MESH = pl.DeviceIdType.MESH


def _params(*sem):
    return pltpu.CompilerParams(dimension_semantics=sem, vmem_limit_bytes=VMEM_LIMIT)


def _dot(a, b):
    return jnp.dot(a, b, preferred_element_type=_F32)


def _dot_nt(a, b):
    return lax.dot_general(a, b, (((1,), (1,)), ((), ())), preferred_element_type=_F32)


def _dot_tn(a, b):
    return lax.dot_general(a, b, (((0,), (0,)), ((), ())), preferred_element_type=_F32)


def _rms_r(z):
    return lax.rsqrt(jnp.mean(z * z, axis=-1, keepdims=True) + NORM_EPS)


def _rms_bwd(z, g, dn):
    r = _rms_r(z)
    t = dn * g
    dz = r * t - z * (r * r * r * jnp.mean(t * z, axis=-1, keepdims=True))
    return dz, dn * z * r


def _rope(t, cs, sn):
    return t * cs + pltpu.roll(t, HEAD_D // 2, 1) * sn


def _rope_t(t, cs, sn):
    return t * cs - pltpu.roll(t, HEAD_D // 2, 1) * sn


def _sigmoid(z):
    return 1.0 / (1.0 + jnp.exp(-z))


_GELU_C = math.sqrt(2.0 / math.pi)


def _gelu(z):
    return 0.5 * z * (1.0 + jnp.tanh(_GELU_C * (z + 0.044715 * z * z * z)))


def _gelu_grad(z):
    th = jnp.tanh(_GELU_C * (z + 0.044715 * z * z * z))
    return 0.5 * (1.0 + th) + 0.5 * z * (1.0 - th * th) * _GELU_C * (1.0 + 3 * 0.044715 * z * z)


ROW_CHUNK = 256


def _row_chunks(tm):
    return [pl.ds(i, min(ROW_CHUNK, tm)) for i in range(0, tm, ROW_CHUNK)]


def _ordered(body, in_specs, operands, after):
    k = len(after)
    if not k:
        return body, list(in_specs), tuple(operands)
    return ((lambda *refs: body(*refs[k:])), [pl.BlockSpec(memory_space=pl.ANY)] * k + list(in_specs),
            tuple(after) + tuple(operands))


def _row_spec(tm, n):
    return pl.BlockSpec((tm, n), lambda i: (i, 0))


def _full_spec(shape):
    nd = len(shape)
    return pl.BlockSpec(shape, lambda *_: (0,) * nd)


def _weight_spec(shape):
    nd = len(shape)
    return pl.BlockSpec(shape, lambda *_: (0,) * nd, pipeline_mode=pl.Buffered(1))


def _rope_tables(L):
    half = HEAD_D // 2
    inv_freq = ROPE_BASE ** (-jnp.arange(half, dtype=_F32) / half)
    twice = lambda t: jnp.concatenate([t, t], axis=-1)
    off = jnp.arange(ROPE_CHUNK, dtype=_F32)[:, None] * inv_freq[None, :]
    start = (ROPE_CHUNK * jnp.arange(L // ROPE_CHUNK, dtype=_F32))[:, None] * inv_freq[None, :]
    return (twice(jnp.cos(off)), twice(jnp.sin(off)),
            twice(jnp.cos(start))[:, None, :], twice(jnp.sin(start))[:, None, :])


def _prenorm(x, g, tm, after=()):
    L = x.shape[0]

    def body(x_ref, g_ref, h_ref):
        xv = x_ref[...]
        h_ref[...] = (xv * _rms_r(xv) * g_ref[...]).astype(_BF)

    body, in_specs, operands = _ordered(body, [_row_spec(tm, D_MODEL), _full_spec((1, D_MODEL))], (x, g), after)
    return pl.pallas_call(
        body, name="prenorm", grid=(L // tm,),
        in_specs=in_specs, out_specs=_row_spec(tm, D_MODEL),
        out_shape=jax.ShapeDtypeStruct((L, D_MODEL), _BF),
        compiler_params=_params("parallel"),
    )(*operands)


def _inproj_fwd(h, w_in_t, rope, tm):
    L = h.shape[0]
    n_chunks = tm // ROPE_CHUNK

    def body(h_ref, w_ref, co_ref, so_ref, cs_ref, ss_ref, q_ref, k_ref, v_ref, gate_ref, u_ref, cos_ref, sin_ref):
        proj = _dot_nt(h_ref[...], w_ref[...])
        lane = lax.broadcasted_iota(jnp.int32, (ROPE_CHUNK, HEAD_D), 1)
        sign = jnp.where(lane < HEAD_D // 2, -1.0, 1.0)
        co, so = co_ref[...], so_ref[...]
        for c in range(n_chunks):
            chunk = pl.program_id(0) * n_chunks + c
            cst, sst = cs_ref[chunk], ss_ref[chunk]
            rows = slice(c * ROPE_CHUNK, (c + 1) * ROPE_CHUNK)
            cs = co * cst - so * sst
            sn = (so * cst + co * sst) * sign
            cos_ref[rows, :] = cs
            sin_ref[rows, :] = sn
            for hh in range(N_HEAD):
                lo = hh * HEAD_D
                q_ref[rows, lo:lo + HEAD_D] = _rope(proj[rows, lo:lo + HEAD_D], cs, sn).astype(_BF)
                kh = _rope(proj[rows, RET_W + lo:RET_W + lo + HEAD_D], cs, sn) * (HEAD_D ** -0.5)
                k_ref[rows, lo:lo + HEAD_D] = kh.astype(_BF)
        v_ref[...] = proj[:, 2 * RET_W:3 * RET_W].astype(_BF)
        gate_ref[...] = proj[:, 3 * RET_W:4 * RET_W]
        u_ref[...] = proj[:, 4 * RET_W:]

    nc = L // ROPE_CHUNK
    return pl.pallas_call(
        body, name="inproj_fwd", grid=(L // tm,),
        in_specs=[_row_spec(tm, D_MODEL), _weight_spec((IN_COLS, D_MODEL)),
                  _full_spec((ROPE_CHUNK, HEAD_D)), _full_spec((ROPE_CHUNK, HEAD_D)),
                  _full_spec((nc, 1, HEAD_D)), _full_spec((nc, 1, HEAD_D))],
        out_specs=[_row_spec(tm, RET_W)] * 5 + [_row_spec(tm, HEAD_D)] * 2,
        out_shape=[jax.ShapeDtypeStruct((L, RET_W), _BF)] * 3 + [jax.ShapeDtypeStruct((L, RET_W), _F32)] * 2
        + [jax.ShapeDtypeStruct((L, HEAD_D), _F32)] * 2,
        compiler_params=_params("parallel"),
    )(h, w_in_t, *rope)


def _ret_consts():
    lg = jnp.log(1.0 - jnp.exp(jnp.linspace(math.log(1.0 / 32), math.log(1.0 / 512), N_HEAD))).astype(_F32)
    idx = jnp.arange(CHUNK, dtype=_F32)
    diff = idx[:, None] - idx[None, :]
    decay = jnp.where(diff[None] >= 0, jnp.exp(jnp.maximum(diff, 0.0)[None] * lg[:, None, None]), 0.0)
    zeta = jnp.exp((CHUNK - 1 - idx)[None, :] * lg[:, None])
    xi = jnp.exp((idx + 1.0)[None, :] * lg[:, None])
    gc = jnp.exp(CHUNK * lg)
    wide = lambda t: jnp.broadcast_to(t[:, :, None], (N_HEAD, CHUNK, HEAD_D)).astype(_F32)
    gcw = jnp.broadcast_to(gc[:, None, None], (N_HEAD, SUBLANES, HEAD_D)).astype(_F32)
    return decay.astype(_F32), wide(xi), wide(zeta), gcw


def _head_specs():
    wide = _full_spec((N_HEAD, CHUNK, HEAD_D))
    return [_full_spec((N_HEAD, CHUNK, CHUNK)), wide, wide, _full_spec((N_HEAD, SUBLANES, HEAD_D))]


def _retention_fwd(q, k, v, gate, ggn, consts):
    L = q.shape[0]
    nc = L // CHUNK
    cps = math.gcd(RET_STEP_CHUNKS, nc)
    blk = pl.BlockSpec((cps * CHUNK, RET_W), lambda n: (n, 0))

    def body(q_ref, k_ref, v_ref, gate_ref, ggn_ref, dm_ref, xi_ref, zeta_ref, gc_ref,
             o_ref, y_ref, rp_ref, r_scr):
        @pl.when(pl.program_id(0) == 0)
        def _():
            r_scr[...] = jnp.zeros_like(r_scr)

        for hh in range(N_HEAD):
            cols = slice(hh * HEAD_D, (hh + 1) * HEAD_D)
            state = r_scr[hh]
            for c in range(cps):
                rows = slice(c * CHUNK, (c + 1) * CHUNK)
                qv, kv, vv = q_ref[rows, cols], k_ref[rows, cols], v_ref[rows, cols]
                s = _dot_nt(qv, kv) * dm_ref[hh]
                o = _dot(s.astype(_BF), vv) + _dot(qv, state.astype(_BF)) * xi_ref[hh]
                o_ref[rows, cols] = o
                rp_ref[hh, c] = state
                vz = (vv.astype(_F32) * zeta_ref[hh]).astype(_BF)
                state = gc_ref[hh, 0:1, :] * state + _dot_tn(kv, vz)
                dlt = o - jnp.mean(o, axis=-1, keepdims=True)
                on = dlt * lax.rsqrt(jnp.mean(dlt * dlt, axis=-1, keepdims=True) + NORM_EPS)
                gt = gate_ref[rows, cols]
                y_ref[rows, cols] = (gt * _sigmoid(gt) * (on * ggn_ref[:, cols])).astype(_BF)
            r_scr[hh] = state

    return pl.pallas_call(
        body, name="retention_fwd", grid=(nc // cps,),
        in_specs=[blk, blk, blk, blk, _full_spec((1, RET_W))] + _head_specs(),
        out_specs=[blk, blk, pl.BlockSpec((N_HEAD, cps, HEAD_D, HEAD_D), lambda n: (0, n, 0, 0))],
        out_shape=[jax.ShapeDtypeStruct((L, RET_W), _F32), jax.ShapeDtypeStruct((L, RET_W), _BF),
                   jax.ShapeDtypeStruct((N_HEAD, nc, HEAD_D, HEAD_D), _F32)],
        scratch_shapes=[pltpu.VMEM((N_HEAD, HEAD_D, HEAD_D), _F32)],
        compiler_params=_params("arbitrary"),
    )(q, k, v, gate, ggn, *consts)


def _rows_to_segments(dst_scr, src_ref, seg):
    for g in range(dst_scr.shape[0]):
        for j in range(SUBLANES):
            dst_scr[g, pl.ds(j, seg, stride=SUBLANES), :] = src_ref[pl.ds(j * seg, seg), g * LANES:(g + 1) * LANES]


def _segments_to_rows(dst_ref, src_scr, seg):
    for g in range(src_scr.shape[0]):
        for j in range(SUBLANES):
            dst_ref[pl.ds(j * seg, seg), g * LANES:(g + 1) * LANES] = src_scr[g, pl.ds(j, seg, stride=SUBLANES), :]


def _scan_segments(x_ref, tab_ref, pw_ref, carry_ref, seg, reverse, entry_ref=None, fwd_ref=None, fwd_entry_ref=None,
                   da_ref=None):
    G = x_ref.shape[0]
    W = KB_STATES
    re, im = pl.ds(0, W), pl.ds(W, W)
    row_id = lax.broadcasted_iota(jnp.int32, (SUBLANES, W), 0)
    edge_in = (row_id == SUBLANES - 1) if reverse else (row_id == 0)
    edge_out = 0 if reverse else SUBLANES - 1
    a_tab = [(tab_ref[g, 0], tab_ref[g, 1]) for g in range(G)]

    def local(i, st):
        r = (seg - 1 - i) if reverse else i
        out = []
        for g in range(G):
            (ar, ai), (sr, si) = a_tab[g], st[g]
            nr = ar * sr - ai * si + x_ref[g, r, :, re]
            ni = ar * si + ai * sr + x_ref[g, r, :, im]
            x_ref[g, r, :, re] = nr
            x_ref[g, r, :, im] = ni
            out.append((nr, ni))
        return tuple(out)

    zero = jnp.zeros((SUBLANES, W), _F32)
    ends = lax.fori_loop(0, seg, local, tuple((zero, zero) for _ in range(G)), unroll=SCAN_UNROLL)

    entry = []
    shift = (SUBLANES - 1) if reverse else 1
    for g in range(G):
        er, ei = ends[g]
        fr = jnp.where(edge_in, carry_ref[g, :, re], pltpu.roll(er, shift, 0))
        fi = jnp.where(edge_in, carry_ref[g, :, im], pltpu.roll(ei, shift, 0))
        for j, dist in enumerate((1, 2, 4)):
            pr, pi = tab_ref[g, 2 + 2 * j], tab_ref[g, 3 + 2 * j]
            sh = (SUBLANES - dist) if reverse else dist
            sr, si = pltpu.roll(fr, sh, 0), pltpu.roll(fi, sh, 0)
            fr, fi = fr + pr * sr - pi * si, fi + pr * si + pi * sr
        br, bi = tab_ref[g, 8], tab_ref[g, 9]
        outr = br * fr - bi * fi + er
        outi = br * fi + bi * fr + ei
        carry_ref[g, :, re] = jnp.broadcast_to(outr[edge_out:edge_out + 1, :], (SUBLANES, W))
        carry_ref[g, :, im] = jnp.broadcast_to(outi[edge_out:edge_out + 1, :], (SUBLANES, W))
        entry.append((fr, fi))
        if entry_ref is not None:
            entry_ref[g, :, re] = fr
            entry_ref[g, :, im] = fi

    add_da = da_ref is not None

    def fix(r, st, first=False):
        out = []
        for g in range(G):
            fr, fi = entry[g]
            pwr, pwi = pw_ref[g, r, :, re], pw_ref[g, r, :, im]
            xr = x_ref[g, r, :, re] + (pwr * fr - pwi * fi)
            xi = x_ref[g, r, :, im] + (pwr * fi + pwi * fr)
            x_ref[g, r, :, re] = xr
            x_ref[g, r, :, im] = xi
            if add_da:
                prev = fwd_entry_ref.at[g] if first else fwd_ref.at[g, r - 1]
                xpr, xpi = prev[:, re], prev[:, im]
                out.append((st[g][0] + (xr * xpr + xi * xpi), st[g][1] + (xi * xpr - xr * xpi)))
            else:
                out.append(st[g])
        return tuple(out)

    if add_da:
        st = fix(0, tuple((zero, zero) for _ in range(G)), first=True)
        st = lax.fori_loop(1, seg, fix, st, unroll=SCAN_UNROLL)
        for g in range(G):
            da_ref[g, :, re] += st[g][0]
            da_ref[g, :, im] += st[g][1]
    else:
        lax.fori_loop(0, seg, fix, tuple((zero[0:1, 0:LANES],) for _ in range(G)), unroll=FIX_UNROLL)


def _s5_specs(seg, time=lambda t: t):
    G = KB_PER_STEP
    return dict(
        x=pl.BlockSpec((G, seg, SUBLANES, 2 * KB_STATES), lambda kb, t: (kb, time(t), 0, 0)),
        ent=pl.BlockSpec((G, 1, SUBLANES, 2 * KB_STATES), lambda kb, t: (kb, time(t), 0, 0)),
        b=pl.BlockSpec((G, LANES, 2 * KB_STATES), lambda kb, t: (kb, 0, 0)),
        c=pl.BlockSpec((G, 2 * KB_STATES, LANES), lambda kb, t: (kb, 0, 0)),
        tab=pl.BlockSpec((G, 10, SUBLANES, KB_STATES), lambda kb, t: (kb, 0, 0, 0)),
        pw=pl.BlockSpec((G, seg, 1, 2 * KB_STATES), lambda kb, t: (kb, 0, 0, 0)),
        d=pl.BlockSpec((1, G * LANES), lambda kb, t: (0, kb)),
    )


def _s5_fwd(u, bmat, cmat, tab_f, pw_f, d_skip, tb):
    L = u.shape[0]
    nt = L // tb
    seg = tb // SUBLANES
    G = KB_PER_STEP
    ucol = pl.BlockSpec((tb, G * LANES), lambda kb, t: (t, kb))
    sp = _s5_specs(seg)

    def body(u_ref, b_ref, c_ref, tab_ref, pw_ref, d_ref, s_ref, x_ref, ent_ref, up_scr, y_scr, carry_scr):
        @pl.when(pl.program_id(1) == 0)
        def _():
            carry_scr[...] = jnp.zeros_like(carry_scr)

        _rows_to_segments(up_scr, u_ref, seg)
        for g in range(G):
            x_ref[g] = _dot(up_scr[g].astype(_BF), b_ref[g]).reshape(seg, SUBLANES, 2 * KB_STATES)
        _scan_segments(x_ref, tab_ref, pw_ref, carry_scr, seg, reverse=False, entry_ref=ent_ref.at[:, 0])
        for g in range(G):
            y = _dot(x_ref[g].reshape(tb, 2 * KB_STATES).astype(_BF), c_ref[g])
            y_scr[g] = y + d_ref[:, g * LANES:(g + 1) * LANES] * up_scr[g]
        _segments_to_rows(s_ref, y_scr, seg)

    return pl.pallas_call(
        body, name="s5_fwd", grid=(N_KB // G, nt),
        in_specs=[ucol, sp["b"], sp["c"], sp["tab"], sp["pw"], sp["d"]],
        out_specs=[ucol, sp["x"], sp["ent"]],
        out_shape=[jax.ShapeDtypeStruct((L, SSM_W), _F32),
                   jax.ShapeDtypeStruct((N_KB, L // SUBLANES, SUBLANES, 2 * KB_STATES), _F32),
                   jax.ShapeDtypeStruct((N_KB, nt, SUBLANES, 2 * KB_STATES), _F32)],
        scratch_shapes=[pltpu.VMEM((G, tb, LANES), _F32)] * 2 + [pltpu.VMEM((G, SUBLANES, 2 * KB_STATES), _F32)],
        compiler_params=_params("parallel", "arbitrary"),
    )(u, bmat, cmat, tab_f, pw_f, d_skip)


def _mixout_fwd(s, y_ret, x, w_glu, w_out, g2, tm):
    L = s.shape[0]

    def body(s_ref, yr_ref, x_ref, wg_ref, wo_ref, g_ref, ys_ref, glu_ref, cat_ref, mix_ref, x2_ref):
        for rows in _row_chunks(tm):
            ys = _gelu(s_ref[rows, :]).astype(_BF)
            ys_ref[rows, :] = ys
            glu = _dot(ys, wg_ref[...])
            glu_ref[rows, :] = glu
            cat_ref[rows, :RET_W] = yr_ref[rows, :]
            cat_ref[rows, RET_W:] = (glu[:, :SSM_W] * _sigmoid(glu[:, SSM_W:])).astype(_BF)
            mix = _dot(cat_ref[rows, :], wo_ref[...])
            mix_ref[rows, :] = mix
            x2_ref[rows, :] = x_ref[rows, :] + mix * _rms_r(mix) * g_ref[...]

    return pl.pallas_call(
        body, name="mixout_fwd", grid=(L // tm,),
        in_specs=[_row_spec(tm, SSM_W), _row_spec(tm, RET_W), _row_spec(tm, D_MODEL),
                  _weight_spec((SSM_W, 2 * SSM_W)), _weight_spec((D_MODEL, D_MODEL)), _full_spec((1, D_MODEL))],
        out_specs=[_row_spec(tm, SSM_W), _row_spec(tm, 2 * SSM_W), _row_spec(tm, D_MODEL),
                   _row_spec(tm, D_MODEL), _row_spec(tm, D_MODEL)],
        out_shape=[jax.ShapeDtypeStruct((L, SSM_W), _BF), jax.ShapeDtypeStruct((L, 2 * SSM_W), _F32),
                   jax.ShapeDtypeStruct((L, D_MODEL), _BF), jax.ShapeDtypeStruct((L, D_MODEL), _F32),
                   jax.ShapeDtypeStruct((L, D_MODEL), _F32)],
        compiler_params=_params("parallel"),
    )(s, y_ret, x, w_glu, w_out, g2)


FF1_COLS = D_FF // N_DEV


def _ff1_fwd(x2, g3, w1, tm):
    L = x2.shape[0]

    def body(x_ref, g_ref, w_ref, h_ref, f_ref):
        for rows in _row_chunks(tm):
            xv = x_ref[rows, :]
            h = (xv * _rms_r(xv) * g_ref[...]).astype(_BF)
            h_ref[rows, :] = h
            for j in range(N_DEV):
                f_ref[rows, j * FF1_COLS:(j + 1) * FF1_COLS] = _dot(h, w_ref[j])

    return pl.pallas_call(
        body, name="ff1_fwd", grid=(L // tm,),
        in_specs=[_row_spec(tm, D_MODEL), _full_spec((1, D_MODEL)), _weight_spec((N_DEV, D_MODEL, FF1_COLS))],
        out_specs=[_row_spec(tm, D_MODEL), _row_spec(tm, D_FF)],
        out_shape=[jax.ShapeDtypeStruct((L, D_MODEL), _BF), jax.ShapeDtypeStruct((L, D_FF), _F32)],
        compiler_params=_params("parallel"),
    )(x2, g3, w1)


def _ff2_loss(f1, x2, tgt, g4, w2, tm):
    L = f1.shape[0]

    def body(f_ref, x_ref, t_ref, g_ref, w_ref, dy_ref, dm_ref, dg_ref, ls_ref):
        @pl.when(pl.program_id(0) == 0)
        def _():
            dg_ref[...] = jnp.zeros_like(dg_ref)
            ls_ref[...] = jnp.zeros_like(ls_ref)

        g = g_ref[...]
        for rows in _row_chunks(tm):
            rl = jnp.maximum(f_ref[rows, :], 0.0)
            m = _dot((rl * rl).astype(_BF), w_ref[...])
            y = x_ref[rows, :] + m * _rms_r(m) * g
            err = y - t_ref[rows, :]
            ls_ref[...] += jnp.sum(err * err, axis=0, keepdims=True)
            dy = err * (1.0 / D_MODEL)
            dy_ref[rows, :] = dy
            dm, dgr = _rms_bwd(m, g, dy)
            dm_ref[rows, :] = dm.astype(_BF)
            dg_ref[...] += jnp.sum(dgr, axis=0, keepdims=True)

    return pl.pallas_call(
        body, name="ff2_loss", grid=(L // tm,),
        in_specs=[_row_spec(tm, D_FF), _row_spec(tm, D_MODEL), _row_spec(tm, D_MODEL),
                  _full_spec((1, D_MODEL)), _weight_spec((D_FF, D_MODEL))],
        out_specs=[_row_spec(tm, D_MODEL), _row_spec(tm, D_MODEL), _full_spec((1, D_MODEL)), _full_spec((1, D_MODEL))],
        out_shape=[jax.ShapeDtypeStruct((L, D_MODEL), _F32), jax.ShapeDtypeStruct((L, D_MODEL), _BF),
                   jax.ShapeDtypeStruct((1, D_MODEL), _F32), jax.ShapeDtypeStruct((1, D_MODEL), _F32)],
        compiler_params=_params("arbitrary"),
    )(f1, x2, tgt, g4, w2)


def _ff2_bwd(dm, f1, w2, tm, tn):
    L = dm.shape[0]
    last = L // tm - 1

    def body(dm_ref, f_ref, w_ref, df_ref, dw_ref, acc):
        @pl.when(pl.program_id(1) == 0)
        def _():
            acc[...] = jnp.zeros_like(acc)

        dmv = dm_ref[...]
        rl = jnp.maximum(f_ref[...], 0.0)
        df_ref[...] = (_dot_nt(dmv, w_ref[...]) * (2.0 * rl)).astype(_BF)
        acc[...] += _dot_tn((rl * rl).astype(_BF), dmv)

        @pl.when(pl.program_id(1) == last)
        def _():
            dw_ref[...] = acc[...].astype(_BF)

    return pl.pallas_call(
        body, name="ff2_bwd", grid=(D_FF // tn, L // tm),
        in_specs=[pl.BlockSpec((tm, D_MODEL), lambda j, i: (i, 0)), pl.BlockSpec((tm, tn), lambda j, i: (i, j)),
                  pl.BlockSpec((tn, D_MODEL), lambda j, i: (j, 0))],
        out_specs=[pl.BlockSpec((tm, tn), lambda j, i: (i, j)), pl.BlockSpec((tn, D_MODEL), lambda j, i: (j, 0))],
        out_shape=[jax.ShapeDtypeStruct((L, D_FF), _BF), jax.ShapeDtypeStruct((D_FF, D_MODEL), _BF)],
        scratch_shapes=[pltpu.VMEM((tn, D_MODEL), _F32)],
        compiler_params=_params("parallel", "arbitrary"),
    )(dm, f1, w2)


def _ff1_bwd(df1, w1, x2, mix, dy, g3, g2, tm):
    L = df1.shape[0]

    def body(df_ref, w_ref, x2_ref, mix_ref, dy_ref, g3_ref, g2_ref, dx2_ref, dmix_ref, dg3_ref, dg2_ref):
        @pl.when(pl.program_id(0) == 0)
        def _():
            dg3_ref[...] = jnp.zeros_like(dg3_ref)
            dg2_ref[...] = jnp.zeros_like(dg2_ref)

        for rows in _row_chunks(tm):
            dh = _dot_nt(df_ref[rows, 0:FF1_COLS], w_ref[0])
            for j in range(1, N_DEV):
                dh = dh + _dot_nt(df_ref[rows, j * FF1_COLS:(j + 1) * FF1_COLS], w_ref[j])
            dz, dgr = _rms_bwd(x2_ref[rows, :], g3_ref[...], dh)
            dg3_ref[...] += jnp.sum(dgr, axis=0, keepdims=True)
            dx2 = dy_ref[rows, :] + dz
            dx2_ref[rows, :] = dx2
            dmx, dgr2 = _rms_bwd(mix_ref[rows, :], g2_ref[...], dx2)
            dg2_ref[...] += jnp.sum(dgr2, axis=0, keepdims=True)
            dmix_ref[rows, :] = dmx.astype(_BF)

    vec = _full_spec((1, D_MODEL))
    return pl.pallas_call(
        body, name="ff1_bwd", grid=(L // tm,),
        in_specs=[_row_spec(tm, D_FF), _weight_spec((N_DEV, D_MODEL, FF1_COLS)), _row_spec(tm, D_MODEL),
                  _row_spec(tm, D_MODEL), _row_spec(tm, D_MODEL), vec, vec],
        out_specs=[_row_spec(tm, D_MODEL), _row_spec(tm, D_MODEL), vec, vec],
        out_shape=[jax.ShapeDtypeStruct((L, D_MODEL), _F32), jax.ShapeDtypeStruct((L, D_MODEL), _BF),
                   jax.ShapeDtypeStruct((1, D_MODEL), _F32), jax.ShapeDtypeStruct((1, D_MODEL), _F32)],
        compiler_params=_params("arbitrary"),
    )(df1, w1, x2, mix, dy, g3, g2)


def _matmul_tn(a, b, tm, tn, name, slots=False):
    L, K = a.shape
    N = b.shape[1]
    last = L // tm - 1

    def body(a_ref, b_ref, o_ref, acc):
        @pl.when(pl.program_id(1) == 0)
        def _():
            acc[...] = jnp.zeros_like(acc)

        acc[...] += _dot_tn(a_ref[...].astype(_BF), b_ref[...].astype(_BF))

        @pl.when(pl.program_id(1) == last)
        def _():
            if slots:
                o_ref[0] = acc[...].astype(_BF)
            else:
                o_ref[...] = acc[...].astype(_BF)

    if slots:
        out_spec = pl.BlockSpec((1, K, tn), lambda j, i: (j, 0, 0))
        out_shape = jax.ShapeDtypeStruct((N // tn, K, tn), _BF)
    else:
        out_spec = pl.BlockSpec((K, tn), lambda j, i: (0, j))
        out_shape = jax.ShapeDtypeStruct((K, N), _BF)
    return pl.pallas_call(
        body, name=name, grid=(N // tn, L // tm),
        in_specs=[pl.BlockSpec((tm, K), lambda j, i: (i, 0)), pl.BlockSpec((tm, tn), lambda j, i: (i, j))],
        out_specs=out_spec, out_shape=out_shape,
        scratch_shapes=[pltpu.VMEM((K, tn), _F32)],
        compiler_params=_params("parallel", "arbitrary"),
    )(a, b)


def _dw_in_t(pieces, h, tk):
    L = h.shape[0]
    last = L // tk - 1

    def body(p0, p1, p2, p3, p4, h_ref, o_ref, acc):
        @pl.when(pl.program_id(0) == 0)
        def _():
            acc[...] = jnp.zeros_like(acc)

        hv = h_ref[...]
        for j, p in enumerate((p0, p1, p2, p3, p4)):
            acc[j * RET_W:(j + 1) * RET_W, :] += _dot_tn(p[...].astype(_BF), hv)

        @pl.when(pl.program_id(0) == last)
        def _():
            o_ref[...] = acc[...].astype(_BF)

    return pl.pallas_call(
        body, name="dw_in", grid=(L // tk,),
        in_specs=[_row_spec(tk, RET_W)] * 5 + [_row_spec(tk, D_MODEL)],
        out_specs=_full_spec((IN_COLS, D_MODEL)), out_shape=jax.ShapeDtypeStruct((IN_COLS, D_MODEL), _BF),
        scratch_shapes=[pltpu.VMEM((IN_COLS, D_MODEL), _F32)],
        compiler_params=_params("arbitrary"),
    )(*pieces, h)


def _mixout_bwd(dmix, w_out, w_glu, glu, s, o, gate, ggn, tm, after=()):
    L = dmix.shape[0]

    def body(dmix_ref, wo_ref, wg_ref, glu_ref, s_ref, o_ref, gate_ref, ggn_ref,
             dglu_ref, ds_ref, dgate_ref, do_ref, dggn_ref):
        @pl.when(pl.program_id(0) == 0)
        def _():
            dggn_ref[...] = jnp.zeros_like(dggn_ref)

        ggn = ggn_ref[...]
        for rows in _row_chunks(tm):
            dcat = _dot_nt(dmix_ref[rows, :], wo_ref[...])
            dy_ret, dy_ssm = dcat[:, :RET_W], dcat[:, RET_W:]
            glu = glu_ref[rows, :]
            ga, sg = glu[:, :SSM_W], _sigmoid(glu[:, SSM_W:])
            dga = (dy_ssm * sg).astype(_BF)
            dgb = (dy_ssm * ga * sg * (1.0 - sg)).astype(_BF)
            dglu_ref[rows, :SSM_W] = dga
            dglu_ref[rows, SSM_W:] = dgb
            dys = _dot_nt(dga, wg_ref[:, :SSM_W]) + _dot_nt(dgb, wg_ref[:, SSM_W:])
            ds_ref[rows, :] = dys * _gelu_grad(s_ref[rows, :])
            gt = gate_ref[rows, :]
            sgt = _sigmoid(gt)
            for hh in range(N_HEAD):
                cols = slice(hh * HEAD_D, (hh + 1) * HEAD_D)
                ov = o_ref[rows, cols]
                dlt = ov - jnp.mean(ov, axis=-1, keepdims=True)
                rstd = lax.rsqrt(jnp.mean(dlt * dlt, axis=-1, keepdims=True) + NORM_EPS)
                on = dlt * rstd
                dyr = dy_ret[:, cols] * (gt[:, cols] * sgt[:, cols])
                dgate_ref[rows, cols] = dy_ret[:, cols] * (on * ggn[:, cols]) * (sgt[:, cols] * (1.0 + gt[:, cols] * (1.0 - sgt[:, cols])))
                dggn_ref[:, cols] += jnp.sum(dyr * on, axis=0, keepdims=True)
                don = dyr * ggn[:, cols]
                do = rstd * (don - jnp.mean(don, axis=-1, keepdims=True) - on * jnp.mean(don * on, axis=-1, keepdims=True))
                do_ref[rows, cols] = do.astype(_BF)

    body, in_specs, operands = _ordered(
        body, [_row_spec(tm, D_MODEL), _weight_spec((D_MODEL, D_MODEL)), _weight_spec((SSM_W, 2 * SSM_W)),
               _row_spec(tm, 2 * SSM_W), _row_spec(tm, SSM_W), _row_spec(tm, RET_W), _row_spec(tm, RET_W),
               _full_spec((1, RET_W))], (dmix, w_out, w_glu, glu, s, o, gate, ggn), after)
    return pl.pallas_call(
        body, name="mixout_bwd", grid=(L // tm,),
        in_specs=in_specs,
        out_specs=[_row_spec(tm, 2 * SSM_W), _row_spec(tm, SSM_W), _row_spec(tm, RET_W), _row_spec(tm, RET_W),
                   _full_spec((1, RET_W))],
        out_shape=[jax.ShapeDtypeStruct((L, 2 * SSM_W), _BF), jax.ShapeDtypeStruct((L, SSM_W), _F32),
                   jax.ShapeDtypeStruct((L, RET_W), _F32), jax.ShapeDtypeStruct((L, RET_W), _BF),
                   jax.ShapeDtypeStruct((1, RET_W), _F32)],
        compiler_params=_params("arbitrary"),
    )(*operands)


def _s5_bwd(u, ds, xs, ent, bmat, cmat, tab_r, pw_r, d_skip, tb, after=()):
    L = u.shape[0]
    nt = L // tb
    seg = tb // SUBLANES
    G = KB_PER_STEP
    rcol = pl.BlockSpec((tb, G * LANES), lambda kb, t: (nt - 1 - t, kb))
    sp = _s5_specs(seg, time=lambda t: nt - 1 - t)
    aspec = pl.BlockSpec((G, SUBLANES, 2 * KB_STATES), lambda kb, t: (kb, 0, 0))

    def body(u_ref, ds_ref, x_ref, ent_ref, b_ref, c_ref, tr_ref, pr_ref, d_ref,
             du_ref, db_ref, dc_ref, da_ref, dd_ref, up_scr, dp_scr, g_scr, lc_scr):
        @pl.when(pl.program_id(1) == 0)
        def _():
            lc_scr[...] = jnp.zeros_like(lc_scr)
            db_ref[...] = jnp.zeros_like(db_ref)
            dc_ref[...] = jnp.zeros_like(dc_ref)
            da_ref[...] = jnp.zeros_like(da_ref)
            dd_ref[...] = jnp.zeros_like(dd_ref)

        _rows_to_segments(up_scr, u_ref, seg)
        _rows_to_segments(dp_scr, ds_ref, seg)
        for g in range(G):
            g_scr[g] = _dot_nt(dp_scr[g].astype(_BF), c_ref[g]).reshape(seg, SUBLANES, 2 * KB_STATES)
        _scan_segments(g_scr, tr_ref, pr_ref, lc_scr, seg, reverse=True, fwd_ref=x_ref, fwd_entry_ref=ent_ref.at[:, 0],
                       da_ref=da_ref)
        for g in range(G):
            cols = slice(g * LANES, (g + 1) * LANES)
            uv, dsv = up_scr[g], dp_scr[g]
            ub, dsb = uv.astype(_BF), dsv.astype(_BF)
            lamb = g_scr[g].reshape(tb, 2 * KB_STATES).astype(_BF)
            db_ref[g] += _dot_tn(ub, lamb)
            dc_ref[g] += _dot_tn(dsb, x_ref[g].reshape(tb, 2 * KB_STATES).astype(_BF))
            dd_ref[:, cols] += jnp.sum(dsv * uv, axis=0, keepdims=True)
            up_scr[g] = _dot_nt(lamb, b_ref[g]) + d_ref[:, cols] * dsv
        _segments_to_rows(du_ref, up_scr, seg)

    body, in_specs, operands = _ordered(
        body, [rcol, rcol, sp["x"], sp["ent"], sp["b"], sp["c"], sp["tab"], sp["pw"], sp["d"]],
        (u, ds, xs, ent, bmat, cmat, tab_r, pw_r, d_skip), after)
    return pl.pallas_call(
        body, name="s5_bwd", grid=(N_KB // G, nt),
        in_specs=in_specs,
        out_specs=[rcol, sp["b"], sp["b"], aspec, sp["d"]],
        out_shape=[jax.ShapeDtypeStruct((L, SSM_W), _F32),
                   jax.ShapeDtypeStruct((N_KB, LANES, 2 * KB_STATES), _F32),
                   jax.ShapeDtypeStruct((N_KB, LANES, 2 * KB_STATES), _F32),
                   jax.ShapeDtypeStruct((N_KB, SUBLANES, 2 * KB_STATES), _F32),
                   jax.ShapeDtypeStruct((1, SSM_W), _F32)],
        scratch_shapes=[pltpu.VMEM((G, tb, LANES), _F32)] * 2
        + [pltpu.VMEM((G, seg, SUBLANES, 2 * KB_STATES), _F32), pltpu.VMEM((G, SUBLANES, 2 * KB_STATES), _F32)],
        compiler_params=_params("parallel", "arbitrary"),
    )(*operands)


def _retention_bwd(q, k, v, do, r_prev, consts, cosf, sinf, after=()):
    L = q.shape[0]
    nc = L // CHUNK
    cps = math.gcd(RET_STEP_CHUNKS, nc)
    nb = nc // cps
    blk = pl.BlockSpec((cps * CHUNK, RET_W), lambda n: (nb - 1 - n, 0))
    rope_blk = pl.BlockSpec((cps * CHUNK, HEAD_D), lambda n: (nb - 1 - n, 0))

    def body(q_ref, k_ref, v_ref, do_ref, rp_ref, dm_ref, xi_ref, zeta_ref, gc_ref, cos_ref, sin_ref,
             dq_ref, dk_ref, dv_ref, g_scr):
        @pl.when(pl.program_id(0) == 0)
        def _():
            g_scr[...] = jnp.zeros_like(g_scr)

        for hh in range(N_HEAD):
            cols = slice(hh * HEAD_D, (hh + 1) * HEAD_D)
            dm, zeta = dm_ref[hh], zeta_ref[hh]
            gst = g_scr[hh]
            for c in reversed(range(cps)):
                rows = slice(c * CHUNK, (c + 1) * CHUNK)
                qv, kv, vv, dov = q_ref[rows, cols], k_ref[rows, cols], v_ref[rows, cols], do_ref[rows, cols]
                rb = rp_ref[hh, c].astype(_BF)
                gb = gst.astype(_BF)
                sb = (_dot_nt(qv, kv) * dm).astype(_BF)
                dab = (_dot_nt(dov, vv) * dm).astype(_BF)
                dox = (dov.astype(_F32) * xi_ref[hh]).astype(_BF)
                vz = (vv.astype(_F32) * zeta).astype(_BF)
                dq = _dot(dab, kv) + _dot_nt(dox, rb)
                dk = _dot_tn(dab, qv) + _dot_nt(vz, gb)
                dv = _dot_tn(sb, dov) + _dot(kv, gb) * zeta
                gst = gc_ref[hh, 0:1, :] * gst + _dot_tn(qv, dox)
                cs, sn = cos_ref[rows, :], sin_ref[rows, :]
                dq_ref[rows, cols] = _rope_t(dq, cs, sn).astype(_BF)
                dk_ref[rows, cols] = (_rope_t(dk, cs, sn) * (HEAD_D ** -0.5)).astype(_BF)
                dv_ref[rows, cols] = dv.astype(_BF)
            g_scr[hh] = gst

    body, in_specs, operands = _ordered(
        body, [blk, blk, blk, blk, pl.BlockSpec((N_HEAD, cps, HEAD_D, HEAD_D), lambda n: (0, nb - 1 - n, 0, 0))]
        + _head_specs() + [rope_blk, rope_blk], (q, k, v, do, r_prev, *consts, cosf, sinf), after)
    return pl.pallas_call(
        body, name="retention_bwd", grid=(nb,),
        in_specs=in_specs,
        out_specs=[blk, blk, blk],
        out_shape=[jax.ShapeDtypeStruct((L, RET_W), _BF)] * 3,
        scratch_shapes=[pltpu.VMEM((N_HEAD, HEAD_D, HEAD_D), _F32)],
        compiler_params=_params("arbitrary"),
    )(*operands)


def _inproj_bwd(pieces, w_in_t, x, dx2, g1, tm, after=()):
    L = x.shape[0]

    def body(p0, p1, p2, p3, p4, w_ref, x_ref, dx2_ref, g_ref, dx_ref, dg_ref):
        @pl.when(pl.program_id(0) == 0)
        def _():
            dg_ref[...] = jnp.zeros_like(dg_ref)

        for rows in _row_chunks(tm):
            dh = None
            for j, p in enumerate((p0, p1, p2, p3, p4)):
                part = _dot(p[rows, :].astype(_BF), w_ref[j * RET_W:(j + 1) * RET_W, :])
                dh = part if dh is None else dh + part
            dz, dgr = _rms_bwd(x_ref[rows, :], g_ref[...], dh)
            dx_ref[rows, :] = dx2_ref[rows, :] + dz
            dg_ref[...] += jnp.sum(dgr, axis=0, keepdims=True)

    body, in_specs, operands = _ordered(
        body, [_row_spec(tm, RET_W)] * 5 + [_weight_spec((IN_COLS, D_MODEL)), _row_spec(tm, D_MODEL),
                                             _row_spec(tm, D_MODEL), _full_spec((1, D_MODEL))],
        (*pieces, w_in_t, x, dx2, g1), after)
    return pl.pallas_call(
        body, name="inproj_bwd", grid=(L // tm,),
        in_specs=in_specs,
        out_specs=[_row_spec(tm, D_MODEL), _full_spec((1, D_MODEL))],
        out_shape=[jax.ShapeDtypeStruct((L, D_MODEL), _F32), jax.ShapeDtypeStruct((1, D_MODEL), _F32)],
        compiler_params=_params("arbitrary"),
    )(*operands)


def _sum_adamw(parts, w, m, v, tr, name):
    _, R, Cc = parts.shape

    def body(p_ref, w_ref, m_ref, v_ref, g_ref, d_ref, nm_ref, nv_ref):
        gv = p_ref[0].astype(_F32)
        for s in range(1, N_DEV):
            gv = gv + p_ref[s].astype(_F32)
        g_ref[...] = gv
        nm = ADAM_B1 * m_ref[...] + (1.0 - ADAM_B1) * gv
        nv = ADAM_B2 * v_ref[...] + (1.0 - ADAM_B2) * (gv * gv)
        m_hat = nm / (1.0 - ADAM_B1 ** ADAM_STEP)
        v_hat = nv / (1.0 - ADAM_B2 ** ADAM_STEP)
        d_ref[...] = -ADAM_LR * (m_hat / (jnp.sqrt(v_hat) + ADAM_EPS) + ADAM_WD * w_ref[...])
        nm_ref[...] = nm
        nv_ref[...] = nv

    spec = _row_spec(tr, Cc)
    return pl.pallas_call(
        body, name=name, grid=(R // tr,),
        in_specs=[pl.BlockSpec((N_DEV, tr, Cc), lambda i: (0, i, 0))] + [spec] * 3, out_specs=[spec] * 4,
        out_shape=[jax.ShapeDtypeStruct((R, Cc), _F32)] * 4,
        compiler_params=_params("parallel"),
    )(parts, w, m, v)


def _my_place():
    return lax.axis_index("x"), lax.axis_index("y"), lax.axis_index("c")


def _all_gather(blocks):
    n = len(blocks)

    def body(*refs):
        x_refs, out_refs, done_ref = refs[:n], refs[n:2 * n], refs[2 * n]
        send_sems, recv_sems, local_sems = refs[2 * n + 1:]
        done_ref[...] = jnp.zeros_like(done_ref)
        x, y, c = _my_place()
        me, sibling = (x, y, c), (x, y, 1 - c)
        chips = [(1 - x, y), (x, 1 - y), (1 - x, 1 - y)]

        def slot(a, px, py, pc):
            return out_refs[a].at[4 * px + 2 * py + pc]

        def copy(a, k, blk, to, own=False):
            return pltpu.make_async_remote_copy(
                src_ref=x_refs[a] if own else slot(a, *blk), dst_ref=slot(a, *blk),
                send_sem=send_sems.at[a, k], recv_sem=recv_sems.at[a, k], device_id=to, device_id_type=MESH)

        mine = [pltpu.make_async_copy(x_refs[a], slot(a, *me), local_sems.at[a]) for a in range(n)]
        for cp in mine:
            cp.start()
        first = []
        for a in range(n):
            first.append(copy(a, 0, me, sibling, own=True))
            first += [copy(a, 1 + j, me, (*chip, c), own=True) for j, chip in enumerate(chips)]
        for cp in first:
            cp.start()
        passed = []
        for j, chip in enumerate(chips):
            for a in range(n):
                copy(a, 1 + j, (*chip, c), me).wait_recv()
                fwd = copy(a, 4 + j, (*chip, c), sibling)
                fwd.start()
                passed.append(fwd)
        for a in range(n):
            copy(a, 0, sibling, me).wait_recv()
            for j, chip in enumerate(chips):
                copy(a, 4 + j, (*chip, 1 - c), me).wait_recv()
        for cp in first + passed:
            cp.wait_send()
        for cp in mine:
            cp.wait()

    any_spec = pl.BlockSpec(memory_space=pl.ANY)
    outs = pl.pallas_call(
        body, name="weights_all_gather",
        in_specs=[any_spec] * n, out_specs=[any_spec] * n + [pl.BlockSpec(memory_space=pltpu.VMEM)],
        out_shape=[jax.ShapeDtypeStruct((N_DEV,) + b.shape, b.dtype) for b in blocks]
        + [jax.ShapeDtypeStruct((SUBLANES, LANES), _F32)],
        scratch_shapes=[pltpu.SemaphoreType.DMA((n, 7)), pltpu.SemaphoreType.DMA((n, 7)), pltpu.SemaphoreType.DMA((n,))],
    )(*blocks)
    return outs[:n], outs[n]


def _exchange(bigs, small):
    n = len(bigs)
    r = small.shape[0]

    def body(*refs):
        in_refs, out_refs = refs[:n + 1], refs[n + 1:2 * n + 2]
        send_sems, recv_sems, local_sems = refs[2 * n + 2:]
        x, y, c = _my_place()
        me = 4 * x + 2 * y + c
        own = [pltpu.make_async_copy(in_refs[a].at[me], out_refs[a].at[me], local_sems.at[a]) for a in range(n)]
        own.append(pltpu.make_async_copy(in_refs[n], out_refs[n].at[me], local_sems.at[n]))
        for cp in own:
            cp.start()
        copies = []
        for kk in range(1, N_DEV):
            px, py, pc = x ^ (kk >> 2), y ^ ((kk >> 1) & 1), c ^ (kk & 1)
            peer = 4 * px + 2 * py + pc
            for a in range(n + 1):
                src = in_refs[a].at[peer] if a < n else in_refs[a]
                copies.append(pltpu.make_async_remote_copy(
                    src_ref=src, dst_ref=out_refs[a].at[me],
                    send_sem=send_sems.at[a, kk - 1], recv_sem=recv_sems.at[a, kk - 1],
                    device_id=(px, py, pc), device_id_type=MESH))
        for cp in copies:
            cp.start()
        for cp in copies:
            cp.wait_recv()
        for cp in copies:
            cp.wait_send()
        for cp in own:
            cp.wait()

    any_spec = pl.BlockSpec(memory_space=pl.ANY)
    outs = pl.pallas_call(
        body, name="grad_exchange",
        in_specs=[any_spec] * (n + 1), out_specs=[any_spec] * (n + 1),
        out_shape=[jax.ShapeDtypeStruct(b.shape, b.dtype) for b in bigs]
        + [jax.ShapeDtypeStruct((N_DEV, r, LANES), small.dtype)],
        scratch_shapes=[pltpu.SemaphoreType.DMA((n + 1, 7)), pltpu.SemaphoreType.DMA((n + 1, 7)),
                        pltpu.SemaphoreType.DMA((n + 1,))],
    )(*bigs, small)
    return outs[:n], outs[n]


HBM_SPEC = pl.BlockSpec(memory_space=pltpu.HBM)
SEM_SPEC = pl.BlockSpec(memory_space=pltpu.SEMAPHORE)
DATAFLOW = pltpu.SideEffectType.DATAFLOW_SIDE_EFFECTING


def _my_index():
    x, y, c = _my_place()
    return 4 * x + 2 * y + c


def _landing(own_block):
    zone = lax.empty((N_DEV,) + own_block.shape, own_block.dtype)
    return lax.dynamic_update_index_in_dim(zone, own_block, _my_index(), 0)


def _split_copies(src_refs, land_refs, send_sems, recv_sems, gather, first=0):
    x, y, c = _my_place()
    me = 4 * x + 2 * y + c
    copies = []
    for a, (src, land) in enumerate(zip(src_refs, land_refs)):
        for kk in range(1, N_DEV):
            px, py, pc = x ^ (kk >> 2), y ^ ((kk >> 1) & 1), c ^ (kk & 1)
            peer = 4 * px + 2 * py + pc
            copies.append(pltpu.make_async_remote_copy(
                src_ref=src if gather else src.at[peer], dst_ref=land.at[me],
                send_sem=send_sems.at[(first + a) * 7 + kk - 1], recv_sem=recv_sems.at[(first + a) * 7 + kk - 1],
                device_id=(px, py, pc), device_id_type=MESH))
    return copies


def _split_start(srcs, lands, gather, name):
    n = len(srcs)

    def body(*refs):
        src_refs, land_refs = refs[:n], refs[n:2 * n]
        send_sems, recv_sems = refs[2 * n], refs[2 * n + 1]
        token = refs[-1]
        for cp in _split_copies(src_refs, land_refs, send_sems, recv_sems, gather):
            cp.start()
        token[...] = jnp.zeros_like(token)

    outs = pl.pallas_call(
        body, name=name,
        out_shape=(pltpu.SemaphoreType.DMA((7 * n,)), pltpu.SemaphoreType.DMA((7 * n,)),
                   *[pltpu.HBM(t.shape, t.dtype) for t in srcs], *[pltpu.HBM(t.shape, t.dtype) for t in lands],
                   jax.ShapeDtypeStruct((SUBLANES, LANES), _F32)),
        in_specs=[HBM_SPEC] * (2 * n),
        out_specs=(SEM_SPEC, SEM_SPEC, *[HBM_SPEC] * (2 * n), pl.BlockSpec(memory_space=pltpu.VMEM)),
        input_output_aliases={i: 2 + i for i in range(2 * n)},
        compiler_params=pltpu.CompilerParams(has_side_effects=DATAFLOW),
    )(*[pltpu.with_memory_space_constraint(t, pltpu.HBM) for t in list(srcs) + list(lands)])
    return outs[0], outs[1], outs[2:2 + n], outs[2 + n:2 + 2 * n], outs[-1]


def _split_wait(send_sems, recv_sems, srcs, lands, after, gather, name, first=0):
    n = len(srcs)

    def body(*refs):
        src_refs, land_refs = refs[:n], refs[n:2 * n]
        send_s, recv_s = refs[2 * n], refs[2 * n + 1]
        for cp in _split_copies(src_refs, land_refs, send_s, recv_s, gather, first):
            cp.wait_send()
            cp.wait_recv()

    outs = pl.pallas_call(
        body, name=name,
        out_shape=tuple(pltpu.HBM(t.shape, t.dtype) for t in list(srcs) + list(lands)),
        in_specs=[HBM_SPEC] * (2 * n) + [SEM_SPEC, SEM_SPEC, pl.BlockSpec(memory_space=pl.ANY)],
        out_specs=tuple([HBM_SPEC] * (2 * n)),
        input_output_aliases={i: i for i in range(2 * n)},
        compiler_params=pltpu.CompilerParams(has_side_effects=DATAFLOW),
    )(*srcs, *lands, send_sems, recv_sems, after)
    return outs[n:]


def _discretize(lam_re, lam_im, log_dt, b_re, b_im):
    lr = jnp.minimum(lam_re, -1e-4)
    li = lam_im
    dt = jnp.exp(log_dt)[:, None]
    er = jnp.exp(lr * dt)
    ar, ai = er * jnp.cos(li * dt), er * jnp.sin(li * dt)
    den = lr * lr + li * li
    cr = ((ar - 1.0) * lr + ai * li) / den
    ci = (ai * lr - (ar - 1.0) * li) / den
    bbr = cr[:, :, None] * b_re - ci[:, :, None] * b_im
    bbi = cr[:, :, None] * b_im + ci[:, :, None] * b_re
    return ar, ai, bbr, bbi


def _cmul(ar, ai, br, bi):
    return ar * br - ai * bi, ar * bi + ai * br


def _cpowers(ar, ai, n):
    pr, pi = ar[None], ai[None]
    while pr.shape[0] < n:
        nr, ni = _cmul(pr, pi, pr[-1][None], pi[-1][None])
        pr, pi = jnp.concatenate([pr, nr]), jnp.concatenate([pi, ni])
    return pr[:n], pi[:n]


def _scan_tables(ar, ai, seg, reverse):
    if reverse:
        ai = -ai
    ar, ai = ar.reshape(N_KB, KB_STATES), ai.reshape(N_KB, KB_STATES)
    pr, pi = _cpowers(ar, ai, seg)
    a1 = (pr[-1], pi[-1])
    a2 = _cmul(*a1, *a1)
    a4 = _cmul(*a2, *a2)
    row = jnp.arange(SUBLANES)[None, :, None]
    wide = lambda t: jnp.broadcast_to(t[:, None, :], (N_KB, SUBLANES, KB_STATES))
    tabs = [wide(ar), wide(ai)]
    for dist, (qr, qi) in ((1, a1), (2, a2), (4, a4)):
        keep = (row < SUBLANES - dist) if reverse else (row >= dist)
        tabs += [jnp.where(keep, wide(qr), 0.0), jnp.where(keep, wide(qi), 0.0)]
    tabs += [wide(a1[0]), wide(a1[1])]
    if reverse:
        pr, pi = pr[::-1], pi[::-1]
    pw = jnp.transpose(jnp.concatenate([pr, pi], axis=-1), (1, 0, 2))[:, :, None, :]
    return jnp.stack(tabs, axis=1).astype(_F32), pw.astype(_F32)


def _block_diag_in(br, bi):
    eye = jnp.eye(GROUPS_PER_KB, dtype=_F32)
    one = lambda t: jnp.einsum("kgpc,gh->kgchp", t.reshape(N_KB, GROUPS_PER_KB, N_STATE, SSM_GC), eye).reshape(
        N_KB, LANES, KB_STATES)
    return jnp.concatenate([one(br), one(bi)], axis=-1)


def _block_diag_in_t(dmat):
    d6 = dmat.reshape(N_KB, GROUPS_PER_KB, SSM_GC, 2, GROUPS_PER_KB, N_STATE)
    eye = jnp.eye(GROUPS_PER_KB, dtype=_F32)
    both = jnp.einsum("kgcrhp,gh->rkgpc", d6, eye).reshape(2, N_GROUP, N_STATE, SSM_GC)
    return both[0], both[1]


def _block_diag_out(c_re, c_im):
    eye = jnp.eye(GROUPS_PER_KB, dtype=_F32)
    one = lambda t: jnp.einsum("kgcp,gh->khpgc", t.reshape(N_KB, GROUPS_PER_KB, SSM_GC, N_STATE), eye).reshape(
        N_KB, KB_STATES, LANES)
    return jnp.concatenate([one(c_re), -one(c_im)], axis=1)


def _block_diag_out_t(dmat_t):
    d6 = dmat_t.reshape(N_KB, GROUPS_PER_KB, SSM_GC, 2, GROUPS_PER_KB, N_STATE)
    eye = jnp.eye(GROUPS_PER_KB, dtype=_F32)
    both = jnp.einsum("kgcrhp,gh->rkgcp", d6, eye).reshape(2, N_GROUP, SSM_GC, N_STATE)
    return both[0], -both[1]


SMALL_NAMES = ("norm_mix_pre", "norm_mix_post", "ret_gn_gain", "ssm_lambda_re", "ssm_lambda_im", "ssm_log_dt",
               "ssm_b_re", "ssm_b_im", "ssm_c_re", "ssm_c_im", "ssm_d", "norm_mlp_pre", "norm_mlp_post")


def _local_grads(x, tgt, small, weights, emit, emit_small, tm, tk, tb, zero=0.0):
    L = x.shape[0]
    g1, g2, ggn = small["norm_mix_pre"], small["norm_mix_post"], small["ret_gn_gain"]
    g3, g4, d_skip = small["norm_mlp_pre"], small["norm_mlp_post"], small["ssm_d"]

    rope = _rope_tables(L)
    consts = _ret_consts()

    disc_in = (small["ssm_lambda_re"][0], small["ssm_lambda_im"][0], small["ssm_log_dt"][0] + zero,
               small["ssm_b_re"][0], small["ssm_b_im"][0])
    (ar, ai, bbr, bbi), disc_vjp = jax.vjp(_discretize, *disc_in)
    bmat = _block_diag_in(bbr, bbi).astype(_BF)
    cmat = _block_diag_out(small["ssm_c_re"][0], small["ssm_c_im"][0]).astype(_BF)
    seg = tb // SUBLANES
    tab_f, pw_f = _scan_tables(ar, ai, seg, False)
    tab_r, pw_r = _scan_tables(ar, ai, seg, True)

    h1 = _prenorm(x, g1, min(4 * tm, L), after=(pw_r,))
    (w_in_t,) = weights("in", h1)
    q, k, v, gate, u, cosf, sinf = _inproj_fwd(h1, w_in_t, rope, min(4 * tm, L))
    o, y_ret, r_prev = _retention_fwd(q, k, v, gate, ggn, consts)
    s, xs, ent = _s5_fwd(u, bmat, cmat, tab_f, pw_f, d_skip, tb)
    w_glu, w_out = weights("mix", s)
    ys, glu, cat, mix, x2 = _mixout_fwd(s, y_ret, x, w_glu, w_out, g2, min(4 * tm, L))
    w_ff1, w_ff2 = weights("mlp", x2)
    h3, f1 = _ff1_fwd(x2, g3, w_ff1, min(2 * tm, L))
    dy, dm, dg4, sq = _ff2_loss(f1, x2, tgt, g4, w_ff2, min(2 * tm, L))

    df1, dw_ff2 = _ff2_bwd(dm, f1, w_ff2, min(1024, L), 1024)
    dx2, dmix, dg3, dg2 = _ff1_bwd(df1, w_ff1, x2, mix, dy, g3, g2, min(2 * tm, L))
    dw_ff1 = _matmul_tn(h3, df1, tk, FF1_COLS, "dw_ff1", slots=True)
    token = emit({"w_ff1": dw_ff1, "w_ff2": dw_ff2})
    dglu, ds, dgate, do, dggn = _mixout_bwd(dmix, w_out, w_glu, glu, s, o, gate, ggn, min(4 * tm, L), after=token)
    dw_out = _matmul_tn(cat, dmix, tk, 1024, "dw_out")
    dw_glu = _matmul_tn(ys, dglu, tk, 1024, "dw_glu")
    token = emit({"w_glu": dw_glu, "w_out": dw_out})
    du, dbmat, dcmat, da8, dd = _s5_bwd(u, ds, xs, ent, bmat, cmat, tab_r, pw_r, d_skip, tb, after=token)

    da = jnp.sum(da8, axis=1)
    dar = da[:, :KB_STATES].reshape(N_GROUP, N_STATE)
    dai = da[:, KB_STATES:].reshape(N_GROUP, N_STATE)
    dbr, dbi = _block_diag_in_t(dbmat)
    dlre, dlim, dldt, dbre, dbim = disc_vjp((dar, dai, dbr, dbi))
    dcre, dcim = _block_diag_out_t(dcmat)
    token = emit_small({
        "norm_mix_post": dg2, "ret_gn_gain": dggn,
        "ssm_lambda_re": dlre[None], "ssm_lambda_im": dlim[None], "ssm_log_dt": dldt[None],
        "ssm_b_re": dbre[None], "ssm_b_im": dbim[None], "ssm_c_re": dcre[None], "ssm_c_im": dcim[None],
        "ssm_d": dd, "norm_mlp_pre": dg3, "norm_mlp_post": dg4,
    }, sq)

    dq, dk, dv = _retention_bwd(q, k, v, do, r_prev, consts, cosf, sinf, after=token)
    pieces = (dq, dk, dv, dgate, du)
    dw_in_t = _dw_in_t(pieces, h1, min(1024, L))
    token = emit({"w_in": dw_in_t})
    gx, dg1 = _inproj_bwd(pieces, w_in_t, x, dx2, g1, min(4 * tm, L), after=token)
    return gx, dg1


BIG_SHAPES = {"w_in": (D_MODEL, IN_COLS // N_DEV), "w_glu": (SSM_W, 2 * SSM_W // N_DEV), "w_out": (D_MODEL // N_DEV, D_MODEL),
              "w_ff1": (D_MODEL, FF1_COLS), "w_ff2": (D_FF // N_DEV, D_MODEL)}
BIG_NAMES = ("w_in", "w_glu", "w_out", "w_ff1", "w_ff2")


def _cols_from_slots(g):
    return jnp.transpose(g, (1, 0, 2)).reshape(g.shape[1], N_DEV * g.shape[2])


def _cols_to_slots(dw):
    r, cols = dw.shape
    return jnp.transpose(dw.reshape(r, N_DEV, cols // N_DEV), (1, 0, 2))


WEIGHT_GROUPS = {"in": ("w_in",), "mix": ("w_glu", "w_out"), "mlp": ("w_ff1", "w_ff2")}


def _weight_from_slots(name, g):
    if name == "w_glu":
        return _cols_from_slots(g)
    if name == "w_ff1":
        return g
    return g.reshape(N_DEV * g.shape[1], g.shape[2])


def _grad_slots(name, dw):
    if name == "w_glu":
        return _cols_to_slots(dw)
    if name == "w_ff1":
        return dw
    if name == "w_in":
        return dw.reshape(N_DEV, BIG_SHAPES[name][1], BIG_SHAPES[name][0])
    return dw.reshape((N_DEV,) + BIG_SHAPES[name])


PIECE_ROWS = 8


VEC_NAMES = tuple(n for n in SMALL_NAMES if n[:6] not in ("ssm_b_", "ssm_c_"))
BC_NAMES = ("ssm_b_re", "ssm_b_im", "ssm_c_re", "ssm_c_im")
BC_ROWS = N_GROUP * SSM_GC


def _bc_view(name, t):
    t = t[0]
    if name.startswith("ssm_b_"):
        t = jnp.swapaxes(t, 1, 2)
    return t.reshape(BC_ROWS, N_STATE)


def _bc_unview(name, t):
    t = t.reshape(N_GROUP, SSM_GC, N_STATE)
    if name.startswith("ssm_b_"):
        t = jnp.swapaxes(t, 1, 2)
    return t[None]


def _pack_bc(vals):
    return jnp.concatenate([_bc_view(n, vals[n]).astype(_F32) for n in BC_NAMES], axis=0)


def _unpack_bc(buf):
    return {n: _bc_unview(n, buf[j * BC_ROWS:(j + 1) * BC_ROWS]) for j, n in enumerate(BC_NAMES)}


def _small_layout(shapes):
    off, rows = {}, 0
    for n in VEC_NAMES:
        off[n] = rows
        rows += -(-math.prod(shapes[n]) // (PIECE_ROWS * LANES)) * PIECE_ROWS
    return off, rows, rows + PIECE_ROWS


def _pack_small(vals, shapes, last=None):
    parts = []
    for n in VEC_NAMES:
        flat = vals[n].reshape(-1).astype(_F32)
        pad = -flat.shape[0] % (PIECE_ROWS * LANES)
        if pad:
            flat = jnp.concatenate([flat, jnp.zeros((pad,), _F32)])
        parts.append(flat.reshape(-1, LANES))
    parts.append(jnp.zeros((PIECE_ROWS, LANES), _F32) if last is None else last)
    return jnp.concatenate(parts, axis=0)


def _unpack_small(buf, shapes):
    off, _, _ = _small_layout(shapes)
    out = {}
    for n in VEC_NAMES:
        size = math.prod(shapes[n])
        rows = -(-size // LANES)
        out[n] = buf[off[n]:off[n] + rows].reshape(-1)[:size].reshape(shapes[n])
    return out


WEIGHT_NAMES = ('norm_mix_pre', 'norm_mix_post', 'w_in', 'ret_gn_gain', 'ssm_lambda_re', 'ssm_lambda_im', 'ssm_log_dt',
                'ssm_b_re', 'ssm_b_im', 'ssm_c_re', 'ssm_c_im', 'ssm_d', 'w_glu', 'w_out', 'norm_mlp_pre',
                'norm_mlp_post', 'w_ff1', 'w_ff2')


def kernel(x, norm_mix_pre, norm_mix_post, w_in, ret_gn_gain, ssm_lambda_re, ssm_lambda_im, ssm_log_dt, ssm_b_re, ssm_b_im, ssm_c_re, ssm_c_im, ssm_d, w_glu, w_out, norm_mlp_pre, norm_mlp_post, w_ff1, w_ff2, loss_target, m_norm_mix_pre, m_norm_mix_post, m_w_in, m_ret_gn_gain, m_ssm_lambda_re, m_ssm_lambda_im, m_ssm_log_dt, m_ssm_b_re, m_ssm_b_im, m_ssm_c_re, m_ssm_c_im, m_ssm_d, m_w_glu, m_w_out, m_norm_mlp_pre, m_norm_mlp_post, m_w_ff1, m_w_ff2, v_norm_mix_pre, v_norm_mix_post, v_w_in, v_ret_gn_gain, v_ssm_lambda_re, v_ssm_lambda_im, v_ssm_log_dt, v_ssm_b_re, v_ssm_b_im, v_ssm_c_re, v_ssm_c_im, v_ssm_d, v_w_glu, v_w_out, v_norm_mlp_pre, v_norm_mlp_post, v_w_ff1, v_w_ff2):
    args = dict(locals())
    w = {n: args[n] for n in WEIGHT_NAMES}
    m = {n: args["m_" + n] for n in WEIGHT_NAMES}
    v = {n: args["v_" + n] for n in WEIGHT_NAMES}
    L = x.shape[1]
    tm = min(256, L)
    tk = min(2048, L)
    tb = min(1024, L)

    order = [n for names in WEIGHT_GROUPS.values() for n in names]
    blocks = [(w[n][0].T if n == "w_in" else w[n][0]).astype(_BF) for n in order]
    gathered = _split_start(blocks, [_landing(b) for b in blocks], True, "weights_start")
    zero = gathered[4][0, 0]

    def weights(group, after):
        names = WEIGHT_GROUPS[group]
        first = order.index(names[0])
        part = slice(first, first + len(names))
        landed = _split_wait(gathered[0], gathered[1], gathered[2][part], gathered[3][part], after, True,
                             "weights_wait_" + group, first=first)
        return [_weight_from_slots(n, g) for n, g in zip(names, landed)]

    in_flight = []

    def emit(dws):
        names = sorted(dws)
        srcs = [_grad_slots(n, dws[n]) for n in names]
        lands = [_landing(lax.dynamic_index_in_dim(t, _my_index(), 0, keepdims=False)) for t in srcs]
        started = _split_start(srcs, lands, False, "grads_start_" + "_".join(names))
        in_flight.append((names, started))
        return (started[4],)

    shapes = {n: w[n].shape for n in SMALL_NAMES}
    first_piece = {SMALL_NAMES[0]: jnp.zeros(shapes[SMALL_NAMES[0]], _F32)}
    small_flight = []

    def emit_small(gs, sq):
        loss_rows = jnp.broadcast_to(0.5 / D_MODEL * jnp.sum(sq), (PIECE_ROWS, LANES)).astype(_F32)
        bufs = [_pack_small({**first_piece, **gs}, shapes, loss_rows), _pack_bc(gs)]
        small_flight.append(_split_start(bufs, [_landing(b) for b in bufs], True, "small_grads_start"))
        return (small_flight[0][4],)

    small_w = {n: w[n] for n in SMALL_NAMES}
    gx, dg1 = _local_grads(x[0], loss_target[0], small_w, weights, emit, emit_small, tm, tk, tb, zero=zero)
    last_buf = dg1.reshape(PIECE_ROWS, LANES)
    last_started = _split_start([last_buf], [_landing(last_buf)], True, "last_grad_start")

    grads, delta, new_m, new_v = {}, {}, {}, {}
    after = last_started[4]
    for names, started in in_flight:
        landed = _split_wait(*started[:4], after, False, "grads_wait_" + "_".join(names))
        for n, parts in zip(names, landed):
            flip = (lambda t: t.T) if n == "w_in" else (lambda t: t)
            res = _sum_adamw(parts, flip(w[n][0]), flip(m[n][0]), flip(v[n][0]), math.gcd(256, parts.shape[1]), "adamw_" + n)
            grads[n], delta[n], new_m[n], new_v[n] = (flip(t)[None] for t in res)
        after = res[1]
    small_parts, bc_parts = _split_wait(*small_flight[0][:4], after, True, "small_grads_wait")
    last_parts = _split_wait(*last_started[:4], small_parts, True, "last_grad_wait")[0]
    small_parts = lax.dynamic_update_slice(small_parts, last_parts, (0, 0, 0))
    res_bc = _sum_adamw(bc_parts, _pack_bc(w), _pack_bc(m), _pack_bc(v), BC_ROWS, "adamw_bc")
    sw, sm, sv = _pack_small(w, shapes), _pack_small(m, shapes), _pack_small(v, shapes)
    res = _sum_adamw(small_parts, sw, sm, sv, sw.shape[0], "adamw_small")
    for dst, buf, buf_bc in zip((grads, delta, new_m, new_v), res, res_bc):
        dst.update(_unpack_small(buf, shapes))
        dst.update(_unpack_bc(buf_bc))
    _, loss_at, _ = _small_layout(shapes)
    loss = res[0][loss_at, 0]

    return (loss, gx[None], *[grads[n] for n in WEIGHT_NAMES], *[delta[n] for n in WEIGHT_NAMES],
            *[new_m[n] for n in WEIGHT_NAMES], *[new_v[n] for n in WEIGHT_NAMES])
```

```python
import math

import jax
import jax.numpy as jnp
from jax import lax
from jax.experimental import pallas as pl
from jax.experimental.pallas import tpu as pltpu

_BF = jnp.bfloat16
_F32 = jnp.float32

D_MODEL = 1024
RET_W = 512
N_HEAD = 4
HEAD_D = 128
CHUNK = 256
ROPE_CHUNK = 128
SSM_W = 512
SSM_GC = 16
N_GROUP = 32
N_STATE = 64
GROUPS_PER_KB = 8
N_KB = 4
KB_STATES = GROUPS_PER_KB * N_STATE
D_FF = 4096
IN_COLS = 2560
NORM_EPS = 1e-6
ROPE_BASE = 10000.0
N_DEV = 8

ADAM_LR = 0.001
ADAM_B1 = 0.9
ADAM_B2 = 0.999
ADAM_EPS = 1e-08
ADAM_WD = 0.01
ADAM_STEP = 10

SUBLANES = 8
LANES = 128
VMEM_LIMIT = 52 * 1024 * 1024
RET_STEP_CHUNKS = 2
KB_PER_STEP = 2
SCAN_UNROLL = True
FIX_UNROLL = 8

MESH = pl.DeviceIdType.MESH


def _params(*sem):
    return pltpu.CompilerParams(dimension_semantics=sem, vmem_limit_bytes=VMEM_LIMIT)


def _dot(a, b):
    return jnp.dot(a, b, preferred_element_type=_F32)


def _dot_nt(a, b):
    return lax.dot_general(a, b, (((1,), (1,)), ((), ())), preferred_element_type=_F32)


def _dot_tn(a, b):
    return lax.dot_general(a, b, (((0,), (0,)), ((), ())), preferred_element_type=_F32)


def _rms_r(z):
    return lax.rsqrt(jnp.mean(z * z, axis=-1, keepdims=True) + NORM_EPS)


def _rms_bwd(z, g, dn):
    r = _rms_r(z)
    t = dn * g
    dz = r * t - z * (r * r * r * jnp.mean(t * z, axis=-1, keepdims=True))
    return dz, dn * z * r


def _rope(t, cs, sn):
    return t * cs + pltpu.roll(t, HEAD_D // 2, 1) * sn


def _rope_t(t, cs, sn):
    return t * cs - pltpu.roll(t, HEAD_D // 2, 1) * sn


def _sigmoid(z):
    return 1.0 / (1.0 + jnp.exp(-z))


_GELU_C = math.sqrt(2.0 / math.pi)


def _gelu(z):
    return 0.5 * z * (1.0 + jnp.tanh(_GELU_C * (z + 0.044715 * z * z * z)))


def _gelu_grad(z):
    th = jnp.tanh(_GELU_C * (z + 0.044715 * z * z * z))
    return 0.5 * (1.0 + th) + 0.5 * z * (1.0 - th * th) * _GELU_C * (1.0 + 3 * 0.044715 * z * z)


ROW_CHUNK = 256


def _row_chunks(tm):
    return [pl.ds(i, min(ROW_CHUNK, tm)) for i in range(0, tm, ROW_CHUNK)]


def _ordered(body, in_specs, operands, after):
    k = len(after)
    if not k:
        return body, list(in_specs), tuple(operands)
    return ((lambda *refs: body(*refs[k:])), [pl.BlockSpec(memory_space=pl.ANY)] * k + list(in_specs),
            tuple(after) + tuple(operands))


def _row_spec(tm, n):
    return pl.BlockSpec((tm, n), lambda i: (i, 0))


def _full_spec(shape):
    nd = len(shape)
    return pl.BlockSpec(shape, lambda *_: (0,) * nd)


def _weight_spec(shape):
    nd = len(shape)
    return pl.BlockSpec(shape, lambda *_: (0,) * nd, pipeline_mode=pl.Buffered(1))


def _rope_tables(L):
    half = HEAD_D // 2
    inv_freq = ROPE_BASE ** (-jnp.arange(half, dtype=_F32) / half)
    twice = lambda t: jnp.concatenate([t, t], axis=-1)
    off = jnp.arange(ROPE_CHUNK, dtype=_F32)[:, None] * inv_freq[None, :]
    start = (ROPE_CHUNK * jnp.arange(L // ROPE_CHUNK, dtype=_F32))[:, None] * inv_freq[None, :]
    return (twice(jnp.cos(off)), twice(jnp.sin(off)),
            twice(jnp.cos(start))[:, None, :], twice(jnp.sin(start))[:, None, :])


def _prenorm(x, g, tm, after=()):
    L = x.shape[0]

    def body(x_ref, g_ref, h_ref):
        xv = x_ref[...]
        h_ref[...] = (xv * _rms_r(xv) * g_ref[...]).astype(_BF)

    body, in_specs, operands = _ordered(body, [_row_spec(tm, D_MODEL), _full_spec((1, D_MODEL))], (x, g), after)
    return pl.pallas_call(
        body, name="prenorm", grid=(L // tm,),
        in_specs=in_specs, out_specs=_row_spec(tm, D_MODEL),
        out_shape=jax.ShapeDtypeStruct((L, D_MODEL), _BF),
        compiler_params=_params("parallel"),
    )(*operands)


def _inproj_fwd(h, w_in_t, rope, tm):
    L = h.shape[0]
    n_chunks = tm // ROPE_CHUNK

    def body(h_ref, w_ref, co_ref, so_ref, cs_ref, ss_ref, q_ref, k_ref, v_ref, gate_ref, u_ref, cos_ref, sin_ref):
        proj = _dot_nt(h_ref[...], w_ref[...])
        lane = lax.broadcasted_iota(jnp.int32, (ROPE_CHUNK, HEAD_D), 1)
        sign = jnp.where(lane < HEAD_D // 2, -1.0, 1.0)
        co, so = co_ref[...], so_ref[...]
        for c in range(n_chunks):
            chunk = pl.program_id(0) * n_chunks + c
            cst, sst = cs_ref[chunk], ss_ref[chunk]
            rows = slice(c * ROPE_CHUNK, (c + 1) * ROPE_CHUNK)
            cs = co * cst - so * sst
            sn = (so * cst + co * sst) * sign
            cos_ref[rows, :] = cs
            sin_ref[rows, :] = sn
            for hh in range(N_HEAD):
                lo = hh * HEAD_D
                q_ref[rows, lo:lo + HEAD_D] = _rope(proj[rows, lo:lo + HEAD_D], cs, sn).astype(_BF)
                kh = _rope(proj[rows, RET_W + lo:RET_W + lo + HEAD_D], cs, sn) * (HEAD_D ** -0.5)
                k_ref[rows, lo:lo + HEAD_D] = kh.astype(_BF)
        v_ref[...] = proj[:, 2 * RET_W:3 * RET_W].astype(_BF)
        gate_ref[...] = proj[:, 3 * RET_W:4 * RET_W]
        u_ref[...] = proj[:, 4 * RET_W:]

    nc = L // ROPE_CHUNK
    return pl.pallas_call(
        body, name="inproj_fwd", grid=(L // tm,),
        in_specs=[_row_spec(tm, D_MODEL), _weight_spec((IN_COLS, D_MODEL)),
                  _full_spec((ROPE_CHUNK, HEAD_D)), _full_spec((ROPE_CHUNK, HEAD_D)),
                  _full_spec((nc, 1, HEAD_D)), _full_spec((nc, 1, HEAD_D))],
        out_specs=[_row_spec(tm, RET_W)] * 5 + [_row_spec(tm, HEAD_D)] * 2,
        out_shape=[jax.ShapeDtypeStruct((L, RET_W), _BF)] * 3 + [jax.ShapeDtypeStruct((L, RET_W), _F32)] * 2
        + [jax.ShapeDtypeStruct((L, HEAD_D), _F32)] * 2,
        compiler_params=_params("parallel"),
    )(h, w_in_t, *rope)


def _ret_consts():
    lg = jnp.log(1.0 - jnp.exp(jnp.linspace(math.log(1.0 / 32), math.log(1.0 / 512), N_HEAD))).astype(_F32)
    idx = jnp.arange(CHUNK, dtype=_F32)
    diff = idx[:, None] - idx[None, :]
    decay = jnp.where(diff[None] >= 0, jnp.exp(jnp.maximum(diff, 0.0)[None] * lg[:, None, None]), 0.0)
    zeta = jnp.exp((CHUNK - 1 - idx)[None, :] * lg[:, None])
    xi = jnp.exp((idx + 1.0)[None, :] * lg[:, None])
    gc = jnp.exp(CHUNK * lg)
    wide = lambda t: jnp.broadcast_to(t[:, :, None], (N_HEAD, CHUNK, HEAD_D)).astype(_F32)
    gcw = jnp.broadcast_to(gc[:, None, None], (N_HEAD, SUBLANES, HEAD_D)).astype(_F32)
    return decay.astype(_F32), wide(xi), wide(zeta), gcw


def _head_specs():
    wide = _full_spec((N_HEAD, CHUNK, HEAD_D))
    return [_full_spec((N_HEAD, CHUNK, CHUNK)), wide, wide, _full_spec((N_HEAD, SUBLANES, HEAD_D))]


def _retention_fwd(q, k, v, gate, ggn, consts):
    L = q.shape[0]
    nc = L // CHUNK
    cps = math.gcd(RET_STEP_CHUNKS, nc)
    blk = pl.BlockSpec((cps * CHUNK, RET_W), lambda n: (n, 0))

    def body(q_ref, k_ref, v_ref, gate_ref, ggn_ref, dm_ref, xi_ref, zeta_ref, gc_ref,
             o_ref, y_ref, rp_ref, r_scr):
        @pl.when(pl.program_id(0) == 0)
        def _():
            r_scr[...] = jnp.zeros_like(r_scr)

        for hh in range(N_HEAD):
            cols = slice(hh * HEAD_D, (hh + 1) * HEAD_D)
            state = r_scr[hh]
            for c in range(cps):
                rows = slice(c * CHUNK, (c + 1) * CHUNK)
                qv, kv, vv = q_ref[rows, cols], k_ref[rows, cols], v_ref[rows, cols]
                s = _dot_nt(qv, kv) * dm_ref[hh]
                o = _dot(s.astype(_BF), vv) + _dot(qv, state.astype(_BF)) * xi_ref[hh]
                o_ref[rows, cols] = o
                rp_ref[hh, c] = state
                vz = (vv.astype(_F32) * zeta_ref[hh]).astype(_BF)
                state = gc_ref[hh, 0:1, :] * state + _dot_tn(kv, vz)
                dlt = o - jnp.mean(o, axis=-1, keepdims=True)
                on = dlt * lax.rsqrt(jnp.mean(dlt * dlt, axis=-1, keepdims=True) + NORM_EPS)
                gt = gate_ref[rows, cols]
                y_ref[rows, cols] = (gt * _sigmoid(gt) * (on * ggn_ref[:, cols])).astype(_BF)
            r_scr[hh] = state

    return pl.pallas_call(
        body, name="retention_fwd", grid=(nc // cps,),
        in_specs=[blk, blk, blk, blk, _full_spec((1, RET_W))] + _head_specs(),
        out_specs=[blk, blk, pl.BlockSpec((N_HEAD, cps, HEAD_D, HEAD_D), lambda n: (0, n, 0, 0))],
        out_shape=[jax.ShapeDtypeStruct((L, RET_W), _F32), jax.ShapeDtypeStruct((L, RET_W), _BF),
                   jax.ShapeDtypeStruct((N_HEAD, nc, HEAD_D, HEAD_D), _F32)],
        scratch_shapes=[pltpu.VMEM((N_HEAD, HEAD_D, HEAD_D), _F32)],
        compiler_params=_params("arbitrary"),
    )(q, k, v, gate, ggn, *consts)


def _rows_to_segments(dst_scr, src_ref, seg):
    for g in range(dst_scr.shape[0]):
        for j in range(SUBLANES):
            dst_scr[g, pl.ds(j, seg, stride=SUBLANES), :] = src_ref[pl.ds(j * seg, seg), g * LANES:(g + 1) * LANES]


def _segments_to_rows(dst_ref, src_scr, seg):
    for g in range(src_scr.shape[0]):
        for j in range(SUBLANES):
            dst_ref[pl.ds(j * seg, seg), g * LANES:(g + 1) * LANES] = src_scr[g, pl.ds(j, seg, stride=SUBLANES), :]


def _scan_segments(x_ref, tab_ref, pw_ref, carry_ref, seg, reverse, entry_ref=None, fwd_ref=None, fwd_entry_ref=None,
                   da_ref=None):
    G = x_ref.shape[0]
    W = KB_STATES
    re, im = pl.ds(0, W), pl.ds(W, W)
    row_id = lax.broadcasted_iota(jnp.int32, (SUBLANES, W), 0)
    edge_in = (row_id == SUBLANES - 1) if reverse else (row_id == 0)
    edge_out = 0 if reverse else SUBLANES - 1
    a_tab = [(tab_ref[g, 0], tab_ref[g, 1]) for g in range(G)]

    def local(i, st):
        r = (seg - 1 - i) if reverse else i
        out = []
        for g in range(G):
            (ar, ai), (sr, si) = a_tab[g], st[g]
            nr = ar * sr - ai * si + x_ref[g, r, :, re]
            ni = ar * si + ai * sr + x_ref[g, r, :, im]
            x_ref[g, r, :, re] = nr
            x_ref[g, r, :, im] = ni
            out.append((nr, ni))
        return tuple(out)

    zero = jnp.zeros((SUBLANES, W), _F32)
    ends = lax.fori_loop(0, seg, local, tuple((zero, zero) for _ in range(G)), unroll=SCAN_UNROLL)

    entry = []
    shift = (SUBLANES - 1) if reverse else 1
    for g in range(G):
        er, ei = ends[g]
        fr = jnp.where(edge_in, carry_ref[g, :, re], pltpu.roll(er, shift, 0))
        fi = jnp.where(edge_in, carry_ref[g, :, im], pltpu.roll(ei, shift, 0))
        for j, dist in enumerate((1, 2, 4)):
            pr, pi = tab_ref[g, 2 + 2 * j], tab_ref[g, 3 + 2 * j]
            sh = (SUBLANES - dist) if reverse else dist
            sr, si = pltpu.roll(fr, sh, 0), pltpu.roll(fi, sh, 0)
            fr, fi = fr + pr * sr - pi * si, fi + pr * si + pi * sr
        br, bi = tab_ref[g, 8], tab_ref[g, 9]
        outr = br * fr - bi * fi + er
        outi = br * fi + bi * fr + ei
        carry_ref[g, :, re] = jnp.broadcast_to(outr[edge_out:edge_out + 1, :], (SUBLANES, W))
        carry_ref[g, :, im] = jnp.broadcast_to(outi[edge_out:edge_out + 1, :], (SUBLANES, W))
        entry.append((fr, fi))
        if entry_ref is not None:
            entry_ref[g, :, re] = fr
            entry_ref[g, :, im] = fi

    add_da = da_ref is not None

    def fix(r, st, first=False):
        out = []
        for g in range(G):
            fr, fi = entry[g]
            pwr, pwi = pw_ref[g, r, :, re], pw_ref[g, r, :, im]
            xr = x_ref[g, r, :, re] + (pwr * fr - pwi * fi)
            xi = x_ref[g, r, :, im] + (pwr * fi + pwi * fr)
            x_ref[g, r, :, re] = xr
            x_ref[g, r, :, im] = xi
            if add_da:
                prev = fwd_entry_ref.at[g] if first else fwd_ref.at[g, r - 1]
                xpr, xpi = prev[:, re], prev[:, im]
                out.append((st[g][0] + (xr * xpr + xi * xpi), st[g][1] + (xi * xpr - xr * xpi)))
            else:
                out.append(st[g])
        return tuple(out)

    if add_da:
        st = fix(0, tuple((zero, zero) for _ in range(G)), first=True)
        st = lax.fori_loop(1, seg, fix, st, unroll=SCAN_UNROLL)
        for g in range(G):
            da_ref[g, :, re] += st[g][0]
            da_ref[g, :, im] += st[g][1]
    else:
        lax.fori_loop(0, seg, fix, tuple((zero[0:1, 0:LANES],) for _ in range(G)), unroll=FIX_UNROLL)


def _s5_specs(seg, time=lambda t: t):
    G = KB_PER_STEP
    return dict(
        x=pl.BlockSpec((G, seg, SUBLANES, 2 * KB_STATES), lambda kb, t: (kb, time(t), 0, 0)),
        ent=pl.BlockSpec((G, 1, SUBLANES, 2 * KB_STATES), lambda kb, t: (kb, time(t), 0, 0)),
        b=pl.BlockSpec((G, LANES, 2 * KB_STATES), lambda kb, t: (kb, 0, 0)),
        c=pl.BlockSpec((G, 2 * KB_STATES, LANES), lambda kb, t: (kb, 0, 0)),
        tab=pl.BlockSpec((G, 10, SUBLANES, KB_STATES), lambda kb, t: (kb, 0, 0, 0)),
        pw=pl.BlockSpec((G, seg, 1, 2 * KB_STATES), lambda kb, t: (kb, 0, 0, 0)),
        d=pl.BlockSpec((1, G * LANES), lambda kb, t: (0, kb)),
    )


def _s5_fwd(u, bmat, cmat, tab_f, pw_f, d_skip, tb):
    L = u.shape[0]
    nt = L // tb
    seg = tb // SUBLANES
    G = KB_PER_STEP
    ucol = pl.BlockSpec((tb, G * LANES), lambda kb, t: (t, kb))
    sp = _s5_specs(seg)

    def body(u_ref, b_ref, c_ref, tab_ref, pw_ref, d_ref, s_ref, x_ref, ent_ref, up_scr, y_scr, carry_scr):
        @pl.when(pl.program_id(1) == 0)
        def _():
            carry_scr[...] = jnp.zeros_like(carry_scr)

        _rows_to_segments(up_scr, u_ref, seg)
        for g in range(G):
            x_ref[g] = _dot(up_scr[g].astype(_BF), b_ref[g]).reshape(seg, SUBLANES, 2 * KB_STATES)
        _scan_segments(x_ref, tab_ref, pw_ref, carry_scr, seg, reverse=False, entry_ref=ent_ref.at[:, 0])
        for g in range(G):
            y = _dot(x_ref[g].reshape(tb, 2 * KB_STATES).astype(_BF), c_ref[g])
            y_scr[g] = y + d_ref[:, g * LANES:(g + 1) * LANES] * up_scr[g]
        _segments_to_rows(s_ref, y_scr, seg)

    return pl.pallas_call(
        body, name="s5_fwd", grid=(N_KB // G, nt),
        in_specs=[ucol, sp["b"], sp["c"], sp["tab"], sp["pw"], sp["d"]],
        out_specs=[ucol, sp["x"], sp["ent"]],
        out_shape=[jax.ShapeDtypeStruct((L, SSM_W), _F32),
                   jax.ShapeDtypeStruct((N_KB, L // SUBLANES, SUBLANES, 2 * KB_STATES), _F32),
                   jax.ShapeDtypeStruct((N_KB, nt, SUBLANES, 2 * KB_STATES), _F32)],
        scratch_shapes=[pltpu.VMEM((G, tb, LANES), _F32)] * 2 + [pltpu.VMEM((G, SUBLANES, 2 * KB_STATES), _F32)],
        compiler_params=_params("parallel", "arbitrary"),
    )(u, bmat, cmat, tab_f, pw_f, d_skip)


def _mixout_fwd(s, y_ret, x, w_glu, w_out, g2, tm):
    L = s.shape[0]

    def body(s_ref, yr_ref, x_ref, wg_ref, wo_ref, g_ref, ys_ref, glu_ref, cat_ref, mix_ref, x2_ref):
        for rows in _row_chunks(tm):
            ys = _gelu(s_ref[rows, :]).astype(_BF)
            ys_ref[rows, :] = ys
            glu = _dot(ys, wg_ref[...])
            glu_ref[rows, :] = glu
            cat_ref[rows, :RET_W] = yr_ref[rows, :]
            cat_ref[rows, RET_W:] = (glu[:, :SSM_W] * _sigmoid(glu[:, SSM_W:])).astype(_BF)
            mix = _dot(cat_ref[rows, :], wo_ref[...])
            mix_ref[rows, :] = mix
            x2_ref[rows, :] = x_ref[rows, :] + mix * _rms_r(mix) * g_ref[...]

    return pl.pallas_call(
        body, name="mixout_fwd", grid=(L // tm,),
        in_specs=[_row_spec(tm, SSM_W), _row_spec(tm, RET_W), _row_spec(tm, D_MODEL),
                  _weight_spec((SSM_W, 2 * SSM_W)), _weight_spec((D_MODEL, D_MODEL)), _full_spec((1, D_MODEL))],
        out_specs=[_row_spec(tm, SSM_W), _row_spec(tm, 2 * SSM_W), _row_spec(tm, D_MODEL),
                   _row_spec(tm, D_MODEL), _row_spec(tm, D_MODEL)],
        out_shape=[jax.ShapeDtypeStruct((L, SSM_W), _BF), jax.ShapeDtypeStruct((L, 2 * SSM_W), _F32),
                   jax.ShapeDtypeStruct((L, D_MODEL), _BF), jax.ShapeDtypeStruct((L, D_MODEL), _F32),
                   jax.ShapeDtypeStruct((L, D_MODEL), _F32)],
        compiler_params=_params("parallel"),
    )(s, y_ret, x, w_glu, w_out, g2)


FF1_COLS = D_FF // N_DEV


def _ff1_fwd(x2, g3, w1, tm):
    L = x2.shape[0]

    def body(x_ref, g_ref, w_ref, h_ref, f_ref, a_ref):
        for rows in _row_chunks(tm):
            xv = x_ref[rows, :]
            h = (xv * _rms_r(xv) * g_ref[...]).astype(_BF)
            h_ref[rows, :] = h
            for j in range(N_DEV):
                cols = slice(j * FF1_COLS, (j + 1) * FF1_COLS)
                f = _dot(h, w_ref[j])
                f_ref[rows, cols] = f
                rl = jnp.maximum(f, 0.0)
                a_ref[rows, cols] = (rl * rl).astype(_BF)

    return pl.pallas_call(
        body, name="ff1_fwd", grid=(L // tm,),
        in_specs=[_row_spec(tm, D_MODEL), _full_spec((1, D_MODEL)), _weight_spec((N_DEV, D_MODEL, FF1_COLS))],
        out_specs=[_row_spec(tm, D_MODEL), _row_spec(tm, D_FF), _row_spec(tm, D_FF)],
        out_shape=[jax.ShapeDtypeStruct((L, D_MODEL), _BF), jax.ShapeDtypeStruct((L, D_FF), _F32),
                   jax.ShapeDtypeStruct((L, D_FF), _BF)],
        compiler_params=_params("parallel"),
    )(x2, g3, w1)


def _ff2_loss(act, x2, tgt, g4, w2, tm):
    L = act.shape[0]

    def body(f_ref, x_ref, t_ref, g_ref, w_ref, dy_ref, dm_ref, dg_ref, ls_ref):
        @pl.when(pl.program_id(0) == 0)
        def _():
            dg_ref[...] = jnp.zeros_like(dg_ref)
            ls_ref[...] = jnp.zeros_like(ls_ref)

        g = g_ref[...]
        for rows in _row_chunks(tm):
            m = _dot(f_ref[rows, :], w_ref[...])
            y = x_ref[rows, :] + m * _rms_r(m) * g
            err = y - t_ref[rows, :]
            ls_ref[...] += jnp.sum(err * err, axis=0, keepdims=True)
            dy = err * (1.0 / D_MODEL)
            dy_ref[rows, :] = dy
            dm, dgr = _rms_bwd(m, g, dy)
            dm_ref[rows, :] = dm.astype(_BF)
            dg_ref[...] += jnp.sum(dgr, axis=0, keepdims=True)

    return pl.pallas_call(
        body, name="ff2_loss", grid=(L // tm,),
        in_specs=[_row_spec(tm, D_FF), _row_spec(tm, D_MODEL), _row_spec(tm, D_MODEL),
                  _full_spec((1, D_MODEL)), _weight_spec((D_FF, D_MODEL))],
        out_specs=[_row_spec(tm, D_MODEL), _row_spec(tm, D_MODEL), _full_spec((1, D_MODEL)), _full_spec((1, D_MODEL))],
        out_shape=[jax.ShapeDtypeStruct((L, D_MODEL), _F32), jax.ShapeDtypeStruct((L, D_MODEL), _BF),
                   jax.ShapeDtypeStruct((1, D_MODEL), _F32), jax.ShapeDtypeStruct((1, D_MODEL), _F32)],
        compiler_params=_params("arbitrary"),
    )(act, x2, tgt, g4, w2)


def _ff2_bwd(dm, f1, w2, tm, tn):
    L = dm.shape[0]
    last = L // tm - 1

    def body(dm_ref, f_ref, w_ref, df_ref, dw_ref, acc):
        @pl.when(pl.program_id(1) == 0)
        def _():
            acc[...] = jnp.zeros_like(acc)

        dmv = dm_ref[...]
        rl = jnp.maximum(f_ref[...], 0.0)
        df_ref[...] = (_dot_nt(dmv, w_ref[...]) * (2.0 * rl)).astype(_BF)
        acc[...] += _dot_tn((rl * rl).astype(_BF), dmv)

        @pl.when(pl.program_id(1) == last)
        def _():
            dw_ref[...] = acc[...].astype(_BF)

    return pl.pallas_call(
        body, name="ff2_bwd", grid=(D_FF // tn, L // tm),
        in_specs=[pl.BlockSpec((tm, D_MODEL), lambda j, i: (i, 0)), pl.BlockSpec((tm, tn), lambda j, i: (i, j)),
                  pl.BlockSpec((tn, D_MODEL), lambda j, i: (j, 0))],
        out_specs=[pl.BlockSpec((tm, tn), lambda j, i: (i, j)), pl.BlockSpec((tn, D_MODEL), lambda j, i: (j, 0))],
        out_shape=[jax.ShapeDtypeStruct((L, D_FF), _BF), jax.ShapeDtypeStruct((D_FF, D_MODEL), _BF)],
        scratch_shapes=[pltpu.VMEM((tn, D_MODEL), _F32)],
        compiler_params=_params("parallel", "arbitrary"),
    )(dm, f1, w2)


def _ff1_bwd(df1, w1, x2, mix, dy, g3, g2, tm):
    L = df1.shape[0]

    def body(df_ref, w_ref, x2_ref, mix_ref, dy_ref, g3_ref, g2_ref, dx2_ref, dmix_ref, dg3_ref, dg2_ref):
        @pl.when(pl.program_id(0) == 0)
        def _():
            dg3_ref[...] = jnp.zeros_like(dg3_ref)
            dg2_ref[...] = jnp.zeros_like(dg2_ref)

        for rows in _row_chunks(tm):
            dh = _dot_nt(df_ref[rows, 0:FF1_COLS], w_ref[0])
            for j in range(1, N_DEV):
                dh = dh + _dot_nt(df_ref[rows, j * FF1_COLS:(j + 1) * FF1_COLS], w_ref[j])
            dz, dgr = _rms_bwd(x2_ref[rows, :], g3_ref[...], dh)
            dg3_ref[...] += jnp.sum(dgr, axis=0, keepdims=True)
            dx2 = dy_ref[rows, :] + dz
            dx2_ref[rows, :] = dx2
            dmx, dgr2 = _rms_bwd(mix_ref[rows, :], g2_ref[...], dx2)
            dg2_ref[...] += jnp.sum(dgr2, axis=0, keepdims=True)
            dmix_ref[rows, :] = dmx.astype(_BF)

    vec = _full_spec((1, D_MODEL))
    return pl.pallas_call(
        body, name="ff1_bwd", grid=(L // tm,),
        in_specs=[_row_spec(tm, D_FF), _weight_spec((N_DEV, D_MODEL, FF1_COLS)), _row_spec(tm, D_MODEL),
                  _row_spec(tm, D_MODEL), _row_spec(tm, D_MODEL), vec, vec],
        out_specs=[_row_spec(tm, D_MODEL), _row_spec(tm, D_MODEL), vec, vec],
        out_shape=[jax.ShapeDtypeStruct((L, D_MODEL), _F32), jax.ShapeDtypeStruct((L, D_MODEL), _BF),
                   jax.ShapeDtypeStruct((1, D_MODEL), _F32), jax.ShapeDtypeStruct((1, D_MODEL), _F32)],
        compiler_params=_params("arbitrary"),
    )(df1, w1, x2, mix, dy, g3, g2)


def _matmul_tn(a, b, tm, tn, name, slots=False):
    L, K = a.shape
    N = b.shape[1]
    last = L // tm - 1

    def body(a_ref, b_ref, o_ref, acc):
        @pl.when(pl.program_id(1) == 0)
        def _():
            acc[...] = jnp.zeros_like(acc)

        acc[...] += _dot_tn(a_ref[...].astype(_BF), b_ref[...].astype(_BF))

        @pl.when(pl.program_id(1) == last)
        def _():
            if slots:
                o_ref[0] = acc[...].astype(_BF)
            else:
                o_ref[...] = acc[...].astype(_BF)

    if slots:
        out_spec = pl.BlockSpec((1, K, tn), lambda j, i: (j, 0, 0))
        out_shape = jax.ShapeDtypeStruct((N // tn, K, tn), _BF)
    else:
        out_spec = pl.BlockSpec((K, tn), lambda j, i: (0, j))
        out_shape = jax.ShapeDtypeStruct((K, N), _BF)
    return pl.pallas_call(
        body, name=name, grid=(N // tn, L // tm),
        in_specs=[pl.BlockSpec((tm, K), lambda j, i: (i, 0)), pl.BlockSpec((tm, tn), lambda j, i: (i, j))],
        out_specs=out_spec, out_shape=out_shape,
        scratch_shapes=[pltpu.VMEM((K, tn), _F32)],
        compiler_params=_params("parallel", "arbitrary"),
    )(a, b)


def _dw_in_t(pieces, h, tk):
    L = h.shape[0]
    last = L // tk - 1

    def body(p0, p1, p2, p3, p4, h_ref, o_ref, acc):
        @pl.when(pl.program_id(0) == 0)
        def _():
            acc[...] = jnp.zeros_like(acc)

        hv = h_ref[...]
        for j, p in enumerate((p0, p1, p2, p3, p4)):
            acc[j * RET_W:(j + 1) * RET_W, :] += _dot_tn(p[...].astype(_BF), hv)

        @pl.when(pl.program_id(0) == last)
        def _():
            o_ref[...] = acc[...].astype(_BF)

    return pl.pallas_call(
        body, name="dw_in", grid=(L // tk,),
        in_specs=[_row_spec(tk, RET_W)] * 5 + [_row_spec(tk, D_MODEL)],
        out_specs=_full_spec((IN_COLS, D_MODEL)), out_shape=jax.ShapeDtypeStruct((IN_COLS, D_MODEL), _BF),
        scratch_shapes=[pltpu.VMEM((IN_COLS, D_MODEL), _F32)],
        compiler_params=_params("arbitrary"),
    )(*pieces, h)


def _mixout_bwd(dmix, w_out, w_glu, glu, s, o, gate, ggn, tm, after=()):
    L = dmix.shape[0]

    def body(dmix_ref, wo_ref, wg_ref, glu_ref, s_ref, o_ref, gate_ref, ggn_ref,
             dglu_ref, ds_ref, dgate_ref, do_ref, dggn_ref):
        @pl.when(pl.program_id(0) == 0)
        def _():
            dggn_ref[...] = jnp.zeros_like(dggn_ref)

        ggn = ggn_ref[...]
        for rows in _row_chunks(tm):
            dcat = _dot_nt(dmix_ref[rows, :], wo_ref[...])
            dy_ret, dy_ssm = dcat[:, :RET_W], dcat[:, RET_W:]
            glu = glu_ref[rows, :]
            ga, sg = glu[:, :SSM_W], _sigmoid(glu[:, SSM_W:])
            dga = (dy_ssm * sg).astype(_BF)
            dgb = (dy_ssm * ga * sg * (1.0 - sg)).astype(_BF)
            dglu_ref[rows, :SSM_W] = dga
            dglu_ref[rows, SSM_W:] = dgb
            dys = _dot_nt(dga, wg_ref[:, :SSM_W]) + _dot_nt(dgb, wg_ref[:, SSM_W:])
            ds_ref[rows, :] = dys * _gelu_grad(s_ref[rows, :])
            gt = gate_ref[rows, :]
            sgt = _sigmoid(gt)
            for hh in range(N_HEAD):
                cols = slice(hh * HEAD_D, (hh + 1) * HEAD_D)
                ov = o_ref[rows, cols]
                dlt = ov - jnp.mean(ov, axis=-1, keepdims=True)
                rstd = lax.rsqrt(jnp.mean(dlt * dlt, axis=-1, keepdims=True) + NORM_EPS)
                on = dlt * rstd
                dyr = dy_ret[:, cols] * (gt[:, cols] * sgt[:, cols])
                dgate_ref[rows, cols] = dy_ret[:, cols] * (on * ggn[:, cols]) * (sgt[:, cols] * (1.0 + gt[:, cols] * (1.0 - sgt[:, cols])))
                dggn_ref[:, cols] += jnp.sum(dyr * on, axis=0, keepdims=True)
                don = dyr * ggn[:, cols]
                do = rstd * (don - jnp.mean(don, axis=-1, keepdims=True) - on * jnp.mean(don * on, axis=-1, keepdims=True))
                do_ref[rows, cols] = do.astype(_BF)

    body, in_specs, operands = _ordered(
        body, [_row_spec(tm, D_MODEL), _weight_spec((D_MODEL, D_MODEL)), _weight_spec((SSM_W, 2 * SSM_W)),
               _row_spec(tm, 2 * SSM_W), _row_spec(tm, SSM_W), _row_spec(tm, RET_W), _row_spec(tm, RET_W),
               _full_spec((1, RET_W))], (dmix, w_out, w_glu, glu, s, o, gate, ggn), after)
    return pl.pallas_call(
        body, name="mixout_bwd", grid=(L // tm,),
        in_specs=in_specs,
        out_specs=[_row_spec(tm, 2 * SSM_W), _row_spec(tm, SSM_W), _row_spec(tm, RET_W), _row_spec(tm, RET_W),
                   _full_spec((1, RET_W))],
        out_shape=[jax.ShapeDtypeStruct((L, 2 * SSM_W), _BF), jax.ShapeDtypeStruct((L, SSM_W), _F32),
                   jax.ShapeDtypeStruct((L, RET_W), _F32), jax.ShapeDtypeStruct((L, RET_W), _BF),
                   jax.ShapeDtypeStruct((1, RET_W), _F32)],
        compiler_params=_params("arbitrary"),
    )(*operands)


def _s5_bwd(u, ds, xs, ent, bmat, cmat, tab_r, pw_r, d_skip, tb, after=()):
    L = u.shape[0]
    nt = L // tb
    seg = tb // SUBLANES
    G = KB_PER_STEP
    rcol = pl.BlockSpec((tb, G * LANES), lambda kb, t: (nt - 1 - t, kb))
    sp = _s5_specs(seg, time=lambda t: nt - 1 - t)
    aspec = pl.BlockSpec((G, SUBLANES, 2 * KB_STATES), lambda kb, t: (kb, 0, 0))

    def body(u_ref, ds_ref, x_ref, ent_ref, b_ref, c_ref, tr_ref, pr_ref, d_ref,
             du_ref, db_ref, dc_ref, da_ref, dd_ref, up_scr, dp_scr, g_scr, lc_scr):
        @pl.when(pl.program_id(1) == 0)
        def _():
            lc_scr[...] = jnp.zeros_like(lc_scr)
            db_ref[...] = jnp.zeros_like(db_ref)
            dc_ref[...] = jnp.zeros_like(dc_ref)
            da_ref[...] = jnp.zeros_like(da_ref)
            dd_ref[...] = jnp.zeros_like(dd_ref)

        _rows_to_segments(up_scr, u_ref, seg)
        _rows_to_segments(dp_scr, ds_ref, seg)
        for g in range(G):
            g_scr[g] = _dot_nt(dp_scr[g].astype(_BF), c_ref[g]).reshape(seg, SUBLANES, 2 * KB_STATES)
        _scan_segments(g_scr, tr_ref, pr_ref, lc_scr, seg, reverse=True, fwd_ref=x_ref, fwd_entry_ref=ent_ref.at[:, 0],
                       da_ref=da_ref)
        for g in range(G):
            cols = slice(g * LANES, (g + 1) * LANES)
            uv, dsv = up_scr[g], dp_scr[g]
            ub, dsb = uv.astype(_BF), dsv.astype(_BF)
            lamb = g_scr[g].reshape(tb, 2 * KB_STATES).astype(_BF)
            db_ref[g] += _dot_tn(ub, lamb)
            dc_ref[g] += _dot_tn(dsb, x_ref[g].reshape(tb, 2 * KB_STATES).astype(_BF))
            dd_ref[:, cols] += jnp.sum(dsv * uv, axis=0, keepdims=True)
            up_scr[g] = _dot_nt(lamb, b_ref[g]) + d_ref[:, cols] * dsv
        _segments_to_rows(du_ref, up_scr, seg)

    body, in_specs, operands = _ordered(
        body, [rcol, rcol, sp["x"], sp["ent"], sp["b"], sp["c"], sp["tab"], sp["pw"], sp["d"]],
        (u, ds, xs, ent, bmat, cmat, tab_r, pw_r, d_skip), after)
    return pl.pallas_call(
        body, name="s5_bwd", grid=(N_KB // G, nt),
        in_specs=in_specs,
        out_specs=[rcol, sp["b"], sp["b"], aspec, sp["d"]],
        out_shape=[jax.ShapeDtypeStruct((L, SSM_W), _F32),
                   jax.ShapeDtypeStruct((N_KB, LANES, 2 * KB_STATES), _F32),
                   jax.ShapeDtypeStruct((N_KB, LANES, 2 * KB_STATES), _F32),
                   jax.ShapeDtypeStruct((N_KB, SUBLANES, 2 * KB_STATES), _F32),
                   jax.ShapeDtypeStruct((1, SSM_W), _F32)],
        scratch_shapes=[pltpu.VMEM((G, tb, LANES), _F32)] * 2
        + [pltpu.VMEM((G, seg, SUBLANES, 2 * KB_STATES), _F32), pltpu.VMEM((G, SUBLANES, 2 * KB_STATES), _F32)],
        compiler_params=_params("parallel", "arbitrary"),
    )(*operands)


def _retention_bwd(q, k, v, do, r_prev, consts, cosf, sinf, after=()):
    L = q.shape[0]
    nc = L // CHUNK
    cps = math.gcd(RET_STEP_CHUNKS, nc)
    nb = nc // cps
    blk = pl.BlockSpec((cps * CHUNK, RET_W), lambda n: (nb - 1 - n, 0))
    rope_blk = pl.BlockSpec((cps * CHUNK, HEAD_D), lambda n: (nb - 1 - n, 0))

    def body(q_ref, k_ref, v_ref, do_ref, rp_ref, dm_ref, xi_ref, zeta_ref, gc_ref, cos_ref, sin_ref,
             dq_ref, dk_ref, dv_ref, g_scr):
        @pl.when(pl.program_id(0) == 0)
        def _():
            g_scr[...] = jnp.zeros_like(g_scr)

        for hh in range(N_HEAD):
            cols = slice(hh * HEAD_D, (hh + 1) * HEAD_D)
            dm, zeta = dm_ref[hh], zeta_ref[hh]
            gst = g_scr[hh]
            for c in reversed(range(cps)):
                rows = slice(c * CHUNK, (c + 1) * CHUNK)
                qv, kv, vv, dov = q_ref[rows, cols], k_ref[rows, cols], v_ref[rows, cols], do_ref[rows, cols]
                rb = rp_ref[hh, c].astype(_BF)
                gb = gst.astype(_BF)
                sb = (_dot_nt(qv, kv) * dm).astype(_BF)
                dab = (_dot_nt(dov, vv) * dm).astype(_BF)
                dox = (dov.astype(_F32) * xi_ref[hh]).astype(_BF)
                vz = (vv.astype(_F32) * zeta).astype(_BF)
                dq = _dot(dab, kv) + _dot_nt(dox, rb)
                dk = _dot_tn(dab, qv) + _dot_nt(vz, gb)
                dv = _dot_tn(sb, dov) + _dot(kv, gb) * zeta
                gst = gc_ref[hh, 0:1, :] * gst + _dot_tn(qv, dox)
                cs, sn = cos_ref[rows, :], sin_ref[rows, :]
                dq_ref[rows, cols] = _rope_t(dq, cs, sn).astype(_BF)
                dk_ref[rows, cols] = (_rope_t(dk, cs, sn) * (HEAD_D ** -0.5)).astype(_BF)
                dv_ref[rows, cols] = dv.astype(_BF)
            g_scr[hh] = gst

    body, in_specs, operands = _ordered(
        body, [blk, blk, blk, blk, pl.BlockSpec((N_HEAD, cps, HEAD_D, HEAD_D), lambda n: (0, nb - 1 - n, 0, 0))]
        + _head_specs() + [rope_blk, rope_blk], (q, k, v, do, r_prev, *consts, cosf, sinf), after)
    return pl.pallas_call(
        body, name="retention_bwd", grid=(nb,),
        in_specs=in_specs,
        out_specs=[blk, blk, blk],
        out_shape=[jax.ShapeDtypeStruct((L, RET_W), _BF)] * 3,
        scratch_shapes=[pltpu.VMEM((N_HEAD, HEAD_D, HEAD_D), _F32)],
        compiler_params=_params("arbitrary"),
    )(*operands)


def _inproj_bwd(pieces, w_in_t, x, dx2, g1, tm, after=()):
    L = x.shape[0]

    def body(p0, p1, p2, p3, p4, w_ref, x_ref, dx2_ref, g_ref, dx_ref, dg_ref):
        @pl.when(pl.program_id(0) == 0)
        def _():
            dg_ref[...] = jnp.zeros_like(dg_ref)

        for rows in _row_chunks(tm):
            dh = None
            for j, p in enumerate((p0, p1, p2, p3, p4)):
                part = _dot(p[rows, :].astype(_BF), w_ref[j * RET_W:(j + 1) * RET_W, :])
                dh = part if dh is None else dh + part
            dz, dgr = _rms_bwd(x_ref[rows, :], g_ref[...], dh)
            dx_ref[rows, :] = dx2_ref[rows, :] + dz
            dg_ref[...] += jnp.sum(dgr, axis=0, keepdims=True)

    body, in_specs, operands = _ordered(
        body, [_row_spec(tm, RET_W)] * 5 + [_weight_spec((IN_COLS, D_MODEL)), _row_spec(tm, D_MODEL),
                                             _row_spec(tm, D_MODEL), _full_spec((1, D_MODEL))],
        (*pieces, w_in_t, x, dx2, g1), after)
    return pl.pallas_call(
        body, name="inproj_bwd", grid=(L // tm,),
        in_specs=in_specs,
        out_specs=[_row_spec(tm, D_MODEL), _full_spec((1, D_MODEL))],
        out_shape=[jax.ShapeDtypeStruct((L, D_MODEL), _F32), jax.ShapeDtypeStruct((1, D_MODEL), _F32)],
        compiler_params=_params("arbitrary"),
    )(*operands)


def _sum_adamw(parts, w, m, v, tr, name):
    _, R, Cc = parts.shape

    def body(p_ref, w_ref, m_ref, v_ref, g_ref, d_ref, nm_ref, nv_ref):
        gv = p_ref[0].astype(_F32)
        for s in range(1, N_DEV):
            gv = gv + p_ref[s].astype(_F32)
        g_ref[...] = gv
        nm = ADAM_B1 * m_ref[...] + (1.0 - ADAM_B1) * gv
        nv = ADAM_B2 * v_ref[...] + (1.0 - ADAM_B2) * (gv * gv)
        m_hat = nm / (1.0 - ADAM_B1 ** ADAM_STEP)
        v_hat = nv / (1.0 - ADAM_B2 ** ADAM_STEP)
        d_ref[...] = -ADAM_LR * (m_hat / (jnp.sqrt(v_hat) + ADAM_EPS) + ADAM_WD * w_ref[...])
        nm_ref[...] = nm
        nv_ref[...] = nv

    spec = _row_spec(tr, Cc)
    return pl.pallas_call(
        body, name=name, grid=(R // tr,),
        in_specs=[pl.BlockSpec((N_DEV, tr, Cc), lambda i: (0, i, 0))] + [spec] * 3, out_specs=[spec] * 4,
        out_shape=[jax.ShapeDtypeStruct((R, Cc), _F32)] * 4,
        compiler_params=_params("parallel"),
    )(parts, w, m, v)


def _my_place():
    return lax.axis_index("x"), lax.axis_index("y"), lax.axis_index("c")


HBM_SPEC = pl.BlockSpec(memory_space=pltpu.HBM)
SEM_SPEC = pl.BlockSpec(memory_space=pltpu.SEMAPHORE)
DATAFLOW = pltpu.SideEffectType.DATAFLOW_SIDE_EFFECTING


def _my_index():
    x, y, c = _my_place()
    return 4 * x + 2 * y + c


def _landing(own_block):
    zone = lax.empty((N_DEV,) + own_block.shape, own_block.dtype)
    return lax.dynamic_update_index_in_dim(zone, own_block, _my_index(), 0)


def _split_copies(src_refs, land_refs, send_sems, recv_sems, gather, first=0):
    x, y, c = _my_place()
    me = 4 * x + 2 * y + c
    copies = []
    for a, (src, land) in enumerate(zip(src_refs, land_refs)):
        for kk in range(1, N_DEV):
            px, py, pc = x ^ (kk >> 2), y ^ ((kk >> 1) & 1), c ^ (kk & 1)
            peer = 4 * px + 2 * py + pc
            copies.append(pltpu.make_async_remote_copy(
                src_ref=src if gather else src.at[peer], dst_ref=land.at[me],
                send_sem=send_sems.at[(first + a) * 7 + kk - 1], recv_sem=recv_sems.at[(first + a) * 7 + kk - 1],
                device_id=(px, py, pc), device_id_type=MESH))
    return copies


def _split_start(srcs, lands, gather, name):
    n = len(srcs)

    def body(*refs):
        src_refs, land_refs = refs[:n], refs[n:2 * n]
        send_sems, recv_sems = refs[2 * n], refs[2 * n + 1]
        token = refs[-1]
        for cp in _split_copies(src_refs, land_refs, send_sems, recv_sems, gather):
            cp.start()
        token[...] = jnp.zeros_like(token)

    outs = pl.pallas_call(
        body, name=name,
        out_shape=(pltpu.SemaphoreType.DMA((7 * n,)), pltpu.SemaphoreType.DMA((7 * n,)),
                   *[pltpu.HBM(t.shape, t.dtype) for t in srcs], *[pltpu.HBM(t.shape, t.dtype) for t in lands],
                   jax.ShapeDtypeStruct((SUBLANES, LANES), _F32)),
        in_specs=[HBM_SPEC] * (2 * n),
        out_specs=(SEM_SPEC, SEM_SPEC, *[HBM_SPEC] * (2 * n), pl.BlockSpec(memory_space=pltpu.VMEM)),
        input_output_aliases={i: 2 + i for i in range(2 * n)},
        compiler_params=pltpu.CompilerParams(has_side_effects=DATAFLOW),
    )(*[pltpu.with_memory_space_constraint(t, pltpu.HBM) for t in list(srcs) + list(lands)])
    return outs[0], outs[1], outs[2:2 + n], outs[2 + n:2 + 2 * n], outs[-1]


def _split_wait(send_sems, recv_sems, srcs, lands, after, gather, name, first=0):
    n = len(srcs)

    def body(*refs):
        src_refs, land_refs = refs[:n], refs[n:2 * n]
        send_s, recv_s = refs[2 * n], refs[2 * n + 1]
        for cp in _split_copies(src_refs, land_refs, send_s, recv_s, gather, first):
            cp.wait_send()
            cp.wait_recv()

    outs = pl.pallas_call(
        body, name=name,
        out_shape=tuple(pltpu.HBM(t.shape, t.dtype) for t in list(srcs) + list(lands)),
        in_specs=[HBM_SPEC] * (2 * n) + [SEM_SPEC, SEM_SPEC, pl.BlockSpec(memory_space=pl.ANY)],
        out_specs=tuple([HBM_SPEC] * (2 * n)),
        input_output_aliases={i: i for i in range(2 * n)},
        compiler_params=pltpu.CompilerParams(has_side_effects=DATAFLOW),
    )(*srcs, *lands, send_sems, recv_sems, after)
    return outs[n:]


def _discretize(lam_re, lam_im, log_dt, b_re, b_im):
    lr = jnp.minimum(lam_re, -1e-4)
    li = lam_im
    dt = jnp.exp(log_dt)[:, None]
    er = jnp.exp(lr * dt)
    ar, ai = er * jnp.cos(li * dt), er * jnp.sin(li * dt)
    den = lr * lr + li * li
    cr = ((ar - 1.0) * lr + ai * li) / den
    ci = (ai * lr - (ar - 1.0) * li) / den
    bbr = cr[:, :, None] * b_re - ci[:, :, None] * b_im
    bbi = cr[:, :, None] * b_im + ci[:, :, None] * b_re
    return ar, ai, bbr, bbi


def _cmul(ar, ai, br, bi):
    return ar * br - ai * bi, ar * bi + ai * br


def _cpowers(ar, ai, n):
    pr, pi = ar[None], ai[None]
    while pr.shape[0] < n:
        nr, ni = _cmul(pr, pi, pr[-1][None], pi[-1][None])
        pr, pi = jnp.concatenate([pr, nr]), jnp.concatenate([pi, ni])
    return pr[:n], pi[:n]


def _scan_tables(ar, ai, seg, reverse):
    if reverse:
        ai = -ai
    ar, ai = ar.reshape(N_KB, KB_STATES), ai.reshape(N_KB, KB_STATES)
    pr, pi = _cpowers(ar, ai, seg)
    a1 = (pr[-1], pi[-1])
    a2 = _cmul(*a1, *a1)
    a4 = _cmul(*a2, *a2)
    row = jnp.arange(SUBLANES)[None, :, None]
    wide = lambda t: jnp.broadcast_to(t[:, None, :], (N_KB, SUBLANES, KB_STATES))
    tabs = [wide(ar), wide(ai)]
    for dist, (qr, qi) in ((1, a1), (2, a2), (4, a4)):
        keep = (row < SUBLANES - dist) if reverse else (row >= dist)
        tabs += [jnp.where(keep, wide(qr), 0.0), jnp.where(keep, wide(qi), 0.0)]
    tabs += [wide(a1[0]), wide(a1[1])]
    if reverse:
        pr, pi = pr[::-1], pi[::-1]
    pw = jnp.transpose(jnp.concatenate([pr, pi], axis=-1), (1, 0, 2))[:, :, None, :]
    return jnp.stack(tabs, axis=1).astype(_F32), pw.astype(_F32)


def _block_diag_in(br, bi):
    eye = jnp.eye(GROUPS_PER_KB, dtype=_F32)
    one = lambda t: jnp.einsum("kgpc,gh->kgchp", t.reshape(N_KB, GROUPS_PER_KB, N_STATE, SSM_GC), eye).reshape(
        N_KB, LANES, KB_STATES)
    return jnp.concatenate([one(br), one(bi)], axis=-1)


def _block_diag_in_t(dmat):
    d6 = dmat.reshape(N_KB, GROUPS_PER_KB, SSM_GC, 2, GROUPS_PER_KB, N_STATE)
    eye = jnp.eye(GROUPS_PER_KB, dtype=_F32)
    both = jnp.einsum("kgcrhp,gh->rkgpc", d6, eye).reshape(2, N_GROUP, N_STATE, SSM_GC)
    return both[0], both[1]


def _block_diag_out(c_re, c_im):
    eye = jnp.eye(GROUPS_PER_KB, dtype=_F32)
    one = lambda t: jnp.einsum("kgcp,gh->khpgc", t.reshape(N_KB, GROUPS_PER_KB, SSM_GC, N_STATE), eye).reshape(
        N_KB, KB_STATES, LANES)
    return jnp.concatenate([one(c_re), -one(c_im)], axis=1)


def _block_diag_out_t(dmat_t):
    d6 = dmat_t.reshape(N_KB, GROUPS_PER_KB, SSM_GC, 2, GROUPS_PER_KB, N_STATE)
    eye = jnp.eye(GROUPS_PER_KB, dtype=_F32)
    both = jnp.einsum("kgcrhp,gh->rkgcp", d6, eye).reshape(2, N_GROUP, SSM_GC, N_STATE)
    return both[0], -both[1]


SMALL_NAMES = ("norm_mix_pre", "norm_mix_post", "ret_gn_gain", "ssm_lambda_re", "ssm_lambda_im", "ssm_log_dt",
               "ssm_b_re", "ssm_b_im", "ssm_c_re", "ssm_c_im", "ssm_d", "norm_mlp_pre", "norm_mlp_post")


def _local_grads(x, tgt, small, weights, emit, emit_small, tm, tk, tb, zero=0.0):
    L = x.shape[0]
    g1, g2, ggn = small["norm_mix_pre"], small["norm_mix_post"], small["ret_gn_gain"]
    g3, g4, d_skip = small["norm_mlp_pre"], small["norm_mlp_post"], small["ssm_d"]

    rope = _rope_tables(L)
    consts = _ret_consts()

    disc_in = (small["ssm_lambda_re"][0], small["ssm_lambda_im"][0], small["ssm_log_dt"][0] + zero,
               small["ssm_b_re"][0], small["ssm_b_im"][0])
    (ar, ai, bbr, bbi), disc_vjp = jax.vjp(_discretize, *disc_in)
    bmat = _block_diag_in(bbr, bbi).astype(_BF)
    cmat = _block_diag_out(small["ssm_c_re"][0], small["ssm_c_im"][0]).astype(_BF)
    seg = tb // SUBLANES
    tab_f, pw_f = _scan_tables(ar, ai, seg, False)
    tab_r, pw_r = _scan_tables(ar, ai, seg, True)

    h1 = _prenorm(x, g1, min(4 * tm, L), after=(pw_r,))
    (w_in_t,) = weights("in", h1)
    q, k, v, gate, u, cosf, sinf = _inproj_fwd(h1, w_in_t, rope, min(4 * tm, L))
    o, y_ret, r_prev = _retention_fwd(q, k, v, gate, ggn, consts)
    s, xs, ent = _s5_fwd(u, bmat, cmat, tab_f, pw_f, d_skip, tb)
    w_glu, w_out = weights("mix", s)
    ys, glu, cat, mix, x2 = _mixout_fwd(s, y_ret, x, w_glu, w_out, g2, min(2 * tm, L))
    w_ff1, w_ff2 = weights("mlp", x2)
    h3, f1, act = _ff1_fwd(x2, g3, w_ff1, min(2 * tm, L))
    dy, dm, dg4, sq = _ff2_loss(act, x2, tgt, g4, w_ff2, min(2 * tm, L))

    df1, dw_ff2 = _ff2_bwd(dm, f1, w_ff2, min(1024, L), 1024)
    dx2, dmix, dg3, dg2 = _ff1_bwd(df1, w_ff1, x2, mix, dy, g3, g2, min(2 * tm, L))
    dw_ff1 = _matmul_tn(h3, df1, tk, FF1_COLS, "dw_ff1", slots=True)
    token = emit({"w_ff1": dw_ff1, "w_ff2": dw_ff2})
    dglu, ds, dgate, do, dggn = _mixout_bwd(dmix, w_out, w_glu, glu, s, o, gate, ggn, min(2 * tm, L), after=token)
    dw_out = _matmul_tn(cat, dmix, tk, 1024, "dw_out")
    dw_glu = _matmul_tn(ys, dglu, tk, 1024, "dw_glu")
    token = emit({"w_glu": dw_glu, "w_out": dw_out})
    du, dbmat, dcmat, da8, dd = _s5_bwd(u, ds, xs, ent, bmat, cmat, tab_r, pw_r, d_skip, tb, after=token)

    da = jnp.sum(da8, axis=1)
    dar = da[:, :KB_STATES].reshape(N_GROUP, N_STATE)
    dai = da[:, KB_STATES:].reshape(N_GROUP, N_STATE)
    dbr, dbi = _block_diag_in_t(dbmat)
    dlre, dlim, dldt, dbre, dbim = disc_vjp((dar, dai, dbr, dbi))
    dcre, dcim = _block_diag_out_t(dcmat)
    token = emit_small({
        "norm_mix_post": dg2, "ret_gn_gain": dggn,
        "ssm_lambda_re": dlre[None], "ssm_lambda_im": dlim[None], "ssm_log_dt": dldt[None],
        "ssm_b_re": dbre[None], "ssm_b_im": dbim[None], "ssm_c_re": dcre[None], "ssm_c_im": dcim[None],
        "ssm_d": dd, "norm_mlp_pre": dg3, "norm_mlp_post": dg4,
    }, sq)

    dq, dk, dv = _retention_bwd(q, k, v, do, r_prev, consts, cosf, sinf, after=token)
    pieces = (dq, dk, dv, dgate, du)
    dw_in_t = _dw_in_t(pieces, h1, min(1024, L))
    token = emit({"w_in": dw_in_t})
    gx, dg1 = _inproj_bwd(pieces, w_in_t, x, dx2, g1, min(2 * tm, L), after=token)
    return gx, dg1


BIG_SHAPES = {"w_in": (D_MODEL, IN_COLS // N_DEV), "w_glu": (SSM_W, 2 * SSM_W // N_DEV), "w_out": (D_MODEL // N_DEV, D_MODEL),
              "w_ff1": (D_MODEL, FF1_COLS), "w_ff2": (D_FF // N_DEV, D_MODEL)}
BIG_NAMES = ("w_in", "w_glu", "w_out", "w_ff1", "w_ff2")


def _cols_from_slots(g):
    return jnp.transpose(g, (1, 0, 2)).reshape(g.shape[1], N_DEV * g.shape[2])


def _cols_to_slots(dw):
    r, cols = dw.shape
    return jnp.transpose(dw.reshape(r, N_DEV, cols // N_DEV), (1, 0, 2))


WEIGHT_GROUPS = {"in": ("w_in",), "mix": ("w_glu", "w_out"), "mlp": ("w_ff1", "w_ff2")}


def _weight_from_slots(name, g):
    if name == "w_glu":
        return _cols_from_slots(g)
    if name == "w_ff1":
        return g
    return g.reshape(N_DEV * g.shape[1], g.shape[2])


def _grad_slots(name, dw):
    if name == "w_glu":
        return _cols_to_slots(dw)
    if name == "w_ff1":
        return dw
    if name == "w_in":
        return dw.reshape(N_DEV, BIG_SHAPES[name][1], BIG_SHAPES[name][0])
    return dw.reshape((N_DEV,) + BIG_SHAPES[name])


PIECE_ROWS = 8


VEC_NAMES = tuple(n for n in SMALL_NAMES if n[:6] not in ("ssm_b_", "ssm_c_"))
BC_NAMES = ("ssm_b_re", "ssm_b_im", "ssm_c_re", "ssm_c_im")
BC_ROWS = N_GROUP * SSM_GC


def _bc_view(name, t):
    t = t[0]
    if name.startswith("ssm_b_"):
        t = jnp.swapaxes(t, 1, 2)
    return t.reshape(BC_ROWS, N_STATE)


def _bc_unview(name, t):
    t = t.reshape(N_GROUP, SSM_GC, N_STATE)
    if name.startswith("ssm_b_"):
        t = jnp.swapaxes(t, 1, 2)
    return t[None]


def _pack_bc(vals):
    return jnp.concatenate([_bc_view(n, vals[n]).astype(_F32) for n in BC_NAMES], axis=0)


def _unpack_bc(buf):
    return {n: _bc_unview(n, buf[j * BC_ROWS:(j + 1) * BC_ROWS]) for j, n in enumerate(BC_NAMES)}


def _small_layout(shapes):
    off, rows = {}, 0
    for n in VEC_NAMES:
        off[n] = rows
        rows += -(-math.prod(shapes[n]) // (PIECE_ROWS * LANES)) * PIECE_ROWS
    return off, rows, rows + PIECE_ROWS


def _pack_small(vals, shapes, last=None):
    parts = []
    for n in VEC_NAMES:
        flat = vals[n].reshape(-1).astype(_F32)
        pad = -flat.shape[0] % (PIECE_ROWS * LANES)
        if pad:
            flat = jnp.concatenate([flat, jnp.zeros((pad,), _F32)])
        parts.append(flat.reshape(-1, LANES))
    parts.append(jnp.zeros((PIECE_ROWS, LANES), _F32) if last is None else last)
    return jnp.concatenate(parts, axis=0)


def _unpack_small(buf, shapes):
    off, _, _ = _small_layout(shapes)
    out = {}
    for n in VEC_NAMES:
        size = math.prod(shapes[n])
        rows = -(-size // LANES)
        out[n] = buf[off[n]:off[n] + rows].reshape(-1)[:size].reshape(shapes[n])
    return out


WEIGHT_NAMES = ('norm_mix_pre', 'norm_mix_post', 'w_in', 'ret_gn_gain', 'ssm_lambda_re', 'ssm_lambda_im', 'ssm_log_dt',
                'ssm_b_re', 'ssm_b_im', 'ssm_c_re', 'ssm_c_im', 'ssm_d', 'w_glu', 'w_out', 'norm_mlp_pre',
                'norm_mlp_post', 'w_ff1', 'w_ff2')


def kernel(x, norm_mix_pre, norm_mix_post, w_in, ret_gn_gain, ssm_lambda_re, ssm_lambda_im, ssm_log_dt, ssm_b_re, ssm_b_im, ssm_c_re, ssm_c_im, ssm_d, w_glu, w_out, norm_mlp_pre, norm_mlp_post, w_ff1, w_ff2, loss_target, m_norm_mix_pre, m_norm_mix_post, m_w_in, m_ret_gn_gain, m_ssm_lambda_re, m_ssm_lambda_im, m_ssm_log_dt, m_ssm_b_re, m_ssm_b_im, m_ssm_c_re, m_ssm_c_im, m_ssm_d, m_w_glu, m_w_out, m_norm_mlp_pre, m_norm_mlp_post, m_w_ff1, m_w_ff2, v_norm_mix_pre, v_norm_mix_post, v_w_in, v_ret_gn_gain, v_ssm_lambda_re, v_ssm_lambda_im, v_ssm_log_dt, v_ssm_b_re, v_ssm_b_im, v_ssm_c_re, v_ssm_c_im, v_ssm_d, v_w_glu, v_w_out, v_norm_mlp_pre, v_norm_mlp_post, v_w_ff1, v_w_ff2):
    args = dict(locals())
    w = {n: args[n] for n in WEIGHT_NAMES}
    m = {n: args["m_" + n] for n in WEIGHT_NAMES}
    v = {n: args["v_" + n] for n in WEIGHT_NAMES}
    L = x.shape[1]
    tm = min(256, L)
    tk = min(2048, L)
    tb = min(1024, L)

    order = [n for names in WEIGHT_GROUPS.values() for n in names]
    blocks = [(w[n][0].T if n == "w_in" else w[n][0]).astype(_BF) for n in order]
    gathered = _split_start(blocks, [_landing(b) for b in blocks], True, "weights_start")
    zero = gathered[4][0, 0]

    def weights(group, after):
        names = WEIGHT_GROUPS[group]
        first = order.index(names[0])
        part = slice(first, first + len(names))
        landed = _split_wait(gathered[0], gathered[1], gathered[2][part], gathered[3][part], after, True,
                             "weights_wait_" + group, first=first)
        return [_weight_from_slots(n, g) for n, g in zip(names, landed)]

    in_flight = []

    def emit(dws):
        names = sorted(dws)
        srcs = [_grad_slots(n, dws[n]) for n in names]
        lands = [_landing(lax.dynamic_index_in_dim(t, _my_index(), 0, keepdims=False)) for t in srcs]
        started = _split_start(srcs, lands, False, "grads_start_" + "_".join(names))
        in_flight.append((names, started))
        return (started[4],)

    shapes = {n: w[n].shape for n in SMALL_NAMES}
    first_piece = {SMALL_NAMES[0]: jnp.zeros(shapes[SMALL_NAMES[0]], _F32)}
    small_flight = []

    def emit_small(gs, sq):
        loss_rows = jnp.broadcast_to(0.5 / D_MODEL * jnp.sum(sq), (PIECE_ROWS, LANES)).astype(_F32)
        bufs = [_pack_small({**first_piece, **gs}, shapes, loss_rows), _pack_bc(gs)]
        small_flight.append(_split_start(bufs, [_landing(b) for b in bufs], True, "small_grads_start"))
        return (small_flight[0][4],)

    small_w = {n: w[n] for n in SMALL_NAMES}
    gx, dg1 = _local_grads(x[0], loss_target[0], small_w, weights, emit, emit_small, tm, tk, tb, zero=zero)
    last_buf = dg1.reshape(PIECE_ROWS, LANES)
    last_started = _split_start([last_buf], [_landing(last_buf)], True, "last_grad_start")

    grads, delta, new_m, new_v = {}, {}, {}, {}
    after = last_started[4]
    for names, started in in_flight:
        landed = _split_wait(*started[:4], after, False, "grads_wait_" + "_".join(names))
        for n, parts in zip(names, landed):
            flip = (lambda t: t.T) if n == "w_in" else (lambda t: t)
            res = _sum_adamw(parts, flip(w[n][0]), flip(m[n][0]), flip(v[n][0]), math.gcd(256, parts.shape[1]), "adamw_" + n)
            grads[n], delta[n], new_m[n], new_v[n] = (flip(t)[None] for t in res)
        after = res[1]
    small_parts, bc_parts = _split_wait(*small_flight[0][:4], after, True, "small_grads_wait")
    last_parts = _split_wait(*last_started[:4], small_parts, True, "last_grad_wait")[0]
    small_parts = lax.dynamic_update_slice(small_parts, last_parts, (0, 0, 0))
    res_bc = _sum_adamw(bc_parts, _pack_bc(w), _pack_bc(m), _pack_bc(v), BC_ROWS, "adamw_bc")
    sw, sm, sv = _pack_small(w, shapes), _pack_small(m, shapes), _pack_small(v, shapes)
    res = _sum_adamw(small_parts, sw, sm, sv, sw.shape[0], "adamw_small")
    for dst, buf, buf_bc in zip((grads, delta, new_m, new_v), res, res_bc):
        dst.update(_unpack_small(buf, shapes))
        dst.update(_unpack_bc(buf_bc))
    _, loss_at, _ = _small_layout(shapes)
    loss = res[0][loss_at, 0]

    return (loss, gx[None], *[grads[n] for n in WEIGHT_NAMES], *[delta[n] for n in WEIGHT_NAMES],
            *[new_m[n] for n in WEIGHT_NAMES], *[new_v[n] for n in WEIGHT_NAMES])
```

```python
import math

import jax
import jax.numpy as jnp
from jax import lax
from jax.experimental import pallas as pl
from jax.experimental.pallas import tpu as pltpu

_BF = jnp.bfloat16
_F32 = jnp.float32

D_MODEL = 1024
RET_W = 512
N_HEAD = 4
HEAD_D = 128
CHUNK = 256
ROPE_CHUNK = 128
SSM_W = 512
SSM_GC = 16
N_GROUP = 32
N_STATE = 64
GROUPS_PER_KB = 8
N_KB = 4
KB_STATES = GROUPS_PER_KB * N_STATE
D_FF = 4096
IN_COLS = 2560
NORM_EPS = 1e-6
ROPE_BASE = 10000.0
N_DEV = 8

ADAM_LR = 0.001
ADAM_B1 = 0.9
ADAM_B2 = 0.999
ADAM_EPS = 1e-08
ADAM_WD = 0.01
ADAM_STEP = 10

SUBLANES = 8
LANES = 128
VMEM_LIMIT = 52 * 1024 * 1024
RET_STEP_CHUNKS = 4
KB_PER_STEP = 2
SCAN_UNROLL = True
FIX_UNROLL = 8

MESH = pl.DeviceIdType.MESH


def _params(*sem):
    return pltpu.CompilerParams(dimension_semantics=sem, vmem_limit_bytes=VMEM_LIMIT)


def _dot(a, b):
    return jnp.dot(a, b, preferred_element_type=_F32)


def _dot_nt(a, b):
    return lax.dot_general(a, b, (((1,), (1,)), ((), ())), preferred_element_type=_F32)


def _dot_tn(a, b):
    return lax.dot_general(a, b, (((0,), (0,)), ((), ())), preferred_element_type=_F32)


def _rms_r(z):
    return lax.rsqrt(jnp.mean(z * z, axis=-1, keepdims=True) + NORM_EPS)


def _rms_bwd(z, g, dn):
    r = _rms_r(z)
    t = dn * g
    dz = r * t - z * (r * r * r * jnp.mean(t * z, axis=-1, keepdims=True))
    return dz, dn * z * r


def _rope(t, cs, sn):
    return t * cs + pltpu.roll(t, HEAD_D // 2, 1) * sn


def _rope_t(t, cs, sn):
    return t * cs - pltpu.roll(t, HEAD_D // 2, 1) * sn


def _sigmoid(z):
    return 1.0 / (1.0 + jnp.exp(-z))


_GELU_C = math.sqrt(2.0 / math.pi)


def _gelu(z):
    return 0.5 * z * (1.0 + jnp.tanh(_GELU_C * (z + 0.044715 * z * z * z)))


def _gelu_grad(z):
    th = jnp.tanh(_GELU_C * (z + 0.044715 * z * z * z))
    return 0.5 * (1.0 + th) + 0.5 * z * (1.0 - th * th) * _GELU_C * (1.0 + 3 * 0.044715 * z * z)


ROW_CHUNK = 256


def _row_chunks(tm):
    return [pl.ds(i, min(ROW_CHUNK, tm)) for i in range(0, tm, ROW_CHUNK)]


def _ordered(body, in_specs, operands, after):
    k = len(after)
    if not k:
        return body, list(in_specs), tuple(operands)
    return ((lambda *refs: body(*refs[k:])), [pl.BlockSpec(memory_space=pl.ANY)] * k + list(in_specs),
            tuple(after) + tuple(operands))


def _row_spec(tm, n):
    return pl.BlockSpec((tm, n), lambda i: (i, 0))


def _full_spec(shape):
    nd = len(shape)
    return pl.BlockSpec(shape, lambda *_: (0,) * nd)


def _weight_spec(shape):
    nd = len(shape)
    return pl.BlockSpec(shape, lambda *_: (0,) * nd, pipeline_mode=pl.Buffered(1))


def _rope_tables(L):
    half = HEAD_D // 2
    inv_freq = ROPE_BASE ** (-jnp.arange(half, dtype=_F32) / half)
    twice = lambda t: jnp.concatenate([t, t], axis=-1)
    off = jnp.arange(ROPE_CHUNK, dtype=_F32)[:, None] * inv_freq[None, :]
    start = (ROPE_CHUNK * jnp.arange(L // ROPE_CHUNK, dtype=_F32))[:, None] * inv_freq[None, :]
    return (twice(jnp.cos(off)), twice(jnp.sin(off)),
            twice(jnp.cos(start))[:, None, :], twice(jnp.sin(start))[:, None, :])


def _prenorm(x, g, tm, after=()):
    L = x.shape[0]

    def body(x_ref, g_ref, h_ref):
        xv = x_ref[...]
        h_ref[...] = (xv * _rms_r(xv) * g_ref[...]).astype(_BF)

    body, in_specs, operands = _ordered(body, [_row_spec(tm, D_MODEL), _full_spec((1, D_MODEL))], (x, g), after)
    return pl.pallas_call(
        body, name="prenorm", grid=(L // tm,),
        in_specs=in_specs, out_specs=_row_spec(tm, D_MODEL),
        out_shape=jax.ShapeDtypeStruct((L, D_MODEL), _BF),
        compiler_params=_params("parallel"),
    )(*operands)


def _inproj_fwd(h, w_in_t, rope, tm):
    L = h.shape[0]
    n_chunks = tm // ROPE_CHUNK

    def body(h_ref, w_ref, co_ref, so_ref, cs_ref, ss_ref, q_ref, k_ref, v_ref, gate_ref, u_ref, cos_ref, sin_ref):
        proj = _dot_nt(h_ref[...], w_ref[...])
        lane = lax.broadcasted_iota(jnp.int32, (ROPE_CHUNK, HEAD_D), 1)
        sign = jnp.where(lane < HEAD_D // 2, -1.0, 1.0)
        co, so = co_ref[...], so_ref[...]
        for c in range(n_chunks):
            chunk = pl.program_id(0) * n_chunks + c
            cst, sst = cs_ref[chunk], ss_ref[chunk]
            rows = slice(c * ROPE_CHUNK, (c + 1) * ROPE_CHUNK)
            cs = co * cst - so * sst
            sn = (so * cst + co * sst) * sign
            cos_ref[rows, :] = cs
            sin_ref[rows, :] = sn
            for hh in range(N_HEAD):
                lo = hh * HEAD_D
                q_ref[rows, lo:lo + HEAD_D] = _rope(proj[rows, lo:lo + HEAD_D], cs, sn).astype(_BF)
                kh = _rope(proj[rows, RET_W + lo:RET_W + lo + HEAD_D], cs, sn) * (HEAD_D ** -0.5)
                k_ref[rows, lo:lo + HEAD_D] = kh.astype(_BF)
        v_ref[...] = proj[:, 2 * RET_W:3 * RET_W].astype(_BF)
        gate_ref[...] = proj[:, 3 * RET_W:4 * RET_W]
        u_ref[...] = proj[:, 4 * RET_W:]

    nc = L // ROPE_CHUNK
    return pl.pallas_call(
        body, name="inproj_fwd", grid=(L // tm,),
        in_specs=[_row_spec(tm, D_MODEL), _weight_spec((IN_COLS, D_MODEL)),
                  _full_spec((ROPE_CHUNK, HEAD_D)), _full_spec((ROPE_CHUNK, HEAD_D)),
                  _full_spec((nc, 1, HEAD_D)), _full_spec((nc, 1, HEAD_D))],
        out_specs=[_row_spec(tm, RET_W)] * 5 + [_row_spec(tm, HEAD_D)] * 2,
        out_shape=[jax.ShapeDtypeStruct((L, RET_W), _BF)] * 3 + [jax.ShapeDtypeStruct((L, RET_W), _F32)] * 2
        + [jax.ShapeDtypeStruct((L, HEAD_D), _F32)] * 2,
        compiler_params=_params("parallel"),
    )(h, w_in_t, *rope)


def _ret_consts():
    lg = jnp.log(1.0 - jnp.exp(jnp.linspace(math.log(1.0 / 32), math.log(1.0 / 512), N_HEAD))).astype(_F32)
    idx = jnp.arange(CHUNK, dtype=_F32)
    diff = idx[:, None] - idx[None, :]
    decay = jnp.where(diff[None] >= 0, jnp.exp(jnp.maximum(diff, 0.0)[None] * lg[:, None, None]), 0.0)
    zeta = jnp.exp((CHUNK - 1 - idx)[None, :] * lg[:, None])
    xi = jnp.exp((idx + 1.0)[None, :] * lg[:, None])
    gc = jnp.exp(CHUNK * lg)
    wide = lambda t: jnp.broadcast_to(t[:, :, None], (N_HEAD, CHUNK, HEAD_D)).astype(_F32)
    gcw = jnp.broadcast_to(gc[:, None, None], (N_HEAD, SUBLANES, HEAD_D)).astype(_F32)
    return decay.astype(_F32), wide(xi), wide(zeta), gcw


def _head_specs():
    wide = _full_spec((N_HEAD, CHUNK, HEAD_D))
    return [_full_spec((N_HEAD, CHUNK, CHUNK)), wide, wide, _full_spec((N_HEAD, SUBLANES, HEAD_D))]


def _retention_fwd(q, k, v, gate, ggn, consts):
    L = q.shape[0]
    nc = L // CHUNK
    cps = math.gcd(RET_STEP_CHUNKS, nc)
    blk = pl.BlockSpec((cps * CHUNK, RET_W), lambda n: (n, 0))

    def body(q_ref, k_ref, v_ref, gate_ref, ggn_ref, dm_ref, xi_ref, zeta_ref, gc_ref,
             o_ref, y_ref, rp_ref, r_scr):
        @pl.when(pl.program_id(0) == 0)
        def _():
            r_scr[...] = jnp.zeros_like(r_scr)

        for hh in range(N_HEAD):
            cols = slice(hh * HEAD_D, (hh + 1) * HEAD_D)
            state = r_scr[hh]
            for c in range(cps):
                rows = slice(c * CHUNK, (c + 1) * CHUNK)
                qv, kv, vv = q_ref[rows, cols], k_ref[rows, cols], v_ref[rows, cols]
                s = _dot_nt(qv, kv) * dm_ref[hh]
                o = _dot(s.astype(_BF), vv) + _dot(qv, state.astype(_BF)) * xi_ref[hh]
                o_ref[rows, cols] = o
                rp_ref[hh, c] = state
                vz = (vv.astype(_F32) * zeta_ref[hh]).astype(_BF)
                state = gc_ref[hh, 0:1, :] * state + _dot_tn(kv, vz)
                dlt = o - jnp.mean(o, axis=-1, keepdims=True)
                on = dlt * lax.rsqrt(jnp.mean(dlt * dlt, axis=-1, keepdims=True) + NORM_EPS)
                gt = gate_ref[rows, cols]
                y_ref[rows, cols] = (gt * _sigmoid(gt) * (on * ggn_ref[:, cols])).astype(_BF)
            r_scr[hh] = state

    return pl.pallas_call(
        body, name="retention_fwd", grid=(nc // cps,),
        in_specs=[blk, blk, blk, blk, _full_spec((1, RET_W))] + _head_specs(),
        out_specs=[blk, blk, pl.BlockSpec((N_HEAD, cps, HEAD_D, HEAD_D), lambda n: (0, n, 0, 0))],
        out_shape=[jax.ShapeDtypeStruct((L, RET_W), _F32), jax.ShapeDtypeStruct((L, RET_W), _BF),
                   jax.ShapeDtypeStruct((N_HEAD, nc, HEAD_D, HEAD_D), _F32)],
        scratch_shapes=[pltpu.VMEM((N_HEAD, HEAD_D, HEAD_D), _F32)],
        compiler_params=_params("arbitrary"),
    )(q, k, v, gate, ggn, *consts)


def _rows_to_segments(dst_scr, src_ref, seg):
    for g in range(dst_scr.shape[0]):
        for j in range(SUBLANES):
            dst_scr[g, pl.ds(j, seg, stride=SUBLANES), :] = src_ref[pl.ds(j * seg, seg), g * LANES:(g + 1) * LANES]


def _segments_to_rows(dst_ref, src_scr, seg):
    for g in range(src_scr.shape[0]):
        for j in range(SUBLANES):
            dst_ref[pl.ds(j * seg, seg), g * LANES:(g + 1) * LANES] = src_scr[g, pl.ds(j, seg, stride=SUBLANES), :]


def _scan_segments(x_ref, tab_ref, pw_ref, carry_ref, seg, reverse, entry_ref=None, fwd_ref=None, fwd_entry_ref=None,
                   da_ref=None):
    G = x_ref.shape[0]
    W = KB_STATES
    re, im = pl.ds(0, W), pl.ds(W, W)
    row_id = lax.broadcasted_iota(jnp.int32, (SUBLANES, W), 0)
    edge_in = (row_id == SUBLANES - 1) if reverse else (row_id == 0)
    edge_out = 0 if reverse else SUBLANES - 1
    a_tab = [(tab_ref[g, 0], tab_ref[g, 1]) for g in range(G)]

    def local(i, st):
        r = (seg - 1 - i) if reverse else i
        out = []
        for g in range(G):
            (ar, ai), (sr, si) = a_tab[g], st[g]
            nr = ar * sr - ai * si + x_ref[g, r, :, re]
            ni = ar * si + ai * sr + x_ref[g, r, :, im]
            x_ref[g, r, :, re] = nr
            x_ref[g, r, :, im] = ni
            out.append((nr, ni))
        return tuple(out)

    zero = jnp.zeros((SUBLANES, W), _F32)
    ends = lax.fori_loop(0, seg, local, tuple((zero, zero) for _ in range(G)), unroll=SCAN_UNROLL)

    entry = []
    shift = (SUBLANES - 1) if reverse else 1
    for g in range(G):
        er, ei = ends[g]
        fr = jnp.where(edge_in, carry_ref[g, :, re], pltpu.roll(er, shift, 0))
        fi = jnp.where(edge_in, carry_ref[g, :, im], pltpu.roll(ei, shift, 0))
        for j, dist in enumerate((1, 2, 4)):
            pr, pi = tab_ref[g, 2 + 2 * j], tab_ref[g, 3 + 2 * j]
            sh = (SUBLANES - dist) if reverse else dist
            sr, si = pltpu.roll(fr, sh, 0), pltpu.roll(fi, sh, 0)
            fr, fi = fr + pr * sr - pi * si, fi + pr * si + pi * sr
        br, bi = tab_ref[g, 8], tab_ref[g, 9]
        outr = br * fr - bi * fi + er
        outi = br * fi + bi * fr + ei
        carry_ref[g, :, re] = jnp.broadcast_to(outr[edge_out:edge_out + 1, :], (SUBLANES, W))
        carry_ref[g, :, im] = jnp.broadcast_to(outi[edge_out:edge_out + 1, :], (SUBLANES, W))
        entry.append((fr, fi))
        if entry_ref is not None:
            entry_ref[g, :, re] = fr
            entry_ref[g, :, im] = fi

    add_da = da_ref is not None

    def fix(r, st, first=False):
        out = []
        for g in range(G):
            fr, fi = entry[g]
            pwr, pwi = pw_ref[g, r, :, re], pw_ref[g, r, :, im]
            xr = x_ref[g, r, :, re] + (pwr * fr - pwi * fi)
            xi = x_ref[g, r, :, im] + (pwr * fi + pwi * fr)
            x_ref[g, r, :, re] = xr
            x_ref[g, r, :, im] = xi
            if add_da:
                prev = fwd_entry_ref.at[g] if first else fwd_ref.at[g, r - 1]
                xpr, xpi = prev[:, re], prev[:, im]
                out.append((st[g][0] + (xr * xpr + xi * xpi), st[g][1] + (xi * xpr - xr * xpi)))
            else:
                out.append(st[g])
        return tuple(out)

    if add_da:
        st = fix(0, tuple((zero, zero) for _ in range(G)), first=True)
        st = lax.fori_loop(1, seg, fix, st, unroll=SCAN_UNROLL)
        for g in range(G):
            da_ref[g, :, re] += st[g][0]
            da_ref[g, :, im] += st[g][1]
    else:
        lax.fori_loop(0, seg, fix, tuple((zero[0:1, 0:LANES],) for _ in range(G)), unroll=FIX_UNROLL)


def _s5_specs(seg, time=lambda t: t):
    G = KB_PER_STEP
    return dict(
        x=pl.BlockSpec((G, seg, SUBLANES, 2 * KB_STATES), lambda kb, t: (kb, time(t), 0, 0)),
        ent=pl.BlockSpec((G, 1, SUBLANES, 2 * KB_STATES), lambda kb, t: (kb, time(t), 0, 0)),
        b=pl.BlockSpec((G, LANES, 2 * KB_STATES), lambda kb, t: (kb, 0, 0)),
        c=pl.BlockSpec((G, 2 * KB_STATES, LANES), lambda kb, t: (kb, 0, 0)),
        tab=pl.BlockSpec((G, 10, SUBLANES, KB_STATES), lambda kb, t: (kb, 0, 0, 0)),
        pw=pl.BlockSpec((G, seg, 1, 2 * KB_STATES), lambda kb, t: (kb, 0, 0, 0)),
        d=pl.BlockSpec((1, G * LANES), lambda kb, t: (0, kb)),
    )


def _s5_fwd(u, bmat, cmat, tab_f, pw_f, d_skip, tb):
    L = u.shape[0]
    nt = L // tb
    seg = tb // SUBLANES
    G = KB_PER_STEP
    ucol = pl.BlockSpec((tb, G * LANES), lambda kb, t: (t, kb))
    sp = _s5_specs(seg)

    def body(u_ref, b_ref, c_ref, tab_ref, pw_ref, d_ref, s_ref, x_ref, ent_ref, up_scr, y_scr, carry_scr):
        @pl.when(pl.program_id(1) == 0)
        def _():
            carry_scr[...] = jnp.zeros_like(carry_scr)

        _rows_to_segments(up_scr, u_ref, seg)
        for g in range(G):
            x_ref[g] = _dot(up_scr[g].astype(_BF), b_ref[g]).reshape(seg, SUBLANES, 2 * KB_STATES)
        _scan_segments(x_ref, tab_ref, pw_ref, carry_scr, seg, reverse=False, entry_ref=ent_ref.at[:, 0])
        for g in range(G):
            y = _dot(x_ref[g].reshape(tb, 2 * KB_STATES).astype(_BF), c_ref[g])
            y_scr[g] = y + d_ref[:, g * LANES:(g + 1) * LANES] * up_scr[g]
        _segments_to_rows(s_ref, y_scr, seg)

    return pl.pallas_call(
        body, name="s5_fwd", grid=(N_KB // G, nt),
        in_specs=[ucol, sp["b"], sp["c"], sp["tab"], sp["pw"], sp["d"]],
        out_specs=[ucol, sp["x"], sp["ent"]],
        out_shape=[jax.ShapeDtypeStruct((L, SSM_W), _F32),
                   jax.ShapeDtypeStruct((N_KB, L // SUBLANES, SUBLANES, 2 * KB_STATES), _F32),
                   jax.ShapeDtypeStruct((N_KB, nt, SUBLANES, 2 * KB_STATES), _F32)],
        scratch_shapes=[pltpu.VMEM((G, tb, LANES), _F32)] * 2 + [pltpu.VMEM((G, SUBLANES, 2 * KB_STATES), _F32)],
        compiler_params=_params("parallel", "arbitrary"),
    )(u, bmat, cmat, tab_f, pw_f, d_skip)


def _mixout_fwd(s, y_ret, x, w_glu, w_out, g2, tm):
    L = s.shape[0]

    def body(s_ref, yr_ref, x_ref, wg_ref, wo_ref, g_ref, ys_ref, glu_ref, cat_ref, mix_ref, x2_ref):
        for rows in _row_chunks(tm):
            ys = _gelu(s_ref[rows, :]).astype(_BF)
            ys_ref[rows, :] = ys
            glu = _dot(ys, wg_ref[...])
            glu_ref[rows, :] = glu
            cat_ref[rows, :RET_W] = yr_ref[rows, :]
            cat_ref[rows, RET_W:] = (glu[:, :SSM_W] * _sigmoid(glu[:, SSM_W:])).astype(_BF)
            mix = _dot(cat_ref[rows, :], wo_ref[...])
            mix_ref[rows, :] = mix
            x2_ref[rows, :] = x_ref[rows, :] + mix * _rms_r(mix) * g_ref[...]

    return pl.pallas_call(
        body, name="mixout_fwd", grid=(L // tm,),
        in_specs=[_row_spec(tm, SSM_W), _row_spec(tm, RET_W), _row_spec(tm, D_MODEL),
                  _weight_spec((SSM_W, 2 * SSM_W)), _weight_spec((D_MODEL, D_MODEL)), _full_spec((1, D_MODEL))],
        out_specs=[_row_spec(tm, SSM_W), _row_spec(tm, 2 * SSM_W), _row_spec(tm, D_MODEL),
                   _row_spec(tm, D_MODEL), _row_spec(tm, D_MODEL)],
        out_shape=[jax.ShapeDtypeStruct((L, SSM_W), _BF), jax.ShapeDtypeStruct((L, 2 * SSM_W), _F32),
                   jax.ShapeDtypeStruct((L, D_MODEL), _BF), jax.ShapeDtypeStruct((L, D_MODEL), _F32),
                   jax.ShapeDtypeStruct((L, D_MODEL), _F32)],
        compiler_params=_params("parallel"),
    )(s, y_ret, x, w_glu, w_out, g2)


FF1_COLS = D_FF // N_DEV


def _ff1_fwd(x2, g3, w1, tm):
    L = x2.shape[0]

    def body(x_ref, g_ref, w_ref, h_ref, f_ref, a_ref):
        for rows in _row_chunks(tm):
            xv = x_ref[rows, :]
            h = (xv * _rms_r(xv) * g_ref[...]).astype(_BF)
            h_ref[rows, :] = h
            for j in range(N_DEV):
                cols = slice(j * FF1_COLS, (j + 1) * FF1_COLS)
                f = _dot(h, w_ref[j])
                f_ref[rows, cols] = f
                rl = jnp.maximum(f, 0.0)
                a_ref[rows, cols] = (rl * rl).astype(_BF)

    return pl.pallas_call(
        body, name="ff1_fwd", grid=(L // tm,),
        in_specs=[_row_spec(tm, D_MODEL), _full_spec((1, D_MODEL)), _weight_spec((N_DEV, D_MODEL, FF1_COLS))],
        out_specs=[_row_spec(tm, D_MODEL), _row_spec(tm, D_FF), _row_spec(tm, D_FF)],
        out_shape=[jax.ShapeDtypeStruct((L, D_MODEL), _BF), jax.ShapeDtypeStruct((L, D_FF), _F32),
                   jax.ShapeDtypeStruct((L, D_FF), _BF)],
        compiler_params=_params("parallel"),
    )(x2, g3, w1)


def _ff2_loss(act, x2, tgt, g4, w2, tm):
    L = act.shape[0]

    def body(f_ref, x_ref, t_ref, g_ref, w_ref, dy_ref, dm_ref, dg_ref, ls_ref):
        @pl.when(pl.program_id(0) == 0)
        def _():
            dg_ref[...] = jnp.zeros_like(dg_ref)
            ls_ref[...] = jnp.zeros_like(ls_ref)

        g = g_ref[...]
        for rows in _row_chunks(tm):
            m = _dot(f_ref[rows, :], w_ref[...])
            y = x_ref[rows, :] + m * _rms_r(m) * g
            err = y - t_ref[rows, :]
            ls_ref[...] += jnp.sum(err * err, axis=0, keepdims=True)
            dy = err * (1.0 / D_MODEL)
            dy_ref[rows, :] = dy
            dm, dgr = _rms_bwd(m, g, dy)
            dm_ref[rows, :] = dm.astype(_BF)
            dg_ref[...] += jnp.sum(dgr, axis=0, keepdims=True)

    return pl.pallas_call(
        body, name="ff2_loss", grid=(L // tm,),
        in_specs=[_row_spec(tm, D_FF), _row_spec(tm, D_MODEL), _row_spec(tm, D_MODEL),
                  _full_spec((1, D_MODEL)), _weight_spec((D_FF, D_MODEL))],
        out_specs=[_row_spec(tm, D_MODEL), _row_spec(tm, D_MODEL), _full_spec((1, D_MODEL)), _full_spec((1, D_MODEL))],
        out_shape=[jax.ShapeDtypeStruct((L, D_MODEL), _F32), jax.ShapeDtypeStruct((L, D_MODEL), _BF),
                   jax.ShapeDtypeStruct((1, D_MODEL), _F32), jax.ShapeDtypeStruct((1, D_MODEL), _F32)],
        compiler_params=_params("arbitrary"),
    )(act, x2, tgt, g4, w2)


def _ff2_bwd(dm, f1, w2, tm, tn):
    L = dm.shape[0]
    last = L // tm - 1

    def body(dm_ref, f_ref, w_ref, df_ref, dw_ref, acc):
        @pl.when(pl.program_id(1) == 0)
        def _():
            acc[...] = jnp.zeros_like(acc)

        dmv = dm_ref[...]
        rl = jnp.maximum(f_ref[...], 0.0)
        df_ref[...] = (_dot_nt(dmv, w_ref[...]) * (2.0 * rl)).astype(_BF)
        acc[...] += _dot_tn((rl * rl).astype(_BF), dmv)

        @pl.when(pl.program_id(1) == last)
        def _():
            dw_ref[...] = acc[...].astype(_BF)

    return pl.pallas_call(
        body, name="ff2_bwd", grid=(D_FF // tn, L // tm),
        in_specs=[pl.BlockSpec((tm, D_MODEL), lambda j, i: (i, 0)), pl.BlockSpec((tm, tn), lambda j, i: (i, j)),
                  pl.BlockSpec((tn, D_MODEL), lambda j, i: (j, 0))],
        out_specs=[pl.BlockSpec((tm, tn), lambda j, i: (i, j)), pl.BlockSpec((tn, D_MODEL), lambda j, i: (j, 0))],
        out_shape=[jax.ShapeDtypeStruct((L, D_FF), _BF), jax.ShapeDtypeStruct((D_FF, D_MODEL), _BF)],
        scratch_shapes=[pltpu.VMEM((tn, D_MODEL), _F32)],
        compiler_params=_params("parallel", "arbitrary"),
    )(dm, f1, w2)


def _ff1_bwd(df1, w1, x2, mix, dy, g3, g2, tm):
    L = df1.shape[0]

    def body(df_ref, w_ref, x2_ref, mix_ref, dy_ref, g3_ref, g2_ref, dx2_ref, dmix_ref, dg3_ref, dg2_ref):
        @pl.when(pl.program_id(0) == 0)
        def _():
            dg3_ref[...] = jnp.zeros_like(dg3_ref)
            dg2_ref[...] = jnp.zeros_like(dg2_ref)

        for rows in _row_chunks(tm):
            dh = _dot_nt(df_ref[rows, 0:FF1_COLS], w_ref[0])
            for j in range(1, N_DEV):
                dh = dh + _dot_nt(df_ref[rows, j * FF1_COLS:(j + 1) * FF1_COLS], w_ref[j])
            dz, dgr = _rms_bwd(x2_ref[rows, :], g3_ref[...], dh)
            dg3_ref[...] += jnp.sum(dgr, axis=0, keepdims=True)
            dx2 = dy_ref[rows, :] + dz
            dx2_ref[rows, :] = dx2
            dmx, dgr2 = _rms_bwd(mix_ref[rows, :], g2_ref[...], dx2)
            dg2_ref[...] += jnp.sum(dgr2, axis=0, keepdims=True)
            dmix_ref[rows, :] = dmx.astype(_BF)

    vec = _full_spec((1, D_MODEL))
    return pl.pallas_call(
        body, name="ff1_bwd", grid=(L // tm,),
        in_specs=[_row_spec(tm, D_FF), _weight_spec((N_DEV, D_MODEL, FF1_COLS)), _row_spec(tm, D_MODEL),
                  _row_spec(tm, D_MODEL), _row_spec(tm, D_MODEL), vec, vec],
        out_specs=[_row_spec(tm, D_MODEL), _row_spec(tm, D_MODEL), vec, vec],
        out_shape=[jax.ShapeDtypeStruct((L, D_MODEL), _F32), jax.ShapeDtypeStruct((L, D_MODEL), _BF),
                   jax.ShapeDtypeStruct((1, D_MODEL), _F32), jax.ShapeDtypeStruct((1, D_MODEL), _F32)],
        compiler_params=_params("arbitrary"),
    )(df1, w1, x2, mix, dy, g3, g2)


def _matmul_tn(a, b, tm, tn, name, slots=False):
    L, K = a.shape
    N = b.shape[1]
    last = L // tm - 1

    def body(a_ref, b_ref, o_ref, acc):
        @pl.when(pl.program_id(1) == 0)
        def _():
            acc[...] = jnp.zeros_like(acc)

        acc[...] += _dot_tn(a_ref[...].astype(_BF), b_ref[...].astype(_BF))

        @pl.when(pl.program_id(1) == last)
        def _():
            if slots:
                o_ref[0] = acc[...].astype(_BF)
            else:
                o_ref[...] = acc[...].astype(_BF)

    if slots:
        out_spec = pl.BlockSpec((1, K, tn), lambda j, i: (j, 0, 0))
        out_shape = jax.ShapeDtypeStruct((N // tn, K, tn), _BF)
    else:
        out_spec = pl.BlockSpec((K, tn), lambda j, i: (0, j))
        out_shape = jax.ShapeDtypeStruct((K, N), _BF)
    return pl.pallas_call(
        body, name=name, grid=(N // tn, L // tm),
        in_specs=[pl.BlockSpec((tm, K), lambda j, i: (i, 0)), pl.BlockSpec((tm, tn), lambda j, i: (i, j))],
        out_specs=out_spec, out_shape=out_shape,
        scratch_shapes=[pltpu.VMEM((K, tn), _F32)],
        compiler_params=_params("parallel", "arbitrary"),
    )(a, b)


def _dw_in_t(pieces, h, tk):
    L = h.shape[0]
    last = L // tk - 1

    def body(p0, p1, p2, p3, p4, h_ref, o_ref, acc):
        @pl.when(pl.program_id(0) == 0)
        def _():
            acc[...] = jnp.zeros_like(acc)

        hv = h_ref[...]
        for j, p in enumerate((p0, p1, p2, p3, p4)):
            acc[j * RET_W:(j + 1) * RET_W, :] += _dot_tn(p[...].astype(_BF), hv)

        @pl.when(pl.program_id(0) == last)
        def _():
            o_ref[...] = acc[...].astype(_BF)

    return pl.pallas_call(
        body, name="dw_in", grid=(L // tk,),
        in_specs=[_row_spec(tk, RET_W)] * 5 + [_row_spec(tk, D_MODEL)],
        out_specs=_full_spec((IN_COLS, D_MODEL)), out_shape=jax.ShapeDtypeStruct((IN_COLS, D_MODEL), _BF),
        scratch_shapes=[pltpu.VMEM((IN_COLS, D_MODEL), _F32)],
        compiler_params=_params("arbitrary"),
    )(*pieces, h)


def _mixout_bwd(dmix, w_out, w_glu, glu, s, o, gate, ggn, tm, after=()):
    L = dmix.shape[0]

    def body(dmix_ref, wo_ref, wg_ref, glu_ref, s_ref, o_ref, gate_ref, ggn_ref,
             dglu_ref, ds_ref, dgate_ref, do_ref, dggn_ref):
        @pl.when(pl.program_id(0) == 0)
        def _():
            dggn_ref[...] = jnp.zeros_like(dggn_ref)

        ggn = ggn_ref[...]
        for rows in _row_chunks(tm):
            dcat = _dot_nt(dmix_ref[rows, :], wo_ref[...])
            dy_ret, dy_ssm = dcat[:, :RET_W], dcat[:, RET_W:]
            glu = glu_ref[rows, :]
            ga, sg = glu[:, :SSM_W], _sigmoid(glu[:, SSM_W:])
            dga = (dy_ssm * sg).astype(_BF)
            dgb = (dy_ssm * ga * sg * (1.0 - sg)).astype(_BF)
            dglu_ref[rows, :SSM_W] = dga
            dglu_ref[rows, SSM_W:] = dgb
            dys = _dot_nt(dga, wg_ref[:, :SSM_W]) + _dot_nt(dgb, wg_ref[:, SSM_W:])
            ds_ref[rows, :] = dys * _gelu_grad(s_ref[rows, :])
            gt = gate_ref[rows, :]
            sgt = _sigmoid(gt)
            for hh in range(N_HEAD):
                cols = slice(hh * HEAD_D, (hh + 1) * HEAD_D)
                ov = o_ref[rows, cols]
                dlt = ov - jnp.mean(ov, axis=-1, keepdims=True)
                rstd = lax.rsqrt(jnp.mean(dlt * dlt, axis=-1, keepdims=True) + NORM_EPS)
                on = dlt * rstd
                dyr = dy_ret[:, cols] * (gt[:, cols] * sgt[:, cols])
                dgate_ref[rows, cols] = dy_ret[:, cols] * (on * ggn[:, cols]) * (sgt[:, cols] * (1.0 + gt[:, cols] * (1.0 - sgt[:, cols])))
                dggn_ref[:, cols] += jnp.sum(dyr * on, axis=0, keepdims=True)
                don = dyr * ggn[:, cols]
                do = rstd * (don - jnp.mean(don, axis=-1, keepdims=True) - on * jnp.mean(don * on, axis=-1, keepdims=True))
                do_ref[rows, cols] = do.astype(_BF)

    body, in_specs, operands = _ordered(
        body, [_row_spec(tm, D_MODEL), _weight_spec((D_MODEL, D_MODEL)), _weight_spec((SSM_W, 2 * SSM_W)),
               _row_spec(tm, 2 * SSM_W), _row_spec(tm, SSM_W), _row_spec(tm, RET_W), _row_spec(tm, RET_W),
               _full_spec((1, RET_W))], (dmix, w_out, w_glu, glu, s, o, gate, ggn), after)
    return pl.pallas_call(
        body, name="mixout_bwd", grid=(L // tm,),
        in_specs=in_specs,
        out_specs=[_row_spec(tm, 2 * SSM_W), _row_spec(tm, SSM_W), _row_spec(tm, RET_W), _row_spec(tm, RET_W),
                   _full_spec((1, RET_W))],
        out_shape=[jax.ShapeDtypeStruct((L, 2 * SSM_W), _BF), jax.ShapeDtypeStruct((L, SSM_W), _F32),
                   jax.ShapeDtypeStruct((L, RET_W), _F32), jax.ShapeDtypeStruct((L, RET_W), _BF),
                   jax.ShapeDtypeStruct((1, RET_W), _F32)],
        compiler_params=_params("arbitrary"),
    )(*operands)


def _s5_bwd(u, ds, xs, ent, bmat, cmat, tab_r, pw_r, d_skip, tb, after=()):
    L = u.shape[0]
    nt = L // tb
    seg = tb // SUBLANES
    G = KB_PER_STEP
    rcol = pl.BlockSpec((tb, G * LANES), lambda kb, t: (nt - 1 - t, kb))
    sp = _s5_specs(seg, time=lambda t: nt - 1 - t)
    aspec = pl.BlockSpec((G, SUBLANES, 2 * KB_STATES), lambda kb, t: (kb, 0, 0))

    def body(u_ref, ds_ref, x_ref, ent_ref, b_ref, c_ref, tr_ref, pr_ref, d_ref,
             du_ref, db_ref, dc_ref, da_ref, dd_ref, up_scr, dp_scr, g_scr, lc_scr):
        @pl.when(pl.program_id(1) == 0)
        def _():
            lc_scr[...] = jnp.zeros_like(lc_scr)
            db_ref[...] = jnp.zeros_like(db_ref)
            dc_ref[...] = jnp.zeros_like(dc_ref)
            da_ref[...] = jnp.zeros_like(da_ref)
            dd_ref[...] = jnp.zeros_like(dd_ref)

        _rows_to_segments(up_scr, u_ref, seg)
        _rows_to_segments(dp_scr, ds_ref, seg)
        for g in range(G):
            g_scr[g] = _dot_nt(dp_scr[g].astype(_BF), c_ref[g]).reshape(seg, SUBLANES, 2 * KB_STATES)
        _scan_segments(g_scr, tr_ref, pr_ref, lc_scr, seg, reverse=True, fwd_ref=x_ref, fwd_entry_ref=ent_ref.at[:, 0],
                       da_ref=da_ref)
        for g in range(G):
            cols = slice(g * LANES, (g + 1) * LANES)
            uv, dsv = up_scr[g], dp_scr[g]
            ub, dsb = uv.astype(_BF), dsv.astype(_BF)
            lamb = g_scr[g].reshape(tb, 2 * KB_STATES).astype(_BF)
            db_ref[g] += _dot_tn(ub, lamb)
            dc_ref[g] += _dot_tn(dsb, x_ref[g].reshape(tb, 2 * KB_STATES).astype(_BF))
            dd_ref[:, cols] += jnp.sum(dsv * uv, axis=0, keepdims=True)
            up_scr[g] = _dot_nt(lamb, b_ref[g]) + d_ref[:, cols] * dsv
        _segments_to_rows(du_ref, up_scr, seg)

    body, in_specs, operands = _ordered(
        body, [rcol, rcol, sp["x"], sp["ent"], sp["b"], sp["c"], sp["tab"], sp["pw"], sp["d"]],
        (u, ds, xs, ent, bmat, cmat, tab_r, pw_r, d_skip), after)
    return pl.pallas_call(
        body, name="s5_bwd", grid=(N_KB // G, nt),
        in_specs=in_specs,
        out_specs=[rcol, sp["b"], sp["b"], aspec, sp["d"]],
        out_shape=[jax.ShapeDtypeStruct((L, SSM_W), _F32),
                   jax.ShapeDtypeStruct((N_KB, LANES, 2 * KB_STATES), _F32),
                   jax.ShapeDtypeStruct((N_KB, LANES, 2 * KB_STATES), _F32),
                   jax.ShapeDtypeStruct((N_KB, SUBLANES, 2 * KB_STATES), _F32),
                   jax.ShapeDtypeStruct((1, SSM_W), _F32)],
        scratch_shapes=[pltpu.VMEM((G, tb, LANES), _F32)] * 2
        + [pltpu.VMEM((G, seg, SUBLANES, 2 * KB_STATES), _F32), pltpu.VMEM((G, SUBLANES, 2 * KB_STATES), _F32)],
        compiler_params=_params("parallel", "arbitrary"),
    )(*operands)


def _retention_bwd(q, k, v, do, r_prev, consts, cosf, sinf, after=()):
    L = q.shape[0]
    nc = L // CHUNK
    cps = math.gcd(RET_STEP_CHUNKS, nc)
    nb = nc // cps
    blk = pl.BlockSpec((cps * CHUNK, RET_W), lambda n: (nb - 1 - n, 0))
    rope_blk = pl.BlockSpec((cps * CHUNK, HEAD_D), lambda n: (nb - 1 - n, 0))

    def body(q_ref, k_ref, v_ref, do_ref, rp_ref, dm_ref, xi_ref, zeta_ref, gc_ref, cos_ref, sin_ref,
             dq_ref, dk_ref, dv_ref, g_scr):
        @pl.when(pl.program_id(0) == 0)
        def _():
            g_scr[...] = jnp.zeros_like(g_scr)

        for hh in range(N_HEAD):
            cols = slice(hh * HEAD_D, (hh + 1) * HEAD_D)
            dm, zeta = dm_ref[hh], zeta_ref[hh]
            gst = g_scr[hh]
            for c in reversed(range(cps)):
                rows = slice(c * CHUNK, (c + 1) * CHUNK)
                qv, kv, vv, dov = q_ref[rows, cols], k_ref[rows, cols], v_ref[rows, cols], do_ref[rows, cols]
                rb = rp_ref[hh, c].astype(_BF)
                gb = gst.astype(_BF)
                sb = (_dot_nt(qv, kv) * dm).astype(_BF)
                dab = (_dot_nt(dov, vv) * dm).astype(_BF)
                dox = (dov.astype(_F32) * xi_ref[hh]).astype(_BF)
                vz = (vv.astype(_F32) * zeta).astype(_BF)
                dq = _dot(dab, kv) + _dot_nt(dox, rb)
                dk = _dot_tn(dab, qv) + _dot_nt(vz, gb)
                dv = _dot_tn(sb, dov) + _dot(kv, gb) * zeta
                gst = gc_ref[hh, 0:1, :] * gst + _dot_tn(qv, dox)
                cs, sn = cos_ref[rows, :], sin_ref[rows, :]
                dq_ref[rows, cols] = _rope_t(dq, cs, sn).astype(_BF)
                dk_ref[rows, cols] = (_rope_t(dk, cs, sn) * (HEAD_D ** -0.5)).astype(_BF)
                dv_ref[rows, cols] = dv.astype(_BF)
            g_scr[hh] = gst

    body, in_specs, operands = _ordered(
        body, [blk, blk, blk, blk, pl.BlockSpec((N_HEAD, cps, HEAD_D, HEAD_D), lambda n: (0, nb - 1 - n, 0, 0))]
        + _head_specs() + [rope_blk, rope_blk], (q, k, v, do, r_prev, *consts, cosf, sinf), after)
    return pl.pallas_call(
        body, name="retention_bwd", grid=(nb,),
        in_specs=in_specs,
        out_specs=[blk, blk, blk],
        out_shape=[jax.ShapeDtypeStruct((L, RET_W), _BF)] * 3,
        scratch_shapes=[pltpu.VMEM((N_HEAD, HEAD_D, HEAD_D), _F32)],
        compiler_params=_params("arbitrary"),
    )(*operands)


def _inproj_bwd(pieces, w_in_t, x, dx2, g1, tm, after=()):
    L = x.shape[0]

    def body(p0, p1, p2, p3, p4, w_ref, x_ref, dx2_ref, g_ref, dx_ref, dg_ref):
        @pl.when(pl.program_id(0) == 0)
        def _():
            dg_ref[...] = jnp.zeros_like(dg_ref)

        for rows in _row_chunks(tm):
            dh = None
            for j, p in enumerate((p0, p1, p2, p3, p4)):
                part = _dot(p[rows, :].astype(_BF), w_ref[j * RET_W:(j + 1) * RET_W, :])
                dh = part if dh is None else dh + part
            dz, dgr = _rms_bwd(x_ref[rows, :], g_ref[...], dh)
            dx_ref[rows, :] = dx2_ref[rows, :] + dz
            dg_ref[...] += jnp.sum(dgr, axis=0, keepdims=True)

    body, in_specs, operands = _ordered(
        body, [_row_spec(tm, RET_W)] * 5 + [_weight_spec((IN_COLS, D_MODEL)), _row_spec(tm, D_MODEL),
                                             _row_spec(tm, D_MODEL), _full_spec((1, D_MODEL))],
        (*pieces, w_in_t, x, dx2, g1), after)
    return pl.pallas_call(
        body, name="inproj_bwd", grid=(L // tm,),
        in_specs=in_specs,
        out_specs=[_row_spec(tm, D_MODEL), _full_spec((1, D_MODEL))],
        out_shape=[jax.ShapeDtypeStruct((L, D_MODEL), _F32), jax.ShapeDtypeStruct((1, D_MODEL), _F32)],
        compiler_params=_params("arbitrary"),
    )(*operands)


def _sum_adamw(parts, w, m, v, tr, name):
    _, R, Cc = parts.shape

    def body(p_ref, w_ref, m_ref, v_ref, g_ref, d_ref, nm_ref, nv_ref):
        gv = p_ref[0].astype(_F32)
        for s in range(1, N_DEV):
            gv = gv + p_ref[s].astype(_F32)
        g_ref[...] = gv
        nm = ADAM_B1 * m_ref[...] + (1.0 - ADAM_B1) * gv
        nv = ADAM_B2 * v_ref[...] + (1.0 - ADAM_B2) * (gv * gv)
        m_hat = nm / (1.0 - ADAM_B1 ** ADAM_STEP)
        v_hat = nv / (1.0 - ADAM_B2 ** ADAM_STEP)
        d_ref[...] = -ADAM_LR * (m_hat / (jnp.sqrt(v_hat) + ADAM_EPS) + ADAM_WD * w_ref[...])
        nm_ref[...] = nm
        nv_ref[...] = nv

    spec = _row_spec(tr, Cc)
    return pl.pallas_call(
        body, name=name, grid=(R // tr,),
        in_specs=[pl.BlockSpec((N_DEV, tr, Cc), lambda i: (0, i, 0))] + [spec] * 3, out_specs=[spec] * 4,
        out_shape=[jax.ShapeDtypeStruct((R, Cc), _F32)] * 4,
        compiler_params=_params("parallel"),
    )(parts, w, m, v)


def _my_place():
    return lax.axis_index("x"), lax.axis_index("y"), lax.axis_index("c")


HBM_SPEC = pl.BlockSpec(memory_space=pltpu.HBM)
SEM_SPEC = pl.BlockSpec(memory_space=pltpu.SEMAPHORE)
DATAFLOW = pltpu.SideEffectType.DATAFLOW_SIDE_EFFECTING


def _my_index():
    x, y, c = _my_place()
    return 4 * x + 2 * y + c


def _landing(own_block):
    zone = lax.empty((N_DEV,) + own_block.shape, own_block.dtype)
    return lax.dynamic_update_index_in_dim(zone, own_block, _my_index(), 0)


def _split_copies(src_refs, land_refs, send_sems, recv_sems, gather, first=0):
    x, y, c = _my_place()
    me = 4 * x + 2 * y + c
    copies = []
    for a, (src, land) in enumerate(zip(src_refs, land_refs)):
        for kk in range(1, N_DEV):
            px, py, pc = x ^ (kk >> 2), y ^ ((kk >> 1) & 1), c ^ (kk & 1)
            peer = 4 * px + 2 * py + pc
            copies.append(pltpu.make_async_remote_copy(
                src_ref=src if gather else src.at[peer], dst_ref=land.at[me],
                send_sem=send_sems.at[(first + a) * 7 + kk - 1], recv_sem=recv_sems.at[(first + a) * 7 + kk - 1],
                device_id=(px, py, pc), device_id_type=MESH))
    return copies


def _split_start(srcs, lands, gather, name):
    n = len(srcs)

    def body(*refs):
        src_refs, land_refs = refs[:n], refs[n:2 * n]
        send_sems, recv_sems = refs[2 * n], refs[2 * n + 1]
        token = refs[-1]
        for cp in _split_copies(src_refs, land_refs, send_sems, recv_sems, gather):
            cp.start()
        token[...] = jnp.zeros_like(token)

    outs = pl.pallas_call(
        body, name=name,
        out_shape=(pltpu.SemaphoreType.DMA((7 * n,)), pltpu.SemaphoreType.DMA((7 * n,)),
                   *[pltpu.HBM(t.shape, t.dtype) for t in srcs], *[pltpu.HBM(t.shape, t.dtype) for t in lands],
                   jax.ShapeDtypeStruct((SUBLANES, LANES), _F32)),
        in_specs=[HBM_SPEC] * (2 * n),
        out_specs=(SEM_SPEC, SEM_SPEC, *[HBM_SPEC] * (2 * n), pl.BlockSpec(memory_space=pltpu.VMEM)),
        input_output_aliases={i: 2 + i for i in range(2 * n)},
        compiler_params=pltpu.CompilerParams(has_side_effects=DATAFLOW),
    )(*[pltpu.with_memory_space_constraint(t, pltpu.HBM) for t in list(srcs) + list(lands)])
    return outs[0], outs[1], outs[2:2 + n], outs[2 + n:2 + 2 * n], outs[-1]


def _split_wait(send_sems, recv_sems, srcs, lands, after, gather, name, first=0):
    n = len(srcs)

    def body(*refs):
        src_refs, land_refs = refs[:n], refs[n:2 * n]
        send_s, recv_s = refs[2 * n], refs[2 * n + 1]
        for cp in _split_copies(src_refs, land_refs, send_s, recv_s, gather, first):
            cp.wait_send()
            cp.wait_recv()

    outs = pl.pallas_call(
        body, name=name,
        out_shape=tuple(pltpu.HBM(t.shape, t.dtype) for t in list(srcs) + list(lands)),
        in_specs=[HBM_SPEC] * (2 * n) + [SEM_SPEC, SEM_SPEC, pl.BlockSpec(memory_space=pl.ANY)],
        out_specs=tuple([HBM_SPEC] * (2 * n)),
        input_output_aliases={i: i for i in range(2 * n)},
        compiler_params=pltpu.CompilerParams(has_side_effects=DATAFLOW),
    )(*srcs, *lands, send_sems, recv_sems, after)
    return outs[n:]


def _discretize(lam_re, lam_im, log_dt, b_re, b_im):
    lr = jnp.minimum(lam_re, -1e-4)
    li = lam_im
    dt = jnp.exp(log_dt)[:, None]
    er = jnp.exp(lr * dt)
    ar, ai = er * jnp.cos(li * dt), er * jnp.sin(li * dt)
    den = lr * lr + li * li
    cr = ((ar - 1.0) * lr + ai * li) / den
    ci = (ai * lr - (ar - 1.0) * li) / den
    bbr = cr[:, :, None] * b_re - ci[:, :, None] * b_im
    bbi = cr[:, :, None] * b_im + ci[:, :, None] * b_re
    return ar, ai, bbr, bbi


def _cmul(ar, ai, br, bi):
    return ar * br - ai * bi, ar * bi + ai * br


def _cpowers(ar, ai, n):
    pr, pi = ar[None], ai[None]
    while pr.shape[0] < n:
        nr, ni = _cmul(pr, pi, pr[-1][None], pi[-1][None])
        pr, pi = jnp.concatenate([pr, nr]), jnp.concatenate([pi, ni])
    return pr[:n], pi[:n]


def _scan_tables(ar, ai, seg, reverse):
    if reverse:
        ai = -ai
    ar, ai = ar.reshape(N_KB, KB_STATES), ai.reshape(N_KB, KB_STATES)
    pr, pi = _cpowers(ar, ai, seg)
    a1 = (pr[-1], pi[-1])
    a2 = _cmul(*a1, *a1)
    a4 = _cmul(*a2, *a2)
    row = jnp.arange(SUBLANES)[None, :, None]
    wide = lambda t: jnp.broadcast_to(t[:, None, :], (N_KB, SUBLANES, KB_STATES))
    tabs = [wide(ar), wide(ai)]
    for dist, (qr, qi) in ((1, a1), (2, a2), (4, a4)):
        keep = (row < SUBLANES - dist) if reverse else (row >= dist)
        tabs += [jnp.where(keep, wide(qr), 0.0), jnp.where(keep, wide(qi), 0.0)]
    tabs += [wide(a1[0]), wide(a1[1])]
    if reverse:
        pr, pi = pr[::-1], pi[::-1]
    pw = jnp.transpose(jnp.concatenate([pr, pi], axis=-1), (1, 0, 2))[:, :, None, :]
    return jnp.stack(tabs, axis=1).astype(_F32), pw.astype(_F32)


def _block_diag_in(br, bi):
    eye = jnp.eye(GROUPS_PER_KB, dtype=_F32)
    one = lambda t: jnp.einsum("kgpc,gh->kgchp", t.reshape(N_KB, GROUPS_PER_KB, N_STATE, SSM_GC), eye).reshape(
        N_KB, LANES, KB_STATES)
    return jnp.concatenate([one(br), one(bi)], axis=-1)


def _block_diag_in_t(dmat):
    d6 = dmat.reshape(N_KB, GROUPS_PER_KB, SSM_GC, 2, GROUPS_PER_KB, N_STATE)
    eye = jnp.eye(GROUPS_PER_KB, dtype=_F32)
    both = jnp.einsum("kgcrhp,gh->rkgpc", d6, eye).reshape(2, N_GROUP, N_STATE, SSM_GC)
    return both[0], both[1]


def _block_diag_out(c_re, c_im):
    eye = jnp.eye(GROUPS_PER_KB, dtype=_F32)
    one = lambda t: jnp.einsum("kgcp,gh->khpgc", t.reshape(N_KB, GROUPS_PER_KB, SSM_GC, N_STATE), eye).reshape(
        N_KB, KB_STATES, LANES)
    return jnp.concatenate([one(c_re), -one(c_im)], axis=1)


def _block_diag_out_t(dmat_t):
    d6 = dmat_t.reshape(N_KB, GROUPS_PER_KB, SSM_GC, 2, GROUPS_PER_KB, N_STATE)
    eye = jnp.eye(GROUPS_PER_KB, dtype=_F32)
    both = jnp.einsum("kgcrhp,gh->rkgcp", d6, eye).reshape(2, N_GROUP, SSM_GC, N_STATE)
    return both[0], -both[1]


SMALL_NAMES = ("norm_mix_pre", "norm_mix_post", "ret_gn_gain", "ssm_lambda_re", "ssm_lambda_im", "ssm_log_dt",
               "ssm_b_re", "ssm_b_im", "ssm_c_re", "ssm_c_im", "ssm_d", "norm_mlp_pre", "norm_mlp_post")


def _local_grads(x, tgt, small, weights, emit, emit_small, tm, tk, tb, zero=0.0):
    L = x.shape[0]
    g1, g2, ggn = small["norm_mix_pre"], small["norm_mix_post"], small["ret_gn_gain"]
    g3, g4, d_skip = small["norm_mlp_pre"], small["norm_mlp_post"], small["ssm_d"]

    rope = _rope_tables(L)
    consts = _ret_consts()

    disc_in = (small["ssm_lambda_re"][0], small["ssm_lambda_im"][0], small["ssm_log_dt"][0] + zero,
               small["ssm_b_re"][0], small["ssm_b_im"][0])
    (ar, ai, bbr, bbi), disc_vjp = jax.vjp(_discretize, *disc_in)
    bmat = _block_diag_in(bbr, bbi).astype(_BF)
    cmat = _block_diag_out(small["ssm_c_re"][0], small["ssm_c_im"][0]).astype(_BF)
    seg = tb // SUBLANES
    tab_f, pw_f = _scan_tables(ar, ai, seg, False)
    tab_r, pw_r = _scan_tables(ar, ai, seg, True)

    h1 = _prenorm(x, g1, min(4 * tm, L), after=(pw_r,))
    (w_in_t,) = weights("in", h1)
    q, k, v, gate, u, cosf, sinf = _inproj_fwd(h1, w_in_t, rope, min(4 * tm, L))
    o, y_ret, r_prev = _retention_fwd(q, k, v, gate, ggn, consts)
    s, xs, ent = _s5_fwd(u, bmat, cmat, tab_f, pw_f, d_skip, tb)
    w_glu, w_out = weights("mix", s)
    ys, glu, cat, mix, x2 = _mixout_fwd(s, y_ret, x, w_glu, w_out, g2, min(2 * tm, L))
    w_ff1, w_ff2 = weights("mlp", x2)
    h3, f1, act = _ff1_fwd(x2, g3, w_ff1, min(2 * tm, L))
    dy, dm, dg4, sq = _ff2_loss(act, x2, tgt, g4, w_ff2, min(2 * tm, L))

    df1, dw_ff2 = _ff2_bwd(dm, f1, w_ff2, min(1024, L), 1024)
    dx2, dmix, dg3, dg2 = _ff1_bwd(df1, w_ff1, x2, mix, dy, g3, g2, min(2 * tm, L))
    dw_ff1 = _matmul_tn(h3, df1, tk, FF1_COLS, "dw_ff1", slots=True)
    token = emit({"w_ff1": dw_ff1, "w_ff2": dw_ff2})
    dglu, ds, dgate, do, dggn = _mixout_bwd(dmix, w_out, w_glu, glu, s, o, gate, ggn, min(2 * tm, L), after=token)
    dw_out = _matmul_tn(cat, dmix, tk, 1024, "dw_out")
    dw_glu = _matmul_tn(ys, dglu, tk, 1024, "dw_glu")
    token = emit({"w_glu": dw_glu, "w_out": dw_out})
    du, dbmat, dcmat, da8, dd = _s5_bwd(u, ds, xs, ent, bmat, cmat, tab_r, pw_r, d_skip, tb, after=token)

    da = jnp.sum(da8, axis=1)
    dar = da[:, :KB_STATES].reshape(N_GROUP, N_STATE)
    dai = da[:, KB_STATES:].reshape(N_GROUP, N_STATE)
    dbr, dbi = _block_diag_in_t(dbmat)
    dlre, dlim, dldt, dbre, dbim = disc_vjp((dar, dai, dbr, dbi))
    dcre, dcim = _block_diag_out_t(dcmat)
    token = emit_small({
        "norm_mix_post": dg2, "ret_gn_gain": dggn,
        "ssm_lambda_re": dlre[None], "ssm_lambda_im": dlim[None], "ssm_log_dt": dldt[None],
        "ssm_b_re": dbre[None], "ssm_b_im": dbim[None], "ssm_c_re": dcre[None], "ssm_c_im": dcim[None],
        "ssm_d": dd, "norm_mlp_pre": dg3, "norm_mlp_post": dg4,
    }, sq)

    dq, dk, dv = _retention_bwd(q, k, v, do, r_prev, consts, cosf, sinf, after=token)
    pieces = (dq, dk, dv, dgate, du)
    dw_in_t = _dw_in_t(pieces, h1, min(1024, L))
    token = emit({"w_in": dw_in_t})
    gx, dg1 = _inproj_bwd(pieces, w_in_t, x, dx2, g1, min(2 * tm, L), after=token)
    return gx, dg1


BIG_SHAPES = {"w_in": (D_MODEL, IN_COLS // N_DEV), "w_glu": (SSM_W, 2 * SSM_W // N_DEV), "w_out": (D_MODEL // N_DEV, D_MODEL),
              "w_ff1": (D_MODEL, FF1_COLS), "w_ff2": (D_FF // N_DEV, D_MODEL)}
BIG_NAMES = ("w_in", "w_glu", "w_out", "w_ff1", "w_ff2")


def _cols_from_slots(g):
    return jnp.transpose(g, (1, 0, 2)).reshape(g.shape[1], N_DEV * g.shape[2])


def _cols_to_slots(dw):
    r, cols = dw.shape
    return jnp.transpose(dw.reshape(r, N_DEV, cols // N_DEV), (1, 0, 2))


WEIGHT_GROUPS = {"in": ("w_in",), "mix": ("w_glu", "w_out"), "mlp": ("w_ff1", "w_ff2")}


def _weight_from_slots(name, g):
    if name == "w_glu":
        return _cols_from_slots(g)
    if name == "w_ff1":
        return g
    return g.reshape(N_DEV * g.shape[1], g.shape[2])


def _grad_slots(name, dw):
    if name == "w_glu":
        return _cols_to_slots(dw)
    if name == "w_ff1":
        return dw
    if name == "w_in":
        return dw.reshape(N_DEV, BIG_SHAPES[name][1], BIG_SHAPES[name][0])
    return dw.reshape((N_DEV,) + BIG_SHAPES[name])


PIECE_ROWS = 8


VEC_NAMES = tuple(n for n in SMALL_NAMES if n[:6] not in ("ssm_b_", "ssm_c_"))
BC_NAMES = ("ssm_b_re", "ssm_b_im", "ssm_c_re", "ssm_c_im")
BC_ROWS = N_GROUP * SSM_GC


def _bc_view(name, t):
    t = t[0]
    if name.startswith("ssm_b_"):
        t = jnp.swapaxes(t, 1, 2)
    return t.reshape(BC_ROWS, N_STATE)


def _bc_unview(name, t):
    t = t.reshape(N_GROUP, SSM_GC, N_STATE)
    if name.startswith("ssm_b_"):
        t = jnp.swapaxes(t, 1, 2)
    return t[None]


def _pack_bc(vals):
    return jnp.concatenate([_bc_view(n, vals[n]).astype(_F32) for n in BC_NAMES], axis=0)


def _unpack_bc(buf):
    return {n: _bc_unview(n, buf[j * BC_ROWS:(j + 1) * BC_ROWS]) for j, n in enumerate(BC_NAMES)}


def _small_layout(shapes):
    off, rows = {}, 0
    for n in VEC_NAMES:
        off[n] = rows
        rows += -(-math.prod(shapes[n]) // (PIECE_ROWS * LANES)) * PIECE_ROWS
    return off, rows, rows + PIECE_ROWS


def _pack_small(vals, shapes, last=None):
    parts = []
    for n in VEC_NAMES:
        flat = vals[n].reshape(-1).astype(_F32)
        pad = -flat.shape[0] % (PIECE_ROWS * LANES)
        if pad:
            flat = jnp.concatenate([flat, jnp.zeros((pad,), _F32)])
        parts.append(flat.reshape(-1, LANES))
    parts.append(jnp.zeros((PIECE_ROWS, LANES), _F32) if last is None else last)
    return jnp.concatenate(parts, axis=0)


def _unpack_small(buf, shapes):
    off, _, _ = _small_layout(shapes)
    out = {}
    for n in VEC_NAMES:
        size = math.prod(shapes[n])
        rows = -(-size // LANES)
        out[n] = buf[off[n]:off[n] + rows].reshape(-1)[:size].reshape(shapes[n])
    return out


WEIGHT_NAMES = ('norm_mix_pre', 'norm_mix_post', 'w_in', 'ret_gn_gain', 'ssm_lambda_re', 'ssm_lambda_im', 'ssm_log_dt',
                'ssm_b_re', 'ssm_b_im', 'ssm_c_re', 'ssm_c_im', 'ssm_d', 'w_glu', 'w_out', 'norm_mlp_pre',
                'norm_mlp_post', 'w_ff1', 'w_ff2')


def kernel(x, norm_mix_pre, norm_mix_post, w_in, ret_gn_gain, ssm_lambda_re, ssm_lambda_im, ssm_log_dt, ssm_b_re, ssm_b_im, ssm_c_re, ssm_c_im, ssm_d, w_glu, w_out, norm_mlp_pre, norm_mlp_post, w_ff1, w_ff2, loss_target, m_norm_mix_pre, m_norm_mix_post, m_w_in, m_ret_gn_gain, m_ssm_lambda_re, m_ssm_lambda_im, m_ssm_log_dt, m_ssm_b_re, m_ssm_b_im, m_ssm_c_re, m_ssm_c_im, m_ssm_d, m_w_glu, m_w_out, m_norm_mlp_pre, m_norm_mlp_post, m_w_ff1, m_w_ff2, v_norm_mix_pre, v_norm_mix_post, v_w_in, v_ret_gn_gain, v_ssm_lambda_re, v_ssm_lambda_im, v_ssm_log_dt, v_ssm_b_re, v_ssm_b_im, v_ssm_c_re, v_ssm_c_im, v_ssm_d, v_w_glu, v_w_out, v_norm_mlp_pre, v_norm_mlp_post, v_w_ff1, v_w_ff2):
    args = dict(locals())
    w = {n: args[n] for n in WEIGHT_NAMES}
    m = {n: args["m_" + n] for n in WEIGHT_NAMES}
    v = {n: args["v_" + n] for n in WEIGHT_NAMES}
    L = x.shape[1]
    tm = min(256, L)
    tk = min(2048, L)
    tb = min(1024, L)

    order = [n for names in WEIGHT_GROUPS.values() for n in names]
    blocks = [(w[n][0].T if n == "w_in" else w[n][0]).astype(_BF) for n in order]
    gathered = _split_start(blocks, [_landing(b) for b in blocks], True, "weights_start")
    zero = gathered[4][0, 0]

    def weights(group, after):
        names = WEIGHT_GROUPS[group]
        first = order.index(names[0])
        part = slice(first, first + len(names))
        landed = _split_wait(gathered[0], gathered[1], gathered[2][part], gathered[3][part], after, True,
                             "weights_wait_" + group, first=first)
        return [_weight_from_slots(n, g) for n, g in zip(names, landed)]

    in_flight = []

    def emit(dws):
        names = sorted(dws)
        srcs = [_grad_slots(n, dws[n]) for n in names]
        lands = [_landing(lax.dynamic_index_in_dim(t, _my_index(), 0, keepdims=False)) for t in srcs]
        started = _split_start(srcs, lands, False, "grads_start_" + "_".join(names))
        in_flight.append((names, started))
        return (started[4],)

    shapes = {n: w[n].shape for n in SMALL_NAMES}
    first_piece = {SMALL_NAMES[0]: jnp.zeros(shapes[SMALL_NAMES[0]], _F32)}
    small_flight = []

    def emit_small(gs, sq):
        loss_rows = jnp.broadcast_to(0.5 / D_MODEL * jnp.sum(sq), (PIECE_ROWS, LANES)).astype(_F32)
        bufs = [_pack_small({**first_piece, **gs}, shapes, loss_rows), _pack_bc(gs)]
        small_flight.append(_split_start(bufs, [_landing(b) for b in bufs], True, "small_grads_start"))
        return (small_flight[0][4],)

    small_w = {n: w[n] for n in SMALL_NAMES}
    gx, dg1 = _local_grads(x[0], loss_target[0], small_w, weights, emit, emit_small, tm, tk, tb, zero=zero)
    last_buf = dg1.reshape(PIECE_ROWS, LANES)
    last_started = _split_start([last_buf], [_landing(last_buf)], True, "last_grad_start")

    grads, delta, new_m, new_v = {}, {}, {}, {}
    after = last_started[4]
    for names, started in in_flight:
        landed = _split_wait(*started[:4], after, False, "grads_wait_" + "_".join(names))
        for n, parts in zip(names, landed):
            flip = (lambda t: t.T) if n == "w_in" else (lambda t: t)
            res = _sum_adamw(parts, flip(w[n][0]), flip(m[n][0]), flip(v[n][0]), math.gcd(512, parts.shape[1]), "adamw_" + n)
            grads[n], delta[n], new_m[n], new_v[n] = (flip(t)[None] for t in res)
        after = res[1]
    small_parts, bc_parts = _split_wait(*small_flight[0][:4], after, True, "small_grads_wait")
    last_parts = _split_wait(*last_started[:4], small_parts, True, "last_grad_wait")[0]
    small_parts = lax.dynamic_update_slice(small_parts, last_parts, (0, 0, 0))
    res_bc = _sum_adamw(bc_parts, _pack_bc(w), _pack_bc(m), _pack_bc(v), BC_ROWS, "adamw_bc")
    sw, sm, sv = _pack_small(w, shapes), _pack_small(m, shapes), _pack_small(v, shapes)
    res = _sum_adamw(small_parts, sw, sm, sv, sw.shape[0], "adamw_small")
    for dst, buf, buf_bc in zip((grads, delta, new_m, new_v), res, res_bc):
        dst.update(_unpack_small(buf, shapes))
        dst.update(_unpack_bc(buf_bc))
    _, loss_at, _ = _small_layout(shapes)
    loss = res[0][loss_at, 0]

    return (loss, gx[None], *[grads[n] for n in WEIGHT_NAMES], *[delta[n] for n in WEIGHT_NAMES],
            *[new_m[n] for n in WEIGHT_NAMES], *[new_v[n] for n in WEIGHT_NAMES])
```

```python
import math

import jax
import jax.numpy as jnp
from jax import lax
from jax.experimental import pallas as pl
from jax.experimental.pallas import tpu as pltpu

_BF = jnp.bfloat16
_F32 = jnp.float32

D_MODEL = 1024
RET_W = 512
N_HEAD = 4
HEAD_D = 128
CHUNK = 256
ROPE_CHUNK = 128
SSM_W = 512
SSM_GC = 16
N_GROUP = 32
N_STATE = 64
GROUPS_PER_KB = 8
N_KB = 4
KB_STATES = GROUPS_PER_KB * N_STATE
D_FF = 4096
IN_COLS = 2560
NORM_EPS = 1e-6
ROPE_BASE = 10000.0
N_DEV = 8

ADAM_LR = 0.001
ADAM_B1 = 0.9
ADAM_B2 = 0.999
ADAM_EPS = 1e-08
ADAM_WD = 0.01
ADAM_STEP = 10

SUBLANES = 8
LANES = 128
VMEM_LIMIT = 52 * 1024 * 1024
RET_STEP_CHUNKS = 2
KB_PER_STEP = 2
SCAN_UNROLL = True
FIX_UNROLL = 8

MESH = pl.DeviceIdType.MESH


def _params(*sem):
    return pltpu.CompilerParams(dimension_semantics=sem, vmem_limit_bytes=VMEM_LIMIT)


def _dot(a, b):
    return jnp.dot(a, b, preferred_element_type=_F32)


def _dot_nt(a, b):
    return lax.dot_general(a, b, (((1,), (1,)), ((), ())), preferred_element_type=_F32)


def _dot_tn(a, b):
    return lax.dot_general(a, b, (((0,), (0,)), ((), ())), preferred_element_type=_F32)


def _rms_r(z):
    return lax.rsqrt(jnp.mean(z * z, axis=-1, keepdims=True) + NORM_EPS)


def _rms_bwd(z, g, dn):
    r = _rms_r(z)
    t = dn * g
    dz = r * t - z * (r * r * r * jnp.mean(t * z, axis=-1, keepdims=True))
    return dz, dn * z * r


def _rope(t, cs, sn):
    return t * cs + pltpu.roll(t, HEAD_D // 2, 1) * sn


def _rope_t(t, cs, sn):
    return t * cs - pltpu.roll(t, HEAD_D // 2, 1) * sn


def _sigmoid(z):
    return 1.0 / (1.0 + jnp.exp(-z))


_GELU_C = math.sqrt(2.0 / math.pi)


def _gelu(z):
    return 0.5 * z * (1.0 + jnp.tanh(_GELU_C * (z + 0.044715 * z * z * z)))


def _gelu_grad(z):
    th = jnp.tanh(_GELU_C * (z + 0.044715 * z * z * z))
    return 0.5 * (1.0 + th) + 0.5 * z * (1.0 - th * th) * _GELU_C * (1.0 + 3 * 0.044715 * z * z)


ROW_CHUNK = 256


def _row_chunks(tm):
    return [pl.ds(i, min(ROW_CHUNK, tm)) for i in range(0, tm, ROW_CHUNK)]


def _ordered(body, in_specs, operands, after):
    k = len(after)
    if not k:
        return body, list(in_specs), tuple(operands)
    return ((lambda *refs: body(*refs[k:])), [pl.BlockSpec(memory_space=pl.ANY)] * k + list(in_specs),
            tuple(after) + tuple(operands))


def _row_spec(tm, n):
    return pl.BlockSpec((tm, n), lambda i: (i, 0))


def _full_spec(shape):
    nd = len(shape)
    return pl.BlockSpec(shape, lambda *_: (0,) * nd)


def _weight_spec(shape):
    nd = len(shape)
    return pl.BlockSpec(shape, lambda *_: (0,) * nd, pipeline_mode=pl.Buffered(1))


def _rope_tables(L):
    half = HEAD_D // 2
    inv_freq = ROPE_BASE ** (-jnp.arange(half, dtype=_F32) / half)
    twice = lambda t: jnp.concatenate([t, t], axis=-1)
    off = jnp.arange(ROPE_CHUNK, dtype=_F32)[:, None] * inv_freq[None, :]
    start = (ROPE_CHUNK * jnp.arange(L // ROPE_CHUNK, dtype=_F32))[:, None] * inv_freq[None, :]
    return (twice(jnp.cos(off)), twice(jnp.sin(off)),
            twice(jnp.cos(start))[:, None, :], twice(jnp.sin(start))[:, None, :])


def _prenorm(x, g, tm, after=()):
    L = x.shape[0]

    def body(x_ref, g_ref, h_ref):
        xv = x_ref[...]
        h_ref[...] = (xv * _rms_r(xv) * g_ref[...]).astype(_BF)

    body, in_specs, operands = _ordered(body, [_row_spec(tm, D_MODEL), _full_spec((1, D_MODEL))], (x, g), after)
    return pl.pallas_call(
        body, name="prenorm", grid=(L // tm,),
        in_specs=in_specs, out_specs=_row_spec(tm, D_MODEL),
        out_shape=jax.ShapeDtypeStruct((L, D_MODEL), _BF),
        compiler_params=_params("parallel"),
    )(*operands)


def _inproj_fwd(h, w_in_t, rope, tm):
    L = h.shape[0]
    n_chunks = tm // ROPE_CHUNK

    def body(h_ref, w_ref, co_ref, so_ref, cs_ref, ss_ref, q_ref, k_ref, v_ref, gate_ref, u_ref, cos_ref, sin_ref):
        proj = _dot_nt(h_ref[...], w_ref[...])
        lane = lax.broadcasted_iota(jnp.int32, (ROPE_CHUNK, HEAD_D), 1)
        sign = jnp.where(lane < HEAD_D // 2, -1.0, 1.0)
        co, so = co_ref[...], so_ref[...]
        for c in range(n_chunks):
            chunk = pl.program_id(0) * n_chunks + c
            cst, sst = cs_ref[chunk], ss_ref[chunk]
            rows = slice(c * ROPE_CHUNK, (c + 1) * ROPE_CHUNK)
            cs = co * cst - so * sst
            sn = (so * cst + co * sst) * sign
            cos_ref[rows, :] = cs
            sin_ref[rows, :] = sn
            for hh in range(N_HEAD):
                lo = hh * HEAD_D
                q_ref[rows, lo:lo + HEAD_D] = _rope(proj[rows, lo:lo + HEAD_D], cs, sn).astype(_BF)
                kh = _rope(proj[rows, RET_W + lo:RET_W + lo + HEAD_D], cs, sn) * (HEAD_D ** -0.5)
                k_ref[rows, lo:lo + HEAD_D] = kh.astype(_BF)
        v_ref[...] = proj[:, 2 * RET_W:3 * RET_W].astype(_BF)
        gate_ref[...] = proj[:, 3 * RET_W:4 * RET_W]
        u_ref[...] = proj[:, 4 * RET_W:]

    nc = L // ROPE_CHUNK
    return pl.pallas_call(
        body, name="inproj_fwd", grid=(L // tm,),
        in_specs=[_row_spec(tm, D_MODEL), _weight_spec((IN_COLS, D_MODEL)),
                  _full_spec((ROPE_CHUNK, HEAD_D)), _full_spec((ROPE_CHUNK, HEAD_D)),
                  _full_spec((nc, 1, HEAD_D)), _full_spec((nc, 1, HEAD_D))],
        out_specs=[_row_spec(tm, RET_W)] * 5 + [_row_spec(tm, HEAD_D)] * 2,
        out_shape=[jax.ShapeDtypeStruct((L, RET_W), _BF)] * 3 + [jax.ShapeDtypeStruct((L, RET_W), _F32)] * 2
        + [jax.ShapeDtypeStruct((L, HEAD_D), _F32)] * 2,
        compiler_params=_params("parallel"),
    )(h, w_in_t, *rope)


def _ret_consts():
    lg = jnp.log(1.0 - jnp.exp(jnp.linspace(math.log(1.0 / 32), math.log(1.0 / 512), N_HEAD))).astype(_F32)
    idx = jnp.arange(CHUNK, dtype=_F32)
    diff = idx[:, None] - idx[None, :]
    decay = jnp.where(diff[None] >= 0, jnp.exp(jnp.maximum(diff, 0.0)[None] * lg[:, None, None]), 0.0)
    zeta = jnp.exp((CHUNK - 1 - idx)[None, :] * lg[:, None])
    xi = jnp.exp((idx + 1.0)[None, :] * lg[:, None])
    gc = jnp.exp(CHUNK * lg)
    wide = lambda t: jnp.broadcast_to(t[:, :, None], (N_HEAD, CHUNK, HEAD_D)).astype(_F32)
    gcw = jnp.broadcast_to(gc[:, None, None], (N_HEAD, SUBLANES, HEAD_D)).astype(_F32)
    return decay.astype(_F32), wide(xi), wide(zeta), gcw


def _head_specs():
    wide = _full_spec((N_HEAD, CHUNK, HEAD_D))
    return [_full_spec((N_HEAD, CHUNK, CHUNK)), wide, wide, _full_spec((N_HEAD, SUBLANES, HEAD_D))]


def _retention_fwd(q, k, v, gate, ggn, consts):
    L = q.shape[0]
    nc = L // CHUNK
    cps = math.gcd(RET_STEP_CHUNKS, nc)
    blk = pl.BlockSpec((cps * CHUNK, RET_W), lambda n: (n, 0))

    def body(q_ref, k_ref, v_ref, gate_ref, ggn_ref, dm_ref, xi_ref, zeta_ref, gc_ref,
             o_ref, y_ref, rp_ref, r_scr):
        @pl.when(pl.program_id(0) == 0)
        def _():
            r_scr[...] = jnp.zeros_like(r_scr)

        for hh in range(N_HEAD):
            cols = slice(hh * HEAD_D, (hh + 1) * HEAD_D)
            state = r_scr[hh]
            for c in range(cps):
                rows = slice(c * CHUNK, (c + 1) * CHUNK)
                qv, kv, vv = q_ref[rows, cols], k_ref[rows, cols], v_ref[rows, cols]
                s = _dot_nt(qv, kv) * dm_ref[hh]
                o = _dot(s.astype(_BF), vv) + _dot(qv, state.astype(_BF)) * xi_ref[hh]
                o_ref[rows, cols] = o
                rp_ref[hh, c] = state
                vz = (vv.astype(_F32) * zeta_ref[hh]).astype(_BF)
                state = gc_ref[hh, 0:1, :] * state + _dot_tn(kv, vz)
                dlt = o - jnp.mean(o, axis=-1, keepdims=True)
                on = dlt * lax.rsqrt(jnp.mean(dlt * dlt, axis=-1, keepdims=True) + NORM_EPS)
                gt = gate_ref[rows, cols]
                y_ref[rows, cols] = (gt * _sigmoid(gt) * (on * ggn_ref[:, cols])).astype(_BF)
            r_scr[hh] = state

    return pl.pallas_call(
        body, name="retention_fwd", grid=(nc // cps,),
        in_specs=[blk, blk, blk, blk, _full_spec((1, RET_W))] + _head_specs(),
        out_specs=[blk, blk, pl.BlockSpec((N_HEAD, cps, HEAD_D, HEAD_D), lambda n: (0, n, 0, 0))],
        out_shape=[jax.ShapeDtypeStruct((L, RET_W), _F32), jax.ShapeDtypeStruct((L, RET_W), _BF),
                   jax.ShapeDtypeStruct((N_HEAD, nc, HEAD_D, HEAD_D), _F32)],
        scratch_shapes=[pltpu.VMEM((N_HEAD, HEAD_D, HEAD_D), _F32)],
        compiler_params=_params("arbitrary"),
    )(q, k, v, gate, ggn, *consts)


def _rows_to_segments(dst_scr, src_ref, seg):
    for g in range(dst_scr.shape[0]):
        for j in range(SUBLANES):
            dst_scr[g, pl.ds(j, seg, stride=SUBLANES), :] = src_ref[pl.ds(j * seg, seg), g * LANES:(g + 1) * LANES]


def _segments_to_rows(dst_ref, src_scr, seg):
    for g in range(src_scr.shape[0]):
        for j in range(SUBLANES):
            dst_ref[pl.ds(j * seg, seg), g * LANES:(g + 1) * LANES] = src_scr[g, pl.ds(j, seg, stride=SUBLANES), :]


def _scan_segments(x_ref, tab_ref, pw_ref, carry_ref, seg, reverse, entry_ref=None, fwd_ref=None, fwd_entry_ref=None,
                   da_ref=None):
    G = x_ref.shape[0]
    W = KB_STATES
    re, im = pl.ds(0, W), pl.ds(W, W)
    row_id = lax.broadcasted_iota(jnp.int32, (SUBLANES, W), 0)
    edge_in = (row_id == SUBLANES - 1) if reverse else (row_id == 0)
    edge_out = 0 if reverse else SUBLANES - 1
    a_tab = [(tab_ref[g, 0], tab_ref[g, 1]) for g in range(G)]

    def local(i, st):
        r = (seg - 1 - i) if reverse else i
        out = []
        for g in range(G):
            (ar, ai), (sr, si) = a_tab[g], st[g]
            nr = ar * sr - ai * si + x_ref[g, r, :, re]
            ni = ar * si + ai * sr + x_ref[g, r, :, im]
            x_ref[g, r, :, re] = nr
            x_ref[g, r, :, im] = ni
            out.append((nr, ni))
        return tuple(out)

    zero = jnp.zeros((SUBLANES, W), _F32)
    ends = lax.fori_loop(0, seg, local, tuple((zero, zero) for _ in range(G)), unroll=SCAN_UNROLL)

    entry = []
    shift = (SUBLANES - 1) if reverse else 1
    for g in range(G):
        er, ei = ends[g]
        fr = jnp.where(edge_in, carry_ref[g, :, re], pltpu.roll(er, shift, 0))
        fi = jnp.where(edge_in, carry_ref[g, :, im], pltpu.roll(ei, shift, 0))
        for j, dist in enumerate((1, 2, 4)):
            pr, pi = tab_ref[g, 2 + 2 * j], tab_ref[g, 3 + 2 * j]
            sh = (SUBLANES - dist) if reverse else dist
            sr, si = pltpu.roll(fr, sh, 0), pltpu.roll(fi, sh, 0)
            fr, fi = fr + pr * sr - pi * si, fi + pr * si + pi * sr
        br, bi = tab_ref[g, 8], tab_ref[g, 9]
        outr = br * fr - bi * fi + er
        outi = br * fi + bi * fr + ei
        carry_ref[g, :, re] = jnp.broadcast_to(outr[edge_out:edge_out + 1, :], (SUBLANES, W))
        carry_ref[g, :, im] = jnp.broadcast_to(outi[edge_out:edge_out + 1, :], (SUBLANES, W))
        entry.append((fr, fi))
        if entry_ref is not None:
            entry_ref[g, :, re] = fr
            entry_ref[g, :, im] = fi

    add_da = da_ref is not None

    def fix(r, st, first=False):
        out = []
        for g in range(G):
            fr, fi = entry[g]
            pwr, pwi = pw_ref[g, r, :, re], pw_ref[g, r, :, im]
            xr = x_ref[g, r, :, re] + (pwr * fr - pwi * fi)
            xi = x_ref[g, r, :, im] + (pwr * fi + pwi * fr)
            x_ref[g, r, :, re] = xr
            x_ref[g, r, :, im] = xi
            if add_da:
                prev = fwd_entry_ref.at[g] if first else fwd_ref.at[g, r - 1]
                xpr, xpi = prev[:, re], prev[:, im]
                out.append((st[g][0] + (xr * xpr + xi * xpi), st[g][1] + (xi * xpr - xr * xpi)))
            else:
                out.append(st[g])
        return tuple(out)

    if add_da:
        st = fix(0, tuple((zero, zero) for _ in range(G)), first=True)
        st = lax.fori_loop(1, seg, fix, st, unroll=SCAN_UNROLL)
        for g in range(G):
            da_ref[g, :, re] += st[g][0]
            da_ref[g, :, im] += st[g][1]
    else:
        lax.fori_loop(0, seg, fix, tuple((zero[0:1, 0:LANES],) for _ in range(G)), unroll=FIX_UNROLL)


def _s5_specs(seg, time=lambda t: t):
    G = KB_PER_STEP
    return dict(
        x=pl.BlockSpec((G, seg, SUBLANES, 2 * KB_STATES), lambda kb, t: (kb, time(t), 0, 0)),
        ent=pl.BlockSpec((G, 1, SUBLANES, 2 * KB_STATES), lambda kb, t: (kb, time(t), 0, 0)),
        b=pl.BlockSpec((G, LANES, 2 * KB_STATES), lambda kb, t: (kb, 0, 0)),
        c=pl.BlockSpec((G, 2 * KB_STATES, LANES), lambda kb, t: (kb, 0, 0)),
        tab=pl.BlockSpec((G, 10, SUBLANES, KB_STATES), lambda kb, t: (kb, 0, 0, 0)),
        pw=pl.BlockSpec((G, seg, 1, 2 * KB_STATES), lambda kb, t: (kb, 0, 0, 0)),
        d=pl.BlockSpec((1, G * LANES), lambda kb, t: (0, kb)),
    )


def _s5_fwd(u, bmat, cmat, tab_f, pw_f, d_skip, tb):
    L = u.shape[0]
    nt = L // tb
    seg = tb // SUBLANES
    G = KB_PER_STEP
    ucol = pl.BlockSpec((tb, G * LANES), lambda kb, t: (t, kb))
    sp = _s5_specs(seg)

    def body(u_ref, b_ref, c_ref, tab_ref, pw_ref, d_ref, s_ref, x_ref, ent_ref, up_scr, y_scr, carry_scr):
        @pl.when(pl.program_id(1) == 0)
        def _():
            carry_scr[...] = jnp.zeros_like(carry_scr)

        _rows_to_segments(up_scr, u_ref, seg)
        for g in range(G):
            x_ref[g] = _dot(up_scr[g].astype(_BF), b_ref[g]).reshape(seg, SUBLANES, 2 * KB_STATES)
        _scan_segments(x_ref, tab_ref, pw_ref, carry_scr, seg, reverse=False, entry_ref=ent_ref.at[:, 0])
        for g in range(G):
            y = _dot(x_ref[g].reshape(tb, 2 * KB_STATES).astype(_BF), c_ref[g])
            y_scr[g] = y + d_ref[:, g * LANES:(g + 1) * LANES] * up_scr[g]
        _segments_to_rows(s_ref, y_scr, seg)

    return pl.pallas_call(
        body, name="s5_fwd", grid=(N_KB // G, nt),
        in_specs=[ucol, sp["b"], sp["c"], sp["tab"], sp["pw"], sp["d"]],
        out_specs=[ucol, sp["x"], sp["ent"]],
        out_shape=[jax.ShapeDtypeStruct((L, SSM_W), _F32),
                   jax.ShapeDtypeStruct((N_KB, L // SUBLANES, SUBLANES, 2 * KB_STATES), _F32),
                   jax.ShapeDtypeStruct((N_KB, nt, SUBLANES, 2 * KB_STATES), _F32)],
        scratch_shapes=[pltpu.VMEM((G, tb, LANES), _F32)] * 2 + [pltpu.VMEM((G, SUBLANES, 2 * KB_STATES), _F32)],
        compiler_params=_params("parallel", "arbitrary"),
    )(u, bmat, cmat, tab_f, pw_f, d_skip)


def _mixout_fwd(s, y_ret, x, w_glu, w_out, g2, tm):
    L = s.shape[0]

    def body(s_ref, yr_ref, x_ref, wg_ref, wo_ref, g_ref, ys_ref, glu_ref, cat_ref, mix_ref, x2_ref):
        for rows in _row_chunks(tm):
            ys = _gelu(s_ref[rows, :]).astype(_BF)
            ys_ref[rows, :] = ys
            glu = _dot(ys, wg_ref[...])
            glu_ref[rows, :] = glu
            cat_ref[rows, :RET_W] = yr_ref[rows, :]
            cat_ref[rows, RET_W:] = (glu[:, :SSM_W] * _sigmoid(glu[:, SSM_W:])).astype(_BF)
            mix = _dot(cat_ref[rows, :], wo_ref[...])
            mix_ref[rows, :] = mix
            x2_ref[rows, :] = x_ref[rows, :] + mix * _rms_r(mix) * g_ref[...]

    return pl.pallas_call(
        body, name="mixout_fwd", grid=(L // tm,),
        in_specs=[_row_spec(tm, SSM_W), _row_spec(tm, RET_W), _row_spec(tm, D_MODEL),
                  _weight_spec((SSM_W, 2 * SSM_W)), _weight_spec((D_MODEL, D_MODEL)), _full_spec((1, D_MODEL))],
        out_specs=[_row_spec(tm, SSM_W), _row_spec(tm, 2 * SSM_W), _row_spec(tm, D_MODEL),
                   _row_spec(tm, D_MODEL), _row_spec(tm, D_MODEL)],
        out_shape=[jax.ShapeDtypeStruct((L, SSM_W), _BF), jax.ShapeDtypeStruct((L, 2 * SSM_W), _F32),
                   jax.ShapeDtypeStruct((L, D_MODEL), _BF), jax.ShapeDtypeStruct((L, D_MODEL), _F32),
                   jax.ShapeDtypeStruct((L, D_MODEL), _F32)],
        compiler_params=_params("parallel"),
    )(s, y_ret, x, w_glu, w_out, g2)


FF1_COLS = D_FF // N_DEV


def _ff1_fwd(x2, g3, w1, tm):
    L = x2.shape[0]

    def body(x_ref, g_ref, w_ref, h_ref, f_ref, a_ref):
        for rows in _row_chunks(tm):
            xv = x_ref[rows, :]
            h = (xv * _rms_r(xv) * g_ref[...]).astype(_BF)
            h_ref[rows, :] = h
            for j in range(N_DEV):
                cols = slice(j * FF1_COLS, (j + 1) * FF1_COLS)
                f = _dot(h, w_ref[j])
                f_ref[rows, cols] = f
                rl = jnp.maximum(f, 0.0)
                a_ref[rows, cols] = (rl * rl).astype(_BF)

    return pl.pallas_call(
        body, name="ff1_fwd", grid=(L // tm,),
        in_specs=[_row_spec(tm, D_MODEL), _full_spec((1, D_MODEL)), _weight_spec((N_DEV, D_MODEL, FF1_COLS))],
        out_specs=[_row_spec(tm, D_MODEL), _row_spec(tm, D_FF), _row_spec(tm, D_FF)],
        out_shape=[jax.ShapeDtypeStruct((L, D_MODEL), _BF), jax.ShapeDtypeStruct((L, D_FF), _F32),
                   jax.ShapeDtypeStruct((L, D_FF), _BF)],
        compiler_params=_params("parallel"),
    )(x2, g3, w1)


def _ff2_loss(act, x2, tgt, g4, w2, tm):
    L = act.shape[0]

    def body(f_ref, x_ref, t_ref, g_ref, w_ref, dy_ref, dm_ref, dg_ref, ls_ref):
        @pl.when(pl.program_id(0) == 0)
        def _():
            dg_ref[...] = jnp.zeros_like(dg_ref)
            ls_ref[...] = jnp.zeros_like(ls_ref)

        g = g_ref[...]
        for rows in _row_chunks(tm):
            m = _dot(f_ref[rows, :], w_ref[...])
            y = x_ref[rows, :] + m * _rms_r(m) * g
            err = y - t_ref[rows, :]
            ls_ref[...] += jnp.sum(err * err, axis=0, keepdims=True)
            dy = err * (1.0 / D_MODEL)
            dy_ref[rows, :] = dy
            dm, dgr = _rms_bwd(m, g, dy)
            dm_ref[rows, :] = dm.astype(_BF)
            dg_ref[...] += jnp.sum(dgr, axis=0, keepdims=True)

    return pl.pallas_call(
        body, name="ff2_loss", grid=(L // tm,),
        in_specs=[_row_spec(tm, D_FF), _row_spec(tm, D_MODEL), _row_spec(tm, D_MODEL),
                  _full_spec((1, D_MODEL)), _weight_spec((D_FF, D_MODEL))],
        out_specs=[_row_spec(tm, D_MODEL), _row_spec(tm, D_MODEL), _full_spec((1, D_MODEL)), _full_spec((1, D_MODEL))],
        out_shape=[jax.ShapeDtypeStruct((L, D_MODEL), _F32), jax.ShapeDtypeStruct((L, D_MODEL), _BF),
                   jax.ShapeDtypeStruct((1, D_MODEL), _F32), jax.ShapeDtypeStruct((1, D_MODEL), _F32)],
        compiler_params=_params("arbitrary"),
    )(act, x2, tgt, g4, w2)


def _ff2_bwd(dm, f1, w2, tm, tn):
    L = dm.shape[0]
    last = L // tm - 1

    def body(dm_ref, f_ref, w_ref, df_ref, dw_ref, acc):
        @pl.when(pl.program_id(1) == 0)
        def _():
            acc[...] = jnp.zeros_like(acc)

        dmv = dm_ref[...]
        rl = jnp.maximum(f_ref[...], 0.0)
        df_ref[...] = (_dot_nt(dmv, w_ref[...]) * (2.0 * rl)).astype(_BF)
        acc[...] += _dot_tn((rl * rl).astype(_BF), dmv)

        @pl.when(pl.program_id(1) == last)
        def _():
            dw_ref[...] = acc[...].astype(_BF)

    return pl.pallas_call(
        body, name="ff2_bwd", grid=(D_FF // tn, L // tm),
        in_specs=[pl.BlockSpec((tm, D_MODEL), lambda j, i: (i, 0)), pl.BlockSpec((tm, tn), lambda j, i: (i, j)),
                  pl.BlockSpec((tn, D_MODEL), lambda j, i: (j, 0))],
        out_specs=[pl.BlockSpec((tm, tn), lambda j, i: (i, j)), pl.BlockSpec((tn, D_MODEL), lambda j, i: (j, 0))],
        out_shape=[jax.ShapeDtypeStruct((L, D_FF), _BF), jax.ShapeDtypeStruct((D_FF, D_MODEL), _BF)],
        scratch_shapes=[pltpu.VMEM((tn, D_MODEL), _F32)],
        compiler_params=_params("parallel", "arbitrary"),
    )(dm, f1, w2)


def _ff1_bwd(df1, w1, x2, mix, dy, g3, g2, tm):
    L = df1.shape[0]

    def body(df_ref, w_ref, x2_ref, mix_ref, dy_ref, g3_ref, g2_ref, dx2_ref, dmix_ref, dg3_ref, dg2_ref):
        @pl.when(pl.program_id(0) == 0)
        def _():
            dg3_ref[...] = jnp.zeros_like(dg3_ref)
            dg2_ref[...] = jnp.zeros_like(dg2_ref)

        for rows in _row_chunks(tm):
            dh = _dot_nt(df_ref[rows, 0:FF1_COLS], w_ref[0])
            for j in range(1, N_DEV):
                dh = dh + _dot_nt(df_ref[rows, j * FF1_COLS:(j + 1) * FF1_COLS], w_ref[j])
            dz, dgr = _rms_bwd(x2_ref[rows, :], g3_ref[...], dh)
            dg3_ref[...] += jnp.sum(dgr, axis=0, keepdims=True)
            dx2 = dy_ref[rows, :] + dz
            dx2_ref[rows, :] = dx2
            dmx, dgr2 = _rms_bwd(mix_ref[rows, :], g2_ref[...], dx2)
            dg2_ref[...] += jnp.sum(dgr2, axis=0, keepdims=True)
            dmix_ref[rows, :] = dmx.astype(_BF)

    vec = _full_spec((1, D_MODEL))
    return pl.pallas_call(
        body, name="ff1_bwd", grid=(L // tm,),
        in_specs=[_row_spec(tm, D_FF), _weight_spec((N_DEV, D_MODEL, FF1_COLS)), _row_spec(tm, D_MODEL),
                  _row_spec(tm, D_MODEL), _row_spec(tm, D_MODEL), vec, vec],
        out_specs=[_row_spec(tm, D_MODEL), _row_spec(tm, D_MODEL), vec, vec],
        out_shape=[jax.ShapeDtypeStruct((L, D_MODEL), _F32), jax.ShapeDtypeStruct((L, D_MODEL), _BF),
                   jax.ShapeDtypeStruct((1, D_MODEL), _F32), jax.ShapeDtypeStruct((1, D_MODEL), _F32)],
        compiler_params=_params("arbitrary"),
    )(df1, w1, x2, mix, dy, g3, g2)


def _matmul_tn(a, b, tm, tn, name, slots=False):
    L, K = a.shape
    N = b.shape[1]
    last = L // tm - 1

    def body(a_ref, b_ref, o_ref, acc):
        @pl.when(pl.program_id(1) == 0)
        def _():
            acc[...] = jnp.zeros_like(acc)

        acc[...] += _dot_tn(a_ref[...].astype(_BF), b_ref[...].astype(_BF))

        @pl.when(pl.program_id(1) == last)
        def _():
            if slots:
                o_ref[0] = acc[...].astype(_BF)
            else:
                o_ref[...] = acc[...].astype(_BF)

    if slots:
        out_spec = pl.BlockSpec((1, K, tn), lambda j, i: (j, 0, 0))
        out_shape = jax.ShapeDtypeStruct((N // tn, K, tn), _BF)
    else:
        out_spec = pl.BlockSpec((K, tn), lambda j, i: (0, j))
        out_shape = jax.ShapeDtypeStruct((K, N), _BF)
    return pl.pallas_call(
        body, name=name, grid=(N // tn, L // tm),
        in_specs=[pl.BlockSpec((tm, K), lambda j, i: (i, 0)), pl.BlockSpec((tm, tn), lambda j, i: (i, j))],
        out_specs=out_spec, out_shape=out_shape,
        scratch_shapes=[pltpu.VMEM((K, tn), _F32)],
        compiler_params=_params("parallel", "arbitrary"),
    )(a, b)


def _dw_in_t(pieces, h, tk):
    L = h.shape[0]
    last = L // tk - 1

    def body(p0, p1, p2, p3, p4, h_ref, o_ref, acc):
        @pl.when(pl.program_id(0) == 0)
        def _():
            acc[...] = jnp.zeros_like(acc)

        hv = h_ref[...]
        for j, p in enumerate((p0, p1, p2, p3, p4)):
            acc[j * RET_W:(j + 1) * RET_W, :] += _dot_tn(p[...].astype(_BF), hv)

        @pl.when(pl.program_id(0) == last)
        def _():
            o_ref[...] = acc[...].astype(_BF)

    return pl.pallas_call(
        body, name="dw_in", grid=(L // tk,),
        in_specs=[_row_spec(tk, RET_W)] * 5 + [_row_spec(tk, D_MODEL)],
        out_specs=_full_spec((IN_COLS, D_MODEL)), out_shape=jax.ShapeDtypeStruct((IN_COLS, D_MODEL), _BF),
        scratch_shapes=[pltpu.VMEM((IN_COLS, D_MODEL), _F32)],
        compiler_params=_params("arbitrary"),
    )(*pieces, h)


def _mixout_bwd(dmix, w_out, w_glu, glu, s, o, gate, ggn, tm, after=()):
    L = dmix.shape[0]

    def body(dmix_ref, wo_ref, wg_ref, glu_ref, s_ref, o_ref, gate_ref, ggn_ref,
             dglu_ref, ds_ref, dgate_ref, do_ref, dggn_ref):
        @pl.when(pl.program_id(0) == 0)
        def _():
            dggn_ref[...] = jnp.zeros_like(dggn_ref)

        ggn = ggn_ref[...]
        for rows in _row_chunks(tm):
            dcat = _dot_nt(dmix_ref[rows, :], wo_ref[...])
            dy_ret, dy_ssm = dcat[:, :RET_W], dcat[:, RET_W:]
            glu = glu_ref[rows, :]
            ga, sg = glu[:, :SSM_W], _sigmoid(glu[:, SSM_W:])
            dga = (dy_ssm * sg).astype(_BF)
            dgb = (dy_ssm * ga * sg * (1.0 - sg)).astype(_BF)
            dglu_ref[rows, :SSM_W] = dga
            dglu_ref[rows, SSM_W:] = dgb
            dys = _dot_nt(dga, wg_ref[:, :SSM_W]) + _dot_nt(dgb, wg_ref[:, SSM_W:])
            ds_ref[rows, :] = dys * _gelu_grad(s_ref[rows, :])
            gt = gate_ref[rows, :]
            sgt = _sigmoid(gt)
            for hh in range(N_HEAD):
                cols = slice(hh * HEAD_D, (hh + 1) * HEAD_D)
                ov = o_ref[rows, cols]
                dlt = ov - jnp.mean(ov, axis=-1, keepdims=True)
                rstd = lax.rsqrt(jnp.mean(dlt * dlt, axis=-1, keepdims=True) + NORM_EPS)
                on = dlt * rstd
                dyr = dy_ret[:, cols] * (gt[:, cols] * sgt[:, cols])
                dgate_ref[rows, cols] = dy_ret[:, cols] * (on * ggn[:, cols]) * (sgt[:, cols] * (1.0 + gt[:, cols] * (1.0 - sgt[:, cols])))
                dggn_ref[:, cols] += jnp.sum(dyr * on, axis=0, keepdims=True)
                don = dyr * ggn[:, cols]
                do = rstd * (don - jnp.mean(don, axis=-1, keepdims=True) - on * jnp.mean(don * on, axis=-1, keepdims=True))
                do_ref[rows, cols] = do.astype(_BF)

    body, in_specs, operands = _ordered(
        body, [_row_spec(tm, D_MODEL), _weight_spec((D_MODEL, D_MODEL)), _weight_spec((SSM_W, 2 * SSM_W)),
               _row_spec(tm, 2 * SSM_W), _row_spec(tm, SSM_W), _row_spec(tm, RET_W), _row_spec(tm, RET_W),
               _full_spec((1, RET_W))], (dmix, w_out, w_glu, glu, s, o, gate, ggn), after)
    return pl.pallas_call(
        body, name="mixout_bwd", grid=(L // tm,),
        in_specs=in_specs,
        out_specs=[_row_spec(tm, 2 * SSM_W), _row_spec(tm, SSM_W), _row_spec(tm, RET_W), _row_spec(tm, RET_W),
                   _full_spec((1, RET_W))],
        out_shape=[jax.ShapeDtypeStruct((L, 2 * SSM_W), _BF), jax.ShapeDtypeStruct((L, SSM_W), _F32),
                   jax.ShapeDtypeStruct((L, RET_W), _F32), jax.ShapeDtypeStruct((L, RET_W), _BF),
                   jax.ShapeDtypeStruct((1, RET_W), _F32)],
        compiler_params=_params("arbitrary"),
    )(*operands)


def _s5_bwd(u, ds, xs, ent, bmat, cmat, tab_r, pw_r, d_skip, tb, after=()):
    L = u.shape[0]
    nt = L // tb
    seg = tb // SUBLANES
    G = KB_PER_STEP
    rcol = pl.BlockSpec((tb, G * LANES), lambda kb, t: (nt - 1 - t, kb))
    sp = _s5_specs(seg, time=lambda t: nt - 1 - t)
    aspec = pl.BlockSpec((G, SUBLANES, 2 * KB_STATES), lambda kb, t: (kb, 0, 0))

    def body(u_ref, ds_ref, x_ref, ent_ref, b_ref, c_ref, tr_ref, pr_ref, d_ref,
             du_ref, db_ref, dc_ref, da_ref, dd_ref, up_scr, dp_scr, g_scr, lc_scr):
        @pl.when(pl.program_id(1) == 0)
        def _():
            lc_scr[...] = jnp.zeros_like(lc_scr)
            db_ref[...] = jnp.zeros_like(db_ref)
            dc_ref[...] = jnp.zeros_like(dc_ref)
            da_ref[...] = jnp.zeros_like(da_ref)
            dd_ref[...] = jnp.zeros_like(dd_ref)

        _rows_to_segments(up_scr, u_ref, seg)
        _rows_to_segments(dp_scr, ds_ref, seg)
        for g in range(G):
            g_scr[g] = _dot_nt(dp_scr[g].astype(_BF), c_ref[g]).reshape(seg, SUBLANES, 2 * KB_STATES)
        _scan_segments(g_scr, tr_ref, pr_ref, lc_scr, seg, reverse=True, fwd_ref=x_ref, fwd_entry_ref=ent_ref.at[:, 0],
                       da_ref=da_ref)
        for g in range(G):
            cols = slice(g * LANES, (g + 1) * LANES)
            uv, dsv = up_scr[g], dp_scr[g]
            ub, dsb = uv.astype(_BF), dsv.astype(_BF)
            lamb = g_scr[g].reshape(tb, 2 * KB_STATES).astype(_BF)
            db_ref[g] += _dot_tn(ub, lamb)
            dc_ref[g] += _dot_tn(dsb, x_ref[g].reshape(tb, 2 * KB_STATES).astype(_BF))
            dd_ref[:, cols] += jnp.sum(dsv * uv, axis=0, keepdims=True)
            up_scr[g] = _dot_nt(lamb, b_ref[g]) + d_ref[:, cols] * dsv
        _segments_to_rows(du_ref, up_scr, seg)

    body, in_specs, operands = _ordered(
        body, [rcol, rcol, sp["x"], sp["ent"], sp["b"], sp["c"], sp["tab"], sp["pw"], sp["d"]],
        (u, ds, xs, ent, bmat, cmat, tab_r, pw_r, d_skip), after)
    return pl.pallas_call(
        body, name="s5_bwd", grid=(N_KB // G, nt),
        in_specs=in_specs,
        out_specs=[rcol, sp["b"], sp["b"], aspec, sp["d"]],
        out_shape=[jax.ShapeDtypeStruct((L, SSM_W), _F32),
                   jax.ShapeDtypeStruct((N_KB, LANES, 2 * KB_STATES), _F32),
                   jax.ShapeDtypeStruct((N_KB, LANES, 2 * KB_STATES), _F32),
                   jax.ShapeDtypeStruct((N_KB, SUBLANES, 2 * KB_STATES), _F32),
                   jax.ShapeDtypeStruct((1, SSM_W), _F32)],
        scratch_shapes=[pltpu.VMEM((G, tb, LANES), _F32)] * 2
        + [pltpu.VMEM((G, seg, SUBLANES, 2 * KB_STATES), _F32), pltpu.VMEM((G, SUBLANES, 2 * KB_STATES), _F32)],
        compiler_params=_params("parallel", "arbitrary"),
    )(*operands)


def _retention_bwd(q, k, v, do, r_prev, consts, cosf, sinf, after=()):
    L = q.shape[0]
    nc = L // CHUNK
    cps = math.gcd(RET_STEP_CHUNKS, nc)
    nb = nc // cps
    blk = pl.BlockSpec((cps * CHUNK, RET_W), lambda n: (nb - 1 - n, 0))
    rope_blk = pl.BlockSpec((cps * CHUNK, HEAD_D), lambda n: (nb - 1 - n, 0))

    def body(q_ref, k_ref, v_ref, do_ref, rp_ref, dm_ref, xi_ref, zeta_ref, gc_ref, cos_ref, sin_ref,
             dq_ref, dk_ref, dv_ref, g_scr):
        @pl.when(pl.program_id(0) == 0)
        def _():
            g_scr[...] = jnp.zeros_like(g_scr)

        for hh in range(N_HEAD):
            cols = slice(hh * HEAD_D, (hh + 1) * HEAD_D)
            dm, zeta = dm_ref[hh], zeta_ref[hh]
            gst = g_scr[hh]
            for c in reversed(range(cps)):
                rows = slice(c * CHUNK, (c + 1) * CHUNK)
                qv, kv, vv, dov = q_ref[rows, cols], k_ref[rows, cols], v_ref[rows, cols], do_ref[rows, cols]
                rb = rp_ref[hh, c].astype(_BF)
                gb = gst.astype(_BF)
                sb = (_dot_nt(qv, kv) * dm).astype(_BF)
                dab = (_dot_nt(dov, vv) * dm).astype(_BF)
                dox = (dov.astype(_F32) * xi_ref[hh]).astype(_BF)
                vz = (vv.astype(_F32) * zeta).astype(_BF)
                dq = _dot(dab, kv) + _dot_nt(dox, rb)
                dk = _dot_tn(dab, qv) + _dot_nt(vz, gb)
                dv = _dot_tn(sb, dov) + _dot(kv, gb) * zeta
                gst = gc_ref[hh, 0:1, :] * gst + _dot_tn(qv, dox)
                cs, sn = cos_ref[rows, :], sin_ref[rows, :]
                dq_ref[rows, cols] = _rope_t(dq, cs, sn).astype(_BF)
                dk_ref[rows, cols] = (_rope_t(dk, cs, sn) * (HEAD_D ** -0.5)).astype(_BF)
                dv_ref[rows, cols] = dv.astype(_BF)
            g_scr[hh] = gst

    body, in_specs, operands = _ordered(
        body, [blk, blk, blk, blk, pl.BlockSpec((N_HEAD, cps, HEAD_D, HEAD_D), lambda n: (0, nb - 1 - n, 0, 0))]
        + _head_specs() + [rope_blk, rope_blk], (q, k, v, do, r_prev, *consts, cosf, sinf), after)
    return pl.pallas_call(
        body, name="retention_bwd", grid=(nb,),
        in_specs=in_specs,
        out_specs=[blk, blk, blk],
        out_shape=[jax.ShapeDtypeStruct((L, RET_W), _BF)] * 3,
        scratch_shapes=[pltpu.VMEM((N_HEAD, HEAD_D, HEAD_D), _F32)],
        compiler_params=_params("arbitrary"),
    )(*operands)


def _inproj_bwd(pieces, w_in_t, x, dx2, g1, tm, after=()):
    L = x.shape[0]

    def body(p0, p1, p2, p3, p4, w_ref, x_ref, dx2_ref, g_ref, dx_ref, dg_ref):
        @pl.when(pl.program_id(0) == 0)
        def _():
            dg_ref[...] = jnp.zeros_like(dg_ref)

        for rows in _row_chunks(tm):
            dh = None
            for j, p in enumerate((p0, p1, p2, p3, p4)):
                part = _dot(p[rows, :].astype(_BF), w_ref[j * RET_W:(j + 1) * RET_W, :])
                dh = part if dh is None else dh + part
            dz, dgr = _rms_bwd(x_ref[rows, :], g_ref[...], dh)
            dx_ref[rows, :] = dx2_ref[rows, :] + dz
            dg_ref[...] += jnp.sum(dgr, axis=0, keepdims=True)

    body, in_specs, operands = _ordered(
        body, [_row_spec(tm, RET_W)] * 5 + [_weight_spec((IN_COLS, D_MODEL)), _row_spec(tm, D_MODEL),
                                             _row_spec(tm, D_MODEL), _full_spec((1, D_MODEL))],
        (*pieces, w_in_t, x, dx2, g1), after)
    return pl.pallas_call(
        body, name="inproj_bwd", grid=(L // tm,),
        in_specs=in_specs,
        out_specs=[_row_spec(tm, D_MODEL), _full_spec((1, D_MODEL))],
        out_shape=[jax.ShapeDtypeStruct((L, D_MODEL), _F32), jax.ShapeDtypeStruct((1, D_MODEL), _F32)],
        compiler_params=_params("arbitrary"),
    )(*operands)


def _sum_adamw(parts, w, m, v, tr, name):
    _, R, Cc = parts.shape

    def body(p_ref, w_ref, m_ref, v_ref, g_ref, d_ref, nm_ref, nv_ref):
        gv = p_ref[0].astype(_F32)
        for s in range(1, N_DEV):
            gv = gv + p_ref[s].astype(_F32)
        g_ref[...] = gv
        nm = ADAM_B1 * m_ref[...] + (1.0 - ADAM_B1) * gv
        nv = ADAM_B2 * v_ref[...] + (1.0 - ADAM_B2) * (gv * gv)
        m_hat = nm / (1.0 - ADAM_B1 ** ADAM_STEP)
        v_hat = nv / (1.0 - ADAM_B2 ** ADAM_STEP)
        d_ref[...] = -ADAM_LR * (m_hat / (jnp.sqrt(v_hat) + ADAM_EPS) + ADAM_WD * w_ref[...])
        nm_ref[...] = nm
        nv_ref[...] = nv

    spec = _row_spec(tr, Cc)
    return pl.pallas_call(
        body, name=name, grid=(R // tr,),
        in_specs=[pl.BlockSpec((N_DEV, tr, Cc), lambda i: (0, i, 0))] + [spec] * 3, out_specs=[spec] * 4,
        out_shape=[jax.ShapeDtypeStruct((R, Cc), _F32)] * 4,
        compiler_params=_params("parallel"),
    )(parts, w, m, v)


def _my_place():
    return lax.axis_index("x"), lax.axis_index("y"), lax.axis_index("c")


HBM_SPEC = pl.BlockSpec(memory_space=pltpu.HBM)
SEM_SPEC = pl.BlockSpec(memory_space=pltpu.SEMAPHORE)
DATAFLOW = pltpu.SideEffectType.DATAFLOW_SIDE_EFFECTING


def _my_index():
    x, y, c = _my_place()
    return 4 * x + 2 * y + c


def _landing(own_block):
    zone = lax.empty((N_DEV,) + own_block.shape, own_block.dtype)
    return lax.dynamic_update_index_in_dim(zone, own_block, _my_index(), 0)


def _split_copies(src_refs, land_refs, send_sems, recv_sems, gather, first=0):
    x, y, c = _my_place()
    me = 4 * x + 2 * y + c
    copies = []
    for a, (src, land) in enumerate(zip(src_refs, land_refs)):
        for kk in range(1, N_DEV):
            px, py, pc = x ^ (kk >> 2), y ^ ((kk >> 1) & 1), c ^ (kk & 1)
            peer = 4 * px + 2 * py + pc
            copies.append(pltpu.make_async_remote_copy(
                src_ref=src if gather else src.at[peer], dst_ref=land.at[me],
                send_sem=send_sems.at[(first + a) * 7 + kk - 1], recv_sem=recv_sems.at[(first + a) * 7 + kk - 1],
                device_id=(px, py, pc), device_id_type=MESH))
    return copies


def _split_start(srcs, lands, gather, name):
    n = len(srcs)

    def body(*refs):
        src_refs, land_refs = refs[:n], refs[n:2 * n]
        send_sems, recv_sems = refs[2 * n], refs[2 * n + 1]
        token = refs[-1]
        for cp in _split_copies(src_refs, land_refs, send_sems, recv_sems, gather):
            cp.start()
        token[...] = jnp.zeros_like(token)

    outs = pl.pallas_call(
        body, name=name,
        out_shape=(pltpu.SemaphoreType.DMA((7 * n,)), pltpu.SemaphoreType.DMA((7 * n,)),
                   *[pltpu.HBM(t.shape, t.dtype) for t in srcs], *[pltpu.HBM(t.shape, t.dtype) for t in lands],
                   jax.ShapeDtypeStruct((SUBLANES, LANES), _F32)),
        in_specs=[HBM_SPEC] * (2 * n),
        out_specs=(SEM_SPEC, SEM_SPEC, *[HBM_SPEC] * (2 * n), pl.BlockSpec(memory_space=pltpu.VMEM)),
        input_output_aliases={i: 2 + i for i in range(2 * n)},
        compiler_params=pltpu.CompilerParams(has_side_effects=DATAFLOW),
    )(*[pltpu.with_memory_space_constraint(t, pltpu.HBM) for t in list(srcs) + list(lands)])
    return outs[0], outs[1], outs[2:2 + n], outs[2 + n:2 + 2 * n], outs[-1]


def _split_wait(send_sems, recv_sems, srcs, lands, after, gather, name, first=0):
    n = len(srcs)

    def body(*refs):
        src_refs, land_refs = refs[:n], refs[n:2 * n]
        send_s, recv_s = refs[2 * n], refs[2 * n + 1]
        for cp in _split_copies(src_refs, land_refs, send_s, recv_s, gather, first):
            cp.wait_send()
            cp.wait_recv()

    outs = pl.pallas_call(
        body, name=name,
        out_shape=tuple(pltpu.HBM(t.shape, t.dtype) for t in list(srcs) + list(lands)),
        in_specs=[HBM_SPEC] * (2 * n) + [SEM_SPEC, SEM_SPEC, pl.BlockSpec(memory_space=pl.ANY)],
        out_specs=tuple([HBM_SPEC] * (2 * n)),
        input_output_aliases={i: i for i in range(2 * n)},
        compiler_params=pltpu.CompilerParams(has_side_effects=DATAFLOW),
    )(*srcs, *lands, send_sems, recv_sems, after)
    return outs[n:]


def _discretize(lam_re, lam_im, log_dt, b_re, b_im):
    lr = jnp.minimum(lam_re, -1e-4)
    li = lam_im
    dt = jnp.exp(log_dt)[:, None]
    er = jnp.exp(lr * dt)
    ar, ai = er * jnp.cos(li * dt), er * jnp.sin(li * dt)
    den = lr * lr + li * li
    cr = ((ar - 1.0) * lr + ai * li) / den
    ci = (ai * lr - (ar - 1.0) * li) / den
    bbr = cr[:, :, None] * b_re - ci[:, :, None] * b_im
    bbi = cr[:, :, None] * b_im + ci[:, :, None] * b_re
    return ar, ai, bbr, bbi


def _cmul(ar, ai, br, bi):
    return ar * br - ai * bi, ar * bi + ai * br


def _cpowers(ar, ai, n):
    pr, pi = ar[None], ai[None]
    while pr.shape[0] < n:
        nr, ni = _cmul(pr, pi, pr[-1][None], pi[-1][None])
        pr, pi = jnp.concatenate([pr, nr]), jnp.concatenate([pi, ni])
    return pr[:n], pi[:n]


def _scan_tables(ar, ai, seg, reverse):
    if reverse:
        ai = -ai
    ar, ai = ar.reshape(N_KB, KB_STATES), ai.reshape(N_KB, KB_STATES)
    pr, pi = _cpowers(ar, ai, seg)
    a1 = (pr[-1], pi[-1])
    a2 = _cmul(*a1, *a1)
    a4 = _cmul(*a2, *a2)
    row = jnp.arange(SUBLANES)[None, :, None]
    wide = lambda t: jnp.broadcast_to(t[:, None, :], (N_KB, SUBLANES, KB_STATES))
    tabs = [wide(ar), wide(ai)]
    for dist, (qr, qi) in ((1, a1), (2, a2), (4, a4)):
        keep = (row < SUBLANES - dist) if reverse else (row >= dist)
        tabs += [jnp.where(keep, wide(qr), 0.0), jnp.where(keep, wide(qi), 0.0)]
    tabs += [wide(a1[0]), wide(a1[1])]
    if reverse:
        pr, pi = pr[::-1], pi[::-1]
    pw = jnp.transpose(jnp.concatenate([pr, pi], axis=-1), (1, 0, 2))[:, :, None, :]
    return jnp.stack(tabs, axis=1).astype(_F32), pw.astype(_F32)


def _block_diag_in(br, bi):
    eye = jnp.eye(GROUPS_PER_KB, dtype=_F32)
    one = lambda t: jnp.einsum("kgpc,gh->kgchp", t.reshape(N_KB, GROUPS_PER_KB, N_STATE, SSM_GC), eye).reshape(
        N_KB, LANES, KB_STATES)
    return jnp.concatenate([one(br), one(bi)], axis=-1)


def _block_diag_in_t(dmat):
    d6 = dmat.reshape(N_KB, GROUPS_PER_KB, SSM_GC, 2, GROUPS_PER_KB, N_STATE)
    eye = jnp.eye(GROUPS_PER_KB, dtype=_F32)
    both = jnp.einsum("kgcrhp,gh->rkgpc", d6, eye).reshape(2, N_GROUP, N_STATE, SSM_GC)
    return both[0], both[1]


def _block_diag_out(c_re, c_im):
    eye = jnp.eye(GROUPS_PER_KB, dtype=_F32)
    one = lambda t: jnp.einsum("kgcp,gh->khpgc", t.reshape(N_KB, GROUPS_PER_KB, SSM_GC, N_STATE), eye).reshape(
        N_KB, KB_STATES, LANES)
    return jnp.concatenate([one(c_re), -one(c_im)], axis=1)


def _block_diag_out_t(dmat_t):
    d6 = dmat_t.reshape(N_KB, GROUPS_PER_KB, SSM_GC, 2, GROUPS_PER_KB, N_STATE)
    eye = jnp.eye(GROUPS_PER_KB, dtype=_F32)
    both = jnp.einsum("kgcrhp,gh->rkgcp", d6, eye).reshape(2, N_GROUP, SSM_GC, N_STATE)
    return both[0], -both[1]


SMALL_NAMES = ("norm_mix_pre", "norm_mix_post", "ret_gn_gain", "ssm_lambda_re", "ssm_lambda_im", "ssm_log_dt",
               "ssm_b_re", "ssm_b_im", "ssm_c_re", "ssm_c_im", "ssm_d", "norm_mlp_pre", "norm_mlp_post")


def _local_grads(x, tgt, small, weights, emit, emit_small, tm, tk, tb, zero=0.0):
    L = x.shape[0]
    g1, g2, ggn = small["norm_mix_pre"], small["norm_mix_post"], small["ret_gn_gain"]
    g3, g4, d_skip = small["norm_mlp_pre"], small["norm_mlp_post"], small["ssm_d"]

    rope = _rope_tables(L)
    consts = _ret_consts()

    disc_in = (small["ssm_lambda_re"][0], small["ssm_lambda_im"][0], small["ssm_log_dt"][0] + zero,
               small["ssm_b_re"][0], small["ssm_b_im"][0])
    (ar, ai, bbr, bbi), disc_vjp = jax.vjp(_discretize, *disc_in)
    bmat = _block_diag_in(bbr, bbi).astype(_BF)
    cmat = _block_diag_out(small["ssm_c_re"][0], small["ssm_c_im"][0]).astype(_BF)
    seg = tb // SUBLANES
    tab_f, pw_f = _scan_tables(ar, ai, seg, False)
    tab_r, pw_r = _scan_tables(ar, ai, seg, True)

    h1 = _prenorm(x, g1, min(4 * tm, L), after=(pw_r,))
    (w_in_t,) = weights("in", h1)
    q, k, v, gate, u, cosf, sinf = _inproj_fwd(h1, w_in_t, rope, min(4 * tm, L))
    o, y_ret, r_prev = _retention_fwd(q, k, v, gate, ggn, consts)
    s, xs, ent = _s5_fwd(u, bmat, cmat, tab_f, pw_f, d_skip, tb)
    w_glu, w_out = weights("mix", s)
    ys, glu, cat, mix, x2 = _mixout_fwd(s, y_ret, x, w_glu, w_out, g2, min(2 * tm, L))
    w_ff1, w_ff2 = weights("mlp", x2)
    h3, f1, act = _ff1_fwd(x2, g3, w_ff1, min(2 * tm, L))
    dy, dm, dg4, sq = _ff2_loss(act, x2, tgt, g4, w_ff2, min(2 * tm, L))

    df1, dw_ff2 = _ff2_bwd(dm, f1, w_ff2, min(1024, L), 1024)
    dx2, dmix, dg3, dg2 = _ff1_bwd(df1, w_ff1, x2, mix, dy, g3, g2, min(2 * tm, L))
    dw_ff1 = _matmul_tn(h3, df1, tk, FF1_COLS, "dw_ff1", slots=True)
    token = emit({"w_ff1": dw_ff1, "w_ff2": dw_ff2})
    dglu, ds, dgate, do, dggn = _mixout_bwd(dmix, w_out, w_glu, glu, s, o, gate, ggn, min(2 * tm, L), after=token)
    dw_out = _matmul_tn(cat, dmix, tk, 1024, "dw_out")
    dw_glu = _matmul_tn(ys, dglu, tk, 1024, "dw_glu")
    token = emit({"w_glu": dw_glu, "w_out": dw_out})
    du, dbmat, dcmat, da8, dd = _s5_bwd(u, ds, xs, ent, bmat, cmat, tab_r, pw_r, d_skip, tb, after=token)

    da = jnp.sum(da8, axis=1)
    dar = da[:, :KB_STATES].reshape(N_GROUP, N_STATE)
    dai = da[:, KB_STATES:].reshape(N_GROUP, N_STATE)
    dbr, dbi = _block_diag_in_t(dbmat)
    dlre, dlim, dldt, dbre, dbim = disc_vjp((dar, dai, dbr, dbi))
    dcre, dcim = _block_diag_out_t(dcmat)
    token = emit_small({
        "norm_mix_post": dg2, "ret_gn_gain": dggn,
        "ssm_lambda_re": dlre[None], "ssm_lambda_im": dlim[None], "ssm_log_dt": dldt[None],
        "ssm_b_re": dbre[None], "ssm_b_im": dbim[None], "ssm_c_re": dcre[None], "ssm_c_im": dcim[None],
        "ssm_d": dd, "norm_mlp_pre": dg3, "norm_mlp_post": dg4,
    }, sq)

    dq, dk, dv = _retention_bwd(q, k, v, do, r_prev, consts, cosf, sinf, after=token)
    pieces = (dq, dk, dv, dgate, du)
    dw_in_t = _dw_in_t(pieces, h1, min(1024, L))
    token = emit({"w_in": dw_in_t})
    gx, dg1 = _inproj_bwd(pieces, w_in_t, x, dx2, g1, min(2 * tm, L), after=token)
    return gx, dg1


BIG_SHAPES = {"w_in": (D_MODEL, IN_COLS // N_DEV), "w_glu": (SSM_W, 2 * SSM_W // N_DEV), "w_out": (D_MODEL // N_DEV, D_MODEL),
              "w_ff1": (D_MODEL, FF1_COLS), "w_ff2": (D_FF // N_DEV, D_MODEL)}
BIG_NAMES = ("w_in", "w_glu", "w_out", "w_ff1", "w_ff2")


def _cols_from_slots(g):
    return jnp.transpose(g, (1, 0, 2)).reshape(g.shape[1], N_DEV * g.shape[2])


def _cols_to_slots(dw):
    r, cols = dw.shape
    return jnp.transpose(dw.reshape(r, N_DEV, cols // N_DEV), (1, 0, 2))


WEIGHT_GROUPS = {"in": ("w_in",), "mix": ("w_glu", "w_out"), "mlp": ("w_ff1", "w_ff2")}


def _weight_from_slots(name, g):
    if name == "w_glu":
        return _cols_from_slots(g)
    if name == "w_ff1":
        return g
    return g.reshape(N_DEV * g.shape[1], g.shape[2])


def _grad_slots(name, dw):
    if name == "w_glu":
        return _cols_to_slots(dw)
    if name == "w_ff1":
        return dw
    if name == "w_in":
        return dw.reshape(N_DEV, BIG_SHAPES[name][1], BIG_SHAPES[name][0])
    return dw.reshape((N_DEV,) + BIG_SHAPES[name])


PIECE_ROWS = 8


VEC_NAMES = tuple(n for n in SMALL_NAMES if n[:6] not in ("ssm_b_", "ssm_c_"))
BC_NAMES = ("ssm_b_re", "ssm_b_im", "ssm_c_re", "ssm_c_im")
BC_ROWS = N_GROUP * SSM_GC


def _bc_view(name, t):
    t = t[0]
    if name.startswith("ssm_b_"):
        t = jnp.swapaxes(t, 1, 2)
    return t.reshape(BC_ROWS, N_STATE)


def _bc_unview(name, t):
    t = t.reshape(N_GROUP, SSM_GC, N_STATE)
    if name.startswith("ssm_b_"):
        t = jnp.swapaxes(t, 1, 2)
    return t[None]


def _pack_bc(vals):
    return jnp.concatenate([_bc_view(n, vals[n]).astype(_F32) for n in BC_NAMES], axis=0)


def _unpack_bc(buf):
    return {n: _bc_unview(n, buf[j * BC_ROWS:(j + 1) * BC_ROWS]) for j, n in enumerate(BC_NAMES)}


def _small_layout(shapes):
    off, rows = {}, 0
    for n in VEC_NAMES:
        off[n] = rows
        rows += -(-math.prod(shapes[n]) // (PIECE_ROWS * LANES)) * PIECE_ROWS
    return off, rows, rows + PIECE_ROWS


def _pack_small(vals, shapes, last=None):
    parts = []
    for n in VEC_NAMES:
        flat = vals[n].reshape(-1).astype(_F32)
        pad = -flat.shape[0] % (PIECE_ROWS * LANES)
        if pad:
            flat = jnp.concatenate([flat, jnp.zeros((pad,), _F32)])
        parts.append(flat.reshape(-1, LANES))
    parts.append(jnp.zeros((PIECE_ROWS, LANES), _F32) if last is None else last)
    return jnp.concatenate(parts, axis=0)


def _unpack_small(buf, shapes):
    off, _, _ = _small_layout(shapes)
    out = {}
    for n in VEC_NAMES:
        size = math.prod(shapes[n])
        rows = -(-size // LANES)
        out[n] = buf[off[n]:off[n] + rows].reshape(-1)[:size].reshape(shapes[n])
    return out


WEIGHT_NAMES = ('norm_mix_pre', 'norm_mix_post', 'w_in', 'ret_gn_gain', 'ssm_lambda_re', 'ssm_lambda_im', 'ssm_log_dt',
                'ssm_b_re', 'ssm_b_im', 'ssm_c_re', 'ssm_c_im', 'ssm_d', 'w_glu', 'w_out', 'norm_mlp_pre',
                'norm_mlp_post', 'w_ff1', 'w_ff2')


def kernel(x, norm_mix_pre, norm_mix_post, w_in, ret_gn_gain, ssm_lambda_re, ssm_lambda_im, ssm_log_dt, ssm_b_re, ssm_b_im, ssm_c_re, ssm_c_im, ssm_d, w_glu, w_out, norm_mlp_pre, norm_mlp_post, w_ff1, w_ff2, loss_target, m_norm_mix_pre, m_norm_mix_post, m_w_in, m_ret_gn_gain, m_ssm_lambda_re, m_ssm_lambda_im, m_ssm_log_dt, m_ssm_b_re, m_ssm_b_im, m_ssm_c_re, m_ssm_c_im, m_ssm_d, m_w_glu, m_w_out, m_norm_mlp_pre, m_norm_mlp_post, m_w_ff1, m_w_ff2, v_norm_mix_pre, v_norm_mix_post, v_w_in, v_ret_gn_gain, v_ssm_lambda_re, v_ssm_lambda_im, v_ssm_log_dt, v_ssm_b_re, v_ssm_b_im, v_ssm_c_re, v_ssm_c_im, v_ssm_d, v_w_glu, v_w_out, v_norm_mlp_pre, v_norm_mlp_post, v_w_ff1, v_w_ff2):
    args = dict(locals())
    w = {n: args[n] for n in WEIGHT_NAMES}
    m = {n: args["m_" + n] for n in WEIGHT_NAMES}
    v = {n: args["v_" + n] for n in WEIGHT_NAMES}
    L = x.shape[1]
    tm = min(256, L)
    tk = min(2048, L)
    tb = min(1024, L)

    calls = {"in": ("w_in",), "rest": WEIGHT_GROUPS["mix"] + WEIGHT_GROUPS["mlp"]}
    started, zero = {}, jnp.zeros((), _F32)
    for call, names in calls.items():
        blocks = [(w[n][0].T if n == "w_in" else w[n][0]).astype(_BF) for n in names]
        blocks[0] = blocks[0] + zero.astype(_BF)
        started[call] = _split_start(blocks, [_landing(b) for b in blocks], True, "weights_start_" + call)
        zero = started[call][4][0, 0]

    def weights(group, after):
        names = WEIGHT_GROUPS[group]
        call = "in" if group == "in" else "rest"
        first = calls[call].index(names[0])
        part = slice(first, first + len(names))
        got = started[call]
        landed = _split_wait(got[0], got[1], got[2][part], got[3][part], after, True, "weights_wait_" + group, first=first)
        return [_weight_from_slots(n, g) for n, g in zip(names, landed)]

    in_flight = []

    def emit(dws):
        names = sorted(dws)
        srcs = [_grad_slots(n, dws[n]) for n in names]
        lands = [_landing(lax.dynamic_index_in_dim(t, _my_index(), 0, keepdims=False)) for t in srcs]
        started = _split_start(srcs, lands, False, "grads_start_" + "_".join(names))
        in_flight.append((names, started))
        return (started[4],)

    shapes = {n: w[n].shape for n in SMALL_NAMES}
    first_piece = {SMALL_NAMES[0]: jnp.zeros(shapes[SMALL_NAMES[0]], _F32)}
    small_flight = []

    def emit_small(gs, sq):
        loss_rows = jnp.broadcast_to(0.5 / D_MODEL * jnp.sum(sq), (PIECE_ROWS, LANES)).astype(_F32)
        bufs = [_pack_small({**first_piece, **gs}, shapes, loss_rows), _pack_bc(gs)]
        small_flight.append(_split_start(bufs, [_landing(b) for b in bufs], True, "small_grads_start"))
        return (small_flight[0][4],)

    small_w = {n: w[n] for n in SMALL_NAMES}
    gx, dg1 = _local_grads(x[0], loss_target[0], small_w, weights, emit, emit_small, tm, tk, tb, zero=zero)
    last_buf = dg1.reshape(PIECE_ROWS, LANES)
    last_started = _split_start([last_buf], [_landing(last_buf)], True, "last_grad_start")

    grads, delta, new_m, new_v = {}, {}, {}, {}
    after = last_started[4]
    for names, started in in_flight:
        landed = _split_wait(*started[:4], after, False, "grads_wait_" + "_".join(names))
        for n, parts in zip(names, landed):
            flip = (lambda t: t.T) if n == "w_in" else (lambda t: t)
            res = _sum_adamw(parts, flip(w[n][0]), flip(m[n][0]), flip(v[n][0]), math.gcd(256, parts.shape[1]), "adamw_" + n)
            grads[n], delta[n], new_m[n], new_v[n] = (flip(t)[None] for t in res)
        after = res[1]
    small_parts, bc_parts = _split_wait(*small_flight[0][:4], after, True, "small_grads_wait")
    last_parts = _split_wait(*last_started[:4], small_parts, True, "last_grad_wait")[0]
    small_parts = lax.dynamic_update_slice(small_parts, last_parts, (0, 0, 0))
    res_bc = _sum_adamw(bc_parts, _pack_bc(w), _pack_bc(m), _pack_bc(v), BC_ROWS, "adamw_bc")
    sw, sm, sv = _pack_small(w, shapes), _pack_small(m, shapes), _pack_small(v, shapes)
    res = _sum_adamw(small_parts, sw, sm, sv, sw.shape[0], "adamw_small")
    for dst, buf, buf_bc in zip((grads, delta, new_m, new_v), res, res_bc):
        dst.update(_unpack_small(buf, shapes))
        dst.update(_unpack_bc(buf_bc))
    _, loss_at, _ = _small_layout(shapes)
    loss = res[0][loss_at, 0]

    return (loss, gx[None], *[grads[n] for n in WEIGHT_NAMES], *[delta[n] for n in WEIGHT_NAMES],
            *[new_m[n] for n in WEIGHT_NAMES], *[new_v[n] for n in WEIGHT_NAMES])
```

```python
import math

import jax
import jax.numpy as jnp
from jax import lax
from jax.experimental import pallas as pl
from jax.experimental.pallas import tpu as pltpu

_BF = jnp.bfloat16
_F32 = jnp.float32

D_MODEL = 1024
RET_W = 512
N_HEAD = 4
HEAD_D = 128
CHUNK = 256
ROPE_CHUNK = 128
SSM_W = 512
SSM_GC = 16
N_GROUP = 32
N_STATE = 64
GROUPS_PER_KB = 8
N_KB = 4
KB_STATES = GROUPS_PER_KB * N_STATE
D_FF = 4096
IN_COLS = 2560
NORM_EPS = 1e-6
ROPE_BASE = 10000.0
N_DEV = 8

ADAM_LR = 0.001
ADAM_B1 = 0.9
ADAM_B2 = 0.999
ADAM_EPS = 1e-08
ADAM_WD = 0.01
ADAM_STEP = 10

SUBLANES = 8
LANES = 128
VMEM_LIMIT = 52 * 1024 * 1024
RET_STEP_CHUNKS = 2
KB_PER_STEP = 2
SCAN_UNROLL = True
FIX_UNROLL = 8

MESH = pl.DeviceIdType.MESH


def _params(*sem):
    return pltpu.CompilerParams(dimension_semantics=sem, vmem_limit_bytes=VMEM_LIMIT)


def _dot(a, b):
    return jnp.dot(a, b, preferred_element_type=_F32)


def _dot_nt(a, b):
    return lax.dot_general(a, b, (((1,), (1,)), ((), ())), preferred_element_type=_F32)


def _dot_tn(a, b):
    return lax.dot_general(a, b, (((0,), (0,)), ((), ())), preferred_element_type=_F32)


def _rms_r(z):
    return lax.rsqrt(jnp.mean(z * z, axis=-1, keepdims=True) + NORM_EPS)


def _rms_bwd(z, g, dn):
    r = _rms_r(z)
    t = dn * g
    dz = r * t - z * (r * r * r * jnp.mean(t * z, axis=-1, keepdims=True))
    return dz, dn * z * r


def _rope(t, cs, sn):
    return t * cs + pltpu.roll(t, HEAD_D // 2, 1) * sn


def _rope_t(t, cs, sn):
    return t * cs - pltpu.roll(t, HEAD_D // 2, 1) * sn


def _sigmoid(z):
    return 1.0 / (1.0 + jnp.exp(-z))


_GELU_C = math.sqrt(2.0 / math.pi)


def _gelu(z):
    return 0.5 * z * (1.0 + jnp.tanh(_GELU_C * (z + 0.044715 * z * z * z)))


def _gelu_grad(z):
    th = jnp.tanh(_GELU_C * (z + 0.044715 * z * z * z))
    return 0.5 * (1.0 + th) + 0.5 * z * (1.0 - th * th) * _GELU_C * (1.0 + 3 * 0.044715 * z * z)


ROW_CHUNK = 256


def _row_chunks(tm):
    return [pl.ds(i, min(ROW_CHUNK, tm)) for i in range(0, tm, ROW_CHUNK)]


def _ordered(body, in_specs, operands, after):
    k = len(after)
    if not k:
        return body, list(in_specs), tuple(operands)
    return ((lambda *refs: body(*refs[k:])), [pl.BlockSpec(memory_space=pl.ANY)] * k + list(in_specs),
            tuple(after) + tuple(operands))


def _row_spec(tm, n):
    return pl.BlockSpec((tm, n), lambda i: (i, 0))


def _full_spec(shape):
    nd = len(shape)
    return pl.BlockSpec(shape, lambda *_: (0,) * nd)


def _weight_spec(shape):
    nd = len(shape)
    return pl.BlockSpec(shape, lambda *_: (0,) * nd, pipeline_mode=pl.Buffered(1))


def _rope_tables(L):
    half = HEAD_D // 2
    inv_freq = ROPE_BASE ** (-jnp.arange(half, dtype=_F32) / half)
    twice = lambda t: jnp.concatenate([t, t], axis=-1)
    off = jnp.arange(ROPE_CHUNK, dtype=_F32)[:, None] * inv_freq[None, :]
    start = (ROPE_CHUNK * jnp.arange(L // ROPE_CHUNK, dtype=_F32))[:, None] * inv_freq[None, :]
    return (twice(jnp.cos(off)), twice(jnp.sin(off)),
            twice(jnp.cos(start))[:, None, :], twice(jnp.sin(start))[:, None, :])


def _prenorm(x, g, tm, after=()):
    L = x.shape[0]

    def body(x_ref, g_ref, h_ref):
        xv = x_ref[...]
        h_ref[...] = (xv * _rms_r(xv) * g_ref[...]).astype(_BF)

    body, in_specs, operands = _ordered(body, [_row_spec(tm, D_MODEL), _full_spec((1, D_MODEL))], (x, g), after)
    return pl.pallas_call(
        body, name="prenorm", grid=(L // tm,),
        in_specs=in_specs, out_specs=_row_spec(tm, D_MODEL),
        out_shape=jax.ShapeDtypeStruct((L, D_MODEL), _BF),
        compiler_params=_params("parallel"),
    )(*operands)


def _inproj_fwd(h, w_in_t, rope, tm):
    L = h.shape[0]
    n_chunks = tm // ROPE_CHUNK

    def body(h_ref, w_ref, co_ref, so_ref, cs_ref, ss_ref, q_ref, k_ref, v_ref, gate_ref, u_ref, cos_ref, sin_ref):
        proj = _dot_nt(h_ref[...], w_ref[...])
        lane = lax.broadcasted_iota(jnp.int32, (ROPE_CHUNK, HEAD_D), 1)
        sign = jnp.where(lane < HEAD_D // 2, -1.0, 1.0)
        co, so = co_ref[...], so_ref[...]
        for c in range(n_chunks):
            chunk = pl.program_id(0) * n_chunks + c
            cst, sst = cs_ref[chunk], ss_ref[chunk]
            rows = slice(c * ROPE_CHUNK, (c + 1) * ROPE_CHUNK)
            cs = co * cst - so * sst
            sn = (so * cst + co * sst) * sign
            cos_ref[rows, :] = cs
            sin_ref[rows, :] = sn
            for hh in range(N_HEAD):
                lo = hh * HEAD_D
                q_ref[rows, lo:lo + HEAD_D] = _rope(proj[rows, lo:lo + HEAD_D], cs, sn).astype(_BF)
                kh = _rope(proj[rows, RET_W + lo:RET_W + lo + HEAD_D], cs, sn) * (HEAD_D ** -0.5)
                k_ref[rows, lo:lo + HEAD_D] = kh.astype(_BF)
        v_ref[...] = proj[:, 2 * RET_W:3 * RET_W].astype(_BF)
        gate_ref[...] = proj[:, 3 * RET_W:4 * RET_W]
        u_ref[...] = proj[:, 4 * RET_W:]

    nc = L // ROPE_CHUNK
    return pl.pallas_call(
        body, name="inproj_fwd", grid=(L // tm,),
        in_specs=[_row_spec(tm, D_MODEL), _weight_spec((IN_COLS, D_MODEL)),
                  _full_spec((ROPE_CHUNK, HEAD_D)), _full_spec((ROPE_CHUNK, HEAD_D)),
                  _full_spec((nc, 1, HEAD_D)), _full_spec((nc, 1, HEAD_D))],
        out_specs=[_row_spec(tm, RET_W)] * 5 + [_row_spec(tm, HEAD_D)] * 2,
        out_shape=[jax.ShapeDtypeStruct((L, RET_W), _BF)] * 3 + [jax.ShapeDtypeStruct((L, RET_W), _F32)] * 2
        + [jax.ShapeDtypeStruct((L, HEAD_D), _F32)] * 2,
        compiler_params=_params("parallel"),
    )(h, w_in_t, *rope)


def _ret_consts():
    lg = jnp.log(1.0 - jnp.exp(jnp.linspace(math.log(1.0 / 32), math.log(1.0 / 512), N_HEAD))).astype(_F32)
    idx = jnp.arange(CHUNK, dtype=_F32)
    diff = idx[:, None] - idx[None, :]
    decay = jnp.where(diff[None] >= 0, jnp.exp(jnp.maximum(diff, 0.0)[None] * lg[:, None, None]), 0.0)
    zeta = jnp.exp((CHUNK - 1 - idx)[None, :] * lg[:, None])
    xi = jnp.exp((idx + 1.0)[None, :] * lg[:, None])
    gc = jnp.exp(CHUNK * lg)
    wide = lambda t: jnp.broadcast_to(t[:, :, None], (N_HEAD, CHUNK, HEAD_D)).astype(_F32)
    gcw = jnp.broadcast_to(gc[:, None, None], (N_HEAD, SUBLANES, HEAD_D)).astype(_F32)
    return decay.astype(_F32), wide(xi), wide(zeta), gcw


def _head_specs():
    wide = _full_spec((N_HEAD, CHUNK, HEAD_D))
    return [_full_spec((N_HEAD, CHUNK, CHUNK)), wide, wide, _full_spec((N_HEAD, SUBLANES, HEAD_D))]


def _retention_fwd(q, k, v, gate, ggn, consts):
    L = q.shape[0]
    nc = L // CHUNK
    cps = math.gcd(RET_STEP_CHUNKS, nc)
    blk = pl.BlockSpec((cps * CHUNK, RET_W), lambda n: (n, 0))

    def body(q_ref, k_ref, v_ref, gate_ref, ggn_ref, dm_ref, xi_ref, zeta_ref, gc_ref,
             o_ref, y_ref, rp_ref, r_scr):
        @pl.when(pl.program_id(0) == 0)
        def _():
            r_scr[...] = jnp.zeros_like(r_scr)

        for hh in range(N_HEAD):
            cols = slice(hh * HEAD_D, (hh + 1) * HEAD_D)
            state = r_scr[hh]
            for c in range(cps):
                rows = slice(c * CHUNK, (c + 1) * CHUNK)
                qv, kv, vv = q_ref[rows, cols], k_ref[rows, cols], v_ref[rows, cols]
                s = _dot_nt(qv, kv) * dm_ref[hh]
                o = _dot(s.astype(_BF), vv) + _dot(qv, state.astype(_BF)) * xi_ref[hh]
                o_ref[rows, cols] = o
                rp_ref[hh, c] = state
                vz = (vv.astype(_F32) * zeta_ref[hh]).astype(_BF)
                state = gc_ref[hh, 0:1, :] * state + _dot_tn(kv, vz)
                dlt = o - jnp.mean(o, axis=-1, keepdims=True)
                on = dlt * lax.rsqrt(jnp.mean(dlt * dlt, axis=-1, keepdims=True) + NORM_EPS)
                gt = gate_ref[rows, cols]
                y_ref[rows, cols] = (gt * _sigmoid(gt) * (on * ggn_ref[:, cols])).astype(_BF)
            r_scr[hh] = state

    return pl.pallas_call(
        body, name="retention_fwd", grid=(nc // cps,),
        in_specs=[blk, blk, blk, blk, _full_spec((1, RET_W))] + _head_specs(),
        out_specs=[blk, blk, pl.BlockSpec((N_HEAD, cps, HEAD_D, HEAD_D), lambda n: (0, n, 0, 0))],
        out_shape=[jax.ShapeDtypeStruct((L, RET_W), _F32), jax.ShapeDtypeStruct((L, RET_W), _BF),
                   jax.ShapeDtypeStruct((N_HEAD, nc, HEAD_D, HEAD_D), _F32)],
        scratch_shapes=[pltpu.VMEM((N_HEAD, HEAD_D, HEAD_D), _F32)],
        compiler_params=_params("arbitrary"),
    )(q, k, v, gate, ggn, *consts)


def _rows_to_segments(dst_scr, src_ref, seg):
    for g in range(dst_scr.shape[0]):
        for j in range(SUBLANES):
            dst_scr[g, pl.ds(j, seg, stride=SUBLANES), :] = src_ref[pl.ds(j * seg, seg), g * LANES:(g + 1) * LANES]


def _segments_to_rows(dst_ref, src_scr, seg):
    for g in range(src_scr.shape[0]):
        for j in range(SUBLANES):
            dst_ref[pl.ds(j * seg, seg), g * LANES:(g + 1) * LANES] = src_scr[g, pl.ds(j, seg, stride=SUBLANES), :]


def _scan_segments(x_ref, tab_ref, pw_ref, carry_ref, seg, reverse, entry_ref=None, fwd_ref=None, fwd_entry_ref=None,
                   da_ref=None):
    G = x_ref.shape[0]
    W = KB_STATES
    re, im = pl.ds(0, W), pl.ds(W, W)
    row_id = lax.broadcasted_iota(jnp.int32, (SUBLANES, W), 0)
    edge_in = (row_id == SUBLANES - 1) if reverse else (row_id == 0)
    edge_out = 0 if reverse else SUBLANES - 1
    a_tab = [(tab_ref[g, 0], tab_ref[g, 1]) for g in range(G)]

    def local(i, st):
        r = (seg - 1 - i) if reverse else i
        out = []
        for g in range(G):
            (ar, ai), (sr, si) = a_tab[g], st[g]
            nr = ar * sr - ai * si + x_ref[g, r, :, re]
            ni = ar * si + ai * sr + x_ref[g, r, :, im]
            x_ref[g, r, :, re] = nr
            x_ref[g, r, :, im] = ni
            out.append((nr, ni))
        return tuple(out)

    zero = jnp.zeros((SUBLANES, W), _F32)
    ends = lax.fori_loop(0, seg, local, tuple((zero, zero) for _ in range(G)), unroll=SCAN_UNROLL)

    entry = []
    shift = (SUBLANES - 1) if reverse else 1
    for g in range(G):
        er, ei = ends[g]
        fr = jnp.where(edge_in, carry_ref[g, :, re], pltpu.roll(er, shift, 0))
        fi = jnp.where(edge_in, carry_ref[g, :, im], pltpu.roll(ei, shift, 0))
        for j, dist in enumerate((1, 2, 4)):
            pr, pi = tab_ref[g, 2 + 2 * j], tab_ref[g, 3 + 2 * j]
            sh = (SUBLANES - dist) if reverse else dist
            sr, si = pltpu.roll(fr, sh, 0), pltpu.roll(fi, sh, 0)
            fr, fi = fr + pr * sr - pi * si, fi + pr * si + pi * sr
        br, bi = tab_ref[g, 8], tab_ref[g, 9]
        outr = br * fr - bi * fi + er
        outi = br * fi + bi * fr + ei
        carry_ref[g, :, re] = jnp.broadcast_to(outr[edge_out:edge_out + 1, :], (SUBLANES, W))
        carry_ref[g, :, im] = jnp.broadcast_to(outi[edge_out:edge_out + 1, :], (SUBLANES, W))
        entry.append((fr, fi))
        if entry_ref is not None:
            entry_ref[g, :, re] = fr
            entry_ref[g, :, im] = fi

    add_da = da_ref is not None

    def fix(r, st, first=False):
        out = []
        for g in range(G):
            fr, fi = entry[g]
            pwr, pwi = pw_ref[g, r, :, re], pw_ref[g, r, :, im]
            xr = x_ref[g, r, :, re] + (pwr * fr - pwi * fi)
            xi = x_ref[g, r, :, im] + (pwr * fi + pwi * fr)
            x_ref[g, r, :, re] = xr
            x_ref[g, r, :, im] = xi
            if add_da:
                prev = fwd_entry_ref.at[g] if first else fwd_ref.at[g, r - 1]
                xpr, xpi = prev[:, re], prev[:, im]
                out.append((st[g][0] + (xr * xpr + xi * xpi), st[g][1] + (xi * xpr - xr * xpi)))
            else:
                out.append(st[g])
        return tuple(out)

    if add_da:
        st = fix(0, tuple((zero, zero) for _ in range(G)), first=True)
        st = lax.fori_loop(1, seg, fix, st, unroll=SCAN_UNROLL)
        for g in range(G):
            da_ref[g, :, re] += st[g][0]
            da_ref[g, :, im] += st[g][1]
    else:
        lax.fori_loop(0, seg, fix, tuple((zero[0:1, 0:LANES],) for _ in range(G)), unroll=FIX_UNROLL)


def _s5_specs(seg, time=lambda t: t):
    G = KB_PER_STEP
    return dict(
        x=pl.BlockSpec((G, seg, SUBLANES, 2 * KB_STATES), lambda kb, t: (kb, time(t), 0, 0)),
        ent=pl.BlockSpec((G, 1, SUBLANES, 2 * KB_STATES), lambda kb, t: (kb, time(t), 0, 0)),
        b=pl.BlockSpec((G, LANES, 2 * KB_STATES), lambda kb, t: (kb, 0, 0)),
        c=pl.BlockSpec((G, 2 * KB_STATES, LANES), lambda kb, t: (kb, 0, 0)),
        tab=pl.BlockSpec((G, 10, SUBLANES, KB_STATES), lambda kb, t: (kb, 0, 0, 0)),
        pw=pl.BlockSpec((G, seg, 1, 2 * KB_STATES), lambda kb, t: (kb, 0, 0, 0)),
        d=pl.BlockSpec((1, G * LANES), lambda kb, t: (0, kb)),
    )


def _s5_fwd(u, bmat, cmat, tab_f, pw_f, d_skip, tb):
    L = u.shape[0]
    nt = L // tb
    seg = tb // SUBLANES
    G = KB_PER_STEP
    ucol = pl.BlockSpec((tb, G * LANES), lambda kb, t: (t, kb))
    sp = _s5_specs(seg)

    def body(u_ref, b_ref, c_ref, tab_ref, pw_ref, d_ref, s_ref, x_ref, ent_ref, up_scr, y_scr, carry_scr):
        @pl.when(pl.program_id(1) == 0)
        def _():
            carry_scr[...] = jnp.zeros_like(carry_scr)

        _rows_to_segments(up_scr, u_ref, seg)
        for g in range(G):
            x_ref[g] = _dot(up_scr[g].astype(_BF), b_ref[g]).reshape(seg, SUBLANES, 2 * KB_STATES)
        _scan_segments(x_ref, tab_ref, pw_ref, carry_scr, seg, reverse=False, entry_ref=ent_ref.at[:, 0])
        for g in range(G):
            y = _dot(x_ref[g].reshape(tb, 2 * KB_STATES).astype(_BF), c_ref[g])
            y_scr[g] = y + d_ref[:, g * LANES:(g + 1) * LANES] * up_scr[g]
        _segments_to_rows(s_ref, y_scr, seg)

    return pl.pallas_call(
        body, name="s5_fwd", grid=(N_KB // G, nt),
        in_specs=[ucol, sp["b"], sp["c"], sp["tab"], sp["pw"], sp["d"]],
        out_specs=[ucol, sp["x"], sp["ent"]],
        out_shape=[jax.ShapeDtypeStruct((L, SSM_W), _F32),
                   jax.ShapeDtypeStruct((N_KB, L // SUBLANES, SUBLANES, 2 * KB_STATES), _F32),
                   jax.ShapeDtypeStruct((N_KB, nt, SUBLANES, 2 * KB_STATES), _F32)],
        scratch_shapes=[pltpu.VMEM((G, tb, LANES), _F32)] * 2 + [pltpu.VMEM((G, SUBLANES, 2 * KB_STATES), _F32)],
        compiler_params=_params("parallel", "arbitrary"),
    )(u, bmat, cmat, tab_f, pw_f, d_skip)


def _mixout_fwd(s, y_ret, x, w_glu, w_out, g2, tm):
    L = s.shape[0]

    def body(s_ref, yr_ref, x_ref, wg_ref, wo_ref, g_ref, ys_ref, cat_ref, mix_ref, x2_ref):
        for rows in _row_chunks(tm):
            ys = _gelu(s_ref[rows, :]).astype(_BF)
            ys_ref[rows, :] = ys
            glu = _dot(ys, wg_ref[...])
            cat_ref[rows, :RET_W] = yr_ref[rows, :]
            cat_ref[rows, RET_W:] = (glu[:, :SSM_W] * _sigmoid(glu[:, SSM_W:])).astype(_BF)
            mix = _dot(cat_ref[rows, :], wo_ref[...])
            mix_ref[rows, :] = mix
            x2_ref[rows, :] = x_ref[rows, :] + mix * _rms_r(mix) * g_ref[...]

    return pl.pallas_call(
        body, name="mixout_fwd", grid=(L // tm,),
        in_specs=[_row_spec(tm, SSM_W), _row_spec(tm, RET_W), _row_spec(tm, D_MODEL),
                  _weight_spec((SSM_W, 2 * SSM_W)), _weight_spec((D_MODEL, D_MODEL)), _full_spec((1, D_MODEL))],
        out_specs=[_row_spec(tm, SSM_W), _row_spec(tm, D_MODEL), _row_spec(tm, D_MODEL), _row_spec(tm, D_MODEL)],
        out_shape=[jax.ShapeDtypeStruct((L, SSM_W), _BF), jax.ShapeDtypeStruct((L, D_MODEL), _BF),
                   jax.ShapeDtypeStruct((L, D_MODEL), _F32), jax.ShapeDtypeStruct((L, D_MODEL), _F32)],
        compiler_params=_params("parallel"),
    )(s, y_ret, x, w_glu, w_out, g2)


FF1_COLS = D_FF // N_DEV


def _ff1_fwd(x2, g3, w1, tm):
    L = x2.shape[0]

    def body(x_ref, g_ref, w_ref, h_ref, f_ref, a_ref):
        for rows in _row_chunks(tm):
            xv = x_ref[rows, :]
            h = (xv * _rms_r(xv) * g_ref[...]).astype(_BF)
            h_ref[rows, :] = h
            for j in range(N_DEV):
                cols = slice(j * FF1_COLS, (j + 1) * FF1_COLS)
                f = _dot(h, w_ref[j])
                f_ref[rows, cols] = f
                rl = jnp.maximum(f, 0.0)
                a_ref[rows, cols] = (rl * rl).astype(_BF)

    return pl.pallas_call(
        body, name="ff1_fwd", grid=(L // tm,),
        in_specs=[_row_spec(tm, D_MODEL), _full_spec((1, D_MODEL)), _weight_spec((N_DEV, D_MODEL, FF1_COLS))],
        out_specs=[_row_spec(tm, D_MODEL), _row_spec(tm, D_FF), _row_spec(tm, D_FF)],
        out_shape=[jax.ShapeDtypeStruct((L, D_MODEL), _BF), jax.ShapeDtypeStruct((L, D_FF), _F32),
                   jax.ShapeDtypeStruct((L, D_FF), _BF)],
        compiler_params=_params("parallel"),
    )(x2, g3, w1)


def _ff2_loss(act, x2, tgt, g4, w2, tm):
    L = act.shape[0]

    def body(f_ref, x_ref, t_ref, g_ref, w_ref, dy_ref, dm_ref, dg_ref, ls_ref):
        @pl.when(pl.program_id(0) == 0)
        def _():
            dg_ref[...] = jnp.zeros_like(dg_ref)
            ls_ref[...] = jnp.zeros_like(ls_ref)

        g = g_ref[...]
        for rows in _row_chunks(tm):
            m = _dot(f_ref[rows, :], w_ref[...])
            y = x_ref[rows, :] + m * _rms_r(m) * g
            err = y - t_ref[rows, :]
            ls_ref[...] += jnp.sum(err * err, axis=0, keepdims=True)
            dy = err * (1.0 / D_MODEL)
            dy_ref[rows, :] = dy
            dm, dgr = _rms_bwd(m, g, dy)
            dm_ref[rows, :] = dm.astype(_BF)
            dg_ref[...] += jnp.sum(dgr, axis=0, keepdims=True)

    return pl.pallas_call(
        body, name="ff2_loss", grid=(L // tm,),
        in_specs=[_row_spec(tm, D_FF), _row_spec(tm, D_MODEL), _row_spec(tm, D_MODEL),
                  _full_spec((1, D_MODEL)), _weight_spec((D_FF, D_MODEL))],
        out_specs=[_row_spec(tm, D_MODEL), _row_spec(tm, D_MODEL), _full_spec((1, D_MODEL)), _full_spec((1, D_MODEL))],
        out_shape=[jax.ShapeDtypeStruct((L, D_MODEL), _F32), jax.ShapeDtypeStruct((L, D_MODEL), _BF),
                   jax.ShapeDtypeStruct((1, D_MODEL), _F32), jax.ShapeDtypeStruct((1, D_MODEL), _F32)],
        compiler_params=_params("arbitrary"),
    )(act, x2, tgt, g4, w2)


def _ff2_bwd(dm, f1, w2, tm, tn):
    L = dm.shape[0]
    last = L // tm - 1

    def body(dm_ref, f_ref, w_ref, df_ref, dw_ref, acc):
        @pl.when(pl.program_id(1) == 0)
        def _():
            acc[...] = jnp.zeros_like(acc)

        dmv = dm_ref[...]
        rl = jnp.maximum(f_ref[...], 0.0)
        df_ref[...] = (_dot_nt(dmv, w_ref[...]) * (2.0 * rl)).astype(_BF)
        acc[...] += _dot_tn((rl * rl).astype(_BF), dmv)

        @pl.when(pl.program_id(1) == last)
        def _():
            dw_ref[...] = acc[...].astype(_BF)

    return pl.pallas_call(
        body, name="ff2_bwd", grid=(D_FF // tn, L // tm),
        in_specs=[pl.BlockSpec((tm, D_MODEL), lambda j, i: (i, 0)), pl.BlockSpec((tm, tn), lambda j, i: (i, j)),
                  pl.BlockSpec((tn, D_MODEL), lambda j, i: (j, 0))],
        out_specs=[pl.BlockSpec((tm, tn), lambda j, i: (i, j)), pl.BlockSpec((tn, D_MODEL), lambda j, i: (j, 0))],
        out_shape=[jax.ShapeDtypeStruct((L, D_FF), _BF), jax.ShapeDtypeStruct((D_FF, D_MODEL), _BF)],
        scratch_shapes=[pltpu.VMEM((tn, D_MODEL), _F32)],
        compiler_params=_params("parallel", "arbitrary"),
    )(dm, f1, w2)


def _ff1_bwd(df1, w1, x2, mix, dy, g3, g2, tm):
    L = df1.shape[0]

    def body(df_ref, w_ref, x2_ref, mix_ref, dy_ref, g3_ref, g2_ref, dx2_ref, dmix_ref, dg3_ref, dg2_ref):
        @pl.when(pl.program_id(0) == 0)
        def _():
            dg3_ref[...] = jnp.zeros_like(dg3_ref)
            dg2_ref[...] = jnp.zeros_like(dg2_ref)

        for rows in _row_chunks(tm):
            dh = _dot_nt(df_ref[rows, 0:FF1_COLS], w_ref[0])
            for j in range(1, N_DEV):
                dh = dh + _dot_nt(df_ref[rows, j * FF1_COLS:(j + 1) * FF1_COLS], w_ref[j])
            dz, dgr = _rms_bwd(x2_ref[rows, :], g3_ref[...], dh)
            dg3_ref[...] += jnp.sum(dgr, axis=0, keepdims=True)
            dx2 = dy_ref[rows, :] + dz
            dx2_ref[rows, :] = dx2
            dmx, dgr2 = _rms_bwd(mix_ref[rows, :], g2_ref[...], dx2)
            dg2_ref[...] += jnp.sum(dgr2, axis=0, keepdims=True)
            dmix_ref[rows, :] = dmx.astype(_BF)

    vec = _full_spec((1, D_MODEL))
    return pl.pallas_call(
        body, name="ff1_bwd", grid=(L // tm,),
        in_specs=[_row_spec(tm, D_FF), _weight_spec((N_DEV, D_MODEL, FF1_COLS)), _row_spec(tm, D_MODEL),
                  _row_spec(tm, D_MODEL), _row_spec(tm, D_MODEL), vec, vec],
        out_specs=[_row_spec(tm, D_MODEL), _row_spec(tm, D_MODEL), vec, vec],
        out_shape=[jax.ShapeDtypeStruct((L, D_MODEL), _F32), jax.ShapeDtypeStruct((L, D_MODEL), _BF),
                   jax.ShapeDtypeStruct((1, D_MODEL), _F32), jax.ShapeDtypeStruct((1, D_MODEL), _F32)],
        compiler_params=_params("arbitrary"),
    )(df1, w1, x2, mix, dy, g3, g2)


def _matmul_tn(a, b, tm, tn, name, slots=False):
    L, K = a.shape
    N = b.shape[1]
    last = L // tm - 1

    def body(a_ref, b_ref, o_ref, acc):
        @pl.when(pl.program_id(1) == 0)
        def _():
            acc[...] = jnp.zeros_like(acc)

        acc[...] += _dot_tn(a_ref[...].astype(_BF), b_ref[...].astype(_BF))

        @pl.when(pl.program_id(1) == last)
        def _():
            if slots:
                o_ref[0] = acc[...].astype(_BF)
            else:
                o_ref[...] = acc[...].astype(_BF)

    if slots:
        out_spec = pl.BlockSpec((1, K, tn), lambda j, i: (j, 0, 0))
        out_shape = jax.ShapeDtypeStruct((N // tn, K, tn), _BF)
    else:
        out_spec = pl.BlockSpec((K, tn), lambda j, i: (0, j))
        out_shape = jax.ShapeDtypeStruct((K, N), _BF)
    return pl.pallas_call(
        body, name=name, grid=(N // tn, L // tm),
        in_specs=[pl.BlockSpec((tm, K), lambda j, i: (i, 0)), pl.BlockSpec((tm, tn), lambda j, i: (i, j))],
        out_specs=out_spec, out_shape=out_shape,
        scratch_shapes=[pltpu.VMEM((K, tn), _F32)],
        compiler_params=_params("parallel", "arbitrary"),
    )(a, b)


def _dw_in_t(pieces, h, tk):
    L = h.shape[0]
    last = L // tk - 1

    def body(p0, p1, p2, p3, p4, h_ref, o_ref, acc):
        @pl.when(pl.program_id(0) == 0)
        def _():
            acc[...] = jnp.zeros_like(acc)

        hv = h_ref[...]
        for j, p in enumerate((p0, p1, p2, p3, p4)):
            acc[j * RET_W:(j + 1) * RET_W, :] += _dot_tn(p[...].astype(_BF), hv)

        @pl.when(pl.program_id(0) == last)
        def _():
            o_ref[...] = acc[...].astype(_BF)

    return pl.pallas_call(
        body, name="dw_in", grid=(L // tk,),
        in_specs=[_row_spec(tk, RET_W)] * 5 + [_row_spec(tk, D_MODEL)],
        out_specs=_full_spec((IN_COLS, D_MODEL)), out_shape=jax.ShapeDtypeStruct((IN_COLS, D_MODEL), _BF),
        scratch_shapes=[pltpu.VMEM((IN_COLS, D_MODEL), _F32)],
        compiler_params=_params("arbitrary"),
    )(*pieces, h)


def _mixout_bwd(dmix, w_out, w_glu, ys, s, o, gate, ggn, tm, after=()):
    L = dmix.shape[0]

    def body(dmix_ref, wo_ref, wg_ref, ys_ref, s_ref, o_ref, gate_ref, ggn_ref,
             dglu_ref, ds_ref, dgate_ref, do_ref, dggn_ref):
        @pl.when(pl.program_id(0) == 0)
        def _():
            dggn_ref[...] = jnp.zeros_like(dggn_ref)

        ggn = ggn_ref[...]
        for rows in _row_chunks(tm):
            dcat = _dot_nt(dmix_ref[rows, :], wo_ref[...])
            dy_ret, dy_ssm = dcat[:, :RET_W], dcat[:, RET_W:]
            glu = _dot(ys_ref[rows, :], wg_ref[...])
            ga, sg = glu[:, :SSM_W], _sigmoid(glu[:, SSM_W:])
            dga = (dy_ssm * sg).astype(_BF)
            dgb = (dy_ssm * ga * sg * (1.0 - sg)).astype(_BF)
            dglu_ref[rows, :SSM_W] = dga
            dglu_ref[rows, SSM_W:] = dgb
            dys = _dot_nt(dga, wg_ref[:, :SSM_W]) + _dot_nt(dgb, wg_ref[:, SSM_W:])
            ds_ref[rows, :] = dys * _gelu_grad(s_ref[rows, :])
            gt = gate_ref[rows, :]
            sgt = _sigmoid(gt)
            for hh in range(N_HEAD):
                cols = slice(hh * HEAD_D, (hh + 1) * HEAD_D)
                ov = o_ref[rows, cols]
                dlt = ov - jnp.mean(ov, axis=-1, keepdims=True)
                rstd = lax.rsqrt(jnp.mean(dlt * dlt, axis=-1, keepdims=True) + NORM_EPS)
                on = dlt * rstd
                dyr = dy_ret[:, cols] * (gt[:, cols] * sgt[:, cols])
                dgate_ref[rows, cols] = dy_ret[:, cols] * (on * ggn[:, cols]) * (sgt[:, cols] * (1.0 + gt[:, cols] * (1.0 - sgt[:, cols])))
                dggn_ref[:, cols] += jnp.sum(dyr * on, axis=0, keepdims=True)
                don = dyr * ggn[:, cols]
                do = rstd * (don - jnp.mean(don, axis=-1, keepdims=True) - on * jnp.mean(don * on, axis=-1, keepdims=True))
                do_ref[rows, cols] = do.astype(_BF)

    body, in_specs, operands = _ordered(
        body, [_row_spec(tm, D_MODEL), _weight_spec((D_MODEL, D_MODEL)), _weight_spec((SSM_W, 2 * SSM_W)),
               _row_spec(tm, SSM_W), _row_spec(tm, SSM_W), _row_spec(tm, RET_W), _row_spec(tm, RET_W),
               _full_spec((1, RET_W))], (dmix, w_out, w_glu, ys, s, o, gate, ggn), after)
    return pl.pallas_call(
        body, name="mixout_bwd", grid=(L // tm,),
        in_specs=in_specs,
        out_specs=[_row_spec(tm, 2 * SSM_W), _row_spec(tm, SSM_W), _row_spec(tm, RET_W), _row_spec(tm, RET_W),
                   _full_spec((1, RET_W))],
        out_shape=[jax.ShapeDtypeStruct((L, 2 * SSM_W), _BF), jax.ShapeDtypeStruct((L, SSM_W), _F32),
                   jax.ShapeDtypeStruct((L, RET_W), _F32), jax.ShapeDtypeStruct((L, RET_W), _BF),
                   jax.ShapeDtypeStruct((1, RET_W), _F32)],
        compiler_params=_params("arbitrary"),
    )(*operands)


def _s5_bwd(u, ds, xs, ent, bmat, cmat, tab_r, pw_r, d_skip, tb, after=()):
    L = u.shape[0]
    nt = L // tb
    seg = tb // SUBLANES
    G = KB_PER_STEP
    rcol = pl.BlockSpec((tb, G * LANES), lambda kb, t: (nt - 1 - t, kb))
    sp = _s5_specs(seg, time=lambda t: nt - 1 - t)
    aspec = pl.BlockSpec((G, SUBLANES, 2 * KB_STATES), lambda kb, t: (kb, 0, 0))

    def body(u_ref, ds_ref, x_ref, ent_ref, b_ref, c_ref, tr_ref, pr_ref, d_ref,
             du_ref, db_ref, dc_ref, da_ref, dd_ref, up_scr, dp_scr, g_scr, lc_scr):
        @pl.when(pl.program_id(1) == 0)
        def _():
            lc_scr[...] = jnp.zeros_like(lc_scr)
            db_ref[...] = jnp.zeros_like(db_ref)
            dc_ref[...] = jnp.zeros_like(dc_ref)
            da_ref[...] = jnp.zeros_like(da_ref)
            dd_ref[...] = jnp.zeros_like(dd_ref)

        _rows_to_segments(up_scr, u_ref, seg)
        _rows_to_segments(dp_scr, ds_ref, seg)
        for g in range(G):
            g_scr[g] = _dot_nt(dp_scr[g].astype(_BF), c_ref[g]).reshape(seg, SUBLANES, 2 * KB_STATES)
        _scan_segments(g_scr, tr_ref, pr_ref, lc_scr, seg, reverse=True, fwd_ref=x_ref, fwd_entry_ref=ent_ref.at[:, 0],
                       da_ref=da_ref)
        for g in range(G):
            cols = slice(g * LANES, (g + 1) * LANES)
            uv, dsv = up_scr[g], dp_scr[g]
            ub, dsb = uv.astype(_BF), dsv.astype(_BF)
            lamb = g_scr[g].reshape(tb, 2 * KB_STATES).astype(_BF)
            db_ref[g] += _dot_tn(ub, lamb)
            dc_ref[g] += _dot_tn(dsb, x_ref[g].reshape(tb, 2 * KB_STATES).astype(_BF))
            dd_ref[:, cols] += jnp.sum(dsv * uv, axis=0, keepdims=True)
            up_scr[g] = _dot_nt(lamb, b_ref[g]) + d_ref[:, cols] * dsv
        _segments_to_rows(du_ref, up_scr, seg)

    body, in_specs, operands = _ordered(
        body, [rcol, rcol, sp["x"], sp["ent"], sp["b"], sp["c"], sp["tab"], sp["pw"], sp["d"]],
        (u, ds, xs, ent, bmat, cmat, tab_r, pw_r, d_skip), after)
    return pl.pallas_call(
        body, name="s5_bwd", grid=(N_KB // G, nt),
        in_specs=in_specs,
        out_specs=[rcol, sp["b"], sp["b"], aspec, sp["d"]],
        out_shape=[jax.ShapeDtypeStruct((L, SSM_W), _F32),
                   jax.ShapeDtypeStruct((N_KB, LANES, 2 * KB_STATES), _F32),
                   jax.ShapeDtypeStruct((N_KB, LANES, 2 * KB_STATES), _F32),
                   jax.ShapeDtypeStruct((N_KB, SUBLANES, 2 * KB_STATES), _F32),
                   jax.ShapeDtypeStruct((1, SSM_W), _F32)],
        scratch_shapes=[pltpu.VMEM((G, tb, LANES), _F32)] * 2
        + [pltpu.VMEM((G, seg, SUBLANES, 2 * KB_STATES), _F32), pltpu.VMEM((G, SUBLANES, 2 * KB_STATES), _F32)],
        compiler_params=_params("parallel", "arbitrary"),
    )(*operands)


def _retention_bwd(q, k, v, do, r_prev, consts, cosf, sinf, after=()):
    L = q.shape[0]
    nc = L // CHUNK
    cps = math.gcd(RET_STEP_CHUNKS, nc)
    nb = nc // cps
    blk = pl.BlockSpec((cps * CHUNK, RET_W), lambda n: (nb - 1 - n, 0))
    rope_blk = pl.BlockSpec((cps * CHUNK, HEAD_D), lambda n: (nb - 1 - n, 0))

    def body(q_ref, k_ref, v_ref, do_ref, rp_ref, dm_ref, xi_ref, zeta_ref, gc_ref, cos_ref, sin_ref,
             dq_ref, dk_ref, dv_ref, g_scr):
        @pl.when(pl.program_id(0) == 0)
        def _():
            g_scr[...] = jnp.zeros_like(g_scr)

        for hh in range(N_HEAD):
            cols = slice(hh * HEAD_D, (hh + 1) * HEAD_D)
            dm, zeta = dm_ref[hh], zeta_ref[hh]
            gst = g_scr[hh]
            for c in reversed(range(cps)):
                rows = slice(c * CHUNK, (c + 1) * CHUNK)
                qv, kv, vv, dov = q_ref[rows, cols], k_ref[rows, cols], v_ref[rows, cols], do_ref[rows, cols]
                rb = rp_ref[hh, c].astype(_BF)
                gb = gst.astype(_BF)
                sb = (_dot_nt(qv, kv) * dm).astype(_BF)
                dab = (_dot_nt(dov, vv) * dm).astype(_BF)
                dox = (dov.astype(_F32) * xi_ref[hh]).astype(_BF)
                vz = (vv.astype(_F32) * zeta).astype(_BF)
                dq = _dot(dab, kv) + _dot_nt(dox, rb)
                dk = _dot_tn(dab, qv) + _dot_nt(vz, gb)
                dv = _dot_tn(sb, dov) + _dot(kv, gb) * zeta
                gst = gc_ref[hh, 0:1, :] * gst + _dot_tn(qv, dox)
                cs, sn = cos_ref[rows, :], sin_ref[rows, :]
                dq_ref[rows, cols] = _rope_t(dq, cs, sn).astype(_BF)
                dk_ref[rows, cols] = (_rope_t(dk, cs, sn) * (HEAD_D ** -0.5)).astype(_BF)
                dv_ref[rows, cols] = dv.astype(_BF)
            g_scr[hh] = gst

    body, in_specs, operands = _ordered(
        body, [blk, blk, blk, blk, pl.BlockSpec((N_HEAD, cps, HEAD_D, HEAD_D), lambda n: (0, nb - 1 - n, 0, 0))]
        + _head_specs() + [rope_blk, rope_blk], (q, k, v, do, r_prev, *consts, cosf, sinf), after)
    return pl.pallas_call(
        body, name="retention_bwd", grid=(nb,),
        in_specs=in_specs,
        out_specs=[blk, blk, blk],
        out_shape=[jax.ShapeDtypeStruct((L, RET_W), _BF)] * 3,
        scratch_shapes=[pltpu.VMEM((N_HEAD, HEAD_D, HEAD_D), _F32)],
        compiler_params=_params("arbitrary"),
    )(*operands)


def _inproj_bwd(pieces, w_in_t, x, dx2, g1, tm, after=()):
    L = x.shape[0]

    def body(p0, p1, p2, p3, p4, w_ref, x_ref, dx2_ref, g_ref, dx_ref, dg_ref):
        @pl.when(pl.program_id(0) == 0)
        def _():
            dg_ref[...] = jnp.zeros_like(dg_ref)

        for rows in _row_chunks(tm):
            dh = None
            for j, p in enumerate((p0, p1, p2, p3, p4)):
                part = _dot(p[rows, :].astype(_BF), w_ref[j * RET_W:(j + 1) * RET_W, :])
                dh = part if dh is None else dh + part
            dz, dgr = _rms_bwd(x_ref[rows, :], g_ref[...], dh)
            dx_ref[rows, :] = dx2_ref[rows, :] + dz
            dg_ref[...] += jnp.sum(dgr, axis=0, keepdims=True)

    body, in_specs, operands = _ordered(
        body, [_row_spec(tm, RET_W)] * 5 + [_weight_spec((IN_COLS, D_MODEL)), _row_spec(tm, D_MODEL),
                                             _row_spec(tm, D_MODEL), _full_spec((1, D_MODEL))],
        (*pieces, w_in_t, x, dx2, g1), after)
    return pl.pallas_call(
        body, name="inproj_bwd", grid=(L // tm,),
        in_specs=in_specs,
        out_specs=[_row_spec(tm, D_MODEL), _full_spec((1, D_MODEL))],
        out_shape=[jax.ShapeDtypeStruct((L, D_MODEL), _F32), jax.ShapeDtypeStruct((1, D_MODEL), _F32)],
        compiler_params=_params("arbitrary"),
    )(*operands)


def _sum_adamw(parts, w, m, v, tr, name):
    _, R, Cc = parts.shape

    def body(p_ref, w_ref, m_ref, v_ref, g_ref, d_ref, nm_ref, nv_ref):
        gv = p_ref[0].astype(_F32)
        for s in range(1, N_DEV):
            gv = gv + p_ref[s].astype(_F32)
        g_ref[...] = gv
        nm = ADAM_B1 * m_ref[...] + (1.0 - ADAM_B1) * gv
        nv = ADAM_B2 * v_ref[...] + (1.0 - ADAM_B2) * (gv * gv)
        m_hat = nm / (1.0 - ADAM_B1 ** ADAM_STEP)
        v_hat = nv / (1.0 - ADAM_B2 ** ADAM_STEP)
        d_ref[...] = -ADAM_LR * (m_hat / (jnp.sqrt(v_hat) + ADAM_EPS) + ADAM_WD * w_ref[...])
        nm_ref[...] = nm
        nv_ref[...] = nv

    spec = _row_spec(tr, Cc)
    return pl.pallas_call(
        body, name=name, grid=(R // tr,),
        in_specs=[pl.BlockSpec((N_DEV, tr, Cc), lambda i: (0, i, 0))] + [spec] * 3, out_specs=[spec] * 4,
        out_shape=[jax.ShapeDtypeStruct((R, Cc), _F32)] * 4,
        compiler_params=_params("parallel"),
    )(parts, w, m, v)


def _my_place():
    return lax.axis_index("x"), lax.axis_index("y"), lax.axis_index("c")


HBM_SPEC = pl.BlockSpec(memory_space=pltpu.HBM)
SEM_SPEC = pl.BlockSpec(memory_space=pltpu.SEMAPHORE)
DATAFLOW = pltpu.SideEffectType.DATAFLOW_SIDE_EFFECTING


def _my_index():
    x, y, c = _my_place()
    return 4 * x + 2 * y + c


def _landing(own_block):
    zone = lax.empty((N_DEV,) + own_block.shape, own_block.dtype)
    return lax.dynamic_update_index_in_dim(zone, own_block, _my_index(), 0)


def _split_copies(src_refs, land_refs, send_sems, recv_sems, gather, first=0):
    x, y, c = _my_place()
    me = 4 * x + 2 * y + c
    copies = []
    for a, (src, land) in enumerate(zip(src_refs, land_refs)):
        for kk in range(1, N_DEV):
            px, py, pc = x ^ (kk >> 2), y ^ ((kk >> 1) & 1), c ^ (kk & 1)
            peer = 4 * px + 2 * py + pc
            copies.append(pltpu.make_async_remote_copy(
                src_ref=src if gather else src.at[peer], dst_ref=land.at[me],
                send_sem=send_sems.at[(first + a) * 7 + kk - 1], recv_sem=recv_sems.at[(first + a) * 7 + kk - 1],
                device_id=(px, py, pc), device_id_type=MESH))
    return copies


def _split_start(srcs, lands, gather, name):
    n = len(srcs)

    def body(*refs):
        src_refs, land_refs = refs[:n], refs[n:2 * n]
        send_sems, recv_sems = refs[2 * n], refs[2 * n + 1]
        token = refs[-1]
        for cp in _split_copies(src_refs, land_refs, send_sems, recv_sems, gather):
            cp.start()
        token[...] = jnp.zeros_like(token)

    outs = pl.pallas_call(
        body, name=name,
        out_shape=(pltpu.SemaphoreType.DMA((7 * n,)), pltpu.SemaphoreType.DMA((7 * n,)),
                   *[pltpu.HBM(t.shape, t.dtype) for t in srcs], *[pltpu.HBM(t.shape, t.dtype) for t in lands],
                   jax.ShapeDtypeStruct((SUBLANES, LANES), _F32)),
        in_specs=[HBM_SPEC] * (2 * n),
        out_specs=(SEM_SPEC, SEM_SPEC, *[HBM_SPEC] * (2 * n), pl.BlockSpec(memory_space=pltpu.VMEM)),
        input_output_aliases={i: 2 + i for i in range(2 * n)},
        compiler_params=pltpu.CompilerParams(has_side_effects=DATAFLOW),
    )(*[pltpu.with_memory_space_constraint(t, pltpu.HBM) for t in list(srcs) + list(lands)])
    return outs[0], outs[1], outs[2:2 + n], outs[2 + n:2 + 2 * n], outs[-1]


def _split_wait(send_sems, recv_sems, srcs, lands, after, gather, name, first=0):
    n = len(srcs)

    def body(*refs):
        src_refs, land_refs = refs[:n], refs[n:2 * n]
        send_s, recv_s = refs[2 * n], refs[2 * n + 1]
        for cp in _split_copies(src_refs, land_refs, send_s, recv_s, gather, first):
            cp.wait_send()
            cp.wait_recv()

    outs = pl.pallas_call(
        body, name=name,
        out_shape=tuple(pltpu.HBM(t.shape, t.dtype) for t in list(srcs) + list(lands)),
        in_specs=[HBM_SPEC] * (2 * n) + [SEM_SPEC, SEM_SPEC, pl.BlockSpec(memory_space=pl.ANY)],
        out_specs=tuple([HBM_SPEC] * (2 * n)),
        input_output_aliases={i: i for i in range(2 * n)},
        compiler_params=pltpu.CompilerParams(has_side_effects=DATAFLOW),
    )(*srcs, *lands, send_sems, recv_sems, after)
    return outs[n:]


def _discretize(lam_re, lam_im, log_dt, b_re, b_im):
    lr = jnp.minimum(lam_re, -1e-4)
    li = lam_im
    dt = jnp.exp(log_dt)[:, None]
    er = jnp.exp(lr * dt)
    ar, ai = er * jnp.cos(li * dt), er * jnp.sin(li * dt)
    den = lr * lr + li * li
    cr = ((ar - 1.0) * lr + ai * li) / den
    ci = (ai * lr - (ar - 1.0) * li) / den
    bbr = cr[:, :, None] * b_re - ci[:, :, None] * b_im
    bbi = cr[:, :, None] * b_im + ci[:, :, None] * b_re
    return ar, ai, bbr, bbi


def _cmul(ar, ai, br, bi):
    return ar * br - ai * bi, ar * bi + ai * br


def _cpowers(ar, ai, n):
    pr, pi = ar[None], ai[None]
    while pr.shape[0] < n:
        nr, ni = _cmul(pr, pi, pr[-1][None], pi[-1][None])
        pr, pi = jnp.concatenate([pr, nr]), jnp.concatenate([pi, ni])
    return pr[:n], pi[:n]


def _scan_tables(ar, ai, seg, reverse):
    if reverse:
        ai = -ai
    ar, ai = ar.reshape(N_KB, KB_STATES), ai.reshape(N_KB, KB_STATES)
    pr, pi = _cpowers(ar, ai, seg)
    a1 = (pr[-1], pi[-1])
    a2 = _cmul(*a1, *a1)
    a4 = _cmul(*a2, *a2)
    row = jnp.arange(SUBLANES)[None, :, None]
    wide = lambda t: jnp.broadcast_to(t[:, None, :], (N_KB, SUBLANES, KB_STATES))
    tabs = [wide(ar), wide(ai)]
    for dist, (qr, qi) in ((1, a1), (2, a2), (4, a4)):
        keep = (row < SUBLANES - dist) if reverse else (row >= dist)
        tabs += [jnp.where(keep, wide(qr), 0.0), jnp.where(keep, wide(qi), 0.0)]
    tabs += [wide(a1[0]), wide(a1[1])]
    if reverse:
        pr, pi = pr[::-1], pi[::-1]
    pw = jnp.transpose(jnp.concatenate([pr, pi], axis=-1), (1, 0, 2))[:, :, None, :]
    return jnp.stack(tabs, axis=1).astype(_F32), pw.astype(_F32)


def _block_diag_in(br, bi):
    eye = jnp.eye(GROUPS_PER_KB, dtype=_F32)
    one = lambda t: jnp.einsum("kgpc,gh->kgchp", t.reshape(N_KB, GROUPS_PER_KB, N_STATE, SSM_GC), eye).reshape(
        N_KB, LANES, KB_STATES)
    return jnp.concatenate([one(br), one(bi)], axis=-1)


def _block_diag_in_t(dmat):
    d6 = dmat.reshape(N_KB, GROUPS_PER_KB, SSM_GC, 2, GROUPS_PER_KB, N_STATE)
    eye = jnp.eye(GROUPS_PER_KB, dtype=_F32)
    both = jnp.einsum("kgcrhp,gh->rkgpc", d6, eye).reshape(2, N_GROUP, N_STATE, SSM_GC)
    return both[0], both[1]


def _block_diag_out(c_re, c_im):
    eye = jnp.eye(GROUPS_PER_KB, dtype=_F32)
    one = lambda t: jnp.einsum("kgcp,gh->khpgc", t.reshape(N_KB, GROUPS_PER_KB, SSM_GC, N_STATE), eye).reshape(
        N_KB, KB_STATES, LANES)
    return jnp.concatenate([one(c_re), -one(c_im)], axis=1)


def _block_diag_out_t(dmat_t):
    d6 = dmat_t.reshape(N_KB, GROUPS_PER_KB, SSM_GC, 2, GROUPS_PER_KB, N_STATE)
    eye = jnp.eye(GROUPS_PER_KB, dtype=_F32)
    both = jnp.einsum("kgcrhp,gh->rkgcp", d6, eye).reshape(2, N_GROUP, SSM_GC, N_STATE)
    return both[0], -both[1]


SMALL_NAMES = ("norm_mix_pre", "norm_mix_post", "ret_gn_gain", "ssm_lambda_re", "ssm_lambda_im", "ssm_log_dt",
               "ssm_b_re", "ssm_b_im", "ssm_c_re", "ssm_c_im", "ssm_d", "norm_mlp_pre", "norm_mlp_post")


def _local_grads(x, tgt, small, weights, emit, emit_small, tm, tk, tb, zero=0.0):
    L = x.shape[0]
    g1, g2, ggn = small["norm_mix_pre"], small["norm_mix_post"], small["ret_gn_gain"]
    g3, g4, d_skip = small["norm_mlp_pre"], small["norm_mlp_post"], small["ssm_d"]

    rope = _rope_tables(L)
    consts = _ret_consts()

    disc_in = (small["ssm_lambda_re"][0], small["ssm_lambda_im"][0], small["ssm_log_dt"][0] + zero,
               small["ssm_b_re"][0], small["ssm_b_im"][0])
    (ar, ai, bbr, bbi), disc_vjp = jax.vjp(_discretize, *disc_in)
    bmat = _block_diag_in(bbr, bbi).astype(_BF)
    cmat = _block_diag_out(small["ssm_c_re"][0], small["ssm_c_im"][0]).astype(_BF)
    seg = tb // SUBLANES
    tab_f, pw_f = _scan_tables(ar, ai, seg, False)
    tab_r, pw_r = _scan_tables(ar, ai, seg, True)

    h1 = _prenorm(x, g1, min(4 * tm, L), after=(pw_r,))
    (w_in_t,) = weights("in", h1)
    q, k, v, gate, u, cosf, sinf = _inproj_fwd(h1, w_in_t, rope, min(4 * tm, L))
    o, y_ret, r_prev = _retention_fwd(q, k, v, gate, ggn, consts)
    s, xs, ent = _s5_fwd(u, bmat, cmat, tab_f, pw_f, d_skip, tb)
    w_glu, w_out = weights("mix", s)
    ys, cat, mix, x2 = _mixout_fwd(s, y_ret, x, w_glu, w_out, g2, min(2 * tm, L))
    w_ff1, w_ff2 = weights("mlp", x2)
    h3, f1, act = _ff1_fwd(x2, g3, w_ff1, min(2 * tm, L))
    dy, dm, dg4, sq = _ff2_loss(act, x2, tgt, g4, w_ff2, min(2 * tm, L))

    df1, dw_ff2 = _ff2_bwd(dm, f1, w_ff2, min(1024, L), 1024)
    dx2, dmix, dg3, dg2 = _ff1_bwd(df1, w_ff1, x2, mix, dy, g3, g2, min(2 * tm, L))
    dw_ff1 = _matmul_tn(h3, df1, tk, FF1_COLS, "dw_ff1", slots=True)
    token = emit({"w_ff1": dw_ff1, "w_ff2": dw_ff2})
    dglu, ds, dgate, do, dggn = _mixout_bwd(dmix, w_out, w_glu, ys, s, o, gate, ggn, min(2 * tm, L), after=token)
    dw_out = _matmul_tn(cat, dmix, tk, 1024, "dw_out")
    dw_glu = _matmul_tn(ys, dglu, tk, 1024, "dw_glu")
    token = emit({"w_glu": dw_glu, "w_out": dw_out})
    du, dbmat, dcmat, da8, dd = _s5_bwd(u, ds, xs, ent, bmat, cmat, tab_r, pw_r, d_skip, tb, after=token)

    da = jnp.sum(da8, axis=1)
    dar = da[:, :KB_STATES].reshape(N_GROUP, N_STATE)
    dai = da[:, KB_STATES:].reshape(N_GROUP, N_STATE)
    dbr, dbi = _block_diag_in_t(dbmat)
    dlre, dlim, dldt, dbre, dbim = disc_vjp((dar, dai, dbr, dbi))
    dcre, dcim = _block_diag_out_t(dcmat)
    token = emit_small({
        "norm_mix_post": dg2, "ret_gn_gain": dggn,
        "ssm_lambda_re": dlre[None], "ssm_lambda_im": dlim[None], "ssm_log_dt": dldt[None],
        "ssm_b_re": dbre[None], "ssm_b_im": dbim[None], "ssm_c_re": dcre[None], "ssm_c_im": dcim[None],
        "ssm_d": dd, "norm_mlp_pre": dg3, "norm_mlp_post": dg4,
    }, sq)

    dq, dk, dv = _retention_bwd(q, k, v, do, r_prev, consts, cosf, sinf, after=token)
    pieces = (dq, dk, dv, dgate, du)
    dw_in_t = _dw_in_t(pieces, h1, min(1024, L))
    token = emit({"w_in": dw_in_t})
    gx, dg1 = _inproj_bwd(pieces, w_in_t, x, dx2, g1, min(2 * tm, L), after=token)
    return gx, dg1


BIG_SHAPES = {"w_in": (D_MODEL, IN_COLS // N_DEV), "w_glu": (SSM_W, 2 * SSM_W // N_DEV), "w_out": (D_MODEL // N_DEV, D_MODEL),
              "w_ff1": (D_MODEL, FF1_COLS), "w_ff2": (D_FF // N_DEV, D_MODEL)}
BIG_NAMES = ("w_in", "w_glu", "w_out", "w_ff1", "w_ff2")


def _cols_from_slots(g):
    return jnp.transpose(g, (1, 0, 2)).reshape(g.shape[1], N_DEV * g.shape[2])


def _cols_to_slots(dw):
    r, cols = dw.shape
    return jnp.transpose(dw.reshape(r, N_DEV, cols // N_DEV), (1, 0, 2))


WEIGHT_GROUPS = {"in": ("w_in",), "mix": ("w_glu", "w_out"), "mlp": ("w_ff1", "w_ff2")}


def _weight_from_slots(name, g):
    if name == "w_glu":
        return _cols_from_slots(g)
    if name == "w_ff1":
        return g
    return g.reshape(N_DEV * g.shape[1], g.shape[2])


def _grad_slots(name, dw):
    if name == "w_glu":
        return _cols_to_slots(dw)
    if name == "w_ff1":
        return dw
    if name == "w_in":
        return dw.reshape(N_DEV, BIG_SHAPES[name][1], BIG_SHAPES[name][0])
    return dw.reshape((N_DEV,) + BIG_SHAPES[name])


PIECE_ROWS = 8


VEC_NAMES = tuple(n for n in SMALL_NAMES if n[:6] not in ("ssm_b_", "ssm_c_"))
BC_NAMES = ("ssm_b_re", "ssm_b_im", "ssm_c_re", "ssm_c_im")
BC_ROWS = N_GROUP * SSM_GC


def _bc_view(name, t):
    t = t[0]
    if name.startswith("ssm_b_"):
        t = jnp.swapaxes(t, 1, 2)
    return t.reshape(BC_ROWS, N_STATE)


def _bc_unview(name, t):
    t = t.reshape(N_GROUP, SSM_GC, N_STATE)
    if name.startswith("ssm_b_"):
        t = jnp.swapaxes(t, 1, 2)
    return t[None]


def _pack_bc(vals):
    return jnp.concatenate([_bc_view(n, vals[n]).astype(_F32) for n in BC_NAMES], axis=0)


def _unpack_bc(buf):
    return {n: _bc_unview(n, buf[j * BC_ROWS:(j + 1) * BC_ROWS]) for j, n in enumerate(BC_NAMES)}


def _small_layout(shapes):
    off, rows = {}, 0
    for n in VEC_NAMES:
        off[n] = rows
        rows += -(-math.prod(shapes[n]) // (PIECE_ROWS * LANES)) * PIECE_ROWS
    return off, rows, rows + PIECE_ROWS


def _pack_small(vals, shapes, last=None):
    parts = []
    for n in VEC_NAMES:
        flat = vals[n].reshape(-1).astype(_F32)
        pad = -flat.shape[0] % (PIECE_ROWS * LANES)
        if pad:
            flat = jnp.concatenate([flat, jnp.zeros((pad,), _F32)])
        parts.append(flat.reshape(-1, LANES))
    parts.append(jnp.zeros((PIECE_ROWS, LANES), _F32) if last is None else last)
    return jnp.concatenate(parts, axis=0)


def _unpack_small(buf, shapes):
    off, _, _ = _small_layout(shapes)
    out = {}
    for n in VEC_NAMES:
        size = math.prod(shapes[n])
        rows = -(-size // LANES)
        out[n] = buf[off[n]:off[n] + rows].reshape(-1)[:size].reshape(shapes[n])
    return out


WEIGHT_NAMES = ('norm_mix_pre', 'norm_mix_post', 'w_in', 'ret_gn_gain', 'ssm_lambda_re', 'ssm_lambda_im', 'ssm_log_dt',
                'ssm_b_re', 'ssm_b_im', 'ssm_c_re', 'ssm_c_im', 'ssm_d', 'w_glu', 'w_out', 'norm_mlp_pre',
                'norm_mlp_post', 'w_ff1', 'w_ff2')


def kernel(x, norm_mix_pre, norm_mix_post, w_in, ret_gn_gain, ssm_lambda_re, ssm_lambda_im, ssm_log_dt, ssm_b_re, ssm_b_im, ssm_c_re, ssm_c_im, ssm_d, w_glu, w_out, norm_mlp_pre, norm_mlp_post, w_ff1, w_ff2, loss_target, m_norm_mix_pre, m_norm_mix_post, m_w_in, m_ret_gn_gain, m_ssm_lambda_re, m_ssm_lambda_im, m_ssm_log_dt, m_ssm_b_re, m_ssm_b_im, m_ssm_c_re, m_ssm_c_im, m_ssm_d, m_w_glu, m_w_out, m_norm_mlp_pre, m_norm_mlp_post, m_w_ff1, m_w_ff2, v_norm_mix_pre, v_norm_mix_post, v_w_in, v_ret_gn_gain, v_ssm_lambda_re, v_ssm_lambda_im, v_ssm_log_dt, v_ssm_b_re, v_ssm_b_im, v_ssm_c_re, v_ssm_c_im, v_ssm_d, v_w_glu, v_w_out, v_norm_mlp_pre, v_norm_mlp_post, v_w_ff1, v_w_ff2):
    args = dict(locals())
    w = {n: args[n] for n in WEIGHT_NAMES}
    m = {n: args["m_" + n] for n in WEIGHT_NAMES}
    v = {n: args["v_" + n] for n in WEIGHT_NAMES}
    L = x.shape[1]
    tm = min(256, L)
    tk = min(2048, L)
    tb = min(1024, L)

    calls = {"in": ("w_in",), "rest": WEIGHT_GROUPS["mix"] + WEIGHT_GROUPS["mlp"]}
    started, zero = {}, jnp.zeros((), _F32)
    for call, names in calls.items():
        blocks = [(w[n][0].T if n == "w_in" else w[n][0]).astype(_BF) for n in names]
        blocks[0] = blocks[0] + zero.astype(_BF)
        started[call] = _split_start(blocks, [_landing(b) for b in blocks], True, "weights_start_" + call)
        zero = started[call][4][0, 0]

    def weights(group, after):
        names = WEIGHT_GROUPS[group]
        call = "in" if group == "in" else "rest"
        first = calls[call].index(names[0])
        part = slice(first, first + len(names))
        got = started[call]
        landed = _split_wait(got[0], got[1], got[2][part], got[3][part], after, True, "weights_wait_" + group, first=first)
        return [_weight_from_slots(n, g) for n, g in zip(names, landed)]

    in_flight = []

    def emit(dws):
        names = sorted(dws)
        srcs = [_grad_slots(n, dws[n]) for n in names]
        lands = [_landing(lax.dynamic_index_in_dim(t, _my_index(), 0, keepdims=False)) for t in srcs]
        started = _split_start(srcs, lands, False, "grads_start_" + "_".join(names))
        in_flight.append((names, started))
        return (started[4],)

    shapes = {n: w[n].shape for n in SMALL_NAMES}
    first_piece = {SMALL_NAMES[0]: jnp.zeros(shapes[SMALL_NAMES[0]], _F32)}
    small_flight = []

    def emit_small(gs, sq):
        loss_rows = jnp.broadcast_to(0.5 / D_MODEL * jnp.sum(sq), (PIECE_ROWS, LANES)).astype(_F32)
        bufs = [_pack_small({**first_piece, **gs}, shapes, loss_rows), _pack_bc(gs)]
        small_flight.append(_split_start(bufs, [_landing(b) for b in bufs], True, "small_grads_start"))
        return (small_flight[0][4],)

    small_w = {n: w[n] for n in SMALL_NAMES}
    gx, dg1 = _local_grads(x[0], loss_target[0], small_w, weights, emit, emit_small, tm, tk, tb, zero=zero)
    last_buf = dg1.reshape(PIECE_ROWS, LANES)
    last_started = _split_start([last_buf], [_landing(last_buf)], True, "last_grad_start")

    grads, delta, new_m, new_v = {}, {}, {}, {}
    after = last_started[4]
    for names, started in in_flight:
        landed = _split_wait(*started[:4], after, False, "grads_wait_" + "_".join(names))
        for n, parts in zip(names, landed):
            flip = (lambda t: t.T) if n == "w_in" else (lambda t: t)
            res = _sum_adamw(parts, flip(w[n][0]), flip(m[n][0]), flip(v[n][0]), math.gcd(256, parts.shape[1]), "adamw_" + n)
            grads[n], delta[n], new_m[n], new_v[n] = (flip(t)[None] for t in res)
        after = res[1]
    small_parts, bc_parts = _split_wait(*small_flight[0][:4], after, True, "small_grads_wait")
    last_parts = _split_wait(*last_started[:4], small_parts, True, "last_grad_wait")[0]
    small_parts = lax.dynamic_update_slice(small_parts, last_parts, (0, 0, 0))
    res_bc = _sum_adamw(bc_parts, _pack_bc(w), _pack_bc(m), _pack_bc(v), BC_ROWS, "adamw_bc")
    sw, sm, sv = _pack_small(w, shapes), _pack_small(m, shapes), _pack_small(v, shapes)
    res = _sum_adamw(small_parts, sw, sm, sv, sw.shape[0], "adamw_small")
    for dst, buf, buf_bc in zip((grads, delta, new_m, new_v), res, res_bc):
        dst.update(_unpack_small(buf, shapes))
        dst.update(_unpack_bc(buf_bc))
    _, loss_at, _ = _small_layout(shapes)
    loss = res[0][loss_at, 0]

    return (loss, gx[None], *[grads[n] for n in WEIGHT_NAMES], *[delta[n] for n in WEIGHT_NAMES],
            *[new_m[n] for n in WEIGHT_NAMES], *[new_v[n] for n in WEIGHT_NAMES])
```

```python
import math

import jax
import jax.numpy as jnp
from jax import lax
from jax.experimental import pallas as pl
from jax.experimental.pallas import tpu as pltpu

_BF = jnp.bfloat16
_F32 = jnp.float32

D_MODEL = 1024
RET_W = 512
N_HEAD = 4
HEAD_D = 128
CHUNK = 256
ROPE_CHUNK = 128
SSM_W = 512
SSM_GC = 16
N_GROUP = 32
N_STATE = 64
GROUPS_PER_KB = 8
N_KB = 4
KB_STATES = GROUPS_PER_KB * N_STATE
D_FF = 4096
IN_COLS = 2560
NORM_EPS = 1e-6
ROPE_BASE = 10000.0
N_DEV = 8

ADAM_LR = 0.001
ADAM_B1 = 0.9
ADAM_B2 = 0.999
ADAM_EPS = 1e-08
ADAM_WD = 0.01
ADAM_STEP = 10

SUBLANES = 8
LANES = 128
VMEM_LIMIT = 52 * 1024 * 1024
RET_STEP_CHUNKS = 2
KB_PER_STEP = 2
SCAN_UNROLL = True
FIX_UNROLL = 8

MESH = pl.DeviceIdType.MESH


def _params(*sem):
    return pltpu.CompilerParams(dimension_semantics=sem, vmem_limit_bytes=VMEM_LIMIT)


def _dot(a, b):
    return jnp.dot(a, b, preferred_element_type=_F32)


def _dot_nt(a, b):
    return lax.dot_general(a, b, (((1,), (1,)), ((), ())), preferred_element_type=_F32)


def _dot_tn(a, b):
    return lax.dot_general(a, b, (((0,), (0,)), ((), ())), preferred_element_type=_F32)


def _rms_r(z):
    return lax.rsqrt(jnp.mean(z * z, axis=-1, keepdims=True) + NORM_EPS)


def _rms_bwd(z, g, dn):
    r = _rms_r(z)
    t = dn * g
    dz = r * t - z * (r * r * r * jnp.mean(t * z, axis=-1, keepdims=True))
    return dz, dn * z * r


def _rope(t, cs, sn):
    return t * cs + pltpu.roll(t, HEAD_D // 2, 1) * sn


def _rope_t(t, cs, sn):
    return t * cs - pltpu.roll(t, HEAD_D // 2, 1) * sn


def _sigmoid(z):
    return 1.0 / (1.0 + jnp.exp(-z))


_GELU_C = math.sqrt(2.0 / math.pi)


def _gelu(z):
    return 0.5 * z * (1.0 + jnp.tanh(_GELU_C * (z + 0.044715 * z * z * z)))


def _gelu_grad(z):
    th = jnp.tanh(_GELU_C * (z + 0.044715 * z * z * z))
    return 0.5 * (1.0 + th) + 0.5 * z * (1.0 - th * th) * _GELU_C * (1.0 + 3 * 0.044715 * z * z)


ROW_CHUNK = 256


def _row_chunks(tm):
    return [pl.ds(i, min(ROW_CHUNK, tm)) for i in range(0, tm, ROW_CHUNK)]


def _ordered(body, in_specs, operands, after):
    k = len(after)
    if not k:
        return body, list(in_specs), tuple(operands)
    return ((lambda *refs: body(*refs[k:])), [pl.BlockSpec(memory_space=pl.ANY)] * k + list(in_specs),
            tuple(after) + tuple(operands))


def _row_spec(tm, n):
    return pl.BlockSpec((tm, n), lambda i: (i, 0))


def _full_spec(shape):
    nd = len(shape)
    return pl.BlockSpec(shape, lambda *_: (0,) * nd)


def _weight_spec(shape):
    nd = len(shape)
    return pl.BlockSpec(shape, lambda *_: (0,) * nd, pipeline_mode=pl.Buffered(1))


def _rope_tables(L):
    half = HEAD_D // 2
    inv_freq = ROPE_BASE ** (-jnp.arange(half, dtype=_F32) / half)
    twice = lambda t: jnp.concatenate([t, t], axis=-1)
    off = jnp.arange(ROPE_CHUNK, dtype=_F32)[:, None] * inv_freq[None, :]
    start = (ROPE_CHUNK * jnp.arange(L // ROPE_CHUNK, dtype=_F32))[:, None] * inv_freq[None, :]
    return (twice(jnp.cos(off)), twice(jnp.sin(off)),
            twice(jnp.cos(start))[:, None, :], twice(jnp.sin(start))[:, None, :])


def _prenorm(x, g, tm, after=()):
    L = x.shape[0]

    def body(x_ref, g_ref, h_ref):
        xv = x_ref[...]
        h_ref[...] = (xv * _rms_r(xv) * g_ref[...]).astype(_BF)

    body, in_specs, operands = _ordered(body, [_row_spec(tm, D_MODEL), _full_spec((1, D_MODEL))], (x, g), after)
    return pl.pallas_call(
        body, name="prenorm", grid=(L // tm,),
        in_specs=in_specs, out_specs=_row_spec(tm, D_MODEL),
        out_shape=jax.ShapeDtypeStruct((L, D_MODEL), _BF),
        compiler_params=_params("parallel"),
    )(*operands)


def _inproj_fwd(h, w_in_t, rope, tm):
    L = h.shape[0]
    n_chunks = tm // ROPE_CHUNK

    def body(h_ref, w_ref, co_ref, so_ref, cs_ref, ss_ref, q_ref, k_ref, v_ref, gate_ref, u_ref, cos_ref, sin_ref):
        proj = _dot_nt(h_ref[...], w_ref[...])
        lane = lax.broadcasted_iota(jnp.int32, (ROPE_CHUNK, HEAD_D), 1)
        sign = jnp.where(lane < HEAD_D // 2, -1.0, 1.0)
        co, so = co_ref[...], so_ref[...]
        for c in range(n_chunks):
            chunk = pl.program_id(0) * n_chunks + c
            cst, sst = cs_ref[chunk], ss_ref[chunk]
            rows = slice(c * ROPE_CHUNK, (c + 1) * ROPE_CHUNK)
            cs = co * cst - so * sst
            sn = (so * cst + co * sst) * sign
            cos_ref[rows, :] = cs
            sin_ref[rows, :] = sn
            for hh in range(N_HEAD):
                lo = hh * HEAD_D
                q_ref[rows, lo:lo + HEAD_D] = _rope(proj[rows, lo:lo + HEAD_D], cs, sn).astype(_BF)
                kh = _rope(proj[rows, RET_W + lo:RET_W + lo + HEAD_D], cs, sn) * (HEAD_D ** -0.5)
                k_ref[rows, lo:lo + HEAD_D] = kh.astype(_BF)
        v_ref[...] = proj[:, 2 * RET_W:3 * RET_W].astype(_BF)
        gate_ref[...] = proj[:, 3 * RET_W:4 * RET_W]
        u_ref[...] = proj[:, 4 * RET_W:]

    nc = L // ROPE_CHUNK
    return pl.pallas_call(
        body, name="inproj_fwd", grid=(L // tm,),
        in_specs=[_row_spec(tm, D_MODEL), _weight_spec((IN_COLS, D_MODEL)),
                  _full_spec((ROPE_CHUNK, HEAD_D)), _full_spec((ROPE_CHUNK, HEAD_D)),
                  _full_spec((nc, 1, HEAD_D)), _full_spec((nc, 1, HEAD_D))],
        out_specs=[_row_spec(tm, RET_W)] * 5 + [_row_spec(tm, HEAD_D)] * 2,
        out_shape=[jax.ShapeDtypeStruct((L, RET_W), _BF)] * 3 + [jax.ShapeDtypeStruct((L, RET_W), _F32)] * 2
        + [jax.ShapeDtypeStruct((L, HEAD_D), _F32)] * 2,
        compiler_params=_params("parallel"),
    )(h, w_in_t, *rope)


def _ret_consts():
    lg = jnp.log(1.0 - jnp.exp(jnp.linspace(math.log(1.0 / 32), math.log(1.0 / 512), N_HEAD))).astype(_F32)
    idx = jnp.arange(CHUNK, dtype=_F32)
    diff = idx[:, None] - idx[None, :]
    decay = jnp.where(diff[None] >= 0, jnp.exp(jnp.maximum(diff, 0.0)[None] * lg[:, None, None]), 0.0)
    zeta = jnp.exp((CHUNK - 1 - idx)[None, :] * lg[:, None])
    xi = jnp.exp((idx + 1.0)[None, :] * lg[:, None])
    gc = jnp.exp(CHUNK * lg)
    wide = lambda t: jnp.broadcast_to(t[:, :, None], (N_HEAD, CHUNK, HEAD_D)).astype(_F32)
    gcw = jnp.broadcast_to(gc[:, None, None], (N_HEAD, SUBLANES, HEAD_D)).astype(_F32)
    return decay.astype(_F32), wide(xi), wide(zeta), gcw


def _head_specs():
    wide = _full_spec((N_HEAD, CHUNK, HEAD_D))
    return [_full_spec((N_HEAD, CHUNK, CHUNK)), wide, wide, _full_spec((N_HEAD, SUBLANES, HEAD_D))]


def _retention_fwd(q, k, v, gate, ggn, consts):
    L = q.shape[0]
    nc = L // CHUNK
    cps = math.gcd(RET_STEP_CHUNKS, nc)
    blk = pl.BlockSpec((cps * CHUNK, RET_W), lambda n: (n, 0))

    def body(q_ref, k_ref, v_ref, gate_ref, ggn_ref, dm_ref, xi_ref, zeta_ref, gc_ref,
             o_ref, y_ref, rp_ref, r_scr):
        @pl.when(pl.program_id(0) == 0)
        def _():
            r_scr[...] = jnp.zeros_like(r_scr)

        for hh in range(N_HEAD):
            cols = slice(hh * HEAD_D, (hh + 1) * HEAD_D)
            state = r_scr[hh]
            for c in range(cps):
                rows = slice(c * CHUNK, (c + 1) * CHUNK)
                qv, kv, vv = q_ref[rows, cols], k_ref[rows, cols], v_ref[rows, cols]
                s = _dot_nt(qv, kv) * dm_ref[hh]
                o = _dot(s.astype(_BF), vv) + _dot(qv, state.astype(_BF)) * xi_ref[hh]
                o_ref[rows, cols] = o
                rp_ref[hh, c] = state
                vz = (vv.astype(_F32) * zeta_ref[hh]).astype(_BF)
                state = gc_ref[hh, 0:1, :] * state + _dot_tn(kv, vz)
                dlt = o - jnp.mean(o, axis=-1, keepdims=True)
                on = dlt * lax.rsqrt(jnp.mean(dlt * dlt, axis=-1, keepdims=True) + NORM_EPS)
                gt = gate_ref[rows, cols]
                y_ref[rows, cols] = (gt * _sigmoid(gt) * (on * ggn_ref[:, cols])).astype(_BF)
            r_scr[hh] = state

    return pl.pallas_call(
        body, name="retention_fwd", grid=(nc // cps,),
        in_specs=[blk, blk, blk, blk, _full_spec((1, RET_W))] + _head_specs(),
        out_specs=[blk, blk, pl.BlockSpec((N_HEAD, cps, HEAD_D, HEAD_D), lambda n: (0, n, 0, 0))],
        out_shape=[jax.ShapeDtypeStruct((L, RET_W), _F32), jax.ShapeDtypeStruct((L, RET_W), _BF),
                   jax.ShapeDtypeStruct((N_HEAD, nc, HEAD_D, HEAD_D), _F32)],
        scratch_shapes=[pltpu.VMEM((N_HEAD, HEAD_D, HEAD_D), _F32)],
        compiler_params=_params("arbitrary"),
    )(q, k, v, gate, ggn, *consts)


def _rows_to_segments(dst_scr, src_ref, seg):
    for g in range(dst_scr.shape[0]):
        for j in range(SUBLANES):
            dst_scr[g, pl.ds(j, seg, stride=SUBLANES), :] = src_ref[pl.ds(j * seg, seg), g * LANES:(g + 1) * LANES]


def _segments_to_rows(dst_ref, src_scr, seg):
    for g in range(src_scr.shape[0]):
        for j in range(SUBLANES):
            dst_ref[pl.ds(j * seg, seg), g * LANES:(g + 1) * LANES] = src_scr[g, pl.ds(j, seg, stride=SUBLANES), :]


def _scan_segments(x_ref, tab_ref, pw_ref, carry_ref, seg, reverse, entry_ref=None, fwd_ref=None, fwd_entry_ref=None,
                   da_ref=None):
    G = x_ref.shape[0]
    W = KB_STATES
    re, im = pl.ds(0, W), pl.ds(W, W)
    row_id = lax.broadcasted_iota(jnp.int32, (SUBLANES, W), 0)
    edge_in = (row_id == SUBLANES - 1) if reverse else (row_id == 0)
    edge_out = 0 if reverse else SUBLANES - 1
    a_tab = [(tab_ref[g, 0], tab_ref[g, 1]) for g in range(G)]

    def local(i, st):
        r = (seg - 1 - i) if reverse else i
        out = []
        for g in range(G):
            (ar, ai), (sr, si) = a_tab[g], st[g]
            nr = ar * sr - ai * si + x_ref[g, r, :, re]
            ni = ar * si + ai * sr + x_ref[g, r, :, im]
            x_ref[g, r, :, re] = nr
            x_ref[g, r, :, im] = ni
            out.append((nr, ni))
        return tuple(out)

    zero = jnp.zeros((SUBLANES, W), _F32)
    ends = lax.fori_loop(0, seg, local, tuple((zero, zero) for _ in range(G)), unroll=SCAN_UNROLL)

    entry = []
    shift = (SUBLANES - 1) if reverse else 1
    for g in range(G):
        er, ei = ends[g]
        fr = jnp.where(edge_in, carry_ref[g, :, re], pltpu.roll(er, shift, 0))
        fi = jnp.where(edge_in, carry_ref[g, :, im], pltpu.roll(ei, shift, 0))
        for j, dist in enumerate((1, 2, 4)):
            pr, pi = tab_ref[g, 2 + 2 * j], tab_ref[g, 3 + 2 * j]
            sh = (SUBLANES - dist) if reverse else dist
            sr, si = pltpu.roll(fr, sh, 0), pltpu.roll(fi, sh, 0)
            fr, fi = fr + pr * sr - pi * si, fi + pr * si + pi * sr
        br, bi = tab_ref[g, 8], tab_ref[g, 9]
        outr = br * fr - bi * fi + er
        outi = br * fi + bi * fr + ei
        carry_ref[g, :, re] = jnp.broadcast_to(outr[edge_out:edge_out + 1, :], (SUBLANES, W))
        carry_ref[g, :, im] = jnp.broadcast_to(outi[edge_out:edge_out + 1, :], (SUBLANES, W))
        entry.append((fr, fi))
        if entry_ref is not None:
            entry_ref[g, :, re] = fr
            entry_ref[g, :, im] = fi

    add_da = da_ref is not None

    def fix(r, st, first=False):
        out = []
        for g in range(G):
            fr, fi = entry[g]
            pwr, pwi = pw_ref[g, r, :, re], pw_ref[g, r, :, im]
            xr = x_ref[g, r, :, re] + (pwr * fr - pwi * fi)
            xi = x_ref[g, r, :, im] + (pwr * fi + pwi * fr)
            x_ref[g, r, :, re] = xr
            x_ref[g, r, :, im] = xi
            if add_da:
                prev = fwd_entry_ref.at[g] if first else fwd_ref.at[g, r - 1]
                xpr, xpi = prev[:, re], prev[:, im]
                out.append((st[g][0] + (xr * xpr + xi * xpi), st[g][1] + (xi * xpr - xr * xpi)))
            else:
                out.append(st[g])
        return tuple(out)

    if add_da:
        st = fix(0, tuple((zero, zero) for _ in range(G)), first=True)
        st = lax.fori_loop(1, seg, fix, st, unroll=SCAN_UNROLL)
        for g in range(G):
            da_ref[g, :, re] += st[g][0]
            da_ref[g, :, im] += st[g][1]
    else:
        lax.fori_loop(0, seg, fix, tuple((zero[0:1, 0:LANES],) for _ in range(G)), unroll=FIX_UNROLL)


def _s5_specs(seg, time=lambda t: t):
    G = KB_PER_STEP
    return dict(
        x=pl.BlockSpec((G, seg, SUBLANES, 2 * KB_STATES), lambda kb, t: (kb, time(t), 0, 0)),
        ent=pl.BlockSpec((G, 1, SUBLANES, 2 * KB_STATES), lambda kb, t: (kb, time(t), 0, 0)),
        b=pl.BlockSpec((G, LANES, 2 * KB_STATES), lambda kb, t: (kb, 0, 0)),
        c=pl.BlockSpec((G, 2 * KB_STATES, LANES), lambda kb, t: (kb, 0, 0)),
        tab=pl.BlockSpec((G, 10, SUBLANES, KB_STATES), lambda kb, t: (kb, 0, 0, 0)),
        pw=pl.BlockSpec((G, seg, 1, 2 * KB_STATES), lambda kb, t: (kb, 0, 0, 0)),
        d=pl.BlockSpec((1, G * LANES), lambda kb, t: (0, kb)),
    )


def _s5_fwd(u, bmat, cmat, tab_f, pw_f, d_skip, tb):
    L = u.shape[0]
    nt = L // tb
    seg = tb // SUBLANES
    G = KB_PER_STEP
    ucol = pl.BlockSpec((tb, G * LANES), lambda kb, t: (t, kb))
    sp = _s5_specs(seg)

    def body(u_ref, b_ref, c_ref, tab_ref, pw_ref, d_ref, s_ref, x_ref, ent_ref, up_scr, y_scr, carry_scr):
        @pl.when(pl.program_id(1) == 0)
        def _():
            carry_scr[...] = jnp.zeros_like(carry_scr)

        _rows_to_segments(up_scr, u_ref, seg)
        for g in range(G):
            x_ref[g] = _dot(up_scr[g].astype(_BF), b_ref[g]).reshape(seg, SUBLANES, 2 * KB_STATES)
        _scan_segments(x_ref, tab_ref, pw_ref, carry_scr, seg, reverse=False, entry_ref=ent_ref.at[:, 0])
        for g in range(G):
            y = _dot(x_ref[g].reshape(tb, 2 * KB_STATES).astype(_BF), c_ref[g])
            y_scr[g] = y + d_ref[:, g * LANES:(g + 1) * LANES] * up_scr[g]
        _segments_to_rows(s_ref, y_scr, seg)

    return pl.pallas_call(
        body, name="s5_fwd", grid=(N_KB // G, nt),
        in_specs=[ucol, sp["b"], sp["c"], sp["tab"], sp["pw"], sp["d"]],
        out_specs=[ucol, sp["x"], sp["ent"]],
        out_shape=[jax.ShapeDtypeStruct((L, SSM_W), _F32),
                   jax.ShapeDtypeStruct((N_KB, L // SUBLANES, SUBLANES, 2 * KB_STATES), _F32),
                   jax.ShapeDtypeStruct((N_KB, nt, SUBLANES, 2 * KB_STATES), _F32)],
        scratch_shapes=[pltpu.VMEM((G, tb, LANES), _F32)] * 2 + [pltpu.VMEM((G, SUBLANES, 2 * KB_STATES), _F32)],
        compiler_params=_params("parallel", "arbitrary"),
    )(u, bmat, cmat, tab_f, pw_f, d_skip)


def _mixout_fwd(s, y_ret, x, w_glu, w_out, g2, tm):
    L = s.shape[0]

    def body(s_ref, yr_ref, x_ref, wg_ref, wo_ref, g_ref, ys_ref, cat_ref, mix_ref, x2_ref):
        for rows in _row_chunks(tm):
            ys = _gelu(s_ref[rows, :]).astype(_BF)
            ys_ref[rows, :] = ys
            glu = _dot(ys, wg_ref[...])
            cat_ref[rows, :RET_W] = yr_ref[rows, :]
            cat_ref[rows, RET_W:] = (glu[:, :SSM_W] * _sigmoid(glu[:, SSM_W:])).astype(_BF)
            mix = _dot(cat_ref[rows, :], wo_ref[...])
            mix_ref[rows, :] = mix
            x2_ref[rows, :] = x_ref[rows, :] + mix * _rms_r(mix) * g_ref[...]

    return pl.pallas_call(
        body, name="mixout_fwd", grid=(L // tm,),
        in_specs=[_row_spec(tm, SSM_W), _row_spec(tm, RET_W), _row_spec(tm, D_MODEL),
                  _weight_spec((SSM_W, 2 * SSM_W)), _weight_spec((D_MODEL, D_MODEL)), _full_spec((1, D_MODEL))],
        out_specs=[_row_spec(tm, SSM_W), _row_spec(tm, D_MODEL), _row_spec(tm, D_MODEL), _row_spec(tm, D_MODEL)],
        out_shape=[jax.ShapeDtypeStruct((L, SSM_W), _BF), jax.ShapeDtypeStruct((L, D_MODEL), _BF),
                   jax.ShapeDtypeStruct((L, D_MODEL), _F32), jax.ShapeDtypeStruct((L, D_MODEL), _F32)],
        compiler_params=_params("parallel"),
    )(s, y_ret, x, w_glu, w_out, g2)


FF1_COLS = D_FF // N_DEV


def _ff1_fwd(x2, g3, w1, tm):
    L = x2.shape[0]

    def body(x_ref, g_ref, w_ref, h_ref, a_ref):
        for rows in _row_chunks(tm):
            xv = x_ref[rows, :]
            h = (xv * _rms_r(xv) * g_ref[...]).astype(_BF)
            h_ref[rows, :] = h
            for j in range(N_DEV):
                cols = slice(j * FF1_COLS, (j + 1) * FF1_COLS)
                rl = jnp.maximum(_dot(h, w_ref[j]), 0.0)
                a_ref[rows, cols] = (rl * rl).astype(_BF)

    return pl.pallas_call(
        body, name="ff1_fwd", grid=(L // tm,),
        in_specs=[_row_spec(tm, D_MODEL), _full_spec((1, D_MODEL)), _weight_spec((N_DEV, D_MODEL, FF1_COLS))],
        out_specs=[_row_spec(tm, D_MODEL), _row_spec(tm, D_FF)],
        out_shape=[jax.ShapeDtypeStruct((L, D_MODEL), _BF), jax.ShapeDtypeStruct((L, D_FF), _BF)],
        compiler_params=_params("parallel"),
    )(x2, g3, w1)


def _ff2_loss(act, x2, tgt, g4, w2, tm):
    L = act.shape[0]

    def body(f_ref, x_ref, t_ref, g_ref, w_ref, dy_ref, dm_ref, dg_ref, ls_ref):
        @pl.when(pl.program_id(0) == 0)
        def _():
            dg_ref[...] = jnp.zeros_like(dg_ref)
            ls_ref[...] = jnp.zeros_like(ls_ref)

        g = g_ref[...]
        for rows in _row_chunks(tm):
            m = _dot(f_ref[rows, :], w_ref[...])
            y = x_ref[rows, :] + m * _rms_r(m) * g
            err = y - t_ref[rows, :]
            ls_ref[...] += jnp.sum(err * err, axis=0, keepdims=True)
            dy = err * (1.0 / D_MODEL)
            dy_ref[rows, :] = dy
            dm, dgr = _rms_bwd(m, g, dy)
            dm_ref[rows, :] = dm.astype(_BF)
            dg_ref[...] += jnp.sum(dgr, axis=0, keepdims=True)

    return pl.pallas_call(
        body, name="ff2_loss", grid=(L // tm,),
        in_specs=[_row_spec(tm, D_FF), _row_spec(tm, D_MODEL), _row_spec(tm, D_MODEL),
                  _full_spec((1, D_MODEL)), _weight_spec((D_FF, D_MODEL))],
        out_specs=[_row_spec(tm, D_MODEL), _row_spec(tm, D_MODEL), _full_spec((1, D_MODEL)), _full_spec((1, D_MODEL))],
        out_shape=[jax.ShapeDtypeStruct((L, D_MODEL), _F32), jax.ShapeDtypeStruct((L, D_MODEL), _BF),
                   jax.ShapeDtypeStruct((1, D_MODEL), _F32), jax.ShapeDtypeStruct((1, D_MODEL), _F32)],
        compiler_params=_params("arbitrary"),
    )(act, x2, tgt, g4, w2)


def _ff2_bwd(dm, act, w2, tm, tn):
    L = dm.shape[0]
    last = L // tm - 1

    def body(dm_ref, a_ref, w_ref, df_ref, dw_ref, acc):
        @pl.when(pl.program_id(1) == 0)
        def _():
            acc[...] = jnp.zeros_like(acc)

        dmv = dm_ref[...]
        av = a_ref[...]
        df_ref[...] = (_dot_nt(dmv, w_ref[...]) * jnp.sqrt(4.0 * av.astype(_F32))).astype(_BF)
        acc[...] += _dot_tn(av, dmv)

        @pl.when(pl.program_id(1) == last)
        def _():
            dw_ref[...] = acc[...].astype(_BF)

    return pl.pallas_call(
        body, name="ff2_bwd", grid=(D_FF // tn, L // tm),
        in_specs=[pl.BlockSpec((tm, D_MODEL), lambda j, i: (i, 0)), pl.BlockSpec((tm, tn), lambda j, i: (i, j)),
                  pl.BlockSpec((tn, D_MODEL), lambda j, i: (j, 0))],
        out_specs=[pl.BlockSpec((tm, tn), lambda j, i: (i, j)), pl.BlockSpec((tn, D_MODEL), lambda j, i: (j, 0))],
        out_shape=[jax.ShapeDtypeStruct((L, D_FF), _BF), jax.ShapeDtypeStruct((D_FF, D_MODEL), _BF)],
        scratch_shapes=[pltpu.VMEM((tn, D_MODEL), _F32)],
        compiler_params=_params("parallel", "arbitrary"),
    )(dm, act, w2)


def _ff1_bwd(df1, w1, x2, mix, dy, g3, g2, tm):
    L = df1.shape[0]

    def body(df_ref, w_ref, x2_ref, mix_ref, dy_ref, g3_ref, g2_ref, dx2_ref, dmix_ref, dg3_ref, dg2_ref):
        @pl.when(pl.program_id(0) == 0)
        def _():
            dg3_ref[...] = jnp.zeros_like(dg3_ref)
            dg2_ref[...] = jnp.zeros_like(dg2_ref)

        for rows in _row_chunks(tm):
            dh = _dot_nt(df_ref[rows, 0:FF1_COLS], w_ref[0])
            for j in range(1, N_DEV):
                dh = dh + _dot_nt(df_ref[rows, j * FF1_COLS:(j + 1) * FF1_COLS], w_ref[j])
            dz, dgr = _rms_bwd(x2_ref[rows, :], g3_ref[...], dh)
            dg3_ref[...] += jnp.sum(dgr, axis=0, keepdims=True)
            dx2 = dy_ref[rows, :] + dz
            dx2_ref[rows, :] = dx2
            dmx, dgr2 = _rms_bwd(mix_ref[rows, :], g2_ref[...], dx2)
            dg2_ref[...] += jnp.sum(dgr2, axis=0, keepdims=True)
            dmix_ref[rows, :] = dmx.astype(_BF)

    vec = _full_spec((1, D_MODEL))
    return pl.pallas_call(
        body, name="ff1_bwd", grid=(L // tm,),
        in_specs=[_row_spec(tm, D_FF), _weight_spec((N_DEV, D_MODEL, FF1_COLS)), _row_spec(tm, D_MODEL),
                  _row_spec(tm, D_MODEL), _row_spec(tm, D_MODEL), vec, vec],
        out_specs=[_row_spec(tm, D_MODEL), _row_spec(tm, D_MODEL), vec, vec],
        out_shape=[jax.ShapeDtypeStruct((L, D_MODEL), _F32), jax.ShapeDtypeStruct((L, D_MODEL), _BF),
                   jax.ShapeDtypeStruct((1, D_MODEL), _F32), jax.ShapeDtypeStruct((1, D_MODEL), _F32)],
        compiler_params=_params("arbitrary"),
    )(df1, w1, x2, mix, dy, g3, g2)


def _matmul_tn(a, b, tm, tn, name, slots=False):
    L, K = a.shape
    N = b.shape[1]
    last = L // tm - 1

    def body(a_ref, b_ref, o_ref, acc):
        @pl.when(pl.program_id(1) == 0)
        def _():
            acc[...] = jnp.zeros_like(acc)

        acc[...] += _dot_tn(a_ref[...].astype(_BF), b_ref[...].astype(_BF))

        @pl.when(pl.program_id(1) == last)
        def _():
            if slots:
                o_ref[0] = acc[...].astype(_BF)
            else:
                o_ref[...] = acc[...].astype(_BF)

    if slots:
        out_spec = pl.BlockSpec((1, K, tn), lambda j, i: (j, 0, 0))
        out_shape = jax.ShapeDtypeStruct((N // tn, K, tn), _BF)
    else:
        out_spec = pl.BlockSpec((K, tn), lambda j, i: (0, j))
        out_shape = jax.ShapeDtypeStruct((K, N), _BF)
    return pl.pallas_call(
        body, name=name, grid=(N // tn, L // tm),
        in_specs=[pl.BlockSpec((tm, K), lambda j, i: (i, 0)), pl.BlockSpec((tm, tn), lambda j, i: (i, j))],
        out_specs=out_spec, out_shape=out_shape,
        scratch_shapes=[pltpu.VMEM((K, tn), _F32)],
        compiler_params=_params("parallel", "arbitrary"),
    )(a, b)


def _dw_in_t(pieces, h, tk):
    L = h.shape[0]
    last = L // tk - 1

    def body(p0, p1, p2, p3, p4, h_ref, o_ref, acc):
        @pl.when(pl.program_id(0) == 0)
        def _():
            acc[...] = jnp.zeros_like(acc)

        hv = h_ref[...]
        for j, p in enumerate((p0, p1, p2, p3, p4)):
            acc[j * RET_W:(j + 1) * RET_W, :] += _dot_tn(p[...].astype(_BF), hv)

        @pl.when(pl.program_id(0) == last)
        def _():
            o_ref[...] = acc[...].astype(_BF)

    return pl.pallas_call(
        body, name="dw_in", grid=(L // tk,),
        in_specs=[_row_spec(tk, RET_W)] * 5 + [_row_spec(tk, D_MODEL)],
        out_specs=_full_spec((IN_COLS, D_MODEL)), out_shape=jax.ShapeDtypeStruct((IN_COLS, D_MODEL), _BF),
        scratch_shapes=[pltpu.VMEM((IN_COLS, D_MODEL), _F32)],
        compiler_params=_params("arbitrary"),
    )(*pieces, h)


def _mixout_bwd(dmix, w_out, w_glu, ys, s, o, gate, ggn, tm, after=()):
    L = dmix.shape[0]

    def body(dmix_ref, wo_ref, wg_ref, ys_ref, s_ref, o_ref, gate_ref, ggn_ref,
             dglu_ref, ds_ref, dgate_ref, do_ref, dggn_ref):
        @pl.when(pl.program_id(0) == 0)
        def _():
            dggn_ref[...] = jnp.zeros_like(dggn_ref)

        ggn = ggn_ref[...]
        for rows in _row_chunks(tm):
            dcat = _dot_nt(dmix_ref[rows, :], wo_ref[...])
            dy_ret, dy_ssm = dcat[:, :RET_W], dcat[:, RET_W:]
            glu = _dot(ys_ref[rows, :], wg_ref[...])
            ga, sg = glu[:, :SSM_W], _sigmoid(glu[:, SSM_W:])
            dga = (dy_ssm * sg).astype(_BF)
            dgb = (dy_ssm * ga * sg * (1.0 - sg)).astype(_BF)
            dglu_ref[rows, :SSM_W] = dga
            dglu_ref[rows, SSM_W:] = dgb
            dys = _dot_nt(dga, wg_ref[:, :SSM_W]) + _dot_nt(dgb, wg_ref[:, SSM_W:])
            ds_ref[rows, :] = dys * _gelu_grad(s_ref[rows, :])
            gt = gate_ref[rows, :]
            sgt = _sigmoid(gt)
            for hh in range(N_HEAD):
                cols = slice(hh * HEAD_D, (hh + 1) * HEAD_D)
                ov = o_ref[rows, cols]
                dlt = ov - jnp.mean(ov, axis=-1, keepdims=True)
                rstd = lax.rsqrt(jnp.mean(dlt * dlt, axis=-1, keepdims=True) + NORM_EPS)
                on = dlt * rstd
                dyr = dy_ret[:, cols] * (gt[:, cols] * sgt[:, cols])
                dgate_ref[rows, cols] = dy_ret[:, cols] * (on * ggn[:, cols]) * (sgt[:, cols] * (1.0 + gt[:, cols] * (1.0 - sgt[:, cols])))
                dggn_ref[:, cols] += jnp.sum(dyr * on, axis=0, keepdims=True)
                don = dyr * ggn[:, cols]
                do = rstd * (don - jnp.mean(don, axis=-1, keepdims=True) - on * jnp.mean(don * on, axis=-1, keepdims=True))
                do_ref[rows, cols] = do.astype(_BF)

    body, in_specs, operands = _ordered(
        body, [_row_spec(tm, D_MODEL), _weight_spec((D_MODEL, D_MODEL)), _weight_spec((SSM_W, 2 * SSM_W)),
               _row_spec(tm, SSM_W), _row_spec(tm, SSM_W), _row_spec(tm, RET_W), _row_spec(tm, RET_W),
               _full_spec((1, RET_W))], (dmix, w_out, w_glu, ys, s, o, gate, ggn), after)
    return pl.pallas_call(
        body, name="mixout_bwd", grid=(L // tm,),
        in_specs=in_specs,
        out_specs=[_row_spec(tm, 2 * SSM_W), _row_spec(tm, SSM_W), _row_spec(tm, RET_W), _row_spec(tm, RET_W),
                   _full_spec((1, RET_W))],
        out_shape=[jax.ShapeDtypeStruct((L, 2 * SSM_W), _BF), jax.ShapeDtypeStruct((L, SSM_W), _F32),
                   jax.ShapeDtypeStruct((L, RET_W), _F32), jax.ShapeDtypeStruct((L, RET_W), _BF),
                   jax.ShapeDtypeStruct((1, RET_W), _F32)],
        compiler_params=_params("arbitrary"),
    )(*operands)


def _s5_bwd(u, ds, xs, ent, bmat, cmat, tab_r, pw_r, d_skip, tb, after=()):
    L = u.shape[0]
    nt = L // tb
    seg = tb // SUBLANES
    G = KB_PER_STEP
    rcol = pl.BlockSpec((tb, G * LANES), lambda kb, t: (nt - 1 - t, kb))
    sp = _s5_specs(seg, time=lambda t: nt - 1 - t)
    aspec = pl.BlockSpec((G, SUBLANES, 2 * KB_STATES), lambda kb, t: (kb, 0, 0))

    def body(u_ref, ds_ref, x_ref, ent_ref, b_ref, c_ref, tr_ref, pr_ref, d_ref,
             du_ref, db_ref, dc_ref, da_ref, dd_ref, up_scr, dp_scr, g_scr, lc_scr):
        @pl.when(pl.program_id(1) == 0)
        def _():
            lc_scr[...] = jnp.zeros_like(lc_scr)
            db_ref[...] = jnp.zeros_like(db_ref)
            dc_ref[...] = jnp.zeros_like(dc_ref)
            da_ref[...] = jnp.zeros_like(da_ref)
            dd_ref[...] = jnp.zeros_like(dd_ref)

        _rows_to_segments(up_scr, u_ref, seg)
        _rows_to_segments(dp_scr, ds_ref, seg)
        for g in range(G):
            g_scr[g] = _dot_nt(dp_scr[g].astype(_BF), c_ref[g]).reshape(seg, SUBLANES, 2 * KB_STATES)
        _scan_segments(g_scr, tr_ref, pr_ref, lc_scr, seg, reverse=True, fwd_ref=x_ref, fwd_entry_ref=ent_ref.at[:, 0],
                       da_ref=da_ref)
        for g in range(G):
            cols = slice(g * LANES, (g + 1) * LANES)
            uv, dsv = up_scr[g], dp_scr[g]
            ub, dsb = uv.astype(_BF), dsv.astype(_BF)
            lamb = g_scr[g].reshape(tb, 2 * KB_STATES).astype(_BF)
            db_ref[g] += _dot_tn(ub, lamb)
            dc_ref[g] += _dot_tn(dsb, x_ref[g].reshape(tb, 2 * KB_STATES).astype(_BF))
            dd_ref[:, cols] += jnp.sum(dsv * uv, axis=0, keepdims=True)
            up_scr[g] = _dot_nt(lamb, b_ref[g]) + d_ref[:, cols] * dsv
        _segments_to_rows(du_ref, up_scr, seg)

    body, in_specs, operands = _ordered(
        body, [rcol, rcol, sp["x"], sp["ent"], sp["b"], sp["c"], sp["tab"], sp["pw"], sp["d"]],
        (u, ds, xs, ent, bmat, cmat, tab_r, pw_r, d_skip), after)
    return pl.pallas_call(
        body, name="s5_bwd", grid=(N_KB // G, nt),
        in_specs=in_specs,
        out_specs=[rcol, sp["b"], sp["b"], aspec, sp["d"]],
        out_shape=[jax.ShapeDtypeStruct((L, SSM_W), _F32),
                   jax.ShapeDtypeStruct((N_KB, LANES, 2 * KB_STATES), _F32),
                   jax.ShapeDtypeStruct((N_KB, LANES, 2 * KB_STATES), _F32),
                   jax.ShapeDtypeStruct((N_KB, SUBLANES, 2 * KB_STATES), _F32),
                   jax.ShapeDtypeStruct((1, SSM_W), _F32)],
        scratch_shapes=[pltpu.VMEM((G, tb, LANES), _F32)] * 2
        + [pltpu.VMEM((G, seg, SUBLANES, 2 * KB_STATES), _F32), pltpu.VMEM((G, SUBLANES, 2 * KB_STATES), _F32)],
        compiler_params=_params("parallel", "arbitrary"),
    )(*operands)


def _retention_bwd(q, k, v, do, r_prev, consts, cosf, sinf, after=()):
    L = q.shape[0]
    nc = L // CHUNK
    cps = math.gcd(RET_STEP_CHUNKS, nc)
    nb = nc // cps
    blk = pl.BlockSpec((cps * CHUNK, RET_W), lambda n: (nb - 1 - n, 0))
    rope_blk = pl.BlockSpec((cps * CHUNK, HEAD_D), lambda n: (nb - 1 - n, 0))

    def body(q_ref, k_ref, v_ref, do_ref, rp_ref, dm_ref, xi_ref, zeta_ref, gc_ref, cos_ref, sin_ref,
             dq_ref, dk_ref, dv_ref, g_scr):
        @pl.when(pl.program_id(0) == 0)
        def _():
            g_scr[...] = jnp.zeros_like(g_scr)

        for hh in range(N_HEAD):
            cols = slice(hh * HEAD_D, (hh + 1) * HEAD_D)
            dm, zeta = dm_ref[hh], zeta_ref[hh]
            gst = g_scr[hh]
            for c in reversed(range(cps)):
                rows = slice(c * CHUNK, (c + 1) * CHUNK)
                qv, kv, vv, dov = q_ref[rows, cols], k_ref[rows, cols], v_ref[rows, cols], do_ref[rows, cols]
                rb = rp_ref[hh, c].astype(_BF)
                gb = gst.astype(_BF)
                sb = (_dot_nt(qv, kv) * dm).astype(_BF)
                dab = (_dot_nt(dov, vv) * dm).astype(_BF)
                dox = (dov.astype(_F32) * xi_ref[hh]).astype(_BF)
                vz = (vv.astype(_F32) * zeta).astype(_BF)
                dq = _dot(dab, kv) + _dot_nt(dox, rb)
                dk = _dot_tn(dab, qv) + _dot_nt(vz, gb)
                dv = _dot_tn(sb, dov) + _dot(kv, gb) * zeta
                gst = gc_ref[hh, 0:1, :] * gst + _dot_tn(qv, dox)
                cs, sn = cos_ref[rows, :], sin_ref[rows, :]
                dq_ref[rows, cols] = _rope_t(dq, cs, sn).astype(_BF)
                dk_ref[rows, cols] = (_rope_t(dk, cs, sn) * (HEAD_D ** -0.5)).astype(_BF)
                dv_ref[rows, cols] = dv.astype(_BF)
            g_scr[hh] = gst

    body, in_specs, operands = _ordered(
        body, [blk, blk, blk, blk, pl.BlockSpec((N_HEAD, cps, HEAD_D, HEAD_D), lambda n: (0, nb - 1 - n, 0, 0))]
        + _head_specs() + [rope_blk, rope_blk], (q, k, v, do, r_prev, *consts, cosf, sinf), after)
    return pl.pallas_call(
        body, name="retention_bwd", grid=(nb,),
        in_specs=in_specs,
        out_specs=[blk, blk, blk],
        out_shape=[jax.ShapeDtypeStruct((L, RET_W), _BF)] * 3,
        scratch_shapes=[pltpu.VMEM((N_HEAD, HEAD_D, HEAD_D), _F32)],
        compiler_params=_params("arbitrary"),
    )(*operands)


def _inproj_bwd(pieces, w_in_t, x, dx2, g1, tm, after=()):
    L = x.shape[0]

    def body(p0, p1, p2, p3, p4, w_ref, x_ref, dx2_ref, g_ref, dx_ref, dg_ref):
        @pl.when(pl.program_id(0) == 0)
        def _():
            dg_ref[...] = jnp.zeros_like(dg_ref)

        for rows in _row_chunks(tm):
            dh = None
            for j, p in enumerate((p0, p1, p2, p3, p4)):
                part = _dot(p[rows, :].astype(_BF), w_ref[j * RET_W:(j + 1) * RET_W, :])
                dh = part if dh is None else dh + part
            dz, dgr = _rms_bwd(x_ref[rows, :], g_ref[...], dh)
            dx_ref[rows, :] = dx2_ref[rows, :] + dz
            dg_ref[...] += jnp.sum(dgr, axis=0, keepdims=True)

    body, in_specs, operands = _ordered(
        body, [_row_spec(tm, RET_W)] * 5 + [_weight_spec((IN_COLS, D_MODEL)), _row_spec(tm, D_MODEL),
                                             _row_spec(tm, D_MODEL), _full_spec((1, D_MODEL))],
        (*pieces, w_in_t, x, dx2, g1), after)
    return pl.pallas_call(
        body, name="inproj_bwd", grid=(L // tm,),
        in_specs=in_specs,
        out_specs=[_row_spec(tm, D_MODEL), _full_spec((1, D_MODEL))],
        out_shape=[jax.ShapeDtypeStruct((L, D_MODEL), _F32), jax.ShapeDtypeStruct((1, D_MODEL), _F32)],
        compiler_params=_params("arbitrary"),
    )(*operands)


def _sum_adamw(parts, w, m, v, tr, name):
    _, R, Cc = parts.shape

    def body(p_ref, w_ref, m_ref, v_ref, g_ref, d_ref, nm_ref, nv_ref):
        gv = p_ref[0].astype(_F32)
        for s in range(1, N_DEV):
            gv = gv + p_ref[s].astype(_F32)
        g_ref[...] = gv
        nm = ADAM_B1 * m_ref[...] + (1.0 - ADAM_B1) * gv
        nv = ADAM_B2 * v_ref[...] + (1.0 - ADAM_B2) * (gv * gv)
        m_hat = nm / (1.0 - ADAM_B1 ** ADAM_STEP)
        v_hat = nv / (1.0 - ADAM_B2 ** ADAM_STEP)
        d_ref[...] = -ADAM_LR * (m_hat / (jnp.sqrt(v_hat) + ADAM_EPS) + ADAM_WD * w_ref[...])
        nm_ref[...] = nm
        nv_ref[...] = nv

    spec = _row_spec(tr, Cc)
    return pl.pallas_call(
        body, name=name, grid=(R // tr,),
        in_specs=[pl.BlockSpec((N_DEV, tr, Cc), lambda i: (0, i, 0))] + [spec] * 3, out_specs=[spec] * 4,
        out_shape=[jax.ShapeDtypeStruct((R, Cc), _F32)] * 4,
        compiler_params=_params("parallel"),
    )(parts, w, m, v)


def _my_place():
    return lax.axis_index("x"), lax.axis_index("y"), lax.axis_index("c")


HBM_SPEC = pl.BlockSpec(memory_space=pltpu.HBM)
SEM_SPEC = pl.BlockSpec(memory_space=pltpu.SEMAPHORE)
DATAFLOW = pltpu.SideEffectType.DATAFLOW_SIDE_EFFECTING


def _my_index():
    x, y, c = _my_place()
    return 4 * x + 2 * y + c


def _landing(own_block):
    zone = lax.empty((N_DEV,) + own_block.shape, own_block.dtype)
    return lax.dynamic_update_index_in_dim(zone, own_block, _my_index(), 0)


def _split_copies(src_refs, land_refs, send_sems, recv_sems, gather, first=0):
    x, y, c = _my_place()
    me = 4 * x + 2 * y + c
    copies = []
    for a, (src, land) in enumerate(zip(src_refs, land_refs)):
        for kk in range(1, N_DEV):
            px, py, pc = x ^ (kk >> 2), y ^ ((kk >> 1) & 1), c ^ (kk & 1)
            peer = 4 * px + 2 * py + pc
            copies.append(pltpu.make_async_remote_copy(
                src_ref=src if gather else src.at[peer], dst_ref=land.at[me],
                send_sem=send_sems.at[(first + a) * 7 + kk - 1], recv_sem=recv_sems.at[(first + a) * 7 + kk - 1],
                device_id=(px, py, pc), device_id_type=MESH))
    return copies


def _split_start(srcs, lands, gather, name):
    n = len(srcs)

    def body(*refs):
        src_refs, land_refs = refs[:n], refs[n:2 * n]
        send_sems, recv_sems = refs[2 * n], refs[2 * n + 1]
        token = refs[-1]
        for cp in _split_copies(src_refs, land_refs, send_sems, recv_sems, gather):
            cp.start()
        token[...] = jnp.zeros_like(token)

    outs = pl.pallas_call(
        body, name=name,
        out_shape=(pltpu.SemaphoreType.DMA((7 * n,)), pltpu.SemaphoreType.DMA((7 * n,)),
                   *[pltpu.HBM(t.shape, t.dtype) for t in srcs], *[pltpu.HBM(t.shape, t.dtype) for t in lands],
                   jax.ShapeDtypeStruct((SUBLANES, LANES), _F32)),
        in_specs=[HBM_SPEC] * (2 * n),
        out_specs=(SEM_SPEC, SEM_SPEC, *[HBM_SPEC] * (2 * n), pl.BlockSpec(memory_space=pltpu.VMEM)),
        input_output_aliases={i: 2 + i for i in range(2 * n)},
        compiler_params=pltpu.CompilerParams(has_side_effects=DATAFLOW),
    )(*[pltpu.with_memory_space_constraint(t, pltpu.HBM) for t in list(srcs) + list(lands)])
    return outs[0], outs[1], outs[2:2 + n], outs[2 + n:2 + 2 * n], outs[-1]


def _split_wait(send_sems, recv_sems, srcs, lands, after, gather, name, first=0):
    n = len(srcs)

    def body(*refs):
        src_refs, land_refs = refs[:n], refs[n:2 * n]
        send_s, recv_s = refs[2 * n], refs[2 * n + 1]
        for cp in _split_copies(src_refs, land_refs, send_s, recv_s, gather, first):
            cp.wait_send()
            cp.wait_recv()

    outs = pl.pallas_call(
        body, name=name,
        out_shape=tuple(pltpu.HBM(t.shape, t.dtype) for t in list(srcs) + list(lands)),
        in_specs=[HBM_SPEC] * (2 * n) + [SEM_SPEC, SEM_SPEC, pl.BlockSpec(memory_space=pl.ANY)],
        out_specs=tuple([HBM_SPEC] * (2 * n)),
        input_output_aliases={i: i for i in range(2 * n)},
        compiler_params=pltpu.CompilerParams(has_side_effects=DATAFLOW),
    )(*srcs, *lands, send_sems, recv_sems, after)
    return outs[n:]


def _discretize(lam_re, lam_im, log_dt, b_re, b_im):
    lr = jnp.minimum(lam_re, -1e-4)
    li = lam_im
    dt = jnp.exp(log_dt)[:, None]
    er = jnp.exp(lr * dt)
    ar, ai = er * jnp.cos(li * dt), er * jnp.sin(li * dt)
    den = lr * lr + li * li
    cr = ((ar - 1.0) * lr + ai * li) / den
    ci = (ai * lr - (ar - 1.0) * li) / den
    bbr = cr[:, :, None] * b_re - ci[:, :, None] * b_im
    bbi = cr[:, :, None] * b_im + ci[:, :, None] * b_re
    return ar, ai, bbr, bbi


def _cmul(ar, ai, br, bi):
    return ar * br - ai * bi, ar * bi + ai * br


def _cpowers(ar, ai, n):
    pr, pi = ar[None], ai[None]
    while pr.shape[0] < n:
        nr, ni = _cmul(pr, pi, pr[-1][None], pi[-1][None])
        pr, pi = jnp.concatenate([pr, nr]), jnp.concatenate([pi, ni])
    return pr[:n], pi[:n]


def _scan_tables(ar, ai, seg, reverse):
    if reverse:
        ai = -ai
    ar, ai = ar.reshape(N_KB, KB_STATES), ai.reshape(N_KB, KB_STATES)
    pr, pi = _cpowers(ar, ai, seg)
    a1 = (pr[-1], pi[-1])
    a2 = _cmul(*a1, *a1)
    a4 = _cmul(*a2, *a2)
    row = jnp.arange(SUBLANES)[None, :, None]
    wide = lambda t: jnp.broadcast_to(t[:, None, :], (N_KB, SUBLANES, KB_STATES))
    tabs = [wide(ar), wide(ai)]
    for dist, (qr, qi) in ((1, a1), (2, a2), (4, a4)):
        keep = (row < SUBLANES - dist) if reverse else (row >= dist)
        tabs += [jnp.where(keep, wide(qr), 0.0), jnp.where(keep, wide(qi), 0.0)]
    tabs += [wide(a1[0]), wide(a1[1])]
    if reverse:
        pr, pi = pr[::-1], pi[::-1]
    pw = jnp.transpose(jnp.concatenate([pr, pi], axis=-1), (1, 0, 2))[:, :, None, :]
    return jnp.stack(tabs, axis=1).astype(_F32), pw.astype(_F32)


def _block_diag_in(br, bi):
    eye = jnp.eye(GROUPS_PER_KB, dtype=_F32)
    one = lambda t: jnp.einsum("kgpc,gh->kgchp", t.reshape(N_KB, GROUPS_PER_KB, N_STATE, SSM_GC), eye).reshape(
        N_KB, LANES, KB_STATES)
    return jnp.concatenate([one(br), one(bi)], axis=-1)


def _block_diag_in_t(dmat):
    d6 = dmat.reshape(N_KB, GROUPS_PER_KB, SSM_GC, 2, GROUPS_PER_KB, N_STATE)
    eye = jnp.eye(GROUPS_PER_KB, dtype=_F32)
    both = jnp.einsum("kgcrhp,gh->rkgpc", d6, eye).reshape(2, N_GROUP, N_STATE, SSM_GC)
    return both[0], both[1]


def _block_diag_out(c_re, c_im):
    eye = jnp.eye(GROUPS_PER_KB, dtype=_F32)
    one = lambda t: jnp.einsum("kgcp,gh->khpgc", t.reshape(N_KB, GROUPS_PER_KB, SSM_GC, N_STATE), eye).reshape(
        N_KB, KB_STATES, LANES)
    return jnp.concatenate([one(c_re), -one(c_im)], axis=1)


def _block_diag_out_t(dmat_t):
    d6 = dmat_t.reshape(N_KB, GROUPS_PER_KB, SSM_GC, 2, GROUPS_PER_KB, N_STATE)
    eye = jnp.eye(GROUPS_PER_KB, dtype=_F32)
    both = jnp.einsum("kgcrhp,gh->rkgcp", d6, eye).reshape(2, N_GROUP, SSM_GC, N_STATE)
    return both[0], -both[1]


SMALL_NAMES = ("norm_mix_pre", "norm_mix_post", "ret_gn_gain", "ssm_lambda_re", "ssm_lambda_im", "ssm_log_dt",
               "ssm_b_re", "ssm_b_im", "ssm_c_re", "ssm_c_im", "ssm_d", "norm_mlp_pre", "norm_mlp_post")


def _local_grads(x, tgt, small, weights, emit, emit_small, tm, tk, tb, zero=0.0):
    L = x.shape[0]
    g1, g2, ggn = small["norm_mix_pre"], small["norm_mix_post"], small["ret_gn_gain"]
    g3, g4, d_skip = small["norm_mlp_pre"], small["norm_mlp_post"], small["ssm_d"]

    rope = _rope_tables(L)
    consts = _ret_consts()

    disc_in = (small["ssm_lambda_re"][0], small["ssm_lambda_im"][0], small["ssm_log_dt"][0] + zero,
               small["ssm_b_re"][0], small["ssm_b_im"][0])
    (ar, ai, bbr, bbi), disc_vjp = jax.vjp(_discretize, *disc_in)
    bmat = _block_diag_in(bbr, bbi).astype(_BF)
    cmat = _block_diag_out(small["ssm_c_re"][0], small["ssm_c_im"][0]).astype(_BF)
    seg = tb // SUBLANES
    tab_f, pw_f = _scan_tables(ar, ai, seg, False)
    tab_r, pw_r = _scan_tables(ar, ai, seg, True)

    h1 = _prenorm(x, g1, min(4 * tm, L), after=(pw_r,))
    (w_in_t,) = weights("in", h1)
    q, k, v, gate, u, cosf, sinf = _inproj_fwd(h1, w_in_t, rope, min(4 * tm, L))
    o, y_ret, r_prev = _retention_fwd(q, k, v, gate, ggn, consts)
    s, xs, ent = _s5_fwd(u, bmat, cmat, tab_f, pw_f, d_skip, tb)
    w_glu, w_out = weights("mix", s)
    ys, cat, mix, x2 = _mixout_fwd(s, y_ret, x, w_glu, w_out, g2, min(2 * tm, L))
    w_ff1, w_ff2 = weights("mlp", x2)
    h3, act = _ff1_fwd(x2, g3, w_ff1, min(2 * tm, L))
    dy, dm, dg4, sq = _ff2_loss(act, x2, tgt, g4, w_ff2, min(2 * tm, L))

    df1, dw_ff2 = _ff2_bwd(dm, act, w_ff2, min(1024, L), 1024)
    dx2, dmix, dg3, dg2 = _ff1_bwd(df1, w_ff1, x2, mix, dy, g3, g2, min(2 * tm, L))
    dw_ff1 = _matmul_tn(h3, df1, tk, FF1_COLS, "dw_ff1", slots=True)
    token = emit({"w_ff1": dw_ff1, "w_ff2": dw_ff2})
    dglu, ds, dgate, do, dggn = _mixout_bwd(dmix, w_out, w_glu, ys, s, o, gate, ggn, min(2 * tm, L), after=token)
    dw_out = _matmul_tn(cat, dmix, tk, 1024, "dw_out")
    dw_glu = _matmul_tn(ys, dglu, tk, 1024, "dw_glu")
    token = emit({"w_glu": dw_glu, "w_out": dw_out})
    du, dbmat, dcmat, da8, dd = _s5_bwd(u, ds, xs, ent, bmat, cmat, tab_r, pw_r, d_skip, tb, after=token)

    da = jnp.sum(da8, axis=1)
    dar = da[:, :KB_STATES].reshape(N_GROUP, N_STATE)
    dai = da[:, KB_STATES:].reshape(N_GROUP, N_STATE)
    dbr, dbi = _block_diag_in_t(dbmat)
    dlre, dlim, dldt, dbre, dbim = disc_vjp((dar, dai, dbr, dbi))
    dcre, dcim = _block_diag_out_t(dcmat)
    token = emit_small({
        "norm_mix_post": dg2, "ret_gn_gain": dggn,
        "ssm_lambda_re": dlre[None], "ssm_lambda_im": dlim[None], "ssm_log_dt": dldt[None],
        "ssm_b_re": dbre[None], "ssm_b_im": dbim[None], "ssm_c_re": dcre[None], "ssm_c_im": dcim[None],
        "ssm_d": dd, "norm_mlp_pre": dg3, "norm_mlp_post": dg4,
    }, sq)

    dq, dk, dv = _retention_bwd(q, k, v, do, r_prev, consts, cosf, sinf, after=token)
    pieces = (dq, dk, dv, dgate, du)
    dw_in_t = _dw_in_t(pieces, h1, min(1024, L))
    token = emit({"w_in": dw_in_t})
    gx, dg1 = _inproj_bwd(pieces, w_in_t, x, dx2, g1, min(2 * tm, L), after=token)
    return gx, dg1


BIG_SHAPES = {"w_in": (D_MODEL, IN_COLS // N_DEV), "w_glu": (SSM_W, 2 * SSM_W // N_DEV), "w_out": (D_MODEL // N_DEV, D_MODEL),
              "w_ff1": (D_MODEL, FF1_COLS), "w_ff2": (D_FF // N_DEV, D_MODEL)}
BIG_NAMES = ("w_in", "w_glu", "w_out", "w_ff1", "w_ff2")


def _cols_from_slots(g):
    return jnp.transpose(g, (1, 0, 2)).reshape(g.shape[1], N_DEV * g.shape[2])


def _cols_to_slots(dw):
    r, cols = dw.shape
    return jnp.transpose(dw.reshape(r, N_DEV, cols // N_DEV), (1, 0, 2))


WEIGHT_GROUPS = {"in": ("w_in",), "mix": ("w_glu", "w_out"), "mlp": ("w_ff1", "w_ff2")}


def _weight_from_slots(name, g):
    if name == "w_glu":
        return _cols_from_slots(g)
    if name == "w_ff1":
        return g
    return g.reshape(N_DEV * g.shape[1], g.shape[2])


def _grad_slots(name, dw):
    if name == "w_glu":
        return _cols_to_slots(dw)
    if name == "w_ff1":
        return dw
    if name == "w_in":
        return dw.reshape(N_DEV, BIG_SHAPES[name][1], BIG_SHAPES[name][0])
    return dw.reshape((N_DEV,) + BIG_SHAPES[name])


PIECE_ROWS = 8


VEC_NAMES = tuple(n for n in SMALL_NAMES if n[:6] not in ("ssm_b_", "ssm_c_"))
BC_NAMES = ("ssm_b_re", "ssm_b_im", "ssm_c_re", "ssm_c_im")
BC_ROWS = N_GROUP * SSM_GC


def _bc_view(name, t):
    t = t[0]
    if name.startswith("ssm_b_"):
        t = jnp.swapaxes(t, 1, 2)
    return t.reshape(BC_ROWS, N_STATE)


def _bc_unview(name, t):
    t = t.reshape(N_GROUP, SSM_GC, N_STATE)
    if name.startswith("ssm_b_"):
        t = jnp.swapaxes(t, 1, 2)
    return t[None]


def _pack_bc(vals):
    return jnp.concatenate([_bc_view(n, vals[n]).astype(_F32) for n in BC_NAMES], axis=0)


def _unpack_bc(buf):
    return {n: _bc_unview(n, buf[j * BC_ROWS:(j + 1) * BC_ROWS]) for j, n in enumerate(BC_NAMES)}


def _small_layout(shapes):
    off, rows = {}, 0
    for n in VEC_NAMES:
        off[n] = rows
        rows += -(-math.prod(shapes[n]) // (PIECE_ROWS * LANES)) * PIECE_ROWS
    return off, rows, rows + PIECE_ROWS


def _pack_small(vals, shapes, last=None):
    parts = []
    for n in VEC_NAMES:
        flat = vals[n].reshape(-1).astype(_F32)
        pad = -flat.shape[0] % (PIECE_ROWS * LANES)
        if pad:
            flat = jnp.concatenate([flat, jnp.zeros((pad,), _F32)])
        parts.append(flat.reshape(-1, LANES))
    parts.append(jnp.zeros((PIECE_ROWS, LANES), _F32) if last is None else last)
    return jnp.concatenate(parts, axis=0)


def _unpack_small(buf, shapes):
    off, _, _ = _small_layout(shapes)
    out = {}
    for n in VEC_NAMES:
        size = math.prod(shapes[n])
        rows = -(-size // LANES)
        out[n] = buf[off[n]:off[n] + rows].reshape(-1)[:size].reshape(shapes[n])
    return out


WEIGHT_NAMES = ('norm_mix_pre', 'norm_mix_post', 'w_in', 'ret_gn_gain', 'ssm_lambda_re', 'ssm_lambda_im', 'ssm_log_dt',
                'ssm_b_re', 'ssm_b_im', 'ssm_c_re', 'ssm_c_im', 'ssm_d', 'w_glu', 'w_out', 'norm_mlp_pre',
                'norm_mlp_post', 'w_ff1', 'w_ff2')


def kernel(x, norm_mix_pre, norm_mix_post, w_in, ret_gn_gain, ssm_lambda_re, ssm_lambda_im, ssm_log_dt, ssm_b_re, ssm_b_im, ssm_c_re, ssm_c_im, ssm_d, w_glu, w_out, norm_mlp_pre, norm_mlp_post, w_ff1, w_ff2, loss_target, m_norm_mix_pre, m_norm_mix_post, m_w_in, m_ret_gn_gain, m_ssm_lambda_re, m_ssm_lambda_im, m_ssm_log_dt, m_ssm_b_re, m_ssm_b_im, m_ssm_c_re, m_ssm_c_im, m_ssm_d, m_w_glu, m_w_out, m_norm_mlp_pre, m_norm_mlp_post, m_w_ff1, m_w_ff2, v_norm_mix_pre, v_norm_mix_post, v_w_in, v_ret_gn_gain, v_ssm_lambda_re, v_ssm_lambda_im, v_ssm_log_dt, v_ssm_b_re, v_ssm_b_im, v_ssm_c_re, v_ssm_c_im, v_ssm_d, v_w_glu, v_w_out, v_norm_mlp_pre, v_norm_mlp_post, v_w_ff1, v_w_ff2):
    args = dict(locals())
    w = {n: args[n] for n in WEIGHT_NAMES}
    m = {n: args["m_" + n] for n in WEIGHT_NAMES}
    v = {n: args["v_" + n] for n in WEIGHT_NAMES}
    L = x.shape[1]
    tm = min(256, L)
    tk = min(2048, L)
    tb = min(1024, L)

    calls = {"in": ("w_in",), "rest": WEIGHT_GROUPS["mix"] + WEIGHT_GROUPS["mlp"]}
    started, zero = {}, jnp.zeros((), _F32)
    for call, names in calls.items():
        blocks = [(w[n][0].T if n == "w_in" else w[n][0]).astype(_BF) for n in names]
        blocks[0] = blocks[0] + zero.astype(_BF)
        started[call] = _split_start(blocks, [_landing(b) for b in blocks], True, "weights_start_" + call)
        zero = started[call][4][0, 0]

    def weights(group, after):
        names = WEIGHT_GROUPS[group]
        call = "in" if group == "in" else "rest"
        first = calls[call].index(names[0])
        part = slice(first, first + len(names))
        got = started[call]
        landed = _split_wait(got[0], got[1], got[2][part], got[3][part], after, True, "weights_wait_" + group, first=first)
        return [_weight_from_slots(n, g) for n, g in zip(names, landed)]

    in_flight = []

    def emit(dws):
        names = sorted(dws)
        srcs = [_grad_slots(n, dws[n]) for n in names]
        lands = [_landing(lax.dynamic_index_in_dim(t, _my_index(), 0, keepdims=False)) for t in srcs]
        started = _split_start(srcs, lands, False, "grads_start_" + "_".join(names))
        in_flight.append((names, started))
        return (started[4],)

    shapes = {n: w[n].shape for n in SMALL_NAMES}
    first_piece = {SMALL_NAMES[0]: jnp.zeros(shapes[SMALL_NAMES[0]], _F32)}
    small_flight = []

    def emit_small(gs, sq):
        loss_rows = jnp.broadcast_to(0.5 / D_MODEL * jnp.sum(sq), (PIECE_ROWS, LANES)).astype(_F32)
        bufs = [_pack_small({**first_piece, **gs}, shapes, loss_rows), _pack_bc(gs)]
        small_flight.append(_split_start(bufs, [_landing(b) for b in bufs], True, "small_grads_start"))
        return (small_flight[0][4],)

    small_w = {n: w[n] for n in SMALL_NAMES}
    gx, dg1 = _local_grads(x[0], loss_target[0], small_w, weights, emit, emit_small, tm, tk, tb, zero=zero)
    last_buf = dg1.reshape(PIECE_ROWS, LANES)
    last_started = _split_start([last_buf], [_landing(last_buf)], True, "last_grad_start")

    grads, delta, new_m, new_v = {}, {}, {}, {}
    after = last_started[4]
    for names, started in in_flight:
        landed = _split_wait(*started[:4], after, False, "grads_wait_" + "_".join(names))
        for n, parts in zip(names, landed):
            flip = (lambda t: t.T) if n == "w_in" else (lambda t: t)
            res = _sum_adamw(parts, flip(w[n][0]), flip(m[n][0]), flip(v[n][0]), math.gcd(256, parts.shape[1]), "adamw_" + n)
            grads[n], delta[n], new_m[n], new_v[n] = (flip(t)[None] for t in res)
        after = res[1]
    small_parts, bc_parts = _split_wait(*small_flight[0][:4], after, True, "small_grads_wait")
    last_parts = _split_wait(*last_started[:4], small_parts, True, "last_grad_wait")[0]
    small_parts = lax.dynamic_update_slice(small_parts, last_parts, (0, 0, 0))
    res_bc = _sum_adamw(bc_parts, _pack_bc(w), _pack_bc(m), _pack_bc(v), BC_ROWS, "adamw_bc")
    sw, sm, sv = _pack_small(w, shapes), _pack_small(m, shapes), _pack_small(v, shapes)
    res = _sum_adamw(small_parts, sw, sm, sv, sw.shape[0], "adamw_small")
    for dst, buf, buf_bc in zip((grads, delta, new_m, new_v), res, res_bc):
        dst.update(_unpack_small(buf, shapes))
        dst.update(_unpack_bc(buf_bc))
    _, loss_at, _ = _small_layout(shapes)
    loss = res[0][loss_at, 0]

    return (loss, gx[None], *[grads[n] for n in WEIGHT_NAMES], *[delta[n] for n in WEIGHT_NAMES],
            *[new_m[n] for n in WEIGHT_NAMES], *[new_v[n] for n in WEIGHT_NAMES])
```

```python
import math

import jax
import jax.numpy as jnp
from jax import lax
from jax.experimental import pallas as pl
from jax.experimental.pallas import tpu as pltpu

_BF = jnp.bfloat16
_F32 = jnp.float32

D_MODEL = 1024
RET_W = 512
N_HEAD = 4
HEAD_D = 128
CHUNK = 256
ROPE_CHUNK = 128
SSM_W = 512
SSM_GC = 16
N_GROUP = 32
N_STATE = 64
GROUPS_PER_KB = 8
N_KB = 4
KB_STATES = GROUPS_PER_KB * N_STATE
D_FF = 4096
IN_COLS = 2560
NORM_EPS = 1e-6
ROPE_BASE = 10000.0
N_DEV = 8

ADAM_LR = 0.001
ADAM_B1 = 0.9
ADAM_B2 = 0.999
ADAM_EPS = 1e-08
ADAM_WD = 0.01
ADAM_STEP = 10

SUBLANES = 8
LANES = 128
VMEM_LIMIT = 52 * 1024 * 1024
RET_STEP_CHUNKS = 2
KB_PER_STEP = 2
SCAN_UNROLL = True
FIX_UNROLL = 8

MESH = pl.DeviceIdType.MESH


def _params(*sem):
    return pltpu.CompilerParams(dimension_semantics=sem, vmem_limit_bytes=VMEM_LIMIT)


def _dot(a, b):
    return jnp.dot(a, b, preferred_element_type=_F32)


def _dot_nt(a, b):
    return lax.dot_general(a, b, (((1,), (1,)), ((), ())), preferred_element_type=_F32)


def _dot_tn(a, b):
    return lax.dot_general(a, b, (((0,), (0,)), ((), ())), preferred_element_type=_F32)


def _rms_r(z):
    return lax.rsqrt(jnp.mean(z * z, axis=-1, keepdims=True) + NORM_EPS)


def _rms_bwd(z, g, dn):
    r = _rms_r(z)
    t = dn * g
    dz = r * t - z * (r * r * r * jnp.mean(t * z, axis=-1, keepdims=True))
    return dz, dn * z * r


def _rope(t, cs, sn):
    return t * cs + pltpu.roll(t, HEAD_D // 2, 1) * sn


def _rope_t(t, cs, sn):
    return t * cs - pltpu.roll(t, HEAD_D // 2, 1) * sn


def _sigmoid(z):
    return 1.0 / (1.0 + jnp.exp(-z))


_GELU_C = math.sqrt(2.0 / math.pi)


def _gelu(z):
    return 0.5 * z * (1.0 + jnp.tanh(_GELU_C * (z + 0.044715 * z * z * z)))


def _gelu_grad(z):
    th = jnp.tanh(_GELU_C * (z + 0.044715 * z * z * z))
    return 0.5 * (1.0 + th) + 0.5 * z * (1.0 - th * th) * _GELU_C * (1.0 + 3 * 0.044715 * z * z)


ROW_CHUNK = 256


def _row_chunks(tm):
    return [pl.ds(i, min(ROW_CHUNK, tm)) for i in range(0, tm, ROW_CHUNK)]


def _ordered(body, in_specs, operands, after):
    k = len(after)
    if not k:
        return body, list(in_specs), tuple(operands)
    return ((lambda *refs: body(*refs[k:])), [pl.BlockSpec(memory_space=pl.ANY)] * k + list(in_specs),
            tuple(after) + tuple(operands))


def _row_spec(tm, n):
    return pl.BlockSpec((tm, n), lambda i: (i, 0))


def _full_spec(shape):
    nd = len(shape)
    return pl.BlockSpec(shape, lambda *_: (0,) * nd)


def _weight_spec(shape):
    nd = len(shape)
    return pl.BlockSpec(shape, lambda *_: (0,) * nd, pipeline_mode=pl.Buffered(1))


def _rope_tables(L):
    half = HEAD_D // 2
    inv_freq = ROPE_BASE ** (-jnp.arange(half, dtype=_F32) / half)
    twice = lambda t: jnp.concatenate([t, t], axis=-1)
    off = jnp.arange(ROPE_CHUNK, dtype=_F32)[:, None] * inv_freq[None, :]
    start = (ROPE_CHUNK * jnp.arange(L // ROPE_CHUNK, dtype=_F32))[:, None] * inv_freq[None, :]
    return (twice(jnp.cos(off)), twice(jnp.sin(off)),
            twice(jnp.cos(start))[:, None, :], twice(jnp.sin(start))[:, None, :])


def _prenorm(x, g, tm, after=()):
    L = x.shape[0]

    def body(x_ref, g_ref, h_ref):
        xv = x_ref[...]
        h_ref[...] = (xv * _rms_r(xv) * g_ref[...]).astype(_BF)

    body, in_specs, operands = _ordered(body, [_row_spec(tm, D_MODEL), _full_spec((1, D_MODEL))], (x, g), after)
    return pl.pallas_call(
        body, name="prenorm", grid=(L // tm,),
        in_specs=in_specs, out_specs=_row_spec(tm, D_MODEL),
        out_shape=jax.ShapeDtypeStruct((L, D_MODEL), _BF),
        compiler_params=_params("parallel"),
    )(*operands)


def _inproj_fwd(h, w_in_t, rope, tm):
    L = h.shape[0]
    n_chunks = tm // ROPE_CHUNK

    def body(h_ref, w_ref, co_ref, so_ref, cs_ref, ss_ref, q_ref, k_ref, v_ref, gate_ref, u_ref, cos_ref, sin_ref):
        proj = _dot_nt(h_ref[...], w_ref[...])
        lane = lax.broadcasted_iota(jnp.int32, (ROPE_CHUNK, HEAD_D), 1)
        sign = jnp.where(lane < HEAD_D // 2, -1.0, 1.0)
        co, so = co_ref[...], so_ref[...]
        for c in range(n_chunks):
            chunk = pl.program_id(0) * n_chunks + c
            cst, sst = cs_ref[chunk], ss_ref[chunk]
            rows = slice(c * ROPE_CHUNK, (c + 1) * ROPE_CHUNK)
            cs = co * cst - so * sst
            sn = (so * cst + co * sst) * sign
            cos_ref[rows, :] = cs
            sin_ref[rows, :] = sn
            for hh in range(N_HEAD):
                lo = hh * HEAD_D
                q_ref[rows, lo:lo + HEAD_D] = _rope(proj[rows, lo:lo + HEAD_D], cs, sn).astype(_BF)
                kh = _rope(proj[rows, RET_W + lo:RET_W + lo + HEAD_D], cs, sn) * (HEAD_D ** -0.5)
                k_ref[rows, lo:lo + HEAD_D] = kh.astype(_BF)
        v_ref[...] = proj[:, 2 * RET_W:3 * RET_W].astype(_BF)
        gate_ref[...] = proj[:, 3 * RET_W:4 * RET_W]
        u_ref[...] = proj[:, 4 * RET_W:]

    nc = L // ROPE_CHUNK
    return pl.pallas_call(
        body, name="inproj_fwd", grid=(L // tm,),
        in_specs=[_row_spec(tm, D_MODEL), _weight_spec((IN_COLS, D_MODEL)),
                  _full_spec((ROPE_CHUNK, HEAD_D)), _full_spec((ROPE_CHUNK, HEAD_D)),
                  _full_spec((nc, 1, HEAD_D)), _full_spec((nc, 1, HEAD_D))],
        out_specs=[_row_spec(tm, RET_W)] * 5 + [_row_spec(tm, HEAD_D)] * 2,
        out_shape=[jax.ShapeDtypeStruct((L, RET_W), _BF)] * 3 + [jax.ShapeDtypeStruct((L, RET_W), _F32)] * 2
        + [jax.ShapeDtypeStruct((L, HEAD_D), _F32)] * 2,
        compiler_params=_params("parallel"),
    )(h, w_in_t, *rope)


def _ret_consts():
    lg = jnp.log(1.0 - jnp.exp(jnp.linspace(math.log(1.0 / 32), math.log(1.0 / 512), N_HEAD))).astype(_F32)
    idx = jnp.arange(CHUNK, dtype=_F32)
    diff = idx[:, None] - idx[None, :]
    decay = jnp.where(diff[None] >= 0, jnp.exp(jnp.maximum(diff, 0.0)[None] * lg[:, None, None]), 0.0)
    zeta = jnp.exp((CHUNK - 1 - idx)[None, :] * lg[:, None])
    xi = jnp.exp((idx + 1.0)[None, :] * lg[:, None])
    gc = jnp.exp(CHUNK * lg)
    wide = lambda t: jnp.broadcast_to(t[:, :, None], (N_HEAD, CHUNK, HEAD_D)).astype(_F32)
    gcw = jnp.broadcast_to(gc[:, None, None], (N_HEAD, SUBLANES, HEAD_D)).astype(_F32)
    return decay.astype(_F32), wide(xi), wide(zeta), gcw


def _head_specs():
    wide = _full_spec((N_HEAD, CHUNK, HEAD_D))
    return [_full_spec((N_HEAD, CHUNK, CHUNK)), wide, wide, _full_spec((N_HEAD, SUBLANES, HEAD_D))]


def _retention_fwd(q, k, v, gate, ggn, consts):
    L = q.shape[0]
    nc = L // CHUNK
    cps = math.gcd(RET_STEP_CHUNKS, nc)
    blk = pl.BlockSpec((cps * CHUNK, RET_W), lambda n: (n, 0))

    def body(q_ref, k_ref, v_ref, gate_ref, ggn_ref, dm_ref, xi_ref, zeta_ref, gc_ref,
             o_ref, y_ref, rp_ref, r_scr):
        @pl.when(pl.program_id(0) == 0)
        def _():
            r_scr[...] = jnp.zeros_like(r_scr)

        for hh in range(N_HEAD):
            cols = slice(hh * HEAD_D, (hh + 1) * HEAD_D)
            state = r_scr[hh]
            for c in range(cps):
                rows = slice(c * CHUNK, (c + 1) * CHUNK)
                qv, kv, vv = q_ref[rows, cols], k_ref[rows, cols], v_ref[rows, cols]
                s = _dot_nt(qv, kv) * dm_ref[hh]
                o = _dot(s.astype(_BF), vv) + _dot(qv, state.astype(_BF)) * xi_ref[hh]
                o_ref[rows, cols] = o
                rp_ref[hh, c] = state
                vz = (vv.astype(_F32) * zeta_ref[hh]).astype(_BF)
                state = gc_ref[hh, 0:1, :] * state + _dot_tn(kv, vz)
                dlt = o - jnp.mean(o, axis=-1, keepdims=True)
                on = dlt * lax.rsqrt(jnp.mean(dlt * dlt, axis=-1, keepdims=True) + NORM_EPS)
                gt = gate_ref[rows, cols]
                y_ref[rows, cols] = (gt * _sigmoid(gt) * (on * ggn_ref[:, cols])).astype(_BF)
            r_scr[hh] = state

    return pl.pallas_call(
        body, name="retention_fwd", grid=(nc // cps,),
        in_specs=[blk, blk, blk, blk, _full_spec((1, RET_W))] + _head_specs(),
        out_specs=[blk, blk, pl.BlockSpec((N_HEAD, cps, HEAD_D, HEAD_D), lambda n: (0, n, 0, 0))],
        out_shape=[jax.ShapeDtypeStruct((L, RET_W), _F32), jax.ShapeDtypeStruct((L, RET_W), _BF),
                   jax.ShapeDtypeStruct((N_HEAD, nc, HEAD_D, HEAD_D), _F32)],
        scratch_shapes=[pltpu.VMEM((N_HEAD, HEAD_D, HEAD_D), _F32)],
        compiler_params=_params("arbitrary"),
    )(q, k, v, gate, ggn, *consts)


def _rows_to_segments(dst_scr, src_ref, seg):
    for g in range(dst_scr.shape[0]):
        for j in range(SUBLANES):
            dst_scr[g, pl.ds(j, seg, stride=SUBLANES), :] = src_ref[pl.ds(j * seg, seg), g * LANES:(g + 1) * LANES]


def _segments_to_rows(dst_ref, src_scr, seg):
    for g in range(src_scr.shape[0]):
        for j in range(SUBLANES):
            dst_ref[pl.ds(j * seg, seg), g * LANES:(g + 1) * LANES] = src_scr[g, pl.ds(j, seg, stride=SUBLANES), :].astype(dst_ref.dtype)


def _scan_segments(x_ref, tab_ref, pw_ref, carry_ref, seg, reverse, entry_ref=None, fwd_ref=None, fwd_entry_ref=None,
                   da_ref=None):
    G = x_ref.shape[0]
    W = KB_STATES
    re, im = pl.ds(0, W), pl.ds(W, W)
    row_id = lax.broadcasted_iota(jnp.int32, (SUBLANES, W), 0)
    edge_in = (row_id == SUBLANES - 1) if reverse else (row_id == 0)
    edge_out = 0 if reverse else SUBLANES - 1
    a_tab = [(tab_ref[g, 0], tab_ref[g, 1]) for g in range(G)]

    def local(i, st):
        r = (seg - 1 - i) if reverse else i
        out = []
        for g in range(G):
            (ar, ai), (sr, si) = a_tab[g], st[g]
            nr = ar * sr - ai * si + x_ref[g, r, :, re]
            ni = ar * si + ai * sr + x_ref[g, r, :, im]
            x_ref[g, r, :, re] = nr
            x_ref[g, r, :, im] = ni
            out.append((nr, ni))
        return tuple(out)

    zero = jnp.zeros((SUBLANES, W), _F32)
    ends = lax.fori_loop(0, seg, local, tuple((zero, zero) for _ in range(G)), unroll=SCAN_UNROLL)

    entry = []
    shift = (SUBLANES - 1) if reverse else 1
    for g in range(G):
        er, ei = ends[g]
        fr = jnp.where(edge_in, carry_ref[g, :, re], pltpu.roll(er, shift, 0))
        fi = jnp.where(edge_in, carry_ref[g, :, im], pltpu.roll(ei, shift, 0))
        for j, dist in enumerate((1, 2, 4)):
            pr, pi = tab_ref[g, 2 + 2 * j], tab_ref[g, 3 + 2 * j]
            sh = (SUBLANES - dist) if reverse else dist
            sr, si = pltpu.roll(fr, sh, 0), pltpu.roll(fi, sh, 0)
            fr, fi = fr + pr * sr - pi * si, fi + pr * si + pi * sr
        br, bi = tab_ref[g, 8], tab_ref[g, 9]
        outr = br * fr - bi * fi + er
        outi = br * fi + bi * fr + ei
        carry_ref[g, :, re] = jnp.broadcast_to(outr[edge_out:edge_out + 1, :], (SUBLANES, W))
        carry_ref[g, :, im] = jnp.broadcast_to(outi[edge_out:edge_out + 1, :], (SUBLANES, W))
        entry.append((fr, fi))
        if entry_ref is not None:
            entry_ref[g, :, re] = fr
            entry_ref[g, :, im] = fi

    add_da = da_ref is not None

    def fix(r, st, first=False):
        out = []
        for g in range(G):
            fr, fi = entry[g]
            pwr, pwi = pw_ref[g, r, :, re], pw_ref[g, r, :, im]
            xr = x_ref[g, r, :, re] + (pwr * fr - pwi * fi)
            xi = x_ref[g, r, :, im] + (pwr * fi + pwi * fr)
            x_ref[g, r, :, re] = xr
            x_ref[g, r, :, im] = xi
            if add_da:
                prev = fwd_entry_ref.at[g] if first else fwd_ref.at[g, r - 1]
                xpr, xpi = prev[:, re], prev[:, im]
                out.append((st[g][0] + (xr * xpr + xi * xpi), st[g][1] + (xi * xpr - xr * xpi)))
            else:
                out.append(st[g])
        return tuple(out)

    if add_da:
        st = fix(0, tuple((zero, zero) for _ in range(G)), first=True)
        st = lax.fori_loop(1, seg, fix, st, unroll=SCAN_UNROLL)
        for g in range(G):
            da_ref[g, :, re] += st[g][0]
            da_ref[g, :, im] += st[g][1]
    else:
        lax.fori_loop(0, seg, fix, tuple((zero[0:1, 0:LANES],) for _ in range(G)), unroll=FIX_UNROLL)


def _s5_specs(seg, time=lambda t: t):
    G = KB_PER_STEP
    return dict(
        x=pl.BlockSpec((G, seg, SUBLANES, 2 * KB_STATES), lambda kb, t: (kb, time(t), 0, 0)),
        ent=pl.BlockSpec((G, 1, SUBLANES, 2 * KB_STATES), lambda kb, t: (kb, time(t), 0, 0)),
        b=pl.BlockSpec((G, LANES, 2 * KB_STATES), lambda kb, t: (kb, 0, 0)),
        c=pl.BlockSpec((G, 2 * KB_STATES, LANES), lambda kb, t: (kb, 0, 0)),
        tab=pl.BlockSpec((G, 10, SUBLANES, KB_STATES), lambda kb, t: (kb, 0, 0, 0)),
        pw=pl.BlockSpec((G, seg, 1, 2 * KB_STATES), lambda kb, t: (kb, 0, 0, 0)),
        d=pl.BlockSpec((1, G * LANES), lambda kb, t: (0, kb)),
    )


def _s5_fwd(u, bmat, cmat, tab_f, pw_f, d_skip, tb):
    L = u.shape[0]
    nt = L // tb
    seg = tb // SUBLANES
    G = KB_PER_STEP
    ucol = pl.BlockSpec((tb, G * LANES), lambda kb, t: (t, kb))
    sp = _s5_specs(seg)

    def body(u_ref, b_ref, c_ref, tab_ref, pw_ref, d_ref, s_ref, x_ref, ent_ref, up_scr, y_scr, carry_scr):
        @pl.when(pl.program_id(1) == 0)
        def _():
            carry_scr[...] = jnp.zeros_like(carry_scr)

        _rows_to_segments(up_scr, u_ref, seg)
        for g in range(G):
            x_ref[g] = _dot(up_scr[g].astype(_BF), b_ref[g]).reshape(seg, SUBLANES, 2 * KB_STATES)
        _scan_segments(x_ref, tab_ref, pw_ref, carry_scr, seg, reverse=False, entry_ref=ent_ref.at[:, 0])
        for g in range(G):
            y = _dot(x_ref[g].reshape(tb, 2 * KB_STATES).astype(_BF), c_ref[g])
            y_scr[g] = y + d_ref[:, g * LANES:(g + 1) * LANES] * up_scr[g]
        _segments_to_rows(s_ref, y_scr, seg)

    return pl.pallas_call(
        body, name="s5_fwd", grid=(N_KB // G, nt),
        in_specs=[ucol, sp["b"], sp["c"], sp["tab"], sp["pw"], sp["d"]],
        out_specs=[ucol, sp["x"], sp["ent"]],
        out_shape=[jax.ShapeDtypeStruct((L, SSM_W), _F32),
                   jax.ShapeDtypeStruct((N_KB, L // SUBLANES, SUBLANES, 2 * KB_STATES), _F32),
                   jax.ShapeDtypeStruct((N_KB, nt, SUBLANES, 2 * KB_STATES), _F32)],
        scratch_shapes=[pltpu.VMEM((G, tb, LANES), _F32)] * 2 + [pltpu.VMEM((G, SUBLANES, 2 * KB_STATES), _F32)],
        compiler_params=_params("parallel", "arbitrary"),
    )(u, bmat, cmat, tab_f, pw_f, d_skip)


def _mixout_fwd(s, y_ret, x, w_glu, w_out, g2, tm):
    L = s.shape[0]

    def body(s_ref, yr_ref, x_ref, wg_ref, wo_ref, g_ref, ys_ref, cat_ref, mix_ref, x2_ref):
        for rows in _row_chunks(tm):
            ys = _gelu(s_ref[rows, :]).astype(_BF)
            ys_ref[rows, :] = ys
            glu = _dot(ys, wg_ref[...])
            cat_ref[rows, :RET_W] = yr_ref[rows, :]
            cat_ref[rows, RET_W:] = (glu[:, :SSM_W] * _sigmoid(glu[:, SSM_W:])).astype(_BF)
            mix = _dot(cat_ref[rows, :], wo_ref[...])
            mix_ref[rows, :] = mix
            x2_ref[rows, :] = x_ref[rows, :] + mix * _rms_r(mix) * g_ref[...]

    return pl.pallas_call(
        body, name="mixout_fwd", grid=(L // tm,),
        in_specs=[_row_spec(tm, SSM_W), _row_spec(tm, RET_W), _row_spec(tm, D_MODEL),
                  _weight_spec((SSM_W, 2 * SSM_W)), _weight_spec((D_MODEL, D_MODEL)), _full_spec((1, D_MODEL))],
        out_specs=[_row_spec(tm, SSM_W), _row_spec(tm, D_MODEL), _row_spec(tm, D_MODEL), _row_spec(tm, D_MODEL)],
        out_shape=[jax.ShapeDtypeStruct((L, SSM_W), _BF), jax.ShapeDtypeStruct((L, D_MODEL), _BF),
                   jax.ShapeDtypeStruct((L, D_MODEL), _F32), jax.ShapeDtypeStruct((L, D_MODEL), _F32)],
        compiler_params=_params("parallel"),
    )(s, y_ret, x, w_glu, w_out, g2)


FF1_COLS = D_FF // N_DEV


def _ff1_fwd(x2, g3, w1, tm):
    L = x2.shape[0]

    def body(x_ref, g_ref, w_ref, h_ref, a_ref):
        for rows in _row_chunks(tm):
            xv = x_ref[rows, :]
            h = (xv * _rms_r(xv) * g_ref[...]).astype(_BF)
            h_ref[rows, :] = h
            for j in range(N_DEV):
                cols = slice(j * FF1_COLS, (j + 1) * FF1_COLS)
                rl = jnp.maximum(_dot(h, w_ref[j]), 0.0)
                a_ref[rows, cols] = (rl * rl).astype(_BF)

    return pl.pallas_call(
        body, name="ff1_fwd", grid=(L // tm,),
        in_specs=[_row_spec(tm, D_MODEL), _full_spec((1, D_MODEL)), _weight_spec((N_DEV, D_MODEL, FF1_COLS))],
        out_specs=[_row_spec(tm, D_MODEL), _row_spec(tm, D_FF)],
        out_shape=[jax.ShapeDtypeStruct((L, D_MODEL), _BF), jax.ShapeDtypeStruct((L, D_FF), _BF)],
        compiler_params=_params("parallel"),
    )(x2, g3, w1)


def _ff2_loss(act, x2, tgt, g4, w2, tm):
    L = act.shape[0]

    def body(f_ref, x_ref, t_ref, g_ref, w_ref, dy_ref, dm_ref, dg_ref, ls_ref):
        @pl.when(pl.program_id(0) == 0)
        def _():
            dg_ref[...] = jnp.zeros_like(dg_ref)
            ls_ref[...] = jnp.zeros_like(ls_ref)

        g = g_ref[...]
        for rows in _row_chunks(tm):
            m = _dot(f_ref[rows, :], w_ref[...])
            y = x_ref[rows, :] + m * _rms_r(m) * g
            err = y - t_ref[rows, :]
            ls_ref[...] += jnp.sum(err * err, axis=0, keepdims=True)
            dy = err * (1.0 / D_MODEL)
            dy_ref[rows, :] = dy
            dm, dgr = _rms_bwd(m, g, dy)
            dm_ref[rows, :] = dm.astype(_BF)
            dg_ref[...] += jnp.sum(dgr, axis=0, keepdims=True)

    return pl.pallas_call(
        body, name="ff2_loss", grid=(L // tm,),
        in_specs=[_row_spec(tm, D_FF), _row_spec(tm, D_MODEL), _row_spec(tm, D_MODEL),
                  _full_spec((1, D_MODEL)), _weight_spec((D_FF, D_MODEL))],
        out_specs=[_row_spec(tm, D_MODEL), _row_spec(tm, D_MODEL), _full_spec((1, D_MODEL)), _full_spec((1, D_MODEL))],
        out_shape=[jax.ShapeDtypeStruct((L, D_MODEL), _F32), jax.ShapeDtypeStruct((L, D_MODEL), _BF),
                   jax.ShapeDtypeStruct((1, D_MODEL), _F32), jax.ShapeDtypeStruct((1, D_MODEL), _F32)],
        compiler_params=_params("arbitrary"),
    )(act, x2, tgt, g4, w2)


def _ff2_bwd(dm, act, w2, tm, tn):
    L = dm.shape[0]
    last = L // tm - 1

    def body(dm_ref, a_ref, w_ref, df_ref, dw_ref, acc):
        @pl.when(pl.program_id(1) == 0)
        def _():
            acc[...] = jnp.zeros_like(acc)

        dmv = dm_ref[...]
        av = a_ref[...]
        df_ref[...] = (_dot_nt(dmv, w_ref[...]) * jnp.sqrt(4.0 * av.astype(_F32))).astype(_BF)
        acc[...] += _dot_tn(av, dmv)

        @pl.when(pl.program_id(1) == last)
        def _():
            dw_ref[...] = acc[...].astype(_BF)

    return pl.pallas_call(
        body, name="ff2_bwd", grid=(D_FF // tn, L // tm),
        in_specs=[pl.BlockSpec((tm, D_MODEL), lambda j, i: (i, 0)), pl.BlockSpec((tm, tn), lambda j, i: (i, j)),
                  pl.BlockSpec((tn, D_MODEL), lambda j, i: (j, 0))],
        out_specs=[pl.BlockSpec((tm, tn), lambda j, i: (i, j)), pl.BlockSpec((tn, D_MODEL), lambda j, i: (j, 0))],
        out_shape=[jax.ShapeDtypeStruct((L, D_FF), _BF), jax.ShapeDtypeStruct((D_FF, D_MODEL), _BF)],
        scratch_shapes=[pltpu.VMEM((tn, D_MODEL), _F32)],
        compiler_params=_params("parallel", "arbitrary"),
    )(dm, act, w2)


def _ff1_bwd(df1, w1, x2, mix, dy, g3, g2, tm):
    L = df1.shape[0]

    def body(df_ref, w_ref, x2_ref, mix_ref, dy_ref, g3_ref, g2_ref, dx2_ref, dmix_ref, dg3_ref, dg2_ref):
        @pl.when(pl.program_id(0) == 0)
        def _():
            dg3_ref[...] = jnp.zeros_like(dg3_ref)
            dg2_ref[...] = jnp.zeros_like(dg2_ref)

        for rows in _row_chunks(tm):
            dh = _dot_nt(df_ref[rows, 0:FF1_COLS], w_ref[0])
            for j in range(1, N_DEV):
                dh = dh + _dot_nt(df_ref[rows, j * FF1_COLS:(j + 1) * FF1_COLS], w_ref[j])
            dz, dgr = _rms_bwd(x2_ref[rows, :], g3_ref[...], dh)
            dg3_ref[...] += jnp.sum(dgr, axis=0, keepdims=True)
            dx2 = dy_ref[rows, :] + dz
            dx2_ref[rows, :] = dx2
            dmx, dgr2 = _rms_bwd(mix_ref[rows, :], g2_ref[...], dx2)
            dg2_ref[...] += jnp.sum(dgr2, axis=0, keepdims=True)
            dmix_ref[rows, :] = dmx.astype(_BF)

    vec = _full_spec((1, D_MODEL))
    return pl.pallas_call(
        body, name="ff1_bwd", grid=(L // tm,),
        in_specs=[_row_spec(tm, D_FF), _weight_spec((N_DEV, D_MODEL, FF1_COLS)), _row_spec(tm, D_MODEL),
                  _row_spec(tm, D_MODEL), _row_spec(tm, D_MODEL), vec, vec],
        out_specs=[_row_spec(tm, D_MODEL), _row_spec(tm, D_MODEL), vec, vec],
        out_shape=[jax.ShapeDtypeStruct((L, D_MODEL), _F32), jax.ShapeDtypeStruct((L, D_MODEL), _BF),
                   jax.ShapeDtypeStruct((1, D_MODEL), _F32), jax.ShapeDtypeStruct((1, D_MODEL), _F32)],
        compiler_params=_params("arbitrary"),
    )(df1, w1, x2, mix, dy, g3, g2)


def _matmul_tn(a, b, tm, tn, name, slots=0):
    L, K = a.shape
    N = b.shape[1]
    last = L // tm - 1

    def body(a_ref, b_ref, o_ref, acc):
        @pl.when(pl.program_id(1) == 0)
        def _():
            acc[...] = jnp.zeros_like(acc)

        acc[...] += _dot_tn(a_ref[...].astype(_BF), b_ref[...].astype(_BF))

        @pl.when(pl.program_id(1) == last)
        def _():
            if slots:
                for c in range(tn // slots):
                    o_ref[c] = acc[:, c * slots:(c + 1) * slots].astype(_BF)
            else:
                o_ref[...] = acc[...].astype(_BF)

    if slots:
        out_spec = pl.BlockSpec((tn // slots, K, slots), lambda j, i: (j, 0, 0))
        out_shape = jax.ShapeDtypeStruct((N // slots, K, slots), _BF)
    else:
        out_spec = pl.BlockSpec((K, tn), lambda j, i: (0, j))
        out_shape = jax.ShapeDtypeStruct((K, N), _BF)
    return pl.pallas_call(
        body, name=name, grid=(N // tn, L // tm),
        in_specs=[pl.BlockSpec((tm, K), lambda j, i: (i, 0)), pl.BlockSpec((tm, tn), lambda j, i: (i, j))],
        out_specs=out_spec, out_shape=out_shape,
        scratch_shapes=[pltpu.VMEM((K, tn), _F32)],
        compiler_params=_params("parallel", "arbitrary"),
    )(a, b)


def _dw_in_t(pieces, h, tk):
    L = h.shape[0]
    last = L // tk - 1

    def body(p0, p1, p2, p3, p4, h_ref, o_ref, acc):
        @pl.when(pl.program_id(0) == 0)
        def _():
            acc[...] = jnp.zeros_like(acc)

        hv = h_ref[...]
        for j, p in enumerate((p0, p1, p2, p3, p4)):
            acc[j * RET_W:(j + 1) * RET_W, :] += _dot_tn(p[...].astype(_BF), hv)

        @pl.when(pl.program_id(0) == last)
        def _():
            o_ref[...] = acc[...].astype(_BF)

    return pl.pallas_call(
        body, name="dw_in", grid=(L // tk,),
        in_specs=[_row_spec(tk, RET_W)] * 5 + [_row_spec(tk, D_MODEL)],
        out_specs=_full_spec((IN_COLS, D_MODEL)), out_shape=jax.ShapeDtypeStruct((IN_COLS, D_MODEL), _BF),
        scratch_shapes=[pltpu.VMEM((IN_COLS, D_MODEL), _F32)],
        compiler_params=_params("arbitrary"),
    )(*pieces, h)


def _mixout_bwd(dmix, w_out, w_glu, ys, s, o, gate, ggn, tm, after=()):
    L = dmix.shape[0]

    def body(dmix_ref, wo_ref, wg_ref, ys_ref, s_ref, o_ref, gate_ref, ggn_ref,
             dglu_ref, ds_ref, dgate_ref, do_ref, dggn_ref):
        @pl.when(pl.program_id(0) == 0)
        def _():
            dggn_ref[...] = jnp.zeros_like(dggn_ref)

        ggn = ggn_ref[...]
        for rows in _row_chunks(tm):
            dcat = _dot_nt(dmix_ref[rows, :], wo_ref[...])
            dy_ret, dy_ssm = dcat[:, :RET_W], dcat[:, RET_W:]
            glu = _dot(ys_ref[rows, :], wg_ref[...])
            ga, sg = glu[:, :SSM_W], _sigmoid(glu[:, SSM_W:])
            dga = (dy_ssm * sg).astype(_BF)
            dgb = (dy_ssm * ga * sg * (1.0 - sg)).astype(_BF)
            dglu_ref[rows, :SSM_W] = dga
            dglu_ref[rows, SSM_W:] = dgb
            dys = _dot_nt(dga, wg_ref[:, :SSM_W]) + _dot_nt(dgb, wg_ref[:, SSM_W:])
            ds_ref[rows, :] = dys * _gelu_grad(s_ref[rows, :])
            gt = gate_ref[rows, :]
            sgt = _sigmoid(gt)
            for hh in range(N_HEAD):
                cols = slice(hh * HEAD_D, (hh + 1) * HEAD_D)
                ov = o_ref[rows, cols]
                dlt = ov - jnp.mean(ov, axis=-1, keepdims=True)
                rstd = lax.rsqrt(jnp.mean(dlt * dlt, axis=-1, keepdims=True) + NORM_EPS)
                on = dlt * rstd
                dyr = dy_ret[:, cols] * (gt[:, cols] * sgt[:, cols])
                dgate_ref[rows, cols] = (dy_ret[:, cols] * (on * ggn[:, cols]) * (sgt[:, cols] * (1.0 + gt[:, cols] * (1.0 - sgt[:, cols])))).astype(_BF)
                dggn_ref[:, cols] += jnp.sum(dyr * on, axis=0, keepdims=True)
                don = dyr * ggn[:, cols]
                do = rstd * (don - jnp.mean(don, axis=-1, keepdims=True) - on * jnp.mean(don * on, axis=-1, keepdims=True))
                do_ref[rows, cols] = do.astype(_BF)

    body, in_specs, operands = _ordered(
        body, [_row_spec(tm, D_MODEL), _weight_spec((D_MODEL, D_MODEL)), _weight_spec((SSM_W, 2 * SSM_W)),
               _row_spec(tm, SSM_W), _row_spec(tm, SSM_W), _row_spec(tm, RET_W), _row_spec(tm, RET_W),
               _full_spec((1, RET_W))], (dmix, w_out, w_glu, ys, s, o, gate, ggn), after)
    return pl.pallas_call(
        body, name="mixout_bwd", grid=(L // tm,),
        in_specs=in_specs,
        out_specs=[_row_spec(tm, 2 * SSM_W), _row_spec(tm, SSM_W), _row_spec(tm, RET_W), _row_spec(tm, RET_W),
                   _full_spec((1, RET_W))],
        out_shape=[jax.ShapeDtypeStruct((L, 2 * SSM_W), _BF), jax.ShapeDtypeStruct((L, SSM_W), _F32),
                   jax.ShapeDtypeStruct((L, RET_W), _BF), jax.ShapeDtypeStruct((L, RET_W), _BF),
                   jax.ShapeDtypeStruct((1, RET_W), _F32)],
        compiler_params=_params("arbitrary"),
    )(*operands)


def _s5_bwd(u, ds, xs, ent, bmat, cmat, tab_r, pw_r, d_skip, tb, after=()):
    L = u.shape[0]
    nt = L // tb
    seg = tb // SUBLANES
    G = KB_PER_STEP
    rcol = pl.BlockSpec((tb, G * LANES), lambda kb, t: (nt - 1 - t, kb))
    sp = _s5_specs(seg, time=lambda t: nt - 1 - t)
    aspec = pl.BlockSpec((G, SUBLANES, 2 * KB_STATES), lambda kb, t: (kb, 0, 0))

    def body(u_ref, ds_ref, x_ref, ent_ref, b_ref, c_ref, tr_ref, pr_ref, d_ref,
             du_ref, db_ref, dc_ref, da_ref, dd_ref, up_scr, dp_scr, g_scr, lc_scr):
        @pl.when(pl.program_id(1) == 0)
        def _():
            lc_scr[...] = jnp.zeros_like(lc_scr)
            db_ref[...] = jnp.zeros_like(db_ref)
            dc_ref[...] = jnp.zeros_like(dc_ref)
            da_ref[...] = jnp.zeros_like(da_ref)
            dd_ref[...] = jnp.zeros_like(dd_ref)

        _rows_to_segments(up_scr, u_ref, seg)
        _rows_to_segments(dp_scr, ds_ref, seg)
        for g in range(G):
            g_scr[g] = _dot_nt(dp_scr[g].astype(_BF), c_ref[g]).reshape(seg, SUBLANES, 2 * KB_STATES)
        _scan_segments(g_scr, tr_ref, pr_ref, lc_scr, seg, reverse=True, fwd_ref=x_ref, fwd_entry_ref=ent_ref.at[:, 0],
                       da_ref=da_ref)
        for g in range(G):
            cols = slice(g * LANES, (g + 1) * LANES)
            uv, dsv = up_scr[g], dp_scr[g]
            ub, dsb = uv.astype(_BF), dsv.astype(_BF)
            lamb = g_scr[g].reshape(tb, 2 * KB_STATES).astype(_BF)
            db_ref[g] += _dot_tn(ub, lamb)
            dc_ref[g] += _dot_tn(dsb, x_ref[g].reshape(tb, 2 * KB_STATES).astype(_BF))
            dd_ref[:, cols] += jnp.sum(dsv * uv, axis=0, keepdims=True)
            up_scr[g] = _dot_nt(lamb, b_ref[g]) + d_ref[:, cols] * dsv
        _segments_to_rows(du_ref, up_scr, seg)

    body, in_specs, operands = _ordered(
        body, [rcol, rcol, sp["x"], sp["ent"], sp["b"], sp["c"], sp["tab"], sp["pw"], sp["d"]],
        (u, ds, xs, ent, bmat, cmat, tab_r, pw_r, d_skip), after)
    return pl.pallas_call(
        body, name="s5_bwd", grid=(N_KB // G, nt),
        in_specs=in_specs,
        out_specs=[rcol, sp["b"], sp["b"], aspec, sp["d"]],
        out_shape=[jax.ShapeDtypeStruct((L, SSM_W), _BF),
                   jax.ShapeDtypeStruct((N_KB, LANES, 2 * KB_STATES), _F32),
                   jax.ShapeDtypeStruct((N_KB, LANES, 2 * KB_STATES), _F32),
                   jax.ShapeDtypeStruct((N_KB, SUBLANES, 2 * KB_STATES), _F32),
                   jax.ShapeDtypeStruct((1, SSM_W), _F32)],
        scratch_shapes=[pltpu.VMEM((G, tb, LANES), _F32)] * 2
        + [pltpu.VMEM((G, seg, SUBLANES, 2 * KB_STATES), _F32), pltpu.VMEM((G, SUBLANES, 2 * KB_STATES), _F32)],
        compiler_params=_params("parallel", "arbitrary"),
    )(*operands)


def _retention_bwd(q, k, v, do, r_prev, consts, cosf, sinf, after=()):
    L = q.shape[0]
    nc = L // CHUNK
    cps = math.gcd(RET_STEP_CHUNKS, nc)
    nb = nc // cps
    blk = pl.BlockSpec((cps * CHUNK, RET_W), lambda n: (nb - 1 - n, 0))
    rope_blk = pl.BlockSpec((cps * CHUNK, HEAD_D), lambda n: (nb - 1 - n, 0))

    def body(q_ref, k_ref, v_ref, do_ref, rp_ref, dm_ref, xi_ref, zeta_ref, gc_ref, cos_ref, sin_ref,
             dq_ref, dk_ref, dv_ref, g_scr):
        @pl.when(pl.program_id(0) == 0)
        def _():
            g_scr[...] = jnp.zeros_like(g_scr)

        for hh in range(N_HEAD):
            cols = slice(hh * HEAD_D, (hh + 1) * HEAD_D)
            dm, zeta = dm_ref[hh], zeta_ref[hh]
            gst = g_scr[hh]
            for c in reversed(range(cps)):
                rows = slice(c * CHUNK, (c + 1) * CHUNK)
                qv, kv, vv, dov = q_ref[rows, cols], k_ref[rows, cols], v_ref[rows, cols], do_ref[rows, cols]
                rb = rp_ref[hh, c].astype(_BF)
                gb = gst.astype(_BF)
                sb = (_dot_nt(qv, kv) * dm).astype(_BF)
                dab = (_dot_nt(dov, vv) * dm).astype(_BF)
                dox = (dov.astype(_F32) * xi_ref[hh]).astype(_BF)
                vz = (vv.astype(_F32) * zeta).astype(_BF)
                dq = _dot(dab, kv) + _dot_nt(dox, rb)
                dk = _dot_tn(dab, qv) + _dot_nt(vz, gb)
                dv = _dot_tn(sb, dov) + _dot(kv, gb) * zeta
                gst = gc_ref[hh, 0:1, :] * gst + _dot_tn(qv, dox)
                cs, sn = cos_ref[rows, :], sin_ref[rows, :]
                dq_ref[rows, cols] = _rope_t(dq, cs, sn).astype(_BF)
                dk_ref[rows, cols] = (_rope_t(dk, cs, sn) * (HEAD_D ** -0.5)).astype(_BF)
                dv_ref[rows, cols] = dv.astype(_BF)
            g_scr[hh] = gst

    body, in_specs, operands = _ordered(
        body, [blk, blk, blk, blk, pl.BlockSpec((N_HEAD, cps, HEAD_D, HEAD_D), lambda n: (0, nb - 1 - n, 0, 0))]
        + _head_specs() + [rope_blk, rope_blk], (q, k, v, do, r_prev, *consts, cosf, sinf), after)
    return pl.pallas_call(
        body, name="retention_bwd", grid=(nb,),
        in_specs=in_specs,
        out_specs=[blk, blk, blk],
        out_shape=[jax.ShapeDtypeStruct((L, RET_W), _BF)] * 3,
        scratch_shapes=[pltpu.VMEM((N_HEAD, HEAD_D, HEAD_D), _F32)],
        compiler_params=_params("arbitrary"),
    )(*operands)


def _inproj_bwd(pieces, w_in_t, x, dx2, g1, tm, after=()):
    L = x.shape[0]

    def body(p0, p1, p2, p3, p4, w_ref, x_ref, dx2_ref, g_ref, dx_ref, dg_ref):
        @pl.when(pl.program_id(0) == 0)
        def _():
            dg_ref[...] = jnp.zeros_like(dg_ref)

        for rows in _row_chunks(tm):
            dh = None
            for j, p in enumerate((p0, p1, p2, p3, p4)):
                part = _dot(p[rows, :].astype(_BF), w_ref[j * RET_W:(j + 1) * RET_W, :])
                dh = part if dh is None else dh + part
            dz, dgr = _rms_bwd(x_ref[rows, :], g_ref[...], dh)
            dx_ref[rows, :] = dx2_ref[rows, :] + dz
            dg_ref[...] += jnp.sum(dgr, axis=0, keepdims=True)

    body, in_specs, operands = _ordered(
        body, [_row_spec(tm, RET_W)] * 5 + [_weight_spec((IN_COLS, D_MODEL)), _row_spec(tm, D_MODEL),
                                             _row_spec(tm, D_MODEL), _full_spec((1, D_MODEL))],
        (*pieces, w_in_t, x, dx2, g1), after)
    return pl.pallas_call(
        body, name="inproj_bwd", grid=(L // tm,),
        in_specs=in_specs,
        out_specs=[_row_spec(tm, D_MODEL), _full_spec((1, D_MODEL))],
        out_shape=[jax.ShapeDtypeStruct((L, D_MODEL), _F32), jax.ShapeDtypeStruct((1, D_MODEL), _F32)],
        compiler_params=_params("arbitrary"),
    )(*operands)


def _sum_adamw(parts, w, m, v, tr, name):
    _, R, Cc = parts.shape

    def body(p_ref, w_ref, m_ref, v_ref, g_ref, d_ref, nm_ref, nv_ref):
        gv = p_ref[0].astype(_F32)
        for s in range(1, N_DEV):
            gv = gv + p_ref[s].astype(_F32)
        g_ref[...] = gv
        nm = ADAM_B1 * m_ref[...] + (1.0 - ADAM_B1) * gv
        nv = ADAM_B2 * v_ref[...] + (1.0 - ADAM_B2) * (gv * gv)
        m_hat = nm / (1.0 - ADAM_B1 ** ADAM_STEP)
        v_hat = nv / (1.0 - ADAM_B2 ** ADAM_STEP)
        d_ref[...] = -ADAM_LR * (m_hat / (jnp.sqrt(v_hat) + ADAM_EPS) + ADAM_WD * w_ref[...])
        nm_ref[...] = nm
        nv_ref[...] = nv

    spec = _row_spec(tr, Cc)
    return pl.pallas_call(
        body, name=name, grid=(R // tr,),
        in_specs=[pl.BlockSpec((N_DEV, tr, Cc), lambda i: (0, i, 0))] + [spec] * 3, out_specs=[spec] * 4,
        out_shape=[jax.ShapeDtypeStruct((R, Cc), _F32)] * 4,
        compiler_params=_params("parallel"),
    )(parts, w, m, v)


def _my_place():
    return lax.axis_index("x"), lax.axis_index("y"), lax.axis_index("c")


HBM_SPEC = pl.BlockSpec(memory_space=pltpu.HBM)
SEM_SPEC = pl.BlockSpec(memory_space=pltpu.SEMAPHORE)
DATAFLOW = pltpu.SideEffectType.DATAFLOW_SIDE_EFFECTING


def _my_index():
    x, y, c = _my_place()
    return 4 * x + 2 * y + c


def _landing(own_block):
    zone = lax.empty((N_DEV,) + own_block.shape, own_block.dtype)
    return lax.dynamic_update_index_in_dim(zone, own_block, _my_index(), 0)


def _split_copies(src_refs, land_refs, send_sems, recv_sems, gather, first=0):
    x, y, c = _my_place()
    me = 4 * x + 2 * y + c
    copies = []
    for a, (src, land) in enumerate(zip(src_refs, land_refs)):
        for kk in range(1, N_DEV):
            px, py, pc = x ^ (kk >> 2), y ^ ((kk >> 1) & 1), c ^ (kk & 1)
            peer = 4 * px + 2 * py + pc
            copies.append(pltpu.make_async_remote_copy(
                src_ref=src if gather else src.at[peer], dst_ref=land.at[me],
                send_sem=send_sems.at[(first + a) * 7 + kk - 1], recv_sem=recv_sems.at[(first + a) * 7 + kk - 1],
                device_id=(px, py, pc), device_id_type=MESH))
    return copies


def _split_start(srcs, lands, gather, name):
    n = len(srcs)

    def body(*refs):
        src_refs, land_refs = refs[:n], refs[n:2 * n]
        send_sems, recv_sems = refs[2 * n], refs[2 * n + 1]
        token = refs[-1]
        for cp in _split_copies(src_refs, land_refs, send_sems, recv_sems, gather):
            cp.start()
        token[...] = jnp.zeros_like(token)

    outs = pl.pallas_call(
        body, name=name,
        out_shape=(pltpu.SemaphoreType.DMA((7 * n,)), pltpu.SemaphoreType.DMA((7 * n,)),
                   *[pltpu.HBM(t.shape, t.dtype) for t in srcs], *[pltpu.HBM(t.shape, t.dtype) for t in lands],
                   jax.ShapeDtypeStruct((SUBLANES, LANES), _F32)),
        in_specs=[HBM_SPEC] * (2 * n),
        out_specs=(SEM_SPEC, SEM_SPEC, *[HBM_SPEC] * (2 * n), pl.BlockSpec(memory_space=pltpu.VMEM)),
        input_output_aliases={i: 2 + i for i in range(2 * n)},
        compiler_params=pltpu.CompilerParams(has_side_effects=DATAFLOW),
    )(*[pltpu.with_memory_space_constraint(t, pltpu.HBM) for t in list(srcs) + list(lands)])
    return outs[0], outs[1], outs[2:2 + n], outs[2 + n:2 + 2 * n], outs[-1]


def _split_wait(send_sems, recv_sems, srcs, lands, after, gather, name, first=0):
    n = len(srcs)

    def body(*refs):
        src_refs, land_refs = refs[:n], refs[n:2 * n]
        send_s, recv_s = refs[2 * n], refs[2 * n + 1]
        for cp in _split_copies(src_refs, land_refs, send_s, recv_s, gather, first):
            cp.wait_send()
            cp.wait_recv()

    outs = pl.pallas_call(
        body, name=name,
        out_shape=tuple(pltpu.HBM(t.shape, t.dtype) for t in list(srcs) + list(lands)),
        in_specs=[HBM_SPEC] * (2 * n) + [SEM_SPEC, SEM_SPEC, pl.BlockSpec(memory_space=pl.ANY)],
        out_specs=tuple([HBM_SPEC] * (2 * n)),
        input_output_aliases={i: i for i in range(2 * n)},
        compiler_params=pltpu.CompilerParams(has_side_effects=DATAFLOW),
    )(*srcs, *lands, send_sems, recv_sems, after)
    return outs[n:]


def _discretize(lam_re, lam_im, log_dt, b_re, b_im):
    lr = jnp.minimum(lam_re, -1e-4)
    li = lam_im
    dt = jnp.exp(log_dt)[:, None]
    er = jnp.exp(lr * dt)
    ar, ai = er * jnp.cos(li * dt), er * jnp.sin(li * dt)
    den = lr * lr + li * li
    cr = ((ar - 1.0) * lr + ai * li) / den
    ci = (ai * lr - (ar - 1.0) * li) / den
    bbr = cr[:, :, None] * b_re - ci[:, :, None] * b_im
    bbi = cr[:, :, None] * b_im + ci[:, :, None] * b_re
    return ar, ai, bbr, bbi


def _cmul(ar, ai, br, bi):
    return ar * br - ai * bi, ar * bi + ai * br


def _cpowers(ar, ai, n):
    pr, pi = ar[None], ai[None]
    while pr.shape[0] < n:
        nr, ni = _cmul(pr, pi, pr[-1][None], pi[-1][None])
        pr, pi = jnp.concatenate([pr, nr]), jnp.concatenate([pi, ni])
    return pr[:n], pi[:n]


def _scan_tables(ar, ai, seg, reverse):
    if reverse:
        ai = -ai
    ar, ai = ar.reshape(N_KB, KB_STATES), ai.reshape(N_KB, KB_STATES)
    pr, pi = _cpowers(ar, ai, seg)
    a1 = (pr[-1], pi[-1])
    a2 = _cmul(*a1, *a1)
    a4 = _cmul(*a2, *a2)
    row = jnp.arange(SUBLANES)[None, :, None]
    wide = lambda t: jnp.broadcast_to(t[:, None, :], (N_KB, SUBLANES, KB_STATES))
    tabs = [wide(ar), wide(ai)]
    for dist, (qr, qi) in ((1, a1), (2, a2), (4, a4)):
        keep = (row < SUBLANES - dist) if reverse else (row >= dist)
        tabs += [jnp.where(keep, wide(qr), 0.0), jnp.where(keep, wide(qi), 0.0)]
    tabs += [wide(a1[0]), wide(a1[1])]
    if reverse:
        pr, pi = pr[::-1], pi[::-1]
    pw = jnp.transpose(jnp.concatenate([pr, pi], axis=-1), (1, 0, 2))[:, :, None, :]
    return jnp.stack(tabs, axis=1).astype(_F32), pw.astype(_F32)


def _block_diag_in(br, bi):
    eye = jnp.eye(GROUPS_PER_KB, dtype=_F32)
    one = lambda t: jnp.einsum("kgpc,gh->kgchp", t.reshape(N_KB, GROUPS_PER_KB, N_STATE, SSM_GC), eye).reshape(
        N_KB, LANES, KB_STATES)
    return jnp.concatenate([one(br), one(bi)], axis=-1)


def _block_diag_in_t(dmat):
    d6 = dmat.reshape(N_KB, GROUPS_PER_KB, SSM_GC, 2, GROUPS_PER_KB, N_STATE)
    eye = jnp.eye(GROUPS_PER_KB, dtype=_F32)
    both = jnp.einsum("kgcrhp,gh->rkgpc", d6, eye).reshape(2, N_GROUP, N_STATE, SSM_GC)
    return both[0], both[1]


def _block_diag_out(c_re, c_im):
    eye = jnp.eye(GROUPS_PER_KB, dtype=_F32)
    one = lambda t: jnp.einsum("kgcp,gh->khpgc", t.reshape(N_KB, GROUPS_PER_KB, SSM_GC, N_STATE), eye).reshape(
        N_KB, KB_STATES, LANES)
    return jnp.concatenate([one(c_re), -one(c_im)], axis=1)


def _block_diag_out_t(dmat_t):
    d6 = dmat_t.reshape(N_KB, GROUPS_PER_KB, SSM_GC, 2, GROUPS_PER_KB, N_STATE)
    eye = jnp.eye(GROUPS_PER_KB, dtype=_F32)
    both = jnp.einsum("kgcrhp,gh->rkgcp", d6, eye).reshape(2, N_GROUP, SSM_GC, N_STATE)
    return both[0], -both[1]


SMALL_NAMES = ("norm_mix_pre", "norm_mix_post", "ret_gn_gain", "ssm_lambda_re", "ssm_lambda_im", "ssm_log_dt",
               "ssm_b_re", "ssm_b_im", "ssm_c_re", "ssm_c_im", "ssm_d", "norm_mlp_pre", "norm_mlp_post")


def _local_grads(x, tgt, small, weights, emit, emit_small, tm, tk, tb, zero=0.0):
    L = x.shape[0]
    g1, g2, ggn = small["norm_mix_pre"], small["norm_mix_post"], small["ret_gn_gain"]
    g3, g4, d_skip = small["norm_mlp_pre"], small["norm_mlp_post"], small["ssm_d"]

    rope = _rope_tables(L)
    consts = _ret_consts()

    disc_in = (small["ssm_lambda_re"][0], small["ssm_lambda_im"][0], small["ssm_log_dt"][0] + zero,
               small["ssm_b_re"][0], small["ssm_b_im"][0])
    (ar, ai, bbr, bbi), disc_vjp = jax.vjp(_discretize, *disc_in)
    bmat = _block_diag_in(bbr, bbi).astype(_BF)
    cmat = _block_diag_out(small["ssm_c_re"][0], small["ssm_c_im"][0]).astype(_BF)
    seg = tb // SUBLANES
    tab_f, pw_f = _scan_tables(ar, ai, seg, False)
    tab_r, pw_r = _scan_tables(ar, ai, seg, True)

    h1 = _prenorm(x, g1, min(4 * tm, L), after=(pw_r,))
    (w_in_t,) = weights("in", h1)
    q, k, v, gate, u, cosf, sinf = _inproj_fwd(h1, w_in_t, rope, min(4 * tm, L))
    o, y_ret, r_prev = _retention_fwd(q, k, v, gate, ggn, consts)
    s, xs, ent = _s5_fwd(u, bmat, cmat, tab_f, pw_f, d_skip, tb)
    w_glu, w_out = weights("mix", s)
    ys, cat, mix, x2 = _mixout_fwd(s, y_ret, x, w_glu, w_out, g2, min(2 * tm, L))
    w_ff1, w_ff2 = weights("mlp", x2)
    h3, act = _ff1_fwd(x2, g3, w_ff1, min(2 * tm, L))
    dy, dm, dg4, sq = _ff2_loss(act, x2, tgt, g4, w_ff2, min(2 * tm, L))

    df1, dw_ff2 = _ff2_bwd(dm, act, w_ff2, min(1024, L), 1024)
    dx2, dmix, dg3, dg2 = _ff1_bwd(df1, w_ff1, x2, mix, dy, g3, g2, min(2 * tm, L))
    dw_ff1 = _matmul_tn(h3, df1, tk, 2 * FF1_COLS, "dw_ff1", slots=FF1_COLS)
    token = emit({"w_ff1": dw_ff1, "w_ff2": dw_ff2})
    dglu, ds, dgate, do, dggn = _mixout_bwd(dmix, w_out, w_glu, ys, s, o, gate, ggn, min(2 * tm, L), after=token)
    dw_out = _matmul_tn(cat, dmix, tk, 1024, "dw_out")
    dw_glu = _matmul_tn(ys, dglu, tk, 1024, "dw_glu")
    token = emit({"w_glu": dw_glu, "w_out": dw_out})
    du, dbmat, dcmat, da8, dd = _s5_bwd(u, ds, xs, ent, bmat, cmat, tab_r, pw_r, d_skip, tb, after=token)

    da = jnp.sum(da8, axis=1)
    dar = da[:, :KB_STATES].reshape(N_GROUP, N_STATE)
    dai = da[:, KB_STATES:].reshape(N_GROUP, N_STATE)
    dbr, dbi = _block_diag_in_t(dbmat)
    dlre, dlim, dldt, dbre, dbim = disc_vjp((dar, dai, dbr, dbi))
    dcre, dcim = _block_diag_out_t(dcmat)
    token = emit_small({
        "norm_mix_post": dg2, "ret_gn_gain": dggn,
        "ssm_lambda_re": dlre[None], "ssm_lambda_im": dlim[None], "ssm_log_dt": dldt[None],
        "ssm_b_re": dbre[None], "ssm_b_im": dbim[None], "ssm_c_re": dcre[None], "ssm_c_im": dcim[None],
        "ssm_d": dd, "norm_mlp_pre": dg3, "norm_mlp_post": dg4,
    }, sq)

    dq, dk, dv = _retention_bwd(q, k, v, do, r_prev, consts, cosf, sinf, after=token)
    pieces = (dq, dk, dv, dgate, du)
    dw_in_t = _dw_in_t(pieces, h1, min(1024, L))
    token = emit({"w_in": dw_in_t})
    gx, dg1 = _inproj_bwd(pieces, w_in_t, x, dx2, g1, min(2 * tm, L), after=token)
    return gx, dg1


BIG_SHAPES = {"w_in": (D_MODEL, IN_COLS // N_DEV), "w_glu": (SSM_W, 2 * SSM_W // N_DEV), "w_out": (D_MODEL // N_DEV, D_MODEL),
              "w_ff1": (D_MODEL, FF1_COLS), "w_ff2": (D_FF // N_DEV, D_MODEL)}
BIG_NAMES = ("w_in", "w_glu", "w_out", "w_ff1", "w_ff2")


def _cols_from_slots(g):
    return jnp.transpose(g, (1, 0, 2)).reshape(g.shape[1], N_DEV * g.shape[2])


def _cols_to_slots(dw):
    r, cols = dw.shape
    return jnp.transpose(dw.reshape(r, N_DEV, cols // N_DEV), (1, 0, 2))


WEIGHT_GROUPS = {"in": ("w_in",), "mix": ("w_glu", "w_out"), "mlp": ("w_ff1", "w_ff2")}


def _weight_from_slots(name, g):
    if name == "w_glu":
        return _cols_from_slots(g)
    if name == "w_ff1":
        return g
    return g.reshape(N_DEV * g.shape[1], g.shape[2])


def _grad_slots(name, dw):
    if name == "w_glu":
        return _cols_to_slots(dw)
    if name == "w_ff1":
        return dw
    if name == "w_in":
        return dw.reshape(N_DEV, BIG_SHAPES[name][1], BIG_SHAPES[name][0])
    return dw.reshape((N_DEV,) + BIG_SHAPES[name])


PIECE_ROWS = 8


VEC_NAMES = tuple(n for n in SMALL_NAMES if n[:6] not in ("ssm_b_", "ssm_c_"))
BC_NAMES = ("ssm_b_re", "ssm_b_im", "ssm_c_re", "ssm_c_im")
BC_ROWS = N_GROUP * SSM_GC


def _bc_view(name, t):
    t = t[0]
    if name.startswith("ssm_b_"):
        t = jnp.swapaxes(t, 1, 2)
    return t.reshape(BC_ROWS, N_STATE)


def _bc_unview(name, t):
    t = t.reshape(N_GROUP, SSM_GC, N_STATE)
    if name.startswith("ssm_b_"):
        t = jnp.swapaxes(t, 1, 2)
    return t[None]


def _pack_bc(vals):
    return jnp.concatenate([_bc_view(n, vals[n]).astype(_F32) for n in BC_NAMES], axis=0)


def _unpack_bc(buf):
    return {n: _bc_unview(n, buf[j * BC_ROWS:(j + 1) * BC_ROWS]) for j, n in enumerate(BC_NAMES)}


def _small_layout(shapes):
    off, rows = {}, 0
    for n in VEC_NAMES:
        off[n] = rows
        rows += -(-math.prod(shapes[n]) // (PIECE_ROWS * LANES)) * PIECE_ROWS
    return off, rows, rows + PIECE_ROWS


def _pack_small(vals, shapes, last=None):
    parts = []
    for n in VEC_NAMES:
        flat = vals[n].reshape(-1).astype(_F32)
        pad = -flat.shape[0] % (PIECE_ROWS * LANES)
        if pad:
            flat = jnp.concatenate([flat, jnp.zeros((pad,), _F32)])
        parts.append(flat.reshape(-1, LANES))
    parts.append(jnp.zeros((PIECE_ROWS, LANES), _F32) if last is None else last)
    return jnp.concatenate(parts, axis=0)


def _unpack_small(buf, shapes):
    off, _, _ = _small_layout(shapes)
    out = {}
    for n in VEC_NAMES:
        size = math.prod(shapes[n])
        rows = -(-size // LANES)
        out[n] = buf[off[n]:off[n] + rows].reshape(-1)[:size].reshape(shapes[n])
    return out


WEIGHT_NAMES = ('norm_mix_pre', 'norm_mix_post', 'w_in', 'ret_gn_gain', 'ssm_lambda_re', 'ssm_lambda_im', 'ssm_log_dt',
                'ssm_b_re', 'ssm_b_im', 'ssm_c_re', 'ssm_c_im', 'ssm_d', 'w_glu', 'w_out', 'norm_mlp_pre',
                'norm_mlp_post', 'w_ff1', 'w_ff2')


def kernel(x, norm_mix_pre, norm_mix_post, w_in, ret_gn_gain, ssm_lambda_re, ssm_lambda_im, ssm_log_dt, ssm_b_re, ssm_b_im, ssm_c_re, ssm_c_im, ssm_d, w_glu, w_out, norm_mlp_pre, norm_mlp_post, w_ff1, w_ff2, loss_target, m_norm_mix_pre, m_norm_mix_post, m_w_in, m_ret_gn_gain, m_ssm_lambda_re, m_ssm_lambda_im, m_ssm_log_dt, m_ssm_b_re, m_ssm_b_im, m_ssm_c_re, m_ssm_c_im, m_ssm_d, m_w_glu, m_w_out, m_norm_mlp_pre, m_norm_mlp_post, m_w_ff1, m_w_ff2, v_norm_mix_pre, v_norm_mix_post, v_w_in, v_ret_gn_gain, v_ssm_lambda_re, v_ssm_lambda_im, v_ssm_log_dt, v_ssm_b_re, v_ssm_b_im, v_ssm_c_re, v_ssm_c_im, v_ssm_d, v_w_glu, v_w_out, v_norm_mlp_pre, v_norm_mlp_post, v_w_ff1, v_w_ff2):
    args = dict(locals())
    w = {n: args[n] for n in WEIGHT_NAMES}
    m = {n: args["m_" + n] for n in WEIGHT_NAMES}
    v = {n: args["v_" + n] for n in WEIGHT_NAMES}
    L = x.shape[1]
    tm = min(256, L)
    tk = min(2048, L)
    tb = min(1024, L)

    calls = {"in": ("w_in",), "rest": WEIGHT_GROUPS["mix"] + WEIGHT_GROUPS["mlp"]}
    started, zero = {}, jnp.zeros((), _F32)
    for call, names in calls.items():
        blocks = [(w[n][0].T if n == "w_in" else w[n][0]).astype(_BF) for n in names]
        blocks[0] = blocks[0] + zero.astype(_BF)
        started[call] = _split_start(blocks, [_landing(b) for b in blocks], True, "weights_start_" + call)
        zero = started[call][4][0, 0]

    def weights(group, after):
        names = WEIGHT_GROUPS[group]
        call = "in" if group == "in" else "rest"
        first = calls[call].index(names[0])
        part = slice(first, first + len(names))
        got = started[call]
        landed = _split_wait(got[0], got[1], got[2][part], got[3][part], after, True, "weights_wait_" + group, first=first)
        return [_weight_from_slots(n, g) for n, g in zip(names, landed)]

    in_flight = []

    def emit(dws):
        names = sorted(dws)
        srcs = [_grad_slots(n, dws[n]) for n in names]
        lands = [_landing(lax.dynamic_index_in_dim(t, _my_index(), 0, keepdims=False)) for t in srcs]
        started = _split_start(srcs, lands, False, "grads_start_" + "_".join(names))
        in_flight.append((names, started))
        return (started[4],)

    shapes = {n: w[n].shape for n in SMALL_NAMES}
    first_piece = {SMALL_NAMES[0]: jnp.zeros(shapes[SMALL_NAMES[0]], _F32)}
    small_flight = []

    def emit_small(gs, sq):
        loss_rows = jnp.broadcast_to(0.5 / D_MODEL * jnp.sum(sq), (PIECE_ROWS, LANES)).astype(_F32)
        bufs = [_pack_small({**first_piece, **gs}, shapes, loss_rows), _pack_bc(gs)]
        small_flight.append(_split_start(bufs, [_landing(b) for b in bufs], True, "small_grads_start"))
        return (small_flight[0][4],)

    small_w = {n: w[n] for n in SMALL_NAMES}
    gx, dg1 = _local_grads(x[0], loss_target[0], small_w, weights, emit, emit_small, tm, tk, tb, zero=zero)
    last_buf = dg1.reshape(PIECE_ROWS, LANES)
    last_started = _split_start([last_buf], [_landing(last_buf)], True, "last_grad_start")

    grads, delta, new_m, new_v = {}, {}, {}, {}
    after = last_started[4]
    for names, started in in_flight:
        landed = _split_wait(*started[:4], after, False, "grads_wait_" + "_".join(names))
        for n, parts in zip(names, landed):
            flip = (lambda t: t.T) if n == "w_in" else (lambda t: t)
            res = _sum_adamw(parts, flip(w[n][0]), flip(m[n][0]), flip(v[n][0]), math.gcd(256, parts.shape[1]), "adamw_" + n)
            grads[n], delta[n], new_m[n], new_v[n] = (flip(t)[None] for t in res)
        after = res[1]
    small_parts, bc_parts = _split_wait(*small_flight[0][:4], after, True, "small_grads_wait")
    last_parts = _split_wait(*last_started[:4], small_parts, True, "last_grad_wait")[0]
    small_parts = lax.dynamic_update_slice(small_parts, last_parts, (0, 0, 0))
    res_bc = _sum_adamw(bc_parts, _pack_bc(w), _pack_bc(m), _pack_bc(v), BC_ROWS, "adamw_bc")
    sw, sm, sv = _pack_small(w, shapes), _pack_small(m, shapes), _pack_small(v, shapes)
    res = _sum_adamw(small_parts, sw, sm, sv, sw.shape[0], "adamw_small")
    for dst, buf, buf_bc in zip((grads, delta, new_m, new_v), res, res_bc):
        dst.update(_unpack_small(buf, shapes))
        dst.update(_unpack_bc(buf_bc))
    _, loss_at, _ = _small_layout(shapes)
    loss = res[0][loss_at, 0]

    return (loss, gx[None], *[grads[n] for n in WEIGHT_NAMES], *[delta[n] for n in WEIGHT_NAMES],
            *[new_m[n] for n in WEIGHT_NAMES], *[new_v[n] for n in WEIGHT_NAMES])
```

```python
import math

import jax
import jax.numpy as jnp
from jax import lax
from jax.experimental import pallas as pl
from jax.experimental.pallas import tpu as pltpu

_BF = jnp.bfloat16
_F32 = jnp.float32

D_MODEL = 1024
RET_W = 512
N_HEAD = 4
HEAD_D = 128
CHUNK = 256
ROPE_CHUNK = 128
SSM_W = 512
SSM_GC = 16
N_GROUP = 32
N_STATE = 64
GROUPS_PER_KB = 8
N_KB = 4
KB_STATES = GROUPS_PER_KB * N_STATE
D_FF = 4096
IN_COLS = 2560
NORM_EPS = 1e-6
ROPE_BASE = 10000.0
N_DEV = 8

ADAM_LR = 0.001
ADAM_B1 = 0.9
ADAM_B2 = 0.999
ADAM_EPS = 1e-08
ADAM_WD = 0.01
ADAM_STEP = 10

SUBLANES = 8
LANES = 128
VMEM_LIMIT = 52 * 1024 * 1024
RET_STEP_CHUNKS = 2
KB_PER_STEP = 2
SCAN_UNROLL = True
FIX_UNROLL = 8

MESH = pl.DeviceIdType.MESH


def _params(*sem):
    return pltpu.CompilerParams(dimension_semantics=sem, vmem_limit_bytes=VMEM_LIMIT)


def _dot(a, b):
    return jnp.dot(a, b, preferred_element_type=_F32)


def _dot_nt(a, b):
    return lax.dot_general(a, b, (((1,), (1,)), ((), ())), preferred_element_type=_F32)


def _dot_tn(a, b):
    return lax.dot_general(a, b, (((0,), (0,)), ((), ())), preferred_element_type=_F32)


def _rms_r(z):
    return lax.rsqrt(jnp.mean(z * z, axis=-1, keepdims=True) + NORM_EPS)


def _rms_bwd(z, g, dn):
    r = _rms_r(z)
    t = dn * g
    dz = r * t - z * (r * r * r * jnp.mean(t * z, axis=-1, keepdims=True))
    return dz, dn * z * r


def _rope(t, cs, sn):
    return t * cs + pltpu.roll(t, HEAD_D // 2, 1) * sn


def _rope_t(t, cs, sn):
    return t * cs - pltpu.roll(t, HEAD_D // 2, 1) * sn


def _sigmoid(z):
    return 1.0 / (1.0 + jnp.exp(-z))


_GELU_C = math.sqrt(2.0 / math.pi)


def _gelu(z):
    return 0.5 * z * (1.0 + jnp.tanh(_GELU_C * (z + 0.044715 * z * z * z)))


def _gelu_grad(z):
    th = jnp.tanh(_GELU_C * (z + 0.044715 * z * z * z))
    return 0.5 * (1.0 + th) + 0.5 * z * (1.0 - th * th) * _GELU_C * (1.0 + 3 * 0.044715 * z * z)


ROW_CHUNK = 256


def _row_chunks(tm):
    return [pl.ds(i, min(ROW_CHUNK, tm)) for i in range(0, tm, ROW_CHUNK)]


def _ordered(body, in_specs, operands, after):
    k = len(after)
    if not k:
        return body, list(in_specs), tuple(operands)
    return ((lambda *refs: body(*refs[k:])), [pl.BlockSpec(memory_space=pl.ANY)] * k + list(in_specs),
            tuple(after) + tuple(operands))


def _row_spec(tm, n):
    return pl.BlockSpec((tm, n), lambda i: (i, 0))


def _full_spec(shape):
    nd = len(shape)
    return pl.BlockSpec(shape, lambda *_: (0,) * nd)


def _weight_spec(shape):
    nd = len(shape)
    return pl.BlockSpec(shape, lambda *_: (0,) * nd, pipeline_mode=pl.Buffered(1))


def _rope_tables(L):
    half = HEAD_D // 2
    inv_freq = ROPE_BASE ** (-jnp.arange(half, dtype=_F32) / half)
    twice = lambda t: jnp.concatenate([t, t], axis=-1)
    off = jnp.arange(ROPE_CHUNK, dtype=_F32)[:, None] * inv_freq[None, :]
    start = (ROPE_CHUNK * jnp.arange(L // ROPE_CHUNK, dtype=_F32))[:, None] * inv_freq[None, :]
    return (twice(jnp.cos(off)), twice(jnp.sin(off)),
            twice(jnp.cos(start))[:, None, :], twice(jnp.sin(start))[:, None, :])


def _prenorm(x, g, tm, after=()):
    L = x.shape[0]

    def body(x_ref, g_ref, h_ref):
        xv = x_ref[...]
        h_ref[...] = (xv * _rms_r(xv) * g_ref[...]).astype(_BF)

    body, in_specs, operands = _ordered(body, [_row_spec(tm, D_MODEL), _full_spec((1, D_MODEL))], (x, g), after)
    return pl.pallas_call(
        body, name="prenorm", grid=(L // tm,),
        in_specs=in_specs, out_specs=_row_spec(tm, D_MODEL),
        out_shape=jax.ShapeDtypeStruct((L, D_MODEL), _BF),
        compiler_params=_params("parallel"),
    )(*operands)


def _inproj_fwd(h, w_in_t, rope, tm):
    L = h.shape[0]
    n_chunks = tm // ROPE_CHUNK

    def body(h_ref, w_ref, co_ref, so_ref, cs_ref, ss_ref, q_ref, k_ref, v_ref, gate_ref, u_ref, cos_ref, sin_ref):
        proj = _dot_nt(h_ref[...], w_ref[...])
        lane = lax.broadcasted_iota(jnp.int32, (ROPE_CHUNK, HEAD_D), 1)
        sign = jnp.where(lane < HEAD_D // 2, -1.0, 1.0)
        co, so = co_ref[...], so_ref[...]
        for c in range(n_chunks):
            chunk = pl.program_id(0) * n_chunks + c
            cst, sst = cs_ref[chunk], ss_ref[chunk]
            rows = slice(c * ROPE_CHUNK, (c + 1) * ROPE_CHUNK)
            cs = co * cst - so * sst
            sn = (so * cst + co * sst) * sign
            cos_ref[rows, :] = cs
            sin_ref[rows, :] = sn
            for hh in range(N_HEAD):
                lo = hh * HEAD_D
                q_ref[rows, lo:lo + HEAD_D] = _rope(proj[rows, lo:lo + HEAD_D], cs, sn).astype(_BF)
                kh = _rope(proj[rows, RET_W + lo:RET_W + lo + HEAD_D], cs, sn) * (HEAD_D ** -0.5)
                k_ref[rows, lo:lo + HEAD_D] = kh.astype(_BF)
        v_ref[...] = proj[:, 2 * RET_W:3 * RET_W].astype(_BF)
        gate_ref[...] = proj[:, 3 * RET_W:4 * RET_W]
        u_ref[...] = proj[:, 4 * RET_W:]

    nc = L // ROPE_CHUNK
    return pl.pallas_call(
        body, name="inproj_fwd", grid=(L // tm,),
        in_specs=[_row_spec(tm, D_MODEL), _weight_spec((IN_COLS, D_MODEL)),
                  _full_spec((ROPE_CHUNK, HEAD_D)), _full_spec((ROPE_CHUNK, HEAD_D)),
                  _full_spec((nc, 1, HEAD_D)), _full_spec((nc, 1, HEAD_D))],
        out_specs=[_row_spec(tm, RET_W)] * 5 + [_row_spec(tm, HEAD_D)] * 2,
        out_shape=[jax.ShapeDtypeStruct((L, RET_W), _BF)] * 3 + [jax.ShapeDtypeStruct((L, RET_W), _F32)] * 2
        + [jax.ShapeDtypeStruct((L, HEAD_D), _F32)] * 2,
        compiler_params=_params("parallel"),
    )(h, w_in_t, *rope)


def _ret_consts():
    lg = jnp.log(1.0 - jnp.exp(jnp.linspace(math.log(1.0 / 32), math.log(1.0 / 512), N_HEAD))).astype(_F32)
    idx = jnp.arange(CHUNK, dtype=_F32)
    diff = idx[:, None] - idx[None, :]
    decay = jnp.where(diff[None] >= 0, jnp.exp(jnp.maximum(diff, 0.0)[None] * lg[:, None, None]), 0.0)
    zeta = jnp.exp((CHUNK - 1 - idx)[None, :] * lg[:, None])
    xi = jnp.exp((idx + 1.0)[None, :] * lg[:, None])
    gc = jnp.exp(CHUNK * lg)
    wide = lambda t: jnp.broadcast_to(t[:, :, None], (N_HEAD, CHUNK, HEAD_D)).astype(_F32)
    gcw = jnp.broadcast_to(gc[:, None, None], (N_HEAD, SUBLANES, HEAD_D)).astype(_F32)
    return decay.astype(_F32), wide(xi), wide(zeta), gcw


def _head_specs():
    wide = _full_spec((N_HEAD, CHUNK, HEAD_D))
    return [_full_spec((N_HEAD, CHUNK, CHUNK)), wide, wide, _full_spec((N_HEAD, SUBLANES, HEAD_D))]


def _retention_fwd(q, k, v, gate, ggn, consts):
    L = q.shape[0]
    nc = L // CHUNK
    cps = math.gcd(RET_STEP_CHUNKS, nc)
    blk = pl.BlockSpec((cps * CHUNK, RET_W), lambda n: (n, 0))

    def body(q_ref, k_ref, v_ref, gate_ref, ggn_ref, dm_ref, xi_ref, zeta_ref, gc_ref,
             o_ref, y_ref, rp_ref, r_scr):
        @pl.when(pl.program_id(0) == 0)
        def _():
            r_scr[...] = jnp.zeros_like(r_scr)

        for hh in range(N_HEAD):
            cols = slice(hh * HEAD_D, (hh + 1) * HEAD_D)
            state = r_scr[hh]
            for c in range(cps):
                rows = slice(c * CHUNK, (c + 1) * CHUNK)
                qv, kv, vv = q_ref[rows, cols], k_ref[rows, cols], v_ref[rows, cols]
                s = _dot_nt(qv, kv) * dm_ref[hh]
                o = _dot(s.astype(_BF), vv) + _dot(qv, state.astype(_BF)) * xi_ref[hh]
                o_ref[rows, cols] = o
                rp_ref[hh, c] = state
                vz = (vv.astype(_F32) * zeta_ref[hh]).astype(_BF)
                state = gc_ref[hh, 0:1, :] * state + _dot_tn(kv, vz)
                dlt = o - jnp.mean(o, axis=-1, keepdims=True)
                on = dlt * lax.rsqrt(jnp.mean(dlt * dlt, axis=-1, keepdims=True) + NORM_EPS)
                gt = gate_ref[rows, cols]
                y_ref[rows, cols] = (gt * _sigmoid(gt) * (on * ggn_ref[:, cols])).astype(_BF)
            r_scr[hh] = state

    return pl.pallas_call(
        body, name="retention_fwd", grid=(nc // cps,),
        in_specs=[blk, blk, blk, blk, _full_spec((1, RET_W))] + _head_specs(),
        out_specs=[blk, blk, pl.BlockSpec((N_HEAD, cps, HEAD_D, HEAD_D), lambda n: (0, n, 0, 0))],
        out_shape=[jax.ShapeDtypeStruct((L, RET_W), _F32), jax.ShapeDtypeStruct((L, RET_W), _BF),
                   jax.ShapeDtypeStruct((N_HEAD, nc, HEAD_D, HEAD_D), _F32)],
        scratch_shapes=[pltpu.VMEM((N_HEAD, HEAD_D, HEAD_D), _F32)],
        compiler_params=_params("arbitrary"),
    )(q, k, v, gate, ggn, *consts)


def _rows_to_segments(dst_scr, src_ref, seg):
    for g in range(dst_scr.shape[0]):
        for j in range(SUBLANES):
            dst_scr[g, pl.ds(j, seg, stride=SUBLANES), :] = src_ref[pl.ds(j * seg, seg), g * LANES:(g + 1) * LANES]


def _segments_to_rows(dst_ref, src_scr, seg):
    for g in range(src_scr.shape[0]):
        for j in range(SUBLANES):
            dst_ref[pl.ds(j * seg, seg), g * LANES:(g + 1) * LANES] = src_scr[g, pl.ds(j, seg, stride=SUBLANES), :].astype(dst_ref.dtype)


def _scan_segments(x_ref, tab_ref, pw_ref, carry_ref, seg, reverse, entry_ref=None, fwd_ref=None, fwd_entry_ref=None,
                   da_ref=None):
    G = x_ref.shape[0]
    W = KB_STATES
    re, im = pl.ds(0, W), pl.ds(W, W)
    row_id = lax.broadcasted_iota(jnp.int32, (SUBLANES, W), 0)
    edge_in = (row_id == SUBLANES - 1) if reverse else (row_id == 0)
    edge_out = 0 if reverse else SUBLANES - 1
    a_tab = [(tab_ref[g, 0], tab_ref[g, 1]) for g in range(G)]

    def local(i, st):
        r = (seg - 1 - i) if reverse else i
        out = []
        for g in range(G):
            (ar, ai), (sr, si) = a_tab[g], st[g]
            nr = ar * sr - ai * si + x_ref[g, r, :, re]
            ni = ar * si + ai * sr + x_ref[g, r, :, im]
            x_ref[g, r, :, re] = nr
            x_ref[g, r, :, im] = ni
            out.append((nr, ni))
        return tuple(out)

    zero = jnp.zeros((SUBLANES, W), _F32)
    ends = lax.fori_loop(0, seg, local, tuple((zero, zero) for _ in range(G)), unroll=SCAN_UNROLL)

    entry = []
    shift = (SUBLANES - 1) if reverse else 1
    for g in range(G):
        er, ei = ends[g]
        fr = jnp.where(edge_in, carry_ref[g, :, re], pltpu.roll(er, shift, 0))
        fi = jnp.where(edge_in, carry_ref[g, :, im], pltpu.roll(ei, shift, 0))
        for j, dist in enumerate((1, 2, 4)):
            pr, pi = tab_ref[g, 2 + 2 * j], tab_ref[g, 3 + 2 * j]
            sh = (SUBLANES - dist) if reverse else dist
            sr, si = pltpu.roll(fr, sh, 0), pltpu.roll(fi, sh, 0)
            fr, fi = fr + pr * sr - pi * si, fi + pr * si + pi * sr
        br, bi = tab_ref[g, 8], tab_ref[g, 9]
        outr = br * fr - bi * fi + er
        outi = br * fi + bi * fr + ei
        carry_ref[g, :, re] = jnp.broadcast_to(outr[edge_out:edge_out + 1, :], (SUBLANES, W))
        carry_ref[g, :, im] = jnp.broadcast_to(outi[edge_out:edge_out + 1, :], (SUBLANES, W))
        entry.append((fr, fi))
        if entry_ref is not None:
            entry_ref[g, :, re] = fr
            entry_ref[g, :, im] = fi

    add_da = da_ref is not None

    def fix(r, st):
        out = []
        for g in range(G):
            fr, fi = entry[g]
            pwr, pwi = pw_ref[g, r, :, re], pw_ref[g, r, :, im]
            xr = x_ref[g, r, :, re] + (pwr * fr - pwi * fi)
            xi = x_ref[g, r, :, im] + (pwr * fi + pwi * fr)
            x_ref[g, r, :, re] = xr
            x_ref[g, r, :, im] = xi
            if add_da:
                if r == 0:
                    xpr, xpi = fwd_entry_ref[g, :, re], fwd_entry_ref[g, :, im]
                else:
                    two = fwd_ref[g, pl.ds((r - 1) // 2 * 2 * SUBLANES, 2 * SUBLANES), :].astype(_F32)
                    half = two[(r - 1) % 2 * SUBLANES:((r - 1) % 2 + 1) * SUBLANES, :]
                    xpr, xpi = half[:, :W], half[:, W:]
                out.append((st[g][0] + (xr * xpr + xi * xpi), st[g][1] + (xi * xpr - xr * xpi)))
            else:
                out.append(st[g])
        return tuple(out)

    if add_da:
        st = tuple((zero, zero) for _ in range(G))
        for r in range(seg):
            st = fix(r, st)
        for g in range(G):
            da_ref[g, :, re] += st[g][0]
            da_ref[g, :, im] += st[g][1]
    else:
        lax.fori_loop(0, seg, fix, tuple((zero[0:1, 0:LANES],) for _ in range(G)), unroll=FIX_UNROLL)


def _s5_specs(seg, time=lambda t: t):
    G = KB_PER_STEP
    return dict(
        x=pl.BlockSpec((G, seg * SUBLANES, 2 * KB_STATES), lambda kb, t: (kb, time(t), 0)),
        ent=pl.BlockSpec((G, 1, SUBLANES, 2 * KB_STATES), lambda kb, t: (kb, time(t), 0, 0)),
        b=pl.BlockSpec((G, LANES, 2 * KB_STATES), lambda kb, t: (kb, 0, 0)),
        c=pl.BlockSpec((G, 2 * KB_STATES, LANES), lambda kb, t: (kb, 0, 0)),
        tab=pl.BlockSpec((G, 10, SUBLANES, KB_STATES), lambda kb, t: (kb, 0, 0, 0)),
        pw=pl.BlockSpec((G, seg, 1, 2 * KB_STATES), lambda kb, t: (kb, 0, 0, 0)),
        d=pl.BlockSpec((1, G * LANES), lambda kb, t: (0, kb)),
    )


def _s5_fwd(u, bmat, cmat, tab_f, pw_f, d_skip, tb):
    L = u.shape[0]
    nt = L // tb
    seg = tb // SUBLANES
    G = KB_PER_STEP
    ucol = pl.BlockSpec((tb, G * LANES), lambda kb, t: (t, kb))
    sp = _s5_specs(seg)

    def body(u_ref, b_ref, c_ref, tab_ref, pw_ref, d_ref, s_ref, x_ref, ent_ref, up_scr, y_scr, carry_scr, x_scr):
        @pl.when(pl.program_id(1) == 0)
        def _():
            carry_scr[...] = jnp.zeros_like(carry_scr)

        _rows_to_segments(up_scr, u_ref, seg)
        for g in range(G):
            x_scr[g] = _dot(up_scr[g].astype(_BF), b_ref[g]).reshape(seg, SUBLANES, 2 * KB_STATES)
        _scan_segments(x_scr, tab_ref, pw_ref, carry_scr, seg, reverse=False, entry_ref=ent_ref.at[:, 0])
        for g in range(G):
            xb = x_scr[g].reshape(tb, 2 * KB_STATES).astype(_BF)
            x_ref[g] = xb
            y = _dot(xb, c_ref[g])
            y_scr[g] = y + d_ref[:, g * LANES:(g + 1) * LANES] * up_scr[g]
        _segments_to_rows(s_ref, y_scr, seg)

    return pl.pallas_call(
        body, name="s5_fwd", grid=(N_KB // G, nt),
        in_specs=[ucol, sp["b"], sp["c"], sp["tab"], sp["pw"], sp["d"]],
        out_specs=[ucol, sp["x"], sp["ent"]],
        out_shape=[jax.ShapeDtypeStruct((L, SSM_W), _F32),
                   jax.ShapeDtypeStruct((N_KB, L, 2 * KB_STATES), _BF),
                   jax.ShapeDtypeStruct((N_KB, nt, SUBLANES, 2 * KB_STATES), _F32)],
        scratch_shapes=[pltpu.VMEM((G, tb, LANES), _F32)] * 2 + [pltpu.VMEM((G, SUBLANES, 2 * KB_STATES), _F32),
                                                                 pltpu.VMEM((G, seg, SUBLANES, 2 * KB_STATES), _F32)],
        compiler_params=_params("parallel", "arbitrary"),
    )(u, bmat, cmat, tab_f, pw_f, d_skip)


def _mixout_fwd(s, y_ret, x, w_glu, w_out, g2, tm):
    L = s.shape[0]

    def body(s_ref, yr_ref, x_ref, wg_ref, wo_ref, g_ref, ys_ref, cat_ref, mix_ref, x2_ref):
        for rows in _row_chunks(tm):
            ys = _gelu(s_ref[rows, :]).astype(_BF)
            ys_ref[rows, :] = ys
            glu = _dot(ys, wg_ref[...])
            cat_ref[rows, :RET_W] = yr_ref[rows, :]
            cat_ref[rows, RET_W:] = (glu[:, :SSM_W] * _sigmoid(glu[:, SSM_W:])).astype(_BF)
            mix = _dot(cat_ref[rows, :], wo_ref[...])
            mix_ref[rows, :] = mix
            x2_ref[rows, :] = x_ref[rows, :] + mix * _rms_r(mix) * g_ref[...]

    return pl.pallas_call(
        body, name="mixout_fwd", grid=(L // tm,),
        in_specs=[_row_spec(tm, SSM_W), _row_spec(tm, RET_W), _row_spec(tm, D_MODEL),
                  _weight_spec((SSM_W, 2 * SSM_W)), _weight_spec((D_MODEL, D_MODEL)), _full_spec((1, D_MODEL))],
        out_specs=[_row_spec(tm, SSM_W), _row_spec(tm, D_MODEL), _row_spec(tm, D_MODEL), _row_spec(tm, D_MODEL)],
        out_shape=[jax.ShapeDtypeStruct((L, SSM_W), _BF), jax.ShapeDtypeStruct((L, D_MODEL), _BF),
                   jax.ShapeDtypeStruct((L, D_MODEL), _F32), jax.ShapeDtypeStruct((L, D_MODEL), _F32)],
        compiler_params=_params("parallel"),
    )(s, y_ret, x, w_glu, w_out, g2)


FF1_COLS = D_FF // N_DEV


def _ff1_fwd(x2, g3, w1, tm):
    L = x2.shape[0]

    def body(x_ref, g_ref, w_ref, h_ref, a_ref):
        for rows in _row_chunks(tm):
            xv = x_ref[rows, :]
            h = (xv * _rms_r(xv) * g_ref[...]).astype(_BF)
            h_ref[rows, :] = h
            for j in range(N_DEV):
                cols = slice(j * FF1_COLS, (j + 1) * FF1_COLS)
                rl = jnp.maximum(_dot(h, w_ref[j]), 0.0)
                a_ref[rows, cols] = (rl * rl).astype(_BF)

    return pl.pallas_call(
        body, name="ff1_fwd", grid=(L // tm,),
        in_specs=[_row_spec(tm, D_MODEL), _full_spec((1, D_MODEL)), _weight_spec((N_DEV, D_MODEL, FF1_COLS))],
        out_specs=[_row_spec(tm, D_MODEL), _row_spec(tm, D_FF)],
        out_shape=[jax.ShapeDtypeStruct((L, D_MODEL), _BF), jax.ShapeDtypeStruct((L, D_FF), _BF)],
        compiler_params=_params("parallel"),
    )(x2, g3, w1)


def _ff2_loss(act, x2, tgt, g4, w2, tm):
    L = act.shape[0]

    def body(f_ref, x_ref, t_ref, g_ref, w_ref, dy_ref, dm_ref, dg_ref, ls_ref):
        @pl.when(pl.program_id(0) == 0)
        def _():
            dg_ref[...] = jnp.zeros_like(dg_ref)
            ls_ref[...] = jnp.zeros_like(ls_ref)

        g = g_ref[...]
        for rows in _row_chunks(tm):
            m = _dot(f_ref[rows, :], w_ref[...])
            y = x_ref[rows, :] + m * _rms_r(m) * g
            err = y - t_ref[rows, :]
            ls_ref[...] += jnp.sum(err * err, axis=0, keepdims=True)
            dy = err * (1.0 / D_MODEL)
            dy_ref[rows, :] = dy
            dm, dgr = _rms_bwd(m, g, dy)
            dm_ref[rows, :] = dm.astype(_BF)
            dg_ref[...] += jnp.sum(dgr, axis=0, keepdims=True)

    return pl.pallas_call(
        body, name="ff2_loss", grid=(L // tm,),
        in_specs=[_row_spec(tm, D_FF), _row_spec(tm, D_MODEL), _row_spec(tm, D_MODEL),
                  _full_spec((1, D_MODEL)), _weight_spec((D_FF, D_MODEL))],
        out_specs=[_row_spec(tm, D_MODEL), _row_spec(tm, D_MODEL), _full_spec((1, D_MODEL)), _full_spec((1, D_MODEL))],
        out_shape=[jax.ShapeDtypeStruct((L, D_MODEL), _F32), jax.ShapeDtypeStruct((L, D_MODEL), _BF),
                   jax.ShapeDtypeStruct((1, D_MODEL), _F32), jax.ShapeDtypeStruct((1, D_MODEL), _F32)],
        compiler_params=_params("arbitrary"),
    )(act, x2, tgt, g4, w2)


def _ff2_bwd(dm, act, w2, tm, tn):
    L = dm.shape[0]
    last = L // tm - 1

    def body(dm_ref, a_ref, w_ref, df_ref, dw_ref, acc):
        @pl.when(pl.program_id(1) == 0)
        def _():
            acc[...] = jnp.zeros_like(acc)

        dmv = dm_ref[...]
        av = a_ref[...]
        df_ref[...] = (_dot_nt(dmv, w_ref[...]) * jnp.sqrt(4.0 * av.astype(_F32))).astype(_BF)
        acc[...] += _dot_tn(av, dmv)

        @pl.when(pl.program_id(1) == last)
        def _():
            dw_ref[...] = acc[...].astype(_BF)

    return pl.pallas_call(
        body, name="ff2_bwd", grid=(D_FF // tn, L // tm),
        in_specs=[pl.BlockSpec((tm, D_MODEL), lambda j, i: (i, 0)), pl.BlockSpec((tm, tn), lambda j, i: (i, j)),
                  pl.BlockSpec((tn, D_MODEL), lambda j, i: (j, 0))],
        out_specs=[pl.BlockSpec((tm, tn), lambda j, i: (i, j)), pl.BlockSpec((tn, D_MODEL), lambda j, i: (j, 0))],
        out_shape=[jax.ShapeDtypeStruct((L, D_FF), _BF), jax.ShapeDtypeStruct((D_FF, D_MODEL), _BF)],
        scratch_shapes=[pltpu.VMEM((tn, D_MODEL), _F32)],
        compiler_params=_params("parallel", "arbitrary"),
    )(dm, act, w2)


def _ff1_bwd(df1, w1, x2, mix, dy, g3, g2, tm):
    L = df1.shape[0]

    def body(df_ref, w_ref, x2_ref, mix_ref, dy_ref, g3_ref, g2_ref, dx2_ref, dmix_ref, dg3_ref, dg2_ref):
        @pl.when(pl.program_id(0) == 0)
        def _():
            dg3_ref[...] = jnp.zeros_like(dg3_ref)
            dg2_ref[...] = jnp.zeros_like(dg2_ref)

        for rows in _row_chunks(tm):
            dh = _dot_nt(df_ref[rows, 0:FF1_COLS], w_ref[0])
            for j in range(1, N_DEV):
                dh = dh + _dot_nt(df_ref[rows, j * FF1_COLS:(j + 1) * FF1_COLS], w_ref[j])
            dz, dgr = _rms_bwd(x2_ref[rows, :], g3_ref[...], dh)
            dg3_ref[...] += jnp.sum(dgr, axis=0, keepdims=True)
            dx2 = dy_ref[rows, :] + dz
            dx2_ref[rows, :] = dx2
            dmx, dgr2 = _rms_bwd(mix_ref[rows, :], g2_ref[...], dx2)
            dg2_ref[...] += jnp.sum(dgr2, axis=0, keepdims=True)
            dmix_ref[rows, :] = dmx.astype(_BF)

    vec = _full_spec((1, D_MODEL))
    return pl.pallas_call(
        body, name="ff1_bwd", grid=(L // tm,),
        in_specs=[_row_spec(tm, D_FF), _weight_spec((N_DEV, D_MODEL, FF1_COLS)), _row_spec(tm, D_MODEL),
                  _row_spec(tm, D_MODEL), _row_spec(tm, D_MODEL), vec, vec],
        out_specs=[_row_spec(tm, D_MODEL), _row_spec(tm, D_MODEL), vec, vec],
        out_shape=[jax.ShapeDtypeStruct((L, D_MODEL), _F32), jax.ShapeDtypeStruct((L, D_MODEL), _BF),
                   jax.ShapeDtypeStruct((1, D_MODEL), _F32), jax.ShapeDtypeStruct((1, D_MODEL), _F32)],
        compiler_params=_params("arbitrary"),
    )(df1, w1, x2, mix, dy, g3, g2)


def _matmul_tn(a, b, tm, tn, name, slots=0):
    L, K = a.shape
    N = b.shape[1]
    last = L // tm - 1

    def body(a_ref, b_ref, o_ref, acc):
        @pl.when(pl.program_id(1) == 0)
        def _():
            acc[...] = jnp.zeros_like(acc)

        acc[...] += _dot_tn(a_ref[...].astype(_BF), b_ref[...].astype(_BF))

        @pl.when(pl.program_id(1) == last)
        def _():
            if slots:
                for c in range(tn // slots):
                    o_ref[c] = acc[:, c * slots:(c + 1) * slots].astype(_BF)
            else:
                o_ref[...] = acc[...].astype(_BF)

    if slots:
        out_spec = pl.BlockSpec((tn // slots, K, slots), lambda j, i: (j, 0, 0))
        out_shape = jax.ShapeDtypeStruct((N // slots, K, slots), _BF)
    else:
        out_spec = pl.BlockSpec((K, tn), lambda j, i: (0, j))
        out_shape = jax.ShapeDtypeStruct((K, N), _BF)
    return pl.pallas_call(
        body, name=name, grid=(N // tn, L // tm),
        in_specs=[pl.BlockSpec((tm, K), lambda j, i: (i, 0)), pl.BlockSpec((tm, tn), lambda j, i: (i, j))],
        out_specs=out_spec, out_shape=out_shape,
        scratch_shapes=[pltpu.VMEM((K, tn), _F32)],
        compiler_params=_params("parallel", "arbitrary"),
    )(a, b)


def _dw_in_t(pieces, h, tk):
    L = h.shape[0]
    last = L // tk - 1

    def body(p0, p1, p2, p3, p4, h_ref, o_ref, acc):
        @pl.when(pl.program_id(0) == 0)
        def _():
            acc[...] = jnp.zeros_like(acc)

        hv = h_ref[...]
        for j, p in enumerate((p0, p1, p2, p3, p4)):
            acc[j * RET_W:(j + 1) * RET_W, :] += _dot_tn(p[...].astype(_BF), hv)

        @pl.when(pl.program_id(0) == last)
        def _():
            o_ref[...] = acc[...].astype(_BF)

    return pl.pallas_call(
        body, name="dw_in", grid=(L // tk,),
        in_specs=[_row_spec(tk, RET_W)] * 5 + [_row_spec(tk, D_MODEL)],
        out_specs=_full_spec((IN_COLS, D_MODEL)), out_shape=jax.ShapeDtypeStruct((IN_COLS, D_MODEL), _BF),
        scratch_shapes=[pltpu.VMEM((IN_COLS, D_MODEL), _F32)],
        compiler_params=_params("arbitrary"),
    )(*pieces, h)


def _mixout_bwd(dmix, w_out, w_glu, ys, s, o, gate, ggn, tm, after=()):
    L = dmix.shape[0]

    def body(dmix_ref, wo_ref, wg_ref, ys_ref, s_ref, o_ref, gate_ref, ggn_ref,
             dglu_ref, ds_ref, dgate_ref, do_ref, dggn_ref):
        @pl.when(pl.program_id(0) == 0)
        def _():
            dggn_ref[...] = jnp.zeros_like(dggn_ref)

        ggn = ggn_ref[...]
        for rows in _row_chunks(tm):
            dcat = _dot_nt(dmix_ref[rows, :], wo_ref[...])
            dy_ret, dy_ssm = dcat[:, :RET_W], dcat[:, RET_W:]
            glu = _dot(ys_ref[rows, :], wg_ref[...])
            ga, sg = glu[:, :SSM_W], _sigmoid(glu[:, SSM_W:])
            dga = (dy_ssm * sg).astype(_BF)
            dgb = (dy_ssm * ga * sg * (1.0 - sg)).astype(_BF)
            dglu_ref[rows, :SSM_W] = dga
            dglu_ref[rows, SSM_W:] = dgb
            dys = _dot_nt(dga, wg_ref[:, :SSM_W]) + _dot_nt(dgb, wg_ref[:, SSM_W:])
            ds_ref[rows, :] = dys * _gelu_grad(s_ref[rows, :])
            gt = gate_ref[rows, :]
            sgt = _sigmoid(gt)
            for hh in range(N_HEAD):
                cols = slice(hh * HEAD_D, (hh + 1) * HEAD_D)
                ov = o_ref[rows, cols]
                dlt = ov - jnp.mean(ov, axis=-1, keepdims=True)
                rstd = lax.rsqrt(jnp.mean(dlt * dlt, axis=-1, keepdims=True) + NORM_EPS)
                on = dlt * rstd
                dyr = dy_ret[:, cols] * (gt[:, cols] * sgt[:, cols])
                dgate_ref[rows, cols] = (dy_ret[:, cols] * (on * ggn[:, cols]) * (sgt[:, cols] * (1.0 + gt[:, cols] * (1.0 - sgt[:, cols])))).astype(_BF)
                dggn_ref[:, cols] += jnp.sum(dyr * on, axis=0, keepdims=True)
                don = dyr * ggn[:, cols]
                do = rstd * (don - jnp.mean(don, axis=-1, keepdims=True) - on * jnp.mean(don * on, axis=-1, keepdims=True))
                do_ref[rows, cols] = do.astype(_BF)

    body, in_specs, operands = _ordered(
        body, [_row_spec(tm, D_MODEL), _weight_spec((D_MODEL, D_MODEL)), _weight_spec((SSM_W, 2 * SSM_W)),
               _row_spec(tm, SSM_W), _row_spec(tm, SSM_W), _row_spec(tm, RET_W), _row_spec(tm, RET_W),
               _full_spec((1, RET_W))], (dmix, w_out, w_glu, ys, s, o, gate, ggn), after)
    return pl.pallas_call(
        body, name="mixout_bwd", grid=(L // tm,),
        in_specs=in_specs,
        out_specs=[_row_spec(tm, 2 * SSM_W), _row_spec(tm, SSM_W), _row_spec(tm, RET_W), _row_spec(tm, RET_W),
                   _full_spec((1, RET_W))],
        out_shape=[jax.ShapeDtypeStruct((L, 2 * SSM_W), _BF), jax.ShapeDtypeStruct((L, SSM_W), _F32),
                   jax.ShapeDtypeStruct((L, RET_W), _BF), jax.ShapeDtypeStruct((L, RET_W), _BF),
                   jax.ShapeDtypeStruct((1, RET_W), _F32)],
        compiler_params=_params("arbitrary"),
    )(*operands)


def _s5_bwd(u, ds, xs, ent, bmat, cmat, tab_r, pw_r, d_skip, tb, after=()):
    L = u.shape[0]
    nt = L // tb
    seg = tb // SUBLANES
    G = KB_PER_STEP
    rcol = pl.BlockSpec((tb, G * LANES), lambda kb, t: (nt - 1 - t, kb))
    sp = _s5_specs(seg, time=lambda t: nt - 1 - t)
    aspec = pl.BlockSpec((G, SUBLANES, 2 * KB_STATES), lambda kb, t: (kb, 0, 0))

    def body(u_ref, ds_ref, x_ref, ent_ref, b_ref, c_ref, tr_ref, pr_ref, d_ref,
             du_ref, db_ref, dc_ref, da_ref, dd_ref, up_scr, dp_scr, g_scr, lc_scr):
        @pl.when(pl.program_id(1) == 0)
        def _():
            lc_scr[...] = jnp.zeros_like(lc_scr)
            db_ref[...] = jnp.zeros_like(db_ref)
            dc_ref[...] = jnp.zeros_like(dc_ref)
            da_ref[...] = jnp.zeros_like(da_ref)
            dd_ref[...] = jnp.zeros_like(dd_ref)

        _rows_to_segments(up_scr, u_ref, seg)
        _rows_to_segments(dp_scr, ds_ref, seg)
        for g in range(G):
            g_scr[g] = _dot_nt(dp_scr[g].astype(_BF), c_ref[g]).reshape(seg, SUBLANES, 2 * KB_STATES)
        _scan_segments(g_scr, tr_ref, pr_ref, lc_scr, seg, reverse=True, fwd_ref=x_ref, fwd_entry_ref=ent_ref.at[:, 0],
                       da_ref=da_ref)
        for g in range(G):
            cols = slice(g * LANES, (g + 1) * LANES)
            uv, dsv = up_scr[g], dp_scr[g]
            ub, dsb = uv.astype(_BF), dsv.astype(_BF)
            lamb = g_scr[g].reshape(tb, 2 * KB_STATES).astype(_BF)
            db_ref[g] += _dot_tn(ub, lamb)
            dc_ref[g] += _dot_tn(dsb, x_ref[g])
            dd_ref[:, cols] += jnp.sum(dsv * uv, axis=0, keepdims=True)
            up_scr[g] = _dot_nt(lamb, b_ref[g]) + d_ref[:, cols] * dsv
        _segments_to_rows(du_ref, up_scr, seg)

    body, in_specs, operands = _ordered(
        body, [rcol, rcol, sp["x"], sp["ent"], sp["b"], sp["c"], sp["tab"], sp["pw"], sp["d"]],
        (u, ds, xs, ent, bmat, cmat, tab_r, pw_r, d_skip), after)
    return pl.pallas_call(
        body, name="s5_bwd", grid=(N_KB // G, nt),
        in_specs=in_specs,
        out_specs=[rcol, sp["b"], sp["b"], aspec, sp["d"]],
        out_shape=[jax.ShapeDtypeStruct((L, SSM_W), _BF),
                   jax.ShapeDtypeStruct((N_KB, LANES, 2 * KB_STATES), _F32),
                   jax.ShapeDtypeStruct((N_KB, LANES, 2 * KB_STATES), _F32),
                   jax.ShapeDtypeStruct((N_KB, SUBLANES, 2 * KB_STATES), _F32),
                   jax.ShapeDtypeStruct((1, SSM_W), _F32)],
        scratch_shapes=[pltpu.VMEM((G, tb, LANES), _F32)] * 2
        + [pltpu.VMEM((G, seg, SUBLANES, 2 * KB_STATES), _F32), pltpu.VMEM((G, SUBLANES, 2 * KB_STATES), _F32)],
        compiler_params=_params("parallel", "arbitrary"),
    )(*operands)


def _retention_bwd(q, k, v, do, r_prev, consts, cosf, sinf, after=()):
    L = q.shape[0]
    nc = L // CHUNK
    cps = math.gcd(RET_STEP_CHUNKS, nc)
    nb = nc // cps
    blk = pl.BlockSpec((cps * CHUNK, RET_W), lambda n: (nb - 1 - n, 0))
    rope_blk = pl.BlockSpec((cps * CHUNK, HEAD_D), lambda n: (nb - 1 - n, 0))

    def body(q_ref, k_ref, v_ref, do_ref, rp_ref, dm_ref, xi_ref, zeta_ref, gc_ref, cos_ref, sin_ref,
             dq_ref, dk_ref, dv_ref, g_scr):
        @pl.when(pl.program_id(0) == 0)
        def _():
            g_scr[...] = jnp.zeros_like(g_scr)

        for hh in range(N_HEAD):
            cols = slice(hh * HEAD_D, (hh + 1) * HEAD_D)
            dm, zeta = dm_ref[hh], zeta_ref[hh]
            gst = g_scr[hh]
            for c in reversed(range(cps)):
                rows = slice(c * CHUNK, (c + 1) * CHUNK)
                qv, kv, vv, dov = q_ref[rows, cols], k_ref[rows, cols], v_ref[rows, cols], do_ref[rows, cols]
                rb = rp_ref[hh, c].astype(_BF)
                gb = gst.astype(_BF)
                sb = (_dot_nt(qv, kv) * dm).astype(_BF)
                dab = (_dot_nt(dov, vv) * dm).astype(_BF)
                dox = (dov.astype(_F32) * xi_ref[hh]).astype(_BF)
                vz = (vv.astype(_F32) * zeta).astype(_BF)
                dq = _dot(dab, kv) + _dot_nt(dox, rb)
                dk = _dot_tn(dab, qv) + _dot_nt(vz, gb)
                dv = _dot_tn(sb, dov) + _dot(kv, gb) * zeta
                gst = gc_ref[hh, 0:1, :] * gst + _dot_tn(qv, dox)
                cs, sn = cos_ref[rows, :], sin_ref[rows, :]
                dq_ref[rows, cols] = _rope_t(dq, cs, sn).astype(_BF)
                dk_ref[rows, cols] = (_rope_t(dk, cs, sn) * (HEAD_D ** -0.5)).astype(_BF)
                dv_ref[rows, cols] = dv.astype(_BF)
            g_scr[hh] = gst

    body, in_specs, operands = _ordered(
        body, [blk, blk, blk, blk, pl.BlockSpec((N_HEAD, cps, HEAD_D, HEAD_D), lambda n: (0, nb - 1 - n, 0, 0))]
        + _head_specs() + [rope_blk, rope_blk], (q, k, v, do, r_prev, *consts, cosf, sinf), after)
    return pl.pallas_call(
        body, name="retention_bwd", grid=(nb,),
        in_specs=in_specs,
        out_specs=[blk, blk, blk],
        out_shape=[jax.ShapeDtypeStruct((L, RET_W), _BF)] * 3,
        scratch_shapes=[pltpu.VMEM((N_HEAD, HEAD_D, HEAD_D), _F32)],
        compiler_params=_params("arbitrary"),
    )(*operands)


def _inproj_bwd(pieces, w_in_t, x, dx2, g1, tm, after=()):
    L = x.shape[0]

    def body(p0, p1, p2, p3, p4, w_ref, x_ref, dx2_ref, g_ref, dx_ref, dg_ref):
        @pl.when(pl.program_id(0) == 0)
        def _():
            dg_ref[...] = jnp.zeros_like(dg_ref)

        for rows in _row_chunks(tm):
            dh = None
            for j, p in enumerate((p0, p1, p2, p3, p4)):
                part = _dot(p[rows, :].astype(_BF), w_ref[j * RET_W:(j + 1) * RET_W, :])
                dh = part if dh is None else dh + part
            dz, dgr = _rms_bwd(x_ref[rows, :], g_ref[...], dh)
            dx_ref[rows, :] = dx2_ref[rows, :] + dz
            dg_ref[...] += jnp.sum(dgr, axis=0, keepdims=True)

    body, in_specs, operands = _ordered(
        body, [_row_spec(tm, RET_W)] * 5 + [_weight_spec((IN_COLS, D_MODEL)), _row_spec(tm, D_MODEL),
                                             _row_spec(tm, D_MODEL), _full_spec((1, D_MODEL))],
        (*pieces, w_in_t, x, dx2, g1), after)
    return pl.pallas_call(
        body, name="inproj_bwd", grid=(L // tm,),
        in_specs=in_specs,
        out_specs=[_row_spec(tm, D_MODEL), _full_spec((1, D_MODEL))],
        out_shape=[jax.ShapeDtypeStruct((L, D_MODEL), _F32), jax.ShapeDtypeStruct((1, D_MODEL), _F32)],
        compiler_params=_params("arbitrary"),
    )(*operands)


def _sum_adamw(parts, w, m, v, tr, name):
    _, R, Cc = parts.shape

    def body(p_ref, w_ref, m_ref, v_ref, g_ref, d_ref, nm_ref, nv_ref):
        gv = p_ref[0].astype(_F32)
        for s in range(1, N_DEV):
            gv = gv + p_ref[s].astype(_F32)
        g_ref[...] = gv
        nm = ADAM_B1 * m_ref[...] + (1.0 - ADAM_B1) * gv
        nv = ADAM_B2 * v_ref[...] + (1.0 - ADAM_B2) * (gv * gv)
        m_hat = nm / (1.0 - ADAM_B1 ** ADAM_STEP)
        v_hat = nv / (1.0 - ADAM_B2 ** ADAM_STEP)
        d_ref[...] = -ADAM_LR * (m_hat / (jnp.sqrt(v_hat) + ADAM_EPS) + ADAM_WD * w_ref[...])
        nm_ref[...] = nm
        nv_ref[...] = nv

    spec = _row_spec(tr, Cc)
    return pl.pallas_call(
        body, name=name, grid=(R // tr,),
        in_specs=[pl.BlockSpec((N_DEV, tr, Cc), lambda i: (0, i, 0))] + [spec] * 3, out_specs=[spec] * 4,
        out_shape=[jax.ShapeDtypeStruct((R, Cc), _F32)] * 4,
        compiler_params=_params("parallel"),
    )(parts, w, m, v)


def _my_place():
    return lax.axis_index("x"), lax.axis_index("y"), lax.axis_index("c")


HBM_SPEC = pl.BlockSpec(memory_space=pltpu.HBM)
SEM_SPEC = pl.BlockSpec(memory_space=pltpu.SEMAPHORE)
DATAFLOW = pltpu.SideEffectType.DATAFLOW_SIDE_EFFECTING


def _my_index():
    x, y, c = _my_place()
    return 4 * x + 2 * y + c


def _landing(own_block):
    zone = lax.empty((N_DEV,) + own_block.shape, own_block.dtype)
    return lax.dynamic_update_index_in_dim(zone, own_block, _my_index(), 0)


def _split_copies(src_refs, land_refs, send_sems, recv_sems, gather, first=0):
    x, y, c = _my_place()
    me = 4 * x + 2 * y + c
    copies = []
    for a, (src, land) in enumerate(zip(src_refs, land_refs)):
        for kk in range(1, N_DEV):
            px, py, pc = x ^ (kk >> 2), y ^ ((kk >> 1) & 1), c ^ (kk & 1)
            peer = 4 * px + 2 * py + pc
            copies.append(pltpu.make_async_remote_copy(
                src_ref=src if gather else src.at[peer], dst_ref=land.at[me],
                send_sem=send_sems.at[(first + a) * 7 + kk - 1], recv_sem=recv_sems.at[(first + a) * 7 + kk - 1],
                device_id=(px, py, pc), device_id_type=MESH))
    return copies


def _split_start(srcs, lands, gather, name):
    n = len(srcs)

    def body(*refs):
        src_refs, land_refs = refs[:n], refs[n:2 * n]
        send_sems, recv_sems = refs[2 * n], refs[2 * n + 1]
        token = refs[-1]
        for cp in _split_copies(src_refs, land_refs, send_sems, recv_sems, gather):
            cp.start()
        token[...] = jnp.zeros_like(token)

    outs = pl.pallas_call(
        body, name=name,
        out_shape=(pltpu.SemaphoreType.DMA((7 * n,)), pltpu.SemaphoreType.DMA((7 * n,)),
                   *[pltpu.HBM(t.shape, t.dtype) for t in srcs], *[pltpu.HBM(t.shape, t.dtype) for t in lands],
                   jax.ShapeDtypeStruct((SUBLANES, LANES), _F32)),
        in_specs=[HBM_SPEC] * (2 * n),
        out_specs=(SEM_SPEC, SEM_SPEC, *[HBM_SPEC] * (2 * n), pl.BlockSpec(memory_space=pltpu.VMEM)),
        input_output_aliases={i: 2 + i for i in range(2 * n)},
        compiler_params=pltpu.CompilerParams(has_side_effects=DATAFLOW),
    )(*[pltpu.with_memory_space_constraint(t, pltpu.HBM) for t in list(srcs) + list(lands)])
    return outs[0], outs[1], outs[2:2 + n], outs[2 + n:2 + 2 * n], outs[-1]


def _split_wait(send_sems, recv_sems, srcs, lands, after, gather, name, first=0):
    n = len(srcs)

    def body(*refs):
        src_refs, land_refs = refs[:n], refs[n:2 * n]
        send_s, recv_s = refs[2 * n], refs[2 * n + 1]
        for cp in _split_copies(src_refs, land_refs, send_s, recv_s, gather, first):
            cp.wait_send()
            cp.wait_recv()

    outs = pl.pallas_call(
        body, name=name,
        out_shape=tuple(pltpu.HBM(t.shape, t.dtype) for t in list(srcs) + list(lands)),
        in_specs=[HBM_SPEC] * (2 * n) + [SEM_SPEC, SEM_SPEC, pl.BlockSpec(memory_space=pl.ANY)],
        out_specs=tuple([HBM_SPEC] * (2 * n)),
        input_output_aliases={i: i for i in range(2 * n)},
        compiler_params=pltpu.CompilerParams(has_side_effects=DATAFLOW),
    )(*srcs, *lands, send_sems, recv_sems, after)
    return outs[n:]


def _discretize(lam_re, lam_im, log_dt, b_re, b_im):
    lr = jnp.minimum(lam_re, -1e-4)
    li = lam_im
    dt = jnp.exp(log_dt)[:, None]
    er = jnp.exp(lr * dt)
    ar, ai = er * jnp.cos(li * dt), er * jnp.sin(li * dt)
    den = lr * lr + li * li
    cr = ((ar - 1.0) * lr + ai * li) / den
    ci = (ai * lr - (ar - 1.0) * li) / den
    bbr = cr[:, :, None] * b_re - ci[:, :, None] * b_im
    bbi = cr[:, :, None] * b_im + ci[:, :, None] * b_re
    return ar, ai, bbr, bbi


def _cmul(ar, ai, br, bi):
    return ar * br - ai * bi, ar * bi + ai * br


def _cpowers(ar, ai, n):
    pr, pi = ar[None], ai[None]
    while pr.shape[0] < n:
        nr, ni = _cmul(pr, pi, pr[-1][None], pi[-1][None])
        pr, pi = jnp.concatenate([pr, nr]), jnp.concatenate([pi, ni])
    return pr[:n], pi[:n]


def _scan_tables(ar, ai, seg, reverse):
    if reverse:
        ai = -ai
    ar, ai = ar.reshape(N_KB, KB_STATES), ai.reshape(N_KB, KB_STATES)
    pr, pi = _cpowers(ar, ai, seg)
    a1 = (pr[-1], pi[-1])
    a2 = _cmul(*a1, *a1)
    a4 = _cmul(*a2, *a2)
    row = jnp.arange(SUBLANES)[None, :, None]
    wide = lambda t: jnp.broadcast_to(t[:, None, :], (N_KB, SUBLANES, KB_STATES))
    tabs = [wide(ar), wide(ai)]
    for dist, (qr, qi) in ((1, a1), (2, a2), (4, a4)):
        keep = (row < SUBLANES - dist) if reverse else (row >= dist)
        tabs += [jnp.where(keep, wide(qr), 0.0), jnp.where(keep, wide(qi), 0.0)]
    tabs += [wide(a1[0]), wide(a1[1])]
    if reverse:
        pr, pi = pr[::-1], pi[::-1]
    pw = jnp.transpose(jnp.concatenate([pr, pi], axis=-1), (1, 0, 2))[:, :, None, :]
    return jnp.stack(tabs, axis=1).astype(_F32), pw.astype(_F32)


def _block_diag_in(br, bi):
    eye = jnp.eye(GROUPS_PER_KB, dtype=_F32)
    one = lambda t: jnp.einsum("kgpc,gh->kgchp", t.reshape(N_KB, GROUPS_PER_KB, N_STATE, SSM_GC), eye).reshape(
        N_KB, LANES, KB_STATES)
    return jnp.concatenate([one(br), one(bi)], axis=-1)


def _block_diag_in_t(dmat):
    d6 = dmat.reshape(N_KB, GROUPS_PER_KB, SSM_GC, 2, GROUPS_PER_KB, N_STATE)
    eye = jnp.eye(GROUPS_PER_KB, dtype=_F32)
    both = jnp.einsum("kgcrhp,gh->rkgpc", d6, eye).reshape(2, N_GROUP, N_STATE, SSM_GC)
    return both[0], both[1]


def _block_diag_out(c_re, c_im):
    eye = jnp.eye(GROUPS_PER_KB, dtype=_F32)
    one = lambda t: jnp.einsum("kgcp,gh->khpgc", t.reshape(N_KB, GROUPS_PER_KB, SSM_GC, N_STATE), eye).reshape(
        N_KB, KB_STATES, LANES)
    return jnp.concatenate([one(c_re), -one(c_im)], axis=1)


def _block_diag_out_t(dmat_t):
    d6 = dmat_t.reshape(N_KB, GROUPS_PER_KB, SSM_GC, 2, GROUPS_PER_KB, N_STATE)
    eye = jnp.eye(GROUPS_PER_KB, dtype=_F32)
    both = jnp.einsum("kgcrhp,gh->rkgcp", d6, eye).reshape(2, N_GROUP, SSM_GC, N_STATE)
    return both[0], -both[1]


SMALL_NAMES = ("norm_mix_pre", "norm_mix_post", "ret_gn_gain", "ssm_lambda_re", "ssm_lambda_im", "ssm_log_dt",
               "ssm_b_re", "ssm_b_im", "ssm_c_re", "ssm_c_im", "ssm_d", "norm_mlp_pre", "norm_mlp_post")


def _local_grads(x, tgt, small, weights, emit, emit_small, tm, tk, tb, zero=0.0):
    L = x.shape[0]
    g1, g2, ggn = small["norm_mix_pre"], small["norm_mix_post"], small["ret_gn_gain"]
    g3, g4, d_skip = small["norm_mlp_pre"], small["norm_mlp_post"], small["ssm_d"]

    rope = _rope_tables(L)
    consts = _ret_consts()

    disc_in = (small["ssm_lambda_re"][0], small["ssm_lambda_im"][0], small["ssm_log_dt"][0] + zero,
               small["ssm_b_re"][0], small["ssm_b_im"][0])
    (ar, ai, bbr, bbi), disc_vjp = jax.vjp(_discretize, *disc_in)
    bmat = _block_diag_in(bbr, bbi).astype(_BF)
    cmat = _block_diag_out(small["ssm_c_re"][0], small["ssm_c_im"][0]).astype(_BF)
    seg = tb // SUBLANES
    tab_f, pw_f = _scan_tables(ar, ai, seg, False)
    tab_r, pw_r = _scan_tables(ar, ai, seg, True)

    h1 = _prenorm(x, g1, min(4 * tm, L), after=(pw_r,))
    (w_in_t,) = weights("in", h1)
    q, k, v, gate, u, cosf, sinf = _inproj_fwd(h1, w_in_t, rope, min(4 * tm, L))
    o, y_ret, r_prev = _retention_fwd(q, k, v, gate, ggn, consts)
    s, xs, ent = _s5_fwd(u, bmat, cmat, tab_f, pw_f, d_skip, tb)
    w_glu, w_out = weights("mix", s)
    ys, cat, mix, x2 = _mixout_fwd(s, y_ret, x, w_glu, w_out, g2, min(2 * tm, L))
    w_ff1, w_ff2 = weights("mlp", x2)
    h3, act = _ff1_fwd(x2, g3, w_ff1, min(2 * tm, L))
    dy, dm, dg4, sq = _ff2_loss(act, x2, tgt, g4, w_ff2, min(2 * tm, L))

    df1, dw_ff2 = _ff2_bwd(dm, act, w_ff2, min(1024, L), 1024)
    dx2, dmix, dg3, dg2 = _ff1_bwd(df1, w_ff1, x2, mix, dy, g3, g2, min(2 * tm, L))
    dw_ff1 = _matmul_tn(h3, df1, tk, 2 * FF1_COLS, "dw_ff1", slots=FF1_COLS)
    token = emit({"w_ff1": dw_ff1, "w_ff2": dw_ff2})
    dglu, ds, dgate, do, dggn = _mixout_bwd(dmix, w_out, w_glu, ys, s, o, gate, ggn, min(2 * tm, L), after=token)
    dw_out = _matmul_tn(cat, dmix, tk, 1024, "dw_out")
    dw_glu = _matmul_tn(ys, dglu, tk, 1024, "dw_glu")
    token = emit({"w_glu": dw_glu, "w_out": dw_out})
    du, dbmat, dcmat, da8, dd = _s5_bwd(u, ds, xs, ent, bmat, cmat, tab_r, pw_r, d_skip, tb, after=token)

    da = jnp.sum(da8, axis=1)
    dar = da[:, :KB_STATES].reshape(N_GROUP, N_STATE)
    dai = da[:, KB_STATES:].reshape(N_GROUP, N_STATE)
    dbr, dbi = _block_diag_in_t(dbmat)
    dlre, dlim, dldt, dbre, dbim = disc_vjp((dar, dai, dbr, dbi))
    dcre, dcim = _block_diag_out_t(dcmat)
    token = emit_small({
        "norm_mix_post": dg2, "ret_gn_gain": dggn,
        "ssm_lambda_re": dlre[None], "ssm_lambda_im": dlim[None], "ssm_log_dt": dldt[None],
        "ssm_b_re": dbre[None], "ssm_b_im": dbim[None], "ssm_c_re": dcre[None], "ssm_c_im": dcim[None],
        "ssm_d": dd, "norm_mlp_pre": dg3, "norm_mlp_post": dg4,
    }, sq)

    dq, dk, dv = _retention_bwd(q, k, v, do, r_prev, consts, cosf, sinf, after=token)
    pieces = (dq, dk, dv, dgate, du)
    dw_in_t = _dw_in_t(pieces, h1, min(1024, L))
    token = emit({"w_in": dw_in_t})
    gx, dg1 = _inproj_bwd(pieces, w_in_t, x, dx2, g1, min(2 * tm, L), after=token)
    return gx, dg1


BIG_SHAPES = {"w_in": (D_MODEL, IN_COLS // N_DEV), "w_glu": (SSM_W, 2 * SSM_W // N_DEV), "w_out": (D_MODEL // N_DEV, D_MODEL),
              "w_ff1": (D_MODEL, FF1_COLS), "w_ff2": (D_FF // N_DEV, D_MODEL)}
BIG_NAMES = ("w_in", "w_glu", "w_out", "w_ff1", "w_ff2")


def _cols_from_slots(g):
    return jnp.transpose(g, (1, 0, 2)).reshape(g.shape[1], N_DEV * g.shape[2])


def _cols_to_slots(dw):
    r, cols = dw.shape
    return jnp.transpose(dw.reshape(r, N_DEV, cols // N_DEV), (1, 0, 2))


WEIGHT_GROUPS = {"in": ("w_in",), "mix": ("w_glu", "w_out"), "mlp": ("w_ff1", "w_ff2")}


def _weight_from_slots(name, g):
    if name == "w_glu":
        return _cols_from_slots(g)
    if name == "w_ff1":
        return g
    return g.reshape(N_DEV * g.shape[1], g.shape[2])


def _grad_slots(name, dw):
    if name == "w_glu":
        return _cols_to_slots(dw)
    if name == "w_ff1":
        return dw
    if name == "w_in":
        return dw.reshape(N_DEV, BIG_SHAPES[name][1], BIG_SHAPES[name][0])
    return dw.reshape((N_DEV,) + BIG_SHAPES[name])


PIECE_ROWS = 8


VEC_NAMES = tuple(n for n in SMALL_NAMES if n[:6] not in ("ssm_b_", "ssm_c_"))
BC_NAMES = ("ssm_b_re", "ssm_b_im", "ssm_c_re", "ssm_c_im")
BC_ROWS = N_GROUP * SSM_GC


def _bc_view(name, t):
    t = t[0]
    if name.startswith("ssm_b_"):
        t = jnp.swapaxes(t, 1, 2)
    return t.reshape(BC_ROWS, N_STATE)


def _bc_unview(name, t):
    t = t.reshape(N_GROUP, SSM_GC, N_STATE)
    if name.startswith("ssm_b_"):
        t = jnp.swapaxes(t, 1, 2)
    return t[None]


def _pack_bc(vals):
    return jnp.concatenate([_bc_view(n, vals[n]).astype(_F32) for n in BC_NAMES], axis=0)


def _unpack_bc(buf):
    return {n: _bc_unview(n, buf[j * BC_ROWS:(j + 1) * BC_ROWS]) for j, n in enumerate(BC_NAMES)}


def _small_layout(shapes):
    off, rows = {}, 0
    for n in VEC_NAMES:
        off[n] = rows
        rows += -(-math.prod(shapes[n]) // (PIECE_ROWS * LANES)) * PIECE_ROWS
    return off, rows, rows + PIECE_ROWS


def _pack_small(vals, shapes, last=None):
    parts = []
    for n in VEC_NAMES:
        flat = vals[n].reshape(-1).astype(_F32)
        pad = -flat.shape[0] % (PIECE_ROWS * LANES)
        if pad:
            flat = jnp.concatenate([flat, jnp.zeros((pad,), _F32)])
        parts.append(flat.reshape(-1, LANES))
    parts.append(jnp.zeros((PIECE_ROWS, LANES), _F32) if last is None else last)
    return jnp.concatenate(parts, axis=0)


def _unpack_small(buf, shapes):
    off, _, _ = _small_layout(shapes)
    out = {}
    for n in VEC_NAMES:
        size = math.prod(shapes[n])
        rows = -(-size // LANES)
        out[n] = buf[off[n]:off[n] + rows].reshape(-1)[:size].reshape(shapes[n])
    return out


WEIGHT_NAMES = ('norm_mix_pre', 'norm_mix_post', 'w_in', 'ret_gn_gain', 'ssm_lambda_re', 'ssm_lambda_im', 'ssm_log_dt',
                'ssm_b_re', 'ssm_b_im', 'ssm_c_re', 'ssm_c_im', 'ssm_d', 'w_glu', 'w_out', 'norm_mlp_pre',
                'norm_mlp_post', 'w_ff1', 'w_ff2')


def kernel(x, norm_mix_pre, norm_mix_post, w_in, ret_gn_gain, ssm_lambda_re, ssm_lambda_im, ssm_log_dt, ssm_b_re, ssm_b_im, ssm_c_re, ssm_c_im, ssm_d, w_glu, w_out, norm_mlp_pre, norm_mlp_post, w_ff1, w_ff2, loss_target, m_norm_mix_pre, m_norm_mix_post, m_w_in, m_ret_gn_gain, m_ssm_lambda_re, m_ssm_lambda_im, m_ssm_log_dt, m_ssm_b_re, m_ssm_b_im, m_ssm_c_re, m_ssm_c_im, m_ssm_d, m_w_glu, m_w_out, m_norm_mlp_pre, m_norm_mlp_post, m_w_ff1, m_w_ff2, v_norm_mix_pre, v_norm_mix_post, v_w_in, v_ret_gn_gain, v_ssm_lambda_re, v_ssm_lambda_im, v_ssm_log_dt, v_ssm_b_re, v_ssm_b_im, v_ssm_c_re, v_ssm_c_im, v_ssm_d, v_w_glu, v_w_out, v_norm_mlp_pre, v_norm_mlp_post, v_w_ff1, v_w_ff2):
    args = dict(locals())
    w = {n: args[n] for n in WEIGHT_NAMES}
    m = {n: args["m_" + n] for n in WEIGHT_NAMES}
    v = {n: args["v_" + n] for n in WEIGHT_NAMES}
    L = x.shape[1]
    tm = min(256, L)
    tk = min(2048, L)
    tb = min(1024, L)

    calls = {"in": ("w_in",), "rest": WEIGHT_GROUPS["mix"] + WEIGHT_GROUPS["mlp"]}
    started, zero = {}, jnp.zeros((), _F32)
    for call, names in calls.items():
        blocks = [(w[n][0].T if n == "w_in" else w[n][0]).astype(_BF) for n in names]
        blocks[0] = blocks[0] + zero.astype(_BF)
        started[call] = _split_start(blocks, [_landing(b) for b in blocks], True, "weights_start_" + call)
        zero = started[call][4][0, 0]

    def weights(group, after):
        names = WEIGHT_GROUPS[group]
        call = "in" if group == "in" else "rest"
        first = calls[call].index(names[0])
        part = slice(first, first + len(names))
        got = started[call]
        landed = _split_wait(got[0], got[1], got[2][part], got[3][part], after, True, "weights_wait_" + group, first=first)
        return [_weight_from_slots(n, g) for n, g in zip(names, landed)]

    in_flight = []

    def emit(dws):
        names = sorted(dws)
        srcs = [_grad_slots(n, dws[n]) for n in names]
        lands = [_landing(lax.dynamic_index_in_dim(t, _my_index(), 0, keepdims=False)) for t in srcs]
        started = _split_start(srcs, lands, False, "grads_start_" + "_".join(names))
        in_flight.append((names, started))
        return (started[4],)

    shapes = {n: w[n].shape for n in SMALL_NAMES}
    first_piece = {SMALL_NAMES[0]: jnp.zeros(shapes[SMALL_NAMES[0]], _F32)}
    small_flight = []

    def emit_small(gs, sq):
        loss_rows = jnp.broadcast_to(0.5 / D_MODEL * jnp.sum(sq), (PIECE_ROWS, LANES)).astype(_F32)
        bufs = [_pack_small({**first_piece, **gs}, shapes, loss_rows), _pack_bc(gs)]
        small_flight.append(_split_start(bufs, [_landing(b) for b in bufs], True, "small_grads_start"))
        return (small_flight[0][4],)

    small_w = {n: w[n] for n in SMALL_NAMES}
    gx, dg1 = _local_grads(x[0], loss_target[0], small_w, weights, emit, emit_small, tm, tk, tb, zero=zero)
    last_buf = dg1.reshape(PIECE_ROWS, LANES)
    last_started = _split_start([last_buf], [_landing(last_buf)], True, "last_grad_start")

    grads, delta, new_m, new_v = {}, {}, {}, {}
    after = last_started[4]
    for names, started in in_flight:
        landed = _split_wait(*started[:4], after, False, "grads_wait_" + "_".join(names))
        for n, parts in zip(names, landed):
            flip = (lambda t: t.T) if n == "w_in" else (lambda t: t)
            res = _sum_adamw(parts, flip(w[n][0]), flip(m[n][0]), flip(v[n][0]), math.gcd(256, parts.shape[1]), "adamw_" + n)
            grads[n], delta[n], new_m[n], new_v[n] = (flip(t)[None] for t in res)
        after = res[1]
    small_parts, bc_parts = _split_wait(*small_flight[0][:4], after, True, "small_grads_wait")
    last_parts = _split_wait(*last_started[:4], small_parts, True, "last_grad_wait")[0]
    small_parts = lax.dynamic_update_slice(small_parts, last_parts, (0, 0, 0))
    res_bc = _sum_adamw(bc_parts, _pack_bc(w), _pack_bc(m), _pack_bc(v), BC_ROWS, "adamw_bc")
    sw, sm, sv = _pack_small(w, shapes), _pack_small(m, shapes), _pack_small(v, shapes)
    res = _sum_adamw(small_parts, sw, sm, sv, sw.shape[0], "adamw_small")
    for dst, buf, buf_bc in zip((grads, delta, new_m, new_v), res, res_bc):
        dst.update(_unpack_small(buf, shapes))
        dst.update(_unpack_bc(buf_bc))
    _, loss_at, _ = _small_layout(shapes)
    loss = res[0][loss_at, 0]

    return (loss, gx[None], *[grads[n] for n in WEIGHT_NAMES], *[delta[n] for n in WEIGHT_NAMES],
            *[new_m[n] for n in WEIGHT_NAMES], *[new_v[n] for n in WEIGHT_NAMES])
```

```python
import math

import jax
import jax.numpy as jnp
from jax import lax
from jax.experimental import pallas as pl
from jax.experimental.pallas import tpu as pltpu

_BF = jnp.bfloat16
_F32 = jnp.float32

D_MODEL = 1024
RET_W = 512
N_HEAD = 4
HEAD_D = 128
CHUNK = 256
ROPE_CHUNK = 128
SSM_W = 512
SSM_GC = 16
N_GROUP = 32
N_STATE = 64
GROUPS_PER_KB = 8
N_KB = 4
KB_STATES = GROUPS_PER_KB * N_STATE
D_FF = 4096
IN_COLS = 2560
NORM_EPS = 1e-6
ROPE_BASE = 10000.0
N_DEV = 8

ADAM_LR = 0.001
ADAM_B1 = 0.9
ADAM_B2 = 0.999
ADAM_EPS = 1e-08
ADAM_WD = 0.01
ADAM_STEP = 10

SUBLANES = 8
LANES = 128
VMEM_LIMIT = 52 * 1024 * 1024
RET_STEP_CHUNKS = 2
KB_PER_STEP = 2
SCAN_UNROLL = True
FIX_UNROLL = 8

MESH = pl.DeviceIdType.MESH


def _params(*sem):
    return pltpu.CompilerParams(dimension_semantics=sem, vmem_limit_bytes=VMEM_LIMIT)


def _dot(a, b):
    return jnp.dot(a, b, preferred_element_type=_F32)


def _dot_nt(a, b):
    return lax.dot_general(a, b, (((1,), (1,)), ((), ())), preferred_element_type=_F32)


def _dot_tn(a, b):
    return lax.dot_general(a, b, (((0,), (0,)), ((), ())), preferred_element_type=_F32)


def _rms_r(z):
    return lax.rsqrt(jnp.mean(z * z, axis=-1, keepdims=True) + NORM_EPS)


def _rms_bwd(z, g, dn):
    r = _rms_r(z)
    t = dn * g
    dz = r * t - z * (r * r * r * jnp.mean(t * z, axis=-1, keepdims=True))
    return dz, dn * z * r


def _rope(t, cs, sn):
    return t * cs + pltpu.roll(t, HEAD_D // 2, 1) * sn


def _rope_t(t, cs, sn):
    return t * cs - pltpu.roll(t, HEAD_D // 2, 1) * sn


def _sigmoid(z):
    return 1.0 / (1.0 + jnp.exp(-z))


_GELU_C = math.sqrt(2.0 / math.pi)


def _gelu(z):
    return 0.5 * z * (1.0 + jnp.tanh(_GELU_C * (z + 0.044715 * z * z * z)))


def _gelu_grad(z):
    th = jnp.tanh(_GELU_C * (z + 0.044715 * z * z * z))
    return 0.5 * (1.0 + th) + 0.5 * z * (1.0 - th * th) * _GELU_C * (1.0 + 3 * 0.044715 * z * z)


ROW_CHUNK = 256


def _row_chunks(tm):
    return [pl.ds(i, min(ROW_CHUNK, tm)) for i in range(0, tm, ROW_CHUNK)]


def _ordered(body, in_specs, operands, after):
    k = len(after)
    if not k:
        return body, list(in_specs), tuple(operands)
    return ((lambda *refs: body(*refs[k:])), [pl.BlockSpec(memory_space=pl.ANY)] * k + list(in_specs),
            tuple(after) + tuple(operands))


def _row_spec(tm, n):
    return pl.BlockSpec((tm, n), lambda i: (i, 0))


def _full_spec(shape):
    nd = len(shape)
    return pl.BlockSpec(shape, lambda *_: (0,) * nd)


def _weight_spec(shape):
    nd = len(shape)
    return pl.BlockSpec(shape, lambda *_: (0,) * nd, pipeline_mode=pl.Buffered(1))


def _rope_tables(L):
    half = HEAD_D // 2
    inv_freq = ROPE_BASE ** (-jnp.arange(half, dtype=_F32) / half)
    twice = lambda t: jnp.concatenate([t, t], axis=-1)
    off = jnp.arange(ROPE_CHUNK, dtype=_F32)[:, None] * inv_freq[None, :]
    start = (ROPE_CHUNK * jnp.arange(L // ROPE_CHUNK, dtype=_F32))[:, None] * inv_freq[None, :]
    return (twice(jnp.cos(off)), twice(jnp.sin(off)),
            twice(jnp.cos(start))[:, None, :], twice(jnp.sin(start))[:, None, :])


def _prenorm(x, g, tm, after=()):
    L = x.shape[0]

    def body(x_ref, g_ref, h_ref):
        xv = x_ref[...]
        h_ref[...] = (xv * _rms_r(xv) * g_ref[...]).astype(_BF)

    body, in_specs, operands = _ordered(body, [_row_spec(tm, D_MODEL), _full_spec((1, D_MODEL))], (x, g), after)
    return pl.pallas_call(
        body, name="prenorm", grid=(L // tm,),
        in_specs=in_specs, out_specs=_row_spec(tm, D_MODEL),
        out_shape=jax.ShapeDtypeStruct((L, D_MODEL), _BF),
        compiler_params=_params("parallel"),
    )(*operands)


def _inproj_fwd(h, w_in_t, rope, tm):
    L = h.shape[0]
    n_chunks = tm // ROPE_CHUNK

    def body(h_ref, w_ref, co_ref, so_ref, cs_ref, ss_ref, q_ref, k_ref, v_ref, gate_ref, u_ref, cos_ref, sin_ref):
        proj = _dot_nt(h_ref[...], w_ref[...])
        lane = lax.broadcasted_iota(jnp.int32, (ROPE_CHUNK, HEAD_D), 1)
        sign = jnp.where(lane < HEAD_D // 2, -1.0, 1.0)
        co, so = co_ref[...], so_ref[...]
        for c in range(n_chunks):
            chunk = pl.program_id(0) * n_chunks + c
            cst, sst = cs_ref[chunk], ss_ref[chunk]
            rows = slice(c * ROPE_CHUNK, (c + 1) * ROPE_CHUNK)
            cs = co * cst - so * sst
            sn = (so * cst + co * sst) * sign
            cos_ref[rows, :] = cs
            sin_ref[rows, :] = sn
            for hh in range(N_HEAD):
                lo = hh * HEAD_D
                q_ref[rows, lo:lo + HEAD_D] = _rope(proj[rows, lo:lo + HEAD_D], cs, sn).astype(_BF)
                kh = _rope(proj[rows, RET_W + lo:RET_W + lo + HEAD_D], cs, sn) * (HEAD_D ** -0.5)
                k_ref[rows, lo:lo + HEAD_D] = kh.astype(_BF)
        v_ref[...] = proj[:, 2 * RET_W:3 * RET_W].astype(_BF)
        gate_ref[...] = proj[:, 3 * RET_W:4 * RET_W]
        u_ref[...] = proj[:, 4 * RET_W:]

    nc = L // ROPE_CHUNK
    return pl.pallas_call(
        body, name="inproj_fwd", grid=(L // tm,),
        in_specs=[_row_spec(tm, D_MODEL), _weight_spec((IN_COLS, D_MODEL)),
                  _full_spec((ROPE_CHUNK, HEAD_D)), _full_spec((ROPE_CHUNK, HEAD_D)),
                  _full_spec((nc, 1, HEAD_D)), _full_spec((nc, 1, HEAD_D))],
        out_specs=[_row_spec(tm, RET_W)] * 5 + [_row_spec(tm, HEAD_D)] * 2,
        out_shape=[jax.ShapeDtypeStruct((L, RET_W), _BF)] * 3 + [jax.ShapeDtypeStruct((L, RET_W), _F32)] * 2
        + [jax.ShapeDtypeStruct((L, HEAD_D), _F32)] * 2,
        compiler_params=_params("parallel"),
    )(h, w_in_t, *rope)


def _ret_consts():
    lg = jnp.log(1.0 - jnp.exp(jnp.linspace(math.log(1.0 / 32), math.log(1.0 / 512), N_HEAD))).astype(_F32)
    idx = jnp.arange(CHUNK, dtype=_F32)
    diff = idx[:, None] - idx[None, :]
    decay = jnp.where(diff[None] >= 0, jnp.exp(jnp.maximum(diff, 0.0)[None] * lg[:, None, None]), 0.0)
    zeta = jnp.exp((CHUNK - 1 - idx)[None, :] * lg[:, None])
    xi = jnp.exp((idx + 1.0)[None, :] * lg[:, None])
    gc = jnp.exp(CHUNK * lg)
    wide = lambda t: jnp.broadcast_to(t[:, :, None], (N_HEAD, CHUNK, HEAD_D)).astype(_F32)
    gcw = jnp.broadcast_to(gc[:, None, None], (N_HEAD, SUBLANES, HEAD_D)).astype(_F32)
    return decay.astype(_F32), wide(xi), wide(zeta), gcw


def _head_specs():
    wide = _full_spec((N_HEAD, CHUNK, HEAD_D))
    return [_full_spec((N_HEAD, CHUNK, CHUNK)), wide, wide, _full_spec((N_HEAD, SUBLANES, HEAD_D))]


def _retention_fwd(q, k, v, gate, ggn, consts):
    L = q.shape[0]
    nc = L // CHUNK
    cps = math.gcd(RET_STEP_CHUNKS, nc)
    blk = pl.BlockSpec((cps * CHUNK, RET_W), lambda n: (n, 0))

    def body(q_ref, k_ref, v_ref, gate_ref, ggn_ref, dm_ref, xi_ref, zeta_ref, gc_ref,
             o_ref, y_ref, rp_ref, r_scr):
        @pl.when(pl.program_id(0) == 0)
        def _():
            r_scr[...] = jnp.zeros_like(r_scr)

        for hh in range(N_HEAD):
            cols = slice(hh * HEAD_D, (hh + 1) * HEAD_D)
            state = r_scr[hh]
            for c in range(cps):
                rows = slice(c * CHUNK, (c + 1) * CHUNK)
                qv, kv, vv = q_ref[rows, cols], k_ref[rows, cols], v_ref[rows, cols]
                s = _dot_nt(qv, kv) * dm_ref[hh]
                o = _dot(s.astype(_BF), vv) + _dot(qv, state.astype(_BF)) * xi_ref[hh]
                o_ref[rows, cols] = o
                rp_ref[hh, c] = state
                vz = (vv.astype(_F32) * zeta_ref[hh]).astype(_BF)
                state = gc_ref[hh, 0:1, :] * state + _dot_tn(kv, vz)
                dlt = o - jnp.mean(o, axis=-1, keepdims=True)
                on = dlt * lax.rsqrt(jnp.mean(dlt * dlt, axis=-1, keepdims=True) + NORM_EPS)
                gt = gate_ref[rows, cols]
                y_ref[rows, cols] = (gt * _sigmoid(gt) * (on * ggn_ref[:, cols])).astype(_BF)
            r_scr[hh] = state

    return pl.pallas_call(
        body, name="retention_fwd", grid=(nc // cps,),
        in_specs=[blk, blk, blk, blk, _full_spec((1, RET_W))] + _head_specs(),
        out_specs=[blk, blk, pl.BlockSpec((N_HEAD, cps, HEAD_D, HEAD_D), lambda n: (0, n, 0, 0))],
        out_shape=[jax.ShapeDtypeStruct((L, RET_W), _F32), jax.ShapeDtypeStruct((L, RET_W), _BF),
                   jax.ShapeDtypeStruct((N_HEAD, nc, HEAD_D, HEAD_D), _F32)],
        scratch_shapes=[pltpu.VMEM((N_HEAD, HEAD_D, HEAD_D), _F32)],
        compiler_params=_params("arbitrary"),
    )(q, k, v, gate, ggn, *consts)


def _rows_to_segments(dst_scr, src_ref, seg):
    for g in range(dst_scr.shape[0]):
        for j in range(SUBLANES):
            dst_scr[g, pl.ds(j, seg, stride=SUBLANES), :] = src_ref[pl.ds(j * seg, seg), g * LANES:(g + 1) * LANES]


def _segments_to_rows(dst_ref, src_scr, seg):
    for g in range(src_scr.shape[0]):
        for j in range(SUBLANES):
            dst_ref[pl.ds(j * seg, seg), g * LANES:(g + 1) * LANES] = src_scr[g, pl.ds(j, seg, stride=SUBLANES), :].astype(dst_ref.dtype)


def _scan_segments(x_ref, tab_ref, pw_ref, carry_ref, seg, reverse, entry_ref=None, fwd_ref=None, fwd_entry_ref=None,
                   da_ref=None):
    G = x_ref.shape[0]
    W = KB_STATES
    re, im = pl.ds(0, W), pl.ds(W, W)
    row_id = lax.broadcasted_iota(jnp.int32, (SUBLANES, W), 0)
    edge_in = (row_id == SUBLANES - 1) if reverse else (row_id == 0)
    edge_out = 0 if reverse else SUBLANES - 1
    a_tab = [(tab_ref[g, 0], tab_ref[g, 1]) for g in range(G)]

    def local(i, st):
        r = (seg - 1 - i) if reverse else i
        out = []
        for g in range(G):
            (ar, ai), (sr, si) = a_tab[g], st[g]
            nr = ar * sr - ai * si + x_ref[g, r, :, re]
            ni = ar * si + ai * sr + x_ref[g, r, :, im]
            x_ref[g, r, :, re] = nr
            x_ref[g, r, :, im] = ni
            out.append((nr, ni))
        return tuple(out)

    zero = jnp.zeros((SUBLANES, W), _F32)
    ends = lax.fori_loop(0, seg, local, tuple((zero, zero) for _ in range(G)), unroll=SCAN_UNROLL)

    entry = []
    shift = (SUBLANES - 1) if reverse else 1
    for g in range(G):
        er, ei = ends[g]
        fr = jnp.where(edge_in, carry_ref[g, :, re], pltpu.roll(er, shift, 0))
        fi = jnp.where(edge_in, carry_ref[g, :, im], pltpu.roll(ei, shift, 0))
        for j, dist in enumerate((1, 2, 4)):
            pr, pi = tab_ref[g, 2 + 2 * j], tab_ref[g, 3 + 2 * j]
            sh = (SUBLANES - dist) if reverse else dist
            sr, si = pltpu.roll(fr, sh, 0), pltpu.roll(fi, sh, 0)
            fr, fi = fr + pr * sr - pi * si, fi + pr * si + pi * sr
        br, bi = tab_ref[g, 8], tab_ref[g, 9]
        outr = br * fr - bi * fi + er
        outi = br * fi + bi * fr + ei
        carry_ref[g, :, re] = jnp.broadcast_to(outr[edge_out:edge_out + 1, :], (SUBLANES, W))
        carry_ref[g, :, im] = jnp.broadcast_to(outi[edge_out:edge_out + 1, :], (SUBLANES, W))
        entry.append((fr, fi))
        if entry_ref is not None:
            entry_ref[g, :, re] = fr
            entry_ref[g, :, im] = fi

    add_da = da_ref is not None

    def fix(r, st, first=False):
        out = []
        for g in range(G):
            fr, fi = entry[g]
            pwr, pwi = pw_ref[g, r, :, re], pw_ref[g, r, :, im]
            xr = x_ref[g, r, :, re] + (pwr * fr - pwi * fi)
            xi = x_ref[g, r, :, im] + (pwr * fi + pwi * fr)
            x_ref[g, r, :, re] = xr
            x_ref[g, r, :, im] = xi
            if add_da:
                prev = fwd_entry_ref.at[g] if first else fwd_ref.at[g, r - 1]
                xpr, xpi = prev[:, re], prev[:, im]
                out.append((st[g][0] + (xr * xpr + xi * xpi), st[g][1] + (xi * xpr - xr * xpi)))
            else:
                out.append(st[g])
        return tuple(out)

    if add_da:
        st = fix(0, tuple((zero, zero) for _ in range(G)), first=True)
        st = lax.fori_loop(1, seg, fix, st, unroll=SCAN_UNROLL)
        for g in range(G):
            da_ref[g, :, re] += st[g][0]
            da_ref[g, :, im] += st[g][1]
    else:
        lax.fori_loop(0, seg, fix, tuple((zero[0:1, 0:LANES],) for _ in range(G)), unroll=FIX_UNROLL)


def _s5_specs(seg, time=lambda t: t):
    G = KB_PER_STEP
    return dict(
        x=pl.BlockSpec((G, seg, SUBLANES, 2 * KB_STATES), lambda kb, t: (kb, time(t), 0, 0)),
        ent=pl.BlockSpec((G, 1, SUBLANES, 2 * KB_STATES), lambda kb, t: (kb, time(t), 0, 0)),
        b=pl.BlockSpec((G, LANES, 2 * KB_STATES), lambda kb, t: (kb, 0, 0)),
        c=pl.BlockSpec((G, 2 * KB_STATES, LANES), lambda kb, t: (kb, 0, 0)),
        tab=pl.BlockSpec((G, 10, SUBLANES, KB_STATES), lambda kb, t: (kb, 0, 0, 0)),
        pw=pl.BlockSpec((G, seg, 1, 2 * KB_STATES), lambda kb, t: (kb, 0, 0, 0)),
        d=pl.BlockSpec((1, G * LANES), lambda kb, t: (0, kb)),
    )


def _s5_fwd(u, bmat, cmat, tab_f, pw_f, d_skip, tb):
    L = u.shape[0]
    nt = L // tb
    seg = tb // SUBLANES
    G = KB_PER_STEP
    ucol = pl.BlockSpec((tb, G * LANES), lambda kb, t: (t, kb))
    sp = _s5_specs(seg)

    def body(u_ref, b_ref, c_ref, tab_ref, pw_ref, d_ref, s_ref, x_ref, ent_ref, up_scr, y_scr, carry_scr):
        @pl.when(pl.program_id(1) == 0)
        def _():
            carry_scr[...] = jnp.zeros_like(carry_scr)

        _rows_to_segments(up_scr, u_ref, seg)
        for g in range(G):
            x_ref[g] = _dot(up_scr[g].astype(_BF), b_ref[g]).reshape(seg, SUBLANES, 2 * KB_STATES)
        _scan_segments(x_ref, tab_ref, pw_ref, carry_scr, seg, reverse=False, entry_ref=ent_ref.at[:, 0])
        for g in range(G):
            y = _dot(x_ref[g].reshape(tb, 2 * KB_STATES).astype(_BF), c_ref[g])
            y_scr[g] = y + d_ref[:, g * LANES:(g + 1) * LANES] * up_scr[g]
        _segments_to_rows(s_ref, y_scr, seg)

    return pl.pallas_call(
        body, name="s5_fwd", grid=(N_KB // G, nt),
        in_specs=[ucol, sp["b"], sp["c"], sp["tab"], sp["pw"], sp["d"]],
        out_specs=[ucol, sp["x"], sp["ent"]],
        out_shape=[jax.ShapeDtypeStruct((L, SSM_W), _F32),
                   jax.ShapeDtypeStruct((N_KB, L // SUBLANES, SUBLANES, 2 * KB_STATES), _F32),
                   jax.ShapeDtypeStruct((N_KB, nt, SUBLANES, 2 * KB_STATES), _F32)],
        scratch_shapes=[pltpu.VMEM((G, tb, LANES), _F32)] * 2 + [pltpu.VMEM((G, SUBLANES, 2 * KB_STATES), _F32)],
        compiler_params=_params("parallel", "arbitrary"),
    )(u, bmat, cmat, tab_f, pw_f, d_skip)


def _mixout_fwd(s, y_ret, x, w_glu, w_out, g2, tm):
    L = s.shape[0]

    def body(s_ref, yr_ref, x_ref, wg_ref, wo_ref, g_ref, ys_ref, gl_ref, mix_ref, x2_ref, cat_scr):
        for rows in _row_chunks(tm):
            ys = _gelu(s_ref[rows, :]).astype(_BF)
            ys_ref[rows, :] = ys
            glu = _dot(ys, wg_ref[...])
            gl = (glu[:, :SSM_W] * _sigmoid(glu[:, SSM_W:])).astype(_BF)
            gl_ref[rows, :] = gl
            cat_scr[rows, :RET_W] = yr_ref[rows, :]
            cat_scr[rows, RET_W:] = gl
            mix = _dot(cat_scr[rows, :], wo_ref[...])
            mix_ref[rows, :] = mix.astype(_BF)
            x2_ref[rows, :] = x_ref[rows, :] + mix * _rms_r(mix) * g_ref[...]

    return pl.pallas_call(
        body, name="mixout_fwd", grid=(L // tm,),
        in_specs=[_row_spec(tm, SSM_W), _row_spec(tm, RET_W), _row_spec(tm, D_MODEL),
                  _weight_spec((SSM_W, 2 * SSM_W)), _weight_spec((D_MODEL, D_MODEL)), _full_spec((1, D_MODEL))],
        out_specs=[_row_spec(tm, SSM_W), _row_spec(tm, SSM_W), _row_spec(tm, D_MODEL), _row_spec(tm, D_MODEL)],
        out_shape=[jax.ShapeDtypeStruct((L, SSM_W), _BF), jax.ShapeDtypeStruct((L, SSM_W), _BF),
                   jax.ShapeDtypeStruct((L, D_MODEL), _BF), jax.ShapeDtypeStruct((L, D_MODEL), _F32)],
        scratch_shapes=[pltpu.VMEM((tm, D_MODEL), _BF)],
        compiler_params=_params("parallel"),
    )(s, y_ret, x, w_glu, w_out, g2)


FF1_COLS = D_FF // N_DEV


def _ff1_fwd(x2, g3, w1, tm):
    L = x2.shape[0]

    def body(x_ref, g_ref, w_ref, h_ref, a_ref):
        for rows in _row_chunks(tm):
            xv = x_ref[rows, :]
            h = (xv * _rms_r(xv) * g_ref[...]).astype(_BF)
            h_ref[rows, :] = h
            for j in range(N_DEV):
                cols = slice(j * FF1_COLS, (j + 1) * FF1_COLS)
                rl = jnp.maximum(_dot(h, w_ref[j]), 0.0)
                a_ref[rows, cols] = (rl * rl).astype(_BF)

    return pl.pallas_call(
        body, name="ff1_fwd", grid=(L // tm,),
        in_specs=[_row_spec(tm, D_MODEL), _full_spec((1, D_MODEL)), _weight_spec((N_DEV, D_MODEL, FF1_COLS))],
        out_specs=[_row_spec(tm, D_MODEL), _row_spec(tm, D_FF)],
        out_shape=[jax.ShapeDtypeStruct((L, D_MODEL), _BF), jax.ShapeDtypeStruct((L, D_FF), _BF)],
        compiler_params=_params("parallel"),
    )(x2, g3, w1)


def _ff2_loss(act, x2, tgt, g4, w2, tm):
    L = act.shape[0]

    def body(f_ref, x_ref, t_ref, g_ref, w_ref, dy_ref, dm_ref, dg_ref, ls_ref):
        @pl.when(pl.program_id(0) == 0)
        def _():
            dg_ref[...] = jnp.zeros_like(dg_ref)
            ls_ref[...] = jnp.zeros_like(ls_ref)

        g = g_ref[...]
        for rows in _row_chunks(tm):
            m = _dot(f_ref[rows, :], w_ref[...])
            y = x_ref[rows, :] + m * _rms_r(m) * g
            err = y - t_ref[rows, :]
            ls_ref[...] += jnp.sum(err * err, axis=0, keepdims=True)
            dy = err * (1.0 / D_MODEL)
            dy_ref[rows, :] = dy
            dm, dgr = _rms_bwd(m, g, dy)
            dm_ref[rows, :] = dm.astype(_BF)
            dg_ref[...] += jnp.sum(dgr, axis=0, keepdims=True)

    return pl.pallas_call(
        body, name="ff2_loss", grid=(L // tm,),
        in_specs=[_row_spec(tm, D_FF), _row_spec(tm, D_MODEL), _row_spec(tm, D_MODEL),
                  _full_spec((1, D_MODEL)), _weight_spec((D_FF, D_MODEL))],
        out_specs=[_row_spec(tm, D_MODEL), _row_spec(tm, D_MODEL), _full_spec((1, D_MODEL)), _full_spec((1, D_MODEL))],
        out_shape=[jax.ShapeDtypeStruct((L, D_MODEL), _F32), jax.ShapeDtypeStruct((L, D_MODEL), _BF),
                   jax.ShapeDtypeStruct((1, D_MODEL), _F32), jax.ShapeDtypeStruct((1, D_MODEL), _F32)],
        compiler_params=_params("arbitrary"),
    )(act, x2, tgt, g4, w2)


def _ff2_bwd(dm, act, w2, tm, tn):
    L = dm.shape[0]
    last = L // tm - 1

    def body(dm_ref, a_ref, w_ref, df_ref, dw_ref, acc):
        @pl.when(pl.program_id(1) == 0)
        def _():
            acc[...] = jnp.zeros_like(acc)

        dmv = dm_ref[...]
        av = a_ref[...]
        df_ref[...] = (_dot_nt(dmv, w_ref[...]) * jnp.sqrt(4.0 * av.astype(_F32))).astype(_BF)
        acc[...] += _dot_tn(av, dmv)

        @pl.when(pl.program_id(1) == last)
        def _():
            dw_ref[...] = acc[...].astype(_BF)

    return pl.pallas_call(
        body, name="ff2_bwd", grid=(D_FF // tn, L // tm),
        in_specs=[pl.BlockSpec((tm, D_MODEL), lambda j, i: (i, 0)), pl.BlockSpec((tm, tn), lambda j, i: (i, j)),
                  pl.BlockSpec((tn, D_MODEL), lambda j, i: (j, 0))],
        out_specs=[pl.BlockSpec((tm, tn), lambda j, i: (i, j)), pl.BlockSpec((tn, D_MODEL), lambda j, i: (j, 0))],
        out_shape=[jax.ShapeDtypeStruct((L, D_FF), _BF), jax.ShapeDtypeStruct((D_FF, D_MODEL), _BF)],
        scratch_shapes=[pltpu.VMEM((tn, D_MODEL), _F32)],
        compiler_params=_params("parallel", "arbitrary"),
    )(dm, act, w2)


def _ff1_bwd(df1, w1, x2, mix, dy, g3, g2, tm):
    L = df1.shape[0]

    def body(df_ref, w_ref, x2_ref, mix_ref, dy_ref, g3_ref, g2_ref, dx2_ref, dmix_ref, dg3_ref, dg2_ref):
        @pl.when(pl.program_id(0) == 0)
        def _():
            dg3_ref[...] = jnp.zeros_like(dg3_ref)
            dg2_ref[...] = jnp.zeros_like(dg2_ref)

        for rows in _row_chunks(tm):
            dh = _dot_nt(df_ref[rows, 0:FF1_COLS], w_ref[0])
            for j in range(1, N_DEV):
                dh = dh + _dot_nt(df_ref[rows, j * FF1_COLS:(j + 1) * FF1_COLS], w_ref[j])
            dz, dgr = _rms_bwd(x2_ref[rows, :], g3_ref[...], dh)
            dg3_ref[...] += jnp.sum(dgr, axis=0, keepdims=True)
            dx2 = dy_ref[rows, :] + dz
            dx2_ref[rows, :] = dx2
            dmx, dgr2 = _rms_bwd(mix_ref[rows, :].astype(_F32), g2_ref[...], dx2)
            dg2_ref[...] += jnp.sum(dgr2, axis=0, keepdims=True)
            dmix_ref[rows, :] = dmx.astype(_BF)

    vec = _full_spec((1, D_MODEL))
    return pl.pallas_call(
        body, name="ff1_bwd", grid=(L // tm,),
        in_specs=[_row_spec(tm, D_FF), _weight_spec((N_DEV, D_MODEL, FF1_COLS)), _row_spec(tm, D_MODEL),
                  _row_spec(tm, D_MODEL), _row_spec(tm, D_MODEL), vec, vec],
        out_specs=[_row_spec(tm, D_MODEL), _row_spec(tm, D_MODEL), vec, vec],
        out_shape=[jax.ShapeDtypeStruct((L, D_MODEL), _F32), jax.ShapeDtypeStruct((L, D_MODEL), _BF),
                   jax.ShapeDtypeStruct((1, D_MODEL), _F32), jax.ShapeDtypeStruct((1, D_MODEL), _F32)],
        compiler_params=_params("arbitrary"),
    )(df1, w1, x2, mix, dy, g3, g2)


def _matmul_tn(a, b, tm, tn, name, slots=0):
    L, K = a.shape
    N = b.shape[1]
    last = L // tm - 1

    def body(a_ref, b_ref, o_ref, acc):
        @pl.when(pl.program_id(1) == 0)
        def _():
            acc[...] = jnp.zeros_like(acc)

        acc[...] += _dot_tn(a_ref[...].astype(_BF), b_ref[...].astype(_BF))

        @pl.when(pl.program_id(1) == last)
        def _():
            if slots:
                for c in range(tn // slots):
                    o_ref[c] = acc[:, c * slots:(c + 1) * slots].astype(_BF)
            else:
                o_ref[...] = acc[...].astype(_BF)

    if slots:
        out_spec = pl.BlockSpec((tn // slots, K, slots), lambda j, i: (j, 0, 0))
        out_shape = jax.ShapeDtypeStruct((N // slots, K, slots), _BF)
    else:
        out_spec = pl.BlockSpec((K, tn), lambda j, i: (0, j))
        out_shape = jax.ShapeDtypeStruct((K, N), _BF)
    return pl.pallas_call(
        body, name=name, grid=(N // tn, L // tm),
        in_specs=[pl.BlockSpec((tm, K), lambda j, i: (i, 0)), pl.BlockSpec((tm, tn), lambda j, i: (i, j))],
        out_specs=out_spec, out_shape=out_shape,
        scratch_shapes=[pltpu.VMEM((K, tn), _F32)],
        compiler_params=_params("parallel", "arbitrary"),
    )(a, b)


def _dw_out(y_ret, gl, dmix, tk):
    L = dmix.shape[0]
    last = L // tk - 1

    def body(a0_ref, a1_ref, b_ref, o_ref, acc):
        @pl.when(pl.program_id(0) == 0)
        def _():
            acc[...] = jnp.zeros_like(acc)

        bv = b_ref[...]
        acc[:RET_W, :] += _dot_tn(a0_ref[...], bv)
        acc[RET_W:, :] += _dot_tn(a1_ref[...], bv)

        @pl.when(pl.program_id(0) == last)
        def _():
            o_ref[...] = acc[...].astype(_BF)

    return pl.pallas_call(
        body, name="dw_out", grid=(L // tk,),
        in_specs=[_row_spec(tk, RET_W), _row_spec(tk, SSM_W), _row_spec(tk, D_MODEL)],
        out_specs=_full_spec((D_MODEL, D_MODEL)), out_shape=jax.ShapeDtypeStruct((D_MODEL, D_MODEL), _BF),
        scratch_shapes=[pltpu.VMEM((D_MODEL, D_MODEL), _F32)],
        compiler_params=_params("arbitrary"),
    )(y_ret, gl, dmix)


def _dw_in_t(pieces, h, tk):
    L = h.shape[0]
    last = L // tk - 1

    def body(p0, p1, p2, p3, p4, h_ref, o_ref, acc):
        @pl.when(pl.program_id(0) == 0)
        def _():
            acc[...] = jnp.zeros_like(acc)

        hv = h_ref[...]
        for j, p in enumerate((p0, p1, p2, p3, p4)):
            acc[j * RET_W:(j + 1) * RET_W, :] += _dot_tn(p[...].astype(_BF), hv)

        @pl.when(pl.program_id(0) == last)
        def _():
            o_ref[...] = acc[...].astype(_BF)

    return pl.pallas_call(
        body, name="dw_in", grid=(L // tk,),
        in_specs=[_row_spec(tk, RET_W)] * 5 + [_row_spec(tk, D_MODEL)],
        out_specs=_full_spec((IN_COLS, D_MODEL)), out_shape=jax.ShapeDtypeStruct((IN_COLS, D_MODEL), _BF),
        scratch_shapes=[pltpu.VMEM((IN_COLS, D_MODEL), _F32)],
        compiler_params=_params("arbitrary"),
    )(*pieces, h)


def _mixout_bwd(dmix, w_out, w_glu, ys, s, o, gate, ggn, tm, after=()):
    L = dmix.shape[0]

    def body(dmix_ref, wo_ref, wg_ref, ys_ref, s_ref, o_ref, gate_ref, ggn_ref,
             dglu_ref, ds_ref, dgate_ref, do_ref, dggn_ref):
        @pl.when(pl.program_id(0) == 0)
        def _():
            dggn_ref[...] = jnp.zeros_like(dggn_ref)

        ggn = ggn_ref[...]
        for rows in _row_chunks(tm):
            dcat = _dot_nt(dmix_ref[rows, :], wo_ref[...])
            dy_ret, dy_ssm = dcat[:, :RET_W], dcat[:, RET_W:]
            glu = _dot(ys_ref[rows, :], wg_ref[...])
            ga, sg = glu[:, :SSM_W], _sigmoid(glu[:, SSM_W:])
            dga = (dy_ssm * sg).astype(_BF)
            dgb = (dy_ssm * ga * sg * (1.0 - sg)).astype(_BF)
            dglu_ref[rows, :SSM_W] = dga
            dglu_ref[rows, SSM_W:] = dgb
            dys = _dot_nt(dga, wg_ref[:, :SSM_W]) + _dot_nt(dgb, wg_ref[:, SSM_W:])
            ds_ref[rows, :] = dys * _gelu_grad(s_ref[rows, :])
            gt = gate_ref[rows, :]
            sgt = _sigmoid(gt)
            for hh in range(N_HEAD):
                cols = slice(hh * HEAD_D, (hh + 1) * HEAD_D)
                ov = o_ref[rows, cols]
                dlt = ov - jnp.mean(ov, axis=-1, keepdims=True)
                rstd = lax.rsqrt(jnp.mean(dlt * dlt, axis=-1, keepdims=True) + NORM_EPS)
                on = dlt * rstd
                dyr = dy_ret[:, cols] * (gt[:, cols] * sgt[:, cols])
                dgate_ref[rows, cols] = (dy_ret[:, cols] * (on * ggn[:, cols]) * (sgt[:, cols] * (1.0 + gt[:, cols] * (1.0 - sgt[:, cols])))).astype(_BF)
                dggn_ref[:, cols] += jnp.sum(dyr * on, axis=0, keepdims=True)
                don = dyr * ggn[:, cols]
                do = rstd * (don - jnp.mean(don, axis=-1, keepdims=True) - on * jnp.mean(don * on, axis=-1, keepdims=True))
                do_ref[rows, cols] = do.astype(_BF)

    body, in_specs, operands = _ordered(
        body, [_row_spec(tm, D_MODEL), _weight_spec((D_MODEL, D_MODEL)), _weight_spec((SSM_W, 2 * SSM_W)),
               _row_spec(tm, SSM_W), _row_spec(tm, SSM_W), _row_spec(tm, RET_W), _row_spec(tm, RET_W),
               _full_spec((1, RET_W))], (dmix, w_out, w_glu, ys, s, o, gate, ggn), after)
    return pl.pallas_call(
        body, name="mixout_bwd", grid=(L // tm,),
        in_specs=in_specs,
        out_specs=[_row_spec(tm, 2 * SSM_W), _row_spec(tm, SSM_W), _row_spec(tm, RET_W), _row_spec(tm, RET_W),
                   _full_spec((1, RET_W))],
        out_shape=[jax.ShapeDtypeStruct((L, 2 * SSM_W), _BF), jax.ShapeDtypeStruct((L, SSM_W), _F32),
                   jax.ShapeDtypeStruct((L, RET_W), _BF), jax.ShapeDtypeStruct((L, RET_W), _BF),
                   jax.ShapeDtypeStruct((1, RET_W), _F32)],
        compiler_params=_params("arbitrary"),
    )(*operands)


def _s5_bwd(u, ds, xs, ent, bmat, cmat, tab_r, pw_r, d_skip, tb, after=()):
    L = u.shape[0]
    nt = L // tb
    seg = tb // SUBLANES
    G = KB_PER_STEP
    rcol = pl.BlockSpec((tb, G * LANES), lambda kb, t: (nt - 1 - t, kb))
    sp = _s5_specs(seg, time=lambda t: nt - 1 - t)
    aspec = pl.BlockSpec((G, SUBLANES, 2 * KB_STATES), lambda kb, t: (kb, 0, 0))

    def body(u_ref, ds_ref, x_ref, ent_ref, b_ref, c_ref, tr_ref, pr_ref, d_ref,
             du_ref, db_ref, dc_ref, da_ref, dd_ref, up_scr, dp_scr, g_scr, lc_scr):
        @pl.when(pl.program_id(1) == 0)
        def _():
            lc_scr[...] = jnp.zeros_like(lc_scr)
            db_ref[...] = jnp.zeros_like(db_ref)
            dc_ref[...] = jnp.zeros_like(dc_ref)
            da_ref[...] = jnp.zeros_like(da_ref)
            dd_ref[...] = jnp.zeros_like(dd_ref)

        _rows_to_segments(up_scr, u_ref, seg)
        _rows_to_segments(dp_scr, ds_ref, seg)
        for g in range(G):
            g_scr[g] = _dot_nt(dp_scr[g].astype(_BF), c_ref[g]).reshape(seg, SUBLANES, 2 * KB_STATES)
        _scan_segments(g_scr, tr_ref, pr_ref, lc_scr, seg, reverse=True, fwd_ref=x_ref, fwd_entry_ref=ent_ref.at[:, 0],
                       da_ref=da_ref)
        for g in range(G):
            cols = slice(g * LANES, (g + 1) * LANES)
            uv, dsv = up_scr[g], dp_scr[g]
            ub, dsb = uv.astype(_BF), dsv.astype(_BF)
            lamb = g_scr[g].reshape(tb, 2 * KB_STATES).astype(_BF)
            db_ref[g] += _dot_tn(ub, lamb)
            dc_ref[g] += _dot_tn(dsb, x_ref[g].reshape(tb, 2 * KB_STATES).astype(_BF))
            dd_ref[:, cols] += jnp.sum(dsv * uv, axis=0, keepdims=True)
            up_scr[g] = _dot_nt(lamb, b_ref[g]) + d_ref[:, cols] * dsv
        _segments_to_rows(du_ref, up_scr, seg)

    body, in_specs, operands = _ordered(
        body, [rcol, rcol, sp["x"], sp["ent"], sp["b"], sp["c"], sp["tab"], sp["pw"], sp["d"]],
        (u, ds, xs, ent, bmat, cmat, tab_r, pw_r, d_skip), after)
    return pl.pallas_call(
        body, name="s5_bwd", grid=(N_KB // G, nt),
        in_specs=in_specs,
        out_specs=[rcol, sp["b"], sp["b"], aspec, sp["d"]],
        out_shape=[jax.ShapeDtypeStruct((L, SSM_W), _BF),
                   jax.ShapeDtypeStruct((N_KB, LANES, 2 * KB_STATES), _F32),
                   jax.ShapeDtypeStruct((N_KB, LANES, 2 * KB_STATES), _F32),
                   jax.ShapeDtypeStruct((N_KB, SUBLANES, 2 * KB_STATES), _F32),
                   jax.ShapeDtypeStruct((1, SSM_W), _F32)],
        scratch_shapes=[pltpu.VMEM((G, tb, LANES), _F32)] * 2
        + [pltpu.VMEM((G, seg, SUBLANES, 2 * KB_STATES), _F32), pltpu.VMEM((G, SUBLANES, 2 * KB_STATES), _F32)],
        compiler_params=_params("parallel", "arbitrary"),
    )(*operands)


def _retention_bwd(q, k, v, do, r_prev, consts, cosf, sinf, after=()):
    L = q.shape[0]
    nc = L // CHUNK
    cps = math.gcd(RET_STEP_CHUNKS, nc)
    nb = nc // cps
    blk = pl.BlockSpec((cps * CHUNK, RET_W), lambda n: (nb - 1 - n, 0))
    rope_blk = pl.BlockSpec((cps * CHUNK, HEAD_D), lambda n: (nb - 1 - n, 0))

    def body(q_ref, k_ref, v_ref, do_ref, rp_ref, dm_ref, xi_ref, zeta_ref, gc_ref, cos_ref, sin_ref,
             dq_ref, dk_ref, dv_ref, g_scr):
        @pl.when(pl.program_id(0) == 0)
        def _():
            g_scr[...] = jnp.zeros_like(g_scr)

        for hh in range(N_HEAD):
            cols = slice(hh * HEAD_D, (hh + 1) * HEAD_D)
            dm, zeta = dm_ref[hh], zeta_ref[hh]
            gst = g_scr[hh]
            for c in reversed(range(cps)):
                rows = slice(c * CHUNK, (c + 1) * CHUNK)
                qv, kv, vv, dov = q_ref[rows, cols], k_ref[rows, cols], v_ref[rows, cols], do_ref[rows, cols]
                rb = rp_ref[hh, c].astype(_BF)
                gb = gst.astype(_BF)
                sb = (_dot_nt(qv, kv) * dm).astype(_BF)
                dab = (_dot_nt(dov, vv) * dm).astype(_BF)
                dox = (dov.astype(_F32) * xi_ref[hh]).astype(_BF)
                vz = (vv.astype(_F32) * zeta).astype(_BF)
                dq = _dot(dab, kv) + _dot_nt(dox, rb)
                dk = _dot_tn(dab, qv) + _dot_nt(vz, gb)
                dv = _dot_tn(sb, dov) + _dot(kv, gb) * zeta
                gst = gc_ref[hh, 0:1, :] * gst + _dot_tn(qv, dox)
                cs, sn = cos_ref[rows, :], sin_ref[rows, :]
                dq_ref[rows, cols] = _rope_t(dq, cs, sn).astype(_BF)
                dk_ref[rows, cols] = (_rope_t(dk, cs, sn) * (HEAD_D ** -0.5)).astype(_BF)
                dv_ref[rows, cols] = dv.astype(_BF)
            g_scr[hh] = gst

    body, in_specs, operands = _ordered(
        body, [blk, blk, blk, blk, pl.BlockSpec((N_HEAD, cps, HEAD_D, HEAD_D), lambda n: (0, nb - 1 - n, 0, 0))]
        + _head_specs() + [rope_blk, rope_blk], (q, k, v, do, r_prev, *consts, cosf, sinf), after)
    return pl.pallas_call(
        body, name="retention_bwd", grid=(nb,),
        in_specs=in_specs,
        out_specs=[blk, blk, blk],
        out_shape=[jax.ShapeDtypeStruct((L, RET_W), _BF)] * 3,
        scratch_shapes=[pltpu.VMEM((N_HEAD, HEAD_D, HEAD_D), _F32)],
        compiler_params=_params("arbitrary"),
    )(*operands)


def _inproj_bwd(pieces, w_in_t, x, dx2, g1, tm, after=()):
    L = x.shape[0]

    def body(p0, p1, p2, p3, p4, w_ref, x_ref, dx2_ref, g_ref, dx_ref, dg_ref):
        @pl.when(pl.program_id(0) == 0)
        def _():
            dg_ref[...] = jnp.zeros_like(dg_ref)

        for rows in _row_chunks(tm):
            dh = None
            for j, p in enumerate((p0, p1, p2, p3, p4)):
                part = _dot(p[rows, :].astype(_BF), w_ref[j * RET_W:(j + 1) * RET_W, :])
                dh = part if dh is None else dh + part
            dz, dgr = _rms_bwd(x_ref[rows, :], g_ref[...], dh)
            dx_ref[rows, :] = dx2_ref[rows, :] + dz
            dg_ref[...] += jnp.sum(dgr, axis=0, keepdims=True)

    body, in_specs, operands = _ordered(
        body, [_row_spec(tm, RET_W)] * 5 + [_weight_spec((IN_COLS, D_MODEL)), _row_spec(tm, D_MODEL),
                                             _row_spec(tm, D_MODEL), _full_spec((1, D_MODEL))],
        (*pieces, w_in_t, x, dx2, g1), after)
    return pl.pallas_call(
        body, name="inproj_bwd", grid=(L // tm,),
        in_specs=in_specs,
        out_specs=[_row_spec(tm, D_MODEL), _full_spec((1, D_MODEL))],
        out_shape=[jax.ShapeDtypeStruct((L, D_MODEL), _F32), jax.ShapeDtypeStruct((1, D_MODEL), _F32)],
        compiler_params=_params("arbitrary"),
    )(*operands)


def _sum_adamw(parts, w, m, v, tr, name):
    _, R, Cc = parts.shape

    def body(p_ref, w_ref, m_ref, v_ref, g_ref, d_ref, nm_ref, nv_ref):
        gv = p_ref[0].astype(_F32)
        for s in range(1, N_DEV):
            gv = gv + p_ref[s].astype(_F32)
        g_ref[...] = gv
        nm = ADAM_B1 * m_ref[...] + (1.0 - ADAM_B1) * gv
        nv = ADAM_B2 * v_ref[...] + (1.0 - ADAM_B2) * (gv * gv)
        m_hat = nm / (1.0 - ADAM_B1 ** ADAM_STEP)
        v_hat = nv / (1.0 - ADAM_B2 ** ADAM_STEP)
        d_ref[...] = -ADAM_LR * (m_hat / (jnp.sqrt(v_hat) + ADAM_EPS) + ADAM_WD * w_ref[...])
        nm_ref[...] = nm
        nv_ref[...] = nv

    spec = _row_spec(tr, Cc)
    return pl.pallas_call(
        body, name=name, grid=(R // tr,),
        in_specs=[pl.BlockSpec((N_DEV, tr, Cc), lambda i: (0, i, 0))] + [spec] * 3, out_specs=[spec] * 4,
        out_shape=[jax.ShapeDtypeStruct((R, Cc), _F32)] * 4,
        compiler_params=_params("parallel"),
    )(parts, w, m, v)


def _my_place():
    return lax.axis_index("x"), lax.axis_index("y"), lax.axis_index("c")


HBM_SPEC = pl.BlockSpec(memory_space=pltpu.HBM)
SEM_SPEC = pl.BlockSpec(memory_space=pltpu.SEMAPHORE)
DATAFLOW = pltpu.SideEffectType.DATAFLOW_SIDE_EFFECTING


def _my_index():
    x, y, c = _my_place()
    return 4 * x + 2 * y + c


def _landing(own_block):
    zone = lax.empty((N_DEV,) + own_block.shape, own_block.dtype)
    return lax.dynamic_update_index_in_dim(zone, own_block, _my_index(), 0)


def _split_copies(src_refs, land_refs, send_sems, recv_sems, gather, first=0):
    x, y, c = _my_place()
    me = 4 * x + 2 * y + c
    copies = []
    for a, (src, land) in enumerate(zip(src_refs, land_refs)):
        for kk in range(1, N_DEV):
            px, py, pc = x ^ (kk >> 2), y ^ ((kk >> 1) & 1), c ^ (kk & 1)
            peer = 4 * px + 2 * py + pc
            copies.append(pltpu.make_async_remote_copy(
                src_ref=src if gather else src.at[peer], dst_ref=land.at[me],
                send_sem=send_sems.at[(first + a) * 7 + kk - 1], recv_sem=recv_sems.at[(first + a) * 7 + kk - 1],
                device_id=(px, py, pc), device_id_type=MESH))
    return copies


def _split_start(srcs, lands, gather, name):
    n = len(srcs)

    def body(*refs):
        src_refs, land_refs = refs[:n], refs[n:2 * n]
        send_sems, recv_sems = refs[2 * n], refs[2 * n + 1]
        token = refs[-1]
        for cp in _split_copies(src_refs, land_refs, send_sems, recv_sems, gather):
            cp.start()
        token[...] = jnp.zeros_like(token)

    outs = pl.pallas_call(
        body, name=name,
        out_shape=(pltpu.SemaphoreType.DMA((7 * n,)), pltpu.SemaphoreType.DMA((7 * n,)),
                   *[pltpu.HBM(t.shape, t.dtype) for t in srcs], *[pltpu.HBM(t.shape, t.dtype) for t in lands],
                   jax.ShapeDtypeStruct((SUBLANES, LANES), _F32)),
        in_specs=[HBM_SPEC] * (2 * n),
        out_specs=(SEM_SPEC, SEM_SPEC, *[HBM_SPEC] * (2 * n), pl.BlockSpec(memory_space=pltpu.VMEM)),
        input_output_aliases={i: 2 + i for i in range(2 * n)},
        compiler_params=pltpu.CompilerParams(has_side_effects=DATAFLOW),
    )(*[pltpu.with_memory_space_constraint(t, pltpu.HBM) for t in list(srcs) + list(lands)])
    return outs[0], outs[1], outs[2:2 + n], outs[2 + n:2 + 2 * n], outs[-1]


def _split_wait(send_sems, recv_sems, srcs, lands, after, gather, name, first=0):
    n = len(srcs)

    def body(*refs):
        src_refs, land_refs = refs[:n], refs[n:2 * n]
        send_s, recv_s = refs[2 * n], refs[2 * n + 1]
        for cp in _split_copies(src_refs, land_refs, send_s, recv_s, gather, first):
            cp.wait_send()
            cp.wait_recv()

    outs = pl.pallas_call(
        body, name=name,
        out_shape=tuple(pltpu.HBM(t.shape, t.dtype) for t in list(srcs) + list(lands)),
        in_specs=[HBM_SPEC] * (2 * n) + [SEM_SPEC, SEM_SPEC, pl.BlockSpec(memory_space=pl.ANY)],
        out_specs=tuple([HBM_SPEC] * (2 * n)),
        input_output_aliases={i: i for i in range(2 * n)},
        compiler_params=pltpu.CompilerParams(has_side_effects=DATAFLOW),
    )(*srcs, *lands, send_sems, recv_sems, after)
    return outs[n:]


def _discretize(lam_re, lam_im, log_dt, b_re, b_im):
    lr = jnp.minimum(lam_re, -1e-4)
    li = lam_im
    dt = jnp.exp(log_dt)[:, None]
    er = jnp.exp(lr * dt)
    ar, ai = er * jnp.cos(li * dt), er * jnp.sin(li * dt)
    den = lr * lr + li * li
    cr = ((ar - 1.0) * lr + ai * li) / den
    ci = (ai * lr - (ar - 1.0) * li) / den
    bbr = cr[:, :, None] * b_re - ci[:, :, None] * b_im
    bbi = cr[:, :, None] * b_im + ci[:, :, None] * b_re
    return ar, ai, bbr, bbi


def _cmul(ar, ai, br, bi):
    return ar * br - ai * bi, ar * bi + ai * br


def _cpowers(ar, ai, n):
    pr, pi = ar[None], ai[None]
    while pr.shape[0] < n:
        nr, ni = _cmul(pr, pi, pr[-1][None], pi[-1][None])
        pr, pi = jnp.concatenate([pr, nr]), jnp.concatenate([pi, ni])
    return pr[:n], pi[:n]


def _scan_tables(ar, ai, seg, reverse):
    if reverse:
        ai = -ai
    ar, ai = ar.reshape(N_KB, KB_STATES), ai.reshape(N_KB, KB_STATES)
    pr, pi = _cpowers(ar, ai, seg)
    a1 = (pr[-1], pi[-1])
    a2 = _cmul(*a1, *a1)
    a4 = _cmul(*a2, *a2)
    row = jnp.arange(SUBLANES)[None, :, None]
    wide = lambda t: jnp.broadcast_to(t[:, None, :], (N_KB, SUBLANES, KB_STATES))
    tabs = [wide(ar), wide(ai)]
    for dist, (qr, qi) in ((1, a1), (2, a2), (4, a4)):
        keep = (row < SUBLANES - dist) if reverse else (row >= dist)
        tabs += [jnp.where(keep, wide(qr), 0.0), jnp.where(keep, wide(qi), 0.0)]
    tabs += [wide(a1[0]), wide(a1[1])]
    if reverse:
        pr, pi = pr[::-1], pi[::-1]
    pw = jnp.transpose(jnp.concatenate([pr, pi], axis=-1), (1, 0, 2))[:, :, None, :]
    return jnp.stack(tabs, axis=1).astype(_F32), pw.astype(_F32)


def _block_diag_in(br, bi):
    eye = jnp.eye(GROUPS_PER_KB, dtype=_F32)
    one = lambda t: jnp.einsum("kgpc,gh->kgchp", t.reshape(N_KB, GROUPS_PER_KB, N_STATE, SSM_GC), eye).reshape(
        N_KB, LANES, KB_STATES)
    return jnp.concatenate([one(br), one(bi)], axis=-1)


def _block_diag_in_t(dmat):
    d6 = dmat.reshape(N_KB, GROUPS_PER_KB, SSM_GC, 2, GROUPS_PER_KB, N_STATE)
    eye = jnp.eye(GROUPS_PER_KB, dtype=_F32)
    both = jnp.einsum("kgcrhp,gh->rkgpc", d6, eye).reshape(2, N_GROUP, N_STATE, SSM_GC)
    return both[0], both[1]


def _block_diag_out(c_re, c_im):
    eye = jnp.eye(GROUPS_PER_KB, dtype=_F32)
    one = lambda t: jnp.einsum("kgcp,gh->khpgc", t.reshape(N_KB, GROUPS_PER_KB, SSM_GC, N_STATE), eye).reshape(
        N_KB, KB_STATES, LANES)
    return jnp.concatenate([one(c_re), -one(c_im)], axis=1)


def _block_diag_out_t(dmat_t):
    d6 = dmat_t.reshape(N_KB, GROUPS_PER_KB, SSM_GC, 2, GROUPS_PER_KB, N_STATE)
    eye = jnp.eye(GROUPS_PER_KB, dtype=_F32)
    both = jnp.einsum("kgcrhp,gh->rkgcp", d6, eye).reshape(2, N_GROUP, SSM_GC, N_STATE)
    return both[0], -both[1]


SMALL_NAMES = ("norm_mix_pre", "norm_mix_post", "ret_gn_gain", "ssm_lambda_re", "ssm_lambda_im", "ssm_log_dt",
               "ssm_b_re", "ssm_b_im", "ssm_c_re", "ssm_c_im", "ssm_d", "norm_mlp_pre", "norm_mlp_post")


def _local_grads(x, tgt, small, weights, emit, emit_small, tm, tk, tb, zero=0.0):
    L = x.shape[0]
    g1, g2, ggn = small["norm_mix_pre"], small["norm_mix_post"], small["ret_gn_gain"]
    g3, g4, d_skip = small["norm_mlp_pre"], small["norm_mlp_post"], small["ssm_d"]

    rope = _rope_tables(L)
    consts = _ret_consts()

    disc_in = (small["ssm_lambda_re"][0], small["ssm_lambda_im"][0], small["ssm_log_dt"][0] + zero,
               small["ssm_b_re"][0], small["ssm_b_im"][0])
    (ar, ai, bbr, bbi), disc_vjp = jax.vjp(_discretize, *disc_in)
    bmat = _block_diag_in(bbr, bbi).astype(_BF)
    cmat = _block_diag_out(small["ssm_c_re"][0], small["ssm_c_im"][0]).astype(_BF)
    seg = tb // SUBLANES
    tab_f, pw_f = _scan_tables(ar, ai, seg, False)
    tab_r, pw_r = _scan_tables(ar, ai, seg, True)

    h1 = _prenorm(x, g1, min(4 * tm, L), after=(pw_r,))
    (w_in_t,) = weights("in", h1)
    q, k, v, gate, u, cosf, sinf = _inproj_fwd(h1, w_in_t, rope, min(4 * tm, L))
    o, y_ret, r_prev = _retention_fwd(q, k, v, gate, ggn, consts)
    s, xs, ent = _s5_fwd(u, bmat, cmat, tab_f, pw_f, d_skip, tb)
    w_glu, w_out = weights("mix", s)
    ys, gl, mix, x2 = _mixout_fwd(s, y_ret, x, w_glu, w_out, g2, min(2 * tm, L))
    w_ff1, w_ff2 = weights("mlp", x2)
    h3, act = _ff1_fwd(x2, g3, w_ff1, min(2 * tm, L))
    dy, dm, dg4, sq = _ff2_loss(act, x2, tgt, g4, w_ff2, min(2 * tm, L))

    df1, dw_ff2 = _ff2_bwd(dm, act, w_ff2, min(1024, L), 1024)
    dx2, dmix, dg3, dg2 = _ff1_bwd(df1, w_ff1, x2, mix, dy, g3, g2, min(2 * tm, L))
    dw_ff1 = _matmul_tn(h3, df1, tk, 2 * FF1_COLS, "dw_ff1", slots=FF1_COLS)
    token = emit({"w_ff1": dw_ff1, "w_ff2": dw_ff2})
    dglu, ds, dgate, do, dggn = _mixout_bwd(dmix, w_out, w_glu, ys, s, o, gate, ggn, min(2 * tm, L), after=token)
    dw_out = _dw_out(y_ret, gl, dmix, tk)
    dw_glu = _matmul_tn(ys, dglu, tk, 1024, "dw_glu")
    token = emit({"w_glu": dw_glu, "w_out": dw_out})
    du, dbmat, dcmat, da8, dd = _s5_bwd(u, ds, xs, ent, bmat, cmat, tab_r, pw_r, d_skip, tb, after=token)

    da = jnp.sum(da8, axis=1)
    dar = da[:, :KB_STATES].reshape(N_GROUP, N_STATE)
    dai = da[:, KB_STATES:].reshape(N_GROUP, N_STATE)
    dbr, dbi = _block_diag_in_t(dbmat)
    dlre, dlim, dldt, dbre, dbim = disc_vjp((dar, dai, dbr, dbi))
    dcre, dcim = _block_diag_out_t(dcmat)
    token = emit_small({
        "norm_mix_post": dg2, "ret_gn_gain": dggn,
        "ssm_lambda_re": dlre[None], "ssm_lambda_im": dlim[None], "ssm_log_dt": dldt[None],
        "ssm_b_re": dbre[None], "ssm_b_im": dbim[None], "ssm_c_re": dcre[None], "ssm_c_im": dcim[None],
        "ssm_d": dd, "norm_mlp_pre": dg3, "norm_mlp_post": dg4,
    }, sq)

    dq, dk, dv = _retention_bwd(q, k, v, do, r_prev, consts, cosf, sinf, after=token)
    pieces = (dq, dk, dv, dgate, du)
    dw_in_t = _dw_in_t(pieces, h1, min(1024, L))
    token = emit({"w_in": dw_in_t})
    gx, dg1 = _inproj_bwd(pieces, w_in_t, x, dx2, g1, min(2 * tm, L), after=token)
    return gx, dg1


BIG_SHAPES = {"w_in": (D_MODEL, IN_COLS // N_DEV), "w_glu": (SSM_W, 2 * SSM_W // N_DEV), "w_out": (D_MODEL // N_DEV, D_MODEL),
              "w_ff1": (D_MODEL, FF1_COLS), "w_ff2": (D_FF // N_DEV, D_MODEL)}
BIG_NAMES = ("w_in", "w_glu", "w_out", "w_ff1", "w_ff2")


def _cols_from_slots(g):
    return jnp.transpose(g, (1, 0, 2)).reshape(g.shape[1], N_DEV * g.shape[2])


def _cols_to_slots(dw):
    r, cols = dw.shape
    return jnp.transpose(dw.reshape(r, N_DEV, cols // N_DEV), (1, 0, 2))


WEIGHT_GROUPS = {"in": ("w_in",), "mix": ("w_glu", "w_out"), "mlp": ("w_ff1", "w_ff2")}


def _weight_from_slots(name, g):
    if name == "w_glu":
        return _cols_from_slots(g)
    if name == "w_ff1":
        return g
    return g.reshape(N_DEV * g.shape[1], g.shape[2])


def _grad_slots(name, dw):
    if name == "w_glu":
        return _cols_to_slots(dw)
    if name == "w_ff1":
        return dw
    if name == "w_in":
        return dw.reshape(N_DEV, BIG_SHAPES[name][1], BIG_SHAPES[name][0])
    return dw.reshape((N_DEV,) + BIG_SHAPES[name])


PIECE_ROWS = 8


VEC_NAMES = tuple(n for n in SMALL_NAMES if n[:6] not in ("ssm_b_", "ssm_c_"))
BC_NAMES = ("ssm_b_re", "ssm_b_im", "ssm_c_re", "ssm_c_im")
BC_ROWS = N_GROUP * SSM_GC


def _bc_view(name, t):
    t = t[0]
    if name.startswith("ssm_b_"):
        t = jnp.swapaxes(t, 1, 2)
    return t.reshape(BC_ROWS, N_STATE)


def _bc_unview(name, t):
    t = t.reshape(N_GROUP, SSM_GC, N_STATE)
    if name.startswith("ssm_b_"):
        t = jnp.swapaxes(t, 1, 2)
    return t[None]


def _pack_bc(vals):
    return jnp.concatenate([_bc_view(n, vals[n]).astype(_F32) for n in BC_NAMES], axis=0)


def _unpack_bc(buf):
    return {n: _bc_unview(n, buf[j * BC_ROWS:(j + 1) * BC_ROWS]) for j, n in enumerate(BC_NAMES)}


def _small_layout(shapes):
    off, rows = {}, 0
    for n in VEC_NAMES:
        off[n] = rows
        rows += -(-math.prod(shapes[n]) // (PIECE_ROWS * LANES)) * PIECE_ROWS
    return off, rows, rows + PIECE_ROWS


def _pack_small(vals, shapes, last=None):
    parts = []
    for n in VEC_NAMES:
        flat = vals[n].reshape(-1).astype(_F32)
        pad = -flat.shape[0] % (PIECE_ROWS * LANES)
        if pad:
            flat = jnp.concatenate([flat, jnp.zeros((pad,), _F32)])
        parts.append(flat.reshape(-1, LANES))
    parts.append(jnp.zeros((PIECE_ROWS, LANES), _F32) if last is None else last)
    return jnp.concatenate(parts, axis=0)


def _unpack_small(buf, shapes):
    off, _, _ = _small_layout(shapes)
    out = {}
    for n in VEC_NAMES:
        size = math.prod(shapes[n])
        rows = -(-size // LANES)
        out[n] = buf[off[n]:off[n] + rows].reshape(-1)[:size].reshape(shapes[n])
    return out


WEIGHT_NAMES = ('norm_mix_pre', 'norm_mix_post', 'w_in', 'ret_gn_gain', 'ssm_lambda_re', 'ssm_lambda_im', 'ssm_log_dt',
                'ssm_b_re', 'ssm_b_im', 'ssm_c_re', 'ssm_c_im', 'ssm_d', 'w_glu', 'w_out', 'norm_mlp_pre',
                'norm_mlp_post', 'w_ff1', 'w_ff2')


def kernel(x, norm_mix_pre, norm_mix_post, w_in, ret_gn_gain, ssm_lambda_re, ssm_lambda_im, ssm_log_dt, ssm_b_re, ssm_b_im, ssm_c_re, ssm_c_im, ssm_d, w_glu, w_out, norm_mlp_pre, norm_mlp_post, w_ff1, w_ff2, loss_target, m_norm_mix_pre, m_norm_mix_post, m_w_in, m_ret_gn_gain, m_ssm_lambda_re, m_ssm_lambda_im, m_ssm_log_dt, m_ssm_b_re, m_ssm_b_im, m_ssm_c_re, m_ssm_c_im, m_ssm_d, m_w_glu, m_w_out, m_norm_mlp_pre, m_norm_mlp_post, m_w_ff1, m_w_ff2, v_norm_mix_pre, v_norm_mix_post, v_w_in, v_ret_gn_gain, v_ssm_lambda_re, v_ssm_lambda_im, v_ssm_log_dt, v_ssm_b_re, v_ssm_b_im, v_ssm_c_re, v_ssm_c_im, v_ssm_d, v_w_glu, v_w_out, v_norm_mlp_pre, v_norm_mlp_post, v_w_ff1, v_w_ff2):
    args = dict(locals())
    w = {n: args[n] for n in WEIGHT_NAMES}
    m = {n: args["m_" + n] for n in WEIGHT_NAMES}
    v = {n: args["v_" + n] for n in WEIGHT_NAMES}
    L = x.shape[1]
    tm = min(256, L)
    tk = min(2048, L)
    tb = min(1024, L)

    calls = {"in": ("w_in",), "rest": WEIGHT_GROUPS["mix"] + WEIGHT_GROUPS["mlp"]}
    started, zero = {}, jnp.zeros((), _F32)
    for call, names in calls.items():
        blocks = [(w[n][0].T if n == "w_in" else w[n][0]).astype(_BF) for n in names]
        blocks[0] = blocks[0] + zero.astype(_BF)
        started[call] = _split_start(blocks, [_landing(b) for b in blocks], True, "weights_start_" + call)
        zero = started[call][4][0, 0]

    def weights(group, after):
        names = WEIGHT_GROUPS[group]
        call = "in" if group == "in" else "rest"
        first = calls[call].index(names[0])
        part = slice(first, first + len(names))
        got = started[call]
        landed = _split_wait(got[0], got[1], got[2][part], got[3][part], after, True, "weights_wait_" + group, first=first)
        return [_weight_from_slots(n, g) for n, g in zip(names, landed)]

    in_flight = []

    def emit(dws):
        names = sorted(dws)
        srcs = [_grad_slots(n, dws[n]) for n in names]
        lands = [_landing(lax.dynamic_index_in_dim(t, _my_index(), 0, keepdims=False)) for t in srcs]
        started = _split_start(srcs, lands, False, "grads_start_" + "_".join(names))
        in_flight.append((names, started))
        return (started[4],)

    shapes = {n: w[n].shape for n in SMALL_NAMES}
    first_piece = {SMALL_NAMES[0]: jnp.zeros(shapes[SMALL_NAMES[0]], _F32)}
    small_flight = []

    def emit_small(gs, sq):
        loss_rows = jnp.broadcast_to(0.5 / D_MODEL * jnp.sum(sq), (PIECE_ROWS, LANES)).astype(_F32)
        bufs = [_pack_small({**first_piece, **gs}, shapes, loss_rows), _pack_bc(gs)]
        small_flight.append(_split_start(bufs, [_landing(b) for b in bufs], True, "small_grads_start"))
        return (small_flight[0][4],)

    small_w = {n: w[n] for n in SMALL_NAMES}
    gx, dg1 = _local_grads(x[0], loss_target[0], small_w, weights, emit, emit_small, tm, tk, tb, zero=zero)
    last_buf = dg1.reshape(PIECE_ROWS, LANES)
    last_started = _split_start([last_buf], [_landing(last_buf)], True, "last_grad_start")

    grads, delta, new_m, new_v = {}, {}, {}, {}
    after = last_started[4]
    for names, started in in_flight:
        landed = _split_wait(*started[:4], after, False, "grads_wait_" + "_".join(names))
        for n, parts in zip(names, landed):
            flip = (lambda t: t.T) if n == "w_in" else (lambda t: t)
            res = _sum_adamw(parts, flip(w[n][0]), flip(m[n][0]), flip(v[n][0]), math.gcd(256, parts.shape[1]), "adamw_" + n)
            grads[n], delta[n], new_m[n], new_v[n] = (flip(t)[None] for t in res)
        after = res[1]
    small_parts, bc_parts = _split_wait(*small_flight[0][:4], after, True, "small_grads_wait")
    last_parts = _split_wait(*last_started[:4], small_parts, True, "last_grad_wait")[0]
    small_parts = lax.dynamic_update_slice(small_parts, last_parts, (0, 0, 0))
    res_bc = _sum_adamw(bc_parts, _pack_bc(w), _pack_bc(m), _pack_bc(v), BC_ROWS, "adamw_bc")
    sw, sm, sv = _pack_small(w, shapes), _pack_small(m, shapes), _pack_small(v, shapes)
    res = _sum_adamw(small_parts, sw, sm, sv, sw.shape[0], "adamw_small")
    for dst, buf, buf_bc in zip((grads, delta, new_m, new_v), res, res_bc):
        dst.update(_unpack_small(buf, shapes))
        dst.update(_unpack_bc(buf_bc))
    _, loss_at, _ = _small_layout(shapes)
    loss = res[0][loss_at, 0]

    return (loss, gx[None], *[grads[n] for n in WEIGHT_NAMES], *[delta[n] for n in WEIGHT_NAMES],
            *[new_m[n] for n in WEIGHT_NAMES], *[new_v[n] for n in WEIGHT_NAMES])
```

```python
import math

import jax
import jax.numpy as jnp
from jax import lax
from jax.experimental import pallas as pl
from jax.experimental.pallas import tpu as pltpu

_BF = jnp.bfloat16
_F32 = jnp.float32

D_MODEL = 1024
RET_W = 512
N_HEAD = 4
HEAD_D = 128
CHUNK = 256
ROPE_CHUNK = 128
SSM_W = 512
SSM_GC = 16
N_GROUP = 32
N_STATE = 64
GROUPS_PER_KB = 8
N_KB = 4
KB_STATES = GROUPS_PER_KB * N_STATE
D_FF = 4096
IN_COLS = 2560
NORM_EPS = 1e-6
ROPE_BASE = 10000.0
N_DEV = 8

ADAM_LR = 0.001
ADAM_B1 = 0.9
ADAM_B2 = 0.999
ADAM_EPS = 1e-08
ADAM_WD = 0.01
ADAM_STEP = 10

SUBLANES = 8
LANES = 128
VMEM_LIMIT = 52 * 1024 * 1024
RET_STEP_CHUNKS = 2
KB_PER_STEP = 2
SCAN_UNROLL = True
FIX_UNROLL = 8

MESH = pl.DeviceIdType.MESH


def _params(*sem):
    return pltpu.CompilerParams(dimension_semantics=sem, vmem_limit_bytes=VMEM_LIMIT)


def _dot(a, b):
    return jnp.dot(a, b, preferred_element_type=_F32)


def _dot_nt(a, b):
    return lax.dot_general(a, b, (((1,), (1,)), ((), ())), preferred_element_type=_F32)


def _dot_tn(a, b):
    return lax.dot_general(a, b, (((0,), (0,)), ((), ())), preferred_element_type=_F32)


def _rms_r(z):
    return lax.rsqrt(jnp.mean(z * z, axis=-1, keepdims=True) + NORM_EPS)


def _rms_bwd(z, g, dn):
    r = _rms_r(z)
    t = dn * g
    dz = r * t - z * (r * r * r * jnp.mean(t * z, axis=-1, keepdims=True))
    return dz, dn * z * r


def _rope(t, cs, sn):
    return t * cs + pltpu.roll(t, HEAD_D // 2, 1) * sn


def _rope_t(t, cs, sn):
    return t * cs - pltpu.roll(t, HEAD_D // 2, 1) * sn


def _sigmoid(z):
    return 1.0 / (1.0 + jnp.exp(-z))


_GELU_C = math.sqrt(2.0 / math.pi)


def _gelu(z):
    return 0.5 * z * (1.0 + jnp.tanh(_GELU_C * (z + 0.044715 * z * z * z)))


def _gelu_grad(z):
    th = jnp.tanh(_GELU_C * (z + 0.044715 * z * z * z))
    return 0.5 * (1.0 + th) + 0.5 * z * (1.0 - th * th) * _GELU_C * (1.0 + 3 * 0.044715 * z * z)


ROW_CHUNK = 256


def _row_chunks(tm):
    return [pl.ds(i, min(ROW_CHUNK, tm)) for i in range(0, tm, ROW_CHUNK)]


def _ordered(body, in_specs, operands, after):
    k = len(after)
    if not k:
        return body, list(in_specs), tuple(operands)
    return ((lambda *refs: body(*refs[k:])), [pl.BlockSpec(memory_space=pl.ANY)] * k + list(in_specs),
            tuple(after) + tuple(operands))


def _row_spec(tm, n):
    return pl.BlockSpec((tm, n), lambda i: (i, 0))


def _full_spec(shape):
    nd = len(shape)
    return pl.BlockSpec(shape, lambda *_: (0,) * nd)


def _weight_spec(shape):
    nd = len(shape)
    return pl.BlockSpec(shape, lambda *_: (0,) * nd, pipeline_mode=pl.Buffered(1))


def _rope_tables(L):
    half = HEAD_D // 2
    inv_freq = ROPE_BASE ** (-jnp.arange(half, dtype=_F32) / half)
    twice = lambda t: jnp.concatenate([t, t], axis=-1)
    off = jnp.arange(ROPE_CHUNK, dtype=_F32)[:, None] * inv_freq[None, :]
    start = (ROPE_CHUNK * jnp.arange(L // ROPE_CHUNK, dtype=_F32))[:, None] * inv_freq[None, :]
    return (twice(jnp.cos(off)), twice(jnp.sin(off)),
            twice(jnp.cos(start))[:, None, :], twice(jnp.sin(start))[:, None, :])


def _prenorm(x, g, tm, after=()):
    L = x.shape[0]

    def body(x_ref, g_ref, h_ref):
        xv = x_ref[...]
        h_ref[...] = (xv * _rms_r(xv) * g_ref[...]).astype(_BF)

    body, in_specs, operands = _ordered(body, [_row_spec(tm, D_MODEL), _full_spec((1, D_MODEL))], (x, g), after)
    return pl.pallas_call(
        body, name="prenorm", grid=(L // tm,),
        in_specs=in_specs, out_specs=_row_spec(tm, D_MODEL),
        out_shape=jax.ShapeDtypeStruct((L, D_MODEL), _BF),
        compiler_params=_params("parallel"),
    )(*operands)


def _inproj_fwd(h, w_in_t, rope, tm):
    L = h.shape[0]
    n_chunks = tm // ROPE_CHUNK

    def body(h_ref, w_ref, co_ref, so_ref, cs_ref, ss_ref, q_ref, k_ref, v_ref, gate_ref, u_ref, cos_ref, sin_ref):
        proj = _dot_nt(h_ref[...], w_ref[...])
        lane = lax.broadcasted_iota(jnp.int32, (ROPE_CHUNK, HEAD_D), 1)
        sign = jnp.where(lane < HEAD_D // 2, -1.0, 1.0)
        co, so = co_ref[...], so_ref[...]
        for c in range(n_chunks):
            chunk = pl.program_id(0) * n_chunks + c
            cst, sst = cs_ref[chunk], ss_ref[chunk]
            rows = slice(c * ROPE_CHUNK, (c + 1) * ROPE_CHUNK)
            cs = co * cst - so * sst
            sn = (so * cst + co * sst) * sign
            cos_ref[rows, :] = cs
            sin_ref[rows, :] = sn
            for hh in range(N_HEAD):
                lo = hh * HEAD_D
                q_ref[rows, lo:lo + HEAD_D] = _rope(proj[rows, lo:lo + HEAD_D], cs, sn).astype(_BF)
                kh = _rope(proj[rows, RET_W + lo:RET_W + lo + HEAD_D], cs, sn) * (HEAD_D ** -0.5)
                k_ref[rows, lo:lo + HEAD_D] = kh.astype(_BF)
        v_ref[...] = proj[:, 2 * RET_W:3 * RET_W].astype(_BF)
        gate_ref[...] = proj[:, 3 * RET_W:4 * RET_W]
        u_ref[...] = proj[:, 4 * RET_W:]

    nc = L // ROPE_CHUNK
    return pl.pallas_call(
        body, name="inproj_fwd", grid=(L // tm,),
        in_specs=[_row_spec(tm, D_MODEL), _weight_spec((IN_COLS, D_MODEL)),
                  _full_spec((ROPE_CHUNK, HEAD_D)), _full_spec((ROPE_CHUNK, HEAD_D)),
                  _full_spec((nc, 1, HEAD_D)), _full_spec((nc, 1, HEAD_D))],
        out_specs=[_row_spec(tm, RET_W)] * 5 + [_row_spec(tm, HEAD_D)] * 2,
        out_shape=[jax.ShapeDtypeStruct((L, RET_W), _BF)] * 3 + [jax.ShapeDtypeStruct((L, RET_W), _F32)] * 2
        + [jax.ShapeDtypeStruct((L, HEAD_D), _F32)] * 2,
        compiler_params=_params("parallel"),
    )(h, w_in_t, *rope)


def _ret_consts():
    lg = jnp.log(1.0 - jnp.exp(jnp.linspace(math.log(1.0 / 32), math.log(1.0 / 512), N_HEAD))).astype(_F32)
    idx = jnp.arange(CHUNK, dtype=_F32)
    diff = idx[:, None] - idx[None, :]
    decay = jnp.where(diff[None] >= 0, jnp.exp(jnp.maximum(diff, 0.0)[None] * lg[:, None, None]), 0.0)
    zeta = jnp.exp((CHUNK - 1 - idx)[None, :] * lg[:, None])
    xi = jnp.exp((idx + 1.0)[None, :] * lg[:, None])
    gc = jnp.exp(CHUNK * lg)
    wide = lambda t: jnp.broadcast_to(t[:, :, None], (N_HEAD, CHUNK, HEAD_D)).astype(_F32)
    gcw = jnp.broadcast_to(gc[:, None, None], (N_HEAD, SUBLANES, HEAD_D)).astype(_F32)
    return decay.astype(_F32), wide(xi), wide(zeta), gcw


def _head_specs():
    wide = _full_spec((N_HEAD, CHUNK, HEAD_D))
    return [_full_spec((N_HEAD, CHUNK, CHUNK)), wide, wide, _full_spec((N_HEAD, SUBLANES, HEAD_D))]


def _retention_fwd(q, k, v, gate, ggn, consts):
    L = q.shape[0]
    nc = L // CHUNK
    cps = math.gcd(RET_STEP_CHUNKS, nc)
    blk = pl.BlockSpec((cps * CHUNK, RET_W), lambda n: (n, 0))

    def body(q_ref, k_ref, v_ref, gate_ref, ggn_ref, dm_ref, xi_ref, zeta_ref, gc_ref,
             o_ref, y_ref, rp_ref, r_scr):
        @pl.when(pl.program_id(0) == 0)
        def _():
            r_scr[...] = jnp.zeros_like(r_scr)

        for hh in range(N_HEAD):
            cols = slice(hh * HEAD_D, (hh + 1) * HEAD_D)
            state = r_scr[hh]
            for c in range(cps):
                rows = slice(c * CHUNK, (c + 1) * CHUNK)
                qv, kv, vv = q_ref[rows, cols], k_ref[rows, cols], v_ref[rows, cols]
                s = _dot_nt(qv, kv) * dm_ref[hh]
                o = _dot(s.astype(_BF), vv) + _dot(qv, state.astype(_BF)) * xi_ref[hh]
                o_ref[rows, cols] = o
                rp_ref[hh, c] = state
                vz = (vv.astype(_F32) * zeta_ref[hh]).astype(_BF)
                state = gc_ref[hh, 0:1, :] * state + _dot_tn(kv, vz)
                dlt = o - jnp.mean(o, axis=-1, keepdims=True)
                on = dlt * lax.rsqrt(jnp.mean(dlt * dlt, axis=-1, keepdims=True) + NORM_EPS)
                gt = gate_ref[rows, cols]
                y_ref[rows, cols] = (gt * _sigmoid(gt) * (on * ggn_ref[:, cols])).astype(_BF)
            r_scr[hh] = state

    return pl.pallas_call(
        body, name="retention_fwd", grid=(nc // cps,),
        in_specs=[blk, blk, blk, blk, _full_spec((1, RET_W))] + _head_specs(),
        out_specs=[blk, blk, pl.BlockSpec((N_HEAD, cps, HEAD_D, HEAD_D), lambda n: (0, n, 0, 0))],
        out_shape=[jax.ShapeDtypeStruct((L, RET_W), _F32), jax.ShapeDtypeStruct((L, RET_W), _BF),
                   jax.ShapeDtypeStruct((N_HEAD, nc, HEAD_D, HEAD_D), _F32)],
        scratch_shapes=[pltpu.VMEM((N_HEAD, HEAD_D, HEAD_D), _F32)],
        compiler_params=_params("arbitrary"),
    )(q, k, v, gate, ggn, *consts)


def _rows_to_segments(dst_scr, src_ref, seg):
    for g in range(dst_scr.shape[0]):
        for j in range(SUBLANES):
            dst_scr[g, pl.ds(j, seg, stride=SUBLANES), :] = src_ref[pl.ds(j * seg, seg), g * LANES:(g + 1) * LANES]


def _segments_to_rows(dst_ref, src_scr, seg):
    for g in range(src_scr.shape[0]):
        for j in range(SUBLANES):
            dst_ref[pl.ds(j * seg, seg), g * LANES:(g + 1) * LANES] = src_scr[g, pl.ds(j, seg, stride=SUBLANES), :].astype(dst_ref.dtype)


def _scan_segments(x_ref, tab_ref, pw_ref, carry_ref, seg, reverse, entry_ref=None, fwd_ref=None, fwd_entry_ref=None,
                   da_ref=None):
    G = x_ref.shape[0]
    W = KB_STATES
    re, im = pl.ds(0, W), pl.ds(W, W)
    row_id = lax.broadcasted_iota(jnp.int32, (SUBLANES, W), 0)
    edge_in = (row_id == SUBLANES - 1) if reverse else (row_id == 0)
    edge_out = 0 if reverse else SUBLANES - 1
    a_tab = [(tab_ref[g, 0], tab_ref[g, 1]) for g in range(G)]

    def local(i, st):
        r = (seg - 1 - i) if reverse else i
        out = []
        for g in range(G):
            (ar, ai), (sr, si) = a_tab[g], st[g]
            nr = ar * sr - ai * si + x_ref[g, r, :, re]
            ni = ar * si + ai * sr + x_ref[g, r, :, im]
            x_ref[g, r, :, re] = nr
            x_ref[g, r, :, im] = ni
            out.append((nr, ni))
        return tuple(out)

    zero = jnp.zeros((SUBLANES, W), _F32)
    ends = lax.fori_loop(0, seg, local, tuple((zero, zero) for _ in range(G)), unroll=SCAN_UNROLL)

    entry = []
    shift = (SUBLANES - 1) if reverse else 1
    for g in range(G):
        er, ei = ends[g]
        fr = jnp.where(edge_in, carry_ref[g, :, re], pltpu.roll(er, shift, 0))
        fi = jnp.where(edge_in, carry_ref[g, :, im], pltpu.roll(ei, shift, 0))
        for j, dist in enumerate((1, 2, 4)):
            pr, pi = tab_ref[g, 2 + 2 * j], tab_ref[g, 3 + 2 * j]
            sh = (SUBLANES - dist) if reverse else dist
            sr, si = pltpu.roll(fr, sh, 0), pltpu.roll(fi, sh, 0)
            fr, fi = fr + pr * sr - pi * si, fi + pr * si + pi * sr
        br, bi = tab_ref[g, 8], tab_ref[g, 9]
        outr = br * fr - bi * fi + er
        outi = br * fi + bi * fr + ei
        carry_ref[g, :, re] = jnp.broadcast_to(outr[edge_out:edge_out + 1, :], (SUBLANES, W))
        carry_ref[g, :, im] = jnp.broadcast_to(outi[edge_out:edge_out + 1, :], (SUBLANES, W))
        entry.append((fr, fi))
        if entry_ref is not None:
            entry_ref[g, :, re] = fr
            entry_ref[g, :, im] = fi

    add_da = da_ref is not None

    def fix(r, st, first=False):
        out = []
        for g in range(G):
            fr, fi = entry[g]
            pwr, pwi = pw_ref[g, r, :, re], pw_ref[g, r, :, im]
            xr = x_ref[g, r, :, re] + (pwr * fr - pwi * fi)
            xi = x_ref[g, r, :, im] + (pwr * fi + pwi * fr)
            x_ref[g, r, :, re] = xr
            x_ref[g, r, :, im] = xi
            if add_da:
                prev = fwd_entry_ref.at[g] if first else fwd_ref.at[g, r - 1]
                xpr, xpi = prev[:, re], prev[:, im]
                out.append((st[g][0] + (xr * xpr + xi * xpi), st[g][1] + (xi * xpr - xr * xpi)))
            else:
                out.append(st[g])
        return tuple(out)

    if add_da:
        st = fix(0, tuple((zero, zero) for _ in range(G)), first=True)
        st = lax.fori_loop(1, seg, fix, st, unroll=SCAN_UNROLL)
        for g in range(G):
            da_ref[g, :, re] += st[g][0]
            da_ref[g, :, im] += st[g][1]
    else:
        lax.fori_loop(0, seg, fix, tuple((zero[0:1, 0:LANES],) for _ in range(G)), unroll=FIX_UNROLL)


def _s5_specs(seg, time=lambda t: t):
    G = KB_PER_STEP
    return dict(
        x=pl.BlockSpec((G, seg, SUBLANES, 2 * KB_STATES), lambda kb, t: (kb, time(t), 0, 0)),
        ent=pl.BlockSpec((G, 1, SUBLANES, 2 * KB_STATES), lambda kb, t: (kb, time(t), 0, 0)),
        b=pl.BlockSpec((G, LANES, 2 * KB_STATES), lambda kb, t: (kb, 0, 0)),
        c=pl.BlockSpec((G, 2 * KB_STATES, LANES), lambda kb, t: (kb, 0, 0)),
        tab=pl.BlockSpec((G, 10, SUBLANES, KB_STATES), lambda kb, t: (kb, 0, 0, 0)),
        pw=pl.BlockSpec((G, seg, 1, 2 * KB_STATES), lambda kb, t: (kb, 0, 0, 0)),
        d=pl.BlockSpec((1, G * LANES), lambda kb, t: (0, kb)),
    )


def _s5_fwd(u, bmat, cmat, tab_f, pw_f, d_skip, tb):
    L = u.shape[0]
    nt = L // tb
    seg = tb // SUBLANES
    G = KB_PER_STEP
    ucol = pl.BlockSpec((tb, G * LANES), lambda kb, t: (t, kb))
    sp = _s5_specs(seg)

    def body(u_ref, b_ref, c_ref, tab_ref, pw_ref, d_ref, s_ref, x_ref, ent_ref, up_scr, y_scr, carry_scr):
        @pl.when(pl.program_id(1) == 0)
        def _():
            carry_scr[...] = jnp.zeros_like(carry_scr)

        _rows_to_segments(up_scr, u_ref, seg)
        for g in range(G):
            x_ref[g] = _dot(up_scr[g].astype(_BF), b_ref[g]).reshape(seg, SUBLANES, 2 * KB_STATES)
        _scan_segments(x_ref, tab_ref, pw_ref, carry_scr, seg, reverse=False, entry_ref=ent_ref.at[:, 0])
        for g in range(G):
            y = _dot(x_ref[g].reshape(tb, 2 * KB_STATES).astype(_BF), c_ref[g])
            y_scr[g] = y + d_ref[:, g * LANES:(g + 1) * LANES] * up_scr[g]
        _segments_to_rows(s_ref, y_scr, seg)

    return pl.pallas_call(
        body, name="s5_fwd", grid=(N_KB // G, nt),
        in_specs=[ucol, sp["b"], sp["c"], sp["tab"], sp["pw"], sp["d"]],
        out_specs=[ucol, sp["x"], sp["ent"]],
        out_shape=[jax.ShapeDtypeStruct((L, SSM_W), _F32),
                   jax.ShapeDtypeStruct((N_KB, L // SUBLANES, SUBLANES, 2 * KB_STATES), _F32),
                   jax.ShapeDtypeStruct((N_KB, nt, SUBLANES, 2 * KB_STATES), _F32)],
        scratch_shapes=[pltpu.VMEM((G, tb, LANES), _F32)] * 2 + [pltpu.VMEM((G, SUBLANES, 2 * KB_STATES), _F32)],
        compiler_params=_params("parallel", "arbitrary"),
    )(u, bmat, cmat, tab_f, pw_f, d_skip)


def _mixout_fwd(s, y_ret, x, w_glu, w_out, g2, tm):
    L = s.shape[0]

    def body(s_ref, yr_ref, x_ref, wg_ref, wo_ref, g_ref, ys_ref, gl_ref, mix_ref, x2_ref, cat_scr):
        for rows in _row_chunks(tm):
            ys = _gelu(s_ref[rows, :]).astype(_BF)
            ys_ref[rows, :] = ys
            glu = _dot(ys, wg_ref[...])
            gl = (glu[:, :SSM_W] * _sigmoid(glu[:, SSM_W:])).astype(_BF)
            gl_ref[rows, :] = gl
            cat_scr[rows, :RET_W] = yr_ref[rows, :]
            cat_scr[rows, RET_W:] = gl
            mix = _dot(cat_scr[rows, :], wo_ref[...])
            mix_ref[rows, :] = mix.astype(_BF)
            x2_ref[rows, :] = x_ref[rows, :] + mix * _rms_r(mix) * g_ref[...]

    return pl.pallas_call(
        body, name="mixout_fwd", grid=(L // tm,),
        in_specs=[_row_spec(tm, SSM_W), _row_spec(tm, RET_W), _row_spec(tm, D_MODEL),
                  _weight_spec((SSM_W, 2 * SSM_W)), _weight_spec((D_MODEL, D_MODEL)), _full_spec((1, D_MODEL))],
        out_specs=[_row_spec(tm, SSM_W), _row_spec(tm, SSM_W), _row_spec(tm, D_MODEL), _row_spec(tm, D_MODEL)],
        out_shape=[jax.ShapeDtypeStruct((L, SSM_W), _BF), jax.ShapeDtypeStruct((L, SSM_W), _BF),
                   jax.ShapeDtypeStruct((L, D_MODEL), _BF), jax.ShapeDtypeStruct((L, D_MODEL), _F32)],
        scratch_shapes=[pltpu.VMEM((tm, D_MODEL), _BF)],
        compiler_params=_params("parallel"),
    )(s, y_ret, x, w_glu, w_out, g2)


FF1_COLS = D_FF // N_DEV


def _ff1_fwd(x2, g3, w1, tm):
    L = x2.shape[0]

    def body(x_ref, g_ref, w_ref, h_ref, a_ref):
        for rows in _row_chunks(tm):
            xv = x_ref[rows, :]
            h = (xv * _rms_r(xv) * g_ref[...]).astype(_BF)
            h_ref[rows, :] = h
            for j in range(N_DEV):
                cols = slice(j * FF1_COLS, (j + 1) * FF1_COLS)
                rl = jnp.maximum(_dot(h, w_ref[j]), 0.0)
                a_ref[rows, cols] = (rl * rl).astype(_BF)

    return pl.pallas_call(
        body, name="ff1_fwd", grid=(L // tm,),
        in_specs=[_row_spec(tm, D_MODEL), _full_spec((1, D_MODEL)), _weight_spec((N_DEV, D_MODEL, FF1_COLS))],
        out_specs=[_row_spec(tm, D_MODEL), _row_spec(tm, D_FF)],
        out_shape=[jax.ShapeDtypeStruct((L, D_MODEL), _BF), jax.ShapeDtypeStruct((L, D_FF), _BF)],
        compiler_params=_params("parallel"),
    )(x2, g3, w1)


def _ff2_loss(act, x2, tgt, g4, w2, tm):
    L = act.shape[0]

    def body(f_ref, x_ref, t_ref, g_ref, w_ref, dy_ref, dm_ref, dg_ref, ls_ref):
        @pl.when(pl.program_id(0) == 0)
        def _():
            dg_ref[...] = jnp.zeros_like(dg_ref)
            ls_ref[...] = jnp.zeros_like(ls_ref)

        g = g_ref[...]
        for rows in _row_chunks(tm):
            m = _dot(f_ref[rows, :], w_ref[...])
            y = x_ref[rows, :] + m * _rms_r(m) * g
            err = y - t_ref[rows, :]
            ls_ref[...] += jnp.sum(err * err, axis=0, keepdims=True)
            dy = err * (1.0 / D_MODEL)
            dy_ref[rows, :] = dy
            dm, dgr = _rms_bwd(m, g, dy)
            dm_ref[rows, :] = dm.astype(_BF)
            dg_ref[...] += jnp.sum(dgr, axis=0, keepdims=True)

    return pl.pallas_call(
        body, name="ff2_loss", grid=(L // tm,),
        in_specs=[_row_spec(tm, D_FF), _row_spec(tm, D_MODEL), _row_spec(tm, D_MODEL),
                  _full_spec((1, D_MODEL)), _weight_spec((D_FF, D_MODEL))],
        out_specs=[_row_spec(tm, D_MODEL), _row_spec(tm, D_MODEL), _full_spec((1, D_MODEL)), _full_spec((1, D_MODEL))],
        out_shape=[jax.ShapeDtypeStruct((L, D_MODEL), _F32), jax.ShapeDtypeStruct((L, D_MODEL), _BF),
                   jax.ShapeDtypeStruct((1, D_MODEL), _F32), jax.ShapeDtypeStruct((1, D_MODEL), _F32)],
        compiler_params=_params("arbitrary"),
    )(act, x2, tgt, g4, w2)


def _ff2_bwd(dm, act, w2, tm, tn):
    L = dm.shape[0]
    last = L // tm - 1

    def body(dm_ref, a_ref, w_ref, df_ref, dw_ref, acc):
        @pl.when(pl.program_id(1) == 0)
        def _():
            acc[...] = jnp.zeros_like(acc)

        dmv = dm_ref[...]
        av = a_ref[...]
        df_ref[...] = (_dot_nt(dmv, w_ref[...]) * jnp.sqrt(4.0 * av.astype(_F32))).astype(_BF)
        acc[...] += _dot_tn(av, dmv)

        @pl.when(pl.program_id(1) == last)
        def _():
            dw_ref[...] = acc[...].astype(_BF)

    return pl.pallas_call(
        body, name="ff2_bwd", grid=(D_FF // tn, L // tm),
        in_specs=[pl.BlockSpec((tm, D_MODEL), lambda j, i: (i, 0)), pl.BlockSpec((tm, tn), lambda j, i: (i, j)),
                  pl.BlockSpec((tn, D_MODEL), lambda j, i: (j, 0))],
        out_specs=[pl.BlockSpec((tm, tn), lambda j, i: (i, j)), pl.BlockSpec((tn, D_MODEL), lambda j, i: (j, 0))],
        out_shape=[jax.ShapeDtypeStruct((L, D_FF), _BF), jax.ShapeDtypeStruct((D_FF, D_MODEL), _BF)],
        scratch_shapes=[pltpu.VMEM((tn, D_MODEL), _F32)],
        compiler_params=_params("parallel", "arbitrary"),
    )(dm, act, w2)


def _ff1_bwd(df1, w1, x2, mix, dy, g3, g2, tm):
    L = df1.shape[0]

    def body(df_ref, w_ref, x2_ref, mix_ref, dy_ref, g3_ref, g2_ref, dx2_ref, dmix_ref, dg3_ref, dg2_ref):
        @pl.when(pl.program_id(0) == 0)
        def _():
            dg3_ref[...] = jnp.zeros_like(dg3_ref)
            dg2_ref[...] = jnp.zeros_like(dg2_ref)

        for rows in _row_chunks(tm):
            dh = _dot_nt(df_ref[rows, 0:FF1_COLS], w_ref[0])
            for j in range(1, N_DEV):
                dh = dh + _dot_nt(df_ref[rows, j * FF1_COLS:(j + 1) * FF1_COLS], w_ref[j])
            dz, dgr = _rms_bwd(x2_ref[rows, :], g3_ref[...], dh)
            dg3_ref[...] += jnp.sum(dgr, axis=0, keepdims=True)
            dx2 = dy_ref[rows, :] + dz
            dx2_ref[rows, :] = dx2
            dmx, dgr2 = _rms_bwd(mix_ref[rows, :].astype(_F32), g2_ref[...], dx2)
            dg2_ref[...] += jnp.sum(dgr2, axis=0, keepdims=True)
            dmix_ref[rows, :] = dmx.astype(_BF)

    vec = _full_spec((1, D_MODEL))
    return pl.pallas_call(
        body, name="ff1_bwd", grid=(L // tm,),
        in_specs=[_row_spec(tm, D_FF), _weight_spec((N_DEV, D_MODEL, FF1_COLS)), _row_spec(tm, D_MODEL),
                  _row_spec(tm, D_MODEL), _row_spec(tm, D_MODEL), vec, vec],
        out_specs=[_row_spec(tm, D_MODEL), _row_spec(tm, D_MODEL), vec, vec],
        out_shape=[jax.ShapeDtypeStruct((L, D_MODEL), _F32), jax.ShapeDtypeStruct((L, D_MODEL), _BF),
                   jax.ShapeDtypeStruct((1, D_MODEL), _F32), jax.ShapeDtypeStruct((1, D_MODEL), _F32)],
        compiler_params=_params("arbitrary"),
    )(df1, w1, x2, mix, dy, g3, g2)


def _matmul_tn(a, b, tm, tn, name, slots=0):
    L, K = a.shape
    N = b.shape[1]
    last = L // tm - 1

    def body(a_ref, b_ref, o_ref, acc):
        @pl.when(pl.program_id(1) == 0)
        def _():
            acc[...] = jnp.zeros_like(acc)

        acc[...] += _dot_tn(a_ref[...].astype(_BF), b_ref[...].astype(_BF))

        @pl.when(pl.program_id(1) == last)
        def _():
            if slots:
                for c in range(tn // slots):
                    o_ref[c] = acc[:, c * slots:(c + 1) * slots].astype(_BF)
            else:
                o_ref[...] = acc[...].astype(_BF)

    if slots:
        out_spec = pl.BlockSpec((tn // slots, K, slots), lambda j, i: (j, 0, 0))
        out_shape = jax.ShapeDtypeStruct((N // slots, K, slots), _BF)
    else:
        out_spec = pl.BlockSpec((K, tn), lambda j, i: (0, j))
        out_shape = jax.ShapeDtypeStruct((K, N), _BF)
    return pl.pallas_call(
        body, name=name, grid=(N // tn, L // tm),
        in_specs=[pl.BlockSpec((tm, K), lambda j, i: (i, 0)), pl.BlockSpec((tm, tn), lambda j, i: (i, j))],
        out_specs=out_spec, out_shape=out_shape,
        scratch_shapes=[pltpu.VMEM((K, tn), _F32)],
        compiler_params=_params("parallel", "arbitrary"),
    )(a, b)


def _dw_out(y_ret, gl, dmix, tk):
    L = dmix.shape[0]
    last = L // tk - 1

    def body(a0_ref, a1_ref, b_ref, o_ref, acc):
        @pl.when(pl.program_id(0) == 0)
        def _():
            acc[...] = jnp.zeros_like(acc)

        bv = b_ref[...]
        acc[:RET_W, :] += _dot_tn(a0_ref[...], bv)
        acc[RET_W:, :] += _dot_tn(a1_ref[...], bv)

        @pl.when(pl.program_id(0) == last)
        def _():
            o_ref[...] = acc[...].astype(_BF)

    return pl.pallas_call(
        body, name="dw_out", grid=(L // tk,),
        in_specs=[_row_spec(tk, RET_W), _row_spec(tk, SSM_W), _row_spec(tk, D_MODEL)],
        out_specs=_full_spec((D_MODEL, D_MODEL)), out_shape=jax.ShapeDtypeStruct((D_MODEL, D_MODEL), _BF),
        scratch_shapes=[pltpu.VMEM((D_MODEL, D_MODEL), _F32)],
        compiler_params=_params("arbitrary"),
    )(y_ret, gl, dmix)


def _dw_in_t(pieces, h, tk):
    L = h.shape[0]
    last = L // tk - 1

    def body(p0, p1, p2, p3, p4, h_ref, o_ref, acc):
        @pl.when(pl.program_id(0) == 0)
        def _():
            acc[...] = jnp.zeros_like(acc)

        hv = h_ref[...]
        for j, p in enumerate((p0, p1, p2, p3, p4)):
            acc[j * RET_W:(j + 1) * RET_W, :] += _dot_tn(p[...].astype(_BF), hv)

        @pl.when(pl.program_id(0) == last)
        def _():
            o_ref[...] = acc[...].astype(_BF)

    return pl.pallas_call(
        body, name="dw_in", grid=(L // tk,),
        in_specs=[_row_spec(tk, RET_W)] * 5 + [_row_spec(tk, D_MODEL)],
        out_specs=_full_spec((IN_COLS, D_MODEL)), out_shape=jax.ShapeDtypeStruct((IN_COLS, D_MODEL), _BF),
        scratch_shapes=[pltpu.VMEM((IN_COLS, D_MODEL), _F32)],
        compiler_params=_params("arbitrary"),
    )(*pieces, h)


def _mixout_bwd(dmix, w_out, w_glu, ys, s, o, gate, ggn, tm, after=()):
    L = dmix.shape[0]

    def body(dmix_ref, wo_ref, wg_ref, ys_ref, s_ref, o_ref, gate_ref, ggn_ref,
             dglu_ref, ds_ref, dgate_ref, do_ref, dggn_ref):
        @pl.when(pl.program_id(0) == 0)
        def _():
            dggn_ref[...] = jnp.zeros_like(dggn_ref)

        ggn = ggn_ref[...]
        for rows in _row_chunks(tm):
            dcat = _dot_nt(dmix_ref[rows, :], wo_ref[...])
            dy_ret, dy_ssm = dcat[:, :RET_W], dcat[:, RET_W:]
            glu = _dot(ys_ref[rows, :], wg_ref[...])
            ga, sg = glu[:, :SSM_W], _sigmoid(glu[:, SSM_W:])
            dga = (dy_ssm * sg).astype(_BF)
            dgb = (dy_ssm * ga * sg * (1.0 - sg)).astype(_BF)
            dglu_ref[rows, :SSM_W] = dga
            dglu_ref[rows, SSM_W:] = dgb
            dys = _dot_nt(dga, wg_ref[:, :SSM_W]) + _dot_nt(dgb, wg_ref[:, SSM_W:])
            ds_ref[rows, :] = dys * _gelu_grad(s_ref[rows, :])
            gt = gate_ref[rows, :]
            sgt = _sigmoid(gt)
            for hh in range(N_HEAD):
                cols = slice(hh * HEAD_D, (hh + 1) * HEAD_D)
                ov = o_ref[rows, cols]
                dlt = ov - jnp.mean(ov, axis=-1, keepdims=True)
                rstd = lax.rsqrt(jnp.mean(dlt * dlt, axis=-1, keepdims=True) + NORM_EPS)
                on = dlt * rstd
                dyr = dy_ret[:, cols] * (gt[:, cols] * sgt[:, cols])
                dgate_ref[rows, cols] = (dy_ret[:, cols] * (on * ggn[:, cols]) * (sgt[:, cols] * (1.0 + gt[:, cols] * (1.0 - sgt[:, cols])))).astype(_BF)
                dggn_ref[:, cols] += jnp.sum(dyr * on, axis=0, keepdims=True)
                don = dyr * ggn[:, cols]
                do = rstd * (don - jnp.mean(don, axis=-1, keepdims=True) - on * jnp.mean(don * on, axis=-1, keepdims=True))
                do_ref[rows, cols] = do.astype(_BF)

    body, in_specs, operands = _ordered(
        body, [_row_spec(tm, D_MODEL), _weight_spec((D_MODEL, D_MODEL)), _weight_spec((SSM_W, 2 * SSM_W)),
               _row_spec(tm, SSM_W), _row_spec(tm, SSM_W), _row_spec(tm, RET_W), _row_spec(tm, RET_W),
               _full_spec((1, RET_W))], (dmix, w_out, w_glu, ys, s, o, gate, ggn), after)
    return pl.pallas_call(
        body, name="mixout_bwd", grid=(L // tm,),
        in_specs=in_specs,
        out_specs=[_row_spec(tm, 2 * SSM_W), _row_spec(tm, SSM_W), _row_spec(tm, RET_W), _row_spec(tm, RET_W),
                   _full_spec((1, RET_W))],
        out_shape=[jax.ShapeDtypeStruct((L, 2 * SSM_W), _BF), jax.ShapeDtypeStruct((L, SSM_W), _F32),
                   jax.ShapeDtypeStruct((L, RET_W), _BF), jax.ShapeDtypeStruct((L, RET_W), _BF),
                   jax.ShapeDtypeStruct((1, RET_W), _F32)],
        compiler_params=_params("arbitrary"),
    )(*operands)


def _s5_bwd(u, ds, xs, ent, bmat, cmat, tab_r, pw_r, d_skip, tb, after=()):
    L = u.shape[0]
    nt = L // tb
    seg = tb // SUBLANES
    G = KB_PER_STEP
    rcol = pl.BlockSpec((tb, G * LANES), lambda kb, t: (nt - 1 - t, kb))
    sp = _s5_specs(seg, time=lambda t: nt - 1 - t)
    aspec = pl.BlockSpec((G, SUBLANES, 2 * KB_STATES), lambda kb, t: (kb, 0, 0))

    def body(u_ref, ds_ref, x_ref, ent_ref, b_ref, c_ref, tr_ref, pr_ref, d_ref,
             du_ref, db_ref, dc_ref, da_ref, dd_ref, up_scr, dp_scr, g_scr, lc_scr):
        @pl.when(pl.program_id(1) == 0)
        def _():
            lc_scr[...] = jnp.zeros_like(lc_scr)
            db_ref[...] = jnp.zeros_like(db_ref)
            dc_ref[...] = jnp.zeros_like(dc_ref)
            da_ref[...] = jnp.zeros_like(da_ref)
            dd_ref[...] = jnp.zeros_like(dd_ref)

        _rows_to_segments(up_scr, u_ref, seg)
        _rows_to_segments(dp_scr, ds_ref, seg)
        for g in range(G):
            g_scr[g] = _dot_nt(dp_scr[g].astype(_BF), c_ref[g]).reshape(seg, SUBLANES, 2 * KB_STATES)
        _scan_segments(g_scr, tr_ref, pr_ref, lc_scr, seg, reverse=True, fwd_ref=x_ref, fwd_entry_ref=ent_ref.at[:, 0],
                       da_ref=da_ref)
        for g in range(G):
            cols = slice(g * LANES, (g + 1) * LANES)
            uv, dsv = up_scr[g], dp_scr[g]
            ub, dsb = uv.astype(_BF), dsv.astype(_BF)
            lamb = g_scr[g].reshape(tb, 2 * KB_STATES).astype(_BF)
            db_ref[g] += _dot_tn(ub, lamb)
            dc_ref[g] += _dot_tn(dsb, x_ref[g].reshape(tb, 2 * KB_STATES).astype(_BF))
            dd_ref[:, cols] += jnp.sum(dsv * uv, axis=0, keepdims=True)
            up_scr[g] = _dot_nt(lamb, b_ref[g]) + d_ref[:, cols] * dsv
        _segments_to_rows(du_ref, up_scr, seg)

    body, in_specs, operands = _ordered(
        body, [rcol, rcol, sp["x"], sp["ent"], sp["b"], sp["c"], sp["tab"], sp["pw"], sp["d"]],
        (u, ds, xs, ent, bmat, cmat, tab_r, pw_r, d_skip), after)
    return pl.pallas_call(
        body, name="s5_bwd", grid=(N_KB // G, nt),
        in_specs=in_specs,
        out_specs=[rcol, sp["b"], sp["b"], aspec, sp["d"]],
        out_shape=[jax.ShapeDtypeStruct((L, SSM_W), _BF),
                   jax.ShapeDtypeStruct((N_KB, LANES, 2 * KB_STATES), _F32),
                   jax.ShapeDtypeStruct((N_KB, LANES, 2 * KB_STATES), _F32),
                   jax.ShapeDtypeStruct((N_KB, SUBLANES, 2 * KB_STATES), _F32),
                   jax.ShapeDtypeStruct((1, SSM_W), _F32)],
        scratch_shapes=[pltpu.VMEM((G, tb, LANES), _F32)] * 2
        + [pltpu.VMEM((G, seg, SUBLANES, 2 * KB_STATES), _F32), pltpu.VMEM((G, SUBLANES, 2 * KB_STATES), _F32)],
        compiler_params=_params("parallel", "arbitrary"),
    )(*operands)


def _retention_bwd(q, k, v, do, r_prev, consts, cosf, sinf, after=()):
    L = q.shape[0]
    nc = L // CHUNK
    cps = math.gcd(RET_STEP_CHUNKS, nc)
    nb = nc // cps
    blk = pl.BlockSpec((cps * CHUNK, RET_W), lambda n: (nb - 1 - n, 0))
    rope_blk = pl.BlockSpec((cps * CHUNK, HEAD_D), lambda n: (nb - 1 - n, 0))

    def body(q_ref, k_ref, v_ref, do_ref, rp_ref, dm_ref, xi_ref, zeta_ref, gc_ref, cos_ref, sin_ref,
             dq_ref, dk_ref, dv_ref, g_scr):
        @pl.when(pl.program_id(0) == 0)
        def _():
            g_scr[...] = jnp.zeros_like(g_scr)

        for hh in range(N_HEAD):
            cols = slice(hh * HEAD_D, (hh + 1) * HEAD_D)
            dm, zeta = dm_ref[hh], zeta_ref[hh]
            gst = g_scr[hh]
            for c in reversed(range(cps)):
                rows = slice(c * CHUNK, (c + 1) * CHUNK)
                qv, kv, vv, dov = q_ref[rows, cols], k_ref[rows, cols], v_ref[rows, cols], do_ref[rows, cols]
                rb = rp_ref[hh, c].astype(_BF)
                gb = gst.astype(_BF)
                sb = (_dot_nt(qv, kv) * dm).astype(_BF)
                dab = (_dot_nt(dov, vv) * dm).astype(_BF)
                dox = (dov.astype(_F32) * xi_ref[hh]).astype(_BF)
                vz = (vv.astype(_F32) * zeta).astype(_BF)
                dq = _dot(dab, kv) + _dot_nt(dox, rb)
                dk = _dot_tn(dab, qv) + _dot_nt(vz, gb)
                dv = _dot_tn(sb, dov) + _dot(kv, gb) * zeta
                gst = gc_ref[hh, 0:1, :] * gst + _dot_tn(qv, dox)
                cs, sn = cos_ref[rows, :], sin_ref[rows, :]
                dq_ref[rows, cols] = _rope_t(dq, cs, sn).astype(_BF)
                dk_ref[rows, cols] = (_rope_t(dk, cs, sn) * (HEAD_D ** -0.5)).astype(_BF)
                dv_ref[rows, cols] = dv.astype(_BF)
            g_scr[hh] = gst

    body, in_specs, operands = _ordered(
        body, [blk, blk, blk, blk, pl.BlockSpec((N_HEAD, cps, HEAD_D, HEAD_D), lambda n: (0, nb - 1 - n, 0, 0))]
        + _head_specs() + [rope_blk, rope_blk], (q, k, v, do, r_prev, *consts, cosf, sinf), after)
    return pl.pallas_call(
        body, name="retention_bwd", grid=(nb,),
        in_specs=in_specs,
        out_specs=[blk, blk, blk],
        out_shape=[jax.ShapeDtypeStruct((L, RET_W), _BF)] * 3,
        scratch_shapes=[pltpu.VMEM((N_HEAD, HEAD_D, HEAD_D), _F32)],
        compiler_params=_params("arbitrary"),
    )(*operands)


def _inproj_bwd(pieces, w_in_t, x, dx2, g1, tm, after=()):
    L = x.shape[0]

    def body(p0, p1, p2, p3, p4, w_ref, x_ref, dx2_ref, g_ref, dx_ref, dg_ref):
        @pl.when(pl.program_id(0) == 0)
        def _():
            dg_ref[...] = jnp.zeros_like(dg_ref)

        for rows in _row_chunks(tm):
            dh = None
            for j, p in enumerate((p0, p1, p2, p3, p4)):
                part = _dot(p[rows, :].astype(_BF), w_ref[j * RET_W:(j + 1) * RET_W, :])
                dh = part if dh is None else dh + part
            dz, dgr = _rms_bwd(x_ref[rows, :], g_ref[...], dh)
            dx_ref[rows, :] = dx2_ref[rows, :] + dz
            dg_ref[...] += jnp.sum(dgr, axis=0, keepdims=True)

    body, in_specs, operands = _ordered(
        body, [_row_spec(tm, RET_W)] * 5 + [_weight_spec((IN_COLS, D_MODEL)), _row_spec(tm, D_MODEL),
                                             _row_spec(tm, D_MODEL), _full_spec((1, D_MODEL))],
        (*pieces, w_in_t, x, dx2, g1), after)
    return pl.pallas_call(
        body, name="inproj_bwd", grid=(L // tm,),
        in_specs=in_specs,
        out_specs=[_row_spec(tm, D_MODEL), _full_spec((1, D_MODEL))],
        out_shape=[jax.ShapeDtypeStruct((L, D_MODEL), _F32), jax.ShapeDtypeStruct((1, D_MODEL), _F32)],
        compiler_params=_params("arbitrary"),
    )(*operands)


def _sum_adamw(parts, w, m, v, tr, name):
    _, R, Cc = parts.shape

    def body(p_ref, w_ref, m_ref, v_ref, g_ref, d_ref, nm_ref, nv_ref):
        gv = p_ref[0].astype(_F32)
        for s in range(1, N_DEV):
            gv = gv + p_ref[s].astype(_F32)
        g_ref[...] = gv
        nm = ADAM_B1 * m_ref[...] + (1.0 - ADAM_B1) * gv
        nv = ADAM_B2 * v_ref[...] + (1.0 - ADAM_B2) * (gv * gv)
        m_hat = nm / (1.0 - ADAM_B1 ** ADAM_STEP)
        v_hat = nv / (1.0 - ADAM_B2 ** ADAM_STEP)
        d_ref[...] = -ADAM_LR * (m_hat / (jnp.sqrt(v_hat) + ADAM_EPS) + ADAM_WD * w_ref[...])
        nm_ref[...] = nm
        nv_ref[...] = nv

    spec = _row_spec(tr, Cc)
    return pl.pallas_call(
        body, name=name, grid=(R // tr,),
        in_specs=[pl.BlockSpec((N_DEV, tr, Cc), lambda i: (0, i, 0))] + [spec] * 3, out_specs=[spec] * 4,
        out_shape=[jax.ShapeDtypeStruct((R, Cc), _F32)] * 4,
        compiler_params=_params("parallel"),
    )(parts, w, m, v)


def _my_place():
    return lax.axis_index("x"), lax.axis_index("y"), lax.axis_index("c")


HBM_SPEC = pl.BlockSpec(memory_space=pltpu.HBM)
SEM_SPEC = pl.BlockSpec(memory_space=pltpu.SEMAPHORE)
DATAFLOW = pltpu.SideEffectType.DATAFLOW_SIDE_EFFECTING


def _my_index():
    x, y, c = _my_place()
    return 4 * x + 2 * y + c


def _landing(own_block):
    return lax.empty((N_DEV,) + own_block.shape, own_block.dtype)


def _own_copy(src, land, sems, a, gather):
    me = _my_index()
    return pltpu.make_async_copy(src if gather else src.at[me], land.at[me], sems.at[sems.shape[0] // N_DEV * 7 + a])


def _split_copies(src_refs, land_refs, send_sems, recv_sems, gather, first=0):
    x, y, c = _my_place()
    me = 4 * x + 2 * y + c
    copies = []
    for a, (src, land) in enumerate(zip(src_refs, land_refs)):
        for kk in range(1, N_DEV):
            px, py, pc = x ^ (kk >> 2), y ^ ((kk >> 1) & 1), c ^ (kk & 1)
            peer = 4 * px + 2 * py + pc
            copies.append(pltpu.make_async_remote_copy(
                src_ref=src if gather else src.at[peer], dst_ref=land.at[me],
                send_sem=send_sems.at[(first + a) * 7 + kk - 1], recv_sem=recv_sems.at[(first + a) * 7 + kk - 1],
                device_id=(px, py, pc), device_id_type=MESH))
    return copies


def _split_start(srcs, lands, gather, name):
    n = len(srcs)

    def body(*refs):
        src_refs, land_refs = refs[:n], refs[n:2 * n]
        send_sems, recv_sems = refs[2 * n], refs[2 * n + 1]
        token = refs[-1]
        for cp in _split_copies(src_refs, land_refs, send_sems, recv_sems, gather):
            cp.start()
        for a in range(n):
            _own_copy(src_refs[a], land_refs[a], send_sems, a, gather).start()
        token[...] = jnp.zeros_like(token)

    outs = pl.pallas_call(
        body, name=name,
        out_shape=(pltpu.SemaphoreType.DMA((N_DEV * n,)), pltpu.SemaphoreType.DMA((7 * n,)),
                   *[pltpu.HBM(t.shape, t.dtype) for t in srcs], *[pltpu.HBM(t.shape, t.dtype) for t in lands],
                   jax.ShapeDtypeStruct((SUBLANES, LANES), _F32)),
        in_specs=[HBM_SPEC] * (2 * n),
        out_specs=(SEM_SPEC, SEM_SPEC, *[HBM_SPEC] * (2 * n), pl.BlockSpec(memory_space=pltpu.VMEM)),
        input_output_aliases={i: 2 + i for i in range(2 * n)},
        compiler_params=pltpu.CompilerParams(has_side_effects=DATAFLOW),
    )(*[pltpu.with_memory_space_constraint(t, pltpu.HBM) for t in list(srcs) + list(lands)])
    return outs[0], outs[1], outs[2:2 + n], outs[2 + n:2 + 2 * n], outs[-1]


def _split_wait(send_sems, recv_sems, srcs, lands, after, gather, name, first=0):
    n = len(srcs)

    def body(*refs):
        src_refs, land_refs = refs[:n], refs[n:2 * n]
        send_s, recv_s = refs[2 * n], refs[2 * n + 1]
        for cp in _split_copies(src_refs, land_refs, send_s, recv_s, gather, first):
            cp.wait_send()
            cp.wait_recv()
        for a in range(n):
            _own_copy(src_refs[a], land_refs[a], send_s, first + a, gather).wait()

    outs = pl.pallas_call(
        body, name=name,
        out_shape=tuple(pltpu.HBM(t.shape, t.dtype) for t in list(srcs) + list(lands)),
        in_specs=[HBM_SPEC] * (2 * n) + [SEM_SPEC, SEM_SPEC, pl.BlockSpec(memory_space=pl.ANY)],
        out_specs=tuple([HBM_SPEC] * (2 * n)),
        input_output_aliases={i: i for i in range(2 * n)},
        compiler_params=pltpu.CompilerParams(has_side_effects=DATAFLOW),
    )(*srcs, *lands, send_sems, recv_sems, after)
    return outs[n:]


def _discretize(lam_re, lam_im, log_dt, b_re, b_im):
    lr = jnp.minimum(lam_re, -1e-4)
    li = lam_im
    dt = jnp.exp(log_dt)[:, None]
    er = jnp.exp(lr * dt)
    ar, ai = er * jnp.cos(li * dt), er * jnp.sin(li * dt)
    den = lr * lr + li * li
    cr = ((ar - 1.0) * lr + ai * li) / den
    ci = (ai * lr - (ar - 1.0) * li) / den
    bbr = cr[:, :, None] * b_re - ci[:, :, None] * b_im
    bbi = cr[:, :, None] * b_im + ci[:, :, None] * b_re
    return ar, ai, bbr, bbi


def _cmul(ar, ai, br, bi):
    return ar * br - ai * bi, ar * bi + ai * br


def _cpowers(ar, ai, n):
    pr, pi = ar[None], ai[None]
    while pr.shape[0] < n:
        nr, ni = _cmul(pr, pi, pr[-1][None], pi[-1][None])
        pr, pi = jnp.concatenate([pr, nr]), jnp.concatenate([pi, ni])
    return pr[:n], pi[:n]


def _scan_tables(ar, ai, seg, reverse):
    if reverse:
        ai = -ai
    ar, ai = ar.reshape(N_KB, KB_STATES), ai.reshape(N_KB, KB_STATES)
    pr, pi = _cpowers(ar, ai, seg)
    a1 = (pr[-1], pi[-1])
    a2 = _cmul(*a1, *a1)
    a4 = _cmul(*a2, *a2)
    row = jnp.arange(SUBLANES)[None, :, None]
    wide = lambda t: jnp.broadcast_to(t[:, None, :], (N_KB, SUBLANES, KB_STATES))
    tabs = [wide(ar), wide(ai)]
    for dist, (qr, qi) in ((1, a1), (2, a2), (4, a4)):
        keep = (row < SUBLANES - dist) if reverse else (row >= dist)
        tabs += [jnp.where(keep, wide(qr), 0.0), jnp.where(keep, wide(qi), 0.0)]
    tabs += [wide(a1[0]), wide(a1[1])]
    if reverse:
        pr, pi = pr[::-1], pi[::-1]
    pw = jnp.transpose(jnp.concatenate([pr, pi], axis=-1), (1, 0, 2))[:, :, None, :]
    return jnp.stack(tabs, axis=1).astype(_F32), pw.astype(_F32)


def _block_diag_in(br, bi):
    eye = jnp.eye(GROUPS_PER_KB, dtype=_F32)
    one = lambda t: jnp.einsum("kgpc,gh->kgchp", t.reshape(N_KB, GROUPS_PER_KB, N_STATE, SSM_GC), eye).reshape(
        N_KB, LANES, KB_STATES)
    return jnp.concatenate([one(br), one(bi)], axis=-1)


def _block_diag_in_t(dmat):
    d6 = dmat.reshape(N_KB, GROUPS_PER_KB, SSM_GC, 2, GROUPS_PER_KB, N_STATE)
    eye = jnp.eye(GROUPS_PER_KB, dtype=_F32)
    both = jnp.einsum("kgcrhp,gh->rkgpc", d6, eye).reshape(2, N_GROUP, N_STATE, SSM_GC)
    return both[0], both[1]


def _block_diag_out(c_re, c_im):
    eye = jnp.eye(GROUPS_PER_KB, dtype=_F32)
    one = lambda t: jnp.einsum("kgcp,gh->khpgc", t.reshape(N_KB, GROUPS_PER_KB, SSM_GC, N_STATE), eye).reshape(
        N_KB, KB_STATES, LANES)
    return jnp.concatenate([one(c_re), -one(c_im)], axis=1)


def _block_diag_out_t(dmat_t):
    d6 = dmat_t.reshape(N_KB, GROUPS_PER_KB, SSM_GC, 2, GROUPS_PER_KB, N_STATE)
    eye = jnp.eye(GROUPS_PER_KB, dtype=_F32)
    both = jnp.einsum("kgcrhp,gh->rkgcp", d6, eye).reshape(2, N_GROUP, SSM_GC, N_STATE)
    return both[0], -both[1]


SMALL_NAMES = ("norm_mix_pre", "norm_mix_post", "ret_gn_gain", "ssm_lambda_re", "ssm_lambda_im", "ssm_log_dt",
               "ssm_b_re", "ssm_b_im", "ssm_c_re", "ssm_c_im", "ssm_d", "norm_mlp_pre", "norm_mlp_post")


def _local_grads(x, tgt, small, weights, emit, emit_small, tm, tk, tb, zero=0.0):
    L = x.shape[0]
    g1, g2, ggn = small["norm_mix_pre"], small["norm_mix_post"], small["ret_gn_gain"]
    g3, g4, d_skip = small["norm_mlp_pre"], small["norm_mlp_post"], small["ssm_d"]

    rope = _rope_tables(L)
    consts = _ret_consts()

    disc_in = (small["ssm_lambda_re"][0], small["ssm_lambda_im"][0], small["ssm_log_dt"][0] + zero,
               small["ssm_b_re"][0], small["ssm_b_im"][0])
    (ar, ai, bbr, bbi), disc_vjp = jax.vjp(_discretize, *disc_in)
    bmat = _block_diag_in(bbr, bbi).astype(_BF)
    cmat = _block_diag_out(small["ssm_c_re"][0], small["ssm_c_im"][0]).astype(_BF)
    seg = tb // SUBLANES
    tab_f, pw_f = _scan_tables(ar, ai, seg, False)
    tab_r, pw_r = _scan_tables(ar, ai, seg, True)

    h1 = _prenorm(x, g1, min(4 * tm, L), after=(pw_r,))
    (w_in_t,) = weights("in", h1)
    q, k, v, gate, u, cosf, sinf = _inproj_fwd(h1, w_in_t, rope, min(4 * tm, L))
    o, y_ret, r_prev = _retention_fwd(q, k, v, gate, ggn, consts)
    s, xs, ent = _s5_fwd(u, bmat, cmat, tab_f, pw_f, d_skip, tb)
    w_glu, w_out = weights("mix", s)
    ys, gl, mix, x2 = _mixout_fwd(s, y_ret, x, w_glu, w_out, g2, min(2 * tm, L))
    w_ff1, w_ff2 = weights("mlp", x2)
    h3, act = _ff1_fwd(x2, g3, w_ff1, min(2 * tm, L))
    dy, dm, dg4, sq = _ff2_loss(act, x2, tgt, g4, w_ff2, min(2 * tm, L))

    df1, dw_ff2 = _ff2_bwd(dm, act, w_ff2, min(1024, L), 1024)
    dx2, dmix, dg3, dg2 = _ff1_bwd(df1, w_ff1, x2, mix, dy, g3, g2, min(2 * tm, L))
    dw_ff1 = _matmul_tn(h3, df1, tk, 2 * FF1_COLS, "dw_ff1", slots=FF1_COLS)
    token = emit({"w_ff1": dw_ff1, "w_ff2": dw_ff2})
    dglu, ds, dgate, do, dggn = _mixout_bwd(dmix, w_out, w_glu, ys, s, o, gate, ggn, min(2 * tm, L), after=token)
    dw_out = _dw_out(y_ret, gl, dmix, tk)
    dw_glu = _matmul_tn(ys, dglu, tk, 1024, "dw_glu")
    token = emit({"w_glu": dw_glu, "w_out": dw_out})
    du, dbmat, dcmat, da8, dd = _s5_bwd(u, ds, xs, ent, bmat, cmat, tab_r, pw_r, d_skip, tb, after=token)

    da = jnp.sum(da8, axis=1)
    dar = da[:, :KB_STATES].reshape(N_GROUP, N_STATE)
    dai = da[:, KB_STATES:].reshape(N_GROUP, N_STATE)
    dbr, dbi = _block_diag_in_t(dbmat)
    dlre, dlim, dldt, dbre, dbim = disc_vjp((dar, dai, dbr, dbi))
    dcre, dcim = _block_diag_out_t(dcmat)
    token = emit_small({
        "norm_mix_post": dg2, "ret_gn_gain": dggn,
        "ssm_lambda_re": dlre[None], "ssm_lambda_im": dlim[None], "ssm_log_dt": dldt[None],
        "ssm_b_re": dbre[None], "ssm_b_im": dbim[None], "ssm_c_re": dcre[None], "ssm_c_im": dcim[None],
        "ssm_d": dd, "norm_mlp_pre": dg3, "norm_mlp_post": dg4,
    }, sq)

    dq, dk, dv = _retention_bwd(q, k, v, do, r_prev, consts, cosf, sinf, after=token)
    pieces = (dq, dk, dv, dgate, du)
    dw_in_t = _dw_in_t(pieces, h1, min(1024, L))
    token = emit({"w_in": dw_in_t})
    gx, dg1 = _inproj_bwd(pieces, w_in_t, x, dx2, g1, min(2 * tm, L), after=token)
    return gx, dg1


BIG_SHAPES = {"w_in": (D_MODEL, IN_COLS // N_DEV), "w_glu": (SSM_W, 2 * SSM_W // N_DEV), "w_out": (D_MODEL // N_DEV, D_MODEL),
              "w_ff1": (D_MODEL, FF1_COLS), "w_ff2": (D_FF // N_DEV, D_MODEL)}
BIG_NAMES = ("w_in", "w_glu", "w_out", "w_ff1", "w_ff2")


def _cols_from_slots(g):
    return jnp.transpose(g, (1, 0, 2)).reshape(g.shape[1], N_DEV * g.shape[2])


def _cols_to_slots(dw):
    r, cols = dw.shape
    return jnp.transpose(dw.reshape(r, N_DEV, cols // N_DEV), (1, 0, 2))


WEIGHT_GROUPS = {"in": ("w_in",), "mix": ("w_glu", "w_out"), "mlp": ("w_ff1", "w_ff2")}


def _weight_from_slots(name, g):
    if name == "w_glu":
        return _cols_from_slots(g)
    if name == "w_ff1":
        return g
    return g.reshape(N_DEV * g.shape[1], g.shape[2])


def _grad_slots(name, dw):
    if name == "w_glu":
        return _cols_to_slots(dw)
    if name == "w_ff1":
        return dw
    if name == "w_in":
        return dw.reshape(N_DEV, BIG_SHAPES[name][1], BIG_SHAPES[name][0])
    return dw.reshape((N_DEV,) + BIG_SHAPES[name])


PIECE_ROWS = 8


VEC_NAMES = tuple(n for n in SMALL_NAMES if n[:6] not in ("ssm_b_", "ssm_c_"))
BC_NAMES = ("ssm_b_re", "ssm_b_im", "ssm_c_re", "ssm_c_im")
BC_ROWS = N_GROUP * SSM_GC


def _bc_view(name, t):
    t = t[0]
    if name.startswith("ssm_b_"):
        t = jnp.swapaxes(t, 1, 2)
    return t.reshape(BC_ROWS, N_STATE)


def _bc_unview(name, t):
    t = t.reshape(N_GROUP, SSM_GC, N_STATE)
    if name.startswith("ssm_b_"):
        t = jnp.swapaxes(t, 1, 2)
    return t[None]


def _pack_bc(vals):
    return jnp.concatenate([_bc_view(n, vals[n]).astype(_F32) for n in BC_NAMES], axis=0)


def _unpack_bc(buf):
    return {n: _bc_unview(n, buf[j * BC_ROWS:(j + 1) * BC_ROWS]) for j, n in enumerate(BC_NAMES)}


def _small_layout(shapes):
    off, rows = {}, 0
    for n in VEC_NAMES:
        off[n] = rows
        rows += -(-math.prod(shapes[n]) // (PIECE_ROWS * LANES)) * PIECE_ROWS
    return off, rows, rows + PIECE_ROWS


def _pack_small(vals, shapes, last=None):
    parts = []
    for n in VEC_NAMES:
        flat = vals[n].reshape(-1).astype(_F32)
        pad = -flat.shape[0] % (PIECE_ROWS * LANES)
        if pad:
            flat = jnp.concatenate([flat, jnp.zeros((pad,), _F32)])
        parts.append(flat.reshape(-1, LANES))
    parts.append(jnp.zeros((PIECE_ROWS, LANES), _F32) if last is None else last)
    return jnp.concatenate(parts, axis=0)


def _unpack_small(buf, shapes):
    off, _, _ = _small_layout(shapes)
    out = {}
    for n in VEC_NAMES:
        size = math.prod(shapes[n])
        rows = -(-size // LANES)
        out[n] = buf[off[n]:off[n] + rows].reshape(-1)[:size].reshape(shapes[n])
    return out


WEIGHT_NAMES = ('norm_mix_pre', 'norm_mix_post', 'w_in', 'ret_gn_gain', 'ssm_lambda_re', 'ssm_lambda_im', 'ssm_log_dt',
                'ssm_b_re', 'ssm_b_im', 'ssm_c_re', 'ssm_c_im', 'ssm_d', 'w_glu', 'w_out', 'norm_mlp_pre',
                'norm_mlp_post', 'w_ff1', 'w_ff2')


def kernel(x, norm_mix_pre, norm_mix_post, w_in, ret_gn_gain, ssm_lambda_re, ssm_lambda_im, ssm_log_dt, ssm_b_re, ssm_b_im, ssm_c_re, ssm_c_im, ssm_d, w_glu, w_out, norm_mlp_pre, norm_mlp_post, w_ff1, w_ff2, loss_target, m_norm_mix_pre, m_norm_mix_post, m_w_in, m_ret_gn_gain, m_ssm_lambda_re, m_ssm_lambda_im, m_ssm_log_dt, m_ssm_b_re, m_ssm_b_im, m_ssm_c_re, m_ssm_c_im, m_ssm_d, m_w_glu, m_w_out, m_norm_mlp_pre, m_norm_mlp_post, m_w_ff1, m_w_ff2, v_norm_mix_pre, v_norm_mix_post, v_w_in, v_ret_gn_gain, v_ssm_lambda_re, v_ssm_lambda_im, v_ssm_log_dt, v_ssm_b_re, v_ssm_b_im, v_ssm_c_re, v_ssm_c_im, v_ssm_d, v_w_glu, v_w_out, v_norm_mlp_pre, v_norm_mlp_post, v_w_ff1, v_w_ff2):
    args = dict(locals())
    w = {n: args[n] for n in WEIGHT_NAMES}
    m = {n: args["m_" + n] for n in WEIGHT_NAMES}
    v = {n: args["v_" + n] for n in WEIGHT_NAMES}
    L = x.shape[1]
    tm = min(256, L)
    tk = min(2048, L)
    tb = min(1024, L)

    calls = {"in": ("w_in",), "rest": WEIGHT_GROUPS["mix"] + WEIGHT_GROUPS["mlp"]}
    started, zero = {}, jnp.zeros((), _F32)
    for call, names in calls.items():
        blocks = [(w[n][0].T if n == "w_in" else w[n][0]).astype(_BF) for n in names]
        blocks[0] = blocks[0] + zero.astype(_BF)
        started[call] = _split_start(blocks, [_landing(b) for b in blocks], True, "weights_start_" + call)
        zero = started[call][4][0, 0]

    def weights(group, after):
        names = WEIGHT_GROUPS[group]
        call = "in" if group == "in" else "rest"
        first = calls[call].index(names[0])
        part = slice(first, first + len(names))
        got = started[call]
        landed = _split_wait(got[0], got[1], got[2][part], got[3][part], after, True, "weights_wait_" + group, first=first)
        return [_weight_from_slots(n, g) for n, g in zip(names, landed)]

    in_flight = []

    def emit(dws):
        names = sorted(dws)
        srcs = [_grad_slots(n, dws[n]) for n in names]
        lands = [_landing(t[0]) for t in srcs]
        started = _split_start(srcs, lands, False, "grads_start_" + "_".join(names))
        in_flight.append((names, started))
        return (started[4],)

    shapes = {n: w[n].shape for n in SMALL_NAMES}
    first_piece = {SMALL_NAMES[0]: jnp.zeros(shapes[SMALL_NAMES[0]], _F32)}
    small_flight = []

    def emit_small(gs, sq):
        loss_rows = jnp.broadcast_to(0.5 / D_MODEL * jnp.sum(sq), (PIECE_ROWS, LANES)).astype(_F32)
        bufs = [_pack_small({**first_piece, **gs}, shapes, loss_rows), _pack_bc(gs)]
        small_flight.append(_split_start(bufs, [_landing(b) for b in bufs], True, "small_grads_start"))
        return (small_flight[0][4],)

    small_w = {n: w[n] for n in SMALL_NAMES}
    gx, dg1 = _local_grads(x[0], loss_target[0], small_w, weights, emit, emit_small, tm, tk, tb, zero=zero)
    last_buf = dg1.reshape(PIECE_ROWS, LANES)
    last_started = _split_start([last_buf], [_landing(last_buf)], True, "last_grad_start")

    grads, delta, new_m, new_v = {}, {}, {}, {}
    after = last_started[4]
    for names, started in in_flight:
        landed = _split_wait(*started[:4], after, False, "grads_wait_" + "_".join(names))
        for n, parts in zip(names, landed):
            flip = (lambda t: t.T) if n == "w_in" else (lambda t: t)
            res = _sum_adamw(parts, flip(w[n][0]), flip(m[n][0]), flip(v[n][0]), math.gcd(256, parts.shape[1]), "adamw_" + n)
            grads[n], delta[n], new_m[n], new_v[n] = (flip(t)[None] for t in res)
        after = res[1]
    small_parts, bc_parts = _split_wait(*small_flight[0][:4], after, True, "small_grads_wait")
    last_parts = _split_wait(*last_started[:4], small_parts, True, "last_grad_wait")[0]
    small_parts = lax.dynamic_update_slice(small_parts, last_parts, (0, 0, 0))
    res_bc = _sum_adamw(bc_parts, _pack_bc(w), _pack_bc(m), _pack_bc(v), BC_ROWS, "adamw_bc")
    sw, sm, sv = _pack_small(w, shapes), _pack_small(m, shapes), _pack_small(v, shapes)
    res = _sum_adamw(small_parts, sw, sm, sv, sw.shape[0], "adamw_small")
    for dst, buf, buf_bc in zip((grads, delta, new_m, new_v), res, res_bc):
        dst.update(_unpack_small(buf, shapes))
        dst.update(_unpack_bc(buf_bc))
    _, loss_at, _ = _small_layout(shapes)
    loss = res[0][loss_at, 0]

    return (loss, gx[None], *[grads[n] for n in WEIGHT_NAMES], *[delta[n] for n in WEIGHT_NAMES],
            *[new_m[n] for n in WEIGHT_NAMES], *[new_v[n] for n in WEIGHT_NAMES])
```

```python
import math

import jax
import jax.numpy as jnp
from jax import lax
from jax.experimental import pallas as pl
from jax.experimental.pallas import tpu as pltpu

_BF = jnp.bfloat16
_F32 = jnp.float32

D_MODEL = 1024
RET_W = 512
N_HEAD = 4
HEAD_D = 128
CHUNK = 256
ROPE_CHUNK = 128
SSM_W = 512
SSM_GC = 16
N_GROUP = 32
N_STATE = 64
GROUPS_PER_KB = 8
N_KB = 4
KB_STATES = GROUPS_PER_KB * N_STATE
D_FF = 4096
IN_COLS = 2560
NORM_EPS = 1e-6
ROPE_BASE = 10000.0
N_DEV = 8

ADAM_LR = 0.001
ADAM_B1 = 0.9
ADAM_B2 = 0.999
ADAM_EPS = 1e-08
ADAM_WD = 0.01
ADAM_STEP = 10

SUBLANES = 8
LANES = 128
VMEM_LIMIT = 52 * 1024 * 1024
RET_STEP_CHUNKS = 2
KB_PER_STEP = 2
SCAN_UNROLL = True
FIX_UNROLL = 8

MESH = pl.DeviceIdType.MESH


def _params(*sem):
    return pltpu.CompilerParams(dimension_semantics=sem, vmem_limit_bytes=VMEM_LIMIT)


def _dot(a, b):
    return jnp.dot(a, b, preferred_element_type=_F32)


def _dot_nt(a, b):
    return lax.dot_general(a, b, (((1,), (1,)), ((), ())), preferred_element_type=_F32)


def _dot_tn(a, b):
    return lax.dot_general(a, b, (((0,), (0,)), ((), ())), preferred_element_type=_F32)


def _rms_r(z):
    return lax.rsqrt(jnp.mean(z * z, axis=-1, keepdims=True) + NORM_EPS)


def _rms_bwd(z, g, dn):
    r = _rms_r(z)
    t = dn * g
    dz = r * t - z * (r * r * r * jnp.mean(t * z, axis=-1, keepdims=True))
    return dz, dn * z * r


def _rope(t, cs, sn):
    return t * cs + pltpu.roll(t, HEAD_D // 2, 1) * sn


def _rope_t(t, cs, sn):
    return t * cs - pltpu.roll(t, HEAD_D // 2, 1) * sn


def _sigmoid(z):
    return 1.0 / (1.0 + jnp.exp(-z))


_GELU_C = math.sqrt(2.0 / math.pi)


def _gelu(z):
    return 0.5 * z * (1.0 + jnp.tanh(_GELU_C * (z + 0.044715 * z * z * z)))


def _gelu_grad(z):
    th = jnp.tanh(_GELU_C * (z + 0.044715 * z * z * z))
    return 0.5 * (1.0 + th) + 0.5 * z * (1.0 - th * th) * _GELU_C * (1.0 + 3 * 0.044715 * z * z)


ROW_CHUNK = 256


def _row_chunks(tm):
    return [pl.ds(i, min(ROW_CHUNK, tm)) for i in range(0, tm, ROW_CHUNK)]


def _ordered(body, in_specs, operands, after):
    k = len(after)
    if not k:
        return body, list(in_specs), tuple(operands)
    return ((lambda *refs: body(*refs[k:])), [pl.BlockSpec(memory_space=pl.ANY)] * k + list(in_specs),
            tuple(after) + tuple(operands))


def _row_spec(tm, n):
    return pl.BlockSpec((tm, n), lambda i: (i, 0))


def _full_spec(shape):
    nd = len(shape)
    return pl.BlockSpec(shape, lambda *_: (0,) * nd)


def _weight_spec(shape):
    nd = len(shape)
    return pl.BlockSpec(shape, lambda *_: (0,) * nd, pipeline_mode=pl.Buffered(1))


def _rope_tables(L):
    half = HEAD_D // 2
    inv_freq = ROPE_BASE ** (-jnp.arange(half, dtype=_F32) / half)
    twice = lambda t: jnp.concatenate([t, t], axis=-1)
    off = jnp.arange(ROPE_CHUNK, dtype=_F32)[:, None] * inv_freq[None, :]
    start = (ROPE_CHUNK * jnp.arange(L // ROPE_CHUNK, dtype=_F32))[:, None] * inv_freq[None, :]
    return (twice(jnp.cos(off)), twice(jnp.sin(off)),
            twice(jnp.cos(start))[:, None, :], twice(jnp.sin(start))[:, None, :])


def _prenorm(x, g, tm, after=()):
    L = x.shape[0]

    def body(x_ref, g_ref, h_ref):
        xv = x_ref[...]
        h_ref[...] = (xv * _rms_r(xv) * g_ref[...]).astype(_BF)

    body, in_specs, operands = _ordered(body, [_row_spec(tm, D_MODEL), _full_spec((1, D_MODEL))], (x, g), after)
    return pl.pallas_call(
        body, name="prenorm", grid=(L // tm,),
        in_specs=in_specs, out_specs=_row_spec(tm, D_MODEL),
        out_shape=jax.ShapeDtypeStruct((L, D_MODEL), _BF),
        compiler_params=_params("parallel"),
    )(*operands)


def _inproj_fwd(h, w_in_t, rope, tm):
    L = h.shape[0]
    n_chunks = tm // ROPE_CHUNK

    def body(h_ref, w_ref, co_ref, so_ref, cs_ref, ss_ref, q_ref, k_ref, v_ref, gate_ref, u_ref, cos_ref, sin_ref):
        proj = _dot_nt(h_ref[...], w_ref[...])
        lane = lax.broadcasted_iota(jnp.int32, (ROPE_CHUNK, HEAD_D), 1)
        sign = jnp.where(lane < HEAD_D // 2, -1.0, 1.0)
        co, so = co_ref[...], so_ref[...]
        for c in range(n_chunks):
            chunk = pl.program_id(0) * n_chunks + c
            cst, sst = cs_ref[chunk], ss_ref[chunk]
            rows = slice(c * ROPE_CHUNK, (c + 1) * ROPE_CHUNK)
            cs = co * cst - so * sst
            sn = (so * cst + co * sst) * sign
            cos_ref[rows, :] = cs
            sin_ref[rows, :] = sn
            for hh in range(N_HEAD):
                lo = hh * HEAD_D
                q_ref[rows, lo:lo + HEAD_D] = _rope(proj[rows, lo:lo + HEAD_D], cs, sn).astype(_BF)
                kh = _rope(proj[rows, RET_W + lo:RET_W + lo + HEAD_D], cs, sn) * (HEAD_D ** -0.5)
                k_ref[rows, lo:lo + HEAD_D] = kh.astype(_BF)
        v_ref[...] = proj[:, 2 * RET_W:3 * RET_W].astype(_BF)
        gate_ref[...] = proj[:, 3 * RET_W:4 * RET_W]
        u_ref[...] = proj[:, 4 * RET_W:]

    nc = L // ROPE_CHUNK
    return pl.pallas_call(
        body, name="inproj_fwd", grid=(L // tm,),
        in_specs=[_row_spec(tm, D_MODEL), _weight_spec((IN_COLS, D_MODEL)),
                  _full_spec((ROPE_CHUNK, HEAD_D)), _full_spec((ROPE_CHUNK, HEAD_D)),
                  _full_spec((nc, 1, HEAD_D)), _full_spec((nc, 1, HEAD_D))],
        out_specs=[_row_spec(tm, RET_W)] * 5 + [_row_spec(tm, HEAD_D)] * 2,
        out_shape=[jax.ShapeDtypeStruct((L, RET_W), _BF)] * 3 + [jax.ShapeDtypeStruct((L, RET_W), _F32)] * 2
        + [jax.ShapeDtypeStruct((L, HEAD_D), _F32)] * 2,
        compiler_params=_params("parallel"),
    )(h, w_in_t, *rope)


def _ret_consts():
    lg = jnp.log(1.0 - jnp.exp(jnp.linspace(math.log(1.0 / 32), math.log(1.0 / 512), N_HEAD))).astype(_F32)
    idx = jnp.arange(CHUNK, dtype=_F32)
    diff = idx[:, None] - idx[None, :]
    decay = jnp.where(diff[None] >= 0, jnp.exp(jnp.maximum(diff, 0.0)[None] * lg[:, None, None]), 0.0)
    zeta = jnp.exp((CHUNK - 1 - idx)[None, :] * lg[:, None])
    xi = jnp.exp((idx + 1.0)[None, :] * lg[:, None])
    gc = jnp.exp(CHUNK * lg)
    wide = lambda t: jnp.broadcast_to(t[:, :, None], (N_HEAD, CHUNK, HEAD_D)).astype(_F32)
    gcw = jnp.broadcast_to(gc[:, None, None], (N_HEAD, SUBLANES, HEAD_D)).astype(_F32)
    return decay.astype(_F32), wide(xi), wide(zeta), gcw


def _head_specs():
    wide = _full_spec((N_HEAD, CHUNK, HEAD_D))
    return [_full_spec((N_HEAD, CHUNK, CHUNK)), wide, wide, _full_spec((N_HEAD, SUBLANES, HEAD_D))]


def _retention_fwd(q, k, v, gate, ggn, consts):
    L = q.shape[0]
    nc = L // CHUNK
    cps = math.gcd(RET_STEP_CHUNKS, nc)
    blk = pl.BlockSpec((cps * CHUNK, RET_W), lambda n: (n, 0))

    def body(q_ref, k_ref, v_ref, gate_ref, ggn_ref, dm_ref, xi_ref, zeta_ref, gc_ref,
             o_ref, y_ref, rp_ref, r_scr):
        @pl.when(pl.program_id(0) == 0)
        def _():
            r_scr[...] = jnp.zeros_like(r_scr)

        for hh in range(N_HEAD):
            cols = slice(hh * HEAD_D, (hh + 1) * HEAD_D)
            state = r_scr[hh]
            for c in range(cps):
                rows = slice(c * CHUNK, (c + 1) * CHUNK)
                qv, kv, vv = q_ref[rows, cols], k_ref[rows, cols], v_ref[rows, cols]
                s = _dot_nt(qv, kv) * dm_ref[hh]
                o = _dot(s.astype(_BF), vv) + _dot(qv, state.astype(_BF)) * xi_ref[hh]
                o_ref[rows, cols] = o
                rp_ref[hh, c] = state
                vz = (vv.astype(_F32) * zeta_ref[hh]).astype(_BF)
                state = gc_ref[hh, 0:1, :] * state + _dot_tn(kv, vz)
                dlt = o - jnp.mean(o, axis=-1, keepdims=True)
                on = dlt * lax.rsqrt(jnp.mean(dlt * dlt, axis=-1, keepdims=True) + NORM_EPS)
                gt = gate_ref[rows, cols]
                y_ref[rows, cols] = (gt * _sigmoid(gt) * (on * ggn_ref[:, cols])).astype(_BF)
            r_scr[hh] = state

    return pl.pallas_call(
        body, name="retention_fwd", grid=(nc // cps,),
        in_specs=[blk, blk, blk, blk, _full_spec((1, RET_W))] + _head_specs(),
        out_specs=[blk, blk, pl.BlockSpec((N_HEAD, cps, HEAD_D, HEAD_D), lambda n: (0, n, 0, 0))],
        out_shape=[jax.ShapeDtypeStruct((L, RET_W), _F32), jax.ShapeDtypeStruct((L, RET_W), _BF),
                   jax.ShapeDtypeStruct((N_HEAD, nc, HEAD_D, HEAD_D), _F32)],
        scratch_shapes=[pltpu.VMEM((N_HEAD, HEAD_D, HEAD_D), _F32)],
        compiler_params=_params("arbitrary"),
    )(q, k, v, gate, ggn, *consts)


def _rows_to_segments(dst_scr, src_ref, seg):
    for g in range(dst_scr.shape[0]):
        for j in range(SUBLANES):
            dst_scr[g, pl.ds(j, seg, stride=SUBLANES), :] = src_ref[pl.ds(j * seg, seg), g * LANES:(g + 1) * LANES]


def _segments_to_rows(dst_ref, src_scr, seg):
    for g in range(src_scr.shape[0]):
        for j in range(SUBLANES):
            dst_ref[pl.ds(j * seg, seg), g * LANES:(g + 1) * LANES] = src_scr[g, pl.ds(j, seg, stride=SUBLANES), :].astype(dst_ref.dtype)


def _scan_segments(x_ref, tab_ref, pw_ref, carry_ref, seg, reverse, entry_ref=None, fwd_ref=None, fwd_entry_ref=None,
                   da_ref=None):
    G = x_ref.shape[0]
    W = KB_STATES
    re, im = pl.ds(0, W), pl.ds(W, W)
    row_id = lax.broadcasted_iota(jnp.int32, (SUBLANES, W), 0)
    edge_in = (row_id == SUBLANES - 1) if reverse else (row_id == 0)
    edge_out = 0 if reverse else SUBLANES - 1
    a_tab = [(tab_ref[g, 0], tab_ref[g, 1]) for g in range(G)]

    def local(i, st):
        r = (seg - 1 - i) if reverse else i
        out = []
        for g in range(G):
            (ar, ai), (sr, si) = a_tab[g], st[g]
            nr = ar * sr - ai * si + x_ref[g, r, :, re]
            ni = ar * si + ai * sr + x_ref[g, r, :, im]
            x_ref[g, r, :, re] = nr
            x_ref[g, r, :, im] = ni
            out.append((nr, ni))
        return tuple(out)

    zero = jnp.zeros((SUBLANES, W), _F32)
    ends = lax.fori_loop(0, seg, local, tuple((zero, zero) for _ in range(G)), unroll=SCAN_UNROLL)

    entry = []
    shift = (SUBLANES - 1) if reverse else 1
    for g in range(G):
        er, ei = ends[g]
        fr = jnp.where(edge_in, carry_ref[g, :, re], pltpu.roll(er, shift, 0))
        fi = jnp.where(edge_in, carry_ref[g, :, im], pltpu.roll(ei, shift, 0))
        for j, dist in enumerate((1, 2, 4)):
            pr, pi = tab_ref[g, 2 + 2 * j], tab_ref[g, 3 + 2 * j]
            sh = (SUBLANES - dist) if reverse else dist
            sr, si = pltpu.roll(fr, sh, 0), pltpu.roll(fi, sh, 0)
            fr, fi = fr + pr * sr - pi * si, fi + pr * si + pi * sr
        br, bi = tab_ref[g, 8], tab_ref[g, 9]
        outr = br * fr - bi * fi + er
        outi = br * fi + bi * fr + ei
        carry_ref[g, :, re] = jnp.broadcast_to(outr[edge_out:edge_out + 1, :], (SUBLANES, W))
        carry_ref[g, :, im] = jnp.broadcast_to(outi[edge_out:edge_out + 1, :], (SUBLANES, W))
        entry.append((fr, fi))
        if entry_ref is not None:
            entry_ref[g, :, re] = fr
            entry_ref[g, :, im] = fi

    add_da = da_ref is not None

    def fix(r, st, first=False):
        out = []
        for g in range(G):
            fr, fi = entry[g]
            pwr, pwi = pw_ref[g, r, :, re], pw_ref[g, r, :, im]
            xr = x_ref[g, r, :, re] + (pwr * fr - pwi * fi)
            xi = x_ref[g, r, :, im] + (pwr * fi + pwi * fr)
            x_ref[g, r, :, re] = xr
            x_ref[g, r, :, im] = xi
            if add_da:
                prev = fwd_entry_ref.at[g] if first else fwd_ref.at[g, r - 1]
                xpr, xpi = prev[:, re], prev[:, im]
                out.append((st[g][0] + (xr * xpr + xi * xpi), st[g][1] + (xi * xpr - xr * xpi)))
            else:
                out.append(st[g])
        return tuple(out)

    if add_da:
        st = fix(0, tuple((zero, zero) for _ in range(G)), first=True)
        st = lax.fori_loop(1, seg, fix, st, unroll=SCAN_UNROLL)
        for g in range(G):
            da_ref[g, :, re] += st[g][0]
            da_ref[g, :, im] += st[g][1]
    else:
        lax.fori_loop(0, seg, fix, tuple((zero[0:1, 0:LANES],) for _ in range(G)), unroll=FIX_UNROLL)


def _s5_specs(seg, time=lambda t: t):
    G = KB_PER_STEP
    return dict(
        x=pl.BlockSpec((G, seg, SUBLANES, 2 * KB_STATES), lambda kb, t: (kb, time(t), 0, 0)),
        ent=pl.BlockSpec((G, 1, SUBLANES, 2 * KB_STATES), lambda kb, t: (kb, time(t), 0, 0)),
        b=pl.BlockSpec((G, LANES, 2 * KB_STATES), lambda kb, t: (kb, 0, 0)),
        c=pl.BlockSpec((G, 2 * KB_STATES, LANES), lambda kb, t: (kb, 0, 0)),
        tab=pl.BlockSpec((G, 10, SUBLANES, KB_STATES), lambda kb, t: (kb, 0, 0, 0)),
        pw=pl.BlockSpec((G, seg, 1, 2 * KB_STATES), lambda kb, t: (kb, 0, 0, 0)),
        d=pl.BlockSpec((1, G * LANES), lambda kb, t: (0, kb)),
    )


def _s5_fwd(u, bmat, cmat, tab_f, pw_f, d_skip, tb):
    L = u.shape[0]
    nt = L // tb
    seg = tb // SUBLANES
    G = KB_PER_STEP
    ucol = pl.BlockSpec((tb, G * LANES), lambda kb, t: (t, kb))
    sp = _s5_specs(seg)

    def body(u_ref, b_ref, c_ref, tab_ref, pw_ref, d_ref, s_ref, x_ref, ent_ref, up_scr, y_scr, carry_scr):
        @pl.when(pl.program_id(1) == 0)
        def _():
            carry_scr[...] = jnp.zeros_like(carry_scr)

        _rows_to_segments(up_scr, u_ref, seg)
        for g in range(G):
            x_ref[g] = _dot(up_scr[g].astype(_BF), b_ref[g]).reshape(seg, SUBLANES, 2 * KB_STATES)
        _scan_segments(x_ref, tab_ref, pw_ref, carry_scr, seg, reverse=False, entry_ref=ent_ref.at[:, 0])
        for g in range(G):
            y = _dot(x_ref[g].reshape(tb, 2 * KB_STATES).astype(_BF), c_ref[g])
            y_scr[g] = y + d_ref[:, g * LANES:(g + 1) * LANES] * up_scr[g]
        _segments_to_rows(s_ref, y_scr, seg)

    return pl.pallas_call(
        body, name="s5_fwd", grid=(N_KB // G, nt),
        in_specs=[ucol, sp["b"], sp["c"], sp["tab"], sp["pw"], sp["d"]],
        out_specs=[ucol, sp["x"], sp["ent"]],
        out_shape=[jax.ShapeDtypeStruct((L, SSM_W), _F32),
                   jax.ShapeDtypeStruct((N_KB, L // SUBLANES, SUBLANES, 2 * KB_STATES), _F32),
                   jax.ShapeDtypeStruct((N_KB, nt, SUBLANES, 2 * KB_STATES), _F32)],
        scratch_shapes=[pltpu.VMEM((G, tb, LANES), _F32)] * 2 + [pltpu.VMEM((G, SUBLANES, 2 * KB_STATES), _F32)],
        compiler_params=_params("parallel", "arbitrary"),
    )(u, bmat, cmat, tab_f, pw_f, d_skip)


def _mixout_fwd(s, y_ret, x, w_glu, w_out, g2, tm):
    L = s.shape[0]

    def body(s_ref, yr_ref, x_ref, wg_ref, wo_ref, g_ref, ys_ref, gl_ref, mix_ref, x2_ref, cat_scr):
        for rows in _row_chunks(tm):
            ys = _gelu(s_ref[rows, :]).astype(_BF)
            ys_ref[rows, :] = ys
            glu = _dot(ys, wg_ref[...])
            gl = (glu[:, :SSM_W] * _sigmoid(glu[:, SSM_W:])).astype(_BF)
            gl_ref[rows, :] = gl
            cat_scr[rows, :RET_W] = yr_ref[rows, :]
            cat_scr[rows, RET_W:] = gl
            mix = _dot(cat_scr[rows, :], wo_ref[...])
            mix_ref[rows, :] = mix.astype(_BF)
            x2_ref[rows, :] = x_ref[rows, :] + mix * _rms_r(mix) * g_ref[...]

    return pl.pallas_call(
        body, name="mixout_fwd", grid=(L // tm,),
        in_specs=[_row_spec(tm, SSM_W), _row_spec(tm, RET_W), _row_spec(tm, D_MODEL),
                  _weight_spec((SSM_W, 2 * SSM_W)), _weight_spec((D_MODEL, D_MODEL)), _full_spec((1, D_MODEL))],
        out_specs=[_row_spec(tm, SSM_W), _row_spec(tm, SSM_W), _row_spec(tm, D_MODEL), _row_spec(tm, D_MODEL)],
        out_shape=[jax.ShapeDtypeStruct((L, SSM_W), _BF), jax.ShapeDtypeStruct((L, SSM_W), _BF),
                   jax.ShapeDtypeStruct((L, D_MODEL), _BF), jax.ShapeDtypeStruct((L, D_MODEL), _F32)],
        scratch_shapes=[pltpu.VMEM((tm, D_MODEL), _BF)],
        compiler_params=_params("parallel"),
    )(s, y_ret, x, w_glu, w_out, g2)


FF1_COLS = D_FF // N_DEV


def _ff1_fwd(x2, g3, w1, tm):
    L = x2.shape[0]

    def body(x_ref, g_ref, w_ref, h_ref, a_ref):
        for rows in _row_chunks(tm):
            xv = x_ref[rows, :]
            h = (xv * _rms_r(xv) * g_ref[...]).astype(_BF)
            h_ref[rows, :] = h
            for j in range(N_DEV):
                cols = slice(j * FF1_COLS, (j + 1) * FF1_COLS)
                rl = jnp.maximum(_dot(h, w_ref[j]), 0.0)
                a_ref[rows, cols] = (rl * rl).astype(_BF)

    return pl.pallas_call(
        body, name="ff1_fwd", grid=(L // tm,),
        in_specs=[_row_spec(tm, D_MODEL), _full_spec((1, D_MODEL)), _weight_spec((N_DEV, D_MODEL, FF1_COLS))],
        out_specs=[_row_spec(tm, D_MODEL), _row_spec(tm, D_FF)],
        out_shape=[jax.ShapeDtypeStruct((L, D_MODEL), _BF), jax.ShapeDtypeStruct((L, D_FF), _BF)],
        compiler_params=_params("parallel"),
    )(x2, g3, w1)


def _ff2_loss(act, x2, tgt, g4, w2, tm):
    L = act.shape[0]

    def body(f_ref, x_ref, t_ref, g_ref, w_ref, dy_ref, dm_ref, dg_ref, ls_ref):
        @pl.when(pl.program_id(0) == 0)
        def _():
            dg_ref[...] = jnp.zeros_like(dg_ref)
            ls_ref[...] = jnp.zeros_like(ls_ref)

        g = g_ref[...]
        for rows in _row_chunks(tm):
            m = _dot(f_ref[rows, :], w_ref[...])
            y = x_ref[rows, :] + m * _rms_r(m) * g
            err = y - t_ref[rows, :]
            ls_ref[...] += jnp.sum(err * err, axis=0, keepdims=True)
            dy = err * (1.0 / D_MODEL)
            dy_ref[rows, :] = dy
            dm, dgr = _rms_bwd(m, g, dy)
            dm_ref[rows, :] = dm.astype(_BF)
            dg_ref[...] += jnp.sum(dgr, axis=0, keepdims=True)

    return pl.pallas_call(
        body, name="ff2_loss", grid=(L // tm,),
        in_specs=[_row_spec(tm, D_FF), _row_spec(tm, D_MODEL), _row_spec(tm, D_MODEL),
                  _full_spec((1, D_MODEL)), _weight_spec((D_FF, D_MODEL))],
        out_specs=[_row_spec(tm, D_MODEL), _row_spec(tm, D_MODEL), _full_spec((1, D_MODEL)), _full_spec((1, D_MODEL))],
        out_shape=[jax.ShapeDtypeStruct((L, D_MODEL), _F32), jax.ShapeDtypeStruct((L, D_MODEL), _BF),
                   jax.ShapeDtypeStruct((1, D_MODEL), _F32), jax.ShapeDtypeStruct((1, D_MODEL), _F32)],
        compiler_params=_params("arbitrary"),
    )(act, x2, tgt, g4, w2)


def _mlp_fwd_loss(x2, tgt, g3, g4, w1, w2, tm):
    L = x2.shape[0]

    def body(x_ref, t_ref, g3_ref, g4_ref, w1_ref, w2_ref, h_ref, a_ref, dy_ref, dm_ref, dg_ref, ls_ref):
        @pl.when(pl.program_id(0) == 0)
        def _():
            dg_ref[...] = jnp.zeros_like(dg_ref)
            ls_ref[...] = jnp.zeros_like(ls_ref)

        g = g4_ref[...]
        for rows in _row_chunks(tm):
            xv = x_ref[rows, :]
            h = (xv * _rms_r(xv) * g3_ref[...]).astype(_BF)
            h_ref[rows, :] = h
            for j in range(N_DEV):
                cols = slice(j * FF1_COLS, (j + 1) * FF1_COLS)
                rl = jnp.maximum(_dot(h, w1_ref[j]), 0.0)
                a_ref[rows, cols] = (rl * rl).astype(_BF)
            m = _dot(a_ref[rows, :], w2_ref[...])
            y = x_ref[rows, :] + m * _rms_r(m) * g
            err = y - t_ref[rows, :]
            ls_ref[...] += jnp.sum(err * err, axis=0, keepdims=True)
            dy = err * (1.0 / D_MODEL)
            dy_ref[rows, :] = dy
            dm, dgr = _rms_bwd(m, g, dy)
            dm_ref[rows, :] = dm.astype(_BF)
            dg_ref[...] += jnp.sum(dgr, axis=0, keepdims=True)

    vec = _full_spec((1, D_MODEL))
    return pl.pallas_call(
        body, name="mlp_fwd_loss", grid=(L // tm,),
        in_specs=[_row_spec(tm, D_MODEL), _row_spec(tm, D_MODEL), vec, vec,
                  _weight_spec((N_DEV, D_MODEL, FF1_COLS)), _weight_spec((D_FF, D_MODEL))],
        out_specs=[_row_spec(tm, D_MODEL), _row_spec(tm, D_FF), _row_spec(tm, D_MODEL), _row_spec(tm, D_MODEL), vec, vec],
        out_shape=[jax.ShapeDtypeStruct((L, D_MODEL), _BF), jax.ShapeDtypeStruct((L, D_FF), _BF),
                   jax.ShapeDtypeStruct((L, D_MODEL), _F32), jax.ShapeDtypeStruct((L, D_MODEL), _BF),
                   jax.ShapeDtypeStruct((1, D_MODEL), _F32), jax.ShapeDtypeStruct((1, D_MODEL), _F32)],
        compiler_params=_params("arbitrary"),
    )(x2, tgt, g3, g4, w1, w2)


def _ff2_bwd(dm, act, w2, tm, tn):
    L = dm.shape[0]
    last = L // tm - 1

    def body(dm_ref, a_ref, w_ref, df_ref, dw_ref, acc):
        @pl.when(pl.program_id(1) == 0)
        def _():
            acc[...] = jnp.zeros_like(acc)

        dmv = dm_ref[...]
        av = a_ref[...]
        df_ref[...] = (_dot_nt(dmv, w_ref[...]) * jnp.sqrt(4.0 * av.astype(_F32))).astype(_BF)
        acc[...] += _dot_tn(av, dmv)

        @pl.when(pl.program_id(1) == last)
        def _():
            dw_ref[...] = acc[...].astype(_BF)

    return pl.pallas_call(
        body, name="ff2_bwd", grid=(D_FF // tn, L // tm),
        in_specs=[pl.BlockSpec((tm, D_MODEL), lambda j, i: (i, 0)), pl.BlockSpec((tm, tn), lambda j, i: (i, j)),
                  pl.BlockSpec((tn, D_MODEL), lambda j, i: (j, 0))],
        out_specs=[pl.BlockSpec((tm, tn), lambda j, i: (i, j)), pl.BlockSpec((tn, D_MODEL), lambda j, i: (j, 0))],
        out_shape=[jax.ShapeDtypeStruct((L, D_FF), _BF), jax.ShapeDtypeStruct((D_FF, D_MODEL), _BF)],
        scratch_shapes=[pltpu.VMEM((tn, D_MODEL), _F32)],
        compiler_params=_params("parallel", "arbitrary"),
    )(dm, act, w2)


def _ff1_bwd(df1, w1, x2, mix, dy, g3, g2, tm):
    L = df1.shape[0]

    def body(df_ref, w_ref, x2_ref, mix_ref, dy_ref, g3_ref, g2_ref, dx2_ref, dmix_ref, dg3_ref, dg2_ref):
        @pl.when(pl.program_id(0) == 0)
        def _():
            dg3_ref[...] = jnp.zeros_like(dg3_ref)
            dg2_ref[...] = jnp.zeros_like(dg2_ref)

        for rows in _row_chunks(tm):
            dh = _dot_nt(df_ref[rows, 0:FF1_COLS], w_ref[0])
            for j in range(1, N_DEV):
                dh = dh + _dot_nt(df_ref[rows, j * FF1_COLS:(j + 1) * FF1_COLS], w_ref[j])
            dz, dgr = _rms_bwd(x2_ref[rows, :], g3_ref[...], dh)
            dg3_ref[...] += jnp.sum(dgr, axis=0, keepdims=True)
            dx2 = dy_ref[rows, :] + dz
            dx2_ref[rows, :] = dx2
            dmx, dgr2 = _rms_bwd(mix_ref[rows, :].astype(_F32), g2_ref[...], dx2)
            dg2_ref[...] += jnp.sum(dgr2, axis=0, keepdims=True)
            dmix_ref[rows, :] = dmx.astype(_BF)

    vec = _full_spec((1, D_MODEL))
    return pl.pallas_call(
        body, name="ff1_bwd", grid=(L // tm,),
        in_specs=[_row_spec(tm, D_FF), _weight_spec((N_DEV, D_MODEL, FF1_COLS)), _row_spec(tm, D_MODEL),
                  _row_spec(tm, D_MODEL), _row_spec(tm, D_MODEL), vec, vec],
        out_specs=[_row_spec(tm, D_MODEL), _row_spec(tm, D_MODEL), vec, vec],
        out_shape=[jax.ShapeDtypeStruct((L, D_MODEL), _F32), jax.ShapeDtypeStruct((L, D_MODEL), _BF),
                   jax.ShapeDtypeStruct((1, D_MODEL), _F32), jax.ShapeDtypeStruct((1, D_MODEL), _F32)],
        compiler_params=_params("arbitrary"),
    )(df1, w1, x2, mix, dy, g3, g2)


def _matmul_tn(a, b, tm, tn, name, slots=0):
    L, K = a.shape
    N = b.shape[1]
    last = L // tm - 1

    def body(a_ref, b_ref, o_ref, acc):
        @pl.when(pl.program_id(1) == 0)
        def _():
            acc[...] = jnp.zeros_like(acc)

        acc[...] += _dot_tn(a_ref[...].astype(_BF), b_ref[...].astype(_BF))

        @pl.when(pl.program_id(1) == last)
        def _():
            if slots:
                for c in range(tn // slots):
                    o_ref[c] = acc[:, c * slots:(c + 1) * slots].astype(_BF)
            else:
                o_ref[...] = acc[...].astype(_BF)

    if slots:
        out_spec = pl.BlockSpec((tn // slots, K, slots), lambda j, i: (j, 0, 0))
        out_shape = jax.ShapeDtypeStruct((N // slots, K, slots), _BF)
    else:
        out_spec = pl.BlockSpec((K, tn), lambda j, i: (0, j))
        out_shape = jax.ShapeDtypeStruct((K, N), _BF)
    return pl.pallas_call(
        body, name=name, grid=(N // tn, L // tm),
        in_specs=[pl.BlockSpec((tm, K), lambda j, i: (i, 0)), pl.BlockSpec((tm, tn), lambda j, i: (i, j))],
        out_specs=out_spec, out_shape=out_shape,
        scratch_shapes=[pltpu.VMEM((K, tn), _F32)],
        compiler_params=_params("parallel", "arbitrary"),
    )(a, b)


def _dw_out(y_ret, gl, dmix, tk):
    L = dmix.shape[0]
    last = L // tk - 1

    def body(a0_ref, a1_ref, b_ref, o_ref, acc):
        @pl.when(pl.program_id(0) == 0)
        def _():
            acc[...] = jnp.zeros_like(acc)

        bv = b_ref[...]
        acc[:RET_W, :] += _dot_tn(a0_ref[...], bv)
        acc[RET_W:, :] += _dot_tn(a1_ref[...], bv)

        @pl.when(pl.program_id(0) == last)
        def _():
            o_ref[...] = acc[...].astype(_BF)

    return pl.pallas_call(
        body, name="dw_out", grid=(L // tk,),
        in_specs=[_row_spec(tk, RET_W), _row_spec(tk, SSM_W), _row_spec(tk, D_MODEL)],
        out_specs=_full_spec((D_MODEL, D_MODEL)), out_shape=jax.ShapeDtypeStruct((D_MODEL, D_MODEL), _BF),
        scratch_shapes=[pltpu.VMEM((D_MODEL, D_MODEL), _F32)],
        compiler_params=_params("arbitrary"),
    )(y_ret, gl, dmix)


def _dw_in_t(pieces, h, tk):
    L = h.shape[0]
    last = L // tk - 1

    def body(p0, p1, p2, p3, p4, h_ref, o_ref, acc):
        @pl.when(pl.program_id(0) == 0)
        def _():
            acc[...] = jnp.zeros_like(acc)

        hv = h_ref[...]
        for j, p in enumerate((p0, p1, p2, p3, p4)):
            acc[j * RET_W:(j + 1) * RET_W, :] += _dot_tn(p[...].astype(_BF), hv)

        @pl.when(pl.program_id(0) == last)
        def _():
            o_ref[...] = acc[...].astype(_BF)

    return pl.pallas_call(
        body, name="dw_in", grid=(L // tk,),
        in_specs=[_row_spec(tk, RET_W)] * 5 + [_row_spec(tk, D_MODEL)],
        out_specs=_full_spec((IN_COLS, D_MODEL)), out_shape=jax.ShapeDtypeStruct((IN_COLS, D_MODEL), _BF),
        scratch_shapes=[pltpu.VMEM((IN_COLS, D_MODEL), _F32)],
        compiler_params=_params("arbitrary"),
    )(*pieces, h)


def _mixout_bwd(dmix, w_out, w_glu, ys, s, o, gate, ggn, tm, after=()):
    L = dmix.shape[0]

    def body(dmix_ref, wo_ref, wg_ref, ys_ref, s_ref, o_ref, gate_ref, ggn_ref,
             dglu_ref, ds_ref, dgate_ref, do_ref, dggn_ref):
        @pl.when(pl.program_id(0) == 0)
        def _():
            dggn_ref[...] = jnp.zeros_like(dggn_ref)

        ggn = ggn_ref[...]
        for rows in _row_chunks(tm):
            dcat = _dot_nt(dmix_ref[rows, :], wo_ref[...])
            dy_ret, dy_ssm = dcat[:, :RET_W], dcat[:, RET_W:]
            glu = _dot(ys_ref[rows, :], wg_ref[...])
            ga, sg = glu[:, :SSM_W], _sigmoid(glu[:, SSM_W:])
            dga = (dy_ssm * sg).astype(_BF)
            dgb = (dy_ssm * ga * sg * (1.0 - sg)).astype(_BF)
            dglu_ref[rows, :SSM_W] = dga
            dglu_ref[rows, SSM_W:] = dgb
            dys = _dot_nt(dga, wg_ref[:, :SSM_W]) + _dot_nt(dgb, wg_ref[:, SSM_W:])
            ds_ref[rows, :] = dys * _gelu_grad(s_ref[rows, :])
            gt = gate_ref[rows, :]
            sgt = _sigmoid(gt)
            for hh in range(N_HEAD):
                cols = slice(hh * HEAD_D, (hh + 1) * HEAD_D)
                ov = o_ref[rows, cols]
                dlt = ov - jnp.mean(ov, axis=-1, keepdims=True)
                rstd = lax.rsqrt(jnp.mean(dlt * dlt, axis=-1, keepdims=True) + NORM_EPS)
                on = dlt * rstd
                dyr = dy_ret[:, cols] * (gt[:, cols] * sgt[:, cols])
                dgate_ref[rows, cols] = (dy_ret[:, cols] * (on * ggn[:, cols]) * (sgt[:, cols] * (1.0 + gt[:, cols] * (1.0 - sgt[:, cols])))).astype(_BF)
                dggn_ref[:, cols] += jnp.sum(dyr * on, axis=0, keepdims=True)
                don = dyr * ggn[:, cols]
                do = rstd * (don - jnp.mean(don, axis=-1, keepdims=True) - on * jnp.mean(don * on, axis=-1, keepdims=True))
                do_ref[rows, cols] = do.astype(_BF)

    body, in_specs, operands = _ordered(
        body, [_row_spec(tm, D_MODEL), _weight_spec((D_MODEL, D_MODEL)), _weight_spec((SSM_W, 2 * SSM_W)),
               _row_spec(tm, SSM_W), _row_spec(tm, SSM_W), _row_spec(tm, RET_W), _row_spec(tm, RET_W),
               _full_spec((1, RET_W))], (dmix, w_out, w_glu, ys, s, o, gate, ggn), after)
    return pl.pallas_call(
        body, name="mixout_bwd", grid=(L // tm,),
        in_specs=in_specs,
        out_specs=[_row_spec(tm, 2 * SSM_W), _row_spec(tm, SSM_W), _row_spec(tm, RET_W), _row_spec(tm, RET_W),
                   _full_spec((1, RET_W))],
        out_shape=[jax.ShapeDtypeStruct((L, 2 * SSM_W), _BF), jax.ShapeDtypeStruct((L, SSM_W), _F32),
                   jax.ShapeDtypeStruct((L, RET_W), _BF), jax.ShapeDtypeStruct((L, RET_W), _BF),
                   jax.ShapeDtypeStruct((1, RET_W), _F32)],
        compiler_params=_params("arbitrary"),
    )(*operands)


def _s5_bwd(u, ds, xs, ent, bmat, cmat, tab_r, pw_r, d_skip, tb, after=()):
    L = u.shape[0]
    nt = L // tb
    seg = tb // SUBLANES
    G = KB_PER_STEP
    rcol = pl.BlockSpec((tb, G * LANES), lambda kb, t: (nt - 1 - t, kb))
    sp = _s5_specs(seg, time=lambda t: nt - 1 - t)
    aspec = pl.BlockSpec((G, SUBLANES, 2 * KB_STATES), lambda kb, t: (kb, 0, 0))

    def body(u_ref, ds_ref, x_ref, ent_ref, b_ref, c_ref, tr_ref, pr_ref, d_ref,
             du_ref, db_ref, dc_ref, da_ref, dd_ref, up_scr, dp_scr, g_scr, lc_scr):
        @pl.when(pl.program_id(1) == 0)
        def _():
            lc_scr[...] = jnp.zeros_like(lc_scr)
            db_ref[...] = jnp.zeros_like(db_ref)
            dc_ref[...] = jnp.zeros_like(dc_ref)
            da_ref[...] = jnp.zeros_like(da_ref)
            dd_ref[...] = jnp.zeros_like(dd_ref)

        _rows_to_segments(up_scr, u_ref, seg)
        _rows_to_segments(dp_scr, ds_ref, seg)
        for g in range(G):
            g_scr[g] = _dot_nt(dp_scr[g].astype(_BF), c_ref[g]).reshape(seg, SUBLANES, 2 * KB_STATES)
        _scan_segments(g_scr, tr_ref, pr_ref, lc_scr, seg, reverse=True, fwd_ref=x_ref, fwd_entry_ref=ent_ref.at[:, 0],
                       da_ref=da_ref)
        for g in range(G):
            cols = slice(g * LANES, (g + 1) * LANES)
            uv, dsv = up_scr[g], dp_scr[g]
            ub, dsb = uv.astype(_BF), dsv.astype(_BF)
            lamb = g_scr[g].reshape(tb, 2 * KB_STATES).astype(_BF)
            db_ref[g] += _dot_tn(ub, lamb)
            dc_ref[g] += _dot_tn(dsb, x_ref[g].reshape(tb, 2 * KB_STATES).astype(_BF))
            dd_ref[:, cols] += jnp.sum(dsv * uv, axis=0, keepdims=True)
            up_scr[g] = _dot_nt(lamb, b_ref[g]) + d_ref[:, cols] * dsv
        _segments_to_rows(du_ref, up_scr, seg)

    body, in_specs, operands = _ordered(
        body, [rcol, rcol, sp["x"], sp["ent"], sp["b"], sp["c"], sp["tab"], sp["pw"], sp["d"]],
        (u, ds, xs, ent, bmat, cmat, tab_r, pw_r, d_skip), after)
    return pl.pallas_call(
        body, name="s5_bwd", grid=(N_KB // G, nt),
        in_specs=in_specs,
        out_specs=[rcol, sp["b"], sp["b"], aspec, sp["d"]],
        out_shape=[jax.ShapeDtypeStruct((L, SSM_W), _BF),
                   jax.ShapeDtypeStruct((N_KB, LANES, 2 * KB_STATES), _F32),
                   jax.ShapeDtypeStruct((N_KB, LANES, 2 * KB_STATES), _F32),
                   jax.ShapeDtypeStruct((N_KB, SUBLANES, 2 * KB_STATES), _F32),
                   jax.ShapeDtypeStruct((1, SSM_W), _F32)],
        scratch_shapes=[pltpu.VMEM((G, tb, LANES), _F32)] * 2
        + [pltpu.VMEM((G, seg, SUBLANES, 2 * KB_STATES), _F32), pltpu.VMEM((G, SUBLANES, 2 * KB_STATES), _F32)],
        compiler_params=_params("parallel", "arbitrary"),
    )(*operands)


def _retention_bwd(q, k, v, do, r_prev, consts, cosf, sinf, after=()):
    L = q.shape[0]
    nc = L // CHUNK
    cps = math.gcd(RET_STEP_CHUNKS, nc)
    nb = nc // cps
    blk = pl.BlockSpec((cps * CHUNK, RET_W), lambda n: (nb - 1 - n, 0))
    rope_blk = pl.BlockSpec((cps * CHUNK, HEAD_D), lambda n: (nb - 1 - n, 0))

    def body(q_ref, k_ref, v_ref, do_ref, rp_ref, dm_ref, xi_ref, zeta_ref, gc_ref, cos_ref, sin_ref,
             dq_ref, dk_ref, dv_ref, g_scr):
        @pl.when(pl.program_id(0) == 0)
        def _():
            g_scr[...] = jnp.zeros_like(g_scr)

        for hh in range(N_HEAD):
            cols = slice(hh * HEAD_D, (hh + 1) * HEAD_D)
            dm, zeta = dm_ref[hh], zeta_ref[hh]
            gst = g_scr[hh]
            for c in reversed(range(cps)):
                rows = slice(c * CHUNK, (c + 1) * CHUNK)
                qv, kv, vv, dov = q_ref[rows, cols], k_ref[rows, cols], v_ref[rows, cols], do_ref[rows, cols]
                rb = rp_ref[hh, c].astype(_BF)
                gb = gst.astype(_BF)
                sb = (_dot_nt(qv, kv) * dm).astype(_BF)
                dab = (_dot_nt(dov, vv) * dm).astype(_BF)
                dox = (dov.astype(_F32) * xi_ref[hh]).astype(_BF)
                vz = (vv.astype(_F32) * zeta).astype(_BF)
                dq = _dot(dab, kv) + _dot_nt(dox, rb)
                dk = _dot_tn(dab, qv) + _dot_nt(vz, gb)
                dv = _dot_tn(sb, dov) + _dot(kv, gb) * zeta
                gst = gc_ref[hh, 0:1, :] * gst + _dot_tn(qv, dox)
                cs, sn = cos_ref[rows, :], sin_ref[rows, :]
                dq_ref[rows, cols] = _rope_t(dq, cs, sn).astype(_BF)
                dk_ref[rows, cols] = (_rope_t(dk, cs, sn) * (HEAD_D ** -0.5)).astype(_BF)
                dv_ref[rows, cols] = dv.astype(_BF)
            g_scr[hh] = gst

    body, in_specs, operands = _ordered(
        body, [blk, blk, blk, blk, pl.BlockSpec((N_HEAD, cps, HEAD_D, HEAD_D), lambda n: (0, nb - 1 - n, 0, 0))]
        + _head_specs() + [rope_blk, rope_blk], (q, k, v, do, r_prev, *consts, cosf, sinf), after)
    return pl.pallas_call(
        body, name="retention_bwd", grid=(nb,),
        in_specs=in_specs,
        out_specs=[blk, blk, blk],
        out_shape=[jax.ShapeDtypeStruct((L, RET_W), _BF)] * 3,
        scratch_shapes=[pltpu.VMEM((N_HEAD, HEAD_D, HEAD_D), _F32)],
        compiler_params=_params("arbitrary"),
    )(*operands)


def _inproj_bwd(pieces, w_in_t, x, dx2, g1, tm, after=()):
    L = x.shape[0]

    def body(p0, p1, p2, p3, p4, w_ref, x_ref, dx2_ref, g_ref, dx_ref, dg_ref):
        @pl.when(pl.program_id(0) == 0)
        def _():
            dg_ref[...] = jnp.zeros_like(dg_ref)

        for rows in _row_chunks(tm):
            dh = None
            for j, p in enumerate((p0, p1, p2, p3, p4)):
                part = _dot(p[rows, :].astype(_BF), w_ref[j * RET_W:(j + 1) * RET_W, :])
                dh = part if dh is None else dh + part
            dz, dgr = _rms_bwd(x_ref[rows, :], g_ref[...], dh)
            dx_ref[rows, :] = dx2_ref[rows, :] + dz
            dg_ref[...] += jnp.sum(dgr, axis=0, keepdims=True)

    body, in_specs, operands = _ordered(
        body, [_row_spec(tm, RET_W)] * 5 + [_weight_spec((IN_COLS, D_MODEL)), _row_spec(tm, D_MODEL),
                                             _row_spec(tm, D_MODEL), _full_spec((1, D_MODEL))],
        (*pieces, w_in_t, x, dx2, g1), after)
    return pl.pallas_call(
        body, name="inproj_bwd", grid=(L // tm,),
        in_specs=in_specs,
        out_specs=[_row_spec(tm, D_MODEL), _full_spec((1, D_MODEL))],
        out_shape=[jax.ShapeDtypeStruct((L, D_MODEL), _F32), jax.ShapeDtypeStruct((1, D_MODEL), _F32)],
        compiler_params=_params("arbitrary"),
    )(*operands)


def _sum_adamw(parts, w, m, v, tr, name):
    _, R, Cc = parts.shape

    def body(p_ref, w_ref, m_ref, v_ref, g_ref, d_ref, nm_ref, nv_ref):
        gv = p_ref[0].astype(_F32)
        for s in range(1, N_DEV):
            gv = gv + p_ref[s].astype(_F32)
        g_ref[...] = gv
        nm = ADAM_B1 * m_ref[...] + (1.0 - ADAM_B1) * gv
        nv = ADAM_B2 * v_ref[...] + (1.0 - ADAM_B2) * (gv * gv)
        m_hat = nm / (1.0 - ADAM_B1 ** ADAM_STEP)
        v_hat = nv / (1.0 - ADAM_B2 ** ADAM_STEP)
        d_ref[...] = -ADAM_LR * (m_hat / (jnp.sqrt(v_hat) + ADAM_EPS) + ADAM_WD * w_ref[...])
        nm_ref[...] = nm
        nv_ref[...] = nv

    spec = _row_spec(tr, Cc)
    return pl.pallas_call(
        body, name=name, grid=(R // tr,),
        in_specs=[pl.BlockSpec((N_DEV, tr, Cc), lambda i: (0, i, 0))] + [spec] * 3, out_specs=[spec] * 4,
        out_shape=[jax.ShapeDtypeStruct((R, Cc), _F32)] * 4,
        compiler_params=_params("parallel"),
    )(parts, w, m, v)


def _my_place():
    return lax.axis_index("x"), lax.axis_index("y"), lax.axis_index("c")


HBM_SPEC = pl.BlockSpec(memory_space=pltpu.HBM)
SEM_SPEC = pl.BlockSpec(memory_space=pltpu.SEMAPHORE)
DATAFLOW = pltpu.SideEffectType.DATAFLOW_SIDE_EFFECTING


def _my_index():
    x, y, c = _my_place()
    return 4 * x + 2 * y + c


def _landing(own_block):
    return lax.empty((N_DEV,) + own_block.shape, own_block.dtype)


def _own_copy(src, land, sems, a, gather):
    me = _my_index()
    return pltpu.make_async_copy(src if gather else src.at[me], land.at[me], sems.at[sems.shape[0] // N_DEV * 7 + a])


def _split_copies(src_refs, land_refs, send_sems, recv_sems, gather, first=0):
    x, y, c = _my_place()
    me = 4 * x + 2 * y + c
    copies = []
    for a, (src, land) in enumerate(zip(src_refs, land_refs)):
        for kk in range(1, N_DEV):
            px, py, pc = x ^ (kk >> 2), y ^ ((kk >> 1) & 1), c ^ (kk & 1)
            peer = 4 * px + 2 * py + pc
            copies.append(pltpu.make_async_remote_copy(
                src_ref=src if gather else src.at[peer], dst_ref=land.at[me],
                send_sem=send_sems.at[(first + a) * 7 + kk - 1], recv_sem=recv_sems.at[(first + a) * 7 + kk - 1],
                device_id=(px, py, pc), device_id_type=MESH))
    return copies


def _split_start(srcs, lands, gather, name):
    n = len(srcs)

    def body(*refs):
        src_refs, land_refs = refs[:n], refs[n:2 * n]
        send_sems, recv_sems = refs[2 * n], refs[2 * n + 1]
        token = refs[-1]
        for cp in _split_copies(src_refs, land_refs, send_sems, recv_sems, gather):
            cp.start()
        for a in range(n):
            _own_copy(src_refs[a], land_refs[a], send_sems, a, gather).start()
        token[...] = jnp.zeros_like(token)

    outs = pl.pallas_call(
        body, name=name,
        out_shape=(pltpu.SemaphoreType.DMA((N_DEV * n,)), pltpu.SemaphoreType.DMA((7 * n,)),
                   *[pltpu.HBM(t.shape, t.dtype) for t in srcs], *[pltpu.HBM(t.shape, t.dtype) for t in lands],
                   jax.ShapeDtypeStruct((SUBLANES, LANES), _F32)),
        in_specs=[HBM_SPEC] * (2 * n),
        out_specs=(SEM_SPEC, SEM_SPEC, *[HBM_SPEC] * (2 * n), pl.BlockSpec(memory_space=pltpu.VMEM)),
        input_output_aliases={i: 2 + i for i in range(2 * n)},
        compiler_params=pltpu.CompilerParams(has_side_effects=DATAFLOW),
    )(*[pltpu.with_memory_space_constraint(t, pltpu.HBM) for t in list(srcs) + list(lands)])
    return outs[0], outs[1], outs[2:2 + n], outs[2 + n:2 + 2 * n], outs[-1]


def _split_wait(send_sems, recv_sems, srcs, lands, after, gather, name, first=0):
    n = len(srcs)

    def body(*refs):
        src_refs, land_refs = refs[:n], refs[n:2 * n]
        send_s, recv_s = refs[2 * n], refs[2 * n + 1]
        for cp in _split_copies(src_refs, land_refs, send_s, recv_s, gather, first):
            cp.wait_send()
            cp.wait_recv()
        for a in range(n):
            _own_copy(src_refs[a], land_refs[a], send_s, first + a, gather).wait()

    outs = pl.pallas_call(
        body, name=name,
        out_shape=tuple(pltpu.HBM(t.shape, t.dtype) for t in list(srcs) + list(lands)),
        in_specs=[HBM_SPEC] * (2 * n) + [SEM_SPEC, SEM_SPEC, pl.BlockSpec(memory_space=pl.ANY)],
        out_specs=tuple([HBM_SPEC] * (2 * n)),
        input_output_aliases={i: i for i in range(2 * n)},
        compiler_params=pltpu.CompilerParams(has_side_effects=DATAFLOW),
    )(*srcs, *lands, send_sems, recv_sems, after)
    return outs[n:]


def _discretize(lam_re, lam_im, log_dt, b_re, b_im):
    lr = jnp.minimum(lam_re, -1e-4)
    li = lam_im
    dt = jnp.exp(log_dt)[:, None]
    er = jnp.exp(lr * dt)
    ar, ai = er * jnp.cos(li * dt), er * jnp.sin(li * dt)
    den = lr * lr + li * li
    cr = ((ar - 1.0) * lr + ai * li) / den
    ci = (ai * lr - (ar - 1.0) * li) / den
    bbr = cr[:, :, None] * b_re - ci[:, :, None] * b_im
    bbi = cr[:, :, None] * b_im + ci[:, :, None] * b_re
    return ar, ai, bbr, bbi


def _cmul(ar, ai, br, bi):
    return ar * br - ai * bi, ar * bi + ai * br


def _cpowers(ar, ai, n):
    pr, pi = ar[None], ai[None]
    while pr.shape[0] < n:
        nr, ni = _cmul(pr, pi, pr[-1][None], pi[-1][None])
        pr, pi = jnp.concatenate([pr, nr]), jnp.concatenate([pi, ni])
    return pr[:n], pi[:n]


def _scan_tables(ar, ai, seg, reverse):
    if reverse:
        ai = -ai
    ar, ai = ar.reshape(N_KB, KB_STATES), ai.reshape(N_KB, KB_STATES)
    pr, pi = _cpowers(ar, ai, seg)
    a1 = (pr[-1], pi[-1])
    a2 = _cmul(*a1, *a1)
    a4 = _cmul(*a2, *a2)
    row = jnp.arange(SUBLANES)[None, :, None]
    wide = lambda t: jnp.broadcast_to(t[:, None, :], (N_KB, SUBLANES, KB_STATES))
    tabs = [wide(ar), wide(ai)]
    for dist, (qr, qi) in ((1, a1), (2, a2), (4, a4)):
        keep = (row < SUBLANES - dist) if reverse else (row >= dist)
        tabs += [jnp.where(keep, wide(qr), 0.0), jnp.where(keep, wide(qi), 0.0)]
    tabs += [wide(a1[0]), wide(a1[1])]
    if reverse:
        pr, pi = pr[::-1], pi[::-1]
    pw = jnp.transpose(jnp.concatenate([pr, pi], axis=-1), (1, 0, 2))[:, :, None, :]
    return jnp.stack(tabs, axis=1).astype(_F32), pw.astype(_F32)


def _block_diag_in(br, bi):
    eye = jnp.eye(GROUPS_PER_KB, dtype=_F32)
    one = lambda t: jnp.einsum("kgpc,gh->kgchp", t.reshape(N_KB, GROUPS_PER_KB, N_STATE, SSM_GC), eye).reshape(
        N_KB, LANES, KB_STATES)
    return jnp.concatenate([one(br), one(bi)], axis=-1)


def _block_diag_in_t(dmat):
    d6 = dmat.reshape(N_KB, GROUPS_PER_KB, SSM_GC, 2, GROUPS_PER_KB, N_STATE)
    eye = jnp.eye(GROUPS_PER_KB, dtype=_F32)
    both = jnp.einsum("kgcrhp,gh->rkgpc", d6, eye).reshape(2, N_GROUP, N_STATE, SSM_GC)
    return both[0], both[1]


def _block_diag_out(c_re, c_im):
    eye = jnp.eye(GROUPS_PER_KB, dtype=_F32)
    one = lambda t: jnp.einsum("kgcp,gh->khpgc", t.reshape(N_KB, GROUPS_PER_KB, SSM_GC, N_STATE), eye).reshape(
        N_KB, KB_STATES, LANES)
    return jnp.concatenate([one(c_re), -one(c_im)], axis=1)


def _block_diag_out_t(dmat_t):
    d6 = dmat_t.reshape(N_KB, GROUPS_PER_KB, SSM_GC, 2, GROUPS_PER_KB, N_STATE)
    eye = jnp.eye(GROUPS_PER_KB, dtype=_F32)
    both = jnp.einsum("kgcrhp,gh->rkgcp", d6, eye).reshape(2, N_GROUP, SSM_GC, N_STATE)
    return both[0], -both[1]


SMALL_NAMES = ("norm_mix_pre", "norm_mix_post", "ret_gn_gain", "ssm_lambda_re", "ssm_lambda_im", "ssm_log_dt",
               "ssm_b_re", "ssm_b_im", "ssm_c_re", "ssm_c_im", "ssm_d", "norm_mlp_pre", "norm_mlp_post")


def _local_grads(x, tgt, small, weights, emit, emit_small, tm, tk, tb, zero=0.0):
    L = x.shape[0]
    g1, g2, ggn = small["norm_mix_pre"], small["norm_mix_post"], small["ret_gn_gain"]
    g3, g4, d_skip = small["norm_mlp_pre"], small["norm_mlp_post"], small["ssm_d"]

    rope = _rope_tables(L)
    consts = _ret_consts()

    disc_in = (small["ssm_lambda_re"][0], small["ssm_lambda_im"][0], small["ssm_log_dt"][0] + zero,
               small["ssm_b_re"][0], small["ssm_b_im"][0])
    (ar, ai, bbr, bbi), disc_vjp = jax.vjp(_discretize, *disc_in)
    bmat = _block_diag_in(bbr, bbi).astype(_BF)
    cmat = _block_diag_out(small["ssm_c_re"][0], small["ssm_c_im"][0]).astype(_BF)
    seg = tb // SUBLANES
    tab_f, pw_f = _scan_tables(ar, ai, seg, False)
    tab_r, pw_r = _scan_tables(ar, ai, seg, True)

    h1 = _prenorm(x, g1, min(4 * tm, L), after=(pw_r,))
    (w_in_t,) = weights("in", h1)
    q, k, v, gate, u, cosf, sinf = _inproj_fwd(h1, w_in_t, rope, min(4 * tm, L))
    o, y_ret, r_prev = _retention_fwd(q, k, v, gate, ggn, consts)
    s, xs, ent = _s5_fwd(u, bmat, cmat, tab_f, pw_f, d_skip, tb)
    w_glu, w_out = weights("mix", s)
    ys, gl, mix, x2 = _mixout_fwd(s, y_ret, x, w_glu, w_out, g2, min(2 * tm, L))
    w_ff1, w_ff2 = weights("mlp", x2)
    h3, act, dy, dm, dg4, sq = _mlp_fwd_loss(x2, tgt, g3, g4, w_ff1, w_ff2, min(2 * tm, L))

    df1, dw_ff2 = _ff2_bwd(dm, act, w_ff2, min(1024, L), 1024)
    dx2, dmix, dg3, dg2 = _ff1_bwd(df1, w_ff1, x2, mix, dy, g3, g2, min(2 * tm, L))
    dw_ff1 = _matmul_tn(h3, df1, tk, 2 * FF1_COLS, "dw_ff1", slots=FF1_COLS)
    token = emit({"w_ff1": dw_ff1, "w_ff2": dw_ff2})
    dglu, ds, dgate, do, dggn = _mixout_bwd(dmix, w_out, w_glu, ys, s, o, gate, ggn, min(2 * tm, L), after=token)
    dw_out = _dw_out(y_ret, gl, dmix, tk)
    dw_glu = _matmul_tn(ys, dglu, tk, 1024, "dw_glu")
    token = emit({"w_glu": dw_glu, "w_out": dw_out})
    du, dbmat, dcmat, da8, dd = _s5_bwd(u, ds, xs, ent, bmat, cmat, tab_r, pw_r, d_skip, tb, after=token)

    da = jnp.sum(da8, axis=1)
    dar = da[:, :KB_STATES].reshape(N_GROUP, N_STATE)
    dai = da[:, KB_STATES:].reshape(N_GROUP, N_STATE)
    dbr, dbi = _block_diag_in_t(dbmat)
    dlre, dlim, dldt, dbre, dbim = disc_vjp((dar, dai, dbr, dbi))
    dcre, dcim = _block_diag_out_t(dcmat)
    token = emit_small({
        "norm_mix_post": dg2, "ret_gn_gain": dggn,
        "ssm_lambda_re": dlre[None], "ssm_lambda_im": dlim[None], "ssm_log_dt": dldt[None],
        "ssm_b_re": dbre[None], "ssm_b_im": dbim[None], "ssm_c_re": dcre[None], "ssm_c_im": dcim[None],
        "ssm_d": dd, "norm_mlp_pre": dg3, "norm_mlp_post": dg4,
    }, sq)

    dq, dk, dv = _retention_bwd(q, k, v, do, r_prev, consts, cosf, sinf, after=token)
    pieces = (dq, dk, dv, dgate, du)
    dw_in_t = _dw_in_t(pieces, h1, min(1024, L))
    token = emit({"w_in": dw_in_t})
    gx, dg1 = _inproj_bwd(pieces, w_in_t, x, dx2, g1, min(2 * tm, L), after=token)
    return gx, dg1


BIG_SHAPES = {"w_in": (D_MODEL, IN_COLS // N_DEV), "w_glu": (SSM_W, 2 * SSM_W // N_DEV), "w_out": (D_MODEL // N_DEV, D_MODEL),
              "w_ff1": (D_MODEL, FF1_COLS), "w_ff2": (D_FF // N_DEV, D_MODEL)}
BIG_NAMES = ("w_in", "w_glu", "w_out", "w_ff1", "w_ff2")


def _cols_from_slots(g):
    return jnp.transpose(g, (1, 0, 2)).reshape(g.shape[1], N_DEV * g.shape[2])


def _cols_to_slots(dw):
    r, cols = dw.shape
    return jnp.transpose(dw.reshape(r, N_DEV, cols // N_DEV), (1, 0, 2))


WEIGHT_GROUPS = {"in": ("w_in",), "mix": ("w_glu", "w_out"), "mlp": ("w_ff1", "w_ff2")}


def _weight_from_slots(name, g):
    if name == "w_glu":
        return _cols_from_slots(g)
    if name == "w_ff1":
        return g
    return g.reshape(N_DEV * g.shape[1], g.shape[2])


def _grad_slots(name, dw):
    if name == "w_glu":
        return _cols_to_slots(dw)
    if name == "w_ff1":
        return dw
    if name == "w_in":
        return dw.reshape(N_DEV, BIG_SHAPES[name][1], BIG_SHAPES[name][0])
    return dw.reshape((N_DEV,) + BIG_SHAPES[name])


PIECE_ROWS = 8


VEC_NAMES = tuple(n for n in SMALL_NAMES if n[:6] not in ("ssm_b_", "ssm_c_"))
BC_NAMES = ("ssm_b_re", "ssm_b_im", "ssm_c_re", "ssm_c_im")
BC_ROWS = N_GROUP * SSM_GC


def _bc_view(name, t):
    t = t[0]
    if name.startswith("ssm_b_"):
        t = jnp.swapaxes(t, 1, 2)
    return t.reshape(BC_ROWS, N_STATE)


def _bc_unview(name, t):
    t = t.reshape(N_GROUP, SSM_GC, N_STATE)
    if name.startswith("ssm_b_"):
        t = jnp.swapaxes(t, 1, 2)
    return t[None]


def _pack_bc(vals):
    return jnp.concatenate([_bc_view(n, vals[n]).astype(_F32) for n in BC_NAMES], axis=0)


def _unpack_bc(buf):
    return {n: _bc_unview(n, buf[j * BC_ROWS:(j + 1) * BC_ROWS]) for j, n in enumerate(BC_NAMES)}


def _small_layout(shapes):
    off, rows = {}, 0
    for n in VEC_NAMES:
        off[n] = rows
        rows += -(-math.prod(shapes[n]) // (PIECE_ROWS * LANES)) * PIECE_ROWS
    return off, rows, rows + PIECE_ROWS


def _pack_small(vals, shapes, last=None):
    parts = []
    for n in VEC_NAMES:
        flat = vals[n].reshape(-1).astype(_F32)
        pad = -flat.shape[0] % (PIECE_ROWS * LANES)
        if pad:
            flat = jnp.concatenate([flat, jnp.zeros((pad,), _F32)])
        parts.append(flat.reshape(-1, LANES))
    parts.append(jnp.zeros((PIECE_ROWS, LANES), _F32) if last is None else last)
    return jnp.concatenate(parts, axis=0)


def _unpack_small(buf, shapes):
    off, _, _ = _small_layout(shapes)
    out = {}
    for n in VEC_NAMES:
        size = math.prod(shapes[n])
        rows = -(-size // LANES)
        out[n] = buf[off[n]:off[n] + rows].reshape(-1)[:size].reshape(shapes[n])
    return out


WEIGHT_NAMES = ('norm_mix_pre', 'norm_mix_post', 'w_in', 'ret_gn_gain', 'ssm_lambda_re', 'ssm_lambda_im', 'ssm_log_dt',
                'ssm_b_re', 'ssm_b_im', 'ssm_c_re', 'ssm_c_im', 'ssm_d', 'w_glu', 'w_out', 'norm_mlp_pre',
                'norm_mlp_post', 'w_ff1', 'w_ff2')


def kernel(x, norm_mix_pre, norm_mix_post, w_in, ret_gn_gain, ssm_lambda_re, ssm_lambda_im, ssm_log_dt, ssm_b_re, ssm_b_im, ssm_c_re, ssm_c_im, ssm_d, w_glu, w_out, norm_mlp_pre, norm_mlp_post, w_ff1, w_ff2, loss_target, m_norm_mix_pre, m_norm_mix_post, m_w_in, m_ret_gn_gain, m_ssm_lambda_re, m_ssm_lambda_im, m_ssm_log_dt, m_ssm_b_re, m_ssm_b_im, m_ssm_c_re, m_ssm_c_im, m_ssm_d, m_w_glu, m_w_out, m_norm_mlp_pre, m_norm_mlp_post, m_w_ff1, m_w_ff2, v_norm_mix_pre, v_norm_mix_post, v_w_in, v_ret_gn_gain, v_ssm_lambda_re, v_ssm_lambda_im, v_ssm_log_dt, v_ssm_b_re, v_ssm_b_im, v_ssm_c_re, v_ssm_c_im, v_ssm_d, v_w_glu, v_w_out, v_norm_mlp_pre, v_norm_mlp_post, v_w_ff1, v_w_ff2):
    args = dict(locals())
    w = {n: args[n] for n in WEIGHT_NAMES}
    m = {n: args["m_" + n] for n in WEIGHT_NAMES}
    v = {n: args["v_" + n] for n in WEIGHT_NAMES}
    L = x.shape[1]
    tm = min(256, L)
    tk = min(2048, L)
    tb = min(1024, L)

    calls = {"in": ("w_in",), "rest": WEIGHT_GROUPS["mix"] + WEIGHT_GROUPS["mlp"]}
    started, zero = {}, jnp.zeros((), _F32)
    for call, names in calls.items():
        blocks = [(w[n][0].T if n == "w_in" else w[n][0]).astype(_BF) for n in names]
        blocks[0] = blocks[0] + zero.astype(_BF)
        started[call] = _split_start(blocks, [_landing(b) for b in blocks], True, "weights_start_" + call)
        zero = started[call][4][0, 0]

    def weights(group, after):
        names = WEIGHT_GROUPS[group]
        call = "in" if group == "in" else "rest"
        first = calls[call].index(names[0])
        part = slice(first, first + len(names))
        got = started[call]
        landed = _split_wait(got[0], got[1], got[2][part], got[3][part], after, True, "weights_wait_" + group, first=first)
        return [_weight_from_slots(n, g) for n, g in zip(names, landed)]

    in_flight = []

    def emit(dws):
        names = sorted(dws)
        srcs = [_grad_slots(n, dws[n]) for n in names]
        lands = [_landing(t[0]) for t in srcs]
        started = _split_start(srcs, lands, False, "grads_start_" + "_".join(names))
        in_flight.append((names, started))
        return (started[4],)

    shapes = {n: w[n].shape for n in SMALL_NAMES}
    first_piece = {SMALL_NAMES[0]: jnp.zeros(shapes[SMALL_NAMES[0]], _F32)}
    small_flight = []

    def emit_small(gs, sq):
        loss_rows = jnp.broadcast_to(0.5 / D_MODEL * jnp.sum(sq), (PIECE_ROWS, LANES)).astype(_F32)
        bufs = [_pack_small({**first_piece, **gs}, shapes, loss_rows), _pack_bc(gs)]
        small_flight.append(_split_start(bufs, [_landing(b) for b in bufs], True, "small_grads_start"))
        return (small_flight[0][4],)

    small_w = {n: w[n] for n in SMALL_NAMES}
    gx, dg1 = _local_grads(x[0], loss_target[0], small_w, weights, emit, emit_small, tm, tk, tb, zero=zero)
    last_buf = dg1.reshape(PIECE_ROWS, LANES)
    last_started = _split_start([last_buf], [_landing(last_buf)], True, "last_grad_start")

    grads, delta, new_m, new_v = {}, {}, {}, {}
    after = last_started[4]
    for names, started in in_flight:
        landed = _split_wait(*started[:4], after, False, "grads_wait_" + "_".join(names))
        for n, parts in zip(names, landed):
            flip = (lambda t: t.T) if n == "w_in" else (lambda t: t)
            res = _sum_adamw(parts, flip(w[n][0]), flip(m[n][0]), flip(v[n][0]), math.gcd(256, parts.shape[1]), "adamw_" + n)
            grads[n], delta[n], new_m[n], new_v[n] = (flip(t)[None] for t in res)
        after = res[1]
    small_parts, bc_parts = _split_wait(*small_flight[0][:4], after, True, "small_grads_wait")
    last_parts = _split_wait(*last_started[:4], small_parts, True, "last_grad_wait")[0]
    small_parts = lax.dynamic_update_slice(small_parts, last_parts, (0, 0, 0))
    res_bc = _sum_adamw(bc_parts, _pack_bc(w), _pack_bc(m), _pack_bc(v), BC_ROWS, "adamw_bc")
    sw, sm, sv = _pack_small(w, shapes), _pack_small(m, shapes), _pack_small(v, shapes)
    res = _sum_adamw(small_parts, sw, sm, sv, sw.shape[0], "adamw_small")
    for dst, buf, buf_bc in zip((grads, delta, new_m, new_v), res, res_bc):
        dst.update(_unpack_small(buf, shapes))
        dst.update(_unpack_bc(buf_bc))
    _, loss_at, _ = _small_layout(shapes)
    loss = res[0][loss_at, 0]

    return (loss, gx[None], *[grads[n] for n in WEIGHT_NAMES], *[delta[n] for n in WEIGHT_NAMES],
            *[new_m[n] for n in WEIGHT_NAMES], *[new_v[n] for n in WEIGHT_NAMES])
```

```python
import math

import jax
import jax.numpy as jnp
from jax import lax
from jax.experimental import pallas as pl
from jax.experimental.pallas import tpu as pltpu

_BF = jnp.bfloat16
_F32 = jnp.float32

D_MODEL = 1024
RET_W = 512
N_HEAD = 4
HEAD_D = 128
CHUNK = 256
ROPE_CHUNK = 128
SSM_W = 512
SSM_GC = 16
N_GROUP = 32
N_STATE = 64
GROUPS_PER_KB = 8
N_KB = 4
KB_STATES = GROUPS_PER_KB * N_STATE
D_FF = 4096
IN_COLS = 2560
NORM_EPS = 1e-6
ROPE_BASE = 10000.0
N_DEV = 8

ADAM_LR = 0.001
ADAM_B1 = 0.9
ADAM_B2 = 0.999
ADAM_EPS = 1e-08
ADAM_WD = 0.01
ADAM_STEP = 10

SUBLANES = 8
LANES = 128
VMEM_LIMIT = 52 * 1024 * 1024
RET_STEP_CHUNKS = 2
KB_PER_STEP = 2
SCAN_UNROLL = True
FIX_UNROLL = 8

MESH = pl.DeviceIdType.MESH


def _params(*sem):
    return pltpu.CompilerParams(dimension_semantics=sem, vmem_limit_bytes=VMEM_LIMIT)


def _dot(a, b):
    return jnp.dot(a, b, preferred_element_type=_F32)


def _dot_nt(a, b):
    return lax.dot_general(a, b, (((1,), (1,)), ((), ())), preferred_element_type=_F32)


def _dot_tn(a, b):
    return lax.dot_general(a, b, (((0,), (0,)), ((), ())), preferred_element_type=_F32)


def _rms_r(z):
    return lax.rsqrt(jnp.mean(z * z, axis=-1, keepdims=True) + NORM_EPS)


def _rms_bwd(z, g, dn):
    r = _rms_r(z)
    t = dn * g
    dz = r * t - z * (r * r * r * jnp.mean(t * z, axis=-1, keepdims=True))
    return dz, dn * z * r


def _rope(t, cs, sn):
    return t * cs + pltpu.roll(t, HEAD_D // 2, 1) * sn


def _rope_t(t, cs, sn):
    return t * cs - pltpu.roll(t, HEAD_D // 2, 1) * sn


def _sigmoid(z):
    return 1.0 / (1.0 + jnp.exp(-z))


_GELU_C = math.sqrt(2.0 / math.pi)


def _gelu(z):
    return 0.5 * z * (1.0 + jnp.tanh(_GELU_C * (z + 0.044715 * z * z * z)))


def _gelu_grad(z):
    th = jnp.tanh(_GELU_C * (z + 0.044715 * z * z * z))
    return 0.5 * (1.0 + th) + 0.5 * z * (1.0 - th * th) * _GELU_C * (1.0 + 3 * 0.044715 * z * z)


ROW_CHUNK = 256


def _row_chunks(tm):
    return [pl.ds(i, min(ROW_CHUNK, tm)) for i in range(0, tm, ROW_CHUNK)]


def _ordered(body, in_specs, operands, after):
    k = len(after)
    if not k:
        return body, list(in_specs), tuple(operands)
    return ((lambda *refs: body(*refs[k:])), [pl.BlockSpec(memory_space=pl.ANY)] * k + list(in_specs),
            tuple(after) + tuple(operands))


def _row_spec(tm, n):
    return pl.BlockSpec((tm, n), lambda i: (i, 0))


def _full_spec(shape):
    nd = len(shape)
    return pl.BlockSpec(shape, lambda *_: (0,) * nd)


def _weight_spec(shape):
    nd = len(shape)
    return pl.BlockSpec(shape, lambda *_: (0,) * nd, pipeline_mode=pl.Buffered(1))


def _rope_tables(L):
    half = HEAD_D // 2
    inv_freq = ROPE_BASE ** (-jnp.arange(half, dtype=_F32) / half)
    twice = lambda t: jnp.concatenate([t, t], axis=-1)
    off = jnp.arange(ROPE_CHUNK, dtype=_F32)[:, None] * inv_freq[None, :]
    start = (ROPE_CHUNK * jnp.arange(L // ROPE_CHUNK, dtype=_F32))[:, None] * inv_freq[None, :]
    return (twice(jnp.cos(off)), twice(jnp.sin(off)),
            twice(jnp.cos(start))[:, None, :], twice(jnp.sin(start))[:, None, :])


def _prenorm(x, g, tm, after=()):
    L = x.shape[0]

    def body(x_ref, g_ref, h_ref):
        xv = x_ref[...]
        h_ref[...] = (xv * _rms_r(xv) * g_ref[...]).astype(_BF)

    body, in_specs, operands = _ordered(body, [_row_spec(tm, D_MODEL), _full_spec((1, D_MODEL))], (x, g), after)
    return pl.pallas_call(
        body, name="prenorm", grid=(L // tm,),
        in_specs=in_specs, out_specs=_row_spec(tm, D_MODEL),
        out_shape=jax.ShapeDtypeStruct((L, D_MODEL), _BF),
        compiler_params=_params("parallel"),
    )(*operands)


def _inproj_fwd(h, w_in_t, rope, tm):
    L = h.shape[0]
    n_chunks = tm // ROPE_CHUNK

    def body(h_ref, w_ref, co_ref, so_ref, cs_ref, ss_ref, q_ref, k_ref, v_ref, gate_ref, u_ref, cos_ref, sin_ref):
        proj = _dot_nt(h_ref[...], w_ref[...])
        lane = lax.broadcasted_iota(jnp.int32, (ROPE_CHUNK, HEAD_D), 1)
        sign = jnp.where(lane < HEAD_D // 2, -1.0, 1.0)
        co, so = co_ref[...], so_ref[...]
        for c in range(n_chunks):
            chunk = pl.program_id(0) * n_chunks + c
            cst, sst = cs_ref[chunk], ss_ref[chunk]
            rows = slice(c * ROPE_CHUNK, (c + 1) * ROPE_CHUNK)
            cs = co * cst - so * sst
            sn = (so * cst + co * sst) * sign
            cos_ref[rows, :] = cs
            sin_ref[rows, :] = sn
            for hh in range(N_HEAD):
                lo = hh * HEAD_D
                q_ref[rows, lo:lo + HEAD_D] = _rope(proj[rows, lo:lo + HEAD_D], cs, sn).astype(_BF)
                kh = _rope(proj[rows, RET_W + lo:RET_W + lo + HEAD_D], cs, sn) * (HEAD_D ** -0.5)
                k_ref[rows, lo:lo + HEAD_D] = kh.astype(_BF)
        v_ref[...] = proj[:, 2 * RET_W:3 * RET_W].astype(_BF)
        gate_ref[...] = proj[:, 3 * RET_W:4 * RET_W]
        u_ref[...] = proj[:, 4 * RET_W:]

    nc = L // ROPE_CHUNK
    return pl.pallas_call(
        body, name="inproj_fwd", grid=(L // tm,),
        in_specs=[_row_spec(tm, D_MODEL), _weight_spec((IN_COLS, D_MODEL)),
                  _full_spec((ROPE_CHUNK, HEAD_D)), _full_spec((ROPE_CHUNK, HEAD_D)),
                  _full_spec((nc, 1, HEAD_D)), _full_spec((nc, 1, HEAD_D))],
        out_specs=[_row_spec(tm, RET_W)] * 5 + [_row_spec(tm, HEAD_D)] * 2,
        out_shape=[jax.ShapeDtypeStruct((L, RET_W), _BF)] * 3 + [jax.ShapeDtypeStruct((L, RET_W), _F32)] * 2
        + [jax.ShapeDtypeStruct((L, HEAD_D), _F32)] * 2,
        compiler_params=_params("parallel"),
    )(h, w_in_t, *rope)


def _ret_consts():
    lg = jnp.log(1.0 - jnp.exp(jnp.linspace(math.log(1.0 / 32), math.log(1.0 / 512), N_HEAD))).astype(_F32)
    idx = jnp.arange(CHUNK, dtype=_F32)
    diff = idx[:, None] - idx[None, :]
    decay = jnp.where(diff[None] >= 0, jnp.exp(jnp.maximum(diff, 0.0)[None] * lg[:, None, None]), 0.0)
    zeta = jnp.exp((CHUNK - 1 - idx)[None, :] * lg[:, None])
    xi = jnp.exp((idx + 1.0)[None, :] * lg[:, None])
    gc = jnp.exp(CHUNK * lg)
    wide = lambda t: jnp.broadcast_to(t[:, :, None], (N_HEAD, CHUNK, HEAD_D)).astype(_F32)
    gcw = jnp.broadcast_to(gc[:, None, None], (N_HEAD, SUBLANES, HEAD_D)).astype(_F32)
    return decay.astype(_F32), wide(xi), wide(zeta), gcw


def _head_specs():
    wide = _full_spec((N_HEAD, CHUNK, HEAD_D))
    return [_full_spec((N_HEAD, CHUNK, CHUNK)), wide, wide, _full_spec((N_HEAD, SUBLANES, HEAD_D))]


def _retention_fwd(q, k, v, gate, ggn, consts):
    L = q.shape[0]
    nc = L // CHUNK
    cps = math.gcd(RET_STEP_CHUNKS, nc)
    blk = pl.BlockSpec((cps * CHUNK, RET_W), lambda n: (n, 0))

    def body(q_ref, k_ref, v_ref, gate_ref, ggn_ref, dm_ref, xi_ref, zeta_ref, gc_ref,
             o_ref, y_ref, rp_ref, r_scr):
        @pl.when(pl.program_id(0) == 0)
        def _():
            r_scr[...] = jnp.zeros_like(r_scr)

        for hh in range(N_HEAD):
            cols = slice(hh * HEAD_D, (hh + 1) * HEAD_D)
            state = r_scr[hh]
            for c in range(cps):
                rows = slice(c * CHUNK, (c + 1) * CHUNK)
                qv, kv, vv = q_ref[rows, cols], k_ref[rows, cols], v_ref[rows, cols]
                s = _dot_nt(qv, kv) * dm_ref[hh]
                o = _dot(s.astype(_BF), vv) + _dot(qv, state.astype(_BF)) * xi_ref[hh]
                o_ref[rows, cols] = o
                rp_ref[hh, c] = state
                vz = (vv.astype(_F32) * zeta_ref[hh]).astype(_BF)
                state = gc_ref[hh, 0:1, :] * state + _dot_tn(kv, vz)
                dlt = o - jnp.mean(o, axis=-1, keepdims=True)
                on = dlt * lax.rsqrt(jnp.mean(dlt * dlt, axis=-1, keepdims=True) + NORM_EPS)
                gt = gate_ref[rows, cols]
                y_ref[rows, cols] = (gt * _sigmoid(gt) * (on * ggn_ref[:, cols])).astype(_BF)
            r_scr[hh] = state

    return pl.pallas_call(
        body, name="retention_fwd", grid=(nc // cps,),
        in_specs=[blk, blk, blk, blk, _full_spec((1, RET_W))] + _head_specs(),
        out_specs=[blk, blk, pl.BlockSpec((N_HEAD, cps, HEAD_D, HEAD_D), lambda n: (0, n, 0, 0))],
        out_shape=[jax.ShapeDtypeStruct((L, RET_W), _F32), jax.ShapeDtypeStruct((L, RET_W), _BF),
                   jax.ShapeDtypeStruct((N_HEAD, nc, HEAD_D, HEAD_D), _F32)],
        scratch_shapes=[pltpu.VMEM((N_HEAD, HEAD_D, HEAD_D), _F32)],
        compiler_params=_params("arbitrary"),
    )(q, k, v, gate, ggn, *consts)


def _rows_to_segments(dst_scr, src_ref, seg):
    for g in range(dst_scr.shape[0]):
        for j in range(SUBLANES):
            dst_scr[g, pl.ds(j, seg, stride=SUBLANES), :] = src_ref[pl.ds(j * seg, seg), g * LANES:(g + 1) * LANES]


def _segments_to_rows(dst_ref, src_scr, seg):
    for g in range(src_scr.shape[0]):
        for j in range(SUBLANES):
            dst_ref[pl.ds(j * seg, seg), g * LANES:(g + 1) * LANES] = src_scr[g, pl.ds(j, seg, stride=SUBLANES), :].astype(dst_ref.dtype)


def _scan_segments(x_ref, tab_ref, pw_ref, carry_ref, seg, reverse, entry_ref=None, fwd_ref=None, fwd_entry_ref=None,
                   da_ref=None):
    G = x_ref.shape[0]
    W = KB_STATES
    re, im = pl.ds(0, W), pl.ds(W, W)
    row_id = lax.broadcasted_iota(jnp.int32, (SUBLANES, W), 0)
    edge_in = (row_id == SUBLANES - 1) if reverse else (row_id == 0)
    edge_out = 0 if reverse else SUBLANES - 1
    a_tab = [(tab_ref[g, 0], tab_ref[g, 1]) for g in range(G)]

    def local(i, st):
        r = (seg - 1 - i) if reverse else i
        out = []
        for g in range(G):
            (ar, ai), (sr, si) = a_tab[g], st[g]
            nr = ar * sr - ai * si + x_ref[g, r, :, re]
            ni = ar * si + ai * sr + x_ref[g, r, :, im]
            x_ref[g, r, :, re] = nr
            x_ref[g, r, :, im] = ni
            out.append((nr, ni))
        return tuple(out)

    zero = jnp.zeros((SUBLANES, W), _F32)
    ends = lax.fori_loop(0, seg, local, tuple((zero, zero) for _ in range(G)), unroll=SCAN_UNROLL)

    entry = []
    shift = (SUBLANES - 1) if reverse else 1
    for g in range(G):
        er, ei = ends[g]
        fr = jnp.where(edge_in, carry_ref[g, :, re], pltpu.roll(er, shift, 0))
        fi = jnp.where(edge_in, carry_ref[g, :, im], pltpu.roll(ei, shift, 0))
        for j, dist in enumerate((1, 2, 4)):
            pr, pi = tab_ref[g, 2 + 2 * j], tab_ref[g, 3 + 2 * j]
            sh = (SUBLANES - dist) if reverse else dist
            sr, si = pltpu.roll(fr, sh, 0), pltpu.roll(fi, sh, 0)
            fr, fi = fr + pr * sr - pi * si, fi + pr * si + pi * sr
        br, bi = tab_ref[g, 8], tab_ref[g, 9]
        outr = br * fr - bi * fi + er
        outi = br * fi + bi * fr + ei
        carry_ref[g, :, re] = jnp.broadcast_to(outr[edge_out:edge_out + 1, :], (SUBLANES, W))
        carry_ref[g, :, im] = jnp.broadcast_to(outi[edge_out:edge_out + 1, :], (SUBLANES, W))
        entry.append((fr, fi))
        if entry_ref is not None:
            entry_ref[g, :, re] = fr
            entry_ref[g, :, im] = fi

    add_da = da_ref is not None

    def fix(r, st, first=False):
        out = []
        for g in range(G):
            fr, fi = entry[g]
            pwr, pwi = pw_ref[g, r, :, re], pw_ref[g, r, :, im]
            xr = x_ref[g, r, :, re] + (pwr * fr - pwi * fi)
            xi = x_ref[g, r, :, im] + (pwr * fi + pwi * fr)
            x_ref[g, r, :, re] = xr
            x_ref[g, r, :, im] = xi
            if add_da:
                prev = fwd_entry_ref.at[g] if first else fwd_ref.at[g, r - 1]
                xpr, xpi = prev[:, re], prev[:, im]
                out.append((st[g][0] + (xr * xpr + xi * xpi), st[g][1] + (xi * xpr - xr * xpi)))
            else:
                out.append(st[g])
        return tuple(out)

    if add_da:
        st = fix(0, tuple((zero, zero) for _ in range(G)), first=True)
        st = lax.fori_loop(1, seg, fix, st, unroll=SCAN_UNROLL)
        for g in range(G):
            da_ref[g, :, re] += st[g][0]
            da_ref[g, :, im] += st[g][1]
    else:
        lax.fori_loop(0, seg, fix, tuple((zero[0:1, 0:LANES],) for _ in range(G)), unroll=FIX_UNROLL)


def _s5_specs(seg, time=lambda t: t):
    G = KB_PER_STEP
    return dict(
        x=pl.BlockSpec((G, seg, SUBLANES, 2 * KB_STATES), lambda kb, t: (kb, time(t), 0, 0)),
        ent=pl.BlockSpec((G, 1, SUBLANES, 2 * KB_STATES), lambda kb, t: (kb, time(t), 0, 0)),
        b=pl.BlockSpec((G, LANES, 2 * KB_STATES), lambda kb, t: (kb, 0, 0)),
        c=pl.BlockSpec((G, 2 * KB_STATES, LANES), lambda kb, t: (kb, 0, 0)),
        tab=pl.BlockSpec((G, 10, SUBLANES, KB_STATES), lambda kb, t: (kb, 0, 0, 0)),
        pw=pl.BlockSpec((G, seg, 1, 2 * KB_STATES), lambda kb, t: (kb, 0, 0, 0)),
        d=pl.BlockSpec((1, G * LANES), lambda kb, t: (0, kb)),
    )


def _s5_fwd(u, bmat, cmat, tab_f, pw_f, d_skip, tb):
    L = u.shape[0]
    nt = L // tb
    seg = tb // SUBLANES
    G = KB_PER_STEP
    ucol = pl.BlockSpec((tb, G * LANES), lambda kb, t: (t, kb))
    sp = _s5_specs(seg)

    def body(u_ref, b_ref, c_ref, tab_ref, pw_ref, d_ref, s_ref, x_ref, ent_ref, up_scr, y_scr, carry_scr):
        @pl.when(pl.program_id(1) == 0)
        def _():
            carry_scr[...] = jnp.zeros_like(carry_scr)

        _rows_to_segments(up_scr, u_ref, seg)
        for g in range(G):
            x_ref[g] = _dot(up_scr[g].astype(_BF), b_ref[g]).reshape(seg, SUBLANES, 2 * KB_STATES)
        _scan_segments(x_ref, tab_ref, pw_ref, carry_scr, seg, reverse=False, entry_ref=ent_ref.at[:, 0])
        for g in range(G):
            y = _dot(x_ref[g].reshape(tb, 2 * KB_STATES).astype(_BF), c_ref[g])
            y_scr[g] = y + d_ref[:, g * LANES:(g + 1) * LANES] * up_scr[g]
        _segments_to_rows(s_ref, y_scr, seg)

    return pl.pallas_call(
        body, name="s5_fwd", grid=(N_KB // G, nt),
        in_specs=[ucol, sp["b"], sp["c"], sp["tab"], sp["pw"], sp["d"]],
        out_specs=[ucol, sp["x"], sp["ent"]],
        out_shape=[jax.ShapeDtypeStruct((L, SSM_W), _F32),
                   jax.ShapeDtypeStruct((N_KB, L // SUBLANES, SUBLANES, 2 * KB_STATES), _F32),
                   jax.ShapeDtypeStruct((N_KB, nt, SUBLANES, 2 * KB_STATES), _F32)],
        scratch_shapes=[pltpu.VMEM((G, tb, LANES), _F32)] * 2 + [pltpu.VMEM((G, SUBLANES, 2 * KB_STATES), _F32)],
        compiler_params=_params("parallel", "arbitrary"),
    )(u, bmat, cmat, tab_f, pw_f, d_skip)


def _mixout_fwd(s, y_ret, x, w_glu, w_out, g2, tm):
    L = s.shape[0]

    def body(s_ref, yr_ref, x_ref, wg_ref, wo_ref, g_ref, ys_ref, gl_ref, mix_ref, x2_ref, cat_scr):
        for rows in _row_chunks(tm):
            ys = _gelu(s_ref[rows, :]).astype(_BF)
            ys_ref[rows, :] = ys
            glu = _dot(ys, wg_ref[...])
            gl = (glu[:, :SSM_W] * _sigmoid(glu[:, SSM_W:])).astype(_BF)
            gl_ref[rows, :] = gl
            cat_scr[rows, :RET_W] = yr_ref[rows, :]
            cat_scr[rows, RET_W:] = gl
            mix = _dot(cat_scr[rows, :], wo_ref[...])
            mix_ref[rows, :] = mix.astype(_BF)
            x2_ref[rows, :] = x_ref[rows, :] + mix * _rms_r(mix) * g_ref[...]

    return pl.pallas_call(
        body, name="mixout_fwd", grid=(L // tm,),
        in_specs=[_row_spec(tm, SSM_W), _row_spec(tm, RET_W), _row_spec(tm, D_MODEL),
                  _weight_spec((SSM_W, 2 * SSM_W)), _weight_spec((D_MODEL, D_MODEL)), _full_spec((1, D_MODEL))],
        out_specs=[_row_spec(tm, SSM_W), _row_spec(tm, SSM_W), _row_spec(tm, D_MODEL), _row_spec(tm, D_MODEL)],
        out_shape=[jax.ShapeDtypeStruct((L, SSM_W), _BF), jax.ShapeDtypeStruct((L, SSM_W), _BF),
                   jax.ShapeDtypeStruct((L, D_MODEL), _BF), jax.ShapeDtypeStruct((L, D_MODEL), _F32)],
        scratch_shapes=[pltpu.VMEM((tm, D_MODEL), _BF)],
        compiler_params=_params("parallel"),
    )(s, y_ret, x, w_glu, w_out, g2)


FF1_COLS = D_FF // N_DEV


def _ff1_fwd(x2, g3, w1, tm):
    L = x2.shape[0]

    def body(x_ref, g_ref, w_ref, h_ref, a_ref):
        for rows in _row_chunks(tm):
            xv = x_ref[rows, :]
            h = (xv * _rms_r(xv) * g_ref[...]).astype(_BF)
            h_ref[rows, :] = h
            for j in range(N_DEV):
                cols = slice(j * FF1_COLS, (j + 1) * FF1_COLS)
                rl = jnp.maximum(_dot(h, w_ref[j]), 0.0)
                a_ref[rows, cols] = (rl * rl).astype(_BF)

    return pl.pallas_call(
        body, name="ff1_fwd", grid=(L // tm,),
        in_specs=[_row_spec(tm, D_MODEL), _full_spec((1, D_MODEL)), _weight_spec((N_DEV, D_MODEL, FF1_COLS))],
        out_specs=[_row_spec(tm, D_MODEL), _row_spec(tm, D_FF)],
        out_shape=[jax.ShapeDtypeStruct((L, D_MODEL), _BF), jax.ShapeDtypeStruct((L, D_FF), _BF)],
        compiler_params=_params("parallel"),
    )(x2, g3, w1)


def _ff2_loss(act, x2, tgt, g4, w2, tm):
    L = act.shape[0]

    def body(f_ref, x_ref, t_ref, g_ref, w_ref, dy_ref, dm_ref, dg_ref, ls_ref):
        @pl.when(pl.program_id(0) == 0)
        def _():
            dg_ref[...] = jnp.zeros_like(dg_ref)
            ls_ref[...] = jnp.zeros_like(ls_ref)

        g = g_ref[...]
        for rows in _row_chunks(tm):
            m = _dot(f_ref[rows, :], w_ref[...])
            y = x_ref[rows, :] + m * _rms_r(m) * g
            err = y - t_ref[rows, :]
            ls_ref[...] += jnp.sum(err * err, axis=0, keepdims=True)
            dy = err * (1.0 / D_MODEL)
            dy_ref[rows, :] = dy
            dm, dgr = _rms_bwd(m, g, dy)
            dm_ref[rows, :] = dm.astype(_BF)
            dg_ref[...] += jnp.sum(dgr, axis=0, keepdims=True)

    return pl.pallas_call(
        body, name="ff2_loss", grid=(L // tm,),
        in_specs=[_row_spec(tm, D_FF), _row_spec(tm, D_MODEL), _row_spec(tm, D_MODEL),
                  _full_spec((1, D_MODEL)), _weight_spec((D_FF, D_MODEL))],
        out_specs=[_row_spec(tm, D_MODEL), _row_spec(tm, D_MODEL), _full_spec((1, D_MODEL)), _full_spec((1, D_MODEL))],
        out_shape=[jax.ShapeDtypeStruct((L, D_MODEL), _F32), jax.ShapeDtypeStruct((L, D_MODEL), _BF),
                   jax.ShapeDtypeStruct((1, D_MODEL), _F32), jax.ShapeDtypeStruct((1, D_MODEL), _F32)],
        compiler_params=_params("arbitrary"),
    )(act, x2, tgt, g4, w2)


def _mlp_fwd_loss(x2, tgt, g3, g4, w1, w2, tm):
    L = x2.shape[0]

    def body(x_ref, t_ref, g3_ref, g4_ref, w1_ref, w2_ref, h_ref, a_ref, dy_ref, dm_ref, dg_ref, ls_ref):
        @pl.when(pl.program_id(0) == 0)
        def _():
            dg_ref[...] = jnp.zeros_like(dg_ref)
            ls_ref[...] = jnp.zeros_like(ls_ref)

        g = g4_ref[...]
        for rows in _row_chunks(tm):
            xv = x_ref[rows, :]
            h = (xv * _rms_r(xv) * g3_ref[...]).astype(_BF)
            h_ref[rows, :] = h
            for j in range(N_DEV):
                cols = slice(j * FF1_COLS, (j + 1) * FF1_COLS)
                rl = jnp.maximum(_dot(h, w1_ref[j]), 0.0)
                a_ref[rows, cols] = (rl * rl).astype(_BF)
            m = _dot(a_ref[rows, :], w2_ref[...])
            y = x_ref[rows, :] + m * _rms_r(m) * g
            err = y - t_ref[rows, :]
            ls_ref[...] += jnp.sum(err * err, axis=0, keepdims=True)
            dy = err * (1.0 / D_MODEL)
            dy_ref[rows, :] = dy
            dm, dgr = _rms_bwd(m, g, dy)
            dm_ref[rows, :] = dm.astype(_BF)
            dg_ref[...] += jnp.sum(dgr, axis=0, keepdims=True)

    vec = _full_spec((1, D_MODEL))
    return pl.pallas_call(
        body, name="mlp_fwd_loss", grid=(L // tm,),
        in_specs=[_row_spec(tm, D_MODEL), _row_spec(tm, D_MODEL), vec, vec,
                  _weight_spec((N_DEV, D_MODEL, FF1_COLS)), _weight_spec((D_FF, D_MODEL))],
        out_specs=[_row_spec(tm, D_MODEL), _row_spec(tm, D_FF), _row_spec(tm, D_MODEL), _row_spec(tm, D_MODEL), vec, vec],
        out_shape=[jax.ShapeDtypeStruct((L, D_MODEL), _BF), jax.ShapeDtypeStruct((L, D_FF), _BF),
                   jax.ShapeDtypeStruct((L, D_MODEL), _F32), jax.ShapeDtypeStruct((L, D_MODEL), _BF),
                   jax.ShapeDtypeStruct((1, D_MODEL), _F32), jax.ShapeDtypeStruct((1, D_MODEL), _F32)],
        compiler_params=_params("arbitrary"),
    )(x2, tgt, g3, g4, w1, w2)


def _mix_mlp_fwd_loss(s, y_ret, x, tgt, w_glu, w_out, g2, g3, g4, w1, w2, tm):
    L = s.shape[0]

    def body(s_ref, yr_ref, x_ref, t_ref, wg_ref, wo_ref, g2_ref, g3_ref, g4_ref, w1_ref, w2_ref,
             ys_ref, gl_ref, mix_ref, x2_ref, h_ref, a_ref, dy_ref, dm_ref, dg_ref, ls_ref, cat_scr):
        @pl.when(pl.program_id(0) == 0)
        def _():
            dg_ref[...] = jnp.zeros_like(dg_ref)
            ls_ref[...] = jnp.zeros_like(ls_ref)

        g = g4_ref[...]
        for rows in _row_chunks(tm):
            ys = _gelu(s_ref[rows, :]).astype(_BF)
            ys_ref[rows, :] = ys
            glu = _dot(ys, wg_ref[...])
            gl = (glu[:, :SSM_W] * _sigmoid(glu[:, SSM_W:])).astype(_BF)
            gl_ref[rows, :] = gl
            cat_scr[rows, :RET_W] = yr_ref[rows, :]
            cat_scr[rows, RET_W:] = gl
            mix = _dot(cat_scr[rows, :], wo_ref[...])
            mix_ref[rows, :] = mix.astype(_BF)
            x2_ref[rows, :] = x_ref[rows, :] + mix * _rms_r(mix) * g2_ref[...]

            xv = x2_ref[rows, :]
            h = (xv * _rms_r(xv) * g3_ref[...]).astype(_BF)
            h_ref[rows, :] = h
            for j in range(N_DEV):
                cols = slice(j * FF1_COLS, (j + 1) * FF1_COLS)
                rl = jnp.maximum(_dot(h, w1_ref[j]), 0.0)
                a_ref[rows, cols] = (rl * rl).astype(_BF)
            m = _dot(a_ref[rows, :], w2_ref[...])
            y = x2_ref[rows, :] + m * _rms_r(m) * g
            err = y - t_ref[rows, :]
            ls_ref[...] += jnp.sum(err * err, axis=0, keepdims=True)
            dy = err * (1.0 / D_MODEL)
            dy_ref[rows, :] = dy
            dm, dgr = _rms_bwd(m, g, dy)
            dm_ref[rows, :] = dm.astype(_BF)
            dg_ref[...] += jnp.sum(dgr, axis=0, keepdims=True)

    vec = _full_spec((1, D_MODEL))
    wide, half = _row_spec(tm, D_MODEL), _row_spec(tm, SSM_W)
    return pl.pallas_call(
        body, name="mix_mlp_fwd_loss", grid=(L // tm,),
        in_specs=[half, _row_spec(tm, RET_W), wide, wide,
                  _weight_spec((SSM_W, 2 * SSM_W)), _weight_spec((D_MODEL, D_MODEL)), vec, vec, vec,
                  _weight_spec((N_DEV, D_MODEL, FF1_COLS)), _weight_spec((D_FF, D_MODEL))],
        out_specs=[half, half, wide, wide, wide, _row_spec(tm, D_FF), wide, wide, vec, vec],
        out_shape=[jax.ShapeDtypeStruct((L, SSM_W), _BF), jax.ShapeDtypeStruct((L, SSM_W), _BF),
                   jax.ShapeDtypeStruct((L, D_MODEL), _BF), jax.ShapeDtypeStruct((L, D_MODEL), _F32),
                   jax.ShapeDtypeStruct((L, D_MODEL), _BF), jax.ShapeDtypeStruct((L, D_FF), _BF),
                   jax.ShapeDtypeStruct((L, D_MODEL), _F32), jax.ShapeDtypeStruct((L, D_MODEL), _BF),
                   jax.ShapeDtypeStruct((1, D_MODEL), _F32), jax.ShapeDtypeStruct((1, D_MODEL), _F32)],
        scratch_shapes=[pltpu.VMEM((tm, D_MODEL), _BF)],
        compiler_params=_params("arbitrary"),
    )(s, y_ret, x, tgt, w_glu, w_out, g2, g3, g4, w1, w2)


def _ff2_bwd(dm, act, w2, tm, tn):
    L = dm.shape[0]
    last = L // tm - 1

    def body(dm_ref, a_ref, w_ref, df_ref, dw_ref, acc):
        @pl.when(pl.program_id(1) == 0)
        def _():
            acc[...] = jnp.zeros_like(acc)

        dmv = dm_ref[...]
        av = a_ref[...]
        df_ref[...] = (_dot_nt(dmv, w_ref[...]) * jnp.sqrt(4.0 * av.astype(_F32))).astype(_BF)
        acc[...] += _dot_tn(av, dmv)

        @pl.when(pl.program_id(1) == last)
        def _():
            dw_ref[...] = acc[...].astype(_BF)

    return pl.pallas_call(
        body, name="ff2_bwd", grid=(D_FF // tn, L // tm),
        in_specs=[pl.BlockSpec((tm, D_MODEL), lambda j, i: (i, 0)), pl.BlockSpec((tm, tn), lambda j, i: (i, j)),
                  pl.BlockSpec((tn, D_MODEL), lambda j, i: (j, 0))],
        out_specs=[pl.BlockSpec((tm, tn), lambda j, i: (i, j)), pl.BlockSpec((tn, D_MODEL), lambda j, i: (j, 0))],
        out_shape=[jax.ShapeDtypeStruct((L, D_FF), _BF), jax.ShapeDtypeStruct((D_FF, D_MODEL), _BF)],
        scratch_shapes=[pltpu.VMEM((tn, D_MODEL), _F32)],
        compiler_params=_params("parallel", "arbitrary"),
    )(dm, act, w2)


def _ff1_bwd(df1, w1, x2, mix, dy, g3, g2, tm):
    L = df1.shape[0]

    def body(df_ref, w_ref, x2_ref, mix_ref, dy_ref, g3_ref, g2_ref, dx2_ref, dmix_ref, dg3_ref, dg2_ref):
        @pl.when(pl.program_id(0) == 0)
        def _():
            dg3_ref[...] = jnp.zeros_like(dg3_ref)
            dg2_ref[...] = jnp.zeros_like(dg2_ref)

        for rows in _row_chunks(tm):
            dh = _dot_nt(df_ref[rows, 0:FF1_COLS], w_ref[0])
            for j in range(1, N_DEV):
                dh = dh + _dot_nt(df_ref[rows, j * FF1_COLS:(j + 1) * FF1_COLS], w_ref[j])
            dz, dgr = _rms_bwd(x2_ref[rows, :], g3_ref[...], dh)
            dg3_ref[...] += jnp.sum(dgr, axis=0, keepdims=True)
            dx2 = dy_ref[rows, :] + dz
            dx2_ref[rows, :] = dx2
            dmx, dgr2 = _rms_bwd(mix_ref[rows, :].astype(_F32), g2_ref[...], dx2)
            dg2_ref[...] += jnp.sum(dgr2, axis=0, keepdims=True)
            dmix_ref[rows, :] = dmx.astype(_BF)

    vec = _full_spec((1, D_MODEL))
    return pl.pallas_call(
        body, name="ff1_bwd", grid=(L // tm,),
        in_specs=[_row_spec(tm, D_FF), _weight_spec((N_DEV, D_MODEL, FF1_COLS)), _row_spec(tm, D_MODEL),
                  _row_spec(tm, D_MODEL), _row_spec(tm, D_MODEL), vec, vec],
        out_specs=[_row_spec(tm, D_MODEL), _row_spec(tm, D_MODEL), vec, vec],
        out_shape=[jax.ShapeDtypeStruct((L, D_MODEL), _F32), jax.ShapeDtypeStruct((L, D_MODEL), _BF),
                   jax.ShapeDtypeStruct((1, D_MODEL), _F32), jax.ShapeDtypeStruct((1, D_MODEL), _F32)],
        compiler_params=_params("arbitrary"),
    )(df1, w1, x2, mix, dy, g3, g2)


def _matmul_tn(a, b, tm, tn, name, slots=0):
    L, K = a.shape
    N = b.shape[1]
    last = L // tm - 1

    def body(a_ref, b_ref, o_ref, acc):
        @pl.when(pl.program_id(1) == 0)
        def _():
            acc[...] = jnp.zeros_like(acc)

        acc[...] += _dot_tn(a_ref[...].astype(_BF), b_ref[...].astype(_BF))

        @pl.when(pl.program_id(1) == last)
        def _():
            if slots:
                for c in range(tn // slots):
                    o_ref[c] = acc[:, c * slots:(c + 1) * slots].astype(_BF)
            else:
                o_ref[...] = acc[...].astype(_BF)

    if slots:
        out_spec = pl.BlockSpec((tn // slots, K, slots), lambda j, i: (j, 0, 0))
        out_shape = jax.ShapeDtypeStruct((N // slots, K, slots), _BF)
    else:
        out_spec = pl.BlockSpec((K, tn), lambda j, i: (0, j))
        out_shape = jax.ShapeDtypeStruct((K, N), _BF)
    return pl.pallas_call(
        body, name=name, grid=(N // tn, L // tm),
        in_specs=[pl.BlockSpec((tm, K), lambda j, i: (i, 0)), pl.BlockSpec((tm, tn), lambda j, i: (i, j))],
        out_specs=out_spec, out_shape=out_shape,
        scratch_shapes=[pltpu.VMEM((K, tn), _F32)],
        compiler_params=_params("parallel", "arbitrary"),
    )(a, b)


def _dw_out(y_ret, gl, dmix, tk):
    L = dmix.shape[0]
    last = L // tk - 1

    def body(a0_ref, a1_ref, b_ref, o_ref, acc):
        @pl.when(pl.program_id(0) == 0)
        def _():
            acc[...] = jnp.zeros_like(acc)

        bv = b_ref[...]
        acc[:RET_W, :] += _dot_tn(a0_ref[...], bv)
        acc[RET_W:, :] += _dot_tn(a1_ref[...], bv)

        @pl.when(pl.program_id(0) == last)
        def _():
            o_ref[...] = acc[...].astype(_BF)

    return pl.pallas_call(
        body, name="dw_out", grid=(L // tk,),
        in_specs=[_row_spec(tk, RET_W), _row_spec(tk, SSM_W), _row_spec(tk, D_MODEL)],
        out_specs=_full_spec((D_MODEL, D_MODEL)), out_shape=jax.ShapeDtypeStruct((D_MODEL, D_MODEL), _BF),
        scratch_shapes=[pltpu.VMEM((D_MODEL, D_MODEL), _F32)],
        compiler_params=_params("arbitrary"),
    )(y_ret, gl, dmix)


def _dw_in_t(pieces, h, tk):
    L = h.shape[0]
    last = L // tk - 1

    def body(p0, p1, p2, p3, p4, h_ref, o_ref, acc):
        @pl.when(pl.program_id(0) == 0)
        def _():
            acc[...] = jnp.zeros_like(acc)

        hv = h_ref[...]
        for j, p in enumerate((p0, p1, p2, p3, p4)):
            acc[j * RET_W:(j + 1) * RET_W, :] += _dot_tn(p[...].astype(_BF), hv)

        @pl.when(pl.program_id(0) == last)
        def _():
            o_ref[...] = acc[...].astype(_BF)

    return pl.pallas_call(
        body, name="dw_in", grid=(L // tk,),
        in_specs=[_row_spec(tk, RET_W)] * 5 + [_row_spec(tk, D_MODEL)],
        out_specs=_full_spec((IN_COLS, D_MODEL)), out_shape=jax.ShapeDtypeStruct((IN_COLS, D_MODEL), _BF),
        scratch_shapes=[pltpu.VMEM((IN_COLS, D_MODEL), _F32)],
        compiler_params=_params("arbitrary"),
    )(*pieces, h)


def _mixout_bwd(dmix, w_out, w_glu, ys, s, o, gate, ggn, tm, after=()):
    L = dmix.shape[0]

    def body(dmix_ref, wo_ref, wg_ref, ys_ref, s_ref, o_ref, gate_ref, ggn_ref,
             dglu_ref, ds_ref, dgate_ref, do_ref, dggn_ref):
        @pl.when(pl.program_id(0) == 0)
        def _():
            dggn_ref[...] = jnp.zeros_like(dggn_ref)

        ggn = ggn_ref[...]
        for rows in _row_chunks(tm):
            dcat = _dot_nt(dmix_ref[rows, :], wo_ref[...])
            dy_ret, dy_ssm = dcat[:, :RET_W], dcat[:, RET_W:]
            glu = _dot(ys_ref[rows, :], wg_ref[...])
            ga, sg = glu[:, :SSM_W], _sigmoid(glu[:, SSM_W:])
            dga = (dy_ssm * sg).astype(_BF)
            dgb = (dy_ssm * ga * sg * (1.0 - sg)).astype(_BF)
            dglu_ref[rows, :SSM_W] = dga
            dglu_ref[rows, SSM_W:] = dgb
            dys = _dot_nt(dga, wg_ref[:, :SSM_W]) + _dot_nt(dgb, wg_ref[:, SSM_W:])
            ds_ref[rows, :] = dys * _gelu_grad(s_ref[rows, :])
            gt = gate_ref[rows, :]
            sgt = _sigmoid(gt)
            for hh in range(N_HEAD):
                cols = slice(hh * HEAD_D, (hh + 1) * HEAD_D)
                ov = o_ref[rows, cols]
                dlt = ov - jnp.mean(ov, axis=-1, keepdims=True)
                rstd = lax.rsqrt(jnp.mean(dlt * dlt, axis=-1, keepdims=True) + NORM_EPS)
                on = dlt * rstd
                dyr = dy_ret[:, cols] * (gt[:, cols] * sgt[:, cols])
                dgate_ref[rows, cols] = (dy_ret[:, cols] * (on * ggn[:, cols]) * (sgt[:, cols] * (1.0 + gt[:, cols] * (1.0 - sgt[:, cols])))).astype(_BF)
                dggn_ref[:, cols] += jnp.sum(dyr * on, axis=0, keepdims=True)
                don = dyr * ggn[:, cols]
                do = rstd * (don - jnp.mean(don, axis=-1, keepdims=True) - on * jnp.mean(don * on, axis=-1, keepdims=True))
                do_ref[rows, cols] = do.astype(_BF)

    body, in_specs, operands = _ordered(
        body, [_row_spec(tm, D_MODEL), _weight_spec((D_MODEL, D_MODEL)), _weight_spec((SSM_W, 2 * SSM_W)),
               _row_spec(tm, SSM_W), _row_spec(tm, SSM_W), _row_spec(tm, RET_W), _row_spec(tm, RET_W),
               _full_spec((1, RET_W))], (dmix, w_out, w_glu, ys, s, o, gate, ggn), after)
    return pl.pallas_call(
        body, name="mixout_bwd", grid=(L // tm,),
        in_specs=in_specs,
        out_specs=[_row_spec(tm, 2 * SSM_W), _row_spec(tm, SSM_W), _row_spec(tm, RET_W), _row_spec(tm, RET_W),
                   _full_spec((1, RET_W))],
        out_shape=[jax.ShapeDtypeStruct((L, 2 * SSM_W), _BF), jax.ShapeDtypeStruct((L, SSM_W), _F32),
                   jax.ShapeDtypeStruct((L, RET_W), _BF), jax.ShapeDtypeStruct((L, RET_W), _BF),
                   jax.ShapeDtypeStruct((1, RET_W), _F32)],
        compiler_params=_params("arbitrary"),
    )(*operands)


def _s5_bwd(u, ds, xs, ent, bmat, cmat, tab_r, pw_r, d_skip, tb, after=()):
    L = u.shape[0]
    nt = L // tb
    seg = tb // SUBLANES
    G = KB_PER_STEP
    rcol = pl.BlockSpec((tb, G * LANES), lambda kb, t: (nt - 1 - t, kb))
    sp = _s5_specs(seg, time=lambda t: nt - 1 - t)
    aspec = pl.BlockSpec((G, SUBLANES, 2 * KB_STATES), lambda kb, t: (kb, 0, 0))

    def body(u_ref, ds_ref, x_ref, ent_ref, b_ref, c_ref, tr_ref, pr_ref, d_ref,
             du_ref, db_ref, dc_ref, da_ref, dd_ref, up_scr, dp_scr, g_scr, lc_scr):
        @pl.when(pl.program_id(1) == 0)
        def _():
            lc_scr[...] = jnp.zeros_like(lc_scr)
            db_ref[...] = jnp.zeros_like(db_ref)
            dc_ref[...] = jnp.zeros_like(dc_ref)
            da_ref[...] = jnp.zeros_like(da_ref)
            dd_ref[...] = jnp.zeros_like(dd_ref)

        _rows_to_segments(up_scr, u_ref, seg)
        _rows_to_segments(dp_scr, ds_ref, seg)
        for g in range(G):
            g_scr[g] = _dot_nt(dp_scr[g].astype(_BF), c_ref[g]).reshape(seg, SUBLANES, 2 * KB_STATES)
        _scan_segments(g_scr, tr_ref, pr_ref, lc_scr, seg, reverse=True, fwd_ref=x_ref, fwd_entry_ref=ent_ref.at[:, 0],
                       da_ref=da_ref)
        for g in range(G):
            cols = slice(g * LANES, (g + 1) * LANES)
            uv, dsv = up_scr[g], dp_scr[g]
            ub, dsb = uv.astype(_BF), dsv.astype(_BF)
            lamb = g_scr[g].reshape(tb, 2 * KB_STATES).astype(_BF)
            db_ref[g] += _dot_tn(ub, lamb)
            dc_ref[g] += _dot_tn(dsb, x_ref[g].reshape(tb, 2 * KB_STATES).astype(_BF))
            dd_ref[:, cols] += jnp.sum(dsv * uv, axis=0, keepdims=True)
            up_scr[g] = _dot_nt(lamb, b_ref[g]) + d_ref[:, cols] * dsv
        _segments_to_rows(du_ref, up_scr, seg)

    body, in_specs, operands = _ordered(
        body, [rcol, rcol, sp["x"], sp["ent"], sp["b"], sp["c"], sp["tab"], sp["pw"], sp["d"]],
        (u, ds, xs, ent, bmat, cmat, tab_r, pw_r, d_skip), after)
    return pl.pallas_call(
        body, name="s5_bwd", grid=(N_KB // G, nt),
        in_specs=in_specs,
        out_specs=[rcol, sp["b"], sp["b"], aspec, sp["d"]],
        out_shape=[jax.ShapeDtypeStruct((L, SSM_W), _BF),
                   jax.ShapeDtypeStruct((N_KB, LANES, 2 * KB_STATES), _F32),
                   jax.ShapeDtypeStruct((N_KB, LANES, 2 * KB_STATES), _F32),
                   jax.ShapeDtypeStruct((N_KB, SUBLANES, 2 * KB_STATES), _F32),
                   jax.ShapeDtypeStruct((1, SSM_W), _F32)],
        scratch_shapes=[pltpu.VMEM((G, tb, LANES), _F32)] * 2
        + [pltpu.VMEM((G, seg, SUBLANES, 2 * KB_STATES), _F32), pltpu.VMEM((G, SUBLANES, 2 * KB_STATES), _F32)],
        compiler_params=_params("parallel", "arbitrary"),
    )(*operands)


def _retention_bwd(q, k, v, do, r_prev, consts, cosf, sinf, after=()):
    L = q.shape[0]
    nc = L // CHUNK
    cps = math.gcd(RET_STEP_CHUNKS, nc)
    nb = nc // cps
    blk = pl.BlockSpec((cps * CHUNK, RET_W), lambda n: (nb - 1 - n, 0))
    rope_blk = pl.BlockSpec((cps * CHUNK, HEAD_D), lambda n: (nb - 1 - n, 0))

    def body(q_ref, k_ref, v_ref, do_ref, rp_ref, dm_ref, xi_ref, zeta_ref, gc_ref, cos_ref, sin_ref,
             dq_ref, dk_ref, dv_ref, g_scr):
        @pl.when(pl.program_id(0) == 0)
        def _():
            g_scr[...] = jnp.zeros_like(g_scr)

        for hh in range(N_HEAD):
            cols = slice(hh * HEAD_D, (hh + 1) * HEAD_D)
            dm, zeta = dm_ref[hh], zeta_ref[hh]
            gst = g_scr[hh]
            for c in reversed(range(cps)):
                rows = slice(c * CHUNK, (c + 1) * CHUNK)
                qv, kv, vv, dov = q_ref[rows, cols], k_ref[rows, cols], v_ref[rows, cols], do_ref[rows, cols]
                rb = rp_ref[hh, c].astype(_BF)
                gb = gst.astype(_BF)
                sb = (_dot_nt(qv, kv) * dm).astype(_BF)
                dab = (_dot_nt(dov, vv) * dm).astype(_BF)
                dox = (dov.astype(_F32) * xi_ref[hh]).astype(_BF)
                vz = (vv.astype(_F32) * zeta).astype(_BF)
                dq = _dot(dab, kv) + _dot_nt(dox, rb)
                dk = _dot_tn(dab, qv) + _dot_nt(vz, gb)
                dv = _dot_tn(sb, dov) + _dot(kv, gb) * zeta
                gst = gc_ref[hh, 0:1, :] * gst + _dot_tn(qv, dox)
                cs, sn = cos_ref[rows, :], sin_ref[rows, :]
                dq_ref[rows, cols] = _rope_t(dq, cs, sn).astype(_BF)
                dk_ref[rows, cols] = (_rope_t(dk, cs, sn) * (HEAD_D ** -0.5)).astype(_BF)
                dv_ref[rows, cols] = dv.astype(_BF)
            g_scr[hh] = gst

    body, in_specs, operands = _ordered(
        body, [blk, blk, blk, blk, pl.BlockSpec((N_HEAD, cps, HEAD_D, HEAD_D), lambda n: (0, nb - 1 - n, 0, 0))]
        + _head_specs() + [rope_blk, rope_blk], (q, k, v, do, r_prev, *consts, cosf, sinf), after)
    return pl.pallas_call(
        body, name="retention_bwd", grid=(nb,),
        in_specs=in_specs,
        out_specs=[blk, blk, blk],
        out_shape=[jax.ShapeDtypeStruct((L, RET_W), _BF)] * 3,
        scratch_shapes=[pltpu.VMEM((N_HEAD, HEAD_D, HEAD_D), _F32)],
        compiler_params=_params("arbitrary"),
    )(*operands)


def _inproj_bwd(pieces, w_in_t, x, dx2, g1, tm, after=()):
    L = x.shape[0]

    def body(p0, p1, p2, p3, p4, w_ref, x_ref, dx2_ref, g_ref, dx_ref, dg_ref):
        @pl.when(pl.program_id(0) == 0)
        def _():
            dg_ref[...] = jnp.zeros_like(dg_ref)

        for rows in _row_chunks(tm):
            dh = None
            for j, p in enumerate((p0, p1, p2, p3, p4)):
                part = _dot(p[rows, :].astype(_BF), w_ref[j * RET_W:(j + 1) * RET_W, :])
                dh = part if dh is None else dh + part
            dz, dgr = _rms_bwd(x_ref[rows, :], g_ref[...], dh)
            dx_ref[rows, :] = dx2_ref[rows, :] + dz
            dg_ref[...] += jnp.sum(dgr, axis=0, keepdims=True)

    body, in_specs, operands = _ordered(
        body, [_row_spec(tm, RET_W)] * 5 + [_weight_spec((IN_COLS, D_MODEL)), _row_spec(tm, D_MODEL),
                                             _row_spec(tm, D_MODEL), _full_spec((1, D_MODEL))],
        (*pieces, w_in_t, x, dx2, g1), after)
    return pl.pallas_call(
        body, name="inproj_bwd", grid=(L // tm,),
        in_specs=in_specs,
        out_specs=[_row_spec(tm, D_MODEL), _full_spec((1, D_MODEL))],
        out_shape=[jax.ShapeDtypeStruct((L, D_MODEL), _F32), jax.ShapeDtypeStruct((1, D_MODEL), _F32)],
        compiler_params=_params("arbitrary"),
    )(*operands)


def _sum_adamw(parts, w, m, v, tr, name):
    _, R, Cc = parts.shape

    def body(p_ref, w_ref, m_ref, v_ref, g_ref, d_ref, nm_ref, nv_ref):
        gv = p_ref[0].astype(_F32)
        for s in range(1, N_DEV):
            gv = gv + p_ref[s].astype(_F32)
        g_ref[...] = gv
        nm = ADAM_B1 * m_ref[...] + (1.0 - ADAM_B1) * gv
        nv = ADAM_B2 * v_ref[...] + (1.0 - ADAM_B2) * (gv * gv)
        m_hat = nm / (1.0 - ADAM_B1 ** ADAM_STEP)
        v_hat = nv / (1.0 - ADAM_B2 ** ADAM_STEP)
        d_ref[...] = -ADAM_LR * (m_hat / (jnp.sqrt(v_hat) + ADAM_EPS) + ADAM_WD * w_ref[...])
        nm_ref[...] = nm
        nv_ref[...] = nv

    spec = _row_spec(tr, Cc)
    return pl.pallas_call(
        body, name=name, grid=(R // tr,),
        in_specs=[pl.BlockSpec((N_DEV, tr, Cc), lambda i: (0, i, 0))] + [spec] * 3, out_specs=[spec] * 4,
        out_shape=[jax.ShapeDtypeStruct((R, Cc), _F32)] * 4,
        compiler_params=_params("parallel"),
    )(parts, w, m, v)


def _my_place():
    return lax.axis_index("x"), lax.axis_index("y"), lax.axis_index("c")


HBM_SPEC = pl.BlockSpec(memory_space=pltpu.HBM)
SEM_SPEC = pl.BlockSpec(memory_space=pltpu.SEMAPHORE)
DATAFLOW = pltpu.SideEffectType.DATAFLOW_SIDE_EFFECTING


def _my_index():
    x, y, c = _my_place()
    return 4 * x + 2 * y + c


def _landing(own_block):
    return lax.empty((N_DEV,) + own_block.shape, own_block.dtype)


def _own_copy(src, land, sems, a, gather):
    me = _my_index()
    return pltpu.make_async_copy(src if gather else src.at[me], land.at[me], sems.at[sems.shape[0] // N_DEV * 7 + a])


def _split_copies(src_refs, land_refs, send_sems, recv_sems, gather, first=0):
    x, y, c = _my_place()
    me = 4 * x + 2 * y + c
    copies = []
    for a, (src, land) in enumerate(zip(src_refs, land_refs)):
        for kk in range(1, N_DEV):
            px, py, pc = x ^ (kk >> 2), y ^ ((kk >> 1) & 1), c ^ (kk & 1)
            peer = 4 * px + 2 * py + pc
            copies.append(pltpu.make_async_remote_copy(
                src_ref=src if gather else src.at[peer], dst_ref=land.at[me],
                send_sem=send_sems.at[(first + a) * 7 + kk - 1], recv_sem=recv_sems.at[(first + a) * 7 + kk - 1],
                device_id=(px, py, pc), device_id_type=MESH))
    return copies


def _split_start(srcs, lands, gather, name):
    n = len(srcs)

    def body(*refs):
        src_refs, land_refs = refs[:n], refs[n:2 * n]
        send_sems, recv_sems = refs[2 * n], refs[2 * n + 1]
        token = refs[-1]
        for cp in _split_copies(src_refs, land_refs, send_sems, recv_sems, gather):
            cp.start()
        for a in range(n):
            _own_copy(src_refs[a], land_refs[a], send_sems, a, gather).start()
        token[...] = jnp.zeros_like(token)

    outs = pl.pallas_call(
        body, name=name,
        out_shape=(pltpu.SemaphoreType.DMA((N_DEV * n,)), pltpu.SemaphoreType.DMA((7 * n,)),
                   *[pltpu.HBM(t.shape, t.dtype) for t in srcs], *[pltpu.HBM(t.shape, t.dtype) for t in lands],
                   jax.ShapeDtypeStruct((SUBLANES, LANES), _F32)),
        in_specs=[HBM_SPEC] * (2 * n),
        out_specs=(SEM_SPEC, SEM_SPEC, *[HBM_SPEC] * (2 * n), pl.BlockSpec(memory_space=pltpu.VMEM)),
        input_output_aliases={i: 2 + i for i in range(2 * n)},
        compiler_params=pltpu.CompilerParams(has_side_effects=DATAFLOW),
    )(*[pltpu.with_memory_space_constraint(t, pltpu.HBM) for t in list(srcs) + list(lands)])
    return outs[0], outs[1], outs[2:2 + n], outs[2 + n:2 + 2 * n], outs[-1]


def _split_wait(send_sems, recv_sems, srcs, lands, after, gather, name, first=0):
    n = len(srcs)

    def body(*refs):
        src_refs, land_refs = refs[:n], refs[n:2 * n]
        send_s, recv_s = refs[2 * n], refs[2 * n + 1]
        for cp in _split_copies(src_refs, land_refs, send_s, recv_s, gather, first):
            cp.wait_send()
            cp.wait_recv()
        for a in range(n):
            _own_copy(src_refs[a], land_refs[a], send_s, first + a, gather).wait()

    outs = pl.pallas_call(
        body, name=name,
        out_shape=tuple(pltpu.HBM(t.shape, t.dtype) for t in list(srcs) + list(lands)),
        in_specs=[HBM_SPEC] * (2 * n) + [SEM_SPEC, SEM_SPEC, pl.BlockSpec(memory_space=pl.ANY)],
        out_specs=tuple([HBM_SPEC] * (2 * n)),
        input_output_aliases={i: i for i in range(2 * n)},
        compiler_params=pltpu.CompilerParams(has_side_effects=DATAFLOW),
    )(*srcs, *lands, send_sems, recv_sems, after)
    return outs[n:]


def _discretize(lam_re, lam_im, log_dt, b_re, b_im):
    lr = jnp.minimum(lam_re, -1e-4)
    li = lam_im
    dt = jnp.exp(log_dt)[:, None]
    er = jnp.exp(lr * dt)
    ar, ai = er * jnp.cos(li * dt), er * jnp.sin(li * dt)
    den = lr * lr + li * li
    cr = ((ar - 1.0) * lr + ai * li) / den
    ci = (ai * lr - (ar - 1.0) * li) / den
    bbr = cr[:, :, None] * b_re - ci[:, :, None] * b_im
    bbi = cr[:, :, None] * b_im + ci[:, :, None] * b_re
    return ar, ai, bbr, bbi


def _cmul(ar, ai, br, bi):
    return ar * br - ai * bi, ar * bi + ai * br


def _cpowers(ar, ai, n):
    pr, pi = ar[None], ai[None]
    while pr.shape[0] < n:
        nr, ni = _cmul(pr, pi, pr[-1][None], pi[-1][None])
        pr, pi = jnp.concatenate([pr, nr]), jnp.concatenate([pi, ni])
    return pr[:n], pi[:n]


def _scan_tables(ar, ai, seg, reverse):
    if reverse:
        ai = -ai
    ar, ai = ar.reshape(N_KB, KB_STATES), ai.reshape(N_KB, KB_STATES)
    pr, pi = _cpowers(ar, ai, seg)
    a1 = (pr[-1], pi[-1])
    a2 = _cmul(*a1, *a1)
    a4 = _cmul(*a2, *a2)
    row = jnp.arange(SUBLANES)[None, :, None]
    wide = lambda t: jnp.broadcast_to(t[:, None, :], (N_KB, SUBLANES, KB_STATES))
    tabs = [wide(ar), wide(ai)]
    for dist, (qr, qi) in ((1, a1), (2, a2), (4, a4)):
        keep = (row < SUBLANES - dist) if reverse else (row >= dist)
        tabs += [jnp.where(keep, wide(qr), 0.0), jnp.where(keep, wide(qi), 0.0)]
    tabs += [wide(a1[0]), wide(a1[1])]
    if reverse:
        pr, pi = pr[::-1], pi[::-1]
    pw = jnp.transpose(jnp.concatenate([pr, pi], axis=-1), (1, 0, 2))[:, :, None, :]
    return jnp.stack(tabs, axis=1).astype(_F32), pw.astype(_F32)


def _block_diag_in(br, bi):
    eye = jnp.eye(GROUPS_PER_KB, dtype=_F32)
    one = lambda t: jnp.einsum("kgpc,gh->kgchp", t.reshape(N_KB, GROUPS_PER_KB, N_STATE, SSM_GC), eye).reshape(
        N_KB, LANES, KB_STATES)
    return jnp.concatenate([one(br), one(bi)], axis=-1)


def _block_diag_in_t(dmat):
    d6 = dmat.reshape(N_KB, GROUPS_PER_KB, SSM_GC, 2, GROUPS_PER_KB, N_STATE)
    eye = jnp.eye(GROUPS_PER_KB, dtype=_F32)
    both = jnp.einsum("kgcrhp,gh->rkgpc", d6, eye).reshape(2, N_GROUP, N_STATE, SSM_GC)
    return both[0], both[1]


def _block_diag_out(c_re, c_im):
    eye = jnp.eye(GROUPS_PER_KB, dtype=_F32)
    one = lambda t: jnp.einsum("kgcp,gh->khpgc", t.reshape(N_KB, GROUPS_PER_KB, SSM_GC, N_STATE), eye).reshape(
        N_KB, KB_STATES, LANES)
    return jnp.concatenate([one(c_re), -one(c_im)], axis=1)


def _block_diag_out_t(dmat_t):
    d6 = dmat_t.reshape(N_KB, GROUPS_PER_KB, SSM_GC, 2, GROUPS_PER_KB, N_STATE)
    eye = jnp.eye(GROUPS_PER_KB, dtype=_F32)
    both = jnp.einsum("kgcrhp,gh->rkgcp", d6, eye).reshape(2, N_GROUP, SSM_GC, N_STATE)
    return both[0], -both[1]


SMALL_NAMES = ("norm_mix_pre", "norm_mix_post", "ret_gn_gain", "ssm_lambda_re", "ssm_lambda_im", "ssm_log_dt",
               "ssm_b_re", "ssm_b_im", "ssm_c_re", "ssm_c_im", "ssm_d", "norm_mlp_pre", "norm_mlp_post")


def _local_grads(x, tgt, small, weights, emit, emit_small, tm, tk, tb, zero=0.0):
    L = x.shape[0]
    g1, g2, ggn = small["norm_mix_pre"], small["norm_mix_post"], small["ret_gn_gain"]
    g3, g4, d_skip = small["norm_mlp_pre"], small["norm_mlp_post"], small["ssm_d"]

    rope = _rope_tables(L)
    consts = _ret_consts()

    disc_in = (small["ssm_lambda_re"][0], small["ssm_lambda_im"][0], small["ssm_log_dt"][0] + zero,
               small["ssm_b_re"][0], small["ssm_b_im"][0])
    (ar, ai, bbr, bbi), disc_vjp = jax.vjp(_discretize, *disc_in)
    bmat = _block_diag_in(bbr, bbi).astype(_BF)
    cmat = _block_diag_out(small["ssm_c_re"][0], small["ssm_c_im"][0]).astype(_BF)
    seg = tb // SUBLANES
    tab_f, pw_f = _scan_tables(ar, ai, seg, False)
    tab_r, pw_r = _scan_tables(ar, ai, seg, True)

    h1 = _prenorm(x, g1, min(4 * tm, L), after=(pw_r,))
    (w_in_t,) = weights("in", h1)
    q, k, v, gate, u, cosf, sinf = _inproj_fwd(h1, w_in_t, rope, min(4 * tm, L))
    o, y_ret, r_prev = _retention_fwd(q, k, v, gate, ggn, consts)
    s, xs, ent = _s5_fwd(u, bmat, cmat, tab_f, pw_f, d_skip, tb)
    w_glu, w_out = weights("mix", s)
    w_ff1, w_ff2 = weights("mlp", s)
    ys, gl, mix, x2, h3, act, dy, dm, dg4, sq = _mix_mlp_fwd_loss(s, y_ret, x, tgt, w_glu, w_out, g2, g3, g4,
                                                                 w_ff1, w_ff2, tm)

    df1, dw_ff2 = _ff2_bwd(dm, act, w_ff2, min(1024, L), 1024)
    dx2, dmix, dg3, dg2 = _ff1_bwd(df1, w_ff1, x2, mix, dy, g3, g2, min(2 * tm, L))
    dw_ff1 = _matmul_tn(h3, df1, tk, 2 * FF1_COLS, "dw_ff1", slots=FF1_COLS)
    token = emit({"w_ff1": dw_ff1, "w_ff2": dw_ff2})
    dglu, ds, dgate, do, dggn = _mixout_bwd(dmix, w_out, w_glu, ys, s, o, gate, ggn, min(2 * tm, L), after=token)
    dw_out = _dw_out(y_ret, gl, dmix, tk)
    dw_glu = _matmul_tn(ys, dglu, tk, 1024, "dw_glu")
    token = emit({"w_glu": dw_glu, "w_out": dw_out})
    du, dbmat, dcmat, da8, dd = _s5_bwd(u, ds, xs, ent, bmat, cmat, tab_r, pw_r, d_skip, tb, after=token)

    da = jnp.sum(da8, axis=1)
    dar = da[:, :KB_STATES].reshape(N_GROUP, N_STATE)
    dai = da[:, KB_STATES:].reshape(N_GROUP, N_STATE)
    dbr, dbi = _block_diag_in_t(dbmat)
    dlre, dlim, dldt, dbre, dbim = disc_vjp((dar, dai, dbr, dbi))
    dcre, dcim = _block_diag_out_t(dcmat)
    token = emit_small({
        "norm_mix_post": dg2, "ret_gn_gain": dggn,
        "ssm_lambda_re": dlre[None], "ssm_lambda_im": dlim[None], "ssm_log_dt": dldt[None],
        "ssm_b_re": dbre[None], "ssm_b_im": dbim[None], "ssm_c_re": dcre[None], "ssm_c_im": dcim[None],
        "ssm_d": dd, "norm_mlp_pre": dg3, "norm_mlp_post": dg4,
    }, sq)

    dq, dk, dv = _retention_bwd(q, k, v, do, r_prev, consts, cosf, sinf, after=token)
    pieces = (dq, dk, dv, dgate, du)
    dw_in_t = _dw_in_t(pieces, h1, min(1024, L))
    token = emit({"w_in": dw_in_t})
    gx, dg1 = _inproj_bwd(pieces, w_in_t, x, dx2, g1, min(2 * tm, L), after=token)
    return gx, dg1


BIG_SHAPES = {"w_in": (D_MODEL, IN_COLS // N_DEV), "w_glu": (SSM_W, 2 * SSM_W // N_DEV), "w_out": (D_MODEL // N_DEV, D_MODEL),
              "w_ff1": (D_MODEL, FF1_COLS), "w_ff2": (D_FF // N_DEV, D_MODEL)}
BIG_NAMES = ("w_in", "w_glu", "w_out", "w_ff1", "w_ff2")


def _cols_from_slots(g):
    return jnp.transpose(g, (1, 0, 2)).reshape(g.shape[1], N_DEV * g.shape[2])


def _cols_to_slots(dw):
    r, cols = dw.shape
    return jnp.transpose(dw.reshape(r, N_DEV, cols // N_DEV), (1, 0, 2))


WEIGHT_GROUPS = {"in": ("w_in",), "mix": ("w_glu", "w_out"), "mlp": ("w_ff1", "w_ff2")}


def _weight_from_slots(name, g):
    if name == "w_glu":
        return _cols_from_slots(g)
    if name == "w_ff1":
        return g
    return g.reshape(N_DEV * g.shape[1], g.shape[2])


def _grad_slots(name, dw):
    if name == "w_glu":
        return _cols_to_slots(dw)
    if name == "w_ff1":
        return dw
    if name == "w_in":
        return dw.reshape(N_DEV, BIG_SHAPES[name][1], BIG_SHAPES[name][0])
    return dw.reshape((N_DEV,) + BIG_SHAPES[name])


PIECE_ROWS = 8


VEC_NAMES = tuple(n for n in SMALL_NAMES if n[:6] not in ("ssm_b_", "ssm_c_"))
BC_NAMES = ("ssm_b_re", "ssm_b_im", "ssm_c_re", "ssm_c_im")
BC_ROWS = N_GROUP * SSM_GC


def _bc_view(name, t):
    t = t[0]
    if name.startswith("ssm_b_"):
        t = jnp.swapaxes(t, 1, 2)
    return t.reshape(BC_ROWS, N_STATE)


def _bc_unview(name, t):
    t = t.reshape(N_GROUP, SSM_GC, N_STATE)
    if name.startswith("ssm_b_"):
        t = jnp.swapaxes(t, 1, 2)
    return t[None]


def _pack_bc(vals):
    return jnp.concatenate([_bc_view(n, vals[n]).astype(_F32) for n in BC_NAMES], axis=0)


def _unpack_bc(buf):
    return {n: _bc_unview(n, buf[j * BC_ROWS:(j + 1) * BC_ROWS]) for j, n in enumerate(BC_NAMES)}


def _small_layout(shapes):
    off, rows = {}, 0
    for n in VEC_NAMES:
        off[n] = rows
        rows += -(-math.prod(shapes[n]) // (PIECE_ROWS * LANES)) * PIECE_ROWS
    return off, rows, rows + PIECE_ROWS


def _pack_small(vals, shapes, last=None):
    parts = []
    for n in VEC_NAMES:
        flat = vals[n].reshape(-1).astype(_F32)
        pad = -flat.shape[0] % (PIECE_ROWS * LANES)
        if pad:
            flat = jnp.concatenate([flat, jnp.zeros((pad,), _F32)])
        parts.append(flat.reshape(-1, LANES))
    parts.append(jnp.zeros((PIECE_ROWS, LANES), _F32) if last is None else last)
    return jnp.concatenate(parts, axis=0)


def _unpack_small(buf, shapes):
    off, _, _ = _small_layout(shapes)
    out = {}
    for n in VEC_NAMES:
        size = math.prod(shapes[n])
        rows = -(-size // LANES)
        out[n] = buf[off[n]:off[n] + rows].reshape(-1)[:size].reshape(shapes[n])
    return out


WEIGHT_NAMES = ('norm_mix_pre', 'norm_mix_post', 'w_in', 'ret_gn_gain', 'ssm_lambda_re', 'ssm_lambda_im', 'ssm_log_dt',
                'ssm_b_re', 'ssm_b_im', 'ssm_c_re', 'ssm_c_im', 'ssm_d', 'w_glu', 'w_out', 'norm_mlp_pre',
                'norm_mlp_post', 'w_ff1', 'w_ff2')


def kernel(x, norm_mix_pre, norm_mix_post, w_in, ret_gn_gain, ssm_lambda_re, ssm_lambda_im, ssm_log_dt, ssm_b_re, ssm_b_im, ssm_c_re, ssm_c_im, ssm_d, w_glu, w_out, norm_mlp_pre, norm_mlp_post, w_ff1, w_ff2, loss_target, m_norm_mix_pre, m_norm_mix_post, m_w_in, m_ret_gn_gain, m_ssm_lambda_re, m_ssm_lambda_im, m_ssm_log_dt, m_ssm_b_re, m_ssm_b_im, m_ssm_c_re, m_ssm_c_im, m_ssm_d, m_w_glu, m_w_out, m_norm_mlp_pre, m_norm_mlp_post, m_w_ff1, m_w_ff2, v_norm_mix_pre, v_norm_mix_post, v_w_in, v_ret_gn_gain, v_ssm_lambda_re, v_ssm_lambda_im, v_ssm_log_dt, v_ssm_b_re, v_ssm_b_im, v_ssm_c_re, v_ssm_c_im, v_ssm_d, v_w_glu, v_w_out, v_norm_mlp_pre, v_norm_mlp_post, v_w_ff1, v_w_ff2):
    args = dict(locals())
    w = {n: args[n] for n in WEIGHT_NAMES}
    m = {n: args["m_" + n] for n in WEIGHT_NAMES}
    v = {n: args["v_" + n] for n in WEIGHT_NAMES}
    L = x.shape[1]
    tm = min(256, L)
    tk = min(2048, L)
    tb = min(1024, L)

    calls = {"in": ("w_in",), "rest": WEIGHT_GROUPS["mix"] + WEIGHT_GROUPS["mlp"]}
    started, zero = {}, jnp.zeros((), _F32)
    for call, names in calls.items():
        blocks = [(w[n][0].T if n == "w_in" else w[n][0]).astype(_BF) for n in names]
        blocks[0] = blocks[0] + zero.astype(_BF)
        started[call] = _split_start(blocks, [_landing(b) for b in blocks], True, "weights_start_" + call)
        zero = started[call][4][0, 0]

    def weights(group, after):
        names = WEIGHT_GROUPS[group]
        call = "in" if group == "in" else "rest"
        first = calls[call].index(names[0])
        part = slice(first, first + len(names))
        got = started[call]
        landed = _split_wait(got[0], got[1], got[2][part], got[3][part], after, True, "weights_wait_" + group, first=first)
        return [_weight_from_slots(n, g) for n, g in zip(names, landed)]

    in_flight = []

    def emit(dws):
        names = sorted(dws)
        srcs = [_grad_slots(n, dws[n]) for n in names]
        lands = [_landing(t[0]) for t in srcs]
        started = _split_start(srcs, lands, False, "grads_start_" + "_".join(names))
        in_flight.append((names, started))
        return (started[4],)

    shapes = {n: w[n].shape for n in SMALL_NAMES}
    first_piece = {SMALL_NAMES[0]: jnp.zeros(shapes[SMALL_NAMES[0]], _F32)}
    small_flight = []

    def emit_small(gs, sq):
        loss_rows = jnp.broadcast_to(0.5 / D_MODEL * jnp.sum(sq), (PIECE_ROWS, LANES)).astype(_F32)
        bufs = [_pack_small({**first_piece, **gs}, shapes, loss_rows), _pack_bc(gs)]
        small_flight.append(_split_start(bufs, [_landing(b) for b in bufs], True, "small_grads_start"))
        return (small_flight[0][4],)

    small_w = {n: w[n] for n in SMALL_NAMES}
    gx, dg1 = _local_grads(x[0], loss_target[0], small_w, weights, emit, emit_small, tm, tk, tb, zero=zero)
    last_buf = dg1.reshape(PIECE_ROWS, LANES)
    last_started = _split_start([last_buf], [_landing(last_buf)], True, "last_grad_start")

    grads, delta, new_m, new_v = {}, {}, {}, {}
    after = last_started[4]
    for names, started in in_flight:
        landed = _split_wait(*started[:4], after, False, "grads_wait_" + "_".join(names))
        for n, parts in zip(names, landed):
            flip = (lambda t: t.T) if n == "w_in" else (lambda t: t)
            res = _sum_adamw(parts, flip(w[n][0]), flip(m[n][0]), flip(v[n][0]), math.gcd(256, parts.shape[1]), "adamw_" + n)
            grads[n], delta[n], new_m[n], new_v[n] = (flip(t)[None] for t in res)
        after = res[1]
    small_parts, bc_parts = _split_wait(*small_flight[0][:4], after, True, "small_grads_wait")
    last_parts = _split_wait(*last_started[:4], small_parts, True, "last_grad_wait")[0]
    small_parts = lax.dynamic_update_slice(small_parts, last_parts, (0, 0, 0))
    res_bc = _sum_adamw(bc_parts, _pack_bc(w), _pack_bc(m), _pack_bc(v), BC_ROWS, "adamw_bc")
    sw, sm, sv = _pack_small(w, shapes), _pack_small(m, shapes), _pack_small(v, shapes)
    res = _sum_adamw(small_parts, sw, sm, sv, sw.shape[0], "adamw_small")
    for dst, buf, buf_bc in zip((grads, delta, new_m, new_v), res, res_bc):
        dst.update(_unpack_small(buf, shapes))
        dst.update(_unpack_bc(buf_bc))
    _, loss_at, _ = _small_layout(shapes)
    loss = res[0][loss_at, 0]

    return (loss, gx[None], *[grads[n] for n in WEIGHT_NAMES], *[delta[n] for n in WEIGHT_NAMES],
            *[new_m[n] for n in WEIGHT_NAMES], *[new_v[n] for n in WEIGHT_NAMES])
```

```python
import math

import jax
import jax.numpy as jnp
from jax import lax
from jax.experimental import pallas as pl
from jax.experimental.pallas import tpu as pltpu

_BF = jnp.bfloat16
_F32 = jnp.float32

D_MODEL = 1024
RET_W = 512
N_HEAD = 4
HEAD_D = 128
CHUNK = 256
ROPE_CHUNK = 128
SSM_W = 512
SSM_GC = 16
N_GROUP = 32
N_STATE = 64
GROUPS_PER_KB = 8
N_KB = 4
KB_STATES = GROUPS_PER_KB * N_STATE
D_FF = 4096
IN_COLS = 2560
NORM_EPS = 1e-6
ROPE_BASE = 10000.0
N_DEV = 8

ADAM_LR = 0.001
ADAM_B1 = 0.9
ADAM_B2 = 0.999
ADAM_EPS = 1e-08
ADAM_WD = 0.01
ADAM_STEP = 10

SUBLANES = 8
LANES = 128
VMEM_LIMIT = 52 * 1024 * 1024
RET_STEP_CHUNKS = 2
KB_PER_STEP = 2
SCAN_UNROLL = True
FIX_UNROLL = 8

MESH = pl.DeviceIdType.MESH


def _params(*sem):
    return pltpu.CompilerParams(dimension_semantics=sem, vmem_limit_bytes=VMEM_LIMIT)


def _dot(a, b):
    return jnp.dot(a, b, preferred_element_type=_F32)


def _dot_nt(a, b):
    return lax.dot_general(a, b, (((1,), (1,)), ((), ())), preferred_element_type=_F32)


def _dot_tn(a, b):
    return lax.dot_general(a, b, (((0,), (0,)), ((), ())), preferred_element_type=_F32)


def _rms_r(z):
    return lax.rsqrt(jnp.mean(z * z, axis=-1, keepdims=True) + NORM_EPS)


def _rms_bwd(z, g, dn):
    r = _rms_r(z)
    t = dn * g
    dz = r * t - z * (r * r * r * jnp.mean(t * z, axis=-1, keepdims=True))
    return dz, dn * z * r


def _rope(t, cs, sn):
    return t * cs + pltpu.roll(t, HEAD_D // 2, 1) * sn


def _rope_t(t, cs, sn):
    return t * cs - pltpu.roll(t, HEAD_D // 2, 1) * sn


def _sigmoid(z):
    return 1.0 / (1.0 + jnp.exp(-z))


_GELU_C = math.sqrt(2.0 / math.pi)


def _gelu(z):
    return 0.5 * z * (1.0 + jnp.tanh(_GELU_C * (z + 0.044715 * z * z * z)))


def _gelu_grad(z):
    th = jnp.tanh(_GELU_C * (z + 0.044715 * z * z * z))
    return 0.5 * (1.0 + th) + 0.5 * z * (1.0 - th * th) * _GELU_C * (1.0 + 3 * 0.044715 * z * z)


ROW_CHUNK = 256


def _row_chunks(tm):
    return [pl.ds(i, min(ROW_CHUNK, tm)) for i in range(0, tm, ROW_CHUNK)]


def _ordered(body, in_specs, operands, after):
    k = len(after)
    if not k:
        return body, list(in_specs), tuple(operands)
    return ((lambda *refs: body(*refs[k:])), [pl.BlockSpec(memory_space=pl.ANY)] * k + list(in_specs),
            tuple(after) + tuple(operands))


def _row_spec(tm, n):
    return pl.BlockSpec((tm, n), lambda i: (i, 0))


def _full_spec(shape):
    nd = len(shape)
    return pl.BlockSpec(shape, lambda *_: (0,) * nd)


def _weight_spec(shape):
    nd = len(shape)
    return pl.BlockSpec(shape, lambda *_: (0,) * nd, pipeline_mode=pl.Buffered(1))


def _rope_tables(L):
    half = HEAD_D // 2
    inv_freq = ROPE_BASE ** (-jnp.arange(half, dtype=_F32) / half)
    twice = lambda t: jnp.concatenate([t, t], axis=-1)
    off = jnp.arange(ROPE_CHUNK, dtype=_F32)[:, None] * inv_freq[None, :]
    start = (ROPE_CHUNK * jnp.arange(L // ROPE_CHUNK, dtype=_F32))[:, None] * inv_freq[None, :]
    return (twice(jnp.cos(off)), twice(jnp.sin(off)),
            twice(jnp.cos(start))[:, None, :], twice(jnp.sin(start))[:, None, :])


def _prenorm(x, g, tm, after=()):
    L = x.shape[0]

    def body(x_ref, g_ref, h_ref):
        xv = x_ref[...]
        h_ref[...] = (xv * _rms_r(xv) * g_ref[...]).astype(_BF)

    body, in_specs, operands = _ordered(body, [_row_spec(tm, D_MODEL), _full_spec((1, D_MODEL))], (x, g), after)
    return pl.pallas_call(
        body, name="prenorm", grid=(L // tm,),
        in_specs=in_specs, out_specs=_row_spec(tm, D_MODEL),
        out_shape=jax.ShapeDtypeStruct((L, D_MODEL), _BF),
        compiler_params=_params("parallel"),
    )(*operands)


def _inproj_fwd(h, w_in_t, rope, tm):
    L = h.shape[0]
    n_chunks = tm // ROPE_CHUNK

    def body(h_ref, w_ref, co_ref, so_ref, cs_ref, ss_ref, q_ref, k_ref, v_ref, gate_ref, u_ref, cos_ref, sin_ref):
        proj = _dot_nt(h_ref[...], w_ref[...])
        lane = lax.broadcasted_iota(jnp.int32, (ROPE_CHUNK, HEAD_D), 1)
        sign = jnp.where(lane < HEAD_D // 2, -1.0, 1.0)
        co, so = co_ref[...], so_ref[...]
        for c in range(n_chunks):
            chunk = pl.program_id(0) * n_chunks + c
            cst, sst = cs_ref[chunk], ss_ref[chunk]
            rows = slice(c * ROPE_CHUNK, (c + 1) * ROPE_CHUNK)
            cs = co * cst - so * sst
            sn = (so * cst + co * sst) * sign
            cos_ref[rows, :] = cs
            sin_ref[rows, :] = sn
            for hh in range(N_HEAD):
                lo = hh * HEAD_D
                q_ref[rows, lo:lo + HEAD_D] = _rope(proj[rows, lo:lo + HEAD_D], cs, sn).astype(_BF)
                kh = _rope(proj[rows, RET_W + lo:RET_W + lo + HEAD_D], cs, sn) * (HEAD_D ** -0.5)
                k_ref[rows, lo:lo + HEAD_D] = kh.astype(_BF)
        v_ref[...] = proj[:, 2 * RET_W:3 * RET_W].astype(_BF)
        gate_ref[...] = proj[:, 3 * RET_W:4 * RET_W]
        u_ref[...] = proj[:, 4 * RET_W:]

    nc = L // ROPE_CHUNK
    return pl.pallas_call(
        body, name="inproj_fwd", grid=(L // tm,),
        in_specs=[_row_spec(tm, D_MODEL), _weight_spec((IN_COLS, D_MODEL)),
                  _full_spec((ROPE_CHUNK, HEAD_D)), _full_spec((ROPE_CHUNK, HEAD_D)),
                  _full_spec((nc, 1, HEAD_D)), _full_spec((nc, 1, HEAD_D))],
        out_specs=[_row_spec(tm, RET_W)] * 5 + [_row_spec(tm, HEAD_D)] * 2,
        out_shape=[jax.ShapeDtypeStruct((L, RET_W), _BF)] * 3 + [jax.ShapeDtypeStruct((L, RET_W), _F32)] * 2
        + [jax.ShapeDtypeStruct((L, HEAD_D), _F32)] * 2,
        compiler_params=_params("parallel"),
    )(h, w_in_t, *rope)


def _ret_consts():
    lg = jnp.log(1.0 - jnp.exp(jnp.linspace(math.log(1.0 / 32), math.log(1.0 / 512), N_HEAD))).astype(_F32)
    idx = jnp.arange(CHUNK, dtype=_F32)
    diff = idx[:, None] - idx[None, :]
    decay = jnp.where(diff[None] >= 0, jnp.exp(jnp.maximum(diff, 0.0)[None] * lg[:, None, None]), 0.0)
    zeta = jnp.exp((CHUNK - 1 - idx)[None, :] * lg[:, None])
    xi = jnp.exp((idx + 1.0)[None, :] * lg[:, None])
    gc = jnp.exp(CHUNK * lg)
    wide = lambda t: jnp.broadcast_to(t[:, :, None], (N_HEAD, CHUNK, HEAD_D)).astype(_F32)
    gcw = jnp.broadcast_to(gc[:, None, None], (N_HEAD, SUBLANES, HEAD_D)).astype(_F32)
    return decay.astype(_F32), wide(xi), wide(zeta), gcw


def _head_specs():
    wide = _full_spec((N_HEAD, CHUNK, HEAD_D))
    return [_full_spec((N_HEAD, CHUNK, CHUNK)), wide, wide, _full_spec((N_HEAD, SUBLANES, HEAD_D))]


def _retention_fwd(q, k, v, gate, ggn, consts):
    L = q.shape[0]
    nc = L // CHUNK
    cps = math.gcd(RET_STEP_CHUNKS, nc)
    blk = pl.BlockSpec((cps * CHUNK, RET_W), lambda n: (n, 0))

    def body(q_ref, k_ref, v_ref, gate_ref, ggn_ref, dm_ref, xi_ref, zeta_ref, gc_ref,
             o_ref, y_ref, rp_ref, r_scr):
        @pl.when(pl.program_id(0) == 0)
        def _():
            r_scr[...] = jnp.zeros_like(r_scr)

        for hh in range(N_HEAD):
            cols = slice(hh * HEAD_D, (hh + 1) * HEAD_D)
            state = r_scr[hh]
            for c in range(cps):
                rows = slice(c * CHUNK, (c + 1) * CHUNK)
                qv, kv, vv = q_ref[rows, cols], k_ref[rows, cols], v_ref[rows, cols]
                s = _dot_nt(qv, kv) * dm_ref[hh]
                o = _dot(s.astype(_BF), vv) + _dot(qv, state.astype(_BF)) * xi_ref[hh]
                o_ref[rows, cols] = o
                rp_ref[hh, c] = state
                vz = (vv.astype(_F32) * zeta_ref[hh]).astype(_BF)
                state = gc_ref[hh, 0:1, :] * state + _dot_tn(kv, vz)
                dlt = o - jnp.mean(o, axis=-1, keepdims=True)
                on = dlt * lax.rsqrt(jnp.mean(dlt * dlt, axis=-1, keepdims=True) + NORM_EPS)
                gt = gate_ref[rows, cols]
                y_ref[rows, cols] = (gt * _sigmoid(gt) * (on * ggn_ref[:, cols])).astype(_BF)
            r_scr[hh] = state

    return pl.pallas_call(
        body, name="retention_fwd", grid=(nc // cps,),
        in_specs=[blk, blk, blk, blk, _full_spec((1, RET_W))] + _head_specs(),
        out_specs=[blk, blk, pl.BlockSpec((N_HEAD, cps, HEAD_D, HEAD_D), lambda n: (0, n, 0, 0))],
        out_shape=[jax.ShapeDtypeStruct((L, RET_W), _F32), jax.ShapeDtypeStruct((L, RET_W), _BF),
                   jax.ShapeDtypeStruct((N_HEAD, nc, HEAD_D, HEAD_D), _F32)],
        scratch_shapes=[pltpu.VMEM((N_HEAD, HEAD_D, HEAD_D), _F32)],
        compiler_params=_params("arbitrary"),
    )(q, k, v, gate, ggn, *consts)


def _rows_to_segments(dst_scr, src_ref, seg):
    for g in range(dst_scr.shape[0]):
        for j in range(SUBLANES):
            dst_scr[g, pl.ds(j, seg, stride=SUBLANES), :] = src_ref[pl.ds(j * seg, seg), g * LANES:(g + 1) * LANES]


def _segments_to_rows(dst_ref, src_scr, seg):
    for g in range(src_scr.shape[0]):
        for j in range(SUBLANES):
            dst_ref[pl.ds(j * seg, seg), g * LANES:(g + 1) * LANES] = src_scr[g, pl.ds(j, seg, stride=SUBLANES), :].astype(dst_ref.dtype)


def _scan_segments(x_ref, tab_ref, pw_ref, carry_ref, seg, reverse, entry_ref=None, fwd_ref=None, fwd_entry_ref=None,
                   da_ref=None):
    G = x_ref.shape[0]
    W = KB_STATES
    re, im = pl.ds(0, W), pl.ds(W, W)
    row_id = lax.broadcasted_iota(jnp.int32, (SUBLANES, W), 0)
    edge_in = (row_id == SUBLANES - 1) if reverse else (row_id == 0)
    edge_out = 0 if reverse else SUBLANES - 1
    a_tab = [(tab_ref[g, 0], tab_ref[g, 1]) for g in range(G)]

    def local(i, st):
        r = (seg - 1 - i) if reverse else i
        out = []
        for g in range(G):
            (ar, ai), (sr, si) = a_tab[g], st[g]
            nr = ar * sr - ai * si + x_ref[g, r, :, re]
            ni = ar * si + ai * sr + x_ref[g, r, :, im]
            x_ref[g, r, :, re] = nr
            x_ref[g, r, :, im] = ni
            out.append((nr, ni))
        return tuple(out)

    zero = jnp.zeros((SUBLANES, W), _F32)
    ends = lax.fori_loop(0, seg, local, tuple((zero, zero) for _ in range(G)), unroll=SCAN_UNROLL)

    entry = []
    shift = (SUBLANES - 1) if reverse else 1
    for g in range(G):
        er, ei = ends[g]
        fr = jnp.where(edge_in, carry_ref[g, :, re], pltpu.roll(er, shift, 0))
        fi = jnp.where(edge_in, carry_ref[g, :, im], pltpu.roll(ei, shift, 0))
        for j, dist in enumerate((1, 2, 4)):
            pr, pi = tab_ref[g, 2 + 2 * j], tab_ref[g, 3 + 2 * j]
            sh = (SUBLANES - dist) if reverse else dist
            sr, si = pltpu.roll(fr, sh, 0), pltpu.roll(fi, sh, 0)
            fr, fi = fr + pr * sr - pi * si, fi + pr * si + pi * sr
        br, bi = tab_ref[g, 8], tab_ref[g, 9]
        outr = br * fr - bi * fi + er
        outi = br * fi + bi * fr + ei
        carry_ref[g, :, re] = jnp.broadcast_to(outr[edge_out:edge_out + 1, :], (SUBLANES, W))
        carry_ref[g, :, im] = jnp.broadcast_to(outi[edge_out:edge_out + 1, :], (SUBLANES, W))
        entry.append((fr, fi))
        if entry_ref is not None:
            entry_ref[g, :, re] = fr
            entry_ref[g, :, im] = fi

    add_da = da_ref is not None

    def fix(r, st, first=False):
        out = []
        for g in range(G):
            fr, fi = entry[g]
            pwr, pwi = pw_ref[g, r, :, re], pw_ref[g, r, :, im]
            xr = x_ref[g, r, :, re] + (pwr * fr - pwi * fi)
            xi = x_ref[g, r, :, im] + (pwr * fi + pwi * fr)
            x_ref[g, r, :, re] = xr
            x_ref[g, r, :, im] = xi
            if add_da:
                prev = fwd_entry_ref.at[g] if first else fwd_ref.at[g, r - 1]
                xpr, xpi = prev[:, re], prev[:, im]
                out.append((st[g][0] + (xr * xpr + xi * xpi), st[g][1] + (xi * xpr - xr * xpi)))
            else:
                out.append(st[g])
        return tuple(out)

    if add_da:
        st = fix(0, tuple((zero, zero) for _ in range(G)), first=True)
        st = lax.fori_loop(1, seg, fix, st, unroll=SCAN_UNROLL)
        for g in range(G):
            da_ref[g, :, re] += st[g][0]
            da_ref[g, :, im] += st[g][1]
    else:
        lax.fori_loop(0, seg, fix, tuple((zero[0:1, 0:LANES],) for _ in range(G)), unroll=FIX_UNROLL)


def _s5_specs(seg, time=lambda t: t):
    G = KB_PER_STEP
    return dict(
        x=pl.BlockSpec((G, seg, SUBLANES, 2 * KB_STATES), lambda kb, t: (kb, time(t), 0, 0)),
        ent=pl.BlockSpec((G, 1, SUBLANES, 2 * KB_STATES), lambda kb, t: (kb, time(t), 0, 0)),
        b=pl.BlockSpec((G, LANES, 2 * KB_STATES), lambda kb, t: (kb, 0, 0)),
        c=pl.BlockSpec((G, 2 * KB_STATES, LANES), lambda kb, t: (kb, 0, 0)),
        tab=pl.BlockSpec((G, 10, SUBLANES, KB_STATES), lambda kb, t: (kb, 0, 0, 0)),
        pw=pl.BlockSpec((G, seg, 1, 2 * KB_STATES), lambda kb, t: (kb, 0, 0, 0)),
        d=pl.BlockSpec((1, G * LANES), lambda kb, t: (0, kb)),
    )


def _s5_fwd(u, bmat, cmat, tab_f, pw_f, d_skip, tb):
    L = u.shape[0]
    nt = L // tb
    seg = tb // SUBLANES
    G = KB_PER_STEP
    ucol = pl.BlockSpec((tb, G * LANES), lambda kb, t: (t, kb))
    sp = _s5_specs(seg)

    def body(u_ref, b_ref, c_ref, tab_ref, pw_ref, d_ref, s_ref, x_ref, ent_ref, up_scr, y_scr, carry_scr):
        @pl.when(pl.program_id(1) == 0)
        def _():
            carry_scr[...] = jnp.zeros_like(carry_scr)

        _rows_to_segments(up_scr, u_ref, seg)
        for g in range(G):
            x_ref[g] = _dot(up_scr[g].astype(_BF), b_ref[g]).reshape(seg, SUBLANES, 2 * KB_STATES)
        _scan_segments(x_ref, tab_ref, pw_ref, carry_scr, seg, reverse=False, entry_ref=ent_ref.at[:, 0])
        for g in range(G):
            y = _dot(x_ref[g].reshape(tb, 2 * KB_STATES).astype(_BF), c_ref[g])
            y_scr[g] = y + d_ref[:, g * LANES:(g + 1) * LANES] * up_scr[g]
        _segments_to_rows(s_ref, y_scr, seg)

    return pl.pallas_call(
        body, name="s5_fwd", grid=(N_KB // G, nt),
        in_specs=[ucol, sp["b"], sp["c"], sp["tab"], sp["pw"], sp["d"]],
        out_specs=[ucol, sp["x"], sp["ent"]],
        out_shape=[jax.ShapeDtypeStruct((L, SSM_W), _F32),
                   jax.ShapeDtypeStruct((N_KB, L // SUBLANES, SUBLANES, 2 * KB_STATES), _F32),
                   jax.ShapeDtypeStruct((N_KB, nt, SUBLANES, 2 * KB_STATES), _F32)],
        scratch_shapes=[pltpu.VMEM((G, tb, LANES), _F32)] * 2 + [pltpu.VMEM((G, SUBLANES, 2 * KB_STATES), _F32)],
        compiler_params=_params("parallel", "arbitrary"),
    )(u, bmat, cmat, tab_f, pw_f, d_skip)


def _mixout_fwd(s, y_ret, x, w_glu, w_out, g2, tm):
    L = s.shape[0]

    def body(s_ref, yr_ref, x_ref, wg_ref, wo_ref, g_ref, ys_ref, gl_ref, mix_ref, x2_ref, cat_scr):
        for rows in _row_chunks(tm):
            ys = _gelu(s_ref[rows, :]).astype(_BF)
            ys_ref[rows, :] = ys
            glu = _dot(ys, wg_ref[...])
            gl = (glu[:, :SSM_W] * _sigmoid(glu[:, SSM_W:])).astype(_BF)
            gl_ref[rows, :] = gl
            cat_scr[rows, :RET_W] = yr_ref[rows, :]
            cat_scr[rows, RET_W:] = gl
            mix = _dot(cat_scr[rows, :], wo_ref[...])
            mix_ref[rows, :] = mix.astype(_BF)
            x2_ref[rows, :] = x_ref[rows, :] + mix * _rms_r(mix) * g_ref[...]

    return pl.pallas_call(
        body, name="mixout_fwd", grid=(L // tm,),
        in_specs=[_row_spec(tm, SSM_W), _row_spec(tm, RET_W), _row_spec(tm, D_MODEL),
                  _weight_spec((SSM_W, 2 * SSM_W)), _weight_spec((D_MODEL, D_MODEL)), _full_spec((1, D_MODEL))],
        out_specs=[_row_spec(tm, SSM_W), _row_spec(tm, SSM_W), _row_spec(tm, D_MODEL), _row_spec(tm, D_MODEL)],
        out_shape=[jax.ShapeDtypeStruct((L, SSM_W), _BF), jax.ShapeDtypeStruct((L, SSM_W), _BF),
                   jax.ShapeDtypeStruct((L, D_MODEL), _BF), jax.ShapeDtypeStruct((L, D_MODEL), _F32)],
        scratch_shapes=[pltpu.VMEM((tm, D_MODEL), _BF)],
        compiler_params=_params("parallel"),
    )(s, y_ret, x, w_glu, w_out, g2)


FF1_COLS = D_FF // N_DEV


def _ff1_fwd(x2, g3, w1, tm):
    L = x2.shape[0]

    def body(x_ref, g_ref, w_ref, h_ref, a_ref):
        for rows in _row_chunks(tm):
            xv = x_ref[rows, :]
            h = (xv * _rms_r(xv) * g_ref[...]).astype(_BF)
            h_ref[rows, :] = h
            for j in range(N_DEV):
                cols = slice(j * FF1_COLS, (j + 1) * FF1_COLS)
                rl = jnp.maximum(_dot(h, w_ref[j]), 0.0)
                a_ref[rows, cols] = (rl * rl).astype(_BF)

    return pl.pallas_call(
        body, name="ff1_fwd", grid=(L // tm,),
        in_specs=[_row_spec(tm, D_MODEL), _full_spec((1, D_MODEL)), _weight_spec((N_DEV, D_MODEL, FF1_COLS))],
        out_specs=[_row_spec(tm, D_MODEL), _row_spec(tm, D_FF)],
        out_shape=[jax.ShapeDtypeStruct((L, D_MODEL), _BF), jax.ShapeDtypeStruct((L, D_FF), _BF)],
        compiler_params=_params("parallel"),
    )(x2, g3, w1)


def _ff2_loss(act, x2, tgt, g4, w2, tm):
    L = act.shape[0]

    def body(f_ref, x_ref, t_ref, g_ref, w_ref, dy_ref, dm_ref, dg_ref, ls_ref):
        @pl.when(pl.program_id(0) == 0)
        def _():
            dg_ref[...] = jnp.zeros_like(dg_ref)
            ls_ref[...] = jnp.zeros_like(ls_ref)

        g = g_ref[...]
        for rows in _row_chunks(tm):
            m = _dot(f_ref[rows, :], w_ref[...])
            y = x_ref[rows, :] + m * _rms_r(m) * g
            err = y - t_ref[rows, :]
            ls_ref[...] += jnp.sum(err * err, axis=0, keepdims=True)
            dy = err * (1.0 / D_MODEL)
            dy_ref[rows, :] = dy
            dm, dgr = _rms_bwd(m, g, dy)
            dm_ref[rows, :] = dm.astype(_BF)
            dg_ref[...] += jnp.sum(dgr, axis=0, keepdims=True)

    return pl.pallas_call(
        body, name="ff2_loss", grid=(L // tm,),
        in_specs=[_row_spec(tm, D_FF), _row_spec(tm, D_MODEL), _row_spec(tm, D_MODEL),
                  _full_spec((1, D_MODEL)), _weight_spec((D_FF, D_MODEL))],
        out_specs=[_row_spec(tm, D_MODEL), _row_spec(tm, D_MODEL), _full_spec((1, D_MODEL)), _full_spec((1, D_MODEL))],
        out_shape=[jax.ShapeDtypeStruct((L, D_MODEL), _F32), jax.ShapeDtypeStruct((L, D_MODEL), _BF),
                   jax.ShapeDtypeStruct((1, D_MODEL), _F32), jax.ShapeDtypeStruct((1, D_MODEL), _F32)],
        compiler_params=_params("arbitrary"),
    )(act, x2, tgt, g4, w2)


def _mlp_fwd_loss(x2, tgt, g3, g4, w1, w2, tm):
    L = x2.shape[0]

    def body(x_ref, t_ref, g3_ref, g4_ref, w1_ref, w2_ref, h_ref, a_ref, dy_ref, dm_ref, dg_ref, ls_ref):
        @pl.when(pl.program_id(0) == 0)
        def _():
            dg_ref[...] = jnp.zeros_like(dg_ref)
            ls_ref[...] = jnp.zeros_like(ls_ref)

        g = g4_ref[...]
        for rows in _row_chunks(tm):
            xv = x_ref[rows, :]
            h = (xv * _rms_r(xv) * g3_ref[...]).astype(_BF)
            h_ref[rows, :] = h
            for j in range(N_DEV):
                cols = slice(j * FF1_COLS, (j + 1) * FF1_COLS)
                rl = jnp.maximum(_dot(h, w1_ref[j]), 0.0)
                a_ref[rows, cols] = (rl * rl).astype(_BF)
            m = _dot(a_ref[rows, :], w2_ref[...])
            y = x_ref[rows, :] + m * _rms_r(m) * g
            err = y - t_ref[rows, :]
            ls_ref[...] += jnp.sum(err * err, axis=0, keepdims=True)
            dy = err * (1.0 / D_MODEL)
            dy_ref[rows, :] = dy
            dm, dgr = _rms_bwd(m, g, dy)
            dm_ref[rows, :] = dm.astype(_BF)
            dg_ref[...] += jnp.sum(dgr, axis=0, keepdims=True)

    vec = _full_spec((1, D_MODEL))
    return pl.pallas_call(
        body, name="mlp_fwd_loss", grid=(L // tm,),
        in_specs=[_row_spec(tm, D_MODEL), _row_spec(tm, D_MODEL), vec, vec,
                  _weight_spec((N_DEV, D_MODEL, FF1_COLS)), _weight_spec((D_FF, D_MODEL))],
        out_specs=[_row_spec(tm, D_MODEL), _row_spec(tm, D_FF), _row_spec(tm, D_MODEL), _row_spec(tm, D_MODEL), vec, vec],
        out_shape=[jax.ShapeDtypeStruct((L, D_MODEL), _BF), jax.ShapeDtypeStruct((L, D_FF), _BF),
                   jax.ShapeDtypeStruct((L, D_MODEL), _F32), jax.ShapeDtypeStruct((L, D_MODEL), _BF),
                   jax.ShapeDtypeStruct((1, D_MODEL), _F32), jax.ShapeDtypeStruct((1, D_MODEL), _F32)],
        compiler_params=_params("arbitrary"),
    )(x2, tgt, g3, g4, w1, w2)


def _ff2_bwd(dm, act, w2, tm, tn):
    L = dm.shape[0]
    last = L // tm - 1

    def body(dm_ref, a_ref, w_ref, df_ref, dw_ref, acc):
        @pl.when(pl.program_id(1) == 0)
        def _():
            acc[...] = jnp.zeros_like(acc)

        dmv = dm_ref[...]
        av = a_ref[...]
        df_ref[...] = (_dot_nt(dmv, w_ref[...]) * jnp.sqrt(4.0 * av.astype(_F32))).astype(_BF)
        acc[...] += _dot_tn(av, dmv)

        @pl.when(pl.program_id(1) == last)
        def _():
            dw_ref[...] = acc[...].astype(_BF)

    return pl.pallas_call(
        body, name="ff2_bwd", grid=(D_FF // tn, L // tm),
        in_specs=[pl.BlockSpec((tm, D_MODEL), lambda j, i: (i, 0)), pl.BlockSpec((tm, tn), lambda j, i: (i, j)),
                  pl.BlockSpec((tn, D_MODEL), lambda j, i: (j, 0))],
        out_specs=[pl.BlockSpec((tm, tn), lambda j, i: (i, j)), pl.BlockSpec((tn, D_MODEL), lambda j, i: (j, 0))],
        out_shape=[jax.ShapeDtypeStruct((L, D_FF), _BF), jax.ShapeDtypeStruct((D_FF, D_MODEL), _BF)],
        scratch_shapes=[pltpu.VMEM((tn, D_MODEL), _F32)],
        compiler_params=_params("parallel", "arbitrary"),
    )(dm, act, w2)


def _ff1_bwd(df1, w1, x2, mix, dy, g3, g2, tm):
    L = df1.shape[0]

    def body(df_ref, w_ref, x2_ref, mix_ref, dy_ref, g3_ref, g2_ref, dx2_ref, dmix_ref, dg3_ref, dg2_ref):
        @pl.when(pl.program_id(0) == 0)
        def _():
            dg3_ref[...] = jnp.zeros_like(dg3_ref)
            dg2_ref[...] = jnp.zeros_like(dg2_ref)

        for rows in _row_chunks(tm):
            dh = _dot_nt(df_ref[rows, 0:FF1_COLS], w_ref[0])
            for j in range(1, N_DEV):
                dh = dh + _dot_nt(df_ref[rows, j * FF1_COLS:(j + 1) * FF1_COLS], w_ref[j])
            dz, dgr = _rms_bwd(x2_ref[rows, :], g3_ref[...], dh)
            dg3_ref[...] += jnp.sum(dgr, axis=0, keepdims=True)
            dx2 = dy_ref[rows, :] + dz
            dx2_ref[rows, :] = dx2
            dmx, dgr2 = _rms_bwd(mix_ref[rows, :].astype(_F32), g2_ref[...], dx2)
            dg2_ref[...] += jnp.sum(dgr2, axis=0, keepdims=True)
            dmix_ref[rows, :] = dmx.astype(_BF)

    vec = _full_spec((1, D_MODEL))
    return pl.pallas_call(
        body, name="ff1_bwd", grid=(L // tm,),
        in_specs=[_row_spec(tm, D_FF), _weight_spec((N_DEV, D_MODEL, FF1_COLS)), _row_spec(tm, D_MODEL),
                  _row_spec(tm, D_MODEL), _row_spec(tm, D_MODEL), vec, vec],
        out_specs=[_row_spec(tm, D_MODEL), _row_spec(tm, D_MODEL), vec, vec],
        out_shape=[jax.ShapeDtypeStruct((L, D_MODEL), _F32), jax.ShapeDtypeStruct((L, D_MODEL), _BF),
                   jax.ShapeDtypeStruct((1, D_MODEL), _F32), jax.ShapeDtypeStruct((1, D_MODEL), _F32)],
        compiler_params=_params("arbitrary"),
    )(df1, w1, x2, mix, dy, g3, g2)


def _matmul_tn(a, b, tm, tn, name, slots=0):
    L, K = a.shape
    N = b.shape[1]
    last = L // tm - 1

    def body(a_ref, b_ref, o_ref, acc):
        @pl.when(pl.program_id(1) == 0)
        def _():
            acc[...] = jnp.zeros_like(acc)

        acc[...] += _dot_tn(a_ref[...].astype(_BF), b_ref[...].astype(_BF))

        @pl.when(pl.program_id(1) == last)
        def _():
            if slots:
                for c in range(tn // slots):
                    o_ref[c] = acc[:, c * slots:(c + 1) * slots].astype(_BF)
            else:
                o_ref[...] = acc[...].astype(_BF)

    if slots:
        out_spec = pl.BlockSpec((tn // slots, K, slots), lambda j, i: (j, 0, 0))
        out_shape = jax.ShapeDtypeStruct((N // slots, K, slots), _BF)
    else:
        out_spec = pl.BlockSpec((K, tn), lambda j, i: (0, j))
        out_shape = jax.ShapeDtypeStruct((K, N), _BF)
    return pl.pallas_call(
        body, name=name, grid=(N // tn, L // tm),
        in_specs=[pl.BlockSpec((tm, K), lambda j, i: (i, 0)), pl.BlockSpec((tm, tn), lambda j, i: (i, j))],
        out_specs=out_spec, out_shape=out_shape,
        scratch_shapes=[pltpu.VMEM((K, tn), _F32)],
        compiler_params=_params("parallel", "arbitrary"),
    )(a, b)


def _dw_out_glu(y_ret, gl, dmix, ys, dglu, tk):
    L = dmix.shape[0]
    last = L // tk - 1

    def body(a0_ref, a1_ref, b_ref, ys_ref, dglu_ref, o_ref, og_ref, acc, acc_g):
        @pl.when(pl.program_id(0) == 0)
        def _():
            acc[...] = jnp.zeros_like(acc)
            acc_g[...] = jnp.zeros_like(acc_g)

        bv = b_ref[...]
        acc[:RET_W, :] += _dot_tn(a0_ref[...], bv)
        acc[RET_W:, :] += _dot_tn(a1_ref[...], bv)
        acc_g[...] += _dot_tn(ys_ref[...], dglu_ref[...])

        @pl.when(pl.program_id(0) == last)
        def _():
            o_ref[...] = acc[...].astype(_BF)
            og_ref[...] = acc_g[...].astype(_BF)

    return pl.pallas_call(
        body, name="dw_out_glu", grid=(L // tk,),
        in_specs=[_row_spec(tk, RET_W), _row_spec(tk, SSM_W), _row_spec(tk, D_MODEL),
                  _row_spec(tk, SSM_W), _row_spec(tk, 2 * SSM_W)],
        out_specs=[_full_spec((D_MODEL, D_MODEL)), _full_spec((SSM_W, 2 * SSM_W))],
        out_shape=[jax.ShapeDtypeStruct((D_MODEL, D_MODEL), _BF), jax.ShapeDtypeStruct((SSM_W, 2 * SSM_W), _BF)],
        scratch_shapes=[pltpu.VMEM((D_MODEL, D_MODEL), _F32), pltpu.VMEM((SSM_W, 2 * SSM_W), _F32)],
        compiler_params=_params("arbitrary"),
    )(y_ret, gl, dmix, ys, dglu)


def _dw_in_t(pieces, h, tk):
    L = h.shape[0]
    last = L // tk - 1

    def body(p0, p1, p2, p3, p4, h_ref, o_ref, acc):
        @pl.when(pl.program_id(0) == 0)
        def _():
            acc[...] = jnp.zeros_like(acc)

        hv = h_ref[...]
        for j, p in enumerate((p0, p1, p2, p3, p4)):
            acc[j * RET_W:(j + 1) * RET_W, :] += _dot_tn(p[...].astype(_BF), hv)

        @pl.when(pl.program_id(0) == last)
        def _():
            o_ref[...] = acc[...].astype(_BF)

    return pl.pallas_call(
        body, name="dw_in", grid=(L // tk,),
        in_specs=[_row_spec(tk, RET_W)] * 5 + [_row_spec(tk, D_MODEL)],
        out_specs=_full_spec((IN_COLS, D_MODEL)), out_shape=jax.ShapeDtypeStruct((IN_COLS, D_MODEL), _BF),
        scratch_shapes=[pltpu.VMEM((IN_COLS, D_MODEL), _F32)],
        compiler_params=_params("arbitrary"),
    )(*pieces, h)


def _mixout_bwd(dmix, w_out, w_glu, ys, s, o, gate, ggn, tm, after=()):
    L = dmix.shape[0]

    def body(dmix_ref, wo_ref, wg_ref, ys_ref, s_ref, o_ref, gate_ref, ggn_ref,
             dglu_ref, ds_ref, dgate_ref, do_ref, dggn_ref):
        @pl.when(pl.program_id(0) == 0)
        def _():
            dggn_ref[...] = jnp.zeros_like(dggn_ref)

        ggn = ggn_ref[...]
        for rows in _row_chunks(tm):
            dcat = _dot_nt(dmix_ref[rows, :], wo_ref[...])
            dy_ret, dy_ssm = dcat[:, :RET_W], dcat[:, RET_W:]
            glu = _dot(ys_ref[rows, :], wg_ref[...])
            ga, sg = glu[:, :SSM_W], _sigmoid(glu[:, SSM_W:])
            dga = (dy_ssm * sg).astype(_BF)
            dgb = (dy_ssm * ga * sg * (1.0 - sg)).astype(_BF)
            dglu_ref[rows, :SSM_W] = dga
            dglu_ref[rows, SSM_W:] = dgb
            dys = _dot_nt(dga, wg_ref[:, :SSM_W]) + _dot_nt(dgb, wg_ref[:, SSM_W:])
            ds_ref[rows, :] = dys * _gelu_grad(s_ref[rows, :])
            gt = gate_ref[rows, :]
            sgt = _sigmoid(gt)
            for hh in range(N_HEAD):
                cols = slice(hh * HEAD_D, (hh + 1) * HEAD_D)
                ov = o_ref[rows, cols]
                dlt = ov - jnp.mean(ov, axis=-1, keepdims=True)
                rstd = lax.rsqrt(jnp.mean(dlt * dlt, axis=-1, keepdims=True) + NORM_EPS)
                on = dlt * rstd
                dyr = dy_ret[:, cols] * (gt[:, cols] * sgt[:, cols])
                dgate_ref[rows, cols] = (dy_ret[:, cols] * (on * ggn[:, cols]) * (sgt[:, cols] * (1.0 + gt[:, cols] * (1.0 - sgt[:, cols])))).astype(_BF)
                dggn_ref[:, cols] += jnp.sum(dyr * on, axis=0, keepdims=True)
                don = dyr * ggn[:, cols]
                do = rstd * (don - jnp.mean(don, axis=-1, keepdims=True) - on * jnp.mean(don * on, axis=-1, keepdims=True))
                do_ref[rows, cols] = do.astype(_BF)

    body, in_specs, operands = _ordered(
        body, [_row_spec(tm, D_MODEL), _weight_spec((D_MODEL, D_MODEL)), _weight_spec((SSM_W, 2 * SSM_W)),
               _row_spec(tm, SSM_W), _row_spec(tm, SSM_W), _row_spec(tm, RET_W), _row_spec(tm, RET_W),
               _full_spec((1, RET_W))], (dmix, w_out, w_glu, ys, s, o, gate, ggn), after)
    return pl.pallas_call(
        body, name="mixout_bwd", grid=(L // tm,),
        in_specs=in_specs,
        out_specs=[_row_spec(tm, 2 * SSM_W), _row_spec(tm, SSM_W), _row_spec(tm, RET_W), _row_spec(tm, RET_W),
                   _full_spec((1, RET_W))],
        out_shape=[jax.ShapeDtypeStruct((L, 2 * SSM_W), _BF), jax.ShapeDtypeStruct((L, SSM_W), _F32),
                   jax.ShapeDtypeStruct((L, RET_W), _BF), jax.ShapeDtypeStruct((L, RET_W), _BF),
                   jax.ShapeDtypeStruct((1, RET_W), _F32)],
        compiler_params=_params("arbitrary"),
    )(*operands)


def _s5_bwd(u, ds, xs, ent, bmat, cmat, tab_r, pw_r, d_skip, tb, after=()):
    L = u.shape[0]
    nt = L // tb
    seg = tb // SUBLANES
    G = KB_PER_STEP
    rcol = pl.BlockSpec((tb, G * LANES), lambda kb, t: (nt - 1 - t, kb))
    sp = _s5_specs(seg, time=lambda t: nt - 1 - t)
    aspec = pl.BlockSpec((G, SUBLANES, 2 * KB_STATES), lambda kb, t: (kb, 0, 0))

    def body(u_ref, ds_ref, x_ref, ent_ref, b_ref, c_ref, tr_ref, pr_ref, d_ref,
             du_ref, db_ref, dc_ref, da_ref, dd_ref, up_scr, dp_scr, g_scr, lc_scr):
        @pl.when(pl.program_id(1) == 0)
        def _():
            lc_scr[...] = jnp.zeros_like(lc_scr)
            db_ref[...] = jnp.zeros_like(db_ref)
            dc_ref[...] = jnp.zeros_like(dc_ref)
            da_ref[...] = jnp.zeros_like(da_ref)
            dd_ref[...] = jnp.zeros_like(dd_ref)

        _rows_to_segments(up_scr, u_ref, seg)
        _rows_to_segments(dp_scr, ds_ref, seg)
        for g in range(G):
            g_scr[g] = _dot_nt(dp_scr[g].astype(_BF), c_ref[g]).reshape(seg, SUBLANES, 2 * KB_STATES)
        _scan_segments(g_scr, tr_ref, pr_ref, lc_scr, seg, reverse=True, fwd_ref=x_ref, fwd_entry_ref=ent_ref.at[:, 0],
                       da_ref=da_ref)
        for g in range(G):
            cols = slice(g * LANES, (g + 1) * LANES)
            uv, dsv = up_scr[g], dp_scr[g]
            ub, dsb = uv.astype(_BF), dsv.astype(_BF)
            lamb = g_scr[g].reshape(tb, 2 * KB_STATES).astype(_BF)
            db_ref[g] += _dot_tn(ub, lamb)
            dc_ref[g] += _dot_tn(dsb, x_ref[g].reshape(tb, 2 * KB_STATES).astype(_BF))
            dd_ref[:, cols] += jnp.sum(dsv * uv, axis=0, keepdims=True)
            up_scr[g] = _dot_nt(lamb, b_ref[g]) + d_ref[:, cols] * dsv
        _segments_to_rows(du_ref, up_scr, seg)

    body, in_specs, operands = _ordered(
        body, [rcol, rcol, sp["x"], sp["ent"], sp["b"], sp["c"], sp["tab"], sp["pw"], sp["d"]],
        (u, ds, xs, ent, bmat, cmat, tab_r, pw_r, d_skip), after)
    return pl.pallas_call(
        body, name="s5_bwd", grid=(N_KB // G, nt),
        in_specs=in_specs,
        out_specs=[rcol, sp["b"], sp["b"], aspec, sp["d"]],
        out_shape=[jax.ShapeDtypeStruct((L, SSM_W), _BF),
                   jax.ShapeDtypeStruct((N_KB, LANES, 2 * KB_STATES), _F32),
                   jax.ShapeDtypeStruct((N_KB, LANES, 2 * KB_STATES), _F32),
                   jax.ShapeDtypeStruct((N_KB, SUBLANES, 2 * KB_STATES), _F32),
                   jax.ShapeDtypeStruct((1, SSM_W), _F32)],
        scratch_shapes=[pltpu.VMEM((G, tb, LANES), _F32)] * 2
        + [pltpu.VMEM((G, seg, SUBLANES, 2 * KB_STATES), _F32), pltpu.VMEM((G, SUBLANES, 2 * KB_STATES), _F32)],
        compiler_params=_params("parallel", "arbitrary"),
    )(*operands)


def _retention_bwd(q, k, v, do, r_prev, consts, cosf, sinf, after=()):
    L = q.shape[0]
    nc = L // CHUNK
    cps = math.gcd(RET_STEP_CHUNKS, nc)
    nb = nc // cps
    blk = pl.BlockSpec((cps * CHUNK, RET_W), lambda n: (nb - 1 - n, 0))
    rope_blk = pl.BlockSpec((cps * CHUNK, HEAD_D), lambda n: (nb - 1 - n, 0))

    def body(q_ref, k_ref, v_ref, do_ref, rp_ref, dm_ref, xi_ref, zeta_ref, gc_ref, cos_ref, sin_ref,
             dq_ref, dk_ref, dv_ref, g_scr):
        @pl.when(pl.program_id(0) == 0)
        def _():
            g_scr[...] = jnp.zeros_like(g_scr)

        for hh in range(N_HEAD):
            cols = slice(hh * HEAD_D, (hh + 1) * HEAD_D)
            dm, zeta = dm_ref[hh], zeta_ref[hh]
            gst = g_scr[hh]
            for c in reversed(range(cps)):
                rows = slice(c * CHUNK, (c + 1) * CHUNK)
                qv, kv, vv, dov = q_ref[rows, cols], k_ref[rows, cols], v_ref[rows, cols], do_ref[rows, cols]
                rb = rp_ref[hh, c].astype(_BF)
                gb = gst.astype(_BF)
                sb = (_dot_nt(qv, kv) * dm).astype(_BF)
                dab = (_dot_nt(dov, vv) * dm).astype(_BF)
                dox = (dov.astype(_F32) * xi_ref[hh]).astype(_BF)
                vz = (vv.astype(_F32) * zeta).astype(_BF)
                dq = _dot(dab, kv) + _dot_nt(dox, rb)
                dk = _dot_tn(dab, qv) + _dot_nt(vz, gb)
                dv = _dot_tn(sb, dov) + _dot(kv, gb) * zeta
                gst = gc_ref[hh, 0:1, :] * gst + _dot_tn(qv, dox)
                cs, sn = cos_ref[rows, :], sin_ref[rows, :]
                dq_ref[rows, cols] = _rope_t(dq, cs, sn).astype(_BF)
                dk_ref[rows, cols] = (_rope_t(dk, cs, sn) * (HEAD_D ** -0.5)).astype(_BF)
                dv_ref[rows, cols] = dv.astype(_BF)
            g_scr[hh] = gst

    body, in_specs, operands = _ordered(
        body, [blk, blk, blk, blk, pl.BlockSpec((N_HEAD, cps, HEAD_D, HEAD_D), lambda n: (0, nb - 1 - n, 0, 0))]
        + _head_specs() + [rope_blk, rope_blk], (q, k, v, do, r_prev, *consts, cosf, sinf), after)
    return pl.pallas_call(
        body, name="retention_bwd", grid=(nb,),
        in_specs=in_specs,
        out_specs=[blk, blk, blk],
        out_shape=[jax.ShapeDtypeStruct((L, RET_W), _BF)] * 3,
        scratch_shapes=[pltpu.VMEM((N_HEAD, HEAD_D, HEAD_D), _F32)],
        compiler_params=_params("arbitrary"),
    )(*operands)


def _inproj_bwd(pieces, w_in_t, x, dx2, g1, tm, after=()):
    L = x.shape[0]

    def body(p0, p1, p2, p3, p4, w_ref, x_ref, dx2_ref, g_ref, dx_ref, dg_ref):
        @pl.when(pl.program_id(0) == 0)
        def _():
            dg_ref[...] = jnp.zeros_like(dg_ref)

        for rows in _row_chunks(tm):
            dh = None
            for j, p in enumerate((p0, p1, p2, p3, p4)):
                part = _dot(p[rows, :].astype(_BF), w_ref[j * RET_W:(j + 1) * RET_W, :])
                dh = part if dh is None else dh + part
            dz, dgr = _rms_bwd(x_ref[rows, :], g_ref[...], dh)
            dx_ref[rows, :] = dx2_ref[rows, :] + dz
            dg_ref[...] += jnp.sum(dgr, axis=0, keepdims=True)

    body, in_specs, operands = _ordered(
        body, [_row_spec(tm, RET_W)] * 5 + [_weight_spec((IN_COLS, D_MODEL)), _row_spec(tm, D_MODEL),
                                             _row_spec(tm, D_MODEL), _full_spec((1, D_MODEL))],
        (*pieces, w_in_t, x, dx2, g1), after)
    return pl.pallas_call(
        body, name="inproj_bwd", grid=(L // tm,),
        in_specs=in_specs,
        out_specs=[_row_spec(tm, D_MODEL), _full_spec((1, D_MODEL))],
        out_shape=[jax.ShapeDtypeStruct((L, D_MODEL), _F32), jax.ShapeDtypeStruct((1, D_MODEL), _F32)],
        compiler_params=_params("arbitrary"),
    )(*operands)


def _sum_adamw(parts, w, m, v, tr, name):
    _, R, Cc = parts.shape

    def body(p_ref, w_ref, m_ref, v_ref, g_ref, d_ref, nm_ref, nv_ref):
        gv = p_ref[0].astype(_F32)
        for s in range(1, N_DEV):
            gv = gv + p_ref[s].astype(_F32)
        g_ref[...] = gv
        nm = ADAM_B1 * m_ref[...] + (1.0 - ADAM_B1) * gv
        nv = ADAM_B2 * v_ref[...] + (1.0 - ADAM_B2) * (gv * gv)
        m_hat = nm / (1.0 - ADAM_B1 ** ADAM_STEP)
        v_hat = nv / (1.0 - ADAM_B2 ** ADAM_STEP)
        d_ref[...] = -ADAM_LR * (m_hat / (jnp.sqrt(v_hat) + ADAM_EPS) + ADAM_WD * w_ref[...])
        nm_ref[...] = nm
        nv_ref[...] = nv

    spec = _row_spec(tr, Cc)
    return pl.pallas_call(
        body, name=name, grid=(R // tr,),
        in_specs=[pl.BlockSpec((N_DEV, tr, Cc), lambda i: (0, i, 0))] + [spec] * 3, out_specs=[spec] * 4,
        out_shape=[jax.ShapeDtypeStruct((R, Cc), _F32)] * 4,
        compiler_params=_params("parallel"),
    )(parts, w, m, v)


def _my_place():
    return lax.axis_index("x"), lax.axis_index("y"), lax.axis_index("c")


HBM_SPEC = pl.BlockSpec(memory_space=pltpu.HBM)
SEM_SPEC = pl.BlockSpec(memory_space=pltpu.SEMAPHORE)
DATAFLOW = pltpu.SideEffectType.DATAFLOW_SIDE_EFFECTING


def _my_index():
    x, y, c = _my_place()
    return 4 * x + 2 * y + c


def _landing(own_block):
    return lax.empty((N_DEV,) + own_block.shape, own_block.dtype)


def _own_copy(src, land, sems, a, gather):
    me = _my_index()
    return pltpu.make_async_copy(src if gather else src.at[me], land.at[me], sems.at[sems.shape[0] // N_DEV * 7 + a])


def _split_copies(src_refs, land_refs, send_sems, recv_sems, gather, first=0):
    x, y, c = _my_place()
    me = 4 * x + 2 * y + c
    copies = []
    for a, (src, land) in enumerate(zip(src_refs, land_refs)):
        for kk in range(1, N_DEV):
            px, py, pc = x ^ (kk >> 2), y ^ ((kk >> 1) & 1), c ^ (kk & 1)
            peer = 4 * px + 2 * py + pc
            copies.append(pltpu.make_async_remote_copy(
                src_ref=src if gather else src.at[peer], dst_ref=land.at[me],
                send_sem=send_sems.at[(first + a) * 7 + kk - 1], recv_sem=recv_sems.at[(first + a) * 7 + kk - 1],
                device_id=(px, py, pc), device_id_type=MESH))
    return copies


def _split_start(srcs, lands, gather, name):
    n = len(srcs)

    def body(*refs):
        src_refs, land_refs = refs[:n], refs[n:2 * n]
        send_sems, recv_sems = refs[2 * n], refs[2 * n + 1]
        token = refs[-1]
        for cp in _split_copies(src_refs, land_refs, send_sems, recv_sems, gather):
            cp.start()
        for a in range(n):
            _own_copy(src_refs[a], land_refs[a], send_sems, a, gather).start()
        token[...] = jnp.zeros_like(token)

    outs = pl.pallas_call(
        body, name=name,
        out_shape=(pltpu.SemaphoreType.DMA((N_DEV * n,)), pltpu.SemaphoreType.DMA((7 * n,)),
                   *[pltpu.HBM(t.shape, t.dtype) for t in srcs], *[pltpu.HBM(t.shape, t.dtype) for t in lands],
                   jax.ShapeDtypeStruct((SUBLANES, LANES), _F32)),
        in_specs=[HBM_SPEC] * (2 * n),
        out_specs=(SEM_SPEC, SEM_SPEC, *[HBM_SPEC] * (2 * n), pl.BlockSpec(memory_space=pltpu.VMEM)),
        input_output_aliases={i: 2 + i for i in range(2 * n)},
        compiler_params=pltpu.CompilerParams(has_side_effects=DATAFLOW),
    )(*[pltpu.with_memory_space_constraint(t, pltpu.HBM) for t in list(srcs) + list(lands)])
    return outs[0], outs[1], outs[2:2 + n], outs[2 + n:2 + 2 * n], outs[-1]


def _split_wait(send_sems, recv_sems, srcs, lands, after, gather, name, first=0):
    n = len(srcs)

    def body(*refs):
        src_refs, land_refs = refs[:n], refs[n:2 * n]
        send_s, recv_s = refs[2 * n], refs[2 * n + 1]
        for cp in _split_copies(src_refs, land_refs, send_s, recv_s, gather, first):
            cp.wait_send()
            cp.wait_recv()
        for a in range(n):
            _own_copy(src_refs[a], land_refs[a], send_s, first + a, gather).wait()

    outs = pl.pallas_call(
        body, name=name,
        out_shape=tuple(pltpu.HBM(t.shape, t.dtype) for t in list(srcs) + list(lands)),
        in_specs=[HBM_SPEC] * (2 * n) + [SEM_SPEC, SEM_SPEC, pl.BlockSpec(memory_space=pl.ANY)],
        out_specs=tuple([HBM_SPEC] * (2 * n)),
        input_output_aliases={i: i for i in range(2 * n)},
        compiler_params=pltpu.CompilerParams(has_side_effects=DATAFLOW),
    )(*srcs, *lands, send_sems, recv_sems, after)
    return outs[n:]


def _discretize(lam_re, lam_im, log_dt, b_re, b_im):
    lr = jnp.minimum(lam_re, -1e-4)
    li = lam_im
    dt = jnp.exp(log_dt)[:, None]
    er = jnp.exp(lr * dt)
    ar, ai = er * jnp.cos(li * dt), er * jnp.sin(li * dt)
    den = lr * lr + li * li
    cr = ((ar - 1.0) * lr + ai * li) / den
    ci = (ai * lr - (ar - 1.0) * li) / den
    bbr = cr[:, :, None] * b_re - ci[:, :, None] * b_im
    bbi = cr[:, :, None] * b_im + ci[:, :, None] * b_re
    return ar, ai, bbr, bbi


def _cmul(ar, ai, br, bi):
    return ar * br - ai * bi, ar * bi + ai * br


def _cpowers(ar, ai, n):
    pr, pi = ar[None], ai[None]
    while pr.shape[0] < n:
        nr, ni = _cmul(pr, pi, pr[-1][None], pi[-1][None])
        pr, pi = jnp.concatenate([pr, nr]), jnp.concatenate([pi, ni])
    return pr[:n], pi[:n]


def _scan_tables(ar, ai, seg, reverse):
    if reverse:
        ai = -ai
    ar, ai = ar.reshape(N_KB, KB_STATES), ai.reshape(N_KB, KB_STATES)
    pr, pi = _cpowers(ar, ai, seg)
    a1 = (pr[-1], pi[-1])
    a2 = _cmul(*a1, *a1)
    a4 = _cmul(*a2, *a2)
    row = jnp.arange(SUBLANES)[None, :, None]
    wide = lambda t: jnp.broadcast_to(t[:, None, :], (N_KB, SUBLANES, KB_STATES))
    tabs = [wide(ar), wide(ai)]
    for dist, (qr, qi) in ((1, a1), (2, a2), (4, a4)):
        keep = (row < SUBLANES - dist) if reverse else (row >= dist)
        tabs += [jnp.where(keep, wide(qr), 0.0), jnp.where(keep, wide(qi), 0.0)]
    tabs += [wide(a1[0]), wide(a1[1])]
    if reverse:
        pr, pi = pr[::-1], pi[::-1]
    pw = jnp.transpose(jnp.concatenate([pr, pi], axis=-1), (1, 0, 2))[:, :, None, :]
    return jnp.stack(tabs, axis=1).astype(_F32), pw.astype(_F32)


def _block_diag_in(br, bi):
    eye = jnp.eye(GROUPS_PER_KB, dtype=_F32)
    one = lambda t: jnp.einsum("kgpc,gh->kgchp", t.reshape(N_KB, GROUPS_PER_KB, N_STATE, SSM_GC), eye).reshape(
        N_KB, LANES, KB_STATES)
    return jnp.concatenate([one(br), one(bi)], axis=-1)


def _block_diag_in_t(dmat):
    d6 = dmat.reshape(N_KB, GROUPS_PER_KB, SSM_GC, 2, GROUPS_PER_KB, N_STATE)
    eye = jnp.eye(GROUPS_PER_KB, dtype=_F32)
    both = jnp.einsum("kgcrhp,gh->rkgpc", d6, eye).reshape(2, N_GROUP, N_STATE, SSM_GC)
    return both[0], both[1]


def _block_diag_out(c_re, c_im):
    eye = jnp.eye(GROUPS_PER_KB, dtype=_F32)
    one = lambda t: jnp.einsum("kgcp,gh->khpgc", t.reshape(N_KB, GROUPS_PER_KB, SSM_GC, N_STATE), eye).reshape(
        N_KB, KB_STATES, LANES)
    return jnp.concatenate([one(c_re), -one(c_im)], axis=1)


def _block_diag_out_t(dmat_t):
    d6 = dmat_t.reshape(N_KB, GROUPS_PER_KB, SSM_GC, 2, GROUPS_PER_KB, N_STATE)
    eye = jnp.eye(GROUPS_PER_KB, dtype=_F32)
    both = jnp.einsum("kgcrhp,gh->rkgcp", d6, eye).reshape(2, N_GROUP, SSM_GC, N_STATE)
    return both[0], -both[1]


SMALL_NAMES = ("norm_mix_pre", "norm_mix_post", "ret_gn_gain", "ssm_lambda_re", "ssm_lambda_im", "ssm_log_dt",
               "ssm_b_re", "ssm_b_im", "ssm_c_re", "ssm_c_im", "ssm_d", "norm_mlp_pre", "norm_mlp_post")


def _local_grads(x, tgt, small, weights, emit, emit_small, tm, tk, tb, zero=0.0):
    L = x.shape[0]
    g1, g2, ggn = small["norm_mix_pre"], small["norm_mix_post"], small["ret_gn_gain"]
    g3, g4, d_skip = small["norm_mlp_pre"], small["norm_mlp_post"], small["ssm_d"]

    rope = _rope_tables(L)
    consts = _ret_consts()

    disc_in = (small["ssm_lambda_re"][0], small["ssm_lambda_im"][0], small["ssm_log_dt"][0] + zero,
               small["ssm_b_re"][0], small["ssm_b_im"][0])
    (ar, ai, bbr, bbi), disc_vjp = jax.vjp(_discretize, *disc_in)
    bmat = _block_diag_in(bbr, bbi).astype(_BF)
    cmat = _block_diag_out(small["ssm_c_re"][0], small["ssm_c_im"][0]).astype(_BF)
    seg = tb // SUBLANES
    tab_f, pw_f = _scan_tables(ar, ai, seg, False)
    tab_r, pw_r = _scan_tables(ar, ai, seg, True)

    h1 = _prenorm(x, g1, min(4 * tm, L), after=(pw_r,))
    (w_in_t,) = weights("in", h1)
    q, k, v, gate, u, cosf, sinf = _inproj_fwd(h1, w_in_t, rope, min(4 * tm, L))
    o, y_ret, r_prev = _retention_fwd(q, k, v, gate, ggn, consts)
    s, xs, ent = _s5_fwd(u, bmat, cmat, tab_f, pw_f, d_skip, tb)
    w_glu, w_out = weights("mix", s)
    ys, gl, mix, x2 = _mixout_fwd(s, y_ret, x, w_glu, w_out, g2, min(2 * tm, L))
    w_ff1, w_ff2 = weights("mlp", x2)
    h3, act, dy, dm, dg4, sq = _mlp_fwd_loss(x2, tgt, g3, g4, w_ff1, w_ff2, min(2 * tm, L))

    df1, dw_ff2 = _ff2_bwd(dm, act, w_ff2, min(1024, L), 1024)
    dx2, dmix, dg3, dg2 = _ff1_bwd(df1, w_ff1, x2, mix, dy, g3, g2, min(2 * tm, L))
    dw_ff1 = _matmul_tn(h3, df1, tk, 2 * FF1_COLS, "dw_ff1", slots=FF1_COLS)
    token = emit({"w_ff1": dw_ff1, "w_ff2": dw_ff2})
    dglu, ds, dgate, do, dggn = _mixout_bwd(dmix, w_out, w_glu, ys, s, o, gate, ggn, min(2 * tm, L), after=token)
    dw_out, dw_glu = _dw_out_glu(y_ret, gl, dmix, ys, dglu, tk)
    token = emit({"w_glu": dw_glu, "w_out": dw_out})
    du, dbmat, dcmat, da8, dd = _s5_bwd(u, ds, xs, ent, bmat, cmat, tab_r, pw_r, d_skip, tb, after=token)

    da = jnp.sum(da8, axis=1)
    dar = da[:, :KB_STATES].reshape(N_GROUP, N_STATE)
    dai = da[:, KB_STATES:].reshape(N_GROUP, N_STATE)
    dbr, dbi = _block_diag_in_t(dbmat)
    dlre, dlim, dldt, dbre, dbim = disc_vjp((dar, dai, dbr, dbi))
    dcre, dcim = _block_diag_out_t(dcmat)
    token = emit_small({
        "norm_mix_post": dg2, "ret_gn_gain": dggn,
        "ssm_lambda_re": dlre[None], "ssm_lambda_im": dlim[None], "ssm_log_dt": dldt[None],
        "ssm_b_re": dbre[None], "ssm_b_im": dbim[None], "ssm_c_re": dcre[None], "ssm_c_im": dcim[None],
        "ssm_d": dd, "norm_mlp_pre": dg3, "norm_mlp_post": dg4,
    }, sq)

    dq, dk, dv = _retention_bwd(q, k, v, do, r_prev, consts, cosf, sinf, after=token)
    pieces = (dq, dk, dv, dgate, du)
    dw_in_t = _dw_in_t(pieces, h1, min(1024, L))
    token = emit({"w_in": dw_in_t})
    gx, dg1 = _inproj_bwd(pieces, w_in_t, x, dx2, g1, min(2 * tm, L), after=token)
    return gx, dg1


BIG_SHAPES = {"w_in": (D_MODEL, IN_COLS // N_DEV), "w_glu": (SSM_W, 2 * SSM_W // N_DEV), "w_out": (D_MODEL // N_DEV, D_MODEL),
              "w_ff1": (D_MODEL, FF1_COLS), "w_ff2": (D_FF // N_DEV, D_MODEL)}
BIG_NAMES = ("w_in", "w_glu", "w_out", "w_ff1", "w_ff2")


def _cols_from_slots(g):
    return jnp.transpose(g, (1, 0, 2)).reshape(g.shape[1], N_DEV * g.shape[2])


def _cols_to_slots(dw):
    r, cols = dw.shape
    return jnp.transpose(dw.reshape(r, N_DEV, cols // N_DEV), (1, 0, 2))


WEIGHT_GROUPS = {"in": ("w_in",), "mix": ("w_glu", "w_out"), "mlp": ("w_ff1", "w_ff2")}


def _weight_from_slots(name, g):
    if name == "w_glu":
        return _cols_from_slots(g)
    if name == "w_ff1":
        return g
    return g.reshape(N_DEV * g.shape[1], g.shape[2])


def _grad_slots(name, dw):
    if name == "w_glu":
        return _cols_to_slots(dw)
    if name == "w_ff1":
        return dw
    if name == "w_in":
        return dw.reshape(N_DEV, BIG_SHAPES[name][1], BIG_SHAPES[name][0])
    return dw.reshape((N_DEV,) + BIG_SHAPES[name])


PIECE_ROWS = 8


VEC_NAMES = tuple(n for n in SMALL_NAMES if n[:6] not in ("ssm_b_", "ssm_c_"))
BC_NAMES = ("ssm_b_re", "ssm_b_im", "ssm_c_re", "ssm_c_im")
BC_ROWS = N_GROUP * SSM_GC


def _bc_view(name, t):
    t = t[0]
    if name.startswith("ssm_b_"):
        t = jnp.swapaxes(t, 1, 2)
    return t.reshape(BC_ROWS, N_STATE)


def _bc_unview(name, t):
    t = t.reshape(N_GROUP, SSM_GC, N_STATE)
    if name.startswith("ssm_b_"):
        t = jnp.swapaxes(t, 1, 2)
    return t[None]


def _pack_bc(vals):
    return jnp.concatenate([_bc_view(n, vals[n]).astype(_F32) for n in BC_NAMES], axis=0)


def _unpack_bc(buf):
    return {n: _bc_unview(n, buf[j * BC_ROWS:(j + 1) * BC_ROWS]) for j, n in enumerate(BC_NAMES)}


def _small_layout(shapes):
    off, rows = {}, 0
    for n in VEC_NAMES:
        off[n] = rows
        rows += -(-math.prod(shapes[n]) // (PIECE_ROWS * LANES)) * PIECE_ROWS
    return off, rows, rows + PIECE_ROWS


def _pack_small(vals, shapes, last=None):
    parts = []
    for n in VEC_NAMES:
        flat = vals[n].reshape(-1).astype(_F32)
        pad = -flat.shape[0] % (PIECE_ROWS * LANES)
        if pad:
            flat = jnp.concatenate([flat, jnp.zeros((pad,), _F32)])
        parts.append(flat.reshape(-1, LANES))
    parts.append(jnp.zeros((PIECE_ROWS, LANES), _F32) if last is None else last)
    return jnp.concatenate(parts, axis=0)


def _unpack_small(buf, shapes):
    off, _, _ = _small_layout(shapes)
    out = {}
    for n in VEC_NAMES:
        size = math.prod(shapes[n])
        rows = -(-size // LANES)
        out[n] = buf[off[n]:off[n] + rows].reshape(-1)[:size].reshape(shapes[n])
    return out


WEIGHT_NAMES = ('norm_mix_pre', 'norm_mix_post', 'w_in', 'ret_gn_gain', 'ssm_lambda_re', 'ssm_lambda_im', 'ssm_log_dt',
                'ssm_b_re', 'ssm_b_im', 'ssm_c_re', 'ssm_c_im', 'ssm_d', 'w_glu', 'w_out', 'norm_mlp_pre',
                'norm_mlp_post', 'w_ff1', 'w_ff2')


def kernel(x, norm_mix_pre, norm_mix_post, w_in, ret_gn_gain, ssm_lambda_re, ssm_lambda_im, ssm_log_dt, ssm_b_re, ssm_b_im, ssm_c_re, ssm_c_im, ssm_d, w_glu, w_out, norm_mlp_pre, norm_mlp_post, w_ff1, w_ff2, loss_target, m_norm_mix_pre, m_norm_mix_post, m_w_in, m_ret_gn_gain, m_ssm_lambda_re, m_ssm_lambda_im, m_ssm_log_dt, m_ssm_b_re, m_ssm_b_im, m_ssm_c_re, m_ssm_c_im, m_ssm_d, m_w_glu, m_w_out, m_norm_mlp_pre, m_norm_mlp_post, m_w_ff1, m_w_ff2, v_norm_mix_pre, v_norm_mix_post, v_w_in, v_ret_gn_gain, v_ssm_lambda_re, v_ssm_lambda_im, v_ssm_log_dt, v_ssm_b_re, v_ssm_b_im, v_ssm_c_re, v_ssm_c_im, v_ssm_d, v_w_glu, v_w_out, v_norm_mlp_pre, v_norm_mlp_post, v_w_ff1, v_w_ff2):
    args = dict(locals())
    w = {n: args[n] for n in WEIGHT_NAMES}
    m = {n: args["m_" + n] for n in WEIGHT_NAMES}
    v = {n: args["v_" + n] for n in WEIGHT_NAMES}
    L = x.shape[1]
    tm = min(256, L)
    tk = min(2048, L)
    tb = min(1024, L)

    calls = {"in": ("w_in",), "rest": WEIGHT_GROUPS["mix"] + WEIGHT_GROUPS["mlp"]}
    started, zero = {}, jnp.zeros((), _F32)
    for call, names in calls.items():
        blocks = [(w[n][0].T if n == "w_in" else w[n][0]).astype(_BF) for n in names]
        blocks[0] = blocks[0] + zero.astype(_BF)
        started[call] = _split_start(blocks, [_landing(b) for b in blocks], True, "weights_start_" + call)
        zero = started[call][4][0, 0]

    def weights(group, after):
        names = WEIGHT_GROUPS[group]
        call = "in" if group == "in" else "rest"
        first = calls[call].index(names[0])
        part = slice(first, first + len(names))
        got = started[call]
        landed = _split_wait(got[0], got[1], got[2][part], got[3][part], after, True, "weights_wait_" + group, first=first)
        return [_weight_from_slots(n, g) for n, g in zip(names, landed)]

    in_flight = []

    def emit(dws):
        names = sorted(dws)
        srcs = [_grad_slots(n, dws[n]) for n in names]
        lands = [_landing(t[0]) for t in srcs]
        started = _split_start(srcs, lands, False, "grads_start_" + "_".join(names))
        in_flight.append((names, started))
        return (started[4],)

    shapes = {n: w[n].shape for n in SMALL_NAMES}
    first_piece = {SMALL_NAMES[0]: jnp.zeros(shapes[SMALL_NAMES[0]], _F32)}
    small_flight = []

    def emit_small(gs, sq):
        loss_rows = jnp.broadcast_to(0.5 / D_MODEL * jnp.sum(sq), (PIECE_ROWS, LANES)).astype(_F32)
        bufs = [_pack_small({**first_piece, **gs}, shapes, loss_rows), _pack_bc(gs)]
        small_flight.append(_split_start(bufs, [_landing(b) for b in bufs], True, "small_grads_start"))
        return (small_flight[0][4],)

    small_w = {n: w[n] for n in SMALL_NAMES}
    gx, dg1 = _local_grads(x[0], loss_target[0], small_w, weights, emit, emit_small, tm, tk, tb, zero=zero)
    last_buf = dg1.reshape(PIECE_ROWS, LANES)
    last_started = _split_start([last_buf], [_landing(last_buf)], True, "last_grad_start")

    grads, delta, new_m, new_v = {}, {}, {}, {}
    after = last_started[4]
    for names, started in in_flight:
        landed = _split_wait(*started[:4], after, False, "grads_wait_" + "_".join(names))
        for n, parts in zip(names, landed):
            flip = (lambda t: t.T) if n == "w_in" else (lambda t: t)
            res = _sum_adamw(parts, flip(w[n][0]), flip(m[n][0]), flip(v[n][0]), math.gcd(256, parts.shape[1]), "adamw_" + n)
            grads[n], delta[n], new_m[n], new_v[n] = (flip(t)[None] for t in res)
        after = res[1]
    small_parts, bc_parts = _split_wait(*small_flight[0][:4], after, True, "small_grads_wait")
    last_parts = _split_wait(*last_started[:4], small_parts, True, "last_grad_wait")[0]
    small_parts = lax.dynamic_update_slice(small_parts, last_parts, (0, 0, 0))
    res_bc = _sum_adamw(bc_parts, _pack_bc(w), _pack_bc(m), _pack_bc(v), BC_ROWS, "adamw_bc")
    sw, sm, sv = _pack_small(w, shapes), _pack_small(m, shapes), _pack_small(v, shapes)
    res = _sum_adamw(small_parts, sw, sm, sv, sw.shape[0], "adamw_small")
    for dst, buf, buf_bc in zip((grads, delta, new_m, new_v), res, res_bc):
        dst.update(_unpack_small(buf, shapes))
        dst.update(_unpack_bc(buf_bc))
    _, loss_at, _ = _small_layout(shapes)
    loss = res[0][loss_at, 0]

    return (loss, gx[None], *[grads[n] for n in WEIGHT_NAMES], *[delta[n] for n in WEIGHT_NAMES],
            *[new_m[n] for n in WEIGHT_NAMES], *[new_v[n] for n in WEIGHT_NAMES])
```

```python
import math

import jax
import jax.numpy as jnp
from jax import lax
from jax.experimental import pallas as pl
from jax.experimental.pallas import tpu as pltpu

_BF = jnp.bfloat16
_F32 = jnp.float32

D_MODEL = 1024
RET_W = 512
N_HEAD = 4
HEAD_D = 128
CHUNK = 256
ROPE_CHUNK = 128
SSM_W = 512
SSM_GC = 16
N_GROUP = 32
N_STATE = 64
GROUPS_PER_KB = 8
N_KB = 4
KB_STATES = GROUPS_PER_KB * N_STATE
D_FF = 4096
IN_COLS = 2560
NORM_EPS = 1e-6
ROPE_BASE = 10000.0
N_DEV = 8

ADAM_LR = 0.001
ADAM_B1 = 0.9
ADAM_B2 = 0.999
ADAM_EPS = 1e-08
ADAM_WD = 0.01
ADAM_STEP = 10

SUBLANES = 8
LANES = 128
VMEM_LIMIT = 52 * 1024 * 1024
RET_STEP_CHUNKS = 2
KB_PER_STEP = 2
SCAN_UNROLL = True
FIX_UNROLL = 8

MESH = pl.DeviceIdType.MESH


def _params(*sem):
    return pltpu.CompilerParams(dimension_semantics=sem, vmem_limit_bytes=VMEM_LIMIT)


def _dot(a, b):
    return jnp.dot(a, b, preferred_element_type=_F32)


def _dot_nt(a, b):
    return lax.dot_general(a, b, (((1,), (1,)), ((), ())), preferred_element_type=_F32)


def _dot_tn(a, b):
    return lax.dot_general(a, b, (((0,), (0,)), ((), ())), preferred_element_type=_F32)


def _rms_r(z):
    return lax.rsqrt(jnp.mean(z * z, axis=-1, keepdims=True) + NORM_EPS)


def _rms_bwd(z, g, dn):
    r = _rms_r(z)
    t = dn * g
    dz = r * t - z * (r * r * r * jnp.mean(t * z, axis=-1, keepdims=True))
    return dz, dn * z * r


def _rope(t, cs, sn):
    return t * cs + pltpu.roll(t, HEAD_D // 2, 1) * sn


def _rope_t(t, cs, sn):
    return t * cs - pltpu.roll(t, HEAD_D // 2, 1) * sn


def _sigmoid(z):
    return 1.0 / (1.0 + jnp.exp(-z))


_GELU_C = math.sqrt(2.0 / math.pi)


def _gelu(z):
    return 0.5 * z * (1.0 + jnp.tanh(_GELU_C * (z + 0.044715 * z * z * z)))


def _gelu_grad(z):
    th = jnp.tanh(_GELU_C * (z + 0.044715 * z * z * z))
    return 0.5 * (1.0 + th) + 0.5 * z * (1.0 - th * th) * _GELU_C * (1.0 + 3 * 0.044715 * z * z)


ROW_CHUNK = 256


def _row_chunks(tm):
    return [pl.ds(i, min(ROW_CHUNK, tm)) for i in range(0, tm, ROW_CHUNK)]


def _ordered(body, in_specs, operands, after):
    k = len(after)
    if not k:
        return body, list(in_specs), tuple(operands)
    return ((lambda *refs: body(*refs[k:])), [pl.BlockSpec(memory_space=pl.ANY)] * k + list(in_specs),
            tuple(after) + tuple(operands))


def _row_spec(tm, n):
    return pl.BlockSpec((tm, n), lambda i: (i, 0))


def _full_spec(shape):
    nd = len(shape)
    return pl.BlockSpec(shape, lambda *_: (0,) * nd)


def _weight_spec(shape):
    nd = len(shape)
    return pl.BlockSpec(shape, lambda *_: (0,) * nd, pipeline_mode=pl.Buffered(1))


def _rope_tables(L):
    half = HEAD_D // 2
    inv_freq = ROPE_BASE ** (-jnp.arange(half, dtype=_F32) / half)
    twice = lambda t: jnp.concatenate([t, t], axis=-1)
    off = jnp.arange(ROPE_CHUNK, dtype=_F32)[:, None] * inv_freq[None, :]
    start = (ROPE_CHUNK * jnp.arange(L // ROPE_CHUNK, dtype=_F32))[:, None] * inv_freq[None, :]
    return (twice(jnp.cos(off)), twice(jnp.sin(off)),
            twice(jnp.cos(start))[:, None, :], twice(jnp.sin(start))[:, None, :])


def _prenorm(x, g, tm, after=()):
    L = x.shape[0]

    def body(x_ref, g_ref, h_ref):
        xv = x_ref[...]
        h_ref[...] = (xv * _rms_r(xv) * g_ref[...]).astype(_BF)

    body, in_specs, operands = _ordered(body, [_row_spec(tm, D_MODEL), _full_spec((1, D_MODEL))], (x, g), after)
    return pl.pallas_call(
        body, name="prenorm", grid=(L // tm,),
        in_specs=in_specs, out_specs=_row_spec(tm, D_MODEL),
        out_shape=jax.ShapeDtypeStruct((L, D_MODEL), _BF),
        compiler_params=_params("parallel"),
    )(*operands)


def _inproj_fwd(h, w_in_t, rope, tm):
    L = h.shape[0]
    n_chunks = tm // ROPE_CHUNK

    def body(h_ref, w_ref, co_ref, so_ref, cs_ref, ss_ref, q_ref, k_ref, v_ref, gate_ref, u_ref, cos_ref, sin_ref):
        proj = _dot_nt(h_ref[...], w_ref[...])
        lane = lax.broadcasted_iota(jnp.int32, (ROPE_CHUNK, HEAD_D), 1)
        sign = jnp.where(lane < HEAD_D // 2, -1.0, 1.0)
        co, so = co_ref[...], so_ref[...]
        for c in range(n_chunks):
            chunk = pl.program_id(0) * n_chunks + c
            cst, sst = cs_ref[chunk], ss_ref[chunk]
            rows = slice(c * ROPE_CHUNK, (c + 1) * ROPE_CHUNK)
            cs = co * cst - so * sst
            sn = (so * cst + co * sst) * sign
            cos_ref[rows, :] = cs
            sin_ref[rows, :] = sn
            for hh in range(N_HEAD):
                lo = hh * HEAD_D
                q_ref[rows, lo:lo + HEAD_D] = _rope(proj[rows, lo:lo + HEAD_D], cs, sn).astype(_BF)
                kh = _rope(proj[rows, RET_W + lo:RET_W + lo + HEAD_D], cs, sn) * (HEAD_D ** -0.5)
                k_ref[rows, lo:lo + HEAD_D] = kh.astype(_BF)
        v_ref[...] = proj[:, 2 * RET_W:3 * RET_W].astype(_BF)
        gate_ref[...] = proj[:, 3 * RET_W:4 * RET_W]
        u_ref[...] = proj[:, 4 * RET_W:]

    nc = L // ROPE_CHUNK
    return pl.pallas_call(
        body, name="inproj_fwd", grid=(L // tm,),
        in_specs=[_row_spec(tm, D_MODEL), _weight_spec((IN_COLS, D_MODEL)),
                  _full_spec((ROPE_CHUNK, HEAD_D)), _full_spec((ROPE_CHUNK, HEAD_D)),
                  _full_spec((nc, 1, HEAD_D)), _full_spec((nc, 1, HEAD_D))],
        out_specs=[_row_spec(tm, RET_W)] * 5 + [_row_spec(tm, HEAD_D)] * 2,
        out_shape=[jax.ShapeDtypeStruct((L, RET_W), _BF)] * 3 + [jax.ShapeDtypeStruct((L, RET_W), _F32)] * 2
        + [jax.ShapeDtypeStruct((L, HEAD_D), _F32)] * 2,
        compiler_params=_params("parallel"),
    )(h, w_in_t, *rope)


def _ret_consts():
    lg = jnp.log(1.0 - jnp.exp(jnp.linspace(math.log(1.0 / 32), math.log(1.0 / 512), N_HEAD))).astype(_F32)
    idx = jnp.arange(CHUNK, dtype=_F32)
    diff = idx[:, None] - idx[None, :]
    decay = jnp.where(diff[None] >= 0, jnp.exp(jnp.maximum(diff, 0.0)[None] * lg[:, None, None]), 0.0)
    zeta = jnp.exp((CHUNK - 1 - idx)[None, :] * lg[:, None])
    xi = jnp.exp((idx + 1.0)[None, :] * lg[:, None])
    gc = jnp.exp(CHUNK * lg)
    wide = lambda t: jnp.broadcast_to(t[:, :, None], (N_HEAD, CHUNK, HEAD_D)).astype(_F32)
    gcw = jnp.broadcast_to(gc[:, None, None], (N_HEAD, SUBLANES, HEAD_D)).astype(_F32)
    return decay.astype(_F32), wide(xi), wide(zeta), gcw


def _head_specs():
    wide = _full_spec((N_HEAD, CHUNK, HEAD_D))
    return [_full_spec((N_HEAD, CHUNK, CHUNK)), wide, wide, _full_spec((N_HEAD, SUBLANES, HEAD_D))]


def _retention_fwd(q, k, v, gate, ggn, consts):
    L = q.shape[0]
    nc = L // CHUNK
    cps = math.gcd(RET_STEP_CHUNKS, nc)
    blk = pl.BlockSpec((cps * CHUNK, RET_W), lambda n: (n, 0))

    def body(q_ref, k_ref, v_ref, gate_ref, ggn_ref, dm_ref, xi_ref, zeta_ref, gc_ref,
             o_ref, y_ref, rp_ref, r_scr):
        @pl.when(pl.program_id(0) == 0)
        def _():
            r_scr[...] = jnp.zeros_like(r_scr)

        for hh in range(N_HEAD):
            cols = slice(hh * HEAD_D, (hh + 1) * HEAD_D)
            state = r_scr[hh]
            for c in range(cps):
                rows = slice(c * CHUNK, (c + 1) * CHUNK)
                qv, kv, vv = q_ref[rows, cols], k_ref[rows, cols], v_ref[rows, cols]
                s = _dot_nt(qv, kv) * dm_ref[hh]
                o = _dot(s.astype(_BF), vv) + _dot(qv, state.astype(_BF)) * xi_ref[hh]
                o_ref[rows, cols] = o
                rp_ref[hh, c] = state
                vz = (vv.astype(_F32) * zeta_ref[hh]).astype(_BF)
                state = gc_ref[hh, 0:1, :] * state + _dot_tn(kv, vz)
                dlt = o - jnp.mean(o, axis=-1, keepdims=True)
                on = dlt * lax.rsqrt(jnp.mean(dlt * dlt, axis=-1, keepdims=True) + NORM_EPS)
                gt = gate_ref[rows, cols]
                y_ref[rows, cols] = (gt * _sigmoid(gt) * (on * ggn_ref[:, cols])).astype(_BF)
            r_scr[hh] = state

    return pl.pallas_call(
        body, name="retention_fwd", grid=(nc // cps,),
        in_specs=[blk, blk, blk, blk, _full_spec((1, RET_W))] + _head_specs(),
        out_specs=[blk, blk, pl.BlockSpec((N_HEAD, cps, HEAD_D, HEAD_D), lambda n: (0, n, 0, 0))],
        out_shape=[jax.ShapeDtypeStruct((L, RET_W), _F32), jax.ShapeDtypeStruct((L, RET_W), _BF),
                   jax.ShapeDtypeStruct((N_HEAD, nc, HEAD_D, HEAD_D), _F32)],
        scratch_shapes=[pltpu.VMEM((N_HEAD, HEAD_D, HEAD_D), _F32)],
        compiler_params=_params("arbitrary"),
    )(q, k, v, gate, ggn, *consts)


def _rows_to_segments(dst_scr, src_ref, seg):
    for g in range(dst_scr.shape[0]):
        for j in range(SUBLANES):
            dst_scr[g, pl.ds(j, seg, stride=SUBLANES), :] = src_ref[pl.ds(j * seg, seg), g * LANES:(g + 1) * LANES]


def _segments_to_rows(dst_ref, src_scr, seg):
    for g in range(src_scr.shape[0]):
        for j in range(SUBLANES):
            dst_ref[pl.ds(j * seg, seg), g * LANES:(g + 1) * LANES] = src_scr[g, pl.ds(j, seg, stride=SUBLANES), :].astype(dst_ref.dtype)


def _scan_segments(x_ref, tab_ref, pw_ref, carry_ref, seg, reverse, entry_ref=None, fwd_ref=None, fwd_entry_ref=None,
                   da_ref=None):
    G = x_ref.shape[0]
    W = KB_STATES
    re, im = pl.ds(0, W), pl.ds(W, W)
    row_id = lax.broadcasted_iota(jnp.int32, (SUBLANES, W), 0)
    edge_in = (row_id == SUBLANES - 1) if reverse else (row_id == 0)
    edge_out = 0 if reverse else SUBLANES - 1
    a_tab = [(tab_ref[g, 0], tab_ref[g, 1]) for g in range(G)]

    def local(i, st):
        r = (seg - 1 - i) if reverse else i
        out = []
        for g in range(G):
            (ar, ai), (sr, si) = a_tab[g], st[g]
            nr = ar * sr - ai * si + x_ref[g, r, :, re]
            ni = ar * si + ai * sr + x_ref[g, r, :, im]
            x_ref[g, r, :, re] = nr
            x_ref[g, r, :, im] = ni
            out.append((nr, ni))
        return tuple(out)

    zero = jnp.zeros((SUBLANES, W), _F32)
    ends = lax.fori_loop(0, seg, local, tuple((zero, zero) for _ in range(G)), unroll=SCAN_UNROLL)

    entry = []
    shift = (SUBLANES - 1) if reverse else 1
    for g in range(G):
        er, ei = ends[g]
        fr = jnp.where(edge_in, carry_ref[g, :, re], pltpu.roll(er, shift, 0))
        fi = jnp.where(edge_in, carry_ref[g, :, im], pltpu.roll(ei, shift, 0))
        for j, dist in enumerate((1, 2, 4)):
            pr, pi = tab_ref[g, 2 + 2 * j], tab_ref[g, 3 + 2 * j]
            sh = (SUBLANES - dist) if reverse else dist
            sr, si = pltpu.roll(fr, sh, 0), pltpu.roll(fi, sh, 0)
            fr, fi = fr + pr * sr - pi * si, fi + pr * si + pi * sr
        br, bi = tab_ref[g, 8], tab_ref[g, 9]
        outr = br * fr - bi * fi + er
        outi = br * fi + bi * fr + ei
        carry_ref[g, :, re] = jnp.broadcast_to(outr[edge_out:edge_out + 1, :], (SUBLANES, W))
        carry_ref[g, :, im] = jnp.broadcast_to(outi[edge_out:edge_out + 1, :], (SUBLANES, W))
        entry.append((fr, fi))
        if entry_ref is not None:
            entry_ref[g, :, re] = fr
            entry_ref[g, :, im] = fi

    add_da = da_ref is not None

    def fix(r, st, first=False):
        out = []
        for g in range(G):
            fr, fi = entry[g]
            pwr, pwi = pw_ref[g, r, :, re], pw_ref[g, r, :, im]
            xr = x_ref[g, r, :, re] + (pwr * fr - pwi * fi)
            xi = x_ref[g, r, :, im] + (pwr * fi + pwi * fr)
            x_ref[g, r, :, re] = xr
            x_ref[g, r, :, im] = xi
            if add_da:
                prev = fwd_entry_ref.at[g] if first else fwd_ref.at[g, r - 1]
                xpr, xpi = prev[:, re], prev[:, im]
                out.append((st[g][0] + (xr * xpr + xi * xpi), st[g][1] + (xi * xpr - xr * xpi)))
            else:
                out.append(st[g])
        return tuple(out)

    if add_da:
        st = fix(0, tuple((zero, zero) for _ in range(G)), first=True)
        st = lax.fori_loop(1, seg, fix, st, unroll=SCAN_UNROLL)
        for g in range(G):
            da_ref[g, :, re] += st[g][0]
            da_ref[g, :, im] += st[g][1]
    else:
        lax.fori_loop(0, seg, fix, tuple((zero[0:1, 0:LANES],) for _ in range(G)), unroll=FIX_UNROLL)


def _s5_specs(seg, time=lambda t: t):
    G = KB_PER_STEP
    return dict(
        x=pl.BlockSpec((G, seg, SUBLANES, 2 * KB_STATES), lambda kb, t: (kb, time(t), 0, 0)),
        ent=pl.BlockSpec((G, 1, SUBLANES, 2 * KB_STATES), lambda kb, t: (kb, time(t), 0, 0)),
        b=pl.BlockSpec((G, LANES, 2 * KB_STATES), lambda kb, t: (kb, 0, 0)),
        c=pl.BlockSpec((G, 2 * KB_STATES, LANES), lambda kb, t: (kb, 0, 0)),
        tab=pl.BlockSpec((G, 10, SUBLANES, KB_STATES), lambda kb, t: (kb, 0, 0, 0)),
        pw=pl.BlockSpec((G, seg, 1, 2 * KB_STATES), lambda kb, t: (kb, 0, 0, 0)),
        d=pl.BlockSpec((1, G * LANES), lambda kb, t: (0, kb)),
    )


def _s5_fwd(u, bmat, cmat, tab_f, pw_f, d_skip, tb):
    L = u.shape[0]
    nt = L // tb
    seg = tb // SUBLANES
    G = KB_PER_STEP
    ucol = pl.BlockSpec((tb, G * LANES), lambda kb, t: (t, kb))
    sp = _s5_specs(seg)

    def body(u_ref, b_ref, c_ref, tab_ref, pw_ref, d_ref, s_ref, x_ref, ent_ref, up_scr, y_scr, carry_scr):
        @pl.when(pl.program_id(1) == 0)
        def _():
            carry_scr[...] = jnp.zeros_like(carry_scr)

        _rows_to_segments(up_scr, u_ref, seg)
        for g in range(G):
            x_ref[g] = _dot(up_scr[g].astype(_BF), b_ref[g]).reshape(seg, SUBLANES, 2 * KB_STATES)
        _scan_segments(x_ref, tab_ref, pw_ref, carry_scr, seg, reverse=False, entry_ref=ent_ref.at[:, 0])
        for g in range(G):
            y = _dot(x_ref[g].reshape(tb, 2 * KB_STATES).astype(_BF), c_ref[g])
            y_scr[g] = y + d_ref[:, g * LANES:(g + 1) * LANES] * up_scr[g]
        _segments_to_rows(s_ref, y_scr, seg)

    return pl.pallas_call(
        body, name="s5_fwd", grid=(N_KB // G, nt),
        in_specs=[ucol, sp["b"], sp["c"], sp["tab"], sp["pw"], sp["d"]],
        out_specs=[ucol, sp["x"], sp["ent"]],
        out_shape=[jax.ShapeDtypeStruct((L, SSM_W), _F32),
                   jax.ShapeDtypeStruct((N_KB, L // SUBLANES, SUBLANES, 2 * KB_STATES), _F32),
                   jax.ShapeDtypeStruct((N_KB, nt, SUBLANES, 2 * KB_STATES), _F32)],
        scratch_shapes=[pltpu.VMEM((G, tb, LANES), _F32)] * 2 + [pltpu.VMEM((G, SUBLANES, 2 * KB_STATES), _F32)],
        compiler_params=_params("parallel", "arbitrary"),
    )(u, bmat, cmat, tab_f, pw_f, d_skip)


def _mixout_fwd(s, y_ret, x, w_glu, w_out, g2, tm):
    L = s.shape[0]
    steps = L // tm

    def body(s_hbm, yr_ref, x_hbm, wg_ref, wo_ref, g_ref, ys_ref, gl_ref, mix_ref, x2_ref, cat_scr,
             s_ring, x_ring, sems):
        i = pl.program_id(0)

        def fetch(t):
            slot = lax.rem(t, 3)
            return (pltpu.make_async_copy(s_hbm.at[pl.ds(t * tm, tm), :], s_ring.at[slot], sems.at[0, slot]),
                    pltpu.make_async_copy(x_hbm.at[pl.ds(t * tm, tm), :], x_ring.at[slot], sems.at[1, slot]))

        @pl.when(i == 0)
        def _():
            for t in range(min(2, steps)):
                for cp in fetch(t):
                    cp.start()

        @pl.when(i + 2 < steps)
        def _():
            for cp in fetch(i + 2):
                cp.start()

        for cp in fetch(i):
            cp.wait()
        s_ref, x_ref = s_ring.at[lax.rem(i, 3)], x_ring.at[lax.rem(i, 3)]
        for rows in _row_chunks(tm):
            ys = _gelu(s_ref[rows, :]).astype(_BF)
            ys_ref[rows, :] = ys
            glu = _dot(ys, wg_ref[...])
            gl = (glu[:, :SSM_W] * _sigmoid(glu[:, SSM_W:])).astype(_BF)
            gl_ref[rows, :] = gl
            cat_scr[rows, :RET_W] = yr_ref[rows, :]
            cat_scr[rows, RET_W:] = gl
            mix = _dot(cat_scr[rows, :], wo_ref[...])
            mix_ref[rows, :] = mix.astype(_BF)
            x2_ref[rows, :] = x_ref[rows, :] + mix * _rms_r(mix) * g_ref[...]

    return pl.pallas_call(
        body, name="mixout_fwd", grid=(L // tm,),
        in_specs=[pl.BlockSpec(memory_space=pl.ANY), _row_spec(tm, RET_W), pl.BlockSpec(memory_space=pl.ANY),
                  _weight_spec((SSM_W, 2 * SSM_W)), _weight_spec((D_MODEL, D_MODEL)), _full_spec((1, D_MODEL))],
        out_specs=[_row_spec(tm, SSM_W), _row_spec(tm, SSM_W), _row_spec(tm, D_MODEL), _row_spec(tm, D_MODEL)],
        out_shape=[jax.ShapeDtypeStruct((L, SSM_W), _BF), jax.ShapeDtypeStruct((L, SSM_W), _BF),
                   jax.ShapeDtypeStruct((L, D_MODEL), _BF), jax.ShapeDtypeStruct((L, D_MODEL), _F32)],
        scratch_shapes=[pltpu.VMEM((tm, D_MODEL), _BF), pltpu.VMEM((3, tm, SSM_W), _F32),
                        pltpu.VMEM((3, tm, D_MODEL), _F32), pltpu.SemaphoreType.DMA((2, 3))],
        compiler_params=_params("arbitrary"),
    )(s, y_ret, x, w_glu, w_out, g2)


FF1_COLS = D_FF // N_DEV


def _ff1_fwd(x2, g3, w1, tm):
    L = x2.shape[0]

    def body(x_ref, g_ref, w_ref, h_ref, a_ref):
        for rows in _row_chunks(tm):
            xv = x_ref[rows, :]
            h = (xv * _rms_r(xv) * g_ref[...]).astype(_BF)
            h_ref[rows, :] = h
            for j in range(N_DEV):
                cols = slice(j * FF1_COLS, (j + 1) * FF1_COLS)
                rl = jnp.maximum(_dot(h, w_ref[j]), 0.0)
                a_ref[rows, cols] = (rl * rl).astype(_BF)

    return pl.pallas_call(
        body, name="ff1_fwd", grid=(L // tm,),
        in_specs=[_row_spec(tm, D_MODEL), _full_spec((1, D_MODEL)), _weight_spec((N_DEV, D_MODEL, FF1_COLS))],
        out_specs=[_row_spec(tm, D_MODEL), _row_spec(tm, D_FF)],
        out_shape=[jax.ShapeDtypeStruct((L, D_MODEL), _BF), jax.ShapeDtypeStruct((L, D_FF), _BF)],
        compiler_params=_params("parallel"),
    )(x2, g3, w1)


def _ff2_loss(act, x2, tgt, g4, w2, tm):
    L = act.shape[0]

    def body(f_ref, x_ref, t_ref, g_ref, w_ref, dy_ref, dm_ref, dg_ref, ls_ref):
        @pl.when(pl.program_id(0) == 0)
        def _():
            dg_ref[...] = jnp.zeros_like(dg_ref)
            ls_ref[...] = jnp.zeros_like(ls_ref)

        g = g_ref[...]
        for rows in _row_chunks(tm):
            m = _dot(f_ref[rows, :], w_ref[...])
            y = x_ref[rows, :] + m * _rms_r(m) * g
            err = y - t_ref[rows, :]
            ls_ref[...] += jnp.sum(err * err, axis=0, keepdims=True)
            dy = err * (1.0 / D_MODEL)
            dy_ref[rows, :] = dy
            dm, dgr = _rms_bwd(m, g, dy)
            dm_ref[rows, :] = dm.astype(_BF)
            dg_ref[...] += jnp.sum(dgr, axis=0, keepdims=True)

    return pl.pallas_call(
        body, name="ff2_loss", grid=(L // tm,),
        in_specs=[_row_spec(tm, D_FF), _row_spec(tm, D_MODEL), _row_spec(tm, D_MODEL),
                  _full_spec((1, D_MODEL)), _weight_spec((D_FF, D_MODEL))],
        out_specs=[_row_spec(tm, D_MODEL), _row_spec(tm, D_MODEL), _full_spec((1, D_MODEL)), _full_spec((1, D_MODEL))],
        out_shape=[jax.ShapeDtypeStruct((L, D_MODEL), _F32), jax.ShapeDtypeStruct((L, D_MODEL), _BF),
                   jax.ShapeDtypeStruct((1, D_MODEL), _F32), jax.ShapeDtypeStruct((1, D_MODEL), _F32)],
        compiler_params=_params("arbitrary"),
    )(act, x2, tgt, g4, w2)


def _mlp_fwd_loss(x2, tgt, g3, g4, w1, w2, tm):
    L = x2.shape[0]

    def body(x_ref, t_ref, g3_ref, g4_ref, w1_ref, w2_ref, h_ref, a_ref, dy_ref, dm_ref, dg_ref, ls_ref):
        @pl.when(pl.program_id(0) == 0)
        def _():
            dg_ref[...] = jnp.zeros_like(dg_ref)
            ls_ref[...] = jnp.zeros_like(ls_ref)

        g = g4_ref[...]
        for rows in _row_chunks(tm):
            xv = x_ref[rows, :]
            h = (xv * _rms_r(xv) * g3_ref[...]).astype(_BF)
            h_ref[rows, :] = h
            for j in range(N_DEV):
                cols = slice(j * FF1_COLS, (j + 1) * FF1_COLS)
                rl = jnp.maximum(_dot(h, w1_ref[j]), 0.0)
                a_ref[rows, cols] = (rl * rl).astype(_BF)
            m = _dot(a_ref[rows, :], w2_ref[...])
            y = x_ref[rows, :] + m * _rms_r(m) * g
            err = y - t_ref[rows, :]
            ls_ref[...] += jnp.sum(err * err, axis=0, keepdims=True)
            dy = err * (1.0 / D_MODEL)
            dy_ref[rows, :] = dy
            dm, dgr = _rms_bwd(m, g, dy)
            dm_ref[rows, :] = dm.astype(_BF)
            dg_ref[...] += jnp.sum(dgr, axis=0, keepdims=True)

    vec = _full_spec((1, D_MODEL))
    return pl.pallas_call(
        body, name="mlp_fwd_loss", grid=(L // tm,),
        in_specs=[_row_spec(tm, D_MODEL), _row_spec(tm, D_MODEL), vec, vec,
                  _weight_spec((N_DEV, D_MODEL, FF1_COLS)), _weight_spec((D_FF, D_MODEL))],
        out_specs=[_row_spec(tm, D_MODEL), _row_spec(tm, D_FF), _row_spec(tm, D_MODEL), _row_spec(tm, D_MODEL), vec, vec],
        out_shape=[jax.ShapeDtypeStruct((L, D_MODEL), _BF), jax.ShapeDtypeStruct((L, D_FF), _BF),
                   jax.ShapeDtypeStruct((L, D_MODEL), _F32), jax.ShapeDtypeStruct((L, D_MODEL), _BF),
                   jax.ShapeDtypeStruct((1, D_MODEL), _F32), jax.ShapeDtypeStruct((1, D_MODEL), _F32)],
        compiler_params=_params("arbitrary"),
    )(x2, tgt, g3, g4, w1, w2)


def _ff2_bwd(dm, act, w2, tm, tn):
    L = dm.shape[0]
    last = L // tm - 1

    def body(dm_ref, a_ref, w_ref, df_ref, dw_ref, acc):
        @pl.when(pl.program_id(1) == 0)
        def _():
            acc[...] = jnp.zeros_like(acc)

        dmv = dm_ref[...]
        av = a_ref[...]
        df_ref[...] = (_dot_nt(dmv, w_ref[...]) * jnp.sqrt(4.0 * av.astype(_F32))).astype(_BF)
        acc[...] += _dot_tn(av, dmv)

        @pl.when(pl.program_id(1) == last)
        def _():
            dw_ref[...] = acc[...].astype(_BF)

    return pl.pallas_call(
        body, name="ff2_bwd", grid=(D_FF // tn, L // tm),
        in_specs=[pl.BlockSpec((tm, D_MODEL), lambda j, i: (i, 0)), pl.BlockSpec((tm, tn), lambda j, i: (i, j)),
                  pl.BlockSpec((tn, D_MODEL), lambda j, i: (j, 0))],
        out_specs=[pl.BlockSpec((tm, tn), lambda j, i: (i, j)), pl.BlockSpec((tn, D_MODEL), lambda j, i: (j, 0))],
        out_shape=[jax.ShapeDtypeStruct((L, D_FF), _BF), jax.ShapeDtypeStruct((D_FF, D_MODEL), _BF)],
        scratch_shapes=[pltpu.VMEM((tn, D_MODEL), _F32)],
        compiler_params=_params("parallel", "arbitrary"),
    )(dm, act, w2)


def _ff1_bwd(df1, w1, x2, mix, dy, g3, g2, tm):
    L = df1.shape[0]

    def body(df_ref, w_ref, x2_ref, mix_ref, dy_ref, g3_ref, g2_ref, dx2_ref, dmix_ref, dg3_ref, dg2_ref):
        @pl.when(pl.program_id(0) == 0)
        def _():
            dg3_ref[...] = jnp.zeros_like(dg3_ref)
            dg2_ref[...] = jnp.zeros_like(dg2_ref)

        for rows in _row_chunks(tm):
            dh = _dot_nt(df_ref[rows, 0:FF1_COLS], w_ref[0])
            for j in range(1, N_DEV):
                dh = dh + _dot_nt(df_ref[rows, j * FF1_COLS:(j + 1) * FF1_COLS], w_ref[j])
            dz, dgr = _rms_bwd(x2_ref[rows, :], g3_ref[...], dh)
            dg3_ref[...] += jnp.sum(dgr, axis=0, keepdims=True)
            dx2 = dy_ref[rows, :] + dz
            dx2_ref[rows, :] = dx2
            dmx, dgr2 = _rms_bwd(mix_ref[rows, :].astype(_F32), g2_ref[...], dx2)
            dg2_ref[...] += jnp.sum(dgr2, axis=0, keepdims=True)
            dmix_ref[rows, :] = dmx.astype(_BF)

    vec = _full_spec((1, D_MODEL))
    return pl.pallas_call(
        body, name="ff1_bwd", grid=(L // tm,),
        in_specs=[_row_spec(tm, D_FF), _weight_spec((N_DEV, D_MODEL, FF1_COLS)), _row_spec(tm, D_MODEL),
                  _row_spec(tm, D_MODEL), _row_spec(tm, D_MODEL), vec, vec],
        out_specs=[_row_spec(tm, D_MODEL), _row_spec(tm, D_MODEL), vec, vec],
        out_shape=[jax.ShapeDtypeStruct((L, D_MODEL), _F32), jax.ShapeDtypeStruct((L, D_MODEL), _BF),
                   jax.ShapeDtypeStruct((1, D_MODEL), _F32), jax.ShapeDtypeStruct((1, D_MODEL), _F32)],
        compiler_params=_params("arbitrary"),
    )(df1, w1, x2, mix, dy, g3, g2)


def _matmul_tn(a, b, tm, tn, name, slots=0):
    L, K = a.shape
    N = b.shape[1]
    last = L // tm - 1

    def body(a_ref, b_ref, o_ref, acc):
        @pl.when(pl.program_id(1) == 0)
        def _():
            acc[...] = jnp.zeros_like(acc)

        acc[...] += _dot_tn(a_ref[...].astype(_BF), b_ref[...].astype(_BF))

        @pl.when(pl.program_id(1) == last)
        def _():
            if slots:
                for c in range(tn // slots):
                    o_ref[c] = acc[:, c * slots:(c + 1) * slots].astype(_BF)
            else:
                o_ref[...] = acc[...].astype(_BF)

    if slots:
        out_spec = pl.BlockSpec((tn // slots, K, slots), lambda j, i: (j, 0, 0))
        out_shape = jax.ShapeDtypeStruct((N // slots, K, slots), _BF)
    else:
        out_spec = pl.BlockSpec((K, tn), lambda j, i: (0, j))
        out_shape = jax.ShapeDtypeStruct((K, N), _BF)
    return pl.pallas_call(
        body, name=name, grid=(N // tn, L // tm),
        in_specs=[pl.BlockSpec((tm, K), lambda j, i: (i, 0)), pl.BlockSpec((tm, tn), lambda j, i: (i, j))],
        out_specs=out_spec, out_shape=out_shape,
        scratch_shapes=[pltpu.VMEM((K, tn), _F32)],
        compiler_params=_params("parallel", "arbitrary"),
    )(a, b)


def _dw_out_glu(y_ret, gl, dmix, ys, dglu, tk):
    L = dmix.shape[0]
    last = L // tk - 1

    def body(a0_ref, a1_ref, b_ref, ys_ref, dglu_ref, o_ref, og_ref, acc, acc_g):
        @pl.when(pl.program_id(0) == 0)
        def _():
            acc[...] = jnp.zeros_like(acc)
            acc_g[...] = jnp.zeros_like(acc_g)

        bv = b_ref[...]
        acc[:RET_W, :] += _dot_tn(a0_ref[...], bv)
        acc[RET_W:, :] += _dot_tn(a1_ref[...], bv)
        acc_g[...] += _dot_tn(ys_ref[...], dglu_ref[...])

        @pl.when(pl.program_id(0) == last)
        def _():
            o_ref[...] = acc[...].astype(_BF)
            og_ref[...] = acc_g[...].astype(_BF)

    return pl.pallas_call(
        body, name="dw_out_glu", grid=(L // tk,),
        in_specs=[_row_spec(tk, RET_W), _row_spec(tk, SSM_W), _row_spec(tk, D_MODEL),
                  _row_spec(tk, SSM_W), _row_spec(tk, 2 * SSM_W)],
        out_specs=[_full_spec((D_MODEL, D_MODEL)), _full_spec((SSM_W, 2 * SSM_W))],
        out_shape=[jax.ShapeDtypeStruct((D_MODEL, D_MODEL), _BF), jax.ShapeDtypeStruct((SSM_W, 2 * SSM_W), _BF)],
        scratch_shapes=[pltpu.VMEM((D_MODEL, D_MODEL), _F32), pltpu.VMEM((SSM_W, 2 * SSM_W), _F32)],
        compiler_params=_params("arbitrary"),
    )(y_ret, gl, dmix, ys, dglu)


def _dw_in_t(pieces, h, tk):
    L = h.shape[0]
    last = L // tk - 1

    def body(p0, p1, p2, p3, p4, h_ref, o_ref, acc):
        @pl.when(pl.program_id(0) == 0)
        def _():
            acc[...] = jnp.zeros_like(acc)

        hv = h_ref[...]
        for j, p in enumerate((p0, p1, p2, p3, p4)):
            acc[j * RET_W:(j + 1) * RET_W, :] += _dot_tn(p[...].astype(_BF), hv)

        @pl.when(pl.program_id(0) == last)
        def _():
            o_ref[...] = acc[...].astype(_BF)

    return pl.pallas_call(
        body, name="dw_in", grid=(L // tk,),
        in_specs=[_row_spec(tk, RET_W)] * 5 + [_row_spec(tk, D_MODEL)],
        out_specs=_full_spec((IN_COLS, D_MODEL)), out_shape=jax.ShapeDtypeStruct((IN_COLS, D_MODEL), _BF),
        scratch_shapes=[pltpu.VMEM((IN_COLS, D_MODEL), _F32)],
        compiler_params=_params("arbitrary"),
    )(*pieces, h)


def _mixout_bwd(dmix, w_out, w_glu, ys, s, o, gate, ggn, tm, after=()):
    L = dmix.shape[0]

    def body(dmix_ref, wo_ref, wg_ref, ys_ref, s_ref, o_ref, gate_ref, ggn_ref,
             dglu_ref, ds_ref, dgate_ref, do_ref, dggn_ref):
        @pl.when(pl.program_id(0) == 0)
        def _():
            dggn_ref[...] = jnp.zeros_like(dggn_ref)

        ggn = ggn_ref[...]
        for rows in _row_chunks(tm):
            dcat = _dot_nt(dmix_ref[rows, :], wo_ref[...])
            dy_ret, dy_ssm = dcat[:, :RET_W], dcat[:, RET_W:]
            glu = _dot(ys_ref[rows, :], wg_ref[...])
            ga, sg = glu[:, :SSM_W], _sigmoid(glu[:, SSM_W:])
            dga = (dy_ssm * sg).astype(_BF)
            dgb = (dy_ssm * ga * sg * (1.0 - sg)).astype(_BF)
            dglu_ref[rows, :SSM_W] = dga
            dglu_ref[rows, SSM_W:] = dgb
            dys = _dot_nt(dga, wg_ref[:, :SSM_W]) + _dot_nt(dgb, wg_ref[:, SSM_W:])
            ds_ref[rows, :] = dys * _gelu_grad(s_ref[rows, :])
            gt = gate_ref[rows, :]
            sgt = _sigmoid(gt)
            for hh in range(N_HEAD):
                cols = slice(hh * HEAD_D, (hh + 1) * HEAD_D)
                ov = o_ref[rows, cols]
                dlt = ov - jnp.mean(ov, axis=-1, keepdims=True)
                rstd = lax.rsqrt(jnp.mean(dlt * dlt, axis=-1, keepdims=True) + NORM_EPS)
                on = dlt * rstd
                dyr = dy_ret[:, cols] * (gt[:, cols] * sgt[:, cols])
                dgate_ref[rows, cols] = (dy_ret[:, cols] * (on * ggn[:, cols]) * (sgt[:, cols] * (1.0 + gt[:, cols] * (1.0 - sgt[:, cols])))).astype(_BF)
                dggn_ref[:, cols] += jnp.sum(dyr * on, axis=0, keepdims=True)
                don = dyr * ggn[:, cols]
                do = rstd * (don - jnp.mean(don, axis=-1, keepdims=True) - on * jnp.mean(don * on, axis=-1, keepdims=True))
                do_ref[rows, cols] = do.astype(_BF)

    body, in_specs, operands = _ordered(
        body, [_row_spec(tm, D_MODEL), _weight_spec((D_MODEL, D_MODEL)), _weight_spec((SSM_W, 2 * SSM_W)),
               _row_spec(tm, SSM_W), _row_spec(tm, SSM_W), _row_spec(tm, RET_W), _row_spec(tm, RET_W),
               _full_spec((1, RET_W))], (dmix, w_out, w_glu, ys, s, o, gate, ggn), after)
    return pl.pallas_call(
        body, name="mixout_bwd", grid=(L // tm,),
        in_specs=in_specs,
        out_specs=[_row_spec(tm, 2 * SSM_W), _row_spec(tm, SSM_W), _row_spec(tm, RET_W), _row_spec(tm, RET_W),
                   _full_spec((1, RET_W))],
        out_shape=[jax.ShapeDtypeStruct((L, 2 * SSM_W), _BF), jax.ShapeDtypeStruct((L, SSM_W), _F32),
                   jax.ShapeDtypeStruct((L, RET_W), _BF), jax.ShapeDtypeStruct((L, RET_W), _BF),
                   jax.ShapeDtypeStruct((1, RET_W), _F32)],
        compiler_params=_params("arbitrary"),
    )(*operands)


def _s5_bwd(u, ds, xs, ent, bmat, cmat, tab_r, pw_r, d_skip, tb, after=()):
    L = u.shape[0]
    nt = L // tb
    seg = tb // SUBLANES
    G = KB_PER_STEP
    rcol = pl.BlockSpec((tb, G * LANES), lambda kb, t: (nt - 1 - t, kb))
    sp = _s5_specs(seg, time=lambda t: nt - 1 - t)
    aspec = pl.BlockSpec((G, SUBLANES, 2 * KB_STATES), lambda kb, t: (kb, 0, 0))

    def body(u_ref, ds_ref, x_ref, ent_ref, b_ref, c_ref, tr_ref, pr_ref, d_ref,
             du_ref, db_ref, dc_ref, da_ref, dd_ref, up_scr, dp_scr, g_scr, lc_scr):
        @pl.when(pl.program_id(1) == 0)
        def _():
            lc_scr[...] = jnp.zeros_like(lc_scr)
            db_ref[...] = jnp.zeros_like(db_ref)
            dc_ref[...] = jnp.zeros_like(dc_ref)
            da_ref[...] = jnp.zeros_like(da_ref)
            dd_ref[...] = jnp.zeros_like(dd_ref)

        _rows_to_segments(up_scr, u_ref, seg)
        _rows_to_segments(dp_scr, ds_ref, seg)
        for g in range(G):
            g_scr[g] = _dot_nt(dp_scr[g].astype(_BF), c_ref[g]).reshape(seg, SUBLANES, 2 * KB_STATES)
        _scan_segments(g_scr, tr_ref, pr_ref, lc_scr, seg, reverse=True, fwd_ref=x_ref, fwd_entry_ref=ent_ref.at[:, 0],
                       da_ref=da_ref)
        for g in range(G):
            cols = slice(g * LANES, (g + 1) * LANES)
            uv, dsv = up_scr[g], dp_scr[g]
            ub, dsb = uv.astype(_BF), dsv.astype(_BF)
            lamb = g_scr[g].reshape(tb, 2 * KB_STATES).astype(_BF)
            db_ref[g] += _dot_tn(ub, lamb)
            dc_ref[g] += _dot_tn(dsb, x_ref[g].reshape(tb, 2 * KB_STATES).astype(_BF))
            dd_ref[:, cols] += jnp.sum(dsv * uv, axis=0, keepdims=True)
            up_scr[g] = _dot_nt(lamb, b_ref[g]) + d_ref[:, cols] * dsv
        _segments_to_rows(du_ref, up_scr, seg)

    body, in_specs, operands = _ordered(
        body, [rcol, rcol, sp["x"], sp["ent"], sp["b"], sp["c"], sp["tab"], sp["pw"], sp["d"]],
        (u, ds, xs, ent, bmat, cmat, tab_r, pw_r, d_skip), after)
    return pl.pallas_call(
        body, name="s5_bwd", grid=(N_KB // G, nt),
        in_specs=in_specs,
        out_specs=[rcol, sp["b"], sp["b"], aspec, sp["d"]],
        out_shape=[jax.ShapeDtypeStruct((L, SSM_W), _BF),
                   jax.ShapeDtypeStruct((N_KB, LANES, 2 * KB_STATES), _F32),
                   jax.ShapeDtypeStruct((N_KB, LANES, 2 * KB_STATES), _F32),
                   jax.ShapeDtypeStruct((N_KB, SUBLANES, 2 * KB_STATES), _F32),
                   jax.ShapeDtypeStruct((1, SSM_W), _F32)],
        scratch_shapes=[pltpu.VMEM((G, tb, LANES), _F32)] * 2
        + [pltpu.VMEM((G, seg, SUBLANES, 2 * KB_STATES), _F32), pltpu.VMEM((G, SUBLANES, 2 * KB_STATES), _F32)],
        compiler_params=_params("parallel", "arbitrary"),
    )(*operands)


def _retention_bwd(q, k, v, do, r_prev, consts, cosf, sinf, after=()):
    L = q.shape[0]
    nc = L // CHUNK
    cps = math.gcd(RET_STEP_CHUNKS, nc)
    nb = nc // cps
    blk = pl.BlockSpec((cps * CHUNK, RET_W), lambda n: (nb - 1 - n, 0))
    rope_blk = pl.BlockSpec((cps * CHUNK, HEAD_D), lambda n: (nb - 1 - n, 0))

    def body(q_ref, k_ref, v_ref, do_ref, rp_ref, dm_ref, xi_ref, zeta_ref, gc_ref, cos_ref, sin_ref,
             dq_ref, dk_ref, dv_ref, g_scr):
        @pl.when(pl.program_id(0) == 0)
        def _():
            g_scr[...] = jnp.zeros_like(g_scr)

        for hh in range(N_HEAD):
            cols = slice(hh * HEAD_D, (hh + 1) * HEAD_D)
            dm, zeta = dm_ref[hh], zeta_ref[hh]
            gst = g_scr[hh]
            for c in reversed(range(cps)):
                rows = slice(c * CHUNK, (c + 1) * CHUNK)
                qv, kv, vv, dov = q_ref[rows, cols], k_ref[rows, cols], v_ref[rows, cols], do_ref[rows, cols]
                rb = rp_ref[hh, c].astype(_BF)
                gb = gst.astype(_BF)
                sb = (_dot_nt(qv, kv) * dm).astype(_BF)
                dab = (_dot_nt(dov, vv) * dm).astype(_BF)
                dox = (dov.astype(_F32) * xi_ref[hh]).astype(_BF)
                vz = (vv.astype(_F32) * zeta).astype(_BF)
                dq = _dot(dab, kv) + _dot_nt(dox, rb)
                dk = _dot_tn(dab, qv) + _dot_nt(vz, gb)
                dv = _dot_tn(sb, dov) + _dot(kv, gb) * zeta
                gst = gc_ref[hh, 0:1, :] * gst + _dot_tn(qv, dox)
                cs, sn = cos_ref[rows, :], sin_ref[rows, :]
                dq_ref[rows, cols] = _rope_t(dq, cs, sn).astype(_BF)
                dk_ref[rows, cols] = (_rope_t(dk, cs, sn) * (HEAD_D ** -0.5)).astype(_BF)
                dv_ref[rows, cols] = dv.astype(_BF)
            g_scr[hh] = gst

    body, in_specs, operands = _ordered(
        body, [blk, blk, blk, blk, pl.BlockSpec((N_HEAD, cps, HEAD_D, HEAD_D), lambda n: (0, nb - 1 - n, 0, 0))]
        + _head_specs() + [rope_blk, rope_blk], (q, k, v, do, r_prev, *consts, cosf, sinf), after)
    return pl.pallas_call(
        body, name="retention_bwd", grid=(nb,),
        in_specs=in_specs,
        out_specs=[blk, blk, blk],
        out_shape=[jax.ShapeDtypeStruct((L, RET_W), _BF)] * 3,
        scratch_shapes=[pltpu.VMEM((N_HEAD, HEAD_D, HEAD_D), _F32)],
        compiler_params=_params("arbitrary"),
    )(*operands)


def _inproj_bwd(pieces, w_in_t, x, dx2, g1, tm, after=()):
    L = x.shape[0]

    def body(p0, p1, p2, p3, p4, w_ref, x_ref, dx2_ref, g_ref, dx_ref, dg_ref):
        @pl.when(pl.program_id(0) == 0)
        def _():
            dg_ref[...] = jnp.zeros_like(dg_ref)

        for rows in _row_chunks(tm):
            dh = None
            for j, p in enumerate((p0, p1, p2, p3, p4)):
                part = _dot(p[rows, :].astype(_BF), w_ref[j * RET_W:(j + 1) * RET_W, :])
                dh = part if dh is None else dh + part
            dz, dgr = _rms_bwd(x_ref[rows, :], g_ref[...], dh)
            dx_ref[rows, :] = dx2_ref[rows, :] + dz
            dg_ref[...] += jnp.sum(dgr, axis=0, keepdims=True)

    body, in_specs, operands = _ordered(
        body, [_row_spec(tm, RET_W)] * 5 + [_weight_spec((IN_COLS, D_MODEL)), _row_spec(tm, D_MODEL),
                                             _row_spec(tm, D_MODEL), _full_spec((1, D_MODEL))],
        (*pieces, w_in_t, x, dx2, g1), after)
    return pl.pallas_call(
        body, name="inproj_bwd", grid=(L // tm,),
        in_specs=in_specs,
        out_specs=[_row_spec(tm, D_MODEL), _full_spec((1, D_MODEL))],
        out_shape=[jax.ShapeDtypeStruct((L, D_MODEL), _F32), jax.ShapeDtypeStruct((1, D_MODEL), _F32)],
        compiler_params=_params("arbitrary"),
    )(*operands)


def _sum_adamw(parts, w, m, v, tr, name):
    _, R, Cc = parts.shape

    def body(p_ref, w_ref, m_ref, v_ref, g_ref, d_ref, nm_ref, nv_ref):
        gv = p_ref[0].astype(_F32)
        for s in range(1, N_DEV):
            gv = gv + p_ref[s].astype(_F32)
        g_ref[...] = gv
        nm = ADAM_B1 * m_ref[...] + (1.0 - ADAM_B1) * gv
        nv = ADAM_B2 * v_ref[...] + (1.0 - ADAM_B2) * (gv * gv)
        m_hat = nm / (1.0 - ADAM_B1 ** ADAM_STEP)
        v_hat = nv / (1.0 - ADAM_B2 ** ADAM_STEP)
        d_ref[...] = -ADAM_LR * (m_hat / (jnp.sqrt(v_hat) + ADAM_EPS) + ADAM_WD * w_ref[...])
        nm_ref[...] = nm
        nv_ref[...] = nv

    spec = _row_spec(tr, Cc)
    return pl.pallas_call(
        body, name=name, grid=(R // tr,),
        in_specs=[pl.BlockSpec((N_DEV, tr, Cc), lambda i: (0, i, 0))] + [spec] * 3, out_specs=[spec] * 4,
        out_shape=[jax.ShapeDtypeStruct((R, Cc), _F32)] * 4,
        compiler_params=_params("parallel"),
    )(parts, w, m, v)


def _my_place():
    return lax.axis_index("x"), lax.axis_index("y"), lax.axis_index("c")


HBM_SPEC = pl.BlockSpec(memory_space=pltpu.HBM)
SEM_SPEC = pl.BlockSpec(memory_space=pltpu.SEMAPHORE)
DATAFLOW = pltpu.SideEffectType.DATAFLOW_SIDE_EFFECTING


def _my_index():
    x, y, c = _my_place()
    return 4 * x + 2 * y + c


def _landing(own_block):
    return lax.empty((N_DEV,) + own_block.shape, own_block.dtype)


def _own_copy(src, land, sems, a, gather):
    me = _my_index()
    return pltpu.make_async_copy(src if gather else src.at[me], land.at[me], sems.at[sems.shape[0] // N_DEV * 7 + a])


def _split_copies(src_refs, land_refs, send_sems, recv_sems, gather, first=0):
    x, y, c = _my_place()
    me = 4 * x + 2 * y + c
    copies = []
    for a, (src, land) in enumerate(zip(src_refs, land_refs)):
        for kk in range(1, N_DEV):
            px, py, pc = x ^ (kk >> 2), y ^ ((kk >> 1) & 1), c ^ (kk & 1)
            peer = 4 * px + 2 * py + pc
            copies.append(pltpu.make_async_remote_copy(
                src_ref=src if gather else src.at[peer], dst_ref=land.at[me],
                send_sem=send_sems.at[(first + a) * 7 + kk - 1], recv_sem=recv_sems.at[(first + a) * 7 + kk - 1],
                device_id=(px, py, pc), device_id_type=MESH))
    return copies


def _split_start(srcs, lands, gather, name):
    n = len(srcs)

    def body(*refs):
        src_refs, land_refs = refs[:n], refs[n:2 * n]
        send_sems, recv_sems = refs[2 * n], refs[2 * n + 1]
        token = refs[-1]
        for cp in _split_copies(src_refs, land_refs, send_sems, recv_sems, gather):
            cp.start()
        for a in range(n):
            _own_copy(src_refs[a], land_refs[a], send_sems, a, gather).start()
        token[...] = jnp.zeros_like(token)

    outs = pl.pallas_call(
        body, name=name,
        out_shape=(pltpu.SemaphoreType.DMA((N_DEV * n,)), pltpu.SemaphoreType.DMA((7 * n,)),
                   *[pltpu.HBM(t.shape, t.dtype) for t in srcs], *[pltpu.HBM(t.shape, t.dtype) for t in lands],
                   jax.ShapeDtypeStruct((SUBLANES, LANES), _F32)),
        in_specs=[HBM_SPEC] * (2 * n),
        out_specs=(SEM_SPEC, SEM_SPEC, *[HBM_SPEC] * (2 * n), pl.BlockSpec(memory_space=pltpu.VMEM)),
        input_output_aliases={i: 2 + i for i in range(2 * n)},
        compiler_params=pltpu.CompilerParams(has_side_effects=DATAFLOW),
    )(*[pltpu.with_memory_space_constraint(t, pltpu.HBM) for t in list(srcs) + list(lands)])
    return outs[0], outs[1], outs[2:2 + n], outs[2 + n:2 + 2 * n], outs[-1]


def _split_wait(send_sems, recv_sems, srcs, lands, after, gather, name, first=0):
    n = len(srcs)

    def body(*refs):
        src_refs, land_refs = refs[:n], refs[n:2 * n]
        send_s, recv_s = refs[2 * n], refs[2 * n + 1]
        for cp in _split_copies(src_refs, land_refs, send_s, recv_s, gather, first):
            cp.wait_send()
            cp.wait_recv()
        for a in range(n):
            _own_copy(src_refs[a], land_refs[a], send_s, first + a, gather).wait()

    outs = pl.pallas_call(
        body, name=name,
        out_shape=tuple(pltpu.HBM(t.shape, t.dtype) for t in list(srcs) + list(lands)),
        in_specs=[HBM_SPEC] * (2 * n) + [SEM_SPEC, SEM_SPEC, pl.BlockSpec(memory_space=pl.ANY)],
        out_specs=tuple([HBM_SPEC] * (2 * n)),
        input_output_aliases={i: i for i in range(2 * n)},
        compiler_params=pltpu.CompilerParams(has_side_effects=DATAFLOW),
    )(*srcs, *lands, send_sems, recv_sems, after)
    return outs[n:]


def _discretize(lam_re, lam_im, log_dt, b_re, b_im):
    lr = jnp.minimum(lam_re, -1e-4)
    li = lam_im
    dt = jnp.exp(log_dt)[:, None]
    er = jnp.exp(lr * dt)
    ar, ai = er * jnp.cos(li * dt), er * jnp.sin(li * dt)
    den = lr * lr + li * li
    cr = ((ar - 1.0) * lr + ai * li) / den
    ci = (ai * lr - (ar - 1.0) * li) / den
    bbr = cr[:, :, None] * b_re - ci[:, :, None] * b_im
    bbi = cr[:, :, None] * b_im + ci[:, :, None] * b_re
    return ar, ai, bbr, bbi


def _cmul(ar, ai, br, bi):
    return ar * br - ai * bi, ar * bi + ai * br


def _cpowers(ar, ai, n):
    pr, pi = ar[None], ai[None]
    while pr.shape[0] < n:
        nr, ni = _cmul(pr, pi, pr[-1][None], pi[-1][None])
        pr, pi = jnp.concatenate([pr, nr]), jnp.concatenate([pi, ni])
    return pr[:n], pi[:n]


def _scan_tables(ar, ai, seg, reverse):
    if reverse:
        ai = -ai
    ar, ai = ar.reshape(N_KB, KB_STATES), ai.reshape(N_KB, KB_STATES)
    pr, pi = _cpowers(ar, ai, seg)
    a1 = (pr[-1], pi[-1])
    a2 = _cmul(*a1, *a1)
    a4 = _cmul(*a2, *a2)
    row = jnp.arange(SUBLANES)[None, :, None]
    wide = lambda t: jnp.broadcast_to(t[:, None, :], (N_KB, SUBLANES, KB_STATES))
    tabs = [wide(ar), wide(ai)]
    for dist, (qr, qi) in ((1, a1), (2, a2), (4, a4)):
        keep = (row < SUBLANES - dist) if reverse else (row >= dist)
        tabs += [jnp.where(keep, wide(qr), 0.0), jnp.where(keep, wide(qi), 0.0)]
    tabs += [wide(a1[0]), wide(a1[1])]
    if reverse:
        pr, pi = pr[::-1], pi[::-1]
    pw = jnp.transpose(jnp.concatenate([pr, pi], axis=-1), (1, 0, 2))[:, :, None, :]
    return jnp.stack(tabs, axis=1).astype(_F32), pw.astype(_F32)


def _block_diag_in(br, bi):
    eye = jnp.eye(GROUPS_PER_KB, dtype=_F32)
    one = lambda t: jnp.einsum("kgpc,gh->kgchp", t.reshape(N_KB, GROUPS_PER_KB, N_STATE, SSM_GC), eye).reshape(
        N_KB, LANES, KB_STATES)
    return jnp.concatenate([one(br), one(bi)], axis=-1)


def _block_diag_in_t(dmat):
    d6 = dmat.reshape(N_KB, GROUPS_PER_KB, SSM_GC, 2, GROUPS_PER_KB, N_STATE)
    eye = jnp.eye(GROUPS_PER_KB, dtype=_F32)
    both = jnp.einsum("kgcrhp,gh->rkgpc", d6, eye).reshape(2, N_GROUP, N_STATE, SSM_GC)
    return both[0], both[1]


def _block_diag_out(c_re, c_im):
    eye = jnp.eye(GROUPS_PER_KB, dtype=_F32)
    one = lambda t: jnp.einsum("kgcp,gh->khpgc", t.reshape(N_KB, GROUPS_PER_KB, SSM_GC, N_STATE), eye).reshape(
        N_KB, KB_STATES, LANES)
    return jnp.concatenate([one(c_re), -one(c_im)], axis=1)


def _block_diag_out_t(dmat_t):
    d6 = dmat_t.reshape(N_KB, GROUPS_PER_KB, SSM_GC, 2, GROUPS_PER_KB, N_STATE)
    eye = jnp.eye(GROUPS_PER_KB, dtype=_F32)
    both = jnp.einsum("kgcrhp,gh->rkgcp", d6, eye).reshape(2, N_GROUP, SSM_GC, N_STATE)
    return both[0], -both[1]


SMALL_NAMES = ("norm_mix_pre", "norm_mix_post", "ret_gn_gain", "ssm_lambda_re", "ssm_lambda_im", "ssm_log_dt",
               "ssm_b_re", "ssm_b_im", "ssm_c_re", "ssm_c_im", "ssm_d", "norm_mlp_pre", "norm_mlp_post")


def _local_grads(x, tgt, small, weights, emit, emit_small, tm, tk, tb, zero=0.0):
    L = x.shape[0]
    g1, g2, ggn = small["norm_mix_pre"], small["norm_mix_post"], small["ret_gn_gain"]
    g3, g4, d_skip = small["norm_mlp_pre"], small["norm_mlp_post"], small["ssm_d"]

    rope = _rope_tables(L)
    consts = _ret_consts()

    disc_in = (small["ssm_lambda_re"][0], small["ssm_lambda_im"][0], small["ssm_log_dt"][0] + zero,
               small["ssm_b_re"][0], small["ssm_b_im"][0])
    (ar, ai, bbr, bbi), disc_vjp = jax.vjp(_discretize, *disc_in)
    bmat = _block_diag_in(bbr, bbi).astype(_BF)
    cmat = _block_diag_out(small["ssm_c_re"][0], small["ssm_c_im"][0]).astype(_BF)
    seg = tb // SUBLANES
    tab_f, pw_f = _scan_tables(ar, ai, seg, False)
    tab_r, pw_r = _scan_tables(ar, ai, seg, True)

    h1 = _prenorm(x, g1, min(4 * tm, L), after=(pw_r,))
    (w_in_t,) = weights("in", h1)
    q, k, v, gate, u, cosf, sinf = _inproj_fwd(h1, w_in_t, rope, min(4 * tm, L))
    o, y_ret, r_prev = _retention_fwd(q, k, v, gate, ggn, consts)
    s, xs, ent = _s5_fwd(u, bmat, cmat, tab_f, pw_f, d_skip, tb)
    w_glu, w_out = weights("mix", s)
    ys, gl, mix, x2 = _mixout_fwd(s, y_ret, x, w_glu, w_out, g2, min(2 * tm, L))
    w_ff1, w_ff2 = weights("mlp", x2)
    h3, act, dy, dm, dg4, sq = _mlp_fwd_loss(x2, tgt, g3, g4, w_ff1, w_ff2, min(2 * tm, L))

    df1, dw_ff2 = _ff2_bwd(dm, act, w_ff2, min(1024, L), 1024)
    dx2, dmix, dg3, dg2 = _ff1_bwd(df1, w_ff1, x2, mix, dy, g3, g2, min(2 * tm, L))
    dw_ff1 = _matmul_tn(h3, df1, tk, 2 * FF1_COLS, "dw_ff1", slots=FF1_COLS)
    token = emit({"w_ff1": dw_ff1, "w_ff2": dw_ff2})
    dglu, ds, dgate, do, dggn = _mixout_bwd(dmix, w_out, w_glu, ys, s, o, gate, ggn, min(2 * tm, L), after=token)
    dw_out, dw_glu = _dw_out_glu(y_ret, gl, dmix, ys, dglu, tk)
    token = emit({"w_glu": dw_glu, "w_out": dw_out})
    du, dbmat, dcmat, da8, dd = _s5_bwd(u, ds, xs, ent, bmat, cmat, tab_r, pw_r, d_skip, tb, after=token)

    da = jnp.sum(da8, axis=1)
    dar = da[:, :KB_STATES].reshape(N_GROUP, N_STATE)
    dai = da[:, KB_STATES:].reshape(N_GROUP, N_STATE)
    dbr, dbi = _block_diag_in_t(dbmat)
    dlre, dlim, dldt, dbre, dbim = disc_vjp((dar, dai, dbr, dbi))
    dcre, dcim = _block_diag_out_t(dcmat)
    token = emit_small({
        "norm_mix_post": dg2, "ret_gn_gain": dggn,
        "ssm_lambda_re": dlre[None], "ssm_lambda_im": dlim[None], "ssm_log_dt": dldt[None],
        "ssm_b_re": dbre[None], "ssm_b_im": dbim[None], "ssm_c_re": dcre[None], "ssm_c_im": dcim[None],
        "ssm_d": dd, "norm_mlp_pre": dg3, "norm_mlp_post": dg4,
    }, sq)

    dq, dk, dv = _retention_bwd(q, k, v, do, r_prev, consts, cosf, sinf, after=token)
    pieces = (dq, dk, dv, dgate, du)
    dw_in_t = _dw_in_t(pieces, h1, min(1024, L))
    token = emit({"w_in": dw_in_t})
    gx, dg1 = _inproj_bwd(pieces, w_in_t, x, dx2, g1, min(2 * tm, L), after=token)
    return gx, dg1


BIG_SHAPES = {"w_in": (D_MODEL, IN_COLS // N_DEV), "w_glu": (SSM_W, 2 * SSM_W // N_DEV), "w_out": (D_MODEL // N_DEV, D_MODEL),
              "w_ff1": (D_MODEL, FF1_COLS), "w_ff2": (D_FF // N_DEV, D_MODEL)}
BIG_NAMES = ("w_in", "w_glu", "w_out", "w_ff1", "w_ff2")


def _cols_from_slots(g):
    return jnp.transpose(g, (1, 0, 2)).reshape(g.shape[1], N_DEV * g.shape[2])


def _cols_to_slots(dw):
    r, cols = dw.shape
    return jnp.transpose(dw.reshape(r, N_DEV, cols // N_DEV), (1, 0, 2))


WEIGHT_GROUPS = {"in": ("w_in",), "mix": ("w_glu", "w_out"), "mlp": ("w_ff1", "w_ff2")}


def _weight_from_slots(name, g):
    if name == "w_glu":
        return _cols_from_slots(g)
    if name == "w_ff1":
        return g
    return g.reshape(N_DEV * g.shape[1], g.shape[2])


def _grad_slots(name, dw):
    if name == "w_glu":
        return _cols_to_slots(dw)
    if name == "w_ff1":
        return dw
    if name == "w_in":
        return dw.reshape(N_DEV, BIG_SHAPES[name][1], BIG_SHAPES[name][0])
    return dw.reshape((N_DEV,) + BIG_SHAPES[name])


PIECE_ROWS = 8


VEC_NAMES = tuple(n for n in SMALL_NAMES if n[:6] not in ("ssm_b_", "ssm_c_"))
BC_NAMES = ("ssm_b_re", "ssm_b_im", "ssm_c_re", "ssm_c_im")
BC_ROWS = N_GROUP * SSM_GC


def _bc_view(name, t):
    t = t[0]
    if name.startswith("ssm_b_"):
        t = jnp.swapaxes(t, 1, 2)
    return t.reshape(BC_ROWS, N_STATE)


def _bc_unview(name, t):
    t = t.reshape(N_GROUP, SSM_GC, N_STATE)
    if name.startswith("ssm_b_"):
        t = jnp.swapaxes(t, 1, 2)
    return t[None]


def _pack_bc(vals):
    return jnp.concatenate([_bc_view(n, vals[n]).astype(_F32) for n in BC_NAMES], axis=0)


def _unpack_bc(buf):
    return {n: _bc_unview(n, buf[j * BC_ROWS:(j + 1) * BC_ROWS]) for j, n in enumerate(BC_NAMES)}


def _small_layout(shapes):
    off, rows = {}, 0
    for n in VEC_NAMES:
        off[n] = rows
        rows += -(-math.prod(shapes[n]) // (PIECE_ROWS * LANES)) * PIECE_ROWS
    return off, rows, rows + PIECE_ROWS


def _pack_small(vals, shapes, last=None):
    parts = []
    for n in VEC_NAMES:
        flat = vals[n].reshape(-1).astype(_F32)
        pad = -flat.shape[0] % (PIECE_ROWS * LANES)
        if pad:
            flat = jnp.concatenate([flat, jnp.zeros((pad,), _F32)])
        parts.append(flat.reshape(-1, LANES))
    parts.append(jnp.zeros((PIECE_ROWS, LANES), _F32) if last is None else last)
    return jnp.concatenate(parts, axis=0)


def _unpack_small(buf, shapes):
    off, _, _ = _small_layout(shapes)
    out = {}
    for n in VEC_NAMES:
        size = math.prod(shapes[n])
        rows = -(-size // LANES)
        out[n] = buf[off[n]:off[n] + rows].reshape(-1)[:size].reshape(shapes[n])
    return out


WEIGHT_NAMES = ('norm_mix_pre', 'norm_mix_post', 'w_in', 'ret_gn_gain', 'ssm_lambda_re', 'ssm_lambda_im', 'ssm_log_dt',
                'ssm_b_re', 'ssm_b_im', 'ssm_c_re', 'ssm_c_im', 'ssm_d', 'w_glu', 'w_out', 'norm_mlp_pre',
                'norm_mlp_post', 'w_ff1', 'w_ff2')


def kernel(x, norm_mix_pre, norm_mix_post, w_in, ret_gn_gain, ssm_lambda_re, ssm_lambda_im, ssm_log_dt, ssm_b_re, ssm_b_im, ssm_c_re, ssm_c_im, ssm_d, w_glu, w_out, norm_mlp_pre, norm_mlp_post, w_ff1, w_ff2, loss_target, m_norm_mix_pre, m_norm_mix_post, m_w_in, m_ret_gn_gain, m_ssm_lambda_re, m_ssm_lambda_im, m_ssm_log_dt, m_ssm_b_re, m_ssm_b_im, m_ssm_c_re, m_ssm_c_im, m_ssm_d, m_w_glu, m_w_out, m_norm_mlp_pre, m_norm_mlp_post, m_w_ff1, m_w_ff2, v_norm_mix_pre, v_norm_mix_post, v_w_in, v_ret_gn_gain, v_ssm_lambda_re, v_ssm_lambda_im, v_ssm_log_dt, v_ssm_b_re, v_ssm_b_im, v_ssm_c_re, v_ssm_c_im, v_ssm_d, v_w_glu, v_w_out, v_norm_mlp_pre, v_norm_mlp_post, v_w_ff1, v_w_ff2):
    args = dict(locals())
    w = {n: args[n] for n in WEIGHT_NAMES}
    m = {n: args["m_" + n] for n in WEIGHT_NAMES}
    v = {n: args["v_" + n] for n in WEIGHT_NAMES}
    L = x.shape[1]
    tm = min(256, L)
    tk = min(2048, L)
    tb = min(1024, L)

    calls = {"in": ("w_in",), "rest": WEIGHT_GROUPS["mix"] + WEIGHT_GROUPS["mlp"]}
    started, zero = {}, jnp.zeros((), _F32)
    for call, names in calls.items():
        blocks = [(w[n][0].T if n == "w_in" else w[n][0]).astype(_BF) for n in names]
        blocks[0] = blocks[0] + zero.astype(_BF)
        started[call] = _split_start(blocks, [_landing(b) for b in blocks], True, "weights_start_" + call)
        zero = started[call][4][0, 0]

    def weights(group, after):
        names = WEIGHT_GROUPS[group]
        call = "in" if group == "in" else "rest"
        first = calls[call].index(names[0])
        part = slice(first, first + len(names))
        got = started[call]
        landed = _split_wait(got[0], got[1], got[2][part], got[3][part], after, True, "weights_wait_" + group, first=first)
        return [_weight_from_slots(n, g) for n, g in zip(names, landed)]

    in_flight = []

    def emit(dws):
        names = sorted(dws)
        srcs = [_grad_slots(n, dws[n]) for n in names]
        lands = [_landing(t[0]) for t in srcs]
        started = _split_start(srcs, lands, False, "grads_start_" + "_".join(names))
        in_flight.append((names, started))
        return (started[4],)

    shapes = {n: w[n].shape for n in SMALL_NAMES}
    first_piece = {SMALL_NAMES[0]: jnp.zeros(shapes[SMALL_NAMES[0]], _F32)}
    small_flight = []

    def emit_small(gs, sq):
        loss_rows = jnp.broadcast_to(0.5 / D_MODEL * jnp.sum(sq), (PIECE_ROWS, LANES)).astype(_F32)
        bufs = [_pack_small({**first_piece, **gs}, shapes, loss_rows), _pack_bc(gs)]
        small_flight.append(_split_start(bufs, [_landing(b) for b in bufs], True, "small_grads_start"))
        return (small_flight[0][4],)

    small_w = {n: w[n] for n in SMALL_NAMES}
    gx, dg1 = _local_grads(x[0], loss_target[0], small_w, weights, emit, emit_small, tm, tk, tb, zero=zero)
    last_buf = dg1.reshape(PIECE_ROWS, LANES)
    last_started = _split_start([last_buf], [_landing(last_buf)], True, "last_grad_start")

    grads, delta, new_m, new_v = {}, {}, {}, {}
    after = last_started[4]
    for names, started in in_flight:
        landed = _split_wait(*started[:4], after, False, "grads_wait_" + "_".join(names))
        for n, parts in zip(names, landed):
            flip = (lambda t: t.T) if n == "w_in" else (lambda t: t)
            res = _sum_adamw(parts, flip(w[n][0]), flip(m[n][0]), flip(v[n][0]), math.gcd(256, parts.shape[1]), "adamw_" + n)
            grads[n], delta[n], new_m[n], new_v[n] = (flip(t)[None] for t in res)
        after = res[1]
    small_parts, bc_parts = _split_wait(*small_flight[0][:4], after, True, "small_grads_wait")
    last_parts = _split_wait(*last_started[:4], small_parts, True, "last_grad_wait")[0]
    small_parts = lax.dynamic_update_slice(small_parts, last_parts, (0, 0, 0))
    res_bc = _sum_adamw(bc_parts, _pack_bc(w), _pack_bc(m), _pack_bc(v), BC_ROWS, "adamw_bc")
    sw, sm, sv = _pack_small(w, shapes), _pack_small(m, shapes), _pack_small(v, shapes)
    res = _sum_adamw(small_parts, sw, sm, sv, sw.shape[0], "adamw_small")
    for dst, buf, buf_bc in zip((grads, delta, new_m, new_v), res, res_bc):
        dst.update(_unpack_small(buf, shapes))
        dst.update(_unpack_bc(buf_bc))
    _, loss_at, _ = _small_layout(shapes)
    loss = res[0][loss_at, 0]

    return (loss, gx[None], *[grads[n] for n in WEIGHT_NAMES], *[delta[n] for n in WEIGHT_NAMES],
            *[new_m[n] for n in WEIGHT_NAMES], *[new_v[n] for n in WEIGHT_NAMES])
```

```python
import math

import jax
import jax.numpy as jnp
from jax import lax
from jax.experimental import pallas as pl
from jax.experimental.pallas import tpu as pltpu

_BF = jnp.bfloat16
_F32 = jnp.float32

D_MODEL = 1024
RET_W = 512
N_HEAD = 4
HEAD_D = 128
CHUNK = 256
ROPE_CHUNK = 128
SSM_W = 512
SSM_GC = 16
N_GROUP = 32
N_STATE = 64
GROUPS_PER_KB = 8
N_KB = 4
KB_STATES = GROUPS_PER_KB * N_STATE
D_FF = 4096
IN_COLS = 2560
NORM_EPS = 1e-6
ROPE_BASE = 10000.0
N_DEV = 8

ADAM_LR = 0.001
ADAM_B1 = 0.9
ADAM_B2 = 0.999
ADAM_EPS = 1e-08
ADAM_WD = 0.01
ADAM_STEP = 10

SUBLANES = 8
LANES = 128
VMEM_LIMIT = 52 * 1024 * 1024
RET_STEP_CHUNKS = 2
KB_PER_STEP = 2
SCAN_UNROLL = True
FIX_UNROLL = 8

MESH = pl.DeviceIdType.MESH


def _params(*sem):
    return pltpu.CompilerParams(dimension_semantics=sem, vmem_limit_bytes=VMEM_LIMIT)


def _dot(a, b):
    return jnp.dot(a, b, preferred_element_type=_F32)


def _dot_nt(a, b):
    return lax.dot_general(a, b, (((1,), (1,)), ((), ())), preferred_element_type=_F32)


def _dot_tn(a, b):
    return lax.dot_general(a, b, (((0,), (0,)), ((), ())), preferred_element_type=_F32)


def _rms_r(z):
    return lax.rsqrt(jnp.mean(z * z, axis=-1, keepdims=True) + NORM_EPS)


def _rms_bwd(z, g, dn):
    r = _rms_r(z)
    t = dn * g
    dz = r * t - z * (r * r * r * jnp.mean(t * z, axis=-1, keepdims=True))
    return dz, dn * z * r


def _rope(t, cs, sn):
    return t * cs + pltpu.roll(t, HEAD_D // 2, 1) * sn


def _rope_t(t, cs, sn):
    return t * cs - pltpu.roll(t, HEAD_D // 2, 1) * sn


def _sigmoid(z):
    return 1.0 / (1.0 + jnp.exp(-z))


_GELU_C = math.sqrt(2.0 / math.pi)


def _gelu(z):
    return 0.5 * z * (1.0 + jnp.tanh(_GELU_C * (z + 0.044715 * z * z * z)))


def _gelu_grad(z):
    th = jnp.tanh(_GELU_C * (z + 0.044715 * z * z * z))
    return 0.5 * (1.0 + th) + 0.5 * z * (1.0 - th * th) * _GELU_C * (1.0 + 3 * 0.044715 * z * z)


ROW_CHUNK = 256


def _row_chunks(tm):
    return [pl.ds(i, min(ROW_CHUNK, tm)) for i in range(0, tm, ROW_CHUNK)]


def _ordered(body, in_specs, operands, after):
    k = len(after)
    if not k:
        return body, list(in_specs), tuple(operands)
    return ((lambda *refs: body(*refs[k:])), [pl.BlockSpec(memory_space=pl.ANY)] * k + list(in_specs),
            tuple(after) + tuple(operands))


def _row_spec(tm, n):
    return pl.BlockSpec((tm, n), lambda i: (i, 0))


def _full_spec(shape):
    nd = len(shape)
    return pl.BlockSpec(shape, lambda *_: (0,) * nd)


def _weight_spec(shape):
    nd = len(shape)
    return pl.BlockSpec(shape, lambda *_: (0,) * nd, pipeline_mode=pl.Buffered(1))


def _rope_tables(L):
    half = HEAD_D // 2
    inv_freq = ROPE_BASE ** (-jnp.arange(half, dtype=_F32) / half)
    twice = lambda t: jnp.concatenate([t, t], axis=-1)
    off = jnp.arange(ROPE_CHUNK, dtype=_F32)[:, None] * inv_freq[None, :]
    start = (ROPE_CHUNK * jnp.arange(L // ROPE_CHUNK, dtype=_F32))[:, None] * inv_freq[None, :]
    return (twice(jnp.cos(off)), twice(jnp.sin(off)),
            twice(jnp.cos(start))[:, None, :], twice(jnp.sin(start))[:, None, :])


def _prenorm(x, g, tm, after=()):
    L = x.shape[0]

    def body(x_ref, g_ref, h_ref):
        xv = x_ref[...]
        h_ref[...] = (xv * _rms_r(xv) * g_ref[...]).astype(_BF)

    body, in_specs, operands = _ordered(body, [_row_spec(tm, D_MODEL), _full_spec((1, D_MODEL))], (x, g), after)
    return pl.pallas_call(
        body, name="prenorm", grid=(L // tm,),
        in_specs=in_specs, out_specs=_row_spec(tm, D_MODEL),
        out_shape=jax.ShapeDtypeStruct((L, D_MODEL), _BF),
        compiler_params=_params("parallel"),
    )(*operands)


def _inproj_fwd(h, w_in_t, rope, tm):
    L = h.shape[0]
    n_chunks = tm // ROPE_CHUNK

    def body(h_ref, w_ref, co_ref, so_ref, cs_ref, ss_ref, q_ref, k_ref, v_ref, gate_ref, u_ref, cos_ref, sin_ref):
        proj = _dot_nt(h_ref[...], w_ref[...])
        lane = lax.broadcasted_iota(jnp.int32, (ROPE_CHUNK, HEAD_D), 1)
        sign = jnp.where(lane < HEAD_D // 2, -1.0, 1.0)
        co, so = co_ref[...], so_ref[...]
        for c in range(n_chunks):
            chunk = pl.program_id(0) * n_chunks + c
            cst, sst = cs_ref[chunk], ss_ref[chunk]
            rows = slice(c * ROPE_CHUNK, (c + 1) * ROPE_CHUNK)
            cs = co * cst - so * sst
            sn = (so * cst + co * sst) * sign
            cos_ref[rows, :] = cs
            sin_ref[rows, :] = sn
            for hh in range(N_HEAD):
                lo = hh * HEAD_D
                q_ref[rows, lo:lo + HEAD_D] = _rope(proj[rows, lo:lo + HEAD_D], cs, sn).astype(_BF)
                kh = _rope(proj[rows, RET_W + lo:RET_W + lo + HEAD_D], cs, sn) * (HEAD_D ** -0.5)
                k_ref[rows, lo:lo + HEAD_D] = kh.astype(_BF)
        v_ref[...] = proj[:, 2 * RET_W:3 * RET_W].astype(_BF)
        gate_ref[...] = proj[:, 3 * RET_W:4 * RET_W]
        u_ref[...] = proj[:, 4 * RET_W:]

    nc = L // ROPE_CHUNK
    return pl.pallas_call(
        body, name="inproj_fwd", grid=(L // tm,),
        in_specs=[_row_spec(tm, D_MODEL), _weight_spec((IN_COLS, D_MODEL)),
                  _full_spec((ROPE_CHUNK, HEAD_D)), _full_spec((ROPE_CHUNK, HEAD_D)),
                  _full_spec((nc, 1, HEAD_D)), _full_spec((nc, 1, HEAD_D))],
        out_specs=[_row_spec(tm, RET_W)] * 5 + [_row_spec(tm, HEAD_D)] * 2,
        out_shape=[jax.ShapeDtypeStruct((L, RET_W), _BF)] * 3 + [jax.ShapeDtypeStruct((L, RET_W), _F32)] * 2
        + [jax.ShapeDtypeStruct((L, HEAD_D), _F32)] * 2,
        compiler_params=_params("parallel"),
    )(h, w_in_t, *rope)


def _ret_consts():
    lg = jnp.log(1.0 - jnp.exp(jnp.linspace(math.log(1.0 / 32), math.log(1.0 / 512), N_HEAD))).astype(_F32)
    idx = jnp.arange(CHUNK, dtype=_F32)
    diff = idx[:, None] - idx[None, :]
    decay = jnp.where(diff[None] >= 0, jnp.exp(jnp.maximum(diff, 0.0)[None] * lg[:, None, None]), 0.0)
    zeta = jnp.exp((CHUNK - 1 - idx)[None, :] * lg[:, None])
    xi = jnp.exp((idx + 1.0)[None, :] * lg[:, None])
    gc = jnp.exp(CHUNK * lg)
    wide = lambda t: jnp.broadcast_to(t[:, :, None], (N_HEAD, CHUNK, HEAD_D)).astype(_F32)
    gcw = jnp.broadcast_to(gc[:, None, None], (N_HEAD, SUBLANES, HEAD_D)).astype(_F32)
    return decay.astype(_F32), wide(xi), wide(zeta), gcw


def _head_specs():
    wide = _full_spec((N_HEAD, CHUNK, HEAD_D))
    return [_full_spec((N_HEAD, CHUNK, CHUNK)), wide, wide, _full_spec((N_HEAD, SUBLANES, HEAD_D))]


def _retention_fwd(q, k, v, gate, ggn, consts):
    L = q.shape[0]
    nc = L // CHUNK
    cps = math.gcd(RET_STEP_CHUNKS, nc)
    blk = pl.BlockSpec((cps * CHUNK, RET_W), lambda n: (n, 0))

    def body(q_ref, k_ref, v_ref, gate_ref, ggn_ref, dm_ref, xi_ref, zeta_ref, gc_ref,
             o_ref, y_ref, rp_ref, r_scr):
        @pl.when(pl.program_id(0) == 0)
        def _():
            r_scr[...] = jnp.zeros_like(r_scr)

        for hh in range(N_HEAD):
            cols = slice(hh * HEAD_D, (hh + 1) * HEAD_D)
            state = r_scr[hh]
            for c in range(cps):
                rows = slice(c * CHUNK, (c + 1) * CHUNK)
                qv, kv, vv = q_ref[rows, cols], k_ref[rows, cols], v_ref[rows, cols]
                s = _dot_nt(qv, kv) * dm_ref[hh]
                o = _dot(s.astype(_BF), vv) + _dot(qv, state.astype(_BF)) * xi_ref[hh]
                o_ref[rows, cols] = o
                rp_ref[hh, c] = state
                vz = (vv.astype(_F32) * zeta_ref[hh]).astype(_BF)
                state = gc_ref[hh, 0:1, :] * state + _dot_tn(kv, vz)
                dlt = o - jnp.mean(o, axis=-1, keepdims=True)
                on = dlt * lax.rsqrt(jnp.mean(dlt * dlt, axis=-1, keepdims=True) + NORM_EPS)
                gt = gate_ref[rows, cols]
                y_ref[rows, cols] = (gt * _sigmoid(gt) * (on * ggn_ref[:, cols])).astype(_BF)
            r_scr[hh] = state

    return pl.pallas_call(
        body, name="retention_fwd", grid=(nc // cps,),
        in_specs=[blk, blk, blk, blk, _full_spec((1, RET_W))] + _head_specs(),
        out_specs=[blk, blk, pl.BlockSpec((N_HEAD, cps, HEAD_D, HEAD_D), lambda n: (0, n, 0, 0))],
        out_shape=[jax.ShapeDtypeStruct((L, RET_W), _F32), jax.ShapeDtypeStruct((L, RET_W), _BF),
                   jax.ShapeDtypeStruct((N_HEAD, nc, HEAD_D, HEAD_D), _F32)],
        scratch_shapes=[pltpu.VMEM((N_HEAD, HEAD_D, HEAD_D), _F32)],
        compiler_params=_params("arbitrary"),
    )(q, k, v, gate, ggn, *consts)


def _rows_to_segments(dst_scr, src_ref, seg):
    for g in range(dst_scr.shape[0]):
        for j in range(SUBLANES):
            dst_scr[g, pl.ds(j, seg, stride=SUBLANES), :] = src_ref[pl.ds(j * seg, seg), g * LANES:(g + 1) * LANES]


def _segments_to_rows(dst_ref, src_scr, seg):
    for g in range(src_scr.shape[0]):
        for j in range(SUBLANES):
            dst_ref[pl.ds(j * seg, seg), g * LANES:(g + 1) * LANES] = src_scr[g, pl.ds(j, seg, stride=SUBLANES), :].astype(dst_ref.dtype)


def _scan_segments(x_ref, tab_ref, pw_ref, carry_ref, seg, reverse, entry_ref=None, fwd_ref=None, fwd_entry_ref=None,
                   da_ref=None):
    G = x_ref.shape[0]
    W = KB_STATES
    re, im = pl.ds(0, W), pl.ds(W, W)
    row_id = lax.broadcasted_iota(jnp.int32, (SUBLANES, W), 0)
    edge_in = (row_id == SUBLANES - 1) if reverse else (row_id == 0)
    edge_out = 0 if reverse else SUBLANES - 1
    a_tab = [(tab_ref[g, 0], tab_ref[g, 1]) for g in range(G)]

    def local(i, st):
        r = (seg - 1 - i) if reverse else i
        out = []
        for g in range(G):
            (ar, ai), (sr, si) = a_tab[g], st[g]
            nr = ar * sr - ai * si + x_ref[g, r, :, re]
            ni = ar * si + ai * sr + x_ref[g, r, :, im]
            x_ref[g, r, :, re] = nr
            x_ref[g, r, :, im] = ni
            out.append((nr, ni))
        return tuple(out)

    zero = jnp.zeros((SUBLANES, W), _F32)
    ends = lax.fori_loop(0, seg, local, tuple((zero, zero) for _ in range(G)), unroll=SCAN_UNROLL)

    entry = []
    shift = (SUBLANES - 1) if reverse else 1
    for g in range(G):
        er, ei = ends[g]
        fr = jnp.where(edge_in, carry_ref[g, :, re], pltpu.roll(er, shift, 0))
        fi = jnp.where(edge_in, carry_ref[g, :, im], pltpu.roll(ei, shift, 0))
        for j, dist in enumerate((1, 2, 4)):
            pr, pi = tab_ref[g, 2 + 2 * j], tab_ref[g, 3 + 2 * j]
            sh = (SUBLANES - dist) if reverse else dist
            sr, si = pltpu.roll(fr, sh, 0), pltpu.roll(fi, sh, 0)
            fr, fi = fr + pr * sr - pi * si, fi + pr * si + pi * sr
        br, bi = tab_ref[g, 8], tab_ref[g, 9]
        outr = br * fr - bi * fi + er
        outi = br * fi + bi * fr + ei
        carry_ref[g, :, re] = jnp.broadcast_to(outr[edge_out:edge_out + 1, :], (SUBLANES, W))
        carry_ref[g, :, im] = jnp.broadcast_to(outi[edge_out:edge_out + 1, :], (SUBLANES, W))
        entry.append((fr, fi))
        if entry_ref is not None:
            entry_ref[g, :, re] = fr
            entry_ref[g, :, im] = fi

    add_da = da_ref is not None

    def fix(r, st, first=False):
        out = []
        for g in range(G):
            fr, fi = entry[g]
            pwr, pwi = pw_ref[g, r, :, re], pw_ref[g, r, :, im]
            xr = x_ref[g, r, :, re] + (pwr * fr - pwi * fi)
            xi = x_ref[g, r, :, im] + (pwr * fi + pwi * fr)
            x_ref[g, r, :, re] = xr
            x_ref[g, r, :, im] = xi
            if add_da:
                prev = fwd_entry_ref.at[g] if first else fwd_ref.at[g, r - 1]
                xpr, xpi = prev[:, re], prev[:, im]
                out.append((st[g][0] + (xr * xpr + xi * xpi), st[g][1] + (xi * xpr - xr * xpi)))
            else:
                out.append(st[g])
        return tuple(out)

    if add_da:
        st = fix(0, tuple((zero, zero) for _ in range(G)), first=True)
        st = lax.fori_loop(1, seg, fix, st, unroll=SCAN_UNROLL)
        for g in range(G):
            da_ref[g, :, re] += st[g][0]
            da_ref[g, :, im] += st[g][1]
    else:
        lax.fori_loop(0, seg, fix, tuple((zero[0:1, 0:LANES],) for _ in range(G)), unroll=FIX_UNROLL)


def _s5_specs(seg, time=lambda t: t):
    G = KB_PER_STEP
    return dict(
        x=pl.BlockSpec((G, seg, SUBLANES, 2 * KB_STATES), lambda kb, t: (kb, time(t), 0, 0)),
        ent=pl.BlockSpec((G, 1, SUBLANES, 2 * KB_STATES), lambda kb, t: (kb, time(t), 0, 0)),
        b=pl.BlockSpec((G, LANES, 2 * KB_STATES), lambda kb, t: (kb, 0, 0)),
        c=pl.BlockSpec((G, 2 * KB_STATES, LANES), lambda kb, t: (kb, 0, 0)),
        tab=pl.BlockSpec((G, 10, SUBLANES, KB_STATES), lambda kb, t: (kb, 0, 0, 0)),
        pw=pl.BlockSpec((G, seg, 1, 2 * KB_STATES), lambda kb, t: (kb, 0, 0, 0)),
        d=pl.BlockSpec((1, G * LANES), lambda kb, t: (0, kb)),
    )


def _s5_fwd(u, bmat, cmat, tab_f, pw_f, d_skip, tb):
    L = u.shape[0]
    nt = L // tb
    seg = tb // SUBLANES
    G = KB_PER_STEP
    ucol = pl.BlockSpec((tb, G * LANES), lambda kb, t: (t, kb))
    sp = _s5_specs(seg)

    def body(u_ref, b_ref, c_ref, tab_ref, pw_ref, d_ref, s_ref, x_ref, ent_ref, up_scr, y_scr, carry_scr):
        @pl.when(pl.program_id(1) == 0)
        def _():
            carry_scr[...] = jnp.zeros_like(carry_scr)

        _rows_to_segments(up_scr, u_ref, seg)
        for g in range(G):
            x_ref[g] = _dot(up_scr[g].astype(_BF), b_ref[g]).reshape(seg, SUBLANES, 2 * KB_STATES)
        _scan_segments(x_ref, tab_ref, pw_ref, carry_scr, seg, reverse=False, entry_ref=ent_ref.at[:, 0])
        for g in range(G):
            y = _dot(x_ref[g].reshape(tb, 2 * KB_STATES).astype(_BF), c_ref[g])
            y_scr[g] = y + d_ref[:, g * LANES:(g + 1) * LANES] * up_scr[g]
        _segments_to_rows(s_ref, y_scr, seg)

    return pl.pallas_call(
        body, name="s5_fwd", grid=(N_KB // G, nt),
        in_specs=[ucol, sp["b"], sp["c"], sp["tab"], sp["pw"], sp["d"]],
        out_specs=[ucol, sp["x"], sp["ent"]],
        out_shape=[jax.ShapeDtypeStruct((L, SSM_W), _F32),
                   jax.ShapeDtypeStruct((N_KB, L // SUBLANES, SUBLANES, 2 * KB_STATES), _F32),
                   jax.ShapeDtypeStruct((N_KB, nt, SUBLANES, 2 * KB_STATES), _F32)],
        scratch_shapes=[pltpu.VMEM((G, tb, LANES), _F32)] * 2 + [pltpu.VMEM((G, SUBLANES, 2 * KB_STATES), _F32)],
        compiler_params=_params("parallel", "arbitrary"),
    )(u, bmat, cmat, tab_f, pw_f, d_skip)


def _mixout_fwd(s, y_ret, x, w_glu, w_out, g2, tm):
    L = s.shape[0]
    steps = L // tm

    def body(s_hbm, yr_ref, x_hbm, wg_ref, wo_ref, g_ref, ys_ref, gl_ref, mix_ref, x2_ref, cat_scr,
             s_ring, x_ring, sems):
        i = pl.program_id(0)

        def fetch(t):
            slot = lax.rem(t, 3)
            return (pltpu.make_async_copy(s_hbm.at[pl.ds(t * tm, tm), :], s_ring.at[slot], sems.at[0, slot]),
                    pltpu.make_async_copy(x_hbm.at[pl.ds(t * tm, tm), :], x_ring.at[slot], sems.at[1, slot]))

        @pl.when(i == 0)
        def _():
            for t in range(min(2, steps)):
                for cp in fetch(t):
                    cp.start()

        @pl.when(i + 2 < steps)
        def _():
            for cp in fetch(i + 2):
                cp.start()

        for cp in fetch(i):
            cp.wait()
        s_ref, x_ref = s_ring.at[lax.rem(i, 3)], x_ring.at[lax.rem(i, 3)]
        for rows in _row_chunks(tm):
            ys = _gelu(s_ref[rows, :]).astype(_BF)
            ys_ref[rows, :] = ys
            glu = _dot(ys, wg_ref[...])
            gl = (glu[:, :SSM_W] * _sigmoid(glu[:, SSM_W:])).astype(_BF)
            gl_ref[rows, :] = gl
            cat_scr[rows, :RET_W] = yr_ref[rows, :]
            cat_scr[rows, RET_W:] = gl
            mix = _dot(cat_scr[rows, :], wo_ref[...])
            mix_ref[rows, :] = mix.astype(_BF)
            x2_ref[rows, :] = x_ref[rows, :] + mix * _rms_r(mix) * g_ref[...]

    return pl.pallas_call(
        body, name="mixout_fwd", grid=(L // tm,),
        in_specs=[pl.BlockSpec(memory_space=pl.ANY), _row_spec(tm, RET_W), pl.BlockSpec(memory_space=pl.ANY),
                  _weight_spec((SSM_W, 2 * SSM_W)), _weight_spec((D_MODEL, D_MODEL)), _full_spec((1, D_MODEL))],
        out_specs=[_row_spec(tm, SSM_W), _row_spec(tm, SSM_W), _row_spec(tm, D_MODEL), _row_spec(tm, D_MODEL)],
        out_shape=[jax.ShapeDtypeStruct((L, SSM_W), _BF), jax.ShapeDtypeStruct((L, SSM_W), _BF),
                   jax.ShapeDtypeStruct((L, D_MODEL), _BF), jax.ShapeDtypeStruct((L, D_MODEL), _F32)],
        scratch_shapes=[pltpu.VMEM((tm, D_MODEL), _BF), pltpu.VMEM((3, tm, SSM_W), _F32),
                        pltpu.VMEM((3, tm, D_MODEL), _F32), pltpu.SemaphoreType.DMA((2, 3))],
        compiler_params=_params("arbitrary"),
    )(s, y_ret, x, w_glu, w_out, g2)


FF1_COLS = D_FF // N_DEV


def _ff1_fwd(x2, g3, w1, tm):
    L = x2.shape[0]

    def body(x_ref, g_ref, w_ref, h_ref, a_ref):
        for rows in _row_chunks(tm):
            xv = x_ref[rows, :]
            h = (xv * _rms_r(xv) * g_ref[...]).astype(_BF)
            h_ref[rows, :] = h
            for j in range(N_DEV):
                cols = slice(j * FF1_COLS, (j + 1) * FF1_COLS)
                rl = jnp.maximum(_dot(h, w_ref[j]), 0.0)
                a_ref[rows, cols] = (rl * rl).astype(_BF)

    return pl.pallas_call(
        body, name="ff1_fwd", grid=(L // tm,),
        in_specs=[_row_spec(tm, D_MODEL), _full_spec((1, D_MODEL)), _weight_spec((N_DEV, D_MODEL, FF1_COLS))],
        out_specs=[_row_spec(tm, D_MODEL), _row_spec(tm, D_FF)],
        out_shape=[jax.ShapeDtypeStruct((L, D_MODEL), _BF), jax.ShapeDtypeStruct((L, D_FF), _BF)],
        compiler_params=_params("parallel"),
    )(x2, g3, w1)


def _ff2_loss(act, x2, tgt, g4, w2, tm):
    L = act.shape[0]

    def body(f_ref, x_ref, t_ref, g_ref, w_ref, dy_ref, dm_ref, dg_ref, ls_ref):
        @pl.when(pl.program_id(0) == 0)
        def _():
            dg_ref[...] = jnp.zeros_like(dg_ref)
            ls_ref[...] = jnp.zeros_like(ls_ref)

        g = g_ref[...]
        for rows in _row_chunks(tm):
            m = _dot(f_ref[rows, :], w_ref[...])
            y = x_ref[rows, :] + m * _rms_r(m) * g
            err = y - t_ref[rows, :]
            ls_ref[...] += jnp.sum(err * err, axis=0, keepdims=True)
            dy = err * (1.0 / D_MODEL)
            dy_ref[rows, :] = dy
            dm, dgr = _rms_bwd(m, g, dy)
            dm_ref[rows, :] = dm.astype(_BF)
            dg_ref[...] += jnp.sum(dgr, axis=0, keepdims=True)

    return pl.pallas_call(
        body, name="ff2_loss", grid=(L // tm,),
        in_specs=[_row_spec(tm, D_FF), _row_spec(tm, D_MODEL), _row_spec(tm, D_MODEL),
                  _full_spec((1, D_MODEL)), _weight_spec((D_FF, D_MODEL))],
        out_specs=[_row_spec(tm, D_MODEL), _row_spec(tm, D_MODEL), _full_spec((1, D_MODEL)), _full_spec((1, D_MODEL))],
        out_shape=[jax.ShapeDtypeStruct((L, D_MODEL), _F32), jax.ShapeDtypeStruct((L, D_MODEL), _BF),
                   jax.ShapeDtypeStruct((1, D_MODEL), _F32), jax.ShapeDtypeStruct((1, D_MODEL), _F32)],
        compiler_params=_params("arbitrary"),
    )(act, x2, tgt, g4, w2)


def _mlp_fwd_loss(x2, tgt, g3, g4, w1, w2, tm):
    L = x2.shape[0]

    def body(x_ref, t_ref, g3_ref, g4_ref, w1_ref, w2_ref, h_ref, a_ref, dy_ref, dm_ref, dg_ref, ls_ref):
        @pl.when(pl.program_id(0) == 0)
        def _():
            dg_ref[...] = jnp.zeros_like(dg_ref)
            ls_ref[...] = jnp.zeros_like(ls_ref)

        g = g4_ref[...]
        for rows in _row_chunks(tm):
            xv = x_ref[rows, :]
            h = (xv * _rms_r(xv) * g3_ref[...]).astype(_BF)
            h_ref[rows, :] = h
            for j in range(N_DEV):
                cols = slice(j * FF1_COLS, (j + 1) * FF1_COLS)
                rl = jnp.maximum(_dot(h, w1_ref[j]), 0.0)
                a_ref[rows, cols] = (rl * rl).astype(_BF)
            m = _dot(a_ref[rows, :], w2_ref[...])
            y = x_ref[rows, :] + m * _rms_r(m) * g
            err = y - t_ref[rows, :]
            ls_ref[...] += jnp.sum(err * err, axis=0, keepdims=True)
            dy = err * (1.0 / D_MODEL)
            dy_ref[rows, :] = dy
            dm, dgr = _rms_bwd(m, g, dy)
            dm_ref[rows, :] = dm.astype(_BF)
            dg_ref[...] += jnp.sum(dgr, axis=0, keepdims=True)

    vec = _full_spec((1, D_MODEL))
    return pl.pallas_call(
        body, name="mlp_fwd_loss", grid=(L // tm,),
        in_specs=[_row_spec(tm, D_MODEL), _row_spec(tm, D_MODEL), vec, vec,
                  _weight_spec((N_DEV, D_MODEL, FF1_COLS)), _weight_spec((D_FF, D_MODEL))],
        out_specs=[_row_spec(tm, D_MODEL), _row_spec(tm, D_FF), _row_spec(tm, D_MODEL), _row_spec(tm, D_MODEL), vec, vec],
        out_shape=[jax.ShapeDtypeStruct((L, D_MODEL), _BF), jax.ShapeDtypeStruct((L, D_FF), _BF),
                   jax.ShapeDtypeStruct((L, D_MODEL), _F32), jax.ShapeDtypeStruct((L, D_MODEL), _BF),
                   jax.ShapeDtypeStruct((1, D_MODEL), _F32), jax.ShapeDtypeStruct((1, D_MODEL), _F32)],
        compiler_params=_params("arbitrary"),
    )(x2, tgt, g3, g4, w1, w2)


def _ff2_bwd(dm, act, w2, tm, tn):
    L = dm.shape[0]
    last = L // tm - 1

    def body(dm_ref, a_ref, w_ref, df_ref, dw_ref, acc):
        @pl.when(pl.program_id(1) == 0)
        def _():
            acc[...] = jnp.zeros_like(acc)

        dmv = dm_ref[...]
        av = a_ref[...]
        df_ref[...] = (_dot_nt(dmv, w_ref[...]) * jnp.sqrt(4.0 * av.astype(_F32))).astype(_BF)
        acc[...] += _dot_tn(av, dmv)

        @pl.when(pl.program_id(1) == last)
        def _():
            dw_ref[...] = acc[...].astype(_BF)

    return pl.pallas_call(
        body, name="ff2_bwd", grid=(D_FF // tn, L // tm),
        in_specs=[pl.BlockSpec((tm, D_MODEL), lambda j, i: (i, 0)), pl.BlockSpec((tm, tn), lambda j, i: (i, j)),
                  pl.BlockSpec((tn, D_MODEL), lambda j, i: (j, 0))],
        out_specs=[pl.BlockSpec((tm, tn), lambda j, i: (i, j)), pl.BlockSpec((tn, D_MODEL), lambda j, i: (j, 0))],
        out_shape=[jax.ShapeDtypeStruct((L, D_FF), _BF), jax.ShapeDtypeStruct((D_FF, D_MODEL), _BF)],
        scratch_shapes=[pltpu.VMEM((tn, D_MODEL), _F32)],
        compiler_params=_params("parallel", "arbitrary"),
    )(dm, act, w2)


def _ff1_bwd(df1, w1, x2, mix, dy, g3, g2, tm):
    L = df1.shape[0]
    steps = L // tm

    def body(df_ref, w_ref, x2_hbm, mix_ref, dy_hbm, g3_ref, g2_ref, dx2_ref, dmix_ref, dg3_ref, dg2_ref,
             x2_ring, dy_ring, sems):
        i = pl.program_id(0)

        def fetch(t):
            slot = lax.rem(t, 3)
            return (pltpu.make_async_copy(x2_hbm.at[pl.ds(t * tm, tm), :], x2_ring.at[slot], sems.at[0, slot]),
                    pltpu.make_async_copy(dy_hbm.at[pl.ds(t * tm, tm), :], dy_ring.at[slot], sems.at[1, slot]))

        @pl.when(i == 0)
        def _():
            dg3_ref[...] = jnp.zeros_like(dg3_ref)
            dg2_ref[...] = jnp.zeros_like(dg2_ref)
            for t in range(min(2, steps)):
                for cp in fetch(t):
                    cp.start()

        @pl.when(i + 2 < steps)
        def _():
            for cp in fetch(i + 2):
                cp.start()

        for cp in fetch(i):
            cp.wait()
        x2_ref, dy_ref = x2_ring.at[lax.rem(i, 3)], dy_ring.at[lax.rem(i, 3)]

        for rows in _row_chunks(tm):
            dh = _dot_nt(df_ref[rows, 0:FF1_COLS], w_ref[0])
            for j in range(1, N_DEV):
                dh = dh + _dot_nt(df_ref[rows, j * FF1_COLS:(j + 1) * FF1_COLS], w_ref[j])
            dz, dgr = _rms_bwd(x2_ref[rows, :], g3_ref[...], dh)
            dg3_ref[...] += jnp.sum(dgr, axis=0, keepdims=True)
            dx2 = dy_ref[rows, :] + dz
            dx2_ref[rows, :] = dx2
            dmx, dgr2 = _rms_bwd(mix_ref[rows, :].astype(_F32), g2_ref[...], dx2)
            dg2_ref[...] += jnp.sum(dgr2, axis=0, keepdims=True)
            dmix_ref[rows, :] = dmx.astype(_BF)

    vec = _full_spec((1, D_MODEL))
    return pl.pallas_call(
        body, name="ff1_bwd", grid=(L // tm,),
        in_specs=[_row_spec(tm, D_FF), _weight_spec((N_DEV, D_MODEL, FF1_COLS)), pl.BlockSpec(memory_space=pl.ANY),
                  _row_spec(tm, D_MODEL), pl.BlockSpec(memory_space=pl.ANY), vec, vec],
        out_specs=[_row_spec(tm, D_MODEL), _row_spec(tm, D_MODEL), vec, vec],
        out_shape=[jax.ShapeDtypeStruct((L, D_MODEL), _F32), jax.ShapeDtypeStruct((L, D_MODEL), _BF),
                   jax.ShapeDtypeStruct((1, D_MODEL), _F32), jax.ShapeDtypeStruct((1, D_MODEL), _F32)],
        scratch_shapes=[pltpu.VMEM((3, tm, D_MODEL), _F32), pltpu.VMEM((3, tm, D_MODEL), _F32),
                        pltpu.SemaphoreType.DMA((2, 3))],
        compiler_params=_params("arbitrary"),
    )(df1, w1, x2, mix, dy, g3, g2)


def _matmul_tn(a, b, tm, tn, name, slots=0):
    L, K = a.shape
    N = b.shape[1]
    last = L // tm - 1

    def body(a_ref, b_ref, o_ref, acc):
        @pl.when(pl.program_id(1) == 0)
        def _():
            acc[...] = jnp.zeros_like(acc)

        acc[...] += _dot_tn(a_ref[...].astype(_BF), b_ref[...].astype(_BF))

        @pl.when(pl.program_id(1) == last)
        def _():
            if slots:
                for c in range(tn // slots):
                    o_ref[c] = acc[:, c * slots:(c + 1) * slots].astype(_BF)
            else:
                o_ref[...] = acc[...].astype(_BF)

    if slots:
        out_spec = pl.BlockSpec((tn // slots, K, slots), lambda j, i: (j, 0, 0))
        out_shape = jax.ShapeDtypeStruct((N // slots, K, slots), _BF)
    else:
        out_spec = pl.BlockSpec((K, tn), lambda j, i: (0, j))
        out_shape = jax.ShapeDtypeStruct((K, N), _BF)
    return pl.pallas_call(
        body, name=name, grid=(N // tn, L // tm),
        in_specs=[pl.BlockSpec((tm, K), lambda j, i: (i, 0)), pl.BlockSpec((tm, tn), lambda j, i: (i, j))],
        out_specs=out_spec, out_shape=out_shape,
        scratch_shapes=[pltpu.VMEM((K, tn), _F32)],
        compiler_params=_params("parallel", "arbitrary"),
    )(a, b)


def _dw_out_glu(y_ret, gl, dmix, ys, dglu, tk):
    L = dmix.shape[0]
    last = L // tk - 1

    def body(a0_ref, a1_ref, b_ref, ys_ref, dglu_ref, o_ref, og_ref, acc, acc_g):
        @pl.when(pl.program_id(0) == 0)
        def _():
            acc[...] = jnp.zeros_like(acc)
            acc_g[...] = jnp.zeros_like(acc_g)

        bv = b_ref[...]
        acc[:RET_W, :] += _dot_tn(a0_ref[...], bv)
        acc[RET_W:, :] += _dot_tn(a1_ref[...], bv)
        acc_g[...] += _dot_tn(ys_ref[...], dglu_ref[...])

        @pl.when(pl.program_id(0) == last)
        def _():
            o_ref[...] = acc[...].astype(_BF)
            og_ref[...] = acc_g[...].astype(_BF)

    return pl.pallas_call(
        body, name="dw_out_glu", grid=(L // tk,),
        in_specs=[_row_spec(tk, RET_W), _row_spec(tk, SSM_W), _row_spec(tk, D_MODEL),
                  _row_spec(tk, SSM_W), _row_spec(tk, 2 * SSM_W)],
        out_specs=[_full_spec((D_MODEL, D_MODEL)), _full_spec((SSM_W, 2 * SSM_W))],
        out_shape=[jax.ShapeDtypeStruct((D_MODEL, D_MODEL), _BF), jax.ShapeDtypeStruct((SSM_W, 2 * SSM_W), _BF)],
        scratch_shapes=[pltpu.VMEM((D_MODEL, D_MODEL), _F32), pltpu.VMEM((SSM_W, 2 * SSM_W), _F32)],
        compiler_params=_params("arbitrary"),
    )(y_ret, gl, dmix, ys, dglu)


def _dw_in_t(pieces, h, tk):
    L = h.shape[0]
    last = L // tk - 1

    def body(p0, p1, p2, p3, p4, h_ref, o_ref, acc):
        @pl.when(pl.program_id(0) == 0)
        def _():
            acc[...] = jnp.zeros_like(acc)

        hv = h_ref[...]
        for j, p in enumerate((p0, p1, p2, p3, p4)):
            acc[j * RET_W:(j + 1) * RET_W, :] += _dot_tn(p[...].astype(_BF), hv)

        @pl.when(pl.program_id(0) == last)
        def _():
            o_ref[...] = acc[...].astype(_BF)

    return pl.pallas_call(
        body, name="dw_in", grid=(L // tk,),
        in_specs=[_row_spec(tk, RET_W)] * 5 + [_row_spec(tk, D_MODEL)],
        out_specs=_full_spec((IN_COLS, D_MODEL)), out_shape=jax.ShapeDtypeStruct((IN_COLS, D_MODEL), _BF),
        scratch_shapes=[pltpu.VMEM((IN_COLS, D_MODEL), _F32)],
        compiler_params=_params("arbitrary"),
    )(*pieces, h)


def _mixout_bwd(dmix, w_out, w_glu, ys, s, o, gate, ggn, tm, after=()):
    L = dmix.shape[0]

    def body(dmix_ref, wo_ref, wg_ref, ys_ref, s_ref, o_ref, gate_ref, ggn_ref,
             dglu_ref, ds_ref, dgate_ref, do_ref, dggn_ref):
        @pl.when(pl.program_id(0) == 0)
        def _():
            dggn_ref[...] = jnp.zeros_like(dggn_ref)

        ggn = ggn_ref[...]
        for rows in _row_chunks(tm):
            dcat = _dot_nt(dmix_ref[rows, :], wo_ref[...])
            dy_ret, dy_ssm = dcat[:, :RET_W], dcat[:, RET_W:]
            glu = _dot(ys_ref[rows, :], wg_ref[...])
            ga, sg = glu[:, :SSM_W], _sigmoid(glu[:, SSM_W:])
            dga = (dy_ssm * sg).astype(_BF)
            dgb = (dy_ssm * ga * sg * (1.0 - sg)).astype(_BF)
            dglu_ref[rows, :SSM_W] = dga
            dglu_ref[rows, SSM_W:] = dgb
            dys = _dot_nt(dga, wg_ref[:, :SSM_W]) + _dot_nt(dgb, wg_ref[:, SSM_W:])
            ds_ref[rows, :] = dys * _gelu_grad(s_ref[rows, :])
            gt = gate_ref[rows, :]
            sgt = _sigmoid(gt)
            for hh in range(N_HEAD):
                cols = slice(hh * HEAD_D, (hh + 1) * HEAD_D)
                ov = o_ref[rows, cols]
                dlt = ov - jnp.mean(ov, axis=-1, keepdims=True)
                rstd = lax.rsqrt(jnp.mean(dlt * dlt, axis=-1, keepdims=True) + NORM_EPS)
                on = dlt * rstd
                dyr = dy_ret[:, cols] * (gt[:, cols] * sgt[:, cols])
                dgate_ref[rows, cols] = (dy_ret[:, cols] * (on * ggn[:, cols]) * (sgt[:, cols] * (1.0 + gt[:, cols] * (1.0 - sgt[:, cols])))).astype(_BF)
                dggn_ref[:, cols] += jnp.sum(dyr * on, axis=0, keepdims=True)
                don = dyr * ggn[:, cols]
                do = rstd * (don - jnp.mean(don, axis=-1, keepdims=True) - on * jnp.mean(don * on, axis=-1, keepdims=True))
                do_ref[rows, cols] = do.astype(_BF)

    body, in_specs, operands = _ordered(
        body, [_row_spec(tm, D_MODEL), _weight_spec((D_MODEL, D_MODEL)), _weight_spec((SSM_W, 2 * SSM_W)),
               _row_spec(tm, SSM_W), _row_spec(tm, SSM_W), _row_spec(tm, RET_W), _row_spec(tm, RET_W),
               _full_spec((1, RET_W))], (dmix, w_out, w_glu, ys, s, o, gate, ggn), after)
    return pl.pallas_call(
        body, name="mixout_bwd", grid=(L // tm,),
        in_specs=in_specs,
        out_specs=[_row_spec(tm, 2 * SSM_W), _row_spec(tm, SSM_W), _row_spec(tm, RET_W), _row_spec(tm, RET_W),
                   _full_spec((1, RET_W))],
        out_shape=[jax.ShapeDtypeStruct((L, 2 * SSM_W), _BF), jax.ShapeDtypeStruct((L, SSM_W), _F32),
                   jax.ShapeDtypeStruct((L, RET_W), _BF), jax.ShapeDtypeStruct((L, RET_W), _BF),
                   jax.ShapeDtypeStruct((1, RET_W), _F32)],
        compiler_params=_params("arbitrary"),
    )(*operands)


def _s5_bwd(u, ds, xs, ent, bmat, cmat, tab_r, pw_r, d_skip, tb, after=()):
    L = u.shape[0]
    nt = L // tb
    seg = tb // SUBLANES
    G = KB_PER_STEP
    rcol = pl.BlockSpec((tb, G * LANES), lambda kb, t: (nt - 1 - t, kb))
    sp = _s5_specs(seg, time=lambda t: nt - 1 - t)
    aspec = pl.BlockSpec((G, SUBLANES, 2 * KB_STATES), lambda kb, t: (kb, 0, 0))

    def body(u_ref, ds_ref, x_ref, ent_ref, b_ref, c_ref, tr_ref, pr_ref, d_ref,
             du_ref, db_ref, dc_ref, da_ref, dd_ref, up_scr, dp_scr, g_scr, lc_scr):
        @pl.when(pl.program_id(1) == 0)
        def _():
            lc_scr[...] = jnp.zeros_like(lc_scr)
            db_ref[...] = jnp.zeros_like(db_ref)
            dc_ref[...] = jnp.zeros_like(dc_ref)
            da_ref[...] = jnp.zeros_like(da_ref)
            dd_ref[...] = jnp.zeros_like(dd_ref)

        _rows_to_segments(up_scr, u_ref, seg)
        _rows_to_segments(dp_scr, ds_ref, seg)
        for g in range(G):
            g_scr[g] = _dot_nt(dp_scr[g].astype(_BF), c_ref[g]).reshape(seg, SUBLANES, 2 * KB_STATES)
        _scan_segments(g_scr, tr_ref, pr_ref, lc_scr, seg, reverse=True, fwd_ref=x_ref, fwd_entry_ref=ent_ref.at[:, 0],
                       da_ref=da_ref)
        for g in range(G):
            cols = slice(g * LANES, (g + 1) * LANES)
            uv, dsv = up_scr[g], dp_scr[g]
            ub, dsb = uv.astype(_BF), dsv.astype(_BF)
            lamb = g_scr[g].reshape(tb, 2 * KB_STATES).astype(_BF)
            db_ref[g] += _dot_tn(ub, lamb)
            dc_ref[g] += _dot_tn(dsb, x_ref[g].reshape(tb, 2 * KB_STATES).astype(_BF))
            dd_ref[:, cols] += jnp.sum(dsv * uv, axis=0, keepdims=True)
            up_scr[g] = _dot_nt(lamb, b_ref[g]) + d_ref[:, cols] * dsv
        _segments_to_rows(du_ref, up_scr, seg)

    body, in_specs, operands = _ordered(
        body, [rcol, rcol, sp["x"], sp["ent"], sp["b"], sp["c"], sp["tab"], sp["pw"], sp["d"]],
        (u, ds, xs, ent, bmat, cmat, tab_r, pw_r, d_skip), after)
    return pl.pallas_call(
        body, name="s5_bwd", grid=(N_KB // G, nt),
        in_specs=in_specs,
        out_specs=[rcol, sp["b"], sp["b"], aspec, sp["d"]],
        out_shape=[jax.ShapeDtypeStruct((L, SSM_W), _BF),
                   jax.ShapeDtypeStruct((N_KB, LANES, 2 * KB_STATES), _F32),
                   jax.ShapeDtypeStruct((N_KB, LANES, 2 * KB_STATES), _F32),
                   jax.ShapeDtypeStruct((N_KB, SUBLANES, 2 * KB_STATES), _F32),
                   jax.ShapeDtypeStruct((1, SSM_W), _F32)],
        scratch_shapes=[pltpu.VMEM((G, tb, LANES), _F32)] * 2
        + [pltpu.VMEM((G, seg, SUBLANES, 2 * KB_STATES), _F32), pltpu.VMEM((G, SUBLANES, 2 * KB_STATES), _F32)],
        compiler_params=_params("parallel", "arbitrary"),
    )(*operands)


def _retention_bwd(q, k, v, do, r_prev, consts, cosf, sinf, after=()):
    L = q.shape[0]
    nc = L // CHUNK
    cps = math.gcd(RET_STEP_CHUNKS, nc)
    nb = nc // cps
    blk = pl.BlockSpec((cps * CHUNK, RET_W), lambda n: (nb - 1 - n, 0))
    rope_blk = pl.BlockSpec((cps * CHUNK, HEAD_D), lambda n: (nb - 1 - n, 0))

    def body(q_ref, k_ref, v_ref, do_ref, rp_ref, dm_ref, xi_ref, zeta_ref, gc_ref, cos_ref, sin_ref,
             dq_ref, dk_ref, dv_ref, g_scr):
        @pl.when(pl.program_id(0) == 0)
        def _():
            g_scr[...] = jnp.zeros_like(g_scr)

        for hh in range(N_HEAD):
            cols = slice(hh * HEAD_D, (hh + 1) * HEAD_D)
            dm, zeta = dm_ref[hh], zeta_ref[hh]
            gst = g_scr[hh]
            for c in reversed(range(cps)):
                rows = slice(c * CHUNK, (c + 1) * CHUNK)
                qv, kv, vv, dov = q_ref[rows, cols], k_ref[rows, cols], v_ref[rows, cols], do_ref[rows, cols]
                rb = rp_ref[hh, c].astype(_BF)
                gb = gst.astype(_BF)
                sb = (_dot_nt(qv, kv) * dm).astype(_BF)
                dab = (_dot_nt(dov, vv) * dm).astype(_BF)
                dox = (dov.astype(_F32) * xi_ref[hh]).astype(_BF)
                vz = (vv.astype(_F32) * zeta).astype(_BF)
                dq = _dot(dab, kv) + _dot_nt(dox, rb)
                dk = _dot_tn(dab, qv) + _dot_nt(vz, gb)
                dv = _dot_tn(sb, dov) + _dot(kv, gb) * zeta
                gst = gc_ref[hh, 0:1, :] * gst + _dot_tn(qv, dox)
                cs, sn = cos_ref[rows, :], sin_ref[rows, :]
                dq_ref[rows, cols] = _rope_t(dq, cs, sn).astype(_BF)
                dk_ref[rows, cols] = (_rope_t(dk, cs, sn) * (HEAD_D ** -0.5)).astype(_BF)
                dv_ref[rows, cols] = dv.astype(_BF)
            g_scr[hh] = gst

    body, in_specs, operands = _ordered(
        body, [blk, blk, blk, blk, pl.BlockSpec((N_HEAD, cps, HEAD_D, HEAD_D), lambda n: (0, nb - 1 - n, 0, 0))]
        + _head_specs() + [rope_blk, rope_blk], (q, k, v, do, r_prev, *consts, cosf, sinf), after)
    return pl.pallas_call(
        body, name="retention_bwd", grid=(nb,),
        in_specs=in_specs,
        out_specs=[blk, blk, blk],
        out_shape=[jax.ShapeDtypeStruct((L, RET_W), _BF)] * 3,
        scratch_shapes=[pltpu.VMEM((N_HEAD, HEAD_D, HEAD_D), _F32)],
        compiler_params=_params("arbitrary"),
    )(*operands)


def _inproj_bwd(pieces, w_in_t, x, dx2, g1, tm, after=()):
    L = x.shape[0]

    def body(p0, p1, p2, p3, p4, w_ref, x_ref, dx2_ref, g_ref, dx_ref, dg_ref):
        @pl.when(pl.program_id(0) == 0)
        def _():
            dg_ref[...] = jnp.zeros_like(dg_ref)

        for rows in _row_chunks(tm):
            dh = None
            for j, p in enumerate((p0, p1, p2, p3, p4)):
                part = _dot(p[rows, :].astype(_BF), w_ref[j * RET_W:(j + 1) * RET_W, :])
                dh = part if dh is None else dh + part
            dz, dgr = _rms_bwd(x_ref[rows, :], g_ref[...], dh)
            dx_ref[rows, :] = dx2_ref[rows, :] + dz
            dg_ref[...] += jnp.sum(dgr, axis=0, keepdims=True)

    body, in_specs, operands = _ordered(
        body, [_row_spec(tm, RET_W)] * 5 + [_weight_spec((IN_COLS, D_MODEL)), _row_spec(tm, D_MODEL),
                                             _row_spec(tm, D_MODEL), _full_spec((1, D_MODEL))],
        (*pieces, w_in_t, x, dx2, g1), after)
    return pl.pallas_call(
        body, name="inproj_bwd", grid=(L // tm,),
        in_specs=in_specs,
        out_specs=[_row_spec(tm, D_MODEL), _full_spec((1, D_MODEL))],
        out_shape=[jax.ShapeDtypeStruct((L, D_MODEL), _F32), jax.ShapeDtypeStruct((1, D_MODEL), _F32)],
        compiler_params=_params("arbitrary"),
    )(*operands)


def _sum_adamw(parts, w, m, v, tr, name):
    _, R, Cc = parts.shape

    def body(p_ref, w_ref, m_ref, v_ref, g_ref, d_ref, nm_ref, nv_ref):
        gv = p_ref[0].astype(_F32)
        for s in range(1, N_DEV):
            gv = gv + p_ref[s].astype(_F32)
        g_ref[...] = gv
        nm = ADAM_B1 * m_ref[...] + (1.0 - ADAM_B1) * gv
        nv = ADAM_B2 * v_ref[...] + (1.0 - ADAM_B2) * (gv * gv)
        m_hat = nm / (1.0 - ADAM_B1 ** ADAM_STEP)
        v_hat = nv / (1.0 - ADAM_B2 ** ADAM_STEP)
        d_ref[...] = -ADAM_LR * (m_hat / (jnp.sqrt(v_hat) + ADAM_EPS) + ADAM_WD * w_ref[...])
        nm_ref[...] = nm
        nv_ref[...] = nv

    spec = _row_spec(tr, Cc)
    return pl.pallas_call(
        body, name=name, grid=(R // tr,),
        in_specs=[pl.BlockSpec((N_DEV, tr, Cc), lambda i: (0, i, 0))] + [spec] * 3, out_specs=[spec] * 4,
        out_shape=[jax.ShapeDtypeStruct((R, Cc), _F32)] * 4,
        compiler_params=_params("parallel"),
    )(parts, w, m, v)


def _my_place():
    return lax.axis_index("x"), lax.axis_index("y"), lax.axis_index("c")


HBM_SPEC = pl.BlockSpec(memory_space=pltpu.HBM)
SEM_SPEC = pl.BlockSpec(memory_space=pltpu.SEMAPHORE)
DATAFLOW = pltpu.SideEffectType.DATAFLOW_SIDE_EFFECTING


def _my_index():
    x, y, c = _my_place()
    return 4 * x + 2 * y + c


def _landing(own_block):
    return lax.empty((N_DEV,) + own_block.shape, own_block.dtype)


def _own_copy(src, land, sems, a, gather):
    me = _my_index()
    return pltpu.make_async_copy(src if gather else src.at[me], land.at[me], sems.at[sems.shape[0] // N_DEV * 7 + a])


def _split_copies(src_refs, land_refs, send_sems, recv_sems, gather, first=0):
    x, y, c = _my_place()
    me = 4 * x + 2 * y + c
    copies = []
    for a, (src, land) in enumerate(zip(src_refs, land_refs)):
        for kk in range(1, N_DEV):
            px, py, pc = x ^ (kk >> 2), y ^ ((kk >> 1) & 1), c ^ (kk & 1)
            peer = 4 * px + 2 * py + pc
            copies.append(pltpu.make_async_remote_copy(
                src_ref=src if gather else src.at[peer], dst_ref=land.at[me],
                send_sem=send_sems.at[(first + a) * 7 + kk - 1], recv_sem=recv_sems.at[(first + a) * 7 + kk - 1],
                device_id=(px, py, pc), device_id_type=MESH))
    return copies


def _split_start(srcs, lands, gather, name):
    n = len(srcs)

    def body(*refs):
        src_refs, land_refs = refs[:n], refs[n:2 * n]
        send_sems, recv_sems = refs[2 * n], refs[2 * n + 1]
        token = refs[-1]
        for cp in _split_copies(src_refs, land_refs, send_sems, recv_sems, gather):
            cp.start()
        for a in range(n):
            _own_copy(src_refs[a], land_refs[a], send_sems, a, gather).start()
        token[...] = jnp.zeros_like(token)

    outs = pl.pallas_call(
        body, name=name,
        out_shape=(pltpu.SemaphoreType.DMA((N_DEV * n,)), pltpu.SemaphoreType.DMA((7 * n,)),
                   *[pltpu.HBM(t.shape, t.dtype) for t in srcs], *[pltpu.HBM(t.shape, t.dtype) for t in lands],
                   jax.ShapeDtypeStruct((SUBLANES, LANES), _F32)),
        in_specs=[HBM_SPEC] * (2 * n),
        out_specs=(SEM_SPEC, SEM_SPEC, *[HBM_SPEC] * (2 * n), pl.BlockSpec(memory_space=pltpu.VMEM)),
        input_output_aliases={i: 2 + i for i in range(2 * n)},
        compiler_params=pltpu.CompilerParams(has_side_effects=DATAFLOW),
    )(*[pltpu.with_memory_space_constraint(t, pltpu.HBM) for t in list(srcs) + list(lands)])
    return outs[0], outs[1], outs[2:2 + n], outs[2 + n:2 + 2 * n], outs[-1]


def _split_wait(send_sems, recv_sems, srcs, lands, after, gather, name, first=0):
    n = len(srcs)

    def body(*refs):
        src_refs, land_refs = refs[:n], refs[n:2 * n]
        send_s, recv_s = refs[2 * n], refs[2 * n + 1]
        for cp in _split_copies(src_refs, land_refs, send_s, recv_s, gather, first):
            cp.wait_send()
            cp.wait_recv()
        for a in range(n):
            _own_copy(src_refs[a], land_refs[a], send_s, first + a, gather).wait()

    outs = pl.pallas_call(
        body, name=name,
        out_shape=tuple(pltpu.HBM(t.shape, t.dtype) for t in list(srcs) + list(lands)),
        in_specs=[HBM_SPEC] * (2 * n) + [SEM_SPEC, SEM_SPEC, pl.BlockSpec(memory_space=pl.ANY)],
        out_specs=tuple([HBM_SPEC] * (2 * n)),
        input_output_aliases={i: i for i in range(2 * n)},
        compiler_params=pltpu.CompilerParams(has_side_effects=DATAFLOW),
    )(*srcs, *lands, send_sems, recv_sems, after)
    return outs[n:]


def _discretize(lam_re, lam_im, log_dt, b_re, b_im):
    lr = jnp.minimum(lam_re, -1e-4)
    li = lam_im
    dt = jnp.exp(log_dt)[:, None]
    er = jnp.exp(lr * dt)
    ar, ai = er * jnp.cos(li * dt), er * jnp.sin(li * dt)
    den = lr * lr + li * li
    cr = ((ar - 1.0) * lr + ai * li) / den
    ci = (ai * lr - (ar - 1.0) * li) / den
    bbr = cr[:, :, None] * b_re - ci[:, :, None] * b_im
    bbi = cr[:, :, None] * b_im + ci[:, :, None] * b_re
    return ar, ai, bbr, bbi


def _cmul(ar, ai, br, bi):
    return ar * br - ai * bi, ar * bi + ai * br


def _cpowers(ar, ai, n):
    pr, pi = ar[None], ai[None]
    while pr.shape[0] < n:
        nr, ni = _cmul(pr, pi, pr[-1][None], pi[-1][None])
        pr, pi = jnp.concatenate([pr, nr]), jnp.concatenate([pi, ni])
    return pr[:n], pi[:n]


def _scan_tables(ar, ai, seg, reverse):
    if reverse:
        ai = -ai
    ar, ai = ar.reshape(N_KB, KB_STATES), ai.reshape(N_KB, KB_STATES)
    pr, pi = _cpowers(ar, ai, seg)
    a1 = (pr[-1], pi[-1])
    a2 = _cmul(*a1, *a1)
    a4 = _cmul(*a2, *a2)
    row = jnp.arange(SUBLANES)[None, :, None]
    wide = lambda t: jnp.broadcast_to(t[:, None, :], (N_KB, SUBLANES, KB_STATES))
    tabs = [wide(ar), wide(ai)]
    for dist, (qr, qi) in ((1, a1), (2, a2), (4, a4)):
        keep = (row < SUBLANES - dist) if reverse else (row >= dist)
        tabs += [jnp.where(keep, wide(qr), 0.0), jnp.where(keep, wide(qi), 0.0)]
    tabs += [wide(a1[0]), wide(a1[1])]
    if reverse:
        pr, pi = pr[::-1], pi[::-1]
    pw = jnp.transpose(jnp.concatenate([pr, pi], axis=-1), (1, 0, 2))[:, :, None, :]
    return jnp.stack(tabs, axis=1).astype(_F32), pw.astype(_F32)


def _block_diag_in(br, bi):
    eye = jnp.eye(GROUPS_PER_KB, dtype=_F32)
    one = lambda t: jnp.einsum("kgpc,gh->kgchp", t.reshape(N_KB, GROUPS_PER_KB, N_STATE, SSM_GC), eye).reshape(
        N_KB, LANES, KB_STATES)
    return jnp.concatenate([one(br), one(bi)], axis=-1)


def _block_diag_in_t(dmat):
    d6 = dmat.reshape(N_KB, GROUPS_PER_KB, SSM_GC, 2, GROUPS_PER_KB, N_STATE)
    eye = jnp.eye(GROUPS_PER_KB, dtype=_F32)
    both = jnp.einsum("kgcrhp,gh->rkgpc", d6, eye).reshape(2, N_GROUP, N_STATE, SSM_GC)
    return both[0], both[1]


def _block_diag_out(c_re, c_im):
    eye = jnp.eye(GROUPS_PER_KB, dtype=_F32)
    one = lambda t: jnp.einsum("kgcp,gh->khpgc", t.reshape(N_KB, GROUPS_PER_KB, SSM_GC, N_STATE), eye).reshape(
        N_KB, KB_STATES, LANES)
    return jnp.concatenate([one(c_re), -one(c_im)], axis=1)


def _block_diag_out_t(dmat_t):
    d6 = dmat_t.reshape(N_KB, GROUPS_PER_KB, SSM_GC, 2, GROUPS_PER_KB, N_STATE)
    eye = jnp.eye(GROUPS_PER_KB, dtype=_F32)
    both = jnp.einsum("kgcrhp,gh->rkgcp", d6, eye).reshape(2, N_GROUP, SSM_GC, N_STATE)
    return both[0], -both[1]


SMALL_NAMES = ("norm_mix_pre", "norm_mix_post", "ret_gn_gain", "ssm_lambda_re", "ssm_lambda_im", "ssm_log_dt",
               "ssm_b_re", "ssm_b_im", "ssm_c_re", "ssm_c_im", "ssm_d", "norm_mlp_pre", "norm_mlp_post")


def _local_grads(x, tgt, small, weights, emit, emit_small, tm, tk, tb, zero=0.0):
    L = x.shape[0]
    g1, g2, ggn = small["norm_mix_pre"], small["norm_mix_post"], small["ret_gn_gain"]
    g3, g4, d_skip = small["norm_mlp_pre"], small["norm_mlp_post"], small["ssm_d"]

    rope = _rope_tables(L)
    consts = _ret_consts()

    disc_in = (small["ssm_lambda_re"][0], small["ssm_lambda_im"][0], small["ssm_log_dt"][0] + zero,
               small["ssm_b_re"][0], small["ssm_b_im"][0])
    (ar, ai, bbr, bbi), disc_vjp = jax.vjp(_discretize, *disc_in)
    bmat = _block_diag_in(bbr, bbi).astype(_BF)
    cmat = _block_diag_out(small["ssm_c_re"][0], small["ssm_c_im"][0]).astype(_BF)
    seg = tb // SUBLANES
    tab_f, pw_f = _scan_tables(ar, ai, seg, False)
    tab_r, pw_r = _scan_tables(ar, ai, seg, True)

    h1 = _prenorm(x, g1, min(4 * tm, L), after=(pw_r,))
    (w_in_t,) = weights("in", h1)
    q, k, v, gate, u, cosf, sinf = _inproj_fwd(h1, w_in_t, rope, min(4 * tm, L))
    o, y_ret, r_prev = _retention_fwd(q, k, v, gate, ggn, consts)
    s, xs, ent = _s5_fwd(u, bmat, cmat, tab_f, pw_f, d_skip, tb)
    w_glu, w_out = weights("mix", s)
    ys, gl, mix, x2 = _mixout_fwd(s, y_ret, x, w_glu, w_out, g2, min(2 * tm, L))
    w_ff1, w_ff2 = weights("mlp", x2)
    h3, act, dy, dm, dg4, sq = _mlp_fwd_loss(x2, tgt, g3, g4, w_ff1, w_ff2, min(2 * tm, L))

    df1, dw_ff2 = _ff2_bwd(dm, act, w_ff2, min(1024, L), 1024)
    dx2, dmix, dg3, dg2 = _ff1_bwd(df1, w_ff1, x2, mix, dy, g3, g2, min(2 * tm, L))
    dw_ff1 = _matmul_tn(h3, df1, tk, 2 * FF1_COLS, "dw_ff1", slots=FF1_COLS)
    token = emit({"w_ff1": dw_ff1, "w_ff2": dw_ff2})
    dglu, ds, dgate, do, dggn = _mixout_bwd(dmix, w_out, w_glu, ys, s, o, gate, ggn, min(2 * tm, L), after=token)
    dw_out, dw_glu = _dw_out_glu(y_ret, gl, dmix, ys, dglu, tk)
    token = emit({"w_glu": dw_glu, "w_out": dw_out})
    du, dbmat, dcmat, da8, dd = _s5_bwd(u, ds, xs, ent, bmat, cmat, tab_r, pw_r, d_skip, tb, after=token)

    da = jnp.sum(da8, axis=1)
    dar = da[:, :KB_STATES].reshape(N_GROUP, N_STATE)
    dai = da[:, KB_STATES:].reshape(N_GROUP, N_STATE)
    dbr, dbi = _block_diag_in_t(dbmat)
    dlre, dlim, dldt, dbre, dbim = disc_vjp((dar, dai, dbr, dbi))
    dcre, dcim = _block_diag_out_t(dcmat)
    token = emit_small({
        "norm_mix_post": dg2, "ret_gn_gain": dggn,
        "ssm_lambda_re": dlre[None], "ssm_lambda_im": dlim[None], "ssm_log_dt": dldt[None],
        "ssm_b_re": dbre[None], "ssm_b_im": dbim[None], "ssm_c_re": dcre[None], "ssm_c_im": dcim[None],
        "ssm_d": dd, "norm_mlp_pre": dg3, "norm_mlp_post": dg4,
    }, sq)

    dq, dk, dv = _retention_bwd(q, k, v, do, r_prev, consts, cosf, sinf, after=token)
    pieces = (dq, dk, dv, dgate, du)
    dw_in_t = _dw_in_t(pieces, h1, min(1024, L))
    token = emit({"w_in": dw_in_t})
    gx, dg1 = _inproj_bwd(pieces, w_in_t, x, dx2, g1, min(2 * tm, L), after=token)
    return gx, dg1


BIG_SHAPES = {"w_in": (D_MODEL, IN_COLS // N_DEV), "w_glu": (SSM_W, 2 * SSM_W // N_DEV), "w_out": (D_MODEL // N_DEV, D_MODEL),
              "w_ff1": (D_MODEL, FF1_COLS), "w_ff2": (D_FF // N_DEV, D_MODEL)}
BIG_NAMES = ("w_in", "w_glu", "w_out", "w_ff1", "w_ff2")


def _cols_from_slots(g):
    return jnp.transpose(g, (1, 0, 2)).reshape(g.shape[1], N_DEV * g.shape[2])


def _cols_to_slots(dw):
    r, cols = dw.shape
    return jnp.transpose(dw.reshape(r, N_DEV, cols // N_DEV), (1, 0, 2))


WEIGHT_GROUPS = {"in": ("w_in",), "mix": ("w_glu", "w_out"), "mlp": ("w_ff1", "w_ff2")}


def _weight_from_slots(name, g):
    if name == "w_glu":
        return _cols_from_slots(g)
    if name == "w_ff1":
        return g
    return g.reshape(N_DEV * g.shape[1], g.shape[2])


def _grad_slots(name, dw):
    if name == "w_glu":
        return _cols_to_slots(dw)
    if name == "w_ff1":
        return dw
    if name == "w_in":
        return dw.reshape(N_DEV, BIG_SHAPES[name][1], BIG_SHAPES[name][0])
    return dw.reshape((N_DEV,) + BIG_SHAPES[name])


PIECE_ROWS = 8


VEC_NAMES = tuple(n for n in SMALL_NAMES if n[:6] not in ("ssm_b_", "ssm_c_"))
BC_NAMES = ("ssm_b_re", "ssm_b_im", "ssm_c_re", "ssm_c_im")
BC_ROWS = N_GROUP * SSM_GC


def _bc_view(name, t):
    t = t[0]
    if name.startswith("ssm_b_"):
        t = jnp.swapaxes(t, 1, 2)
    return t.reshape(BC_ROWS, N_STATE)


def _bc_unview(name, t):
    t = t.reshape(N_GROUP, SSM_GC, N_STATE)
    if name.startswith("ssm_b_"):
        t = jnp.swapaxes(t, 1, 2)
    return t[None]


def _pack_bc(vals):
    return jnp.concatenate([_bc_view(n, vals[n]).astype(_F32) for n in BC_NAMES], axis=0)


def _unpack_bc(buf):
    return {n: _bc_unview(n, buf[j * BC_ROWS:(j + 1) * BC_ROWS]) for j, n in enumerate(BC_NAMES)}


def _small_layout(shapes):
    off, rows = {}, 0
    for n in VEC_NAMES:
        off[n] = rows
        rows += -(-math.prod(shapes[n]) // (PIECE_ROWS * LANES)) * PIECE_ROWS
    return off, rows, rows + PIECE_ROWS


def _pack_small(vals, shapes, last=None):
    parts = []
    for n in VEC_NAMES:
        flat = vals[n].reshape(-1).astype(_F32)
        pad = -flat.shape[0] % (PIECE_ROWS * LANES)
        if pad:
            flat = jnp.concatenate([flat, jnp.zeros((pad,), _F32)])
        parts.append(flat.reshape(-1, LANES))
    parts.append(jnp.zeros((PIECE_ROWS, LANES), _F32) if last is None else last)
    return jnp.concatenate(parts, axis=0)


def _unpack_small(buf, shapes):
    off, _, _ = _small_layout(shapes)
    out = {}
    for n in VEC_NAMES:
        size = math.prod(shapes[n])
        rows = -(-size // LANES)
        out[n] = buf[off[n]:off[n] + rows].reshape(-1)[:size].reshape(shapes[n])
    return out


WEIGHT_NAMES = ('norm_mix_pre', 'norm_mix_post', 'w_in', 'ret_gn_gain', 'ssm_lambda_re', 'ssm_lambda_im', 'ssm_log_dt',
                'ssm_b_re', 'ssm_b_im', 'ssm_c_re', 'ssm_c_im', 'ssm_d', 'w_glu', 'w_out', 'norm_mlp_pre',
                'norm_mlp_post', 'w_ff1', 'w_ff2')


def kernel(x, norm_mix_pre, norm_mix_post, w_in, ret_gn_gain, ssm_lambda_re, ssm_lambda_im, ssm_log_dt, ssm_b_re, ssm_b_im, ssm_c_re, ssm_c_im, ssm_d, w_glu, w_out, norm_mlp_pre, norm_mlp_post, w_ff1, w_ff2, loss_target, m_norm_mix_pre, m_norm_mix_post, m_w_in, m_ret_gn_gain, m_ssm_lambda_re, m_ssm_lambda_im, m_ssm_log_dt, m_ssm_b_re, m_ssm_b_im, m_ssm_c_re, m_ssm_c_im, m_ssm_d, m_w_glu, m_w_out, m_norm_mlp_pre, m_norm_mlp_post, m_w_ff1, m_w_ff2, v_norm_mix_pre, v_norm_mix_post, v_w_in, v_ret_gn_gain, v_ssm_lambda_re, v_ssm_lambda_im, v_ssm_log_dt, v_ssm_b_re, v_ssm_b_im, v_ssm_c_re, v_ssm_c_im, v_ssm_d, v_w_glu, v_w_out, v_norm_mlp_pre, v_norm_mlp_post, v_w_ff1, v_w_ff2):
    args = dict(locals())
    w = {n: args[n] for n in WEIGHT_NAMES}
    m = {n: args["m_" + n] for n in WEIGHT_NAMES}
    v = {n: args["v_" + n] for n in WEIGHT_NAMES}
    L = x.shape[1]
    tm = min(256, L)
    tk = min(2048, L)
    tb = min(1024, L)

    calls = {"in": ("w_in",), "rest": WEIGHT_GROUPS["mix"] + WEIGHT_GROUPS["mlp"]}
    started, zero = {}, jnp.zeros((), _F32)
    for call, names in calls.items():
        blocks = [(w[n][0].T if n == "w_in" else w[n][0]).astype(_BF) for n in names]
        blocks[0] = blocks[0] + zero.astype(_BF)
        started[call] = _split_start(blocks, [_landing(b) for b in blocks], True, "weights_start_" + call)
        zero = started[call][4][0, 0]

    def weights(group, after):
        names = WEIGHT_GROUPS[group]
        call = "in" if group == "in" else "rest"
        first = calls[call].index(names[0])
        part = slice(first, first + len(names))
        got = started[call]
        landed = _split_wait(got[0], got[1], got[2][part], got[3][part], after, True, "weights_wait_" + group, first=first)
        return [_weight_from_slots(n, g) for n, g in zip(names, landed)]

    in_flight = []

    def emit(dws):
        names = sorted(dws)
        srcs = [_grad_slots(n, dws[n]) for n in names]
        lands = [_landing(t[0]) for t in srcs]
        started = _split_start(srcs, lands, False, "grads_start_" + "_".join(names))
        in_flight.append((names, started))
        return (started[4],)

    shapes = {n: w[n].shape for n in SMALL_NAMES}
    first_piece = {SMALL_NAMES[0]: jnp.zeros(shapes[SMALL_NAMES[0]], _F32)}
    small_flight = []

    def emit_small(gs, sq):
        loss_rows = jnp.broadcast_to(0.5 / D_MODEL * jnp.sum(sq), (PIECE_ROWS, LANES)).astype(_F32)
        bufs = [_pack_small({**first_piece, **gs}, shapes, loss_rows), _pack_bc(gs)]
        small_flight.append(_split_start(bufs, [_landing(b) for b in bufs], True, "small_grads_start"))
        return (small_flight[0][4],)

    small_w = {n: w[n] for n in SMALL_NAMES}
    gx, dg1 = _local_grads(x[0], loss_target[0], small_w, weights, emit, emit_small, tm, tk, tb, zero=zero)
    last_buf = dg1.reshape(PIECE_ROWS, LANES)
    last_started = _split_start([last_buf], [_landing(last_buf)], True, "last_grad_start")

    grads, delta, new_m, new_v = {}, {}, {}, {}
    after = last_started[4]
    for names, started in in_flight:
        landed = _split_wait(*started[:4], after, False, "grads_wait_" + "_".join(names))
        for n, parts in zip(names, landed):
            flip = (lambda t: t.T) if n == "w_in" else (lambda t: t)
            res = _sum_adamw(parts, flip(w[n][0]), flip(m[n][0]), flip(v[n][0]), math.gcd(256, parts.shape[1]), "adamw_" + n)
            grads[n], delta[n], new_m[n], new_v[n] = (flip(t)[None] for t in res)
        after = res[1]
    small_parts, bc_parts = _split_wait(*small_flight[0][:4], after, True, "small_grads_wait")
    last_parts = _split_wait(*last_started[:4], small_parts, True, "last_grad_wait")[0]
    small_parts = lax.dynamic_update_slice(small_parts, last_parts, (0, 0, 0))
    res_bc = _sum_adamw(bc_parts, _pack_bc(w), _pack_bc(m), _pack_bc(v), BC_ROWS, "adamw_bc")
    sw, sm, sv = _pack_small(w, shapes), _pack_small(m, shapes), _pack_small(v, shapes)
    res = _sum_adamw(small_parts, sw, sm, sv, sw.shape[0], "adamw_small")
    for dst, buf, buf_bc in zip((grads, delta, new_m, new_v), res, res_bc):
        dst.update(_unpack_small(buf, shapes))
        dst.update(_unpack_bc(buf_bc))
    _, loss_at, _ = _small_layout(shapes)
    loss = res[0][loss_at, 0]

    return (loss, gx[None], *[grads[n] for n in WEIGHT_NAMES], *[delta[n] for n in WEIGHT_NAMES],
            *[new_m[n] for n in WEIGHT_NAMES], *[new_v[n] for n in WEIGHT_NAMES])
```
